```python
import math
import jax
import jax.numpy as jnp
from jax import lax
import numpy as np

D_MODEL = 1024
BATCH = 8
SEQ = 2048
DEPTH = 2

GRID_W = 64
CTX_LEN = 256
N_MIXERS = 2
N_MLA_LAYERS = (DEPTH + N_MIXERS - 1) // N_MIXERS
N_S5_LAYERS = DEPTH // N_MIXERS
EPS = 1e-6

MLA_HEADS = 16
QK_NOPE_DIM = 64
QK_ROPE_DIM = 32
V_HEAD_DIM = 64
Q_LORA_RANK = 256
KV_LORA_RANK = 128
MLA_WIDTH = MLA_HEADS * V_HEAD_DIM
QK_DIM = QK_NOPE_DIM + QK_ROPE_DIM
SOFTMAX_SCALE = QK_DIM ** -0.5
ROPE_THETA = 10000.0
Q_BLOCK = 128
MLA_IN_WIDTH = Q_LORA_RANK + KV_LORA_RANK + QK_ROPE_DIM + MLA_WIDTH

S5_WIDTH = D_MODEL
S5_GROUP = 16
S5_GROUPS = S5_WIDTH // S5_GROUP
S5_STATE = 64
DT_MIN = 0.001
DT_MAX = 0.1

kernel_name = "hybrid_mla_s5_context_prefix_dit"


def rmsnorm(x, g):
    xf = x.astype(jnp.float32)
    y = xf * lax.rsqrt(jnp.mean(xf * xf, axis=-1, keepdims=True) + EPS)
    return (y * g.astype(jnp.float32)).astype(x.dtype)


def grid_positions(L):
    rows = L // GRID_W
    row = jnp.repeat(jnp.arange(rows, dtype=jnp.int32), GRID_W)
    col = jnp.tile(jnp.arange(GRID_W, dtype=jnp.int32), rows)
    return row, col


def rope_1d(x, pos):
    d = x.shape[-1]
    inv = 1.0 / (ROPE_THETA ** (jnp.arange(0, d, 2, dtype=jnp.float32) / d))
    ang = pos.astype(jnp.float32)[:, None] * inv[None, :]
    cos = jnp.cos(ang)[:, None, :].astype(x.dtype)
    sin = jnp.sin(ang)[:, None, :].astype(x.dtype)
    x1, x2 = x[..., : d // 2], x[..., d // 2:]
    return jnp.concatenate([x1 * cos - x2 * sin, x1 * sin + x2 * cos], axis=-1)


def axial_rope(x, row, col):
    h = x.shape[-1] // 2
    return jnp.concatenate([rope_1d(x[..., :h], row), rope_1d(x[..., h:], col)], axis=-1)


def mla_project(h, w_in, q_norm, w_uq, kv_norm, w_ukv):
    B_, L, _ = h.shape
    p = h @ w_in
    o1 = Q_LORA_RANK
    o2 = o1 + KV_LORA_RANK
    o3 = o2 + QK_ROPE_DIM
    cq, ckv, kr, z = p[..., :o1], p[..., o1:o2], p[..., o2:o3], p[..., o3:]
    q = (rmsnorm(cq, q_norm) @ w_uq).reshape(B_, L, MLA_HEADS, QK_DIM)
    kv = (rmsnorm(ckv, kv_norm) @ w_ukv).reshape(B_, L, MLA_HEADS, QK_NOPE_DIM + V_HEAD_DIM)
    q_nope, q_rope = q[..., :QK_NOPE_DIM], q[..., QK_NOPE_DIM:]
    k_nope, v = kv[..., :QK_NOPE_DIM], kv[..., QK_NOPE_DIM:]
    return q_nope, q_rope, k_nope, kr[:, :, None, :], v, z


def mla_keys(k_nope, kr):
    kr_b = jnp.broadcast_to(kr, k_nope.shape[:-1] + (QK_ROPE_DIM,))
    return jnp.concatenate([k_nope, kr_b], axis=-1)


def attend(q, k, v):
    s = jnp.einsum('bqhd,bkhd->bhqk', q.astype(jnp.float32), k.astype(jnp.float32)) * SOFTMAX_SCALE
    p = jax.nn.softmax(s, axis=-1)
    return jnp.einsum('bhqk,bkhd->bqhd', p, v.astype(jnp.float32)).astype(v.dtype)


def mla_mixer(h_lat, h_ctx, need_ctx, w_in, q_norm, w_uq, kv_norm, w_ukv, w_out):
    B_, L, _ = h_lat.shape
    Lc = h_ctx.shape[1]
    row, col = grid_positions(L)
    qn_l, qr_l, kn_l, kr_l, v_l, z_l = mla_project(h_lat, w_in, q_norm, w_uq, kv_norm, w_ukv)
    qn_c, qr_c, kn_c, kr_c, v_c, z_c = mla_project(h_ctx, w_in, q_norm, w_uq, kv_norm, w_ukv)
    q_lat = jnp.concatenate([qn_l, axial_rope(qr_l, row, col)], axis=-1)
    k_lat = mla_keys(kn_l, axial_rope(kr_l, row, col))
    k_ctx = mla_keys(kn_c, kr_c)
    k_all = jnp.concatenate([k_ctx, k_lat], axis=1)
    v_all = jnp.concatenate([v_c, v_l], axis=1)
    nb = L // Q_BLOCK
    qb = jnp.transpose(q_lat.reshape(B_, nb, Q_BLOCK, MLA_HEADS, QK_DIM), (1, 0, 2, 3, 4))
    ob = lax.map(lambda qq: attend(qq, k_all, v_all), qb)
    o_lat = jnp.transpose(ob, (1, 0, 2, 3, 4)).reshape(B_, L, MLA_WIDTH)
    out_lat = (o_lat * jax.nn.silu(z_l)) @ w_out
    out_ctx = None
    if need_ctx:
        q_ctx = jnp.concatenate([qn_c, qr_c], axis=-1)
        o_ctx = attend(q_ctx, k_ctx, v_c).reshape(B_, Lc, MLA_WIDTH)
        out_ctx = (o_ctx * jax.nn.silu(z_c)) @ w_out
    return out_lat, out_ctx


def s5_discretise(a_re, a_im, log_step, b_re, b_im):
    dt = jnp.exp(log_step.astype(jnp.float32))[:, None]
    ar = a_re.astype(jnp.float32)
    ai = a_im.astype(jnp.float32)
    mag = jnp.exp(ar * dt)
    lb_re = mag * jnp.cos(ai * dt)
    lb_im = mag * jnp.sin(ai * dt)
    den = ar * ar + ai * ai
    nr = lb_re - 1.0
    f_re = ((nr * ar + lb_im * ai) / den)[..., None]
    f_im = ((lb_im * ar - nr * ai) / den)[..., None]
    br = b_re.astype(jnp.float32)
    bi = b_im.astype(jnp.float32)
    bb_re = f_re * br - f_im * bi
    bb_im = f_re * bi + f_im * br
    return lb_re, lb_im, bb_re, bb_im


def linear_recurrence_combine(e1, e2):
    a1r, a1i, b1r, b1i = e1
    a2r, a2i, b2r, b2i = e2
    return (a2r * a1r - a2i * a1i,
            a2r * a1i + a2i * a1r,
            a2r * b1r - a2i * b1i + b2r,
            a2r * b1i + a2i * b1r + b2i)


def s5_scan(u, disc, c_re, c_im, s0, reverse):
    lb_re, lb_im, bb_re, bb_im = disc
    L = u.shape[1]
    bu_re = jnp.einsum('blgc,gpc->blgp', u, bb_re)
    bu_im = jnp.einsum('blgc,gpc->blgp', u, bb_im)
    if s0 is not None:
        s0r, s0i = s0
        idx = L - 1 if reverse else 0
        bu_re = bu_re.at[:, idx].add(lb_re * s0r - lb_im * s0i)
        bu_im = bu_im.at[:, idx].add(lb_re * s0i + lb_im * s0r)
    a_re = jnp.broadcast_to(lb_re, (1, L) + lb_re.shape)
    a_im = jnp.broadcast_to(lb_im, (1, L) + lb_im.shape)
    _, _, s_re, s_im = lax.associative_scan(
        linear_recurrence_combine, (a_re, a_im, bu_re, bu_im), reverse=reverse, axis=1)
    y = (jnp.einsum('blgp,gcp->blgc', s_re, c_re.astype(jnp.float32))
         - jnp.einsum('blgp,gcp->blgc', s_im, c_im.astype(jnp.float32)))
    fin = (s_re[:, 0], s_im[:, 0]) if reverse else (s_re[:, -1], s_im[:, -1])
    return y, fin


def s5_finish(y_ssm, u, z, d, w_glu, b_glu, w_out):
    B_, L, _ = u.shape
    y = y_ssm.reshape(B_, L, S5_WIDTH) + d.astype(jnp.float32) * u.astype(jnp.float32)
    y = jax.nn.gelu(y).astype(u.dtype)
    y = y * jax.nn.sigmoid(y @ w_glu + b_glu)
    return (y * jax.nn.silu(z)) @ w_out


def s5_mixer(h_lat, h_ctx, need_ctx, w_in, a_re, a_im, log_step, b_re, b_im, c_re, c_im,
             d, w_glu, b_glu, w_out):
    B_, L, _ = h_lat.shape
    Lc = h_ctx.shape[1]
    p_l = h_lat @ w_in
    p_c = h_ctx @ w_in
    u_l, z_l = p_l[..., :S5_WIDTH], p_l[..., S5_WIDTH:]
    u_c, z_c = p_c[..., :S5_WIDTH], p_c[..., S5_WIDTH:]
    g_l = u_l.astype(jnp.float32).reshape(B_, L, S5_GROUPS, S5_GROUP)
    g_c = u_c.astype(jnp.float32).reshape(B_, Lc, S5_GROUPS, S5_GROUP)
    y_l = jnp.zeros_like(g_l)
    y_c = jnp.zeros_like(g_c)
    for k, rev in enumerate((False, True)):
        disc = s5_discretise(a_re[k], a_im[k], log_step[k], b_re[k], b_im[k])
        yc_k, s_fin = s5_scan(g_c, disc, c_re[k], c_im[k], None, rev)
        yl_k, _ = s5_scan(g_l, disc, c_re[k], c_im[k], s_fin, rev)
        y_l = y_l + yl_k
        y_c = y_c + yc_k
    out_lat = s5_finish(y_l, u_l, z_l, d, w_glu, b_glu, w_out)
    out_ctx = s5_finish(y_c, u_c, z_c, d, w_glu, b_glu, w_out) if need_ctx else None
    return out_lat, out_ctx


def _fwd_setup_inputs(seed: int = 0) -> dict:
    key = jax.random.key(seed)
    ks = jax.random.split(key, 32)

    def nrm(k, shape, scale):
        return jax.random.normal(k, shape, jnp.float32) * scale

    D, E = D_MODEL, MLA_WIDTH
    G, P, CH = S5_GROUPS, S5_STATE, S5_GROUP
    a_im_base = jnp.pi * jnp.arange(P, dtype=jnp.float32)
    return {
        'x': nrm(ks[0], (BATCH, SEQ, D), 1.0),
        'c': nrm(ks[1], (BATCH, D), 1.0),
        'ctx': nrm(ks[2], (BATCH, CTX_LEN, D), 1.0),
        'c_ctx': nrm(ks[3], (D,), 1.0),
        'ada_w': nrm(ks[4], (DEPTH, D, 3 * D), 0.5 * D ** -0.5),
        'ada_b': nrm(ks[5], (DEPTH, 3 * D), 0.01),
        'norm_g': 1.0 + nrm(ks[6], (DEPTH, D), 0.01),
        'mla_w_in': nrm(ks[7], (N_MLA_LAYERS, D, MLA_IN_WIDTH), D ** -0.5),
        'mla_q_norm': 1.0 + nrm(ks[8], (N_MLA_LAYERS, Q_LORA_RANK), 0.01),
        'mla_w_uq': nrm(ks[9], (N_MLA_LAYERS, Q_LORA_RANK, MLA_HEADS * QK_DIM), Q_LORA_RANK ** -0.5),
        'mla_kv_norm': 1.0 + nrm(ks[10], (N_MLA_LAYERS, KV_LORA_RANK), 0.01),
        'mla_w_ukv': nrm(ks[11], (N_MLA_LAYERS, KV_LORA_RANK, MLA_HEADS * (QK_NOPE_DIM + V_HEAD_DIM)),
                         KV_LORA_RANK ** -0.5),
        'mla_w_out': nrm(ks[12], (N_MLA_LAYERS, E, D), E ** -0.5),
        's5_w_in': nrm(ks[13], (N_S5_LAYERS, D, 2 * S5_WIDTH), D ** -0.5),
        's5_a_re': -0.5 + nrm(ks[14], (N_S5_LAYERS, 2, G, P), 0.01),
        's5_a_im': a_im_base + nrm(ks[15], (N_S5_LAYERS, 2, G, P), 0.01),
        's5_log_step': jax.random.uniform(ks[16], (N_S5_LAYERS, 2, G), jnp.float32,
                                          math.log(DT_MIN), math.log(DT_MAX)),
        's5_b_re': nrm(ks[17], (N_S5_LAYERS, 2, G, P, CH), (2 * CH) ** -0.5),
        's5_b_im': nrm(ks[18], (N_S5_LAYERS, 2, G, P, CH), (2 * CH) ** -0.5),
        's5_c_re': nrm(ks[19], (N_S5_LAYERS, 2, G, CH, P), P ** -0.5),
        's5_c_im': nrm(ks[20], (N_S5_LAYERS, 2, G, CH, P), P ** -0.5),
        's5_d': nrm(ks[21], (N_S5_LAYERS, S5_WIDTH), 1.0),
        's5_w_glu': nrm(ks[22], (N_S5_LAYERS, S5_WIDTH, S5_WIDTH), S5_WIDTH ** -0.5),
        's5_b_glu': nrm(ks[23], (N_S5_LAYERS, S5_WIDTH), 0.01),
        's5_w_out': nrm(ks[24], (N_S5_LAYERS, S5_WIDTH, D), S5_WIDTH ** -0.5),
        'final_g': 1.0 + nrm(ks[25], (D,), 0.01),
    }


def _fwd_reference(x, c, ctx, c_ctx, ada_w, ada_b, norm_g,
              mla_w_in, mla_q_norm, mla_w_uq, mla_kv_norm, mla_w_ukv, mla_w_out,
              s5_w_in, s5_a_re, s5_a_im, s5_log_step, s5_b_re, s5_b_im, s5_c_re, s5_c_im,
              s5_d, s5_w_glu, s5_b_glu, s5_w_out, final_g):
    silu_c = jax.nn.silu(c)
    silu_cc = jax.nn.silu(c_ctx)
    for i in range(DEPTH):
        need_ctx = i < DEPTH - 1
        mod_l = silu_c @ ada_w[i] + ada_b[i]
        mod_c = silu_cc @ ada_w[i] + ada_b[i]
        sh_l, sc_l, gt_l = jnp.split(mod_l, 3, axis=-1)
        sh_c, sc_c, gt_c = jnp.split(mod_c, 3, axis=-1)
        h_l = rmsnorm(x, norm_g[i]) * (1.0 + sc_l[:, None, :]) + sh_l[:, None, :]
        h_c = rmsnorm(ctx, norm_g[i]) * (1.0 + sc_c) + sh_c
        j = i // N_MIXERS
        if i % N_MIXERS == 0:
            o_l, o_c = mla_mixer(h_l, h_c, need_ctx, mla_w_in[j], mla_q_norm[j], mla_w_uq[j],
                                 mla_kv_norm[j], mla_w_ukv[j], mla_w_out[j])
        else:
            o_l, o_c = s5_mixer(h_l, h_c, need_ctx, s5_w_in[j], s5_a_re[j], s5_a_im[j],
                                s5_log_step[j], s5_b_re[j], s5_b_im[j], s5_c_re[j], s5_c_im[j],
                                s5_d[j], s5_w_glu[j], s5_b_glu[j], s5_w_out[j])
        x = x + gt_l[:, None, :] * o_l
        if need_ctx:
            ctx = ctx + gt_c * o_c
    return rmsnorm(x, final_g)


import jax as _jax
import jax.numpy as _jnp

TWIN_FORMAT = 'train_step'
FWD_PARAMS = ['x', 'c', 'ctx', 'c_ctx', 'ada_w', 'ada_b', 'norm_g', 'mla_w_in', 'mla_q_norm', 'mla_w_uq', 'mla_kv_norm', 'mla_w_ukv', 'mla_w_out', 's5_w_in', 's5_a_re', 's5_a_im', 's5_log_step', 's5_b_re', 's5_b_im', 's5_c_re', 's5_c_im', 's5_d', 's5_w_glu', 's5_b_glu', 's5_w_out', 'final_g']
TWIN_WEIGHTS = ['c_ctx', 'ada_w', 'ada_b', 'norm_g', 'mla_w_in', 'mla_q_norm', 'mla_w_uq', 'mla_kv_norm', 'mla_w_ukv', 'mla_w_out', 's5_w_in', 's5_a_re', 's5_a_im', 's5_log_step', 's5_b_re', 's5_b_im', 's5_c_re', 's5_c_im', 's5_d', 's5_w_glu', 's5_b_glu', 's5_w_out', 'final_g']
TWIN_DIFF_INPUT = 'x'
TWIN_INPUTS = ['x', 'c', 'ctx', 'c_ctx', 'ada_w', 'ada_b', 'norm_g', 'mla_w_in', 'mla_q_norm', 'mla_w_uq', 'mla_kv_norm', 'mla_w_ukv', 'mla_w_out', 's5_w_in', 's5_a_re', 's5_a_im', 's5_log_step', 's5_b_re', 's5_b_im', 's5_c_re', 's5_c_im', 's5_d', 's5_w_glu', 's5_b_glu', 's5_w_out', 'final_g', 'loss_target', 'm_c_ctx', 'm_ada_w', 'm_ada_b', 'm_norm_g', 'm_mla_w_in', 'm_mla_q_norm', 'm_mla_w_uq', 'm_mla_kv_norm', 'm_mla_w_ukv', 'm_mla_w_out', 'm_s5_w_in', 'm_s5_a_re', 'm_s5_a_im', 'm_s5_log_step', 'm_s5_b_re', 'm_s5_b_im', 'm_s5_c_re', 'm_s5_c_im', 'm_s5_d', 'm_s5_w_glu', 'm_s5_b_glu', 'm_s5_w_out', 'm_final_g', 'v_c_ctx', 'v_ada_w', 'v_ada_b', 'v_norm_g', 'v_mla_w_in', 'v_mla_q_norm', 'v_mla_w_uq', 'v_mla_kv_norm', 'v_mla_w_ukv', 'v_mla_w_out', 'v_s5_w_in', 'v_s5_a_re', 'v_s5_a_im', 'v_s5_log_step', 'v_s5_b_re', 'v_s5_b_im', 'v_s5_c_re', 'v_s5_c_im', 'v_s5_d', 'v_s5_w_glu', 'v_s5_b_glu', 'v_s5_w_out', 'v_final_g']
TWIN_OUTPUTS = ['loss', 'grad_x', 'grad_c_ctx', 'grad_ada_w', 'grad_ada_b', 'grad_norm_g', 'grad_mla_w_in', 'grad_mla_q_norm', 'grad_mla_w_uq', 'grad_mla_kv_norm', 'grad_mla_w_ukv', 'grad_mla_w_out', 'grad_s5_w_in', 'grad_s5_a_re', 'grad_s5_a_im', 'grad_s5_log_step', 'grad_s5_b_re', 'grad_s5_b_im', 'grad_s5_c_re', 'grad_s5_c_im', 'grad_s5_d', 'grad_s5_w_glu', 'grad_s5_b_glu', 'grad_s5_w_out', 'grad_final_g', 'delta_c_ctx', 'delta_ada_w', 'delta_ada_b', 'delta_norm_g', 'delta_mla_w_in', 'delta_mla_q_norm', 'delta_mla_w_uq', 'delta_mla_kv_norm', 'delta_mla_w_ukv', 'delta_mla_w_out', 'delta_s5_w_in', 'delta_s5_a_re', 'delta_s5_a_im', 'delta_s5_log_step', 'delta_s5_b_re', 'delta_s5_b_im', 'delta_s5_c_re', 'delta_s5_c_im', 'delta_s5_d', 'delta_s5_w_glu', 'delta_s5_b_glu', 'delta_s5_w_out', 'delta_final_g', 'new_m_c_ctx', 'new_m_ada_w', 'new_m_ada_b', 'new_m_norm_g', 'new_m_mla_w_in', 'new_m_mla_q_norm', 'new_m_mla_w_uq', 'new_m_mla_kv_norm', 'new_m_mla_w_ukv', 'new_m_mla_w_out', 'new_m_s5_w_in', 'new_m_s5_a_re', 'new_m_s5_a_im', 'new_m_s5_log_step', 'new_m_s5_b_re', 'new_m_s5_b_im', 'new_m_s5_c_re', 'new_m_s5_c_im', 'new_m_s5_d', 'new_m_s5_w_glu', 'new_m_s5_b_glu', 'new_m_s5_w_out', 'new_m_final_g', 'new_v_c_ctx', 'new_v_ada_w', 'new_v_ada_b', 'new_v_norm_g', 'new_v_mla_w_in', 'new_v_mla_q_norm', 'new_v_mla_w_uq', 'new_v_mla_kv_norm', 'new_v_mla_w_ukv', 'new_v_mla_w_out', 'new_v_s5_w_in', 'new_v_s5_a_re', 'new_v_s5_a_im', 'new_v_s5_log_step', 'new_v_s5_b_re', 'new_v_s5_b_im', 'new_v_s5_c_re', 'new_v_s5_c_im', 'new_v_s5_d', 'new_v_s5_w_glu', 'new_v_s5_b_glu', 'new_v_s5_w_out', 'new_v_final_g']
TWIN_LEAF_KINDS = {'loss': 'loss', 'grad_x': 'grad_x', 'grad_c_ctx': 'grad_w', 'grad_ada_w': 'grad_w', 'grad_ada_b': 'grad_w', 'grad_norm_g': 'grad_w', 'grad_mla_w_in': 'grad_w', 'grad_mla_q_norm': 'grad_w', 'grad_mla_w_uq': 'grad_w', 'grad_mla_kv_norm': 'grad_w', 'grad_mla_w_ukv': 'grad_w', 'grad_mla_w_out': 'grad_w', 'grad_s5_w_in': 'grad_w', 'grad_s5_a_re': 'grad_w', 'grad_s5_a_im': 'grad_w', 'grad_s5_log_step': 'grad_w', 'grad_s5_b_re': 'grad_w', 'grad_s5_b_im': 'grad_w', 'grad_s5_c_re': 'grad_w', 'grad_s5_c_im': 'grad_w', 'grad_s5_d': 'grad_w', 'grad_s5_w_glu': 'grad_w', 'grad_s5_b_glu': 'grad_w', 'grad_s5_w_out': 'grad_w', 'grad_final_g': 'grad_w', 'delta_c_ctx': 'delta_w', 'delta_ada_w': 'delta_w', 'delta_ada_b': 'delta_w', 'delta_norm_g': 'delta_w', 'delta_mla_w_in': 'delta_w', 'delta_mla_q_norm': 'delta_w', 'delta_mla_w_uq': 'delta_w', 'delta_mla_kv_norm': 'delta_w', 'delta_mla_w_ukv': 'delta_w', 'delta_mla_w_out': 'delta_w', 'delta_s5_w_in': 'delta_w', 'delta_s5_a_re': 'delta_w', 'delta_s5_a_im': 'delta_w', 'delta_s5_log_step': 'delta_w', 'delta_s5_b_re': 'delta_w', 'delta_s5_b_im': 'delta_w', 'delta_s5_c_re': 'delta_w', 'delta_s5_c_im': 'delta_w', 'delta_s5_d': 'delta_w', 'delta_s5_w_glu': 'delta_w', 'delta_s5_b_glu': 'delta_w', 'delta_s5_w_out': 'delta_w', 'delta_final_g': 'delta_w', 'new_m_c_ctx': 'new_m', 'new_m_ada_w': 'new_m', 'new_m_ada_b': 'new_m', 'new_m_norm_g': 'new_m', 'new_m_mla_w_in': 'new_m', 'new_m_mla_q_norm': 'new_m', 'new_m_mla_w_uq': 'new_m', 'new_m_mla_kv_norm': 'new_m', 'new_m_mla_w_ukv': 'new_m', 'new_m_mla_w_out': 'new_m', 'new_m_s5_w_in': 'new_m', 'new_m_s5_a_re': 'new_m', 'new_m_s5_a_im': 'new_m', 'new_m_s5_log_step': 'new_m', 'new_m_s5_b_re': 'new_m', 'new_m_s5_b_im': 'new_m', 'new_m_s5_c_re': 'new_m', 'new_m_s5_c_im': 'new_m', 'new_m_s5_d': 'new_m', 'new_m_s5_w_glu': 'new_m', 'new_m_s5_b_glu': 'new_m', 'new_m_s5_w_out': 'new_m', 'new_m_final_g': 'new_m', 'new_v_c_ctx': 'new_v', 'new_v_ada_w': 'new_v', 'new_v_ada_b': 'new_v', 'new_v_norm_g': 'new_v', 'new_v_mla_w_in': 'new_v', 'new_v_mla_q_norm': 'new_v', 'new_v_mla_w_uq': 'new_v', 'new_v_mla_kv_norm': 'new_v', 'new_v_mla_w_ukv': 'new_v', 'new_v_mla_w_out': 'new_v', 'new_v_s5_w_in': 'new_v', 'new_v_s5_a_re': 'new_v', 'new_v_s5_a_im': 'new_v', 'new_v_s5_log_step': 'new_v', 'new_v_s5_b_re': 'new_v', 'new_v_s5_b_im': 'new_v', 'new_v_s5_c_re': 'new_v', 'new_v_s5_c_im': 'new_v', 'new_v_s5_d': 'new_v', 'new_v_s5_w_glu': 'new_v', 'new_v_s5_b_glu': 'new_v', 'new_v_s5_w_out': 'new_v', 'new_v_final_g': 'new_v'}


def _forward(args):
    return _fwd_reference(*[args[k] for k in FWD_PARAMS])


def _output_shape():
    out = _jax.eval_shape(lambda: _forward(_fwd_setup_inputs(0)))
    return out.shape, out.dtype

N_MICROBATCH = 1
ADAM_LR = 0.001
ADAM_B1 = 0.9
ADAM_B2 = 0.999
ADAM_EPS = 1e-08
ADAM_WD = 0.01
ADAM_STEP = 10
PER_EXAMPLE_BATCH_AXIS = {'x': 0, 'c': 0, 'ctx': 0, 'loss_target': 0}
SHARED_INPUTS = []
_WEIGHT_DTYPES = {'c_ctx': _jnp.float32, 'ada_w': _jnp.float32, 'ada_b': _jnp.float32, 'norm_g': _jnp.float32, 'mla_w_in': _jnp.float32, 'mla_q_norm': _jnp.float32, 'mla_w_uq': _jnp.float32, 'mla_kv_norm': _jnp.float32, 'mla_w_ukv': _jnp.float32, 'mla_w_out': _jnp.float32, 's5_w_in': _jnp.float32, 's5_a_re': _jnp.float32, 's5_a_im': _jnp.float32, 's5_log_step': _jnp.float32, 's5_b_re': _jnp.float32, 's5_b_im': _jnp.float32, 's5_c_re': _jnp.float32, 's5_c_im': _jnp.float32, 's5_d': _jnp.float32, 's5_w_glu': _jnp.float32, 's5_b_glu': _jnp.float32, 's5_w_out': _jnp.float32, 'final_g': _jnp.float32}
MOMENT_SCALE = {'c_ctx': 3.732277e-03, 'ada_w': 1.257621e-02, 'ada_b': 2.058810e-02, 'norm_g': 1.173768e-02, 'mla_w_in': 8.385360e-03, 'mla_q_norm': 5.696501e-03, 'mla_w_uq': 2.272983e-03, 'mla_kv_norm': 3.169778e-02, 'mla_w_ukv': 5.024857e-03, 'mla_w_out': 7.026757e-03, 's5_w_in': 1.034752e-02, 's5_a_re': 9.233380e-04, 's5_a_im': 8.454713e-04, 's5_log_step': 5.331237e-01, 's5_b_re': 5.700341e-04, 's5_b_im': 5.346345e-04, 's5_c_re': 7.671336e-04, 's5_c_im': 7.648610e-04, 's5_d': 1.072586e-02, 's5_w_glu': 3.139839e-03, 's5_b_glu': 4.060695e-03, 's5_w_out': 9.881198e-03, 'final_g': 1.598979e+01}


def _to_microbatches(a, axis):
    t = _jnp.moveaxis(a, axis, 0)
    t = t.reshape((N_MICROBATCH, t.shape[0] // N_MICROBATCH) + t.shape[1:])
    return _jnp.moveaxis(t, 1, axis + 1)


def setup_inputs(seed: int = 0) -> dict:
    inp = _fwd_setup_inputs(seed)
    key = _jax.random.fold_in(_jax.random.key(seed), 7919)
    shape, _ = _output_shape()
    out = dict(inp)
    out["loss_target"] = _jax.random.normal(_jax.random.fold_in(key, 0), shape, _jnp.float32)
    for i, name in enumerate(TWIN_WEIGHTS):
        w = inp[name].astype(_jnp.float32)
        if MOMENT_SCALE is None:
            s = _jnp.sqrt(_jnp.mean(_jnp.square(w)) + 1e-30)
        else:
            s = MOMENT_SCALE[name]
        km, kv = _jax.random.split(_jax.random.fold_in(key, i + 1))
        out[name] = w
        out["m_" + name] = s * _jax.random.normal(km, w.shape, _jnp.float32)
        out["v_" + name] = (s * s) * _jax.random.uniform(kv, w.shape, _jnp.float32, 0.5, 1.5)
    if N_MICROBATCH > 1:
        for name, axis in PER_EXAMPLE_BATCH_AXIS.items():
            out[name] = _to_microbatches(out[name], axis)
    return {'x': out['x'], 'c': out['c'], 'ctx': out['ctx'], 'c_ctx': out['c_ctx'], 'ada_w': out['ada_w'], 'ada_b': out['ada_b'], 'norm_g': out['norm_g'], 'mla_w_in': out['mla_w_in'], 'mla_q_norm': out['mla_q_norm'], 'mla_w_uq': out['mla_w_uq'], 'mla_kv_norm': out['mla_kv_norm'], 'mla_w_ukv': out['mla_w_ukv'], 'mla_w_out': out['mla_w_out'], 's5_w_in': out['s5_w_in'], 's5_a_re': out['s5_a_re'], 's5_a_im': out['s5_a_im'], 's5_log_step': out['s5_log_step'], 's5_b_re': out['s5_b_re'], 's5_b_im': out['s5_b_im'], 's5_c_re': out['s5_c_re'], 's5_c_im': out['s5_c_im'], 's5_d': out['s5_d'], 's5_w_glu': out['s5_w_glu'], 's5_b_glu': out['s5_b_glu'], 's5_w_out': out['s5_w_out'], 'final_g': out['final_g'], 'loss_target': out['loss_target'], 'm_c_ctx': out['m_c_ctx'], 'm_ada_w': out['m_ada_w'], 'm_ada_b': out['m_ada_b'], 'm_norm_g': out['m_norm_g'], 'm_mla_w_in': out['m_mla_w_in'], 'm_mla_q_norm': out['m_mla_q_norm'], 'm_mla_w_uq': out['m_mla_w_uq'], 'm_mla_kv_norm': out['m_mla_kv_norm'], 'm_mla_w_ukv': out['m_mla_w_ukv'], 'm_mla_w_out': out['m_mla_w_out'], 'm_s5_w_in': out['m_s5_w_in'], 'm_s5_a_re': out['m_s5_a_re'], 'm_s5_a_im': out['m_s5_a_im'], 'm_s5_log_step': out['m_s5_log_step'], 'm_s5_b_re': out['m_s5_b_re'], 'm_s5_b_im': out['m_s5_b_im'], 'm_s5_c_re': out['m_s5_c_re'], 'm_s5_c_im': out['m_s5_c_im'], 'm_s5_d': out['m_s5_d'], 'm_s5_w_glu': out['m_s5_w_glu'], 'm_s5_b_glu': out['m_s5_b_glu'], 'm_s5_w_out': out['m_s5_w_out'], 'm_final_g': out['m_final_g'], 'v_c_ctx': out['v_c_ctx'], 'v_ada_w': out['v_ada_w'], 'v_ada_b': out['v_ada_b'], 'v_norm_g': out['v_norm_g'], 'v_mla_w_in': out['v_mla_w_in'], 'v_mla_q_norm': out['v_mla_q_norm'], 'v_mla_w_uq': out['v_mla_w_uq'], 'v_mla_kv_norm': out['v_mla_kv_norm'], 'v_mla_w_ukv': out['v_mla_w_ukv'], 'v_mla_w_out': out['v_mla_w_out'], 'v_s5_w_in': out['v_s5_w_in'], 'v_s5_a_re': out['v_s5_a_re'], 'v_s5_a_im': out['v_s5_a_im'], 'v_s5_log_step': out['v_s5_log_step'], 'v_s5_b_re': out['v_s5_b_re'], 'v_s5_b_im': out['v_s5_b_im'], 'v_s5_c_re': out['v_s5_c_re'], 'v_s5_c_im': out['v_s5_c_im'], 'v_s5_d': out['v_s5_d'], 'v_s5_w_glu': out['v_s5_w_glu'], 'v_s5_b_glu': out['v_s5_b_glu'], 'v_s5_w_out': out['v_s5_w_out'], 'v_final_g': out['v_final_g']}


def _loss(weights, diff, rest, loss_target):
    with _jax.named_scope("forward"):
        args = {**rest, TWIN_DIFF_INPUT: diff, **{k: w.astype(_WEIGHT_DTYPES[k]) for k, w in weights.items()}}
        y = _forward(args)
    with _jax.named_scope("loss_head"):
        err = _jnp.square(y.astype(_jnp.float32) - loss_target)
        return 0.5 * _jnp.sum(_jnp.mean(err, axis=-1)) if err.ndim else 0.5 * err


def _adamw(w, g, m, v):
    m = ADAM_B1 * m + (1.0 - ADAM_B1) * g
    v = ADAM_B2 * v + (1.0 - ADAM_B2) * _jnp.square(g)
    m_hat = m / (1.0 - ADAM_B1 ** ADAM_STEP)
    v_hat = v / (1.0 - ADAM_B2 ** ADAM_STEP)
    delta = -ADAM_LR * (m_hat / (_jnp.sqrt(v_hat) + ADAM_EPS) + ADAM_WD * w)
    return delta, m, v


def reference(x, c, ctx, c_ctx, ada_w, ada_b, norm_g, mla_w_in, mla_q_norm, mla_w_uq, mla_kv_norm, mla_w_ukv, mla_w_out, s5_w_in, s5_a_re, s5_a_im, s5_log_step, s5_b_re, s5_b_im, s5_c_re, s5_c_im, s5_d, s5_w_glu, s5_b_glu, s5_w_out, final_g, loss_target, m_c_ctx, m_ada_w, m_ada_b, m_norm_g, m_mla_w_in, m_mla_q_norm, m_mla_w_uq, m_mla_kv_norm, m_mla_w_ukv, m_mla_w_out, m_s5_w_in, m_s5_a_re, m_s5_a_im, m_s5_log_step, m_s5_b_re, m_s5_b_im, m_s5_c_re, m_s5_c_im, m_s5_d, m_s5_w_glu, m_s5_b_glu, m_s5_w_out, m_final_g, v_c_ctx, v_ada_w, v_ada_b, v_norm_g, v_mla_w_in, v_mla_q_norm, v_mla_w_uq, v_mla_kv_norm, v_mla_w_ukv, v_mla_w_out, v_s5_w_in, v_s5_a_re, v_s5_a_im, v_s5_log_step, v_s5_b_re, v_s5_b_im, v_s5_c_re, v_s5_c_im, v_s5_d, v_s5_w_glu, v_s5_b_glu, v_s5_w_out, v_final_g):
    given = dict(x=x, c=c, ctx=ctx, c_ctx=c_ctx, ada_w=ada_w, ada_b=ada_b, norm_g=norm_g, mla_w_in=mla_w_in, mla_q_norm=mla_q_norm, mla_w_uq=mla_w_uq, mla_kv_norm=mla_kv_norm, mla_w_ukv=mla_w_ukv, mla_w_out=mla_w_out, s5_w_in=s5_w_in, s5_a_re=s5_a_re, s5_a_im=s5_a_im, s5_log_step=s5_log_step, s5_b_re=s5_b_re, s5_b_im=s5_b_im, s5_c_re=s5_c_re, s5_c_im=s5_c_im, s5_d=s5_d, s5_w_glu=s5_w_glu, s5_b_glu=s5_b_glu, s5_w_out=s5_w_out, final_g=final_g, loss_target=loss_target, m_c_ctx=m_c_ctx, m_ada_w=m_ada_w, m_ada_b=m_ada_b, m_norm_g=m_norm_g, m_mla_w_in=m_mla_w_in, m_mla_q_norm=m_mla_q_norm, m_mla_w_uq=m_mla_w_uq, m_mla_kv_norm=m_mla_kv_norm, m_mla_w_ukv=m_mla_w_ukv, m_mla_w_out=m_mla_w_out, m_s5_w_in=m_s5_w_in, m_s5_a_re=m_s5_a_re, m_s5_a_im=m_s5_a_im, m_s5_log_step=m_s5_log_step, m_s5_b_re=m_s5_b_re, m_s5_b_im=m_s5_b_im, m_s5_c_re=m_s5_c_re, m_s5_c_im=m_s5_c_im, m_s5_d=m_s5_d, m_s5_w_glu=m_s5_w_glu, m_s5_b_glu=m_s5_b_glu, m_s5_w_out=m_s5_w_out, m_final_g=m_final_g, v_c_ctx=v_c_ctx, v_ada_w=v_ada_w, v_ada_b=v_ada_b, v_norm_g=v_norm_g, v_mla_w_in=v_mla_w_in, v_mla_q_norm=v_mla_q_norm, v_mla_w_uq=v_mla_w_uq, v_mla_kv_norm=v_mla_kv_norm, v_mla_w_ukv=v_mla_w_ukv, v_mla_w_out=v_mla_w_out, v_s5_w_in=v_s5_w_in, v_s5_a_re=v_s5_a_re, v_s5_a_im=v_s5_a_im, v_s5_log_step=v_s5_log_step, v_s5_b_re=v_s5_b_re, v_s5_b_im=v_s5_b_im, v_s5_c_re=v_s5_c_re, v_s5_c_im=v_s5_c_im, v_s5_d=v_s5_d, v_s5_w_glu=v_s5_w_glu, v_s5_b_glu=v_s5_b_glu, v_s5_w_out=v_s5_w_out, v_final_g=v_final_g)
    weights = {n: given[n] for n in TWIN_WEIGHTS}
    shared = {n: given[n] for n in SHARED_INPUTS}
    per_example = {n: given[n] for n in ['x', 'c', 'ctx']}
    grad_fn = _jax.value_and_grad(_loss, argnums=(0, 1))

    def one_microbatch(ex, loss_target):
        ex = dict(ex)
        diff = ex.pop(TWIN_DIFF_INPUT)
        return grad_fn(weights, diff, {**shared, **ex}, loss_target)

    if N_MICROBATCH == 1:
        loss, (grad_w, grad_x) = one_microbatch(per_example, given["loss_target"])
    else:
        def body(carry, xs):
            loss_sum, grad_sum = carry
            l_k, (gw_k, gx_k) = one_microbatch(xs[0], xs[1])
            with _jax.named_scope("update"):
                return (loss_sum + l_k, _jax.tree.map(_jnp.add, grad_sum, gw_k)), gx_k

        init = (_jnp.zeros((), _jnp.float32), _jax.tree.map(_jnp.zeros_like, weights))
        (loss, grad_w), grad_x = _jax.lax.scan(body, init, (per_example, given["loss_target"]))
    with _jax.named_scope("update"):
        delta_w, new_m, new_v = {}, {}, {}
        for n in TWIN_WEIGHTS:
            delta_w[n], new_m[n], new_v[n] = _adamw(weights[n], grad_w[n], given["m_" + n], given["v_" + n])
    return (loss, grad_x, *[grad_w[n] for n in TWIN_WEIGHTS], *[delta_w[n] for n in TWIN_WEIGHTS],
            *[new_m[n] for n in TWIN_WEIGHTS], *[new_v[n] for n in TWIN_WEIGHTS])
```

```python
import math

import numpy as np
import jax
import jax.numpy as jnp
from jax import lax
from jax.experimental import pallas as pl
from jax.experimental.pallas import tpu as pltpu

F32 = jnp.float32
BF16 = jnp.bfloat16

D = 1024
L = 2048
LC = 256
NDEV = 8
GRID_W = 64
EPS = 1e-6
HEADS = 16
NOPE = 64
ROPE = 32
QK = NOPE + ROPE
VD = 64
IN_W = 256 + 128 + ROPE + HEADS * 64
IN_WP = 1536
QL = 256
KVL = 128
SCALE = QK ** -0.5
THETA = 10000.0
G = 64
P = 64
CH = 16
GB = 8
NJ = G // GB
UB = GB * CH
SB = GB * P
SEG = 8
TB = 256
LANES = 1024
VMEM_LIMIT = 56 * 1024 * 1024
B1, B2, LR, AEPS, WD, STEP = 0.9, 0.999, 0.001, 1e-8, 0.01, 10
MESH_T = pl.DeviceIdType.MESH


def _cp(sem=None):
    return pltpu.CompilerParams(dimension_semantics=sem, vmem_limit_bytes=VMEM_LIMIT)


def _sig(x):
    return 1.0 / (1.0 + jnp.exp(-x))


def _silu(x):
    return x * _sig(x)


def _dsilu(x):
    s = _sig(x)
    return s * (1.0 + x * (1.0 - s))


_GK = math.sqrt(2.0 / math.pi)


def _gelu(x):
    return 0.5 * x * (1.0 + jnp.tanh(_GK * (x + 0.044715 * x * x * x)))


def _dgelu(x):
    t = jnp.tanh(_GK * (x + 0.044715 * x * x * x))
    return 0.5 * (1.0 + t) + 0.5 * x * (1.0 - t * t) * _GK * (1.0 + 3 * 0.044715 * x * x)


def _rs(x):
    return lax.rsqrt(jnp.mean(x * x, axis=-1, keepdims=True) + EPS)


def _sum0(x):
    return jnp.sum(x, axis=0, keepdims=True)


def st_norm_mod(x, g, sc, sh):
    y = x * _rs(x) * g
    return (y * (1.0 + sc) + sh,), ()


def st_norm_mod_bwd(x, dh, dres, g, sc):
    r = _rs(x)
    xn = x * r
    y = xn * g
    dy = dh * (1.0 + sc)
    dxn = dy * g
    dx = r * (dxn - xn * jnp.mean(dxn * xn, axis=-1, keepdims=True))
    return (dres + dx,), (_sum0(dh), _sum0(dh * y), _sum0(dy * xn))


def st_rms(x, g):
    return (x * _rs(x) * g,), ()


def st_rms_bwd(x, dy, g):
    r = _rs(x)
    n = x * r
    dn = dy * g
    dx = r * (dn - n * jnp.mean(dn * n, axis=-1, keepdims=True))
    return (dx,), (_sum0(dy * n),)


def st_gate(o, z):
    return (o * _silu(z),), ()


def st_gate_bwd(dog, o, z):
    return (dog * _silu(z), dog * o * _dsilu(z)), ()


def st_resid(x, out, gt):
    return (x + gt * out,), ()


def st_resid_bwd(dx, out, gt):
    return (dx * gt,), (_sum0(dx * out),)


def st_s5a(yssm, u, d):
    y = yssm + d * u
    return (y, _gelu(y)), ()


def st_s5b(y, gl, z, b):
    return (_gelu(y) * _sig(gl + b) * _silu(z),), ()


def st_s5b_bwd(dy3, y, gl, z, b):
    y1 = _gelu(y)
    s = _sig(gl + b)
    dy2 = dy3 * _silu(z)
    dz = dy3 * y1 * s * _dsilu(z)
    dgl = dy2 * y1 * s * (1.0 - s)
    return (dgl, dz, dy2 * s), (_sum0(dgl),)


def st_s5a_bwd(dy1a, dy1b, y, u, d):
    dy = (dy1a + dy1b) * _dgelu(y)
    return (dy, dy * d), (_sum0(dy * u),)


def st_final(x2, tgt, g):
    r = _rs(x2)
    n = x2 * r
    e = n * g - tgt
    dyo = e * (1.0 / D)
    dn = dyo * g
    dx = r * (dn - n * jnp.mean(dn * n, axis=-1, keepdims=True))
    lsum = jnp.sum(_sum0(e * e), axis=1, keepdims=True) * (0.5 / D)
    return (dx,), (_sum0(dyo * n), jnp.broadcast_to(lsum, (1, 128)))


def rowwise(fn, rows, vecs, out_rows, out_sums, name, lat_only=False):
    rows = [a if isinstance(a, tuple) else (a, 0, a.shape[1]) for a in rows]
    nrows = L if lat_only else rows[0][0].shape[0]
    nb = nrows // TB
    nr, nv, no = len(rows), len(vecs), len(out_rows)

    def body(*refs):
        i = pl.program_id(0)
        vals = [r[...] for r in refs[:nr]] + [r[0] for r in refs[nr:nr + nv]]
        outs, sums = fn(*vals)
        for r, o in zip(refs[nr + nv:nr + nv + no], outs):
            r[...] = o.astype(r.dtype)
        sum_refs = refs[nr + nv + no:]
        if sum_refs:
            @pl.when((i == 0) if lat_only else (i <= 1))
            def _():
                for r in sum_refs:
                    r[...] = jnp.zeros_like(r)
            for r, s in zip(sum_refs, sums):
                r[0] += s

    def row_spec(a):
        arr, cb, width = a
        off = (arr.shape[0] - nrows) // TB
        return pl.BlockSpec((TB, width), lambda i: (i + off, cb))

    if lat_only:
        kind = lambda i: (1, 0, 0)
    else:
        kind = lambda i: (jnp.minimum(i, 1), 0, 0)
    in_specs = [row_spec(a) for a in rows] + [pl.BlockSpec((1, 1, v.shape[2]), kind) for v in vecs]
    out_specs = [pl.BlockSpec((TB, c), lambda i: (i, 0)) for c, _ in out_rows] + \
                [pl.BlockSpec((1, 1, c), kind) for c in out_sums]
    out_shape = [jax.ShapeDtypeStruct((nrows, c), dt) for c, dt in out_rows] + \
                [jax.ShapeDtypeStruct((2, 1, c), F32) for c in out_sums]
    res = pl.pallas_call(body, grid=(nb,), in_specs=in_specs, out_specs=out_specs, out_shape=out_shape,
                         compiler_params=_cp(("arbitrary",)), name=name)(*[a[0] for a in rows], *vecs)
    return res[:no], res[no:]


_DN = {"nn": (((1,), (0,)), ((), ())), "nt": (((1,), (1,)), ((), ())), "tn": (((0,), (0,)), ((), ()))}


def mm(a, b, mode, name, out_dtype=F32, tm=256, tn=None):
    if mode == "nn":
        (M, K), (_, N) = a.shape, b.shape
    elif mode == "nt":
        (M, K), (N, _) = a.shape, b.shape
    else:
        (K, M), (_, N) = a.shape, b.shape
    tm = min(tm, M)
    tn = N if tn is None else tn
    dn = _DN[mode]

    def body(a_ref, b_ref, o_ref):
        o_ref[...] = lax.dot_general(a_ref[...].astype(BF16), b_ref[...].astype(BF16), dn,
                                     preferred_element_type=F32).astype(o_ref.dtype)

    a_spec = pl.BlockSpec((K, tm), lambda i, j: (0, i)) if mode == "tn" else pl.BlockSpec((tm, K), lambda i, j: (i, 0))
    b_spec = pl.BlockSpec((tn, K), lambda i, j: (j, 0)) if mode == "nt" else pl.BlockSpec((K, tn), lambda i, j: (0, j))
    return pl.pallas_call(body, grid=(M // tm, N // tn), in_specs=[a_spec, b_spec],
                          out_specs=pl.BlockSpec((tm, tn), lambda i, j: (i, j)),
                          out_shape=jax.ShapeDtypeStruct((M, N), out_dtype),
                          compiler_params=_cp(("parallel", "arbitrary")), name=name)(a, b)


def _rope_tables(T):
    nlat = T - LC
    pos = np.arange(nlat)
    row, col = pos // GRID_W, pos % GRID_W
    half = ROPE // 2
    inv = 1.0 / (THETA ** (np.arange(0, half, 2, dtype=np.float64) / half))
    cosf = np.ones((T, QK), np.float64)
    sinf = np.zeros((T, QK), np.float64)
    perm = np.zeros((QK, QK), np.float32)
    for m in range(ROPE):
        j = NOPE + m
        blk, w = m // half, m % half
        ang = (row if blk == 0 else col)[:, None] * inv[None, :]
        f = w % (half // 2)
        cosf[LC:, j] = np.cos(ang[:, f])
        if w < half // 2:
            sinf[LC:, j] = -np.sin(ang[:, f])
            perm[j + half // 2, j] = 1.0
        else:
            sinf[LC:, j] = np.sin(ang[:, f])
            perm[j - half // 2, j] = 1.0
    return jnp.asarray(cosf, F32), jnp.asarray(sinf, F32), jnp.asarray(perm, BF16), jnp.asarray(perm.T, BF16)


def _exact_perm(x, pm):
    hi = x.astype(BF16)
    r1 = x - hi.astype(F32)
    mid = r1.astype(BF16)
    lo = (r1 - mid.astype(F32)).astype(BF16)
    dot = lambda a: jnp.dot(a, pm, preferred_element_type=F32)
    return dot(hi) + dot(mid) + dot(lo)


def rope(x, cosf, sinf, pm, inverse, out_dtype, name, head_sum=False):
    H, T, _ = x.shape

    def body(x_ref, c_ref, s_ref, p_ref, o_ref, *rest):
        xv = x_ref[0]
        if inverse:
            out = xv * c_ref[...] + _exact_perm(xv * s_ref[...], p_ref[...])
        else:
            out = xv * c_ref[...] + _exact_perm(xv, p_ref[...]) * s_ref[...]
        o_ref[0] = out.astype(o_ref.dtype)
        if head_sum:
            acc = rest[0]

            @pl.when(pl.program_id(1) == 0)
            def _():
                acc[...] = jnp.zeros_like(acc)
            acc[...] += out

    out_shape = [jax.ShapeDtypeStruct((H, T, QK), out_dtype)]
    out_specs = [pl.BlockSpec((1, TB, QK), lambda i, h: (h, i, 0))]
    if head_sum:
        out_shape.append(jax.ShapeDtypeStruct((T, QK), F32))
        out_specs.append(pl.BlockSpec((TB, QK), lambda i, h: (i, 0)))
    res = pl.pallas_call(
        body, grid=(T // TB, H),
        in_specs=[pl.BlockSpec((1, TB, QK), lambda i, h: (h, i, 0)), pl.BlockSpec((TB, QK), lambda i, h: (i, 0)),
                  pl.BlockSpec((TB, QK), lambda i, h: (i, 0)), pl.BlockSpec((QK, QK), lambda i, h: (0, 0))],
        out_specs=out_specs, out_shape=out_shape, compiler_params=_cp(("arbitrary", "arbitrary")), name=name)(x, cosf, sinf, pm)
    return res if head_sum else res[0]


def _scores(q, k, qi):
    s = lax.dot_general(q, k, _DN["nt"], preferred_element_type=F32) * SCALE
    col = lax.broadcasted_iota(jnp.int32, s.shape, 1)
    return jnp.where(jnp.logical_and(qi == 0, col >= LC), -1e30, s)


def attn_fwd(q, k, v, name):
    H, T, _ = q.shape

    def body(q_ref, k_ref, v_ref, o_ref, lse_ref):
        s = _scores(q_ref[0], k_ref[0], pl.program_id(1))
        m = jnp.max(s, axis=1, keepdims=True)
        p = jnp.exp(s - m)
        l = jnp.sum(p, axis=1, keepdims=True)
        o = jnp.dot(p.astype(BF16), v_ref[0], preferred_element_type=F32)
        o_ref[0] = o / l
        lse_ref[0] = m + jnp.log(l)

    return pl.pallas_call(
        body, grid=(H, T // TB),
        in_specs=[pl.BlockSpec((1, TB, QK), lambda h, i: (h, i, 0)), pl.BlockSpec((1, T, QK), lambda h, i: (h, 0, 0)),
                  pl.BlockSpec((1, T, VD), lambda h, i: (h, 0, 0))],
        out_specs=[pl.BlockSpec((1, TB, VD), lambda h, i: (h, i, 0)), pl.BlockSpec((1, TB, 1), lambda h, i: (h, i, 0))],
        out_shape=[jax.ShapeDtypeStruct((H, T, VD), F32), jax.ShapeDtypeStruct((H, T, 1), F32)],
        compiler_params=_cp(("parallel", "arbitrary")), name=name)(q, k, v)


def attn_bwd(q, k, v, o, lse, do, name):
    H, T, _ = q.shape

    def body(q_ref, k_ref, v_ref, o_ref, lse_ref, do_ref, dq_ref, dk_ref, dv_ref):
        i = pl.program_id(1)

        @pl.when(i == 0)
        def _():
            dk_ref[...] = jnp.zeros_like(dk_ref)
            dv_ref[...] = jnp.zeros_like(dv_ref)

        qv, kv, dov = q_ref[0], k_ref[0], do_ref[0]
        p = jnp.exp(_scores(qv, kv, i) - lse_ref[0])
        delta = jnp.sum(dov * o_ref[0], axis=1, keepdims=True)
        dob = dov.astype(BF16)
        dv_ref[0] += lax.dot_general(p.astype(BF16), dob, _DN["tn"], preferred_element_type=F32)
        dp = lax.dot_general(dob, v_ref[0], _DN["nt"], preferred_element_type=F32)
        ds = (p * (dp - delta) * SCALE).astype(BF16)
        dq_ref[0] = jnp.dot(ds, kv, preferred_element_type=F32)
        dk_ref[0] += lax.dot_general(ds, qv, _DN["tn"], preferred_element_type=F32)

    blk = lambda c: pl.BlockSpec((1, TB, c), lambda h, i: (h, i, 0))
    full = lambda c: pl.BlockSpec((1, T, c), lambda h, i: (h, 0, 0))
    return pl.pallas_call(
        body, grid=(H, T // TB),
        in_specs=[blk(QK), full(QK), full(VD), blk(VD), blk(1), blk(VD)],
        out_specs=[blk(QK), full(QK), full(VD)],
        out_shape=[jax.ShapeDtypeStruct((H, T, QK), F32), jax.ShapeDtypeStruct((H, T, QK), F32),
                   jax.ShapeDtypeStruct((H, T, VD), F32)],
        compiler_params=_cp(("parallel", "arbitrary")), name=name)(q, k, v, o, lse, do)


def disc_fwd(a_re, a_im, ls, name):
    def body(ar_ref, ai_ref, ls_ref, lr_ref, li_ref, fr_ref, fi_ref):
        ar, ai = ar_ref[...], ai_ref[...]
        dt = jnp.exp(ls_ref[...])
        mag = jnp.exp(ar * dt)
        lr = mag * jnp.cos(ai * dt)
        li = mag * jnp.sin(ai * dt)
        den = ar * ar + ai * ai
        nr = lr - 1.0
        lr_ref[...] = lr
        li_ref[...] = li
        fr_ref[...] = (nr * ar + li * ai) / den
        fi_ref[...] = (li * ar - nr * ai) / den

    return pl.pallas_call(body, out_shape=[jax.ShapeDtypeStruct(a_re.shape, F32)] * 4, name=name)(a_re, a_im, ls)


def disc_b(f_re, f_im, b_re, b_im, name):
    def body(fr_ref, fi_ref, br_ref, bi_ref, or_ref, oi_ref):
        fr, fi, br, bi = fr_ref[...], fi_ref[...], br_ref[...], bi_ref[...]
        or_ref[...] = fr * br - fi * bi
        oi_ref[...] = fr * bi + fi * br

    fs, bs = _disc_b_specs()
    return pl.pallas_call(body, grid=(2, G * P // DISC_ROWS), in_specs=[fs, fs, bs, bs], out_specs=[bs, bs],
                          out_shape=[jax.ShapeDtypeStruct(b_re.shape, F32)] * 2, name=name)(f_re, f_im, b_re, b_im)


DISC_ROWS = 1024


def _disc_b_specs():
    return (pl.BlockSpec((1, DISC_ROWS, 1), lambda d, i: (d, i, 0)), pl.BlockSpec((1, DISC_ROWS, CH), lambda d, i: (d, i, 0)))


def disc_b_bwd(f_re, f_im, b_re, b_im, dbb_re, dbb_im, name):
    def body(fr_ref, fi_ref, br_ref, bi_ref, dr_ref, di_ref, dbr_ref, dbi_ref, dfr_ref, dfi_ref):
        fr, fi, br, bi, dr, di = fr_ref[...], fi_ref[...], br_ref[...], bi_ref[...], dr_ref[...], di_ref[...]
        dbr_ref[...] = fr * dr + fi * di
        dbi_ref[...] = fr * di - fi * dr
        dfr_ref[...] = jnp.sum(dr * br + di * bi, axis=-1, keepdims=True)
        dfi_ref[...] = jnp.sum(di * br - dr * bi, axis=-1, keepdims=True)

    fs, bs = _disc_b_specs()
    return pl.pallas_call(body, grid=(2, G * P // DISC_ROWS), in_specs=[fs, fs, bs, bs, bs, bs], out_specs=[bs, bs, fs, fs],
                          out_shape=[jax.ShapeDtypeStruct(b_re.shape, F32)] * 2 + [jax.ShapeDtypeStruct(f_re.shape, F32)] * 2,
                          name=name)(f_re, f_im, b_re, b_im, dbb_re, dbb_im)


def disc_a_bwd(a_re, a_im, ls, dlr, dli, dfr, dfi, name):
    def body(ar_ref, ai_ref, ls_ref, dlr_ref, dli_ref, dfr_ref, dfi_ref, dar_ref, dai_ref, dls_ref):
        ar, ai = ar_ref[...], ai_ref[...]
        dt = jnp.exp(ls_ref[...])
        mag = jnp.exp(ar * dt)
        cs, sn = jnp.cos(ai * dt), jnp.sin(ai * dt)
        lr, li = mag * cs, mag * sn
        den = ar * ar + ai * ai
        nr = lr - 1.0
        f_re = (nr * ar + li * ai) / den
        f_im = (li * ar - nr * ai) / den
        dn1 = dfr_ref[...] / den
        dn2 = dfi_ref[...] / den
        dden = -(dfr_ref[...] * f_re + dfi_ref[...] * f_im) / den
        dlr_t = dlr_ref[...] + dn1 * ar - dn2 * ai
        dli_t = dli_ref[...] + dn1 * ai + dn2 * ar
        dar = dn1 * nr + dn2 * li + dden * 2.0 * ar
        dai = dn1 * li - dn2 * nr + dden * 2.0 * ai
        dmag = dlr_t * cs + dli_t * sn
        dth = dli_t * lr - dlr_t * li
        dar_ref[...] = dar + dmag * mag * dt
        dai_ref[...] = dai + dth * dt
        dls_ref[...] = jnp.sum(dmag * mag * ar + dth * ai, axis=-1, keepdims=True) * dt

    return pl.pallas_call(body, out_shape=[jax.ShapeDtypeStruct(a_re.shape, F32)] * 2 +
                          [jax.ShapeDtypeStruct(ls.shape, F32)], name=name)(a_re, a_im, ls, dlr, dli, dfr, dfi)


def _cpow(lr, li, n):
    rr, ri = None, None
    br, bi = lr, li
    while n:
        if n & 1:
            if rr is None:
                rr, ri = br, bi
            else:
                rr, ri = rr * br - ri * bi, rr * bi + ri * br
        n >>= 1
        if n:
            br, bi = br * br - bi * bi, 2.0 * br * bi
    return rr, ri


NQ = SB // 128
UNROLL = 4


def _seg_scan(xre, xim, lam8, pw, base, seglen, rev, init, fin_re, fin_im, ini_re, ini_im, prev=None):
    def tile(ref, q, t):
        return ref[q, pl.ds(base + t, SEG, stride=seglen), :]

    tmap = (lambda n: seglen - 1 - n) if rev else (lambda n: n)
    zeros = tuple(jnp.zeros((SEG, 128), F32) for _ in range(2 * NQ))

    def advance(c, t):
        out = []
        for q in range(NQ):
            (lr, li), a, b = lam8[q], c[2 * q], c[2 * q + 1]
            out.append(lr * a - li * b + tile(xre, q, t))
            out.append(lr * b + li * a + tile(xim, q, t))
        return out

    fin = lax.fori_loop(0, seglen, lambda n, c: tuple(advance(c, tmap(n))), zeros, unroll=UNROLL)
    cur = list(init)
    for q in range(NQ):
        fin_re[q] = fin[2 * q]
        fin_im[q] = fin[2 * q + 1]
    for i in (range(SEG - 1, -1, -1) if rev else range(SEG)):
        for q in range(NQ):
            (cr, ci), (pr, pi) = cur[q], pw[q]
            ini_re[q, pl.ds(i, 1), :] = cr
            ini_im[q, pl.ds(i, 1), :] = ci
            cur[q] = (pr * cr - pi * ci + fin_re[q, pl.ds(i, 1), :], pr * ci + pi * cr + fin_im[q, pl.ds(i, 1), :])
    start = tuple(r[q] for q in range(NQ) for r in (ini_re, ini_im))

    def store(c, t):
        new = advance(c, t)
        for q in range(NQ):
            xre[q, pl.ds(base + t, SEG, stride=seglen), :] = new[2 * q]
            xim[q, pl.ds(base + t, SEG, stride=seglen), :] = new[2 * q + 1]
        return new

    if prev is None:
        lax.fori_loop(0, seglen, lambda n, c: tuple(store(c, tmap(n))), start, unroll=UNROLL)
        return cur, None

    sre, sim, s_ini_re, s_ini_im = prev

    def acc_step(c, t, pre, pim):
        new = store(c[:2 * NQ], t)
        acc = []
        for q in range(NQ):
            na, nb = new[2 * q], new[2 * q + 1]
            acc.append(c[2 * NQ + 2 * q] + na * pre[q] + nb * pim[q])
            acc.append(c[2 * NQ + 2 * q + 1] + nb * pre[q] - na * pim[q])
        return tuple(new) + tuple(acc)

    def body(n, c):
        t = tmap(n)
        tp = t - 1 if rev else t + 1
        return acc_step(c, t, [tile(sre, q, tp) for q in range(NQ)], [tile(sim, q, tp) for q in range(NQ)])

    c = lax.fori_loop(0, seglen - 1, body, start + zeros, unroll=UNROLL)
    c = acc_step(c, 0 if rev else seglen - 1, [s_ini_re[q] for q in range(NQ)], [s_ini_im[q] for q in range(NQ)])
    return cur, c[2 * NQ:]


def _lam_chunks(lr, li, lens, conj=False):
    if conj:
        li = -li
    lam8, pws = [], [[] for _ in lens]
    for q in range(NQ):
        r, i = lr[:, 128 * q:128 * (q + 1)], li[:, 128 * q:128 * (q + 1)]
        lam8.append((jnp.broadcast_to(r, (SEG, 128)), jnp.broadcast_to(i, (SEG, 128))))
        for k, n in enumerate(lens):
            pws[k].append(_cpow(r, i, n))
    return lam8, pws


def _fill(ref, val):
    for q in range(NQ):
        ref[q] = val[:, 128 * q:128 * (q + 1)]


def _chunk(ref, d, q):
    return ref[d, 0, :, pl.ds(128 * q, 128)].astype(BF16)


def _scan_specs(T):
    ublk = pl.BlockSpec((T, UB), lambda j: (0, j))
    lam = pl.BlockSpec((2, 1, 1, SB), lambda j: (0, j, 0, 0))
    mat = pl.BlockSpec((2, 1, UB, SB), lambda j: (0, j, 0, 0))
    return ublk, lam, mat


def _dotf(a, b, mode="nn"):
    return lax.dot_general(a, b, _DN[mode], preferred_element_type=F32)


def _zero_state():
    return [(jnp.zeros((1, 128), F32), jnp.zeros((1, 128), F32)) for _ in range(NQ)]


def scan_fwd(u, lam_re, lam_im, bre, bim, cre, cim, name):
    T = u.shape[0]
    s_ctx, s_lat = LC // SEG, (T - LC) // SEG

    def body(u_ref, lr_ref, li_ref, bre_ref, bim_ref, cre_ref, cim_ref, y_ref, sre, sim, fre, fim, ire, iim):
        ub = u_ref[...].astype(BF16)
        y = None
        for d in range(2):
            lam8, (pw_c, pw_l) = _lam_chunks(lr_ref[d, 0], li_ref[d, 0], (s_ctx, s_lat))
            _fill(sre, _dotf(ub, bre_ref[d, 0].astype(BF16)))
            _fill(sim, _dotf(ub, bim_ref[d, 0].astype(BF16)))
            end_c, _ = _seg_scan(sre, sim, lam8, pw_c, 0, s_ctx, bool(d), _zero_state(), fre, fim, ire, iim)
            _seg_scan(sre, sim, lam8, pw_l, LC, s_lat, bool(d), end_c, fre, fim, ire, iim)
            for q in range(NQ):
                part = (_dotf(sre[q].astype(BF16), _chunk(cre_ref, d, q), "nt")
                        - _dotf(sim[q].astype(BF16), _chunk(cim_ref, d, q), "nt"))
                y = part if y is None else y + part
        y_ref[...] = y

    ublk, lam, mat = _scan_specs(T)
    return pl.pallas_call(
        body, grid=(NJ,), in_specs=[ublk, lam, lam, mat, mat, mat, mat], out_specs=ublk,
        out_shape=jax.ShapeDtypeStruct((T, G * CH), F32),
        scratch_shapes=[pltpu.VMEM((NQ, T, 128), F32)] * 2 + [pltpu.VMEM((NQ, SEG, 128), F32)] * 4,
        compiler_params=_cp(("arbitrary",)), name=name)(u, lam_re, lam_im, bre, bim, cre, cim)


def scan_bwd(u, dy, lam_re, lam_im, bre, bim, cre, cim, name):
    T = u.shape[0]
    s_ctx, s_lat = LC // SEG, (T - LC) // SEG

    def body(u_ref, dy_ref, lr_ref, li_ref, bre_ref, bim_ref, cre_ref, cim_ref,
             du_ref, dlr_ref, dli_ref, dbre_ref, dbim_ref, dcre_ref, dcim_ref,
             sre, sim, gre, gim, fre, fim, ic_re, ic_im, il_re, il_im, jre, jim):
        ub, dyb = u_ref[...].astype(BF16), dy_ref[...].astype(BF16)
        du = None
        for d in range(2):
            rev = bool(d)
            lam8, (pw_c, pw_l) = _lam_chunks(lr_ref[d, 0], li_ref[d, 0], (s_ctx, s_lat))
            cam8, (cw_c, cw_l) = _lam_chunks(lr_ref[d, 0], li_ref[d, 0], (s_ctx, s_lat), conj=True)
            _fill(sre, _dotf(ub, bre_ref[d, 0].astype(BF16)))
            _fill(sim, _dotf(ub, bim_ref[d, 0].astype(BF16)))
            end_c, _ = _seg_scan(sre, sim, lam8, pw_c, 0, s_ctx, rev, _zero_state(), fre, fim, ic_re, ic_im)
            _seg_scan(sre, sim, lam8, pw_l, LC, s_lat, rev, end_c, fre, fim, il_re, il_im)
            _fill(gre, _dotf(dyb, cre_ref[d, 0].astype(BF16)))
            _fill(gim, -_dotf(dyb, cim_ref[d, 0].astype(BF16)))
            end_g, acc_l = _seg_scan(gre, gim, cam8, cw_l, LC, s_lat, not rev, _zero_state(), fre, fim, jre, jim,
                                     prev=(sre, sim, il_re, il_im))
            _, acc_c = _seg_scan(gre, gim, cam8, cw_c, 0, s_ctx, not rev, end_g, fre, fim, jre, jim,
                                 prev=(sre, sim, ic_re, ic_im))
            for q in range(NQ):
                cols = pl.ds(128 * q, 128)
                dlr_ref[d, 0, :, cols] = _sum0(acc_l[2 * q] + acc_c[2 * q])
                dli_ref[d, 0, :, cols] = _sum0(acc_l[2 * q + 1] + acc_c[2 * q + 1])
                grb, gib = gre[q].astype(BF16), gim[q].astype(BF16)
                part = _dotf(grb, _chunk(bre_ref, d, q), "nt") + _dotf(gib, _chunk(bim_ref, d, q), "nt")
                du = part if du is None else du + part
                dbre_ref[d, 0, :, cols] = _dotf(ub, grb, "tn")
                dbim_ref[d, 0, :, cols] = _dotf(ub, gib, "tn")
                dcre_ref[d, 0, :, cols] = _dotf(dyb, sre[q].astype(BF16), "tn")
                dcim_ref[d, 0, :, cols] = -_dotf(dyb, sim[q].astype(BF16), "tn")
        du_ref[...] = du

    ublk, lam, mat = _scan_specs(T)
    lam_s = jax.ShapeDtypeStruct(lam_re.shape, F32)
    mat_s = jax.ShapeDtypeStruct(bre.shape, F32)
    return pl.pallas_call(
        body, grid=(NJ,), in_specs=[ublk, ublk, lam, lam, mat, mat, mat, mat],
        out_specs=[ublk, lam, lam, mat, mat, mat, mat],
        out_shape=[jax.ShapeDtypeStruct((T, G * CH), F32), lam_s, lam_s, mat_s, mat_s, mat_s, mat_s],
        scratch_shapes=[pltpu.VMEM((NQ, T, 128), F32)] * 4 + [pltpu.VMEM((NQ, SEG, 128), F32)] * 8,
        compiler_params=_cp(("arbitrary",)), name=name)(u, dy, lam_re, lam_im, bre, bim, cre, cim)


def _block_diag(m):
    m5 = m.reshape(2, NJ, GB, CH, P)
    eye = jnp.eye(GB, dtype=m.dtype)
    return (m5[:, :, :, :, None, :] * eye[None, None, :, None, :, None]).reshape(2, NJ, UB, SB)


def _diag_blocks(m):
    m6 = m.reshape(2, NJ, GB, CH, GB, P)
    idx = jnp.arange(GB)
    return m6[:, :, idx, :, idx, :].transpose(1, 2, 0, 3, 4).reshape(2, G, CH, P)


def exchange(xs, scatter, name):
    n = len(xs)

    def body(*refs):
        x_refs, out_refs = refs[:n], refs[n:2 * n]
        send_sems, recv_sems, local_sems = refs[2 * n:]
        mx, my, mc = lax.axis_index("x"), lax.axis_index("y"), lax.axis_index("c")
        me = 4 * mx + 2 * my + mc
        copies = []
        for a, (x_ref, out_ref) in enumerate(zip(x_refs, out_refs)):
            local = pltpu.make_async_copy(x_ref.at[me] if scatter else x_ref, out_ref.at[me], local_sems.at[a])
            local.start()
            copies.append(local)
        sends, recvs = [], []
        for k in range(1, NDEV):
            peer = (1 - mx if k & 4 else mx, 1 - my if k & 2 else my, 1 - mc if k & 1 else mc)
            pid = 4 * peer[0] + 2 * peer[1] + peer[2]
            for a, (x_ref, out_ref) in enumerate(zip(x_refs, out_refs)):
                src = x_ref.at[pid] if scatter else x_ref
                cp = pltpu.make_async_remote_copy(
                    src_ref=src, dst_ref=out_ref.at[me], send_sem=send_sems.at[k - 1, a], recv_sem=recv_sems.at[k - 1, a],
                    device_id=peer, device_id_type=MESH_T)
                cp.start()
                sends.append(cp)
                recvs.append(pltpu.make_async_remote_copy(
                    src_ref=src, dst_ref=out_ref.at[pid], send_sem=send_sems.at[k - 1, a], recv_sem=recv_sems.at[k - 1, a],
                    device_id=peer, device_id_type=MESH_T))
        for cp in recvs:
            cp.wait_recv()
        for cp in sends:
            cp.wait_send()
        for cp in copies:
            cp.wait()

    hbm = pl.BlockSpec(memory_space=pl.ANY)
    return pl.pallas_call(
        body, in_specs=[hbm] * n, out_specs=[hbm] * n,
        out_shape=[jax.ShapeDtypeStruct((NDEV,) + tuple(x.shape[1:] if scatter else x.shape), x.dtype) for x in xs],
        scratch_shapes=[pltpu.SemaphoreType.DMA((NDEV - 1, n)), pltpu.SemaphoreType.DMA((NDEV - 1, n)),
                        pltpu.SemaphoreType.DMA((n,))],
        compiler_params=pltpu.CompilerParams(has_side_effects=True), name=name)(*xs)


def _dot_f32(a, b, dn):
    return lax.dot_general(a, b, dn, preferred_element_type=F32, precision=lax.Precision.HIGHEST)


def ada_fwd(cg, c_ctx, ada_w, ada_b_loc, name):
    W = ada_w.shape[2]

    def body(cg_ref, cc_ref, w_ref, b_ref, o_ref):
        a = jnp.concatenate([_silu(cg_ref[...]), jnp.broadcast_to(_silu(cc_ref[...]), (NDEV, D))], axis=0)
        for i in range(2):
            o_ref[i] = _dot_f32(a, w_ref[i], _DN["nn"]) + b_ref[i]

    return pl.pallas_call(body, out_shape=jax.ShapeDtypeStruct((2, 2 * NDEV, W), F32),
                          compiler_params=_cp(), name=name)(cg, c_ctx, ada_w, ada_b_loc)


def ada_bwd(cg, c_ctx, ada_w, dm_loc, dm_all, name):
    W = ada_w.shape[2]

    def body(cg_ref, cc_ref, w_ref, dl_ref, da_ref, gw_ref, dcc_ref, gb_ref):
        a = jnp.concatenate([_silu(cg_ref[...]), jnp.broadcast_to(_silu(cc_ref[...]), (NDEV, D))], axis=0)
        dcc = jnp.zeros((1, D), F32)
        for i in range(2):
            dl = dl_ref[i]
            gw_ref[i] = _dot_f32(a, dl, _DN["tn"])
            dctx = jnp.sum(dl[NDEV:], axis=0, keepdims=True)
            dcc = dcc + _dot_f32(dctx, w_ref[i], _DN["nt"])
        dcc_ref[...] = dcc
        gb_ref[...] = jnp.sum(da_ref[...], axis=0)

    return pl.pallas_call(body, out_shape=[jax.ShapeDtypeStruct((2, D, W), F32), jax.ShapeDtypeStruct((1, D), F32),
                                           jax.ShapeDtypeStruct((2, 3 * D), F32)],
                          compiler_params=_cp(), name=name)(cg, c_ctx, ada_w, dm_loc, dm_all)


def cctx_finish(parts, c_ctx, name):
    def body(p_ref, cc_ref, o_ref):
        o_ref[...] = jnp.sum(p_ref[...], axis=0, keepdims=True) * _dsilu(cc_ref[...])

    return pl.pallas_call(body, out_shape=jax.ShapeDtypeStruct((1, D), F32), name=name)(parts, c_ctx)


def adamw(gstack, w, m, v, name, tr=512):
    n, R, C = gstack.shape
    tr = max(t for t in range(8, min(tr, R) + 1, 8) if R % t == 0)
    nb = R // tr
    c1 = 1.0 / (1.0 - B1 ** STEP)
    c2 = 1.0 / (1.0 - B2 ** STEP)

    def body(g_ref, w_ref, m_ref, v_ref, go_ref, d_ref, mo_ref, vo_ref):
        g = g_ref[0]
        for s in range(1, n):
            g = g + g_ref[s]
        mn = B1 * m_ref[...] + (1.0 - B1) * g
        vn = B2 * v_ref[...] + (1.0 - B2) * g * g
        go_ref[...] = g
        mo_ref[...] = mn
        vo_ref[...] = vn
        d_ref[...] = -LR * ((mn * c1) / (jnp.sqrt(vn * c2) + AEPS) + WD * w_ref[...])

    spec = pl.BlockSpec((tr, C), lambda i: (i, 0))
    return pl.pallas_call(body, grid=(nb,), in_specs=[pl.BlockSpec((n, tr, C), lambda i: (0, i, 0)), spec, spec, spec],
                          out_specs=[spec] * 4, out_shape=[jax.ShapeDtypeStruct((R, C), F32)] * 4,
                          compiler_params=_cp(("parallel",)), name=name)(gstack, w, m, v)


def _pack(parts, rows=None):
    flat = jnp.concatenate([p.reshape(-1) for p in parts])
    rows = 8 * (-(-flat.shape[0] // (8 * LANES))) if rows is None else rows
    return jnp.pad(flat, (0, rows * LANES - flat.shape[0])).reshape(rows, LANES)


def _unpack(slab, shapes):
    flat = slab.reshape(-1)
    out, off = [], 0
    for s in shapes:
        n = int(np.prod(s))
        out.append(flat[off:off + n].reshape(s))
        off += n
    return out


def _col_shards(g):
    R, N = g.shape
    return g.reshape(R, NDEV, N // NDEV).transpose(1, 0, 2).reshape(NDEV, -1)


def _from_col_shards(a):
    n, R, w = a.shape
    return a.transpose(1, 0, 2).reshape(R, n * w)


def _vec2(v):
    return jnp.broadcast_to(v.reshape(1, 1, -1), (2, 1, v.size))


SHARD_ROWS = {"mla_w_in": 192, "mla_w_uq": 192, "mla_w_ukv": 256, "s5_w_in": 256}


def _t_shard(wsh, rows):
    t = wsh[0].T.astype(BF16)
    return jnp.pad(t, ((0, rows - t.shape[0]), (0, 0)))


def _win_order():
    w = IN_W // NDEV
    perm = np.zeros((IN_WP, NDEV * SHARD_ROWS["mla_w_in"]), np.float32)
    first = QL + KVL + ROPE
    for c in range(IN_W):
        n = c + HEADS * VD if c < first else c - first
        perm[n, (c // w) * SHARD_ROWS["mla_w_in"] + c % w] = 1.0
    return jnp.asarray(perm, BF16)


def local_step(xa, tgt, mod, Wt, small):
    T = xa.shape[0]
    sh = [mod[i, :, None, 0:D] for i in range(2)]
    sc = [mod[i, :, None, D:2 * D] for i in range(2)]
    gt = [mod[i, :, None, 2 * D:] for i in range(2)]
    ng = [_vec2(small["norm_g"][i]) for i in range(2)]
    qg, kvg = _vec2(small["mla_q_norm"]), _vec2(small["mla_kv_norm"])
    cosf, sinf, pm, pmt = _rope_tables(T)

    (h0,), _ = rowwise(st_norm_mod, [xa], [ng[0], sc[0], sh[0]], [(D, BF16)], [], "l0_norm")
    p0 = mm(h0, Wt["mla_w_in"], "nt", "l0_in")
    z0, cq, ckv = (p0, 0, HEADS * VD), (p0, HEADS * VD // QL, QL), (p0, (HEADS * VD + QL) // KVL, KVL)
    kr = p0[:, HEADS * VD + QL + KVL:HEADS * VD + QL + KVL + ROPE]
    (cqn,), _ = rowwise(st_rms, [cq], [qg], [(QL, BF16)], [], "l0_qnorm")
    (ckvn,), _ = rowwise(st_rms, [ckv], [kvg], [(KVL, BF16)], [], "l0_kvnorm")
    q = mm(cqn, Wt["mla_w_uq"], "nt", "l0_uq")
    kv = mm(ckvn, Wt["mla_w_ukv"], "nt", "l0_ukv")
    qh = q.reshape(T, HEADS, QK).transpose(1, 0, 2)
    kvh = kv.reshape(T, HEADS, NOPE + VD).transpose(1, 0, 2)
    kraw = jnp.concatenate([kvh[..., :NOPE], jnp.broadcast_to(kr[None], (HEADS, T, ROPE))], axis=-1)
    Q = rope(qh, cosf, sinf, pm, False, BF16, "l0_rope_q")
    K = rope(kraw, cosf, sinf, pm, False, BF16, "l0_rope_k")
    V = kvh[..., NOPE:].astype(BF16)
    o, lse = attn_fwd(Q, K, V, "l0_attn")
    o2 = o.transpose(1, 0, 2).reshape(T, HEADS * VD)
    (og,), _ = rowwise(st_gate, [o2, z0], [], [(D, BF16)], [], "l0_gate")
    out0 = mm(og, Wt["mla_w_out"], "nn", "l0_out")
    (x1,), _ = rowwise(st_resid, [xa, out0], [gt[0]], [(D, F32)], [], "l0_resid")

    ls = small["s5_log_step"].reshape(2, G, 1)
    a_re, a_im = small["s5_a_re"].reshape(2, G, P), small["s5_a_im"].reshape(2, G, P)
    b_re, b_im = small["s5_b_re"].reshape(2, G * P, CH), small["s5_b_im"].reshape(2, G * P, CH)
    lam_re, lam_im, f_re, f_im = disc_fwd(a_re, a_im, ls, "s5_disc")
    f_re2, f_im2 = f_re.reshape(2, G * P, 1), f_im.reshape(2, G * P, 1)
    bb_re, bb_im = disc_b(f_re2, f_im2, b_re, b_im, "s5_disc_b")
    bre = _block_diag(bb_re.reshape(2, G, P, CH).transpose(0, 1, 3, 2))
    bim = _block_diag(bb_im.reshape(2, G, P, CH).transpose(0, 1, 3, 2))
    cre = _block_diag(small["s5_c_re"].reshape(2, G, CH, P))
    cim = _block_diag(small["s5_c_im"].reshape(2, G, CH, P))
    lam_re4, lam_im4 = lam_re.reshape(2, NJ, 1, SB), lam_im.reshape(2, NJ, 1, SB)

    (h1,), _ = rowwise(st_norm_mod, [x1], [ng[1], sc[1], sh[1]], [(D, BF16)], [], "l1_norm")
    p1 = mm(h1, Wt["s5_w_in"], "nt", "l1_in")
    u, z1 = (p1, 0, D), (p1, 1, D)
    yssm = scan_fwd(p1, lam_re4, lam_im4, bre, bim, cre, cim, "s5_scan")
    dvec, bglu = _vec2(small["s5_d"]), _vec2(small["s5_b_glu"])
    (y, y1b), _ = rowwise(st_s5a, [yssm, u], [dvec], [(D, F32), (D, BF16)], [], "l1_gelu")
    gl = mm(y1b, Wt["s5_w_glu"], "nn", "l1_glu")
    (y3,), _ = rowwise(st_s5b, [y, gl, z1], [bglu], [(D, BF16)], [], "l1_gate")
    out1 = mm(y3, Wt["s5_w_out"], "nn", "l1_out")
    (x2,), _ = rowwise(st_resid, [x1, out1], [gt[1]], [(D, F32)], [], "l1_resid")

    fg = _vec2(small["final_g"])
    (dx2l,), (dfg, lvec) = rowwise(st_final, [x2, tgt], [fg], [(D, F32)], [D, 128], "final", lat_only=True)
    dx2 = jnp.concatenate([jnp.zeros((LC, D), F32), dx2l], axis=0)

    (dout1,), (dgt1,) = rowwise(st_resid_bwd, [dx2, out1], [gt[1]], [(D, BF16)], [D], "l1_resid_b")
    g_w_out5 = mm(y3, dout1, "tn", "l1_out_dw")
    dy3 = mm(dout1, Wt["s5_w_out"], "nt", "l1_out_dx")
    (dgl, dz1, dy1a), (dbglu,) = rowwise(st_s5b_bwd, [dy3, y, gl, z1], [bglu], [(D, BF16), (D, BF16), (D, F32)], [D], "l1_gate_b")
    g_w_glu = mm(y1b, dgl, "tn", "l1_glu_dw")
    dy1b = mm(dgl, Wt["s5_w_glu"], "nt", "l1_glu_dx")
    (dy, du_d), (dd,) = rowwise(st_s5a_bwd, [dy1a, dy1b, y, u], [dvec], [(D, F32), (D, F32)], [D], "l1_gelu_b")
    du_s, dlr, dli, dbre, dbim, dcre, dcim = scan_bwd(p1, dy, lam_re4, lam_im4, bre, bim, cre, cim, "s5_scan_b")
    du = du_d + du_s
    dbb_re = _diag_blocks(dbre).transpose(0, 1, 3, 2).reshape(2, G * P, CH)
    dbb_im = _diag_blocks(dbim).transpose(0, 1, 3, 2).reshape(2, G * P, CH)
    g_c_re, g_c_im = _diag_blocks(dcre), _diag_blocks(dcim)
    g_b_re, g_b_im, dfr, dfi = disc_b_bwd(f_re2, f_im2, b_re, b_im, dbb_re, dbb_im, "s5_disc_b_b")
    g_a_re, g_a_im, g_ls = disc_a_bwd(a_re, a_im, ls, dlr.reshape(2, G, P), dli.reshape(2, G, P),
                                      dfr.reshape(2, G, P), dfi.reshape(2, G, P), "s5_disc_b_a")
    dp1 = jnp.concatenate([du.astype(BF16), dz1], axis=1)
    g_w_in5 = mm(h1, dp1, "tn", "l1_in_dw", tn=D)
    dh1 = mm(dp1, Wt["s5_w_in"], "nn", "l1_in_dx")
    (dx1,), (dsh1, dsc1, dng1) = rowwise(st_norm_mod_bwd, [x1, dh1, dx2], [ng[1], sc[1]], [(D, F32)], [D, D, D], "l1_norm_b")

    (dout0,), (dgt0,) = rowwise(st_resid_bwd, [dx1, out0], [gt[0]], [(D, BF16)], [D], "l0_resid_b")
    g_w_out = mm(og, dout0, "tn", "l0_out_dw")
    dog = mm(dout0, Wt["mla_w_out"], "nt", "l0_out_dx")
    (do2, dz0), _ = rowwise(st_gate_bwd, [dog, o2, z0], [], [(D, F32), (D, F32)], [], "l0_gate_b")
    doh = do2.reshape(T, HEADS, VD).transpose(1, 0, 2)
    dQ, dK, dV = attn_bwd(Q, K, V, o, lse, doh, "l0_attn_b")
    dqh = rope(dQ, cosf, sinf, pmt, True, F32, "l0_rope_q_b")
    dkraw, dksum = rope(dK, cosf, sinf, pmt, True, F32, "l0_rope_k_b", head_sum=True)
    dq = dqh.transpose(1, 0, 2).reshape(T, HEADS * QK).astype(BF16)
    dkv = jnp.concatenate([dkraw[..., :NOPE], dV], axis=-1).transpose(1, 0, 2).reshape(T, HEADS * (NOPE + VD)).astype(BF16)
    dkr = dksum[:, NOPE:]
    g_w_uq = mm(cqn, dq, "tn", "l0_uq_dw")
    dcqn = mm(dq, Wt["mla_w_uq"], "nn", "l0_uq_dx")
    g_w_ukv = mm(ckvn, dkv, "tn", "l0_ukv_dw", tm=KVL)
    dckvn = mm(dkv, Wt["mla_w_ukv"], "nn", "l0_ukv_dx")
    (dcq,), (dqg,) = rowwise(st_rms_bwd, [cq, dcqn], [qg], [(QL, F32)], [QL], "l0_qnorm_b")
    (dckv,), (dkvg,) = rowwise(st_rms_bwd, [ckv, dckvn], [kvg], [(KVL, F32)], [KVL], "l0_kvnorm_b")
    dp0 = jnp.concatenate([dz0, dcq, dckv, dkr, jnp.zeros((T, IN_WP - IN_W), F32)], axis=1).astype(BF16)
    g_p = mm(h0, dp0, "tn", "l0_in_dw")
    g_w_in = jnp.concatenate([g_p[:, HEADS * VD:IN_W], g_p[:, :HEADS * VD]], axis=1)
    dh0 = mm(dp0, Wt["mla_w_in"], "nn", "l0_in_dx")
    (dxa,), (dsh0, dsc0, dng0) = rowwise(st_norm_mod_bwd, [xa, dh0, dx1], [ng[0], sc[0]], [(D, F32)], [D, D, D], "l0_norm_b")

    dmod = jnp.stack([jnp.concatenate([dsh0, dsc0, dgt0], axis=-1)[:, 0], jnp.concatenate([dsh1, dsc1, dgt1], axis=-1)[:, 0]])
    both = lambda s: s[0, 0] + s[1, 0]
    gbig = {"mla_w_in": g_w_in, "mla_w_uq": g_w_uq, "mla_w_ukv": g_w_ukv, "mla_w_out": g_w_out,
            "s5_w_in": g_w_in5, "s5_w_glu": g_w_glu, "s5_w_out": g_w_out5, "s5_d": both(dd), "s5_b_glu": both(dbglu)}
    gsmall = {"norm_g": jnp.stack([both(dng0), both(dng1)]), "mla_q_norm": both(dqg), "mla_kv_norm": both(dkvg),
              "s5_a_re": g_a_re, "s5_a_im": g_a_im, "s5_log_step": g_ls, "s5_b_re": g_b_re, "s5_b_im": g_b_im,
              "s5_c_re": g_c_re, "s5_c_im": g_c_im, "final_g": dfg[1, 0]}
    return lvec[1], dxa, dmod, gbig, gsmall


COL_SHARDED = ("mla_w_in", "mla_w_uq", "mla_w_ukv", "s5_w_in")
ROW_SHARDED = ("mla_w_out", "s5_w_glu", "s5_w_out")
VEC_SHARDED = ("s5_d", "s5_b_glu")
BIG = COL_SHARDED + ROW_SHARDED
SHARDED = BIG + VEC_SHARDED
SMALL_RS = ("norm_g", "mla_q_norm", "mla_kv_norm", "s5_a_re", "s5_a_im", "s5_log_step", "s5_b_re", "s5_b_im",
            "s5_c_re", "s5_c_im", "final_g")
REPL = ("c_ctx", "ada_b") + SMALL_RS
ORDER = ("c_ctx", "ada_w", "ada_b", "norm_g", "mla_w_in", "mla_q_norm", "mla_w_uq", "mla_kv_norm", "mla_w_ukv",
         "mla_w_out", "s5_w_in", "s5_a_re", "s5_a_im", "s5_log_step", "s5_b_re", "s5_b_im", "s5_c_re", "s5_c_im",
         "s5_d", "s5_w_glu", "s5_b_glu", "s5_w_out", "final_g")


def kernel(x, c, ctx, c_ctx, ada_w, ada_b, norm_g, mla_w_in, mla_q_norm, mla_w_uq, mla_kv_norm, mla_w_ukv, mla_w_out, s5_w_in, s5_a_re, s5_a_im, s5_log_step, s5_b_re, s5_b_im, s5_c_re, s5_c_im, s5_d, s5_w_glu, s5_b_glu, s5_w_out, final_g, loss_target, m_c_ctx, m_ada_w, m_ada_b, m_norm_g, m_mla_w_in, m_mla_q_norm, m_mla_w_uq, m_mla_kv_norm, m_mla_w_ukv, m_mla_w_out, m_s5_w_in, m_s5_a_re, m_s5_a_im, m_s5_log_step, m_s5_b_re, m_s5_b_im, m_s5_c_re, m_s5_c_im, m_s5_d, m_s5_w_glu, m_s5_b_glu, m_s5_w_out, m_final_g, v_c_ctx, v_ada_w, v_ada_b, v_norm_g, v_mla_w_in, v_mla_q_norm, v_mla_w_uq, v_mla_kv_norm, v_mla_w_ukv, v_mla_w_out, v_s5_w_in, v_s5_a_re, v_s5_a_im, v_s5_log_step, v_s5_b_re, v_s5_b_im, v_s5_c_re, v_s5_c_im, v_s5_d, v_s5_w_glu, v_s5_b_glu, v_s5_w_out, v_final_g):
    w = dict(c_ctx=c_ctx, ada_w=ada_w, ada_b=ada_b, norm_g=norm_g, mla_w_in=mla_w_in, mla_q_norm=mla_q_norm,
             mla_w_uq=mla_w_uq, mla_kv_norm=mla_kv_norm, mla_w_ukv=mla_w_ukv, mla_w_out=mla_w_out, s5_w_in=s5_w_in,
             s5_a_re=s5_a_re, s5_a_im=s5_a_im, s5_log_step=s5_log_step, s5_b_re=s5_b_re, s5_b_im=s5_b_im,
             s5_c_re=s5_c_re, s5_c_im=s5_c_im, s5_d=s5_d, s5_w_glu=s5_w_glu, s5_b_glu=s5_b_glu, s5_w_out=s5_w_out,
             final_g=final_g)
    m = dict(c_ctx=m_c_ctx, ada_w=m_ada_w, ada_b=m_ada_b, norm_g=m_norm_g, mla_w_in=m_mla_w_in, mla_q_norm=m_mla_q_norm,
             mla_w_uq=m_mla_w_uq, mla_kv_norm=m_mla_kv_norm, mla_w_ukv=m_mla_w_ukv, mla_w_out=m_mla_w_out,
             s5_w_in=m_s5_w_in, s5_a_re=m_s5_a_re, s5_a_im=m_s5_a_im, s5_log_step=m_s5_log_step, s5_b_re=m_s5_b_re,
             s5_b_im=m_s5_b_im, s5_c_re=m_s5_c_re, s5_c_im=m_s5_c_im, s5_d=m_s5_d, s5_w_glu=m_s5_w_glu,
             s5_b_glu=m_s5_b_glu, s5_w_out=m_s5_w_out, final_g=m_final_g)
    v = dict(c_ctx=v_c_ctx, ada_w=v_ada_w, ada_b=v_ada_b, norm_g=v_norm_g, mla_w_in=v_mla_w_in, mla_q_norm=v_mla_q_norm,
             mla_w_uq=v_mla_w_uq, mla_kv_norm=v_mla_kv_norm, mla_w_ukv=v_mla_w_ukv, mla_w_out=v_mla_w_out,
             s5_w_in=v_s5_w_in, s5_a_re=v_s5_a_re, s5_a_im=v_s5_a_im, s5_log_step=v_s5_log_step, s5_b_re=v_s5_b_re,
             s5_b_im=v_s5_b_im, s5_c_re=v_s5_c_re, s5_c_im=v_s5_c_im, s5_d=v_s5_d, s5_w_glu=v_s5_w_glu,
             s5_b_glu=v_s5_b_glu, s5_w_out=v_s5_w_out, final_g=v_final_g)

    me = 4 * lax.axis_index("x") + 2 * lax.axis_index("y") + lax.axis_index("c")
    WA = ada_w.shape[2]

    cg = exchange([c], False, "gather_c")[0].reshape(NDEV, D)
    cc2 = c_ctx.reshape(1, D)
    ada_b_loc = lax.dynamic_slice_in_dim(ada_b.reshape(2, 3 * D // WA, WA), me, 1, axis=1)
    part = ada_fwd(cg, cc2, ada_w, ada_b_loc, "ada_fwd")
    pg = exchange([part.reshape(2 * 2 * NDEV, WA)], False, "gather_mod")[0].reshape(NDEV, 2, 2 * NDEV, WA)
    mod_l = lax.dynamic_index_in_dim(pg, me, axis=2, keepdims=False).transpose(1, 0, 2).reshape(2, 3 * D)
    mod_c = pg[:, :, NDEV, :].transpose(1, 0, 2).reshape(2, 3 * D)
    mod = jnp.stack([mod_c, mod_l], axis=1)

    vec_bits = lax.bitcast_convert_type(jnp.concatenate([s5_d, s5_b_glu], axis=0), BF16).reshape(2, -1)
    wsend = [_t_shard(w[n], SHARD_ROWS[n]) for n in COL_SHARDED] + [w[n][0].astype(BF16) for n in ROW_SHARDED] + [vec_bits]
    wgot = exchange(wsend, False, "gather_w")
    Wt = {n: a.reshape(-1, a.shape[-1]) for n, a in zip(BIG, wgot)}
    Wt["mla_w_in"] = mm(_win_order(), Wt["mla_w_in"], "nn", "w_in_order", out_dtype=BF16)
    vecs = lax.bitcast_convert_type(wgot[-1].reshape(NDEV, 2, -1, 2), F32)

    small = {n: w[n] for n in SMALL_RS}
    small["s5_d"] = vecs[:, 0, :].reshape(D)
    small["s5_b_glu"] = vecs[:, 1, :].reshape(D)

    xa = jnp.concatenate([ctx[0], x[0]], axis=0)
    lvec, dxa, dmod, gbig, gsmall = local_step(xa, loss_target[0], mod, Wt, small)
    loss = lax.psum(lvec[0, 0], ("x", "y", "c"))
    grad_x = dxa[LC:][None]

    small_flat = jnp.concatenate([gsmall[n].reshape(-1) for n in SMALL_RS])
    small_rows = -(-small_flat.shape[0] // (NDEV * LANES))
    small_flat = jnp.pad(small_flat, (0, NDEV * small_rows * LANES - small_flat.shape[0])).reshape(NDEV, small_rows * LANES)
    pieces = [_col_shards(gbig[n]) if n in COL_SHARDED else gbig[n].reshape(NDEV, -1) for n in SHARDED]
    sharded_len = sum(p.shape[1] for p in pieces)
    sharded_rows = -(-sharded_len // LANES)
    pad = jnp.zeros((NDEV, sharded_rows * LANES - sharded_len), F32)
    send = jnp.concatenate(pieces + [pad, small_flat], axis=1).reshape(NDEV, sharded_rows + small_rows, LANES)
    recv = exchange([send], True, "scatter_grads")[0]

    zero_rows = jnp.zeros((small_rows, LANES), F32)
    slab = lambda t: jnp.concatenate([_pack([t[n] for n in SHARDED], sharded_rows), zero_rows], axis=0)
    g_sh, d_sh, m_sh, v_sh = adamw(recv, slab(w), slab(m), slab(v), "adamw_sharded")
    shapes_sh = [w[n].shape for n in SHARDED]
    out = {}
    for key, arr in (("g", g_sh), ("d", d_sh), ("m", m_sh), ("v", v_sh)):
        for n, a in zip(SHARDED, _unpack(arr[:sharded_rows], shapes_sh)):
            out[key, n] = a

    dm_rows = (2 * 2 * 3 * D) // LANES
    mine = jnp.concatenate([g_sh[sharded_rows:], dmod.reshape(dm_rows, LANES)], axis=0)
    got = exchange([mine], False, "gather_small")[0]
    gsm = _unpack(got[:, :small_rows].reshape(-1), [w[n].shape for n in SMALL_RS])
    dm_all = got[:, small_rows:].reshape(NDEV, 2, 2, 3 * D)

    dm_cols = lax.dynamic_slice_in_dim(dm_all.reshape(NDEV, 2, 2, 3 * D // WA, WA), me, 1, axis=3)[:, :, :, 0]
    dm_loc = jnp.concatenate([dm_cols[:, :, 1].transpose(1, 0, 2), dm_cols[:, :, 0].transpose(1, 0, 2)], axis=1)
    g_ada_w, dcc_part, g_ada_b = ada_bwd(cg, cc2, ada_w, dm_loc, dm_all.transpose(0, 2, 1, 3).reshape(2 * NDEV, 2, 3 * D), "ada_bwd")
    dcc_all = exchange([dcc_part], False, "gather_dcc")[0].reshape(NDEV, D)
    g_c_ctx = cctx_finish(dcc_all, cc2, "cctx_finish").reshape(D)

    aw = lambda t: t.reshape(-1, LANES)
    g_a, d_a, m_a, v_a = adamw(aw(g_ada_w)[None], aw(ada_w), aw(m_ada_w), aw(v_ada_w), "adamw_ada")
    for key, arr in (("g", g_a), ("d", d_a), ("m", m_a), ("v", v_a)):
        out[key, "ada_w"] = arr.reshape(ada_w.shape)
    grep = {"c_ctx": g_c_ctx, "ada_b": g_ada_b}
    grep.update(dict(zip(SMALL_RS, gsm)))
    rslab = lambda t: _pack([t[n] for n in REPL])
    g_r, d_r, m_r, v_r = adamw(rslab(grep)[None], rslab(w), rslab(m), rslab(v), "adamw_repl")
    shapes_r = [w[n].shape for n in REPL]
    for key, arr in (("g", g_r), ("d", d_r), ("m", m_r), ("v", v_r)):
        for n, a in zip(REPL, _unpack(arr, shapes_r)):
            out[key, n] = a

    return (loss, grad_x, *[out["g", n] for n in ORDER], *[out["d", n] for n in ORDER],
            *[out["m", n] for n in ORDER], *[out["v", n] for n in ORDER])
```

```python
import math

import numpy as np
import jax
import jax.numpy as jnp
from jax import lax
from jax.experimental import pallas as pl
from jax.experimental.pallas import tpu as pltpu

F32 = jnp.float32
BF16 = jnp.bfloat16

D = 1024
L = 2048
LC = 256
NDEV = 8
GRID_W = 64
EPS = 1e-6
HEADS = 16
NOPE = 64
ROPE = 32
QK = NOPE + ROPE
VD = 64
IN_W = 256 + 128 + ROPE + HEADS * 64
IN_WP = 1536
QL = 256
KVL = 128
SCALE = QK ** -0.5
THETA = 10000.0
G = 64
P = 64
CH = 16
GB = 8
NJ = G // GB
UB = GB * CH
SB = GB * P
SEG = 8
TB = 256
LANES = 1024
VMEM_LIMIT = 56 * 1024 * 1024
B1, B2, LR, AEPS, WD, STEP = 0.9, 0.999, 0.001, 1e-8, 0.01, 10
MESH_T = pl.DeviceIdType.MESH


def _cp(sem=None):
    return pltpu.CompilerParams(dimension_semantics=sem, vmem_limit_bytes=VMEM_LIMIT)


def _sig(x):
    return 1.0 / (1.0 + jnp.exp(-x))


def _silu(x):
    return x * _sig(x)


def _dsilu(x):
    s = _sig(x)
    return s * (1.0 + x * (1.0 - s))


_GK = math.sqrt(2.0 / math.pi)


def _gelu(x):
    return 0.5 * x * (1.0 + jnp.tanh(_GK * (x + 0.044715 * x * x * x)))


def _dgelu(x):
    t = jnp.tanh(_GK * (x + 0.044715 * x * x * x))
    return 0.5 * (1.0 + t) + 0.5 * x * (1.0 - t * t) * _GK * (1.0 + 3 * 0.044715 * x * x)


def _rs(x):
    return lax.rsqrt(jnp.mean(x * x, axis=-1, keepdims=True) + EPS)


def _sum0(x):
    return jnp.sum(x, axis=0, keepdims=True)


def st_norm_mod(x, g, sc, sh):
    y = x * _rs(x) * g
    return (y * (1.0 + sc) + sh,), ()


def st_norm_mod_bwd(x, dh, dres, g, sc):
    r = _rs(x)
    xn = x * r
    y = xn * g
    dy = dh * (1.0 + sc)
    dxn = dy * g
    dx = r * (dxn - xn * jnp.mean(dxn * xn, axis=-1, keepdims=True))
    return (dres + dx,), (_sum0(dh), _sum0(dh * y), _sum0(dy * xn))


def st_rms(x, g):
    return (x * _rs(x) * g,), ()


def st_rms_bwd(x, dy, g):
    r = _rs(x)
    n = x * r
    dn = dy * g
    dx = r * (dn - n * jnp.mean(dn * n, axis=-1, keepdims=True))
    return (dx,), (_sum0(dy * n),)


def st_gate(o, z):
    return (o * _silu(z),), ()


def st_gate_bwd(dog, o, z):
    return (dog * _silu(z), dog * o * _dsilu(z)), ()


def st_resid(x, out, gt):
    return (x + gt * out,), ()


def st_resid_bwd(dx, out, gt):
    return (dx * gt,), (_sum0(dx * out),)


def st_s5a(yssm, u, d):
    y = yssm + d * u
    return (y, _gelu(y)), ()


def st_s5b(y, gl, z, b):
    return (_gelu(y) * _sig(gl + b) * _silu(z),), ()


def st_s5b_bwd(dy3, y, gl, z, b):
    y1 = _gelu(y)
    s = _sig(gl + b)
    dy2 = dy3 * _silu(z)
    dz = dy3 * y1 * s * _dsilu(z)
    dgl = dy2 * y1 * s * (1.0 - s)
    return (dgl, dz, dy2 * s), (_sum0(dgl),)


def st_s5a_bwd(dy1a, dy1b, y, u, d):
    dy = (dy1a + dy1b) * _dgelu(y)
    return (dy, dy * d), (_sum0(dy * u),)


def st_final(x2, tgt, g):
    r = _rs(x2)
    n = x2 * r
    e = n * g - tgt
    dyo = e * (1.0 / D)
    dn = dyo * g
    dx = r * (dn - n * jnp.mean(dn * n, axis=-1, keepdims=True))
    lsum = jnp.sum(_sum0(e * e), axis=1, keepdims=True) * (0.5 / D)
    return (dx,), (_sum0(dyo * n), jnp.broadcast_to(lsum, (1, 128)))


def rowwise(fn, rows, vecs, out_rows, out_sums, name, lat_only=False):
    rows = [a if isinstance(a, tuple) else (a, 0, a.shape[1]) for a in rows]
    nrows = L if lat_only else rows[0][0].shape[0]
    nb = nrows // TB
    nr, nv, no = len(rows), len(vecs), len(out_rows)

    def body(*refs):
        i = pl.program_id(0)
        vals = [r[...] for r in refs[:nr]] + [r[0] for r in refs[nr:nr + nv]]
        outs, sums = fn(*vals)
        for r, o in zip(refs[nr + nv:nr + nv + no], outs):
            r[...] = o.astype(r.dtype)
        sum_refs = refs[nr + nv + no:]
        if sum_refs:
            @pl.when((i == 0) if lat_only else (i <= 1))
            def _():
                for r in sum_refs:
                    r[...] = jnp.zeros_like(r)
            for r, s in zip(sum_refs, sums):
                r[0] += s

    def row_spec(a):
        arr, cb, width = a
        off = (arr.shape[0] - nrows) // TB
        return pl.BlockSpec((TB, width), lambda i: (i + off, cb))

    if lat_only:
        kind = lambda i: (1, 0, 0)
    else:
        kind = lambda i: (jnp.minimum(i, 1), 0, 0)
    in_specs = [row_spec(a) for a in rows] + [pl.BlockSpec((1, 1, v.shape[2]), kind) for v in vecs]
    out_specs = [pl.BlockSpec((TB, c), lambda i: (i, 0)) for c, _ in out_rows] + \
                [pl.BlockSpec((1, 1, c), kind) for c in out_sums]
    out_shape = [jax.ShapeDtypeStruct((nrows, c), dt) for c, dt in out_rows] + \
                [jax.ShapeDtypeStruct((2, 1, c), F32) for c in out_sums]
    res = pl.pallas_call(body, grid=(nb,), in_specs=in_specs, out_specs=out_specs, out_shape=out_shape,
                         compiler_params=_cp(("arbitrary",)), name=name)(*[a[0] for a in rows], *vecs)
    return res[:no], res[no:]


_DN = {"nn": (((1,), (0,)), ((), ())), "nt": (((1,), (1,)), ((), ())), "tn": (((0,), (0,)), ((), ()))}


def mm(a, b, mode, name, out_dtype=F32, tm=256, tn=None):
    if mode == "nn":
        (M, K), (_, N) = a.shape, b.shape
    elif mode == "nt":
        (M, K), (N, _) = a.shape, b.shape
    else:
        (K, M), (_, N) = a.shape, b.shape
    tm = min(tm, M)
    tn = N if tn is None else tn
    dn = _DN[mode]

    def body(a_ref, b_ref, o_ref):
        o_ref[...] = lax.dot_general(a_ref[...].astype(BF16), b_ref[...].astype(BF16), dn,
                                     preferred_element_type=F32).astype(o_ref.dtype)

    a_spec = pl.BlockSpec((K, tm), lambda i, j: (0, i)) if mode == "tn" else pl.BlockSpec((tm, K), lambda i, j: (i, 0))
    b_spec = pl.BlockSpec((tn, K), lambda i, j: (j, 0)) if mode == "nt" else pl.BlockSpec((K, tn), lambda i, j: (0, j))
    return pl.pallas_call(body, grid=(M // tm, N // tn), in_specs=[a_spec, b_spec],
                          out_specs=pl.BlockSpec((tm, tn), lambda i, j: (i, j)),
                          out_shape=jax.ShapeDtypeStruct((M, N), out_dtype),
                          compiler_params=_cp(("parallel", "arbitrary")), name=name)(a, b)


def _rope_tables(T):
    nlat = T - LC
    pos = np.arange(nlat)
    row, col = pos // GRID_W, pos % GRID_W
    half = ROPE // 2
    inv = 1.0 / (THETA ** (np.arange(0, half, 2, dtype=np.float64) / half))
    cosf = np.ones((T, QK), np.float64)
    sinf = np.zeros((T, QK), np.float64)
    perm = np.zeros((QK, QK), np.float32)
    for m in range(ROPE):
        j = NOPE + m
        blk, w = m // half, m % half
        ang = (row if blk == 0 else col)[:, None] * inv[None, :]
        f = w % (half // 2)
        cosf[LC:, j] = np.cos(ang[:, f])
        if w < half // 2:
            sinf[LC:, j] = -np.sin(ang[:, f])
            perm[j + half // 2, j] = 1.0
        else:
            sinf[LC:, j] = np.sin(ang[:, f])
            perm[j - half // 2, j] = 1.0
    return jnp.asarray(cosf, F32), jnp.asarray(sinf, F32), jnp.asarray(perm, BF16), jnp.asarray(perm.T, BF16)


def _exact_perm(x, pm):
    hi = x.astype(BF16)
    r1 = x - hi.astype(F32)
    mid = r1.astype(BF16)
    lo = (r1 - mid.astype(F32)).astype(BF16)
    dot = lambda a: jnp.dot(a, pm, preferred_element_type=F32)
    return dot(hi) + dot(mid) + dot(lo)


def rope(x, cosf, sinf, pm, inverse, out_dtype, name, head_sum=False):
    H, T, _ = x.shape

    def body(x_ref, c_ref, s_ref, p_ref, o_ref, *rest):
        cv, sv, pv = c_ref[...], s_ref[...], p_ref[...]
        total = None
        for h in range(H):
            xv = x_ref[h]
            if inverse:
                out = xv * cv + _exact_perm(xv * sv, pv)
            else:
                out = xv * cv + _exact_perm(xv, pv) * sv
            o_ref[h] = out.astype(o_ref.dtype)
            if head_sum:
                total = out if total is None else total + out
        if head_sum:
            rest[0][...] = total

    out_shape = [jax.ShapeDtypeStruct((H, T, QK), out_dtype)]
    out_specs = [pl.BlockSpec((H, TB, QK), lambda i: (0, i, 0))]
    if head_sum:
        out_shape.append(jax.ShapeDtypeStruct((T, QK), F32))
        out_specs.append(pl.BlockSpec((TB, QK), lambda i: (i, 0)))
    res = pl.pallas_call(
        body, grid=(T // TB,),
        in_specs=[pl.BlockSpec((H, TB, QK), lambda i: (0, i, 0)), pl.BlockSpec((TB, QK), lambda i: (i, 0)),
                  pl.BlockSpec((TB, QK), lambda i: (i, 0)), pl.BlockSpec((QK, QK), lambda i: (0, 0))],
        out_specs=out_specs, out_shape=out_shape, compiler_params=_cp(("parallel",)), name=name)(x, cosf, sinf, pm)
    return res if head_sum else res[0]


def _scores(q, k, qi):
    s = lax.dot_general(q, k, _DN["nt"], preferred_element_type=F32) * SCALE
    col = lax.broadcasted_iota(jnp.int32, s.shape, 1)
    return jnp.where(jnp.logical_and(qi == 0, col >= LC), -1e30, s)


def attn_fwd(q, k, v, name):
    H, T, _ = q.shape

    def body(q_ref, k_ref, v_ref, o_ref, lse_ref):
        s = _scores(q_ref[0], k_ref[0], pl.program_id(1))
        m = jnp.max(s, axis=1, keepdims=True)
        p = jnp.exp(s - m)
        l = jnp.sum(p, axis=1, keepdims=True)
        o = jnp.dot(p.astype(BF16), v_ref[0], preferred_element_type=F32)
        o_ref[0] = o / l
        lse_ref[0] = m + jnp.log(l)

    return pl.pallas_call(
        body, grid=(H, T // TB),
        in_specs=[pl.BlockSpec((1, TB, QK), lambda h, i: (h, i, 0)), pl.BlockSpec((1, T, QK), lambda h, i: (h, 0, 0)),
                  pl.BlockSpec((1, T, VD), lambda h, i: (h, 0, 0))],
        out_specs=[pl.BlockSpec((1, TB, VD), lambda h, i: (h, i, 0)), pl.BlockSpec((1, TB, 1), lambda h, i: (h, i, 0))],
        out_shape=[jax.ShapeDtypeStruct((H, T, VD), F32), jax.ShapeDtypeStruct((H, T, 1), F32)],
        compiler_params=_cp(("parallel", "arbitrary")), name=name)(q, k, v)


def attn_bwd(q, k, v, o, lse, do, name):
    H, T, _ = q.shape

    def body(q_ref, k_ref, v_ref, o_ref, lse_ref, do_ref, dq_ref, dk_ref, dv_ref):
        i = pl.program_id(1)

        @pl.when(i == 0)
        def _():
            dk_ref[...] = jnp.zeros_like(dk_ref)
            dv_ref[...] = jnp.zeros_like(dv_ref)

        qv, kv, dov = q_ref[0], k_ref[0], do_ref[0]
        p = jnp.exp(_scores(qv, kv, i) - lse_ref[0])
        delta = jnp.sum(dov * o_ref[0], axis=1, keepdims=True)
        dob = dov.astype(BF16)
        dv_ref[0] += lax.dot_general(p.astype(BF16), dob, _DN["tn"], preferred_element_type=F32)
        dp = lax.dot_general(dob, v_ref[0], _DN["nt"], preferred_element_type=F32)
        ds = (p * (dp - delta) * SCALE).astype(BF16)
        dq_ref[0] = jnp.dot(ds, kv, preferred_element_type=F32)
        dk_ref[0] += lax.dot_general(ds, qv, _DN["tn"], preferred_element_type=F32)

    blk = lambda c: pl.BlockSpec((1, TB, c), lambda h, i: (h, i, 0))
    full = lambda c: pl.BlockSpec((1, T, c), lambda h, i: (h, 0, 0))
    return pl.pallas_call(
        body, grid=(H, T // TB),
        in_specs=[blk(QK), full(QK), full(VD), blk(VD), blk(1), blk(VD)],
        out_specs=[blk(QK), full(QK), full(VD)],
        out_shape=[jax.ShapeDtypeStruct((H, T, QK), F32), jax.ShapeDtypeStruct((H, T, QK), F32),
                   jax.ShapeDtypeStruct((H, T, VD), F32)],
        compiler_params=_cp(("parallel", "arbitrary")), name=name)(q, k, v, o, lse, do)


def disc_fwd(a_re, a_im, ls, name):
    def body(ar_ref, ai_ref, ls_ref, lr_ref, li_ref, fr_ref, fi_ref):
        ar, ai = ar_ref[...], ai_ref[...]
        dt = jnp.exp(ls_ref[...])
        mag = jnp.exp(ar * dt)
        lr = mag * jnp.cos(ai * dt)
        li = mag * jnp.sin(ai * dt)
        den = ar * ar + ai * ai
        nr = lr - 1.0
        lr_ref[...] = lr
        li_ref[...] = li
        fr_ref[...] = (nr * ar + li * ai) / den
        fi_ref[...] = (li * ar - nr * ai) / den

    return pl.pallas_call(body, out_shape=[jax.ShapeDtypeStruct(a_re.shape, F32)] * 4, name=name)(a_re, a_im, ls)


def disc_b(f_re, f_im, b_re, b_im, name):
    def body(fr_ref, fi_ref, br_ref, bi_ref, or_ref, oi_ref):
        fr, fi, br, bi = fr_ref[...], fi_ref[...], br_ref[...], bi_ref[...]
        or_ref[...] = fr * br - fi * bi
        oi_ref[...] = fr * bi + fi * br

    fs, bs = _disc_b_specs()
    return pl.pallas_call(body, grid=(2, G * P // DISC_ROWS), in_specs=[fs, fs, bs, bs], out_specs=[bs, bs],
                          out_shape=[jax.ShapeDtypeStruct(b_re.shape, F32)] * 2, name=name)(f_re, f_im, b_re, b_im)


DISC_ROWS = 1024


def _disc_b_specs():
    return (pl.BlockSpec((1, DISC_ROWS, 1), lambda d, i: (d, i, 0)), pl.BlockSpec((1, DISC_ROWS, CH), lambda d, i: (d, i, 0)))


def disc_b_bwd(f_re, f_im, b_re, b_im, dbb_re, dbb_im, name):
    def body(fr_ref, fi_ref, br_ref, bi_ref, dr_ref, di_ref, dbr_ref, dbi_ref, dfr_ref, dfi_ref):
        fr, fi, br, bi, dr, di = fr_ref[...], fi_ref[...], br_ref[...], bi_ref[...], dr_ref[...], di_ref[...]
        dbr_ref[...] = fr * dr + fi * di
        dbi_ref[...] = fr * di - fi * dr
        dfr_ref[...] = jnp.sum(dr * br + di * bi, axis=-1, keepdims=True)
        dfi_ref[...] = jnp.sum(di * br - dr * bi, axis=-1, keepdims=True)

    fs, bs = _disc_b_specs()
    return pl.pallas_call(body, grid=(2, G * P // DISC_ROWS), in_specs=[fs, fs, bs, bs, bs, bs], out_specs=[bs, bs, fs, fs],
                          out_shape=[jax.ShapeDtypeStruct(b_re.shape, F32)] * 2 + [jax.ShapeDtypeStruct(f_re.shape, F32)] * 2,
                          name=name)(f_re, f_im, b_re, b_im, dbb_re, dbb_im)


def disc_a_bwd(a_re, a_im, ls, dlr, dli, dfr, dfi, name):
    def body(ar_ref, ai_ref, ls_ref, dlr_ref, dli_ref, dfr_ref, dfi_ref, dar_ref, dai_ref, dls_ref):
        ar, ai = ar_ref[...], ai_ref[...]
        dt = jnp.exp(ls_ref[...])
        mag = jnp.exp(ar * dt)
        cs, sn = jnp.cos(ai * dt), jnp.sin(ai * dt)
        lr, li = mag * cs, mag * sn
        den = ar * ar + ai * ai
        nr = lr - 1.0
        f_re = (nr * ar + li * ai) / den
        f_im = (li * ar - nr * ai) / den
        dn1 = dfr_ref[...] / den
        dn2 = dfi_ref[...] / den
        dden = -(dfr_ref[...] * f_re + dfi_ref[...] * f_im) / den
        dlr_t = dlr_ref[...] + dn1 * ar - dn2 * ai
        dli_t = dli_ref[...] + dn1 * ai + dn2 * ar
        dar = dn1 * nr + dn2 * li + dden * 2.0 * ar
        dai = dn1 * li - dn2 * nr + dden * 2.0 * ai
        dmag = dlr_t * cs + dli_t * sn
        dth = dli_t * lr - dlr_t * li
        dar_ref[...] = dar + dmag * mag * dt
        dai_ref[...] = dai + dth * dt
        dls_ref[...] = jnp.sum(dmag * mag * ar + dth * ai, axis=-1, keepdims=True) * dt

    return pl.pallas_call(body, out_shape=[jax.ShapeDtypeStruct(a_re.shape, F32)] * 2 +
                          [jax.ShapeDtypeStruct(ls.shape, F32)], name=name)(a_re, a_im, ls, dlr, dli, dfr, dfi)


def _cpow(lr, li, n):
    rr, ri = None, None
    br, bi = lr, li
    while n:
        if n & 1:
            if rr is None:
                rr, ri = br, bi
            else:
                rr, ri = rr * br - ri * bi, rr * bi + ri * br
        n >>= 1
        if n:
            br, bi = br * br - bi * bi, 2.0 * br * bi
    return rr, ri


UNROLL = 4


def _seg_scan(xre, xim, lam8, pw, base, seglen, rev, init, fin_re, fin_im, ini_re, ini_im, prev=None):
    lr, li = lam8

    def rows(t):
        return pl.ds(pl.multiple_of(base + t * SEG, SEG), SEG)

    tmap = (lambda n: seglen - 1 - n) if rev else (lambda n: n)
    zero = jnp.zeros((SEG, SB), F32)

    def advance(c, t):
        a, b = c
        return lr * a - li * b + xre[rows(t), :], lr * b + li * a + xim[rows(t), :]

    fin = lax.fori_loop(0, seglen, lambda n, c: advance(c, tmap(n)), (zero, zero), unroll=UNROLL)
    fin_re[...] = fin[0]
    fin_im[...] = fin[1]
    (cr, ci), (pr, pi) = init, pw
    for i in (range(SEG - 1, -1, -1) if rev else range(SEG)):
        ini_re[pl.ds(i, 1), :] = cr
        ini_im[pl.ds(i, 1), :] = ci
        cr, ci = pr * cr - pi * ci + fin_re[pl.ds(i, 1), :], pr * ci + pi * cr + fin_im[pl.ds(i, 1), :]
    start = (ini_re[...], ini_im[...])

    def store(c, t):
        na, nb = advance(c, t)
        xre[rows(t), :] = na
        xim[rows(t), :] = nb
        return na, nb

    if prev is None:
        lax.fori_loop(0, seglen, lambda n, c: store(c, tmap(n)), start, unroll=UNROLL)
        return (cr, ci), None

    sre, sim, s_ini_re, s_ini_im = prev

    def acc_step(c, t, pre, pim):
        na, nb = store(c[:2], t)
        return na, nb, c[2] + na * pre + nb * pim, c[3] + nb * pre - na * pim

    def body(n, c):
        t = tmap(n)
        tp = t - 1 if rev else t + 1
        return acc_step(c, t, sre[rows(tp), :], sim[rows(tp), :])

    c = lax.fori_loop(0, seglen - 1, body, start + (zero, zero), unroll=UNROLL)
    c = acc_step(c, 0 if rev else seglen - 1, s_ini_re[...], s_ini_im[...])
    return (cr, ci), c[2:]


def _lam_tiles(lr, li, lens, conj=False):
    if conj:
        li = -li
    lam8 = (jnp.broadcast_to(lr, (SEG, SB)), jnp.broadcast_to(li, (SEG, SB)))
    return lam8, [_cpow(lr, li, n) for n in lens]


def _stretches(T):
    return ((0, LC // SEG), (LC, (T - LC) // SEG))


def _to_seg_order(src, dst, T):
    for base, seglen in _stretches(T):
        def body(t, carry, base=base, seglen=seglen):
            dst[pl.ds(pl.multiple_of(base + t * SEG, SEG), SEG), :] = src[pl.ds(base + t, SEG, stride=seglen), :]
            return carry
        lax.fori_loop(0, seglen, body, 0, unroll=8)


def _from_seg_order(src, dst, T):
    for base, seglen in _stretches(T):
        def body(t, carry, base=base, seglen=seglen):
            dst[pl.ds(base + t, SEG, stride=seglen), :] = src[pl.ds(pl.multiple_of(base + t * SEG, SEG), SEG), :]
            return carry
        lax.fori_loop(0, seglen, body, 0, unroll=8)


def _scan_specs(T):
    ublk = pl.BlockSpec((T, UB), lambda j: (0, j))
    lam = pl.BlockSpec((2, 1, 1, SB), lambda j: (0, j, 0, 0))
    mat = pl.BlockSpec((2, 1, UB, SB), lambda j: (0, j, 0, 0))
    return ublk, lam, mat


def _dotf(a, b, mode="nn"):
    return lax.dot_general(a, b, _DN[mode], preferred_element_type=F32)


def _zero_state():
    return jnp.zeros((1, SB), F32), jnp.zeros((1, SB), F32)


def scan_fwd(u, lam_re, lam_im, bre, bim, cre, cim, name):
    T = u.shape[0]
    s_ctx, s_lat = LC // SEG, (T - LC) // SEG

    def body(u_ref, lr_ref, li_ref, bre_ref, bim_ref, cre_ref, cim_ref, y_ref, us, ys, sre, sim, fre, fim, ire, iim):
        _to_seg_order(u_ref, us, T)
        ub = us[...].astype(BF16)
        for d in range(2):
            lam8, (pw_c, pw_l) = _lam_tiles(lr_ref[d, 0], li_ref[d, 0], (s_ctx, s_lat))
            sre[...] = _dotf(ub, bre_ref[d, 0].astype(BF16))
            sim[...] = _dotf(ub, bim_ref[d, 0].astype(BF16))
            end_c, _ = _seg_scan(sre, sim, lam8, pw_c, 0, s_ctx, bool(d), _zero_state(), fre, fim, ire, iim)
            _seg_scan(sre, sim, lam8, pw_l, LC, s_lat, bool(d), end_c, fre, fim, ire, iim)
            y = (_dotf(sre[...].astype(BF16), cre_ref[d, 0].astype(BF16), "nt")
                 - _dotf(sim[...].astype(BF16), cim_ref[d, 0].astype(BF16), "nt"))
            if d == 0:
                ys[...] = y
            else:
                ys[...] += y
        _from_seg_order(ys, y_ref, T)

    ublk, lam, mat = _scan_specs(T)
    return pl.pallas_call(
        body, grid=(NJ,), in_specs=[ublk, lam, lam, mat, mat, mat, mat], out_specs=ublk,
        out_shape=jax.ShapeDtypeStruct((T, G * CH), F32),
        scratch_shapes=[pltpu.VMEM((T, UB), F32)] * 2 + [pltpu.VMEM((T, SB), F32)] * 2 + [pltpu.VMEM((SEG, SB), F32)] * 4,
        compiler_params=_cp(("arbitrary",)), name=name)(u, lam_re, lam_im, bre, bim, cre, cim)


def scan_bwd(u, dy, lam_re, lam_im, bre, bim, cre, cim, name):
    T = u.shape[0]
    s_ctx, s_lat = LC // SEG, (T - LC) // SEG

    def body(u_ref, dy_ref, lr_ref, li_ref, bre_ref, bim_ref, cre_ref, cim_ref,
             du_ref, dlr_ref, dli_ref, dbre_ref, dbim_ref, dcre_ref, dcim_ref,
             us, dys, dus, sre, sim, gre, gim, fre, fim, ic_re, ic_im, il_re, il_im, jre, jim):
        _to_seg_order(u_ref, us, T)
        _to_seg_order(dy_ref, dys, T)
        ub, dyb = us[...].astype(BF16), dys[...].astype(BF16)
        for d in range(2):
            rev = bool(d)
            lam8, (pw_c, pw_l) = _lam_tiles(lr_ref[d, 0], li_ref[d, 0], (s_ctx, s_lat))
            cam8, (cw_c, cw_l) = _lam_tiles(lr_ref[d, 0], li_ref[d, 0], (s_ctx, s_lat), conj=True)
            bre_v, bim_v = bre_ref[d, 0].astype(BF16), bim_ref[d, 0].astype(BF16)
            sre[...] = _dotf(ub, bre_v)
            sim[...] = _dotf(ub, bim_v)
            end_c, _ = _seg_scan(sre, sim, lam8, pw_c, 0, s_ctx, rev, _zero_state(), fre, fim, ic_re, ic_im)
            _seg_scan(sre, sim, lam8, pw_l, LC, s_lat, rev, end_c, fre, fim, il_re, il_im)
            gre[...] = _dotf(dyb, cre_ref[d, 0].astype(BF16))
            gim[...] = -_dotf(dyb, cim_ref[d, 0].astype(BF16))
            end_g, acc_l = _seg_scan(gre, gim, cam8, cw_l, LC, s_lat, not rev, _zero_state(), fre, fim, jre, jim,
                                     prev=(sre, sim, il_re, il_im))
            _, acc_c = _seg_scan(gre, gim, cam8, cw_c, 0, s_ctx, not rev, end_g, fre, fim, jre, jim,
                                 prev=(sre, sim, ic_re, ic_im))
            dlr_ref[d, 0] = _sum0(acc_l[0] + acc_c[0])
            dli_ref[d, 0] = _sum0(acc_l[1] + acc_c[1])
            grb, gib = gre[...].astype(BF16), gim[...].astype(BF16)
            du = _dotf(grb, bre_v, "nt") + _dotf(gib, bim_v, "nt")
            if d == 0:
                dus[...] = du
            else:
                dus[...] += du
            dbre_ref[d, 0] = _dotf(ub, grb, "tn")
            dbim_ref[d, 0] = _dotf(ub, gib, "tn")
            dcre_ref[d, 0] = _dotf(dyb, sre[...].astype(BF16), "tn")
            dcim_ref[d, 0] = -_dotf(dyb, sim[...].astype(BF16), "tn")
        _from_seg_order(dus, du_ref, T)

    ublk, lam, mat = _scan_specs(T)
    lam_s = jax.ShapeDtypeStruct(lam_re.shape, F32)
    mat_s = jax.ShapeDtypeStruct(bre.shape, F32)
    return pl.pallas_call(
        body, grid=(NJ,), in_specs=[ublk, ublk, lam, lam, mat, mat, mat, mat],
        out_specs=[ublk, lam, lam, mat, mat, mat, mat],
        out_shape=[jax.ShapeDtypeStruct((T, G * CH), F32), lam_s, lam_s, mat_s, mat_s, mat_s, mat_s],
        scratch_shapes=[pltpu.VMEM((T, UB), F32)] * 3 + [pltpu.VMEM((T, SB), F32)] * 4 + [pltpu.VMEM((SEG, SB), F32)] * 8,
        compiler_params=_cp(("arbitrary",)), name=name)(u, dy, lam_re, lam_im, bre, bim, cre, cim)


def _block_diag(m):
    m5 = m.reshape(2, NJ, GB, CH, P)
    eye = jnp.eye(GB, dtype=m.dtype)
    return (m5[:, :, :, :, None, :] * eye[None, None, :, None, :, None]).reshape(2, NJ, UB, SB)


def _diag_blocks(m):
    m6 = m.reshape(2, NJ, GB, CH, GB, P)
    idx = jnp.arange(GB)
    return m6[:, :, idx, :, idx, :].transpose(1, 2, 0, 3, 4).reshape(2, G, CH, P)


def exchange(xs, scatter, name):
    n = len(xs)

    def body(*refs):
        x_refs, out_refs = refs[:n], refs[n:2 * n]
        send_sems, recv_sems, local_sems = refs[2 * n:]
        mx, my, mc = lax.axis_index("x"), lax.axis_index("y"), lax.axis_index("c")
        me = 4 * mx + 2 * my + mc
        copies = []
        for a, (x_ref, out_ref) in enumerate(zip(x_refs, out_refs)):
            local = pltpu.make_async_copy(x_ref.at[me] if scatter else x_ref, out_ref.at[me], local_sems.at[a])
            local.start()
            copies.append(local)
        sends, recvs = [], []
        for k in range(1, NDEV):
            peer = (1 - mx if k & 4 else mx, 1 - my if k & 2 else my, 1 - mc if k & 1 else mc)
            pid = 4 * peer[0] + 2 * peer[1] + peer[2]
            for a, (x_ref, out_ref) in enumerate(zip(x_refs, out_refs)):
                src = x_ref.at[pid] if scatter else x_ref
                cp = pltpu.make_async_remote_copy(
                    src_ref=src, dst_ref=out_ref.at[me], send_sem=send_sems.at[k - 1, a], recv_sem=recv_sems.at[k - 1, a],
                    device_id=peer, device_id_type=MESH_T)
                cp.start()
                sends.append(cp)
                recvs.append(pltpu.make_async_remote_copy(
                    src_ref=src, dst_ref=out_ref.at[pid], send_sem=send_sems.at[k - 1, a], recv_sem=recv_sems.at[k - 1, a],
                    device_id=peer, device_id_type=MESH_T))
        for cp in recvs:
            cp.wait_recv()
        for cp in sends:
            cp.wait_send()
        for cp in copies:
            cp.wait()

    hbm = pl.BlockSpec(memory_space=pl.ANY)
    return pl.pallas_call(
        body, in_specs=[hbm] * n, out_specs=[hbm] * n,
        out_shape=[jax.ShapeDtypeStruct((NDEV,) + tuple(x.shape[1:] if scatter else x.shape), x.dtype) for x in xs],
        scratch_shapes=[pltpu.SemaphoreType.DMA((NDEV - 1, n)), pltpu.SemaphoreType.DMA((NDEV - 1, n)),
                        pltpu.SemaphoreType.DMA((n,))],
        compiler_params=pltpu.CompilerParams(has_side_effects=True), name=name)(*xs)


def _dot_f32(a, b, dn):
    return lax.dot_general(a, b, dn, preferred_element_type=F32, precision=lax.Precision.HIGHEST)


def ada_fwd(cg, c_ctx, ada_w, ada_b_loc, name):
    W = ada_w.shape[2]

    def body(cg_ref, cc_ref, w_ref, b_ref, o_ref):
        a = jnp.concatenate([_silu(cg_ref[...]), jnp.broadcast_to(_silu(cc_ref[...]), (NDEV, D))], axis=0)
        for i in range(2):
            o_ref[i] = _dot_f32(a, w_ref[i], _DN["nn"]) + b_ref[i]

    return pl.pallas_call(body, out_shape=jax.ShapeDtypeStruct((2, 2 * NDEV, W), F32),
                          compiler_params=_cp(), name=name)(cg, c_ctx, ada_w, ada_b_loc)


def ada_bwd(cg, c_ctx, ada_w, dm_loc, dm_all, name):
    W = ada_w.shape[2]

    def body(cg_ref, cc_ref, w_ref, dl_ref, da_ref, gw_ref, dcc_ref, gb_ref):
        a = jnp.concatenate([_silu(cg_ref[...]), jnp.broadcast_to(_silu(cc_ref[...]), (NDEV, D))], axis=0)
        dcc = jnp.zeros((1, D), F32)
        for i in range(2):
            dl = dl_ref[i]
            gw_ref[i] = _dot_f32(a, dl, _DN["tn"])
            dctx = jnp.sum(dl[NDEV:], axis=0, keepdims=True)
            dcc = dcc + _dot_f32(dctx, w_ref[i], _DN["nt"])
        dcc_ref[...] = dcc
        gb_ref[...] = jnp.sum(da_ref[...], axis=0)

    return pl.pallas_call(body, out_shape=[jax.ShapeDtypeStruct((2, D, W), F32), jax.ShapeDtypeStruct((1, D), F32),
                                           jax.ShapeDtypeStruct((2, 3 * D), F32)],
                          compiler_params=_cp(), name=name)(cg, c_ctx, ada_w, dm_loc, dm_all)


def cctx_finish(parts, c_ctx, name):
    def body(p_ref, cc_ref, o_ref):
        o_ref[...] = jnp.sum(p_ref[...], axis=0, keepdims=True) * _dsilu(cc_ref[...])

    return pl.pallas_call(body, out_shape=jax.ShapeDtypeStruct((1, D), F32), name=name)(parts, c_ctx)


def adamw(gstack, w, m, v, name, tr=512):
    n, R, C = gstack.shape
    tr = max(t for t in range(8, min(tr, R) + 1, 8) if R % t == 0)
    nb = R // tr
    c1 = 1.0 / (1.0 - B1 ** STEP)
    c2 = 1.0 / (1.0 - B2 ** STEP)

    def body(g_ref, w_ref, m_ref, v_ref, go_ref, d_ref, mo_ref, vo_ref):
        g = g_ref[0]
        for s in range(1, n):
            g = g + g_ref[s]
        mn = B1 * m_ref[...] + (1.0 - B1) * g
        vn = B2 * v_ref[...] + (1.0 - B2) * g * g
        go_ref[...] = g
        mo_ref[...] = mn
        vo_ref[...] = vn
        d_ref[...] = -LR * ((mn * c1) / (jnp.sqrt(vn * c2) + AEPS) + WD * w_ref[...])

    spec = pl.BlockSpec((tr, C), lambda i: (i, 0))
    return pl.pallas_call(body, grid=(nb,), in_specs=[pl.BlockSpec((n, tr, C), lambda i: (0, i, 0)), spec, spec, spec],
                          out_specs=[spec] * 4, out_shape=[jax.ShapeDtypeStruct((R, C), F32)] * 4,
                          compiler_params=_cp(("parallel",)), name=name)(gstack, w, m, v)


def _pack(parts, rows=None):
    flat = jnp.concatenate([p.reshape(-1) for p in parts])
    rows = 8 * (-(-flat.shape[0] // (8 * LANES))) if rows is None else rows
    return jnp.pad(flat, (0, rows * LANES - flat.shape[0])).reshape(rows, LANES)


def _unpack(slab, shapes):
    flat = slab.reshape(-1)
    out, off = [], 0
    for s in shapes:
        n = int(np.prod(s))
        out.append(flat[off:off + n].reshape(s))
        off += n
    return out


def _col_shards(g):
    R, N = g.shape
    return g.reshape(R, NDEV, N // NDEV).transpose(1, 0, 2).reshape(NDEV, -1)


def _from_col_shards(a):
    n, R, w = a.shape
    return a.transpose(1, 0, 2).reshape(R, n * w)


def _vec2(v):
    return jnp.broadcast_to(v.reshape(1, 1, -1), (2, 1, v.size))


SHARD_ROWS = {"mla_w_in": 192, "mla_w_uq": 192, "mla_w_ukv": 256, "s5_w_in": 256}


def _t_shard(wsh, rows):
    t = wsh[0].T.astype(BF16)
    return jnp.pad(t, ((0, rows - t.shape[0]), (0, 0)))


def _win_order():
    w = IN_W // NDEV
    perm = np.zeros((IN_WP, NDEV * SHARD_ROWS["mla_w_in"]), np.float32)
    first = QL + KVL + ROPE
    for c in range(IN_W):
        n = c + HEADS * VD if c < first else c - first
        perm[n, (c // w) * SHARD_ROWS["mla_w_in"] + c % w] = 1.0
    return jnp.asarray(perm, BF16)


def local_step(xa, tgt, mod, Wt, small):
    T = xa.shape[0]
    sh = [mod[i, :, None, 0:D] for i in range(2)]
    sc = [mod[i, :, None, D:2 * D] for i in range(2)]
    gt = [mod[i, :, None, 2 * D:] for i in range(2)]
    ng = [_vec2(small["norm_g"][i]) for i in range(2)]
    qg, kvg = _vec2(small["mla_q_norm"]), _vec2(small["mla_kv_norm"])
    cosf, sinf, pm, pmt = _rope_tables(T)

    (h0,), _ = rowwise(st_norm_mod, [xa], [ng[0], sc[0], sh[0]], [(D, BF16)], [], "l0_norm")
    p0 = mm(h0, Wt["mla_w_in"], "nt", "l0_in")
    z0, cq, ckv = (p0, 0, HEADS * VD), (p0, HEADS * VD // QL, QL), (p0, (HEADS * VD + QL) // KVL, KVL)
    kr = p0[:, HEADS * VD + QL + KVL:HEADS * VD + QL + KVL + ROPE]
    (cqn,), _ = rowwise(st_rms, [cq], [qg], [(QL, BF16)], [], "l0_qnorm")
    (ckvn,), _ = rowwise(st_rms, [ckv], [kvg], [(KVL, BF16)], [], "l0_kvnorm")
    q = mm(cqn, Wt["mla_w_uq"], "nt", "l0_uq")
    kv = mm(ckvn, Wt["mla_w_ukv"], "nt", "l0_ukv")
    qh = q.reshape(T, HEADS, QK).transpose(1, 0, 2)
    kvh = kv.reshape(T, HEADS, NOPE + VD).transpose(1, 0, 2)
    kraw = jnp.concatenate([kvh[..., :NOPE], jnp.broadcast_to(kr[None], (HEADS, T, ROPE))], axis=-1)
    Q = rope(qh, cosf, sinf, pm, False, BF16, "l0_rope_q")
    K = rope(kraw, cosf, sinf, pm, False, BF16, "l0_rope_k")
    V = kvh[..., NOPE:].astype(BF16)
    o, lse = attn_fwd(Q, K, V, "l0_attn")
    o2 = o.transpose(1, 0, 2).reshape(T, HEADS * VD)
    (og,), _ = rowwise(st_gate, [o2, z0], [], [(D, BF16)], [], "l0_gate")
    out0 = mm(og, Wt["mla_w_out"], "nn", "l0_out")
    (x1,), _ = rowwise(st_resid, [xa, out0], [gt[0]], [(D, F32)], [], "l0_resid")

    ls = small["s5_log_step"].reshape(2, G, 1)
    a_re, a_im = small["s5_a_re"].reshape(2, G, P), small["s5_a_im"].reshape(2, G, P)
    b_re, b_im = small["s5_b_re"].reshape(2, G * P, CH), small["s5_b_im"].reshape(2, G * P, CH)
    lam_re, lam_im, f_re, f_im = disc_fwd(a_re, a_im, ls, "s5_disc")
    f_re2, f_im2 = f_re.reshape(2, G * P, 1), f_im.reshape(2, G * P, 1)
    bb_re, bb_im = disc_b(f_re2, f_im2, b_re, b_im, "s5_disc_b")
    bre = _block_diag(bb_re.reshape(2, G, P, CH).transpose(0, 1, 3, 2))
    bim = _block_diag(bb_im.reshape(2, G, P, CH).transpose(0, 1, 3, 2))
    cre = _block_diag(small["s5_c_re"].reshape(2, G, CH, P))
    cim = _block_diag(small["s5_c_im"].reshape(2, G, CH, P))
    lam_re4, lam_im4 = lam_re.reshape(2, NJ, 1, SB), lam_im.reshape(2, NJ, 1, SB)

    (h1,), _ = rowwise(st_norm_mod, [x1], [ng[1], sc[1], sh[1]], [(D, BF16)], [], "l1_norm")
    p1 = mm(h1, Wt["s5_w_in"], "nt", "l1_in")
    u, z1 = (p1, 0, D), (p1, 1, D)
    yssm = scan_fwd(p1, lam_re4, lam_im4, bre, bim, cre, cim, "s5_scan")
    dvec, bglu = _vec2(small["s5_d"]), _vec2(small["s5_b_glu"])
    (y, y1b), _ = rowwise(st_s5a, [yssm, u], [dvec], [(D, F32), (D, BF16)], [], "l1_gelu")
    gl = mm(y1b, Wt["s5_w_glu"], "nn", "l1_glu")
    (y3,), _ = rowwise(st_s5b, [y, gl, z1], [bglu], [(D, BF16)], [], "l1_gate")
    out1 = mm(y3, Wt["s5_w_out"], "nn", "l1_out")
    (x2,), _ = rowwise(st_resid, [x1, out1], [gt[1]], [(D, F32)], [], "l1_resid")

    fg = _vec2(small["final_g"])
    (dx2l,), (dfg, lvec) = rowwise(st_final, [x2, tgt], [fg], [(D, F32)], [D, 128], "final", lat_only=True)
    dx2 = jnp.concatenate([jnp.zeros((LC, D), F32), dx2l], axis=0)

    (dout1,), (dgt1,) = rowwise(st_resid_bwd, [dx2, out1], [gt[1]], [(D, BF16)], [D], "l1_resid_b")
    g_w_out5 = mm(y3, dout1, "tn", "l1_out_dw")
    dy3 = mm(dout1, Wt["s5_w_out"], "nt", "l1_out_dx")
    (dgl, dz1, dy1a), (dbglu,) = rowwise(st_s5b_bwd, [dy3, y, gl, z1], [bglu], [(D, BF16), (D, BF16), (D, F32)], [D], "l1_gate_b")
    g_w_glu = mm(y1b, dgl, "tn", "l1_glu_dw")
    dy1b = mm(dgl, Wt["s5_w_glu"], "nt", "l1_glu_dx")
    (dy, du_d), (dd,) = rowwise(st_s5a_bwd, [dy1a, dy1b, y, u], [dvec], [(D, F32), (D, F32)], [D], "l1_gelu_b")
    du_s, dlr, dli, dbre, dbim, dcre, dcim = scan_bwd(p1, dy, lam_re4, lam_im4, bre, bim, cre, cim, "s5_scan_b")
    du = du_d + du_s
    dbb_re = _diag_blocks(dbre).transpose(0, 1, 3, 2).reshape(2, G * P, CH)
    dbb_im = _diag_blocks(dbim).transpose(0, 1, 3, 2).reshape(2, G * P, CH)
    g_c_re, g_c_im = _diag_blocks(dcre), _diag_blocks(dcim)
    g_b_re, g_b_im, dfr, dfi = disc_b_bwd(f_re2, f_im2, b_re, b_im, dbb_re, dbb_im, "s5_disc_b_b")
    g_a_re, g_a_im, g_ls = disc_a_bwd(a_re, a_im, ls, dlr.reshape(2, G, P), dli.reshape(2, G, P),
                                      dfr.reshape(2, G, P), dfi.reshape(2, G, P), "s5_disc_b_a")
    dp1 = jnp.concatenate([du.astype(BF16), dz1], axis=1)
    g_w_in5 = mm(h1, dp1, "tn", "l1_in_dw", tn=D)
    dh1 = mm(dp1, Wt["s5_w_in"], "nn", "l1_in_dx")
    (dx1,), (dsh1, dsc1, dng1) = rowwise(st_norm_mod_bwd, [x1, dh1, dx2], [ng[1], sc[1]], [(D, F32)], [D, D, D], "l1_norm_b")

    (dout0,), (dgt0,) = rowwise(st_resid_bwd, [dx1, out0], [gt[0]], [(D, BF16)], [D], "l0_resid_b")
    g_w_out = mm(og, dout0, "tn", "l0_out_dw")
    dog = mm(dout0, Wt["mla_w_out"], "nt", "l0_out_dx")
    (do2, dz0), _ = rowwise(st_gate_bwd, [dog, o2, z0], [], [(D, F32), (D, F32)], [], "l0_gate_b")
    doh = do2.reshape(T, HEADS, VD).transpose(1, 0, 2)
    dQ, dK, dV = attn_bwd(Q, K, V, o, lse, doh, "l0_attn_b")
    dqh = rope(dQ, cosf, sinf, pmt, True, F32, "l0_rope_q_b")
    dkraw, dksum = rope(dK, cosf, sinf, pmt, True, F32, "l0_rope_k_b", head_sum=True)
    dq = dqh.transpose(1, 0, 2).reshape(T, HEADS * QK).astype(BF16)
    dkv = jnp.concatenate([dkraw[..., :NOPE], dV], axis=-1).transpose(1, 0, 2).reshape(T, HEADS * (NOPE + VD)).astype(BF16)
    dkr = dksum[:, NOPE:]
    g_w_uq = mm(cqn, dq, "tn", "l0_uq_dw")
    dcqn = mm(dq, Wt["mla_w_uq"], "nn", "l0_uq_dx")
    g_w_ukv = mm(ckvn, dkv, "tn", "l0_ukv_dw", tm=KVL)
    dckvn = mm(dkv, Wt["mla_w_ukv"], "nn", "l0_ukv_dx")
    (dcq,), (dqg,) = rowwise(st_rms_bwd, [cq, dcqn], [qg], [(QL, F32)], [QL], "l0_qnorm_b")
    (dckv,), (dkvg,) = rowwise(st_rms_bwd, [ckv, dckvn], [kvg], [(KVL, F32)], [KVL], "l0_kvnorm_b")
    dp0 = jnp.concatenate([dz0, dcq, dckv, dkr, jnp.zeros((T, IN_WP - IN_W), F32)], axis=1).astype(BF16)
    g_p = mm(h0, dp0, "tn", "l0_in_dw")
    g_w_in = jnp.concatenate([g_p[:, HEADS * VD:IN_W], g_p[:, :HEADS * VD]], axis=1)
    dh0 = mm(dp0, Wt["mla_w_in"], "nn", "l0_in_dx")
    (dxa,), (dsh0, dsc0, dng0) = rowwise(st_norm_mod_bwd, [xa, dh0, dx1], [ng[0], sc[0]], [(D, F32)], [D, D, D], "l0_norm_b")

    dmod = jnp.stack([jnp.concatenate([dsh0, dsc0, dgt0], axis=-1)[:, 0], jnp.concatenate([dsh1, dsc1, dgt1], axis=-1)[:, 0]])
    both = lambda s: s[0, 0] + s[1, 0]
    gbig = {"mla_w_in": g_w_in, "mla_w_uq": g_w_uq, "mla_w_ukv": g_w_ukv, "mla_w_out": g_w_out,
            "s5_w_in": g_w_in5, "s5_w_glu": g_w_glu, "s5_w_out": g_w_out5, "s5_d": both(dd), "s5_b_glu": both(dbglu)}
    gsmall = {"norm_g": jnp.stack([both(dng0), both(dng1)]), "mla_q_norm": both(dqg), "mla_kv_norm": both(dkvg),
              "s5_a_re": g_a_re, "s5_a_im": g_a_im, "s5_log_step": g_ls, "s5_b_re": g_b_re, "s5_b_im": g_b_im,
              "s5_c_re": g_c_re, "s5_c_im": g_c_im, "final_g": dfg[1, 0]}
    return lvec[1], dxa, dmod, gbig, gsmall


COL_SHARDED = ("mla_w_in", "mla_w_uq", "mla_w_ukv", "s5_w_in")
ROW_SHARDED = ("mla_w_out", "s5_w_glu", "s5_w_out")
VEC_SHARDED = ("s5_d", "s5_b_glu")
BIG = COL_SHARDED + ROW_SHARDED
SHARDED = BIG + VEC_SHARDED
SMALL_RS = ("norm_g", "mla_q_norm", "mla_kv_norm", "s5_a_re", "s5_a_im", "s5_log_step", "s5_b_re", "s5_b_im",
            "s5_c_re", "s5_c_im", "final_g")
REPL = ("c_ctx", "ada_b") + SMALL_RS
ORDER = ("c_ctx", "ada_w", "ada_b", "norm_g", "mla_w_in", "mla_q_norm", "mla_w_uq", "mla_kv_norm", "mla_w_ukv",
         "mla_w_out", "s5_w_in", "s5_a_re", "s5_a_im", "s5_log_step", "s5_b_re", "s5_b_im", "s5_c_re", "s5_c_im",
         "s5_d", "s5_w_glu", "s5_b_glu", "s5_w_out", "final_g")


def kernel(x, c, ctx, c_ctx, ada_w, ada_b, norm_g, mla_w_in, mla_q_norm, mla_w_uq, mla_kv_norm, mla_w_ukv, mla_w_out, s5_w_in, s5_a_re, s5_a_im, s5_log_step, s5_b_re, s5_b_im, s5_c_re, s5_c_im, s5_d, s5_w_glu, s5_b_glu, s5_w_out, final_g, loss_target, m_c_ctx, m_ada_w, m_ada_b, m_norm_g, m_mla_w_in, m_mla_q_norm, m_mla_w_uq, m_mla_kv_norm, m_mla_w_ukv, m_mla_w_out, m_s5_w_in, m_s5_a_re, m_s5_a_im, m_s5_log_step, m_s5_b_re, m_s5_b_im, m_s5_c_re, m_s5_c_im, m_s5_d, m_s5_w_glu, m_s5_b_glu, m_s5_w_out, m_final_g, v_c_ctx, v_ada_w, v_ada_b, v_norm_g, v_mla_w_in, v_mla_q_norm, v_mla_w_uq, v_mla_kv_norm, v_mla_w_ukv, v_mla_w_out, v_s5_w_in, v_s5_a_re, v_s5_a_im, v_s5_log_step, v_s5_b_re, v_s5_b_im, v_s5_c_re, v_s5_c_im, v_s5_d, v_s5_w_glu, v_s5_b_glu, v_s5_w_out, v_final_g):
    w = dict(c_ctx=c_ctx, ada_w=ada_w, ada_b=ada_b, norm_g=norm_g, mla_w_in=mla_w_in, mla_q_norm=mla_q_norm,
             mla_w_uq=mla_w_uq, mla_kv_norm=mla_kv_norm, mla_w_ukv=mla_w_ukv, mla_w_out=mla_w_out, s5_w_in=s5_w_in,
             s5_a_re=s5_a_re, s5_a_im=s5_a_im, s5_log_step=s5_log_step, s5_b_re=s5_b_re, s5_b_im=s5_b_im,
             s5_c_re=s5_c_re, s5_c_im=s5_c_im, s5_d=s5_d, s5_w_glu=s5_w_glu, s5_b_glu=s5_b_glu, s5_w_out=s5_w_out,
             final_g=final_g)
    m = dict(c_ctx=m_c_ctx, ada_w=m_ada_w, ada_b=m_ada_b, norm_g=m_norm_g, mla_w_in=m_mla_w_in, mla_q_norm=m_mla_q_norm,
             mla_w_uq=m_mla_w_uq, mla_kv_norm=m_mla_kv_norm, mla_w_ukv=m_mla_w_ukv, mla_w_out=m_mla_w_out,
             s5_w_in=m_s5_w_in, s5_a_re=m_s5_a_re, s5_a_im=m_s5_a_im, s5_log_step=m_s5_log_step, s5_b_re=m_s5_b_re,
             s5_b_im=m_s5_b_im, s5_c_re=m_s5_c_re, s5_c_im=m_s5_c_im, s5_d=m_s5_d, s5_w_glu=m_s5_w_glu,
             s5_b_glu=m_s5_b_glu, s5_w_out=m_s5_w_out, final_g=m_final_g)
    v = dict(c_ctx=v_c_ctx, ada_w=v_ada_w, ada_b=v_ada_b, norm_g=v_norm_g, mla_w_in=v_mla_w_in, mla_q_norm=v_mla_q_norm,
             mla_w_uq=v_mla_w_uq, mla_kv_norm=v_mla_kv_norm, mla_w_ukv=v_mla_w_ukv, mla_w_out=v_mla_w_out,
             s5_w_in=v_s5_w_in, s5_a_re=v_s5_a_re, s5_a_im=v_s5_a_im, s5_log_step=v_s5_log_step, s5_b_re=v_s5_b_re,
             s5_b_im=v_s5_b_im, s5_c_re=v_s5_c_re, s5_c_im=v_s5_c_im, s5_d=v_s5_d, s5_w_glu=v_s5_w_glu,
             s5_b_glu=v_s5_b_glu, s5_w_out=v_s5_w_out, final_g=v_final_g)

    me = 4 * lax.axis_index("x") + 2 * lax.axis_index("y") + lax.axis_index("c")
    WA = ada_w.shape[2]

    cg = exchange([c], False, "gather_c")[0].reshape(NDEV, D)
    cc2 = c_ctx.reshape(1, D)
    ada_b_loc = lax.dynamic_slice_in_dim(ada_b.reshape(2, 3 * D // WA, WA), me, 1, axis=1)
    part = ada_fwd(cg, cc2, ada_w, ada_b_loc, "ada_fwd")
    pg = exchange([part.reshape(2 * 2 * NDEV, WA)], False, "gather_mod")[0].reshape(NDEV, 2, 2 * NDEV, WA)
    mod_l = lax.dynamic_index_in_dim(pg, me, axis=2, keepdims=False).transpose(1, 0, 2).reshape(2, 3 * D)
    mod_c = pg[:, :, NDEV, :].transpose(1, 0, 2).reshape(2, 3 * D)
    mod = jnp.stack([mod_c, mod_l], axis=1)

    vec_bits = lax.bitcast_convert_type(jnp.concatenate([s5_d, s5_b_glu], axis=0), BF16).reshape(2, -1)
    wsend = [_t_shard(w[n], SHARD_ROWS[n]) for n in COL_SHARDED] + [w[n][0].astype(BF16) for n in ROW_SHARDED] + [vec_bits]
    wgot = exchange(wsend, False, "gather_w")
    Wt = {n: a.reshape(-1, a.shape[-1]) for n, a in zip(BIG, wgot)}
    Wt["mla_w_in"] = mm(_win_order(), Wt["mla_w_in"], "nn", "w_in_order", out_dtype=BF16)
    vecs = lax.bitcast_convert_type(wgot[-1].reshape(NDEV, 2, -1, 2), F32)

    small = {n: w[n] for n in SMALL_RS}
    small["s5_d"] = vecs[:, 0, :].reshape(D)
    small["s5_b_glu"] = vecs[:, 1, :].reshape(D)

    xa = jnp.concatenate([ctx[0], x[0]], axis=0)
    lvec, dxa, dmod, gbig, gsmall = local_step(xa, loss_target[0], mod, Wt, small)
    loss = lax.psum(lvec[0, 0], ("x", "y", "c"))
    grad_x = dxa[LC:][None]

    small_flat = jnp.concatenate([gsmall[n].reshape(-1) for n in SMALL_RS])
    small_rows = -(-small_flat.shape[0] // (NDEV * LANES))
    small_flat = jnp.pad(small_flat, (0, NDEV * small_rows * LANES - small_flat.shape[0])).reshape(NDEV, small_rows * LANES)
    pieces = [_col_shards(gbig[n]) if n in COL_SHARDED else gbig[n].reshape(NDEV, -1) for n in SHARDED]
    sharded_len = sum(p.shape[1] for p in pieces)
    sharded_rows = -(-sharded_len // LANES)
    pad = jnp.zeros((NDEV, sharded_rows * LANES - sharded_len), F32)
    send = jnp.concatenate(pieces + [pad, small_flat], axis=1).reshape(NDEV, sharded_rows + small_rows, LANES)
    recv = exchange([send], True, "scatter_grads")[0]

    zero_rows = jnp.zeros((small_rows, LANES), F32)
    slab = lambda t: jnp.concatenate([_pack([t[n] for n in SHARDED], sharded_rows), zero_rows], axis=0)
    g_sh, d_sh, m_sh, v_sh = adamw(recv, slab(w), slab(m), slab(v), "adamw_sharded")
    shapes_sh = [w[n].shape for n in SHARDED]
    out = {}
    for key, arr in (("g", g_sh), ("d", d_sh), ("m", m_sh), ("v", v_sh)):
        for n, a in zip(SHARDED, _unpack(arr[:sharded_rows], shapes_sh)):
            out[key, n] = a

    dm_rows = (2 * 2 * 3 * D) // LANES
    mine = jnp.concatenate([g_sh[sharded_rows:], dmod.reshape(dm_rows, LANES)], axis=0)
    got = exchange([mine], False, "gather_small")[0]
    gsm = _unpack(got[:, :small_rows].reshape(-1), [w[n].shape for n in SMALL_RS])
    dm_all = got[:, small_rows:].reshape(NDEV, 2, 2, 3 * D)

    dm_cols = lax.dynamic_slice_in_dim(dm_all.reshape(NDEV, 2, 2, 3 * D // WA, WA), me, 1, axis=3)[:, :, :, 0]
    dm_loc = jnp.concatenate([dm_cols[:, :, 1].transpose(1, 0, 2), dm_cols[:, :, 0].transpose(1, 0, 2)], axis=1)
    g_ada_w, dcc_part, g_ada_b = ada_bwd(cg, cc2, ada_w, dm_loc, dm_all.transpose(0, 2, 1, 3).reshape(2 * NDEV, 2, 3 * D), "ada_bwd")
    dcc_all = exchange([dcc_part], False, "gather_dcc")[0].reshape(NDEV, D)
    g_c_ctx = cctx_finish(dcc_all, cc2, "cctx_finish").reshape(D)

    aw = lambda t: t.reshape(-1, LANES)
    g_a, d_a, m_a, v_a = adamw(aw(g_ada_w)[None], aw(ada_w), aw(m_ada_w), aw(v_ada_w), "adamw_ada")
    for key, arr in (("g", g_a), ("d", d_a), ("m", m_a), ("v", v_a)):
        out[key, "ada_w"] = arr.reshape(ada_w.shape)
    grep = {"c_ctx": g_c_ctx, "ada_b": g_ada_b}
    grep.update(dict(zip(SMALL_RS, gsm)))
    rslab = lambda t: _pack([t[n] for n in REPL])
    g_r, d_r, m_r, v_r = adamw(rslab(grep)[None], rslab(w), rslab(m), rslab(v), "adamw_repl")
    shapes_r = [w[n].shape for n in REPL]
    for key, arr in (("g", g_r), ("d", d_r), ("m", m_r), ("v", v_r)):
        for n, a in zip(REPL, _unpack(arr, shapes_r)):
            out[key, n] = a

    return (loss, grad_x, *[out["g", n] for n in ORDER], *[out["d", n] for n in ORDER],
            *[out["m", n] for n in ORDER], *[out["v", n] for n in ORDER])
```

```python
import math

import numpy as np
import jax
import jax.numpy as jnp
from jax import lax
from jax.experimental import pallas as pl
from jax.experimental.pallas import tpu as pltpu

F32 = jnp.float32
BF16 = jnp.bfloat16

D = 1024
L = 2048
LC = 256
NDEV = 8
GRID_W = 64
EPS = 1e-6
HEADS = 16
NOPE = 64
ROPE = 32
QK = NOPE + ROPE
VD = 64
IN_W = 256 + 128 + ROPE + HEADS * 64
IN_WP = 1536
QL = 256
KVL = 128
SCALE = QK ** -0.5
THETA = 10000.0
G = 64
P = 64
CH = 16
GB = 8
NJ = G // GB
UB = GB * CH
SB = GB * P
SEG = 8
TB = 256
VMEM_LIMIT = 56 * 1024 * 1024
B1, B2, LR, AEPS, WD, STEP = 0.9, 0.999, 0.001, 1e-8, 0.01, 10
MESH_T = pl.DeviceIdType.MESH


def _cp(sem=None):
    return pltpu.CompilerParams(dimension_semantics=sem, vmem_limit_bytes=VMEM_LIMIT)


def _sig(x):
    return 1.0 / (1.0 + jnp.exp(-x))


def _silu(x):
    return x * _sig(x)


def _dsilu(x):
    s = _sig(x)
    return s * (1.0 + x * (1.0 - s))


_GK = math.sqrt(2.0 / math.pi)


def _gelu(x):
    return 0.5 * x * (1.0 + jnp.tanh(_GK * (x + 0.044715 * x * x * x)))


def _dgelu(x):
    t = jnp.tanh(_GK * (x + 0.044715 * x * x * x))
    return 0.5 * (1.0 + t) + 0.5 * x * (1.0 - t * t) * _GK * (1.0 + 3 * 0.044715 * x * x)


def _rs(x):
    return lax.rsqrt(jnp.mean(x * x, axis=-1, keepdims=True) + EPS)


def _sum0(x):
    return jnp.sum(x, axis=0, keepdims=True)


def st_norm_mod(x, g, sc, sh):
    y = x * _rs(x) * g
    return (y * (1.0 + sc) + sh,), ()


def st_norm_mod_bwd(x, dh, dres, g, sc):
    r = _rs(x)
    xn = x * r
    y = xn * g
    dy = dh * (1.0 + sc)
    dxn = dy * g
    dx = r * (dxn - xn * jnp.mean(dxn * xn, axis=-1, keepdims=True))
    return (dres + dx,), (_sum0(dh), _sum0(dh * y), _sum0(dy * xn))


def st_rms(x, g):
    return (x * _rs(x) * g,), ()


def st_rms_bwd(x, dy, g):
    r = _rs(x)
    n = x * r
    dn = dy * g
    dx = r * (dn - n * jnp.mean(dn * n, axis=-1, keepdims=True))
    return (dx,), (_sum0(dy * n),)


def st_gate(o, z):
    return (o * _silu(z),), ()


def st_gate_bwd(dog, o, z):
    return (dog * _silu(z), dog * o * _dsilu(z)), ()


def st_resid(x, out, gt):
    return (x + gt * out,), ()


def st_resid_bwd(dx, out, gt):
    return (dx * gt,), (_sum0(dx * out),)


def st_s5a(yssm, u, d):
    y = yssm + d * u
    return (y, _gelu(y)), ()


def st_s5b(y, gl, z, b):
    return (_gelu(y) * _sig(gl + b) * _silu(z),), ()


def st_s5b_bwd(dy3, y, gl, z, b):
    y1 = _gelu(y)
    s = _sig(gl + b)
    dy2 = dy3 * _silu(z)
    dz = dy3 * y1 * s * _dsilu(z)
    dgl = dy2 * y1 * s * (1.0 - s)
    return (dgl, dz, dy2 * s), (_sum0(dgl),)


def st_s5a_bwd(dy1a, dy1b, y, u, d):
    dy = (dy1a + dy1b) * _dgelu(y)
    return (dy, dy * d), (_sum0(dy * u),)


def st_final(x2, tgt, g):
    r = _rs(x2)
    n = x2 * r
    e = n * g - tgt
    dyo = e * (1.0 / D)
    dn = dyo * g
    dx = r * (dn - n * jnp.mean(dn * n, axis=-1, keepdims=True))
    lsum = jnp.sum(_sum0(e * e), axis=1, keepdims=True) * (0.5 / D)
    return (dx,), (_sum0(dyo * n), jnp.broadcast_to(lsum, (1, 128)))


def rowwise(fn, rows, vecs, out_rows, out_sums, name, lat_only=False):
    rows = [a if isinstance(a, tuple) else (a, 0, a.shape[1]) for a in rows]
    nrows = L if lat_only else rows[0][0].shape[0]
    nb = nrows // TB
    nr, nv, no = len(rows), len(vecs), len(out_rows)

    def body(*refs):
        i = pl.program_id(0)
        vals = [r[...] for r in refs[:nr]] + [r[0] for r in refs[nr:nr + nv]]
        outs, sums = fn(*vals)
        for r, o in zip(refs[nr + nv:nr + nv + no], outs):
            r[...] = o.astype(r.dtype)
        sum_refs = refs[nr + nv + no:]
        if sum_refs:
            @pl.when((i == 0) if lat_only else (i <= 1))
            def _():
                for r in sum_refs:
                    r[...] = jnp.zeros_like(r)
            for r, s in zip(sum_refs, sums):
                r[0] += s

    def row_spec(a):
        arr, cb, width = a
        off = (arr.shape[0] - nrows) // TB
        return pl.BlockSpec((TB, width), lambda i: (i + off, cb))

    if lat_only:
        kind = lambda i: (1, 0, 0)
    else:
        kind = lambda i: (jnp.minimum(i, 1), 0, 0)
    in_specs = [row_spec(a) for a in rows] + [pl.BlockSpec((1, 1, v.shape[2]), kind) for v in vecs]
    out_specs = [pl.BlockSpec((TB, c), lambda i: (i, 0)) for c, _ in out_rows] + \
                [pl.BlockSpec((1, 1, c), kind) for c in out_sums]
    out_shape = [jax.ShapeDtypeStruct((nrows, c), dt) for c, dt in out_rows] + \
                [jax.ShapeDtypeStruct((2, 1, c), F32) for c in out_sums]
    res = pl.pallas_call(body, grid=(nb,), in_specs=in_specs, out_specs=out_specs, out_shape=out_shape,
                         compiler_params=_cp(("arbitrary",)), name=name)(*[a[0] for a in rows], *vecs)
    return res[:no], res[no:]


_DN = {"nn": (((1,), (0,)), ((), ())), "nt": (((1,), (1,)), ((), ())), "tn": (((0,), (0,)), ((), ()))}


def mm(a, b, mode, name, out_dtype=F32, tm=256, tn=None, shard_out=False):
    if mode == "nn":
        (M, K), (_, N) = a.shape, b.shape
    elif mode == "nt":
        (M, K), (N, _) = a.shape, b.shape
    else:
        (K, M), (_, N) = a.shape, b.shape
    tm = min(tm, M)
    tn = N if tn is None else tn
    dn = _DN[mode]

    def body(a_ref, b_ref, o_ref):
        o_ref[...] = lax.dot_general(a_ref[...].astype(BF16), b_ref[...].astype(BF16), dn,
                                     preferred_element_type=F32).astype(o_ref.dtype)

    a_spec = pl.BlockSpec((K, tm), lambda i, j: (0, i)) if mode == "tn" else pl.BlockSpec((tm, K), lambda i, j: (i, 0))
    b_spec = pl.BlockSpec((tn, K), lambda i, j: (j, 0)) if mode == "nt" else pl.BlockSpec((K, tn), lambda i, j: (0, j))
    if shard_out:
        def body(a_ref, b_ref, o_ref):
            o_ref[0] = lax.dot_general(a_ref[...].astype(BF16), b_ref[...].astype(BF16), dn,
                                       preferred_element_type=F32).astype(o_ref.dtype)
        out_spec = pl.BlockSpec((1, tm, tn), lambda i, j: (j, i, 0))
        out_shape = jax.ShapeDtypeStruct((N // tn, M, tn), out_dtype)
    else:
        out_spec = pl.BlockSpec((tm, tn), lambda i, j: (i, j))
        out_shape = jax.ShapeDtypeStruct((M, N), out_dtype)
    return pl.pallas_call(body, grid=(M // tm, N // tn), in_specs=[a_spec, b_spec], out_specs=out_spec, out_shape=out_shape,
                          compiler_params=_cp(("parallel", "arbitrary")), name=name)(a, b)


def _rope_tables(T):
    nlat = T - LC
    pos = np.arange(nlat)
    row, col = pos // GRID_W, pos % GRID_W
    half = ROPE // 2
    inv = 1.0 / (THETA ** (np.arange(0, half, 2, dtype=np.float64) / half))
    cosf = np.ones((T, QK), np.float64)
    sinf = np.zeros((T, QK), np.float64)
    perm = np.zeros((QK, QK), np.float32)
    for m in range(ROPE):
        j = NOPE + m
        blk, w = m // half, m % half
        ang = (row if blk == 0 else col)[:, None] * inv[None, :]
        f = w % (half // 2)
        cosf[LC:, j] = np.cos(ang[:, f])
        if w < half // 2:
            sinf[LC:, j] = -np.sin(ang[:, f])
            perm[j + half // 2, j] = 1.0
        else:
            sinf[LC:, j] = np.sin(ang[:, f])
            perm[j - half // 2, j] = 1.0
    return jnp.asarray(cosf, F32), jnp.asarray(sinf, F32), jnp.asarray(perm, BF16), jnp.asarray(perm.T, BF16)


def _exact_perm(x, pm):
    hi = x.astype(BF16)
    r1 = x - hi.astype(F32)
    mid = r1.astype(BF16)
    lo = (r1 - mid.astype(F32)).astype(BF16)
    dot = lambda a: jnp.dot(a, pm, preferred_element_type=F32)
    return dot(hi) + dot(mid) + dot(lo)


def rope(x, cosf, sinf, pm, inverse, out_dtype, name, head_sum=False):
    H, T, _ = x.shape

    def body(x_ref, c_ref, s_ref, p_ref, o_ref, *rest):
        cv, sv, pv = c_ref[...], s_ref[...], p_ref[...]
        total = None
        for h in range(H):
            xv = x_ref[h]
            if inverse:
                out = xv * cv + _exact_perm(xv * sv, pv)
            else:
                out = xv * cv + _exact_perm(xv, pv) * sv
            o_ref[h] = out.astype(o_ref.dtype)
            if head_sum:
                total = out if total is None else total + out
        if head_sum:
            rest[0][...] = total

    out_shape = [jax.ShapeDtypeStruct((H, T, QK), out_dtype)]
    out_specs = [pl.BlockSpec((H, TB, QK), lambda i: (0, i, 0))]
    if head_sum:
        out_shape.append(jax.ShapeDtypeStruct((T, QK), F32))
        out_specs.append(pl.BlockSpec((TB, QK), lambda i: (i, 0)))
    res = pl.pallas_call(
        body, grid=(T // TB,),
        in_specs=[pl.BlockSpec((H, TB, QK), lambda i: (0, i, 0)), pl.BlockSpec((TB, QK), lambda i: (i, 0)),
                  pl.BlockSpec((TB, QK), lambda i: (i, 0)), pl.BlockSpec((QK, QK), lambda i: (0, 0))],
        out_specs=out_specs, out_shape=out_shape, compiler_params=_cp(("parallel",)), name=name)(x, cosf, sinf, pm)
    return res if head_sum else res[0]


def _scores(q, k, qi):
    s = lax.dot_general(q, k, _DN["nt"], preferred_element_type=F32) * SCALE
    col = lax.broadcasted_iota(jnp.int32, s.shape, 1)
    return jnp.where(jnp.logical_and(qi == 0, col >= LC), -1e30, s)


def attn_fwd(q, k, v, name):
    H, T, _ = q.shape

    def body(q_ref, k_ref, v_ref, o_ref, lse_ref):
        s = _scores(q_ref[0], k_ref[0], pl.program_id(1))
        m = jnp.max(s, axis=1, keepdims=True)
        p = jnp.exp(s - m)
        l = jnp.sum(p, axis=1, keepdims=True)
        o = jnp.dot(p.astype(BF16), v_ref[0], preferred_element_type=F32)
        o_ref[0] = o / l
        lse_ref[0] = m + jnp.log(l)

    return pl.pallas_call(
        body, grid=(H, T // TB),
        in_specs=[pl.BlockSpec((1, TB, QK), lambda h, i: (h, i, 0)), pl.BlockSpec((1, T, QK), lambda h, i: (h, 0, 0)),
                  pl.BlockSpec((1, T, VD), lambda h, i: (h, 0, 0))],
        out_specs=[pl.BlockSpec((1, TB, VD), lambda h, i: (h, i, 0)), pl.BlockSpec((1, TB, 1), lambda h, i: (h, i, 0))],
        out_shape=[jax.ShapeDtypeStruct((H, T, VD), F32), jax.ShapeDtypeStruct((H, T, 1), F32)],
        compiler_params=_cp(("parallel", "arbitrary")), name=name)(q, k, v)


def attn_bwd(q, k, v, o, lse, do, name):
    H, T, _ = q.shape

    def body(q_ref, k_ref, v_ref, o_ref, lse_ref, do_ref, dq_ref, dk_ref, dv_ref):
        i = pl.program_id(1)

        @pl.when(i == 0)
        def _():
            dk_ref[...] = jnp.zeros_like(dk_ref)
            dv_ref[...] = jnp.zeros_like(dv_ref)

        qv, kv, dov = q_ref[0], k_ref[0], do_ref[0]
        p = jnp.exp(_scores(qv, kv, i) - lse_ref[0])
        delta = jnp.sum(dov * o_ref[0], axis=1, keepdims=True)
        dob = dov.astype(BF16)
        dv_ref[0] += lax.dot_general(p.astype(BF16), dob, _DN["tn"], preferred_element_type=F32)
        dp = lax.dot_general(dob, v_ref[0], _DN["nt"], preferred_element_type=F32)
        ds = (p * (dp - delta) * SCALE).astype(BF16)
        dq_ref[0] = jnp.dot(ds, kv, preferred_element_type=F32)
        dk_ref[0] += lax.dot_general(ds, qv, _DN["tn"], preferred_element_type=F32)

    blk = lambda c: pl.BlockSpec((1, TB, c), lambda h, i: (h, i, 0))
    full = lambda c: pl.BlockSpec((1, T, c), lambda h, i: (h, 0, 0))
    return pl.pallas_call(
        body, grid=(H, T // TB),
        in_specs=[blk(QK), full(QK), full(VD), blk(VD), blk(1), blk(VD)],
        out_specs=[blk(QK), full(QK), full(VD)],
        out_shape=[jax.ShapeDtypeStruct((H, T, QK), F32), jax.ShapeDtypeStruct((H, T, QK), F32),
                   jax.ShapeDtypeStruct((H, T, VD), F32)],
        compiler_params=_cp(("parallel", "arbitrary")), name=name)(q, k, v, o, lse, do)


def disc_fwd(a_re, a_im, ls, name):
    def body(ar_ref, ai_ref, ls_ref, lr_ref, li_ref, fr_ref, fi_ref):
        ar, ai = ar_ref[...], ai_ref[...]
        dt = jnp.exp(ls_ref[...])
        mag = jnp.exp(ar * dt)
        lr = mag * jnp.cos(ai * dt)
        li = mag * jnp.sin(ai * dt)
        den = ar * ar + ai * ai
        nr = lr - 1.0
        lr_ref[...] = lr
        li_ref[...] = li
        fr_ref[...] = (nr * ar + li * ai) / den
        fi_ref[...] = (li * ar - nr * ai) / den

    return pl.pallas_call(body, out_shape=[jax.ShapeDtypeStruct(a_re.shape, F32)] * 4, name=name)(a_re, a_im, ls)


def disc_b(f_re, f_im, b_re, b_im, name):
    def body(fr_ref, fi_ref, br_ref, bi_ref, or_ref, oi_ref):
        fr, fi, br, bi = fr_ref[...], fi_ref[...], br_ref[...], bi_ref[...]
        or_ref[...] = fr * br - fi * bi
        oi_ref[...] = fr * bi + fi * br

    fs, bs = _disc_b_specs()
    return pl.pallas_call(body, grid=(2, G * P // DISC_ROWS), in_specs=[fs, fs, bs, bs], out_specs=[bs, bs],
                          out_shape=[jax.ShapeDtypeStruct(b_re.shape, F32)] * 2, name=name)(f_re, f_im, b_re, b_im)


DISC_ROWS = 1024


def _disc_b_specs():
    return (pl.BlockSpec((1, DISC_ROWS, 1), lambda d, i: (d, i, 0)), pl.BlockSpec((1, DISC_ROWS, CH), lambda d, i: (d, i, 0)))


def disc_b_bwd(f_re, f_im, b_re, b_im, dbb_re, dbb_im, name):
    def body(fr_ref, fi_ref, br_ref, bi_ref, dr_ref, di_ref, dbr_ref, dbi_ref, dfr_ref, dfi_ref):
        fr, fi, br, bi, dr, di = fr_ref[...], fi_ref[...], br_ref[...], bi_ref[...], dr_ref[...], di_ref[...]
        dbr_ref[...] = fr * dr + fi * di
        dbi_ref[...] = fr * di - fi * dr
        dfr_ref[...] = jnp.sum(dr * br + di * bi, axis=-1, keepdims=True)
        dfi_ref[...] = jnp.sum(di * br - dr * bi, axis=-1, keepdims=True)

    fs, bs = _disc_b_specs()
    return pl.pallas_call(body, grid=(2, G * P // DISC_ROWS), in_specs=[fs, fs, bs, bs, bs, bs], out_specs=[bs, bs, fs, fs],
                          out_shape=[jax.ShapeDtypeStruct(b_re.shape, F32)] * 2 + [jax.ShapeDtypeStruct(f_re.shape, F32)] * 2,
                          name=name)(f_re, f_im, b_re, b_im, dbb_re, dbb_im)


def disc_a_bwd(a_re, a_im, ls, dlr, dli, dfr, dfi, name):
    def body(ar_ref, ai_ref, ls_ref, dlr_ref, dli_ref, dfr_ref, dfi_ref, dar_ref, dai_ref, dls_ref):
        ar, ai = ar_ref[...], ai_ref[...]
        dt = jnp.exp(ls_ref[...])
        mag = jnp.exp(ar * dt)
        cs, sn = jnp.cos(ai * dt), jnp.sin(ai * dt)
        lr, li = mag * cs, mag * sn
        den = ar * ar + ai * ai
        nr = lr - 1.0
        f_re = (nr * ar + li * ai) / den
        f_im = (li * ar - nr * ai) / den
        dn1 = dfr_ref[...] / den
        dn2 = dfi_ref[...] / den
        dden = -(dfr_ref[...] * f_re + dfi_ref[...] * f_im) / den
        dlr_t = dlr_ref[...] + dn1 * ar - dn2 * ai
        dli_t = dli_ref[...] + dn1 * ai + dn2 * ar
        dar = dn1 * nr + dn2 * li + dden * 2.0 * ar
        dai = dn1 * li - dn2 * nr + dden * 2.0 * ai
        dmag = dlr_t * cs + dli_t * sn
        dth = dli_t * lr - dlr_t * li
        dar_ref[...] = dar + dmag * mag * dt
        dai_ref[...] = dai + dth * dt
        dls_ref[...] = jnp.sum(dmag * mag * ar + dth * ai, axis=-1, keepdims=True) * dt

    return pl.pallas_call(body, out_shape=[jax.ShapeDtypeStruct(a_re.shape, F32)] * 2 +
                          [jax.ShapeDtypeStruct(ls.shape, F32)], name=name)(a_re, a_im, ls, dlr, dli, dfr, dfi)


def _cpow(lr, li, n):
    rr, ri = None, None
    br, bi = lr, li
    while n:
        if n & 1:
            if rr is None:
                rr, ri = br, bi
            else:
                rr, ri = rr * br - ri * bi, rr * bi + ri * br
        n >>= 1
        if n:
            br, bi = br * br - bi * bi, 2.0 * br * bi
    return rr, ri


UNROLL = 4


def _seg_scan(xre, xim, lam8, pw, base, seglen, rev, init, fin_re, fin_im, ini_re, ini_im, prev=None):
    lr, li = lam8

    def rows(t):
        return pl.ds(pl.multiple_of(base + t * SEG, SEG), SEG)

    tmap = (lambda n: seglen - 1 - n) if rev else (lambda n: n)
    zero = jnp.zeros((SEG, SB), F32)

    def advance(c, t):
        a, b = c
        return lr * a - li * b + xre[rows(t), :], lr * b + li * a + xim[rows(t), :]

    fin = lax.fori_loop(0, seglen, lambda n, c: advance(c, tmap(n)), (zero, zero), unroll=UNROLL)
    fin_re[...] = fin[0]
    fin_im[...] = fin[1]
    (cr, ci), (pr, pi) = init, pw
    for i in (range(SEG - 1, -1, -1) if rev else range(SEG)):
        ini_re[pl.ds(i, 1), :] = cr
        ini_im[pl.ds(i, 1), :] = ci
        cr, ci = pr * cr - pi * ci + fin_re[pl.ds(i, 1), :], pr * ci + pi * cr + fin_im[pl.ds(i, 1), :]
    start = (ini_re[...], ini_im[...])

    def store(c, t):
        na, nb = advance(c, t)
        xre[rows(t), :] = na
        xim[rows(t), :] = nb
        return na, nb

    if prev is None:
        lax.fori_loop(0, seglen, lambda n, c: store(c, tmap(n)), start, unroll=UNROLL)
        return (cr, ci), None

    sre, sim, s_ini_re, s_ini_im = prev

    def acc_step(c, t, pre, pim):
        na, nb = store(c[:2], t)
        return na, nb, c[2] + na * pre + nb * pim, c[3] + nb * pre - na * pim

    def body(n, c):
        t = tmap(n)
        tp = t - 1 if rev else t + 1
        return acc_step(c, t, sre[rows(tp), :], sim[rows(tp), :])

    c = lax.fori_loop(0, seglen - 1, body, start + (zero, zero), unroll=UNROLL)
    c = acc_step(c, 0 if rev else seglen - 1, s_ini_re[...], s_ini_im[...])
    return (cr, ci), c[2:]


def _lam_tiles(lr, li, lens, conj=False):
    if conj:
        li = -li
    lam8 = (jnp.broadcast_to(lr, (SEG, SB)), jnp.broadcast_to(li, (SEG, SB)))
    return lam8, [_cpow(lr, li, n) for n in lens]


def _stretches(T):
    return ((0, LC // SEG), (LC, (T - LC) // SEG))


def _to_seg_order(src, dst, T):
    for base, seglen in _stretches(T):
        def body(t, carry, base=base, seglen=seglen):
            dst[pl.ds(pl.multiple_of(base + t * SEG, SEG), SEG), :] = src[pl.ds(base + t, SEG, stride=seglen), :]
            return carry
        lax.fori_loop(0, seglen, body, 0, unroll=8)


def _from_seg_order(src, dst, T):
    for base, seglen in _stretches(T):
        def body(t, carry, base=base, seglen=seglen):
            dst[pl.ds(base + t, SEG, stride=seglen), :] = src[pl.ds(pl.multiple_of(base + t * SEG, SEG), SEG), :]
            return carry
        lax.fori_loop(0, seglen, body, 0, unroll=8)


def _scan_specs(T):
    ublk = pl.BlockSpec((T, UB), lambda j: (0, j))
    lam = pl.BlockSpec((2, 1, 1, SB), lambda j: (0, j, 0, 0))
    mat = pl.BlockSpec((2, 1, UB, SB), lambda j: (0, j, 0, 0))
    return ublk, lam, mat


def _dotf(a, b, mode="nn"):
    return lax.dot_general(a, b, _DN[mode], preferred_element_type=F32)


def _zero_state():
    return jnp.zeros((1, SB), F32), jnp.zeros((1, SB), F32)


def scan_fwd(u, lam_re, lam_im, bre, bim, cre, cim, name):
    T = u.shape[0]
    s_ctx, s_lat = LC // SEG, (T - LC) // SEG

    def body(u_ref, lr_ref, li_ref, bre_ref, bim_ref, cre_ref, cim_ref, y_ref, us, ys, sre, sim, fre, fim, ire, iim):
        _to_seg_order(u_ref, us, T)
        ub = us[...].astype(BF16)
        for d in range(2):
            lam8, (pw_c, pw_l) = _lam_tiles(lr_ref[d, 0], li_ref[d, 0], (s_ctx, s_lat))
            sre[...] = _dotf(ub, bre_ref[d, 0].astype(BF16))
            sim[...] = _dotf(ub, bim_ref[d, 0].astype(BF16))
            end_c, _ = _seg_scan(sre, sim, lam8, pw_c, 0, s_ctx, bool(d), _zero_state(), fre, fim, ire, iim)
            _seg_scan(sre, sim, lam8, pw_l, LC, s_lat, bool(d), end_c, fre, fim, ire, iim)
            y = (_dotf(sre[...].astype(BF16), cre_ref[d, 0].astype(BF16), "nt")
                 - _dotf(sim[...].astype(BF16), cim_ref[d, 0].astype(BF16), "nt"))
            if d == 0:
                ys[...] = y
            else:
                ys[...] += y
        _from_seg_order(ys, y_ref, T)

    ublk, lam, mat = _scan_specs(T)
    return pl.pallas_call(
        body, grid=(NJ,), in_specs=[ublk, lam, lam, mat, mat, mat, mat], out_specs=ublk,
        out_shape=jax.ShapeDtypeStruct((T, G * CH), F32),
        scratch_shapes=[pltpu.VMEM((T, UB), F32)] * 2 + [pltpu.VMEM((T, SB), F32)] * 2 + [pltpu.VMEM((SEG, SB), F32)] * 4,
        compiler_params=_cp(("arbitrary",)), name=name)(u, lam_re, lam_im, bre, bim, cre, cim)


def scan_bwd(u, dy, lam_re, lam_im, bre, bim, cre, cim, name):
    T = u.shape[0]
    s_ctx, s_lat = LC // SEG, (T - LC) // SEG

    def body(u_ref, dy_ref, lr_ref, li_ref, bre_ref, bim_ref, cre_ref, cim_ref,
             du_ref, dlr_ref, dli_ref, dbre_ref, dbim_ref, dcre_ref, dcim_ref,
             us, dys, dus, sre, sim, gre, gim, fre, fim, ic_re, ic_im, il_re, il_im, jre, jim):
        _to_seg_order(u_ref, us, T)
        _to_seg_order(dy_ref, dys, T)
        ub, dyb = us[...].astype(BF16), dys[...].astype(BF16)
        for d in range(2):
            rev = bool(d)
            lam8, (pw_c, pw_l) = _lam_tiles(lr_ref[d, 0], li_ref[d, 0], (s_ctx, s_lat))
            cam8, (cw_c, cw_l) = _lam_tiles(lr_ref[d, 0], li_ref[d, 0], (s_ctx, s_lat), conj=True)
            bre_v, bim_v = bre_ref[d, 0].astype(BF16), bim_ref[d, 0].astype(BF16)
            sre[...] = _dotf(ub, bre_v)
            sim[...] = _dotf(ub, bim_v)
            end_c, _ = _seg_scan(sre, sim, lam8, pw_c, 0, s_ctx, rev, _zero_state(), fre, fim, ic_re, ic_im)
            _seg_scan(sre, sim, lam8, pw_l, LC, s_lat, rev, end_c, fre, fim, il_re, il_im)
            gre[...] = _dotf(dyb, cre_ref[d, 0].astype(BF16))
            gim[...] = -_dotf(dyb, cim_ref[d, 0].astype(BF16))
            end_g, acc_l = _seg_scan(gre, gim, cam8, cw_l, LC, s_lat, not rev, _zero_state(), fre, fim, jre, jim,
                                     prev=(sre, sim, il_re, il_im))
            _, acc_c = _seg_scan(gre, gim, cam8, cw_c, 0, s_ctx, not rev, end_g, fre, fim, jre, jim,
                                 prev=(sre, sim, ic_re, ic_im))
            dlr_ref[d, 0] = _sum0(acc_l[0] + acc_c[0])
            dli_ref[d, 0] = _sum0(acc_l[1] + acc_c[1])
            grb, gib = gre[...].astype(BF16), gim[...].astype(BF16)
            du = _dotf(grb, bre_v, "nt") + _dotf(gib, bim_v, "nt")
            if d == 0:
                dus[...] = du
            else:
                dus[...] += du
            dbre_ref[d, 0] = _dotf(ub, grb, "tn")
            dbim_ref[d, 0] = _dotf(ub, gib, "tn")
            dcre_ref[d, 0] = _dotf(dyb, sre[...].astype(BF16), "tn")
            dcim_ref[d, 0] = -_dotf(dyb, sim[...].astype(BF16), "tn")
        _from_seg_order(dus, du_ref, T)

    ublk, lam, mat = _scan_specs(T)
    lam_s = jax.ShapeDtypeStruct(lam_re.shape, F32)
    mat_s = jax.ShapeDtypeStruct(bre.shape, F32)
    return pl.pallas_call(
        body, grid=(NJ,), in_specs=[ublk, ublk, lam, lam, mat, mat, mat, mat],
        out_specs=[ublk, lam, lam, mat, mat, mat, mat],
        out_shape=[jax.ShapeDtypeStruct((T, G * CH), F32), lam_s, lam_s, mat_s, mat_s, mat_s, mat_s],
        scratch_shapes=[pltpu.VMEM((T, UB), F32)] * 3 + [pltpu.VMEM((T, SB), F32)] * 4 + [pltpu.VMEM((SEG, SB), F32)] * 8,
        compiler_params=_cp(("arbitrary",)), name=name)(u, dy, lam_re, lam_im, bre, bim, cre, cim)


def _block_diag(m):
    m5 = m.reshape(2, NJ, GB, CH, P)
    eye = jnp.eye(GB, dtype=m.dtype)
    return (m5[:, :, :, :, None, :] * eye[None, None, :, None, :, None]).reshape(2, NJ, UB, SB)


def _diag_blocks(m):
    m6 = m.reshape(2, NJ, GB, CH, GB, P)
    idx = jnp.arange(GB)
    return m6[:, :, idx, :, idx, :].transpose(1, 2, 0, 3, 4).reshape(2, G, CH, P)


def exchange(xs, modes, name):
    n = len(xs)
    modes = [modes] * n if isinstance(modes, (str, int)) else modes

    def piece(x_ref, mode, dev):
        if mode == "gather":
            return x_ref
        return x_ref.at[dev] if mode == "lead" else x_ref.at[:, pl.ds(dev * mode, mode)]

    def out_shape(x, mode):
        if mode == "gather":
            return (NDEV,) + tuple(x.shape)
        return tuple(x.shape) if mode == "lead" else (NDEV, x.shape[0], mode) + tuple(x.shape[2:])

    def body(*refs):
        x_refs, out_refs = refs[:n], refs[n:2 * n]
        send_sems, recv_sems, local_sems = refs[2 * n:]
        mx, my, mc = lax.axis_index("x"), lax.axis_index("y"), lax.axis_index("c")
        me = 4 * mx + 2 * my + mc
        copies = []
        for a, (x_ref, out_ref) in enumerate(zip(x_refs, out_refs)):
            local = pltpu.make_async_copy(piece(x_ref, modes[a], me), out_ref.at[me], local_sems.at[a])
            local.start()
            copies.append(local)
        sends, recvs = [], []
        for k in range(1, NDEV):
            peer = (1 - mx if k & 4 else mx, 1 - my if k & 2 else my, 1 - mc if k & 1 else mc)
            pid = 4 * peer[0] + 2 * peer[1] + peer[2]
            for a, (x_ref, out_ref) in enumerate(zip(x_refs, out_refs)):
                src = piece(x_ref, modes[a], pid)
                cp = pltpu.make_async_remote_copy(
                    src_ref=src, dst_ref=out_ref.at[me], send_sem=send_sems.at[k - 1, a], recv_sem=recv_sems.at[k - 1, a],
                    device_id=peer, device_id_type=MESH_T)
                cp.start()
                sends.append(cp)
                recvs.append(pltpu.make_async_remote_copy(
                    src_ref=src, dst_ref=out_ref.at[pid], send_sem=send_sems.at[k - 1, a], recv_sem=recv_sems.at[k - 1, a],
                    device_id=peer, device_id_type=MESH_T))
        for cp in recvs:
            cp.wait_recv()
        for cp in sends:
            cp.wait_send()
        for cp in copies:
            cp.wait()

    hbm = pl.BlockSpec(memory_space=pl.ANY)
    return pl.pallas_call(
        body, in_specs=[hbm] * n, out_specs=[hbm] * n,
        out_shape=[jax.ShapeDtypeStruct(out_shape(x, md), x.dtype) for x, md in zip(xs, modes)],
        scratch_shapes=[pltpu.SemaphoreType.DMA((NDEV - 1, n)), pltpu.SemaphoreType.DMA((NDEV - 1, n)),
                        pltpu.SemaphoreType.DMA((n,))],
        compiler_params=pltpu.CompilerParams(has_side_effects=True), name=name)(*xs)


def _dot_f32(a, b, dn):
    return lax.dot_general(a, b, dn, preferred_element_type=F32, precision=lax.Precision.HIGHEST)


def ada_fwd(cg, c_ctx, ada_w, ada_b_loc, name):
    W = ada_w.shape[2]

    def body(cg_ref, cc_ref, w_ref, b_ref, o_ref):
        a = jnp.concatenate([_silu(cg_ref[...]), jnp.broadcast_to(_silu(cc_ref[...]), (NDEV, D))], axis=0)
        for i in range(2):
            o_ref[i] = _dot_f32(a, w_ref[i], _DN["nn"]) + b_ref[i]

    return pl.pallas_call(body, out_shape=jax.ShapeDtypeStruct((2, 2 * NDEV, W), F32),
                          compiler_params=_cp(), name=name)(cg, c_ctx, ada_w, ada_b_loc)


def ada_bwd(cg, c_ctx, ada_w, dm_loc, dm_all, name):
    W = ada_w.shape[2]

    def body(cg_ref, cc_ref, w_ref, dl_ref, da_ref, gw_ref, dcc_ref, gb_ref):
        a = jnp.concatenate([_silu(cg_ref[...]), jnp.broadcast_to(_silu(cc_ref[...]), (NDEV, D))], axis=0)
        dcc = jnp.zeros((1, D), F32)
        for i in range(2):
            dl = dl_ref[i]
            gw_ref[i] = _dot_f32(a, dl, _DN["tn"])
            dctx = jnp.sum(dl[NDEV:], axis=0, keepdims=True)
            dcc = dcc + _dot_f32(dctx, w_ref[i], _DN["nt"])
        dcc_ref[...] = dcc
        gb_ref[...] = jnp.sum(da_ref[...], axis=0)

    return pl.pallas_call(body, out_shape=[jax.ShapeDtypeStruct((2, D, W), F32), jax.ShapeDtypeStruct((1, D), F32),
                                           jax.ShapeDtypeStruct((2, 3 * D), F32)],
                          compiler_params=_cp(), name=name)(cg, c_ctx, ada_w, dm_loc, dm_all)


def cctx_finish(parts, c_ctx, name):
    def body(p_ref, cc_ref, o_ref):
        o_ref[...] = jnp.sum(p_ref[...], axis=0, keepdims=True) * _dsilu(cc_ref[...])

    return pl.pallas_call(body, out_shape=jax.ShapeDtypeStruct((1, D), F32), name=name)(parts, c_ctx)


def _adamw_update(g_ref, w_ref, m_ref, v_ref, go_ref, d_ref, mo_ref, vo_ref):
    g = g_ref[0].astype(F32)
    for s in range(1, g_ref.shape[0]):
        g = g + g_ref[s].astype(F32)
    mn = B1 * m_ref[...] + (1.0 - B1) * g
    vn = B2 * v_ref[...] + (1.0 - B2) * g * g
    go_ref[...] = g
    mo_ref[...] = mn
    vo_ref[...] = vn
    d_ref[...] = -LR * ((mn * (1.0 / (1.0 - B1 ** STEP))) / (jnp.sqrt(vn * (1.0 / (1.0 - B2 ** STEP))) + AEPS) + WD * w_ref[...])


def adamw(gstack, w, m, v, name, tr=256):
    n, R, C = gstack.shape
    tr = max(t for t in range(8, min(tr, R) + 1, 8) if R % t == 0)
    spec = pl.BlockSpec((tr, C), lambda i: (i, 0))
    return pl.pallas_call(_adamw_body(1), grid=(R // tr,),
                          in_specs=[pl.BlockSpec((n, tr, C), lambda i: (0, i, 0)), spec, spec, spec],
                          out_specs=[spec] * 4, out_shape=[jax.ShapeDtypeStruct((R, C), F32)] * 4,
                          compiler_params=_cp(("parallel",)), name=name)(gstack, w, m, v)


def _adamw_body(k):
    def body(*refs):
        for t in range(k):
            _adamw_update(*refs[4 * t:4 * t + 4], *refs[4 * k + 4 * t:4 * k + 4 * t + 4])
    return body


def adamw_multi(items, grid, name):
    k = len(items)
    ins, in_specs, out_specs, out_shape = [], [], [], []
    for g, g_spec, w, m, v, w_spec in items:
        ins += [g, w, m, v]
        in_specs += [g_spec, w_spec, w_spec, w_spec]
    for g, g_spec, w, m, v, w_spec in items:
        out_specs += [w_spec] * 4
        out_shape += [jax.ShapeDtypeStruct(w.shape, F32)] * 4
    res = pl.pallas_call(_adamw_body(k), grid=grid, in_specs=in_specs, out_specs=out_specs, out_shape=out_shape,
                         compiler_params=_cp(("arbitrary",) * len(grid)), name=name)(*ins)
    return [res[4 * t:4 * t + 4] for t in range(k)]


def _whole(a, grid_rank):
    zeros = (0,) * a.ndim
    return pl.BlockSpec(a.shape, lambda *idx: zeros)


def sum_slots(xs, name):
    def body(*refs):
        for x_ref, o_ref in zip(refs[:len(xs)], refs[len(xs):]):
            acc = x_ref[0]
            for s in range(1, NDEV):
                acc = acc + x_ref[s]
            o_ref[...] = acc

    return pl.pallas_call(body, out_shape=[jax.ShapeDtypeStruct(x.shape[1:], F32) for x in xs],
                          compiler_params=_cp(), name=name)(*xs)


def _col_shards(g):
    R, N = g.shape
    return g.reshape(R, NDEV, N // NDEV).transpose(1, 0, 2)


def _vec2(v):
    return jnp.broadcast_to(v.reshape(1, 1, -1), (2, 1, v.size))


SHARD_ROWS = {"mla_w_in": 192, "mla_w_uq": 192, "mla_w_ukv": 256, "s5_w_in": 256}


def _t_shard(wsh, rows):
    t = wsh[0].T.astype(BF16)
    return jnp.pad(t, ((0, rows - t.shape[0]), (0, 0)))


def _win_order():
    w = IN_W // NDEV
    perm = np.zeros((IN_WP, NDEV * SHARD_ROWS["mla_w_in"]), np.float32)
    first = QL + KVL + ROPE
    for c in range(IN_W):
        n = c + HEADS * VD if c < first else c - first
        perm[n, (c // w) * SHARD_ROWS["mla_w_in"] + c % w] = 1.0
    return jnp.asarray(perm, BF16)


def local_step(xa, tgt, mod, Wt, small):
    T = xa.shape[0]
    sh = [mod[i, :, None, 0:D] for i in range(2)]
    sc = [mod[i, :, None, D:2 * D] for i in range(2)]
    gt = [mod[i, :, None, 2 * D:] for i in range(2)]
    ng = [_vec2(small["norm_g"][i]) for i in range(2)]
    qg, kvg = _vec2(small["mla_q_norm"]), _vec2(small["mla_kv_norm"])
    cosf, sinf, pm, pmt = _rope_tables(T)

    (h0,), _ = rowwise(st_norm_mod, [xa], [ng[0], sc[0], sh[0]], [(D, BF16)], [], "l0_norm")
    p0 = mm(h0, Wt["mla_w_in"], "nt", "l0_in")
    z0, cq, ckv = (p0, 0, HEADS * VD), (p0, HEADS * VD // QL, QL), (p0, (HEADS * VD + QL) // KVL, KVL)
    kr = p0[:, HEADS * VD + QL + KVL:HEADS * VD + QL + KVL + ROPE]
    (cqn,), _ = rowwise(st_rms, [cq], [qg], [(QL, BF16)], [], "l0_qnorm")
    (ckvn,), _ = rowwise(st_rms, [ckv], [kvg], [(KVL, BF16)], [], "l0_kvnorm")
    q = mm(cqn, Wt["mla_w_uq"], "nt", "l0_uq")
    kv = mm(ckvn, Wt["mla_w_ukv"], "nt", "l0_ukv")
    qh = q.reshape(T, HEADS, QK).transpose(1, 0, 2)
    kvh = kv.reshape(T, HEADS, NOPE + VD).transpose(1, 0, 2)
    kraw = jnp.concatenate([kvh[..., :NOPE], jnp.broadcast_to(kr[None], (HEADS, T, ROPE))], axis=-1)
    Q = rope(qh, cosf, sinf, pm, False, BF16, "l0_rope_q")
    K = rope(kraw, cosf, sinf, pm, False, BF16, "l0_rope_k")
    V = kvh[..., NOPE:].astype(BF16)
    o, lse = attn_fwd(Q, K, V, "l0_attn")
    o2 = o.transpose(1, 0, 2).reshape(T, HEADS * VD)
    (og,), _ = rowwise(st_gate, [o2, z0], [], [(D, BF16)], [], "l0_gate")
    out0 = mm(og, Wt["mla_w_out"], "nn", "l0_out")
    (x1,), _ = rowwise(st_resid, [xa, out0], [gt[0]], [(D, F32)], [], "l0_resid")

    ls = small["s5_log_step"].reshape(2, G, 1)
    a_re, a_im = small["s5_a_re"].reshape(2, G, P), small["s5_a_im"].reshape(2, G, P)
    b_re, b_im = small["s5_b_re"].reshape(2, G * P, CH), small["s5_b_im"].reshape(2, G * P, CH)
    lam_re, lam_im, f_re, f_im = disc_fwd(a_re, a_im, ls, "s5_disc")
    f_re2, f_im2 = f_re.reshape(2, G * P, 1), f_im.reshape(2, G * P, 1)
    bb_re, bb_im = disc_b(f_re2, f_im2, b_re, b_im, "s5_disc_b")
    bre = _block_diag(bb_re.reshape(2, G, P, CH).transpose(0, 1, 3, 2))
    bim = _block_diag(bb_im.reshape(2, G, P, CH).transpose(0, 1, 3, 2))
    cre = _block_diag(small["s5_c_re"].reshape(2, G, CH, P))
    cim = _block_diag(small["s5_c_im"].reshape(2, G, CH, P))
    lam_re4, lam_im4 = lam_re.reshape(2, NJ, 1, SB), lam_im.reshape(2, NJ, 1, SB)

    (h1,), _ = rowwise(st_norm_mod, [x1], [ng[1], sc[1], sh[1]], [(D, BF16)], [], "l1_norm")
    p1 = mm(h1, Wt["s5_w_in"], "nt", "l1_in")
    u, z1 = (p1, 0, D), (p1, 1, D)
    yssm = scan_fwd(p1, lam_re4, lam_im4, bre, bim, cre, cim, "s5_scan")
    dvec, bglu = _vec2(small["s5_d"]), _vec2(small["s5_b_glu"])
    (y, y1b), _ = rowwise(st_s5a, [yssm, u], [dvec], [(D, F32), (D, BF16)], [], "l1_gelu")
    gl = mm(y1b, Wt["s5_w_glu"], "nn", "l1_glu")
    (y3,), _ = rowwise(st_s5b, [y, gl, z1], [bglu], [(D, BF16)], [], "l1_gate")
    out1 = mm(y3, Wt["s5_w_out"], "nn", "l1_out")
    (x2,), _ = rowwise(st_resid, [x1, out1], [gt[1]], [(D, F32)], [], "l1_resid")

    fg = _vec2(small["final_g"])
    (dx2l,), (dfg, lvec) = rowwise(st_final, [x2, tgt], [fg], [(D, F32)], [D, 128], "final", lat_only=True)
    dx2 = jnp.concatenate([jnp.zeros((LC, D), F32), dx2l], axis=0)

    (dout1,), (dgt1,) = rowwise(st_resid_bwd, [dx2, out1], [gt[1]], [(D, BF16)], [D], "l1_resid_b")
    g_w_out5 = mm(y3, dout1, "tn", "l1_out_dw", out_dtype=BF16)
    dy3 = mm(dout1, Wt["s5_w_out"], "nt", "l1_out_dx")
    (dgl, dz1, dy1a), (dbglu,) = rowwise(st_s5b_bwd, [dy3, y, gl, z1], [bglu], [(D, BF16), (D, BF16), (D, F32)], [D], "l1_gate_b")
    g_w_glu = mm(y1b, dgl, "tn", "l1_glu_dw", out_dtype=BF16)
    dy1b = mm(dgl, Wt["s5_w_glu"], "nt", "l1_glu_dx")
    (dy, du_d), (dd,) = rowwise(st_s5a_bwd, [dy1a, dy1b, y, u], [dvec], [(D, F32), (D, F32)], [D], "l1_gelu_b")
    du_s, dlr, dli, dbre, dbim, dcre, dcim = scan_bwd(p1, dy, lam_re4, lam_im4, bre, bim, cre, cim, "s5_scan_b")
    du = du_d + du_s
    dbb_re = _diag_blocks(dbre).transpose(0, 1, 3, 2).reshape(2, G * P, CH)
    dbb_im = _diag_blocks(dbim).transpose(0, 1, 3, 2).reshape(2, G * P, CH)
    g_c_re, g_c_im = _diag_blocks(dcre), _diag_blocks(dcim)
    g_b_re, g_b_im, dfr, dfi = disc_b_bwd(f_re2, f_im2, b_re, b_im, dbb_re, dbb_im, "s5_disc_b_b")
    g_a_re, g_a_im, g_ls = disc_a_bwd(a_re, a_im, ls, dlr.reshape(2, G, P), dli.reshape(2, G, P),
                                      dfr.reshape(2, G, P), dfi.reshape(2, G, P), "s5_disc_b_a")
    dp1 = jnp.concatenate([du.astype(BF16), dz1], axis=1)
    g_w_in5 = mm(h1, dp1, "tn", "l1_in_dw", out_dtype=BF16, tn=2 * D // NDEV, shard_out=True)
    dh1 = mm(dp1, Wt["s5_w_in"], "nn", "l1_in_dx")
    (dx1,), (dsh1, dsc1, dng1) = rowwise(st_norm_mod_bwd, [x1, dh1, dx2], [ng[1], sc[1]], [(D, F32)], [D, D, D], "l1_norm_b")

    (dout0,), (dgt0,) = rowwise(st_resid_bwd, [dx1, out0], [gt[0]], [(D, BF16)], [D], "l0_resid_b")
    g_w_out = mm(og, dout0, "tn", "l0_out_dw", out_dtype=BF16)
    dog = mm(dout0, Wt["mla_w_out"], "nt", "l0_out_dx")
    (do2, dz0), _ = rowwise(st_gate_bwd, [dog, o2, z0], [], [(D, F32), (D, F32)], [], "l0_gate_b")
    doh = do2.reshape(T, HEADS, VD).transpose(1, 0, 2)
    dQ, dK, dV = attn_bwd(Q, K, V, o, lse, doh, "l0_attn_b")
    dqh = rope(dQ, cosf, sinf, pmt, True, F32, "l0_rope_q_b")
    dkraw, dksum = rope(dK, cosf, sinf, pmt, True, F32, "l0_rope_k_b", head_sum=True)
    dq = dqh.transpose(1, 0, 2).reshape(T, HEADS * QK).astype(BF16)
    dkv = jnp.concatenate([dkraw[..., :NOPE], dV], axis=-1).transpose(1, 0, 2).reshape(T, HEADS * (NOPE + VD)).astype(BF16)
    dkr = dksum[:, NOPE:]
    g_w_uq = _col_shards(mm(cqn, dq, "tn", "l0_uq_dw", out_dtype=BF16))
    dcqn = mm(dq, Wt["mla_w_uq"], "nn", "l0_uq_dx")
    g_w_ukv = mm(ckvn, dkv, "tn", "l0_ukv_dw", out_dtype=BF16, tm=KVL, tn=HEADS * (NOPE + VD) // NDEV, shard_out=True)
    dckvn = mm(dkv, Wt["mla_w_ukv"], "nn", "l0_ukv_dx")
    (dcq,), (dqg,) = rowwise(st_rms_bwd, [cq, dcqn], [qg], [(QL, F32)], [QL], "l0_qnorm_b")
    (dckv,), (dkvg,) = rowwise(st_rms_bwd, [ckv, dckvn], [kvg], [(KVL, F32)], [KVL], "l0_kvnorm_b")
    dp0 = jnp.concatenate([dz0, dcq, dckv, dkr, jnp.zeros((T, IN_WP - IN_W), F32)], axis=1).astype(BF16)
    g_p = mm(h0, dp0, "tn", "l0_in_dw", out_dtype=BF16)
    g_w_in = _col_shards(jnp.concatenate([g_p[:, HEADS * VD:IN_W], g_p[:, :HEADS * VD]], axis=1))
    dh0 = mm(dp0, Wt["mla_w_in"], "nn", "l0_in_dx")
    (dxa,), (dsh0, dsc0, dng0) = rowwise(st_norm_mod_bwd, [xa, dh0, dx1], [ng[0], sc[0]], [(D, F32)], [D, D, D], "l0_norm_b")

    dmod = jnp.stack([jnp.concatenate([dsh0, dsc0, dgt0], axis=-1)[:, 0], jnp.concatenate([dsh1, dsc1, dgt1], axis=-1)[:, 0]])
    both = lambda s: s[0, 0] + s[1, 0]
    rows8 = lambda g: g.reshape(NDEV, -1, g.shape[-1])
    gbig = {"mla_w_in": g_w_in, "mla_w_uq": g_w_uq, "mla_w_ukv": g_w_ukv, "mla_w_out": rows8(g_w_out),
            "s5_w_in": g_w_in5, "s5_w_glu": rows8(g_w_glu), "s5_w_out": rows8(g_w_out5),
            "s5_d": both(dd).reshape(NDEV, 1, -1), "s5_b_glu": both(dbglu).reshape(NDEV, 1, -1)}
    gsmall = {"norm_g": jnp.stack([both(dng0), both(dng1)]), "mla_q_norm": both(dqg), "mla_kv_norm": both(dkvg),
              "s5_a_re": g_a_re, "s5_a_im": g_a_im, "s5_log_step": g_ls, "s5_b_re": g_b_re, "s5_b_im": g_b_im,
              "s5_c_re": g_c_re, "s5_c_im": g_c_im, "final_g": dfg[1, 0]}
    return lvec[1], dxa, dmod, gbig, gsmall


COL_SHARDED = ("mla_w_in", "mla_w_uq", "mla_w_ukv", "s5_w_in")
ROW_SHARDED = ("mla_w_out", "s5_w_glu", "s5_w_out")
VEC_SHARDED = ("s5_d", "s5_b_glu")
BIG = COL_SHARDED + ROW_SHARDED
SHARDED = BIG + VEC_SHARDED
SMALL_RS = ("norm_g", "mla_q_norm", "mla_kv_norm", "s5_a_re", "s5_a_im", "s5_log_step", "s5_b_re", "s5_b_im",
            "s5_c_re", "s5_c_im", "final_g")
CHUNKED = ("s5_b_re", "s5_b_im", "s5_c_re", "s5_c_im")
TINY = ("norm_g", "mla_q_norm", "mla_kv_norm", "s5_a_re", "s5_a_im", "s5_log_step", "final_g")
ORDER = ("c_ctx", "ada_w", "ada_b", "norm_g", "mla_w_in", "mla_q_norm", "mla_w_uq", "mla_kv_norm", "mla_w_ukv",
         "mla_w_out", "s5_w_in", "s5_a_re", "s5_a_im", "s5_log_step", "s5_b_re", "s5_b_im", "s5_c_re", "s5_c_im",
         "s5_d", "s5_w_glu", "s5_b_glu", "s5_w_out", "final_g")


def kernel(x, c, ctx, c_ctx, ada_w, ada_b, norm_g, mla_w_in, mla_q_norm, mla_w_uq, mla_kv_norm, mla_w_ukv, mla_w_out, s5_w_in, s5_a_re, s5_a_im, s5_log_step, s5_b_re, s5_b_im, s5_c_re, s5_c_im, s5_d, s5_w_glu, s5_b_glu, s5_w_out, final_g, loss_target, m_c_ctx, m_ada_w, m_ada_b, m_norm_g, m_mla_w_in, m_mla_q_norm, m_mla_w_uq, m_mla_kv_norm, m_mla_w_ukv, m_mla_w_out, m_s5_w_in, m_s5_a_re, m_s5_a_im, m_s5_log_step, m_s5_b_re, m_s5_b_im, m_s5_c_re, m_s5_c_im, m_s5_d, m_s5_w_glu, m_s5_b_glu, m_s5_w_out, m_final_g, v_c_ctx, v_ada_w, v_ada_b, v_norm_g, v_mla_w_in, v_mla_q_norm, v_mla_w_uq, v_mla_kv_norm, v_mla_w_ukv, v_mla_w_out, v_s5_w_in, v_s5_a_re, v_s5_a_im, v_s5_log_step, v_s5_b_re, v_s5_b_im, v_s5_c_re, v_s5_c_im, v_s5_d, v_s5_w_glu, v_s5_b_glu, v_s5_w_out, v_final_g):
    w = dict(c_ctx=c_ctx, ada_w=ada_w, ada_b=ada_b, norm_g=norm_g, mla_w_in=mla_w_in, mla_q_norm=mla_q_norm,
             mla_w_uq=mla_w_uq, mla_kv_norm=mla_kv_norm, mla_w_ukv=mla_w_ukv, mla_w_out=mla_w_out, s5_w_in=s5_w_in,
             s5_a_re=s5_a_re, s5_a_im=s5_a_im, s5_log_step=s5_log_step, s5_b_re=s5_b_re, s5_b_im=s5_b_im,
             s5_c_re=s5_c_re, s5_c_im=s5_c_im, s5_d=s5_d, s5_w_glu=s5_w_glu, s5_b_glu=s5_b_glu, s5_w_out=s5_w_out,
             final_g=final_g)
    m = dict(c_ctx=m_c_ctx, ada_w=m_ada_w, ada_b=m_ada_b, norm_g=m_norm_g, mla_w_in=m_mla_w_in, mla_q_norm=m_mla_q_norm,
             mla_w_uq=m_mla_w_uq, mla_kv_norm=m_mla_kv_norm, mla_w_ukv=m_mla_w_ukv, mla_w_out=m_mla_w_out,
             s5_w_in=m_s5_w_in, s5_a_re=m_s5_a_re, s5_a_im=m_s5_a_im, s5_log_step=m_s5_log_step, s5_b_re=m_s5_b_re,
             s5_b_im=m_s5_b_im, s5_c_re=m_s5_c_re, s5_c_im=m_s5_c_im, s5_d=m_s5_d, s5_w_glu=m_s5_w_glu,
             s5_b_glu=m_s5_b_glu, s5_w_out=m_s5_w_out, final_g=m_final_g)
    v = dict(c_ctx=v_c_ctx, ada_w=v_ada_w, ada_b=v_ada_b, norm_g=v_norm_g, mla_w_in=v_mla_w_in, mla_q_norm=v_mla_q_norm,
             mla_w_uq=v_mla_w_uq, mla_kv_norm=v_mla_kv_norm, mla_w_ukv=v_mla_w_ukv, mla_w_out=v_mla_w_out,
             s5_w_in=v_s5_w_in, s5_a_re=v_s5_a_re, s5_a_im=v_s5_a_im, s5_log_step=v_s5_log_step, s5_b_re=v_s5_b_re,
             s5_b_im=v_s5_b_im, s5_c_re=v_s5_c_re, s5_c_im=v_s5_c_im, s5_d=v_s5_d, s5_w_glu=v_s5_w_glu,
             s5_b_glu=v_s5_b_glu, s5_w_out=v_s5_w_out, final_g=v_final_g)

    me = 4 * lax.axis_index("x") + 2 * lax.axis_index("y") + lax.axis_index("c")
    WA = ada_w.shape[2]

    cg = exchange([c], "gather", "gather_c")[0].reshape(NDEV, D)
    cc2 = c_ctx.reshape(1, D)
    ada_b_loc = lax.dynamic_slice_in_dim(ada_b.reshape(2, 3 * D // WA, WA), me, 1, axis=1)
    part = ada_fwd(cg, cc2, ada_w, ada_b_loc, "ada_fwd")
    pg = exchange([part], "gather", "gather_mod")[0]
    mod_l = lax.dynamic_index_in_dim(pg, me, axis=2, keepdims=False).transpose(1, 0, 2).reshape(2, 3 * D)
    mod_c = pg[:, :, NDEV, :].transpose(1, 0, 2).reshape(2, 3 * D)
    mod = jnp.stack([mod_c, mod_l], axis=1)

    vec_bits = lax.bitcast_convert_type(jnp.concatenate([s5_d, s5_b_glu], axis=0), BF16).reshape(2, -1)
    wsend = [_t_shard(w[n], SHARD_ROWS[n]) for n in COL_SHARDED] + [w[n][0].astype(BF16) for n in ROW_SHARDED] + [vec_bits]
    wgot = exchange(wsend, "gather", "gather_w")
    Wt = {n: a.reshape(-1, a.shape[-1]) for n, a in zip(BIG, wgot)}
    Wt["mla_w_in"] = mm(_win_order(), Wt["mla_w_in"], "nn", "w_in_order", out_dtype=BF16)
    vecs = lax.bitcast_convert_type(wgot[-1].reshape(NDEV, 2, -1, 2), F32)

    small = {n: w[n] for n in SMALL_RS}
    small["s5_d"] = vecs[:, 0, :].reshape(D)
    small["s5_b_glu"] = vecs[:, 1, :].reshape(D)

    xa = jnp.concatenate([ctx[0], x[0]], axis=0)
    lvec, dxa, dmod, gbig, gsmall = local_step(xa, loss_target[0], mod, Wt, small)
    loss = lax.psum(lvec[0, 0], ("x", "y", "c"))
    grad_x = dxa[LC:][None]

    per_dev = G // NDEV
    chunk_send = [gsmall[n].reshape((2, G) + w[n].shape[3:]) for n in CHUNKED]
    recv = exchange([gbig[n] for n in SHARDED] + chunk_send, ["lead"] * len(SHARDED) + [per_dev] * len(CHUNKED),
                    "scatter_grads")
    out = {}

    def keep(n, res):
        for key, arr in zip("gdmv", res):
            out[key, n] = arr.reshape(w[n].shape)

    for n, g8 in zip(BIG, recv):
        keep(n, adamw(g8, w[n][0], m[n][0], v[n][0], "adamw_" + n))
    reduced = sum_slots(recv[len(SHARDED):], "sum_chunks")

    kshape = lambda n: w[n].shape if w[n].ndim > 1 else (1, w[n].size)
    got = exchange(list(reduced) + [gsmall[n].reshape(kshape(n)) for n in TINY] + [dmod], "gather", "gather_small")
    chunk_all, tiny_all, dm_all = got[:len(CHUNKED)], got[len(CHUNKED):-1], got[-1]

    dm_cols = lax.dynamic_slice_in_dim(dm_all.reshape(NDEV, 2, 2, 3 * D // WA, WA), me, 1, axis=3)[:, :, :, 0]
    dm_loc = jnp.concatenate([dm_cols[:, :, 1].transpose(1, 0, 2), dm_cols[:, :, 0].transpose(1, 0, 2)], axis=1)
    g_ada_w, dcc_part, g_ada_b = ada_bwd(cg, cc2, ada_w, dm_loc, dm_all.transpose(0, 2, 1, 3).reshape(2 * NDEV, 2, 3 * D), "ada_bwd")
    dcc_all = exchange([dcc_part], "gather", "gather_dcc")[0].reshape(NDEV, D)
    g_c_ctx = cctx_finish(dcc_all, cc2, "cctx_finish")

    flat2 = lambda t: t.reshape(-1, t.shape[-1])
    keep("ada_w", adamw(flat2(g_ada_w)[None], flat2(ada_w), flat2(m_ada_w), flat2(v_ada_w), "adamw_ada"))
    items = []
    for n, g in zip(CHUNKED, chunk_all):
        blk = (1, 1, per_dev) + w[n].shape[3:]
        g_spec = pl.BlockSpec((1, 1, 1) + blk[2:], lambda d, s: (0, s, d, 0, 0, 0))
        items.append((g[None], g_spec, w[n], m[n], v[n], pl.BlockSpec(blk, lambda d, s: (0, d, s, 0, 0))))
    for n, res in zip(CHUNKED, adamw_multi(items, (2, NDEV), "adamw_bc")):
        keep(n, res)
    tiny_g = dict(zip(TINY, tiny_all))
    tiny_g.update(dict(zip(VEC_SHARDED, recv[len(BIG):len(SHARDED)])))
    tiny_g["c_ctx"], tiny_g["ada_b"] = g_c_ctx[None], g_ada_b[None]
    names = list(tiny_g)
    items = [(tiny_g[n], _whole(tiny_g[n], 1)) + tuple(t[n].reshape(kshape(n)) for t in (w, m, v))
             + (pl.BlockSpec(kshape(n), lambda i, r=len(kshape(n)): (0,) * r),) for n in names]
    for n, res in zip(names, adamw_multi(items, (1,), "adamw_small")):
        keep(n, res)

    return (loss, grad_x, *[out["g", n] for n in ORDER], *[out["d", n] for n in ORDER],
            *[out["m", n] for n in ORDER], *[out["v", n] for n in ORDER])
```

```python
import math

import numpy as np
import jax
import jax.numpy as jnp
from jax import lax
from jax.experimental import pallas as pl
from jax.experimental.pallas import tpu as pltpu

F32 = jnp.float32
BF16 = jnp.bfloat16

D = 1024
L = 2048
LC = 256
NDEV = 8
GRID_W = 64
EPS = 1e-6
HEADS = 16
NOPE = 64
ROPE = 32
QK = NOPE + ROPE
VD = 64
IN_W = 256 + 128 + ROPE + HEADS * 64
IN_WP = 1536
QL = 256
KVL = 128
SCALE = QK ** -0.5
THETA = 10000.0
G = 64
P = 64
CH = 16
GB = 8
NJ = G // GB
UB = GB * CH
SB = GB * P
SEG = 8
TB = 256
VMEM_LIMIT = 56 * 1024 * 1024
B1, B2, LR, AEPS, WD, STEP = 0.9, 0.999, 0.001, 1e-8, 0.01, 10
MESH_T = pl.DeviceIdType.MESH


def _cp(sem=None):
    return pltpu.CompilerParams(dimension_semantics=sem, vmem_limit_bytes=VMEM_LIMIT)


def _sig(x):
    return 1.0 / (1.0 + jnp.exp(-x))


def _silu(x):
    return x * _sig(x)


def _dsilu(x):
    s = _sig(x)
    return s * (1.0 + x * (1.0 - s))


_GK = math.sqrt(2.0 / math.pi)


def _gelu(x):
    return 0.5 * x * (1.0 + jnp.tanh(_GK * (x + 0.044715 * x * x * x)))


def _dgelu(x):
    t = jnp.tanh(_GK * (x + 0.044715 * x * x * x))
    return 0.5 * (1.0 + t) + 0.5 * x * (1.0 - t * t) * _GK * (1.0 + 3 * 0.044715 * x * x)


def _rs(x):
    return lax.rsqrt(jnp.mean(x * x, axis=-1, keepdims=True) + EPS)


def _sum0(x):
    return jnp.sum(x, axis=0, keepdims=True)


def st_norm_mod(x, g, sc, sh):
    y = x * _rs(x) * g
    return (y * (1.0 + sc) + sh,), ()


def st_norm_mod_bwd(x, dh, dres, g, sc):
    r = _rs(x)
    xn = x * r
    y = xn * g
    dy = dh * (1.0 + sc)
    dxn = dy * g
    dx = r * (dxn - xn * jnp.mean(dxn * xn, axis=-1, keepdims=True))
    return (dres + dx,), (_sum0(dh), _sum0(dh * y), _sum0(dy * xn))


def st_rms(x, g):
    return (x * _rs(x) * g,), ()


def st_rms_bwd(x, dy, g):
    r = _rs(x)
    n = x * r
    dn = dy * g
    dx = r * (dn - n * jnp.mean(dn * n, axis=-1, keepdims=True))
    return (dx,), (_sum0(dy * n),)


def st_gate(o, z):
    return (o * _silu(z),), ()


def st_gate_bwd(dog, o, z):
    return (dog * _silu(z), dog * o * _dsilu(z)), ()


def st_resid(x, out, gt):
    return (x + gt * out,), ()


def st_resid_bwd(dx, out, gt):
    return (dx * gt,), (_sum0(dx * out),)


def st_s5a(yssm, u, d):
    y = yssm + d * u
    return (y, _gelu(y)), ()


def st_s5b(y, gl, z, b):
    return (_gelu(y) * _sig(gl + b) * _silu(z),), ()


def st_s5b_bwd(dy3, y, gl, z, b):
    y1 = _gelu(y)
    s = _sig(gl + b)
    dy2 = dy3 * _silu(z)
    dz = dy3 * y1 * s * _dsilu(z)
    dgl = dy2 * y1 * s * (1.0 - s)
    return (dgl, dz, dy2 * s), (_sum0(dgl),)


def st_s5a_bwd(dy1a, dy1b, y, u, d):
    dy = (dy1a + dy1b) * _dgelu(y)
    return (dy, dy * d), (_sum0(dy * u),)


def st_final(x2, tgt, g, mask):
    r = _rs(x2)
    n = x2 * r
    e = n * g - tgt
    dyo = e * (1.0 / D)
    dn = dyo * g
    dx = r * (dn - n * jnp.mean(dn * n, axis=-1, keepdims=True))
    lsum = jnp.sum(_sum0(e * e), axis=1, keepdims=True) * (0.5 / D)
    return (dx * mask,), (_sum0(dyo * n), jnp.broadcast_to(lsum, (1, 128)))


def rowwise(fn, rows, vecs, out_rows, out_sums, name):
    lat_blk = lambda i: jnp.maximum(i - 1, 0)
    arrays, in_specs, pick = [], [], []
    for a in rows:
        if not isinstance(a, tuple):
            a = (a, 0, a.shape[1])
        tag = a[0] if isinstance(a[0], str) else None
        if tag == "cat":
            _, ctx, x = a
            arrays += [ctx, x]
            in_specs += [pl.BlockSpec((TB, ctx.shape[1]), lambda i: (0, 0)),
                         pl.BlockSpec((TB, x.shape[1]), lambda i: (lat_blk(i), 0))]
            pick.append(2)
        elif tag == "lat":
            arrays.append(a[1])
            in_specs.append(pl.BlockSpec((TB, a[1].shape[1]), lambda i: (lat_blk(i), 0)))
            pick.append(1)
        else:
            arr, cb, width = a
            arrays.append(arr)
            in_specs.append(pl.BlockSpec((TB, width), lambda i, cb=cb: (i, cb)))
            pick.append(1)
    T = LC + L
    nin, nv, no = len(arrays), len(vecs), len(out_rows)

    def body(*refs):
        i = pl.program_id(0)
        vals, k = [], 0
        for p in pick:
            if p == 2:
                vals.append(jnp.where(i == 0, refs[k][...], refs[k + 1][...]))
            else:
                vals.append(refs[k][...])
            k += p
        vals += [r[0] for r in refs[nin:nin + nv]]
        outs, sums = fn(*vals)
        for r, o in zip(refs[nin + nv:nin + nv + no], outs):
            r[...] = o.astype(r.dtype)
        sum_refs = refs[nin + nv + no:]
        if sum_refs:
            @pl.when(i <= 1)
            def _():
                for r in sum_refs:
                    r[...] = jnp.zeros_like(r)
            for r, s in zip(sum_refs, sums):
                r[0] += s

    kind = lambda i: (jnp.minimum(i, 1), 0, 0)
    in_specs += [pl.BlockSpec((1, 1, v.shape[2]), kind) for v in vecs]
    out_specs, out_shape = [], []
    for o in out_rows:
        lat = len(o) == 3
        out_specs.append(pl.BlockSpec((TB, o[0]), (lambda i: (lat_blk(i), 0)) if lat else (lambda i: (i, 0))))
        out_shape.append(jax.ShapeDtypeStruct((L if lat else T, o[0]), o[1]))
    out_specs += [pl.BlockSpec((1, 1, c), kind) for c in out_sums]
    out_shape += [jax.ShapeDtypeStruct((2, 1, c), F32) for c in out_sums]
    res = pl.pallas_call(body, grid=(T // TB,), in_specs=in_specs, out_specs=out_specs, out_shape=out_shape,
                         compiler_params=_cp(("arbitrary",)), name=name)(*arrays, *vecs)
    return res[:no], res[no:]


_DN = {"nn": (((1,), (0,)), ((), ())), "nt": (((1,), (1,)), ((), ())), "tn": (((0,), (0,)), ((), ()))}


def mm(a, b, mode, name, out_dtype=F32, tm=256, tn=None, shard_out=False):
    if mode == "nn":
        (M, K), (_, N) = a.shape, b.shape
    elif mode == "nt":
        (M, K), (N, _) = a.shape, b.shape
    else:
        (K, M), (_, N) = a.shape, b.shape
    tm = min(tm, M)
    tn = N if tn is None else tn
    dn = _DN[mode]

    def body(a_ref, b_ref, o_ref):
        o_ref[...] = lax.dot_general(a_ref[...].astype(BF16), b_ref[...].astype(BF16), dn,
                                     preferred_element_type=F32).astype(o_ref.dtype)

    a_spec = pl.BlockSpec((K, tm), lambda i, j: (0, i)) if mode == "tn" else pl.BlockSpec((tm, K), lambda i, j: (i, 0))
    b_spec = pl.BlockSpec((tn, K), lambda i, j: (j, 0)) if mode == "nt" else pl.BlockSpec((K, tn), lambda i, j: (0, j))
    if shard_out:
        def body(a_ref, b_ref, o_ref):
            o_ref[0] = lax.dot_general(a_ref[...].astype(BF16), b_ref[...].astype(BF16), dn,
                                       preferred_element_type=F32).astype(o_ref.dtype)
        out_spec = pl.BlockSpec((1, tm, tn), lambda i, j: (j, i, 0))
        out_shape = jax.ShapeDtypeStruct((N // tn, M, tn), out_dtype)
    else:
        out_spec = pl.BlockSpec((tm, tn), lambda i, j: (i, j))
        out_shape = jax.ShapeDtypeStruct((M, N), out_dtype)
    return pl.pallas_call(body, grid=(M // tm, N // tn), in_specs=[a_spec, b_spec], out_specs=out_spec, out_shape=out_shape,
                          compiler_params=_cp(("parallel", "arbitrary")), name=name)(a, b)


def _rope_tables(T):
    nlat = T - LC
    pos = np.arange(nlat)
    row, col = pos // GRID_W, pos % GRID_W
    half = ROPE // 2
    inv = 1.0 / (THETA ** (np.arange(0, half, 2, dtype=np.float64) / half))
    cosf = np.ones((T, QK), np.float64)
    sinf = np.zeros((T, QK), np.float64)
    perm = np.zeros((QK, QK), np.float32)
    for m in range(ROPE):
        j = NOPE + m
        blk, w = m // half, m % half
        ang = (row if blk == 0 else col)[:, None] * inv[None, :]
        f = w % (half // 2)
        cosf[LC:, j] = np.cos(ang[:, f])
        if w < half // 2:
            sinf[LC:, j] = -np.sin(ang[:, f])
            perm[j + half // 2, j] = 1.0
        else:
            sinf[LC:, j] = np.sin(ang[:, f])
            perm[j - half // 2, j] = 1.0
    return jnp.asarray(cosf, F32), jnp.asarray(sinf, F32), jnp.asarray(perm, BF16), jnp.asarray(perm.T, BF16)


def _exact_perm(x, pm):
    hi = x.astype(BF16)
    r1 = x - hi.astype(F32)
    mid = r1.astype(BF16)
    lo = (r1 - mid.astype(F32)).astype(BF16)
    dot = lambda a: jnp.dot(a, pm, preferred_element_type=F32)
    return dot(hi) + dot(mid) + dot(lo)


def rope(x, cosf, sinf, pm, inverse, out_dtype, name, head_sum=False):
    H, T, _ = x.shape

    def body(x_ref, c_ref, s_ref, p_ref, o_ref, *rest):
        cv, sv, pv = c_ref[...], s_ref[...], p_ref[...]
        total = None
        for h in range(H):
            xv = x_ref[h]
            if inverse:
                out = xv * cv + _exact_perm(xv * sv, pv)
            else:
                out = xv * cv + _exact_perm(xv, pv) * sv
            o_ref[h] = out.astype(o_ref.dtype)
            if head_sum:
                total = out if total is None else total + out
        if head_sum:
            rest[0][...] = total

    out_shape = [jax.ShapeDtypeStruct((H, T, QK), out_dtype)]
    out_specs = [pl.BlockSpec((H, TB, QK), lambda i: (0, i, 0))]
    if head_sum:
        out_shape.append(jax.ShapeDtypeStruct((T, QK), F32))
        out_specs.append(pl.BlockSpec((TB, QK), lambda i: (i, 0)))
    res = pl.pallas_call(
        body, grid=(T // TB,),
        in_specs=[pl.BlockSpec((H, TB, QK), lambda i: (0, i, 0)), pl.BlockSpec((TB, QK), lambda i: (i, 0)),
                  pl.BlockSpec((TB, QK), lambda i: (i, 0)), pl.BlockSpec((QK, QK), lambda i: (0, 0))],
        out_specs=out_specs, out_shape=out_shape, compiler_params=_cp(("parallel",)), name=name)(x, cosf, sinf, pm)
    return res if head_sum else res[0]


def _scores(q, k, qi):
    s = lax.dot_general(q, k, _DN["nt"], preferred_element_type=F32) * SCALE
    col = lax.broadcasted_iota(jnp.int32, s.shape, 1)
    return jnp.where(jnp.logical_and(qi == 0, col >= LC), -1e30, s)


def _with_rider(body, nin, nout, ride, grid):
    if ride is None:
        return body
    n = ride.n

    def wrapped(*refs):
        ins, xs = refs[:nin], refs[nin:nin + n]
        outs, got = refs[nin + n:nin + n + nout], refs[nin + n + nout:nin + 2 * n + nout]
        sems = refs[nin + 2 * n + nout:]
        step = pl.program_id(0) * grid[1] + pl.program_id(1)

        @pl.when(step == 0)
        def _():
            ride.start(xs, got, sems)

        body(*ins, *outs)

        @pl.when(step == grid[0] * grid[1] - 1)
        def _():
            ride.finish(xs, got, sems)

    return wrapped


def _ride_call(body, grid, in_specs, out_specs, out_shape, ride, rode, name, args):
    if ride is None:
        return pl.pallas_call(body, grid=grid, in_specs=in_specs, out_specs=out_specs, out_shape=out_shape,
                              compiler_params=_cp(("parallel", "arbitrary")), name=name)(*args), []
    res = pl.pallas_call(
        _with_rider(body, len(in_specs), len(out_specs), ride, grid), grid=grid,
        in_specs=in_specs + ride.specs, out_specs=out_specs + ride.specs, out_shape=out_shape + ride.out_shape,
        scratch_shapes=ride.scratch,
        compiler_params=pltpu.CompilerParams(dimension_semantics=("arbitrary", "arbitrary"), vmem_limit_bytes=VMEM_LIMIT,
                                             has_side_effects=True), name=name)(*args, *rode)
    return res[:len(out_specs)], res[len(out_specs):]


def attn_fwd(q, k, v, name, rode=None, modes=None):
    H, T, _ = q.shape

    def body(q_ref, k_ref, v_ref, o_ref, lse_ref):
        s = _scores(q_ref[0], k_ref[0], pl.program_id(1))
        m = jnp.max(s, axis=1, keepdims=True)
        p = jnp.exp(s - m)
        l = jnp.sum(p, axis=1, keepdims=True)
        o = jnp.dot(p.astype(BF16), v_ref[0], preferred_element_type=F32)
        o_ref[0] = o / l
        lse_ref[0] = m + jnp.log(l)

    return _ride_call(
        body, (H, T // TB),
        [pl.BlockSpec((1, TB, QK), lambda h, i: (h, i, 0)), pl.BlockSpec((1, T, QK), lambda h, i: (h, 0, 0)),
         pl.BlockSpec((1, T, VD), lambda h, i: (h, 0, 0))],
        [pl.BlockSpec((1, TB, VD), lambda h, i: (h, i, 0)), pl.BlockSpec((1, TB, 1), lambda h, i: (h, i, 0))],
        [jax.ShapeDtypeStruct((H, T, VD), F32), jax.ShapeDtypeStruct((H, T, 1), F32)],
        Exchange(rode, modes) if rode else None, rode, name, (q, k, v))


def attn_bwd(q, k, v, o, lse, do, name, rode=None, modes=None):
    H, T, _ = q.shape

    def body(q_ref, k_ref, v_ref, o_ref, lse_ref, do_ref, dq_ref, dk_ref, dv_ref):
        i = pl.program_id(1)

        @pl.when(i == 0)
        def _():
            dk_ref[...] = jnp.zeros_like(dk_ref)
            dv_ref[...] = jnp.zeros_like(dv_ref)

        qv, kv, dov = q_ref[0], k_ref[0], do_ref[0]
        p = jnp.exp(_scores(qv, kv, i) - lse_ref[0])
        delta = jnp.sum(dov * o_ref[0], axis=1, keepdims=True)
        dob = dov.astype(BF16)
        dv_ref[0] += lax.dot_general(p.astype(BF16), dob, _DN["tn"], preferred_element_type=F32)
        dp = lax.dot_general(dob, v_ref[0], _DN["nt"], preferred_element_type=F32)
        ds = (p * (dp - delta) * SCALE).astype(BF16)
        dq_ref[0] = jnp.dot(ds, kv, preferred_element_type=F32)
        dk_ref[0] += lax.dot_general(ds, qv, _DN["tn"], preferred_element_type=F32)

    blk = lambda c: pl.BlockSpec((1, TB, c), lambda h, i: (h, i, 0))
    full = lambda c: pl.BlockSpec((1, T, c), lambda h, i: (h, 0, 0))
    return _ride_call(
        body, (H, T // TB), [blk(QK), full(QK), full(VD), blk(VD), blk(1), blk(VD)], [blk(QK), full(QK), full(VD)],
        [jax.ShapeDtypeStruct((H, T, QK), F32), jax.ShapeDtypeStruct((H, T, QK), F32), jax.ShapeDtypeStruct((H, T, VD), F32)],
        Exchange(rode, modes) if rode else None, rode, name, (q, k, v, o, lse, do))


def disc_fwd(a_re, a_im, ls, name):
    def body(ar_ref, ai_ref, ls_ref, lr_ref, li_ref, fr_ref, fi_ref):
        ar, ai = ar_ref[...], ai_ref[...]
        dt = jnp.exp(ls_ref[...])
        mag = jnp.exp(ar * dt)
        lr = mag * jnp.cos(ai * dt)
        li = mag * jnp.sin(ai * dt)
        den = ar * ar + ai * ai
        nr = lr - 1.0
        lr_ref[...] = lr
        li_ref[...] = li
        fr_ref[...] = (nr * ar + li * ai) / den
        fi_ref[...] = (li * ar - nr * ai) / den

    return pl.pallas_call(body, out_shape=[jax.ShapeDtypeStruct(a_re.shape, F32)] * 4, name=name)(a_re, a_im, ls)


def disc_b(f_re, f_im, b_re, b_im, name):
    def body(fr_ref, fi_ref, br_ref, bi_ref, or_ref, oi_ref):
        fr, fi, br, bi = fr_ref[...], fi_ref[...], br_ref[...], bi_ref[...]
        or_ref[...] = fr * br - fi * bi
        oi_ref[...] = fr * bi + fi * br

    fs, bs = _disc_b_specs()
    return pl.pallas_call(body, grid=(2, G * P // DISC_ROWS), in_specs=[fs, fs, bs, bs], out_specs=[bs, bs],
                          out_shape=[jax.ShapeDtypeStruct(b_re.shape, F32)] * 2, name=name)(f_re, f_im, b_re, b_im)


DISC_ROWS = 1024


def _disc_b_specs():
    return (pl.BlockSpec((1, DISC_ROWS, 1), lambda d, i: (d, i, 0)), pl.BlockSpec((1, DISC_ROWS, CH), lambda d, i: (d, i, 0)))


def disc_b_bwd(f_re, f_im, b_re, b_im, dbb_re, dbb_im, name):
    def body(fr_ref, fi_ref, br_ref, bi_ref, dr_ref, di_ref, dbr_ref, dbi_ref, dfr_ref, dfi_ref):
        fr, fi, br, bi, dr, di = fr_ref[...], fi_ref[...], br_ref[...], bi_ref[...], dr_ref[...], di_ref[...]
        dbr_ref[...] = fr * dr + fi * di
        dbi_ref[...] = fr * di - fi * dr
        dfr_ref[...] = jnp.sum(dr * br + di * bi, axis=-1, keepdims=True)
        dfi_ref[...] = jnp.sum(di * br - dr * bi, axis=-1, keepdims=True)

    fs, bs = _disc_b_specs()
    return pl.pallas_call(body, grid=(2, G * P // DISC_ROWS), in_specs=[fs, fs, bs, bs, bs, bs], out_specs=[bs, bs, fs, fs],
                          out_shape=[jax.ShapeDtypeStruct(b_re.shape, F32)] * 2 + [jax.ShapeDtypeStruct(f_re.shape, F32)] * 2,
                          name=name)(f_re, f_im, b_re, b_im, dbb_re, dbb_im)


def disc_a_bwd(a_re, a_im, ls, dlr, dli, dfr, dfi, name):
    def body(ar_ref, ai_ref, ls_ref, dlr_ref, dli_ref, dfr_ref, dfi_ref, dar_ref, dai_ref, dls_ref):
        ar, ai = ar_ref[...], ai_ref[...]
        dt = jnp.exp(ls_ref[...])
        mag = jnp.exp(ar * dt)
        cs, sn = jnp.cos(ai * dt), jnp.sin(ai * dt)
        lr, li = mag * cs, mag * sn
        den = ar * ar + ai * ai
        nr = lr - 1.0
        f_re = (nr * ar + li * ai) / den
        f_im = (li * ar - nr * ai) / den
        dn1 = dfr_ref[...] / den
        dn2 = dfi_ref[...] / den
        dden = -(dfr_ref[...] * f_re + dfi_ref[...] * f_im) / den
        dlr_t = dlr_ref[...] + dn1 * ar - dn2 * ai
        dli_t = dli_ref[...] + dn1 * ai + dn2 * ar
        dar = dn1 * nr + dn2 * li + dden * 2.0 * ar
        dai = dn1 * li - dn2 * nr + dden * 2.0 * ai
        dmag = dlr_t * cs + dli_t * sn
        dth = dli_t * lr - dlr_t * li
        dar_ref[...] = dar + dmag * mag * dt
        dai_ref[...] = dai + dth * dt
        dls_ref[...] = jnp.sum(dmag * mag * ar + dth * ai, axis=-1, keepdims=True) * dt

    return pl.pallas_call(body, out_shape=[jax.ShapeDtypeStruct(a_re.shape, F32)] * 2 +
                          [jax.ShapeDtypeStruct(ls.shape, F32)], name=name)(a_re, a_im, ls, dlr, dli, dfr, dfi)


def _cpow(lr, li, n):
    rr, ri = None, None
    br, bi = lr, li
    while n:
        if n & 1:
            if rr is None:
                rr, ri = br, bi
            else:
                rr, ri = rr * br - ri * bi, rr * bi + ri * br
        n >>= 1
        if n:
            br, bi = br * br - bi * bi, 2.0 * br * bi
    return rr, ri


UNROLL = 4


def _seg_scan(xre, xim, lam8, pw, base, seglen, rev, init, fin_re, fin_im, ini_re, ini_im, prev=None):
    lr, li = lam8

    def rows(t):
        return pl.ds(pl.multiple_of(base + t * SEG, SEG), SEG)

    tmap = (lambda n: seglen - 1 - n) if rev else (lambda n: n)
    zero = jnp.zeros((SEG, SB), F32)

    def advance(c, t):
        a, b = c
        return lr * a - li * b + xre[rows(t), :], lr * b + li * a + xim[rows(t), :]

    fin = lax.fori_loop(0, seglen, lambda n, c: advance(c, tmap(n)), (zero, zero), unroll=UNROLL)
    fin_re[...] = fin[0]
    fin_im[...] = fin[1]
    (cr, ci), (pr, pi) = init, pw
    for i in (range(SEG - 1, -1, -1) if rev else range(SEG)):
        ini_re[pl.ds(i, 1), :] = cr
        ini_im[pl.ds(i, 1), :] = ci
        cr, ci = pr * cr - pi * ci + fin_re[pl.ds(i, 1), :], pr * ci + pi * cr + fin_im[pl.ds(i, 1), :]
    start = (ini_re[...], ini_im[...])

    def store(c, t):
        na, nb = advance(c, t)
        xre[rows(t), :] = na
        xim[rows(t), :] = nb
        return na, nb

    if prev is None:
        lax.fori_loop(0, seglen, lambda n, c: store(c, tmap(n)), start, unroll=UNROLL)
        return (cr, ci), None

    sre, sim, s_ini_re, s_ini_im = prev

    def acc_step(c, t, pre, pim):
        na, nb = store(c[:2], t)
        return na, nb, c[2] + na * pre + nb * pim, c[3] + nb * pre - na * pim

    def body(n, c):
        t = tmap(n)
        tp = t - 1 if rev else t + 1
        return acc_step(c, t, sre[rows(tp), :], sim[rows(tp), :])

    c = lax.fori_loop(0, seglen - 1, body, start + (zero, zero), unroll=UNROLL)
    c = acc_step(c, 0 if rev else seglen - 1, s_ini_re[...], s_ini_im[...])
    return (cr, ci), c[2:]


def _lam_tiles(lr, li, lens, conj=False):
    if conj:
        li = -li
    lam8 = (jnp.broadcast_to(lr, (SEG, SB)), jnp.broadcast_to(li, (SEG, SB)))
    return lam8, [_cpow(lr, li, n) for n in lens]


def _stretches(T):
    return ((0, LC // SEG), (LC, (T - LC) // SEG))


def _to_seg_order(src, dst, T):
    for base, seglen in _stretches(T):
        def body(t, carry, base=base, seglen=seglen):
            dst[pl.ds(pl.multiple_of(base + t * SEG, SEG), SEG), :] = src[pl.ds(base + t, SEG, stride=seglen), :]
            return carry
        lax.fori_loop(0, seglen, body, 0, unroll=8)


def _from_seg_order(src, dst, T):
    for base, seglen in _stretches(T):
        def body(t, carry, base=base, seglen=seglen):
            dst[pl.ds(base + t, SEG, stride=seglen), :] = src[pl.ds(pl.multiple_of(base + t * SEG, SEG), SEG), :]
            return carry
        lax.fori_loop(0, seglen, body, 0, unroll=8)


def _scan_specs(T):
    ublk = pl.BlockSpec((T, UB), lambda j: (0, j))
    lam = pl.BlockSpec((2, 1, 1, SB), lambda j: (0, j, 0, 0))
    mat = pl.BlockSpec((2, 1, UB, SB), lambda j: (0, j, 0, 0))
    return ublk, lam, mat


def _dotf(a, b, mode="nn"):
    return lax.dot_general(a, b, _DN[mode], preferred_element_type=F32)


def _zero_state():
    return jnp.zeros((1, SB), F32), jnp.zeros((1, SB), F32)


def scan_fwd(u, lam_re, lam_im, bre, bim, cre, cim, name):
    T = u.shape[0]
    s_ctx, s_lat = LC // SEG, (T - LC) // SEG

    def body(u_ref, lr_ref, li_ref, bre_ref, bim_ref, cre_ref, cim_ref, y_ref, us, ys, sre, sim, fre, fim, ire, iim):
        _to_seg_order(u_ref, us, T)
        ub = us[...].astype(BF16)
        for d in range(2):
            lam8, (pw_c, pw_l) = _lam_tiles(lr_ref[d, 0], li_ref[d, 0], (s_ctx, s_lat))
            sre[...] = _dotf(ub, bre_ref[d, 0].astype(BF16))
            sim[...] = _dotf(ub, bim_ref[d, 0].astype(BF16))
            end_c, _ = _seg_scan(sre, sim, lam8, pw_c, 0, s_ctx, bool(d), _zero_state(), fre, fim, ire, iim)
            _seg_scan(sre, sim, lam8, pw_l, LC, s_lat, bool(d), end_c, fre, fim, ire, iim)
            y = (_dotf(sre[...].astype(BF16), cre_ref[d, 0].astype(BF16), "nt")
                 - _dotf(sim[...].astype(BF16), cim_ref[d, 0].astype(BF16), "nt"))
            if d == 0:
                ys[...] = y
            else:
                ys[...] += y
        _from_seg_order(ys, y_ref, T)

    ublk, lam, mat = _scan_specs(T)
    return pl.pallas_call(
        body, grid=(NJ,), in_specs=[ublk, lam, lam, mat, mat, mat, mat], out_specs=ublk,
        out_shape=jax.ShapeDtypeStruct((T, G * CH), F32),
        scratch_shapes=[pltpu.VMEM((T, UB), F32)] * 2 + [pltpu.VMEM((T, SB), F32)] * 2 + [pltpu.VMEM((SEG, SB), F32)] * 4,
        compiler_params=_cp(("arbitrary",)), name=name)(u, lam_re, lam_im, bre, bim, cre, cim)


def scan_bwd(u, dy, lam_re, lam_im, bre, bim, cre, cim, name):
    T = u.shape[0]
    s_ctx, s_lat = LC // SEG, (T - LC) // SEG

    def body(u_ref, dy_ref, lr_ref, li_ref, bre_ref, bim_ref, cre_ref, cim_ref,
             du_ref, dlr_ref, dli_ref, dbre_ref, dbim_ref, dcre_ref, dcim_ref,
             us, dys, dus, sre, sim, gre, gim, fre, fim, ic_re, ic_im, il_re, il_im, jre, jim):
        _to_seg_order(u_ref, us, T)
        _to_seg_order(dy_ref, dys, T)
        ub, dyb = us[...].astype(BF16), dys[...].astype(BF16)
        for d in range(2):
            rev = bool(d)
            lam8, (pw_c, pw_l) = _lam_tiles(lr_ref[d, 0], li_ref[d, 0], (s_ctx, s_lat))
            cam8, (cw_c, cw_l) = _lam_tiles(lr_ref[d, 0], li_ref[d, 0], (s_ctx, s_lat), conj=True)
            bre_v, bim_v = bre_ref[d, 0].astype(BF16), bim_ref[d, 0].astype(BF16)
            sre[...] = _dotf(ub, bre_v)
            sim[...] = _dotf(ub, bim_v)
            end_c, _ = _seg_scan(sre, sim, lam8, pw_c, 0, s_ctx, rev, _zero_state(), fre, fim, ic_re, ic_im)
            _seg_scan(sre, sim, lam8, pw_l, LC, s_lat, rev, end_c, fre, fim, il_re, il_im)
            gre[...] = _dotf(dyb, cre_ref[d, 0].astype(BF16))
            gim[...] = -_dotf(dyb, cim_ref[d, 0].astype(BF16))
            end_g, acc_l = _seg_scan(gre, gim, cam8, cw_l, LC, s_lat, not rev, _zero_state(), fre, fim, jre, jim,
                                     prev=(sre, sim, il_re, il_im))
            _, acc_c = _seg_scan(gre, gim, cam8, cw_c, 0, s_ctx, not rev, end_g, fre, fim, jre, jim,
                                 prev=(sre, sim, ic_re, ic_im))
            dlr_ref[d, 0] = _sum0(acc_l[0] + acc_c[0])
            dli_ref[d, 0] = _sum0(acc_l[1] + acc_c[1])
            grb, gib = gre[...].astype(BF16), gim[...].astype(BF16)
            du = _dotf(grb, bre_v, "nt") + _dotf(gib, bim_v, "nt")
            if d == 0:
                dus[...] = du
            else:
                dus[...] += du
            dbre_ref[d, 0] = _dotf(ub, grb, "tn")
            dbim_ref[d, 0] = _dotf(ub, gib, "tn")
            dcre_ref[d, 0] = _dotf(dyb, sre[...].astype(BF16), "tn")
            dcim_ref[d, 0] = -_dotf(dyb, sim[...].astype(BF16), "tn")
        _from_seg_order(dus, du_ref, T)

    ublk, lam, mat = _scan_specs(T)
    lam_s = jax.ShapeDtypeStruct(lam_re.shape, F32)
    mat_s = jax.ShapeDtypeStruct(bre.shape, F32)
    return pl.pallas_call(
        body, grid=(NJ,), in_specs=[ublk, ublk, lam, lam, mat, mat, mat, mat],
        out_specs=[ublk, lam, lam, mat, mat, mat, mat],
        out_shape=[jax.ShapeDtypeStruct((T, G * CH), F32), lam_s, lam_s, mat_s, mat_s, mat_s, mat_s],
        scratch_shapes=[pltpu.VMEM((T, UB), F32)] * 3 + [pltpu.VMEM((T, SB), F32)] * 4 + [pltpu.VMEM((SEG, SB), F32)] * 8,
        compiler_params=_cp(("arbitrary",)), name=name)(u, dy, lam_re, lam_im, bre, bim, cre, cim)


def _block_diag(m):
    m5 = m.reshape(2, NJ, GB, CH, P)
    eye = jnp.eye(GB, dtype=m.dtype)
    return (m5[:, :, :, :, None, :] * eye[None, None, :, None, :, None]).reshape(2, NJ, UB, SB)


def _diag_blocks(m):
    m6 = m.reshape(2, NJ, GB, CH, GB, P)
    idx = jnp.arange(GB)
    return m6[:, :, idx, :, idx, :].transpose(1, 2, 0, 3, 4).reshape(2, G, CH, P)


class Exchange:
    def __init__(self, xs, modes):
        self.n = len(xs)
        self.modes = [modes] * self.n if isinstance(modes, (str, int)) else list(modes)
        self.out_shape = [jax.ShapeDtypeStruct(self._shape(x, md), x.dtype) for x, md in zip(xs, self.modes)]
        self.scratch = [pltpu.SemaphoreType.DMA((NDEV - 1, self.n)), pltpu.SemaphoreType.DMA((NDEV - 1, self.n)),
                        pltpu.SemaphoreType.DMA((self.n,))]
        self.specs = [pl.BlockSpec(memory_space=pl.ANY)] * self.n

    @staticmethod
    def _shape(x, mode):
        if mode == "gather":
            return (NDEV,) + tuple(x.shape)
        return tuple(x.shape) if mode == "lead" else (NDEV, x.shape[0], mode) + tuple(x.shape[2:])

    @staticmethod
    def _piece(x_ref, mode, dev):
        if mode == "gather":
            return x_ref
        return x_ref.at[dev] if mode == "lead" else x_ref.at[:, pl.ds(dev * mode, mode)]

    def _copies(self, x_refs, out_refs, sems):
        send_sems, recv_sems, local_sems = sems
        mx, my, mc = lax.axis_index("x"), lax.axis_index("y"), lax.axis_index("c")
        me = 4 * mx + 2 * my + mc
        local = [pltpu.make_async_copy(self._piece(x_ref, self.modes[a], me), out_ref.at[me], local_sems.at[a])
                 for a, (x_ref, out_ref) in enumerate(zip(x_refs, out_refs))]
        sends, recvs = [], []
        for k in range(1, NDEV):
            peer = (1 - mx if k & 4 else mx, 1 - my if k & 2 else my, 1 - mc if k & 1 else mc)
            pid = 4 * peer[0] + 2 * peer[1] + peer[2]
            for a, (x_ref, out_ref) in enumerate(zip(x_refs, out_refs)):
                src = self._piece(x_ref, self.modes[a], pid)
                sems_k = dict(send_sem=send_sems.at[k - 1, a], recv_sem=recv_sems.at[k - 1, a], device_id=peer,
                              device_id_type=MESH_T)
                sends.append(pltpu.make_async_remote_copy(src_ref=src, dst_ref=out_ref.at[me], **sems_k))
                recvs.append(pltpu.make_async_remote_copy(src_ref=src, dst_ref=out_ref.at[pid], **sems_k))
        return local, sends, recvs

    def start(self, x_refs, out_refs, sems):
        local, sends, _ = self._copies(x_refs, out_refs, sems)
        for cp in local + sends:
            cp.start()

    def finish(self, x_refs, out_refs, sems):
        local, sends, recvs = self._copies(x_refs, out_refs, sems)
        for cp in recvs:
            cp.wait_recv()
        for cp in sends:
            cp.wait_send()
        for cp in local:
            cp.wait()


def exchange(xs, modes, name):
    ex = Exchange(xs, modes)
    n = ex.n

    def body(*refs):
        ex.start(refs[:n], refs[n:2 * n], refs[2 * n:])
        ex.finish(refs[:n], refs[n:2 * n], refs[2 * n:])

    return pl.pallas_call(body, in_specs=ex.specs, out_specs=ex.specs, out_shape=ex.out_shape, scratch_shapes=ex.scratch,
                          compiler_params=pltpu.CompilerParams(has_side_effects=True), name=name)(*xs)


def _dot_f32(a, b, dn):
    return lax.dot_general(a, b, dn, preferred_element_type=F32, precision=lax.Precision.HIGHEST)


def ada_fwd(cg, c_ctx, ada_w, ada_b_loc, name):
    W = ada_w.shape[2]

    def body(cg_ref, cc_ref, w_ref, b_ref, o_ref):
        a = jnp.concatenate([_silu(cg_ref[...]), jnp.broadcast_to(_silu(cc_ref[...]), (NDEV, D))], axis=0)
        for i in range(2):
            o_ref[i] = _dot_f32(a, w_ref[i], _DN["nn"]) + b_ref[i]

    return pl.pallas_call(body, out_shape=jax.ShapeDtypeStruct((2, 2 * NDEV, W), F32),
                          compiler_params=_cp(), name=name)(cg, c_ctx, ada_w, ada_b_loc)


def ada_bwd(cg, c_ctx, ada_w, dm_loc, dm_all, name):
    W = ada_w.shape[2]

    def body(cg_ref, cc_ref, w_ref, dl_ref, da_ref, gw_ref, dcc_ref, gb_ref):
        a = jnp.concatenate([_silu(cg_ref[...]), jnp.broadcast_to(_silu(cc_ref[...]), (NDEV, D))], axis=0)
        dcc = jnp.zeros((1, D), F32)
        for i in range(2):
            dl = dl_ref[i]
            gw_ref[i] = _dot_f32(a, dl, _DN["tn"])
            dctx = jnp.sum(dl[NDEV:], axis=0, keepdims=True)
            dcc = dcc + _dot_f32(dctx, w_ref[i], _DN["nt"])
        dcc_ref[...] = dcc
        gb_ref[...] = jnp.sum(da_ref[...], axis=0)

    return pl.pallas_call(body, out_shape=[jax.ShapeDtypeStruct((2, D, W), F32), jax.ShapeDtypeStruct((1, D), F32),
                                           jax.ShapeDtypeStruct((2, 3 * D), F32)],
                          compiler_params=_cp(), name=name)(cg, c_ctx, ada_w, dm_loc, dm_all)


def cctx_finish(parts, c_ctx, name):
    def body(p_ref, cc_ref, o_ref):
        o_ref[...] = jnp.sum(p_ref[...], axis=0, keepdims=True) * _dsilu(cc_ref[...])

    return pl.pallas_call(body, out_shape=jax.ShapeDtypeStruct((1, D), F32), name=name)(parts, c_ctx)


def _adamw_update(g_ref, w_ref, m_ref, v_ref, go_ref, d_ref, mo_ref, vo_ref):
    g = g_ref[0].astype(F32)
    for s in range(1, g_ref.shape[0]):
        g = g + g_ref[s].astype(F32)
    mn = B1 * m_ref[...] + (1.0 - B1) * g
    vn = B2 * v_ref[...] + (1.0 - B2) * g * g
    go_ref[...] = g
    mo_ref[...] = mn
    vo_ref[...] = vn
    d_ref[...] = -LR * ((mn * (1.0 / (1.0 - B1 ** STEP))) / (jnp.sqrt(vn * (1.0 / (1.0 - B2 ** STEP))) + AEPS) + WD * w_ref[...])


def adamw(gstack, w, m, v, name, tr=256):
    n, R, C = gstack.shape
    tr = max(t for t in range(8, min(tr, R) + 1, 8) if R % t == 0)
    spec = pl.BlockSpec((tr, C), lambda i: (i, 0))
    return pl.pallas_call(_adamw_body(1), grid=(R // tr,),
                          in_specs=[pl.BlockSpec((n, tr, C), lambda i: (0, i, 0)), spec, spec, spec],
                          out_specs=[spec] * 4, out_shape=[jax.ShapeDtypeStruct((R, C), F32)] * 4,
                          compiler_params=_cp(("parallel",)), name=name)(gstack, w, m, v)


def _adamw_body(k):
    def body(*refs):
        for t in range(k):
            _adamw_update(*refs[4 * t:4 * t + 4], *refs[4 * k + 4 * t:4 * k + 4 * t + 4])
    return body


def adamw_multi(items, grid, name):
    k = len(items)
    ins, in_specs, out_specs, out_shape = [], [], [], []
    for g, g_spec, w, m, v, w_spec in items:
        ins += [g, w, m, v]
        in_specs += [g_spec, w_spec, w_spec, w_spec]
    for g, g_spec, w, m, v, w_spec in items:
        out_specs += [w_spec] * 4
        out_shape += [jax.ShapeDtypeStruct(w.shape, F32)] * 4
    res = pl.pallas_call(_adamw_body(k), grid=grid, in_specs=in_specs, out_specs=out_specs, out_shape=out_shape,
                         compiler_params=_cp(("arbitrary",) * len(grid)), name=name)(*ins)
    return [res[4 * t:4 * t + 4] for t in range(k)]


def _whole(a, grid_rank):
    zeros = (0,) * a.ndim
    return pl.BlockSpec(a.shape, lambda *idx: zeros)


def sum_slots(xs, name):
    def body(*refs):
        for x_ref, o_ref in zip(refs[:len(xs)], refs[len(xs):]):
            acc = x_ref[0]
            for s in range(1, NDEV):
                acc = acc + x_ref[s]
            o_ref[...] = acc

    return pl.pallas_call(body, out_shape=[jax.ShapeDtypeStruct(x.shape[1:], F32) for x in xs],
                          compiler_params=_cp(), name=name)(*xs)


def _col_shards(g):
    R, N = g.shape
    return g.reshape(R, NDEV, N // NDEV).transpose(1, 0, 2)


def _vec2(v):
    return jnp.broadcast_to(v.reshape(1, 1, -1), (2, 1, v.size))


SHARD_ROWS = {"mla_w_in": 192, "mla_w_uq": 192, "mla_w_ukv": 256, "s5_w_in": 256}


def _t_shard(wsh, rows):
    t = wsh[0].T.astype(BF16)
    return jnp.pad(t, ((0, rows - t.shape[0]), (0, 0)))


def _win_order():
    w = IN_W // NDEV
    perm = np.zeros((IN_WP, NDEV * SHARD_ROWS["mla_w_in"]), np.float32)
    first = QL + KVL + ROPE
    for c in range(IN_W):
        n = c + HEADS * VD if c < first else c - first
        perm[n, (c // w) * SHARD_ROWS["mla_w_in"] + c % w] = 1.0
    return jnp.asarray(perm, BF16)


def local_step(ctx, x, tgt, mod, Wt, small, l1_shards):
    T = LC + x.shape[0]
    xa = ("cat", ctx, x)
    sh = [mod[i, :, None, 0:D] for i in range(2)]
    sc = [mod[i, :, None, D:2 * D] for i in range(2)]
    gt = [mod[i, :, None, 2 * D:] for i in range(2)]
    ng = [_vec2(small["norm_g"][i]) for i in range(2)]
    qg, kvg = _vec2(small["mla_q_norm"]), _vec2(small["mla_kv_norm"])
    cosf, sinf, pm, pmt = _rope_tables(T)

    (h0,), _ = rowwise(st_norm_mod, [xa], [ng[0], sc[0], sh[0]], [(D, BF16)], [], "l0_norm")
    p0 = mm(h0, Wt["mla_w_in"], "nt", "l0_in")
    z0, cq, ckv = (p0, 0, HEADS * VD), (p0, HEADS * VD // QL, QL), (p0, (HEADS * VD + QL) // KVL, KVL)
    kr = p0[:, HEADS * VD + QL + KVL:HEADS * VD + QL + KVL + ROPE]
    (cqn,), _ = rowwise(st_rms, [cq], [qg], [(QL, BF16)], [], "l0_qnorm")
    (ckvn,), _ = rowwise(st_rms, [ckv], [kvg], [(KVL, BF16)], [], "l0_kvnorm")
    q = mm(cqn, Wt["mla_w_uq"], "nt", "l0_uq")
    kv = mm(ckvn, Wt["mla_w_ukv"], "nt", "l0_ukv")
    qh = q.reshape(T, HEADS, QK).transpose(1, 0, 2)
    kvh = kv.reshape(T, HEADS, NOPE + VD).transpose(1, 0, 2)
    kraw = jnp.concatenate([kvh[..., :NOPE], jnp.broadcast_to(kr[None], (HEADS, T, ROPE))], axis=-1)
    Q = rope(qh, cosf, sinf, pm, False, BF16, "l0_rope_q")
    K = rope(kraw, cosf, sinf, pm, False, BF16, "l0_rope_k")
    V = kvh[..., NOPE:].astype(BF16)
    (o, lse), got = attn_fwd(Q, K, V, "l0_attn", rode=l1_shards, modes="gather")
    Wt, small = dict(Wt), dict(small)
    for n, a in zip(L1_BIG, got):
        Wt[n] = a.reshape(-1, a.shape[-1])
    vecs = lax.bitcast_convert_type(got[-1].reshape(NDEV, 2, -1, 2), F32)
    small["s5_d"], small["s5_b_glu"] = vecs[:, 0, :].reshape(D), vecs[:, 1, :].reshape(D)
    o2 = o.transpose(1, 0, 2).reshape(T, HEADS * VD)
    (og,), _ = rowwise(st_gate, [o2, z0], [], [(D, BF16)], [], "l0_gate")
    out0 = mm(og, Wt["mla_w_out"], "nn", "l0_out")
    (x1,), _ = rowwise(st_resid, [xa, out0], [gt[0]], [(D, F32)], [], "l0_resid")

    ls = small["s5_log_step"].reshape(2, G, 1)
    a_re, a_im = small["s5_a_re"].reshape(2, G, P), small["s5_a_im"].reshape(2, G, P)
    b_re, b_im = small["s5_b_re"].reshape(2, G * P, CH), small["s5_b_im"].reshape(2, G * P, CH)
    lam_re, lam_im, f_re, f_im = disc_fwd(a_re, a_im, ls, "s5_disc")
    f_re2, f_im2 = f_re.reshape(2, G * P, 1), f_im.reshape(2, G * P, 1)
    bb_re, bb_im = disc_b(f_re2, f_im2, b_re, b_im, "s5_disc_b")
    bre = _block_diag(bb_re.reshape(2, G, P, CH).transpose(0, 1, 3, 2))
    bim = _block_diag(bb_im.reshape(2, G, P, CH).transpose(0, 1, 3, 2))
    cre = _block_diag(small["s5_c_re"].reshape(2, G, CH, P))
    cim = _block_diag(small["s5_c_im"].reshape(2, G, CH, P))
    lam_re4, lam_im4 = lam_re.reshape(2, NJ, 1, SB), lam_im.reshape(2, NJ, 1, SB)

    (h1,), _ = rowwise(st_norm_mod, [x1], [ng[1], sc[1], sh[1]], [(D, BF16)], [], "l1_norm")
    p1 = mm(h1, Wt["s5_w_in"], "nt", "l1_in")
    u, z1 = (p1, 0, D), (p1, 1, D)
    yssm = scan_fwd(p1, lam_re4, lam_im4, bre, bim, cre, cim, "s5_scan")
    dvec, bglu = _vec2(small["s5_d"]), _vec2(small["s5_b_glu"])
    (y, y1b), _ = rowwise(st_s5a, [yssm, u], [dvec], [(D, F32), (D, BF16)], [], "l1_gelu")
    gl = mm(y1b, Wt["s5_w_glu"], "nn", "l1_glu")
    (y3,), _ = rowwise(st_s5b, [y, gl, z1], [bglu], [(D, BF16)], [], "l1_gate")
    out1 = mm(y3, Wt["s5_w_out"], "nn", "l1_out")
    (x2,), _ = rowwise(st_resid, [x1, out1], [gt[1]], [(D, F32)], [], "l1_resid")

    fg = _vec2(small["final_g"])
    lat_mask = jnp.stack([jnp.zeros((1, D), F32), jnp.ones((1, D), F32)])
    (dx2,), (dfg, lvec) = rowwise(st_final, [x2, ("lat", tgt)], [fg, lat_mask], [(D, F32)], [D, 128], "final")

    (dout1,), (dgt1,) = rowwise(st_resid_bwd, [dx2, out1], [gt[1]], [(D, BF16)], [D], "l1_resid_b")
    g_w_out5 = mm(y3, dout1, "tn", "l1_out_dw", out_dtype=BF16)
    dy3 = mm(dout1, Wt["s5_w_out"], "nt", "l1_out_dx")
    (dgl, dz1, dy1a), (dbglu,) = rowwise(st_s5b_bwd, [dy3, y, gl, z1], [bglu], [(D, BF16), (D, BF16), (D, F32)], [D], "l1_gate_b")
    g_w_glu = mm(y1b, dgl, "tn", "l1_glu_dw", out_dtype=BF16)
    dy1b = mm(dgl, Wt["s5_w_glu"], "nt", "l1_glu_dx")
    (dy, du_d), (dd,) = rowwise(st_s5a_bwd, [dy1a, dy1b, y, u], [dvec], [(D, F32), (D, F32)], [D], "l1_gelu_b")
    du_s, dlr, dli, dbre, dbim, dcre, dcim = scan_bwd(p1, dy, lam_re4, lam_im4, bre, bim, cre, cim, "s5_scan_b")
    du = du_d + du_s
    dbb_re = _diag_blocks(dbre).transpose(0, 1, 3, 2).reshape(2, G * P, CH)
    dbb_im = _diag_blocks(dbim).transpose(0, 1, 3, 2).reshape(2, G * P, CH)
    g_c_re, g_c_im = _diag_blocks(dcre), _diag_blocks(dcim)
    g_b_re, g_b_im, dfr, dfi = disc_b_bwd(f_re2, f_im2, b_re, b_im, dbb_re, dbb_im, "s5_disc_b_b")
    g_a_re, g_a_im, g_ls = disc_a_bwd(a_re, a_im, ls, dlr.reshape(2, G, P), dli.reshape(2, G, P),
                                      dfr.reshape(2, G, P), dfi.reshape(2, G, P), "s5_disc_b_a")
    dp1 = jnp.concatenate([du.astype(BF16), dz1], axis=1)
    g_w_in5 = mm(h1, dp1, "tn", "l1_in_dw", out_dtype=BF16, tm=D, tn=2 * D // NDEV, shard_out=True)
    dh1 = mm(dp1, Wt["s5_w_in"], "nn", "l1_in_dx")
    (dx1,), (dsh1, dsc1, dng1) = rowwise(st_norm_mod_bwd, [x1, dh1, dx2], [ng[1], sc[1]], [(D, F32)], [D, D, D], "l1_norm_b")

    (dout0,), (dgt0,) = rowwise(st_resid_bwd, [dx1, out0], [gt[0]], [(D, BF16)], [D], "l0_resid_b")
    g_w_out = mm(og, dout0, "tn", "l0_out_dw", out_dtype=BF16)
    dog = mm(dout0, Wt["mla_w_out"], "nt", "l0_out_dx")
    (do2, dz0), _ = rowwise(st_gate_bwd, [dog, o2, z0], [], [(D, F32), (D, F32)], [], "l0_gate_b")
    doh = do2.reshape(T, HEADS, VD).transpose(1, 0, 2)
    rows8 = lambda g: g.reshape(NDEV, -1, g.shape[-1])
    both = lambda s: s[0, 0] + s[1, 0]
    dense = lambda g: g.reshape(2, G * P * CH // 128, 128)
    chunks = [dense(g_b_re), dense(g_b_im), g_c_re, g_c_im]
    l1_send = [g_w_in5, rows8(g_w_glu), rows8(g_w_out5), both(dd).reshape(NDEV, 1, -1), both(dbglu).reshape(NDEV, 1, -1)]
    (dQ, dK, dV), l1_recv = attn_bwd(Q, K, V, o, lse, doh, "l0_attn_b", rode=l1_send + chunks,
                                     modes=["lead"] * len(l1_send) + [a.shape[1] // NDEV for a in chunks])
    dqh = rope(dQ, cosf, sinf, pmt, True, F32, "l0_rope_q_b")
    dkraw, dksum = rope(dK, cosf, sinf, pmt, True, F32, "l0_rope_k_b", head_sum=True)
    dq = dqh.transpose(1, 0, 2).reshape(T, HEADS * QK).astype(BF16)
    dkv = jnp.concatenate([dkraw[..., :NOPE], dV], axis=-1).transpose(1, 0, 2).reshape(T, HEADS * (NOPE + VD)).astype(BF16)
    dkr = dksum[:, NOPE:]
    g_w_uq = _col_shards(mm(cqn, dq, "tn", "l0_uq_dw", out_dtype=BF16))
    dcqn = mm(dq, Wt["mla_w_uq"], "nn", "l0_uq_dx")
    g_w_ukv = mm(ckvn, dkv, "tn", "l0_ukv_dw", out_dtype=BF16, tm=KVL, tn=HEADS * (NOPE + VD) // NDEV, shard_out=True)
    dckvn = mm(dkv, Wt["mla_w_ukv"], "nn", "l0_ukv_dx")
    (dcq,), (dqg,) = rowwise(st_rms_bwd, [cq, dcqn], [qg], [(QL, F32)], [QL], "l0_qnorm_b")
    (dckv,), (dkvg,) = rowwise(st_rms_bwd, [ckv, dckvn], [kvg], [(KVL, F32)], [KVL], "l0_kvnorm_b")
    dp0 = jnp.concatenate([dz0, dcq, dckv, dkr, jnp.zeros((T, IN_WP - IN_W), F32)], axis=1).astype(BF16)
    g_p = mm(h0, dp0, "tn", "l0_in_dw", out_dtype=BF16)
    g_w_in = _col_shards(jnp.concatenate([g_p[:, HEADS * VD:IN_W], g_p[:, :HEADS * VD]], axis=1))
    dh0 = mm(dp0, Wt["mla_w_in"], "nn", "l0_in_dx")
    (grad_x,), (dsh0, dsc0, dng0) = rowwise(st_norm_mod_bwd, [xa, dh0, dx1], [ng[0], sc[0]], [(D, F32, "lat")], [D, D, D], "l0_norm_b")

    dmod = jnp.stack([jnp.concatenate([dsh0, dsc0, dgt0], axis=-1)[:, 0], jnp.concatenate([dsh1, dsc1, dgt1], axis=-1)[:, 0]])
    gbig = {"mla_w_in": g_w_in, "mla_w_uq": g_w_uq, "mla_w_ukv": g_w_ukv, "mla_w_out": rows8(g_w_out)}
    gsmall = {"norm_g": jnp.stack([both(dng0), both(dng1)]), "mla_q_norm": both(dqg), "mla_kv_norm": both(dkvg),
              "s5_a_re": g_a_re, "s5_a_im": g_a_im, "s5_log_step": g_ls, "final_g": dfg[1, 0]}
    return lvec[1], grad_x, dmod, gbig, gsmall, l1_recv


COL_SHARDED = ("mla_w_in", "mla_w_uq", "mla_w_ukv", "s5_w_in")
ROW_SHARDED = ("mla_w_out", "s5_w_glu", "s5_w_out")
VEC_SHARDED = ("s5_d", "s5_b_glu")
BIG = COL_SHARDED + ROW_SHARDED
L0_BIG = ("mla_w_in", "mla_w_uq", "mla_w_ukv", "mla_w_out")
L1_BIG = ("s5_w_in", "s5_w_glu", "s5_w_out")
BITS16 = jnp.bfloat16
SMALL_RS = ("norm_g", "mla_q_norm", "mla_kv_norm", "s5_a_re", "s5_a_im", "s5_log_step", "s5_b_re", "s5_b_im",
            "s5_c_re", "s5_c_im", "final_g")
CHUNKED = ("s5_b_re", "s5_b_im", "s5_c_re", "s5_c_im")
DENSE = ("s5_b_re", "s5_b_im")
TINY = ("norm_g", "mla_q_norm", "mla_kv_norm", "s5_a_re", "s5_a_im", "s5_log_step", "final_g")
ORDER = ("c_ctx", "ada_w", "ada_b", "norm_g", "mla_w_in", "mla_q_norm", "mla_w_uq", "mla_kv_norm", "mla_w_ukv",
         "mla_w_out", "s5_w_in", "s5_a_re", "s5_a_im", "s5_log_step", "s5_b_re", "s5_b_im", "s5_c_re", "s5_c_im",
         "s5_d", "s5_w_glu", "s5_b_glu", "s5_w_out", "final_g")


def kernel(x, c, ctx, c_ctx, ada_w, ada_b, norm_g, mla_w_in, mla_q_norm, mla_w_uq, mla_kv_norm, mla_w_ukv, mla_w_out, s5_w_in, s5_a_re, s5_a_im, s5_log_step, s5_b_re, s5_b_im, s5_c_re, s5_c_im, s5_d, s5_w_glu, s5_b_glu, s5_w_out, final_g, loss_target, m_c_ctx, m_ada_w, m_ada_b, m_norm_g, m_mla_w_in, m_mla_q_norm, m_mla_w_uq, m_mla_kv_norm, m_mla_w_ukv, m_mla_w_out, m_s5_w_in, m_s5_a_re, m_s5_a_im, m_s5_log_step, m_s5_b_re, m_s5_b_im, m_s5_c_re, m_s5_c_im, m_s5_d, m_s5_w_glu, m_s5_b_glu, m_s5_w_out, m_final_g, v_c_ctx, v_ada_w, v_ada_b, v_norm_g, v_mla_w_in, v_mla_q_norm, v_mla_w_uq, v_mla_kv_norm, v_mla_w_ukv, v_mla_w_out, v_s5_w_in, v_s5_a_re, v_s5_a_im, v_s5_log_step, v_s5_b_re, v_s5_b_im, v_s5_c_re, v_s5_c_im, v_s5_d, v_s5_w_glu, v_s5_b_glu, v_s5_w_out, v_final_g):
    w = dict(c_ctx=c_ctx, ada_w=ada_w, ada_b=ada_b, norm_g=norm_g, mla_w_in=mla_w_in, mla_q_norm=mla_q_norm,
             mla_w_uq=mla_w_uq, mla_kv_norm=mla_kv_norm, mla_w_ukv=mla_w_ukv, mla_w_out=mla_w_out, s5_w_in=s5_w_in,
             s5_a_re=s5_a_re, s5_a_im=s5_a_im, s5_log_step=s5_log_step, s5_b_re=s5_b_re, s5_b_im=s5_b_im,
             s5_c_re=s5_c_re, s5_c_im=s5_c_im, s5_d=s5_d, s5_w_glu=s5_w_glu, s5_b_glu=s5_b_glu, s5_w_out=s5_w_out,
             final_g=final_g)
    m = dict(c_ctx=m_c_ctx, ada_w=m_ada_w, ada_b=m_ada_b, norm_g=m_norm_g, mla_w_in=m_mla_w_in, mla_q_norm=m_mla_q_norm,
             mla_w_uq=m_mla_w_uq, mla_kv_norm=m_mla_kv_norm, mla_w_ukv=m_mla_w_ukv, mla_w_out=m_mla_w_out,
             s5_w_in=m_s5_w_in, s5_a_re=m_s5_a_re, s5_a_im=m_s5_a_im, s5_log_step=m_s5_log_step, s5_b_re=m_s5_b_re,
             s5_b_im=m_s5_b_im, s5_c_re=m_s5_c_re, s5_c_im=m_s5_c_im, s5_d=m_s5_d, s5_w_glu=m_s5_w_glu,
             s5_b_glu=m_s5_b_glu, s5_w_out=m_s5_w_out, final_g=m_final_g)
    v = dict(c_ctx=v_c_ctx, ada_w=v_ada_w, ada_b=v_ada_b, norm_g=v_norm_g, mla_w_in=v_mla_w_in, mla_q_norm=v_mla_q_norm,
             mla_w_uq=v_mla_w_uq, mla_kv_norm=v_mla_kv_norm, mla_w_ukv=v_mla_w_ukv, mla_w_out=v_mla_w_out,
             s5_w_in=v_s5_w_in, s5_a_re=v_s5_a_re, s5_a_im=v_s5_a_im, s5_log_step=v_s5_log_step, s5_b_re=v_s5_b_re,
             s5_b_im=v_s5_b_im, s5_c_re=v_s5_c_re, s5_c_im=v_s5_c_im, s5_d=v_s5_d, s5_w_glu=v_s5_w_glu,
             s5_b_glu=v_s5_b_glu, s5_w_out=v_s5_w_out, final_g=v_final_g)

    me = 4 * lax.axis_index("x") + 2 * lax.axis_index("y") + lax.axis_index("c")
    WA = ada_w.shape[2]

    cg = exchange([c], "gather", "gather_c")[0].reshape(NDEV, D)
    cc2 = c_ctx.reshape(1, D)
    ada_b_loc = lax.dynamic_slice_in_dim(ada_b.reshape(2, 3 * D // WA, WA), me, 1, axis=1)
    part = ada_fwd(cg, cc2, ada_w, ada_b_loc, "ada_fwd")
    pg = exchange([part], "gather", "gather_mod")[0]
    mod_l = lax.dynamic_index_in_dim(pg, me, axis=2, keepdims=False).transpose(1, 0, 2).reshape(2, 3 * D)
    mod_c = pg[:, :, NDEV, :].transpose(1, 0, 2).reshape(2, 3 * D)
    mod = jnp.stack([mod_c, mod_l], axis=1)

    def shard(n):
        return _t_shard(w[n], SHARD_ROWS[n]) if n in COL_SHARDED else w[n][0].astype(BF16)

    wgot = exchange([shard(n) for n in L0_BIG], "gather", "gather_w")
    Wt = {n: a.reshape(-1, a.shape[-1]) for n, a in zip(L0_BIG, wgot)}
    Wt["mla_w_in"] = mm(_win_order(), Wt["mla_w_in"], "nn", "w_in_order", out_dtype=BF16)
    vec_bits = lax.bitcast_convert_type(jnp.concatenate([s5_d, s5_b_glu], axis=0), BITS16).reshape(2, -1)
    small = {n: w[n] for n in SMALL_RS}

    lvec, grad_x, dmod, gbig, gsmall, l1_recv = local_step(ctx[0], x[0], loss_target[0], mod, Wt, small,
                                                           [shard(n) for n in L1_BIG] + [vec_bits])
    loss = lax.psum(lvec[0, 0], ("x", "y", "c"))
    grad_x = grad_x[None]

    per_dev = G // NDEV
    recv = dict(zip(L0_BIG, exchange([gbig[n] for n in L0_BIG], "lead", "scatter_grads")))
    recv.update(dict(zip(L1_BIG + VEC_SHARDED, l1_recv)))
    out = {}

    def keep(n, res):
        for key, arr in zip("gdmv", res):
            out[key, n] = arr.reshape(w[n].shape)

    for n in BIG:
        keep(n, adamw(recv[n], w[n][0], m[n][0], v[n][0], "adamw_" + n))
    reduced = sum_slots(l1_recv[len(L1_BIG + VEC_SHARDED):], "sum_chunks")

    kshape = lambda n: w[n].shape if w[n].ndim > 1 else (1, w[n].size)
    got = exchange(list(reduced) + [gsmall[n].reshape(kshape(n)) for n in TINY] + [dmod], "gather", "gather_small")
    chunk_all, tiny_all, dm_all = got[:len(CHUNKED)], got[len(CHUNKED):-1], got[-1]

    dm_cols = lax.dynamic_slice_in_dim(dm_all.reshape(NDEV, 2, 2, 3 * D // WA, WA), me, 1, axis=3)[:, :, :, 0]
    dm_loc = jnp.concatenate([dm_cols[:, :, 1].transpose(1, 0, 2), dm_cols[:, :, 0].transpose(1, 0, 2)], axis=1)
    g_ada_w, dcc_part, g_ada_b = ada_bwd(cg, cc2, ada_w, dm_loc, dm_all.transpose(0, 2, 1, 3).reshape(2 * NDEV, 2, 3 * D), "ada_bwd")
    dcc_all = exchange([dcc_part], "gather", "gather_dcc")[0].reshape(NDEV, D)
    g_c_ctx = cctx_finish(dcc_all, cc2, "cctx_finish")

    flat2 = lambda t: t.reshape(-1, t.shape[-1])
    keep("ada_w", adamw(flat2(g_ada_w)[None], flat2(ada_w), flat2(m_ada_w), flat2(v_ada_w), "adamw_ada"))
    items = []
    for n, g in zip(CHUNKED, chunk_all):
        blk = (1, 1, per_dev) + w[n].shape[3:]
        if n in DENSE:
            g = g.transpose(1, 0, 2, 3).reshape(w[n].shape)
            g_spec = pl.BlockSpec((1, 1, 1) + blk[2:], lambda d, s: (0, 0, d, s, 0, 0))
        else:
            g_spec = pl.BlockSpec((1, 1, 1) + blk[2:], lambda d, s: (0, s, d, 0, 0, 0))
        items.append((g[None], g_spec, w[n], m[n], v[n], pl.BlockSpec(blk, lambda d, s: (0, d, s, 0, 0))))
    for n, res in zip(CHUNKED, adamw_multi(items, (2, NDEV), "adamw_bc")):
        keep(n, res)
    tiny_g = dict(zip(TINY, tiny_all))
    tiny_g.update({n: recv[n] for n in VEC_SHARDED})
    tiny_g["c_ctx"], tiny_g["ada_b"] = g_c_ctx[None], g_ada_b[None]
    names = list(tiny_g)
    items = [(tiny_g[n], _whole(tiny_g[n], 1)) + tuple(t[n].reshape(kshape(n)) for t in (w, m, v))
             + (pl.BlockSpec(kshape(n), lambda i, r=len(kshape(n)): (0,) * r),) for n in names]
    for n, res in zip(names, adamw_multi(items, (1,), "adamw_small")):
        keep(n, res)

    return (loss, grad_x, *[out["g", n] for n in ORDER], *[out["d", n] for n in ORDER],
            *[out["m", n] for n in ORDER], *[out["v", n] for n in ORDER])
```

```python
import math

import numpy as np
import jax
import jax.numpy as jnp
from jax import lax
from jax.experimental import pallas as pl
from jax.experimental.pallas import tpu as pltpu

F32 = jnp.float32
BF16 = jnp.bfloat16

D = 1024
L = 2048
LC = 256
NDEV = 8
GRID_W = 64
EPS = 1e-6
HEADS = 16
NOPE = 64
ROPE = 32
QK = NOPE + ROPE
VD = 64
IN_W = 256 + 128 + ROPE + HEADS * 64
IN_WP = 1536
QL = 256
KVL = 128
SCALE = QK ** -0.5
THETA = 10000.0
G = 64
P = 64
CH = 16
GB = 8
NJ = G // GB
UB = GB * CH
SB = GB * P
SEG = 8
TB = 256
VMEM_LIMIT = 56 * 1024 * 1024
B1, B2, LR, AEPS, WD, STEP = 0.9, 0.999, 0.001, 1e-8, 0.01, 10
MESH_T = pl.DeviceIdType.MESH


def _cp(sem=None):
    return pltpu.CompilerParams(dimension_semantics=sem, vmem_limit_bytes=VMEM_LIMIT)


def _sig(x):
    return 1.0 / (1.0 + jnp.exp(-x))


def _silu(x):
    return x * _sig(x)


def _dsilu(x):
    s = _sig(x)
    return s * (1.0 + x * (1.0 - s))


_GK = math.sqrt(2.0 / math.pi)


def _gelu(x):
    return 0.5 * x * (1.0 + jnp.tanh(_GK * (x + 0.044715 * x * x * x)))


def _dgelu(x):
    t = jnp.tanh(_GK * (x + 0.044715 * x * x * x))
    return 0.5 * (1.0 + t) + 0.5 * x * (1.0 - t * t) * _GK * (1.0 + 3 * 0.044715 * x * x)


def _rs(x):
    return lax.rsqrt(jnp.mean(x * x, axis=-1, keepdims=True) + EPS)


def _sum0(x):
    return jnp.sum(x, axis=0, keepdims=True)


def st_norm_mod(x, g, sc, sh):
    y = x * _rs(x) * g
    return (y * (1.0 + sc) + sh,), ()


def st_norm_mod_bwd(x, dh, dres, g, sc):
    r = _rs(x)
    xn = x * r
    y = xn * g
    dy = dh * (1.0 + sc)
    dxn = dy * g
    dx = r * (dxn - xn * jnp.mean(dxn * xn, axis=-1, keepdims=True))
    return (dres + dx,), (_sum0(dh), _sum0(dh * y), _sum0(dy * xn))


def st_rms(x, g):
    return (x * _rs(x) * g,), ()


def st_rms_bwd(x, dy, g):
    r = _rs(x)
    n = x * r
    dn = dy * g
    dx = r * (dn - n * jnp.mean(dn * n, axis=-1, keepdims=True))
    return (dx,), (_sum0(dy * n),)


def st_gate(o, z):
    return (o * _silu(z),), ()


def st_gate_bwd(dog, o, z):
    return (dog * _silu(z), dog * o * _dsilu(z)), ()


def st_resid(x, out, gt):
    return (x + gt * out,), ()


def st_resid_bwd(dx, out, gt):
    return (dx * gt,), (_sum0(dx * out),)


def st_s5a(yssm, u, d):
    y = yssm + d * u
    return (y, _gelu(y)), ()


def st_s5b(y, gl, z, b):
    return (_gelu(y) * _sig(gl + b) * _silu(z),), ()


def st_s5b_bwd(dy3, y, gl, z, b):
    y1 = _gelu(y)
    s = _sig(gl + b)
    dy2 = dy3 * _silu(z)
    dz = dy3 * y1 * s * _dsilu(z)
    dgl = dy2 * y1 * s * (1.0 - s)
    return (dgl, dz, dy2 * s), (_sum0(dgl),)


def st_s5a_bwd(dy1a, dy1b, y, u, d):
    dy = (dy1a + dy1b) * _dgelu(y)
    return (dy, dy * d), (_sum0(dy * u),)


def st_final(x2, tgt, g, mask):
    r = _rs(x2)
    n = x2 * r
    e = n * g - tgt
    dyo = e * (1.0 / D)
    dn = dyo * g
    dx = r * (dn - n * jnp.mean(dn * n, axis=-1, keepdims=True))
    lsum = jnp.sum(_sum0(e * e), axis=1, keepdims=True) * (0.5 / D)
    return (dx * mask,), (_sum0(dyo * n), jnp.broadcast_to(lsum, (1, 128)))


def rowwise(fn, rows, vecs, out_rows, out_sums, name):
    lat_blk = lambda i: jnp.maximum(i - 1, 0)
    arrays, in_specs, pick = [], [], []
    for a in rows:
        if not isinstance(a, tuple):
            a = (a, 0, a.shape[1])
        tag = a[0] if isinstance(a[0], str) else None
        if tag == "cat":
            _, ctx, x = a
            arrays += [ctx, x]
            in_specs += [pl.BlockSpec((TB, ctx.shape[1]), lambda i: (0, 0)),
                         pl.BlockSpec((TB, x.shape[1]), lambda i: (lat_blk(i), 0))]
            pick.append(2)
        elif tag == "lat":
            arrays.append(a[1])
            in_specs.append(pl.BlockSpec((TB, a[1].shape[1]), lambda i: (lat_blk(i), 0)))
            pick.append(1)
        else:
            arr, cb, width = a
            arrays.append(arr)
            in_specs.append(pl.BlockSpec((TB, width), lambda i, cb=cb: (i, cb)))
            pick.append(1)
    T = LC + L
    nin, nv, no = len(arrays), len(vecs), len(out_rows)

    def body(*refs):
        i = pl.program_id(0)
        vals, k = [], 0
        for p in pick:
            if p == 2:
                vals.append(jnp.where(i == 0, refs[k][...], refs[k + 1][...]))
            else:
                vals.append(refs[k][...])
            k += p
        vals += [r[0] for r in refs[nin:nin + nv]]
        outs, sums = fn(*vals)
        for r, o in zip(refs[nin + nv:nin + nv + no], outs):
            r[...] = o.astype(r.dtype)
        sum_refs = refs[nin + nv + no:]
        if sum_refs:
            @pl.when(i <= 1)
            def _():
                for r in sum_refs:
                    r[...] = jnp.zeros_like(r)
            for r, s in zip(sum_refs, sums):
                r[0] += s

    kind = lambda i: (jnp.minimum(i, 1), 0, 0)
    in_specs += [pl.BlockSpec((1, 1, v.shape[2]), kind) for v in vecs]
    out_specs, out_shape = [], []
    for o in out_rows:
        lat = len(o) == 3
        out_specs.append(pl.BlockSpec((TB, o[0]), (lambda i: (lat_blk(i), 0)) if lat else (lambda i: (i, 0))))
        out_shape.append(jax.ShapeDtypeStruct((L if lat else T, o[0]), o[1]))
    out_specs += [pl.BlockSpec((1, 1, c), kind) for c in out_sums]
    out_shape += [jax.ShapeDtypeStruct((2, 1, c), F32) for c in out_sums]
    res = pl.pallas_call(body, grid=(T // TB,), in_specs=in_specs, out_specs=out_specs, out_shape=out_shape,
                         compiler_params=_cp(("arbitrary",)), name=name)(*arrays, *vecs)
    return res[:no], res[no:]


_DN = {"nn": (((1,), (0,)), ((), ())), "nt": (((1,), (1,)), ((), ())), "tn": (((0,), (0,)), ((), ()))}


def mm(a, b, mode, name, out_dtype=F32, tm=256, tn=None, shard_out=False):
    if mode == "nn":
        (M, K), (_, N) = a.shape, b.shape
    elif mode == "nt":
        (M, K), (N, _) = a.shape, b.shape
    else:
        (K, M), (_, N) = a.shape, b.shape
    tm = min(tm, M)
    tn = N if tn is None else tn
    dn = _DN[mode]

    def body(a_ref, b_ref, o_ref):
        o_ref[...] = lax.dot_general(a_ref[...].astype(BF16), b_ref[...].astype(BF16), dn,
                                     preferred_element_type=F32).astype(o_ref.dtype)

    a_spec = pl.BlockSpec((K, tm), lambda i, j: (0, i)) if mode == "tn" else pl.BlockSpec((tm, K), lambda i, j: (i, 0))
    b_spec = pl.BlockSpec((tn, K), lambda i, j: (j, 0)) if mode == "nt" else pl.BlockSpec((K, tn), lambda i, j: (0, j))
    if shard_out:
        def body(a_ref, b_ref, o_ref):
            o_ref[0] = lax.dot_general(a_ref[...].astype(BF16), b_ref[...].astype(BF16), dn,
                                       preferred_element_type=F32).astype(o_ref.dtype)
        out_spec = pl.BlockSpec((1, tm, tn), lambda i, j: (j, i, 0))
        out_shape = jax.ShapeDtypeStruct((N // tn, M, tn), out_dtype)
    else:
        out_spec = pl.BlockSpec((tm, tn), lambda i, j: (i, j))
        out_shape = jax.ShapeDtypeStruct((M, N), out_dtype)
    return pl.pallas_call(body, grid=(M // tm, N // tn), in_specs=[a_spec, b_spec], out_specs=out_spec, out_shape=out_shape,
                          compiler_params=_cp(("parallel", "arbitrary")), name=name)(a, b)


def _rope_tables(T, width=QK, first=NOPE):
    nlat = T - LC
    pos = np.arange(nlat)
    row, col = pos // GRID_W, pos % GRID_W
    half = ROPE // 2
    inv = 1.0 / (THETA ** (np.arange(0, half, 2, dtype=np.float64) / half))
    cosf = np.ones((T, width), np.float64)
    sinf = np.zeros((T, width), np.float64)
    perm = np.zeros((width, width), np.float32)
    for m in range(ROPE):
        j = first + m
        blk, w = m // half, m % half
        ang = (row if blk == 0 else col)[:, None] * inv[None, :]
        f = w % (half // 2)
        cosf[LC:, j] = np.cos(ang[:, f])
        if w < half // 2:
            sinf[LC:, j] = -np.sin(ang[:, f])
            perm[j + half // 2, j] = 1.0
        else:
            sinf[LC:, j] = np.sin(ang[:, f])
            perm[j - half // 2, j] = 1.0
    return jnp.asarray(cosf, F32), jnp.asarray(sinf, F32), jnp.asarray(perm, BF16), jnp.asarray(perm.T, BF16)


def _exact_perm(x, pm):
    hi = x.astype(BF16)
    r1 = x - hi.astype(F32)
    mid = r1.astype(BF16)
    lo = (r1 - mid.astype(F32)).astype(BF16)
    dot = lambda a: jnp.dot(a, pm, preferred_element_type=F32)
    return dot(hi) + dot(mid) + dot(lo)


def _rot(x, cv, sv, pv, inverse):
    if inverse:
        return x * cv + _exact_perm(x * sv, pv)
    return x * cv + _exact_perm(x, pv) * sv


def rope(x, cosf, sinf, pm, inverse, out_dtype, name, scale=1.0):
    H, T, _ = x.shape

    def body(x_ref, c_ref, s_ref, p_ref, o_ref):
        cv, sv, pv = c_ref[...], s_ref[...], p_ref[...]
        for h in range(H):
            o_ref[h] = (_rot(x_ref[h], cv, sv, pv, inverse) * scale).astype(o_ref.dtype)

    return pl.pallas_call(
        body, grid=(T // TB,),
        in_specs=[pl.BlockSpec((H, TB, QK), lambda i: (0, i, 0)), pl.BlockSpec((TB, QK), lambda i: (i, 0)),
                  pl.BlockSpec((TB, QK), lambda i: (i, 0)), pl.BlockSpec((QK, QK), lambda i: (0, 0))],
        out_specs=pl.BlockSpec((H, TB, QK), lambda i: (0, i, 0)), out_shape=jax.ShapeDtypeStruct((H, T, QK), out_dtype),
        compiler_params=_cp(("parallel",)), name=name)(x, cosf, sinf, pm)


KVW = NOPE + VD


def _kv_selectors():
    s_kn = np.zeros((KVW, QK), np.float32)
    s_kr = np.zeros((128, QK), np.float32)
    s_v = np.zeros((KVW, VD), np.float32)
    for l in range(NOPE):
        s_kn[l, l] = 1.0
    for l in range(ROPE):
        s_kr[l, NOPE + l] = 1.0
    for l in range(VD):
        s_v[NOPE + l, l] = 1.0
    return s_kn, s_kr, s_v


def assemble_kv(kvh, p0, kr_block, name):
    H, T, _ = kvh.shape
    cosf, sinf, pm, _ = _rope_tables(T, 128, 0)
    s_kn, s_kr, s_v = (jnp.asarray(s, BF16) for s in _kv_selectors())

    def body(kv_ref, kr_ref, c_ref, s_ref, p_ref, skn_ref, skr_ref, sv_ref, k_ref, v_ref):
        krr = _rot(kr_ref[...], c_ref[...], s_ref[...], p_ref[...], False).astype(BF16)
        kr_part = jnp.dot(krr, skr_ref[...], preferred_element_type=F32)
        for h in range(H):
            kvb = kv_ref[h].astype(BF16)
            k_ref[h] = (jnp.dot(kvb, skn_ref[...], preferred_element_type=F32) + kr_part).astype(BF16)
            v_ref[h] = jnp.dot(kvb, sv_ref[...], preferred_element_type=F32).astype(BF16)

    rows = lambda c: pl.BlockSpec((TB, c), lambda i: (i, 0))
    const = lambda a: pl.BlockSpec(a.shape, lambda i: (0, 0))
    return pl.pallas_call(
        body, grid=(T // TB,),
        in_specs=[pl.BlockSpec((H, TB, KVW), lambda i: (0, i, 0)), pl.BlockSpec((TB, 128), lambda i: (i, kr_block)),
                  rows(128), rows(128), const(pm), const(s_kn), const(s_kr), const(s_v)],
        out_specs=[pl.BlockSpec((H, TB, QK), lambda i: (0, i, 0)), pl.BlockSpec((H, TB, VD), lambda i: (0, i, 0))],
        out_shape=[jax.ShapeDtypeStruct((H, T, QK), BF16), jax.ShapeDtypeStruct((H, T, VD), BF16)],
        compiler_params=_cp(("parallel",)), name=name)(kvh, p0, cosf, sinf, pm, s_kn, s_kr, s_v)


def split_kv_grads(dk, dv, name):
    H, T, _ = dk.shape
    cosf, sinf, _, pmt = _rope_tables(T, 128, 0)
    s_kn, s_kr, s_v = _kv_selectors()
    s_knt, s_krt, s_vt = (jnp.asarray(s.T, BF16) for s in (s_kn, s_kr, s_v))

    def body(dk_ref, dv_ref, c_ref, s_ref, p_ref, skn_ref, skr_ref, sv_ref, dkv_ref, dkr_ref):
        total = None
        for h in range(H):
            dkh = dk_ref[h]
            total = dkh if total is None else total + dkh
            dkv_ref[:, pl.ds(h * KVW, KVW)] = (
                jnp.dot(dkh.astype(BF16), skn_ref[...], preferred_element_type=F32)
                + jnp.dot(dv_ref[h].astype(BF16), sv_ref[...], preferred_element_type=F32)).astype(BF16)
        dkr_ref[...] = _rot(_exact_perm(total, skr_ref[...]), c_ref[...], s_ref[...], p_ref[...], True)

    rows = lambda c: pl.BlockSpec((TB, c), lambda i: (i, 0))
    const = lambda a: pl.BlockSpec(a.shape, lambda i: (0, 0))
    return pl.pallas_call(
        body, grid=(T // TB,),
        in_specs=[pl.BlockSpec((H, TB, QK), lambda i: (0, i, 0)), pl.BlockSpec((H, TB, VD), lambda i: (0, i, 0)),
                  rows(128), rows(128), const(pmt), const(s_knt), const(s_krt), const(s_vt)],
        out_specs=[rows(H * KVW), rows(128)],
        out_shape=[jax.ShapeDtypeStruct((T, H * KVW), BF16), jax.ShapeDtypeStruct((T, 128), F32)],
        compiler_params=_cp(("parallel",)), name=name)(dk, dv, cosf, sinf, pmt, s_knt, s_krt, s_vt)


def _by_query_block(run, T):
    @pl.when(pl.program_id(1) == 0)
    def _():
        run(LC)

    @pl.when(pl.program_id(1) > 0)
    def _():
        run(T)


def _with_rider(body, nin, nout, ride, grid):
    if ride is None:
        return body
    n = ride.n

    def wrapped(*refs):
        ins, xs = refs[:nin], refs[nin:nin + n]
        outs, got = refs[nin + n:nin + n + nout], refs[nin + n + nout:nin + 2 * n + nout]
        sems = refs[nin + 2 * n + nout:]
        step = pl.program_id(0) * grid[1] + pl.program_id(1)

        @pl.when(step == 0)
        def _():
            ride.start(xs, got, sems)

        body(*ins, *outs)

        @pl.when(step == grid[0] * grid[1] - 1)
        def _():
            ride.finish(xs, got, sems)

    return wrapped


def _ride_call(body, grid, in_specs, out_specs, out_shape, ride, rode, name, args):
    if ride is None:
        return pl.pallas_call(body, grid=grid, in_specs=in_specs, out_specs=out_specs, out_shape=out_shape,
                              compiler_params=_cp(("parallel", "arbitrary")), name=name)(*args), []
    res = pl.pallas_call(
        _with_rider(body, len(in_specs), len(out_specs), ride, grid), grid=grid,
        in_specs=in_specs + ride.specs, out_specs=out_specs + ride.specs, out_shape=out_shape + ride.out_shape,
        scratch_shapes=ride.scratch,
        compiler_params=pltpu.CompilerParams(dimension_semantics=("arbitrary", "arbitrary"), vmem_limit_bytes=VMEM_LIMIT,
                                             has_side_effects=True), name=name)(*args, *rode)
    return res[:len(out_specs)], res[len(out_specs):]


def attn_fwd(q, k, v, name, rode=None, modes=None):
    H, T, _ = q.shape

    def body(q_ref, k_ref, v_ref, o_ref, lse_ref):
        def run(nk):
            s = _dotf(q_ref[0], k_ref[0, pl.ds(0, nk), :], "nt")
            m = jnp.max(s, axis=1, keepdims=True)
            p = jnp.exp(s - m)
            l = jnp.sum(p, axis=1, keepdims=True)
            o = jnp.dot(p.astype(BF16), v_ref[0, pl.ds(0, nk), :], preferred_element_type=F32)
            o_ref[0] = o / l
            lse_ref[0] = m + jnp.log(l)

        _by_query_block(run, T)

    return _ride_call(
        body, (H, T // TB),
        [pl.BlockSpec((1, TB, QK), lambda h, i: (h, i, 0)), pl.BlockSpec((1, T, QK), lambda h, i: (h, 0, 0)),
         pl.BlockSpec((1, T, VD), lambda h, i: (h, 0, 0))],
        [pl.BlockSpec((1, TB, VD), lambda h, i: (h, i, 0)), pl.BlockSpec((1, TB, 1), lambda h, i: (h, i, 0))],
        [jax.ShapeDtypeStruct((H, T, VD), F32), jax.ShapeDtypeStruct((H, T, 1), F32)],
        Exchange(rode, modes) if rode else None, rode, name, (q, k, v))


def attn_bwd(q, k, v, o, lse, do, name, rode=None, modes=None):
    H, T, _ = q.shape

    def body(q_ref, k_ref, v_ref, o_ref, lse_ref, do_ref, dq_ref, dk_ref, dv_ref):
        i = pl.program_id(1)

        @pl.when(i == 0)
        def _():
            dk_ref[...] = jnp.zeros_like(dk_ref)
            dv_ref[...] = jnp.zeros_like(dv_ref)

        def run(nk):
            keys = pl.ds(0, nk)
            qv, kv, dov = q_ref[0], k_ref[0, keys, :], do_ref[0]
            p = jnp.exp(_dotf(qv, kv, "nt") - lse_ref[0])
            delta = jnp.sum(dov * o_ref[0], axis=1, keepdims=True)
            dob = dov.astype(BF16)
            dv_ref[0, keys, :] += _dotf(p.astype(BF16), dob, "tn")
            dp = _dotf(dob, v_ref[0, keys, :], "nt")
            ds = (p * (dp - delta)).astype(BF16)
            dq_ref[0] = jnp.dot(ds, kv, preferred_element_type=F32)
            dk_ref[0, keys, :] += _dotf(ds, qv, "tn")

        _by_query_block(run, T)

    blk = lambda c: pl.BlockSpec((1, TB, c), lambda h, i: (h, i, 0))
    full = lambda c: pl.BlockSpec((1, T, c), lambda h, i: (h, 0, 0))
    return _ride_call(
        body, (H, T // TB), [blk(QK), full(QK), full(VD), blk(VD), blk(1), blk(VD)], [blk(QK), full(QK), full(VD)],
        [jax.ShapeDtypeStruct((H, T, QK), F32), jax.ShapeDtypeStruct((H, T, QK), F32), jax.ShapeDtypeStruct((H, T, VD), F32)],
        Exchange(rode, modes) if rode else None, rode, name, (q, k, v, o, lse, do))


def disc_fwd(a_re, a_im, ls, name):
    def body(ar_ref, ai_ref, ls_ref, lr_ref, li_ref, fr_ref, fi_ref):
        ar, ai = ar_ref[...], ai_ref[...]
        dt = jnp.exp(ls_ref[...])
        mag = jnp.exp(ar * dt)
        lr = mag * jnp.cos(ai * dt)
        li = mag * jnp.sin(ai * dt)
        den = ar * ar + ai * ai
        nr = lr - 1.0
        lr_ref[...] = lr
        li_ref[...] = li
        fr_ref[...] = (nr * ar + li * ai) / den
        fi_ref[...] = (li * ar - nr * ai) / den

    return pl.pallas_call(body, out_shape=[jax.ShapeDtypeStruct(a_re.shape, F32)] * 4, name=name)(a_re, a_im, ls)


def disc_b(f_re, f_im, b_re, b_im, name):
    def body(fr_ref, fi_ref, br_ref, bi_ref, or_ref, oi_ref):
        fr, fi, br, bi = fr_ref[...], fi_ref[...], br_ref[...], bi_ref[...]
        or_ref[...] = fr * br - fi * bi
        oi_ref[...] = fr * bi + fi * br

    fs, bs = _disc_b_specs()
    return pl.pallas_call(body, grid=(2, G * P // DISC_ROWS), in_specs=[fs, fs, bs, bs], out_specs=[bs, bs],
                          out_shape=[jax.ShapeDtypeStruct(b_re.shape, F32)] * 2, name=name)(f_re, f_im, b_re, b_im)


DISC_ROWS = 1024


def _disc_b_specs():
    return (pl.BlockSpec((1, DISC_ROWS, 1), lambda d, i: (d, i, 0)), pl.BlockSpec((1, DISC_ROWS, CH), lambda d, i: (d, i, 0)))


def disc_b_bwd(f_re, f_im, b_re, b_im, dbb_re, dbb_im, name):
    def body(fr_ref, fi_ref, br_ref, bi_ref, dr_ref, di_ref, dbr_ref, dbi_ref, dfr_ref, dfi_ref):
        fr, fi, br, bi, dr, di = fr_ref[...], fi_ref[...], br_ref[...], bi_ref[...], dr_ref[...], di_ref[...]
        dbr_ref[...] = fr * dr + fi * di
        dbi_ref[...] = fr * di - fi * dr
        dfr_ref[...] = jnp.sum(dr * br + di * bi, axis=-1, keepdims=True)
        dfi_ref[...] = jnp.sum(di * br - dr * bi, axis=-1, keepdims=True)

    fs, bs = _disc_b_specs()
    return pl.pallas_call(body, grid=(2, G * P // DISC_ROWS), in_specs=[fs, fs, bs, bs, bs, bs], out_specs=[bs, bs, fs, fs],
                          out_shape=[jax.ShapeDtypeStruct(b_re.shape, F32)] * 2 + [jax.ShapeDtypeStruct(f_re.shape, F32)] * 2,
                          name=name)(f_re, f_im, b_re, b_im, dbb_re, dbb_im)


def disc_a_bwd(a_re, a_im, ls, dlr, dli, dfr, dfi, name):
    def body(ar_ref, ai_ref, ls_ref, dlr_ref, dli_ref, dfr_ref, dfi_ref, dar_ref, dai_ref, dls_ref):
        ar, ai = ar_ref[...], ai_ref[...]
        dt = jnp.exp(ls_ref[...])
        mag = jnp.exp(ar * dt)
        cs, sn = jnp.cos(ai * dt), jnp.sin(ai * dt)
        lr, li = mag * cs, mag * sn
        den = ar * ar + ai * ai
        nr = lr - 1.0
        f_re = (nr * ar + li * ai) / den
        f_im = (li * ar - nr * ai) / den
        dn1 = dfr_ref[...] / den
        dn2 = dfi_ref[...] / den
        dden = -(dfr_ref[...] * f_re + dfi_ref[...] * f_im) / den
        dlr_t = dlr_ref[...] + dn1 * ar - dn2 * ai
        dli_t = dli_ref[...] + dn1 * ai + dn2 * ar
        dar = dn1 * nr + dn2 * li + dden * 2.0 * ar
        dai = dn1 * li - dn2 * nr + dden * 2.0 * ai
        dmag = dlr_t * cs + dli_t * sn
        dth = dli_t * lr - dlr_t * li
        dar_ref[...] = dar + dmag * mag * dt
        dai_ref[...] = dai + dth * dt
        dls_ref[...] = jnp.sum(dmag * mag * ar + dth * ai, axis=-1, keepdims=True) * dt

    return pl.pallas_call(body, out_shape=[jax.ShapeDtypeStruct(a_re.shape, F32)] * 2 +
                          [jax.ShapeDtypeStruct(ls.shape, F32)], name=name)(a_re, a_im, ls, dlr, dli, dfr, dfi)


def _cpow(lr, li, n):
    rr, ri = None, None
    br, bi = lr, li
    while n:
        if n & 1:
            if rr is None:
                rr, ri = br, bi
            else:
                rr, ri = rr * br - ri * bi, rr * bi + ri * br
        n >>= 1
        if n:
            br, bi = br * br - bi * bi, 2.0 * br * bi
    return rr, ri


UNROLL = 4


def _seg_scan(xre, xim, lam8, pw, base, seglen, rev, init, fin_re, fin_im, ini_re, ini_im, prev=None):
    lr, li = lam8

    def rows(t):
        return pl.ds(pl.multiple_of(base + t * SEG, SEG), SEG)

    tmap = (lambda n: seglen - 1 - n) if rev else (lambda n: n)
    zero = jnp.zeros((SEG, SB), F32)

    def advance(c, t):
        a, b = c
        return lr * a - li * b + xre[rows(t), :], lr * b + li * a + xim[rows(t), :]

    fin = lax.fori_loop(0, seglen, lambda n, c: advance(c, tmap(n)), (zero, zero), unroll=UNROLL)
    fin_re[...] = fin[0]
    fin_im[...] = fin[1]
    (cr, ci), (pr, pi) = init, pw
    for i in (range(SEG - 1, -1, -1) if rev else range(SEG)):
        ini_re[pl.ds(i, 1), :] = cr
        ini_im[pl.ds(i, 1), :] = ci
        cr, ci = pr * cr - pi * ci + fin_re[pl.ds(i, 1), :], pr * ci + pi * cr + fin_im[pl.ds(i, 1), :]
    start = (ini_re[...], ini_im[...])

    def store(c, t):
        na, nb = advance(c, t)
        xre[rows(t), :] = na
        xim[rows(t), :] = nb
        return na, nb

    if prev is None:
        lax.fori_loop(0, seglen, lambda n, c: store(c, tmap(n)), start, unroll=UNROLL)
        return (cr, ci), None

    sre, sim, s_ini_re, s_ini_im = prev

    def acc_step(c, t, pre, pim):
        na, nb = store(c[:2], t)
        return na, nb, c[2] + na * pre + nb * pim, c[3] + nb * pre - na * pim

    def body(n, c):
        t = tmap(n)
        tp = t - 1 if rev else t + 1
        return acc_step(c, t, sre[rows(tp), :], sim[rows(tp), :])

    c = lax.fori_loop(0, seglen - 1, body, start + (zero, zero), unroll=UNROLL)
    c = acc_step(c, 0 if rev else seglen - 1, s_ini_re[...], s_ini_im[...])
    return (cr, ci), c[2:]


def _lam_tiles(lr, li, lens, conj=False):
    if conj:
        li = -li
    lam8 = (jnp.broadcast_to(lr, (SEG, SB)), jnp.broadcast_to(li, (SEG, SB)))
    return lam8, [_cpow(lr, li, n) for n in lens]


def _stretches(T):
    return ((0, LC // SEG), (LC, (T - LC) // SEG))


def _to_seg_order(src, dst, T):
    for base, seglen in _stretches(T):
        def body(t, carry, base=base, seglen=seglen):
            dst[pl.ds(pl.multiple_of(base + t * SEG, SEG), SEG), :] = src[pl.ds(base + t, SEG, stride=seglen), :]
            return carry
        lax.fori_loop(0, seglen, body, 0, unroll=8)


def _from_seg_order(src, dst, T):
    for base, seglen in _stretches(T):
        def body(t, carry, base=base, seglen=seglen):
            dst[pl.ds(base + t, SEG, stride=seglen), :] = src[pl.ds(pl.multiple_of(base + t * SEG, SEG), SEG), :]
            return carry
        lax.fori_loop(0, seglen, body, 0, unroll=8)


def _scan_specs(T):
    ublk = pl.BlockSpec((T, UB), lambda j: (0, j))
    lam = pl.BlockSpec((2, 1, 1, SB), lambda j: (0, j, 0, 0))
    mat = pl.BlockSpec((2, 1, UB, SB), lambda j: (0, j, 0, 0))
    return ublk, lam, mat


def _dotf(a, b, mode="nn"):
    return lax.dot_general(a, b, _DN[mode], preferred_element_type=F32)


def _zero_state():
    return jnp.zeros((1, SB), F32), jnp.zeros((1, SB), F32)


def scan_fwd(u, lam_re, lam_im, bre, bim, cre, cim, name):
    T = u.shape[0]
    s_ctx, s_lat = LC // SEG, (T - LC) // SEG

    def body(u_ref, lr_ref, li_ref, bre_ref, bim_ref, cre_ref, cim_ref, y_ref, us, ys, sre, sim, fre, fim, ire, iim):
        _to_seg_order(u_ref, us, T)
        ub = us[...].astype(BF16)
        for d in range(2):
            lam8, (pw_c, pw_l) = _lam_tiles(lr_ref[d, 0], li_ref[d, 0], (s_ctx, s_lat))
            sre[...] = _dotf(ub, bre_ref[d, 0].astype(BF16))
            sim[...] = _dotf(ub, bim_ref[d, 0].astype(BF16))
            end_c, _ = _seg_scan(sre, sim, lam8, pw_c, 0, s_ctx, bool(d), _zero_state(), fre, fim, ire, iim)
            _seg_scan(sre, sim, lam8, pw_l, LC, s_lat, bool(d), end_c, fre, fim, ire, iim)
            y = (_dotf(sre[...].astype(BF16), cre_ref[d, 0].astype(BF16), "nt")
                 - _dotf(sim[...].astype(BF16), cim_ref[d, 0].astype(BF16), "nt"))
            if d == 0:
                ys[...] = y
            else:
                ys[...] += y
        _from_seg_order(ys, y_ref, T)

    ublk, lam, mat = _scan_specs(T)
    return pl.pallas_call(
        body, grid=(NJ,), in_specs=[ublk, lam, lam, mat, mat, mat, mat], out_specs=ublk,
        out_shape=jax.ShapeDtypeStruct((T, G * CH), F32),
        scratch_shapes=[pltpu.VMEM((T, UB), F32)] * 2 + [pltpu.VMEM((T, SB), F32)] * 2 + [pltpu.VMEM((SEG, SB), F32)] * 4,
        compiler_params=_cp(("arbitrary",)), name=name)(u, lam_re, lam_im, bre, bim, cre, cim)


def scan_bwd(u, dy, lam_re, lam_im, bre, bim, cre, cim, name):
    T = u.shape[0]
    s_ctx, s_lat = LC // SEG, (T - LC) // SEG

    def body(u_ref, dy_ref, lr_ref, li_ref, bre_ref, bim_ref, cre_ref, cim_ref,
             du_ref, dlr_ref, dli_ref, dbre_ref, dbim_ref, dcre_ref, dcim_ref,
             us, dys, dus, sre, sim, gre, gim, fre, fim, ic_re, ic_im, il_re, il_im, jre, jim):
        _to_seg_order(u_ref, us, T)
        _to_seg_order(dy_ref, dys, T)
        ub, dyb = us[...].astype(BF16), dys[...].astype(BF16)
        for d in range(2):
            rev = bool(d)
            lam8, (pw_c, pw_l) = _lam_tiles(lr_ref[d, 0], li_ref[d, 0], (s_ctx, s_lat))
            cam8, (cw_c, cw_l) = _lam_tiles(lr_ref[d, 0], li_ref[d, 0], (s_ctx, s_lat), conj=True)
            bre_v, bim_v = bre_ref[d, 0].astype(BF16), bim_ref[d, 0].astype(BF16)
            sre[...] = _dotf(ub, bre_v)
            sim[...] = _dotf(ub, bim_v)
            end_c, _ = _seg_scan(sre, sim, lam8, pw_c, 0, s_ctx, rev, _zero_state(), fre, fim, ic_re, ic_im)
            _seg_scan(sre, sim, lam8, pw_l, LC, s_lat, rev, end_c, fre, fim, il_re, il_im)
            gre[...] = _dotf(dyb, cre_ref[d, 0].astype(BF16))
            gim[...] = -_dotf(dyb, cim_ref[d, 0].astype(BF16))
            end_g, acc_l = _seg_scan(gre, gim, cam8, cw_l, LC, s_lat, not rev, _zero_state(), fre, fim, jre, jim,
                                     prev=(sre, sim, il_re, il_im))
            _, acc_c = _seg_scan(gre, gim, cam8, cw_c, 0, s_ctx, not rev, end_g, fre, fim, jre, jim,
                                 prev=(sre, sim, ic_re, ic_im))
            dlr_ref[d, 0] = _sum0(acc_l[0] + acc_c[0])
            dli_ref[d, 0] = _sum0(acc_l[1] + acc_c[1])
            grb, gib = gre[...].astype(BF16), gim[...].astype(BF16)
            du = _dotf(grb, bre_v, "nt") + _dotf(gib, bim_v, "nt")
            if d == 0:
                dus[...] = du
            else:
                dus[...] += du
            dbre_ref[d, 0] = _dotf(ub, grb, "tn")
            dbim_ref[d, 0] = _dotf(ub, gib, "tn")
            dcre_ref[d, 0] = _dotf(dyb, sre[...].astype(BF16), "tn")
            dcim_ref[d, 0] = -_dotf(dyb, sim[...].astype(BF16), "tn")
        _from_seg_order(dus, du_ref, T)

    ublk, lam, mat = _scan_specs(T)
    lam_s = jax.ShapeDtypeStruct(lam_re.shape, F32)
    mat_s = jax.ShapeDtypeStruct(bre.shape, F32)
    return pl.pallas_call(
        body, grid=(NJ,), in_specs=[ublk, ublk, lam, lam, mat, mat, mat, mat],
        out_specs=[ublk, lam, lam, mat, mat, mat, mat],
        out_shape=[jax.ShapeDtypeStruct((T, G * CH), F32), lam_s, lam_s, mat_s, mat_s, mat_s, mat_s],
        scratch_shapes=[pltpu.VMEM((T, UB), F32)] * 3 + [pltpu.VMEM((T, SB), F32)] * 4 + [pltpu.VMEM((SEG, SB), F32)] * 8,
        compiler_params=_cp(("arbitrary",)), name=name)(u, dy, lam_re, lam_im, bre, bim, cre, cim)


def _block_diag(m):
    m5 = m.reshape(2, NJ, GB, CH, P)
    eye = jnp.eye(GB, dtype=m.dtype)
    return (m5[:, :, :, :, None, :] * eye[None, None, :, None, :, None]).reshape(2, NJ, UB, SB)


def _diag_blocks(m):
    m6 = m.reshape(2, NJ, GB, CH, GB, P)
    idx = jnp.arange(GB)
    return m6[:, :, idx, :, idx, :].transpose(1, 2, 0, 3, 4).reshape(2, G, CH, P)


class Exchange:
    def __init__(self, xs, modes):
        self.n = len(xs)
        self.modes = [modes] * self.n if isinstance(modes, (str, int)) else list(modes)
        self.out_shape = [jax.ShapeDtypeStruct(self._shape(x, md), x.dtype) for x, md in zip(xs, self.modes)]
        self.scratch = [pltpu.SemaphoreType.DMA((NDEV - 1, self.n)), pltpu.SemaphoreType.DMA((NDEV - 1, self.n)),
                        pltpu.SemaphoreType.DMA((self.n,))]
        self.specs = [pl.BlockSpec(memory_space=pl.ANY)] * self.n

    @staticmethod
    def _shape(x, mode):
        if mode == "gather":
            return (NDEV,) + tuple(x.shape)
        return tuple(x.shape) if mode == "lead" else (NDEV, x.shape[0], mode) + tuple(x.shape[2:])

    @staticmethod
    def _piece(x_ref, mode, dev):
        if mode == "gather":
            return x_ref
        return x_ref.at[dev] if mode == "lead" else x_ref.at[:, pl.ds(dev * mode, mode)]

    def _copies(self, x_refs, out_refs, sems):
        send_sems, recv_sems, local_sems = sems
        mx, my, mc = lax.axis_index("x"), lax.axis_index("y"), lax.axis_index("c")
        me = 4 * mx + 2 * my + mc
        local = [pltpu.make_async_copy(self._piece(x_ref, self.modes[a], me), out_ref.at[me], local_sems.at[a])
                 for a, (x_ref, out_ref) in enumerate(zip(x_refs, out_refs))]
        sends, recvs = [], []
        for k in range(1, NDEV):
            peer = (1 - mx if k & 4 else mx, 1 - my if k & 2 else my, 1 - mc if k & 1 else mc)
            pid = 4 * peer[0] + 2 * peer[1] + peer[2]
            for a, (x_ref, out_ref) in enumerate(zip(x_refs, out_refs)):
                src = self._piece(x_ref, self.modes[a], pid)
                sems_k = dict(send_sem=send_sems.at[k - 1, a], recv_sem=recv_sems.at[k - 1, a], device_id=peer,
                              device_id_type=MESH_T)
                sends.append(pltpu.make_async_remote_copy(src_ref=src, dst_ref=out_ref.at[me], **sems_k))
                recvs.append(pltpu.make_async_remote_copy(src_ref=src, dst_ref=out_ref.at[pid], **sems_k))
        return local, sends, recvs

    def start(self, x_refs, out_refs, sems):
        local, sends, _ = self._copies(x_refs, out_refs, sems)
        for cp in local + sends:
            cp.start()

    def finish(self, x_refs, out_refs, sems):
        local, sends, recvs = self._copies(x_refs, out_refs, sems)
        for cp in recvs:
            cp.wait_recv()
        for cp in sends:
            cp.wait_send()
        for cp in local:
            cp.wait()


def exchange(xs, modes, name):
    ex = Exchange(xs, modes)
    n = ex.n

    def body(*refs):
        ex.start(refs[:n], refs[n:2 * n], refs[2 * n:])
        ex.finish(refs[:n], refs[n:2 * n], refs[2 * n:])

    return pl.pallas_call(body, in_specs=ex.specs, out_specs=ex.specs, out_shape=ex.out_shape, scratch_shapes=ex.scratch,
                          compiler_params=pltpu.CompilerParams(has_side_effects=True), name=name)(*xs)


def _dot_f32(a, b, dn):
    return lax.dot_general(a, b, dn, preferred_element_type=F32, precision=lax.Precision.HIGHEST)


def ada_fwd(cg, c_ctx, ada_w, ada_b_loc, name):
    W = ada_w.shape[2]

    def body(cg_ref, cc_ref, w_ref, b_ref, o_ref):
        a = jnp.concatenate([_silu(cg_ref[...]), jnp.broadcast_to(_silu(cc_ref[...]), (NDEV, D))], axis=0)
        for i in range(2):
            o_ref[i] = _dot_f32(a, w_ref[i], _DN["nn"]) + b_ref[i]

    return pl.pallas_call(body, out_shape=jax.ShapeDtypeStruct((2, 2 * NDEV, W), F32),
                          compiler_params=_cp(), name=name)(cg, c_ctx, ada_w, ada_b_loc)


def ada_bwd(cg, c_ctx, ada_w, dm_loc, dm_all, name):
    W = ada_w.shape[2]

    def body(cg_ref, cc_ref, w_ref, dl_ref, da_ref, gw_ref, dcc_ref, gb_ref):
        a = jnp.concatenate([_silu(cg_ref[...]), jnp.broadcast_to(_silu(cc_ref[...]), (NDEV, D))], axis=0)
        dcc = jnp.zeros((1, D), F32)
        for i in range(2):
            dl = dl_ref[i]
            gw_ref[i] = _dot_f32(a, dl, _DN["tn"])
            dctx = jnp.sum(dl[NDEV:], axis=0, keepdims=True)
            dcc = dcc + _dot_f32(dctx, w_ref[i], _DN["nt"])
        dcc_ref[...] = dcc
        gb_ref[...] = jnp.sum(da_ref[...], axis=0)

    return pl.pallas_call(body, out_shape=[jax.ShapeDtypeStruct((2, D, W), F32), jax.ShapeDtypeStruct((1, D), F32),
                                           jax.ShapeDtypeStruct((2, 3 * D), F32)],
                          compiler_params=_cp(), name=name)(cg, c_ctx, ada_w, dm_loc, dm_all)


def cctx_finish(parts, c_ctx, name):
    def body(p_ref, cc_ref, o_ref):
        o_ref[...] = jnp.sum(p_ref[...], axis=0, keepdims=True) * _dsilu(cc_ref[...])

    return pl.pallas_call(body, out_shape=jax.ShapeDtypeStruct((1, D), F32), name=name)(parts, c_ctx)


def _adamw_update(g_ref, w_ref, m_ref, v_ref, go_ref, d_ref, mo_ref, vo_ref):
    g = g_ref[0].astype(F32)
    for s in range(1, g_ref.shape[0]):
        g = g + g_ref[s].astype(F32)
    mn = B1 * m_ref[...] + (1.0 - B1) * g
    vn = B2 * v_ref[...] + (1.0 - B2) * g * g
    go_ref[...] = g
    mo_ref[...] = mn
    vo_ref[...] = vn
    d_ref[...] = -LR * ((mn * (1.0 / (1.0 - B1 ** STEP))) / (jnp.sqrt(vn * (1.0 / (1.0 - B2 ** STEP))) + AEPS) + WD * w_ref[...])


def adamw(gstack, w, m, v, name, tr=256):
    n, R, C = gstack.shape
    tr = max(t for t in range(8, min(tr, R) + 1, 8) if R % t == 0)
    spec = pl.BlockSpec((tr, C), lambda i: (i, 0))
    return pl.pallas_call(_adamw_body(1), grid=(R // tr,),
                          in_specs=[pl.BlockSpec((n, tr, C), lambda i: (0, i, 0)), spec, spec, spec],
                          out_specs=[spec] * 4, out_shape=[jax.ShapeDtypeStruct((R, C), F32)] * 4,
                          compiler_params=_cp(("parallel",)), name=name)(gstack, w, m, v)


def _adamw_body(k):
    def body(*refs):
        for t in range(k):
            _adamw_update(*refs[4 * t:4 * t + 4], *refs[4 * k + 4 * t:4 * k + 4 * t + 4])
    return body


def adamw_multi(items, grid, name):
    k = len(items)
    ins, in_specs, out_specs, out_shape = [], [], [], []
    for g, g_spec, w, m, v, w_spec in items:
        ins += [g, w, m, v]
        in_specs += [g_spec, w_spec, w_spec, w_spec]
    for g, g_spec, w, m, v, w_spec in items:
        out_specs += [w_spec] * 4
        out_shape += [jax.ShapeDtypeStruct(w.shape, F32)] * 4
    res = pl.pallas_call(_adamw_body(k), grid=grid, in_specs=in_specs, out_specs=out_specs, out_shape=out_shape,
                         compiler_params=_cp(("arbitrary",) * len(grid)), name=name)(*ins)
    return [res[4 * t:4 * t + 4] for t in range(k)]


def _whole(a, grid_rank):
    zeros = (0,) * a.ndim
    return pl.BlockSpec(a.shape, lambda *idx: zeros)


def sum_slots(xs, name):
    def body(*refs):
        for x_ref, o_ref in zip(refs[:len(xs)], refs[len(xs):]):
            acc = x_ref[0]
            for s in range(1, NDEV):
                acc = acc + x_ref[s]
            o_ref[...] = acc

    return pl.pallas_call(body, out_shape=[jax.ShapeDtypeStruct(x.shape[1:], F32) for x in xs],
                          compiler_params=_cp(), name=name)(*xs)


def _col_shards(g):
    R, N = g.shape
    return g.reshape(R, NDEV, N // NDEV).transpose(1, 0, 2)


def _vec2(v):
    return jnp.broadcast_to(v.reshape(1, 1, -1), (2, 1, v.size))


SHARD_ROWS = {"mla_w_in": 192, "mla_w_uq": 192, "mla_w_ukv": 256, "s5_w_in": 256}


def _t_shard(wsh, rows):
    t = wsh[0].T.astype(BF16)
    return jnp.pad(t, ((0, rows - t.shape[0]), (0, 0)))


def _win_order():
    w = IN_W // NDEV
    perm = np.zeros((IN_WP, NDEV * SHARD_ROWS["mla_w_in"]), np.float32)
    first = QL + KVL + ROPE
    for c in range(IN_W):
        n = c + HEADS * VD if c < first else c - first
        perm[n, (c // w) * SHARD_ROWS["mla_w_in"] + c % w] = 1.0
    return jnp.asarray(perm, BF16)


def local_step(ctx, x, tgt, mod, Wt, small, l1_shards):
    T = LC + x.shape[0]
    xa = ("cat", ctx, x)
    sh = [mod[i, :, None, 0:D] for i in range(2)]
    sc = [mod[i, :, None, D:2 * D] for i in range(2)]
    gt = [mod[i, :, None, 2 * D:] for i in range(2)]
    ng = [_vec2(small["norm_g"][i]) for i in range(2)]
    qg, kvg = _vec2(small["mla_q_norm"]), _vec2(small["mla_kv_norm"])
    cosf, sinf, pm, pmt = _rope_tables(T)

    (h0,), _ = rowwise(st_norm_mod, [xa], [ng[0], sc[0], sh[0]], [(D, BF16)], [], "l0_norm")
    p0 = mm(h0, Wt["mla_w_in"], "nt", "l0_in")
    z0, cq, ckv = (p0, 0, HEADS * VD), (p0, HEADS * VD // QL, QL), (p0, (HEADS * VD + QL) // KVL, KVL)
    (cqn,), _ = rowwise(st_rms, [cq], [qg], [(QL, BF16)], [], "l0_qnorm")
    (ckvn,), _ = rowwise(st_rms, [ckv], [kvg], [(KVL, BF16)], [], "l0_kvnorm")
    qh = mm(cqn, Wt["mla_w_uq"], "nt", "l0_uq", tn=QK, shard_out=True)
    kvh = mm(ckvn, Wt["mla_w_ukv"], "nt", "l0_ukv", tn=KVW, shard_out=True)
    Q = rope(qh, cosf, sinf, pm, False, BF16, "l0_rope_q", scale=SCALE)
    K, V = assemble_kv(kvh, p0, (HEADS * VD + QL + KVL) // 128, "l0_kv")
    (o, lse), got = attn_fwd(Q, K, V, "l0_attn", rode=l1_shards, modes="gather")
    Wt, small = dict(Wt), dict(small)
    for n, a in zip(L1_BIG, got):
        Wt[n] = a.reshape(-1, a.shape[-1])
    vecs = lax.bitcast_convert_type(got[-1].reshape(NDEV, 2, -1, 2), F32)
    small["s5_d"], small["s5_b_glu"] = vecs[:, 0, :].reshape(D), vecs[:, 1, :].reshape(D)
    o2 = o.transpose(1, 0, 2).reshape(T, HEADS * VD)
    (og,), _ = rowwise(st_gate, [o2, z0], [], [(D, BF16)], [], "l0_gate")
    out0 = mm(og, Wt["mla_w_out"], "nn", "l0_out")
    (x1,), _ = rowwise(st_resid, [xa, out0], [gt[0]], [(D, F32)], [], "l0_resid")

    ls = small["s5_log_step"].reshape(2, G, 1)
    a_re, a_im = small["s5_a_re"].reshape(2, G, P), small["s5_a_im"].reshape(2, G, P)
    b_re, b_im = small["s5_b_re"].reshape(2, G * P, CH), small["s5_b_im"].reshape(2, G * P, CH)
    lam_re, lam_im, f_re, f_im = disc_fwd(a_re, a_im, ls, "s5_disc")
    f_re2, f_im2 = f_re.reshape(2, G * P, 1), f_im.reshape(2, G * P, 1)
    bb_re, bb_im = disc_b(f_re2, f_im2, b_re, b_im, "s5_disc_b")
    bre = _block_diag(bb_re.reshape(2, G, P, CH).transpose(0, 1, 3, 2))
    bim = _block_diag(bb_im.reshape(2, G, P, CH).transpose(0, 1, 3, 2))
    cre = _block_diag(small["s5_c_re"].reshape(2, G, CH, P))
    cim = _block_diag(small["s5_c_im"].reshape(2, G, CH, P))
    lam_re4, lam_im4 = lam_re.reshape(2, NJ, 1, SB), lam_im.reshape(2, NJ, 1, SB)

    (h1,), _ = rowwise(st_norm_mod, [x1], [ng[1], sc[1], sh[1]], [(D, BF16)], [], "l1_norm")
    p1 = mm(h1, Wt["s5_w_in"], "nt", "l1_in")
    u, z1 = (p1, 0, D), (p1, 1, D)
    yssm = scan_fwd(p1, lam_re4, lam_im4, bre, bim, cre, cim, "s5_scan")
    dvec, bglu = _vec2(small["s5_d"]), _vec2(small["s5_b_glu"])
    (y, y1b), _ = rowwise(st_s5a, [yssm, u], [dvec], [(D, F32), (D, BF16)], [], "l1_gelu")
    gl = mm(y1b, Wt["s5_w_glu"], "nn", "l1_glu")
    (y3,), _ = rowwise(st_s5b, [y, gl, z1], [bglu], [(D, BF16)], [], "l1_gate")
    out1 = mm(y3, Wt["s5_w_out"], "nn", "l1_out")
    (x2,), _ = rowwise(st_resid, [x1, out1], [gt[1]], [(D, F32)], [], "l1_resid")

    fg = _vec2(small["final_g"])
    lat_mask = jnp.stack([jnp.zeros((1, D), F32), jnp.ones((1, D), F32)])
    (dx2,), (dfg, lvec) = rowwise(st_final, [x2, ("lat", tgt)], [fg, lat_mask], [(D, F32)], [D, 128], "final")

    (dout1,), (dgt1,) = rowwise(st_resid_bwd, [dx2, out1], [gt[1]], [(D, BF16)], [D], "l1_resid_b")
    g_w_out5 = mm(y3, dout1, "tn", "l1_out_dw", out_dtype=BF16)
    dy3 = mm(dout1, Wt["s5_w_out"], "nt", "l1_out_dx")
    (dgl, dz1, dy1a), (dbglu,) = rowwise(st_s5b_bwd, [dy3, y, gl, z1], [bglu], [(D, BF16), (D, BF16), (D, F32)], [D], "l1_gate_b")
    g_w_glu = mm(y1b, dgl, "tn", "l1_glu_dw", out_dtype=BF16)
    dy1b = mm(dgl, Wt["s5_w_glu"], "nt", "l1_glu_dx")
    (dy, du_d), (dd,) = rowwise(st_s5a_bwd, [dy1a, dy1b, y, u], [dvec], [(D, F32), (D, F32)], [D], "l1_gelu_b")
    du_s, dlr, dli, dbre, dbim, dcre, dcim = scan_bwd(p1, dy, lam_re4, lam_im4, bre, bim, cre, cim, "s5_scan_b")
    du = du_d + du_s
    dbb_re = _diag_blocks(dbre).transpose(0, 1, 3, 2).reshape(2, G * P, CH)
    dbb_im = _diag_blocks(dbim).transpose(0, 1, 3, 2).reshape(2, G * P, CH)
    g_c_re, g_c_im = _diag_blocks(dcre), _diag_blocks(dcim)
    g_b_re, g_b_im, dfr, dfi = disc_b_bwd(f_re2, f_im2, b_re, b_im, dbb_re, dbb_im, "s5_disc_b_b")
    g_a_re, g_a_im, g_ls = disc_a_bwd(a_re, a_im, ls, dlr.reshape(2, G, P), dli.reshape(2, G, P),
                                      dfr.reshape(2, G, P), dfi.reshape(2, G, P), "s5_disc_b_a")
    dp1 = jnp.concatenate([du.astype(BF16), dz1], axis=1)
    g_w_in5 = mm(h1, dp1, "tn", "l1_in_dw", out_dtype=BF16, tm=D, tn=2 * D // NDEV, shard_out=True)
    dh1 = mm(dp1, Wt["s5_w_in"], "nn", "l1_in_dx")
    (dx1,), (dsh1, dsc1, dng1) = rowwise(st_norm_mod_bwd, [x1, dh1, dx2], [ng[1], sc[1]], [(D, F32)], [D, D, D], "l1_norm_b")

    (dout0,), (dgt0,) = rowwise(st_resid_bwd, [dx1, out0], [gt[0]], [(D, BF16)], [D], "l0_resid_b")
    g_w_out = mm(og, dout0, "tn", "l0_out_dw", out_dtype=BF16)
    dog = mm(dout0, Wt["mla_w_out"], "nt", "l0_out_dx")
    (do2, dz0), _ = rowwise(st_gate_bwd, [dog, o2, z0], [], [(D, F32), (D, F32)], [], "l0_gate_b")
    doh = do2.reshape(T, HEADS, VD).transpose(1, 0, 2)
    rows8 = lambda g: g.reshape(NDEV, -1, g.shape[-1])
    both = lambda s: s[0, 0] + s[1, 0]
    dense = lambda g: g.reshape(2, G * P * CH // 128, 128)
    chunks = [dense(g_b_re), dense(g_b_im), g_c_re, g_c_im]
    l1_send = [g_w_in5, rows8(g_w_glu), rows8(g_w_out5), both(dd).reshape(NDEV, 1, -1), both(dbglu).reshape(NDEV, 1, -1)]
    (dQ, dK, dV), l1_recv = attn_bwd(Q, K, V, o, lse, doh, "l0_attn_b", rode=l1_send + chunks,
                                     modes=["lead"] * len(l1_send) + [a.shape[1] // NDEV for a in chunks])
    dqh = rope(dQ, cosf, sinf, pmt, True, BF16, "l0_rope_q_b", scale=SCALE)
    dq = dqh.transpose(1, 0, 2).reshape(T, HEADS * QK)
    dkv, dkr = split_kv_grads(dK, dV, "l0_kv_b")
    g_w_uq = _col_shards(mm(cqn, dq, "tn", "l0_uq_dw", out_dtype=BF16))
    dcqn = mm(dq, Wt["mla_w_uq"], "nn", "l0_uq_dx")
    g_w_ukv = mm(ckvn, dkv, "tn", "l0_ukv_dw", out_dtype=BF16, tm=KVL, tn=HEADS * (NOPE + VD) // NDEV, shard_out=True)
    dckvn = mm(dkv, Wt["mla_w_ukv"], "nn", "l0_ukv_dx")
    (dcq,), (dqg,) = rowwise(st_rms_bwd, [cq, dcqn], [qg], [(QL, F32)], [QL], "l0_qnorm_b")
    (dckv,), (dkvg,) = rowwise(st_rms_bwd, [ckv, dckvn], [kvg], [(KVL, F32)], [KVL], "l0_kvnorm_b")
    dp0 = jnp.concatenate([dz0, dcq, dckv, dkr], axis=1).astype(BF16)
    g_p = mm(h0, dp0, "tn", "l0_in_dw", out_dtype=BF16)
    g_w_in = _col_shards(jnp.concatenate([g_p[:, HEADS * VD:IN_W], g_p[:, :HEADS * VD]], axis=1))
    dh0 = mm(dp0, Wt["mla_w_in"], "nn", "l0_in_dx")
    (grad_x,), (dsh0, dsc0, dng0) = rowwise(st_norm_mod_bwd, [xa, dh0, dx1], [ng[0], sc[0]], [(D, F32, "lat")], [D, D, D], "l0_norm_b")

    dmod = jnp.stack([jnp.concatenate([dsh0, dsc0, dgt0], axis=-1)[:, 0], jnp.concatenate([dsh1, dsc1, dgt1], axis=-1)[:, 0]])
    gbig = {"mla_w_in": g_w_in, "mla_w_uq": g_w_uq, "mla_w_ukv": g_w_ukv, "mla_w_out": rows8(g_w_out)}
    gsmall = {"norm_g": jnp.stack([both(dng0), both(dng1)]), "mla_q_norm": both(dqg), "mla_kv_norm": both(dkvg),
              "s5_a_re": g_a_re, "s5_a_im": g_a_im, "s5_log_step": g_ls, "final_g": dfg[1, 0]}
    return lvec[1], grad_x, dmod, gbig, gsmall, l1_recv


COL_SHARDED = ("mla_w_in", "mla_w_uq", "mla_w_ukv", "s5_w_in")
ROW_SHARDED = ("mla_w_out", "s5_w_glu", "s5_w_out")
VEC_SHARDED = ("s5_d", "s5_b_glu")
BIG = COL_SHARDED + ROW_SHARDED
L0_BIG = ("mla_w_in", "mla_w_uq", "mla_w_ukv", "mla_w_out")
L1_BIG = ("s5_w_in", "s5_w_glu", "s5_w_out")
BITS16 = jnp.bfloat16
SMALL_RS = ("norm_g", "mla_q_norm", "mla_kv_norm", "s5_a_re", "s5_a_im", "s5_log_step", "s5_b_re", "s5_b_im",
            "s5_c_re", "s5_c_im", "final_g")
CHUNKED = ("s5_b_re", "s5_b_im", "s5_c_re", "s5_c_im")
DENSE = ("s5_b_re", "s5_b_im")
TINY = ("norm_g", "mla_q_norm", "mla_kv_norm", "s5_a_re", "s5_a_im", "s5_log_step", "final_g")
ORDER = ("c_ctx", "ada_w", "ada_b", "norm_g", "mla_w_in", "mla_q_norm", "mla_w_uq", "mla_kv_norm", "mla_w_ukv",
         "mla_w_out", "s5_w_in", "s5_a_re", "s5_a_im", "s5_log_step", "s5_b_re", "s5_b_im", "s5_c_re", "s5_c_im",
         "s5_d", "s5_w_glu", "s5_b_glu", "s5_w_out", "final_g")


def kernel(x, c, ctx, c_ctx, ada_w, ada_b, norm_g, mla_w_in, mla_q_norm, mla_w_uq, mla_kv_norm, mla_w_ukv, mla_w_out, s5_w_in, s5_a_re, s5_a_im, s5_log_step, s5_b_re, s5_b_im, s5_c_re, s5_c_im, s5_d, s5_w_glu, s5_b_glu, s5_w_out, final_g, loss_target, m_c_ctx, m_ada_w, m_ada_b, m_norm_g, m_mla_w_in, m_mla_q_norm, m_mla_w_uq, m_mla_kv_norm, m_mla_w_ukv, m_mla_w_out, m_s5_w_in, m_s5_a_re, m_s5_a_im, m_s5_log_step, m_s5_b_re, m_s5_b_im, m_s5_c_re, m_s5_c_im, m_s5_d, m_s5_w_glu, m_s5_b_glu, m_s5_w_out, m_final_g, v_c_ctx, v_ada_w, v_ada_b, v_norm_g, v_mla_w_in, v_mla_q_norm, v_mla_w_uq, v_mla_kv_norm, v_mla_w_ukv, v_mla_w_out, v_s5_w_in, v_s5_a_re, v_s5_a_im, v_s5_log_step, v_s5_b_re, v_s5_b_im, v_s5_c_re, v_s5_c_im, v_s5_d, v_s5_w_glu, v_s5_b_glu, v_s5_w_out, v_final_g):
    w = dict(c_ctx=c_ctx, ada_w=ada_w, ada_b=ada_b, norm_g=norm_g, mla_w_in=mla_w_in, mla_q_norm=mla_q_norm,
             mla_w_uq=mla_w_uq, mla_kv_norm=mla_kv_norm, mla_w_ukv=mla_w_ukv, mla_w_out=mla_w_out, s5_w_in=s5_w_in,
             s5_a_re=s5_a_re, s5_a_im=s5_a_im, s5_log_step=s5_log_step, s5_b_re=s5_b_re, s5_b_im=s5_b_im,
             s5_c_re=s5_c_re, s5_c_im=s5_c_im, s5_d=s5_d, s5_w_glu=s5_w_glu, s5_b_glu=s5_b_glu, s5_w_out=s5_w_out,
             final_g=final_g)
    m = dict(c_ctx=m_c_ctx, ada_w=m_ada_w, ada_b=m_ada_b, norm_g=m_norm_g, mla_w_in=m_mla_w_in, mla_q_norm=m_mla_q_norm,
             mla_w_uq=m_mla_w_uq, mla_kv_norm=m_mla_kv_norm, mla_w_ukv=m_mla_w_ukv, mla_w_out=m_mla_w_out,
             s5_w_in=m_s5_w_in, s5_a_re=m_s5_a_re, s5_a_im=m_s5_a_im, s5_log_step=m_s5_log_step, s5_b_re=m_s5_b_re,
             s5_b_im=m_s5_b_im, s5_c_re=m_s5_c_re, s5_c_im=m_s5_c_im, s5_d=m_s5_d, s5_w_glu=m_s5_w_glu,
             s5_b_glu=m_s5_b_glu, s5_w_out=m_s5_w_out, final_g=m_final_g)
    v = dict(c_ctx=v_c_ctx, ada_w=v_ada_w, ada_b=v_ada_b, norm_g=v_norm_g, mla_w_in=v_mla_w_in, mla_q_norm=v_mla_q_norm,
             mla_w_uq=v_mla_w_uq, mla_kv_norm=v_mla_kv_norm, mla_w_ukv=v_mla_w_ukv, mla_w_out=v_mla_w_out,
             s5_w_in=v_s5_w_in, s5_a_re=v_s5_a_re, s5_a_im=v_s5_a_im, s5_log_step=v_s5_log_step, s5_b_re=v_s5_b_re,
             s5_b_im=v_s5_b_im, s5_c_re=v_s5_c_re, s5_c_im=v_s5_c_im, s5_d=v_s5_d, s5_w_glu=v_s5_w_glu,
             s5_b_glu=v_s5_b_glu, s5_w_out=v_s5_w_out, final_g=v_final_g)

    me = 4 * lax.axis_index("x") + 2 * lax.axis_index("y") + lax.axis_index("c")
    WA = ada_w.shape[2]

    cg = exchange([c], "gather", "gather_c")[0].reshape(NDEV, D)
    cc2 = c_ctx.reshape(1, D)
    ada_b_loc = lax.dynamic_slice_in_dim(ada_b.reshape(2, 3 * D // WA, WA), me, 1, axis=1)
    part = ada_fwd(cg, cc2, ada_w, ada_b_loc, "ada_fwd")
    pg = exchange([part], "gather", "gather_mod")[0]
    mod_l = lax.dynamic_index_in_dim(pg, me, axis=2, keepdims=False).transpose(1, 0, 2).reshape(2, 3 * D)
    mod_c = pg[:, :, NDEV, :].transpose(1, 0, 2).reshape(2, 3 * D)
    mod = jnp.stack([mod_c, mod_l], axis=1)

    def shard(n):
        return _t_shard(w[n], SHARD_ROWS[n]) if n in COL_SHARDED else w[n][0].astype(BF16)

    wgot = exchange([shard(n) for n in L0_BIG], "gather", "gather_w")
    Wt = {n: a.reshape(-1, a.shape[-1]) for n, a in zip(L0_BIG, wgot)}
    Wt["mla_w_in"] = mm(_win_order(), Wt["mla_w_in"], "nn", "w_in_order", out_dtype=BF16)
    vec_bits = lax.bitcast_convert_type(jnp.concatenate([s5_d, s5_b_glu], axis=0), BITS16).reshape(2, -1)
    small = {n: w[n] for n in SMALL_RS}

    lvec, grad_x, dmod, gbig, gsmall, l1_recv = local_step(ctx[0], x[0], loss_target[0], mod, Wt, small,
                                                           [shard(n) for n in L1_BIG] + [vec_bits])
    loss = lax.psum(lvec[0, 0], ("x", "y", "c"))
    grad_x = grad_x[None]

    per_dev = G // NDEV
    recv = dict(zip(L0_BIG, exchange([gbig[n] for n in L0_BIG], "lead", "scatter_grads")))
    recv.update(dict(zip(L1_BIG + VEC_SHARDED, l1_recv)))
    out = {}

    def keep(n, res):
        for key, arr in zip("gdmv", res):
            out[key, n] = arr.reshape(w[n].shape)

    for n in BIG:
        keep(n, adamw(recv[n], w[n][0], m[n][0], v[n][0], "adamw_" + n))
    reduced = sum_slots(l1_recv[len(L1_BIG + VEC_SHARDED):], "sum_chunks")

    kshape = lambda n: w[n].shape if w[n].ndim > 1 else (1, w[n].size)
    got = exchange(list(reduced) + [gsmall[n].reshape(kshape(n)) for n in TINY] + [dmod], "gather", "gather_small")
    chunk_all, tiny_all, dm_all = got[:len(CHUNKED)], got[len(CHUNKED):-1], got[-1]

    dm_cols = lax.dynamic_slice_in_dim(dm_all.reshape(NDEV, 2, 2, 3 * D // WA, WA), me, 1, axis=3)[:, :, :, 0]
    dm_loc = jnp.concatenate([dm_cols[:, :, 1].transpose(1, 0, 2), dm_cols[:, :, 0].transpose(1, 0, 2)], axis=1)
    g_ada_w, dcc_part, g_ada_b = ada_bwd(cg, cc2, ada_w, dm_loc, dm_all.transpose(0, 2, 1, 3).reshape(2 * NDEV, 2, 3 * D), "ada_bwd")
    dcc_all = exchange([dcc_part], "gather", "gather_dcc")[0].reshape(NDEV, D)
    g_c_ctx = cctx_finish(dcc_all, cc2, "cctx_finish")

    flat2 = lambda t: t.reshape(-1, t.shape[-1])
    keep("ada_w", adamw(flat2(g_ada_w)[None], flat2(ada_w), flat2(m_ada_w), flat2(v_ada_w), "adamw_ada"))
    items = []
    for n, g in zip(CHUNKED, chunk_all):
        blk = (1, 1, per_dev) + w[n].shape[3:]
        if n in DENSE:
            g = g.transpose(1, 0, 2, 3).reshape(w[n].shape)
            g_spec = pl.BlockSpec((1, 1, 1) + blk[2:], lambda d, s: (0, 0, d, s, 0, 0))
        else:
            g_spec = pl.BlockSpec((1, 1, 1) + blk[2:], lambda d, s: (0, s, d, 0, 0, 0))
        items.append((g[None], g_spec, w[n], m[n], v[n], pl.BlockSpec(blk, lambda d, s: (0, d, s, 0, 0))))
    for n, res in zip(CHUNKED, adamw_multi(items, (2, NDEV), "adamw_bc")):
        keep(n, res)
    tiny_g = dict(zip(TINY, tiny_all))
    tiny_g.update({n: recv[n] for n in VEC_SHARDED})
    tiny_g["c_ctx"], tiny_g["ada_b"] = g_c_ctx[None], g_ada_b[None]
    names = list(tiny_g)
    items = [(tiny_g[n], _whole(tiny_g[n], 1)) + tuple(t[n].reshape(kshape(n)) for t in (w, m, v))
             + (pl.BlockSpec(kshape(n), lambda i, r=len(kshape(n)): (0,) * r),) for n in names]
    for n, res in zip(names, adamw_multi(items, (1,), "adamw_small")):
        keep(n, res)

    return (loss, grad_x, *[out["g", n] for n in ORDER], *[out["d", n] for n in ORDER],
            *[out["m", n] for n in ORDER], *[out["v", n] for n in ORDER])
```

```python
import math

import numpy as np
import jax
import jax.numpy as jnp
from jax import lax
from jax.experimental import pallas as pl
from jax.experimental.pallas import tpu as pltpu

F32 = jnp.float32
BF16 = jnp.bfloat16

D = 1024
L = 2048
LC = 256
NDEV = 8
GRID_W = 64
EPS = 1e-6
HEADS = 16
NOPE = 64
ROPE = 32
QK = NOPE + ROPE
VD = 64
IN_W = 256 + 128 + ROPE + HEADS * 64
IN_WP = 1536
QL = 256
KVL = 128
SCALE = QK ** -0.5
THETA = 10000.0
G = 64
P = 64
CH = 16
GB = 8
NJ = G // GB
UB = GB * CH
SB = GB * P
SEG = 8
TB = 256
VMEM_LIMIT = 56 * 1024 * 1024
B1, B2, LR, AEPS, WD, STEP = 0.9, 0.999, 0.001, 1e-8, 0.01, 10
MESH_T = pl.DeviceIdType.MESH


def _cp(sem=None):
    return pltpu.CompilerParams(dimension_semantics=sem, vmem_limit_bytes=VMEM_LIMIT)


def _sig(x):
    return 1.0 / (1.0 + jnp.exp(-x))


def _silu(x):
    return x * _sig(x)


def _dsilu(x):
    s = _sig(x)
    return s * (1.0 + x * (1.0 - s))


_GK = math.sqrt(2.0 / math.pi)


def _gelu(x):
    return 0.5 * x * (1.0 + jnp.tanh(_GK * (x + 0.044715 * x * x * x)))


def _dgelu(x):
    t = jnp.tanh(_GK * (x + 0.044715 * x * x * x))
    return 0.5 * (1.0 + t) + 0.5 * x * (1.0 - t * t) * _GK * (1.0 + 3 * 0.044715 * x * x)


def _rs(x):
    return lax.rsqrt(jnp.mean(x * x, axis=-1, keepdims=True) + EPS)


def _sum0(x):
    return jnp.sum(x, axis=0, keepdims=True)


def st_norm_mod(x, g, sc, sh):
    y = x * _rs(x) * g
    return (y * (1.0 + sc) + sh,), ()


def st_norm_mod_bwd(x, dh, dres, g, sc):
    r = _rs(x)
    xn = x * r
    y = xn * g
    dy = dh * (1.0 + sc)
    dxn = dy * g
    dx = r * (dxn - xn * jnp.mean(dxn * xn, axis=-1, keepdims=True))
    return (dres + dx,), (_sum0(dh), _sum0(dh * y), _sum0(dy * xn))


def st_rms(x, g):
    return (x * _rs(x) * g,), ()


def st_rms_bwd(x, dy, g):
    r = _rs(x)
    n = x * r
    dn = dy * g
    dx = r * (dn - n * jnp.mean(dn * n, axis=-1, keepdims=True))
    return (dx,), (_sum0(dy * n),)


def st_gate(o, z):
    return (o * _silu(z),), ()


def st_gate_bwd(dog, o, z):
    return (dog * _silu(z), dog * o * _dsilu(z)), ()


def st_resid(x, out, gt):
    return (x + gt * out,), ()


def st_resid_bwd(dx, out, gt):
    return (dx * gt,), (_sum0(dx * out),)


def st_s5a(yssm, u, d):
    y = yssm + d * u
    return (y, _gelu(y)), ()


def st_s5b(y, gl, z, b):
    return (_gelu(y) * _sig(gl + b) * _silu(z),), ()


def st_s5b_bwd(dy3, y, gl, z, b):
    y1 = _gelu(y)
    s = _sig(gl + b)
    dy2 = dy3 * _silu(z)
    dz = dy3 * y1 * s * _dsilu(z)
    dgl = dy2 * y1 * s * (1.0 - s)
    return (dgl, dz, dy2 * s), (_sum0(dgl),)


def st_s5a_bwd(dy1a, dy1b, y, u, d):
    dy = (dy1a + dy1b) * _dgelu(y)
    return (dy, dy * d), (_sum0(dy * u),)


def st_final(x2, tgt, g, mask):
    r = _rs(x2)
    n = x2 * r
    e = n * g - tgt
    dyo = e * (1.0 / D)
    dn = dyo * g
    dx = r * (dn - n * jnp.mean(dn * n, axis=-1, keepdims=True))
    lsum = jnp.sum(_sum0(e * e), axis=1, keepdims=True) * (0.5 / D)
    return (dx * mask,), (_sum0(dyo * n), jnp.broadcast_to(lsum, (1, 128)))


def rowwise(fn, rows, vecs, out_rows, out_sums, name):
    lat_blk = lambda i: jnp.maximum(i - 1, 0)
    arrays, in_specs, pick = [], [], []
    for a in rows:
        if not isinstance(a, tuple):
            a = (a, 0, a.shape[1])
        tag = a[0] if isinstance(a[0], str) else None
        if tag == "cat":
            _, ctx, x = a
            arrays += [ctx, x]
            in_specs += [pl.BlockSpec((TB, ctx.shape[1]), lambda i: (0, 0)),
                         pl.BlockSpec((TB, x.shape[1]), lambda i: (lat_blk(i), 0))]
            pick.append(2)
        elif tag == "lat":
            arrays.append(a[1])
            in_specs.append(pl.BlockSpec((TB, a[1].shape[1]), lambda i: (lat_blk(i), 0)))
            pick.append(1)
        else:
            arr, cb, width = a
            arrays.append(arr)
            in_specs.append(pl.BlockSpec((TB, width), lambda i, cb=cb: (i, cb)))
            pick.append(1)
    T = LC + L
    nin, nv, no = len(arrays), len(vecs), len(out_rows)

    def body(*refs):
        i = pl.program_id(0)
        vals, k = [], 0
        for p in pick:
            if p == 2:
                vals.append(jnp.where(i == 0, refs[k][...], refs[k + 1][...]))
            else:
                vals.append(refs[k][...])
            k += p
        vals += [r[0] for r in refs[nin:nin + nv]]
        outs, sums = fn(*vals)
        for r, o in zip(refs[nin + nv:nin + nv + no], outs):
            r[...] = o.astype(r.dtype)
        sum_refs = refs[nin + nv + no:]
        if sum_refs:
            @pl.when(i <= 1)
            def _():
                for r in sum_refs:
                    r[...] = jnp.zeros_like(r)
            for r, s in zip(sum_refs, sums):
                r[0] += s

    kind = lambda i: (jnp.minimum(i, 1), 0, 0)
    in_specs += [pl.BlockSpec((1, 1, v.shape[2]), kind) for v in vecs]
    out_specs, out_shape = [], []
    for o in out_rows:
        lat = len(o) == 3
        out_specs.append(pl.BlockSpec((TB, o[0]), (lambda i: (lat_blk(i), 0)) if lat else (lambda i: (i, 0))))
        out_shape.append(jax.ShapeDtypeStruct((L if lat else T, o[0]), o[1]))
    out_specs += [pl.BlockSpec((1, 1, c), kind) for c in out_sums]
    out_shape += [jax.ShapeDtypeStruct((2, 1, c), F32) for c in out_sums]
    res = pl.pallas_call(body, grid=(T // TB,), in_specs=in_specs, out_specs=out_specs, out_shape=out_shape,
                         compiler_params=_cp(("arbitrary",)), name=name)(*arrays, *vecs)
    return res[:no], res[no:]


_DN = {"nn": (((1,), (0,)), ((), ())), "nt": (((1,), (1,)), ((), ())), "tn": (((0,), (0,)), ((), ()))}


def mm(a, b, mode, name, out_dtype=F32, tm=256, tn=None, shard_out=False):
    if mode == "nn":
        (M, K), (_, N) = a.shape, b.shape
    elif mode == "nt":
        (M, K), (N, _) = a.shape, b.shape
    else:
        (K, M), (_, N) = a.shape, b.shape
    tm = min(tm, M)
    tn = N if tn is None else tn
    dn = _DN[mode]

    def body(a_ref, b_ref, o_ref):
        o_ref[...] = lax.dot_general(a_ref[...].astype(BF16), b_ref[...].astype(BF16), dn,
                                     preferred_element_type=F32).astype(o_ref.dtype)

    if shard_out:
        def body(a_ref, b_ref, o_ref):
            av = a_ref[...].astype(BF16)
            for j in range(N // tn):
                bj = b_ref[pl.ds(j * tn, tn), :] if mode == "nt" else b_ref[:, pl.ds(j * tn, tn)]
                o_ref[j] = lax.dot_general(av, bj.astype(BF16), dn, preferred_element_type=F32).astype(o_ref.dtype)

        a_spec = pl.BlockSpec((K, tm), lambda i: (0, i)) if mode == "tn" else pl.BlockSpec((tm, K), lambda i: (i, 0))
        return pl.pallas_call(body, grid=(M // tm,), in_specs=[a_spec, pl.BlockSpec(b.shape, lambda i: (0, 0))],
                              out_specs=pl.BlockSpec((N // tn, tm, tn), lambda i: (0, i, 0)),
                              out_shape=jax.ShapeDtypeStruct((N // tn, M, tn), out_dtype),
                              compiler_params=_cp(("parallel",)), name=name)(a, b)
    a_spec = pl.BlockSpec((K, tm), lambda i, j: (0, i)) if mode == "tn" else pl.BlockSpec((tm, K), lambda i, j: (i, 0))
    b_spec = pl.BlockSpec((tn, K), lambda i, j: (j, 0)) if mode == "nt" else pl.BlockSpec((K, tn), lambda i, j: (0, j))
    return pl.pallas_call(body, grid=(M // tm, N // tn), in_specs=[a_spec, b_spec],
                          out_specs=pl.BlockSpec((tm, tn), lambda i, j: (i, j)), out_shape=jax.ShapeDtypeStruct((M, N), out_dtype),
                          compiler_params=_cp(("parallel", "arbitrary")), name=name)(a, b)


def _rope_tables(T, width=QK, first=NOPE):
    nlat = T - LC
    pos = np.arange(nlat)
    row, col = pos // GRID_W, pos % GRID_W
    half = ROPE // 2
    inv = 1.0 / (THETA ** (np.arange(0, half, 2, dtype=np.float64) / half))
    cosf = np.ones((T, width), np.float64)
    sinf = np.zeros((T, width), np.float64)
    perm = np.zeros((width, width), np.float32)
    for m in range(ROPE):
        j = first + m
        blk, w = m // half, m % half
        ang = (row if blk == 0 else col)[:, None] * inv[None, :]
        f = w % (half // 2)
        cosf[LC:, j] = np.cos(ang[:, f])
        if w < half // 2:
            sinf[LC:, j] = -np.sin(ang[:, f])
            perm[j + half // 2, j] = 1.0
        else:
            sinf[LC:, j] = np.sin(ang[:, f])
            perm[j - half // 2, j] = 1.0
    return jnp.asarray(cosf, F32), jnp.asarray(sinf, F32), jnp.asarray(perm, BF16), jnp.asarray(perm.T, BF16)


def _exact_perm(x, pm):
    hi = x.astype(BF16)
    r1 = x - hi.astype(F32)
    mid = r1.astype(BF16)
    lo = (r1 - mid.astype(F32)).astype(BF16)
    dot = lambda a: jnp.dot(a, pm, preferred_element_type=F32)
    return dot(hi) + dot(mid) + dot(lo)


def _rot(x, cv, sv, pv, inverse):
    if inverse:
        return x * cv + _exact_perm(x * sv, pv)
    return x * cv + _exact_perm(x, pv) * sv


def rope(x, cosf, sinf, pm, inverse, out_dtype, name, scale=1.0):
    H, T, _ = x.shape

    def body(x_ref, c_ref, s_ref, p_ref, o_ref):
        cv, sv, pv = c_ref[...], s_ref[...], p_ref[...]
        for h in range(H):
            o_ref[h] = (_rot(x_ref[h], cv, sv, pv, inverse) * scale).astype(o_ref.dtype)

    return pl.pallas_call(
        body, grid=(T // TB,),
        in_specs=[pl.BlockSpec((H, TB, QK), lambda i: (0, i, 0)), pl.BlockSpec((TB, QK), lambda i: (i, 0)),
                  pl.BlockSpec((TB, QK), lambda i: (i, 0)), pl.BlockSpec((QK, QK), lambda i: (0, 0))],
        out_specs=pl.BlockSpec((H, TB, QK), lambda i: (0, i, 0)), out_shape=jax.ShapeDtypeStruct((H, T, QK), out_dtype),
        compiler_params=_cp(("parallel",)), name=name)(x, cosf, sinf, pm)


KVW = NOPE + VD


def _kv_selectors():
    s_kn = np.zeros((KVW, QK), np.float32)
    s_kr = np.zeros((128, QK), np.float32)
    s_v = np.zeros((KVW, VD), np.float32)
    for l in range(NOPE):
        s_kn[l, l] = 1.0
    for l in range(ROPE):
        s_kr[l, NOPE + l] = 1.0
    for l in range(VD):
        s_v[NOPE + l, l] = 1.0
    return s_kn, s_kr, s_v


def assemble_kv(kvh, p0, kr_block, name):
    H, T, _ = kvh.shape
    cosf, sinf, pm, _ = _rope_tables(T, 128, 0)
    s_kn, s_kr, s_v = (jnp.asarray(s, BF16) for s in _kv_selectors())

    def body(kv_ref, kr_ref, c_ref, s_ref, p_ref, skn_ref, skr_ref, sv_ref, k_ref, v_ref):
        krr = _rot(kr_ref[...], c_ref[...], s_ref[...], p_ref[...], False).astype(BF16)
        kr_part = jnp.dot(krr, skr_ref[...], preferred_element_type=F32)
        for h in range(H):
            kvb = kv_ref[h].astype(BF16)
            k_ref[h] = (jnp.dot(kvb, skn_ref[...], preferred_element_type=F32) + kr_part).astype(BF16)
            v_ref[h] = jnp.dot(kvb, sv_ref[...], preferred_element_type=F32).astype(BF16)

    rows = lambda c: pl.BlockSpec((TB, c), lambda i: (i, 0))
    const = lambda a: pl.BlockSpec(a.shape, lambda i: (0, 0))
    return pl.pallas_call(
        body, grid=(T // TB,),
        in_specs=[pl.BlockSpec((H, TB, KVW), lambda i: (0, i, 0)), pl.BlockSpec((TB, 128), lambda i: (i, kr_block)),
                  rows(128), rows(128), const(pm), const(s_kn), const(s_kr), const(s_v)],
        out_specs=[pl.BlockSpec((H, TB, QK), lambda i: (0, i, 0)), pl.BlockSpec((H, TB, VD), lambda i: (0, i, 0))],
        out_shape=[jax.ShapeDtypeStruct((H, T, QK), BF16), jax.ShapeDtypeStruct((H, T, VD), BF16)],
        compiler_params=_cp(("parallel",)), name=name)(kvh, p0, cosf, sinf, pm, s_kn, s_kr, s_v)


def split_kv_grads(dk, dv, name):
    H, T, _ = dk.shape
    cosf, sinf, _, pmt = _rope_tables(T, 128, 0)
    s_kn, s_kr, s_v = _kv_selectors()
    s_knt, s_krt, s_vt = (jnp.asarray(s.T, BF16) for s in (s_kn, s_kr, s_v))

    def body(dk_ref, dv_ref, c_ref, s_ref, p_ref, skn_ref, skr_ref, sv_ref, dkv_ref, dkr_ref):
        total = None
        for h in range(H):
            dkh = dk_ref[h]
            total = dkh if total is None else total + dkh
            dkv_ref[:, pl.ds(h * KVW, KVW)] = (
                jnp.dot(dkh.astype(BF16), skn_ref[...], preferred_element_type=F32)
                + jnp.dot(dv_ref[h].astype(BF16), sv_ref[...], preferred_element_type=F32)).astype(BF16)
        dkr_ref[...] = _rot(_exact_perm(total, skr_ref[...]), c_ref[...], s_ref[...], p_ref[...], True)

    rows = lambda c: pl.BlockSpec((TB, c), lambda i: (i, 0))
    const = lambda a: pl.BlockSpec(a.shape, lambda i: (0, 0))
    return pl.pallas_call(
        body, grid=(T // TB,),
        in_specs=[pl.BlockSpec((H, TB, QK), lambda i: (0, i, 0)), pl.BlockSpec((H, TB, VD), lambda i: (0, i, 0)),
                  rows(128), rows(128), const(pmt), const(s_knt), const(s_krt), const(s_vt)],
        out_specs=[rows(H * KVW), rows(128)],
        out_shape=[jax.ShapeDtypeStruct((T, H * KVW), BF16), jax.ShapeDtypeStruct((T, 128), F32)],
        compiler_params=_cp(("parallel",)), name=name)(dk, dv, cosf, sinf, pmt, s_knt, s_krt, s_vt)


def _by_query_block(run, T):
    @pl.when(pl.program_id(1) == 0)
    def _():
        run(LC)

    @pl.when(pl.program_id(1) > 0)
    def _():
        run(T)


def _with_rider(body, nin, nout, ride, grid):
    if ride is None:
        return body
    n = ride.n

    def wrapped(*refs):
        ins, xs = refs[:nin], refs[nin:nin + n]
        outs, got = refs[nin + n:nin + n + nout], refs[nin + n + nout:nin + 2 * n + nout]
        sems = refs[nin + 2 * n + nout:]
        step = pl.program_id(0) * grid[1] + pl.program_id(1)

        @pl.when(step == 0)
        def _():
            ride.start(xs, got, sems)

        body(*ins, *outs)

        @pl.when(step == grid[0] * grid[1] - 1)
        def _():
            ride.finish(xs, got, sems)

    return wrapped


def _ride_call(body, grid, in_specs, out_specs, out_shape, ride, rode, name, args):
    if ride is None:
        return pl.pallas_call(body, grid=grid, in_specs=in_specs, out_specs=out_specs, out_shape=out_shape,
                              compiler_params=_cp(("parallel", "arbitrary")), name=name)(*args), []
    res = pl.pallas_call(
        _with_rider(body, len(in_specs), len(out_specs), ride, grid), grid=grid,
        in_specs=in_specs + ride.specs, out_specs=out_specs + ride.specs, out_shape=out_shape + ride.out_shape,
        scratch_shapes=ride.scratch,
        compiler_params=pltpu.CompilerParams(dimension_semantics=("arbitrary", "arbitrary"), vmem_limit_bytes=VMEM_LIMIT,
                                             has_side_effects=True), name=name)(*args, *rode)
    return res[:len(out_specs)], res[len(out_specs):]


def attn_fwd(q, k, v, name, rode=None, modes=None):
    H, T, _ = q.shape

    def body(q_ref, k_ref, v_ref, o_ref, lse_ref):
        def run(nk):
            s = _dotf(q_ref[0], k_ref[0, pl.ds(0, nk), :], "nt")
            m = jnp.max(s, axis=1, keepdims=True)
            p = jnp.exp(s - m)
            l = jnp.sum(p, axis=1, keepdims=True)
            o = jnp.dot(p.astype(BF16), v_ref[0, pl.ds(0, nk), :], preferred_element_type=F32)
            o_ref[0] = o / l
            lse_ref[0] = m + jnp.log(l)

        _by_query_block(run, T)

    return _ride_call(
        body, (H, T // TB),
        [pl.BlockSpec((1, TB, QK), lambda h, i: (h, i, 0)), pl.BlockSpec((1, T, QK), lambda h, i: (h, 0, 0)),
         pl.BlockSpec((1, T, VD), lambda h, i: (h, 0, 0))],
        [pl.BlockSpec((1, TB, VD), lambda h, i: (h, i, 0)), pl.BlockSpec((1, TB, 1), lambda h, i: (h, i, 0))],
        [jax.ShapeDtypeStruct((H, T, VD), F32), jax.ShapeDtypeStruct((H, T, 1), F32)],
        Exchange(rode, modes) if rode else None, rode, name, (q, k, v))


def attn_bwd(q, k, v, o, lse, do, name, rode=None, modes=None):
    H, T, _ = q.shape

    def body(q_ref, k_ref, v_ref, o_ref, lse_ref, do_ref, dq_ref, dk_ref, dv_ref):
        i = pl.program_id(1)

        @pl.when(i == 0)
        def _():
            dk_ref[...] = jnp.zeros_like(dk_ref)
            dv_ref[...] = jnp.zeros_like(dv_ref)

        def run(nk):
            keys = pl.ds(0, nk)
            qv, kv, dov = q_ref[0], k_ref[0, keys, :], do_ref[0]
            p = jnp.exp(_dotf(qv, kv, "nt") - lse_ref[0])
            delta = jnp.sum(dov * o_ref[0], axis=1, keepdims=True)
            dob = dov.astype(BF16)
            dv_ref[0, keys, :] += _dotf(p.astype(BF16), dob, "tn")
            dp = _dotf(dob, v_ref[0, keys, :], "nt")
            ds = (p * (dp - delta)).astype(BF16)
            dq_ref[0] = jnp.dot(ds, kv, preferred_element_type=F32)
            dk_ref[0, keys, :] += _dotf(ds, qv, "tn")

        _by_query_block(run, T)

    blk = lambda c: pl.BlockSpec((1, TB, c), lambda h, i: (h, i, 0))
    full = lambda c: pl.BlockSpec((1, T, c), lambda h, i: (h, 0, 0))
    return _ride_call(
        body, (H, T // TB), [blk(QK), full(QK), full(VD), blk(VD), blk(1), blk(VD)], [blk(QK), full(QK), full(VD)],
        [jax.ShapeDtypeStruct((H, T, QK), F32), jax.ShapeDtypeStruct((H, T, QK), F32), jax.ShapeDtypeStruct((H, T, VD), F32)],
        Exchange(rode, modes) if rode else None, rode, name, (q, k, v, o, lse, do))


def disc_fwd(a_re, a_im, ls, name):
    def body(ar_ref, ai_ref, ls_ref, lr_ref, li_ref, fr_ref, fi_ref):
        ar, ai = ar_ref[...], ai_ref[...]
        dt = jnp.exp(ls_ref[...])
        mag = jnp.exp(ar * dt)
        lr = mag * jnp.cos(ai * dt)
        li = mag * jnp.sin(ai * dt)
        den = ar * ar + ai * ai
        nr = lr - 1.0
        lr_ref[...] = lr
        li_ref[...] = li
        fr_ref[...] = (nr * ar + li * ai) / den
        fi_ref[...] = (li * ar - nr * ai) / den

    return pl.pallas_call(body, out_shape=[jax.ShapeDtypeStruct(a_re.shape, F32)] * 4, name=name)(a_re, a_im, ls)


def disc_b(f_re, f_im, b_re, b_im, name):
    def body(fr_ref, fi_ref, br_ref, bi_ref, or_ref, oi_ref):
        fr, fi, br, bi = fr_ref[...], fi_ref[...], br_ref[...], bi_ref[...]
        or_ref[...] = fr * br - fi * bi
        oi_ref[...] = fr * bi + fi * br

    fs, bs = _disc_b_specs()
    return pl.pallas_call(body, grid=(2, G * P // DISC_ROWS), in_specs=[fs, fs, bs, bs], out_specs=[bs, bs],
                          out_shape=[jax.ShapeDtypeStruct(b_re.shape, F32)] * 2, name=name)(f_re, f_im, b_re, b_im)


DISC_ROWS = 1024


def _disc_b_specs():
    return (pl.BlockSpec((1, DISC_ROWS, 1), lambda d, i: (d, i, 0)), pl.BlockSpec((1, DISC_ROWS, CH), lambda d, i: (d, i, 0)))


def disc_b_bwd(f_re, f_im, b_re, b_im, dbb_re, dbb_im, name):
    def body(fr_ref, fi_ref, br_ref, bi_ref, dr_ref, di_ref, dbr_ref, dbi_ref, dfr_ref, dfi_ref):
        fr, fi, br, bi, dr, di = fr_ref[...], fi_ref[...], br_ref[...], bi_ref[...], dr_ref[...], di_ref[...]
        dbr_ref[...] = fr * dr + fi * di
        dbi_ref[...] = fr * di - fi * dr
        dfr_ref[...] = jnp.sum(dr * br + di * bi, axis=-1, keepdims=True)
        dfi_ref[...] = jnp.sum(di * br - dr * bi, axis=-1, keepdims=True)

    fs, bs = _disc_b_specs()
    return pl.pallas_call(body, grid=(2, G * P // DISC_ROWS), in_specs=[fs, fs, bs, bs, bs, bs], out_specs=[bs, bs, fs, fs],
                          out_shape=[jax.ShapeDtypeStruct(b_re.shape, F32)] * 2 + [jax.ShapeDtypeStruct(f_re.shape, F32)] * 2,
                          name=name)(f_re, f_im, b_re, b_im, dbb_re, dbb_im)


def disc_a_bwd(a_re, a_im, ls, dlr, dli, dfr, dfi, name):
    def body(ar_ref, ai_ref, ls_ref, dlr_ref, dli_ref, dfr_ref, dfi_ref, dar_ref, dai_ref, dls_ref):
        ar, ai = ar_ref[...], ai_ref[...]
        dt = jnp.exp(ls_ref[...])
        mag = jnp.exp(ar * dt)
        cs, sn = jnp.cos(ai * dt), jnp.sin(ai * dt)
        lr, li = mag * cs, mag * sn
        den = ar * ar + ai * ai
        nr = lr - 1.0
        f_re = (nr * ar + li * ai) / den
        f_im = (li * ar - nr * ai) / den
        dn1 = dfr_ref[...] / den
        dn2 = dfi_ref[...] / den
        dden = -(dfr_ref[...] * f_re + dfi_ref[...] * f_im) / den
        dlr_t = dlr_ref[...] + dn1 * ar - dn2 * ai
        dli_t = dli_ref[...] + dn1 * ai + dn2 * ar
        dar = dn1 * nr + dn2 * li + dden * 2.0 * ar
        dai = dn1 * li - dn2 * nr + dden * 2.0 * ai
        dmag = dlr_t * cs + dli_t * sn
        dth = dli_t * lr - dlr_t * li
        dar_ref[...] = dar + dmag * mag * dt
        dai_ref[...] = dai + dth * dt
        dls_ref[...] = jnp.sum(dmag * mag * ar + dth * ai, axis=-1, keepdims=True) * dt

    return pl.pallas_call(body, out_shape=[jax.ShapeDtypeStruct(a_re.shape, F32)] * 2 +
                          [jax.ShapeDtypeStruct(ls.shape, F32)], name=name)(a_re, a_im, ls, dlr, dli, dfr, dfi)


def _cpow(lr, li, n):
    rr, ri = None, None
    br, bi = lr, li
    while n:
        if n & 1:
            if rr is None:
                rr, ri = br, bi
            else:
                rr, ri = rr * br - ri * bi, rr * bi + ri * br
        n >>= 1
        if n:
            br, bi = br * br - bi * bi, 2.0 * br * bi
    return rr, ri


UNROLL = 4


def _seg_scan(xre, xim, lam8, pw, base, seglen, rev, init, fin_re, fin_im, ini_re, ini_im, prev=None):
    lr, li = lam8

    def rows(t):
        return pl.ds(pl.multiple_of(base + t * SEG, SEG), SEG)

    tmap = (lambda n: seglen - 1 - n) if rev else (lambda n: n)
    zero = jnp.zeros((SEG, SB), F32)

    def advance(c, t):
        a, b = c
        return lr * a - li * b + xre[rows(t), :], lr * b + li * a + xim[rows(t), :]

    fin = lax.fori_loop(0, seglen, lambda n, c: advance(c, tmap(n)), (zero, zero), unroll=UNROLL)
    fin_re[...] = fin[0]
    fin_im[...] = fin[1]
    (cr, ci), (pr, pi) = init, pw
    for i in (range(SEG - 1, -1, -1) if rev else range(SEG)):
        ini_re[pl.ds(i, 1), :] = cr
        ini_im[pl.ds(i, 1), :] = ci
        cr, ci = pr * cr - pi * ci + fin_re[pl.ds(i, 1), :], pr * ci + pi * cr + fin_im[pl.ds(i, 1), :]
    start = (ini_re[...], ini_im[...])

    def store(c, t):
        na, nb = advance(c, t)
        xre[rows(t), :] = na
        xim[rows(t), :] = nb
        return na, nb

    if prev is None:
        lax.fori_loop(0, seglen, lambda n, c: store(c, tmap(n)), start, unroll=UNROLL)
        return (cr, ci), None

    sre, sim, s_ini_re, s_ini_im = prev

    def acc_step(c, t, pre, pim):
        na, nb = store(c[:2], t)
        return na, nb, c[2] + na * pre + nb * pim, c[3] + nb * pre - na * pim

    def body(n, c):
        t = tmap(n)
        tp = t - 1 if rev else t + 1
        return acc_step(c, t, sre[rows(tp), :], sim[rows(tp), :])

    c = lax.fori_loop(0, seglen - 1, body, start + (zero, zero), unroll=UNROLL)
    c = acc_step(c, 0 if rev else seglen - 1, s_ini_re[...], s_ini_im[...])
    return (cr, ci), c[2:]


def _lam_tiles(lr, li, lens, conj=False):
    if conj:
        li = -li
    lam8 = (jnp.broadcast_to(lr, (SEG, SB)), jnp.broadcast_to(li, (SEG, SB)))
    return lam8, [_cpow(lr, li, n) for n in lens]


def _stretches(T):
    return ((0, LC // SEG), (LC, (T - LC) // SEG))


def _to_seg_order(src, dst, T):
    for base, seglen in _stretches(T):
        def body(t, carry, base=base, seglen=seglen):
            dst[pl.ds(pl.multiple_of(base + t * SEG, SEG), SEG), :] = src[pl.ds(base + t, SEG, stride=seglen), :]
            return carry
        lax.fori_loop(0, seglen, body, 0, unroll=8)


def _from_seg_order(src, dst, T):
    for base, seglen in _stretches(T):
        def body(t, carry, base=base, seglen=seglen):
            dst[pl.ds(base + t, SEG, stride=seglen), :] = src[pl.ds(pl.multiple_of(base + t * SEG, SEG), SEG), :]
            return carry
        lax.fori_loop(0, seglen, body, 0, unroll=8)


def _scan_specs(T):
    ublk = pl.BlockSpec((T, UB), lambda j: (0, j))
    lam = pl.BlockSpec((2, 1, 1, SB), lambda j: (0, j, 0, 0))
    mat = pl.BlockSpec((2, 1, UB, P), lambda j: (0, j, 0, 0))
    return ublk, lam, mat


def _dotf(a, b, mode="nn"):
    return lax.dot_general(a, b, _DN[mode], preferred_element_type=F32)


def _diag_mask():
    r = lax.broadcasted_iota(jnp.int32, (UB, SB), 0)
    c = lax.broadcasted_iota(jnp.int32, (UB, SB), 1)
    return lax.shift_right_logical(r, int(math.log2(CH))) == lax.shift_right_logical(c, int(math.log2(P)))


def _expand(m):
    p = lax.broadcasted_iota(jnp.int32, (P, SB), 0)
    c = lax.broadcasted_iota(jnp.int32, (P, SB), 1)
    tile = jnp.where(lax.bitwise_and(c, P - 1) == p, 1.0, 0.0).astype(BF16)
    wide = jnp.dot(m.astype(BF16), tile, preferred_element_type=F32)
    return jnp.where(_diag_mask(), wide, 0.0).astype(BF16)


def _collapse(full):
    c = lax.broadcasted_iota(jnp.int32, (SB, P), 0)
    p = lax.broadcasted_iota(jnp.int32, (SB, P), 1)
    pick = jnp.where(lax.bitwise_and(c, P - 1) == p, 1.0, 0.0).astype(BF16)
    return _exact_perm(jnp.where(_diag_mask(), full, 0.0), pick)


def _zero_state():
    return jnp.zeros((1, SB), F32), jnp.zeros((1, SB), F32)


def scan_fwd(u, lam_re, lam_im, bre, bim, cre, cim, name):
    T = u.shape[0]
    s_ctx, s_lat = LC // SEG, (T - LC) // SEG

    def body(u_ref, lr_ref, li_ref, bre_ref, bim_ref, cre_ref, cim_ref, y_ref, us, ys, sre, sim, fre, fim, ire, iim):
        _to_seg_order(u_ref, us, T)
        ub = us[...].astype(BF16)
        for d in range(2):
            lam8, (pw_c, pw_l) = _lam_tiles(lr_ref[d, 0], li_ref[d, 0], (s_ctx, s_lat))
            sre[...] = _dotf(ub, _expand(bre_ref[d, 0]))
            sim[...] = _dotf(ub, _expand(bim_ref[d, 0]))
            end_c, _ = _seg_scan(sre, sim, lam8, pw_c, 0, s_ctx, bool(d), _zero_state(), fre, fim, ire, iim)
            _seg_scan(sre, sim, lam8, pw_l, LC, s_lat, bool(d), end_c, fre, fim, ire, iim)
            y = (_dotf(sre[...].astype(BF16), _expand(cre_ref[d, 0]), "nt")
                 - _dotf(sim[...].astype(BF16), _expand(cim_ref[d, 0]), "nt"))
            if d == 0:
                ys[...] = y
            else:
                ys[...] += y
        _from_seg_order(ys, y_ref, T)

    ublk, lam, mat = _scan_specs(T)
    return pl.pallas_call(
        body, grid=(NJ,), in_specs=[ublk, lam, lam, mat, mat, mat, mat], out_specs=ublk,
        out_shape=jax.ShapeDtypeStruct((T, G * CH), F32),
        scratch_shapes=[pltpu.VMEM((T, UB), F32)] * 2 + [pltpu.VMEM((T, SB), F32)] * 2 + [pltpu.VMEM((SEG, SB), F32)] * 4,
        compiler_params=_cp(("arbitrary",)), name=name)(u, lam_re, lam_im, bre, bim, cre, cim)


def scan_bwd(u, dy, lam_re, lam_im, bre, bim, cre, cim, name):
    T = u.shape[0]
    s_ctx, s_lat = LC // SEG, (T - LC) // SEG

    def body(u_ref, dy_ref, lr_ref, li_ref, bre_ref, bim_ref, cre_ref, cim_ref,
             du_ref, dlr_ref, dli_ref, dbre_ref, dbim_ref, dcre_ref, dcim_ref,
             us, dys, dus, sre, sim, gre, gim, fre, fim, ic_re, ic_im, il_re, il_im, jre, jim):
        _to_seg_order(u_ref, us, T)
        _to_seg_order(dy_ref, dys, T)
        ub, dyb = us[...].astype(BF16), dys[...].astype(BF16)
        for d in range(2):
            rev = bool(d)
            lam8, (pw_c, pw_l) = _lam_tiles(lr_ref[d, 0], li_ref[d, 0], (s_ctx, s_lat))
            cam8, (cw_c, cw_l) = _lam_tiles(lr_ref[d, 0], li_ref[d, 0], (s_ctx, s_lat), conj=True)
            bre_v, bim_v = _expand(bre_ref[d, 0]), _expand(bim_ref[d, 0])
            sre[...] = _dotf(ub, bre_v)
            sim[...] = _dotf(ub, bim_v)
            end_c, _ = _seg_scan(sre, sim, lam8, pw_c, 0, s_ctx, rev, _zero_state(), fre, fim, ic_re, ic_im)
            _seg_scan(sre, sim, lam8, pw_l, LC, s_lat, rev, end_c, fre, fim, il_re, il_im)
            gre[...] = _dotf(dyb, _expand(cre_ref[d, 0]))
            gim[...] = -_dotf(dyb, _expand(cim_ref[d, 0]))
            end_g, acc_l = _seg_scan(gre, gim, cam8, cw_l, LC, s_lat, not rev, _zero_state(), fre, fim, jre, jim,
                                     prev=(sre, sim, il_re, il_im))
            _, acc_c = _seg_scan(gre, gim, cam8, cw_c, 0, s_ctx, not rev, end_g, fre, fim, jre, jim,
                                 prev=(sre, sim, ic_re, ic_im))
            dlr_ref[d, 0] = _sum0(acc_l[0] + acc_c[0])
            dli_ref[d, 0] = _sum0(acc_l[1] + acc_c[1])
            grb, gib = gre[...].astype(BF16), gim[...].astype(BF16)
            du = _dotf(grb, bre_v, "nt") + _dotf(gib, bim_v, "nt")
            if d == 0:
                dus[...] = du
            else:
                dus[...] += du
            dbre_ref[d, 0] = _collapse(_dotf(ub, grb, "tn"))
            dbim_ref[d, 0] = _collapse(_dotf(ub, gib, "tn"))
            dcre_ref[d, 0] = _collapse(_dotf(dyb, sre[...].astype(BF16), "tn"))
            dcim_ref[d, 0] = -_collapse(_dotf(dyb, sim[...].astype(BF16), "tn"))
        _from_seg_order(dus, du_ref, T)

    ublk, lam, mat = _scan_specs(T)
    lam_s = jax.ShapeDtypeStruct(lam_re.shape, F32)
    mat_s = jax.ShapeDtypeStruct(bre.shape, F32)
    return pl.pallas_call(
        body, grid=(NJ,), in_specs=[ublk, ublk, lam, lam, mat, mat, mat, mat],
        out_specs=[ublk, lam, lam, mat, mat, mat, mat],
        out_shape=[jax.ShapeDtypeStruct((T, G * CH), F32), lam_s, lam_s, mat_s, mat_s, mat_s, mat_s],
        scratch_shapes=[pltpu.VMEM((T, UB), F32)] * 3 + [pltpu.VMEM((T, SB), F32)] * 4 + [pltpu.VMEM((SEG, SB), F32)] * 8,
        compiler_params=_cp(("arbitrary",)), name=name)(u, dy, lam_re, lam_im, bre, bim, cre, cim)


class Exchange:
    def __init__(self, xs, modes):
        self.n = len(xs)
        self.modes = [modes] * self.n if isinstance(modes, (str, int)) else list(modes)
        self.out_shape = [jax.ShapeDtypeStruct(self._shape(x, md), x.dtype) for x, md in zip(xs, self.modes)]
        self.scratch = [pltpu.SemaphoreType.DMA((NDEV - 1, self.n)), pltpu.SemaphoreType.DMA((NDEV - 1, self.n)),
                        pltpu.SemaphoreType.DMA((self.n,))]
        self.specs = [pl.BlockSpec(memory_space=pl.ANY)] * self.n

    @staticmethod
    def _shape(x, mode):
        if mode == "gather":
            return (NDEV,) + tuple(x.shape)
        return tuple(x.shape) if mode == "lead" else (NDEV, x.shape[0], mode) + tuple(x.shape[2:])

    @staticmethod
    def _piece(x_ref, mode, dev):
        if mode == "gather":
            return x_ref
        return x_ref.at[dev] if mode == "lead" else x_ref.at[:, pl.ds(dev * mode, mode)]

    def _copies(self, x_refs, out_refs, sems):
        send_sems, recv_sems, local_sems = sems
        mx, my, mc = lax.axis_index("x"), lax.axis_index("y"), lax.axis_index("c")
        me = 4 * mx + 2 * my + mc
        local = [pltpu.make_async_copy(self._piece(x_ref, self.modes[a], me), out_ref.at[me], local_sems.at[a])
                 for a, (x_ref, out_ref) in enumerate(zip(x_refs, out_refs))]
        sends, recvs = [], []
        for k in range(1, NDEV):
            peer = (1 - mx if k & 4 else mx, 1 - my if k & 2 else my, 1 - mc if k & 1 else mc)
            pid = 4 * peer[0] + 2 * peer[1] + peer[2]
            for a, (x_ref, out_ref) in enumerate(zip(x_refs, out_refs)):
                src = self._piece(x_ref, self.modes[a], pid)
                sems_k = dict(send_sem=send_sems.at[k - 1, a], recv_sem=recv_sems.at[k - 1, a], device_id=peer,
                              device_id_type=MESH_T)
                sends.append(pltpu.make_async_remote_copy(src_ref=src, dst_ref=out_ref.at[me], **sems_k))
                recvs.append(pltpu.make_async_remote_copy(src_ref=src, dst_ref=out_ref.at[pid], **sems_k))
        return local, sends, recvs

    def start(self, x_refs, out_refs, sems):
        local, sends, _ = self._copies(x_refs, out_refs, sems)
        for cp in local + sends:
            cp.start()

    def finish(self, x_refs, out_refs, sems):
        local, sends, recvs = self._copies(x_refs, out_refs, sems)
        for cp in recvs:
            cp.wait_recv()
        for cp in sends:
            cp.wait_send()
        for cp in local:
            cp.wait()


def exchange(xs, modes, name):
    ex = Exchange(xs, modes)
    n = ex.n

    def body(*refs):
        ex.start(refs[:n], refs[n:2 * n], refs[2 * n:])
        ex.finish(refs[:n], refs[n:2 * n], refs[2 * n:])

    return pl.pallas_call(body, in_specs=ex.specs, out_specs=ex.specs, out_shape=ex.out_shape, scratch_shapes=ex.scratch,
                          compiler_params=pltpu.CompilerParams(has_side_effects=True), name=name)(*xs)


def _dot_f32(a, b, dn):
    return lax.dot_general(a, b, dn, preferred_element_type=F32, precision=lax.Precision.HIGHEST)


def ada_fwd(cg, c_ctx, ada_w, ada_b_loc, name):
    W = ada_w.shape[2]

    def body(cg_ref, cc_ref, w_ref, b_ref, o_ref):
        a = jnp.concatenate([_silu(cg_ref[...]), jnp.broadcast_to(_silu(cc_ref[...]), (NDEV, D))], axis=0)
        for i in range(2):
            o_ref[i] = _dot_f32(a, w_ref[i], _DN["nn"]) + b_ref[i]

    return pl.pallas_call(body, out_shape=jax.ShapeDtypeStruct((2, 2 * NDEV, W), F32),
                          compiler_params=_cp(), name=name)(cg, c_ctx, ada_w, ada_b_loc)


def ada_bwd(cg, c_ctx, ada_w, dm_loc, dm_all, name):
    W = ada_w.shape[2]

    def body(cg_ref, cc_ref, w_ref, dl_ref, da_ref, gw_ref, dcc_ref, gb_ref):
        a = jnp.concatenate([_silu(cg_ref[...]), jnp.broadcast_to(_silu(cc_ref[...]), (NDEV, D))], axis=0)
        dcc = jnp.zeros((1, D), F32)
        for i in range(2):
            dl = dl_ref[i]
            gw_ref[i] = _dot_f32(a, dl, _DN["tn"])
            dctx = jnp.sum(dl[NDEV:], axis=0, keepdims=True)
            dcc = dcc + _dot_f32(dctx, w_ref[i], _DN["nt"])
        dcc_ref[...] = dcc
        gb_ref[...] = jnp.sum(da_ref[...], axis=0)

    return pl.pallas_call(body, out_shape=[jax.ShapeDtypeStruct((2, D, W), F32), jax.ShapeDtypeStruct((1, D), F32),
                                           jax.ShapeDtypeStruct((2, 3 * D), F32)],
                          compiler_params=_cp(), name=name)(cg, c_ctx, ada_w, dm_loc, dm_all)


def cctx_finish(parts, c_ctx, name):
    def body(p_ref, cc_ref, o_ref):
        o_ref[...] = jnp.sum(p_ref[...], axis=0, keepdims=True) * _dsilu(cc_ref[...])

    return pl.pallas_call(body, out_shape=jax.ShapeDtypeStruct((1, D), F32), name=name)(parts, c_ctx)


def _adamw_update(g_ref, w_ref, m_ref, v_ref, go_ref, d_ref, mo_ref, vo_ref):
    g = g_ref[0].astype(F32)
    for s in range(1, g_ref.shape[0]):
        g = g + g_ref[s].astype(F32)
    mn = B1 * m_ref[...] + (1.0 - B1) * g
    vn = B2 * v_ref[...] + (1.0 - B2) * g * g
    go_ref[...] = g
    mo_ref[...] = mn
    vo_ref[...] = vn
    d_ref[...] = -LR * ((mn * (1.0 / (1.0 - B1 ** STEP))) / (jnp.sqrt(vn * (1.0 / (1.0 - B2 ** STEP))) + AEPS) + WD * w_ref[...])


def adamw(gstack, w, m, v, name, tr=256):
    n, R, C = gstack.shape
    tr = max(t for t in range(8, min(tr, R) + 1, 8) if R % t == 0)
    spec = pl.BlockSpec((tr, C), lambda i: (i, 0))
    return pl.pallas_call(_adamw_body(1), grid=(R // tr,),
                          in_specs=[pl.BlockSpec((n, tr, C), lambda i: (0, i, 0)), spec, spec, spec],
                          out_specs=[spec] * 4, out_shape=[jax.ShapeDtypeStruct((R, C), F32)] * 4,
                          compiler_params=_cp(("parallel",)), name=name)(gstack, w, m, v)


def _adamw_body(k):
    def body(*refs):
        for t in range(k):
            _adamw_update(*refs[4 * t:4 * t + 4], *refs[4 * k + 4 * t:4 * k + 4 * t + 4])
    return body


def adamw_multi(items, grid, name):
    k = len(items)
    ins, in_specs, out_specs, out_shape = [], [], [], []
    for g, g_spec, w, m, v, w_spec in items:
        ins += [g, w, m, v]
        in_specs += [g_spec, w_spec, w_spec, w_spec]
    for g, g_spec, w, m, v, w_spec in items:
        out_specs += [w_spec] * 4
        out_shape += [jax.ShapeDtypeStruct(w.shape, F32)] * 4
    res = pl.pallas_call(_adamw_body(k), grid=grid, in_specs=in_specs, out_specs=out_specs, out_shape=out_shape,
                         compiler_params=_cp(("arbitrary",) * len(grid)), name=name)(*ins)
    return [res[4 * t:4 * t + 4] for t in range(k)]


def _whole(a, grid_rank):
    zeros = (0,) * a.ndim
    return pl.BlockSpec(a.shape, lambda *idx: zeros)


def sum_slots(xs, name):
    def body(*refs):
        for x_ref, o_ref in zip(refs[:len(xs)], refs[len(xs):]):
            acc = x_ref[0]
            for s in range(1, NDEV):
                acc = acc + x_ref[s]
            o_ref[...] = acc

    return pl.pallas_call(body, out_shape=[jax.ShapeDtypeStruct(x.shape[1:], F32) for x in xs],
                          compiler_params=_cp(), name=name)(*xs)


def _col_shards(g):
    R, N = g.shape
    return g.reshape(R, NDEV, N // NDEV).transpose(1, 0, 2)


def _vec2(v):
    return jnp.broadcast_to(v.reshape(1, 1, -1), (2, 1, v.size))


SHARD_ROWS = {"mla_w_in": 192, "mla_w_uq": 192, "mla_w_ukv": 256, "s5_w_in": 256}


def _t_shard(wsh, rows):
    t = wsh[0].T.astype(BF16)
    return jnp.pad(t, ((0, rows - t.shape[0]), (0, 0)))


def _win_order():
    w = IN_W // NDEV
    perm = np.zeros((IN_WP, NDEV * SHARD_ROWS["mla_w_in"]), np.float32)
    first = QL + KVL + ROPE
    for c in range(IN_W):
        n = c + HEADS * VD if c < first else c - first
        perm[n, (c // w) * SHARD_ROWS["mla_w_in"] + c % w] = 1.0
    return jnp.asarray(perm, BF16)


def local_step(ctx, x, tgt, mod, Wt, small, l1_shards):
    T = LC + x.shape[0]
    xa = ("cat", ctx, x)
    sh = [mod[i, :, None, 0:D] for i in range(2)]
    sc = [mod[i, :, None, D:2 * D] for i in range(2)]
    gt = [mod[i, :, None, 2 * D:] for i in range(2)]
    ng = [_vec2(small["norm_g"][i]) for i in range(2)]
    qg, kvg = _vec2(small["mla_q_norm"]), _vec2(small["mla_kv_norm"])
    cosf, sinf, pm, pmt = _rope_tables(T)

    (h0,), _ = rowwise(st_norm_mod, [xa], [ng[0], sc[0], sh[0]], [(D, BF16)], [], "l0_norm")
    p0 = mm(h0, Wt["mla_w_in"], "nt", "l0_in")
    z0, cq, ckv = (p0, 0, HEADS * VD), (p0, HEADS * VD // QL, QL), (p0, (HEADS * VD + QL) // KVL, KVL)
    (cqn,), _ = rowwise(st_rms, [cq], [qg], [(QL, BF16)], [], "l0_qnorm")
    (ckvn,), _ = rowwise(st_rms, [ckv], [kvg], [(KVL, BF16)], [], "l0_kvnorm")
    qh = mm(cqn, Wt["mla_w_uq"], "nt", "l0_uq", tn=QK, shard_out=True)
    kvh = mm(ckvn, Wt["mla_w_ukv"], "nt", "l0_ukv", tn=KVW, shard_out=True)
    Q = rope(qh, cosf, sinf, pm, False, BF16, "l0_rope_q", scale=SCALE)
    K, V = assemble_kv(kvh, p0, (HEADS * VD + QL + KVL) // 128, "l0_kv")
    (o, lse), got = attn_fwd(Q, K, V, "l0_attn", rode=l1_shards, modes="gather")
    Wt, small = dict(Wt), dict(small)
    for n, a in zip(L1_BIG, got):
        Wt[n] = a.reshape(-1, a.shape[-1])
    vecs = lax.bitcast_convert_type(got[-1].reshape(NDEV, 2, -1, 2), F32)
    small["s5_d"], small["s5_b_glu"] = vecs[:, 0, :].reshape(D), vecs[:, 1, :].reshape(D)
    o2 = o.transpose(1, 0, 2).reshape(T, HEADS * VD)
    (og,), _ = rowwise(st_gate, [o2, z0], [], [(D, BF16)], [], "l0_gate")
    out0 = mm(og, Wt["mla_w_out"], "nn", "l0_out")
    (x1,), _ = rowwise(st_resid, [xa, out0], [gt[0]], [(D, F32)], [], "l0_resid")

    ls = small["s5_log_step"].reshape(2, G, 1)
    a_re, a_im = small["s5_a_re"].reshape(2, G, P), small["s5_a_im"].reshape(2, G, P)
    b_re, b_im = small["s5_b_re"].reshape(2, G * P, CH), small["s5_b_im"].reshape(2, G * P, CH)
    lam_re, lam_im, f_re, f_im = disc_fwd(a_re, a_im, ls, "s5_disc")
    f_re2, f_im2 = f_re.reshape(2, G * P, 1), f_im.reshape(2, G * P, 1)
    bb_re, bb_im = disc_b(f_re2, f_im2, b_re, b_im, "s5_disc_b")
    compact = lambda m: m.reshape(2, NJ, UB, P)
    bre = compact(bb_re.reshape(2, G, P, CH).transpose(0, 1, 3, 2))
    bim = compact(bb_im.reshape(2, G, P, CH).transpose(0, 1, 3, 2))
    cre, cim = compact(small["s5_c_re"]), compact(small["s5_c_im"])
    lam_re4, lam_im4 = lam_re.reshape(2, NJ, 1, SB), lam_im.reshape(2, NJ, 1, SB)

    (h1,), _ = rowwise(st_norm_mod, [x1], [ng[1], sc[1], sh[1]], [(D, BF16)], [], "l1_norm")
    p1 = mm(h1, Wt["s5_w_in"], "nt", "l1_in")
    u, z1 = (p1, 0, D), (p1, 1, D)
    yssm = scan_fwd(p1, lam_re4, lam_im4, bre, bim, cre, cim, "s5_scan")
    dvec, bglu = _vec2(small["s5_d"]), _vec2(small["s5_b_glu"])
    (y, y1b), _ = rowwise(st_s5a, [yssm, u], [dvec], [(D, F32), (D, BF16)], [], "l1_gelu")
    gl = mm(y1b, Wt["s5_w_glu"], "nn", "l1_glu")
    (y3,), _ = rowwise(st_s5b, [y, gl, z1], [bglu], [(D, BF16)], [], "l1_gate")
    out1 = mm(y3, Wt["s5_w_out"], "nn", "l1_out")
    (x2,), _ = rowwise(st_resid, [x1, out1], [gt[1]], [(D, F32)], [], "l1_resid")

    fg = _vec2(small["final_g"])
    lat_mask = jnp.stack([jnp.zeros((1, D), F32), jnp.ones((1, D), F32)])
    (dx2,), (dfg, lvec) = rowwise(st_final, [x2, ("lat", tgt)], [fg, lat_mask], [(D, F32)], [D, 128], "final")

    (dout1,), (dgt1,) = rowwise(st_resid_bwd, [dx2, out1], [gt[1]], [(D, BF16)], [D], "l1_resid_b")
    g_w_out5 = mm(y3, dout1, "tn", "l1_out_dw", out_dtype=BF16)
    dy3 = mm(dout1, Wt["s5_w_out"], "nt", "l1_out_dx")
    (dgl, dz1, dy1a), (dbglu,) = rowwise(st_s5b_bwd, [dy3, y, gl, z1], [bglu], [(D, BF16), (D, BF16), (D, F32)], [D], "l1_gate_b")
    g_w_glu = mm(y1b, dgl, "tn", "l1_glu_dw", out_dtype=BF16)
    dy1b = mm(dgl, Wt["s5_w_glu"], "nt", "l1_glu_dx")
    (dy, du_d), (dd,) = rowwise(st_s5a_bwd, [dy1a, dy1b, y, u], [dvec], [(D, F32), (D, F32)], [D], "l1_gelu_b")
    du_s, dlr, dli, dbre, dbim, dcre, dcim = scan_bwd(p1, dy, lam_re4, lam_im4, bre, bim, cre, cim, "s5_scan_b")
    du = du_d + du_s
    dbb_re = dbre.reshape(2, G, CH, P).transpose(0, 1, 3, 2).reshape(2, G * P, CH)
    dbb_im = dbim.reshape(2, G, CH, P).transpose(0, 1, 3, 2).reshape(2, G * P, CH)
    g_c_re, g_c_im = dcre.reshape(2, G, CH, P), dcim.reshape(2, G, CH, P)
    g_b_re, g_b_im, dfr, dfi = disc_b_bwd(f_re2, f_im2, b_re, b_im, dbb_re, dbb_im, "s5_disc_b_b")
    g_a_re, g_a_im, g_ls = disc_a_bwd(a_re, a_im, ls, dlr.reshape(2, G, P), dli.reshape(2, G, P),
                                      dfr.reshape(2, G, P), dfi.reshape(2, G, P), "s5_disc_b_a")
    dp1 = jnp.concatenate([du.astype(BF16), dz1], axis=1)
    g_w_in5 = mm(h1, dp1, "tn", "l1_in_dw", out_dtype=BF16, tm=D, tn=2 * D // NDEV, shard_out=True)
    dh1 = mm(dp1, Wt["s5_w_in"], "nn", "l1_in_dx")
    (dx1,), (dsh1, dsc1, dng1) = rowwise(st_norm_mod_bwd, [x1, dh1, dx2], [ng[1], sc[1]], [(D, F32)], [D, D, D], "l1_norm_b")

    (dout0,), (dgt0,) = rowwise(st_resid_bwd, [dx1, out0], [gt[0]], [(D, BF16)], [D], "l0_resid_b")
    g_w_out = mm(og, dout0, "tn", "l0_out_dw", out_dtype=BF16)
    dog = mm(dout0, Wt["mla_w_out"], "nt", "l0_out_dx")
    (do2, dz0), _ = rowwise(st_gate_bwd, [dog, o2, z0], [], [(D, F32), (D, F32)], [], "l0_gate_b")
    doh = do2.reshape(T, HEADS, VD).transpose(1, 0, 2)
    rows8 = lambda g: g.reshape(NDEV, -1, g.shape[-1])
    both = lambda s: s[0, 0] + s[1, 0]
    dense = lambda g: g.reshape(2, G * P * CH // 128, 128)
    chunks = [dense(g_b_re), dense(g_b_im), g_c_re, g_c_im]
    l1_send = [g_w_in5, rows8(g_w_glu), rows8(g_w_out5), both(dd).reshape(NDEV, 1, -1), both(dbglu).reshape(NDEV, 1, -1)]
    (dQ, dK, dV), l1_recv = attn_bwd(Q, K, V, o, lse, doh, "l0_attn_b", rode=l1_send + chunks,
                                     modes=["lead"] * len(l1_send) + [a.shape[1] // NDEV for a in chunks])
    dqh = rope(dQ, cosf, sinf, pmt, True, BF16, "l0_rope_q_b", scale=SCALE)
    dq = dqh.transpose(1, 0, 2).reshape(T, HEADS * QK)
    dkv, dkr = split_kv_grads(dK, dV, "l0_kv_b")
    g_w_uq = _col_shards(mm(cqn, dq, "tn", "l0_uq_dw", out_dtype=BF16))
    dcqn = mm(dq, Wt["mla_w_uq"], "nn", "l0_uq_dx")
    g_w_ukv = mm(ckvn, dkv, "tn", "l0_ukv_dw", out_dtype=BF16, tm=KVL, tn=HEADS * (NOPE + VD) // NDEV, shard_out=True)
    dckvn = mm(dkv, Wt["mla_w_ukv"], "nn", "l0_ukv_dx")
    (dcq,), (dqg,) = rowwise(st_rms_bwd, [cq, dcqn], [qg], [(QL, F32)], [QL], "l0_qnorm_b")
    (dckv,), (dkvg,) = rowwise(st_rms_bwd, [ckv, dckvn], [kvg], [(KVL, F32)], [KVL], "l0_kvnorm_b")
    dp0 = jnp.concatenate([dz0, dcq, dckv, dkr], axis=1).astype(BF16)
    g_p = mm(h0, dp0, "tn", "l0_in_dw", out_dtype=BF16)
    g_w_in = _col_shards(jnp.concatenate([g_p[:, HEADS * VD:IN_W], g_p[:, :HEADS * VD]], axis=1))
    dh0 = mm(dp0, Wt["mla_w_in"], "nn", "l0_in_dx")
    (grad_x,), (dsh0, dsc0, dng0) = rowwise(st_norm_mod_bwd, [xa, dh0, dx1], [ng[0], sc[0]], [(D, F32, "lat")], [D, D, D], "l0_norm_b")

    dmod = jnp.stack([jnp.concatenate([dsh0, dsc0, dgt0], axis=-1)[:, 0], jnp.concatenate([dsh1, dsc1, dgt1], axis=-1)[:, 0]])
    gbig = {"mla_w_in": g_w_in, "mla_w_uq": g_w_uq, "mla_w_ukv": g_w_ukv, "mla_w_out": rows8(g_w_out)}
    gsmall = {"norm_g": jnp.stack([both(dng0), both(dng1)]), "mla_q_norm": both(dqg), "mla_kv_norm": both(dkvg),
              "s5_a_re": g_a_re, "s5_a_im": g_a_im, "s5_log_step": g_ls, "final_g": dfg[1, 0]}
    return lvec[1], grad_x, dmod, gbig, gsmall, l1_recv


COL_SHARDED = ("mla_w_in", "mla_w_uq", "mla_w_ukv", "s5_w_in")
ROW_SHARDED = ("mla_w_out", "s5_w_glu", "s5_w_out")
VEC_SHARDED = ("s5_d", "s5_b_glu")
BIG = COL_SHARDED + ROW_SHARDED
L0_BIG = ("mla_w_in", "mla_w_uq", "mla_w_ukv", "mla_w_out")
L1_BIG = ("s5_w_in", "s5_w_glu", "s5_w_out")
BITS16 = jnp.bfloat16
SMALL_RS = ("norm_g", "mla_q_norm", "mla_kv_norm", "s5_a_re", "s5_a_im", "s5_log_step", "s5_b_re", "s5_b_im",
            "s5_c_re", "s5_c_im", "final_g")
CHUNKED = ("s5_b_re", "s5_b_im", "s5_c_re", "s5_c_im")
DENSE = ("s5_b_re", "s5_b_im")
TINY = ("norm_g", "mla_q_norm", "mla_kv_norm", "s5_a_re", "s5_a_im", "s5_log_step", "final_g")
ORDER = ("c_ctx", "ada_w", "ada_b", "norm_g", "mla_w_in", "mla_q_norm", "mla_w_uq", "mla_kv_norm", "mla_w_ukv",
         "mla_w_out", "s5_w_in", "s5_a_re", "s5_a_im", "s5_log_step", "s5_b_re", "s5_b_im", "s5_c_re", "s5_c_im",
         "s5_d", "s5_w_glu", "s5_b_glu", "s5_w_out", "final_g")


def kernel(x, c, ctx, c_ctx, ada_w, ada_b, norm_g, mla_w_in, mla_q_norm, mla_w_uq, mla_kv_norm, mla_w_ukv, mla_w_out, s5_w_in, s5_a_re, s5_a_im, s5_log_step, s5_b_re, s5_b_im, s5_c_re, s5_c_im, s5_d, s5_w_glu, s5_b_glu, s5_w_out, final_g, loss_target, m_c_ctx, m_ada_w, m_ada_b, m_norm_g, m_mla_w_in, m_mla_q_norm, m_mla_w_uq, m_mla_kv_norm, m_mla_w_ukv, m_mla_w_out, m_s5_w_in, m_s5_a_re, m_s5_a_im, m_s5_log_step, m_s5_b_re, m_s5_b_im, m_s5_c_re, m_s5_c_im, m_s5_d, m_s5_w_glu, m_s5_b_glu, m_s5_w_out, m_final_g, v_c_ctx, v_ada_w, v_ada_b, v_norm_g, v_mla_w_in, v_mla_q_norm, v_mla_w_uq, v_mla_kv_norm, v_mla_w_ukv, v_mla_w_out, v_s5_w_in, v_s5_a_re, v_s5_a_im, v_s5_log_step, v_s5_b_re, v_s5_b_im, v_s5_c_re, v_s5_c_im, v_s5_d, v_s5_w_glu, v_s5_b_glu, v_s5_w_out, v_final_g):
    w = dict(c_ctx=c_ctx, ada_w=ada_w, ada_b=ada_b, norm_g=norm_g, mla_w_in=mla_w_in, mla_q_norm=mla_q_norm,
             mla_w_uq=mla_w_uq, mla_kv_norm=mla_kv_norm, mla_w_ukv=mla_w_ukv, mla_w_out=mla_w_out, s5_w_in=s5_w_in,
             s5_a_re=s5_a_re, s5_a_im=s5_a_im, s5_log_step=s5_log_step, s5_b_re=s5_b_re, s5_b_im=s5_b_im,
             s5_c_re=s5_c_re, s5_c_im=s5_c_im, s5_d=s5_d, s5_w_glu=s5_w_glu, s5_b_glu=s5_b_glu, s5_w_out=s5_w_out,
             final_g=final_g)
    m = dict(c_ctx=m_c_ctx, ada_w=m_ada_w, ada_b=m_ada_b, norm_g=m_norm_g, mla_w_in=m_mla_w_in, mla_q_norm=m_mla_q_norm,
             mla_w_uq=m_mla_w_uq, mla_kv_norm=m_mla_kv_norm, mla_w_ukv=m_mla_w_ukv, mla_w_out=m_mla_w_out,
             s5_w_in=m_s5_w_in, s5_a_re=m_s5_a_re, s5_a_im=m_s5_a_im, s5_log_step=m_s5_log_step, s5_b_re=m_s5_b_re,
             s5_b_im=m_s5_b_im, s5_c_re=m_s5_c_re, s5_c_im=m_s5_c_im, s5_d=m_s5_d, s5_w_glu=m_s5_w_glu,
             s5_b_glu=m_s5_b_glu, s5_w_out=m_s5_w_out, final_g=m_final_g)
    v = dict(c_ctx=v_c_ctx, ada_w=v_ada_w, ada_b=v_ada_b, norm_g=v_norm_g, mla_w_in=v_mla_w_in, mla_q_norm=v_mla_q_norm,
             mla_w_uq=v_mla_w_uq, mla_kv_norm=v_mla_kv_norm, mla_w_ukv=v_mla_w_ukv, mla_w_out=v_mla_w_out,
             s5_w_in=v_s5_w_in, s5_a_re=v_s5_a_re, s5_a_im=v_s5_a_im, s5_log_step=v_s5_log_step, s5_b_re=v_s5_b_re,
             s5_b_im=v_s5_b_im, s5_c_re=v_s5_c_re, s5_c_im=v_s5_c_im, s5_d=v_s5_d, s5_w_glu=v_s5_w_glu,
             s5_b_glu=v_s5_b_glu, s5_w_out=v_s5_w_out, final_g=v_final_g)

    me = 4 * lax.axis_index("x") + 2 * lax.axis_index("y") + lax.axis_index("c")
    WA = ada_w.shape[2]

    cg = exchange([c], "gather", "gather_c")[0].reshape(NDEV, D)
    cc2 = c_ctx.reshape(1, D)
    ada_b_loc = lax.dynamic_slice_in_dim(ada_b.reshape(2, 3 * D // WA, WA), me, 1, axis=1)
    part = ada_fwd(cg, cc2, ada_w, ada_b_loc, "ada_fwd")
    pg = exchange([part], "gather", "gather_mod")[0]
    mod_l = lax.dynamic_index_in_dim(pg, me, axis=2, keepdims=False).transpose(1, 0, 2).reshape(2, 3 * D)
    mod_c = pg[:, :, NDEV, :].transpose(1, 0, 2).reshape(2, 3 * D)
    mod = jnp.stack([mod_c, mod_l], axis=1)

    def shard(n):
        return _t_shard(w[n], SHARD_ROWS[n]) if n in COL_SHARDED else w[n][0].astype(BF16)

    wgot = exchange([shard(n) for n in L0_BIG], "gather", "gather_w")
    Wt = {n: a.reshape(-1, a.shape[-1]) for n, a in zip(L0_BIG, wgot)}
    Wt["mla_w_in"] = mm(_win_order(), Wt["mla_w_in"], "nn", "w_in_order", out_dtype=BF16)
    vec_bits = lax.bitcast_convert_type(jnp.concatenate([s5_d, s5_b_glu], axis=0), BITS16).reshape(2, -1)
    small = {n: w[n] for n in SMALL_RS}

    lvec, grad_x, dmod, gbig, gsmall, l1_recv = local_step(ctx[0], x[0], loss_target[0], mod, Wt, small,
                                                           [shard(n) for n in L1_BIG] + [vec_bits])
    loss = lax.psum(lvec[0, 0], ("x", "y", "c"))
    grad_x = grad_x[None]

    per_dev = G // NDEV
    recv = dict(zip(L0_BIG, exchange([gbig[n] for n in L0_BIG], "lead", "scatter_grads")))
    recv.update(dict(zip(L1_BIG + VEC_SHARDED, l1_recv)))
    out = {}

    def keep(n, res):
        for key, arr in zip("gdmv", res):
            out[key, n] = arr.reshape(w[n].shape)

    for n in BIG:
        keep(n, adamw(recv[n], w[n][0], m[n][0], v[n][0], "adamw_" + n))
    reduced = sum_slots(l1_recv[len(L1_BIG + VEC_SHARDED):], "sum_chunks")

    kshape = lambda n: w[n].shape if w[n].ndim > 1 else (1, w[n].size)
    got = exchange(list(reduced) + [gsmall[n].reshape(kshape(n)) for n in TINY] + [dmod], "gather", "gather_small")
    chunk_all, tiny_all, dm_all = got[:len(CHUNKED)], got[len(CHUNKED):-1], got[-1]

    dm_cols = lax.dynamic_slice_in_dim(dm_all.reshape(NDEV, 2, 2, 3 * D // WA, WA), me, 1, axis=3)[:, :, :, 0]
    dm_loc = jnp.concatenate([dm_cols[:, :, 1].transpose(1, 0, 2), dm_cols[:, :, 0].transpose(1, 0, 2)], axis=1)
    g_ada_w, dcc_part, g_ada_b = ada_bwd(cg, cc2, ada_w, dm_loc, dm_all.transpose(0, 2, 1, 3).reshape(2 * NDEV, 2, 3 * D), "ada_bwd")
    dcc_all = exchange([dcc_part], "gather", "gather_dcc")[0].reshape(NDEV, D)
    g_c_ctx = cctx_finish(dcc_all, cc2, "cctx_finish")

    flat2 = lambda t: t.reshape(-1, t.shape[-1])
    keep("ada_w", adamw(flat2(g_ada_w)[None], flat2(ada_w), flat2(m_ada_w), flat2(v_ada_w), "adamw_ada"))
    items = []
    for n, g in zip(CHUNKED, chunk_all):
        blk = (1, 1, per_dev) + w[n].shape[3:]
        if n in DENSE:
            g = g.transpose(1, 0, 2, 3).reshape(w[n].shape)
            g_spec = pl.BlockSpec((1, 1, 1) + blk[2:], lambda d, s: (0, 0, d, s, 0, 0))
        else:
            g_spec = pl.BlockSpec((1, 1, 1) + blk[2:], lambda d, s: (0, s, d, 0, 0, 0))
        items.append((g[None], g_spec, w[n], m[n], v[n], pl.BlockSpec(blk, lambda d, s: (0, d, s, 0, 0))))
    for n, res in zip(CHUNKED, adamw_multi(items, (2, NDEV), "adamw_bc")):
        keep(n, res)
    tiny_g = dict(zip(TINY, tiny_all))
    tiny_g.update({n: recv[n] for n in VEC_SHARDED})
    tiny_g["c_ctx"], tiny_g["ada_b"] = g_c_ctx[None], g_ada_b[None]
    names = list(tiny_g)
    items = [(tiny_g[n], _whole(tiny_g[n], 1)) + tuple(t[n].reshape(kshape(n)) for t in (w, m, v))
             + (pl.BlockSpec(kshape(n), lambda i, r=len(kshape(n)): (0,) * r),) for n in names]
    for n, res in zip(names, adamw_multi(items, (1,), "adamw_small")):
        keep(n, res)

    return (loss, grad_x, *[out["g", n] for n in ORDER], *[out["d", n] for n in ORDER],
            *[out["m", n] for n in ORDER], *[out["v", n] for n in ORDER])
```

```python
import math

import numpy as np
import jax
import jax.numpy as jnp
from jax import lax
from jax.experimental import pallas as pl
from jax.experimental.pallas import tpu as pltpu

F32 = jnp.float32
BF16 = jnp.bfloat16

D = 1024
L = 2048
LC = 256
NDEV = 8
GRID_W = 64
EPS = 1e-6
HEADS = 16
NOPE = 64
ROPE = 32
QK = NOPE + ROPE
VD = 64
IN_W = 256 + 128 + ROPE + HEADS * 64
IN_WP = 1536
QL = 256
KVL = 128
SCALE = QK ** -0.5
THETA = 10000.0
G = 64
P = 64
CH = 16
GB = 8
NJ = G // GB
UB = GB * CH
SB = GB * P
SEG = 8
TB = 256
VMEM_LIMIT = 56 * 1024 * 1024
B1, B2, LR, AEPS, WD, STEP = 0.9, 0.999, 0.001, 1e-8, 0.01, 10
MESH_T = pl.DeviceIdType.MESH


def _cp(sem=None):
    return pltpu.CompilerParams(dimension_semantics=sem, vmem_limit_bytes=VMEM_LIMIT)


def _sig(x):
    return 1.0 / (1.0 + jnp.exp(-x))


def _silu(x):
    return x * _sig(x)


def _dsilu(x):
    s = _sig(x)
    return s * (1.0 + x * (1.0 - s))


_GK = math.sqrt(2.0 / math.pi)


def _gelu(x):
    return 0.5 * x * (1.0 + jnp.tanh(_GK * (x + 0.044715 * x * x * x)))


def _dgelu(x):
    t = jnp.tanh(_GK * (x + 0.044715 * x * x * x))
    return 0.5 * (1.0 + t) + 0.5 * x * (1.0 - t * t) * _GK * (1.0 + 3 * 0.044715 * x * x)


def _rs(x):
    return lax.rsqrt(jnp.mean(x * x, axis=-1, keepdims=True) + EPS)


def _sum0(x):
    return jnp.sum(x, axis=0, keepdims=True)


def st_norm_mod(x, g, sc, sh):
    y = x * _rs(x) * g
    return (y * (1.0 + sc) + sh,), ()


def st_norm_mod_bwd(x, dh, dres, g, sc):
    r = _rs(x)
    xn = x * r
    y = xn * g
    dy = dh * (1.0 + sc)
    dxn = dy * g
    dx = r * (dxn - xn * jnp.mean(dxn * xn, axis=-1, keepdims=True))
    return (dres + dx,), (_sum0(dh), _sum0(dh * y), _sum0(dy * xn))


def st_rms(x, g):
    return (x * _rs(x) * g,), ()


def st_rms_bwd(x, dy, g):
    r = _rs(x)
    n = x * r
    dn = dy * g
    dx = r * (dn - n * jnp.mean(dn * n, axis=-1, keepdims=True))
    return (dx,), (_sum0(dy * n),)


def st_rms2(x1, x2, g1, g2):
    return st_rms(x1, g1)[0] + st_rms(x2, g2)[0], ()


def st_rms2_bwd(x1, dy1, x2, dy2, g1, g2):
    (d1,), (s1,) = st_rms_bwd(x1, dy1, g1)
    (d2,), (s2,) = st_rms_bwd(x2, dy2, g2)
    return (d1, d2), (s1, s2)


def st_gate(o, z):
    return (o * _silu(z),), ()


def st_gate_bwd(dog, o, z):
    return (dog * _silu(z), dog * o * _dsilu(z)), ()


def st_resid(x, out, gt):
    return (x + gt * out,), ()


def st_resid_bwd(dx, out, gt):
    return (dx * gt,), (_sum0(dx * out),)


def st_s5a(yssm, u, d):
    y = yssm + d * u
    return (y, _gelu(y)), ()


def st_s5b(y, gl, z, b):
    return (_gelu(y) * _sig(gl + b) * _silu(z),), ()


def st_s5b_bwd(dy3, y, gl, z, b):
    y1 = _gelu(y)
    s = _sig(gl + b)
    dy2 = dy3 * _silu(z)
    dz = dy3 * y1 * s * _dsilu(z)
    dgl = dy2 * y1 * s * (1.0 - s)
    return (dgl, dz, dy2 * s), (_sum0(dgl),)


def st_s5a_bwd(dy1a, dy1b, y, u, d):
    dy = (dy1a + dy1b) * _dgelu(y)
    return (dy, dy * d), (_sum0(dy * u),)


def st_final(x2, tgt, g, mask):
    r = _rs(x2)
    n = x2 * r
    e = n * g - tgt
    dyo = e * (1.0 / D)
    dn = dyo * g
    dx = r * (dn - n * jnp.mean(dn * n, axis=-1, keepdims=True))
    lsum = jnp.sum(_sum0(e * e), axis=1, keepdims=True) * (0.5 / D)
    return (dx * mask,), (_sum0(dyo * n), jnp.broadcast_to(lsum, (1, 128)))


def rowwise(fn, rows, vecs, out_rows, out_sums, name):
    lat_blk = lambda i: jnp.maximum(i - 1, 0)
    arrays, in_specs, pick = [], [], []
    for a in rows:
        if not isinstance(a, tuple):
            a = (a, 0, a.shape[1])
        tag = a[0] if isinstance(a[0], str) else None
        if tag == "cat":
            _, ctx, x = a
            arrays += [ctx, x]
            in_specs += [pl.BlockSpec((TB, ctx.shape[1]), lambda i: (0, 0)),
                         pl.BlockSpec((TB, x.shape[1]), lambda i: (lat_blk(i), 0))]
            pick.append(2)
        elif tag == "lat":
            arrays.append(a[1])
            in_specs.append(pl.BlockSpec((TB, a[1].shape[1]), lambda i: (lat_blk(i), 0)))
            pick.append(1)
        else:
            arr, cb, width = a
            arrays.append(arr)
            in_specs.append(pl.BlockSpec((TB, width), lambda i, cb=cb: (i, cb)))
            pick.append(1)
    T = LC + L
    nin, nv, no = len(arrays), len(vecs), len(out_rows)

    def body(*refs):
        i = pl.program_id(0)
        vals, k = [], 0
        for p in pick:
            if p == 2:
                vals.append(jnp.where(i == 0, refs[k][...], refs[k + 1][...]))
            else:
                vals.append(refs[k][...])
            k += p
        vals += [r[0] for r in refs[nin:nin + nv]]
        outs, sums = fn(*vals)
        for r, o in zip(refs[nin + nv:nin + nv + no], outs):
            r[...] = o.astype(r.dtype)
        sum_refs = refs[nin + nv + no:]
        if sum_refs:
            @pl.when(i <= 1)
            def _():
                for r in sum_refs:
                    r[...] = jnp.zeros_like(r)
            for r, s in zip(sum_refs, sums):
                r[0] += s

    kind = lambda i: (jnp.minimum(i, 1), 0, 0)
    in_specs += [pl.BlockSpec((1, 1, v.shape[2]), kind) for v in vecs]
    out_specs, out_shape = [], []
    for o in out_rows:
        lat = len(o) == 3
        out_specs.append(pl.BlockSpec((TB, o[0]), (lambda i: (lat_blk(i), 0)) if lat else (lambda i: (i, 0))))
        out_shape.append(jax.ShapeDtypeStruct((L if lat else T, o[0]), o[1]))
    out_specs += [pl.BlockSpec((1, 1, c), kind) for c in out_sums]
    out_shape += [jax.ShapeDtypeStruct((2, 1, c), F32) for c in out_sums]
    res = pl.pallas_call(body, grid=(T // TB,), in_specs=in_specs, out_specs=out_specs, out_shape=out_shape,
                         compiler_params=_cp(("arbitrary",)), name=name)(*arrays, *vecs)
    return res[:no], res[no:]


_DN = {"nn": (((1,), (0,)), ((), ())), "nt": (((1,), (1,)), ((), ())), "tn": (((0,), (0,)), ((), ()))}


def mm(a, b, mode, name, out_dtype=F32, tm=None, tn=None, shard_out=False):
    if mode == "nn":
        (M, K), (_, N) = a.shape, b.shape
    elif mode == "nt":
        (M, K), (N, _) = a.shape, b.shape
    else:
        (K, M), (_, N) = a.shape, b.shape
    if tm is None:
        tm = next((t for t in (768, 512, 256) if M % t == 0 and M > t), M)
    tn = N if tn is None else tn
    dn = _DN[mode]

    def body(a_ref, b_ref, o_ref):
        o_ref[...] = lax.dot_general(a_ref[...].astype(BF16), b_ref[...].astype(BF16), dn,
                                     preferred_element_type=F32).astype(o_ref.dtype)

    if shard_out:
        def body(a_ref, b_ref, o_ref):
            av = a_ref[...].astype(BF16)
            for j in range(N // tn):
                bj = b_ref[pl.ds(j * tn, tn), :] if mode == "nt" else b_ref[:, pl.ds(j * tn, tn)]
                o_ref[j] = lax.dot_general(av, bj.astype(BF16), dn, preferred_element_type=F32).astype(o_ref.dtype)

        a_spec = pl.BlockSpec((K, tm), lambda i: (0, i)) if mode == "tn" else pl.BlockSpec((tm, K), lambda i: (i, 0))
        return pl.pallas_call(body, grid=(M // tm,), in_specs=[a_spec, pl.BlockSpec(b.shape, lambda i: (0, 0))],
                              out_specs=pl.BlockSpec((N // tn, tm, tn), lambda i: (0, i, 0)),
                              out_shape=jax.ShapeDtypeStruct((N // tn, M, tn), out_dtype),
                              compiler_params=_cp(("parallel",)), name=name)(a, b)
    a_spec = pl.BlockSpec((K, tm), lambda i, j: (0, i)) if mode == "tn" else pl.BlockSpec((tm, K), lambda i, j: (i, 0))
    b_spec = pl.BlockSpec((tn, K), lambda i, j: (j, 0)) if mode == "nt" else pl.BlockSpec((K, tn), lambda i, j: (0, j))
    return pl.pallas_call(body, grid=(M // tm, N // tn), in_specs=[a_spec, b_spec],
                          out_specs=pl.BlockSpec((tm, tn), lambda i, j: (i, j)), out_shape=jax.ShapeDtypeStruct((M, N), out_dtype),
                          compiler_params=_cp(("parallel", "arbitrary")), name=name)(a, b)


def _rope_tables(T, width=QK, first=NOPE):
    nlat = T - LC
    pos = np.arange(nlat)
    row, col = pos // GRID_W, pos % GRID_W
    half = ROPE // 2
    inv = 1.0 / (THETA ** (np.arange(0, half, 2, dtype=np.float64) / half))
    cosf = np.ones((T, width), np.float64)
    sinf = np.zeros((T, width), np.float64)
    perm = np.zeros((width, width), np.float32)
    for m in range(ROPE):
        j = first + m
        blk, w = m // half, m % half
        ang = (row if blk == 0 else col)[:, None] * inv[None, :]
        f = w % (half // 2)
        cosf[LC:, j] = np.cos(ang[:, f])
        if w < half // 2:
            sinf[LC:, j] = -np.sin(ang[:, f])
            perm[j + half // 2, j] = 1.0
        else:
            sinf[LC:, j] = np.sin(ang[:, f])
            perm[j - half // 2, j] = 1.0
    return jnp.asarray(cosf, F32), jnp.asarray(sinf, F32), jnp.asarray(perm, BF16), jnp.asarray(perm.T, BF16)


def _exact_perm(x, pm):
    hi = x.astype(BF16)
    r1 = x - hi.astype(F32)
    mid = r1.astype(BF16)
    lo = (r1 - mid.astype(F32)).astype(BF16)
    dot = lambda a: jnp.dot(a, pm, preferred_element_type=F32)
    return dot(hi) + dot(mid) + dot(lo)


def _rot(x, cv, sv, pv, inverse):
    if inverse:
        return x * cv + _exact_perm(x * sv, pv)
    return x * cv + _exact_perm(x, pv) * sv


def rope(x, cosf, sinf, pm, inverse, out_dtype, name, scale=1.0):
    H, T, _ = x.shape

    def body(x_ref, c_ref, s_ref, p_ref, o_ref):
        cv, sv, pv = c_ref[...], s_ref[...], p_ref[...]
        for h in range(H):
            o_ref[h] = (_rot(x_ref[h], cv, sv, pv, inverse) * scale).astype(o_ref.dtype)

    return pl.pallas_call(
        body, grid=(T // TB,),
        in_specs=[pl.BlockSpec((H, TB, QK), lambda i: (0, i, 0)), pl.BlockSpec((TB, QK), lambda i: (i, 0)),
                  pl.BlockSpec((TB, QK), lambda i: (i, 0)), pl.BlockSpec((QK, QK), lambda i: (0, 0))],
        out_specs=pl.BlockSpec((H, TB, QK), lambda i: (0, i, 0)), out_shape=jax.ShapeDtypeStruct((H, T, QK), out_dtype),
        compiler_params=_cp(("parallel",)), name=name)(x, cosf, sinf, pm)


KVW = NOPE + VD


def _kv_selectors():
    s_kn = np.zeros((KVW, QK), np.float32)
    s_kr = np.zeros((128, QK), np.float32)
    s_v = np.zeros((KVW, VD), np.float32)
    for l in range(NOPE):
        s_kn[l, l] = 1.0
    for l in range(ROPE):
        s_kr[l, NOPE + l] = 1.0
    for l in range(VD):
        s_v[NOPE + l, l] = 1.0
    return s_kn, s_kr, s_v


def project_q(cqn, w, cosf, sinf, pm, name):
    T = cqn.shape[0]

    def body(a_ref, w_ref, c_ref, s_ref, p_ref, o_ref):
        a, cv, sv, pv = a_ref[...], c_ref[...], s_ref[...], p_ref[...]
        for h in range(HEADS):
            qh = _dotf(a, w_ref[pl.ds(h * QK, QK), :], "nt")
            o_ref[h] = (_rot(qh, cv, sv, pv, False) * SCALE).astype(BF16)

    rows = lambda c: pl.BlockSpec((TB, c), lambda i: (i, 0))
    const = lambda x: pl.BlockSpec(x.shape, lambda i: (0, 0))
    return pl.pallas_call(
        body, grid=(T // TB,), in_specs=[rows(QL), const(w), rows(QK), rows(QK), const(pm)],
        out_specs=pl.BlockSpec((HEADS, TB, QK), lambda i: (0, i, 0)), out_shape=jax.ShapeDtypeStruct((HEADS, T, QK), BF16),
        compiler_params=_cp(("parallel",)), name=name)(cqn, w, cosf, sinf, pm)


def project_kv(ckvn, w, p0, kr_block, name):
    T = ckvn.shape[0]
    cosf, sinf, pm, _ = _rope_tables(T, 128, 0)
    s_kn, s_kr, s_v = (jnp.asarray(s, BF16) for s in _kv_selectors())

    def body(a_ref, w_ref, kr_ref, c_ref, s_ref, p_ref, skn_ref, skr_ref, sv_ref, k_ref, v_ref):
        a = a_ref[...]
        krr = _rot(kr_ref[...], c_ref[...], s_ref[...], p_ref[...], False).astype(BF16)
        kr_part = jnp.dot(krr, skr_ref[...], preferred_element_type=F32)
        for h in range(HEADS):
            kvb = _dotf(a, w_ref[pl.ds(h * KVW, KVW), :], "nt").astype(BF16)
            k_ref[h] = (jnp.dot(kvb, skn_ref[...], preferred_element_type=F32) + kr_part).astype(BF16)
            v_ref[h] = jnp.dot(kvb, sv_ref[...], preferred_element_type=F32).astype(BF16)

    rows = lambda c: pl.BlockSpec((TB, c), lambda i: (i, 0))
    const = lambda x: pl.BlockSpec(x.shape, lambda i: (0, 0))
    return pl.pallas_call(
        body, grid=(T // TB,),
        in_specs=[rows(KVL), const(w), pl.BlockSpec((TB, 128), lambda i: (i, kr_block)),
                  rows(128), rows(128), const(pm), const(s_kn), const(s_kr), const(s_v)],
        out_specs=[pl.BlockSpec((HEADS, TB, QK), lambda i: (0, i, 0)), pl.BlockSpec((HEADS, TB, VD), lambda i: (0, i, 0))],
        out_shape=[jax.ShapeDtypeStruct((HEADS, T, QK), BF16), jax.ShapeDtypeStruct((HEADS, T, VD), BF16)],
        compiler_params=_cp(("parallel",)), name=name)(ckvn, w, p0, cosf, sinf, pm, s_kn, s_kr, s_v)


def split_kv_grads(dk, dv, name):
    H, T, _ = dk.shape
    cosf, sinf, _, pmt = _rope_tables(T, 128, 0)
    s_kn, s_kr, s_v = _kv_selectors()
    s_knt, s_krt, s_vt = (jnp.asarray(s.T, BF16) for s in (s_kn, s_kr, s_v))

    def body(dk_ref, dv_ref, c_ref, s_ref, p_ref, skn_ref, skr_ref, sv_ref, dkv_ref, dkr_ref):
        total = None
        for h in range(H):
            dkh = dk_ref[h]
            total = dkh if total is None else total + dkh
            dkv_ref[:, pl.ds(h * KVW, KVW)] = (
                jnp.dot(dkh.astype(BF16), skn_ref[...], preferred_element_type=F32)
                + jnp.dot(dv_ref[h].astype(BF16), sv_ref[...], preferred_element_type=F32)).astype(BF16)
        dkr_ref[...] = _rot(_exact_perm(total, skr_ref[...]), c_ref[...], s_ref[...], p_ref[...], True)

    rows = lambda c: pl.BlockSpec((TB, c), lambda i: (i, 0))
    const = lambda a: pl.BlockSpec(a.shape, lambda i: (0, 0))
    return pl.pallas_call(
        body, grid=(T // TB,),
        in_specs=[pl.BlockSpec((H, TB, QK), lambda i: (0, i, 0)), pl.BlockSpec((H, TB, VD), lambda i: (0, i, 0)),
                  rows(128), rows(128), const(pmt), const(s_knt), const(s_krt), const(s_vt)],
        out_specs=[rows(H * KVW), rows(128)],
        out_shape=[jax.ShapeDtypeStruct((T, H * KVW), BF16), jax.ShapeDtypeStruct((T, 128), F32)],
        compiler_params=_cp(("parallel",)), name=name)(dk, dv, cosf, sinf, pmt, s_knt, s_krt, s_vt)


def _by_query_block(run, T):
    @pl.when(pl.program_id(1) == 0)
    def _():
        run(LC)

    @pl.when(pl.program_id(1) > 0)
    def _():
        run(T)


def _with_rider(body, nin, nout, ride, grid):
    if ride is None:
        return body
    n = ride.n

    def wrapped(*refs):
        ins, xs = refs[:nin], refs[nin:nin + n]
        outs, got = refs[nin + n:nin + n + nout], refs[nin + n + nout:nin + 2 * n + nout]
        sems = refs[nin + 2 * n + nout:]
        step = pl.program_id(0) * grid[1] + pl.program_id(1)

        @pl.when(step == 0)
        def _():
            ride.start(xs, got, sems)

        body(*ins, *outs)

        @pl.when(step == grid[0] * grid[1] - 1)
        def _():
            ride.finish(xs, got, sems)

    return wrapped


def _ride_call(body, grid, in_specs, out_specs, out_shape, ride, rode, name, args):
    if ride is None:
        return pl.pallas_call(body, grid=grid, in_specs=in_specs, out_specs=out_specs, out_shape=out_shape,
                              compiler_params=_cp(("parallel", "arbitrary")), name=name)(*args), []
    res = pl.pallas_call(
        _with_rider(body, len(in_specs), len(out_specs), ride, grid), grid=grid,
        in_specs=in_specs + ride.specs, out_specs=out_specs + ride.specs, out_shape=out_shape + ride.out_shape,
        scratch_shapes=ride.scratch,
        compiler_params=pltpu.CompilerParams(dimension_semantics=("arbitrary", "arbitrary"), vmem_limit_bytes=VMEM_LIMIT,
                                             has_side_effects=True), name=name)(*args, *rode)
    return res[:len(out_specs)], res[len(out_specs):]


def attn_fwd(q, k, v, name, rode=None, modes=None):
    H, T, _ = q.shape

    def body(q_ref, k_ref, v_ref, o_ref, lse_ref):
        def run(nk):
            s = _dotf(q_ref[0], k_ref[0, pl.ds(0, nk), :], "nt")
            m = jnp.max(s, axis=1, keepdims=True)
            p = jnp.exp(s - m)
            l = jnp.sum(p, axis=1, keepdims=True)
            o = jnp.dot(p.astype(BF16), v_ref[0, pl.ds(0, nk), :], preferred_element_type=F32)
            o_ref[0] = o / l
            lse_ref[0] = m + jnp.log(l)

        _by_query_block(run, T)

    return _ride_call(
        body, (H, T // TB),
        [pl.BlockSpec((1, TB, QK), lambda h, i: (h, i, 0)), pl.BlockSpec((1, T, QK), lambda h, i: (h, 0, 0)),
         pl.BlockSpec((1, T, VD), lambda h, i: (h, 0, 0))],
        [pl.BlockSpec((1, TB, VD), lambda h, i: (h, i, 0)), pl.BlockSpec((1, TB, 1), lambda h, i: (h, i, 0))],
        [jax.ShapeDtypeStruct((H, T, VD), F32), jax.ShapeDtypeStruct((H, T, 1), F32)],
        Exchange(rode, modes) if rode else None, rode, name, (q, k, v))


def attn_bwd(q, k, v, o, lse, do, name, rode=None, modes=None):
    H, T, _ = q.shape

    def body(q_ref, k_ref, v_ref, o_ref, lse_ref, do_ref, dq_ref, dk_ref, dv_ref):
        i = pl.program_id(1)

        @pl.when(i == 0)
        def _():
            dk_ref[...] = jnp.zeros_like(dk_ref)
            dv_ref[...] = jnp.zeros_like(dv_ref)

        def run(nk):
            keys = pl.ds(0, nk)
            qv, kv, dov = q_ref[0], k_ref[0, keys, :], do_ref[0]
            p = jnp.exp(_dotf(qv, kv, "nt") - lse_ref[0])
            delta = jnp.sum(dov * o_ref[0], axis=1, keepdims=True)
            dob = dov.astype(BF16)
            dv_ref[0, keys, :] += _dotf(p.astype(BF16), dob, "tn")
            dp = _dotf(dob, v_ref[0, keys, :], "nt")
            ds = (p * (dp - delta)).astype(BF16)
            dq_ref[0] = jnp.dot(ds, kv, preferred_element_type=F32)
            dk_ref[0, keys, :] += _dotf(ds, qv, "tn")

        _by_query_block(run, T)

    blk = lambda c: pl.BlockSpec((1, TB, c), lambda h, i: (h, i, 0))
    full = lambda c: pl.BlockSpec((1, T, c), lambda h, i: (h, 0, 0))
    return _ride_call(
        body, (H, T // TB), [blk(QK), full(QK), full(VD), blk(VD), blk(1), blk(VD)], [blk(QK), full(QK), full(VD)],
        [jax.ShapeDtypeStruct((H, T, QK), F32), jax.ShapeDtypeStruct((H, T, QK), F32), jax.ShapeDtypeStruct((H, T, VD), F32)],
        Exchange(rode, modes) if rode else None, rode, name, (q, k, v, o, lse, do))


def disc_fwd(a_re, a_im, ls, name):
    def body(ar_ref, ai_ref, ls_ref, lr_ref, li_ref, fr_ref, fi_ref):
        ar, ai = ar_ref[...], ai_ref[...]
        dt = jnp.exp(ls_ref[...])
        mag = jnp.exp(ar * dt)
        lr = mag * jnp.cos(ai * dt)
        li = mag * jnp.sin(ai * dt)
        den = ar * ar + ai * ai
        nr = lr - 1.0
        lr_ref[...] = lr
        li_ref[...] = li
        fr_ref[...] = (nr * ar + li * ai) / den
        fi_ref[...] = (li * ar - nr * ai) / den

    return pl.pallas_call(body, out_shape=[jax.ShapeDtypeStruct(a_re.shape, F32)] * 4, name=name)(a_re, a_im, ls)


def disc_b(f_re, f_im, b_re, b_im, name):
    def body(fr_ref, fi_ref, br_ref, bi_ref, or_ref, oi_ref):
        fr, fi, br, bi = fr_ref[...], fi_ref[...], br_ref[...], bi_ref[...]
        or_ref[...] = fr * br - fi * bi
        oi_ref[...] = fr * bi + fi * br

    fs, bs = _disc_b_specs()
    return pl.pallas_call(body, grid=(2, G * P // DISC_ROWS), in_specs=[fs, fs, bs, bs], out_specs=[bs, bs],
                          out_shape=[jax.ShapeDtypeStruct(b_re.shape, F32)] * 2, name=name)(f_re, f_im, b_re, b_im)


DISC_ROWS = 1024


def _disc_b_specs():
    return (pl.BlockSpec((1, DISC_ROWS, 1), lambda d, i: (d, i, 0)), pl.BlockSpec((1, DISC_ROWS, CH), lambda d, i: (d, i, 0)))


def disc_b_bwd(f_re, f_im, b_re, b_im, dbb_re, dbb_im, name):
    def body(fr_ref, fi_ref, br_ref, bi_ref, dr_ref, di_ref, dbr_ref, dbi_ref, dfr_ref, dfi_ref):
        fr, fi, br, bi, dr, di = fr_ref[...], fi_ref[...], br_ref[...], bi_ref[...], dr_ref[...], di_ref[...]
        dbr_ref[...] = fr * dr + fi * di
        dbi_ref[...] = fr * di - fi * dr
        dfr_ref[...] = jnp.sum(dr * br + di * bi, axis=-1, keepdims=True)
        dfi_ref[...] = jnp.sum(di * br - dr * bi, axis=-1, keepdims=True)

    fs, bs = _disc_b_specs()
    return pl.pallas_call(body, grid=(2, G * P // DISC_ROWS), in_specs=[fs, fs, bs, bs, bs, bs], out_specs=[bs, bs, fs, fs],
                          out_shape=[jax.ShapeDtypeStruct(b_re.shape, F32)] * 2 + [jax.ShapeDtypeStruct(f_re.shape, F32)] * 2,
                          name=name)(f_re, f_im, b_re, b_im, dbb_re, dbb_im)


def disc_a_bwd(a_re, a_im, ls, dlr, dli, dfr, dfi, name):
    def body(ar_ref, ai_ref, ls_ref, dlr_ref, dli_ref, dfr_ref, dfi_ref, dar_ref, dai_ref, dls_ref):
        ar, ai = ar_ref[...], ai_ref[...]
        dt = jnp.exp(ls_ref[...])
        mag = jnp.exp(ar * dt)
        cs, sn = jnp.cos(ai * dt), jnp.sin(ai * dt)
        lr, li = mag * cs, mag * sn
        den = ar * ar + ai * ai
        nr = lr - 1.0
        f_re = (nr * ar + li * ai) / den
        f_im = (li * ar - nr * ai) / den
        dn1 = dfr_ref[...] / den
        dn2 = dfi_ref[...] / den
        dden = -(dfr_ref[...] * f_re + dfi_ref[...] * f_im) / den
        dlr_t = dlr_ref[...] + dn1 * ar - dn2 * ai
        dli_t = dli_ref[...] + dn1 * ai + dn2 * ar
        dar = dn1 * nr + dn2 * li + dden * 2.0 * ar
        dai = dn1 * li - dn2 * nr + dden * 2.0 * ai
        dmag = dlr_t * cs + dli_t * sn
        dth = dli_t * lr - dlr_t * li
        dar_ref[...] = dar + dmag * mag * dt
        dai_ref[...] = dai + dth * dt
        dls_ref[...] = jnp.sum(dmag * mag * ar + dth * ai, axis=-1, keepdims=True) * dt

    return pl.pallas_call(body, out_shape=[jax.ShapeDtypeStruct(a_re.shape, F32)] * 2 +
                          [jax.ShapeDtypeStruct(ls.shape, F32)], name=name)(a_re, a_im, ls, dlr, dli, dfr, dfi)


def _cpow(lr, li, n):
    rr, ri = None, None
    br, bi = lr, li
    while n:
        if n & 1:
            if rr is None:
                rr, ri = br, bi
            else:
                rr, ri = rr * br - ri * bi, rr * bi + ri * br
        n >>= 1
        if n:
            br, bi = br * br - bi * bi, 2.0 * br * bi
    return rr, ri


UNROLL = 4


def _seg_scan(xre, xim, lam8, pw, base, seglen, rev, init, fin_re, fin_im, ini_re, ini_im, prev=None):
    lr, li = lam8

    def rows(t):
        return pl.ds(pl.multiple_of(base + t * SEG, SEG), SEG)

    tmap = (lambda n: seglen - 1 - n) if rev else (lambda n: n)
    zero = jnp.zeros((SEG, SB), F32)

    def advance(c, t):
        a, b = c
        return lr * a - li * b + xre[rows(t), :], lr * b + li * a + xim[rows(t), :]

    fin = lax.fori_loop(0, seglen, lambda n, c: advance(c, tmap(n)), (zero, zero), unroll=UNROLL)
    fin_re[...] = fin[0]
    fin_im[...] = fin[1]
    (cr, ci), (pr, pi) = init, pw
    for i in (range(SEG - 1, -1, -1) if rev else range(SEG)):
        ini_re[pl.ds(i, 1), :] = cr
        ini_im[pl.ds(i, 1), :] = ci
        cr, ci = pr * cr - pi * ci + fin_re[pl.ds(i, 1), :], pr * ci + pi * cr + fin_im[pl.ds(i, 1), :]
    start = (ini_re[...], ini_im[...])

    def store(c, t):
        na, nb = advance(c, t)
        xre[rows(t), :] = na
        xim[rows(t), :] = nb
        return na, nb

    if prev is None:
        lax.fori_loop(0, seglen, lambda n, c: store(c, tmap(n)), start, unroll=UNROLL)
        return (cr, ci), None

    sre, sim, s_ini_re, s_ini_im = prev

    def acc_step(c, t, pre, pim):
        na, nb = store(c[:2], t)
        return na, nb, c[2] + na * pre + nb * pim, c[3] + nb * pre - na * pim

    def body(n, c):
        t = tmap(n)
        tp = t - 1 if rev else t + 1
        return acc_step(c, t, sre[rows(tp), :], sim[rows(tp), :])

    c = lax.fori_loop(0, seglen - 1, body, start + (zero, zero), unroll=UNROLL)
    c = acc_step(c, 0 if rev else seglen - 1, s_ini_re[...], s_ini_im[...])
    return (cr, ci), c[2:]


def _lam_tiles(lr, li, lens, conj=False):
    if conj:
        li = -li
    lam8 = (jnp.broadcast_to(lr, (SEG, SB)), jnp.broadcast_to(li, (SEG, SB)))
    return lam8, [_cpow(lr, li, n) for n in lens]


def _stretches(T):
    return ((0, LC // SEG), (LC, (T - LC) // SEG))


def _to_seg_order(src, dst, T):
    for base, seglen in _stretches(T):
        def body(t, carry, base=base, seglen=seglen):
            dst[pl.ds(pl.multiple_of(base + t * SEG, SEG), SEG), :] = src[pl.ds(base + t, SEG, stride=seglen), :]
            return carry
        lax.fori_loop(0, seglen, body, 0, unroll=8)


def _from_seg_order(src, dst, T):
    for base, seglen in _stretches(T):
        def body(t, carry, base=base, seglen=seglen):
            dst[pl.ds(base + t, SEG, stride=seglen), :] = src[pl.ds(pl.multiple_of(base + t * SEG, SEG), SEG), :]
            return carry
        lax.fori_loop(0, seglen, body, 0, unroll=8)


def _scan_specs(T):
    ublk = pl.BlockSpec((T, UB), lambda j: (0, j))
    lam = pl.BlockSpec((2, 1, 1, SB), lambda j: (0, j, 0, 0))
    mat = pl.BlockSpec((2, 1, UB, P), lambda j: (0, j, 0, 0))
    return ublk, lam, mat


def _dotf(a, b, mode="nn"):
    return lax.dot_general(a, b, _DN[mode], preferred_element_type=F32)


def _diag_mask():
    r = lax.broadcasted_iota(jnp.int32, (UB, SB), 0)
    c = lax.broadcasted_iota(jnp.int32, (UB, SB), 1)
    return lax.shift_right_logical(r, int(math.log2(CH))) == lax.shift_right_logical(c, int(math.log2(P)))


def _expand(m):
    p = lax.broadcasted_iota(jnp.int32, (P, SB), 0)
    c = lax.broadcasted_iota(jnp.int32, (P, SB), 1)
    tile = jnp.where(lax.bitwise_and(c, P - 1) == p, 1.0, 0.0).astype(BF16)
    wide = jnp.dot(m.astype(BF16), tile, preferred_element_type=F32)
    return jnp.where(_diag_mask(), wide, 0.0).astype(BF16)


def _collapse(full):
    c = lax.broadcasted_iota(jnp.int32, (SB, P), 0)
    p = lax.broadcasted_iota(jnp.int32, (SB, P), 1)
    pick = jnp.where(lax.bitwise_and(c, P - 1) == p, 1.0, 0.0).astype(BF16)
    return _exact_perm(jnp.where(_diag_mask(), full, 0.0), pick)


def _zero_state():
    return jnp.zeros((1, SB), F32), jnp.zeros((1, SB), F32)


def scan_fwd(u, lam_re, lam_im, bre, bim, cre, cim, name):
    T = u.shape[0]
    s_ctx, s_lat = LC // SEG, (T - LC) // SEG

    def body(u_ref, lr_ref, li_ref, bre_ref, bim_ref, cre_ref, cim_ref, y_ref, us, ys, sre, sim, fre, fim, ire, iim):
        _to_seg_order(u_ref, us, T)
        ub = us[...].astype(BF16)
        for d in range(2):
            lam8, (pw_c, pw_l) = _lam_tiles(lr_ref[d, 0], li_ref[d, 0], (s_ctx, s_lat))
            sre[...] = _dotf(ub, _expand(bre_ref[d, 0]))
            sim[...] = _dotf(ub, _expand(bim_ref[d, 0]))
            end_c, _ = _seg_scan(sre, sim, lam8, pw_c, 0, s_ctx, bool(d), _zero_state(), fre, fim, ire, iim)
            _seg_scan(sre, sim, lam8, pw_l, LC, s_lat, bool(d), end_c, fre, fim, ire, iim)
            y = (_dotf(sre[...].astype(BF16), _expand(cre_ref[d, 0]), "nt")
                 - _dotf(sim[...].astype(BF16), _expand(cim_ref[d, 0]), "nt"))
            if d == 0:
                ys[...] = y
            else:
                ys[...] += y
        _from_seg_order(ys, y_ref, T)

    ublk, lam, mat = _scan_specs(T)
    return pl.pallas_call(
        body, grid=(NJ,), in_specs=[ublk, lam, lam, mat, mat, mat, mat], out_specs=ublk,
        out_shape=jax.ShapeDtypeStruct((T, G * CH), F32),
        scratch_shapes=[pltpu.VMEM((T, UB), F32)] * 2 + [pltpu.VMEM((T, SB), F32)] * 2 + [pltpu.VMEM((SEG, SB), F32)] * 4,
        compiler_params=_cp(("arbitrary",)), name=name)(u, lam_re, lam_im, bre, bim, cre, cim)


def scan_bwd(u, dy, lam_re, lam_im, bre, bim, cre, cim, name):
    T = u.shape[0]
    s_ctx, s_lat = LC // SEG, (T - LC) // SEG

    def body(u_ref, dy_ref, lr_ref, li_ref, bre_ref, bim_ref, cre_ref, cim_ref,
             du_ref, dlr_ref, dli_ref, dbre_ref, dbim_ref, dcre_ref, dcim_ref,
             us, dys, dus, sre, sim, gre, gim, fre, fim, ic_re, ic_im, il_re, il_im, jre, jim):
        _to_seg_order(u_ref, us, T)
        _to_seg_order(dy_ref, dys, T)
        ub, dyb = us[...].astype(BF16), dys[...].astype(BF16)
        for d in range(2):
            rev = bool(d)
            lam8, (pw_c, pw_l) = _lam_tiles(lr_ref[d, 0], li_ref[d, 0], (s_ctx, s_lat))
            cam8, (cw_c, cw_l) = _lam_tiles(lr_ref[d, 0], li_ref[d, 0], (s_ctx, s_lat), conj=True)
            bre_v, bim_v = _expand(bre_ref[d, 0]), _expand(bim_ref[d, 0])
            sre[...] = _dotf(ub, bre_v)
            sim[...] = _dotf(ub, bim_v)
            end_c, _ = _seg_scan(sre, sim, lam8, pw_c, 0, s_ctx, rev, _zero_state(), fre, fim, ic_re, ic_im)
            _seg_scan(sre, sim, lam8, pw_l, LC, s_lat, rev, end_c, fre, fim, il_re, il_im)
            gre[...] = _dotf(dyb, _expand(cre_ref[d, 0]))
            gim[...] = -_dotf(dyb, _expand(cim_ref[d, 0]))
            end_g, acc_l = _seg_scan(gre, gim, cam8, cw_l, LC, s_lat, not rev, _zero_state(), fre, fim, jre, jim,
                                     prev=(sre, sim, il_re, il_im))
            _, acc_c = _seg_scan(gre, gim, cam8, cw_c, 0, s_ctx, not rev, end_g, fre, fim, jre, jim,
                                 prev=(sre, sim, ic_re, ic_im))
            dlr_ref[d, 0] = _sum0(acc_l[0] + acc_c[0])
            dli_ref[d, 0] = _sum0(acc_l[1] + acc_c[1])
            grb, gib = gre[...].astype(BF16), gim[...].astype(BF16)
            du = _dotf(grb, bre_v, "nt") + _dotf(gib, bim_v, "nt")
            if d == 0:
                dus[...] = du
            else:
                dus[...] += du
            dbre_ref[d, 0] = _collapse(_dotf(ub, grb, "tn"))
            dbim_ref[d, 0] = _collapse(_dotf(ub, gib, "tn"))
            dcre_ref[d, 0] = _collapse(_dotf(dyb, sre[...].astype(BF16), "tn"))
            dcim_ref[d, 0] = -_collapse(_dotf(dyb, sim[...].astype(BF16), "tn"))
        _from_seg_order(dus, du_ref, T)

    ublk, lam, mat = _scan_specs(T)
    lam_s = jax.ShapeDtypeStruct(lam_re.shape, F32)
    mat_s = jax.ShapeDtypeStruct(bre.shape, F32)
    return pl.pallas_call(
        body, grid=(NJ,), in_specs=[ublk, ublk, lam, lam, mat, mat, mat, mat],
        out_specs=[ublk, lam, lam, mat, mat, mat, mat],
        out_shape=[jax.ShapeDtypeStruct((T, G * CH), F32), lam_s, lam_s, mat_s, mat_s, mat_s, mat_s],
        scratch_shapes=[pltpu.VMEM((T, UB), F32)] * 3 + [pltpu.VMEM((T, SB), F32)] * 4 + [pltpu.VMEM((SEG, SB), F32)] * 8,
        compiler_params=_cp(("arbitrary",)), name=name)(u, dy, lam_re, lam_im, bre, bim, cre, cim)


class Exchange:
    def __init__(self, xs, modes):
        self.n = len(xs)
        self.modes = [modes] * self.n if isinstance(modes, (str, int)) else list(modes)
        self.out_shape = [jax.ShapeDtypeStruct(self._shape(x, md), x.dtype) for x, md in zip(xs, self.modes)]
        self.scratch = [pltpu.SemaphoreType.DMA((NDEV - 1, self.n)), pltpu.SemaphoreType.DMA((NDEV - 1, self.n)),
                        pltpu.SemaphoreType.DMA((self.n,))]
        self.specs = [pl.BlockSpec(memory_space=pl.ANY)] * self.n

    @staticmethod
    def _shape(x, mode):
        if mode == "gather":
            return (NDEV,) + tuple(x.shape)
        return tuple(x.shape) if mode == "lead" else (NDEV, x.shape[0], mode) + tuple(x.shape[2:])

    @staticmethod
    def _piece(x_ref, mode, dev):
        if mode == "gather":
            return x_ref
        return x_ref.at[dev] if mode == "lead" else x_ref.at[:, pl.ds(dev * mode, mode)]

    def _copies(self, x_refs, out_refs, sems):
        send_sems, recv_sems, local_sems = sems
        mx, my, mc = lax.axis_index("x"), lax.axis_index("y"), lax.axis_index("c")
        me = 4 * mx + 2 * my + mc
        local = [pltpu.make_async_copy(self._piece(x_ref, self.modes[a], me), out_ref.at[me], local_sems.at[a])
                 for a, (x_ref, out_ref) in enumerate(zip(x_refs, out_refs))]
        sends, recvs = [], []
        for k in range(1, NDEV):
            peer = (1 - mx if k & 4 else mx, 1 - my if k & 2 else my, 1 - mc if k & 1 else mc)
            pid = 4 * peer[0] + 2 * peer[1] + peer[2]
            for a, (x_ref, out_ref) in enumerate(zip(x_refs, out_refs)):
                src = self._piece(x_ref, self.modes[a], pid)
                sems_k = dict(send_sem=send_sems.at[k - 1, a], recv_sem=recv_sems.at[k - 1, a], device_id=peer,
                              device_id_type=MESH_T)
                sends.append(pltpu.make_async_remote_copy(src_ref=src, dst_ref=out_ref.at[me], **sems_k))
                recvs.append(pltpu.make_async_remote_copy(src_ref=src, dst_ref=out_ref.at[pid], **sems_k))
        return local, sends, recvs

    def start(self, x_refs, out_refs, sems):
        local, sends, _ = self._copies(x_refs, out_refs, sems)
        for cp in local + sends:
            cp.start()

    def finish(self, x_refs, out_refs, sems):
        local, sends, recvs = self._copies(x_refs, out_refs, sems)
        for cp in recvs:
            cp.wait_recv()
        for cp in sends:
            cp.wait_send()
        for cp in local:
            cp.wait()


def exchange(xs, modes, name):
    ex = Exchange(xs, modes)
    n = ex.n

    def body(*refs):
        ex.start(refs[:n], refs[n:2 * n], refs[2 * n:])
        ex.finish(refs[:n], refs[n:2 * n], refs[2 * n:])

    return pl.pallas_call(body, in_specs=ex.specs, out_specs=ex.specs, out_shape=ex.out_shape, scratch_shapes=ex.scratch,
                          compiler_params=pltpu.CompilerParams(has_side_effects=True), name=name)(*xs)


def _dot_f32(a, b, dn):
    return lax.dot_general(a, b, dn, preferred_element_type=F32, precision=lax.Precision.HIGHEST)


def ada_fwd(cg, c_ctx, ada_w, ada_b_loc, name):
    W = ada_w.shape[2]

    def body(cg_ref, cc_ref, w_ref, b_ref, o_ref):
        a = jnp.concatenate([_silu(cg_ref[...]), jnp.broadcast_to(_silu(cc_ref[...]), (NDEV, D))], axis=0)
        for i in range(2):
            o_ref[i] = _dot_f32(a, w_ref[i], _DN["nn"]) + b_ref[i]

    return pl.pallas_call(body, out_shape=jax.ShapeDtypeStruct((2, 2 * NDEV, W), F32),
                          compiler_params=_cp(), name=name)(cg, c_ctx, ada_w, ada_b_loc)


def ada_bwd(cg, c_ctx, ada_w, dm_loc, dm_all, name):
    W = ada_w.shape[2]

    def body(cg_ref, cc_ref, w_ref, dl_ref, da_ref, gw_ref, dcc_ref, gb_ref):
        a = jnp.concatenate([_silu(cg_ref[...]), jnp.broadcast_to(_silu(cc_ref[...]), (NDEV, D))], axis=0)
        dcc = jnp.zeros((1, D), F32)
        for i in range(2):
            dl = dl_ref[i]
            gw_ref[i] = _dot_f32(a, dl, _DN["tn"])
            dctx = jnp.sum(dl[NDEV:], axis=0, keepdims=True)
            dcc = dcc + _dot_f32(dctx, w_ref[i], _DN["nt"])
        dcc_ref[...] = dcc
        gb_ref[...] = jnp.sum(da_ref[...], axis=0)

    return pl.pallas_call(body, out_shape=[jax.ShapeDtypeStruct((2, D, W), F32), jax.ShapeDtypeStruct((1, D), F32),
                                           jax.ShapeDtypeStruct((2, 3 * D), F32)],
                          compiler_params=_cp(), name=name)(cg, c_ctx, ada_w, dm_loc, dm_all)


def cctx_finish(parts, c_ctx, name):
    def body(p_ref, cc_ref, o_ref):
        o_ref[...] = jnp.sum(p_ref[...], axis=0, keepdims=True) * _dsilu(cc_ref[...])

    return pl.pallas_call(body, out_shape=jax.ShapeDtypeStruct((1, D), F32), name=name)(parts, c_ctx)


def _adamw_update(g_ref, w_ref, m_ref, v_ref, go_ref, d_ref, mo_ref, vo_ref):
    g = g_ref[0].astype(F32)
    for s in range(1, g_ref.shape[0]):
        g = g + g_ref[s].astype(F32)
    mn = B1 * m_ref[...] + (1.0 - B1) * g
    vn = B2 * v_ref[...] + (1.0 - B2) * g * g
    go_ref[...] = g
    mo_ref[...] = mn
    vo_ref[...] = vn
    d_ref[...] = -LR * ((mn * (1.0 / (1.0 - B1 ** STEP))) / (jnp.sqrt(vn * (1.0 / (1.0 - B2 ** STEP))) + AEPS) + WD * w_ref[...])


def adamw(gstack, w, m, v, name, tr=256):
    n, R, C = gstack.shape
    tr = max(t for t in range(8, min(tr, R) + 1, 8) if R % t == 0)
    spec = pl.BlockSpec((tr, C), lambda i: (i, 0))
    return pl.pallas_call(_adamw_body(1), grid=(R // tr,),
                          in_specs=[pl.BlockSpec((n, tr, C), lambda i: (0, i, 0)), spec, spec, spec],
                          out_specs=[spec] * 4, out_shape=[jax.ShapeDtypeStruct((R, C), F32)] * 4,
                          compiler_params=_cp(("parallel",)), name=name)(gstack, w, m, v)


def _adamw_body(k):
    def body(*refs):
        for t in range(k):
            _adamw_update(*refs[4 * t:4 * t + 4], *refs[4 * k + 4 * t:4 * k + 4 * t + 4])
    return body


def adamw_multi(items, grid, name):
    k = len(items)
    ins, in_specs, out_specs, out_shape = [], [], [], []
    for g, g_spec, w, m, v, w_spec in items:
        ins += [g, w, m, v]
        in_specs += [g_spec, w_spec, w_spec, w_spec]
    for g, g_spec, w, m, v, w_spec in items:
        out_specs += [w_spec] * 4
        out_shape += [jax.ShapeDtypeStruct(w.shape, F32)] * 4
    res = pl.pallas_call(_adamw_body(k), grid=grid, in_specs=in_specs, out_specs=out_specs, out_shape=out_shape,
                         compiler_params=_cp(("arbitrary",) * len(grid)), name=name)(*ins)
    return [res[4 * t:4 * t + 4] for t in range(k)]


def _whole(a, grid_rank):
    zeros = (0,) * a.ndim
    return pl.BlockSpec(a.shape, lambda *idx: zeros)


def sum_slots(xs, name):
    def body(*refs):
        for x_ref, o_ref in zip(refs[:len(xs)], refs[len(xs):]):
            acc = x_ref[0]
            for s in range(1, NDEV):
                acc = acc + x_ref[s]
            o_ref[...] = acc

    return pl.pallas_call(body, out_shape=[jax.ShapeDtypeStruct(x.shape[1:], F32) for x in xs],
                          compiler_params=_cp(), name=name)(*xs)


def _col_shards(g):
    R, N = g.shape
    return g.reshape(R, NDEV, N // NDEV).transpose(1, 0, 2)


def _vec2(v):
    return jnp.broadcast_to(v.reshape(1, 1, -1), (2, 1, v.size))


SHARD_ROWS = {"mla_w_in": 192, "mla_w_uq": 192, "mla_w_ukv": 256, "s5_w_in": 256}


def _t_shard(wsh, rows):
    t = wsh[0].T.astype(BF16)
    return jnp.pad(t, ((0, rows - t.shape[0]), (0, 0)))


def _win_order():
    w = IN_W // NDEV
    perm = np.zeros((IN_WP, NDEV * SHARD_ROWS["mla_w_in"]), np.float32)
    first = QL + KVL + ROPE
    for c in range(IN_W):
        n = c + HEADS * VD if c < first else c - first
        perm[n, (c // w) * SHARD_ROWS["mla_w_in"] + c % w] = 1.0
    return jnp.asarray(perm, BF16)


def local_step(ctx, x, tgt, mod, Wt, small, l1_shards):
    T = LC + x.shape[0]
    xa = ("cat", ctx, x)
    sh = [mod[i, :, None, 0:D] for i in range(2)]
    sc = [mod[i, :, None, D:2 * D] for i in range(2)]
    gt = [mod[i, :, None, 2 * D:] for i in range(2)]
    ng = [_vec2(small["norm_g"][i]) for i in range(2)]
    qg, kvg = _vec2(small["mla_q_norm"]), _vec2(small["mla_kv_norm"])
    cosf, sinf, pm, pmt = _rope_tables(T)

    (h0,), _ = rowwise(st_norm_mod, [xa], [ng[0], sc[0], sh[0]], [(D, BF16)], [], "l0_norm")
    p0 = mm(h0, Wt["mla_w_in"], "nt", "l0_in")
    z0, cq, ckv = (p0, 0, HEADS * VD), (p0, HEADS * VD // QL, QL), (p0, (HEADS * VD + QL) // KVL, KVL)
    (cqn, ckvn), _ = rowwise(st_rms2, [cq, ckv], [qg, kvg], [(QL, BF16), (KVL, BF16)], [], "l0_qkvnorm")
    Q = project_q(cqn, Wt["mla_w_uq"], cosf, sinf, pm, "l0_uq")
    K, V = project_kv(ckvn, Wt["mla_w_ukv"], p0, (HEADS * VD + QL + KVL) // 128, "l0_ukv")
    (o, lse), got = attn_fwd(Q, K, V, "l0_attn", rode=l1_shards, modes="gather")
    Wt, small = dict(Wt), dict(small)
    for n, a in zip(L1_BIG, got):
        Wt[n] = a.reshape(-1, a.shape[-1])
    vecs = lax.bitcast_convert_type(got[-1].reshape(NDEV, 2, -1, 2), F32)
    small["s5_d"], small["s5_b_glu"] = vecs[:, 0, :].reshape(D), vecs[:, 1, :].reshape(D)
    o2 = o.transpose(1, 0, 2).reshape(T, HEADS * VD)
    (og,), _ = rowwise(st_gate, [o2, z0], [], [(D, BF16)], [], "l0_gate")
    out0 = mm(og, Wt["mla_w_out"], "nn", "l0_out")
    (x1,), _ = rowwise(st_resid, [xa, out0], [gt[0]], [(D, F32)], [], "l0_resid")

    ls = small["s5_log_step"].reshape(2, G, 1)
    a_re, a_im = small["s5_a_re"].reshape(2, G, P), small["s5_a_im"].reshape(2, G, P)
    b_re, b_im = small["s5_b_re"].reshape(2, G * P, CH), small["s5_b_im"].reshape(2, G * P, CH)
    lam_re, lam_im, f_re, f_im = disc_fwd(a_re, a_im, ls, "s5_disc")
    f_re2, f_im2 = f_re.reshape(2, G * P, 1), f_im.reshape(2, G * P, 1)
    bb_re, bb_im = disc_b(f_re2, f_im2, b_re, b_im, "s5_disc_b")
    compact = lambda m: m.reshape(2, NJ, UB, P)
    bre = compact(bb_re.reshape(2, G, P, CH).transpose(0, 1, 3, 2))
    bim = compact(bb_im.reshape(2, G, P, CH).transpose(0, 1, 3, 2))
    cre, cim = compact(small["s5_c_re"]), compact(small["s5_c_im"])
    lam_re4, lam_im4 = lam_re.reshape(2, NJ, 1, SB), lam_im.reshape(2, NJ, 1, SB)

    (h1,), _ = rowwise(st_norm_mod, [x1], [ng[1], sc[1], sh[1]], [(D, BF16)], [], "l1_norm")
    p1 = mm(h1, Wt["s5_w_in"], "nt", "l1_in")
    u, z1 = (p1, 0, D), (p1, 1, D)
    yssm = scan_fwd(p1, lam_re4, lam_im4, bre, bim, cre, cim, "s5_scan")
    dvec, bglu = _vec2(small["s5_d"]), _vec2(small["s5_b_glu"])
    (y, y1b), _ = rowwise(st_s5a, [yssm, u], [dvec], [(D, F32), (D, BF16)], [], "l1_gelu")
    gl = mm(y1b, Wt["s5_w_glu"], "nn", "l1_glu")
    (y3,), _ = rowwise(st_s5b, [y, gl, z1], [bglu], [(D, BF16)], [], "l1_gate")
    out1 = mm(y3, Wt["s5_w_out"], "nn", "l1_out")
    (x2,), _ = rowwise(st_resid, [x1, out1], [gt[1]], [(D, F32)], [], "l1_resid")

    fg = _vec2(small["final_g"])
    lat_mask = jnp.stack([jnp.zeros((1, D), F32), jnp.ones((1, D), F32)])
    (dx2,), (dfg, lvec) = rowwise(st_final, [x2, ("lat", tgt)], [fg, lat_mask], [(D, F32)], [D, 128], "final")

    (dout1,), (dgt1,) = rowwise(st_resid_bwd, [dx2, out1], [gt[1]], [(D, BF16)], [D], "l1_resid_b")
    g_w_out5 = mm(y3, dout1, "tn", "l1_out_dw", out_dtype=BF16)
    dy3 = mm(dout1, Wt["s5_w_out"], "nt", "l1_out_dx")
    (dgl, dz1, dy1a), (dbglu,) = rowwise(st_s5b_bwd, [dy3, y, gl, z1], [bglu], [(D, BF16), (D, BF16), (D, F32)], [D], "l1_gate_b")
    g_w_glu = mm(y1b, dgl, "tn", "l1_glu_dw", out_dtype=BF16)
    dy1b = mm(dgl, Wt["s5_w_glu"], "nt", "l1_glu_dx")
    (dy, du_d), (dd,) = rowwise(st_s5a_bwd, [dy1a, dy1b, y, u], [dvec], [(D, F32), (D, F32)], [D], "l1_gelu_b")
    du_s, dlr, dli, dbre, dbim, dcre, dcim = scan_bwd(p1, dy, lam_re4, lam_im4, bre, bim, cre, cim, "s5_scan_b")
    du = du_d + du_s
    dbb_re = dbre.reshape(2, G, CH, P).transpose(0, 1, 3, 2).reshape(2, G * P, CH)
    dbb_im = dbim.reshape(2, G, CH, P).transpose(0, 1, 3, 2).reshape(2, G * P, CH)
    g_c_re, g_c_im = dcre.reshape(2, G, CH, P), dcim.reshape(2, G, CH, P)
    g_b_re, g_b_im, dfr, dfi = disc_b_bwd(f_re2, f_im2, b_re, b_im, dbb_re, dbb_im, "s5_disc_b_b")
    g_a_re, g_a_im, g_ls = disc_a_bwd(a_re, a_im, ls, dlr.reshape(2, G, P), dli.reshape(2, G, P),
                                      dfr.reshape(2, G, P), dfi.reshape(2, G, P), "s5_disc_b_a")
    dp1 = jnp.concatenate([du.astype(BF16), dz1], axis=1)
    g_w_in5 = mm(h1, dp1, "tn", "l1_in_dw", out_dtype=BF16, tm=D, tn=2 * D // NDEV, shard_out=True)
    dh1 = mm(dp1, Wt["s5_w_in"], "nn", "l1_in_dx")
    (dx1,), (dsh1, dsc1, dng1) = rowwise(st_norm_mod_bwd, [x1, dh1, dx2], [ng[1], sc[1]], [(D, F32)], [D, D, D], "l1_norm_b")

    (dout0,), (dgt0,) = rowwise(st_resid_bwd, [dx1, out0], [gt[0]], [(D, BF16)], [D], "l0_resid_b")
    g_w_out = mm(og, dout0, "tn", "l0_out_dw", out_dtype=BF16)
    dog = mm(dout0, Wt["mla_w_out"], "nt", "l0_out_dx")
    (do2, dz0), _ = rowwise(st_gate_bwd, [dog, o2, z0], [], [(D, F32), (D, F32)], [], "l0_gate_b")
    doh = do2.reshape(T, HEADS, VD).transpose(1, 0, 2)
    rows8 = lambda g: g.reshape(NDEV, -1, g.shape[-1])
    both = lambda s: s[0, 0] + s[1, 0]
    dense = lambda g: g.reshape(2, G * P * CH // 128, 128)
    chunks = [dense(g_b_re), dense(g_b_im), g_c_re, g_c_im]
    l1_send = [g_w_in5, rows8(g_w_glu), rows8(g_w_out5), both(dd).reshape(NDEV, 1, -1), both(dbglu).reshape(NDEV, 1, -1)]
    (dQ, dK, dV), l1_recv = attn_bwd(Q, K, V, o, lse, doh, "l0_attn_b", rode=l1_send + chunks,
                                     modes=["lead"] * len(l1_send) + [a.shape[1] // NDEV for a in chunks])
    dqh = rope(dQ, cosf, sinf, pmt, True, BF16, "l0_rope_q_b", scale=SCALE)
    dq = dqh.transpose(1, 0, 2).reshape(T, HEADS * QK)
    dkv, dkr = split_kv_grads(dK, dV, "l0_kv_b")
    g_w_uq = _col_shards(mm(cqn, dq, "tn", "l0_uq_dw", out_dtype=BF16))
    dcqn = mm(dq, Wt["mla_w_uq"], "nn", "l0_uq_dx")
    g_w_ukv = mm(ckvn, dkv, "tn", "l0_ukv_dw", out_dtype=BF16, tm=KVL, tn=HEADS * (NOPE + VD) // NDEV, shard_out=True)
    dckvn = mm(dkv, Wt["mla_w_ukv"], "nn", "l0_ukv_dx")
    (dcq, dckv), (dqg, dkvg) = rowwise(st_rms2_bwd, [cq, dcqn, ckv, dckvn], [qg, kvg], [(QL, F32), (KVL, F32)], [QL, KVL],
                                       "l0_qkvnorm_b")
    dp0 = jnp.concatenate([dz0, dcq, dckv, dkr], axis=1).astype(BF16)
    g_p = mm(h0, dp0, "tn", "l0_in_dw", out_dtype=BF16)
    g_w_in = _col_shards(jnp.concatenate([g_p[:, HEADS * VD:IN_W], g_p[:, :HEADS * VD]], axis=1))
    dh0 = mm(dp0, Wt["mla_w_in"], "nn", "l0_in_dx")
    (grad_x,), (dsh0, dsc0, dng0) = rowwise(st_norm_mod_bwd, [xa, dh0, dx1], [ng[0], sc[0]], [(D, F32, "lat")], [D, D, D], "l0_norm_b")

    dmod = jnp.stack([jnp.concatenate([dsh0, dsc0, dgt0], axis=-1)[:, 0], jnp.concatenate([dsh1, dsc1, dgt1], axis=-1)[:, 0]])
    gbig = {"mla_w_in": g_w_in, "mla_w_uq": g_w_uq, "mla_w_ukv": g_w_ukv, "mla_w_out": rows8(g_w_out)}
    gsmall = {"norm_g": jnp.stack([both(dng0), both(dng1)]), "mla_q_norm": both(dqg), "mla_kv_norm": both(dkvg),
              "s5_a_re": g_a_re, "s5_a_im": g_a_im, "s5_log_step": g_ls, "final_g": dfg[1, 0]}
    return lvec[1], grad_x, dmod, gbig, gsmall, l1_recv


COL_SHARDED = ("mla_w_in", "mla_w_uq", "mla_w_ukv", "s5_w_in")
ROW_SHARDED = ("mla_w_out", "s5_w_glu", "s5_w_out")
VEC_SHARDED = ("s5_d", "s5_b_glu")
BIG = COL_SHARDED + ROW_SHARDED
L0_BIG = ("mla_w_in", "mla_w_uq", "mla_w_ukv", "mla_w_out")
L1_BIG = ("s5_w_in", "s5_w_glu", "s5_w_out")
BITS16 = jnp.bfloat16
SMALL_RS = ("norm_g", "mla_q_norm", "mla_kv_norm", "s5_a_re", "s5_a_im", "s5_log_step", "s5_b_re", "s5_b_im",
            "s5_c_re", "s5_c_im", "final_g")
CHUNKED = ("s5_b_re", "s5_b_im", "s5_c_re", "s5_c_im")
DENSE = ("s5_b_re", "s5_b_im")
TINY = ("norm_g", "mla_q_norm", "mla_kv_norm", "s5_a_re", "s5_a_im", "s5_log_step", "final_g")
ORDER = ("c_ctx", "ada_w", "ada_b", "norm_g", "mla_w_in", "mla_q_norm", "mla_w_uq", "mla_kv_norm", "mla_w_ukv",
         "mla_w_out", "s5_w_in", "s5_a_re", "s5_a_im", "s5_log_step", "s5_b_re", "s5_b_im", "s5_c_re", "s5_c_im",
         "s5_d", "s5_w_glu", "s5_b_glu", "s5_w_out", "final_g")


def kernel(x, c, ctx, c_ctx, ada_w, ada_b, norm_g, mla_w_in, mla_q_norm, mla_w_uq, mla_kv_norm, mla_w_ukv, mla_w_out, s5_w_in, s5_a_re, s5_a_im, s5_log_step, s5_b_re, s5_b_im, s5_c_re, s5_c_im, s5_d, s5_w_glu, s5_b_glu, s5_w_out, final_g, loss_target, m_c_ctx, m_ada_w, m_ada_b, m_norm_g, m_mla_w_in, m_mla_q_norm, m_mla_w_uq, m_mla_kv_norm, m_mla_w_ukv, m_mla_w_out, m_s5_w_in, m_s5_a_re, m_s5_a_im, m_s5_log_step, m_s5_b_re, m_s5_b_im, m_s5_c_re, m_s5_c_im, m_s5_d, m_s5_w_glu, m_s5_b_glu, m_s5_w_out, m_final_g, v_c_ctx, v_ada_w, v_ada_b, v_norm_g, v_mla_w_in, v_mla_q_norm, v_mla_w_uq, v_mla_kv_norm, v_mla_w_ukv, v_mla_w_out, v_s5_w_in, v_s5_a_re, v_s5_a_im, v_s5_log_step, v_s5_b_re, v_s5_b_im, v_s5_c_re, v_s5_c_im, v_s5_d, v_s5_w_glu, v_s5_b_glu, v_s5_w_out, v_final_g):
    w = dict(c_ctx=c_ctx, ada_w=ada_w, ada_b=ada_b, norm_g=norm_g, mla_w_in=mla_w_in, mla_q_norm=mla_q_norm,
             mla_w_uq=mla_w_uq, mla_kv_norm=mla_kv_norm, mla_w_ukv=mla_w_ukv, mla_w_out=mla_w_out, s5_w_in=s5_w_in,
             s5_a_re=s5_a_re, s5_a_im=s5_a_im, s5_log_step=s5_log_step, s5_b_re=s5_b_re, s5_b_im=s5_b_im,
             s5_c_re=s5_c_re, s5_c_im=s5_c_im, s5_d=s5_d, s5_w_glu=s5_w_glu, s5_b_glu=s5_b_glu, s5_w_out=s5_w_out,
             final_g=final_g)
    m = dict(c_ctx=m_c_ctx, ada_w=m_ada_w, ada_b=m_ada_b, norm_g=m_norm_g, mla_w_in=m_mla_w_in, mla_q_norm=m_mla_q_norm,
             mla_w_uq=m_mla_w_uq, mla_kv_norm=m_mla_kv_norm, mla_w_ukv=m_mla_w_ukv, mla_w_out=m_mla_w_out,
             s5_w_in=m_s5_w_in, s5_a_re=m_s5_a_re, s5_a_im=m_s5_a_im, s5_log_step=m_s5_log_step, s5_b_re=m_s5_b_re,
             s5_b_im=m_s5_b_im, s5_c_re=m_s5_c_re, s5_c_im=m_s5_c_im, s5_d=m_s5_d, s5_w_glu=m_s5_w_glu,
             s5_b_glu=m_s5_b_glu, s5_w_out=m_s5_w_out, final_g=m_final_g)
    v = dict(c_ctx=v_c_ctx, ada_w=v_ada_w, ada_b=v_ada_b, norm_g=v_norm_g, mla_w_in=v_mla_w_in, mla_q_norm=v_mla_q_norm,
             mla_w_uq=v_mla_w_uq, mla_kv_norm=v_mla_kv_norm, mla_w_ukv=v_mla_w_ukv, mla_w_out=v_mla_w_out,
             s5_w_in=v_s5_w_in, s5_a_re=v_s5_a_re, s5_a_im=v_s5_a_im, s5_log_step=v_s5_log_step, s5_b_re=v_s5_b_re,
             s5_b_im=v_s5_b_im, s5_c_re=v_s5_c_re, s5_c_im=v_s5_c_im, s5_d=v_s5_d, s5_w_glu=v_s5_w_glu,
             s5_b_glu=v_s5_b_glu, s5_w_out=v_s5_w_out, final_g=v_final_g)

    me = 4 * lax.axis_index("x") + 2 * lax.axis_index("y") + lax.axis_index("c")
    WA = ada_w.shape[2]

    def shard(n):
        return _t_shard(w[n], SHARD_ROWS[n]) if n in COL_SHARDED else w[n][0].astype(BF16)

    wgot = exchange([c] + [shard(n) for n in L0_BIG], "gather", "gather_w")

    cg = wgot[0].reshape(NDEV, D)
    cc2 = c_ctx.reshape(1, D)
    ada_b_loc = lax.dynamic_slice_in_dim(ada_b.reshape(2, 3 * D // WA, WA), me, 1, axis=1)
    part = ada_fwd(cg, cc2, ada_w, ada_b_loc, "ada_fwd")
    pg = exchange([part], "gather", "gather_mod")[0]
    mod_l = lax.dynamic_index_in_dim(pg, me, axis=2, keepdims=False).transpose(1, 0, 2).reshape(2, 3 * D)
    mod_c = pg[:, :, NDEV, :].transpose(1, 0, 2).reshape(2, 3 * D)
    mod = jnp.stack([mod_c, mod_l], axis=1)

    Wt = {n: a.reshape(-1, a.shape[-1]) for n, a in zip(L0_BIG, wgot[1:])}
    Wt["mla_w_in"] = mm(_win_order(), Wt["mla_w_in"], "nn", "w_in_order", out_dtype=BF16)
    vec_bits = lax.bitcast_convert_type(jnp.concatenate([s5_d, s5_b_glu], axis=0), BITS16).reshape(2, -1)
    small = {n: w[n] for n in SMALL_RS}

    lvec, grad_x, dmod, gbig, gsmall, l1_recv = local_step(ctx[0], x[0], loss_target[0], mod, Wt, small,
                                                           [shard(n) for n in L1_BIG] + [vec_bits])
    loss = lax.psum(lvec[0, 0], ("x", "y", "c"))
    grad_x = grad_x[None]

    per_dev = G // NDEV
    recv = dict(zip(L0_BIG, exchange([gbig[n] for n in L0_BIG], "lead", "scatter_grads")))
    recv.update(dict(zip(L1_BIG + VEC_SHARDED, l1_recv)))
    out = {}

    def keep(n, res):
        for key, arr in zip("gdmv", res):
            out[key, n] = arr.reshape(w[n].shape)

    for n in BIG:
        keep(n, adamw(recv[n], w[n][0], m[n][0], v[n][0], "adamw_" + n))
    reduced = sum_slots(l1_recv[len(L1_BIG + VEC_SHARDED):], "sum_chunks")

    kshape = lambda n: w[n].shape if w[n].ndim > 1 else (1, w[n].size)
    got = exchange(list(reduced) + [gsmall[n].reshape(kshape(n)) for n in TINY] + [dmod], "gather", "gather_small")
    chunk_all, tiny_all, dm_all = got[:len(CHUNKED)], got[len(CHUNKED):-1], got[-1]

    dm_cols = lax.dynamic_slice_in_dim(dm_all.reshape(NDEV, 2, 2, 3 * D // WA, WA), me, 1, axis=3)[:, :, :, 0]
    dm_loc = jnp.concatenate([dm_cols[:, :, 1].transpose(1, 0, 2), dm_cols[:, :, 0].transpose(1, 0, 2)], axis=1)
    g_ada_w, dcc_part, g_ada_b = ada_bwd(cg, cc2, ada_w, dm_loc, dm_all.transpose(0, 2, 1, 3).reshape(2 * NDEV, 2, 3 * D), "ada_bwd")
    dcc_all = exchange([dcc_part], "gather", "gather_dcc")[0].reshape(NDEV, D)
    g_c_ctx = cctx_finish(dcc_all, cc2, "cctx_finish")

    flat2 = lambda t: t.reshape(-1, t.shape[-1])
    keep("ada_w", adamw(flat2(g_ada_w)[None], flat2(ada_w), flat2(m_ada_w), flat2(v_ada_w), "adamw_ada"))
    items = []
    for n, g in zip(CHUNKED, chunk_all):
        blk = (1, 1, per_dev) + w[n].shape[3:]
        if n in DENSE:
            g = g.transpose(1, 0, 2, 3).reshape(w[n].shape)
            g_spec = pl.BlockSpec((1, 1, 1) + blk[2:], lambda d, s: (0, 0, d, s, 0, 0))
        else:
            g_spec = pl.BlockSpec((1, 1, 1) + blk[2:], lambda d, s: (0, s, d, 0, 0, 0))
        items.append((g[None], g_spec, w[n], m[n], v[n], pl.BlockSpec(blk, lambda d, s: (0, d, s, 0, 0))))
    for n, res in zip(CHUNKED, adamw_multi(items, (2, NDEV), "adamw_bc")):
        keep(n, res)
    tiny_g = dict(zip(TINY, tiny_all))
    tiny_g.update({n: recv[n] for n in VEC_SHARDED})
    tiny_g["c_ctx"], tiny_g["ada_b"] = g_c_ctx[None], g_ada_b[None]
    names = list(tiny_g)
    items = [(tiny_g[n], _whole(tiny_g[n], 1)) + tuple(t[n].reshape(kshape(n)) for t in (w, m, v))
             + (pl.BlockSpec(kshape(n), lambda i, r=len(kshape(n)): (0,) * r),) for n in names]
    for n, res in zip(names, adamw_multi(items, (1,), "adamw_small")):
        keep(n, res)

    return (loss, grad_x, *[out["g", n] for n in ORDER], *[out["d", n] for n in ORDER],
            *[out["m", n] for n in ORDER], *[out["v", n] for n in ORDER])
```

```python
import math

import numpy as np
import jax
import jax.numpy as jnp
from jax import lax
from jax.experimental import pallas as pl
from jax.experimental.pallas import tpu as pltpu

F32 = jnp.float32
BF16 = jnp.bfloat16

D = 1024
L = 2048
LC = 256
NDEV = 8
GRID_W = 64
EPS = 1e-6
HEADS = 16
NOPE = 64
ROPE = 32
QK = NOPE + ROPE
VD = 64
IN_W = 256 + 128 + ROPE + HEADS * 64
IN_WP = 1536
QL = 256
KVL = 128
SCALE = QK ** -0.5
THETA = 10000.0
G = 64
P = 64
CH = 16
GB = 8
NJ = G // GB
UB = GB * CH
SB = GB * P
SEG = 8
TB = 256
VMEM_LIMIT = 56 * 1024 * 1024
B1, B2, LR, AEPS, WD, STEP = 0.9, 0.999, 0.001, 1e-8, 0.01, 10
MESH_T = pl.DeviceIdType.MESH


def _cp(sem=None):
    return pltpu.CompilerParams(dimension_semantics=sem, vmem_limit_bytes=VMEM_LIMIT)


def _sig(x):
    return 1.0 / (1.0 + jnp.exp(-x))


def _silu(x):
    return x * _sig(x)


def _dsilu(x):
    s = _sig(x)
    return s * (1.0 + x * (1.0 - s))


_GK = math.sqrt(2.0 / math.pi)


def _gelu(x):
    return 0.5 * x * (1.0 + jnp.tanh(_GK * (x + 0.044715 * x * x * x)))


def _dgelu(x):
    t = jnp.tanh(_GK * (x + 0.044715 * x * x * x))
    return 0.5 * (1.0 + t) + 0.5 * x * (1.0 - t * t) * _GK * (1.0 + 3 * 0.044715 * x * x)


def _rs(x):
    return lax.rsqrt(jnp.mean(x * x, axis=-1, keepdims=True) + EPS)


def _sum0(x):
    return jnp.sum(x, axis=0, keepdims=True)


def st_norm_mod(x, g, sc, sh):
    y = x * _rs(x) * g
    return (y * (1.0 + sc) + sh,), ()


def st_norm_mod_bwd(x, dh, dres, g, sc):
    r = _rs(x)
    xn = x * r
    y = xn * g
    dy = dh * (1.0 + sc)
    dxn = dy * g
    dx = r * (dxn - xn * jnp.mean(dxn * xn, axis=-1, keepdims=True))
    return (dres + dx,), (_sum0(dh), _sum0(dh * y), _sum0(dy * xn))


def st_rms(x, g):
    return (x * _rs(x) * g,), ()


def st_rms_bwd(x, dy, g):
    r = _rs(x)
    n = x * r
    dn = dy * g
    dx = r * (dn - n * jnp.mean(dn * n, axis=-1, keepdims=True))
    return (dx,), (_sum0(dy * n),)


def st_rms2(x1, x2, g1, g2):
    return st_rms(x1, g1)[0] + st_rms(x2, g2)[0], ()


def st_rms2_bwd(x1, dy1, x2, dy2, g1, g2):
    (d1,), (s1,) = st_rms_bwd(x1, dy1, g1)
    (d2,), (s2,) = st_rms_bwd(x2, dy2, g2)
    return (d1, d2), (s1, s2)


def st_gate(o, z):
    return (o * _silu(z),), ()


def st_gate_bwd(dog, o, z):
    return (dog * _silu(z), dog * o * _dsilu(z)), ()


def st_resid(x, out, gt):
    return (x + gt * out,), ()


def st_resid_bwd(dx, out, gt):
    return (dx * gt,), (_sum0(dx * out),)


def st_s5a(yssm, u, d):
    y = yssm + d * u
    return (y, _gelu(y)), ()


def st_s5b(y, gl, z, b):
    return (_gelu(y) * _sig(gl + b) * _silu(z),), ()


def st_s5b_bwd(dy3, y, gl, z, b):
    y1 = _gelu(y)
    s = _sig(gl + b)
    dy2 = dy3 * _silu(z)
    dz = dy3 * y1 * s * _dsilu(z)
    dgl = dy2 * y1 * s * (1.0 - s)
    return (dgl, dz, dy2 * s), (_sum0(dgl),)


def st_s5a_bwd(dy1a, dy1b, y, u, d):
    dy = (dy1a + dy1b) * _dgelu(y)
    return (dy, dy * d), (_sum0(dy * u),)


def st_l0_post(o, z, x, gt, w_out):
    og = (o * _silu(z)).astype(BF16)
    out = jnp.dot(og, w_out, preferred_element_type=F32)
    return (og, out, x + gt * out), ()


def st_l0_post_bwd(dx1, out, og, o, z, gt, w_out):
    (dout,), (dgt,) = st_resid_bwd(dx1, out, gt)
    doutb = dout.astype(BF16)
    dog = lax.dot_general(doutb, w_out, _DN["nt"], preferred_element_type=F32)
    return st_gate_bwd(dog, o, z)[0], (dgt,), (lax.dot_general(og, doutb, _DN["tn"], preferred_element_type=F32),)


def st_l1_mlp(yssm, u, z, x1, tgt, d, bglu, gt, fg, mask, w_glu, w_out):
    (y, y1), _ = st_s5a(yssm, u, d)
    y1b = y1.astype(BF16)
    gl = jnp.dot(y1b, w_glu, preferred_element_type=F32)
    y3 = (y1 * _sig(gl + bglu) * _silu(z)).astype(BF16)
    out = jnp.dot(y3, w_out, preferred_element_type=F32)
    (dx2,), sums = st_final(x1 + gt * out, tgt, fg, mask)
    return (y, y1b, gl, y3, out, dx2), sums


def st_l1_mlp_bwd(dx2, out, y3, y, gl, z, u, y1b, gt, bglu, d, w_out, w_glu):
    (dout,), (dgt,) = st_resid_bwd(dx2, out, gt)
    doutb = dout.astype(BF16)
    dy3 = lax.dot_general(doutb, w_out, _DN["nt"], preferred_element_type=F32)
    (dgl, dz, dy1a), (dbglu,) = st_s5b_bwd(dy3, y, gl, z, bglu)
    dglb = dgl.astype(BF16)
    dy1b = lax.dot_general(dglb, w_glu, _DN["nt"], preferred_element_type=F32)
    (dy, du), (dd,) = st_s5a_bwd(dy1a, dy1b, y, u, d)
    g_w_out = lax.dot_general(y3, doutb, _DN["tn"], preferred_element_type=F32)
    g_w_glu = lax.dot_general(y1b, dglb, _DN["tn"], preferred_element_type=F32)
    return (dz, dy, du), (dgt, dbglu, dd), (g_w_out, g_w_glu)


def st_final(x2, tgt, g, mask):
    r = _rs(x2)
    n = x2 * r
    e = n * g - tgt
    dyo = e * (1.0 / D)
    dn = dyo * g
    dx = r * (dn - n * jnp.mean(dn * n, axis=-1, keepdims=True))
    lsum = jnp.sum(_sum0(e * e), axis=1, keepdims=True) * (0.5 / D)
    return (dx * mask,), (_sum0(dyo * n), jnp.broadcast_to(lsum, (1, 128)))


def rowwise(fn, rows, vecs, out_rows, out_sums, name, mats=(), out_accs=()):
    lat_blk = lambda i: jnp.maximum(i - 1, 0)
    arrays, in_specs, pick = [], [], []
    for a in rows:
        if not isinstance(a, tuple):
            a = (a, 0, a.shape[1])
        tag = a[0] if isinstance(a[0], str) else None
        if tag == "cat":
            _, ctx, x = a
            arrays += [ctx, x]
            in_specs += [pl.BlockSpec((TB, ctx.shape[1]), lambda i: (0, 0)),
                         pl.BlockSpec((TB, x.shape[1]), lambda i: (lat_blk(i), 0))]
            pick.append(2)
        elif tag == "lat":
            arrays.append(a[1])
            in_specs.append(pl.BlockSpec((TB, a[1].shape[1]), lambda i: (lat_blk(i), 0)))
            pick.append(1)
        else:
            arr, cb, width = a
            arrays.append(arr)
            in_specs.append(pl.BlockSpec((TB, width), lambda i, cb=cb: (i, cb)))
            pick.append(1)
    T = LC + L
    nin, nv, nm, no, ns = len(arrays), len(vecs), len(mats), len(out_rows), len(out_sums)

    def body(*refs):
        i = pl.program_id(0)
        vals, k = [], 0
        for p in pick:
            if p == 2:
                vals.append(jnp.where(i == 0, refs[k][...], refs[k + 1][...]))
            else:
                vals.append(refs[k][...])
            k += p
        vals += [r[0] for r in refs[nin:nin + nv]] + [r[...] for r in refs[nin + nv:nin + nv + nm]]
        res = fn(*vals)
        first_out = nin + nv + nm
        for r, o in zip(refs[first_out:first_out + no], res[0]):
            r[...] = o.astype(r.dtype)
        sum_refs = refs[first_out + no:first_out + no + ns]
        if sum_refs:
            @pl.when(i <= 1)
            def _():
                for r in sum_refs:
                    r[...] = jnp.zeros_like(r)
            for r, s in zip(sum_refs, res[1]):
                r[0] += s
        acc_refs = refs[first_out + no + ns:]
        if acc_refs:
            @pl.when(i == 0)
            def _():
                for r in acc_refs:
                    r[...] = jnp.zeros_like(r)
            for r, a in zip(acc_refs, res[2]):
                r[...] += a

    kind = lambda i: (jnp.minimum(i, 1), 0, 0)
    in_specs += [pl.BlockSpec((1, 1, v.shape[2]), kind) for v in vecs]
    in_specs += [pl.BlockSpec(m.shape, lambda i: (0, 0), pipeline_mode=pl.Buffered(1)) for m in mats]
    out_specs, out_shape = [], []
    for o in out_rows:
        lat = len(o) == 3
        out_specs.append(pl.BlockSpec((TB, o[0]), (lambda i: (lat_blk(i), 0)) if lat else (lambda i: (i, 0))))
        out_shape.append(jax.ShapeDtypeStruct((L if lat else T, o[0]), o[1]))
    out_specs += [pl.BlockSpec((1, 1, c), kind) for c in out_sums]
    out_shape += [jax.ShapeDtypeStruct((2, 1, c), F32) for c in out_sums]
    out_specs += [pl.BlockSpec(s, lambda i: (0, 0)) for s in out_accs]
    out_shape += [jax.ShapeDtypeStruct(s, F32) for s in out_accs]
    res = pl.pallas_call(body, grid=(T // TB,), in_specs=in_specs, out_specs=out_specs, out_shape=out_shape,
                         compiler_params=_cp(("arbitrary",)), name=name)(*arrays, *vecs, *mats)
    if out_accs:
        return res[:no], res[no:no + ns], res[no + ns:]
    return res[:no], res[no:]


_DN = {"nn": (((1,), (0,)), ((), ())), "nt": (((1,), (1,)), ((), ())), "tn": (((0,), (0,)), ((), ()))}


def mm(a, b, mode, name, out_dtype=F32, tm=None, tn=None, shard_out=False):
    if mode == "nn":
        (M, K), (_, N) = a.shape, b.shape
    elif mode == "nt":
        (M, K), (N, _) = a.shape, b.shape
    else:
        (K, M), (_, N) = a.shape, b.shape
    if tm is None:
        tm = next((t for t in (768, 512, 256) if M % t == 0 and M > t), M)
    tn = N if tn is None else tn
    dn = _DN[mode]

    def body(a_ref, b_ref, o_ref):
        o_ref[...] = lax.dot_general(a_ref[...].astype(BF16), b_ref[...].astype(BF16), dn,
                                     preferred_element_type=F32).astype(o_ref.dtype)

    if shard_out:
        def body(a_ref, b_ref, o_ref):
            av = a_ref[...].astype(BF16)
            for j in range(N // tn):
                bj = b_ref[pl.ds(j * tn, tn), :] if mode == "nt" else b_ref[:, pl.ds(j * tn, tn)]
                o_ref[j] = lax.dot_general(av, bj.astype(BF16), dn, preferred_element_type=F32).astype(o_ref.dtype)

        a_spec = pl.BlockSpec((K, tm), lambda i: (0, i)) if mode == "tn" else pl.BlockSpec((tm, K), lambda i: (i, 0))
        return pl.pallas_call(body, grid=(M // tm,), in_specs=[a_spec, pl.BlockSpec(b.shape, lambda i: (0, 0))],
                              out_specs=pl.BlockSpec((N // tn, tm, tn), lambda i: (0, i, 0)),
                              out_shape=jax.ShapeDtypeStruct((N // tn, M, tn), out_dtype),
                              compiler_params=_cp(("parallel",)), name=name)(a, b)
    a_spec = pl.BlockSpec((K, tm), lambda i, j: (0, i)) if mode == "tn" else pl.BlockSpec((tm, K), lambda i, j: (i, 0))
    b_spec = pl.BlockSpec((tn, K), lambda i, j: (j, 0)) if mode == "nt" else pl.BlockSpec((K, tn), lambda i, j: (0, j))
    return pl.pallas_call(body, grid=(M // tm, N // tn), in_specs=[a_spec, b_spec],
                          out_specs=pl.BlockSpec((tm, tn), lambda i, j: (i, j)), out_shape=jax.ShapeDtypeStruct((M, N), out_dtype),
                          compiler_params=_cp(("parallel", "arbitrary")), name=name)(a, b)


def _rope_tables(T, width=QK, first=NOPE):
    nlat = T - LC
    pos = np.arange(nlat)
    row, col = pos // GRID_W, pos % GRID_W
    half = ROPE // 2
    inv = 1.0 / (THETA ** (np.arange(0, half, 2, dtype=np.float64) / half))
    cosf = np.ones((T, width), np.float64)
    sinf = np.zeros((T, width), np.float64)
    perm = np.zeros((width, width), np.float32)
    for m in range(ROPE):
        j = first + m
        blk, w = m // half, m % half
        ang = (row if blk == 0 else col)[:, None] * inv[None, :]
        f = w % (half // 2)
        cosf[LC:, j] = np.cos(ang[:, f])
        if w < half // 2:
            sinf[LC:, j] = -np.sin(ang[:, f])
            perm[j + half // 2, j] = 1.0
        else:
            sinf[LC:, j] = np.sin(ang[:, f])
            perm[j - half // 2, j] = 1.0
    return jnp.asarray(cosf, F32), jnp.asarray(sinf, F32), jnp.asarray(perm, BF16), jnp.asarray(perm.T, BF16)


def _exact_perm(x, pm):
    hi = x.astype(BF16)
    r1 = x - hi.astype(F32)
    mid = r1.astype(BF16)
    lo = (r1 - mid.astype(F32)).astype(BF16)
    dot = lambda a: jnp.dot(a, pm, preferred_element_type=F32)
    return dot(hi) + dot(mid) + dot(lo)


def _rot(x, cv, sv, pv, inverse):
    if inverse:
        return x * cv + _exact_perm(x * sv, pv)
    return x * cv + _exact_perm(x, pv) * sv


def rope(x, cosf, sinf, pm, inverse, out_dtype, name, scale=1.0):
    H, T, _ = x.shape

    def body(x_ref, c_ref, s_ref, p_ref, o_ref):
        cv, sv, pv = c_ref[...], s_ref[...], p_ref[...]
        for h in range(H):
            o_ref[h] = (_rot(x_ref[h], cv, sv, pv, inverse) * scale).astype(o_ref.dtype)

    return pl.pallas_call(
        body, grid=(T // TB,),
        in_specs=[pl.BlockSpec((H, TB, QK), lambda i: (0, i, 0)), pl.BlockSpec((TB, QK), lambda i: (i, 0)),
                  pl.BlockSpec((TB, QK), lambda i: (i, 0)), pl.BlockSpec((QK, QK), lambda i: (0, 0))],
        out_specs=pl.BlockSpec((H, TB, QK), lambda i: (0, i, 0)), out_shape=jax.ShapeDtypeStruct((H, T, QK), out_dtype),
        compiler_params=_cp(("parallel",)), name=name)(x, cosf, sinf, pm)


KVW = NOPE + VD


def _kv_selectors():
    s_kn = np.zeros((KVW, QK), np.float32)
    s_kr = np.zeros((128, QK), np.float32)
    s_v = np.zeros((KVW, VD), np.float32)
    for l in range(NOPE):
        s_kn[l, l] = 1.0
    for l in range(ROPE):
        s_kr[l, NOPE + l] = 1.0
    for l in range(VD):
        s_v[NOPE + l, l] = 1.0
    return s_kn, s_kr, s_v


def project_q(cqn, w, cosf, sinf, pm, name):
    T = cqn.shape[0]

    def body(a_ref, w_ref, c_ref, s_ref, p_ref, o_ref):
        a, cv, sv, pv = a_ref[...], c_ref[...], s_ref[...], p_ref[...]
        for h in range(HEADS):
            qh = _dotf(a, w_ref[pl.ds(h * QK, QK), :], "nt")
            o_ref[h] = (_rot(qh, cv, sv, pv, False) * SCALE).astype(BF16)

    rows = lambda c: pl.BlockSpec((TB, c), lambda i: (i, 0))
    const = lambda x: pl.BlockSpec(x.shape, lambda i: (0, 0))
    return pl.pallas_call(
        body, grid=(T // TB,), in_specs=[rows(QL), const(w), rows(QK), rows(QK), const(pm)],
        out_specs=pl.BlockSpec((HEADS, TB, QK), lambda i: (0, i, 0)), out_shape=jax.ShapeDtypeStruct((HEADS, T, QK), BF16),
        compiler_params=_cp(("parallel",)), name=name)(cqn, w, cosf, sinf, pm)


def project_kv(ckvn, w, p0, kr_block, name):
    T = ckvn.shape[0]
    cosf, sinf, pm, _ = _rope_tables(T, 128, 0)
    s_kn, s_kr, s_v = (jnp.asarray(s, BF16) for s in _kv_selectors())

    def body(a_ref, w_ref, kr_ref, c_ref, s_ref, p_ref, skn_ref, skr_ref, sv_ref, k_ref, v_ref):
        a = a_ref[...]
        krr = _rot(kr_ref[...], c_ref[...], s_ref[...], p_ref[...], False).astype(BF16)
        kr_part = jnp.dot(krr, skr_ref[...], preferred_element_type=F32)
        for h in range(HEADS):
            kvb = _dotf(a, w_ref[pl.ds(h * KVW, KVW), :], "nt").astype(BF16)
            k_ref[h] = (jnp.dot(kvb, skn_ref[...], preferred_element_type=F32) + kr_part).astype(BF16)
            v_ref[h] = jnp.dot(kvb, sv_ref[...], preferred_element_type=F32).astype(BF16)

    rows = lambda c: pl.BlockSpec((TB, c), lambda i: (i, 0))
    const = lambda x: pl.BlockSpec(x.shape, lambda i: (0, 0))
    return pl.pallas_call(
        body, grid=(T // TB,),
        in_specs=[rows(KVL), const(w), pl.BlockSpec((TB, 128), lambda i: (i, kr_block)),
                  rows(128), rows(128), const(pm), const(s_kn), const(s_kr), const(s_v)],
        out_specs=[pl.BlockSpec((HEADS, TB, QK), lambda i: (0, i, 0)), pl.BlockSpec((HEADS, TB, VD), lambda i: (0, i, 0))],
        out_shape=[jax.ShapeDtypeStruct((HEADS, T, QK), BF16), jax.ShapeDtypeStruct((HEADS, T, VD), BF16)],
        compiler_params=_cp(("parallel",)), name=name)(ckvn, w, p0, cosf, sinf, pm, s_kn, s_kr, s_v)


def split_kv_grads(dk, dv, name):
    H, T, _ = dk.shape
    cosf, sinf, _, pmt = _rope_tables(T, 128, 0)
    s_kn, s_kr, s_v = _kv_selectors()
    s_knt, s_krt, s_vt = (jnp.asarray(s.T, BF16) for s in (s_kn, s_kr, s_v))

    def body(dk_ref, dv_ref, c_ref, s_ref, p_ref, skn_ref, skr_ref, sv_ref, dkv_ref, dkr_ref):
        total = None
        for h in range(H):
            dkh = dk_ref[h]
            total = dkh if total is None else total + dkh
            dkv_ref[:, pl.ds(h * KVW, KVW)] = (
                jnp.dot(dkh.astype(BF16), skn_ref[...], preferred_element_type=F32)
                + jnp.dot(dv_ref[h].astype(BF16), sv_ref[...], preferred_element_type=F32)).astype(BF16)
        dkr_ref[...] = _rot(_exact_perm(total, skr_ref[...]), c_ref[...], s_ref[...], p_ref[...], True)

    rows = lambda c: pl.BlockSpec((TB, c), lambda i: (i, 0))
    const = lambda a: pl.BlockSpec(a.shape, lambda i: (0, 0))
    return pl.pallas_call(
        body, grid=(T // TB,),
        in_specs=[pl.BlockSpec((H, TB, QK), lambda i: (0, i, 0)), pl.BlockSpec((H, TB, VD), lambda i: (0, i, 0)),
                  rows(128), rows(128), const(pmt), const(s_knt), const(s_krt), const(s_vt)],
        out_specs=[rows(H * KVW), rows(128)],
        out_shape=[jax.ShapeDtypeStruct((T, H * KVW), BF16), jax.ShapeDtypeStruct((T, 128), F32)],
        compiler_params=_cp(("parallel",)), name=name)(dk, dv, cosf, sinf, pmt, s_knt, s_krt, s_vt)


def _by_query_block(run, T):
    @pl.when(pl.program_id(1) == 0)
    def _():
        run(LC)

    @pl.when(pl.program_id(1) > 0)
    def _():
        run(T)


def _with_rider(body, nin, nout, ride, grid):
    if ride is None:
        return body
    n = ride.n

    def wrapped(*refs):
        ins, xs = refs[:nin], refs[nin:nin + n]
        outs, got = refs[nin + n:nin + n + nout], refs[nin + n + nout:nin + 2 * n + nout]
        sems = refs[nin + 2 * n + nout:]
        step = pl.program_id(0) * grid[1] + pl.program_id(1)

        @pl.when(step == 0)
        def _():
            ride.start(xs, got, sems)

        body(*ins, *outs)

        @pl.when(step == grid[0] * grid[1] - 1)
        def _():
            ride.finish(xs, got, sems)

    return wrapped


def _ride_call(body, grid, in_specs, out_specs, out_shape, ride, rode, name, args):
    if ride is None:
        return pl.pallas_call(body, grid=grid, in_specs=in_specs, out_specs=out_specs, out_shape=out_shape,
                              compiler_params=_cp(("parallel", "arbitrary")), name=name)(*args), []
    res = pl.pallas_call(
        _with_rider(body, len(in_specs), len(out_specs), ride, grid), grid=grid,
        in_specs=in_specs + ride.specs, out_specs=out_specs + ride.specs, out_shape=out_shape + ride.out_shape,
        scratch_shapes=ride.scratch,
        compiler_params=pltpu.CompilerParams(dimension_semantics=("arbitrary", "arbitrary"), vmem_limit_bytes=VMEM_LIMIT,
                                             has_side_effects=True), name=name)(*args, *rode)
    return res[:len(out_specs)], res[len(out_specs):]


def attn_fwd(q, k, v, name, rode=None, modes=None):
    H, T, _ = q.shape

    def body(q_ref, k_ref, v_ref, o_ref, lse_ref):
        def run(nk):
            s = _dotf(q_ref[0], k_ref[0, pl.ds(0, nk), :], "nt")
            m = jnp.max(s, axis=1, keepdims=True)
            p = jnp.exp(s - m)
            l = jnp.sum(p, axis=1, keepdims=True)
            o = jnp.dot(p.astype(BF16), v_ref[0, pl.ds(0, nk), :], preferred_element_type=F32)
            o_ref[0] = o / l
            lse_ref[0] = m + jnp.log(l)

        _by_query_block(run, T)

    return _ride_call(
        body, (H, T // TB),
        [pl.BlockSpec((1, TB, QK), lambda h, i: (h, i, 0)), pl.BlockSpec((1, T, QK), lambda h, i: (h, 0, 0)),
         pl.BlockSpec((1, T, VD), lambda h, i: (h, 0, 0))],
        [pl.BlockSpec((1, TB, VD), lambda h, i: (h, i, 0)), pl.BlockSpec((1, TB, 1), lambda h, i: (h, i, 0))],
        [jax.ShapeDtypeStruct((H, T, VD), F32), jax.ShapeDtypeStruct((H, T, 1), F32)],
        Exchange(rode, modes) if rode else None, rode, name, (q, k, v))


def attn_bwd(q, k, v, o, lse, do, name, rode=None, modes=None):
    H, T, _ = q.shape

    def body(q_ref, k_ref, v_ref, o_ref, lse_ref, do_ref, dq_ref, dk_ref, dv_ref):
        i = pl.program_id(1)

        @pl.when(i == 0)
        def _():
            dk_ref[...] = jnp.zeros_like(dk_ref)
            dv_ref[...] = jnp.zeros_like(dv_ref)

        def run(nk):
            keys = pl.ds(0, nk)
            qv, kv, dov = q_ref[0], k_ref[0, keys, :], do_ref[0]
            p = jnp.exp(_dotf(qv, kv, "nt") - lse_ref[0])
            delta = jnp.sum(dov * o_ref[0], axis=1, keepdims=True)
            dob = dov.astype(BF16)
            dv_ref[0, keys, :] += _dotf(p.astype(BF16), dob, "tn")
            dp = _dotf(dob, v_ref[0, keys, :], "nt")
            ds = (p * (dp - delta)).astype(BF16)
            dq_ref[0] = jnp.dot(ds, kv, preferred_element_type=F32)
            dk_ref[0, keys, :] += _dotf(ds, qv, "tn")

        _by_query_block(run, T)

    blk = lambda c: pl.BlockSpec((1, TB, c), lambda h, i: (h, i, 0))
    full = lambda c: pl.BlockSpec((1, T, c), lambda h, i: (h, 0, 0))
    return _ride_call(
        body, (H, T // TB), [blk(QK), full(QK), full(VD), blk(VD), blk(1), blk(VD)], [blk(QK), full(QK), full(VD)],
        [jax.ShapeDtypeStruct((H, T, QK), F32), jax.ShapeDtypeStruct((H, T, QK), F32), jax.ShapeDtypeStruct((H, T, VD), F32)],
        Exchange(rode, modes) if rode else None, rode, name, (q, k, v, o, lse, do))


def disc_fwd(a_re, a_im, ls, name):
    def body(ar_ref, ai_ref, ls_ref, lr_ref, li_ref, fr_ref, fi_ref):
        ar, ai = ar_ref[...], ai_ref[...]
        dt = jnp.exp(ls_ref[...])
        mag = jnp.exp(ar * dt)
        lr = mag * jnp.cos(ai * dt)
        li = mag * jnp.sin(ai * dt)
        den = ar * ar + ai * ai
        nr = lr - 1.0
        lr_ref[...] = lr
        li_ref[...] = li
        fr_ref[...] = (nr * ar + li * ai) / den
        fi_ref[...] = (li * ar - nr * ai) / den

    return pl.pallas_call(body, out_shape=[jax.ShapeDtypeStruct(a_re.shape, F32)] * 4, name=name)(a_re, a_im, ls)


def disc_b(f_re, f_im, b_re, b_im, name):
    def body(fr_ref, fi_ref, br_ref, bi_ref, or_ref, oi_ref):
        fr, fi, br, bi = fr_ref[...], fi_ref[...], br_ref[...], bi_ref[...]
        or_ref[...] = fr * br - fi * bi
        oi_ref[...] = fr * bi + fi * br

    fs, bs = _disc_b_specs()
    return pl.pallas_call(body, grid=(2, G * P // DISC_ROWS), in_specs=[fs, fs, bs, bs], out_specs=[bs, bs],
                          out_shape=[jax.ShapeDtypeStruct(b_re.shape, F32)] * 2, name=name)(f_re, f_im, b_re, b_im)


DISC_ROWS = 1024


def _disc_b_specs():
    return (pl.BlockSpec((1, DISC_ROWS, 1), lambda d, i: (d, i, 0)), pl.BlockSpec((1, DISC_ROWS, CH), lambda d, i: (d, i, 0)))


def disc_b_bwd(f_re, f_im, b_re, b_im, dbb_re, dbb_im, name):
    def body(fr_ref, fi_ref, br_ref, bi_ref, dr_ref, di_ref, dbr_ref, dbi_ref, dfr_ref, dfi_ref):
        fr, fi, br, bi, dr, di = fr_ref[...], fi_ref[...], br_ref[...], bi_ref[...], dr_ref[...], di_ref[...]
        dbr_ref[...] = fr * dr + fi * di
        dbi_ref[...] = fr * di - fi * dr
        dfr_ref[...] = jnp.sum(dr * br + di * bi, axis=-1, keepdims=True)
        dfi_ref[...] = jnp.sum(di * br - dr * bi, axis=-1, keepdims=True)

    fs, bs = _disc_b_specs()
    return pl.pallas_call(body, grid=(2, G * P // DISC_ROWS), in_specs=[fs, fs, bs, bs, bs, bs], out_specs=[bs, bs, fs, fs],
                          out_shape=[jax.ShapeDtypeStruct(b_re.shape, F32)] * 2 + [jax.ShapeDtypeStruct(f_re.shape, F32)] * 2,
                          name=name)(f_re, f_im, b_re, b_im, dbb_re, dbb_im)


def disc_a_bwd(a_re, a_im, ls, dlr, dli, dfr, dfi, name):
    def body(ar_ref, ai_ref, ls_ref, dlr_ref, dli_ref, dfr_ref, dfi_ref, dar_ref, dai_ref, dls_ref):
        ar, ai = ar_ref[...], ai_ref[...]
        dt = jnp.exp(ls_ref[...])
        mag = jnp.exp(ar * dt)
        cs, sn = jnp.cos(ai * dt), jnp.sin(ai * dt)
        lr, li = mag * cs, mag * sn
        den = ar * ar + ai * ai
        nr = lr - 1.0
        f_re = (nr * ar + li * ai) / den
        f_im = (li * ar - nr * ai) / den
        dn1 = dfr_ref[...] / den
        dn2 = dfi_ref[...] / den
        dden = -(dfr_ref[...] * f_re + dfi_ref[...] * f_im) / den
        dlr_t = dlr_ref[...] + dn1 * ar - dn2 * ai
        dli_t = dli_ref[...] + dn1 * ai + dn2 * ar
        dar = dn1 * nr + dn2 * li + dden * 2.0 * ar
        dai = dn1 * li - dn2 * nr + dden * 2.0 * ai
        dmag = dlr_t * cs + dli_t * sn
        dth = dli_t * lr - dlr_t * li
        dar_ref[...] = dar + dmag * mag * dt
        dai_ref[...] = dai + dth * dt
        dls_ref[...] = jnp.sum(dmag * mag * ar + dth * ai, axis=-1, keepdims=True) * dt

    return pl.pallas_call(body, out_shape=[jax.ShapeDtypeStruct(a_re.shape, F32)] * 2 +
                          [jax.ShapeDtypeStruct(ls.shape, F32)], name=name)(a_re, a_im, ls, dlr, dli, dfr, dfi)


def _cpow(lr, li, n):
    rr, ri = None, None
    br, bi = lr, li
    while n:
        if n & 1:
            if rr is None:
                rr, ri = br, bi
            else:
                rr, ri = rr * br - ri * bi, rr * bi + ri * br
        n >>= 1
        if n:
            br, bi = br * br - bi * bi, 2.0 * br * bi
    return rr, ri


UNROLL = 4


def _seg_scan(xre, xim, lam8, pw, base, seglen, rev, init, fin_re, fin_im, ini_re, ini_im, prev=None):
    lr, li = lam8

    def rows(t):
        return pl.ds(pl.multiple_of(base + t * SEG, SEG), SEG)

    tmap = (lambda n: seglen - 1 - n) if rev else (lambda n: n)
    zero = jnp.zeros((SEG, SB), F32)

    def advance(c, t):
        a, b = c
        return lr * a - li * b + xre[rows(t), :], lr * b + li * a + xim[rows(t), :]

    fin = lax.fori_loop(0, seglen, lambda n, c: advance(c, tmap(n)), (zero, zero), unroll=UNROLL)
    fin_re[...] = fin[0]
    fin_im[...] = fin[1]
    (cr, ci), (pr, pi) = init, pw
    for i in (range(SEG - 1, -1, -1) if rev else range(SEG)):
        ini_re[pl.ds(i, 1), :] = cr
        ini_im[pl.ds(i, 1), :] = ci
        cr, ci = pr * cr - pi * ci + fin_re[pl.ds(i, 1), :], pr * ci + pi * cr + fin_im[pl.ds(i, 1), :]
    start = (ini_re[...], ini_im[...])

    def store(c, t):
        na, nb = advance(c, t)
        xre[rows(t), :] = na
        xim[rows(t), :] = nb
        return na, nb

    if prev is None:
        lax.fori_loop(0, seglen, lambda n, c: store(c, tmap(n)), start, unroll=UNROLL)
        return (cr, ci), None

    sre, sim, s_ini_re, s_ini_im = prev

    def acc_step(c, t, pre, pim):
        na, nb = store(c[:2], t)
        return na, nb, c[2] + na * pre + nb * pim, c[3] + nb * pre - na * pim

    def body(n, c):
        t = tmap(n)
        tp = t - 1 if rev else t + 1
        return acc_step(c, t, sre[rows(tp), :], sim[rows(tp), :])

    c = lax.fori_loop(0, seglen - 1, body, start + (zero, zero), unroll=UNROLL)
    c = acc_step(c, 0 if rev else seglen - 1, s_ini_re[...], s_ini_im[...])
    return (cr, ci), c[2:]


def _lam_tiles(lr, li, lens, conj=False):
    if conj:
        li = -li
    lam8 = (jnp.broadcast_to(lr, (SEG, SB)), jnp.broadcast_to(li, (SEG, SB)))
    return lam8, [_cpow(lr, li, n) for n in lens]


def _stretches(T):
    return ((0, LC // SEG), (LC, (T - LC) // SEG))


def _to_seg_order(src, dst, T):
    for base, seglen in _stretches(T):
        def body(t, carry, base=base, seglen=seglen):
            dst[pl.ds(pl.multiple_of(base + t * SEG, SEG), SEG), :] = src[pl.ds(base + t, SEG, stride=seglen), :]
            return carry
        lax.fori_loop(0, seglen, body, 0, unroll=8)


def _from_seg_order(src, dst, T):
    for base, seglen in _stretches(T):
        def body(t, carry, base=base, seglen=seglen):
            dst[pl.ds(base + t, SEG, stride=seglen), :] = src[pl.ds(pl.multiple_of(base + t * SEG, SEG), SEG), :]
            return carry
        lax.fori_loop(0, seglen, body, 0, unroll=8)


def _scan_specs(T):
    ublk = pl.BlockSpec((T, UB), lambda j: (0, j))
    lam = pl.BlockSpec((2, 1, 1, SB), lambda j: (0, j, 0, 0))
    mat = pl.BlockSpec((2, 1, UB, P), lambda j: (0, j, 0, 0))
    return ublk, lam, mat


def _dotf(a, b, mode="nn"):
    return lax.dot_general(a, b, _DN[mode], preferred_element_type=F32)


def _diag_mask():
    r = lax.broadcasted_iota(jnp.int32, (UB, SB), 0)
    c = lax.broadcasted_iota(jnp.int32, (UB, SB), 1)
    return lax.shift_right_logical(r, int(math.log2(CH))) == lax.shift_right_logical(c, int(math.log2(P)))


def _expand(m):
    p = lax.broadcasted_iota(jnp.int32, (P, SB), 0)
    c = lax.broadcasted_iota(jnp.int32, (P, SB), 1)
    tile = jnp.where(lax.bitwise_and(c, P - 1) == p, 1.0, 0.0).astype(BF16)
    wide = jnp.dot(m.astype(BF16), tile, preferred_element_type=F32)
    return jnp.where(_diag_mask(), wide, 0.0).astype(BF16)


def _collapse(full):
    c = lax.broadcasted_iota(jnp.int32, (SB, P), 0)
    p = lax.broadcasted_iota(jnp.int32, (SB, P), 1)
    pick = jnp.where(lax.bitwise_and(c, P - 1) == p, 1.0, 0.0).astype(BF16)
    return _exact_perm(jnp.where(_diag_mask(), full, 0.0), pick)


def _zero_state():
    return jnp.zeros((1, SB), F32), jnp.zeros((1, SB), F32)


def scan_fwd(u, lam_re, lam_im, bre, bim, cre, cim, name):
    T = u.shape[0]
    s_ctx, s_lat = LC // SEG, (T - LC) // SEG

    def body(u_ref, lr_ref, li_ref, bre_ref, bim_ref, cre_ref, cim_ref, y_ref, us, ys, sre, sim, fre, fim, ire, iim):
        _to_seg_order(u_ref, us, T)
        ub = us[...].astype(BF16)
        for d in range(2):
            lam8, (pw_c, pw_l) = _lam_tiles(lr_ref[d, 0], li_ref[d, 0], (s_ctx, s_lat))
            sre[...] = _dotf(ub, _expand(bre_ref[d, 0]))
            sim[...] = _dotf(ub, _expand(bim_ref[d, 0]))
            end_c, _ = _seg_scan(sre, sim, lam8, pw_c, 0, s_ctx, bool(d), _zero_state(), fre, fim, ire, iim)
            _seg_scan(sre, sim, lam8, pw_l, LC, s_lat, bool(d), end_c, fre, fim, ire, iim)
            y = (_dotf(sre[...].astype(BF16), _expand(cre_ref[d, 0]), "nt")
                 - _dotf(sim[...].astype(BF16), _expand(cim_ref[d, 0]), "nt"))
            if d == 0:
                ys[...] = y
            else:
                ys[...] += y
        _from_seg_order(ys, y_ref, T)

    ublk, lam, mat = _scan_specs(T)
    return pl.pallas_call(
        body, grid=(NJ,), in_specs=[ublk, lam, lam, mat, mat, mat, mat], out_specs=ublk,
        out_shape=jax.ShapeDtypeStruct((T, G * CH), F32),
        scratch_shapes=[pltpu.VMEM((T, UB), F32)] * 2 + [pltpu.VMEM((T, SB), F32)] * 2 + [pltpu.VMEM((SEG, SB), F32)] * 4,
        compiler_params=_cp(("arbitrary",)), name=name)(u, lam_re, lam_im, bre, bim, cre, cim)


def scan_bwd(u, dy, lam_re, lam_im, bre, bim, cre, cim, name):
    T = u.shape[0]
    s_ctx, s_lat = LC // SEG, (T - LC) // SEG

    def body(u_ref, dy_ref, lr_ref, li_ref, bre_ref, bim_ref, cre_ref, cim_ref,
             du_ref, dlr_ref, dli_ref, dbre_ref, dbim_ref, dcre_ref, dcim_ref,
             us, dys, dus, sre, sim, gre, gim, fre, fim, ic_re, ic_im, il_re, il_im, jre, jim):
        _to_seg_order(u_ref, us, T)
        _to_seg_order(dy_ref, dys, T)
        ub, dyb = us[...].astype(BF16), dys[...].astype(BF16)
        for d in range(2):
            rev = bool(d)
            lam8, (pw_c, pw_l) = _lam_tiles(lr_ref[d, 0], li_ref[d, 0], (s_ctx, s_lat))
            cam8, (cw_c, cw_l) = _lam_tiles(lr_ref[d, 0], li_ref[d, 0], (s_ctx, s_lat), conj=True)
            bre_v, bim_v = _expand(bre_ref[d, 0]), _expand(bim_ref[d, 0])
            sre[...] = _dotf(ub, bre_v)
            sim[...] = _dotf(ub, bim_v)
            end_c, _ = _seg_scan(sre, sim, lam8, pw_c, 0, s_ctx, rev, _zero_state(), fre, fim, ic_re, ic_im)
            _seg_scan(sre, sim, lam8, pw_l, LC, s_lat, rev, end_c, fre, fim, il_re, il_im)
            gre[...] = _dotf(dyb, _expand(cre_ref[d, 0]))
            gim[...] = -_dotf(dyb, _expand(cim_ref[d, 0]))
            end_g, acc_l = _seg_scan(gre, gim, cam8, cw_l, LC, s_lat, not rev, _zero_state(), fre, fim, jre, jim,
                                     prev=(sre, sim, il_re, il_im))
            _, acc_c = _seg_scan(gre, gim, cam8, cw_c, 0, s_ctx, not rev, end_g, fre, fim, jre, jim,
                                 prev=(sre, sim, ic_re, ic_im))
            dlr_ref[d, 0] = _sum0(acc_l[0] + acc_c[0])
            dli_ref[d, 0] = _sum0(acc_l[1] + acc_c[1])
            grb, gib = gre[...].astype(BF16), gim[...].astype(BF16)
            du = _dotf(grb, bre_v, "nt") + _dotf(gib, bim_v, "nt")
            if d == 0:
                dus[...] = du
            else:
                dus[...] += du
            dbre_ref[d, 0] = _collapse(_dotf(ub, grb, "tn"))
            dbim_ref[d, 0] = _collapse(_dotf(ub, gib, "tn"))
            dcre_ref[d, 0] = _collapse(_dotf(dyb, sre[...].astype(BF16), "tn"))
            dcim_ref[d, 0] = -_collapse(_dotf(dyb, sim[...].astype(BF16), "tn"))
        _from_seg_order(dus, du_ref, T)

    ublk, lam, mat = _scan_specs(T)
    lam_s = jax.ShapeDtypeStruct(lam_re.shape, F32)
    mat_s = jax.ShapeDtypeStruct(bre.shape, F32)
    return pl.pallas_call(
        body, grid=(NJ,), in_specs=[ublk, ublk, lam, lam, mat, mat, mat, mat],
        out_specs=[ublk, lam, lam, mat, mat, mat, mat],
        out_shape=[jax.ShapeDtypeStruct((T, G * CH), F32), lam_s, lam_s, mat_s, mat_s, mat_s, mat_s],
        scratch_shapes=[pltpu.VMEM((T, UB), F32)] * 3 + [pltpu.VMEM((T, SB), F32)] * 4 + [pltpu.VMEM((SEG, SB), F32)] * 8,
        compiler_params=_cp(("arbitrary",)), name=name)(u, dy, lam_re, lam_im, bre, bim, cre, cim)


class Exchange:
    def __init__(self, xs, modes):
        self.n = len(xs)
        self.modes = [modes] * self.n if isinstance(modes, (str, int)) else list(modes)
        self.out_shape = [jax.ShapeDtypeStruct(self._shape(x, md), x.dtype) for x, md in zip(xs, self.modes)]
        self.scratch = [pltpu.SemaphoreType.DMA((NDEV - 1, self.n)), pltpu.SemaphoreType.DMA((NDEV - 1, self.n)),
                        pltpu.SemaphoreType.DMA((self.n,))]
        self.specs = [pl.BlockSpec(memory_space=pl.ANY)] * self.n

    @staticmethod
    def _shape(x, mode):
        if mode == "gather":
            return (NDEV,) + tuple(x.shape)
        return tuple(x.shape) if mode == "lead" else (NDEV, x.shape[0], mode) + tuple(x.shape[2:])

    @staticmethod
    def _piece(x_ref, mode, dev):
        if mode == "gather":
            return x_ref
        return x_ref.at[dev] if mode == "lead" else x_ref.at[:, pl.ds(dev * mode, mode)]

    def _copies(self, x_refs, out_refs, sems):
        send_sems, recv_sems, local_sems = sems
        mx, my, mc = lax.axis_index("x"), lax.axis_index("y"), lax.axis_index("c")
        me = 4 * mx + 2 * my + mc
        local = [pltpu.make_async_copy(self._piece(x_ref, self.modes[a], me), out_ref.at[me], local_sems.at[a])
                 for a, (x_ref, out_ref) in enumerate(zip(x_refs, out_refs))]
        sends, recvs = [], []
        for k in range(1, NDEV):
            peer = (1 - mx if k & 4 else mx, 1 - my if k & 2 else my, 1 - mc if k & 1 else mc)
            pid = 4 * peer[0] + 2 * peer[1] + peer[2]
            for a, (x_ref, out_ref) in enumerate(zip(x_refs, out_refs)):
                src = self._piece(x_ref, self.modes[a], pid)
                sems_k = dict(send_sem=send_sems.at[k - 1, a], recv_sem=recv_sems.at[k - 1, a], device_id=peer,
                              device_id_type=MESH_T)
                sends.append(pltpu.make_async_remote_copy(src_ref=src, dst_ref=out_ref.at[me], **sems_k))
                recvs.append(pltpu.make_async_remote_copy(src_ref=src, dst_ref=out_ref.at[pid], **sems_k))
        return local, sends, recvs

    def start(self, x_refs, out_refs, sems):
        local, sends, _ = self._copies(x_refs, out_refs, sems)
        for cp in local + sends:
            cp.start()

    def finish(self, x_refs, out_refs, sems):
        local, sends, recvs = self._copies(x_refs, out_refs, sems)
        for cp in recvs:
            cp.wait_recv()
        for cp in sends:
            cp.wait_send()
        for cp in local:
            cp.wait()


def exchange(xs, modes, name):
    ex = Exchange(xs, modes)
    n = ex.n

    def body(*refs):
        ex.start(refs[:n], refs[n:2 * n], refs[2 * n:])
        ex.finish(refs[:n], refs[n:2 * n], refs[2 * n:])

    return pl.pallas_call(body, in_specs=ex.specs, out_specs=ex.specs, out_shape=ex.out_shape, scratch_shapes=ex.scratch,
                          compiler_params=pltpu.CompilerParams(has_side_effects=True), name=name)(*xs)


def _dot_f32(a, b, dn):
    return lax.dot_general(a, b, dn, preferred_element_type=F32, precision=lax.Precision.HIGHEST)


def ada_fwd(cg, c_ctx, ada_w, ada_b_loc, name):
    W = ada_w.shape[2]

    def body(cg_ref, cc_ref, w_ref, b_ref, o_ref):
        a = jnp.concatenate([_silu(cg_ref[...]), jnp.broadcast_to(_silu(cc_ref[...]), (NDEV, D))], axis=0)
        for i in range(2):
            o_ref[i] = _dot_f32(a, w_ref[i], _DN["nn"]) + b_ref[i]

    return pl.pallas_call(body, out_shape=jax.ShapeDtypeStruct((2, 2 * NDEV, W), F32),
                          compiler_params=_cp(), name=name)(cg, c_ctx, ada_w, ada_b_loc)


def ada_bwd(cg, c_ctx, ada_w, dm_loc, dm_all, name):
    W = ada_w.shape[2]

    def body(cg_ref, cc_ref, w_ref, dl_ref, da_ref, gw_ref, dcc_ref, gb_ref):
        a = jnp.concatenate([_silu(cg_ref[...]), jnp.broadcast_to(_silu(cc_ref[...]), (NDEV, D))], axis=0)
        dcc = jnp.zeros((1, D), F32)
        for i in range(2):
            dl = dl_ref[i]
            gw_ref[i] = _dot_f32(a, dl, _DN["tn"])
            dctx = jnp.sum(dl[NDEV:], axis=0, keepdims=True)
            dcc = dcc + _dot_f32(dctx, w_ref[i], _DN["nt"])
        dcc_ref[...] = dcc
        gb_ref[...] = jnp.sum(da_ref[...], axis=0)

    return pl.pallas_call(body, out_shape=[jax.ShapeDtypeStruct((2, D, W), F32), jax.ShapeDtypeStruct((1, D), F32),
                                           jax.ShapeDtypeStruct((2, 3 * D), F32)],
                          compiler_params=_cp(), name=name)(cg, c_ctx, ada_w, dm_loc, dm_all)


def cctx_finish(parts, c_ctx, name):
    def body(p_ref, cc_ref, o_ref):
        o_ref[...] = jnp.sum(p_ref[...], axis=0, keepdims=True) * _dsilu(cc_ref[...])

    return pl.pallas_call(body, out_shape=jax.ShapeDtypeStruct((1, D), F32), name=name)(parts, c_ctx)


def _adamw_update(g_ref, w_ref, m_ref, v_ref, go_ref, d_ref, mo_ref, vo_ref):
    g = g_ref[0].astype(F32)
    for s in range(1, g_ref.shape[0]):
        g = g + g_ref[s].astype(F32)
    mn = B1 * m_ref[...] + (1.0 - B1) * g
    vn = B2 * v_ref[...] + (1.0 - B2) * g * g
    go_ref[...] = g
    mo_ref[...] = mn
    vo_ref[...] = vn
    d_ref[...] = -LR * ((mn * (1.0 / (1.0 - B1 ** STEP))) / (jnp.sqrt(vn * (1.0 / (1.0 - B2 ** STEP))) + AEPS) + WD * w_ref[...])


def adamw(gstack, w, m, v, name, tr=256):
    n, R, C = gstack.shape
    tr = max(t for t in range(8, min(tr, R) + 1, 8) if R % t == 0)
    spec = pl.BlockSpec((tr, C), lambda i: (i, 0))
    return pl.pallas_call(_adamw_body(1), grid=(R // tr,),
                          in_specs=[pl.BlockSpec((n, tr, C), lambda i: (0, i, 0)), spec, spec, spec],
                          out_specs=[spec] * 4, out_shape=[jax.ShapeDtypeStruct((R, C), F32)] * 4,
                          compiler_params=_cp(("parallel",)), name=name)(gstack, w, m, v)


def _adamw_body(k):
    def body(*refs):
        for t in range(k):
            _adamw_update(*refs[4 * t:4 * t + 4], *refs[4 * k + 4 * t:4 * k + 4 * t + 4])
    return body


def adamw_multi(items, grid, name):
    k = len(items)
    ins, in_specs, out_specs, out_shape = [], [], [], []
    for g, g_spec, w, m, v, w_spec in items:
        ins += [g, w, m, v]
        in_specs += [g_spec, w_spec, w_spec, w_spec]
    for g, g_spec, w, m, v, w_spec in items:
        out_specs += [w_spec] * 4
        out_shape += [jax.ShapeDtypeStruct(w.shape, F32)] * 4
    res = pl.pallas_call(_adamw_body(k), grid=grid, in_specs=in_specs, out_specs=out_specs, out_shape=out_shape,
                         compiler_params=_cp(("arbitrary",) * len(grid)), name=name)(*ins)
    return [res[4 * t:4 * t + 4] for t in range(k)]


def _whole(a, grid_rank):
    zeros = (0,) * a.ndim
    return pl.BlockSpec(a.shape, lambda *idx: zeros)


def sum_slots(xs, name):
    def body(*refs):
        for x_ref, o_ref in zip(refs[:len(xs)], refs[len(xs):]):
            acc = x_ref[0]
            for s in range(1, NDEV):
                acc = acc + x_ref[s]
            o_ref[...] = acc

    return pl.pallas_call(body, out_shape=[jax.ShapeDtypeStruct(x.shape[1:], F32) for x in xs],
                          compiler_params=_cp(), name=name)(*xs)


def _col_shards(g):
    R, N = g.shape
    return g.reshape(R, NDEV, N // NDEV).transpose(1, 0, 2)


def _vec2(v):
    return jnp.broadcast_to(v.reshape(1, 1, -1), (2, 1, v.size))


SHARD_ROWS = {"mla_w_in": 192, "mla_w_uq": 192, "mla_w_ukv": 256, "s5_w_in": 256}


def _t_shard(wsh, rows):
    t = wsh[0].T.astype(BF16)
    return jnp.pad(t, ((0, rows - t.shape[0]), (0, 0)))


def _win_order():
    w = IN_W // NDEV
    perm = np.zeros((IN_WP, NDEV * SHARD_ROWS["mla_w_in"]), np.float32)
    first = QL + KVL + ROPE
    for c in range(IN_W):
        n = c + HEADS * VD if c < first else c - first
        perm[n, (c // w) * SHARD_ROWS["mla_w_in"] + c % w] = 1.0
    return jnp.asarray(perm, BF16)


def local_step(ctx, x, tgt, mod, Wt, small, l1_shards):
    T = LC + x.shape[0]
    xa = ("cat", ctx, x)
    sh = [mod[i, :, None, 0:D] for i in range(2)]
    sc = [mod[i, :, None, D:2 * D] for i in range(2)]
    gt = [mod[i, :, None, 2 * D:] for i in range(2)]
    ng = [_vec2(small["norm_g"][i]) for i in range(2)]
    qg, kvg = _vec2(small["mla_q_norm"]), _vec2(small["mla_kv_norm"])
    cosf, sinf, pm, pmt = _rope_tables(T)

    (h0,), _ = rowwise(st_norm_mod, [xa], [ng[0], sc[0], sh[0]], [(D, BF16)], [], "l0_norm")
    p0 = mm(h0, Wt["mla_w_in"], "nt", "l0_in")
    z0, cq, ckv = (p0, 0, HEADS * VD), (p0, HEADS * VD // QL, QL), (p0, (HEADS * VD + QL) // KVL, KVL)
    (cqn, ckvn), _ = rowwise(st_rms2, [cq, ckv], [qg, kvg], [(QL, BF16), (KVL, BF16)], [], "l0_qkvnorm")
    Q = project_q(cqn, Wt["mla_w_uq"], cosf, sinf, pm, "l0_uq")
    K, V = project_kv(ckvn, Wt["mla_w_ukv"], p0, (HEADS * VD + QL + KVL) // 128, "l0_ukv")
    (o, lse), got = attn_fwd(Q, K, V, "l0_attn", rode=l1_shards, modes="gather")
    Wt, small = dict(Wt), dict(small)
    for n, a in zip(L1_BIG, got):
        Wt[n] = a.reshape(-1, a.shape[-1])
    vecs = lax.bitcast_convert_type(got[-1].reshape(NDEV, 2, -1, 2), F32)
    small["s5_d"], small["s5_b_glu"] = vecs[:, 0, :].reshape(D), vecs[:, 1, :].reshape(D)
    o2 = o.transpose(1, 0, 2).reshape(T, HEADS * VD)
    (og, out0, x1), _ = rowwise(st_l0_post, [o2, z0, xa], [gt[0]], [(D, BF16), (D, F32), (D, F32)], [], "l0_post",
                                mats=[Wt["mla_w_out"]])

    ls = small["s5_log_step"].reshape(2, G, 1)
    a_re, a_im = small["s5_a_re"].reshape(2, G, P), small["s5_a_im"].reshape(2, G, P)
    b_re, b_im = small["s5_b_re"].reshape(2, G * P, CH), small["s5_b_im"].reshape(2, G * P, CH)
    lam_re, lam_im, f_re, f_im = disc_fwd(a_re, a_im, ls, "s5_disc")
    f_re2, f_im2 = f_re.reshape(2, G * P, 1), f_im.reshape(2, G * P, 1)
    bb_re, bb_im = disc_b(f_re2, f_im2, b_re, b_im, "s5_disc_b")
    compact = lambda m: m.reshape(2, NJ, UB, P)
    bre = compact(bb_re.reshape(2, G, P, CH).transpose(0, 1, 3, 2))
    bim = compact(bb_im.reshape(2, G, P, CH).transpose(0, 1, 3, 2))
    cre, cim = compact(small["s5_c_re"]), compact(small["s5_c_im"])
    lam_re4, lam_im4 = lam_re.reshape(2, NJ, 1, SB), lam_im.reshape(2, NJ, 1, SB)

    (h1,), _ = rowwise(st_norm_mod, [x1], [ng[1], sc[1], sh[1]], [(D, BF16)], [], "l1_norm")
    p1 = mm(h1, Wt["s5_w_in"], "nt", "l1_in")
    u, z1 = (p1, 0, D), (p1, 1, D)
    yssm = scan_fwd(p1, lam_re4, lam_im4, bre, bim, cre, cim, "s5_scan")
    dvec, bglu = _vec2(small["s5_d"]), _vec2(small["s5_b_glu"])
    fg = _vec2(small["final_g"])
    lat_mask = jnp.stack([jnp.zeros((1, D), F32), jnp.ones((1, D), F32)])
    (y, y1b, gl, y3, out1, dx2), (dfg, lvec) = rowwise(
        st_l1_mlp, [yssm, u, z1, x1, ("lat", tgt)], [dvec, bglu, gt[1], fg, lat_mask],
        [(D, F32), (D, BF16), (D, F32), (D, BF16), (D, F32), (D, F32)], [D, 128], "l1_mlp",
        mats=[Wt["s5_w_glu"], Wt["s5_w_out"]])

    (dz1, dy, du_d), (dgt1, dbglu, dd), (g_w_out5, g_w_glu) = rowwise(
        st_l1_mlp_bwd, [dx2, out1, y3, y, gl, z1, u, y1b], [gt[1], bglu, dvec], [(D, BF16), (D, F32), (D, F32)], [D, D, D],
        "l1_mlp_b", mats=[Wt["s5_w_out"], Wt["s5_w_glu"]], out_accs=[(D, D), (D, D)])
    du_s, dlr, dli, dbre, dbim, dcre, dcim = scan_bwd(p1, dy, lam_re4, lam_im4, bre, bim, cre, cim, "s5_scan_b")
    du = du_d + du_s
    dbb_re = dbre.reshape(2, G, CH, P).transpose(0, 1, 3, 2).reshape(2, G * P, CH)
    dbb_im = dbim.reshape(2, G, CH, P).transpose(0, 1, 3, 2).reshape(2, G * P, CH)
    g_c_re, g_c_im = dcre.reshape(2, G, CH, P), dcim.reshape(2, G, CH, P)
    g_b_re, g_b_im, dfr, dfi = disc_b_bwd(f_re2, f_im2, b_re, b_im, dbb_re, dbb_im, "s5_disc_b_b")
    g_a_re, g_a_im, g_ls = disc_a_bwd(a_re, a_im, ls, dlr.reshape(2, G, P), dli.reshape(2, G, P),
                                      dfr.reshape(2, G, P), dfi.reshape(2, G, P), "s5_disc_b_a")
    dp1 = jnp.concatenate([du.astype(BF16), dz1], axis=1)
    g_w_in5 = mm(h1, dp1, "tn", "l1_in_dw", out_dtype=BF16, tm=D, tn=2 * D // NDEV, shard_out=True)
    dh1 = mm(dp1, Wt["s5_w_in"], "nn", "l1_in_dx")
    (dx1,), (dsh1, dsc1, dng1) = rowwise(st_norm_mod_bwd, [x1, dh1, dx2], [ng[1], sc[1]], [(D, F32)], [D, D, D], "l1_norm_b")

    (do2, dz0), (dgt0,), (g_w_out,) = rowwise(st_l0_post_bwd, [dx1, out0, og, o2, z0], [gt[0]], [(D, F32), (D, F32)], [D],
                                              "l0_post_b", mats=[Wt["mla_w_out"]], out_accs=[(D, D)])
    g_w_out = g_w_out.astype(BF16)
    doh = do2.reshape(T, HEADS, VD).transpose(1, 0, 2)
    rows8 = lambda g: g.reshape(NDEV, -1, g.shape[-1])
    both = lambda s: s[0, 0] + s[1, 0]
    dense = lambda g: g.reshape(2, G * P * CH // 128, 128)
    chunks = [dense(g_b_re), dense(g_b_im), g_c_re, g_c_im]
    l1_send = [g_w_in5, rows8(g_w_glu), rows8(g_w_out5), both(dd).reshape(NDEV, 1, -1), both(dbglu).reshape(NDEV, 1, -1)]
    (dQ, dK, dV), l1_recv = attn_bwd(Q, K, V, o, lse, doh, "l0_attn_b", rode=l1_send + chunks,
                                     modes=["lead"] * len(l1_send) + [a.shape[1] // NDEV for a in chunks])
    dqh = rope(dQ, cosf, sinf, pmt, True, BF16, "l0_rope_q_b", scale=SCALE)
    dq = dqh.transpose(1, 0, 2).reshape(T, HEADS * QK)
    dkv, dkr = split_kv_grads(dK, dV, "l0_kv_b")
    g_w_uq = _col_shards(mm(cqn, dq, "tn", "l0_uq_dw", out_dtype=BF16))
    dcqn = mm(dq, Wt["mla_w_uq"], "nn", "l0_uq_dx")
    g_w_ukv = mm(ckvn, dkv, "tn", "l0_ukv_dw", out_dtype=BF16, tm=KVL, tn=HEADS * (NOPE + VD) // NDEV, shard_out=True)
    dckvn = mm(dkv, Wt["mla_w_ukv"], "nn", "l0_ukv_dx")
    (dcq, dckv), (dqg, dkvg) = rowwise(st_rms2_bwd, [cq, dcqn, ckv, dckvn], [qg, kvg], [(QL, F32), (KVL, F32)], [QL, KVL],
                                       "l0_qkvnorm_b")
    dp0 = jnp.concatenate([dz0, dcq, dckv, dkr], axis=1).astype(BF16)
    g_p = mm(h0, dp0, "tn", "l0_in_dw", out_dtype=BF16)
    g_w_in = _col_shards(jnp.concatenate([g_p[:, HEADS * VD:IN_W], g_p[:, :HEADS * VD]], axis=1))
    dh0 = mm(dp0, Wt["mla_w_in"], "nn", "l0_in_dx")
    (grad_x,), (dsh0, dsc0, dng0) = rowwise(st_norm_mod_bwd, [xa, dh0, dx1], [ng[0], sc[0]], [(D, F32, "lat")], [D, D, D], "l0_norm_b")

    dmod = jnp.stack([jnp.concatenate([dsh0, dsc0, dgt0], axis=-1)[:, 0], jnp.concatenate([dsh1, dsc1, dgt1], axis=-1)[:, 0]])
    gbig = {"mla_w_in": g_w_in, "mla_w_uq": g_w_uq, "mla_w_ukv": g_w_ukv, "mla_w_out": rows8(g_w_out)}
    gsmall = {"norm_g": jnp.stack([both(dng0), both(dng1)]), "mla_q_norm": both(dqg), "mla_kv_norm": both(dkvg),
              "s5_a_re": g_a_re, "s5_a_im": g_a_im, "s5_log_step": g_ls, "final_g": dfg[1, 0]}
    return lvec[1], grad_x, dmod, gbig, gsmall, l1_recv


COL_SHARDED = ("mla_w_in", "mla_w_uq", "mla_w_ukv", "s5_w_in")
ROW_SHARDED = ("mla_w_out", "s5_w_glu", "s5_w_out")
VEC_SHARDED = ("s5_d", "s5_b_glu")
BIG = COL_SHARDED + ROW_SHARDED
L0_BIG = ("mla_w_in", "mla_w_uq", "mla_w_ukv", "mla_w_out")
L1_BIG = ("s5_w_in", "s5_w_glu", "s5_w_out")
BITS16 = jnp.bfloat16
SMALL_RS = ("norm_g", "mla_q_norm", "mla_kv_norm", "s5_a_re", "s5_a_im", "s5_log_step", "s5_b_re", "s5_b_im",
            "s5_c_re", "s5_c_im", "final_g")
CHUNKED = ("s5_b_re", "s5_b_im", "s5_c_re", "s5_c_im")
DENSE = ("s5_b_re", "s5_b_im")
TINY = ("norm_g", "mla_q_norm", "mla_kv_norm", "s5_a_re", "s5_a_im", "s5_log_step", "final_g")
ORDER = ("c_ctx", "ada_w", "ada_b", "norm_g", "mla_w_in", "mla_q_norm", "mla_w_uq", "mla_kv_norm", "mla_w_ukv",
         "mla_w_out", "s5_w_in", "s5_a_re", "s5_a_im", "s5_log_step", "s5_b_re", "s5_b_im", "s5_c_re", "s5_c_im",
         "s5_d", "s5_w_glu", "s5_b_glu", "s5_w_out", "final_g")


def kernel(x, c, ctx, c_ctx, ada_w, ada_b, norm_g, mla_w_in, mla_q_norm, mla_w_uq, mla_kv_norm, mla_w_ukv, mla_w_out, s5_w_in, s5_a_re, s5_a_im, s5_log_step, s5_b_re, s5_b_im, s5_c_re, s5_c_im, s5_d, s5_w_glu, s5_b_glu, s5_w_out, final_g, loss_target, m_c_ctx, m_ada_w, m_ada_b, m_norm_g, m_mla_w_in, m_mla_q_norm, m_mla_w_uq, m_mla_kv_norm, m_mla_w_ukv, m_mla_w_out, m_s5_w_in, m_s5_a_re, m_s5_a_im, m_s5_log_step, m_s5_b_re, m_s5_b_im, m_s5_c_re, m_s5_c_im, m_s5_d, m_s5_w_glu, m_s5_b_glu, m_s5_w_out, m_final_g, v_c_ctx, v_ada_w, v_ada_b, v_norm_g, v_mla_w_in, v_mla_q_norm, v_mla_w_uq, v_mla_kv_norm, v_mla_w_ukv, v_mla_w_out, v_s5_w_in, v_s5_a_re, v_s5_a_im, v_s5_log_step, v_s5_b_re, v_s5_b_im, v_s5_c_re, v_s5_c_im, v_s5_d, v_s5_w_glu, v_s5_b_glu, v_s5_w_out, v_final_g):
    w = dict(c_ctx=c_ctx, ada_w=ada_w, ada_b=ada_b, norm_g=norm_g, mla_w_in=mla_w_in, mla_q_norm=mla_q_norm,
             mla_w_uq=mla_w_uq, mla_kv_norm=mla_kv_norm, mla_w_ukv=mla_w_ukv, mla_w_out=mla_w_out, s5_w_in=s5_w_in,
             s5_a_re=s5_a_re, s5_a_im=s5_a_im, s5_log_step=s5_log_step, s5_b_re=s5_b_re, s5_b_im=s5_b_im,
             s5_c_re=s5_c_re, s5_c_im=s5_c_im, s5_d=s5_d, s5_w_glu=s5_w_glu, s5_b_glu=s5_b_glu, s5_w_out=s5_w_out,
             final_g=final_g)
    m = dict(c_ctx=m_c_ctx, ada_w=m_ada_w, ada_b=m_ada_b, norm_g=m_norm_g, mla_w_in=m_mla_w_in, mla_q_norm=m_mla_q_norm,
             mla_w_uq=m_mla_w_uq, mla_kv_norm=m_mla_kv_norm, mla_w_ukv=m_mla_w_ukv, mla_w_out=m_mla_w_out,
             s5_w_in=m_s5_w_in, s5_a_re=m_s5_a_re, s5_a_im=m_s5_a_im, s5_log_step=m_s5_log_step, s5_b_re=m_s5_b_re,
             s5_b_im=m_s5_b_im, s5_c_re=m_s5_c_re, s5_c_im=m_s5_c_im, s5_d=m_s5_d, s5_w_glu=m_s5_w_glu,
             s5_b_glu=m_s5_b_glu, s5_w_out=m_s5_w_out, final_g=m_final_g)
    v = dict(c_ctx=v_c_ctx, ada_w=v_ada_w, ada_b=v_ada_b, norm_g=v_norm_g, mla_w_in=v_mla_w_in, mla_q_norm=v_mla_q_norm,
             mla_w_uq=v_mla_w_uq, mla_kv_norm=v_mla_kv_norm, mla_w_ukv=v_mla_w_ukv, mla_w_out=v_mla_w_out,
             s5_w_in=v_s5_w_in, s5_a_re=v_s5_a_re, s5_a_im=v_s5_a_im, s5_log_step=v_s5_log_step, s5_b_re=v_s5_b_re,
             s5_b_im=v_s5_b_im, s5_c_re=v_s5_c_re, s5_c_im=v_s5_c_im, s5_d=v_s5_d, s5_w_glu=v_s5_w_glu,
             s5_b_glu=v_s5_b_glu, s5_w_out=v_s5_w_out, final_g=v_final_g)

    me = 4 * lax.axis_index("x") + 2 * lax.axis_index("y") + lax.axis_index("c")
    WA = ada_w.shape[2]

    def shard(n):
        return _t_shard(w[n], SHARD_ROWS[n]) if n in COL_SHARDED else w[n][0].astype(BF16)

    wgot = exchange([c] + [shard(n) for n in L0_BIG], "gather", "gather_w")

    cg = wgot[0].reshape(NDEV, D)
    cc2 = c_ctx.reshape(1, D)
    ada_b_loc = lax.dynamic_slice_in_dim(ada_b.reshape(2, 3 * D // WA, WA), me, 1, axis=1)
    part = ada_fwd(cg, cc2, ada_w, ada_b_loc, "ada_fwd")
    pg = exchange([part], "gather", "gather_mod")[0]
    mod_l = lax.dynamic_index_in_dim(pg, me, axis=2, keepdims=False).transpose(1, 0, 2).reshape(2, 3 * D)
    mod_c = pg[:, :, NDEV, :].transpose(1, 0, 2).reshape(2, 3 * D)
    mod = jnp.stack([mod_c, mod_l], axis=1)

    Wt = {n: a.reshape(-1, a.shape[-1]) for n, a in zip(L0_BIG, wgot[1:])}
    Wt["mla_w_in"] = mm(_win_order(), Wt["mla_w_in"], "nn", "w_in_order", out_dtype=BF16)
    vec_bits = lax.bitcast_convert_type(jnp.concatenate([s5_d, s5_b_glu], axis=0), BITS16).reshape(2, -1)
    small = {n: w[n] for n in SMALL_RS}

    lvec, grad_x, dmod, gbig, gsmall, l1_recv = local_step(ctx[0], x[0], loss_target[0], mod, Wt, small,
                                                           [shard(n) for n in L1_BIG] + [vec_bits])
    loss = lax.psum(lvec[0, 0], ("x", "y", "c"))
    grad_x = grad_x[None]

    per_dev = G // NDEV
    recv = dict(zip(L0_BIG, exchange([gbig[n] for n in L0_BIG], "lead", "scatter_grads")))
    recv.update(dict(zip(L1_BIG + VEC_SHARDED, l1_recv)))
    out = {}

    def keep(n, res):
        for key, arr in zip("gdmv", res):
            out[key, n] = arr.reshape(w[n].shape)

    for n in BIG:
        keep(n, adamw(recv[n], w[n][0], m[n][0], v[n][0], "adamw_" + n))
    reduced = sum_slots(l1_recv[len(L1_BIG + VEC_SHARDED):], "sum_chunks")

    kshape = lambda n: w[n].shape if w[n].ndim > 1 else (1, w[n].size)
    got = exchange(list(reduced) + [gsmall[n].reshape(kshape(n)) for n in TINY] + [dmod], "gather", "gather_small")
    chunk_all, tiny_all, dm_all = got[:len(CHUNKED)], got[len(CHUNKED):-1], got[-1]

    dm_cols = lax.dynamic_slice_in_dim(dm_all.reshape(NDEV, 2, 2, 3 * D // WA, WA), me, 1, axis=3)[:, :, :, 0]
    dm_loc = jnp.concatenate([dm_cols[:, :, 1].transpose(1, 0, 2), dm_cols[:, :, 0].transpose(1, 0, 2)], axis=1)
    g_ada_w, dcc_part, g_ada_b = ada_bwd(cg, cc2, ada_w, dm_loc, dm_all.transpose(0, 2, 1, 3).reshape(2 * NDEV, 2, 3 * D), "ada_bwd")
    dcc_all = exchange([dcc_part], "gather", "gather_dcc")[0].reshape(NDEV, D)
    g_c_ctx = cctx_finish(dcc_all, cc2, "cctx_finish")

    flat2 = lambda t: t.reshape(-1, t.shape[-1])
    keep("ada_w", adamw(flat2(g_ada_w)[None], flat2(ada_w), flat2(m_ada_w), flat2(v_ada_w), "adamw_ada"))
    items = []
    for n, g in zip(CHUNKED, chunk_all):
        blk = (1, 1, per_dev) + w[n].shape[3:]
        if n in DENSE:
            g = g.transpose(1, 0, 2, 3).reshape(w[n].shape)
            g_spec = pl.BlockSpec((1, 1, 1) + blk[2:], lambda d, s: (0, 0, d, s, 0, 0))
        else:
            g_spec = pl.BlockSpec((1, 1, 1) + blk[2:], lambda d, s: (0, s, d, 0, 0, 0))
        items.append((g[None], g_spec, w[n], m[n], v[n], pl.BlockSpec(blk, lambda d, s: (0, d, s, 0, 0))))
    for n, res in zip(CHUNKED, adamw_multi(items, (2, NDEV), "adamw_bc")):
        keep(n, res)
    tiny_g = dict(zip(TINY, tiny_all))
    tiny_g.update({n: recv[n] for n in VEC_SHARDED})
    tiny_g["c_ctx"], tiny_g["ada_b"] = g_c_ctx[None], g_ada_b[None]
    names = list(tiny_g)
    items = [(tiny_g[n], _whole(tiny_g[n], 1)) + tuple(t[n].reshape(kshape(n)) for t in (w, m, v))
             + (pl.BlockSpec(kshape(n), lambda i, r=len(kshape(n)): (0,) * r),) for n in names]
    for n, res in zip(names, adamw_multi(items, (1,), "adamw_small")):
        keep(n, res)

    return (loss, grad_x, *[out["g", n] for n in ORDER], *[out["d", n] for n in ORDER],
            *[out["m", n] for n in ORDER], *[out["v", n] for n in ORDER])
```

```python
import math

import numpy as np
import jax
import jax.numpy as jnp
from jax import lax
from jax.experimental import pallas as pl
from jax.experimental.pallas import tpu as pltpu

F32 = jnp.float32
BF16 = jnp.bfloat16

D = 1024
L = 2048
LC = 256
NDEV = 8
GRID_W = 64
EPS = 1e-6
HEADS = 16
NOPE = 64
ROPE = 32
QK = NOPE + ROPE
VD = 64
IN_W = 256 + 128 + ROPE + HEADS * 64
IN_WP = 1536
QL = 256
KVL = 128
SCALE = QK ** -0.5
THETA = 10000.0
G = 64
P = 64
CH = 16
GB = 8
NJ = G // GB
UB = GB * CH
SB = GB * P
SEG = 8
TB = 256
VMEM_LIMIT = 56 * 1024 * 1024
B1, B2, LR, AEPS, WD, STEP = 0.9, 0.999, 0.001, 1e-8, 0.01, 10
MESH_T = pl.DeviceIdType.MESH


def _cp(sem=None):
    return pltpu.CompilerParams(dimension_semantics=sem, vmem_limit_bytes=VMEM_LIMIT)


def _sig(x):
    return 1.0 / (1.0 + jnp.exp(-x))


def _silu(x):
    return x * _sig(x)


def _dsilu(x):
    s = _sig(x)
    return s * (1.0 + x * (1.0 - s))


_GK = math.sqrt(2.0 / math.pi)


def _gelu(x):
    return 0.5 * x * (1.0 + jnp.tanh(_GK * (x + 0.044715 * x * x * x)))


def _dgelu(x):
    t = jnp.tanh(_GK * (x + 0.044715 * x * x * x))
    return 0.5 * (1.0 + t) + 0.5 * x * (1.0 - t * t) * _GK * (1.0 + 3 * 0.044715 * x * x)


def _rs(x):
    return lax.rsqrt(jnp.mean(x * x, axis=-1, keepdims=True) + EPS)


def _sum0(x):
    return jnp.sum(x, axis=0, keepdims=True)


def st_norm_mod(x, g, sc, sh):
    y = x * _rs(x) * g
    return (y * (1.0 + sc) + sh,), ()


def st_norm_mod_bwd(x, dh, dres, g, sc):
    r = _rs(x)
    xn = x * r
    y = xn * g
    dy = dh * (1.0 + sc)
    dxn = dy * g
    dx = r * (dxn - xn * jnp.mean(dxn * xn, axis=-1, keepdims=True))
    return (dres + dx,), (_sum0(dh), _sum0(dh * y), _sum0(dy * xn))


def st_rms(x, g):
    return (x * _rs(x) * g,), ()


def st_rms_bwd(x, dy, g):
    r = _rs(x)
    n = x * r
    dn = dy * g
    dx = r * (dn - n * jnp.mean(dn * n, axis=-1, keepdims=True))
    return (dx,), (_sum0(dy * n),)


def st_rms2(x1, x2, g1, g2):
    return st_rms(x1, g1)[0] + st_rms(x2, g2)[0], ()


def st_rms2_bwd(x1, dy1, x2, dy2, g1, g2):
    (d1,), (s1,) = st_rms_bwd(x1, dy1, g1)
    (d2,), (s2,) = st_rms_bwd(x2, dy2, g2)
    return (d1, d2), (s1, s2)


def st_gate(o, z):
    return (o * _silu(z),), ()


def st_gate_bwd(dog, o, z):
    return (dog * _silu(z), dog * o * _dsilu(z)), ()


def st_resid(x, out, gt):
    return (x + gt * out,), ()


def st_resid_bwd(dx, out, gt):
    return (dx * gt,), (_sum0(dx * out),)


def st_s5a(yssm, u, d):
    y = yssm + d * u
    return (y, _gelu(y)), ()


def st_s5b(y, gl, z, b):
    return (_gelu(y) * _sig(gl + b) * _silu(z),), ()


def st_s5b_bwd(dy3, y, gl, z, b):
    y1 = _gelu(y)
    s = _sig(gl + b)
    dy2 = dy3 * _silu(z)
    dz = dy3 * y1 * s * _dsilu(z)
    dgl = dy2 * y1 * s * (1.0 - s)
    return (dgl, dz, dy2 * s), (_sum0(dgl),)


def st_s5a_bwd(dy1a, dy1b, y, u, d):
    dy = (dy1a + dy1b) * _dgelu(y)
    return (dy, dy * d), (_sum0(dy * u),)


def st_l0_pre(x, g, sc, sh, qg, kvg, w_in):
    hb = st_norm_mod(x, g, sc, sh)[0][0].astype(BF16)
    p = lax.dot_general(hb, w_in, _DN["nt"], preferred_element_type=F32)
    cq, ckv = p[:, HEADS * VD:HEADS * VD + QL], p[:, HEADS * VD + QL:HEADS * VD + QL + KVL]
    return (hb, p) + st_rms2(cq, ckv, qg, kvg)[0], ()


def st_l0_tail_bwd(dq, dkv, dkr, dz, cq, ckv, cqn, ckvn, h, x, dres, qg, kvg, g, sc, w_uq, w_ukv, w_in):
    dcqn = jnp.dot(dq, w_uq, preferred_element_type=F32)
    dckvn = jnp.dot(dkv, w_ukv, preferred_element_type=F32)
    (dcq, dckv), (dqg, dkvg) = st_rms2_bwd(cq, dcqn, ckv, dckvn, qg, kvg)
    dp = jnp.concatenate([dz, dcq, dckv, dkr], axis=1).astype(BF16)
    dh = jnp.dot(dp, w_in, preferred_element_type=F32)
    outs, sums = st_norm_mod_bwd(x, dh, dres, g, sc)
    tn = lambda a, b: lax.dot_general(a, b, _DN["tn"], preferred_element_type=F32)
    return outs, (dqg, dkvg) + sums, (tn(cqn, dq), tn(ckvn, dkv), tn(h, dp))


def st_l1_pre(x, g, sc, sh, w_in):
    hb = st_norm_mod(x, g, sc, sh)[0][0].astype(BF16)
    return (hb, lax.dot_general(hb, w_in, _DN["nt"], preferred_element_type=F32)), ()


def st_l1_tail_bwd(du_a, du_b, dz, h, x, dres, g, sc, w_in):
    dp = jnp.concatenate([(du_a + du_b).astype(BF16), dz], axis=1)
    dh = jnp.dot(dp, w_in, preferred_element_type=F32)
    outs, sums = st_norm_mod_bwd(x, dh, dres, g, sc)
    return outs, sums, (lax.dot_general(h, dp, _DN["tn"], preferred_element_type=F32),)


def st_l0_post(o, z, x, gt, w_out):
    og = (o * _silu(z)).astype(BF16)
    out = jnp.dot(og, w_out, preferred_element_type=F32)
    return (og, out, x + gt * out), ()


def st_l0_post_bwd(dx1, out, og, o, z, gt, w_out):
    (dout,), (dgt,) = st_resid_bwd(dx1, out, gt)
    doutb = dout.astype(BF16)
    dog = lax.dot_general(doutb, w_out, _DN["nt"], preferred_element_type=F32)
    return st_gate_bwd(dog, o, z)[0], (dgt,), (lax.dot_general(og, doutb, _DN["tn"], preferred_element_type=F32),)


def st_l1_mlp(yssm, u, z, x1, tgt, d, bglu, gt, fg, mask, w_glu, w_out):
    (y, y1), _ = st_s5a(yssm, u, d)
    y1b = y1.astype(BF16)
    gl = jnp.dot(y1b, w_glu, preferred_element_type=F32)
    y3 = (y1 * _sig(gl + bglu) * _silu(z)).astype(BF16)
    out = jnp.dot(y3, w_out, preferred_element_type=F32)
    (dx2,), sums = st_final(x1 + gt * out, tgt, fg, mask)
    return (y, y1b, gl, y3, out, dx2), sums


def st_l1_mlp_bwd(dx2, out, y3, y, gl, z, u, y1b, gt, bglu, d, w_out, w_glu):
    (dout,), (dgt,) = st_resid_bwd(dx2, out, gt)
    doutb = dout.astype(BF16)
    dy3 = lax.dot_general(doutb, w_out, _DN["nt"], preferred_element_type=F32)
    (dgl, dz, dy1a), (dbglu,) = st_s5b_bwd(dy3, y, gl, z, bglu)
    dglb = dgl.astype(BF16)
    dy1b = lax.dot_general(dglb, w_glu, _DN["nt"], preferred_element_type=F32)
    (dy, du), (dd,) = st_s5a_bwd(dy1a, dy1b, y, u, d)
    g_w_out = lax.dot_general(y3, doutb, _DN["tn"], preferred_element_type=F32)
    g_w_glu = lax.dot_general(y1b, dglb, _DN["tn"], preferred_element_type=F32)
    return (dz, dy, du), (dgt, dbglu, dd), (g_w_out, g_w_glu)


def st_final(x2, tgt, g, mask):
    r = _rs(x2)
    n = x2 * r
    e = n * g - tgt
    dyo = e * (1.0 / D)
    dn = dyo * g
    dx = r * (dn - n * jnp.mean(dn * n, axis=-1, keepdims=True))
    lsum = jnp.sum(_sum0(e * e), axis=1, keepdims=True) * (0.5 / D)
    return (dx * mask,), (_sum0(dyo * n), jnp.broadcast_to(lsum, (1, 128)))


def rowwise(fn, rows, vecs, out_rows, out_sums, name, mats=(), out_accs=()):
    lat_blk = lambda i: jnp.maximum(i - 1, 0)
    arrays, in_specs, pick = [], [], []
    for a in rows:
        if not isinstance(a, tuple):
            a = (a, 0, a.shape[1])
        tag = a[0] if isinstance(a[0], str) else None
        if tag == "cat":
            _, ctx, x = a
            arrays += [ctx, x]
            in_specs += [pl.BlockSpec((TB, ctx.shape[1]), lambda i: (0, 0)),
                         pl.BlockSpec((TB, x.shape[1]), lambda i: (lat_blk(i), 0))]
            pick.append(2)
        elif tag == "lat":
            arrays.append(a[1])
            in_specs.append(pl.BlockSpec((TB, a[1].shape[1]), lambda i: (lat_blk(i), 0)))
            pick.append(1)
        else:
            arr, cb, width = a
            arrays.append(arr)
            in_specs.append(pl.BlockSpec((TB, width), lambda i, cb=cb: (i, cb)))
            pick.append(1)
    T = LC + L
    nin, nv, nm, no, ns = len(arrays), len(vecs), len(mats), len(out_rows), len(out_sums)

    def body(*refs):
        i = pl.program_id(0)
        vals, k = [], 0
        for p in pick:
            if p == 2:
                vals.append(jnp.where(i == 0, refs[k][...], refs[k + 1][...]))
            else:
                vals.append(refs[k][...])
            k += p
        vals += [r[0] for r in refs[nin:nin + nv]] + [r[...] for r in refs[nin + nv:nin + nv + nm]]
        res = fn(*vals)
        first_out = nin + nv + nm
        for r, o in zip(refs[first_out:first_out + no], res[0]):
            r[...] = o.astype(r.dtype)
        sum_refs = refs[first_out + no:first_out + no + ns]
        if sum_refs:
            @pl.when(i <= 1)
            def _():
                for r in sum_refs:
                    r[...] = jnp.zeros_like(r)
            for r, s in zip(sum_refs, res[1]):
                r[0] += s
        acc_refs = refs[first_out + no + ns:]
        if acc_refs:
            @pl.when(i == 0)
            def _():
                for r in acc_refs:
                    r[...] = jnp.zeros_like(r)
            for r, a in zip(acc_refs, res[2]):
                r[...] += a

    kind = lambda i: (jnp.minimum(i, 1), 0, 0)
    in_specs += [pl.BlockSpec((1, 1, v.shape[2]), kind) for v in vecs]
    in_specs += [pl.BlockSpec(m.shape, lambda i: (0, 0), pipeline_mode=pl.Buffered(1)) for m in mats]
    out_specs, out_shape = [], []
    for o in out_rows:
        lat = len(o) == 3
        out_specs.append(pl.BlockSpec((TB, o[0]), (lambda i: (lat_blk(i), 0)) if lat else (lambda i: (i, 0))))
        out_shape.append(jax.ShapeDtypeStruct((L if lat else T, o[0]), o[1]))
    out_specs += [pl.BlockSpec((1, 1, c), kind) for c in out_sums]
    out_shape += [jax.ShapeDtypeStruct((2, 1, c), F32) for c in out_sums]
    out_specs += [pl.BlockSpec(s, lambda i: (0, 0)) for s in out_accs]
    out_shape += [jax.ShapeDtypeStruct(s, F32) for s in out_accs]
    res = pl.pallas_call(body, grid=(T // TB,), in_specs=in_specs, out_specs=out_specs, out_shape=out_shape,
                         compiler_params=_cp(("arbitrary",)), name=name)(*arrays, *vecs, *mats)
    if out_accs:
        return res[:no], res[no:no + ns], res[no + ns:]
    return res[:no], res[no:]


_DN = {"nn": (((1,), (0,)), ((), ())), "nt": (((1,), (1,)), ((), ())), "tn": (((0,), (0,)), ((), ()))}


def mm(a, b, mode, name, out_dtype=F32, tm=None, tn=None, shard_out=False):
    if mode == "nn":
        (M, K), (_, N) = a.shape, b.shape
    elif mode == "nt":
        (M, K), (N, _) = a.shape, b.shape
    else:
        (K, M), (_, N) = a.shape, b.shape
    if tm is None:
        tm = next((t for t in (768, 512, 256) if M % t == 0 and M > t), M)
    tn = N if tn is None else tn
    dn = _DN[mode]

    def body(a_ref, b_ref, o_ref):
        o_ref[...] = lax.dot_general(a_ref[...].astype(BF16), b_ref[...].astype(BF16), dn,
                                     preferred_element_type=F32).astype(o_ref.dtype)

    if shard_out:
        def body(a_ref, b_ref, o_ref):
            av = a_ref[...].astype(BF16)
            for j in range(N // tn):
                bj = b_ref[pl.ds(j * tn, tn), :] if mode == "nt" else b_ref[:, pl.ds(j * tn, tn)]
                o_ref[j] = lax.dot_general(av, bj.astype(BF16), dn, preferred_element_type=F32).astype(o_ref.dtype)

        a_spec = pl.BlockSpec((K, tm), lambda i: (0, i)) if mode == "tn" else pl.BlockSpec((tm, K), lambda i: (i, 0))
        return pl.pallas_call(body, grid=(M // tm,), in_specs=[a_spec, pl.BlockSpec(b.shape, lambda i: (0, 0))],
                              out_specs=pl.BlockSpec((N // tn, tm, tn), lambda i: (0, i, 0)),
                              out_shape=jax.ShapeDtypeStruct((N // tn, M, tn), out_dtype),
                              compiler_params=_cp(("parallel",)), name=name)(a, b)
    a_spec = pl.BlockSpec((K, tm), lambda i, j: (0, i)) if mode == "tn" else pl.BlockSpec((tm, K), lambda i, j: (i, 0))
    b_spec = pl.BlockSpec((tn, K), lambda i, j: (j, 0)) if mode == "nt" else pl.BlockSpec((K, tn), lambda i, j: (0, j))
    return pl.pallas_call(body, grid=(M // tm, N // tn), in_specs=[a_spec, b_spec],
                          out_specs=pl.BlockSpec((tm, tn), lambda i, j: (i, j)), out_shape=jax.ShapeDtypeStruct((M, N), out_dtype),
                          compiler_params=_cp(("parallel", "arbitrary")), name=name)(a, b)


def _rope_tables(T, width=QK, first=NOPE):
    nlat = T - LC
    pos = np.arange(nlat)
    row, col = pos // GRID_W, pos % GRID_W
    half = ROPE // 2
    inv = 1.0 / (THETA ** (np.arange(0, half, 2, dtype=np.float64) / half))
    cosf = np.ones((T, width), np.float64)
    sinf = np.zeros((T, width), np.float64)
    perm = np.zeros((width, width), np.float32)
    for m in range(ROPE):
        j = first + m
        blk, w = m // half, m % half
        ang = (row if blk == 0 else col)[:, None] * inv[None, :]
        f = w % (half // 2)
        cosf[LC:, j] = np.cos(ang[:, f])
        if w < half // 2:
            sinf[LC:, j] = -np.sin(ang[:, f])
            perm[j + half // 2, j] = 1.0
        else:
            sinf[LC:, j] = np.sin(ang[:, f])
            perm[j - half // 2, j] = 1.0
    return jnp.asarray(cosf, F32), jnp.asarray(sinf, F32), jnp.asarray(perm, BF16), jnp.asarray(perm.T, BF16)


def _exact_perm(x, pm):
    hi = x.astype(BF16)
    r1 = x - hi.astype(F32)
    mid = r1.astype(BF16)
    lo = (r1 - mid.astype(F32)).astype(BF16)
    dot = lambda a: jnp.dot(a, pm, preferred_element_type=F32)
    return dot(hi) + dot(mid) + dot(lo)


def _rot(x, cv, sv, pv, inverse):
    if inverse:
        return x * cv + _exact_perm(x * sv, pv)
    return x * cv + _exact_perm(x, pv) * sv


def rope(x, cosf, sinf, pm, inverse, out_dtype, name, scale=1.0):
    H, T, _ = x.shape

    def body(x_ref, c_ref, s_ref, p_ref, o_ref):
        cv, sv, pv = c_ref[...], s_ref[...], p_ref[...]
        for h in range(H):
            o_ref[h] = (_rot(x_ref[h], cv, sv, pv, inverse) * scale).astype(o_ref.dtype)

    return pl.pallas_call(
        body, grid=(T // TB,),
        in_specs=[pl.BlockSpec((H, TB, QK), lambda i: (0, i, 0)), pl.BlockSpec((TB, QK), lambda i: (i, 0)),
                  pl.BlockSpec((TB, QK), lambda i: (i, 0)), pl.BlockSpec((QK, QK), lambda i: (0, 0))],
        out_specs=pl.BlockSpec((H, TB, QK), lambda i: (0, i, 0)), out_shape=jax.ShapeDtypeStruct((H, T, QK), out_dtype),
        compiler_params=_cp(("parallel",)), name=name)(x, cosf, sinf, pm)


KVW = NOPE + VD


def _kv_selectors():
    s_kn = np.zeros((KVW, QK), np.float32)
    s_kr = np.zeros((128, QK), np.float32)
    s_v = np.zeros((KVW, VD), np.float32)
    for l in range(NOPE):
        s_kn[l, l] = 1.0
    for l in range(ROPE):
        s_kr[l, NOPE + l] = 1.0
    for l in range(VD):
        s_v[NOPE + l, l] = 1.0
    return s_kn, s_kr, s_v


def project_q(cqn, w, cosf, sinf, pm, name):
    T = cqn.shape[0]

    def body(a_ref, w_ref, c_ref, s_ref, p_ref, o_ref):
        a, cv, sv, pv = a_ref[...], c_ref[...], s_ref[...], p_ref[...]
        for h in range(HEADS):
            qh = _dotf(a, w_ref[pl.ds(h * QK, QK), :], "nt")
            o_ref[h] = (_rot(qh, cv, sv, pv, False) * SCALE).astype(BF16)

    rows = lambda c: pl.BlockSpec((TB, c), lambda i: (i, 0))
    const = lambda x: pl.BlockSpec(x.shape, lambda i: (0, 0))
    return pl.pallas_call(
        body, grid=(T // TB,), in_specs=[rows(QL), const(w), rows(QK), rows(QK), const(pm)],
        out_specs=pl.BlockSpec((HEADS, TB, QK), lambda i: (0, i, 0)), out_shape=jax.ShapeDtypeStruct((HEADS, T, QK), BF16),
        compiler_params=_cp(("parallel",)), name=name)(cqn, w, cosf, sinf, pm)


def project_kv(ckvn, w, p0, kr_block, name):
    T = ckvn.shape[0]
    cosf, sinf, pm, _ = _rope_tables(T, 128, 0)
    s_kn, s_kr, s_v = (jnp.asarray(s, BF16) for s in _kv_selectors())

    def body(a_ref, w_ref, kr_ref, c_ref, s_ref, p_ref, skn_ref, skr_ref, sv_ref, k_ref, v_ref):
        a = a_ref[...]
        krr = _rot(kr_ref[...], c_ref[...], s_ref[...], p_ref[...], False).astype(BF16)
        kr_part = jnp.dot(krr, skr_ref[...], preferred_element_type=F32)
        for h in range(HEADS):
            kvb = _dotf(a, w_ref[pl.ds(h * KVW, KVW), :], "nt").astype(BF16)
            k_ref[h] = (jnp.dot(kvb, skn_ref[...], preferred_element_type=F32) + kr_part).astype(BF16)
            v_ref[h] = jnp.dot(kvb, sv_ref[...], preferred_element_type=F32).astype(BF16)

    rows = lambda c: pl.BlockSpec((TB, c), lambda i: (i, 0))
    const = lambda x: pl.BlockSpec(x.shape, lambda i: (0, 0))
    return pl.pallas_call(
        body, grid=(T // TB,),
        in_specs=[rows(KVL), const(w), pl.BlockSpec((TB, 128), lambda i: (i, kr_block)),
                  rows(128), rows(128), const(pm), const(s_kn), const(s_kr), const(s_v)],
        out_specs=[pl.BlockSpec((HEADS, TB, QK), lambda i: (0, i, 0)), pl.BlockSpec((HEADS, TB, VD), lambda i: (0, i, 0))],
        out_shape=[jax.ShapeDtypeStruct((HEADS, T, QK), BF16), jax.ShapeDtypeStruct((HEADS, T, VD), BF16)],
        compiler_params=_cp(("parallel",)), name=name)(ckvn, w, p0, cosf, sinf, pm, s_kn, s_kr, s_v)


def split_kv_grads(dk, dv, name):
    H, T, _ = dk.shape
    cosf, sinf, _, pmt = _rope_tables(T, 128, 0)
    s_kn, s_kr, s_v = _kv_selectors()
    s_knt, s_krt, s_vt = (jnp.asarray(s.T, BF16) for s in (s_kn, s_kr, s_v))

    def body(dk_ref, dv_ref, c_ref, s_ref, p_ref, skn_ref, skr_ref, sv_ref, dkv_ref, dkr_ref):
        total = None
        for h in range(H):
            dkh = dk_ref[h]
            total = dkh if total is None else total + dkh
            dkv_ref[:, pl.ds(h * KVW, KVW)] = (
                jnp.dot(dkh.astype(BF16), skn_ref[...], preferred_element_type=F32)
                + jnp.dot(dv_ref[h].astype(BF16), sv_ref[...], preferred_element_type=F32)).astype(BF16)
        dkr_ref[...] = _rot(_exact_perm(total, skr_ref[...]), c_ref[...], s_ref[...], p_ref[...], True)

    rows = lambda c: pl.BlockSpec((TB, c), lambda i: (i, 0))
    const = lambda a: pl.BlockSpec(a.shape, lambda i: (0, 0))
    return pl.pallas_call(
        body, grid=(T // TB,),
        in_specs=[pl.BlockSpec((H, TB, QK), lambda i: (0, i, 0)), pl.BlockSpec((H, TB, VD), lambda i: (0, i, 0)),
                  rows(128), rows(128), const(pmt), const(s_knt), const(s_krt), const(s_vt)],
        out_specs=[rows(H * KVW), rows(128)],
        out_shape=[jax.ShapeDtypeStruct((T, H * KVW), BF16), jax.ShapeDtypeStruct((T, 128), F32)],
        compiler_params=_cp(("parallel",)), name=name)(dk, dv, cosf, sinf, pmt, s_knt, s_krt, s_vt)


def _by_query_block(run, T):
    @pl.when(pl.program_id(1) == 0)
    def _():
        run(LC)

    @pl.when(pl.program_id(1) > 0)
    def _():
        run(T)


def _with_rider(body, nin, nout, ride, grid):
    if ride is None:
        return body
    n = ride.n

    def wrapped(*refs):
        ins, xs = refs[:nin], refs[nin:nin + n]
        outs, got = refs[nin + n:nin + n + nout], refs[nin + n + nout:nin + 2 * n + nout]
        sems = refs[nin + 2 * n + nout:]
        step = pl.program_id(0) * grid[1] + pl.program_id(1)

        @pl.when(step == 0)
        def _():
            ride.start(xs, got, sems)

        body(*ins, *outs)

        @pl.when(step == grid[0] * grid[1] - 1)
        def _():
            ride.finish(xs, got, sems)

    return wrapped


def _ride_call(body, grid, in_specs, out_specs, out_shape, ride, rode, name, args):
    if ride is None:
        return pl.pallas_call(body, grid=grid, in_specs=in_specs, out_specs=out_specs, out_shape=out_shape,
                              compiler_params=_cp(("parallel", "arbitrary")), name=name)(*args), []
    res = pl.pallas_call(
        _with_rider(body, len(in_specs), len(out_specs), ride, grid), grid=grid,
        in_specs=in_specs + ride.specs, out_specs=out_specs + ride.specs, out_shape=out_shape + ride.out_shape,
        scratch_shapes=ride.scratch,
        compiler_params=pltpu.CompilerParams(dimension_semantics=("arbitrary", "arbitrary"), vmem_limit_bytes=VMEM_LIMIT,
                                             has_side_effects=True), name=name)(*args, *rode)
    return res[:len(out_specs)], res[len(out_specs):]


def attn_fwd(q, k, v, name, rode=None, modes=None):
    H, T, _ = q.shape

    def body(q_ref, k_ref, v_ref, o_ref, lse_ref):
        def run(nk):
            s = _dotf(q_ref[0], k_ref[0, pl.ds(0, nk), :], "nt")
            m = jnp.max(s, axis=1, keepdims=True)
            p = jnp.exp(s - m)
            l = jnp.sum(p, axis=1, keepdims=True)
            o = jnp.dot(p.astype(BF16), v_ref[0, pl.ds(0, nk), :], preferred_element_type=F32)
            o_ref[0] = o / l
            lse_ref[0] = m + jnp.log(l)

        _by_query_block(run, T)

    return _ride_call(
        body, (H, T // TB),
        [pl.BlockSpec((1, TB, QK), lambda h, i: (h, i, 0)), pl.BlockSpec((1, T, QK), lambda h, i: (h, 0, 0)),
         pl.BlockSpec((1, T, VD), lambda h, i: (h, 0, 0))],
        [pl.BlockSpec((1, TB, VD), lambda h, i: (h, i, 0)), pl.BlockSpec((1, TB, 1), lambda h, i: (h, i, 0))],
        [jax.ShapeDtypeStruct((H, T, VD), F32), jax.ShapeDtypeStruct((H, T, 1), F32)],
        Exchange(rode, modes) if rode else None, rode, name, (q, k, v))


def attn_bwd(q, k, v, o, lse, do, name, rode=None, modes=None):
    H, T, _ = q.shape

    def body(q_ref, k_ref, v_ref, o_ref, lse_ref, do_ref, dq_ref, dk_ref, dv_ref):
        i = pl.program_id(1)

        @pl.when(i == 0)
        def _():
            dk_ref[...] = jnp.zeros_like(dk_ref)
            dv_ref[...] = jnp.zeros_like(dv_ref)

        def run(nk):
            keys = pl.ds(0, nk)
            qv, kv, dov = q_ref[0], k_ref[0, keys, :], do_ref[0]
            p = jnp.exp(_dotf(qv, kv, "nt") - lse_ref[0])
            delta = jnp.sum(dov * o_ref[0], axis=1, keepdims=True)
            dob = dov.astype(BF16)
            dv_ref[0, keys, :] += _dotf(p.astype(BF16), dob, "tn")
            dp = _dotf(dob, v_ref[0, keys, :], "nt")
            ds = (p * (dp - delta)).astype(BF16)
            dq_ref[0] = jnp.dot(ds, kv, preferred_element_type=F32)
            dk_ref[0, keys, :] += _dotf(ds, qv, "tn")

        _by_query_block(run, T)

    blk = lambda c: pl.BlockSpec((1, TB, c), lambda h, i: (h, i, 0))
    full = lambda c: pl.BlockSpec((1, T, c), lambda h, i: (h, 0, 0))
    return _ride_call(
        body, (H, T // TB), [blk(QK), full(QK), full(VD), blk(VD), blk(1), blk(VD)], [blk(QK), full(QK), full(VD)],
        [jax.ShapeDtypeStruct((H, T, QK), F32), jax.ShapeDtypeStruct((H, T, QK), F32), jax.ShapeDtypeStruct((H, T, VD), F32)],
        Exchange(rode, modes) if rode else None, rode, name, (q, k, v, o, lse, do))


def disc_fwd(a_re, a_im, ls, name):
    def body(ar_ref, ai_ref, ls_ref, lr_ref, li_ref, fr_ref, fi_ref):
        ar, ai = ar_ref[...], ai_ref[...]
        dt = jnp.exp(ls_ref[...])
        mag = jnp.exp(ar * dt)
        lr = mag * jnp.cos(ai * dt)
        li = mag * jnp.sin(ai * dt)
        den = ar * ar + ai * ai
        nr = lr - 1.0
        lr_ref[...] = lr
        li_ref[...] = li
        fr_ref[...] = (nr * ar + li * ai) / den
        fi_ref[...] = (li * ar - nr * ai) / den

    return pl.pallas_call(body, out_shape=[jax.ShapeDtypeStruct(a_re.shape, F32)] * 4, name=name)(a_re, a_im, ls)


def disc_b(f_re, f_im, b_re, b_im, name):
    def body(fr_ref, fi_ref, br_ref, bi_ref, or_ref, oi_ref):
        fr, fi, br, bi = fr_ref[...], fi_ref[...], br_ref[...], bi_ref[...]
        or_ref[...] = fr * br - fi * bi
        oi_ref[...] = fr * bi + fi * br

    fs, bs = _disc_b_specs()
    return pl.pallas_call(body, grid=(2, G * P // DISC_ROWS), in_specs=[fs, fs, bs, bs], out_specs=[bs, bs],
                          out_shape=[jax.ShapeDtypeStruct(b_re.shape, F32)] * 2, name=name)(f_re, f_im, b_re, b_im)


DISC_ROWS = 1024


def _disc_b_specs():
    return (pl.BlockSpec((1, DISC_ROWS, 1), lambda d, i: (d, i, 0)), pl.BlockSpec((1, DISC_ROWS, CH), lambda d, i: (d, i, 0)))


def disc_b_bwd(f_re, f_im, b_re, b_im, dbb_re, dbb_im, name):
    def body(fr_ref, fi_ref, br_ref, bi_ref, dr_ref, di_ref, dbr_ref, dbi_ref, dfr_ref, dfi_ref):
        fr, fi, br, bi, dr, di = fr_ref[...], fi_ref[...], br_ref[...], bi_ref[...], dr_ref[...], di_ref[...]
        dbr_ref[...] = fr * dr + fi * di
        dbi_ref[...] = fr * di - fi * dr
        dfr_ref[...] = jnp.sum(dr * br + di * bi, axis=-1, keepdims=True)
        dfi_ref[...] = jnp.sum(di * br - dr * bi, axis=-1, keepdims=True)

    fs, bs = _disc_b_specs()
    return pl.pallas_call(body, grid=(2, G * P // DISC_ROWS), in_specs=[fs, fs, bs, bs, bs, bs], out_specs=[bs, bs, fs, fs],
                          out_shape=[jax.ShapeDtypeStruct(b_re.shape, F32)] * 2 + [jax.ShapeDtypeStruct(f_re.shape, F32)] * 2,
                          name=name)(f_re, f_im, b_re, b_im, dbb_re, dbb_im)


def disc_a_bwd(a_re, a_im, ls, dlr, dli, dfr, dfi, name):
    def body(ar_ref, ai_ref, ls_ref, dlr_ref, dli_ref, dfr_ref, dfi_ref, dar_ref, dai_ref, dls_ref):
        ar, ai = ar_ref[...], ai_ref[...]
        dt = jnp.exp(ls_ref[...])
        mag = jnp.exp(ar * dt)
        cs, sn = jnp.cos(ai * dt), jnp.sin(ai * dt)
        lr, li = mag * cs, mag * sn
        den = ar * ar + ai * ai
        nr = lr - 1.0
        f_re = (nr * ar + li * ai) / den
        f_im = (li * ar - nr * ai) / den
        dn1 = dfr_ref[...] / den
        dn2 = dfi_ref[...] / den
        dden = -(dfr_ref[...] * f_re + dfi_ref[...] * f_im) / den
        dlr_t = dlr_ref[...] + dn1 * ar - dn2 * ai
        dli_t = dli_ref[...] + dn1 * ai + dn2 * ar
        dar = dn1 * nr + dn2 * li + dden * 2.0 * ar
        dai = dn1 * li - dn2 * nr + dden * 2.0 * ai
        dmag = dlr_t * cs + dli_t * sn
        dth = dli_t * lr - dlr_t * li
        dar_ref[...] = dar + dmag * mag * dt
        dai_ref[...] = dai + dth * dt
        dls_ref[...] = jnp.sum(dmag * mag * ar + dth * ai, axis=-1, keepdims=True) * dt

    return pl.pallas_call(body, out_shape=[jax.ShapeDtypeStruct(a_re.shape, F32)] * 2 +
                          [jax.ShapeDtypeStruct(ls.shape, F32)], name=name)(a_re, a_im, ls, dlr, dli, dfr, dfi)


def _cpow(lr, li, n):
    rr, ri = None, None
    br, bi = lr, li
    while n:
        if n & 1:
            if rr is None:
                rr, ri = br, bi
            else:
                rr, ri = rr * br - ri * bi, rr * bi + ri * br
        n >>= 1
        if n:
            br, bi = br * br - bi * bi, 2.0 * br * bi
    return rr, ri


UNROLL = 4


def _seg_scan(xre, xim, lam8, pw, base, seglen, rev, init, fin_re, fin_im, ini_re, ini_im, prev=None):
    lr, li = lam8

    def rows(t):
        return pl.ds(pl.multiple_of(base + t * SEG, SEG), SEG)

    tmap = (lambda n: seglen - 1 - n) if rev else (lambda n: n)
    zero = jnp.zeros((SEG, SB), F32)

    def advance(c, t):
        a, b = c
        return lr * a - li * b + xre[rows(t), :], lr * b + li * a + xim[rows(t), :]

    fin = lax.fori_loop(0, seglen, lambda n, c: advance(c, tmap(n)), (zero, zero), unroll=UNROLL)
    fin_re[...] = fin[0]
    fin_im[...] = fin[1]
    (cr, ci), (pr, pi) = init, pw
    for i in (range(SEG - 1, -1, -1) if rev else range(SEG)):
        ini_re[pl.ds(i, 1), :] = cr
        ini_im[pl.ds(i, 1), :] = ci
        cr, ci = pr * cr - pi * ci + fin_re[pl.ds(i, 1), :], pr * ci + pi * cr + fin_im[pl.ds(i, 1), :]
    start = (ini_re[...], ini_im[...])

    def store(c, t):
        na, nb = advance(c, t)
        xre[rows(t), :] = na
        xim[rows(t), :] = nb
        return na, nb

    if prev is None:
        lax.fori_loop(0, seglen, lambda n, c: store(c, tmap(n)), start, unroll=UNROLL)
        return (cr, ci), None

    sre, sim, s_ini_re, s_ini_im = prev

    def acc_step(c, t, pre, pim):
        na, nb = store(c[:2], t)
        return na, nb, c[2] + na * pre + nb * pim, c[3] + nb * pre - na * pim

    def body(n, c):
        t = tmap(n)
        tp = t - 1 if rev else t + 1
        return acc_step(c, t, sre[rows(tp), :], sim[rows(tp), :])

    c = lax.fori_loop(0, seglen - 1, body, start + (zero, zero), unroll=UNROLL)
    c = acc_step(c, 0 if rev else seglen - 1, s_ini_re[...], s_ini_im[...])
    return (cr, ci), c[2:]


def _lam_tiles(lr, li, lens, conj=False):
    if conj:
        li = -li
    lam8 = (jnp.broadcast_to(lr, (SEG, SB)), jnp.broadcast_to(li, (SEG, SB)))
    return lam8, [_cpow(lr, li, n) for n in lens]


def _stretches(T):
    return ((0, LC // SEG), (LC, (T - LC) // SEG))


def _to_seg_order(src, dst, T):
    for base, seglen in _stretches(T):
        def body(t, carry, base=base, seglen=seglen):
            dst[pl.ds(pl.multiple_of(base + t * SEG, SEG), SEG), :] = src[pl.ds(base + t, SEG, stride=seglen), :]
            return carry
        lax.fori_loop(0, seglen, body, 0, unroll=8)


def _from_seg_order(src, dst, T):
    for base, seglen in _stretches(T):
        def body(t, carry, base=base, seglen=seglen):
            dst[pl.ds(base + t, SEG, stride=seglen), :] = src[pl.ds(pl.multiple_of(base + t * SEG, SEG), SEG), :]
            return carry
        lax.fori_loop(0, seglen, body, 0, unroll=8)


def _scan_specs(T):
    ublk = pl.BlockSpec((T, UB), lambda j: (0, j))
    lam = pl.BlockSpec((2, 1, 1, SB), lambda j: (0, j, 0, 0))
    mat = pl.BlockSpec((2, 1, UB, P), lambda j: (0, j, 0, 0))
    return ublk, lam, mat


def _dotf(a, b, mode="nn"):
    return lax.dot_general(a, b, _DN[mode], preferred_element_type=F32)


def _diag_mask():
    r = lax.broadcasted_iota(jnp.int32, (UB, SB), 0)
    c = lax.broadcasted_iota(jnp.int32, (UB, SB), 1)
    return lax.shift_right_logical(r, int(math.log2(CH))) == lax.shift_right_logical(c, int(math.log2(P)))


def _expand(m):
    p = lax.broadcasted_iota(jnp.int32, (P, SB), 0)
    c = lax.broadcasted_iota(jnp.int32, (P, SB), 1)
    tile = jnp.where(lax.bitwise_and(c, P - 1) == p, 1.0, 0.0).astype(BF16)
    wide = jnp.dot(m.astype(BF16), tile, preferred_element_type=F32)
    return jnp.where(_diag_mask(), wide, 0.0).astype(BF16)


def _collapse(full):
    c = lax.broadcasted_iota(jnp.int32, (SB, P), 0)
    p = lax.broadcasted_iota(jnp.int32, (SB, P), 1)
    pick = jnp.where(lax.bitwise_and(c, P - 1) == p, 1.0, 0.0).astype(BF16)
    return _exact_perm(jnp.where(_diag_mask(), full, 0.0), pick)


def _zero_state():
    return jnp.zeros((1, SB), F32), jnp.zeros((1, SB), F32)


def scan_fwd(u, lam_re, lam_im, bre, bim, cre, cim, name):
    T = u.shape[0]
    s_ctx, s_lat = LC // SEG, (T - LC) // SEG

    def body(u_ref, lr_ref, li_ref, bre_ref, bim_ref, cre_ref, cim_ref, y_ref, us, ys, sre, sim, fre, fim, ire, iim):
        _to_seg_order(u_ref, us, T)
        ub = us[...].astype(BF16)
        for d in range(2):
            lam8, (pw_c, pw_l) = _lam_tiles(lr_ref[d, 0], li_ref[d, 0], (s_ctx, s_lat))
            sre[...] = _dotf(ub, _expand(bre_ref[d, 0]))
            sim[...] = _dotf(ub, _expand(bim_ref[d, 0]))
            end_c, _ = _seg_scan(sre, sim, lam8, pw_c, 0, s_ctx, bool(d), _zero_state(), fre, fim, ire, iim)
            _seg_scan(sre, sim, lam8, pw_l, LC, s_lat, bool(d), end_c, fre, fim, ire, iim)
            y = (_dotf(sre[...].astype(BF16), _expand(cre_ref[d, 0]), "nt")
                 - _dotf(sim[...].astype(BF16), _expand(cim_ref[d, 0]), "nt"))
            if d == 0:
                ys[...] = y
            else:
                ys[...] += y
        _from_seg_order(ys, y_ref, T)

    ublk, lam, mat = _scan_specs(T)
    return pl.pallas_call(
        body, grid=(NJ,), in_specs=[ublk, lam, lam, mat, mat, mat, mat], out_specs=ublk,
        out_shape=jax.ShapeDtypeStruct((T, G * CH), F32),
        scratch_shapes=[pltpu.VMEM((T, UB), F32)] * 2 + [pltpu.VMEM((T, SB), F32)] * 2 + [pltpu.VMEM((SEG, SB), F32)] * 4,
        compiler_params=_cp(("arbitrary",)), name=name)(u, lam_re, lam_im, bre, bim, cre, cim)


def scan_bwd(u, dy, lam_re, lam_im, bre, bim, cre, cim, name):
    T = u.shape[0]
    s_ctx, s_lat = LC // SEG, (T - LC) // SEG

    def body(u_ref, dy_ref, lr_ref, li_ref, bre_ref, bim_ref, cre_ref, cim_ref,
             du_ref, dlr_ref, dli_ref, dbre_ref, dbim_ref, dcre_ref, dcim_ref,
             us, dys, dus, sre, sim, gre, gim, fre, fim, ic_re, ic_im, il_re, il_im, jre, jim):
        _to_seg_order(u_ref, us, T)
        _to_seg_order(dy_ref, dys, T)
        ub, dyb = us[...].astype(BF16), dys[...].astype(BF16)
        for d in range(2):
            rev = bool(d)
            lam8, (pw_c, pw_l) = _lam_tiles(lr_ref[d, 0], li_ref[d, 0], (s_ctx, s_lat))
            cam8, (cw_c, cw_l) = _lam_tiles(lr_ref[d, 0], li_ref[d, 0], (s_ctx, s_lat), conj=True)
            bre_v, bim_v = _expand(bre_ref[d, 0]), _expand(bim_ref[d, 0])
            sre[...] = _dotf(ub, bre_v)
            sim[...] = _dotf(ub, bim_v)
            end_c, _ = _seg_scan(sre, sim, lam8, pw_c, 0, s_ctx, rev, _zero_state(), fre, fim, ic_re, ic_im)
            _seg_scan(sre, sim, lam8, pw_l, LC, s_lat, rev, end_c, fre, fim, il_re, il_im)
            gre[...] = _dotf(dyb, _expand(cre_ref[d, 0]))
            gim[...] = -_dotf(dyb, _expand(cim_ref[d, 0]))
            end_g, acc_l = _seg_scan(gre, gim, cam8, cw_l, LC, s_lat, not rev, _zero_state(), fre, fim, jre, jim,
                                     prev=(sre, sim, il_re, il_im))
            _, acc_c = _seg_scan(gre, gim, cam8, cw_c, 0, s_ctx, not rev, end_g, fre, fim, jre, jim,
                                 prev=(sre, sim, ic_re, ic_im))
            dlr_ref[d, 0] = _sum0(acc_l[0] + acc_c[0])
            dli_ref[d, 0] = _sum0(acc_l[1] + acc_c[1])
            grb, gib = gre[...].astype(BF16), gim[...].astype(BF16)
            du = _dotf(grb, bre_v, "nt") + _dotf(gib, bim_v, "nt")
            if d == 0:
                dus[...] = du
            else:
                dus[...] += du
            dbre_ref[d, 0] = _collapse(_dotf(ub, grb, "tn"))
            dbim_ref[d, 0] = _collapse(_dotf(ub, gib, "tn"))
            dcre_ref[d, 0] = _collapse(_dotf(dyb, sre[...].astype(BF16), "tn"))
            dcim_ref[d, 0] = -_collapse(_dotf(dyb, sim[...].astype(BF16), "tn"))
        _from_seg_order(dus, du_ref, T)

    ublk, lam, mat = _scan_specs(T)
    lam_s = jax.ShapeDtypeStruct(lam_re.shape, F32)
    mat_s = jax.ShapeDtypeStruct(bre.shape, F32)
    return pl.pallas_call(
        body, grid=(NJ,), in_specs=[ublk, ublk, lam, lam, mat, mat, mat, mat],
        out_specs=[ublk, lam, lam, mat, mat, mat, mat],
        out_shape=[jax.ShapeDtypeStruct((T, G * CH), F32), lam_s, lam_s, mat_s, mat_s, mat_s, mat_s],
        scratch_shapes=[pltpu.VMEM((T, UB), F32)] * 3 + [pltpu.VMEM((T, SB), F32)] * 4 + [pltpu.VMEM((SEG, SB), F32)] * 8,
        compiler_params=_cp(("arbitrary",)), name=name)(u, dy, lam_re, lam_im, bre, bim, cre, cim)


class Exchange:
    def __init__(self, xs, modes):
        self.n = len(xs)
        self.modes = [modes] * self.n if isinstance(modes, (str, int)) else list(modes)
        self.out_shape = [jax.ShapeDtypeStruct(self._shape(x, md), x.dtype) for x, md in zip(xs, self.modes)]
        self.scratch = [pltpu.SemaphoreType.DMA((NDEV - 1, self.n)), pltpu.SemaphoreType.DMA((NDEV - 1, self.n)),
                        pltpu.SemaphoreType.DMA((self.n,))]
        self.specs = [pl.BlockSpec(memory_space=pl.ANY)] * self.n

    @staticmethod
    def _shape(x, mode):
        if mode == "gather":
            return (NDEV,) + tuple(x.shape)
        return tuple(x.shape) if mode == "lead" else (NDEV, x.shape[0], mode) + tuple(x.shape[2:])

    @staticmethod
    def _piece(x_ref, mode, dev):
        if mode == "gather":
            return x_ref
        return x_ref.at[dev] if mode == "lead" else x_ref.at[:, pl.ds(dev * mode, mode)]

    def _copies(self, x_refs, out_refs, sems):
        send_sems, recv_sems, local_sems = sems
        mx, my, mc = lax.axis_index("x"), lax.axis_index("y"), lax.axis_index("c")
        me = 4 * mx + 2 * my + mc
        local = [pltpu.make_async_copy(self._piece(x_ref, self.modes[a], me), out_ref.at[me], local_sems.at[a])
                 for a, (x_ref, out_ref) in enumerate(zip(x_refs, out_refs))]
        sends, recvs = [], []
        for k in range(1, NDEV):
            peer = (1 - mx if k & 4 else mx, 1 - my if k & 2 else my, 1 - mc if k & 1 else mc)
            pid = 4 * peer[0] + 2 * peer[1] + peer[2]
            for a, (x_ref, out_ref) in enumerate(zip(x_refs, out_refs)):
                src = self._piece(x_ref, self.modes[a], pid)
                sems_k = dict(send_sem=send_sems.at[k - 1, a], recv_sem=recv_sems.at[k - 1, a], device_id=peer,
                              device_id_type=MESH_T)
                sends.append(pltpu.make_async_remote_copy(src_ref=src, dst_ref=out_ref.at[me], **sems_k))
                recvs.append(pltpu.make_async_remote_copy(src_ref=src, dst_ref=out_ref.at[pid], **sems_k))
        return local, sends, recvs

    def start(self, x_refs, out_refs, sems):
        local, sends, _ = self._copies(x_refs, out_refs, sems)
        for cp in local + sends:
            cp.start()

    def finish(self, x_refs, out_refs, sems):
        local, sends, recvs = self._copies(x_refs, out_refs, sems)
        for cp in recvs:
            cp.wait_recv()
        for cp in sends:
            cp.wait_send()
        for cp in local:
            cp.wait()


def exchange(xs, modes, name):
    ex = Exchange(xs, modes)
    n = ex.n

    def body(*refs):
        ex.start(refs[:n], refs[n:2 * n], refs[2 * n:])
        ex.finish(refs[:n], refs[n:2 * n], refs[2 * n:])

    return pl.pallas_call(body, in_specs=ex.specs, out_specs=ex.specs, out_shape=ex.out_shape, scratch_shapes=ex.scratch,
                          compiler_params=pltpu.CompilerParams(has_side_effects=True), name=name)(*xs)


def _dot_f32(a, b, dn):
    return lax.dot_general(a, b, dn, preferred_element_type=F32, precision=lax.Precision.HIGHEST)


def ada_fwd(cg, c_ctx, ada_w, ada_b_loc, name):
    W = ada_w.shape[2]

    def body(cg_ref, cc_ref, w_ref, b_ref, o_ref):
        a = jnp.concatenate([_silu(cg_ref[...]), jnp.broadcast_to(_silu(cc_ref[...]), (NDEV, D))], axis=0)
        for i in range(2):
            o_ref[i] = _dot_f32(a, w_ref[i], _DN["nn"]) + b_ref[i]

    return pl.pallas_call(body, out_shape=jax.ShapeDtypeStruct((2, 2 * NDEV, W), F32),
                          compiler_params=_cp(), name=name)(cg, c_ctx, ada_w, ada_b_loc)


def ada_bwd(cg, c_ctx, ada_w, dm_loc, dm_all, name):
    W = ada_w.shape[2]

    def body(cg_ref, cc_ref, w_ref, dl_ref, da_ref, gw_ref, dcc_ref, gb_ref):
        a = jnp.concatenate([_silu(cg_ref[...]), jnp.broadcast_to(_silu(cc_ref[...]), (NDEV, D))], axis=0)
        dcc = jnp.zeros((1, D), F32)
        for i in range(2):
            dl = dl_ref[i]
            gw_ref[i] = _dot_f32(a, dl, _DN["tn"])
            dctx = jnp.sum(dl[NDEV:], axis=0, keepdims=True)
            dcc = dcc + _dot_f32(dctx, w_ref[i], _DN["nt"])
        dcc_ref[...] = dcc
        gb_ref[...] = jnp.sum(da_ref[...], axis=0)

    return pl.pallas_call(body, out_shape=[jax.ShapeDtypeStruct((2, D, W), F32), jax.ShapeDtypeStruct((1, D), F32),
                                           jax.ShapeDtypeStruct((2, 3 * D), F32)],
                          compiler_params=_cp(), name=name)(cg, c_ctx, ada_w, dm_loc, dm_all)


def cctx_finish(parts, c_ctx, name):
    def body(p_ref, cc_ref, o_ref):
        o_ref[...] = jnp.sum(p_ref[...], axis=0, keepdims=True) * _dsilu(cc_ref[...])

    return pl.pallas_call(body, out_shape=jax.ShapeDtypeStruct((1, D), F32), name=name)(parts, c_ctx)


def _adamw_update(g_ref, w_ref, m_ref, v_ref, go_ref, d_ref, mo_ref, vo_ref):
    g = g_ref[0].astype(F32)
    for s in range(1, g_ref.shape[0]):
        g = g + g_ref[s].astype(F32)
    mn = B1 * m_ref[...] + (1.0 - B1) * g
    vn = B2 * v_ref[...] + (1.0 - B2) * g * g
    go_ref[...] = g
    mo_ref[...] = mn
    vo_ref[...] = vn
    d_ref[...] = -LR * ((mn * (1.0 / (1.0 - B1 ** STEP))) / (jnp.sqrt(vn * (1.0 / (1.0 - B2 ** STEP))) + AEPS) + WD * w_ref[...])


def adamw(gstack, w, m, v, name, tr=256):
    n, R, C = gstack.shape
    tr = max(t for t in range(8, min(tr, R) + 1, 8) if R % t == 0)
    spec = pl.BlockSpec((tr, C), lambda i: (i, 0))
    return pl.pallas_call(_adamw_body(1), grid=(R // tr,),
                          in_specs=[pl.BlockSpec((n, tr, C), lambda i: (0, i, 0)), spec, spec, spec],
                          out_specs=[spec] * 4, out_shape=[jax.ShapeDtypeStruct((R, C), F32)] * 4,
                          compiler_params=_cp(("parallel",)), name=name)(gstack, w, m, v)


def _adamw_body(k):
    def body(*refs):
        for t in range(k):
            _adamw_update(*refs[4 * t:4 * t + 4], *refs[4 * k + 4 * t:4 * k + 4 * t + 4])
    return body


def adamw_multi(items, grid, name):
    k = len(items)
    ins, in_specs, out_specs, out_shape = [], [], [], []
    for g, g_spec, w, m, v, w_spec in items:
        ins += [g, w, m, v]
        in_specs += [g_spec, w_spec, w_spec, w_spec]
    for g, g_spec, w, m, v, w_spec in items:
        out_specs += [w_spec] * 4
        out_shape += [jax.ShapeDtypeStruct(w.shape, F32)] * 4
    res = pl.pallas_call(_adamw_body(k), grid=grid, in_specs=in_specs, out_specs=out_specs, out_shape=out_shape,
                         compiler_params=_cp(("arbitrary",) * len(grid)), name=name)(*ins)
    return [res[4 * t:4 * t + 4] for t in range(k)]


def _whole(a, grid_rank):
    zeros = (0,) * a.ndim
    return pl.BlockSpec(a.shape, lambda *idx: zeros)


def sum_slots(xs, name):
    def body(*refs):
        for x_ref, o_ref in zip(refs[:len(xs)], refs[len(xs):]):
            acc = x_ref[0]
            for s in range(1, NDEV):
                acc = acc + x_ref[s]
            o_ref[...] = acc

    return pl.pallas_call(body, out_shape=[jax.ShapeDtypeStruct(x.shape[1:], F32) for x in xs],
                          compiler_params=_cp(), name=name)(*xs)


def _col_shards(g):
    R, N = g.shape
    return g.reshape(R, NDEV, N // NDEV).transpose(1, 0, 2)


def _vec2(v):
    return jnp.broadcast_to(v.reshape(1, 1, -1), (2, 1, v.size))


SHARD_ROWS = {"mla_w_in": 192, "mla_w_uq": 192, "mla_w_ukv": 256, "s5_w_in": 256}


def _t_shard(wsh, rows):
    t = wsh[0].T.astype(BF16)
    return jnp.pad(t, ((0, rows - t.shape[0]), (0, 0)))


def _win_order():
    w = IN_W // NDEV
    perm = np.zeros((IN_WP, NDEV * SHARD_ROWS["mla_w_in"]), np.float32)
    first = QL + KVL + ROPE
    for c in range(IN_W):
        n = c + HEADS * VD if c < first else c - first
        perm[n, (c // w) * SHARD_ROWS["mla_w_in"] + c % w] = 1.0
    return jnp.asarray(perm, BF16)


def local_step(ctx, x, tgt, mod, Wt, small, l1_shards):
    T = LC + x.shape[0]
    xa = ("cat", ctx, x)
    sh = [mod[i, :, None, 0:D] for i in range(2)]
    sc = [mod[i, :, None, D:2 * D] for i in range(2)]
    gt = [mod[i, :, None, 2 * D:] for i in range(2)]
    ng = [_vec2(small["norm_g"][i]) for i in range(2)]
    qg, kvg = _vec2(small["mla_q_norm"]), _vec2(small["mla_kv_norm"])
    cosf, sinf, pm, pmt = _rope_tables(T)

    (h0, p0, cqn, ckvn), _ = rowwise(st_l0_pre, [xa], [ng[0], sc[0], sh[0], qg, kvg],
                                     [(D, BF16), (IN_WP, F32), (QL, BF16), (KVL, BF16)], [], "l0_pre", mats=[Wt["mla_w_in"]])
    z0, cq, ckv = (p0, 0, HEADS * VD), (p0, HEADS * VD // QL, QL), (p0, (HEADS * VD + QL) // KVL, KVL)
    Q = project_q(cqn, Wt["mla_w_uq"], cosf, sinf, pm, "l0_uq")
    K, V = project_kv(ckvn, Wt["mla_w_ukv"], p0, (HEADS * VD + QL + KVL) // 128, "l0_ukv")
    (o, lse), got = attn_fwd(Q, K, V, "l0_attn", rode=l1_shards, modes="gather")
    Wt, small = dict(Wt), dict(small)
    for n, a in zip(L1_BIG, got):
        Wt[n] = a.reshape(-1, a.shape[-1])
    vecs = lax.bitcast_convert_type(got[-1].reshape(NDEV, 2, -1, 2), F32)
    small["s5_d"], small["s5_b_glu"] = vecs[:, 0, :].reshape(D), vecs[:, 1, :].reshape(D)
    o2 = o.transpose(1, 0, 2).reshape(T, HEADS * VD)
    (og, out0, x1), _ = rowwise(st_l0_post, [o2, z0, xa], [gt[0]], [(D, BF16), (D, F32), (D, F32)], [], "l0_post",
                                mats=[Wt["mla_w_out"]])

    ls = small["s5_log_step"].reshape(2, G, 1)
    a_re, a_im = small["s5_a_re"].reshape(2, G, P), small["s5_a_im"].reshape(2, G, P)
    b_re, b_im = small["s5_b_re"].reshape(2, G * P, CH), small["s5_b_im"].reshape(2, G * P, CH)
    lam_re, lam_im, f_re, f_im = disc_fwd(a_re, a_im, ls, "s5_disc")
    f_re2, f_im2 = f_re.reshape(2, G * P, 1), f_im.reshape(2, G * P, 1)
    bb_re, bb_im = disc_b(f_re2, f_im2, b_re, b_im, "s5_disc_b")
    compact = lambda m: m.reshape(2, NJ, UB, P)
    bre = compact(bb_re.reshape(2, G, P, CH).transpose(0, 1, 3, 2))
    bim = compact(bb_im.reshape(2, G, P, CH).transpose(0, 1, 3, 2))
    cre, cim = compact(small["s5_c_re"]), compact(small["s5_c_im"])
    lam_re4, lam_im4 = lam_re.reshape(2, NJ, 1, SB), lam_im.reshape(2, NJ, 1, SB)

    (h1, p1), _ = rowwise(st_l1_pre, [x1], [ng[1], sc[1], sh[1]], [(D, BF16), (2 * D, F32)], [], "l1_pre", mats=[Wt["s5_w_in"]])
    u, z1 = (p1, 0, D), (p1, 1, D)
    yssm = scan_fwd(p1, lam_re4, lam_im4, bre, bim, cre, cim, "s5_scan")
    dvec, bglu = _vec2(small["s5_d"]), _vec2(small["s5_b_glu"])
    fg = _vec2(small["final_g"])
    lat_mask = jnp.stack([jnp.zeros((1, D), F32), jnp.ones((1, D), F32)])
    (y, y1b, gl, y3, out1, dx2), (dfg, lvec) = rowwise(
        st_l1_mlp, [yssm, u, z1, x1, ("lat", tgt)], [dvec, bglu, gt[1], fg, lat_mask],
        [(D, F32), (D, BF16), (D, F32), (D, BF16), (D, F32), (D, F32)], [D, 128], "l1_mlp",
        mats=[Wt["s5_w_glu"], Wt["s5_w_out"]])

    (dz1, dy, du_d), (dgt1, dbglu, dd), (g_w_out5, g_w_glu) = rowwise(
        st_l1_mlp_bwd, [dx2, out1, y3, y, gl, z1, u, y1b], [gt[1], bglu, dvec], [(D, BF16), (D, F32), (D, F32)], [D, D, D],
        "l1_mlp_b", mats=[Wt["s5_w_out"], Wt["s5_w_glu"]], out_accs=[(D, D), (D, D)])
    du_s, dlr, dli, dbre, dbim, dcre, dcim = scan_bwd(p1, dy, lam_re4, lam_im4, bre, bim, cre, cim, "s5_scan_b")
    dbb_re = dbre.reshape(2, G, CH, P).transpose(0, 1, 3, 2).reshape(2, G * P, CH)
    dbb_im = dbim.reshape(2, G, CH, P).transpose(0, 1, 3, 2).reshape(2, G * P, CH)
    g_c_re, g_c_im = dcre.reshape(2, G, CH, P), dcim.reshape(2, G, CH, P)
    g_b_re, g_b_im, dfr, dfi = disc_b_bwd(f_re2, f_im2, b_re, b_im, dbb_re, dbb_im, "s5_disc_b_b")
    g_a_re, g_a_im, g_ls = disc_a_bwd(a_re, a_im, ls, dlr.reshape(2, G, P), dli.reshape(2, G, P),
                                      dfr.reshape(2, G, P), dfi.reshape(2, G, P), "s5_disc_b_a")
    (dx1,), (dsh1, dsc1, dng1), (g_w_in5,) = rowwise(
        st_l1_tail_bwd, [du_d, du_s, dz1, h1, x1, dx2], [ng[1], sc[1]], [(D, F32)], [D, D, D], "l1_pre_b",
        mats=[Wt["s5_w_in"]], out_accs=[(D, 2 * D)])
    g_w_in5 = _col_shards(g_w_in5)

    (do2, dz0), (dgt0,), (g_w_out,) = rowwise(st_l0_post_bwd, [dx1, out0, og, o2, z0], [gt[0]], [(D, F32), (D, F32)], [D],
                                              "l0_post_b", mats=[Wt["mla_w_out"]], out_accs=[(D, D)])
    g_w_out = g_w_out.astype(BF16)
    doh = do2.reshape(T, HEADS, VD).transpose(1, 0, 2)
    rows8 = lambda g: g.reshape(NDEV, -1, g.shape[-1])
    both = lambda s: s[0, 0] + s[1, 0]
    dense = lambda g: g.reshape(2, G * P * CH // 128, 128)
    chunks = [dense(g_b_re), dense(g_b_im), g_c_re, g_c_im]
    l1_send = [g_w_in5, rows8(g_w_glu), rows8(g_w_out5), both(dd).reshape(NDEV, 1, -1), both(dbglu).reshape(NDEV, 1, -1)]
    (dQ, dK, dV), l1_recv = attn_bwd(Q, K, V, o, lse, doh, "l0_attn_b", rode=l1_send + chunks,
                                     modes=["lead"] * len(l1_send) + [a.shape[1] // NDEV for a in chunks])
    dqh = rope(dQ, cosf, sinf, pmt, True, BF16, "l0_rope_q_b", scale=SCALE)
    dq = dqh.transpose(1, 0, 2).reshape(T, HEADS * QK)
    dkv, dkr = split_kv_grads(dK, dV, "l0_kv_b")
    (grad_x,), (dqg, dkvg, dsh0, dsc0, dng0), (g_uq, g_ukv, g_p) = rowwise(
        st_l0_tail_bwd, [dq, dkv, dkr, dz0, cq, ckv, cqn, ckvn, h0, xa, dx1], [qg, kvg, ng[0], sc[0]],
        [(D, F32, "lat")], [QL, KVL, D, D, D], "l0_pre_b", mats=[Wt["mla_w_uq"], Wt["mla_w_ukv"], Wt["mla_w_in"]],
        out_accs=[(QL, HEADS * QK), (KVL, HEADS * KVW), (D, IN_WP)])
    g_w_uq, g_w_ukv = _col_shards(g_uq).astype(BF16), _col_shards(g_ukv).astype(BF16)
    g_w_in = _col_shards(jnp.concatenate([g_p[:, HEADS * VD:IN_W], g_p[:, :HEADS * VD]], axis=1)).astype(BF16)

    dmod = jnp.stack([jnp.concatenate([dsh0, dsc0, dgt0], axis=-1)[:, 0], jnp.concatenate([dsh1, dsc1, dgt1], axis=-1)[:, 0]])
    gbig = {"mla_w_in": g_w_in, "mla_w_uq": g_w_uq, "mla_w_ukv": g_w_ukv, "mla_w_out": rows8(g_w_out)}
    gsmall = {"norm_g": jnp.stack([both(dng0), both(dng1)]), "mla_q_norm": both(dqg), "mla_kv_norm": both(dkvg),
              "s5_a_re": g_a_re, "s5_a_im": g_a_im, "s5_log_step": g_ls, "final_g": dfg[1, 0]}
    return lvec[1], grad_x, dmod, gbig, gsmall, l1_recv


COL_SHARDED = ("mla_w_in", "mla_w_uq", "mla_w_ukv", "s5_w_in")
ROW_SHARDED = ("mla_w_out", "s5_w_glu", "s5_w_out")
VEC_SHARDED = ("s5_d", "s5_b_glu")
BIG = COL_SHARDED + ROW_SHARDED
L0_BIG = ("mla_w_in", "mla_w_uq", "mla_w_ukv", "mla_w_out")
L1_BIG = ("s5_w_in", "s5_w_glu", "s5_w_out")
BITS16 = jnp.bfloat16
SMALL_RS = ("norm_g", "mla_q_norm", "mla_kv_norm", "s5_a_re", "s5_a_im", "s5_log_step", "s5_b_re", "s5_b_im",
            "s5_c_re", "s5_c_im", "final_g")
CHUNKED = ("s5_b_re", "s5_b_im", "s5_c_re", "s5_c_im")
DENSE = ("s5_b_re", "s5_b_im")
TINY = ("norm_g", "mla_q_norm", "mla_kv_norm", "s5_a_re", "s5_a_im", "s5_log_step", "final_g")
ORDER = ("c_ctx", "ada_w", "ada_b", "norm_g", "mla_w_in", "mla_q_norm", "mla_w_uq", "mla_kv_norm", "mla_w_ukv",
         "mla_w_out", "s5_w_in", "s5_a_re", "s5_a_im", "s5_log_step", "s5_b_re", "s5_b_im", "s5_c_re", "s5_c_im",
         "s5_d", "s5_w_glu", "s5_b_glu", "s5_w_out", "final_g")


def kernel(x, c, ctx, c_ctx, ada_w, ada_b, norm_g, mla_w_in, mla_q_norm, mla_w_uq, mla_kv_norm, mla_w_ukv, mla_w_out, s5_w_in, s5_a_re, s5_a_im, s5_log_step, s5_b_re, s5_b_im, s5_c_re, s5_c_im, s5_d, s5_w_glu, s5_b_glu, s5_w_out, final_g, loss_target, m_c_ctx, m_ada_w, m_ada_b, m_norm_g, m_mla_w_in, m_mla_q_norm, m_mla_w_uq, m_mla_kv_norm, m_mla_w_ukv, m_mla_w_out, m_s5_w_in, m_s5_a_re, m_s5_a_im, m_s5_log_step, m_s5_b_re, m_s5_b_im, m_s5_c_re, m_s5_c_im, m_s5_d, m_s5_w_glu, m_s5_b_glu, m_s5_w_out, m_final_g, v_c_ctx, v_ada_w, v_ada_b, v_norm_g, v_mla_w_in, v_mla_q_norm, v_mla_w_uq, v_mla_kv_norm, v_mla_w_ukv, v_mla_w_out, v_s5_w_in, v_s5_a_re, v_s5_a_im, v_s5_log_step, v_s5_b_re, v_s5_b_im, v_s5_c_re, v_s5_c_im, v_s5_d, v_s5_w_glu, v_s5_b_glu, v_s5_w_out, v_final_g):
    w = dict(c_ctx=c_ctx, ada_w=ada_w, ada_b=ada_b, norm_g=norm_g, mla_w_in=mla_w_in, mla_q_norm=mla_q_norm,
             mla_w_uq=mla_w_uq, mla_kv_norm=mla_kv_norm, mla_w_ukv=mla_w_ukv, mla_w_out=mla_w_out, s5_w_in=s5_w_in,
             s5_a_re=s5_a_re, s5_a_im=s5_a_im, s5_log_step=s5_log_step, s5_b_re=s5_b_re, s5_b_im=s5_b_im,
             s5_c_re=s5_c_re, s5_c_im=s5_c_im, s5_d=s5_d, s5_w_glu=s5_w_glu, s5_b_glu=s5_b_glu, s5_w_out=s5_w_out,
             final_g=final_g)
    m = dict(c_ctx=m_c_ctx, ada_w=m_ada_w, ada_b=m_ada_b, norm_g=m_norm_g, mla_w_in=m_mla_w_in, mla_q_norm=m_mla_q_norm,
             mla_w_uq=m_mla_w_uq, mla_kv_norm=m_mla_kv_norm, mla_w_ukv=m_mla_w_ukv, mla_w_out=m_mla_w_out,
             s5_w_in=m_s5_w_in, s5_a_re=m_s5_a_re, s5_a_im=m_s5_a_im, s5_log_step=m_s5_log_step, s5_b_re=m_s5_b_re,
             s5_b_im=m_s5_b_im, s5_c_re=m_s5_c_re, s5_c_im=m_s5_c_im, s5_d=m_s5_d, s5_w_glu=m_s5_w_glu,
             s5_b_glu=m_s5_b_glu, s5_w_out=m_s5_w_out, final_g=m_final_g)
    v = dict(c_ctx=v_c_ctx, ada_w=v_ada_w, ada_b=v_ada_b, norm_g=v_norm_g, mla_w_in=v_mla_w_in, mla_q_norm=v_mla_q_norm,
             mla_w_uq=v_mla_w_uq, mla_kv_norm=v_mla_kv_norm, mla_w_ukv=v_mla_w_ukv, mla_w_out=v_mla_w_out,
             s5_w_in=v_s5_w_in, s5_a_re=v_s5_a_re, s5_a_im=v_s5_a_im, s5_log_step=v_s5_log_step, s5_b_re=v_s5_b_re,
             s5_b_im=v_s5_b_im, s5_c_re=v_s5_c_re, s5_c_im=v_s5_c_im, s5_d=v_s5_d, s5_w_glu=v_s5_w_glu,
             s5_b_glu=v_s5_b_glu, s5_w_out=v_s5_w_out, final_g=v_final_g)

    me = 4 * lax.axis_index("x") + 2 * lax.axis_index("y") + lax.axis_index("c")
    WA = ada_w.shape[2]

    def shard(n):
        return _t_shard(w[n], SHARD_ROWS[n]) if n in COL_SHARDED else w[n][0].astype(BF16)

    wgot = exchange([c] + [shard(n) for n in L0_BIG], "gather", "gather_w")

    cg = wgot[0].reshape(NDEV, D)
    cc2 = c_ctx.reshape(1, D)
    ada_b_loc = lax.dynamic_slice_in_dim(ada_b.reshape(2, 3 * D // WA, WA), me, 1, axis=1)
    part = ada_fwd(cg, cc2, ada_w, ada_b_loc, "ada_fwd")
    pg = exchange([part], "gather", "gather_mod")[0]
    mod_l = lax.dynamic_index_in_dim(pg, me, axis=2, keepdims=False).transpose(1, 0, 2).reshape(2, 3 * D)
    mod_c = pg[:, :, NDEV, :].transpose(1, 0, 2).reshape(2, 3 * D)
    mod = jnp.stack([mod_c, mod_l], axis=1)

    Wt = {n: a.reshape(-1, a.shape[-1]) for n, a in zip(L0_BIG, wgot[1:])}
    Wt["mla_w_in"] = mm(_win_order(), Wt["mla_w_in"], "nn", "w_in_order", out_dtype=BF16)
    vec_bits = lax.bitcast_convert_type(jnp.concatenate([s5_d, s5_b_glu], axis=0), BITS16).reshape(2, -1)
    small = {n: w[n] for n in SMALL_RS}

    lvec, grad_x, dmod, gbig, gsmall, l1_recv = local_step(ctx[0], x[0], loss_target[0], mod, Wt, small,
                                                           [shard(n) for n in L1_BIG] + [vec_bits])
    loss = lax.psum(lvec[0, 0], ("x", "y", "c"))
    grad_x = grad_x[None]

    per_dev = G // NDEV
    recv = dict(zip(L0_BIG, exchange([gbig[n] for n in L0_BIG], "lead", "scatter_grads")))
    recv.update(dict(zip(L1_BIG + VEC_SHARDED, l1_recv)))
    out = {}

    def keep(n, res):
        for key, arr in zip("gdmv", res):
            out[key, n] = arr.reshape(w[n].shape)

    for n in BIG:
        keep(n, adamw(recv[n], w[n][0], m[n][0], v[n][0], "adamw_" + n))
    reduced = sum_slots(l1_recv[len(L1_BIG + VEC_SHARDED):], "sum_chunks")

    kshape = lambda n: w[n].shape if w[n].ndim > 1 else (1, w[n].size)
    got = exchange(list(reduced) + [gsmall[n].reshape(kshape(n)) for n in TINY] + [dmod], "gather", "gather_small")
    chunk_all, tiny_all, dm_all = got[:len(CHUNKED)], got[len(CHUNKED):-1], got[-1]

    dm_cols = lax.dynamic_slice_in_dim(dm_all.reshape(NDEV, 2, 2, 3 * D // WA, WA), me, 1, axis=3)[:, :, :, 0]
    dm_loc = jnp.concatenate([dm_cols[:, :, 1].transpose(1, 0, 2), dm_cols[:, :, 0].transpose(1, 0, 2)], axis=1)
    g_ada_w, dcc_part, g_ada_b = ada_bwd(cg, cc2, ada_w, dm_loc, dm_all.transpose(0, 2, 1, 3).reshape(2 * NDEV, 2, 3 * D), "ada_bwd")
    dcc_all = exchange([dcc_part], "gather", "gather_dcc")[0].reshape(NDEV, D)
    g_c_ctx = cctx_finish(dcc_all, cc2, "cctx_finish")

    flat2 = lambda t: t.reshape(-1, t.shape[-1])
    keep("ada_w", adamw(flat2(g_ada_w)[None], flat2(ada_w), flat2(m_ada_w), flat2(v_ada_w), "adamw_ada"))
    items = []
    for n, g in zip(CHUNKED, chunk_all):
        blk = (1, 1, per_dev) + w[n].shape[3:]
        if n in DENSE:
            g = g.transpose(1, 0, 2, 3).reshape(w[n].shape)
            g_spec = pl.BlockSpec((1, 1, 1) + blk[2:], lambda d, s: (0, 0, d, s, 0, 0))
        else:
            g_spec = pl.BlockSpec((1, 1, 1) + blk[2:], lambda d, s: (0, s, d, 0, 0, 0))
        items.append((g[None], g_spec, w[n], m[n], v[n], pl.BlockSpec(blk, lambda d, s: (0, d, s, 0, 0))))
    for n, res in zip(CHUNKED, adamw_multi(items, (2, NDEV), "adamw_bc")):
        keep(n, res)
    tiny_g = dict(zip(TINY, tiny_all))
    tiny_g.update({n: recv[n] for n in VEC_SHARDED})
    tiny_g["c_ctx"], tiny_g["ada_b"] = g_c_ctx[None], g_ada_b[None]
    names = list(tiny_g)
    items = [(tiny_g[n], _whole(tiny_g[n], 1)) + tuple(t[n].reshape(kshape(n)) for t in (w, m, v))
             + (pl.BlockSpec(kshape(n), lambda i, r=len(kshape(n)): (0,) * r),) for n in names]
    for n, res in zip(names, adamw_multi(items, (1,), "adamw_small")):
        keep(n, res)

    return (loss, grad_x, *[out["g", n] for n in ORDER], *[out["d", n] for n in ORDER],
            *[out["m", n] for n in ORDER], *[out["v", n] for n in ORDER])
```

```python
import math

import numpy as np
import jax
import jax.numpy as jnp
from jax import lax
from jax.experimental import pallas as pl
from jax.experimental.pallas import tpu as pltpu

F32 = jnp.float32
BF16 = jnp.bfloat16

D = 1024
L = 2048
LC = 256
NDEV = 8
GRID_W = 64
EPS = 1e-6
HEADS = 16
NOPE = 64
ROPE = 32
QK = NOPE + ROPE
VD = 64
IN_W = 256 + 128 + ROPE + HEADS * 64
IN_WP = 1536
QL = 256
KVL = 128
SCALE = QK ** -0.5
THETA = 10000.0
G = 64
P = 64
CH = 16
GB = 8
NJ = G // GB
UB = GB * CH
SB = GB * P
SEG = 8
TB = 256
VMEM_LIMIT = 56 * 1024 * 1024
B1, B2, LR, AEPS, WD, STEP = 0.9, 0.999, 0.001, 1e-8, 0.01, 10
MESH_T = pl.DeviceIdType.MESH


def _cp(sem=None):
    return pltpu.CompilerParams(dimension_semantics=sem, vmem_limit_bytes=VMEM_LIMIT)


def _sig(x):
    return 1.0 / (1.0 + jnp.exp(-x))


def _silu(x):
    return x * _sig(x)


def _dsilu(x):
    s = _sig(x)
    return s * (1.0 + x * (1.0 - s))


_GK = math.sqrt(2.0 / math.pi)


def _gelu(x):
    return 0.5 * x * (1.0 + jnp.tanh(_GK * (x + 0.044715 * x * x * x)))


def _dgelu(x):
    t = jnp.tanh(_GK * (x + 0.044715 * x * x * x))
    return 0.5 * (1.0 + t) + 0.5 * x * (1.0 - t * t) * _GK * (1.0 + 3 * 0.044715 * x * x)


def _rs(x):
    return lax.rsqrt(jnp.mean(x * x, axis=-1, keepdims=True) + EPS)


def _sum0(x):
    return jnp.sum(x, axis=0, keepdims=True)


def st_norm_mod(x, g, sc, sh):
    y = x * _rs(x) * g
    return (y * (1.0 + sc) + sh,), ()


def st_norm_mod_bwd(x, dh, dres, g, sc):
    r = _rs(x)
    xn = x * r
    y = xn * g
    dy = dh * (1.0 + sc)
    dxn = dy * g
    dx = r * (dxn - xn * jnp.mean(dxn * xn, axis=-1, keepdims=True))
    return (dres + dx,), (_sum0(dh), _sum0(dh * y), _sum0(dy * xn))


def st_rms(x, g):
    return (x * _rs(x) * g,), ()


def st_rms_bwd(x, dy, g):
    r = _rs(x)
    n = x * r
    dn = dy * g
    dx = r * (dn - n * jnp.mean(dn * n, axis=-1, keepdims=True))
    return (dx,), (_sum0(dy * n),)


def st_rms2(x1, x2, g1, g2):
    return st_rms(x1, g1)[0] + st_rms(x2, g2)[0], ()


def st_rms2_bwd(x1, dy1, x2, dy2, g1, g2):
    (d1,), (s1,) = st_rms_bwd(x1, dy1, g1)
    (d2,), (s2,) = st_rms_bwd(x2, dy2, g2)
    return (d1, d2), (s1, s2)


def st_gate(o, z):
    return (o * _silu(z),), ()


def st_gate_bwd(dog, o, z):
    return (dog * _silu(z), dog * o * _dsilu(z)), ()


def st_resid(x, out, gt):
    return (x + gt * out,), ()


def st_resid_bwd(dx, out, gt):
    return (dx * gt,), (_sum0(dx * out),)


def st_s5a(yssm, u, d):
    y = yssm + d * u
    return (y, _gelu(y)), ()


def st_s5b(y, gl, z, b):
    return (_gelu(y) * _sig(gl + b) * _silu(z),), ()


def st_s5b_bwd(dy3, y, gl, z, b):
    y1 = _gelu(y)
    s = _sig(gl + b)
    dy2 = dy3 * _silu(z)
    dz = dy3 * y1 * s * _dsilu(z)
    dgl = dy2 * y1 * s * (1.0 - s)
    return (dgl, dz, dy2 * s), (_sum0(dgl),)


def st_s5a_bwd(dy1a, dy1b, y, u, d):
    dy = (dy1a + dy1b) * _dgelu(y)
    return (dy, dy * d), (_sum0(dy * u),)


def st_l0_pre(x, g, sc, sh, qg, kvg, w_in):
    hb = st_norm_mod(x, g, sc, sh)[0][0].astype(BF16)
    p = lax.dot_general(hb, w_in, _DN["nt"], preferred_element_type=F32)
    cq, ckv = p[:, HEADS * VD:HEADS * VD + QL], p[:, HEADS * VD + QL:HEADS * VD + QL + KVL]
    return (hb, p) + st_rms2(cq, ckv, qg, kvg)[0], ()


def st_l0_tail_bwd(dq, dkv, dkr, dz, cq, ckv, cqn, ckvn, h, x, dres, qg, kvg, g, sc, w_uq, w_ukv, w_in):
    dcqn = jnp.dot(dq, w_uq, preferred_element_type=F32)
    dckvn = jnp.dot(dkv, w_ukv, preferred_element_type=F32)
    (dcq, dckv), (dqg, dkvg) = st_rms2_bwd(cq, dcqn, ckv, dckvn, qg, kvg)
    dp = jnp.concatenate([dz, dcq, dckv, dkr], axis=1).astype(BF16)
    dh = jnp.dot(dp, w_in, preferred_element_type=F32)
    outs, sums = st_norm_mod_bwd(x, dh, dres, g, sc)
    tn = lambda a, b: lax.dot_general(a, b, _DN["tn"], preferred_element_type=F32)
    return outs, (dqg, dkvg) + sums, (tn(cqn, dq), tn(ckvn, dkv), tn(h, dp))


def st_l1_pre(x, g, sc, sh, w_in):
    hb = st_norm_mod(x, g, sc, sh)[0][0].astype(BF16)
    return (hb, lax.dot_general(hb, w_in, _DN["nt"], preferred_element_type=F32)), ()


def st_l1_tail_bwd(du_a, du_b, dz, h, x, dres, g, sc, w_in):
    dp = jnp.concatenate([(du_a + du_b).astype(BF16), dz], axis=1)
    dh = jnp.dot(dp, w_in, preferred_element_type=F32)
    outs, sums = st_norm_mod_bwd(x, dh, dres, g, sc)
    return outs, sums, (lax.dot_general(h, dp, _DN["tn"], preferred_element_type=F32),)


def st_l0_post(o, z, x, gt, w_out):
    og = (o * _silu(z)).astype(BF16)
    out = jnp.dot(og, w_out, preferred_element_type=F32)
    return (og, out, x + gt * out), ()


def st_l0_post_bwd(dx1, out, og, o, z, gt, w_out):
    (dout,), (dgt,) = st_resid_bwd(dx1, out, gt)
    doutb = dout.astype(BF16)
    dog = lax.dot_general(doutb, w_out, _DN["nt"], preferred_element_type=F32)
    return st_gate_bwd(dog, o, z)[0], (dgt,), (lax.dot_general(og, doutb, _DN["tn"], preferred_element_type=F32),)


def st_l1_mlp(yssm, u, z, x1, tgt, d, bglu, gt, fg, mask, w_glu, w_out):
    (y, y1), _ = st_s5a(yssm, u, d)
    y1b = y1.astype(BF16)
    gl = jnp.dot(y1b, w_glu, preferred_element_type=F32)
    y3 = (y1 * _sig(gl + bglu) * _silu(z)).astype(BF16)
    out = jnp.dot(y3, w_out, preferred_element_type=F32)
    (dx2,), sums = st_final(x1 + gt * out, tgt, fg, mask)
    return (y, y1b, gl, y3, out, dx2), sums


def st_l1_mlp_bwd(dx2, out, y3, y, gl, z, u, y1b, gt, bglu, d, w_out, w_glu):
    (dout,), (dgt,) = st_resid_bwd(dx2, out, gt)
    doutb = dout.astype(BF16)
    dy3 = lax.dot_general(doutb, w_out, _DN["nt"], preferred_element_type=F32)
    (dgl, dz, dy1a), (dbglu,) = st_s5b_bwd(dy3, y, gl, z, bglu)
    dglb = dgl.astype(BF16)
    dy1b = lax.dot_general(dglb, w_glu, _DN["nt"], preferred_element_type=F32)
    (dy, du), (dd,) = st_s5a_bwd(dy1a, dy1b, y, u, d)
    g_w_out = lax.dot_general(y3, doutb, _DN["tn"], preferred_element_type=F32)
    g_w_glu = lax.dot_general(y1b, dglb, _DN["tn"], preferred_element_type=F32)
    return (dz, dy, du), (dgt, dbglu, dd), (g_w_out, g_w_glu)


def st_final(x2, tgt, g, mask):
    r = _rs(x2)
    n = x2 * r
    e = n * g - tgt
    dyo = e * (1.0 / D)
    dn = dyo * g
    dx = r * (dn - n * jnp.mean(dn * n, axis=-1, keepdims=True))
    lsum = jnp.sum(_sum0(e * e), axis=1, keepdims=True) * (0.5 / D)
    return (dx * mask,), (_sum0(dyo * n), jnp.broadcast_to(lsum, (1, 128)))


def rowwise(fn, rows, vecs, out_rows, out_sums, name, mats=(), out_accs=()):
    lat_blk = lambda i: jnp.maximum(i - 1, 0)
    arrays, in_specs, pick = [], [], []
    for a in rows:
        if not isinstance(a, tuple):
            a = (a, 0, a.shape[1])
        tag = a[0] if isinstance(a[0], str) else None
        if tag == "cat":
            _, ctx, x = a
            arrays += [ctx, x]
            in_specs += [pl.BlockSpec((TB, ctx.shape[1]), lambda i: (0, 0)),
                         pl.BlockSpec((TB, x.shape[1]), lambda i: (lat_blk(i), 0))]
            pick.append(2)
        elif tag == "lat":
            arrays.append(a[1])
            in_specs.append(pl.BlockSpec((TB, a[1].shape[1]), lambda i: (lat_blk(i), 0)))
            pick.append(1)
        else:
            arr, cb, width = a
            arrays.append(arr)
            in_specs.append(pl.BlockSpec((TB, width), lambda i, cb=cb: (i, cb)))
            pick.append(1)
    T = LC + L
    nin, nv, nm, no, ns = len(arrays), len(vecs), len(mats), len(out_rows), len(out_sums)

    def body(*refs):
        i = pl.program_id(0)
        vals, k = [], 0
        for p in pick:
            if p == 2:
                vals.append(jnp.where(i == 0, refs[k][...], refs[k + 1][...]))
            else:
                vals.append(refs[k][...])
            k += p
        vals += [r[0] for r in refs[nin:nin + nv]] + [r[...] for r in refs[nin + nv:nin + nv + nm]]
        res = fn(*vals)
        first_out = nin + nv + nm
        for r, o in zip(refs[first_out:first_out + no], res[0]):
            r[...] = o.astype(r.dtype)
        sum_refs = refs[first_out + no:first_out + no + ns]
        if sum_refs:
            @pl.when(i <= 1)
            def _():
                for r in sum_refs:
                    r[...] = jnp.zeros_like(r)
            for r, s in zip(sum_refs, res[1]):
                r[0] += s
        acc_refs = refs[first_out + no + ns:]
        if acc_refs:
            @pl.when(i == 0)
            def _():
                for r in acc_refs:
                    r[...] = jnp.zeros_like(r)
            for r, a in zip(acc_refs, res[2]):
                r[...] += a

    kind = lambda i: (jnp.minimum(i, 1), 0, 0)
    in_specs += [pl.BlockSpec((1, 1, v.shape[2]), kind) for v in vecs]
    in_specs += [pl.BlockSpec(m.shape, lambda i: (0, 0), pipeline_mode=pl.Buffered(1)) for m in mats]
    out_specs, out_shape = [], []
    for o in out_rows:
        lat = len(o) == 3
        out_specs.append(pl.BlockSpec((TB, o[0]), (lambda i: (lat_blk(i), 0)) if lat else (lambda i: (i, 0))))
        out_shape.append(jax.ShapeDtypeStruct((L if lat else T, o[0]), o[1]))
    out_specs += [pl.BlockSpec((1, 1, c), kind) for c in out_sums]
    out_shape += [jax.ShapeDtypeStruct((2, 1, c), F32) for c in out_sums]
    out_specs += [pl.BlockSpec(s, lambda i: (0, 0)) for s in out_accs]
    out_shape += [jax.ShapeDtypeStruct(s, F32) for s in out_accs]
    res = pl.pallas_call(body, grid=(T // TB,), in_specs=in_specs, out_specs=out_specs, out_shape=out_shape,
                         compiler_params=_cp(("arbitrary",)), name=name)(*arrays, *vecs, *mats)
    if out_accs:
        return res[:no], res[no:no + ns], res[no + ns:]
    return res[:no], res[no:]


_DN = {"nn": (((1,), (0,)), ((), ())), "nt": (((1,), (1,)), ((), ())), "tn": (((0,), (0,)), ((), ()))}


def mm(a, b, mode, name, out_dtype=F32, tm=None, tn=None, shard_out=False):
    if mode == "nn":
        (M, K), (_, N) = a.shape, b.shape
    elif mode == "nt":
        (M, K), (N, _) = a.shape, b.shape
    else:
        (K, M), (_, N) = a.shape, b.shape
    if tm is None:
        tm = next((t for t in (768, 512, 256) if M % t == 0 and M > t), M)
    tn = N if tn is None else tn
    dn = _DN[mode]

    def body(a_ref, b_ref, o_ref):
        o_ref[...] = lax.dot_general(a_ref[...].astype(BF16), b_ref[...].astype(BF16), dn,
                                     preferred_element_type=F32).astype(o_ref.dtype)

    if shard_out:
        def body(a_ref, b_ref, o_ref):
            av = a_ref[...].astype(BF16)
            for j in range(N // tn):
                bj = b_ref[pl.ds(j * tn, tn), :] if mode == "nt" else b_ref[:, pl.ds(j * tn, tn)]
                o_ref[j] = lax.dot_general(av, bj.astype(BF16), dn, preferred_element_type=F32).astype(o_ref.dtype)

        a_spec = pl.BlockSpec((K, tm), lambda i: (0, i)) if mode == "tn" else pl.BlockSpec((tm, K), lambda i: (i, 0))
        return pl.pallas_call(body, grid=(M // tm,), in_specs=[a_spec, pl.BlockSpec(b.shape, lambda i: (0, 0))],
                              out_specs=pl.BlockSpec((N // tn, tm, tn), lambda i: (0, i, 0)),
                              out_shape=jax.ShapeDtypeStruct((N // tn, M, tn), out_dtype),
                              compiler_params=_cp(("parallel",)), name=name)(a, b)
    a_spec = pl.BlockSpec((K, tm), lambda i, j: (0, i)) if mode == "tn" else pl.BlockSpec((tm, K), lambda i, j: (i, 0))
    b_spec = pl.BlockSpec((tn, K), lambda i, j: (j, 0)) if mode == "nt" else pl.BlockSpec((K, tn), lambda i, j: (0, j))
    return pl.pallas_call(body, grid=(M // tm, N // tn), in_specs=[a_spec, b_spec],
                          out_specs=pl.BlockSpec((tm, tn), lambda i, j: (i, j)), out_shape=jax.ShapeDtypeStruct((M, N), out_dtype),
                          compiler_params=_cp(("parallel", "arbitrary")), name=name)(a, b)


def _rope_tables(T, width=QK, first=NOPE):
    nlat = T - LC
    pos = np.arange(nlat)
    row, col = pos // GRID_W, pos % GRID_W
    half = ROPE // 2
    inv = 1.0 / (THETA ** (np.arange(0, half, 2, dtype=np.float64) / half))
    cosf = np.ones((T, width), np.float64)
    sinf = np.zeros((T, width), np.float64)
    perm = np.zeros((width, width), np.float32)
    for m in range(ROPE):
        j = first + m
        blk, w = m // half, m % half
        ang = (row if blk == 0 else col)[:, None] * inv[None, :]
        f = w % (half // 2)
        cosf[LC:, j] = np.cos(ang[:, f])
        if w < half // 2:
            sinf[LC:, j] = -np.sin(ang[:, f])
            perm[j + half // 2, j] = 1.0
        else:
            sinf[LC:, j] = np.sin(ang[:, f])
            perm[j - half // 2, j] = 1.0
    return jnp.asarray(cosf, F32), jnp.asarray(sinf, F32), jnp.asarray(perm, BF16), jnp.asarray(perm.T, BF16)


def _exact_perm(x, pm):
    hi = x.astype(BF16)
    r1 = x - hi.astype(F32)
    mid = r1.astype(BF16)
    lo = (r1 - mid.astype(F32)).astype(BF16)
    dot = lambda a: jnp.dot(a, pm, preferred_element_type=F32)
    return dot(hi) + dot(mid) + dot(lo)


def _rot(x, cv, sv, pv, inverse):
    if inverse:
        return x * cv + _exact_perm(x * sv, pv)
    return x * cv + _exact_perm(x, pv) * sv


def rope(x, cosf, sinf, pm, inverse, out_dtype, name, scale=1.0):
    H, T, _ = x.shape

    def body(x_ref, c_ref, s_ref, p_ref, o_ref):
        cv, sv, pv = c_ref[...], s_ref[...], p_ref[...]
        for h in range(H):
            o_ref[h] = (_rot(x_ref[h], cv, sv, pv, inverse) * scale).astype(o_ref.dtype)

    return pl.pallas_call(
        body, grid=(T // TB,),
        in_specs=[pl.BlockSpec((H, TB, QK), lambda i: (0, i, 0)), pl.BlockSpec((TB, QK), lambda i: (i, 0)),
                  pl.BlockSpec((TB, QK), lambda i: (i, 0)), pl.BlockSpec((QK, QK), lambda i: (0, 0))],
        out_specs=pl.BlockSpec((H, TB, QK), lambda i: (0, i, 0)), out_shape=jax.ShapeDtypeStruct((H, T, QK), out_dtype),
        compiler_params=_cp(("parallel",)), name=name)(x, cosf, sinf, pm)


KVW = NOPE + VD


def _kv_selectors():
    s_kn = np.zeros((KVW, QK), np.float32)
    s_kr = np.zeros((128, QK), np.float32)
    s_v = np.zeros((KVW, VD), np.float32)
    for l in range(NOPE):
        s_kn[l, l] = 1.0
    for l in range(ROPE):
        s_kr[l, NOPE + l] = 1.0
    for l in range(VD):
        s_v[NOPE + l, l] = 1.0
    return s_kn, s_kr, s_v


def project_q(cqn, w, cosf, sinf, pm, name):
    T = cqn.shape[0]

    def body(a_ref, w_ref, c_ref, s_ref, p_ref, o_ref):
        a, cv, sv, pv = a_ref[...], c_ref[...], s_ref[...], p_ref[...]
        for h in range(HEADS):
            qh = _dotf(a, w_ref[pl.ds(h * QK, QK), :], "nt")
            o_ref[h] = (_rot(qh, cv, sv, pv, False) * SCALE).astype(BF16)

    rows = lambda c: pl.BlockSpec((TB, c), lambda i: (i, 0))
    const = lambda x: pl.BlockSpec(x.shape, lambda i: (0, 0))
    return pl.pallas_call(
        body, grid=(T // TB,), in_specs=[rows(QL), const(w), rows(QK), rows(QK), const(pm)],
        out_specs=pl.BlockSpec((HEADS, TB, QK), lambda i: (0, i, 0)), out_shape=jax.ShapeDtypeStruct((HEADS, T, QK), BF16),
        compiler_params=_cp(("parallel",)), name=name)(cqn, w, cosf, sinf, pm)


def project_kv(ckvn, w, p0, kr_block, name):
    T = ckvn.shape[0]
    cosf, sinf, pm, _ = _rope_tables(T, 128, 0)
    s_kn, s_kr, s_v = (jnp.asarray(s, BF16) for s in _kv_selectors())

    def body(a_ref, w_ref, kr_ref, c_ref, s_ref, p_ref, skn_ref, skr_ref, sv_ref, k_ref, v_ref):
        a = a_ref[...]
        krr = _rot(kr_ref[...], c_ref[...], s_ref[...], p_ref[...], False).astype(BF16)
        kr_part = jnp.dot(krr, skr_ref[...], preferred_element_type=F32)
        for h in range(HEADS):
            kvb = _dotf(a, w_ref[pl.ds(h * KVW, KVW), :], "nt").astype(BF16)
            k_ref[h] = (jnp.dot(kvb, skn_ref[...], preferred_element_type=F32) + kr_part).astype(BF16)
            v_ref[h] = jnp.dot(kvb, sv_ref[...], preferred_element_type=F32).astype(BF16)

    rows = lambda c: pl.BlockSpec((TB, c), lambda i: (i, 0))
    const = lambda x: pl.BlockSpec(x.shape, lambda i: (0, 0))
    return pl.pallas_call(
        body, grid=(T // TB,),
        in_specs=[rows(KVL), const(w), pl.BlockSpec((TB, 128), lambda i: (i, kr_block)),
                  rows(128), rows(128), const(pm), const(s_kn), const(s_kr), const(s_v)],
        out_specs=[pl.BlockSpec((HEADS, TB, QK), lambda i: (0, i, 0)), pl.BlockSpec((HEADS, TB, VD), lambda i: (0, i, 0))],
        out_shape=[jax.ShapeDtypeStruct((HEADS, T, QK), BF16), jax.ShapeDtypeStruct((HEADS, T, VD), BF16)],
        compiler_params=_cp(("parallel",)), name=name)(ckvn, w, p0, cosf, sinf, pm, s_kn, s_kr, s_v)


def split_kv_grads(dk, dv, name):
    H, T, _ = dk.shape
    cosf, sinf, _, pmt = _rope_tables(T, 128, 0)
    s_kn, s_kr, s_v = _kv_selectors()
    s_knt, s_krt, s_vt = (jnp.asarray(s.T, BF16) for s in (s_kn, s_kr, s_v))

    def body(dk_ref, dv_ref, c_ref, s_ref, p_ref, skn_ref, skr_ref, sv_ref, dkv_ref, dkr_ref):
        total = None
        for h in range(H):
            dkh = dk_ref[h]
            total = dkh if total is None else total + dkh
            dkv_ref[:, pl.ds(h * KVW, KVW)] = (
                jnp.dot(dkh.astype(BF16), skn_ref[...], preferred_element_type=F32)
                + jnp.dot(dv_ref[h].astype(BF16), sv_ref[...], preferred_element_type=F32)).astype(BF16)
        dkr_ref[...] = _rot(_exact_perm(total, skr_ref[...]), c_ref[...], s_ref[...], p_ref[...], True)

    rows = lambda c: pl.BlockSpec((TB, c), lambda i: (i, 0))
    const = lambda a: pl.BlockSpec(a.shape, lambda i: (0, 0))
    return pl.pallas_call(
        body, grid=(T // TB,),
        in_specs=[pl.BlockSpec((H, TB, QK), lambda i: (0, i, 0)), pl.BlockSpec((H, TB, VD), lambda i: (0, i, 0)),
                  rows(128), rows(128), const(pmt), const(s_knt), const(s_krt), const(s_vt)],
        out_specs=[rows(H * KVW), rows(128)],
        out_shape=[jax.ShapeDtypeStruct((T, H * KVW), BF16), jax.ShapeDtypeStruct((T, 128), F32)],
        compiler_params=_cp(("parallel",)), name=name)(dk, dv, cosf, sinf, pmt, s_knt, s_krt, s_vt)


def _by_query_block(run, T):
    @pl.when(pl.program_id(1) == 0)
    def _():
        run(LC)

    @pl.when(pl.program_id(1) > 0)
    def _():
        run(T)


def _with_rider(body, nin, nout, ride, grid):
    if ride is None:
        return body
    n = ride.n

    def wrapped(*refs):
        ins, xs = refs[:nin], refs[nin:nin + n]
        outs, got = refs[nin + n:nin + n + nout], refs[nin + n + nout:nin + 2 * n + nout]
        sems = refs[nin + 2 * n + nout:]
        step = pl.program_id(0) * grid[1] + pl.program_id(1)

        @pl.when(step == 0)
        def _():
            ride.start(xs, got, sems)

        body(*ins, *outs)

        @pl.when(step == grid[0] * grid[1] - 1)
        def _():
            ride.finish(xs, got, sems)

    return wrapped


def _ride_call(body, grid, in_specs, out_specs, out_shape, ride, rode, name, args):
    if ride is None:
        return pl.pallas_call(body, grid=grid, in_specs=in_specs, out_specs=out_specs, out_shape=out_shape,
                              compiler_params=_cp(("parallel", "arbitrary")), name=name)(*args), []
    res = pl.pallas_call(
        _with_rider(body, len(in_specs), len(out_specs), ride, grid), grid=grid,
        in_specs=in_specs + ride.specs, out_specs=out_specs + ride.specs, out_shape=out_shape + ride.out_shape,
        scratch_shapes=ride.scratch,
        compiler_params=pltpu.CompilerParams(dimension_semantics=("arbitrary", "arbitrary"), vmem_limit_bytes=VMEM_LIMIT,
                                             has_side_effects=True), name=name)(*args, *rode)
    return res[:len(out_specs)], res[len(out_specs):]


def attn_fwd(q, k, v, name, rode=None, modes=None):
    H, T, _ = q.shape

    def body(q_ref, k_ref, v_ref, o_ref, lse_ref):
        def run(nk):
            s = _dotf(q_ref[0], k_ref[0, pl.ds(0, nk), :], "nt")
            m = jnp.max(s, axis=1, keepdims=True)
            p = jnp.exp(s - m)
            l = jnp.sum(p, axis=1, keepdims=True)
            o = jnp.dot(p.astype(BF16), v_ref[0, pl.ds(0, nk), :], preferred_element_type=F32)
            o_ref[0] = o / l
            lse_ref[0] = m + jnp.log(l)

        _by_query_block(run, T)

    return _ride_call(
        body, (H, T // TB),
        [pl.BlockSpec((1, TB, QK), lambda h, i: (h, i, 0)), pl.BlockSpec((1, T, QK), lambda h, i: (h, 0, 0)),
         pl.BlockSpec((1, T, VD), lambda h, i: (h, 0, 0))],
        [pl.BlockSpec((1, TB, VD), lambda h, i: (h, i, 0)), pl.BlockSpec((1, TB, 1), lambda h, i: (h, i, 0))],
        [jax.ShapeDtypeStruct((H, T, VD), F32), jax.ShapeDtypeStruct((H, T, 1), F32)],
        Exchange(rode, modes) if rode else None, rode, name, (q, k, v))


def attn_bwd(q, k, v, o, lse, do, name, rode=None, modes=None):
    H, T, _ = q.shape

    def body(q_ref, k_ref, v_ref, o_ref, lse_ref, do_ref, dq_ref, dk_ref, dv_ref):
        i = pl.program_id(1)

        @pl.when(i == 0)
        def _():
            dk_ref[...] = jnp.zeros_like(dk_ref)
            dv_ref[...] = jnp.zeros_like(dv_ref)

        def run(nk):
            keys = pl.ds(0, nk)
            qv, kv, dov = q_ref[0], k_ref[0, keys, :], do_ref[0]
            p = jnp.exp(_dotf(qv, kv, "nt") - lse_ref[0])
            delta = jnp.sum(dov * o_ref[0], axis=1, keepdims=True)
            dob = dov.astype(BF16)
            dv_ref[0, keys, :] += _dotf(p.astype(BF16), dob, "tn")
            dp = _dotf(dob, v_ref[0, keys, :], "nt")
            ds = (p * (dp - delta)).astype(BF16)
            dq_ref[0] = jnp.dot(ds, kv, preferred_element_type=F32)
            dk_ref[0, keys, :] += _dotf(ds, qv, "tn")

        _by_query_block(run, T)

    blk = lambda c: pl.BlockSpec((1, TB, c), lambda h, i: (h, i, 0))
    full = lambda c: pl.BlockSpec((1, T, c), lambda h, i: (h, 0, 0))
    return _ride_call(
        body, (H, T // TB), [blk(QK), full(QK), full(VD), blk(VD), blk(1), blk(VD)], [blk(QK), full(QK), full(VD)],
        [jax.ShapeDtypeStruct((H, T, QK), F32), jax.ShapeDtypeStruct((H, T, QK), F32), jax.ShapeDtypeStruct((H, T, VD), F32)],
        Exchange(rode, modes) if rode else None, rode, name, (q, k, v, o, lse, do))


def disc_fwd(a_re, a_im, ls, name):
    def body(ar_ref, ai_ref, ls_ref, lr_ref, li_ref, fr_ref, fi_ref):
        ar, ai = ar_ref[...], ai_ref[...]
        dt = jnp.exp(ls_ref[...])
        mag = jnp.exp(ar * dt)
        lr = mag * jnp.cos(ai * dt)
        li = mag * jnp.sin(ai * dt)
        den = ar * ar + ai * ai
        nr = lr - 1.0
        lr_ref[...] = lr
        li_ref[...] = li
        fr_ref[...] = (nr * ar + li * ai) / den
        fi_ref[...] = (li * ar - nr * ai) / den

    return pl.pallas_call(body, out_shape=[jax.ShapeDtypeStruct(a_re.shape, F32)] * 4, name=name)(a_re, a_im, ls)


def disc_b(f_re, f_im, b_re, b_im, name):
    def body(fr_ref, fi_ref, br_ref, bi_ref, or_ref, oi_ref):
        fr, fi, br, bi = fr_ref[...], fi_ref[...], br_ref[...], bi_ref[...]
        or_ref[...] = fr * br - fi * bi
        oi_ref[...] = fr * bi + fi * br

    fs, bs = _disc_b_specs()
    return pl.pallas_call(body, grid=(2, G * P // DISC_ROWS), in_specs=[fs, fs, bs, bs], out_specs=[bs, bs],
                          out_shape=[jax.ShapeDtypeStruct(b_re.shape, F32)] * 2, name=name)(f_re, f_im, b_re, b_im)


DISC_ROWS = 1024


def _disc_b_specs():
    return (pl.BlockSpec((1, DISC_ROWS, 1), lambda d, i: (d, i, 0)), pl.BlockSpec((1, DISC_ROWS, CH), lambda d, i: (d, i, 0)))


def disc_b_bwd(f_re, f_im, b_re, b_im, dbb_re, dbb_im, name):
    def body(fr_ref, fi_ref, br_ref, bi_ref, dr_ref, di_ref, dbr_ref, dbi_ref, dfr_ref, dfi_ref):
        fr, fi, br, bi, dr, di = fr_ref[...], fi_ref[...], br_ref[...], bi_ref[...], dr_ref[...], di_ref[...]
        dbr_ref[...] = fr * dr + fi * di
        dbi_ref[...] = fr * di - fi * dr
        dfr_ref[...] = jnp.sum(dr * br + di * bi, axis=-1, keepdims=True)
        dfi_ref[...] = jnp.sum(di * br - dr * bi, axis=-1, keepdims=True)

    fs, bs = _disc_b_specs()
    return pl.pallas_call(body, grid=(2, G * P // DISC_ROWS), in_specs=[fs, fs, bs, bs, bs, bs], out_specs=[bs, bs, fs, fs],
                          out_shape=[jax.ShapeDtypeStruct(b_re.shape, F32)] * 2 + [jax.ShapeDtypeStruct(f_re.shape, F32)] * 2,
                          name=name)(f_re, f_im, b_re, b_im, dbb_re, dbb_im)


def disc_a_bwd(a_re, a_im, ls, dlr, dli, dfr, dfi, name):
    def body(ar_ref, ai_ref, ls_ref, dlr_ref, dli_ref, dfr_ref, dfi_ref, dar_ref, dai_ref, dls_ref):
        ar, ai = ar_ref[...], ai_ref[...]
        dt = jnp.exp(ls_ref[...])
        mag = jnp.exp(ar * dt)
        cs, sn = jnp.cos(ai * dt), jnp.sin(ai * dt)
        lr, li = mag * cs, mag * sn
        den = ar * ar + ai * ai
        nr = lr - 1.0
        f_re = (nr * ar + li * ai) / den
        f_im = (li * ar - nr * ai) / den
        dn1 = dfr_ref[...] / den
        dn2 = dfi_ref[...] / den
        dden = -(dfr_ref[...] * f_re + dfi_ref[...] * f_im) / den
        dlr_t = dlr_ref[...] + dn1 * ar - dn2 * ai
        dli_t = dli_ref[...] + dn1 * ai + dn2 * ar
        dar = dn1 * nr + dn2 * li + dden * 2.0 * ar
        dai = dn1 * li - dn2 * nr + dden * 2.0 * ai
        dmag = dlr_t * cs + dli_t * sn
        dth = dli_t * lr - dlr_t * li
        dar_ref[...] = dar + dmag * mag * dt
        dai_ref[...] = dai + dth * dt
        dls_ref[...] = jnp.sum(dmag * mag * ar + dth * ai, axis=-1, keepdims=True) * dt

    return pl.pallas_call(body, out_shape=[jax.ShapeDtypeStruct(a_re.shape, F32)] * 2 +
                          [jax.ShapeDtypeStruct(ls.shape, F32)], name=name)(a_re, a_im, ls, dlr, dli, dfr, dfi)


def _cpow(lr, li, n):
    rr, ri = None, None
    br, bi = lr, li
    while n:
        if n & 1:
            if rr is None:
                rr, ri = br, bi
            else:
                rr, ri = rr * br - ri * bi, rr * bi + ri * br
        n >>= 1
        if n:
            br, bi = br * br - bi * bi, 2.0 * br * bi
    return rr, ri


UNROLL = 4


def _seg_scan(xre, xim, lam8, pw, base, seglen, rev, init, fin_re, fin_im, ini_re, ini_im, prev=None):
    lr, li = lam8

    def rows(t):
        return pl.ds(pl.multiple_of(base + t * SEG, SEG), SEG)

    tmap = (lambda n: seglen - 1 - n) if rev else (lambda n: n)
    zero = jnp.zeros((SEG, SB), F32)

    def advance(c, t):
        a, b = c
        return lr * a - li * b + xre[rows(t), :], lr * b + li * a + xim[rows(t), :]

    fin = lax.fori_loop(0, seglen, lambda n, c: advance(c, tmap(n)), (zero, zero), unroll=UNROLL)
    fin_re[...] = fin[0]
    fin_im[...] = fin[1]
    (cr, ci), (pr, pi) = init, pw
    for i in (range(SEG - 1, -1, -1) if rev else range(SEG)):
        ini_re[pl.ds(i, 1), :] = cr
        ini_im[pl.ds(i, 1), :] = ci
        cr, ci = pr * cr - pi * ci + fin_re[pl.ds(i, 1), :], pr * ci + pi * cr + fin_im[pl.ds(i, 1), :]
    start = (ini_re[...], ini_im[...])

    def store(c, t):
        na, nb = advance(c, t)
        xre[rows(t), :] = na
        xim[rows(t), :] = nb
        return na, nb

    if prev is None:
        lax.fori_loop(0, seglen, lambda n, c: store(c, tmap(n)), start, unroll=UNROLL)
        return (cr, ci), None

    sre, sim, s_ini_re, s_ini_im = prev

    def acc_step(c, t, pre, pim):
        na, nb = store(c[:2], t)
        return na, nb, c[2] + na * pre + nb * pim, c[3] + nb * pre - na * pim

    def body(n, c):
        t = tmap(n)
        tp = t - 1 if rev else t + 1
        return acc_step(c, t, sre[rows(tp), :], sim[rows(tp), :])

    c = lax.fori_loop(0, seglen - 1, body, start + (zero, zero), unroll=UNROLL)
    c = acc_step(c, 0 if rev else seglen - 1, s_ini_re[...], s_ini_im[...])
    return (cr, ci), c[2:]


def _lam_tiles(lr, li, lens, conj=False):
    if conj:
        li = -li
    lam8 = (jnp.broadcast_to(lr, (SEG, SB)), jnp.broadcast_to(li, (SEG, SB)))
    return lam8, [_cpow(lr, li, n) for n in lens]


def _stretches(T):
    return ((0, LC // SEG), (LC, (T - LC) // SEG))


def _to_seg_order(src, dst, T):
    for base, seglen in _stretches(T):
        def body(t, carry, base=base, seglen=seglen):
            dst[pl.ds(pl.multiple_of(base + t * SEG, SEG), SEG), :] = src[pl.ds(base + t, SEG, stride=seglen), :]
            return carry
        lax.fori_loop(0, seglen, body, 0, unroll=8)


def _from_seg_order(src, dst, T):
    for base, seglen in _stretches(T):
        def body(t, carry, base=base, seglen=seglen):
            dst[pl.ds(base + t, SEG, stride=seglen), :] = src[pl.ds(pl.multiple_of(base + t * SEG, SEG), SEG), :]
            return carry
        lax.fori_loop(0, seglen, body, 0, unroll=8)


def _scan_specs(T):
    ublk = pl.BlockSpec((T, UB), lambda j: (0, j))
    lam = pl.BlockSpec((2, 1, 1, SB), lambda j: (0, j, 0, 0))
    mat = pl.BlockSpec((2, 1, UB, P), lambda j: (0, j, 0, 0))
    return ublk, lam, mat


def _dotf(a, b, mode="nn"):
    return lax.dot_general(a, b, _DN[mode], preferred_element_type=F32)


def _diag_mask():
    r = lax.broadcasted_iota(jnp.int32, (UB, SB), 0)
    c = lax.broadcasted_iota(jnp.int32, (UB, SB), 1)
    return lax.shift_right_logical(r, int(math.log2(CH))) == lax.shift_right_logical(c, int(math.log2(P)))


def _expand(m):
    p = lax.broadcasted_iota(jnp.int32, (P, SB), 0)
    c = lax.broadcasted_iota(jnp.int32, (P, SB), 1)
    tile = jnp.where(lax.bitwise_and(c, P - 1) == p, 1.0, 0.0).astype(BF16)
    wide = jnp.dot(m.astype(BF16), tile, preferred_element_type=F32)
    return jnp.where(_diag_mask(), wide, 0.0).astype(BF16)


def _collapse(full):
    c = lax.broadcasted_iota(jnp.int32, (SB, P), 0)
    p = lax.broadcasted_iota(jnp.int32, (SB, P), 1)
    pick = jnp.where(lax.bitwise_and(c, P - 1) == p, 1.0, 0.0).astype(BF16)
    return _exact_perm(jnp.where(_diag_mask(), full, 0.0), pick)


def _zero_state():
    return jnp.zeros((1, SB), F32), jnp.zeros((1, SB), F32)


def scan_fwd(u, lam_re, lam_im, bre, bim, cre, cim, name):
    T = u.shape[0]
    s_ctx, s_lat = LC // SEG, (T - LC) // SEG

    def body(u_ref, lr_ref, li_ref, bre_ref, bim_ref, cre_ref, cim_ref, y_ref, us, ys, sre, sim, fre, fim, ire, iim):
        _to_seg_order(u_ref, us, T)
        ub = us[...].astype(BF16)
        for d in range(2):
            lam8, (pw_c, pw_l) = _lam_tiles(lr_ref[d, 0], li_ref[d, 0], (s_ctx, s_lat))
            sre[...] = _dotf(ub, _expand(bre_ref[d, 0]))
            sim[...] = _dotf(ub, _expand(bim_ref[d, 0]))
            end_c, _ = _seg_scan(sre, sim, lam8, pw_c, 0, s_ctx, bool(d), _zero_state(), fre, fim, ire, iim)
            _seg_scan(sre, sim, lam8, pw_l, LC, s_lat, bool(d), end_c, fre, fim, ire, iim)
            y = (_dotf(sre[...].astype(BF16), _expand(cre_ref[d, 0]), "nt")
                 - _dotf(sim[...].astype(BF16), _expand(cim_ref[d, 0]), "nt"))
            if d == 0:
                ys[...] = y
            else:
                ys[...] += y
        _from_seg_order(ys, y_ref, T)

    ublk, lam, mat = _scan_specs(T)
    return pl.pallas_call(
        body, grid=(NJ,), in_specs=[ublk, lam, lam, mat, mat, mat, mat], out_specs=ublk,
        out_shape=jax.ShapeDtypeStruct((T, G * CH), F32),
        scratch_shapes=[pltpu.VMEM((T, UB), F32)] * 2 + [pltpu.VMEM((T, SB), F32)] * 2 + [pltpu.VMEM((SEG, SB), F32)] * 4,
        compiler_params=_cp(("arbitrary",)), name=name)(u, lam_re, lam_im, bre, bim, cre, cim)


def scan_bwd(u, dy, lam_re, lam_im, bre, bim, cre, cim, name):
    T = u.shape[0]
    s_ctx, s_lat = LC // SEG, (T - LC) // SEG

    def body(u_ref, dy_ref, lr_ref, li_ref, bre_ref, bim_ref, cre_ref, cim_ref,
             du_ref, dlr_ref, dli_ref, dbre_ref, dbim_ref, dcre_ref, dcim_ref,
             us, dys, dus, sre, sim, gre, gim, fre, fim, ic_re, ic_im, il_re, il_im, jre, jim):
        _to_seg_order(u_ref, us, T)
        _to_seg_order(dy_ref, dys, T)
        ub, dyb = us[...].astype(BF16), dys[...].astype(BF16)
        for d in range(2):
            rev = bool(d)
            lam8, (pw_c, pw_l) = _lam_tiles(lr_ref[d, 0], li_ref[d, 0], (s_ctx, s_lat))
            cam8, (cw_c, cw_l) = _lam_tiles(lr_ref[d, 0], li_ref[d, 0], (s_ctx, s_lat), conj=True)
            bre_v, bim_v = _expand(bre_ref[d, 0]), _expand(bim_ref[d, 0])
            sre[...] = _dotf(ub, bre_v)
            sim[...] = _dotf(ub, bim_v)
            end_c, _ = _seg_scan(sre, sim, lam8, pw_c, 0, s_ctx, rev, _zero_state(), fre, fim, ic_re, ic_im)
            _seg_scan(sre, sim, lam8, pw_l, LC, s_lat, rev, end_c, fre, fim, il_re, il_im)
            gre[...] = _dotf(dyb, _expand(cre_ref[d, 0]))
            gim[...] = -_dotf(dyb, _expand(cim_ref[d, 0]))
            end_g, acc_l = _seg_scan(gre, gim, cam8, cw_l, LC, s_lat, not rev, _zero_state(), fre, fim, jre, jim,
                                     prev=(sre, sim, il_re, il_im))
            _, acc_c = _seg_scan(gre, gim, cam8, cw_c, 0, s_ctx, not rev, end_g, fre, fim, jre, jim,
                                 prev=(sre, sim, ic_re, ic_im))
            dlr_ref[d, 0] = _sum0(acc_l[0] + acc_c[0])
            dli_ref[d, 0] = _sum0(acc_l[1] + acc_c[1])
            grb, gib = gre[...].astype(BF16), gim[...].astype(BF16)
            du = _dotf(grb, bre_v, "nt") + _dotf(gib, bim_v, "nt")
            if d == 0:
                dus[...] = du
            else:
                dus[...] += du
            dbre_ref[d, 0] = _collapse(_dotf(ub, grb, "tn"))
            dbim_ref[d, 0] = _collapse(_dotf(ub, gib, "tn"))
            dcre_ref[d, 0] = _collapse(_dotf(dyb, sre[...].astype(BF16), "tn"))
            dcim_ref[d, 0] = -_collapse(_dotf(dyb, sim[...].astype(BF16), "tn"))
        _from_seg_order(dus, du_ref, T)

    ublk, lam, mat = _scan_specs(T)
    lam_s = jax.ShapeDtypeStruct(lam_re.shape, F32)
    mat_s = jax.ShapeDtypeStruct(bre.shape, F32)
    return pl.pallas_call(
        body, grid=(NJ,), in_specs=[ublk, ublk, lam, lam, mat, mat, mat, mat],
        out_specs=[ublk, lam, lam, mat, mat, mat, mat],
        out_shape=[jax.ShapeDtypeStruct((T, G * CH), F32), lam_s, lam_s, mat_s, mat_s, mat_s, mat_s],
        scratch_shapes=[pltpu.VMEM((T, UB), F32)] * 3 + [pltpu.VMEM((T, SB), F32)] * 4 + [pltpu.VMEM((SEG, SB), F32)] * 8,
        compiler_params=_cp(("arbitrary",)), name=name)(u, dy, lam_re, lam_im, bre, bim, cre, cim)


class Exchange:
    def __init__(self, xs, modes):
        self.n = len(xs)
        self.modes = [modes] * self.n if isinstance(modes, (str, int)) else list(modes)
        self.out_shape = [jax.ShapeDtypeStruct(self._shape(x, md), x.dtype) for x, md in zip(xs, self.modes)]
        self.scratch = [pltpu.SemaphoreType.DMA((NDEV - 1, self.n)), pltpu.SemaphoreType.DMA((NDEV - 1, self.n)),
                        pltpu.SemaphoreType.DMA((self.n,))]
        self.specs = [pl.BlockSpec(memory_space=pl.ANY)] * self.n

    @staticmethod
    def _shape(x, mode):
        if mode == "gather":
            return (NDEV,) + tuple(x.shape)
        return tuple(x.shape) if mode == "lead" else (NDEV, x.shape[0], mode) + tuple(x.shape[2:])

    @staticmethod
    def _piece(x_ref, mode, dev):
        if mode == "gather":
            return x_ref
        return x_ref.at[dev] if mode == "lead" else x_ref.at[:, pl.ds(dev * mode, mode)]

    def _copies(self, x_refs, out_refs, sems):
        send_sems, recv_sems, local_sems = sems
        mx, my, mc = lax.axis_index("x"), lax.axis_index("y"), lax.axis_index("c")
        me = 4 * mx + 2 * my + mc
        local = [pltpu.make_async_copy(self._piece(x_ref, self.modes[a], me), out_ref.at[me], local_sems.at[a])
                 for a, (x_ref, out_ref) in enumerate(zip(x_refs, out_refs))]
        sends, recvs = [], []
        for k in range(1, NDEV):
            peer = (1 - mx if k & 4 else mx, 1 - my if k & 2 else my, 1 - mc if k & 1 else mc)
            pid = 4 * peer[0] + 2 * peer[1] + peer[2]
            for a, (x_ref, out_ref) in enumerate(zip(x_refs, out_refs)):
                src = self._piece(x_ref, self.modes[a], pid)
                sems_k = dict(send_sem=send_sems.at[k - 1, a], recv_sem=recv_sems.at[k - 1, a], device_id=peer,
                              device_id_type=MESH_T)
                sends.append(pltpu.make_async_remote_copy(src_ref=src, dst_ref=out_ref.at[me], **sems_k))
                recvs.append(pltpu.make_async_remote_copy(src_ref=src, dst_ref=out_ref.at[pid], **sems_k))
        return local, sends, recvs

    def start(self, x_refs, out_refs, sems):
        local, sends, _ = self._copies(x_refs, out_refs, sems)
        for cp in local + sends:
            cp.start()

    def finish(self, x_refs, out_refs, sems):
        local, sends, recvs = self._copies(x_refs, out_refs, sems)
        for cp in recvs:
            cp.wait_recv()
        for cp in sends:
            cp.wait_send()
        for cp in local:
            cp.wait()


def exchange(xs, modes, name):
    ex = Exchange(xs, modes)
    n = ex.n

    def body(*refs):
        ex.start(refs[:n], refs[n:2 * n], refs[2 * n:])
        ex.finish(refs[:n], refs[n:2 * n], refs[2 * n:])

    return pl.pallas_call(body, in_specs=ex.specs, out_specs=ex.specs, out_shape=ex.out_shape, scratch_shapes=ex.scratch,
                          compiler_params=pltpu.CompilerParams(has_side_effects=True), name=name)(*xs)


def _dot_f32(a, b, dn):
    return lax.dot_general(a, b, dn, preferred_element_type=F32, precision=lax.Precision.HIGHEST)


def ada_fwd(cg, c_ctx, ada_w, ada_b_loc, name):
    W = ada_w.shape[2]

    def body(cg_ref, cc_ref, w_ref, b_ref, o_ref):
        a = jnp.concatenate([_silu(cg_ref[...]), jnp.broadcast_to(_silu(cc_ref[...]), (NDEV, D))], axis=0)
        for i in range(2):
            o_ref[i] = _dot_f32(a, w_ref[i], _DN["nn"]) + b_ref[i]

    return pl.pallas_call(body, out_shape=jax.ShapeDtypeStruct((2, 2 * NDEV, W), F32),
                          compiler_params=_cp(), name=name)(cg, c_ctx, ada_w, ada_b_loc)


def ada_bwd(cg, c_ctx, ada_w, dm_loc, dm_all, name):
    W = ada_w.shape[2]

    def body(cg_ref, cc_ref, w_ref, dl_ref, da_ref, gw_ref, dcc_ref, gb_ref):
        a = jnp.concatenate([_silu(cg_ref[...]), jnp.broadcast_to(_silu(cc_ref[...]), (NDEV, D))], axis=0)
        dcc = jnp.zeros((1, D), F32)
        for i in range(2):
            dl = dl_ref[i]
            gw_ref[i] = _dot_f32(a, dl, _DN["tn"])
            dctx = jnp.sum(dl[NDEV:], axis=0, keepdims=True)
            dcc = dcc + _dot_f32(dctx, w_ref[i], _DN["nt"])
        dcc_ref[...] = dcc
        gb_ref[...] = jnp.sum(da_ref[...], axis=0)

    return pl.pallas_call(body, out_shape=[jax.ShapeDtypeStruct((2, D, W), F32), jax.ShapeDtypeStruct((1, D), F32),
                                           jax.ShapeDtypeStruct((2, 3 * D), F32)],
                          compiler_params=_cp(), name=name)(cg, c_ctx, ada_w, dm_loc, dm_all)


def cctx_finish(parts, c_ctx, name):
    def body(p_ref, cc_ref, o_ref):
        o_ref[...] = jnp.sum(p_ref[...], axis=0, keepdims=True) * _dsilu(cc_ref[...])

    return pl.pallas_call(body, out_shape=jax.ShapeDtypeStruct((1, D), F32), name=name)(parts, c_ctx)


def _adamw_update(g_ref, w_ref, m_ref, v_ref, go_ref, d_ref, mo_ref, vo_ref):
    g = g_ref[0].astype(F32)
    for s in range(1, g_ref.shape[0]):
        g = g + g_ref[s].astype(F32)
    mn = B1 * m_ref[...] + (1.0 - B1) * g
    vn = B2 * v_ref[...] + (1.0 - B2) * g * g
    go_ref[...] = g
    mo_ref[...] = mn
    vo_ref[...] = vn
    d_ref[...] = -LR * ((mn * (1.0 / (1.0 - B1 ** STEP))) / (jnp.sqrt(vn * (1.0 / (1.0 - B2 ** STEP))) + AEPS) + WD * w_ref[...])


def adamw(gstack, w, m, v, name, tr=256):
    n, R, C = gstack.shape
    tr = max(t for t in range(8, min(tr, R) + 1, 8) if R % t == 0)
    spec = pl.BlockSpec((tr, C), lambda i: (i, 0))
    return pl.pallas_call(_adamw_body(1), grid=(R // tr,),
                          in_specs=[pl.BlockSpec((n, tr, C), lambda i: (0, i, 0)), spec, spec, spec],
                          out_specs=[spec] * 4, out_shape=[jax.ShapeDtypeStruct((R, C), F32)] * 4,
                          compiler_params=_cp(("parallel",)), name=name)(gstack, w, m, v)


def _adamw_body(k):
    def body(*refs):
        for t in range(k):
            _adamw_update(*refs[4 * t:4 * t + 4], *refs[4 * k + 4 * t:4 * k + 4 * t + 4])
    return body


def adamw_multi(items, grid, name):
    k = len(items)
    ins, in_specs, out_specs, out_shape = [], [], [], []
    for g, g_spec, w, m, v, w_spec in items:
        ins += [g, w, m, v]
        in_specs += [g_spec, w_spec, w_spec, w_spec]
    for g, g_spec, w, m, v, w_spec in items:
        out_specs += [w_spec] * 4
        out_shape += [jax.ShapeDtypeStruct(w.shape, F32)] * 4
    res = pl.pallas_call(_adamw_body(k), grid=grid, in_specs=in_specs, out_specs=out_specs, out_shape=out_shape,
                         compiler_params=_cp(("arbitrary",) * len(grid)), name=name)(*ins)
    return [res[4 * t:4 * t + 4] for t in range(k)]


def _whole(a, grid_rank):
    zeros = (0,) * a.ndim
    return pl.BlockSpec(a.shape, lambda *idx: zeros)


def sum_slots(xs, name):
    def body(*refs):
        for x_ref, o_ref in zip(refs[:len(xs)], refs[len(xs):]):
            acc = x_ref[0]
            for s in range(1, NDEV):
                acc = acc + x_ref[s]
            o_ref[...] = acc

    return pl.pallas_call(body, out_shape=[jax.ShapeDtypeStruct(x.shape[1:], F32) for x in xs],
                          compiler_params=_cp(), name=name)(*xs)


def _col_shards(g):
    R, N = g.shape
    return g.reshape(R, NDEV, N // NDEV).transpose(1, 0, 2)


def _vec2(v):
    return jnp.broadcast_to(v.reshape(1, 1, -1), (2, 1, v.size))


SHARD_ROWS = {"mla_w_in": 192, "mla_w_uq": 192, "mla_w_ukv": 256, "s5_w_in": 256}


def _t_shard(wsh, rows):
    t = wsh[0].T.astype(BF16)
    return jnp.pad(t, ((0, rows - t.shape[0]), (0, 0)))


def _win_order():
    w = IN_W // NDEV
    perm = np.zeros((IN_WP, NDEV * SHARD_ROWS["mla_w_in"]), np.float32)
    first = QL + KVL + ROPE
    for c in range(IN_W):
        n = c + HEADS * VD if c < first else c - first
        perm[n, (c // w) * SHARD_ROWS["mla_w_in"] + c % w] = 1.0
    return jnp.asarray(perm, BF16)


def local_step(ctx, x, tgt, mod, Wt, small, l1_shards):
    T = LC + x.shape[0]
    xa = ("cat", ctx, x)
    sh = [mod[i, :, None, 0:D] for i in range(2)]
    sc = [mod[i, :, None, D:2 * D] for i in range(2)]
    gt = [mod[i, :, None, 2 * D:] for i in range(2)]
    ng = [_vec2(small["norm_g"][i]) for i in range(2)]
    qg, kvg = _vec2(small["mla_q_norm"]), _vec2(small["mla_kv_norm"])
    cosf, sinf, pm, pmt = _rope_tables(T)

    (h0, p0, cqn, ckvn), _ = rowwise(st_l0_pre, [xa], [ng[0], sc[0], sh[0], qg, kvg],
                                     [(D, BF16), (IN_WP, F32), (QL, BF16), (KVL, BF16)], [], "l0_pre", mats=[Wt["mla_w_in"]])
    z0, cq, ckv = (p0, 0, HEADS * VD), (p0, HEADS * VD // QL, QL), (p0, (HEADS * VD + QL) // KVL, KVL)
    Q = project_q(cqn, Wt["mla_w_uq"], cosf, sinf, pm, "l0_uq")
    K, V = project_kv(ckvn, Wt["mla_w_ukv"], p0, (HEADS * VD + QL + KVL) // 128, "l0_ukv")
    (o, lse), got = attn_fwd(Q, K, V, "l0_attn", rode=l1_shards, modes="gather")
    Wt, small = dict(Wt), dict(small)
    for n, a in zip(L1_BIG, got):
        Wt[n] = a.reshape(-1, a.shape[-1])
    vecs = lax.bitcast_convert_type(got[-1].reshape(NDEV, 2, -1, 2), F32)
    small["s5_d"], small["s5_b_glu"] = vecs[:, 0, :].reshape(D), vecs[:, 1, :].reshape(D)
    o2 = o.transpose(1, 0, 2).reshape(T, HEADS * VD)
    (og, out0, x1), _ = rowwise(st_l0_post, [o2, z0, xa], [gt[0]], [(D, BF16), (D, F32), (D, F32)], [], "l0_post",
                                mats=[Wt["mla_w_out"]])

    ls = small["s5_log_step"].reshape(2, G, 1)
    a_re, a_im = small["s5_a_re"].reshape(2, G, P), small["s5_a_im"].reshape(2, G, P)
    b_re, b_im = small["s5_b_re"].reshape(2, G * P, CH), small["s5_b_im"].reshape(2, G * P, CH)
    lam_re, lam_im, f_re, f_im = disc_fwd(a_re, a_im, ls, "s5_disc")
    f_re2, f_im2 = f_re.reshape(2, G * P, 1), f_im.reshape(2, G * P, 1)
    bb_re, bb_im = disc_b(f_re2, f_im2, b_re, b_im, "s5_disc_b")
    compact = lambda m: m.reshape(2, NJ, UB, P)
    bre = compact(bb_re.reshape(2, G, P, CH).transpose(0, 1, 3, 2))
    bim = compact(bb_im.reshape(2, G, P, CH).transpose(0, 1, 3, 2))
    cre, cim = compact(small["s5_c_re"]), compact(small["s5_c_im"])
    lam_re4, lam_im4 = lam_re.reshape(2, NJ, 1, SB), lam_im.reshape(2, NJ, 1, SB)

    (h1, p1), _ = rowwise(st_l1_pre, [x1], [ng[1], sc[1], sh[1]], [(D, BF16), (2 * D, F32)], [], "l1_pre", mats=[Wt["s5_w_in"]])
    u, z1 = (p1, 0, D), (p1, 1, D)
    yssm = scan_fwd(p1, lam_re4, lam_im4, bre, bim, cre, cim, "s5_scan")
    dvec, bglu = _vec2(small["s5_d"]), _vec2(small["s5_b_glu"])
    fg = _vec2(small["final_g"])
    lat_mask = jnp.stack([jnp.zeros((1, D), F32), jnp.ones((1, D), F32)])
    (y, y1b, gl, y3, out1, dx2), (dfg, lvec) = rowwise(
        st_l1_mlp, [yssm, u, z1, x1, ("lat", tgt)], [dvec, bglu, gt[1], fg, lat_mask],
        [(D, F32), (D, BF16), (D, F32), (D, BF16), (D, F32), (D, F32)], [D, 128], "l1_mlp",
        mats=[Wt["s5_w_glu"], Wt["s5_w_out"]])

    (dz1, dy, du_d), (dgt1, dbglu, dd), (g_w_out5, g_w_glu) = rowwise(
        st_l1_mlp_bwd, [dx2, out1, y3, y, gl, z1, u, y1b], [gt[1], bglu, dvec], [(D, BF16), (D, F32), (D, F32)], [D, D, D],
        "l1_mlp_b", mats=[Wt["s5_w_out"], Wt["s5_w_glu"]], out_accs=[(D, D), (D, D)])
    du_s, dlr, dli, dbre, dbim, dcre, dcim = scan_bwd(p1, dy, lam_re4, lam_im4, bre, bim, cre, cim, "s5_scan_b")
    dbb_re = dbre.reshape(2, G, CH, P).transpose(0, 1, 3, 2).reshape(2, G * P, CH)
    dbb_im = dbim.reshape(2, G, CH, P).transpose(0, 1, 3, 2).reshape(2, G * P, CH)
    g_c_re, g_c_im = dcre.reshape(2, G, CH, P), dcim.reshape(2, G, CH, P)
    g_b_re, g_b_im, dfr, dfi = disc_b_bwd(f_re2, f_im2, b_re, b_im, dbb_re, dbb_im, "s5_disc_b_b")
    g_a_re, g_a_im, g_ls = disc_a_bwd(a_re, a_im, ls, dlr.reshape(2, G, P), dli.reshape(2, G, P),
                                      dfr.reshape(2, G, P), dfi.reshape(2, G, P), "s5_disc_b_a")
    (dx1,), (dsh1, dsc1, dng1), (g_w_in5,) = rowwise(
        st_l1_tail_bwd, [du_d, du_s, dz1, h1, x1, dx2], [ng[1], sc[1]], [(D, F32)], [D, D, D], "l1_pre_b",
        mats=[Wt["s5_w_in"]], out_accs=[(D, 2 * D)])
    g_w_in5 = _col_shards(g_w_in5)

    (do2, dz0), (dgt0,), (g_w_out,) = rowwise(st_l0_post_bwd, [dx1, out0, og, o2, z0], [gt[0]], [(D, F32), (D, F32)], [D],
                                              "l0_post_b", mats=[Wt["mla_w_out"]], out_accs=[(D, D)])
    doh = do2.reshape(T, HEADS, VD).transpose(1, 0, 2)
    rows8 = lambda g: g.reshape(NDEV, -1, g.shape[-1])
    both = lambda s: s[0, 0] + s[1, 0]
    dense = lambda g: g.reshape(2, G * P * CH // 128, 128)
    chunks = [dense(g_b_re), dense(g_b_im), g_c_re, g_c_im]
    l1_send = [g_w_in5, rows8(g_w_glu), rows8(g_w_out5), rows8(g_w_out),
               both(dd).reshape(NDEV, 1, -1), both(dbglu).reshape(NDEV, 1, -1)]
    (dQ, dK, dV), l1_recv = attn_bwd(Q, K, V, o, lse, doh, "l0_attn_b", rode=l1_send + chunks,
                                     modes=["lead"] * len(l1_send) + [a.shape[1] // NDEV for a in chunks])
    dqh = rope(dQ, cosf, sinf, pmt, True, BF16, "l0_rope_q_b", scale=SCALE)
    dq = dqh.transpose(1, 0, 2).reshape(T, HEADS * QK)
    dkv, dkr = split_kv_grads(dK, dV, "l0_kv_b")
    (grad_x,), (dqg, dkvg, dsh0, dsc0, dng0), (g_uq, g_ukv, g_p) = rowwise(
        st_l0_tail_bwd, [dq, dkv, dkr, dz0, cq, ckv, cqn, ckvn, h0, xa, dx1], [qg, kvg, ng[0], sc[0]],
        [(D, F32, "lat")], [QL, KVL, D, D, D], "l0_pre_b", mats=[Wt["mla_w_uq"], Wt["mla_w_ukv"], Wt["mla_w_in"]],
        out_accs=[(QL, HEADS * QK), (KVL, HEADS * KVW), (D, IN_WP)])
    g_w_uq, g_w_ukv = _col_shards(g_uq).astype(BF16), _col_shards(g_ukv).astype(BF16)
    g_w_in = _col_shards(jnp.concatenate([g_p[:, HEADS * VD:IN_W], g_p[:, :HEADS * VD]], axis=1)).astype(BF16)

    dmod = jnp.stack([jnp.concatenate([dsh0, dsc0, dgt0], axis=-1)[:, 0], jnp.concatenate([dsh1, dsc1, dgt1], axis=-1)[:, 0]])
    gbig = {"mla_w_in": g_w_in, "mla_w_uq": g_w_uq, "mla_w_ukv": g_w_ukv}
    gsmall = {"norm_g": jnp.stack([both(dng0), both(dng1)]), "mla_q_norm": both(dqg), "mla_kv_norm": both(dkvg),
              "s5_a_re": g_a_re, "s5_a_im": g_a_im, "s5_log_step": g_ls, "final_g": dfg[1, 0]}
    return lvec[1], grad_x, dmod, gbig, gsmall, l1_recv


COL_SHARDED = ("mla_w_in", "mla_w_uq", "mla_w_ukv", "s5_w_in")
ROW_SHARDED = ("mla_w_out", "s5_w_glu", "s5_w_out")
VEC_SHARDED = ("s5_d", "s5_b_glu")
BIG = COL_SHARDED + ROW_SHARDED
L0_BIG = ("mla_w_in", "mla_w_uq", "mla_w_ukv")
L1_BIG = ("s5_w_in", "s5_w_glu", "s5_w_out", "mla_w_out")
BITS16 = jnp.bfloat16
SMALL_RS = ("norm_g", "mla_q_norm", "mla_kv_norm", "s5_a_re", "s5_a_im", "s5_log_step", "s5_b_re", "s5_b_im",
            "s5_c_re", "s5_c_im", "final_g")
CHUNKED = ("s5_b_re", "s5_b_im", "s5_c_re", "s5_c_im")
DENSE = ("s5_b_re", "s5_b_im")
TINY = ("norm_g", "mla_q_norm", "mla_kv_norm", "s5_a_re", "s5_a_im", "s5_log_step", "final_g")
ORDER = ("c_ctx", "ada_w", "ada_b", "norm_g", "mla_w_in", "mla_q_norm", "mla_w_uq", "mla_kv_norm", "mla_w_ukv",
         "mla_w_out", "s5_w_in", "s5_a_re", "s5_a_im", "s5_log_step", "s5_b_re", "s5_b_im", "s5_c_re", "s5_c_im",
         "s5_d", "s5_w_glu", "s5_b_glu", "s5_w_out", "final_g")


def kernel(x, c, ctx, c_ctx, ada_w, ada_b, norm_g, mla_w_in, mla_q_norm, mla_w_uq, mla_kv_norm, mla_w_ukv, mla_w_out, s5_w_in, s5_a_re, s5_a_im, s5_log_step, s5_b_re, s5_b_im, s5_c_re, s5_c_im, s5_d, s5_w_glu, s5_b_glu, s5_w_out, final_g, loss_target, m_c_ctx, m_ada_w, m_ada_b, m_norm_g, m_mla_w_in, m_mla_q_norm, m_mla_w_uq, m_mla_kv_norm, m_mla_w_ukv, m_mla_w_out, m_s5_w_in, m_s5_a_re, m_s5_a_im, m_s5_log_step, m_s5_b_re, m_s5_b_im, m_s5_c_re, m_s5_c_im, m_s5_d, m_s5_w_glu, m_s5_b_glu, m_s5_w_out, m_final_g, v_c_ctx, v_ada_w, v_ada_b, v_norm_g, v_mla_w_in, v_mla_q_norm, v_mla_w_uq, v_mla_kv_norm, v_mla_w_ukv, v_mla_w_out, v_s5_w_in, v_s5_a_re, v_s5_a_im, v_s5_log_step, v_s5_b_re, v_s5_b_im, v_s5_c_re, v_s5_c_im, v_s5_d, v_s5_w_glu, v_s5_b_glu, v_s5_w_out, v_final_g):
    w = dict(c_ctx=c_ctx, ada_w=ada_w, ada_b=ada_b, norm_g=norm_g, mla_w_in=mla_w_in, mla_q_norm=mla_q_norm,
             mla_w_uq=mla_w_uq, mla_kv_norm=mla_kv_norm, mla_w_ukv=mla_w_ukv, mla_w_out=mla_w_out, s5_w_in=s5_w_in,
             s5_a_re=s5_a_re, s5_a_im=s5_a_im, s5_log_step=s5_log_step, s5_b_re=s5_b_re, s5_b_im=s5_b_im,
             s5_c_re=s5_c_re, s5_c_im=s5_c_im, s5_d=s5_d, s5_w_glu=s5_w_glu, s5_b_glu=s5_b_glu, s5_w_out=s5_w_out,
             final_g=final_g)
    m = dict(c_ctx=m_c_ctx, ada_w=m_ada_w, ada_b=m_ada_b, norm_g=m_norm_g, mla_w_in=m_mla_w_in, mla_q_norm=m_mla_q_norm,
             mla_w_uq=m_mla_w_uq, mla_kv_norm=m_mla_kv_norm, mla_w_ukv=m_mla_w_ukv, mla_w_out=m_mla_w_out,
             s5_w_in=m_s5_w_in, s5_a_re=m_s5_a_re, s5_a_im=m_s5_a_im, s5_log_step=m_s5_log_step, s5_b_re=m_s5_b_re,
             s5_b_im=m_s5_b_im, s5_c_re=m_s5_c_re, s5_c_im=m_s5_c_im, s5_d=m_s5_d, s5_w_glu=m_s5_w_glu,
             s5_b_glu=m_s5_b_glu, s5_w_out=m_s5_w_out, final_g=m_final_g)
    v = dict(c_ctx=v_c_ctx, ada_w=v_ada_w, ada_b=v_ada_b, norm_g=v_norm_g, mla_w_in=v_mla_w_in, mla_q_norm=v_mla_q_norm,
             mla_w_uq=v_mla_w_uq, mla_kv_norm=v_mla_kv_norm, mla_w_ukv=v_mla_w_ukv, mla_w_out=v_mla_w_out,
             s5_w_in=v_s5_w_in, s5_a_re=v_s5_a_re, s5_a_im=v_s5_a_im, s5_log_step=v_s5_log_step, s5_b_re=v_s5_b_re,
             s5_b_im=v_s5_b_im, s5_c_re=v_s5_c_re, s5_c_im=v_s5_c_im, s5_d=v_s5_d, s5_w_glu=v_s5_w_glu,
             s5_b_glu=v_s5_b_glu, s5_w_out=v_s5_w_out, final_g=v_final_g)

    me = 4 * lax.axis_index("x") + 2 * lax.axis_index("y") + lax.axis_index("c")
    WA = ada_w.shape[2]

    def shard(n):
        return _t_shard(w[n], SHARD_ROWS[n]) if n in COL_SHARDED else w[n][0].astype(BF16)

    wgot = exchange([c] + [shard(n) for n in L0_BIG], "gather", "gather_w")

    cg = wgot[0].reshape(NDEV, D)
    cc2 = c_ctx.reshape(1, D)
    ada_b_loc = lax.dynamic_slice_in_dim(ada_b.reshape(2, 3 * D // WA, WA), me, 1, axis=1)
    part = ada_fwd(cg, cc2, ada_w, ada_b_loc, "ada_fwd")
    pg = exchange([part], "gather", "gather_mod")[0]
    mod_l = lax.dynamic_index_in_dim(pg, me, axis=2, keepdims=False).transpose(1, 0, 2).reshape(2, 3 * D)
    mod_c = pg[:, :, NDEV, :].transpose(1, 0, 2).reshape(2, 3 * D)
    mod = jnp.stack([mod_c, mod_l], axis=1)

    Wt = {n: a.reshape(-1, a.shape[-1]) for n, a in zip(L0_BIG, wgot[1:])}
    Wt["mla_w_in"] = mm(_win_order(), Wt["mla_w_in"], "nn", "w_in_order", out_dtype=BF16)
    vec_bits = lax.bitcast_convert_type(jnp.concatenate([s5_d, s5_b_glu], axis=0), BITS16).reshape(2, -1)
    small = {n: w[n] for n in SMALL_RS}

    lvec, grad_x, dmod, gbig, gsmall, l1_recv = local_step(ctx[0], x[0], loss_target[0], mod, Wt, small,
                                                           [shard(n) for n in L1_BIG] + [vec_bits])
    grad_x = grad_x[None]

    per_dev = G // NDEV
    recv = dict(zip(L0_BIG, exchange([gbig[n] for n in L0_BIG], "lead", "scatter_grads")))
    recv.update(dict(zip(L1_BIG + VEC_SHARDED, l1_recv)))
    out = {}

    def keep(n, res):
        for key, arr in zip("gdmv", res):
            out[key, n] = arr.reshape(w[n].shape)

    for n in BIG:
        keep(n, adamw(recv[n], w[n][0], m[n][0], v[n][0], "adamw_" + n))
    reduced = sum_slots(l1_recv[len(L1_BIG + VEC_SHARDED):], "sum_chunks")

    kshape = lambda n: w[n].shape if w[n].ndim > 1 else (1, w[n].size)
    flat = jnp.concatenate([gsmall[n].reshape(-1) for n in TINY] + [dmod.reshape(-1), lvec.reshape(-1)])[None]
    bb_all, cc_all, flat_all = exchange([jnp.stack(reduced[:2]), jnp.stack(reduced[2:]), flat], "gather", "gather_small")
    chunk_all = [bb_all[:, 0], bb_all[:, 1], cc_all[:, 0], cc_all[:, 1]]
    tiny_all, off = [], 0
    for n in TINY:
        tiny_all.append(flat_all[:, 0, off:off + w[n].size].reshape((NDEV,) + kshape(n)))
        off += w[n].size
    dm_all = flat_all[:, 0, off:off + dmod.size].reshape((NDEV,) + dmod.shape)
    loss = sum_slots([flat_all[:, :, off + dmod.size:]], "loss_sum")[0][0, 0]

    dm_cols = lax.dynamic_slice_in_dim(dm_all.reshape(NDEV, 2, 2, 3 * D // WA, WA), me, 1, axis=3)[:, :, :, 0]
    dm_loc = jnp.concatenate([dm_cols[:, :, 1].transpose(1, 0, 2), dm_cols[:, :, 0].transpose(1, 0, 2)], axis=1)
    g_ada_w, dcc_part, g_ada_b = ada_bwd(cg, cc2, ada_w, dm_loc, dm_all.transpose(0, 2, 1, 3).reshape(2 * NDEV, 2, 3 * D), "ada_bwd")
    dcc_all = exchange([dcc_part], "gather", "gather_dcc")[0].reshape(NDEV, D)
    g_c_ctx = cctx_finish(dcc_all, cc2, "cctx_finish")

    flat2 = lambda t: t.reshape(-1, t.shape[-1])
    keep("ada_w", adamw(flat2(g_ada_w)[None], flat2(ada_w), flat2(m_ada_w), flat2(v_ada_w), "adamw_ada"))
    items = []
    for n, g in zip(CHUNKED, chunk_all):
        blk = (1, 1, per_dev) + w[n].shape[3:]
        if n in DENSE:
            g = g.transpose(1, 0, 2, 3).reshape(w[n].shape)
            g_spec = pl.BlockSpec((1, 1, 1) + blk[2:], lambda d, s: (0, 0, d, s, 0, 0))
        else:
            g_spec = pl.BlockSpec((1, 1, 1) + blk[2:], lambda d, s: (0, s, d, 0, 0, 0))
        items.append((g[None], g_spec, w[n], m[n], v[n], pl.BlockSpec(blk, lambda d, s: (0, d, s, 0, 0))))
    for n, res in zip(CHUNKED, adamw_multi(items, (2, NDEV), "adamw_bc")):
        keep(n, res)
    tiny_g = dict(zip(TINY, tiny_all))
    tiny_g.update({n: recv[n] for n in VEC_SHARDED})
    tiny_g["c_ctx"], tiny_g["ada_b"] = g_c_ctx[None], g_ada_b[None]
    names = list(tiny_g)
    items = [(tiny_g[n], _whole(tiny_g[n], 1)) + tuple(t[n].reshape(kshape(n)) for t in (w, m, v))
             + (pl.BlockSpec(kshape(n), lambda i, r=len(kshape(n)): (0,) * r),) for n in names]
    for n, res in zip(names, adamw_multi(items, (1,), "adamw_small")):
        keep(n, res)

    return (loss, grad_x, *[out["g", n] for n in ORDER], *[out["d", n] for n in ORDER],
            *[out["m", n] for n in ORDER], *[out["v", n] for n in ORDER])
```

```python
import math

import numpy as np
import jax
import jax.numpy as jnp
from jax import lax
from jax.experimental import pallas as pl
from jax.experimental.pallas import tpu as pltpu

F32 = jnp.float32
BF16 = jnp.bfloat16

D = 1024
L = 2048
LC = 256
NDEV = 8
GRID_W = 64
EPS = 1e-6
HEADS = 16
NOPE = 64
ROPE = 32
QK = NOPE + ROPE
VD = 64
IN_W = 256 + 128 + ROPE + HEADS * 64
IN_WP = 1536
QL = 256
KVL = 128
SCALE = QK ** -0.5
THETA = 10000.0
G = 64
P = 64
CH = 16
GB = 8
NJ = G // GB
UB = GB * CH
SB = GB * P
SEG = 8
TB = 256
VMEM_LIMIT = 56 * 1024 * 1024
B1, B2, LR, AEPS, WD, STEP = 0.9, 0.999, 0.001, 1e-8, 0.01, 10
MESH_T = pl.DeviceIdType.MESH


def _cp(sem=None):
    return pltpu.CompilerParams(dimension_semantics=sem, vmem_limit_bytes=VMEM_LIMIT)


def _sig(x):
    return 1.0 / (1.0 + jnp.exp(-x))


def _silu(x):
    return x * _sig(x)


def _dsilu(x):
    s = _sig(x)
    return s * (1.0 + x * (1.0 - s))


_GK = math.sqrt(2.0 / math.pi)


def _gelu(x):
    return 0.5 * x * (1.0 + jnp.tanh(_GK * (x + 0.044715 * x * x * x)))


def _dgelu(x):
    t = jnp.tanh(_GK * (x + 0.044715 * x * x * x))
    return 0.5 * (1.0 + t) + 0.5 * x * (1.0 - t * t) * _GK * (1.0 + 3 * 0.044715 * x * x)


def _rs(x):
    return lax.rsqrt(jnp.mean(x * x, axis=-1, keepdims=True) + EPS)


def _sum0(x):
    return jnp.sum(x, axis=0, keepdims=True)


def st_norm_mod(x, g, sc, sh):
    y = x * _rs(x) * g
    return (y * (1.0 + sc) + sh,), ()


def st_norm_mod_bwd(x, dh, dres, g, sc):
    r = _rs(x)
    xn = x * r
    y = xn * g
    dy = dh * (1.0 + sc)
    dxn = dy * g
    dx = r * (dxn - xn * jnp.mean(dxn * xn, axis=-1, keepdims=True))
    return (dres + dx,), (_sum0(dh), _sum0(dh * y), _sum0(dy * xn))


def st_rms(x, g):
    return (x * _rs(x) * g,), ()


def st_rms_bwd(x, dy, g):
    r = _rs(x)
    n = x * r
    dn = dy * g
    dx = r * (dn - n * jnp.mean(dn * n, axis=-1, keepdims=True))
    return (dx,), (_sum0(dy * n),)


def st_rms2(x1, x2, g1, g2):
    return st_rms(x1, g1)[0] + st_rms(x2, g2)[0], ()


def st_rms2_bwd(x1, dy1, x2, dy2, g1, g2):
    (d1,), (s1,) = st_rms_bwd(x1, dy1, g1)
    (d2,), (s2,) = st_rms_bwd(x2, dy2, g2)
    return (d1, d2), (s1, s2)


def st_gate(o, z):
    return (o * _silu(z),), ()


def st_gate_bwd(dog, o, z):
    return (dog * _silu(z), dog * o * _dsilu(z)), ()


def st_resid(x, out, gt):
    return (x + gt * out,), ()


def st_resid_bwd(dx, out, gt):
    return (dx * gt,), (_sum0(dx * out),)


def st_s5a(yssm, u, d):
    y = yssm + d * u
    return (y, _gelu(y)), ()


def st_s5b(y, gl, z, b):
    return (_gelu(y) * _sig(gl + b) * _silu(z),), ()


def st_s5b_bwd(dy3, y, gl, z, b):
    y1 = _gelu(y)
    s = _sig(gl + b)
    dy2 = dy3 * _silu(z)
    dz = dy3 * y1 * s * _dsilu(z)
    dgl = dy2 * y1 * s * (1.0 - s)
    return (dgl, dz, dy2 * s), (_sum0(dgl),)


def st_s5a_bwd(dy1a, dy1b, y, u, d):
    dy = (dy1a + dy1b) * _dgelu(y)
    return (dy, dy * d), (_sum0(dy * u),)


def st_l0_pre(x, g, sc, sh, qg, kvg, w_in):
    hb = st_norm_mod(x, g, sc, sh)[0][0].astype(BF16)
    p = lax.dot_general(hb, w_in, _DN["nt"], preferred_element_type=F32)
    cq, ckv = p[:, HEADS * VD:HEADS * VD + QL], p[:, HEADS * VD + QL:HEADS * VD + QL + KVL]
    return (hb, p) + st_rms2(cq, ckv, qg, kvg)[0], ()


def st_l0_tail_bwd(dq, dkv, dkr, dz, cq, ckv, cqn, ckvn, h, x, dres, qg, kvg, g, sc, w_uq, w_ukv, w_in):
    dcqn = jnp.dot(dq, w_uq, preferred_element_type=F32)
    dckvn = jnp.dot(dkv, w_ukv, preferred_element_type=F32)
    (dcq, dckv), (dqg, dkvg) = st_rms2_bwd(cq, dcqn, ckv, dckvn, qg, kvg)
    dp = jnp.concatenate([dz, dcq, dckv, dkr], axis=1).astype(BF16)
    dh = jnp.dot(dp, w_in, preferred_element_type=F32)
    outs, sums = st_norm_mod_bwd(x, dh, dres, g, sc)
    tn = lambda a, b: lax.dot_general(a, b, _DN["tn"], preferred_element_type=F32)
    return outs, (dqg, dkvg) + sums, (tn(cqn, dq), tn(ckvn, dkv), tn(h, dp))


def st_l1_pre(x, g, sc, sh, w_in):
    hb = st_norm_mod(x, g, sc, sh)[0][0].astype(BF16)
    return (hb, lax.dot_general(hb, w_in, _DN["nt"], preferred_element_type=F32)), ()


def st_l1_tail_bwd(du_a, du_b, dz, h, x, dres, g, sc, w_in):
    dp = jnp.concatenate([(du_a + du_b).astype(BF16), dz], axis=1)
    dh = jnp.dot(dp, w_in, preferred_element_type=F32)
    outs, sums = st_norm_mod_bwd(x, dh, dres, g, sc)
    return outs, sums, (lax.dot_general(h, dp, _DN["tn"], preferred_element_type=F32),)


def st_l0_post(o, z, x, gt, w_out):
    og = (o * _silu(z)).astype(BF16)
    out = jnp.dot(og, w_out, preferred_element_type=F32)
    return (og, out, x + gt * out), ()


def st_l0_post_bwd(dx1, out, og, o, z, gt, w_out):
    (dout,), (dgt,) = st_resid_bwd(dx1, out, gt)
    doutb = dout.astype(BF16)
    dog = lax.dot_general(doutb, w_out, _DN["nt"], preferred_element_type=F32)
    return st_gate_bwd(dog, o, z)[0], (dgt,), (lax.dot_general(og, doutb, _DN["tn"], preferred_element_type=F32),)


def st_l1_mlp(yssm, u, z, x1, tgt, d, bglu, gt, fg, mask, w_glu, w_out):
    (y, y1), _ = st_s5a(yssm, u, d)
    y1b = y1.astype(BF16)
    gl = jnp.dot(y1b, w_glu, preferred_element_type=F32)
    y3 = (y1 * _sig(gl + bglu) * _silu(z)).astype(BF16)
    out = jnp.dot(y3, w_out, preferred_element_type=F32)
    (dx2,), sums = st_final(x1 + gt * out, tgt, fg, mask)
    return (y, y1b, gl, y3, out, dx2), sums


def st_l1_mlp_bwd(dx2, out, y3, y, gl, z, u, y1b, gt, bglu, d, w_out, w_glu):
    (dout,), (dgt,) = st_resid_bwd(dx2, out, gt)
    doutb = dout.astype(BF16)
    dy3 = lax.dot_general(doutb, w_out, _DN["nt"], preferred_element_type=F32)
    (dgl, dz, dy1a), (dbglu,) = st_s5b_bwd(dy3, y, gl, z, bglu)
    dglb = dgl.astype(BF16)
    dy1b = lax.dot_general(dglb, w_glu, _DN["nt"], preferred_element_type=F32)
    (dy, du), (dd,) = st_s5a_bwd(dy1a, dy1b, y, u, d)
    g_w_out = lax.dot_general(y3, doutb, _DN["tn"], preferred_element_type=F32)
    g_w_glu = lax.dot_general(y1b, dglb, _DN["tn"], preferred_element_type=F32)
    return (dz, dy, du), (dgt, dbglu, dd), (g_w_out, g_w_glu)


def st_final(x2, tgt, g, mask):
    r = _rs(x2)
    n = x2 * r
    e = n * g - tgt
    dyo = e * (1.0 / D)
    dn = dyo * g
    dx = r * (dn - n * jnp.mean(dn * n, axis=-1, keepdims=True))
    lsum = jnp.sum(_sum0(e * e), axis=1, keepdims=True) * (0.5 / D)
    return (dx * mask,), (_sum0(dyo * n), jnp.broadcast_to(lsum, (1, 128)))


def rowwise(fn, rows, vecs, out_rows, out_sums, name, mats=(), out_accs=()):
    lat_blk = lambda i: jnp.maximum(i - 1, 0)
    arrays, in_specs, pick = [], [], []
    for a in rows:
        if not isinstance(a, tuple):
            a = (a, 0, a.shape[1])
        tag = a[0] if isinstance(a[0], str) else None
        if tag == "cat":
            _, ctx, x = a
            arrays += [ctx, x]
            in_specs += [pl.BlockSpec((TB, ctx.shape[1]), lambda i: (0, 0)),
                         pl.BlockSpec((TB, x.shape[1]), lambda i: (lat_blk(i), 0))]
            pick.append(2)
        elif tag == "lat":
            arrays.append(a[1])
            in_specs.append(pl.BlockSpec((TB, a[1].shape[1]), lambda i: (lat_blk(i), 0)))
            pick.append(1)
        else:
            arr, cb, width = a
            arrays.append(arr)
            in_specs.append(pl.BlockSpec((TB, width), lambda i, cb=cb: (i, cb)))
            pick.append(1)
    T = LC + L
    nin, nv, nm, no, ns = len(arrays), len(vecs), len(mats), len(out_rows), len(out_sums)

    def body(*refs):
        i = pl.program_id(0)
        vals, k = [], 0
        for p in pick:
            if p == 2:
                vals.append(jnp.where(i == 0, refs[k][...], refs[k + 1][...]))
            else:
                vals.append(refs[k][...])
            k += p
        vals += [r[0] for r in refs[nin:nin + nv]] + [r[...] for r in refs[nin + nv:nin + nv + nm]]
        res = fn(*vals)
        first_out = nin + nv + nm
        for r, o in zip(refs[first_out:first_out + no], res[0]):
            r[...] = o.astype(r.dtype)
        sum_refs = refs[first_out + no:first_out + no + ns]
        if sum_refs:
            @pl.when(i <= 1)
            def _():
                for r in sum_refs:
                    r[...] = jnp.zeros_like(r)
            for r, s in zip(sum_refs, res[1]):
                r[0] += s
        acc_refs = refs[first_out + no + ns:]
        if acc_refs:
            @pl.when(i == 0)
            def _():
                for r in acc_refs:
                    r[...] = jnp.zeros_like(r)
            for r, a in zip(acc_refs, res[2]):
                r[...] += a

    kind = lambda i: (jnp.minimum(i, 1), 0, 0)
    in_specs += [pl.BlockSpec((1, 1, v.shape[2]), kind) for v in vecs]
    in_specs += [pl.BlockSpec(m.shape, lambda i: (0, 0), pipeline_mode=pl.Buffered(1)) for m in mats]
    out_specs, out_shape = [], []
    for o in out_rows:
        lat = len(o) == 3
        out_specs.append(pl.BlockSpec((TB, o[0]), (lambda i: (lat_blk(i), 0)) if lat else (lambda i: (i, 0))))
        out_shape.append(jax.ShapeDtypeStruct((L if lat else T, o[0]), o[1]))
    out_specs += [pl.BlockSpec((1, 1, c), kind) for c in out_sums]
    out_shape += [jax.ShapeDtypeStruct((2, 1, c), F32) for c in out_sums]
    out_specs += [pl.BlockSpec(s, lambda i: (0, 0)) for s in out_accs]
    out_shape += [jax.ShapeDtypeStruct(s, F32) for s in out_accs]
    res = pl.pallas_call(body, grid=(T // TB,), in_specs=in_specs, out_specs=out_specs, out_shape=out_shape,
                         compiler_params=_cp(("arbitrary",)), name=name)(*arrays, *vecs, *mats)
    if out_accs:
        return res[:no], res[no:no + ns], res[no + ns:]
    return res[:no], res[no:]


_DN = {"nn": (((1,), (0,)), ((), ())), "nt": (((1,), (1,)), ((), ())), "tn": (((0,), (0,)), ((), ()))}


def mm(a, b, mode, name, out_dtype=F32, tm=None, tn=None, shard_out=False):
    if mode == "nn":
        (M, K), (_, N) = a.shape, b.shape
    elif mode == "nt":
        (M, K), (N, _) = a.shape, b.shape
    else:
        (K, M), (_, N) = a.shape, b.shape
    if tm is None:
        tm = next((t for t in (768, 512, 256) if M % t == 0 and M > t), M)
    tn = N if tn is None else tn
    dn = _DN[mode]

    def body(a_ref, b_ref, o_ref):
        o_ref[...] = lax.dot_general(a_ref[...].astype(BF16), b_ref[...].astype(BF16), dn,
                                     preferred_element_type=F32).astype(o_ref.dtype)

    if shard_out:
        def body(a_ref, b_ref, o_ref):
            av = a_ref[...].astype(BF16)
            for j in range(N // tn):
                bj = b_ref[pl.ds(j * tn, tn), :] if mode == "nt" else b_ref[:, pl.ds(j * tn, tn)]
                o_ref[j] = lax.dot_general(av, bj.astype(BF16), dn, preferred_element_type=F32).astype(o_ref.dtype)

        a_spec = pl.BlockSpec((K, tm), lambda i: (0, i)) if mode == "tn" else pl.BlockSpec((tm, K), lambda i: (i, 0))
        return pl.pallas_call(body, grid=(M // tm,), in_specs=[a_spec, pl.BlockSpec(b.shape, lambda i: (0, 0))],
                              out_specs=pl.BlockSpec((N // tn, tm, tn), lambda i: (0, i, 0)),
                              out_shape=jax.ShapeDtypeStruct((N // tn, M, tn), out_dtype),
                              compiler_params=_cp(("parallel",)), name=name)(a, b)
    a_spec = pl.BlockSpec((K, tm), lambda i, j: (0, i)) if mode == "tn" else pl.BlockSpec((tm, K), lambda i, j: (i, 0))
    b_spec = pl.BlockSpec((tn, K), lambda i, j: (j, 0)) if mode == "nt" else pl.BlockSpec((K, tn), lambda i, j: (0, j))
    return pl.pallas_call(body, grid=(M // tm, N // tn), in_specs=[a_spec, b_spec],
                          out_specs=pl.BlockSpec((tm, tn), lambda i, j: (i, j)), out_shape=jax.ShapeDtypeStruct((M, N), out_dtype),
                          compiler_params=_cp(("parallel", "arbitrary")), name=name)(a, b)


def _rope_tables(T, width=QK, first=NOPE):
    nlat = T - LC
    pos = np.arange(nlat)
    row, col = pos // GRID_W, pos % GRID_W
    half = ROPE // 2
    inv = 1.0 / (THETA ** (np.arange(0, half, 2, dtype=np.float64) / half))
    cosf = np.ones((T, width), np.float64)
    sinf = np.zeros((T, width), np.float64)
    perm = np.zeros((width, width), np.float32)
    for m in range(ROPE):
        j = first + m
        blk, w = m // half, m % half
        ang = (row if blk == 0 else col)[:, None] * inv[None, :]
        f = w % (half // 2)
        cosf[LC:, j] = np.cos(ang[:, f])
        if w < half // 2:
            sinf[LC:, j] = -np.sin(ang[:, f])
            perm[j + half // 2, j] = 1.0
        else:
            sinf[LC:, j] = np.sin(ang[:, f])
            perm[j - half // 2, j] = 1.0
    return jnp.asarray(cosf, F32), jnp.asarray(sinf, F32), jnp.asarray(perm, BF16), jnp.asarray(perm.T, BF16)


def _exact_perm(x, pm):
    hi = x.astype(BF16)
    r1 = x - hi.astype(F32)
    mid = r1.astype(BF16)
    lo = (r1 - mid.astype(F32)).astype(BF16)
    dot = lambda a: jnp.dot(a, pm, preferred_element_type=F32)
    return dot(hi) + dot(mid) + dot(lo)


def _rot(x, cv, sv, pv, inverse):
    if inverse:
        return x * cv + _exact_perm(x * sv, pv)
    return x * cv + _exact_perm(x, pv) * sv


def rope(x, cosf, sinf, pm, inverse, out_dtype, name, scale=1.0):
    H, T, _ = x.shape

    def body(x_ref, c_ref, s_ref, p_ref, o_ref):
        cv, sv, pv = c_ref[...], s_ref[...], p_ref[...]
        for h in range(H):
            o_ref[h] = (_rot(x_ref[h], cv, sv, pv, inverse) * scale).astype(o_ref.dtype)

    return pl.pallas_call(
        body, grid=(T // TB,),
        in_specs=[pl.BlockSpec((H, TB, QK), lambda i: (0, i, 0)), pl.BlockSpec((TB, QK), lambda i: (i, 0)),
                  pl.BlockSpec((TB, QK), lambda i: (i, 0)), pl.BlockSpec((QK, QK), lambda i: (0, 0))],
        out_specs=pl.BlockSpec((H, TB, QK), lambda i: (0, i, 0)), out_shape=jax.ShapeDtypeStruct((H, T, QK), out_dtype),
        compiler_params=_cp(("parallel",)), name=name)(x, cosf, sinf, pm)


KVW = NOPE + VD


def _kv_selectors():
    s_kn = np.zeros((KVW, QK), np.float32)
    s_kr = np.zeros((128, QK), np.float32)
    s_v = np.zeros((KVW, VD), np.float32)
    for l in range(NOPE):
        s_kn[l, l] = 1.0
    for l in range(ROPE):
        s_kr[l, NOPE + l] = 1.0
    for l in range(VD):
        s_v[NOPE + l, l] = 1.0
    return s_kn, s_kr, s_v


def project_q(cqn, w, cosf, sinf, pm, name):
    T = cqn.shape[0]

    def body(a_ref, w_ref, c_ref, s_ref, p_ref, o_ref):
        a, cv, sv, pv = a_ref[...], c_ref[...], s_ref[...], p_ref[...]
        for h in range(HEADS):
            qh = _dotf(a, w_ref[pl.ds(h * QK, QK), :], "nt")
            o_ref[h] = (_rot(qh, cv, sv, pv, False) * SCALE).astype(BF16)

    rows = lambda c: pl.BlockSpec((TB, c), lambda i: (i, 0))
    const = lambda x: pl.BlockSpec(x.shape, lambda i: (0, 0))
    return pl.pallas_call(
        body, grid=(T // TB,), in_specs=[rows(QL), const(w), rows(QK), rows(QK), const(pm)],
        out_specs=pl.BlockSpec((HEADS, TB, QK), lambda i: (0, i, 0)), out_shape=jax.ShapeDtypeStruct((HEADS, T, QK), BF16),
        compiler_params=_cp(("parallel",)), name=name)(cqn, w, cosf, sinf, pm)


def project_kv(ckvn, w, p0, kr_block, name):
    T = ckvn.shape[0]
    cosf, sinf, pm, _ = _rope_tables(T, 128, 0)
    s_kn, s_kr, s_v = (jnp.asarray(s, BF16) for s in _kv_selectors())

    def body(a_ref, w_ref, kr_ref, c_ref, s_ref, p_ref, skn_ref, skr_ref, sv_ref, k_ref, v_ref):
        a = a_ref[...]
        krr = _rot(kr_ref[...], c_ref[...], s_ref[...], p_ref[...], False).astype(BF16)
        kr_part = jnp.dot(krr, skr_ref[...], preferred_element_type=F32)
        for h in range(HEADS):
            kvb = _dotf(a, w_ref[pl.ds(h * KVW, KVW), :], "nt").astype(BF16)
            k_ref[h] = (jnp.dot(kvb, skn_ref[...], preferred_element_type=F32) + kr_part).astype(BF16)
            v_ref[h] = jnp.dot(kvb, sv_ref[...], preferred_element_type=F32).astype(BF16)

    rows = lambda c: pl.BlockSpec((TB, c), lambda i: (i, 0))
    const = lambda x: pl.BlockSpec(x.shape, lambda i: (0, 0))
    return pl.pallas_call(
        body, grid=(T // TB,),
        in_specs=[rows(KVL), const(w), pl.BlockSpec((TB, 128), lambda i: (i, kr_block)),
                  rows(128), rows(128), const(pm), const(s_kn), const(s_kr), const(s_v)],
        out_specs=[pl.BlockSpec((HEADS, TB, QK), lambda i: (0, i, 0)), pl.BlockSpec((HEADS, TB, VD), lambda i: (0, i, 0))],
        out_shape=[jax.ShapeDtypeStruct((HEADS, T, QK), BF16), jax.ShapeDtypeStruct((HEADS, T, VD), BF16)],
        compiler_params=_cp(("parallel",)), name=name)(ckvn, w, p0, cosf, sinf, pm, s_kn, s_kr, s_v)


def split_kv_grads(dk, dv, name):
    H, T, _ = dk.shape
    cosf, sinf, _, pmt = _rope_tables(T, 128, 0)
    s_kn, s_kr, s_v = _kv_selectors()
    s_knt, s_krt, s_vt = (jnp.asarray(s.T, BF16) for s in (s_kn, s_kr, s_v))

    def body(dk_ref, dv_ref, c_ref, s_ref, p_ref, skn_ref, skr_ref, sv_ref, dkv_ref, dkr_ref):
        total = None
        for h in range(H):
            dkh = dk_ref[h]
            total = dkh if total is None else total + dkh
            dkv_ref[:, pl.ds(h * KVW, KVW)] = (
                jnp.dot(dkh.astype(BF16), skn_ref[...], preferred_element_type=F32)
                + jnp.dot(dv_ref[h].astype(BF16), sv_ref[...], preferred_element_type=F32)).astype(BF16)
        dkr_ref[...] = _rot(_exact_perm(total, skr_ref[...]), c_ref[...], s_ref[...], p_ref[...], True)

    rows = lambda c: pl.BlockSpec((TB, c), lambda i: (i, 0))
    const = lambda a: pl.BlockSpec(a.shape, lambda i: (0, 0))
    return pl.pallas_call(
        body, grid=(T // TB,),
        in_specs=[pl.BlockSpec((H, TB, QK), lambda i: (0, i, 0)), pl.BlockSpec((H, TB, VD), lambda i: (0, i, 0)),
                  rows(128), rows(128), const(pmt), const(s_knt), const(s_krt), const(s_vt)],
        out_specs=[rows(H * KVW), rows(128)],
        out_shape=[jax.ShapeDtypeStruct((T, H * KVW), BF16), jax.ShapeDtypeStruct((T, 128), F32)],
        compiler_params=_cp(("parallel",)), name=name)(dk, dv, cosf, sinf, pmt, s_knt, s_krt, s_vt)


def _by_query_block(run, T):
    @pl.when(pl.program_id(1) == 0)
    def _():
        run(LC)

    @pl.when(pl.program_id(1) > 0)
    def _():
        run(T)


def _with_rider(body, nin, nout, ride, grid):
    if ride is None:
        return body
    n = ride.n

    def wrapped(*refs):
        ins, xs = refs[:nin], refs[nin:nin + n]
        outs, got = refs[nin + n:nin + n + nout], refs[nin + n + nout:nin + 2 * n + nout]
        sems = refs[nin + 2 * n + nout:]
        step = pl.program_id(0) * grid[1] + pl.program_id(1)

        @pl.when(step == 0)
        def _():
            ride.start(xs, got, sems)

        body(*ins, *outs)

        @pl.when(step == grid[0] * grid[1] - 1)
        def _():
            ride.finish(xs, got, sems)

    return wrapped


def _ride_call(body, grid, in_specs, out_specs, out_shape, ride, rode, name, args):
    if ride is None:
        return pl.pallas_call(body, grid=grid, in_specs=in_specs, out_specs=out_specs, out_shape=out_shape,
                              compiler_params=_cp(("parallel", "arbitrary")), name=name)(*args), []
    res = pl.pallas_call(
        _with_rider(body, len(in_specs), len(out_specs), ride, grid), grid=grid,
        in_specs=in_specs + ride.specs, out_specs=out_specs + ride.specs, out_shape=out_shape + ride.out_shape,
        scratch_shapes=ride.scratch,
        compiler_params=pltpu.CompilerParams(dimension_semantics=("arbitrary", "arbitrary"), vmem_limit_bytes=VMEM_LIMIT,
                                             has_side_effects=True), name=name)(*args, *rode)
    return res[:len(out_specs)], res[len(out_specs):]


def attn_fwd(q, k, v, name, rode=None, modes=None):
    H, T, _ = q.shape

    def body(q_ref, k_ref, v_ref, o_ref, lse_ref):
        def run(nk):
            s = _dotf(q_ref[0], k_ref[0, pl.ds(0, nk), :], "nt")
            m = jnp.max(s, axis=1, keepdims=True)
            p = jnp.exp(s - m)
            l = jnp.sum(p, axis=1, keepdims=True)
            o = jnp.dot(p.astype(BF16), v_ref[0, pl.ds(0, nk), :], preferred_element_type=F32)
            o_ref[0] = o / l
            lse_ref[0] = m + jnp.log(l)

        _by_query_block(run, T)

    return _ride_call(
        body, (H, T // TB),
        [pl.BlockSpec((1, TB, QK), lambda h, i: (h, i, 0)), pl.BlockSpec((1, T, QK), lambda h, i: (h, 0, 0)),
         pl.BlockSpec((1, T, VD), lambda h, i: (h, 0, 0))],
        [pl.BlockSpec((1, TB, VD), lambda h, i: (h, i, 0)), pl.BlockSpec((1, TB, 1), lambda h, i: (h, i, 0))],
        [jax.ShapeDtypeStruct((H, T, VD), F32), jax.ShapeDtypeStruct((H, T, 1), F32)],
        Exchange(rode, modes) if rode else None, rode, name, (q, k, v))


def attn_bwd(q, k, v, o, lse, do, name, rode=None, modes=None):
    H, T, _ = q.shape

    def body(q_ref, k_ref, v_ref, o_ref, lse_ref, do_ref, dq_ref, dk_ref, dv_ref):
        i = pl.program_id(1)

        @pl.when(i == 0)
        def _():
            dk_ref[...] = jnp.zeros_like(dk_ref)
            dv_ref[...] = jnp.zeros_like(dv_ref)

        def run(nk):
            keys = pl.ds(0, nk)
            qv, kv, dov = q_ref[0], k_ref[0, keys, :], do_ref[0]
            p = jnp.exp(_dotf(qv, kv, "nt") - lse_ref[0])
            delta = jnp.sum(dov * o_ref[0], axis=1, keepdims=True)
            dob = dov.astype(BF16)
            dv_ref[0, keys, :] += _dotf(p.astype(BF16), dob, "tn")
            dp = _dotf(dob, v_ref[0, keys, :], "nt")
            ds = (p * (dp - delta)).astype(BF16)
            dq_ref[0] = jnp.dot(ds, kv, preferred_element_type=F32)
            dk_ref[0, keys, :] += _dotf(ds, qv, "tn")

        _by_query_block(run, T)

    blk = lambda c: pl.BlockSpec((1, TB, c), lambda h, i: (h, i, 0))
    full = lambda c: pl.BlockSpec((1, T, c), lambda h, i: (h, 0, 0))
    return _ride_call(
        body, (H, T // TB), [blk(QK), full(QK), full(VD), blk(VD), blk(1), blk(VD)], [blk(QK), full(QK), full(VD)],
        [jax.ShapeDtypeStruct((H, T, QK), F32), jax.ShapeDtypeStruct((H, T, QK), F32), jax.ShapeDtypeStruct((H, T, VD), F32)],
        Exchange(rode, modes) if rode else None, rode, name, (q, k, v, o, lse, do))


def disc_fwd(a_re, a_im, ls, name):
    def body(ar_ref, ai_ref, ls_ref, lr_ref, li_ref, fr_ref, fi_ref):
        ar, ai = ar_ref[...], ai_ref[...]
        dt = jnp.exp(ls_ref[...])
        mag = jnp.exp(ar * dt)
        lr = mag * jnp.cos(ai * dt)
        li = mag * jnp.sin(ai * dt)
        den = ar * ar + ai * ai
        nr = lr - 1.0
        lr_ref[...] = lr
        li_ref[...] = li
        fr_ref[...] = (nr * ar + li * ai) / den
        fi_ref[...] = (li * ar - nr * ai) / den

    return pl.pallas_call(body, out_shape=[jax.ShapeDtypeStruct(a_re.shape, F32)] * 4, name=name)(a_re, a_im, ls)


def disc_b(f_re, f_im, b_re, b_im, name):
    def body(fr_ref, fi_ref, br_ref, bi_ref, or_ref, oi_ref):
        fr, fi, br, bi = fr_ref[...], fi_ref[...], br_ref[...], bi_ref[...]
        or_ref[...] = fr * br - fi * bi
        oi_ref[...] = fr * bi + fi * br

    fs, bs = _disc_b_specs()
    return pl.pallas_call(body, grid=(2, G * P // DISC_ROWS), in_specs=[fs, fs, bs, bs], out_specs=[bs, bs],
                          out_shape=[jax.ShapeDtypeStruct(b_re.shape, F32)] * 2, name=name)(f_re, f_im, b_re, b_im)


DISC_ROWS = G * P


def _disc_b_specs():
    return (pl.BlockSpec((1, DISC_ROWS, 1), lambda d, i: (d, i, 0)), pl.BlockSpec((1, DISC_ROWS, CH), lambda d, i: (d, i, 0)))


def disc_b_bwd(f_re, f_im, b_re, b_im, dbb_re, dbb_im, name):
    def body(fr_ref, fi_ref, br_ref, bi_ref, dr_ref, di_ref, dbr_ref, dbi_ref, dfr_ref, dfi_ref):
        fr, fi, br, bi, dr, di = fr_ref[...], fi_ref[...], br_ref[...], bi_ref[...], dr_ref[...], di_ref[...]
        dbr_ref[...] = fr * dr + fi * di
        dbi_ref[...] = fr * di - fi * dr
        dfr_ref[...] = jnp.sum(dr * br + di * bi, axis=-1, keepdims=True)
        dfi_ref[...] = jnp.sum(di * br - dr * bi, axis=-1, keepdims=True)

    fs, bs = _disc_b_specs()
    return pl.pallas_call(body, grid=(2, G * P // DISC_ROWS), in_specs=[fs, fs, bs, bs, bs, bs], out_specs=[bs, bs, fs, fs],
                          out_shape=[jax.ShapeDtypeStruct(b_re.shape, F32)] * 2 + [jax.ShapeDtypeStruct(f_re.shape, F32)] * 2,
                          name=name)(f_re, f_im, b_re, b_im, dbb_re, dbb_im)


def disc_a_bwd(a_re, a_im, ls, dlr, dli, dfr, dfi, name):
    def body(ar_ref, ai_ref, ls_ref, dlr_ref, dli_ref, dfr_ref, dfi_ref, dar_ref, dai_ref, dls_ref):
        ar, ai = ar_ref[...], ai_ref[...]
        dt = jnp.exp(ls_ref[...])
        mag = jnp.exp(ar * dt)
        cs, sn = jnp.cos(ai * dt), jnp.sin(ai * dt)
        lr, li = mag * cs, mag * sn
        den = ar * ar + ai * ai
        nr = lr - 1.0
        f_re = (nr * ar + li * ai) / den
        f_im = (li * ar - nr * ai) / den
        dn1 = dfr_ref[...] / den
        dn2 = dfi_ref[...] / den
        dden = -(dfr_ref[...] * f_re + dfi_ref[...] * f_im) / den
        dlr_t = dlr_ref[...] + dn1 * ar - dn2 * ai
        dli_t = dli_ref[...] + dn1 * ai + dn2 * ar
        dar = dn1 * nr + dn2 * li + dden * 2.0 * ar
        dai = dn1 * li - dn2 * nr + dden * 2.0 * ai
        dmag = dlr_t * cs + dli_t * sn
        dth = dli_t * lr - dlr_t * li
        dar_ref[...] = dar + dmag * mag * dt
        dai_ref[...] = dai + dth * dt
        dls_ref[...] = jnp.sum(dmag * mag * ar + dth * ai, axis=-1, keepdims=True) * dt

    return pl.pallas_call(body, out_shape=[jax.ShapeDtypeStruct(a_re.shape, F32)] * 2 +
                          [jax.ShapeDtypeStruct(ls.shape, F32)], name=name)(a_re, a_im, ls, dlr, dli, dfr, dfi)


def _cpow(lr, li, n):
    rr, ri = None, None
    br, bi = lr, li
    while n:
        if n & 1:
            if rr is None:
                rr, ri = br, bi
            else:
                rr, ri = rr * br - ri * bi, rr * bi + ri * br
        n >>= 1
        if n:
            br, bi = br * br - bi * bi, 2.0 * br * bi
    return rr, ri


UNROLL = 4


def _seg_scan(xre, xim, lam8, pw, base, seglen, rev, init, fin_re, fin_im, ini_re, ini_im, prev=None):
    lr, li = lam8

    def rows(t):
        return pl.ds(pl.multiple_of(base + t * SEG, SEG), SEG)

    tmap = (lambda n: seglen - 1 - n) if rev else (lambda n: n)
    zero = jnp.zeros((SEG, SB), F32)

    def advance(c, t):
        a, b = c
        return lr * a - li * b + xre[rows(t), :], lr * b + li * a + xim[rows(t), :]

    fin = lax.fori_loop(0, seglen, lambda n, c: advance(c, tmap(n)), (zero, zero), unroll=UNROLL)
    fin_re[...] = fin[0]
    fin_im[...] = fin[1]
    (cr, ci), (pr, pi) = init, pw
    for i in (range(SEG - 1, -1, -1) if rev else range(SEG)):
        ini_re[pl.ds(i, 1), :] = cr
        ini_im[pl.ds(i, 1), :] = ci
        cr, ci = pr * cr - pi * ci + fin_re[pl.ds(i, 1), :], pr * ci + pi * cr + fin_im[pl.ds(i, 1), :]
    start = (ini_re[...], ini_im[...])

    def store(c, t):
        na, nb = advance(c, t)
        xre[rows(t), :] = na
        xim[rows(t), :] = nb
        return na, nb

    if prev is None:
        lax.fori_loop(0, seglen, lambda n, c: store(c, tmap(n)), start, unroll=UNROLL)
        return (cr, ci), None

    sre, sim, s_ini_re, s_ini_im = prev

    def acc_step(c, t, pre, pim):
        na, nb = store(c[:2], t)
        return na, nb, c[2] + na * pre + nb * pim, c[3] + nb * pre - na * pim

    def body(n, c):
        t = tmap(n)
        tp = t - 1 if rev else t + 1
        return acc_step(c, t, sre[rows(tp), :], sim[rows(tp), :])

    c = lax.fori_loop(0, seglen - 1, body, start + (zero, zero), unroll=UNROLL)
    c = acc_step(c, 0 if rev else seglen - 1, s_ini_re[...], s_ini_im[...])
    return (cr, ci), c[2:]


def _lam_tiles(lr, li, lens, conj=False):
    if conj:
        li = -li
    lam8 = (jnp.broadcast_to(lr, (SEG, SB)), jnp.broadcast_to(li, (SEG, SB)))
    return lam8, [_cpow(lr, li, n) for n in lens]


def _stretches(T):
    return ((0, LC // SEG), (LC, (T - LC) // SEG))


def _to_seg_order(src, dst, T):
    for base, seglen in _stretches(T):
        def body(t, carry, base=base, seglen=seglen):
            dst[pl.ds(pl.multiple_of(base + t * SEG, SEG), SEG), :] = src[pl.ds(base + t, SEG, stride=seglen), :]
            return carry
        lax.fori_loop(0, seglen, body, 0, unroll=8)


def _from_seg_order(src, dst, T):
    for base, seglen in _stretches(T):
        def body(t, carry, base=base, seglen=seglen):
            dst[pl.ds(base + t, SEG, stride=seglen), :] = src[pl.ds(pl.multiple_of(base + t * SEG, SEG), SEG), :]
            return carry
        lax.fori_loop(0, seglen, body, 0, unroll=8)


def _scan_specs(T):
    ublk = pl.BlockSpec((T, UB), lambda j: (0, j))
    lam = pl.BlockSpec((2, 1, 1, SB), lambda j: (0, j, 0, 0))
    mat = pl.BlockSpec((2, 1, UB, P), lambda j: (0, j, 0, 0))
    return ublk, lam, mat


def _dotf(a, b, mode="nn"):
    return lax.dot_general(a, b, _DN[mode], preferred_element_type=F32)


def _diag_mask():
    r = lax.broadcasted_iota(jnp.int32, (UB, SB), 0)
    c = lax.broadcasted_iota(jnp.int32, (UB, SB), 1)
    return lax.shift_right_logical(r, int(math.log2(CH))) == lax.shift_right_logical(c, int(math.log2(P)))


def _expand(m):
    p = lax.broadcasted_iota(jnp.int32, (P, SB), 0)
    c = lax.broadcasted_iota(jnp.int32, (P, SB), 1)
    tile = jnp.where(lax.bitwise_and(c, P - 1) == p, 1.0, 0.0).astype(BF16)
    wide = jnp.dot(m.astype(BF16), tile, preferred_element_type=F32)
    return jnp.where(_diag_mask(), wide, 0.0).astype(BF16)


def _collapse(full):
    c = lax.broadcasted_iota(jnp.int32, (SB, P), 0)
    p = lax.broadcasted_iota(jnp.int32, (SB, P), 1)
    pick = jnp.where(lax.bitwise_and(c, P - 1) == p, 1.0, 0.0).astype(BF16)
    return _exact_perm(jnp.where(_diag_mask(), full, 0.0), pick)


def _zero_state():
    return jnp.zeros((1, SB), F32), jnp.zeros((1, SB), F32)


def scan_fwd(u, lam_re, lam_im, bre, bim, cre, cim, name):
    T = u.shape[0]
    s_ctx, s_lat = LC // SEG, (T - LC) // SEG

    def body(u_ref, lr_ref, li_ref, bre_ref, bim_ref, cre_ref, cim_ref, y_ref, us, ys, sre, sim, fre, fim, ire, iim):
        _to_seg_order(u_ref, us, T)
        ub = us[...].astype(BF16)
        for d in range(2):
            lam8, (pw_c, pw_l) = _lam_tiles(lr_ref[d, 0], li_ref[d, 0], (s_ctx, s_lat))
            sre[...] = _dotf(ub, _expand(bre_ref[d, 0]))
            sim[...] = _dotf(ub, _expand(bim_ref[d, 0]))
            end_c, _ = _seg_scan(sre, sim, lam8, pw_c, 0, s_ctx, bool(d), _zero_state(), fre, fim, ire, iim)
            _seg_scan(sre, sim, lam8, pw_l, LC, s_lat, bool(d), end_c, fre, fim, ire, iim)
            y = (_dotf(sre[...].astype(BF16), _expand(cre_ref[d, 0]), "nt")
                 - _dotf(sim[...].astype(BF16), _expand(cim_ref[d, 0]), "nt"))
            if d == 0:
                ys[...] = y
            else:
                ys[...] += y
        _from_seg_order(ys, y_ref, T)

    ublk, lam, mat = _scan_specs(T)
    return pl.pallas_call(
        body, grid=(NJ,), in_specs=[ublk, lam, lam, mat, mat, mat, mat], out_specs=ublk,
        out_shape=jax.ShapeDtypeStruct((T, G * CH), F32),
        scratch_shapes=[pltpu.VMEM((T, UB), F32)] * 2 + [pltpu.VMEM((T, SB), F32)] * 2 + [pltpu.VMEM((SEG, SB), F32)] * 4,
        compiler_params=_cp(("arbitrary",)), name=name)(u, lam_re, lam_im, bre, bim, cre, cim)


def scan_bwd(u, dy, lam_re, lam_im, bre, bim, cre, cim, name):
    T = u.shape[0]
    s_ctx, s_lat = LC // SEG, (T - LC) // SEG

    def body(u_ref, dy_ref, lr_ref, li_ref, bre_ref, bim_ref, cre_ref, cim_ref,
             du_ref, dlr_ref, dli_ref, dbre_ref, dbim_ref, dcre_ref, dcim_ref,
             us, dys, dus, sre, sim, gre, gim, fre, fim, ic_re, ic_im, il_re, il_im, jre, jim):
        _to_seg_order(u_ref, us, T)
        _to_seg_order(dy_ref, dys, T)
        ub, dyb = us[...].astype(BF16), dys[...].astype(BF16)
        for d in range(2):
            rev = bool(d)
            lam8, (pw_c, pw_l) = _lam_tiles(lr_ref[d, 0], li_ref[d, 0], (s_ctx, s_lat))
            cam8, (cw_c, cw_l) = _lam_tiles(lr_ref[d, 0], li_ref[d, 0], (s_ctx, s_lat), conj=True)
            bre_v, bim_v = _expand(bre_ref[d, 0]), _expand(bim_ref[d, 0])
            sre[...] = _dotf(ub, bre_v)
            sim[...] = _dotf(ub, bim_v)
            end_c, _ = _seg_scan(sre, sim, lam8, pw_c, 0, s_ctx, rev, _zero_state(), fre, fim, ic_re, ic_im)
            _seg_scan(sre, sim, lam8, pw_l, LC, s_lat, rev, end_c, fre, fim, il_re, il_im)
            gre[...] = _dotf(dyb, _expand(cre_ref[d, 0]))
            gim[...] = -_dotf(dyb, _expand(cim_ref[d, 0]))
            end_g, acc_l = _seg_scan(gre, gim, cam8, cw_l, LC, s_lat, not rev, _zero_state(), fre, fim, jre, jim,
                                     prev=(sre, sim, il_re, il_im))
            _, acc_c = _seg_scan(gre, gim, cam8, cw_c, 0, s_ctx, not rev, end_g, fre, fim, jre, jim,
                                 prev=(sre, sim, ic_re, ic_im))
            dlr_ref[d, 0] = _sum0(acc_l[0] + acc_c[0])
            dli_ref[d, 0] = _sum0(acc_l[1] + acc_c[1])
            grb, gib = gre[...].astype(BF16), gim[...].astype(BF16)
            du = _dotf(grb, bre_v, "nt") + _dotf(gib, bim_v, "nt")
            if d == 0:
                dus[...] = du
            else:
                dus[...] += du
            dbre_ref[d, 0] = _collapse(_dotf(ub, grb, "tn"))
            dbim_ref[d, 0] = _collapse(_dotf(ub, gib, "tn"))
            dcre_ref[d, 0] = _collapse(_dotf(dyb, sre[...].astype(BF16), "tn"))
            dcim_ref[d, 0] = -_collapse(_dotf(dyb, sim[...].astype(BF16), "tn"))
        _from_seg_order(dus, du_ref, T)

    ublk, lam, mat = _scan_specs(T)
    lam_s = jax.ShapeDtypeStruct(lam_re.shape, F32)
    mat_s = jax.ShapeDtypeStruct(bre.shape, F32)
    return pl.pallas_call(
        body, grid=(NJ,), in_specs=[ublk, ublk, lam, lam, mat, mat, mat, mat],
        out_specs=[ublk, lam, lam, mat, mat, mat, mat],
        out_shape=[jax.ShapeDtypeStruct((T, G * CH), F32), lam_s, lam_s, mat_s, mat_s, mat_s, mat_s],
        scratch_shapes=[pltpu.VMEM((T, UB), F32)] * 3 + [pltpu.VMEM((T, SB), F32)] * 4 + [pltpu.VMEM((SEG, SB), F32)] * 8,
        compiler_params=_cp(("arbitrary",)), name=name)(u, dy, lam_re, lam_im, bre, bim, cre, cim)


class Exchange:
    def __init__(self, xs, modes):
        self.n = len(xs)
        self.modes = [modes] * self.n if isinstance(modes, (str, int)) else list(modes)
        self.out_shape = [jax.ShapeDtypeStruct(self._shape(x, md), x.dtype) for x, md in zip(xs, self.modes)]
        self.scratch = [pltpu.SemaphoreType.DMA((NDEV - 1, self.n)), pltpu.SemaphoreType.DMA((NDEV - 1, self.n)),
                        pltpu.SemaphoreType.DMA((self.n,))]
        self.specs = [pl.BlockSpec(memory_space=pl.ANY)] * self.n

    @staticmethod
    def _shape(x, mode):
        if mode == "gather":
            return (NDEV,) + tuple(x.shape)
        return tuple(x.shape) if mode == "lead" else (NDEV, x.shape[0], mode) + tuple(x.shape[2:])

    @staticmethod
    def _piece(x_ref, mode, dev):
        if mode == "gather":
            return x_ref
        return x_ref.at[dev] if mode == "lead" else x_ref.at[:, pl.ds(dev * mode, mode)]

    def _copies(self, x_refs, out_refs, sems):
        send_sems, recv_sems, local_sems = sems
        mx, my, mc = lax.axis_index("x"), lax.axis_index("y"), lax.axis_index("c")
        me = 4 * mx + 2 * my + mc
        peer_of = lambda k: (1 - mx if k & 4 else mx, 1 - my if k & 2 else my, 1 - mc if k & 1 else mc)
        local, first, relay, arrivals = [], [], [], []
        for a, (x_ref, out_ref) in enumerate(zip(x_refs, out_refs)):
            mode = self.modes[a]
            local.append(pltpu.make_async_copy(self._piece(x_ref, mode, me), out_ref.at[me], local_sems.at[a]))

            def remote(src, dst, k, pair, a=a):
                return pltpu.make_async_remote_copy(src_ref=src, dst_ref=dst, send_sem=send_sems.at[pair, a],
                                                    recv_sem=recv_sems.at[pair, a], device_id=peer_of(k), device_id_type=MESH_T)

            for k in range(1, NDEV):
                peer = peer_of(k)
                pid = 4 * peer[0] + 2 * peer[1] + peer[2]
                if mode != "gather":
                    src = self._piece(x_ref, mode, pid)
                    first.append(remote(src, out_ref.at[me], k, k - 1))
                    arrivals.append(remote(src, out_ref.at[pid], k, k - 1))
                elif k == 1:
                    first.append(remote(x_ref, out_ref.at[me], k, k - 1))
                    arrivals.append(remote(x_ref, out_ref.at[pid], k, k - 1))
                elif k % 2 == 0:
                    first.append(remote(x_ref, out_ref.at[me], k, k - 1))
                    relay.append((remote(x_ref, out_ref.at[pid], k, k - 1), remote(out_ref.at[pid], out_ref.at[pid], 1, k)))
                else:
                    arrivals.append(remote(x_ref, out_ref.at[pid], 1, k - 1))
        return local, first, relay, arrivals

    def start(self, x_refs, out_refs, sems):
        local, first, _, _ = self._copies(x_refs, out_refs, sems)
        for cp in local + first:
            cp.start()

    def finish(self, x_refs, out_refs, sems):
        local, first, relay, arrivals = self._copies(x_refs, out_refs, sems)
        for arrival, onward in relay:
            arrival.wait_recv()
            onward.start()
        for cp in arrivals:
            cp.wait_recv()
        for cp in first + [onward for _, onward in relay]:
            cp.wait_send()
        for cp in local:
            cp.wait()


def exchange(xs, modes, name):
    ex = Exchange(xs, modes)
    n = ex.n

    def body(*refs):
        ex.start(refs[:n], refs[n:2 * n], refs[2 * n:])
        ex.finish(refs[:n], refs[n:2 * n], refs[2 * n:])

    return pl.pallas_call(body, in_specs=ex.specs, out_specs=ex.specs, out_shape=ex.out_shape, scratch_shapes=ex.scratch,
                          compiler_params=pltpu.CompilerParams(has_side_effects=True), name=name)(*xs)


def _dot_f32(a, b, dn):
    return lax.dot_general(a, b, dn, preferred_element_type=F32, precision=lax.Precision.HIGHEST)


def ada_fwd(cg, c_ctx, ada_w, ada_b_loc, name):
    W = ada_w.shape[2]

    def body(cg_ref, cc_ref, w_ref, b_ref, o_ref):
        a = jnp.concatenate([_silu(cg_ref[...]), jnp.broadcast_to(_silu(cc_ref[...]), (NDEV, D))], axis=0)
        for i in range(2):
            o_ref[i] = _dot_f32(a, w_ref[i], _DN["nn"]) + b_ref[i]

    return pl.pallas_call(body, out_shape=jax.ShapeDtypeStruct((2, 2 * NDEV, W), F32),
                          compiler_params=_cp(), name=name)(cg, c_ctx, ada_w, ada_b_loc)


def ada_bwd(cg, c_ctx, ada_w, dm_loc, dm_all, name):
    W = ada_w.shape[2]

    def body(cg_ref, cc_ref, w_ref, dl_ref, da_ref, gw_ref, dcc_ref, gb_ref):
        a = jnp.concatenate([_silu(cg_ref[...]), jnp.broadcast_to(_silu(cc_ref[...]), (NDEV, D))], axis=0)
        dcc = jnp.zeros((1, D), F32)
        for i in range(2):
            dl = dl_ref[i]
            gw_ref[i] = _dot_f32(a, dl, _DN["tn"])
            dctx = jnp.sum(dl[NDEV:], axis=0, keepdims=True)
            dcc = dcc + _dot_f32(dctx, w_ref[i], _DN["nt"])
        dcc_ref[...] = dcc
        gb_ref[...] = jnp.sum(da_ref[...], axis=0)

    return pl.pallas_call(body, out_shape=[jax.ShapeDtypeStruct((2, D, W), F32), jax.ShapeDtypeStruct((1, D), F32),
                                           jax.ShapeDtypeStruct((2, 3 * D), F32)],
                          compiler_params=_cp(), name=name)(cg, c_ctx, ada_w, dm_loc, dm_all)


def cctx_finish(parts, c_ctx, name):
    def body(p_ref, cc_ref, o_ref):
        o_ref[...] = jnp.sum(p_ref[...], axis=0, keepdims=True) * _dsilu(cc_ref[...])

    return pl.pallas_call(body, out_shape=jax.ShapeDtypeStruct((1, D), F32), name=name)(parts, c_ctx)


def _adamw_update(g_ref, w_ref, m_ref, v_ref, go_ref, d_ref, mo_ref, vo_ref):
    g = g_ref[0].astype(F32)
    for s in range(1, g_ref.shape[0]):
        g = g + g_ref[s].astype(F32)
    mn = B1 * m_ref[...] + (1.0 - B1) * g
    vn = B2 * v_ref[...] + (1.0 - B2) * g * g
    go_ref[...] = g
    mo_ref[...] = mn
    vo_ref[...] = vn
    d_ref[...] = -LR * ((mn * (1.0 / (1.0 - B1 ** STEP))) / (jnp.sqrt(vn * (1.0 / (1.0 - B2 ** STEP))) + AEPS) + WD * w_ref[...])


def adamw(gstack, w, m, v, name, tr=256):
    n, R, C = gstack.shape
    tr = max(t for t in range(8, min(tr, R) + 1, 8) if R % t == 0)
    spec = pl.BlockSpec((tr, C), lambda i: (i, 0))
    return pl.pallas_call(_adamw_body(1), grid=(R // tr,),
                          in_specs=[pl.BlockSpec((n, tr, C), lambda i: (0, i, 0)), spec, spec, spec],
                          out_specs=[spec] * 4, out_shape=[jax.ShapeDtypeStruct((R, C), F32)] * 4,
                          compiler_params=_cp(("parallel",)), name=name)(gstack, w, m, v)


def _adamw_body(k):
    def body(*refs):
        for t in range(k):
            _adamw_update(*refs[4 * t:4 * t + 4], *refs[4 * k + 4 * t:4 * k + 4 * t + 4])
    return body


def adamw_multi(items, grid, name):
    k = len(items)
    ins, in_specs, out_specs, out_shape = [], [], [], []
    for g, g_spec, w, m, v, w_spec in items:
        ins += [g, w, m, v]
        in_specs += [g_spec, w_spec, w_spec, w_spec]
    for g, g_spec, w, m, v, w_spec in items:
        out_specs += [w_spec] * 4
        out_shape += [jax.ShapeDtypeStruct(w.shape, F32)] * 4
    res = pl.pallas_call(_adamw_body(k), grid=grid, in_specs=in_specs, out_specs=out_specs, out_shape=out_shape,
                         compiler_params=_cp(("arbitrary",) * len(grid)), name=name)(*ins)
    return [res[4 * t:4 * t + 4] for t in range(k)]


def _whole(a, grid_rank):
    zeros = (0,) * a.ndim
    return pl.BlockSpec(a.shape, lambda *idx: zeros)


def sum_slots(xs, name):
    def body(*refs):
        for x_ref, o_ref in zip(refs[:len(xs)], refs[len(xs):]):
            acc = x_ref[0]
            for s in range(1, NDEV):
                acc = acc + x_ref[s]
            o_ref[...] = acc

    return pl.pallas_call(body, out_shape=[jax.ShapeDtypeStruct(x.shape[1:], F32) for x in xs],
                          compiler_params=_cp(), name=name)(*xs)


def _col_shards(g):
    R, N = g.shape
    return g.reshape(R, NDEV, N // NDEV).transpose(1, 0, 2)


def _vec2(v):
    return jnp.broadcast_to(v.reshape(1, 1, -1), (2, 1, v.size))


SHARD_ROWS = {"mla_w_in": 192, "mla_w_uq": 192, "mla_w_ukv": 256, "s5_w_in": 256}


def _t_shard(wsh, rows):
    t = wsh[0].T.astype(BF16)
    return jnp.pad(t, ((0, rows - t.shape[0]), (0, 0)))


def _win_order():
    w = IN_W // NDEV
    perm = np.zeros((IN_WP, NDEV * SHARD_ROWS["mla_w_in"]), np.float32)
    first = QL + KVL + ROPE
    for c in range(IN_W):
        n = c + HEADS * VD if c < first else c - first
        perm[n, (c // w) * SHARD_ROWS["mla_w_in"] + c % w] = 1.0
    return jnp.asarray(perm, BF16)


def local_step(ctx, x, tgt, mod, Wt, small, l1_shards):
    T = LC + x.shape[0]
    xa = ("cat", ctx, x)
    sh = [mod[i, :, None, 0:D] for i in range(2)]
    sc = [mod[i, :, None, D:2 * D] for i in range(2)]
    gt = [mod[i, :, None, 2 * D:] for i in range(2)]
    ng = [_vec2(small["norm_g"][i]) for i in range(2)]
    qg, kvg = _vec2(small["mla_q_norm"]), _vec2(small["mla_kv_norm"])
    cosf, sinf, pm, pmt = _rope_tables(T)

    (h0, p0, cqn, ckvn), _ = rowwise(st_l0_pre, [xa], [ng[0], sc[0], sh[0], qg, kvg],
                                     [(D, BF16), (IN_WP, F32), (QL, BF16), (KVL, BF16)], [], "l0_pre", mats=[Wt["mla_w_in"]])
    z0, cq, ckv = (p0, 0, HEADS * VD), (p0, HEADS * VD // QL, QL), (p0, (HEADS * VD + QL) // KVL, KVL)
    Q = project_q(cqn, Wt["mla_w_uq"], cosf, sinf, pm, "l0_uq")
    K, V = project_kv(ckvn, Wt["mla_w_ukv"], p0, (HEADS * VD + QL + KVL) // 128, "l0_ukv")
    (o, lse), got = attn_fwd(Q, K, V, "l0_attn", rode=l1_shards, modes="gather")
    Wt, small = dict(Wt), dict(small)
    for n, a in zip(L1_BIG, got):
        Wt[n] = a.reshape(-1, a.shape[-1])
    vecs = lax.bitcast_convert_type(got[-1].reshape(NDEV, 2, -1, 2), F32)
    small["s5_d"], small["s5_b_glu"] = vecs[:, 0, :].reshape(D), vecs[:, 1, :].reshape(D)
    o2 = o.transpose(1, 0, 2).reshape(T, HEADS * VD)
    (og, out0, x1), _ = rowwise(st_l0_post, [o2, z0, xa], [gt[0]], [(D, BF16), (D, F32), (D, F32)], [], "l0_post",
                                mats=[Wt["mla_w_out"]])

    ls = small["s5_log_step"].reshape(2, G, 1)
    a_re, a_im = small["s5_a_re"].reshape(2, G, P), small["s5_a_im"].reshape(2, G, P)
    b_re, b_im = small["s5_b_re"].reshape(2, G * P, CH), small["s5_b_im"].reshape(2, G * P, CH)
    lam_re, lam_im, f_re, f_im = disc_fwd(a_re, a_im, ls, "s5_disc")
    f_re2, f_im2 = f_re.reshape(2, G * P, 1), f_im.reshape(2, G * P, 1)
    bb_re, bb_im = disc_b(f_re2, f_im2, b_re, b_im, "s5_disc_b")
    compact = lambda m: m.reshape(2, NJ, UB, P)
    bre = compact(bb_re.reshape(2, G, P, CH).transpose(0, 1, 3, 2))
    bim = compact(bb_im.reshape(2, G, P, CH).transpose(0, 1, 3, 2))
    cre, cim = compact(small["s5_c_re"]), compact(small["s5_c_im"])
    lam_re4, lam_im4 = lam_re.reshape(2, NJ, 1, SB), lam_im.reshape(2, NJ, 1, SB)

    (h1, p1), _ = rowwise(st_l1_pre, [x1], [ng[1], sc[1], sh[1]], [(D, BF16), (2 * D, F32)], [], "l1_pre", mats=[Wt["s5_w_in"]])
    u, z1 = (p1, 0, D), (p1, 1, D)
    yssm = scan_fwd(p1, lam_re4, lam_im4, bre, bim, cre, cim, "s5_scan")
    dvec, bglu = _vec2(small["s5_d"]), _vec2(small["s5_b_glu"])
    fg = _vec2(small["final_g"])
    lat_mask = jnp.stack([jnp.zeros((1, D), F32), jnp.ones((1, D), F32)])
    (y, y1b, gl, y3, out1, dx2), (dfg, lvec) = rowwise(
        st_l1_mlp, [yssm, u, z1, x1, ("lat", tgt)], [dvec, bglu, gt[1], fg, lat_mask],
        [(D, F32), (D, BF16), (D, F32), (D, BF16), (D, F32), (D, F32)], [D, 128], "l1_mlp",
        mats=[Wt["s5_w_glu"], Wt["s5_w_out"]])

    (dz1, dy, du_d), (dgt1, dbglu, dd), (g_w_out5, g_w_glu) = rowwise(
        st_l1_mlp_bwd, [dx2, out1, y3, y, gl, z1, u, y1b], [gt[1], bglu, dvec], [(D, BF16), (D, F32), (D, F32)], [D, D, D],
        "l1_mlp_b", mats=[Wt["s5_w_out"], Wt["s5_w_glu"]], out_accs=[(D, D), (D, D)])
    du_s, dlr, dli, dbre, dbim, dcre, dcim = scan_bwd(p1, dy, lam_re4, lam_im4, bre, bim, cre, cim, "s5_scan_b")
    dbb_re = dbre.reshape(2, G, CH, P).transpose(0, 1, 3, 2).reshape(2, G * P, CH)
    dbb_im = dbim.reshape(2, G, CH, P).transpose(0, 1, 3, 2).reshape(2, G * P, CH)
    g_c_re, g_c_im = dcre.reshape(2, G, CH, P), dcim.reshape(2, G, CH, P)
    g_b_re, g_b_im, dfr, dfi = disc_b_bwd(f_re2, f_im2, b_re, b_im, dbb_re, dbb_im, "s5_disc_b_b")
    g_a_re, g_a_im, g_ls = disc_a_bwd(a_re, a_im, ls, dlr.reshape(2, G, P), dli.reshape(2, G, P),
                                      dfr.reshape(2, G, P), dfi.reshape(2, G, P), "s5_disc_b_a")
    (dx1,), (dsh1, dsc1, dng1), (g_w_in5,) = rowwise(
        st_l1_tail_bwd, [du_d, du_s, dz1, h1, x1, dx2], [ng[1], sc[1]], [(D, F32)], [D, D, D], "l1_pre_b",
        mats=[Wt["s5_w_in"]], out_accs=[(D, 2 * D)])
    g_w_in5 = _col_shards(g_w_in5)

    (do2, dz0), (dgt0,), (g_w_out,) = rowwise(st_l0_post_bwd, [dx1, out0, og, o2, z0], [gt[0]], [(D, F32), (D, F32)], [D],
                                              "l0_post_b", mats=[Wt["mla_w_out"]], out_accs=[(D, D)])
    doh = do2.reshape(T, HEADS, VD).transpose(1, 0, 2)
    rows8 = lambda g: g.reshape(NDEV, -1, g.shape[-1])
    both = lambda s: s[0, 0] + s[1, 0]
    dense = lambda g: g.reshape(2, G * P * CH // 128, 128)
    chunks = [dense(g_b_re), dense(g_b_im), g_c_re, g_c_im]
    l1_send = [g_w_in5, rows8(g_w_glu), rows8(g_w_out5), rows8(g_w_out),
               both(dd).reshape(NDEV, 1, -1), both(dbglu).reshape(NDEV, 1, -1)]
    (dQ, dK, dV), l1_recv = attn_bwd(Q, K, V, o, lse, doh, "l0_attn_b", rode=l1_send + chunks,
                                     modes=["lead"] * len(l1_send) + [a.shape[1] // NDEV for a in chunks])
    dqh = rope(dQ, cosf, sinf, pmt, True, BF16, "l0_rope_q_b", scale=SCALE)
    dq = dqh.transpose(1, 0, 2).reshape(T, HEADS * QK)
    dkv, dkr = split_kv_grads(dK, dV, "l0_kv_b")
    (grad_x,), (dqg, dkvg, dsh0, dsc0, dng0), (g_uq, g_ukv, g_p) = rowwise(
        st_l0_tail_bwd, [dq, dkv, dkr, dz0, cq, ckv, cqn, ckvn, h0, xa, dx1], [qg, kvg, ng[0], sc[0]],
        [(D, F32, "lat")], [QL, KVL, D, D, D], "l0_pre_b", mats=[Wt["mla_w_uq"], Wt["mla_w_ukv"], Wt["mla_w_in"]],
        out_accs=[(QL, HEADS * QK), (KVL, HEADS * KVW), (D, IN_WP)])
    g_w_uq, g_w_ukv = _col_shards(g_uq).astype(BF16), _col_shards(g_ukv).astype(BF16)
    g_w_in = _col_shards(jnp.concatenate([g_p[:, HEADS * VD:IN_W], g_p[:, :HEADS * VD]], axis=1)).astype(BF16)

    dmod = jnp.stack([jnp.concatenate([dsh0, dsc0, dgt0], axis=-1)[:, 0], jnp.concatenate([dsh1, dsc1, dgt1], axis=-1)[:, 0]])
    gbig = {"mla_w_in": g_w_in, "mla_w_uq": g_w_uq, "mla_w_ukv": g_w_ukv}
    gsmall = {"norm_g": jnp.stack([both(dng0), both(dng1)]), "mla_q_norm": both(dqg), "mla_kv_norm": both(dkvg),
              "s5_a_re": g_a_re, "s5_a_im": g_a_im, "s5_log_step": g_ls, "final_g": dfg[1, 0]}
    return lvec[1], grad_x, dmod, gbig, gsmall, l1_recv


COL_SHARDED = ("mla_w_in", "mla_w_uq", "mla_w_ukv", "s5_w_in")
ROW_SHARDED = ("mla_w_out", "s5_w_glu", "s5_w_out")
VEC_SHARDED = ("s5_d", "s5_b_glu")
BIG = COL_SHARDED + ROW_SHARDED
L0_BIG = ("mla_w_in", "mla_w_uq", "mla_w_ukv")
L1_BIG = ("s5_w_in", "s5_w_glu", "s5_w_out", "mla_w_out")
BITS16 = jnp.bfloat16
SMALL_RS = ("norm_g", "mla_q_norm", "mla_kv_norm", "s5_a_re", "s5_a_im", "s5_log_step", "s5_b_re", "s5_b_im",
            "s5_c_re", "s5_c_im", "final_g")
CHUNKED = ("s5_b_re", "s5_b_im", "s5_c_re", "s5_c_im")
DENSE = ("s5_b_re", "s5_b_im")
TINY = ("norm_g", "mla_q_norm", "mla_kv_norm", "s5_a_re", "s5_a_im", "s5_log_step", "final_g")
ORDER = ("c_ctx", "ada_w", "ada_b", "norm_g", "mla_w_in", "mla_q_norm", "mla_w_uq", "mla_kv_norm", "mla_w_ukv",
         "mla_w_out", "s5_w_in", "s5_a_re", "s5_a_im", "s5_log_step", "s5_b_re", "s5_b_im", "s5_c_re", "s5_c_im",
         "s5_d", "s5_w_glu", "s5_b_glu", "s5_w_out", "final_g")


def kernel(x, c, ctx, c_ctx, ada_w, ada_b, norm_g, mla_w_in, mla_q_norm, mla_w_uq, mla_kv_norm, mla_w_ukv, mla_w_out, s5_w_in, s5_a_re, s5_a_im, s5_log_step, s5_b_re, s5_b_im, s5_c_re, s5_c_im, s5_d, s5_w_glu, s5_b_glu, s5_w_out, final_g, loss_target, m_c_ctx, m_ada_w, m_ada_b, m_norm_g, m_mla_w_in, m_mla_q_norm, m_mla_w_uq, m_mla_kv_norm, m_mla_w_ukv, m_mla_w_out, m_s5_w_in, m_s5_a_re, m_s5_a_im, m_s5_log_step, m_s5_b_re, m_s5_b_im, m_s5_c_re, m_s5_c_im, m_s5_d, m_s5_w_glu, m_s5_b_glu, m_s5_w_out, m_final_g, v_c_ctx, v_ada_w, v_ada_b, v_norm_g, v_mla_w_in, v_mla_q_norm, v_mla_w_uq, v_mla_kv_norm, v_mla_w_ukv, v_mla_w_out, v_s5_w_in, v_s5_a_re, v_s5_a_im, v_s5_log_step, v_s5_b_re, v_s5_b_im, v_s5_c_re, v_s5_c_im, v_s5_d, v_s5_w_glu, v_s5_b_glu, v_s5_w_out, v_final_g):
    w = dict(c_ctx=c_ctx, ada_w=ada_w, ada_b=ada_b, norm_g=norm_g, mla_w_in=mla_w_in, mla_q_norm=mla_q_norm,
             mla_w_uq=mla_w_uq, mla_kv_norm=mla_kv_norm, mla_w_ukv=mla_w_ukv, mla_w_out=mla_w_out, s5_w_in=s5_w_in,
             s5_a_re=s5_a_re, s5_a_im=s5_a_im, s5_log_step=s5_log_step, s5_b_re=s5_b_re, s5_b_im=s5_b_im,
             s5_c_re=s5_c_re, s5_c_im=s5_c_im, s5_d=s5_d, s5_w_glu=s5_w_glu, s5_b_glu=s5_b_glu, s5_w_out=s5_w_out,
             final_g=final_g)
    m = dict(c_ctx=m_c_ctx, ada_w=m_ada_w, ada_b=m_ada_b, norm_g=m_norm_g, mla_w_in=m_mla_w_in, mla_q_norm=m_mla_q_norm,
             mla_w_uq=m_mla_w_uq, mla_kv_norm=m_mla_kv_norm, mla_w_ukv=m_mla_w_ukv, mla_w_out=m_mla_w_out,
             s5_w_in=m_s5_w_in, s5_a_re=m_s5_a_re, s5_a_im=m_s5_a_im, s5_log_step=m_s5_log_step, s5_b_re=m_s5_b_re,
             s5_b_im=m_s5_b_im, s5_c_re=m_s5_c_re, s5_c_im=m_s5_c_im, s5_d=m_s5_d, s5_w_glu=m_s5_w_glu,
             s5_b_glu=m_s5_b_glu, s5_w_out=m_s5_w_out, final_g=m_final_g)
    v = dict(c_ctx=v_c_ctx, ada_w=v_ada_w, ada_b=v_ada_b, norm_g=v_norm_g, mla_w_in=v_mla_w_in, mla_q_norm=v_mla_q_norm,
             mla_w_uq=v_mla_w_uq, mla_kv_norm=v_mla_kv_norm, mla_w_ukv=v_mla_w_ukv, mla_w_out=v_mla_w_out,
             s5_w_in=v_s5_w_in, s5_a_re=v_s5_a_re, s5_a_im=v_s5_a_im, s5_log_step=v_s5_log_step, s5_b_re=v_s5_b_re,
             s5_b_im=v_s5_b_im, s5_c_re=v_s5_c_re, s5_c_im=v_s5_c_im, s5_d=v_s5_d, s5_w_glu=v_s5_w_glu,
             s5_b_glu=v_s5_b_glu, s5_w_out=v_s5_w_out, final_g=v_final_g)

    me = 4 * lax.axis_index("x") + 2 * lax.axis_index("y") + lax.axis_index("c")
    WA = ada_w.shape[2]

    def shard(n):
        return _t_shard(w[n], SHARD_ROWS[n]) if n in COL_SHARDED else w[n][0].astype(BF16)

    wgot = exchange([c] + [shard(n) for n in L0_BIG], "gather", "gather_w")

    cg = wgot[0].reshape(NDEV, D)
    cc2 = c_ctx.reshape(1, D)
    ada_b_loc = lax.dynamic_slice_in_dim(ada_b.reshape(2, 3 * D // WA, WA), me, 1, axis=1)
    part = ada_fwd(cg, cc2, ada_w, ada_b_loc, "ada_fwd")
    pg = exchange([part], "gather", "gather_mod")[0]
    mod_l = lax.dynamic_index_in_dim(pg, me, axis=2, keepdims=False).transpose(1, 0, 2).reshape(2, 3 * D)
    mod_c = pg[:, :, NDEV, :].transpose(1, 0, 2).reshape(2, 3 * D)
    mod = jnp.stack([mod_c, mod_l], axis=1)

    Wt = {n: a.reshape(-1, a.shape[-1]) for n, a in zip(L0_BIG, wgot[1:])}
    Wt["mla_w_in"] = mm(_win_order(), Wt["mla_w_in"], "nn", "w_in_order", out_dtype=BF16)
    vec_bits = lax.bitcast_convert_type(jnp.concatenate([s5_d, s5_b_glu], axis=0), BITS16).reshape(2, -1)
    small = {n: w[n] for n in SMALL_RS}

    lvec, grad_x, dmod, gbig, gsmall, l1_recv = local_step(ctx[0], x[0], loss_target[0], mod, Wt, small,
                                                           [shard(n) for n in L1_BIG] + [vec_bits])
    grad_x = grad_x[None]

    per_dev = G // NDEV
    recv = dict(zip(L0_BIG, exchange([gbig[n] for n in L0_BIG], "lead", "scatter_grads")))
    recv.update(dict(zip(L1_BIG + VEC_SHARDED, l1_recv)))
    out = {}

    def keep(n, res):
        for key, arr in zip("gdmv", res):
            out[key, n] = arr.reshape(w[n].shape)

    for n in BIG:
        keep(n, adamw(recv[n], w[n][0], m[n][0], v[n][0], "adamw_" + n))
    reduced = sum_slots(l1_recv[len(L1_BIG + VEC_SHARDED):], "sum_chunks")

    kshape = lambda n: w[n].shape if w[n].ndim > 1 else (1, w[n].size)
    flat = jnp.concatenate([gsmall[n].reshape(-1) for n in TINY] + [dmod.reshape(-1), lvec.reshape(-1)])[None]
    bb_all, cc_all, flat_all = exchange([jnp.stack(reduced[:2]), jnp.stack(reduced[2:]), flat], "gather", "gather_small")
    chunk_all = [bb_all[:, 0], bb_all[:, 1], cc_all[:, 0], cc_all[:, 1]]
    tiny_all, off = [], 0
    for n in TINY:
        tiny_all.append(flat_all[:, 0, off:off + w[n].size].reshape((NDEV,) + kshape(n)))
        off += w[n].size
    dm_all = flat_all[:, 0, off:off + dmod.size].reshape((NDEV,) + dmod.shape)
    loss = sum_slots([flat_all[:, :, off + dmod.size:]], "loss_sum")[0][0, 0]

    dm_cols = lax.dynamic_slice_in_dim(dm_all.reshape(NDEV, 2, 2, 3 * D // WA, WA), me, 1, axis=3)[:, :, :, 0]
    dm_loc = jnp.concatenate([dm_cols[:, :, 1].transpose(1, 0, 2), dm_cols[:, :, 0].transpose(1, 0, 2)], axis=1)
    g_ada_w, dcc_part, g_ada_b = ada_bwd(cg, cc2, ada_w, dm_loc, dm_all.transpose(0, 2, 1, 3).reshape(2 * NDEV, 2, 3 * D), "ada_bwd")
    dcc_all = exchange([dcc_part], "gather", "gather_dcc")[0].reshape(NDEV, D)
    g_c_ctx = cctx_finish(dcc_all, cc2, "cctx_finish")

    flat2 = lambda t: t.reshape(-1, t.shape[-1])
    keep("ada_w", adamw(flat2(g_ada_w)[None], flat2(ada_w), flat2(m_ada_w), flat2(v_ada_w), "adamw_ada"))
    items = []
    halves = 2
    for n, g in zip(CHUNKED, chunk_all):
        blk = (1, 1, G // halves) + w[n].shape[3:]
        g = jnp.moveaxis(g, 0, 1).reshape(w[n].shape)
        g_spec = pl.BlockSpec((1,) + blk, lambda d, s: (0, 0, d, s, 0, 0))
        items.append((g[None], g_spec, w[n], m[n], v[n], pl.BlockSpec(blk, lambda d, s: (0, d, s, 0, 0))))
    for n, res in zip(CHUNKED, adamw_multi(items, (2, halves), "adamw_bc")):
        keep(n, res)
    tiny_g = dict(zip(TINY, tiny_all))
    tiny_g.update({n: recv[n] for n in VEC_SHARDED})
    tiny_g["c_ctx"], tiny_g["ada_b"] = g_c_ctx[None], g_ada_b[None]
    names = list(tiny_g)
    items = [(tiny_g[n], _whole(tiny_g[n], 1)) + tuple(t[n].reshape(kshape(n)) for t in (w, m, v))
             + (pl.BlockSpec(kshape(n), lambda i, r=len(kshape(n)): (0,) * r),) for n in names]
    for n, res in zip(names, adamw_multi(items, (1,), "adamw_small")):
        keep(n, res)

    return (loss, grad_x, *[out["g", n] for n in ORDER], *[out["d", n] for n in ORDER],
            *[out["m", n] for n in ORDER], *[out["v", n] for n in ORDER])
```

```python
import math

import numpy as np
import jax
import jax.numpy as jnp
from jax import lax
from jax.experimental import pallas as pl
from jax.experimental.pallas import tpu as pltpu

F32 = jnp.float32
BF16 = jnp.bfloat16

D = 1024
L = 2048
LC = 256
NDEV = 8
GRID_W = 64
EPS = 1e-6
HEADS = 16
NOPE = 64
ROPE = 32
QK = NOPE + ROPE
VD = 64
IN_W = 256 + 128 + ROPE + HEADS * 64
IN_WP = 1536
QL = 256
KVL = 128
SCALE = QK ** -0.5
LOG2E = math.log2(math.e)
THETA = 10000.0
G = 64
P = 64
CH = 16
GB = 8
NJ = G // GB
UB = GB * CH
SB = GB * P
SEG = 8
TB = 256
VMEM_LIMIT = 56 * 1024 * 1024
B1, B2, LR, AEPS, WD, STEP = 0.9, 0.999, 0.001, 1e-8, 0.01, 10
MESH_T = pl.DeviceIdType.MESH


def _cp(sem=None):
    return pltpu.CompilerParams(dimension_semantics=sem, vmem_limit_bytes=VMEM_LIMIT)


def _sig(x):
    return 1.0 / (1.0 + jnp.exp(-x))


def _silu(x):
    return x * _sig(x)


def _dsilu(x):
    s = _sig(x)
    return s * (1.0 + x * (1.0 - s))


_GK = math.sqrt(2.0 / math.pi)


def _gelu(x):
    return 0.5 * x * (1.0 + jnp.tanh(_GK * (x + 0.044715 * x * x * x)))


def _dgelu(x):
    t = jnp.tanh(_GK * (x + 0.044715 * x * x * x))
    return 0.5 * (1.0 + t) + 0.5 * x * (1.0 - t * t) * _GK * (1.0 + 3 * 0.044715 * x * x)


def _rs(x):
    return lax.rsqrt(jnp.mean(x * x, axis=-1, keepdims=True) + EPS)


def _sum0(x):
    return jnp.sum(x, axis=0, keepdims=True)


def st_norm_mod(x, g, sc, sh):
    y = x * _rs(x) * g
    return (y * (1.0 + sc) + sh,), ()


def st_norm_mod_bwd(x, dh, dres, g, sc):
    r = _rs(x)
    xn = x * r
    y = xn * g
    dy = dh * (1.0 + sc)
    dxn = dy * g
    dx = r * (dxn - xn * jnp.mean(dxn * xn, axis=-1, keepdims=True))
    return (dres + dx,), (_sum0(dh), _sum0(dh * y), _sum0(dy * xn))


def st_rms(x, g):
    return (x * _rs(x) * g,), ()


def st_rms_bwd(x, dy, g):
    r = _rs(x)
    n = x * r
    dn = dy * g
    dx = r * (dn - n * jnp.mean(dn * n, axis=-1, keepdims=True))
    return (dx,), (_sum0(dy * n),)


def st_rms2(x1, x2, g1, g2):
    return st_rms(x1, g1)[0] + st_rms(x2, g2)[0], ()


def st_rms2_bwd(x1, dy1, x2, dy2, g1, g2):
    (d1,), (s1,) = st_rms_bwd(x1, dy1, g1)
    (d2,), (s2,) = st_rms_bwd(x2, dy2, g2)
    return (d1, d2), (s1, s2)


def st_gate(o, z):
    return (o * _silu(z),), ()


def st_gate_bwd(dog, o, z):
    return (dog * _silu(z), dog * o * _dsilu(z)), ()


def st_resid(x, out, gt):
    return (x + gt * out,), ()


def st_resid_bwd(dx, out, gt):
    return (dx * gt,), (_sum0(dx * out),)


def st_s5a(yssm, u, d):
    y = yssm + d * u
    return (y, _gelu(y)), ()


def st_s5b(y, gl, z, b):
    return (_gelu(y) * _sig(gl + b) * _silu(z),), ()


def st_s5b_bwd(dy3, y, gl, z, b):
    y1 = _gelu(y)
    s = _sig(gl + b)
    dy2 = dy3 * _silu(z)
    dz = dy3 * y1 * s * _dsilu(z)
    dgl = dy2 * y1 * s * (1.0 - s)
    return (dgl, dz, dy2 * s), (_sum0(dgl),)


def st_s5a_bwd(dy1a, dy1b, y, u, d):
    dy = (dy1a + dy1b) * _dgelu(y)
    return (dy, dy * d), (_sum0(dy * u),)


def st_l0_pre(x, g, sc, sh, qg, kvg, w_in):
    hb = st_norm_mod(x, g, sc, sh)[0][0].astype(BF16)
    p = lax.dot_general(hb, w_in, _DN["nt"], preferred_element_type=F32)
    cq, ckv = p[:, HEADS * VD:HEADS * VD + QL], p[:, HEADS * VD + QL:HEADS * VD + QL + KVL]
    return (hb, p) + st_rms2(cq, ckv, qg, kvg)[0], ()


def st_l0_tail_bwd(dq, dkv, dkr, dz, cq, ckv, cqn, ckvn, h, x, dres, qg, kvg, g, sc, w_uq, w_ukv, w_in):
    dcqn = jnp.dot(dq, w_uq, preferred_element_type=F32)
    dckvn = jnp.dot(dkv, w_ukv, preferred_element_type=F32)
    (dcq, dckv), (dqg, dkvg) = st_rms2_bwd(cq, dcqn, ckv, dckvn, qg, kvg)
    dp = jnp.concatenate([dz, dcq, dckv, dkr], axis=1).astype(BF16)
    dh = jnp.dot(dp, w_in, preferred_element_type=F32)
    outs, sums = st_norm_mod_bwd(x, dh, dres, g, sc)
    tn = lambda a, b: lax.dot_general(a, b, _DN["tn"], preferred_element_type=F32)
    return outs, (dqg, dkvg) + sums, (tn(cqn, dq), tn(ckvn, dkv), tn(h, dp))


def st_l1_pre(x, g, sc, sh, w_in):
    hb = st_norm_mod(x, g, sc, sh)[0][0].astype(BF16)
    return (hb, lax.dot_general(hb, w_in, _DN["nt"], preferred_element_type=F32)), ()


def st_l1_tail_bwd(du_a, du_b, dz, h, x, dres, g, sc, w_in):
    dp = jnp.concatenate([(du_a + du_b).astype(BF16), dz], axis=1)
    dh = jnp.dot(dp, w_in, preferred_element_type=F32)
    outs, sums = st_norm_mod_bwd(x, dh, dres, g, sc)
    return outs, sums, (lax.dot_general(h, dp, _DN["tn"], preferred_element_type=F32),)


def st_l0_post(o, z, x, gt, w_out):
    og = (o * _silu(z)).astype(BF16)
    out = jnp.dot(og, w_out, preferred_element_type=F32)
    return (og, out, x + gt * out), ()


def st_l0_post_bwd(dx1, out, og, o, z, gt, w_out):
    (dout,), (dgt,) = st_resid_bwd(dx1, out.astype(F32), gt)
    doutb = dout.astype(BF16)
    dog = lax.dot_general(doutb, w_out, _DN["nt"], preferred_element_type=F32)
    return st_gate_bwd(dog, o, z)[0], (dgt,), (lax.dot_general(og, doutb, _DN["tn"], preferred_element_type=F32),)


def st_l1_mlp(yssm, u, z, x1, tgt, d, bglu, gt, fg, mask, w_glu, w_out):
    (y, y1), _ = st_s5a(yssm, u, d)
    y1b = y1.astype(BF16)
    gl = jnp.dot(y1b, w_glu, preferred_element_type=F32)
    y3 = (y1 * _sig(gl + bglu) * _silu(z)).astype(BF16)
    out = jnp.dot(y3, w_out, preferred_element_type=F32)
    (dx2,), sums = st_final(x1 + gt * out, tgt, fg, mask)
    return (y, y1b, gl, y3, out, dx2), sums


def st_l1_mlp_bwd(dx2, out, y3, y, gl, z, u, y1b, gt, bglu, d, w_out, w_glu):
    out, gl = out.astype(F32), gl.astype(F32)
    (dout,), (dgt,) = st_resid_bwd(dx2, out, gt)
    doutb = dout.astype(BF16)
    dy3 = lax.dot_general(doutb, w_out, _DN["nt"], preferred_element_type=F32)
    (dgl, dz, dy1a), (dbglu,) = st_s5b_bwd(dy3, y, gl, z, bglu)
    dglb = dgl.astype(BF16)
    dy1b = lax.dot_general(dglb, w_glu, _DN["nt"], preferred_element_type=F32)
    (dy, du), (dd,) = st_s5a_bwd(dy1a, dy1b, y, u, d)
    g_w_out = lax.dot_general(y3, doutb, _DN["tn"], preferred_element_type=F32)
    g_w_glu = lax.dot_general(y1b, dglb, _DN["tn"], preferred_element_type=F32)
    return (dz, dy, du), (dgt, dbglu, dd), (g_w_out, g_w_glu)


def st_final(x2, tgt, g, mask):
    r = _rs(x2)
    n = x2 * r
    e = n * g - tgt
    dyo = e * (1.0 / D)
    dn = dyo * g
    dx = r * (dn - n * jnp.mean(dn * n, axis=-1, keepdims=True))
    lsum = jnp.sum(_sum0(e * e), axis=1, keepdims=True) * (0.5 / D)
    return (dx * mask,), (_sum0(dyo * n), jnp.broadcast_to(lsum, (1, 128)))


def rowwise(fn, rows, vecs, out_rows, out_sums, name, mats=(), out_accs=()):
    lat_blk = lambda i: jnp.maximum(i - 1, 0)
    arrays, in_specs, pick = [], [], []
    for a in rows:
        if not isinstance(a, tuple):
            a = (a, 0, a.shape[1])
        tag = a[0] if isinstance(a[0], str) else None
        if tag == "cat":
            _, ctx, x = a
            arrays += [ctx, x]
            in_specs += [pl.BlockSpec((TB, ctx.shape[1]), lambda i: (0, 0)),
                         pl.BlockSpec((TB, x.shape[1]), lambda i: (lat_blk(i), 0))]
            pick.append(2)
        elif tag == "lat":
            arrays.append(a[1])
            in_specs.append(pl.BlockSpec((TB, a[1].shape[1]), lambda i: (lat_blk(i), 0)))
            pick.append(1)
        else:
            arr, cb, width = a
            arrays.append(arr)
            in_specs.append(pl.BlockSpec((TB, width), lambda i, cb=cb: (i, cb)))
            pick.append(1)
    T = LC + L
    nin, nv, nm, no, ns = len(arrays), len(vecs), len(mats), len(out_rows), len(out_sums)

    def body(*refs):
        i = pl.program_id(0)
        vals, k = [], 0
        for p in pick:
            if p == 2:
                vals.append(jnp.where(i == 0, refs[k][...], refs[k + 1][...]))
            else:
                vals.append(refs[k][...])
            k += p
        vals += [r[0] for r in refs[nin:nin + nv]] + [r[...] for r in refs[nin + nv:nin + nv + nm]]
        res = fn(*vals)
        first_out = nin + nv + nm
        for r, o in zip(refs[first_out:first_out + no], res[0]):
            r[...] = o.astype(r.dtype)
        sum_refs = refs[first_out + no:first_out + no + ns]
        if sum_refs:
            @pl.when(i <= 1)
            def _():
                for r in sum_refs:
                    r[...] = jnp.zeros_like(r)
            for r, s in zip(sum_refs, res[1]):
                r[0] += s
        acc_refs = refs[first_out + no + ns:]
        if acc_refs:
            @pl.when(i == 0)
            def _():
                for r in acc_refs:
                    r[...] = jnp.zeros_like(r)
            for r, a in zip(acc_refs, res[2]):
                r[...] += a

    kind = lambda i: (jnp.minimum(i, 1), 0, 0)
    in_specs += [pl.BlockSpec((1, 1, v.shape[2]), kind) for v in vecs]
    in_specs += [pl.BlockSpec(m.shape, lambda i: (0, 0), pipeline_mode=pl.Buffered(1)) for m in mats]
    out_specs, out_shape = [], []
    for o in out_rows:
        lat = len(o) == 3
        out_specs.append(pl.BlockSpec((TB, o[0]), (lambda i: (lat_blk(i), 0)) if lat else (lambda i: (i, 0))))
        out_shape.append(jax.ShapeDtypeStruct((L if lat else T, o[0]), o[1]))
    out_specs += [pl.BlockSpec((1, 1, c), kind) for c in out_sums]
    out_shape += [jax.ShapeDtypeStruct((2, 1, c), F32) for c in out_sums]
    out_specs += [pl.BlockSpec(s, lambda i: (0, 0)) for s in out_accs]
    out_shape += [jax.ShapeDtypeStruct(s, F32) for s in out_accs]
    res = pl.pallas_call(body, grid=(T // TB,), in_specs=in_specs, out_specs=out_specs, out_shape=out_shape,
                         compiler_params=_cp(("arbitrary",)), name=name)(*arrays, *vecs, *mats)
    if out_accs:
        return res[:no], res[no:no + ns], res[no + ns:]
    return res[:no], res[no:]


_DN = {"nn": (((1,), (0,)), ((), ())), "nt": (((1,), (1,)), ((), ())), "tn": (((0,), (0,)), ((), ()))}


def mm(a, b, mode, name, out_dtype=F32, tm=None, tn=None, shard_out=False):
    if mode == "nn":
        (M, K), (_, N) = a.shape, b.shape
    elif mode == "nt":
        (M, K), (N, _) = a.shape, b.shape
    else:
        (K, M), (_, N) = a.shape, b.shape
    if tm is None:
        tm = next((t for t in (768, 512, 256) if M % t == 0 and M > t), M)
    tn = N if tn is None else tn
    dn = _DN[mode]

    def body(a_ref, b_ref, o_ref):
        o_ref[...] = lax.dot_general(a_ref[...].astype(BF16), b_ref[...].astype(BF16), dn,
                                     preferred_element_type=F32).astype(o_ref.dtype)

    if shard_out:
        def body(a_ref, b_ref, o_ref):
            av = a_ref[...].astype(BF16)
            for j in range(N // tn):
                bj = b_ref[pl.ds(j * tn, tn), :] if mode == "nt" else b_ref[:, pl.ds(j * tn, tn)]
                o_ref[j] = lax.dot_general(av, bj.astype(BF16), dn, preferred_element_type=F32).astype(o_ref.dtype)

        a_spec = pl.BlockSpec((K, tm), lambda i: (0, i)) if mode == "tn" else pl.BlockSpec((tm, K), lambda i: (i, 0))
        return pl.pallas_call(body, grid=(M // tm,), in_specs=[a_spec, pl.BlockSpec(b.shape, lambda i: (0, 0))],
                              out_specs=pl.BlockSpec((N // tn, tm, tn), lambda i: (0, i, 0)),
                              out_shape=jax.ShapeDtypeStruct((N // tn, M, tn), out_dtype),
                              compiler_params=_cp(("parallel",)), name=name)(a, b)
    a_spec = pl.BlockSpec((K, tm), lambda i, j: (0, i)) if mode == "tn" else pl.BlockSpec((tm, K), lambda i, j: (i, 0))
    b_spec = pl.BlockSpec((tn, K), lambda i, j: (j, 0)) if mode == "nt" else pl.BlockSpec((K, tn), lambda i, j: (0, j))
    return pl.pallas_call(body, grid=(M // tm, N // tn), in_specs=[a_spec, b_spec],
                          out_specs=pl.BlockSpec((tm, tn), lambda i, j: (i, j)), out_shape=jax.ShapeDtypeStruct((M, N), out_dtype),
                          compiler_params=_cp(("parallel", "arbitrary")), name=name)(a, b)


def _rope_tables(T, width=QK, first=NOPE):
    nlat = T - LC
    pos = np.arange(nlat)
    row, col = pos // GRID_W, pos % GRID_W
    half = ROPE // 2
    inv = 1.0 / (THETA ** (np.arange(0, half, 2, dtype=np.float64) / half))
    cosf = np.ones((T, width), np.float64)
    sinf = np.zeros((T, width), np.float64)
    perm = np.zeros((width, width), np.float32)
    for m in range(ROPE):
        j = first + m
        blk, w = m // half, m % half
        ang = (row if blk == 0 else col)[:, None] * inv[None, :]
        f = w % (half // 2)
        cosf[LC:, j] = np.cos(ang[:, f])
        if w < half // 2:
            sinf[LC:, j] = -np.sin(ang[:, f])
            perm[j + half // 2, j] = 1.0
        else:
            sinf[LC:, j] = np.sin(ang[:, f])
            perm[j - half // 2, j] = 1.0
    return jnp.asarray(cosf, F32), jnp.asarray(sinf, F32), jnp.asarray(perm, BF16), jnp.asarray(perm.T, BF16)


def _exact_perm(x, pm):
    hi = x.astype(BF16)
    r1 = x - hi.astype(F32)
    mid = r1.astype(BF16)
    lo = (r1 - mid.astype(F32)).astype(BF16)
    dot = lambda a: jnp.dot(a, pm, preferred_element_type=F32)
    return dot(hi) + dot(mid) + dot(lo)


def _rot(x, cv, sv, pv, inverse):
    if inverse:
        return x * cv + _exact_perm(x * sv, pv)
    return x * cv + _exact_perm(x, pv) * sv


def rope(x, cosf, sinf, pm, inverse, out_dtype, name, scale=1.0):
    H, T, _ = x.shape

    def body(x_ref, c_ref, s_ref, p_ref, o_ref):
        cv, sv, pv = c_ref[...], s_ref[...], p_ref[...]
        for h in range(H):
            o_ref[h] = (_rot(x_ref[h], cv, sv, pv, inverse) * scale).astype(o_ref.dtype)

    return pl.pallas_call(
        body, grid=(T // TB,),
        in_specs=[pl.BlockSpec((H, TB, QK), lambda i: (0, i, 0)), pl.BlockSpec((TB, QK), lambda i: (i, 0)),
                  pl.BlockSpec((TB, QK), lambda i: (i, 0)), pl.BlockSpec((QK, QK), lambda i: (0, 0))],
        out_specs=pl.BlockSpec((H, TB, QK), lambda i: (0, i, 0)), out_shape=jax.ShapeDtypeStruct((H, T, QK), out_dtype),
        compiler_params=_cp(("parallel",)), name=name)(x, cosf, sinf, pm)


KVW = NOPE + VD


def _kv_selectors():
    s_kn = np.zeros((KVW, QK), np.float32)
    s_kr = np.zeros((128, QK), np.float32)
    s_v = np.zeros((KVW, VD), np.float32)
    for l in range(NOPE):
        s_kn[l, l] = 1.0
    for l in range(ROPE):
        s_kr[l, NOPE + l] = 1.0
    for l in range(VD):
        s_v[NOPE + l, l] = 1.0
    return s_kn, s_kr, s_v


def project_q(cqn, w, cosf, sinf, pm, name):
    T = cqn.shape[0]

    def body(a_ref, w_ref, c_ref, s_ref, p_ref, o_ref):
        a, cv, sv, pv = a_ref[...], c_ref[...], s_ref[...], p_ref[...]
        for h in range(HEADS):
            qh = _dotf(a, w_ref[pl.ds(h * QK, QK), :], "nt")
            o_ref[h] = (_rot(qh, cv, sv, pv, False) * (SCALE * LOG2E)).astype(BF16)

    rows = lambda c: pl.BlockSpec((TB, c), lambda i: (i, 0))
    const = lambda x: pl.BlockSpec(x.shape, lambda i: (0, 0))
    return pl.pallas_call(
        body, grid=(T // TB,), in_specs=[rows(QL), const(w), rows(QK), rows(QK), const(pm)],
        out_specs=pl.BlockSpec((HEADS, TB, QK), lambda i: (0, i, 0)), out_shape=jax.ShapeDtypeStruct((HEADS, T, QK), BF16),
        compiler_params=_cp(("parallel",)), name=name)(cqn, w, cosf, sinf, pm)


def project_kv(ckvn, w, p0, kr_block, name):
    T = ckvn.shape[0]
    cosf, sinf, pm, _ = _rope_tables(T, 128, 0)
    s_kn, s_kr, s_v = (jnp.asarray(s, BF16) for s in _kv_selectors())

    def body(a_ref, w_ref, kr_ref, c_ref, s_ref, p_ref, skn_ref, skr_ref, sv_ref, k_ref, v_ref):
        a = a_ref[...]
        krr = _rot(kr_ref[...], c_ref[...], s_ref[...], p_ref[...], False).astype(BF16)
        kr_part = jnp.dot(krr, skr_ref[...], preferred_element_type=F32)
        for h in range(HEADS):
            kvb = _dotf(a, w_ref[pl.ds(h * KVW, KVW), :], "nt").astype(BF16)
            k_ref[h] = (jnp.dot(kvb, skn_ref[...], preferred_element_type=F32) + kr_part).astype(BF16)
            v_ref[h] = jnp.dot(kvb, sv_ref[...], preferred_element_type=F32).astype(BF16)

    rows = lambda c: pl.BlockSpec((TB, c), lambda i: (i, 0))
    const = lambda x: pl.BlockSpec(x.shape, lambda i: (0, 0))
    return pl.pallas_call(
        body, grid=(T // TB,),
        in_specs=[rows(KVL), const(w), pl.BlockSpec((TB, 128), lambda i: (i, kr_block)),
                  rows(128), rows(128), const(pm), const(s_kn), const(s_kr), const(s_v)],
        out_specs=[pl.BlockSpec((HEADS, TB, QK), lambda i: (0, i, 0)), pl.BlockSpec((HEADS, TB, VD), lambda i: (0, i, 0))],
        out_shape=[jax.ShapeDtypeStruct((HEADS, T, QK), BF16), jax.ShapeDtypeStruct((HEADS, T, VD), BF16)],
        compiler_params=_cp(("parallel",)), name=name)(ckvn, w, p0, cosf, sinf, pm, s_kn, s_kr, s_v)


def split_kv_grads(dk, dv, name):
    H, T, _ = dk.shape
    cosf, sinf, _, pmt = _rope_tables(T, 128, 0)
    s_kn, s_kr, s_v = _kv_selectors()
    s_knt, s_krt, s_vt = (jnp.asarray(s.T, BF16) for s in (s_kn, s_kr, s_v))

    def body(dk_ref, dv_ref, c_ref, s_ref, p_ref, skn_ref, skr_ref, sv_ref, dkv_ref, dkr_ref):
        total = None
        for h in range(H):
            dkh = dk_ref[h] * (1.0 / LOG2E)
            total = dkh if total is None else total + dkh
            dkv_ref[:, pl.ds(h * KVW, KVW)] = (
                jnp.dot(dkh.astype(BF16), skn_ref[...], preferred_element_type=F32)
                + jnp.dot(dv_ref[h].astype(BF16), sv_ref[...], preferred_element_type=F32)).astype(BF16)
        dkr_ref[...] = _rot(_exact_perm(total, skr_ref[...]), c_ref[...], s_ref[...], p_ref[...], True)

    rows = lambda c: pl.BlockSpec((TB, c), lambda i: (i, 0))
    const = lambda a: pl.BlockSpec(a.shape, lambda i: (0, 0))
    return pl.pallas_call(
        body, grid=(T // TB,),
        in_specs=[pl.BlockSpec((H, TB, QK), lambda i: (0, i, 0)), pl.BlockSpec((H, TB, VD), lambda i: (0, i, 0)),
                  rows(128), rows(128), const(pmt), const(s_knt), const(s_krt), const(s_vt)],
        out_specs=[rows(H * KVW), rows(128)],
        out_shape=[jax.ShapeDtypeStruct((T, H * KVW), BF16), jax.ShapeDtypeStruct((T, 128), F32)],
        compiler_params=_cp(("parallel",)), name=name)(dk, dv, cosf, sinf, pmt, s_knt, s_krt, s_vt)


def _by_query_block(run, T):
    @pl.when(pl.program_id(1) == 0)
    def _():
        run(LC)

    @pl.when(pl.program_id(1) > 0)
    def _():
        run(T)


def _with_rider(body, nin, nout, ride, grid):
    if ride is None:
        return body
    n = ride.n

    def wrapped(*refs):
        ins, xs = refs[:nin], refs[nin:nin + n]
        outs, got = refs[nin + n:nin + n + nout], refs[nin + n + nout:nin + 2 * n + nout]
        sems = refs[nin + 2 * n + nout:]
        step = pl.program_id(0) * grid[1] + pl.program_id(1)

        @pl.when(step == 0)
        def _():
            ride.start(xs, got, sems)

        body(*ins, *outs)

        @pl.when(step == grid[0] * grid[1] - 1)
        def _():
            ride.finish(xs, got, sems)

    return wrapped


def _ride_call(body, grid, in_specs, out_specs, out_shape, ride, rode, name, args):
    if ride is None:
        return pl.pallas_call(body, grid=grid, in_specs=in_specs, out_specs=out_specs, out_shape=out_shape,
                              compiler_params=_cp(("parallel", "arbitrary")), name=name)(*args), []
    res = pl.pallas_call(
        _with_rider(body, len(in_specs), len(out_specs), ride, grid), grid=grid,
        in_specs=in_specs + ride.specs, out_specs=out_specs + ride.specs, out_shape=out_shape + ride.out_shape,
        scratch_shapes=ride.scratch,
        compiler_params=pltpu.CompilerParams(dimension_semantics=("arbitrary", "arbitrary"), vmem_limit_bytes=VMEM_LIMIT,
                                             has_side_effects=True), name=name)(*args, *rode)
    return res[:len(out_specs)], res[len(out_specs):]


def attn_fwd(q, k, v, name, rode=None, modes=None):
    H, T, _ = q.shape

    def body(q_ref, k_ref, v_ref, o_ref, lse_ref):
        def run(nk):
            s = _dotf(q_ref[0], k_ref[0, pl.ds(0, nk), :], "nt")
            m = jnp.max(s, axis=1, keepdims=True)
            p = jnp.exp2(s - m)
            l = jnp.sum(p, axis=1, keepdims=True)
            o = jnp.dot(p.astype(BF16), v_ref[0, pl.ds(0, nk), :], preferred_element_type=F32)
            o_ref[0] = o / l
            lse_ref[0] = m + jnp.log2(l)

        _by_query_block(run, T)

    return _ride_call(
        body, (H, T // TB),
        [pl.BlockSpec((1, TB, QK), lambda h, i: (h, i, 0)), pl.BlockSpec((1, T, QK), lambda h, i: (h, 0, 0)),
         pl.BlockSpec((1, T, VD), lambda h, i: (h, 0, 0))],
        [pl.BlockSpec((1, TB, VD), lambda h, i: (h, i, 0)), pl.BlockSpec((1, TB, 1), lambda h, i: (h, i, 0))],
        [jax.ShapeDtypeStruct((H, T, VD), F32), jax.ShapeDtypeStruct((H, T, 1), F32)],
        Exchange(rode, modes) if rode else None, rode, name, (q, k, v))


def attn_bwd(q, k, v, o, lse, do, name, rode=None, modes=None):
    H, T, _ = q.shape

    def body(q_ref, k_ref, v_ref, o_ref, lse_ref, do_ref, dq_ref, dk_ref, dv_ref):
        i = pl.program_id(1)

        @pl.when(i == 0)
        def _():
            dk_ref[...] = jnp.zeros_like(dk_ref)
            dv_ref[...] = jnp.zeros_like(dv_ref)

        def run(nk):
            keys = pl.ds(0, nk)
            qv, kv, dov = q_ref[0], k_ref[0, keys, :], do_ref[0]
            p = jnp.exp2(_dotf(qv, kv, "nt") - lse_ref[0])
            delta = jnp.sum(dov * o_ref[0], axis=1, keepdims=True)
            dob = dov.astype(BF16)
            dv_ref[0, keys, :] += _dotf(p.astype(BF16), dob, "tn")
            dp = _dotf(dob, v_ref[0, keys, :], "nt")
            ds = (p * (dp - delta)).astype(BF16)
            dq_ref[0] = jnp.dot(ds, kv, preferred_element_type=F32)
            dk_ref[0, keys, :] += _dotf(ds, qv, "tn")

        _by_query_block(run, T)

    blk = lambda c: pl.BlockSpec((1, TB, c), lambda h, i: (h, i, 0))
    full = lambda c: pl.BlockSpec((1, T, c), lambda h, i: (h, 0, 0))
    return _ride_call(
        body, (H, T // TB), [blk(QK), full(QK), full(VD), blk(VD), blk(1), blk(VD)], [blk(QK), full(QK), full(VD)],
        [jax.ShapeDtypeStruct((H, T, QK), F32), jax.ShapeDtypeStruct((H, T, QK), F32), jax.ShapeDtypeStruct((H, T, VD), F32)],
        Exchange(rode, modes) if rode else None, rode, name, (q, k, v, o, lse, do))


def disc_fwd(a_re, a_im, ls, name):
    def body(ar_ref, ai_ref, ls_ref, lr_ref, li_ref, fr_ref, fi_ref):
        ar, ai = ar_ref[...], ai_ref[...]
        dt = jnp.exp(ls_ref[...])
        mag = jnp.exp(ar * dt)
        lr = mag * jnp.cos(ai * dt)
        li = mag * jnp.sin(ai * dt)
        den = ar * ar + ai * ai
        nr = lr - 1.0
        lr_ref[...] = lr
        li_ref[...] = li
        fr_ref[...] = (nr * ar + li * ai) / den
        fi_ref[...] = (li * ar - nr * ai) / den

    return pl.pallas_call(body, out_shape=[jax.ShapeDtypeStruct(a_re.shape, F32)] * 4, name=name)(a_re, a_im, ls)


def disc_b(f_re, f_im, b_re, b_im, name):
    def body(fr_ref, fi_ref, br_ref, bi_ref, or_ref, oi_ref):
        fr, fi, br, bi = fr_ref[...], fi_ref[...], br_ref[...], bi_ref[...]
        or_ref[...] = fr * br - fi * bi
        oi_ref[...] = fr * bi + fi * br

    fs, bs = _disc_b_specs()
    return pl.pallas_call(body, grid=(2, G * P // DISC_ROWS), in_specs=[fs, fs, bs, bs], out_specs=[bs, bs],
                          out_shape=[jax.ShapeDtypeStruct(b_re.shape, F32)] * 2, name=name)(f_re, f_im, b_re, b_im)


DISC_ROWS = G * P


def _disc_b_specs():
    return (pl.BlockSpec((1, DISC_ROWS, 1), lambda d, i: (d, i, 0)), pl.BlockSpec((1, DISC_ROWS, CH), lambda d, i: (d, i, 0)))


def disc_b_bwd(f_re, f_im, b_re, b_im, dbb_re, dbb_im, name):
    def body(fr_ref, fi_ref, br_ref, bi_ref, dr_ref, di_ref, dbr_ref, dbi_ref, dfr_ref, dfi_ref):
        fr, fi, br, bi, dr, di = fr_ref[...], fi_ref[...], br_ref[...], bi_ref[...], dr_ref[...], di_ref[...]
        dbr_ref[...] = fr * dr + fi * di
        dbi_ref[...] = fr * di - fi * dr
        dfr_ref[...] = jnp.sum(dr * br + di * bi, axis=-1, keepdims=True)
        dfi_ref[...] = jnp.sum(di * br - dr * bi, axis=-1, keepdims=True)

    fs, bs = _disc_b_specs()
    return pl.pallas_call(body, grid=(2, G * P // DISC_ROWS), in_specs=[fs, fs, bs, bs, bs, bs], out_specs=[bs, bs, fs, fs],
                          out_shape=[jax.ShapeDtypeStruct(b_re.shape, F32)] * 2 + [jax.ShapeDtypeStruct(f_re.shape, F32)] * 2,
                          name=name)(f_re, f_im, b_re, b_im, dbb_re, dbb_im)


def disc_a_bwd(a_re, a_im, ls, dlr, dli, dfr, dfi, name):
    def body(ar_ref, ai_ref, ls_ref, dlr_ref, dli_ref, dfr_ref, dfi_ref, dar_ref, dai_ref, dls_ref):
        ar, ai = ar_ref[...], ai_ref[...]
        dt = jnp.exp(ls_ref[...])
        mag = jnp.exp(ar * dt)
        cs, sn = jnp.cos(ai * dt), jnp.sin(ai * dt)
        lr, li = mag * cs, mag * sn
        den = ar * ar + ai * ai
        nr = lr - 1.0
        f_re = (nr * ar + li * ai) / den
        f_im = (li * ar - nr * ai) / den
        dn1 = dfr_ref[...] / den
        dn2 = dfi_ref[...] / den
        dden = -(dfr_ref[...] * f_re + dfi_ref[...] * f_im) / den
        dlr_t = dlr_ref[...] + dn1 * ar - dn2 * ai
        dli_t = dli_ref[...] + dn1 * ai + dn2 * ar
        dar = dn1 * nr + dn2 * li + dden * 2.0 * ar
        dai = dn1 * li - dn2 * nr + dden * 2.0 * ai
        dmag = dlr_t * cs + dli_t * sn
        dth = dli_t * lr - dlr_t * li
        dar_ref[...] = dar + dmag * mag * dt
        dai_ref[...] = dai + dth * dt
        dls_ref[...] = jnp.sum(dmag * mag * ar + dth * ai, axis=-1, keepdims=True) * dt

    return pl.pallas_call(body, out_shape=[jax.ShapeDtypeStruct(a_re.shape, F32)] * 2 +
                          [jax.ShapeDtypeStruct(ls.shape, F32)], name=name)(a_re, a_im, ls, dlr, dli, dfr, dfi)


def _cpow(lr, li, n):
    rr, ri = None, None
    br, bi = lr, li
    while n:
        if n & 1:
            if rr is None:
                rr, ri = br, bi
            else:
                rr, ri = rr * br - ri * bi, rr * bi + ri * br
        n >>= 1
        if n:
            br, bi = br * br - bi * bi, 2.0 * br * bi
    return rr, ri


UNROLL = 4


def _seg_scan(xre, xim, lam8, pw, base, seglen, rev, init, fin_re, fin_im, ini_re, ini_im, prev=None):
    lr, li = lam8

    def rows(t):
        return pl.ds(pl.multiple_of(base + t * SEG, SEG), SEG)

    tmap = (lambda n: seglen - 1 - n) if rev else (lambda n: n)
    zero = jnp.zeros((SEG, SB), F32)

    def advance(c, t):
        a, b = c
        return lr * a - li * b + xre[rows(t), :], lr * b + li * a + xim[rows(t), :]

    fin = lax.fori_loop(0, seglen, lambda n, c: advance(c, tmap(n)), (zero, zero), unroll=UNROLL)
    fin_re[...] = fin[0]
    fin_im[...] = fin[1]
    (cr, ci), (pr, pi) = init, pw
    for i in (range(SEG - 1, -1, -1) if rev else range(SEG)):
        ini_re[pl.ds(i, 1), :] = cr
        ini_im[pl.ds(i, 1), :] = ci
        cr, ci = pr * cr - pi * ci + fin_re[pl.ds(i, 1), :], pr * ci + pi * cr + fin_im[pl.ds(i, 1), :]
    start = (ini_re[...], ini_im[...])

    def store(c, t):
        na, nb = advance(c, t)
        xre[rows(t), :] = na
        xim[rows(t), :] = nb
        return na, nb

    if prev is None:
        lax.fori_loop(0, seglen, lambda n, c: store(c, tmap(n)), start, unroll=UNROLL)
        return (cr, ci), None

    sre, sim, s_ini_re, s_ini_im = prev

    def acc_step(c, t, pre, pim):
        na, nb = store(c[:2], t)
        return na, nb, c[2] + na * pre + nb * pim, c[3] + nb * pre - na * pim

    def body(n, c):
        t = tmap(n)
        tp = t - 1 if rev else t + 1
        return acc_step(c, t, sre[rows(tp), :], sim[rows(tp), :])

    c = lax.fori_loop(0, seglen - 1, body, start + (zero, zero), unroll=UNROLL)
    c = acc_step(c, 0 if rev else seglen - 1, s_ini_re[...], s_ini_im[...])
    return (cr, ci), c[2:]


def _lam_tiles(lr, li, lens, conj=False):
    if conj:
        li = -li
    lam8 = (jnp.broadcast_to(lr, (SEG, SB)), jnp.broadcast_to(li, (SEG, SB)))
    return lam8, [_cpow(lr, li, n) for n in lens]


def _stretches(T):
    return ((0, LC // SEG), (LC, (T - LC) // SEG))


def _to_seg_order(src, dst, T):
    for base, seglen in _stretches(T):
        def body(t, carry, base=base, seglen=seglen):
            dst[pl.ds(pl.multiple_of(base + t * SEG, SEG), SEG), :] = src[pl.ds(base + t, SEG, stride=seglen), :]
            return carry
        lax.fori_loop(0, seglen, body, 0, unroll=8)


def _from_seg_order(src, dst, T):
    for base, seglen in _stretches(T):
        def body(t, carry, base=base, seglen=seglen):
            dst[pl.ds(base + t, SEG, stride=seglen), :] = src[pl.ds(pl.multiple_of(base + t * SEG, SEG), SEG), :]
            return carry
        lax.fori_loop(0, seglen, body, 0, unroll=8)


def _scan_specs(T):
    ublk = pl.BlockSpec((T, UB), lambda j: (0, j))
    lam = pl.BlockSpec((2, 1, 1, SB), lambda j: (0, j, 0, 0))
    mat = pl.BlockSpec((2, 1, UB, P), lambda j: (0, j, 0, 0))
    return ublk, lam, mat


def _dotf(a, b, mode="nn"):
    return lax.dot_general(a, b, _DN[mode], preferred_element_type=F32)


def _diag_mask():
    r = lax.broadcasted_iota(jnp.int32, (UB, SB), 0)
    c = lax.broadcasted_iota(jnp.int32, (UB, SB), 1)
    return lax.shift_right_logical(r, int(math.log2(CH))) == lax.shift_right_logical(c, int(math.log2(P)))


def _expand(m):
    p = lax.broadcasted_iota(jnp.int32, (P, SB), 0)
    c = lax.broadcasted_iota(jnp.int32, (P, SB), 1)
    tile = jnp.where(lax.bitwise_and(c, P - 1) == p, 1.0, 0.0).astype(BF16)
    wide = jnp.dot(m.astype(BF16), tile, preferred_element_type=F32)
    return jnp.where(_diag_mask(), wide, 0.0).astype(BF16)


def _collapse(full):
    c = lax.broadcasted_iota(jnp.int32, (SB, P), 0)
    p = lax.broadcasted_iota(jnp.int32, (SB, P), 1)
    pick = jnp.where(lax.bitwise_and(c, P - 1) == p, 1.0, 0.0).astype(BF16)
    return _exact_perm(jnp.where(_diag_mask(), full, 0.0), pick)


def _zero_state():
    return jnp.zeros((1, SB), F32), jnp.zeros((1, SB), F32)


def scan_fwd(u, lam_re, lam_im, bre, bim, cre, cim, name):
    T = u.shape[0]
    s_ctx, s_lat = LC // SEG, (T - LC) // SEG

    def body(u_ref, lr_ref, li_ref, bre_ref, bim_ref, cre_ref, cim_ref, y_ref, us, ys, sre, sim, fre, fim, ire, iim):
        _to_seg_order(u_ref, us, T)
        ub = us[...].astype(BF16)
        for d in range(2):
            lam8, (pw_c, pw_l) = _lam_tiles(lr_ref[d, 0], li_ref[d, 0], (s_ctx, s_lat))
            sre[...] = _dotf(ub, _expand(bre_ref[d, 0]))
            sim[...] = _dotf(ub, _expand(bim_ref[d, 0]))
            end_c, _ = _seg_scan(sre, sim, lam8, pw_c, 0, s_ctx, bool(d), _zero_state(), fre, fim, ire, iim)
            _seg_scan(sre, sim, lam8, pw_l, LC, s_lat, bool(d), end_c, fre, fim, ire, iim)
            y = (_dotf(sre[...].astype(BF16), _expand(cre_ref[d, 0]), "nt")
                 - _dotf(sim[...].astype(BF16), _expand(cim_ref[d, 0]), "nt"))
            if d == 0:
                ys[...] = y
            else:
                ys[...] += y
        _from_seg_order(ys, y_ref, T)

    ublk, lam, mat = _scan_specs(T)
    return pl.pallas_call(
        body, grid=(NJ,), in_specs=[ublk, lam, lam, mat, mat, mat, mat], out_specs=ublk,
        out_shape=jax.ShapeDtypeStruct((T, G * CH), F32),
        scratch_shapes=[pltpu.VMEM((T, UB), F32)] * 2 + [pltpu.VMEM((T, SB), F32)] * 2 + [pltpu.VMEM((SEG, SB), F32)] * 4,
        compiler_params=_cp(("arbitrary",)), name=name)(u, lam_re, lam_im, bre, bim, cre, cim)


def scan_bwd(u, dy, lam_re, lam_im, bre, bim, cre, cim, name):
    T = u.shape[0]
    s_ctx, s_lat = LC // SEG, (T - LC) // SEG

    def body(u_ref, dy_ref, lr_ref, li_ref, bre_ref, bim_ref, cre_ref, cim_ref,
             du_ref, dlr_ref, dli_ref, dbre_ref, dbim_ref, dcre_ref, dcim_ref,
             us, dys, dus, sre, sim, gre, gim, fre, fim, ic_re, ic_im, il_re, il_im, jre, jim):
        _to_seg_order(u_ref, us, T)
        _to_seg_order(dy_ref, dys, T)
        ub, dyb = us[...].astype(BF16), dys[...].astype(BF16)
        for d in range(2):
            rev = bool(d)
            lam8, (pw_c, pw_l) = _lam_tiles(lr_ref[d, 0], li_ref[d, 0], (s_ctx, s_lat))
            cam8, (cw_c, cw_l) = _lam_tiles(lr_ref[d, 0], li_ref[d, 0], (s_ctx, s_lat), conj=True)
            bre_v, bim_v = _expand(bre_ref[d, 0]), _expand(bim_ref[d, 0])
            sre[...] = _dotf(ub, bre_v)
            sim[...] = _dotf(ub, bim_v)
            end_c, _ = _seg_scan(sre, sim, lam8, pw_c, 0, s_ctx, rev, _zero_state(), fre, fim, ic_re, ic_im)
            _seg_scan(sre, sim, lam8, pw_l, LC, s_lat, rev, end_c, fre, fim, il_re, il_im)
            gre[...] = _dotf(dyb, _expand(cre_ref[d, 0]))
            gim[...] = -_dotf(dyb, _expand(cim_ref[d, 0]))
            end_g, acc_l = _seg_scan(gre, gim, cam8, cw_l, LC, s_lat, not rev, _zero_state(), fre, fim, jre, jim,
                                     prev=(sre, sim, il_re, il_im))
            _, acc_c = _seg_scan(gre, gim, cam8, cw_c, 0, s_ctx, not rev, end_g, fre, fim, jre, jim,
                                 prev=(sre, sim, ic_re, ic_im))
            dlr_ref[d, 0] = _sum0(acc_l[0] + acc_c[0])
            dli_ref[d, 0] = _sum0(acc_l[1] + acc_c[1])
            grb, gib = gre[...].astype(BF16), gim[...].astype(BF16)
            du = _dotf(grb, bre_v, "nt") + _dotf(gib, bim_v, "nt")
            if d == 0:
                dus[...] = du
            else:
                dus[...] += du
            dbre_ref[d, 0] = _collapse(_dotf(ub, grb, "tn"))
            dbim_ref[d, 0] = _collapse(_dotf(ub, gib, "tn"))
            dcre_ref[d, 0] = _collapse(_dotf(dyb, sre[...].astype(BF16), "tn"))
            dcim_ref[d, 0] = -_collapse(_dotf(dyb, sim[...].astype(BF16), "tn"))
        _from_seg_order(dus, du_ref, T)

    ublk, lam, mat = _scan_specs(T)
    lam_s = jax.ShapeDtypeStruct(lam_re.shape, F32)
    mat_s = jax.ShapeDtypeStruct(bre.shape, F32)
    return pl.pallas_call(
        body, grid=(NJ,), in_specs=[ublk, ublk, lam, lam, mat, mat, mat, mat],
        out_specs=[ublk, lam, lam, mat, mat, mat, mat],
        out_shape=[jax.ShapeDtypeStruct((T, G * CH), F32), lam_s, lam_s, mat_s, mat_s, mat_s, mat_s],
        scratch_shapes=[pltpu.VMEM((T, UB), F32)] * 3 + [pltpu.VMEM((T, SB), F32)] * 4 + [pltpu.VMEM((SEG, SB), F32)] * 8,
        compiler_params=_cp(("arbitrary",)), name=name)(u, dy, lam_re, lam_im, bre, bim, cre, cim)


class Exchange:
    def __init__(self, xs, modes):
        self.n = len(xs)
        self.modes = [modes] * self.n if isinstance(modes, (str, int)) else list(modes)
        self.out_shape = [jax.ShapeDtypeStruct(self._shape(x, md), x.dtype) for x, md in zip(xs, self.modes)]
        self.scratch = [pltpu.SemaphoreType.DMA((NDEV - 1, self.n)), pltpu.SemaphoreType.DMA((NDEV - 1, self.n)),
                        pltpu.SemaphoreType.DMA((self.n,))]
        self.specs = [pl.BlockSpec(memory_space=pl.ANY)] * self.n

    @staticmethod
    def _shape(x, mode):
        if mode == "gather":
            return (NDEV,) + tuple(x.shape)
        return tuple(x.shape) if mode == "lead" else (NDEV, x.shape[0], mode) + tuple(x.shape[2:])

    @staticmethod
    def _piece(x_ref, mode, dev):
        if mode == "gather":
            return x_ref
        return x_ref.at[dev] if mode == "lead" else x_ref.at[:, pl.ds(dev * mode, mode)]

    def _copies(self, x_refs, out_refs, sems):
        send_sems, recv_sems, local_sems = sems
        mx, my, mc = lax.axis_index("x"), lax.axis_index("y"), lax.axis_index("c")
        me = 4 * mx + 2 * my + mc
        peer_of = lambda k: (1 - mx if k & 4 else mx, 1 - my if k & 2 else my, 1 - mc if k & 1 else mc)
        local, first, relay, arrivals = [], [], [], []
        for a, (x_ref, out_ref) in enumerate(zip(x_refs, out_refs)):
            mode = self.modes[a]
            local.append(pltpu.make_async_copy(self._piece(x_ref, mode, me), out_ref.at[me], local_sems.at[a]))

            def remote(src, dst, k, pair, a=a):
                return pltpu.make_async_remote_copy(src_ref=src, dst_ref=dst, send_sem=send_sems.at[pair, a],
                                                    recv_sem=recv_sems.at[pair, a], device_id=peer_of(k), device_id_type=MESH_T)

            for k in range(1, NDEV):
                peer = peer_of(k)
                pid = 4 * peer[0] + 2 * peer[1] + peer[2]
                if mode != "gather":
                    src = self._piece(x_ref, mode, pid)
                    first.append(remote(src, out_ref.at[me], k, k - 1))
                    arrivals.append(remote(src, out_ref.at[pid], k, k - 1))
                elif k == 1:
                    first.append(remote(x_ref, out_ref.at[me], k, k - 1))
                    arrivals.append(remote(x_ref, out_ref.at[pid], k, k - 1))
                elif k % 2 == 0:
                    first.append(remote(x_ref, out_ref.at[me], k, k - 1))
                    relay.append((remote(x_ref, out_ref.at[pid], k, k - 1), remote(out_ref.at[pid], out_ref.at[pid], 1, k)))
                else:
                    arrivals.append(remote(x_ref, out_ref.at[pid], 1, k - 1))
        return local, first, relay, arrivals

    def start(self, x_refs, out_refs, sems):
        local, first, _, _ = self._copies(x_refs, out_refs, sems)
        for cp in local + first:
            cp.start()

    def finish(self, x_refs, out_refs, sems):
        local, first, relay, arrivals = self._copies(x_refs, out_refs, sems)
        for arrival, onward in relay:
            arrival.wait_recv()
            onward.start()
        for cp in arrivals:
            cp.wait_recv()
        for cp in first + [onward for _, onward in relay]:
            cp.wait_send()
        for cp in local:
            cp.wait()


def exchange(xs, modes, name):
    ex = Exchange(xs, modes)
    n = ex.n

    def body(*refs):
        ex.start(refs[:n], refs[n:2 * n], refs[2 * n:])
        ex.finish(refs[:n], refs[n:2 * n], refs[2 * n:])

    return pl.pallas_call(body, in_specs=ex.specs, out_specs=ex.specs, out_shape=ex.out_shape, scratch_shapes=ex.scratch,
                          compiler_params=pltpu.CompilerParams(has_side_effects=True), name=name)(*xs)


def _dot_f32(a, b, dn):
    return lax.dot_general(a, b, dn, preferred_element_type=F32, precision=lax.Precision.HIGHEST)


def ada_fwd(cg, c_ctx, ada_w, ada_b_loc, name):
    W = ada_w.shape[2]

    def body(cg_ref, cc_ref, w_ref, b_ref, o_ref):
        a = jnp.concatenate([_silu(cg_ref[...]), jnp.broadcast_to(_silu(cc_ref[...]), (NDEV, D))], axis=0)
        for i in range(2):
            o_ref[i] = _dot_f32(a, w_ref[i], _DN["nn"]) + b_ref[i]

    return pl.pallas_call(body, out_shape=jax.ShapeDtypeStruct((2, 2 * NDEV, W), F32),
                          compiler_params=_cp(), name=name)(cg, c_ctx, ada_w, ada_b_loc)


def ada_bwd(cg, c_ctx, ada_w, dm_loc, dm_all, name):
    W = ada_w.shape[2]

    def body(cg_ref, cc_ref, w_ref, dl_ref, da_ref, gw_ref, dcc_ref, gb_ref):
        a = jnp.concatenate([_silu(cg_ref[...]), jnp.broadcast_to(_silu(cc_ref[...]), (NDEV, D))], axis=0)
        dcc = jnp.zeros((1, D), F32)
        for i in range(2):
            dl = dl_ref[i]
            gw_ref[i] = _dot_f32(a, dl, _DN["tn"])
            dctx = jnp.sum(dl[NDEV:], axis=0, keepdims=True)
            dcc = dcc + _dot_f32(dctx, w_ref[i], _DN["nt"])
        dcc_ref[...] = dcc
        gb_ref[...] = jnp.sum(da_ref[...], axis=0)

    return pl.pallas_call(body, out_shape=[jax.ShapeDtypeStruct((2, D, W), F32), jax.ShapeDtypeStruct((1, D), F32),
                                           jax.ShapeDtypeStruct((2, 3 * D), F32)],
                          compiler_params=_cp(), name=name)(cg, c_ctx, ada_w, dm_loc, dm_all)


def cctx_finish(parts, c_ctx, name):
    def body(p_ref, cc_ref, o_ref):
        o_ref[...] = jnp.sum(p_ref[...], axis=0, keepdims=True) * _dsilu(cc_ref[...])

    return pl.pallas_call(body, out_shape=jax.ShapeDtypeStruct((1, D), F32), name=name)(parts, c_ctx)


def _adamw_update(g_ref, w_ref, m_ref, v_ref, go_ref, d_ref, mo_ref, vo_ref):
    g = g_ref[0].astype(F32)
    for s in range(1, g_ref.shape[0]):
        g = g + g_ref[s].astype(F32)
    mn = B1 * m_ref[...] + (1.0 - B1) * g
    vn = B2 * v_ref[...] + (1.0 - B2) * g * g
    go_ref[...] = g
    mo_ref[...] = mn
    vo_ref[...] = vn
    d_ref[...] = -LR * ((mn * (1.0 / (1.0 - B1 ** STEP))) / (jnp.sqrt(vn * (1.0 / (1.0 - B2 ** STEP))) + AEPS) + WD * w_ref[...])


def adamw(gstack, w, m, v, name, tr=256):
    n, R, C = gstack.shape
    tr = max(t for t in range(8, min(tr, R) + 1, 8) if R % t == 0)
    spec = pl.BlockSpec((tr, C), lambda i: (i, 0))
    return pl.pallas_call(_adamw_body(1), grid=(R // tr,),
                          in_specs=[pl.BlockSpec((n, tr, C), lambda i: (0, i, 0)), spec, spec, spec],
                          out_specs=[spec] * 4, out_shape=[jax.ShapeDtypeStruct((R, C), F32)] * 4,
                          compiler_params=_cp(("parallel",)), name=name)(gstack, w, m, v)


def _adamw_body(k):
    def body(*refs):
        for t in range(k):
            _adamw_update(*refs[4 * t:4 * t + 4], *refs[4 * k + 4 * t:4 * k + 4 * t + 4])
    return body


def adamw_multi(items, grid, name):
    k = len(items)
    ins, in_specs, out_specs, out_shape = [], [], [], []
    for g, g_spec, w, m, v, w_spec in items:
        ins += [g, w, m, v]
        in_specs += [g_spec, w_spec, w_spec, w_spec]
    for g, g_spec, w, m, v, w_spec in items:
        out_specs += [w_spec] * 4
        out_shape += [jax.ShapeDtypeStruct(w.shape, F32)] * 4
    res = pl.pallas_call(_adamw_body(k), grid=grid, in_specs=in_specs, out_specs=out_specs, out_shape=out_shape,
                         compiler_params=_cp(("arbitrary",) * len(grid)), name=name)(*ins)
    return [res[4 * t:4 * t + 4] for t in range(k)]


def _whole(a, grid_rank):
    zeros = (0,) * a.ndim
    return pl.BlockSpec(a.shape, lambda *idx: zeros)


def sum_slots(xs, name):
    def body(*refs):
        for x_ref, o_ref in zip(refs[:len(xs)], refs[len(xs):]):
            acc = x_ref[0]
            for s in range(1, NDEV):
                acc = acc + x_ref[s]
            o_ref[...] = acc

    return pl.pallas_call(body, out_shape=[jax.ShapeDtypeStruct(x.shape[1:], F32) for x in xs],
                          compiler_params=_cp(), name=name)(*xs)


def _col_shards(g):
    R, N = g.shape
    return g.reshape(R, NDEV, N // NDEV).transpose(1, 0, 2)


def _vec2(v):
    return jnp.broadcast_to(v.reshape(1, 1, -1), (2, 1, v.size))


SHARD_ROWS = {"mla_w_in": 192, "mla_w_uq": 192, "mla_w_ukv": 256, "s5_w_in": 256}


def _t_shard(wsh, rows):
    t = wsh[0].T.astype(BF16)
    return jnp.pad(t, ((0, rows - t.shape[0]), (0, 0)))


def _win_order():
    w = IN_W // NDEV
    perm = np.zeros((IN_WP, NDEV * SHARD_ROWS["mla_w_in"]), np.float32)
    first = QL + KVL + ROPE
    for c in range(IN_W):
        n = c + HEADS * VD if c < first else c - first
        perm[n, (c // w) * SHARD_ROWS["mla_w_in"] + c % w] = 1.0
    return jnp.asarray(perm, BF16)


def local_step(ctx, x, tgt, mod, Wt, small, l1_shards):
    T = LC + x.shape[0]
    xa = ("cat", ctx, x)
    sh = [mod[i, :, None, 0:D] for i in range(2)]
    sc = [mod[i, :, None, D:2 * D] for i in range(2)]
    gt = [mod[i, :, None, 2 * D:] for i in range(2)]
    ng = [_vec2(small["norm_g"][i]) for i in range(2)]
    qg, kvg = _vec2(small["mla_q_norm"]), _vec2(small["mla_kv_norm"])
    cosf, sinf, pm, pmt = _rope_tables(T)

    (h0, p0, cqn, ckvn), _ = rowwise(st_l0_pre, [xa], [ng[0], sc[0], sh[0], qg, kvg],
                                     [(D, BF16), (IN_WP, F32), (QL, BF16), (KVL, BF16)], [], "l0_pre", mats=[Wt["mla_w_in"]])
    z0, cq, ckv = (p0, 0, HEADS * VD), (p0, HEADS * VD // QL, QL), (p0, (HEADS * VD + QL) // KVL, KVL)
    Q = project_q(cqn, Wt["mla_w_uq"], cosf, sinf, pm, "l0_uq")
    K, V = project_kv(ckvn, Wt["mla_w_ukv"], p0, (HEADS * VD + QL + KVL) // 128, "l0_ukv")
    (o, lse), got = attn_fwd(Q, K, V, "l0_attn", rode=l1_shards, modes="gather")
    Wt, small = dict(Wt), dict(small)
    for n, a in zip(L1_BIG, got):
        Wt[n] = a.reshape(-1, a.shape[-1])
    vecs = lax.bitcast_convert_type(got[-1].reshape(NDEV, 2, -1, 2), F32)
    small["s5_d"], small["s5_b_glu"] = vecs[:, 0, :].reshape(D), vecs[:, 1, :].reshape(D)
    o2 = o.transpose(1, 0, 2).reshape(T, HEADS * VD)
    (og, out0, x1), _ = rowwise(st_l0_post, [o2, z0, xa], [gt[0]], [(D, BF16), (D, BF16), (D, F32)], [], "l0_post",
                                mats=[Wt["mla_w_out"]])

    ls = small["s5_log_step"].reshape(2, G, 1)
    a_re, a_im = small["s5_a_re"].reshape(2, G, P), small["s5_a_im"].reshape(2, G, P)
    b_re, b_im = small["s5_b_re"].reshape(2, G * P, CH), small["s5_b_im"].reshape(2, G * P, CH)
    lam_re, lam_im, f_re, f_im = disc_fwd(a_re, a_im, ls, "s5_disc")
    f_re2, f_im2 = f_re.reshape(2, G * P, 1), f_im.reshape(2, G * P, 1)
    bb_re, bb_im = disc_b(f_re2, f_im2, b_re, b_im, "s5_disc_b")
    compact = lambda m: m.reshape(2, NJ, UB, P)
    bre = compact(bb_re.reshape(2, G, P, CH).transpose(0, 1, 3, 2))
    bim = compact(bb_im.reshape(2, G, P, CH).transpose(0, 1, 3, 2))
    cre, cim = compact(small["s5_c_re"]), compact(small["s5_c_im"])
    lam_re4, lam_im4 = lam_re.reshape(2, NJ, 1, SB), lam_im.reshape(2, NJ, 1, SB)

    (h1, p1), _ = rowwise(st_l1_pre, [x1], [ng[1], sc[1], sh[1]], [(D, BF16), (2 * D, F32)], [], "l1_pre", mats=[Wt["s5_w_in"]])
    u, z1 = (p1, 0, D), (p1, 1, D)
    yssm = scan_fwd(p1, lam_re4, lam_im4, bre, bim, cre, cim, "s5_scan")
    dvec, bglu = _vec2(small["s5_d"]), _vec2(small["s5_b_glu"])
    fg = _vec2(small["final_g"])
    lat_mask = jnp.stack([jnp.zeros((1, D), F32), jnp.ones((1, D), F32)])
    (y, y1b, gl, y3, out1, dx2), (dfg, lvec) = rowwise(
        st_l1_mlp, [yssm, u, z1, x1, ("lat", tgt)], [dvec, bglu, gt[1], fg, lat_mask],
        [(D, F32), (D, BF16), (D, BF16), (D, BF16), (D, BF16), (D, F32)], [D, 128], "l1_mlp",
        mats=[Wt["s5_w_glu"], Wt["s5_w_out"]])

    (dz1, dy, du_d), (dgt1, dbglu, dd), (g_w_out5, g_w_glu) = rowwise(
        st_l1_mlp_bwd, [dx2, out1, y3, y, gl, z1, u, y1b], [gt[1], bglu, dvec], [(D, BF16), (D, F32), (D, F32)], [D, D, D],
        "l1_mlp_b", mats=[Wt["s5_w_out"], Wt["s5_w_glu"]], out_accs=[(D, D), (D, D)])
    du_s, dlr, dli, dbre, dbim, dcre, dcim = scan_bwd(p1, dy, lam_re4, lam_im4, bre, bim, cre, cim, "s5_scan_b")
    dbb_re = dbre.reshape(2, G, CH, P).transpose(0, 1, 3, 2).reshape(2, G * P, CH)
    dbb_im = dbim.reshape(2, G, CH, P).transpose(0, 1, 3, 2).reshape(2, G * P, CH)
    g_c_re, g_c_im = dcre.reshape(2, G, CH, P), dcim.reshape(2, G, CH, P)
    g_b_re, g_b_im, dfr, dfi = disc_b_bwd(f_re2, f_im2, b_re, b_im, dbb_re, dbb_im, "s5_disc_b_b")
    g_a_re, g_a_im, g_ls = disc_a_bwd(a_re, a_im, ls, dlr.reshape(2, G, P), dli.reshape(2, G, P),
                                      dfr.reshape(2, G, P), dfi.reshape(2, G, P), "s5_disc_b_a")
    (dx1,), (dsh1, dsc1, dng1), (g_w_in5,) = rowwise(
        st_l1_tail_bwd, [du_d, du_s, dz1, h1, x1, dx2], [ng[1], sc[1]], [(D, F32)], [D, D, D], "l1_pre_b",
        mats=[Wt["s5_w_in"]], out_accs=[(D, 2 * D)])
    g_w_in5 = _col_shards(g_w_in5)

    (do2, dz0), (dgt0,), (g_w_out,) = rowwise(st_l0_post_bwd, [dx1, out0, og, o2, z0], [gt[0]], [(D, F32), (D, F32)], [D],
                                              "l0_post_b", mats=[Wt["mla_w_out"]], out_accs=[(D, D)])
    doh = do2.reshape(T, HEADS, VD).transpose(1, 0, 2)
    rows8 = lambda g: g.reshape(NDEV, -1, g.shape[-1])
    both = lambda s: s[0, 0] + s[1, 0]
    dense = lambda g: g.reshape(2, G * P * CH // 128, 128)
    chunks = [dense(g_b_re), dense(g_b_im), g_c_re, g_c_im]
    l1_send = [g_w_in5, rows8(g_w_glu), rows8(g_w_out5), rows8(g_w_out),
               both(dd).reshape(NDEV, 1, -1), both(dbglu).reshape(NDEV, 1, -1)]
    (dQ, dK, dV), l1_recv = attn_bwd(Q, K, V, o, lse, doh, "l0_attn_b", rode=l1_send + chunks,
                                     modes=["lead"] * len(l1_send) + [a.shape[1] // NDEV for a in chunks])
    dqh = rope(dQ, cosf, sinf, pmt, True, BF16, "l0_rope_q_b", scale=SCALE)
    dq = dqh.transpose(1, 0, 2).reshape(T, HEADS * QK)
    dkv, dkr = split_kv_grads(dK, dV, "l0_kv_b")
    (grad_x,), (dqg, dkvg, dsh0, dsc0, dng0), (g_uq, g_ukv, g_p) = rowwise(
        st_l0_tail_bwd, [dq, dkv, dkr, dz0, cq, ckv, cqn, ckvn, h0, xa, dx1], [qg, kvg, ng[0], sc[0]],
        [(D, F32, "lat")], [QL, KVL, D, D, D], "l0_pre_b", mats=[Wt["mla_w_uq"], Wt["mla_w_ukv"], Wt["mla_w_in"]],
        out_accs=[(QL, HEADS * QK), (KVL, HEADS * KVW), (D, IN_WP)])
    g_w_uq, g_w_ukv = _col_shards(g_uq).astype(BF16), _col_shards(g_ukv).astype(BF16)
    g_w_in = _col_shards(jnp.concatenate([g_p[:, HEADS * VD:IN_W], g_p[:, :HEADS * VD]], axis=1)).astype(BF16)

    dmod = jnp.stack([jnp.concatenate([dsh0, dsc0, dgt0], axis=-1)[:, 0], jnp.concatenate([dsh1, dsc1, dgt1], axis=-1)[:, 0]])
    gbig = {"mla_w_in": g_w_in, "mla_w_uq": g_w_uq, "mla_w_ukv": g_w_ukv}
    gsmall = {"norm_g": jnp.stack([both(dng0), both(dng1)]), "mla_q_norm": both(dqg), "mla_kv_norm": both(dkvg),
              "s5_a_re": g_a_re, "s5_a_im": g_a_im, "s5_log_step": g_ls, "final_g": dfg[1, 0]}
    return lvec[1], grad_x, dmod, gbig, gsmall, l1_recv


COL_SHARDED = ("mla_w_in", "mla_w_uq", "mla_w_ukv", "s5_w_in")
ROW_SHARDED = ("mla_w_out", "s5_w_glu", "s5_w_out")
VEC_SHARDED = ("s5_d", "s5_b_glu")
BIG = COL_SHARDED + ROW_SHARDED
L0_BIG = ("mla_w_in", "mla_w_uq", "mla_w_ukv")
L1_BIG = ("s5_w_in", "s5_w_glu", "s5_w_out", "mla_w_out")
BITS16 = jnp.bfloat16
SMALL_RS = ("norm_g", "mla_q_norm", "mla_kv_norm", "s5_a_re", "s5_a_im", "s5_log_step", "s5_b_re", "s5_b_im",
            "s5_c_re", "s5_c_im", "final_g")
CHUNKED = ("s5_b_re", "s5_b_im", "s5_c_re", "s5_c_im")
DENSE = ("s5_b_re", "s5_b_im")
TINY = ("norm_g", "mla_q_norm", "mla_kv_norm", "s5_a_re", "s5_a_im", "s5_log_step", "final_g")
ORDER = ("c_ctx", "ada_w", "ada_b", "norm_g", "mla_w_in", "mla_q_norm", "mla_w_uq", "mla_kv_norm", "mla_w_ukv",
         "mla_w_out", "s5_w_in", "s5_a_re", "s5_a_im", "s5_log_step", "s5_b_re", "s5_b_im", "s5_c_re", "s5_c_im",
         "s5_d", "s5_w_glu", "s5_b_glu", "s5_w_out", "final_g")


def kernel(x, c, ctx, c_ctx, ada_w, ada_b, norm_g, mla_w_in, mla_q_norm, mla_w_uq, mla_kv_norm, mla_w_ukv, mla_w_out, s5_w_in, s5_a_re, s5_a_im, s5_log_step, s5_b_re, s5_b_im, s5_c_re, s5_c_im, s5_d, s5_w_glu, s5_b_glu, s5_w_out, final_g, loss_target, m_c_ctx, m_ada_w, m_ada_b, m_norm_g, m_mla_w_in, m_mla_q_norm, m_mla_w_uq, m_mla_kv_norm, m_mla_w_ukv, m_mla_w_out, m_s5_w_in, m_s5_a_re, m_s5_a_im, m_s5_log_step, m_s5_b_re, m_s5_b_im, m_s5_c_re, m_s5_c_im, m_s5_d, m_s5_w_glu, m_s5_b_glu, m_s5_w_out, m_final_g, v_c_ctx, v_ada_w, v_ada_b, v_norm_g, v_mla_w_in, v_mla_q_norm, v_mla_w_uq, v_mla_kv_norm, v_mla_w_ukv, v_mla_w_out, v_s5_w_in, v_s5_a_re, v_s5_a_im, v_s5_log_step, v_s5_b_re, v_s5_b_im, v_s5_c_re, v_s5_c_im, v_s5_d, v_s5_w_glu, v_s5_b_glu, v_s5_w_out, v_final_g):
    w = dict(c_ctx=c_ctx, ada_w=ada_w, ada_b=ada_b, norm_g=norm_g, mla_w_in=mla_w_in, mla_q_norm=mla_q_norm,
             mla_w_uq=mla_w_uq, mla_kv_norm=mla_kv_norm, mla_w_ukv=mla_w_ukv, mla_w_out=mla_w_out, s5_w_in=s5_w_in,
             s5_a_re=s5_a_re, s5_a_im=s5_a_im, s5_log_step=s5_log_step, s5_b_re=s5_b_re, s5_b_im=s5_b_im,
             s5_c_re=s5_c_re, s5_c_im=s5_c_im, s5_d=s5_d, s5_w_glu=s5_w_glu, s5_b_glu=s5_b_glu, s5_w_out=s5_w_out,
             final_g=final_g)
    m = dict(c_ctx=m_c_ctx, ada_w=m_ada_w, ada_b=m_ada_b, norm_g=m_norm_g, mla_w_in=m_mla_w_in, mla_q_norm=m_mla_q_norm,
             mla_w_uq=m_mla_w_uq, mla_kv_norm=m_mla_kv_norm, mla_w_ukv=m_mla_w_ukv, mla_w_out=m_mla_w_out,
             s5_w_in=m_s5_w_in, s5_a_re=m_s5_a_re, s5_a_im=m_s5_a_im, s5_log_step=m_s5_log_step, s5_b_re=m_s5_b_re,
             s5_b_im=m_s5_b_im, s5_c_re=m_s5_c_re, s5_c_im=m_s5_c_im, s5_d=m_s5_d, s5_w_glu=m_s5_w_glu,
             s5_b_glu=m_s5_b_glu, s5_w_out=m_s5_w_out, final_g=m_final_g)
    v = dict(c_ctx=v_c_ctx, ada_w=v_ada_w, ada_b=v_ada_b, norm_g=v_norm_g, mla_w_in=v_mla_w_in, mla_q_norm=v_mla_q_norm,
             mla_w_uq=v_mla_w_uq, mla_kv_norm=v_mla_kv_norm, mla_w_ukv=v_mla_w_ukv, mla_w_out=v_mla_w_out,
             s5_w_in=v_s5_w_in, s5_a_re=v_s5_a_re, s5_a_im=v_s5_a_im, s5_log_step=v_s5_log_step, s5_b_re=v_s5_b_re,
             s5_b_im=v_s5_b_im, s5_c_re=v_s5_c_re, s5_c_im=v_s5_c_im, s5_d=v_s5_d, s5_w_glu=v_s5_w_glu,
             s5_b_glu=v_s5_b_glu, s5_w_out=v_s5_w_out, final_g=v_final_g)

    me = 4 * lax.axis_index("x") + 2 * lax.axis_index("y") + lax.axis_index("c")
    WA = ada_w.shape[2]

    def shard(n):
        return _t_shard(w[n], SHARD_ROWS[n]) if n in COL_SHARDED else w[n][0].astype(BF16)

    wgot = exchange([c] + [shard(n) for n in L0_BIG], "gather", "gather_w")

    cg = wgot[0].reshape(NDEV, D)
    cc2 = c_ctx.reshape(1, D)
    ada_b_loc = lax.dynamic_slice_in_dim(ada_b.reshape(2, 3 * D // WA, WA), me, 1, axis=1)
    part = ada_fwd(cg, cc2, ada_w, ada_b_loc, "ada_fwd")
    pg = exchange([part], "gather", "gather_mod")[0]
    mod_l = lax.dynamic_index_in_dim(pg, me, axis=2, keepdims=False).transpose(1, 0, 2).reshape(2, 3 * D)
    mod_c = pg[:, :, NDEV, :].transpose(1, 0, 2).reshape(2, 3 * D)
    mod = jnp.stack([mod_c, mod_l], axis=1)

    Wt = {n: a.reshape(-1, a.shape[-1]) for n, a in zip(L0_BIG, wgot[1:])}
    Wt["mla_w_in"] = mm(_win_order(), Wt["mla_w_in"], "nn", "w_in_order", out_dtype=BF16)
    vec_bits = lax.bitcast_convert_type(jnp.concatenate([s5_d, s5_b_glu], axis=0), BITS16).reshape(2, -1)
    small = {n: w[n] for n in SMALL_RS}

    lvec, grad_x, dmod, gbig, gsmall, l1_recv = local_step(ctx[0], x[0], loss_target[0], mod, Wt, small,
                                                           [shard(n) for n in L1_BIG] + [vec_bits])
    grad_x = grad_x[None]

    per_dev = G // NDEV
    recv = dict(zip(L0_BIG, exchange([gbig[n] for n in L0_BIG], "lead", "scatter_grads")))
    recv.update(dict(zip(L1_BIG + VEC_SHARDED, l1_recv)))
    out = {}

    def keep(n, res):
        for key, arr in zip("gdmv", res):
            out[key, n] = arr.reshape(w[n].shape)

    for n in BIG:
        keep(n, adamw(recv[n], w[n][0], m[n][0], v[n][0], "adamw_" + n))
    reduced = sum_slots(l1_recv[len(L1_BIG + VEC_SHARDED):], "sum_chunks")

    kshape = lambda n: w[n].shape if w[n].ndim > 1 else (1, w[n].size)
    flat = jnp.concatenate([gsmall[n].reshape(-1) for n in TINY] + [dmod.reshape(-1), lvec.reshape(-1)])[None]
    bb_all, cc_all, flat_all = exchange([jnp.stack(reduced[:2]), jnp.stack(reduced[2:]), flat], "gather", "gather_small")
    chunk_all = [bb_all[:, 0], bb_all[:, 1], cc_all[:, 0], cc_all[:, 1]]
    tiny_all, off = [], 0
    for n in TINY:
        tiny_all.append(flat_all[:, 0, off:off + w[n].size].reshape((NDEV,) + kshape(n)))
        off += w[n].size
    dm_all = flat_all[:, 0, off:off + dmod.size].reshape((NDEV,) + dmod.shape)
    loss = sum_slots([flat_all[:, :, off + dmod.size:]], "loss_sum")[0][0, 0]

    dm_cols = lax.dynamic_slice_in_dim(dm_all.reshape(NDEV, 2, 2, 3 * D // WA, WA), me, 1, axis=3)[:, :, :, 0]
    dm_loc = jnp.concatenate([dm_cols[:, :, 1].transpose(1, 0, 2), dm_cols[:, :, 0].transpose(1, 0, 2)], axis=1)
    g_ada_w, dcc_part, g_ada_b = ada_bwd(cg, cc2, ada_w, dm_loc, dm_all.transpose(0, 2, 1, 3).reshape(2 * NDEV, 2, 3 * D), "ada_bwd")
    dcc_all = exchange([dcc_part], "gather", "gather_dcc")[0].reshape(NDEV, D)
    g_c_ctx = cctx_finish(dcc_all, cc2, "cctx_finish")

    flat2 = lambda t: t.reshape(-1, t.shape[-1])
    keep("ada_w", adamw(flat2(g_ada_w)[None], flat2(ada_w), flat2(m_ada_w), flat2(v_ada_w), "adamw_ada"))
    items = []
    halves = 2
    for n, g in zip(CHUNKED, chunk_all):
        blk = (1, 1, G // halves) + w[n].shape[3:]
        g = jnp.moveaxis(g, 0, 1).reshape(w[n].shape)
        g_spec = pl.BlockSpec((1,) + blk, lambda d, s: (0, 0, d, s, 0, 0))
        items.append((g[None], g_spec, w[n], m[n], v[n], pl.BlockSpec(blk, lambda d, s: (0, d, s, 0, 0))))
    for n, res in zip(CHUNKED, adamw_multi(items, (2, halves), "adamw_bc")):
        keep(n, res)
    tiny_g = dict(zip(TINY, tiny_all))
    tiny_g.update({n: recv[n] for n in VEC_SHARDED})
    tiny_g["c_ctx"], tiny_g["ada_b"] = g_c_ctx[None], g_ada_b[None]
    names = list(tiny_g)
    items = [(tiny_g[n], _whole(tiny_g[n], 1)) + tuple(t[n].reshape(kshape(n)) for t in (w, m, v))
             + (pl.BlockSpec(kshape(n), lambda i, r=len(kshape(n)): (0,) * r),) for n in names]
    for n, res in zip(names, adamw_multi(items, (1,), "adamw_small")):
        keep(n, res)

    return (loss, grad_x, *[out["g", n] for n in ORDER], *[out["d", n] for n in ORDER],
            *[out["m", n] for n in ORDER], *[out["v", n] for n in ORDER])
```

```python
import math

import numpy as np
import jax
import jax.numpy as jnp
from jax import lax
from jax.experimental import pallas as pl
from jax.experimental.pallas import tpu as pltpu

F32 = jnp.float32
BF16 = jnp.bfloat16

D = 1024
L = 2048
LC = 256
NDEV = 8
GRID_W = 64
EPS = 1e-6
HEADS = 16
NOPE = 64
ROPE = 32
QK = NOPE + ROPE
VD = 64
IN_W = 256 + 128 + ROPE + HEADS * 64
IN_WP = 1536
QL = 256
KVL = 128
SCALE = QK ** -0.5
LOG2E = math.log2(math.e)
THETA = 10000.0
G = 64
P = 64
CH = 16
GB = 8
NJ = G // GB
UB = GB * CH
SB = GB * P
SEG = 8
TB = 256
VMEM_LIMIT = 56 * 1024 * 1024
B1, B2, LR, AEPS, WD, STEP = 0.9, 0.999, 0.001, 1e-8, 0.01, 10
MESH_T = pl.DeviceIdType.MESH


def _cp(sem=None):
    return pltpu.CompilerParams(dimension_semantics=sem, vmem_limit_bytes=VMEM_LIMIT)


def _sig(x):
    return 1.0 / (1.0 + jnp.exp(-x))


def _silu(x):
    return x * _sig(x)


def _dsilu(x):
    s = _sig(x)
    return s * (1.0 + x * (1.0 - s))


_GK = math.sqrt(2.0 / math.pi)


def _gelu(x):
    return 0.5 * x * (1.0 + jnp.tanh(_GK * (x + 0.044715 * x * x * x)))


def _dgelu(x):
    t = jnp.tanh(_GK * (x + 0.044715 * x * x * x))
    return 0.5 * (1.0 + t) + 0.5 * x * (1.0 - t * t) * _GK * (1.0 + 3 * 0.044715 * x * x)


def _rs(x):
    return lax.rsqrt(jnp.mean(x * x, axis=-1, keepdims=True) + EPS)


def _sum0(x):
    return jnp.sum(x, axis=0, keepdims=True)


def st_norm_mod(x, g, sc, sh):
    y = x * _rs(x) * g
    return (y * (1.0 + sc) + sh,), ()


def st_norm_mod_bwd(x, dh, dres, g, sc):
    r = _rs(x)
    xn = x * r
    y = xn * g
    dy = dh * (1.0 + sc)
    dxn = dy * g
    dx = r * (dxn - xn * jnp.mean(dxn * xn, axis=-1, keepdims=True))
    return (dres + dx,), (_sum0(dh), _sum0(dh * y), _sum0(dy * xn))


def st_rms(x, g):
    return (x * _rs(x) * g,), ()


def st_rms_bwd(x, dy, g):
    r = _rs(x)
    n = x * r
    dn = dy * g
    dx = r * (dn - n * jnp.mean(dn * n, axis=-1, keepdims=True))
    return (dx,), (_sum0(dy * n),)


def st_rms2(x1, x2, g1, g2):
    return st_rms(x1, g1)[0] + st_rms(x2, g2)[0], ()


def st_rms2_bwd(x1, dy1, x2, dy2, g1, g2):
    (d1,), (s1,) = st_rms_bwd(x1, dy1, g1)
    (d2,), (s2,) = st_rms_bwd(x2, dy2, g2)
    return (d1, d2), (s1, s2)


def st_gate(o, z):
    return (o * _silu(z),), ()


def st_gate_bwd(dog, o, z):
    return (dog * _silu(z), dog * o * _dsilu(z)), ()


def st_resid(x, out, gt):
    return (x + gt * out,), ()


def st_resid_bwd(dx, out, gt):
    return (dx * gt,), (_sum0(dx * out),)


def st_s5a(yssm, u, d):
    y = yssm + d * u
    return (y, _gelu(y)), ()


def st_s5b(y, gl, z, b):
    return (_gelu(y) * _sig(gl + b) * _silu(z),), ()


def st_s5b_bwd(dy3, y, gl, z, b):
    y1 = _gelu(y)
    s = _sig(gl + b)
    dy2 = dy3 * _silu(z)
    dz = dy3 * y1 * s * _dsilu(z)
    dgl = dy2 * y1 * s * (1.0 - s)
    return (dgl, dz, dy2 * s), (_sum0(dgl),)


def st_s5a_bwd(dy1a, dy1b, y, u, d):
    dy = (dy1a + dy1b) * _dgelu(y)
    return (dy, dy * d), (_sum0(dy * u),)


def st_l0_pre(x, g, sc, sh, qg, kvg, w_in):
    hb = st_norm_mod(x, g, sc, sh)[0][0].astype(BF16)
    p = lax.dot_general(hb, w_in, _DN["nt"], preferred_element_type=F32)
    cq, ckv = p[:, HEADS * VD:HEADS * VD + QL], p[:, HEADS * VD + QL:HEADS * VD + QL + KVL]
    return (hb, p) + st_rms2(cq, ckv, qg, kvg)[0], ()


def st_l0_tail_bwd(dq, dkv, dkr, dz, cq, ckv, cqn, ckvn, h, x, dres, qg, kvg, g, sc, w_uq, w_ukv, w_in):
    dcqn = jnp.dot(dq, w_uq, preferred_element_type=F32)
    dckvn = jnp.dot(dkv, w_ukv, preferred_element_type=F32)
    (dcq, dckv), (dqg, dkvg) = st_rms2_bwd(cq, dcqn, ckv, dckvn, qg, kvg)
    dp = jnp.concatenate([dz, dcq, dckv, dkr], axis=1).astype(BF16)
    dh = jnp.dot(dp, w_in, preferred_element_type=F32)
    outs, sums = st_norm_mod_bwd(x, dh, dres, g, sc)
    tn = lambda a, b: lax.dot_general(a, b, _DN["tn"], preferred_element_type=F32)
    return outs, (dqg, dkvg) + sums, (tn(cqn, dq), tn(ckvn, dkv), tn(h, dp))


def st_l1_pre(x, g, sc, sh, w_in):
    hb = st_norm_mod(x, g, sc, sh)[0][0].astype(BF16)
    return (hb, lax.dot_general(hb, w_in, _DN["nt"], preferred_element_type=F32)), ()


def st_l1_tail_bwd(du_a, du_b, dz, h, x, dres, g, sc, w_in):
    dp = jnp.concatenate([(du_a + du_b).astype(BF16), dz], axis=1)
    dh = jnp.dot(dp, w_in, preferred_element_type=F32)
    outs, sums = st_norm_mod_bwd(x, dh, dres, g, sc)
    return outs, sums, (lax.dot_general(h, dp, _DN["tn"], preferred_element_type=F32),)


def st_l0_post(o, z, x, gt, w_out):
    og = (o * _silu(z)).astype(BF16)
    out = jnp.dot(og, w_out, preferred_element_type=F32)
    return (og, out, x + gt * out), ()


def st_l0_post_bwd(dx1, out, og, o, z, gt, w_out):
    (dout,), (dgt,) = st_resid_bwd(dx1, out.astype(F32), gt)
    doutb = dout.astype(BF16)
    dog = lax.dot_general(doutb, w_out, _DN["nt"], preferred_element_type=F32)
    return st_gate_bwd(dog, o, z)[0], (dgt,), (lax.dot_general(og, doutb, _DN["tn"], preferred_element_type=F32),)


def st_l1_mlp(yssm, u, z, x1, tgt, d, bglu, gt, fg, mask, w_glu, w_out):
    (y, y1), _ = st_s5a(yssm, u, d)
    y1b = y1.astype(BF16)
    gl = jnp.dot(y1b, w_glu, preferred_element_type=F32)
    y3 = (y1 * _sig(gl + bglu) * _silu(z)).astype(BF16)
    out = jnp.dot(y3, w_out, preferred_element_type=F32)
    (dx2,), sums = st_final(x1 + gt * out, tgt, fg, mask)
    return (y, y1b, gl, y3, out, dx2), sums


def st_l1_mlp_bwd(dx2, out, y3, y, gl, z, u, y1b, gt, bglu, d, w_out, w_glu):
    out, gl = out.astype(F32), gl.astype(F32)
    (dout,), (dgt,) = st_resid_bwd(dx2, out, gt)
    doutb = dout.astype(BF16)
    dy3 = lax.dot_general(doutb, w_out, _DN["nt"], preferred_element_type=F32)
    (dgl, dz, dy1a), (dbglu,) = st_s5b_bwd(dy3, y, gl, z, bglu)
    dglb = dgl.astype(BF16)
    dy1b = lax.dot_general(dglb, w_glu, _DN["nt"], preferred_element_type=F32)
    (dy, du), (dd,) = st_s5a_bwd(dy1a, dy1b, y, u, d)
    g_w_out = lax.dot_general(y3, doutb, _DN["tn"], preferred_element_type=F32)
    g_w_glu = lax.dot_general(y1b, dglb, _DN["tn"], preferred_element_type=F32)
    return (dz, dy, du), (dgt, dbglu, dd), (g_w_out, g_w_glu)


def st_final(x2, tgt, g, mask):
    r = _rs(x2)
    n = x2 * r
    e = n * g - tgt
    dyo = e * (1.0 / D)
    dn = dyo * g
    dx = r * (dn - n * jnp.mean(dn * n, axis=-1, keepdims=True))
    lsum = jnp.sum(_sum0(e * e), axis=1, keepdims=True) * (0.5 / D)
    return (dx * mask,), (_sum0(dyo * n), jnp.broadcast_to(lsum, (1, 128)))


def rowwise(fn, rows, vecs, out_rows, out_sums, name, mats=(), out_accs=()):
    lat_blk = lambda i: jnp.maximum(i - 1, 0)
    arrays, in_specs, pick = [], [], []
    for a in rows:
        if not isinstance(a, tuple):
            a = (a, 0, a.shape[1])
        tag = a[0] if isinstance(a[0], str) else None
        if tag == "cat":
            _, ctx, x = a
            arrays += [ctx, x]
            in_specs += [pl.BlockSpec((TB, ctx.shape[1]), lambda i: (0, 0)),
                         pl.BlockSpec((TB, x.shape[1]), lambda i: (lat_blk(i), 0))]
            pick.append(2)
        elif tag == "lat":
            arrays.append(a[1])
            in_specs.append(pl.BlockSpec((TB, a[1].shape[1]), lambda i: (lat_blk(i), 0)))
            pick.append(1)
        else:
            arr, cb, width = a
            arrays.append(arr)
            in_specs.append(pl.BlockSpec((TB, width), lambda i, cb=cb: (i, cb)))
            pick.append(1)
    T = LC + L
    nin, nv, nm, no, ns = len(arrays), len(vecs), len(mats), len(out_rows), len(out_sums)

    def body(*refs):
        i = pl.program_id(0)
        vals, k = [], 0
        for p in pick:
            if p == 2:
                vals.append(jnp.where(i == 0, refs[k][...], refs[k + 1][...]))
            else:
                vals.append(refs[k][...])
            k += p
        vals += [r[0] for r in refs[nin:nin + nv]] + [r[...] for r in refs[nin + nv:nin + nv + nm]]
        res = fn(*vals)
        first_out = nin + nv + nm
        for r, o in zip(refs[first_out:first_out + no], res[0]):
            r[...] = o.astype(r.dtype)
        sum_refs = refs[first_out + no:first_out + no + ns]
        if sum_refs:
            @pl.when(i <= 1)
            def _():
                for r in sum_refs:
                    r[...] = jnp.zeros_like(r)
            for r, s in zip(sum_refs, res[1]):
                r[0] += s
        acc_refs = refs[first_out + no + ns:]
        if acc_refs:
            @pl.when(i == 0)
            def _():
                for r in acc_refs:
                    r[...] = jnp.zeros_like(r)
            for r, a in zip(acc_refs, res[2]):
                r[...] += a

    kind = lambda i: (jnp.minimum(i, 1), 0, 0)
    in_specs += [pl.BlockSpec((1, 1, v.shape[2]), kind) for v in vecs]
    in_specs += [pl.BlockSpec(m.shape, lambda i: (0, 0), pipeline_mode=pl.Buffered(1)) for m in mats]
    out_specs, out_shape = [], []
    for o in out_rows:
        lat = len(o) == 3
        out_specs.append(pl.BlockSpec((TB, o[0]), (lambda i: (lat_blk(i), 0)) if lat else (lambda i: (i, 0))))
        out_shape.append(jax.ShapeDtypeStruct((L if lat else T, o[0]), o[1]))
    out_specs += [pl.BlockSpec((1, 1, c), kind) for c in out_sums]
    out_shape += [jax.ShapeDtypeStruct((2, 1, c), F32) for c in out_sums]
    out_specs += [pl.BlockSpec(s, lambda i: (0, 0)) for s in out_accs]
    out_shape += [jax.ShapeDtypeStruct(s, F32) for s in out_accs]
    res = pl.pallas_call(body, grid=(T // TB,), in_specs=in_specs, out_specs=out_specs, out_shape=out_shape,
                         compiler_params=_cp(("arbitrary",)), name=name)(*arrays, *vecs, *mats)
    if out_accs:
        return res[:no], res[no:no + ns], res[no + ns:]
    return res[:no], res[no:]


_DN = {"nn": (((1,), (0,)), ((), ())), "nt": (((1,), (1,)), ((), ())), "tn": (((0,), (0,)), ((), ()))}


def mm(a, b, mode, name, out_dtype=F32, tm=None, tn=None, shard_out=False):
    if mode == "nn":
        (M, K), (_, N) = a.shape, b.shape
    elif mode == "nt":
        (M, K), (N, _) = a.shape, b.shape
    else:
        (K, M), (_, N) = a.shape, b.shape
    if tm is None:
        tm = next((t for t in (768, 512, 256) if M % t == 0 and M > t), M)
    tn = N if tn is None else tn
    dn = _DN[mode]

    def body(a_ref, b_ref, o_ref):
        o_ref[...] = lax.dot_general(a_ref[...].astype(BF16), b_ref[...].astype(BF16), dn,
                                     preferred_element_type=F32).astype(o_ref.dtype)

    if shard_out:
        def body(a_ref, b_ref, o_ref):
            av = a_ref[...].astype(BF16)
            for j in range(N // tn):
                bj = b_ref[pl.ds(j * tn, tn), :] if mode == "nt" else b_ref[:, pl.ds(j * tn, tn)]
                o_ref[j] = lax.dot_general(av, bj.astype(BF16), dn, preferred_element_type=F32).astype(o_ref.dtype)

        a_spec = pl.BlockSpec((K, tm), lambda i: (0, i)) if mode == "tn" else pl.BlockSpec((tm, K), lambda i: (i, 0))
        return pl.pallas_call(body, grid=(M // tm,), in_specs=[a_spec, pl.BlockSpec(b.shape, lambda i: (0, 0))],
                              out_specs=pl.BlockSpec((N // tn, tm, tn), lambda i: (0, i, 0)),
                              out_shape=jax.ShapeDtypeStruct((N // tn, M, tn), out_dtype),
                              compiler_params=_cp(("parallel",)), name=name)(a, b)
    a_spec = pl.BlockSpec((K, tm), lambda i, j: (0, i)) if mode == "tn" else pl.BlockSpec((tm, K), lambda i, j: (i, 0))
    b_spec = pl.BlockSpec((tn, K), lambda i, j: (j, 0)) if mode == "nt" else pl.BlockSpec((K, tn), lambda i, j: (0, j))
    return pl.pallas_call(body, grid=(M // tm, N // tn), in_specs=[a_spec, b_spec],
                          out_specs=pl.BlockSpec((tm, tn), lambda i, j: (i, j)), out_shape=jax.ShapeDtypeStruct((M, N), out_dtype),
                          compiler_params=_cp(("parallel", "arbitrary")), name=name)(a, b)


def _rope_tables(T, width=QK, first=NOPE):
    nlat = T - LC
    pos = np.arange(nlat)
    row, col = pos // GRID_W, pos % GRID_W
    half = ROPE // 2
    inv = 1.0 / (THETA ** (np.arange(0, half, 2, dtype=np.float64) / half))
    cosf = np.ones((T, width), np.float64)
    sinf = np.zeros((T, width), np.float64)
    perm = np.zeros((width, width), np.float32)
    for m in range(ROPE):
        j = first + m
        blk, w = m // half, m % half
        ang = (row if blk == 0 else col)[:, None] * inv[None, :]
        f = w % (half // 2)
        cosf[LC:, j] = np.cos(ang[:, f])
        if w < half // 2:
            sinf[LC:, j] = -np.sin(ang[:, f])
            perm[j + half // 2, j] = 1.0
        else:
            sinf[LC:, j] = np.sin(ang[:, f])
            perm[j - half // 2, j] = 1.0
    return jnp.asarray(cosf, F32), jnp.asarray(sinf, F32), jnp.asarray(perm, BF16), jnp.asarray(perm.T, BF16)


def _exact_perm(x, pm):
    hi = x.astype(BF16)
    r1 = x - hi.astype(F32)
    mid = r1.astype(BF16)
    lo = (r1 - mid.astype(F32)).astype(BF16)
    dot = lambda a: jnp.dot(a, pm, preferred_element_type=F32)
    return dot(hi) + dot(mid) + dot(lo)


def _rot(x, cv, sv, pv, inverse):
    if inverse:
        return x * cv + _exact_perm(x * sv, pv)
    return x * cv + _exact_perm(x, pv) * sv


def rope(x, cosf, sinf, pm, inverse, out_dtype, name, scale=1.0):
    H, T, _ = x.shape

    def body(x_ref, c_ref, s_ref, p_ref, o_ref):
        cv, sv, pv = c_ref[...], s_ref[...], p_ref[...]
        for h in range(H):
            o_ref[h] = (_rot(x_ref[h], cv, sv, pv, inverse) * scale).astype(o_ref.dtype)

    return pl.pallas_call(
        body, grid=(T // TB,),
        in_specs=[pl.BlockSpec((H, TB, QK), lambda i: (0, i, 0)), pl.BlockSpec((TB, QK), lambda i: (i, 0)),
                  pl.BlockSpec((TB, QK), lambda i: (i, 0)), pl.BlockSpec((QK, QK), lambda i: (0, 0))],
        out_specs=pl.BlockSpec((H, TB, QK), lambda i: (0, i, 0)), out_shape=jax.ShapeDtypeStruct((H, T, QK), out_dtype),
        compiler_params=_cp(("parallel",)), name=name)(x, cosf, sinf, pm)


KVW = NOPE + VD


def _kv_selectors():
    s_kn = np.zeros((KVW, QK), np.float32)
    s_kr = np.zeros((128, QK), np.float32)
    s_v = np.zeros((KVW, VD), np.float32)
    for l in range(NOPE):
        s_kn[l, l] = 1.0
    for l in range(ROPE):
        s_kr[l, NOPE + l] = 1.0
    for l in range(VD):
        s_v[NOPE + l, l] = 1.0
    return s_kn, s_kr, s_v


def project_q(cqn, w, cosf, sinf, pm, name):
    T = cqn.shape[0]

    def body(a_ref, w_ref, c_ref, s_ref, p_ref, o_ref):
        a, cv, sv, pv = a_ref[...], c_ref[...], s_ref[...], p_ref[...]
        for h in range(HEADS):
            qh = _dotf(a, w_ref[pl.ds(h * QK, QK), :], "nt")
            o_ref[h] = (_rot(qh, cv, sv, pv, False) * (SCALE * LOG2E)).astype(BF16)

    rows = lambda c: pl.BlockSpec((TB, c), lambda i: (i, 0))
    const = lambda x: pl.BlockSpec(x.shape, lambda i: (0, 0))
    return pl.pallas_call(
        body, grid=(T // TB,), in_specs=[rows(QL), const(w), rows(QK), rows(QK), const(pm)],
        out_specs=pl.BlockSpec((HEADS, TB, QK), lambda i: (0, i, 0)), out_shape=jax.ShapeDtypeStruct((HEADS, T, QK), BF16),
        compiler_params=_cp(("parallel",)), name=name)(cqn, w, cosf, sinf, pm)


def project_kv(ckvn, w, p0, kr_block, name):
    T = ckvn.shape[0]
    cosf, sinf, pm, _ = _rope_tables(T, 128, 0)
    s_kn, s_kr, s_v = (jnp.asarray(s, BF16) for s in _kv_selectors())

    def body(a_ref, w_ref, kr_ref, c_ref, s_ref, p_ref, skn_ref, skr_ref, sv_ref, k_ref, v_ref):
        a = a_ref[...]
        krr = _rot(kr_ref[...], c_ref[...], s_ref[...], p_ref[...], False).astype(BF16)
        kr_part = jnp.dot(krr, skr_ref[...], preferred_element_type=F32)
        for h in range(HEADS):
            kvb = _dotf(a, w_ref[pl.ds(h * KVW, KVW), :], "nt").astype(BF16)
            k_ref[h] = (jnp.dot(kvb, skn_ref[...], preferred_element_type=F32) + kr_part).astype(BF16)
            v_ref[h] = jnp.dot(kvb, sv_ref[...], preferred_element_type=F32).astype(BF16)

    rows = lambda c: pl.BlockSpec((TB, c), lambda i: (i, 0))
    const = lambda x: pl.BlockSpec(x.shape, lambda i: (0, 0))
    return pl.pallas_call(
        body, grid=(T // TB,),
        in_specs=[rows(KVL), const(w), pl.BlockSpec((TB, 128), lambda i: (i, kr_block)),
                  rows(128), rows(128), const(pm), const(s_kn), const(s_kr), const(s_v)],
        out_specs=[pl.BlockSpec((HEADS, TB, QK), lambda i: (0, i, 0)), pl.BlockSpec((HEADS, TB, VD), lambda i: (0, i, 0))],
        out_shape=[jax.ShapeDtypeStruct((HEADS, T, QK), BF16), jax.ShapeDtypeStruct((HEADS, T, VD), BF16)],
        compiler_params=_cp(("parallel",)), name=name)(ckvn, w, p0, cosf, sinf, pm, s_kn, s_kr, s_v)


def split_kv_grads(dk, dv, name):
    H, T, _ = dk.shape
    cosf, sinf, _, pmt = _rope_tables(T, 128, 0)
    s_kn, s_kr, s_v = _kv_selectors()
    s_knt, s_krt, s_vt = (jnp.asarray(s.T, BF16) for s in (s_kn, s_kr, s_v))

    def body(dk_ref, dv_ref, c_ref, s_ref, p_ref, skn_ref, skr_ref, sv_ref, dkv_ref, dkr_ref):
        total = None
        for h in range(H):
            dkh = dk_ref[h] * (1.0 / LOG2E)
            total = dkh if total is None else total + dkh
            dkv_ref[:, pl.ds(h * KVW, KVW)] = (
                jnp.dot(dkh.astype(BF16), skn_ref[...], preferred_element_type=F32)
                + jnp.dot(dv_ref[h].astype(BF16), sv_ref[...], preferred_element_type=F32)).astype(BF16)
        dkr_ref[...] = _rot(_exact_perm(total, skr_ref[...]), c_ref[...], s_ref[...], p_ref[...], True)

    rows = lambda c: pl.BlockSpec((TB, c), lambda i: (i, 0))
    const = lambda a: pl.BlockSpec(a.shape, lambda i: (0, 0))
    return pl.pallas_call(
        body, grid=(T // TB,),
        in_specs=[pl.BlockSpec((H, TB, QK), lambda i: (0, i, 0)), pl.BlockSpec((H, TB, VD), lambda i: (0, i, 0)),
                  rows(128), rows(128), const(pmt), const(s_knt), const(s_krt), const(s_vt)],
        out_specs=[rows(H * KVW), rows(128)],
        out_shape=[jax.ShapeDtypeStruct((T, H * KVW), BF16), jax.ShapeDtypeStruct((T, 128), F32)],
        compiler_params=_cp(("parallel",)), name=name)(dk, dv, cosf, sinf, pmt, s_knt, s_krt, s_vt)


def _by_query_block(run, T):
    @pl.when(pl.program_id(1) == 0)
    def _():
        run(LC)

    @pl.when(pl.program_id(1) > 0)
    def _():
        run(T)


def _with_rider(body, nin, nout, ride, grid):
    if ride is None:
        return body
    n = ride.n

    def wrapped(*refs):
        ins, xs = refs[:nin], refs[nin:nin + n]
        outs, got = refs[nin + n:nin + n + nout], refs[nin + n + nout:nin + 2 * n + nout]
        sems = refs[nin + 2 * n + nout:]
        step = pl.program_id(0) * grid[1] + pl.program_id(1)

        @pl.when(step == 0)
        def _():
            ride.start(xs, got, sems)

        body(*ins, *outs)

        @pl.when(step == grid[0] * grid[1] - 1)
        def _():
            ride.finish(xs, got, sems)

    return wrapped


def _ride_call(body, grid, in_specs, out_specs, out_shape, ride, rode, name, args):
    if ride is None:
        return pl.pallas_call(body, grid=grid, in_specs=in_specs, out_specs=out_specs, out_shape=out_shape,
                              compiler_params=_cp(("parallel", "arbitrary")), name=name)(*args), []
    res = pl.pallas_call(
        _with_rider(body, len(in_specs), len(out_specs), ride, grid), grid=grid,
        in_specs=in_specs + ride.specs, out_specs=out_specs + ride.specs, out_shape=out_shape + ride.out_shape,
        scratch_shapes=ride.scratch,
        compiler_params=pltpu.CompilerParams(dimension_semantics=("arbitrary", "arbitrary"), vmem_limit_bytes=VMEM_LIMIT,
                                             has_side_effects=True), name=name)(*args, *rode)
    return res[:len(out_specs)], res[len(out_specs):]


def attn_fwd(q, k, v, name, rode=None, modes=None):
    H, T, _ = q.shape

    def body(q_ref, k_ref, v_ref, o_ref, lse_ref):
        def run(nk):
            s = _dotf(q_ref[0], k_ref[0, pl.ds(0, nk), :], "nt")
            m = jnp.max(s, axis=1, keepdims=True)
            p = jnp.exp2(s - m)
            l = jnp.sum(p, axis=1, keepdims=True)
            o = jnp.dot(p.astype(BF16), v_ref[0, pl.ds(0, nk), :], preferred_element_type=F32)
            o_ref[0] = o / l
            lse_ref[0] = m + jnp.log2(l)

        _by_query_block(run, T)

    return _ride_call(
        body, (H, T // TB),
        [pl.BlockSpec((1, TB, QK), lambda h, i: (h, i, 0)), pl.BlockSpec((1, T, QK), lambda h, i: (h, 0, 0)),
         pl.BlockSpec((1, T, VD), lambda h, i: (h, 0, 0))],
        [pl.BlockSpec((1, TB, VD), lambda h, i: (h, i, 0)), pl.BlockSpec((1, TB, 1), lambda h, i: (h, i, 0))],
        [jax.ShapeDtypeStruct((H, T, VD), F32), jax.ShapeDtypeStruct((H, T, 1), F32)],
        Exchange(rode, modes) if rode else None, rode, name, (q, k, v))


def attn_bwd(q, k, v, o, lse, do, name, rode=None, modes=None):
    H, T, _ = q.shape

    def body(q_ref, k_ref, v_ref, o_ref, lse_ref, do_ref, dq_ref, dk_ref, dv_ref):
        i = pl.program_id(1)

        @pl.when(i == 0)
        def _():
            dk_ref[...] = jnp.zeros_like(dk_ref)
            dv_ref[...] = jnp.zeros_like(dv_ref)

        def run(nk):
            keys = pl.ds(0, nk)
            qv, kv, dov = q_ref[0], k_ref[0, keys, :], do_ref[0]
            p = jnp.exp2(_dotf(qv, kv, "nt") - lse_ref[0])
            delta = jnp.sum(dov * o_ref[0], axis=1, keepdims=True)
            dob = dov.astype(BF16)
            dv_ref[0, keys, :] += _dotf(p.astype(BF16), dob, "tn")
            dp = _dotf(dob, v_ref[0, keys, :], "nt")
            ds = (p * (dp - delta)).astype(BF16)
            dq_ref[0] = jnp.dot(ds, kv, preferred_element_type=F32)
            dk_ref[0, keys, :] += _dotf(ds, qv, "tn")

        _by_query_block(run, T)

    blk = lambda c: pl.BlockSpec((1, TB, c), lambda h, i: (h, i, 0))
    full = lambda c: pl.BlockSpec((1, T, c), lambda h, i: (h, 0, 0))
    return _ride_call(
        body, (H, T // TB), [blk(QK), full(QK), full(VD), blk(VD), blk(1), blk(VD)], [blk(QK), full(QK), full(VD)],
        [jax.ShapeDtypeStruct((H, T, QK), F32), jax.ShapeDtypeStruct((H, T, QK), F32), jax.ShapeDtypeStruct((H, T, VD), F32)],
        Exchange(rode, modes) if rode else None, rode, name, (q, k, v, o, lse, do))


def disc_fwd(a_re, a_im, ls, name):
    def body(ar_ref, ai_ref, ls_ref, lr_ref, li_ref, fr_ref, fi_ref):
        ar, ai = ar_ref[...], ai_ref[...]
        dt = jnp.exp(ls_ref[...])
        mag = jnp.exp(ar * dt)
        lr = mag * jnp.cos(ai * dt)
        li = mag * jnp.sin(ai * dt)
        den = ar * ar + ai * ai
        nr = lr - 1.0
        lr_ref[...] = lr
        li_ref[...] = li
        fr_ref[...] = (nr * ar + li * ai) / den
        fi_ref[...] = (li * ar - nr * ai) / den

    return pl.pallas_call(body, out_shape=[jax.ShapeDtypeStruct(a_re.shape, F32)] * 4, name=name)(a_re, a_im, ls)


def disc_b(f_re, f_im, b_re, b_im, name):
    def body(fr_ref, fi_ref, br_ref, bi_ref, or_ref, oi_ref):
        fr, fi, br, bi = fr_ref[...], fi_ref[...], br_ref[...], bi_ref[...]
        or_ref[...] = fr * br - fi * bi
        oi_ref[...] = fr * bi + fi * br

    fs, bs = _disc_b_specs()
    return pl.pallas_call(body, grid=(2, G * P // DISC_ROWS), in_specs=[fs, fs, bs, bs], out_specs=[bs, bs],
                          out_shape=[jax.ShapeDtypeStruct(b_re.shape, F32)] * 2, name=name)(f_re, f_im, b_re, b_im)


DISC_ROWS = G * P


def _disc_b_specs():
    return (pl.BlockSpec((1, DISC_ROWS, 1), lambda d, i: (d, i, 0)), pl.BlockSpec((1, DISC_ROWS, CH), lambda d, i: (d, i, 0)))


def disc_b_bwd(f_re, f_im, b_re, b_im, dbb_re, dbb_im, name):
    def body(fr_ref, fi_ref, br_ref, bi_ref, dr_ref, di_ref, dbr_ref, dbi_ref, dfr_ref, dfi_ref):
        fr, fi, br, bi, dr, di = fr_ref[...], fi_ref[...], br_ref[...], bi_ref[...], dr_ref[...], di_ref[...]
        dbr_ref[...] = fr * dr + fi * di
        dbi_ref[...] = fr * di - fi * dr
        dfr_ref[...] = jnp.sum(dr * br + di * bi, axis=-1, keepdims=True)
        dfi_ref[...] = jnp.sum(di * br - dr * bi, axis=-1, keepdims=True)

    fs, bs = _disc_b_specs()
    return pl.pallas_call(body, grid=(2, G * P // DISC_ROWS), in_specs=[fs, fs, bs, bs, bs, bs], out_specs=[bs, bs, fs, fs],
                          out_shape=[jax.ShapeDtypeStruct(b_re.shape, F32)] * 2 + [jax.ShapeDtypeStruct(f_re.shape, F32)] * 2,
                          name=name)(f_re, f_im, b_re, b_im, dbb_re, dbb_im)


def disc_a_bwd(a_re, a_im, ls, dlr, dli, dfr, dfi, name):
    def body(ar_ref, ai_ref, ls_ref, dlr_ref, dli_ref, dfr_ref, dfi_ref, dar_ref, dai_ref, dls_ref):
        ar, ai = ar_ref[...], ai_ref[...]
        dt = jnp.exp(ls_ref[...])
        mag = jnp.exp(ar * dt)
        cs, sn = jnp.cos(ai * dt), jnp.sin(ai * dt)
        lr, li = mag * cs, mag * sn
        den = ar * ar + ai * ai
        nr = lr - 1.0
        f_re = (nr * ar + li * ai) / den
        f_im = (li * ar - nr * ai) / den
        dn1 = dfr_ref[...] / den
        dn2 = dfi_ref[...] / den
        dden = -(dfr_ref[...] * f_re + dfi_ref[...] * f_im) / den
        dlr_t = dlr_ref[...] + dn1 * ar - dn2 * ai
        dli_t = dli_ref[...] + dn1 * ai + dn2 * ar
        dar = dn1 * nr + dn2 * li + dden * 2.0 * ar
        dai = dn1 * li - dn2 * nr + dden * 2.0 * ai
        dmag = dlr_t * cs + dli_t * sn
        dth = dli_t * lr - dlr_t * li
        dar_ref[...] = dar + dmag * mag * dt
        dai_ref[...] = dai + dth * dt
        dls_ref[...] = jnp.sum(dmag * mag * ar + dth * ai, axis=-1, keepdims=True) * dt

    return pl.pallas_call(body, out_shape=[jax.ShapeDtypeStruct(a_re.shape, F32)] * 2 +
                          [jax.ShapeDtypeStruct(ls.shape, F32)], name=name)(a_re, a_im, ls, dlr, dli, dfr, dfi)


def _cpow(lr, li, n):
    rr, ri = None, None
    br, bi = lr, li
    while n:
        if n & 1:
            if rr is None:
                rr, ri = br, bi
            else:
                rr, ri = rr * br - ri * bi, rr * bi + ri * br
        n >>= 1
        if n:
            br, bi = br * br - bi * bi, 2.0 * br * bi
    return rr, ri


UNROLL = 4


def _steps(trips, fn, init):
    main = trips // UNROLL

    def body(i, c):
        for j in range(UNROLL):
            c = fn(i * UNROLL + j, c)
        return c

    c = lax.fori_loop(0, main, body, init) if main else init
    for n in range(main * UNROLL, trips):
        c = fn(n, c)
    return c


def _seg_scan(xre, xim, lam8, pw, base, seglen, rev, init, fin_re, fin_im, ini_re, ini_im, prev=None):
    lr, li = lam8

    def rows(t):
        first = base + t * SEG
        return pl.ds(first if isinstance(first, int) else pl.multiple_of(first, SEG), SEG)

    tmap = (lambda n: seglen - 1 - n) if rev else (lambda n: n)
    zero = jnp.zeros((SEG, SB), F32)

    def advance(c, t):
        a, b = c
        return lr * a - li * b + xre[rows(t), :], lr * b + li * a + xim[rows(t), :]

    fin = _steps(seglen, lambda n, c: advance(c, tmap(n)), (zero, zero))
    fin_re[...] = fin[0]
    fin_im[...] = fin[1]
    (cr, ci), (pr, pi) = init, pw
    for i in (range(SEG - 1, -1, -1) if rev else range(SEG)):
        ini_re[pl.ds(i, 1), :] = cr
        ini_im[pl.ds(i, 1), :] = ci
        cr, ci = pr * cr - pi * ci + fin_re[pl.ds(i, 1), :], pr * ci + pi * cr + fin_im[pl.ds(i, 1), :]
    start = (ini_re[...], ini_im[...])

    def store(c, t):
        na, nb = advance(c, t)
        xre[rows(t), :] = na
        xim[rows(t), :] = nb
        return na, nb

    if prev is None:
        _steps(seglen, lambda n, c: store(c, tmap(n)), start)
        return (cr, ci), None

    sre, sim, s_ini_re, s_ini_im = prev

    def acc_step(c, t, pre, pim):
        na, nb = store(c[:2], t)
        return na, nb, c[2] + na * pre + nb * pim, c[3] + nb * pre - na * pim

    def body(n, c):
        t = tmap(n)
        tp = t - 1 if rev else t + 1
        return acc_step(c, t, sre[rows(tp), :], sim[rows(tp), :])

    c = _steps(seglen - 1, body, start + (zero, zero))
    c = acc_step(c, 0 if rev else seglen - 1, s_ini_re[...], s_ini_im[...])
    return (cr, ci), c[2:]


def _lam_tiles(lr, li, lens, conj=False):
    if conj:
        li = -li
    lam8 = (jnp.broadcast_to(lr, (SEG, SB)), jnp.broadcast_to(li, (SEG, SB)))
    return lam8, [_cpow(lr, li, n) for n in lens]


def _stretches(T):
    return ((0, LC // SEG), (LC, (T - LC) // SEG))


def _to_seg_order(src, dst, T):
    for base, seglen in _stretches(T):
        def body(t, carry, base=base, seglen=seglen):
            dst[pl.ds(pl.multiple_of(base + t * SEG, SEG), SEG), :] = src[pl.ds(base + t, SEG, stride=seglen), :]
            return carry
        lax.fori_loop(0, seglen, body, 0, unroll=8)


def _from_seg_order(src, dst, T):
    for base, seglen in _stretches(T):
        def body(t, carry, base=base, seglen=seglen):
            dst[pl.ds(base + t, SEG, stride=seglen), :] = src[pl.ds(pl.multiple_of(base + t * SEG, SEG), SEG), :]
            return carry
        lax.fori_loop(0, seglen, body, 0, unroll=8)


def _scan_specs(T):
    ublk = pl.BlockSpec((T, UB), lambda j: (0, j))
    lam = pl.BlockSpec((2, 1, 1, SB), lambda j: (0, j, 0, 0))
    mat = pl.BlockSpec((2, 1, UB, P), lambda j: (0, j, 0, 0))
    return ublk, lam, mat


def _dotf(a, b, mode="nn"):
    return lax.dot_general(a, b, _DN[mode], preferred_element_type=F32)


def _diag_mask():
    r = lax.broadcasted_iota(jnp.int32, (UB, SB), 0)
    c = lax.broadcasted_iota(jnp.int32, (UB, SB), 1)
    return lax.shift_right_logical(r, int(math.log2(CH))) == lax.shift_right_logical(c, int(math.log2(P)))


def _expand(m):
    p = lax.broadcasted_iota(jnp.int32, (P, SB), 0)
    c = lax.broadcasted_iota(jnp.int32, (P, SB), 1)
    tile = jnp.where(lax.bitwise_and(c, P - 1) == p, 1.0, 0.0).astype(BF16)
    wide = jnp.dot(m.astype(BF16), tile, preferred_element_type=F32)
    return jnp.where(_diag_mask(), wide, 0.0).astype(BF16)


def _collapse(full):
    c = lax.broadcasted_iota(jnp.int32, (SB, P), 0)
    p = lax.broadcasted_iota(jnp.int32, (SB, P), 1)
    pick = jnp.where(lax.bitwise_and(c, P - 1) == p, 1.0, 0.0).astype(BF16)
    return _exact_perm(jnp.where(_diag_mask(), full, 0.0), pick)


def _zero_state():
    return jnp.zeros((1, SB), F32), jnp.zeros((1, SB), F32)


def scan_fwd(u, lam_re, lam_im, bre, bim, cre, cim, name):
    T = u.shape[0]
    s_ctx, s_lat = LC // SEG, (T - LC) // SEG

    def body(u_ref, lr_ref, li_ref, bre_ref, bim_ref, cre_ref, cim_ref, y_ref, us, ys, sre, sim, fre, fim, ire, iim):
        _to_seg_order(u_ref, us, T)
        ub = us[...].astype(BF16)
        for d in range(2):
            lam8, (pw_c, pw_l) = _lam_tiles(lr_ref[d, 0], li_ref[d, 0], (s_ctx, s_lat))
            sre[...] = _dotf(ub, _expand(bre_ref[d, 0]))
            sim[...] = _dotf(ub, _expand(bim_ref[d, 0]))
            end_c, _ = _seg_scan(sre, sim, lam8, pw_c, 0, s_ctx, bool(d), _zero_state(), fre, fim, ire, iim)
            _seg_scan(sre, sim, lam8, pw_l, LC, s_lat, bool(d), end_c, fre, fim, ire, iim)
            y = (_dotf(sre[...].astype(BF16), _expand(cre_ref[d, 0]), "nt")
                 - _dotf(sim[...].astype(BF16), _expand(cim_ref[d, 0]), "nt"))
            if d == 0:
                ys[...] = y
            else:
                ys[...] += y
        _from_seg_order(ys, y_ref, T)

    ublk, lam, mat = _scan_specs(T)
    return pl.pallas_call(
        body, grid=(NJ,), in_specs=[ublk, lam, lam, mat, mat, mat, mat], out_specs=ublk,
        out_shape=jax.ShapeDtypeStruct((T, G * CH), F32),
        scratch_shapes=[pltpu.VMEM((T, UB), F32)] * 2 + [pltpu.VMEM((T, SB), F32)] * 2 + [pltpu.VMEM((SEG, SB), F32)] * 4,
        compiler_params=_cp(("arbitrary",)), name=name)(u, lam_re, lam_im, bre, bim, cre, cim)


def scan_bwd(u, dy, lam_re, lam_im, bre, bim, cre, cim, name):
    T = u.shape[0]
    s_ctx, s_lat = LC // SEG, (T - LC) // SEG

    def body(u_ref, dy_ref, lr_ref, li_ref, bre_ref, bim_ref, cre_ref, cim_ref,
             du_ref, dlr_ref, dli_ref, dbre_ref, dbim_ref, dcre_ref, dcim_ref,
             us, dys, dus, sre, sim, gre, gim, fre, fim, ic_re, ic_im, il_re, il_im, jre, jim):
        _to_seg_order(u_ref, us, T)
        _to_seg_order(dy_ref, dys, T)
        ub, dyb = us[...].astype(BF16), dys[...].astype(BF16)
        for d in range(2):
            rev = bool(d)
            lam8, (pw_c, pw_l) = _lam_tiles(lr_ref[d, 0], li_ref[d, 0], (s_ctx, s_lat))
            cam8, (cw_c, cw_l) = _lam_tiles(lr_ref[d, 0], li_ref[d, 0], (s_ctx, s_lat), conj=True)
            bre_v, bim_v = _expand(bre_ref[d, 0]), _expand(bim_ref[d, 0])
            sre[...] = _dotf(ub, bre_v)
            sim[...] = _dotf(ub, bim_v)
            end_c, _ = _seg_scan(sre, sim, lam8, pw_c, 0, s_ctx, rev, _zero_state(), fre, fim, ic_re, ic_im)
            _seg_scan(sre, sim, lam8, pw_l, LC, s_lat, rev, end_c, fre, fim, il_re, il_im)
            gre[...] = _dotf(dyb, _expand(cre_ref[d, 0]))
            gim[...] = -_dotf(dyb, _expand(cim_ref[d, 0]))
            end_g, acc_l = _seg_scan(gre, gim, cam8, cw_l, LC, s_lat, not rev, _zero_state(), fre, fim, jre, jim,
                                     prev=(sre, sim, il_re, il_im))
            _, acc_c = _seg_scan(gre, gim, cam8, cw_c, 0, s_ctx, not rev, end_g, fre, fim, jre, jim,
                                 prev=(sre, sim, ic_re, ic_im))
            dlr_ref[d, 0] = _sum0(acc_l[0] + acc_c[0])
            dli_ref[d, 0] = _sum0(acc_l[1] + acc_c[1])
            grb, gib = gre[...].astype(BF16), gim[...].astype(BF16)
            du = _dotf(grb, bre_v, "nt") + _dotf(gib, bim_v, "nt")
            if d == 0:
                dus[...] = du
            else:
                dus[...] += du
            dbre_ref[d, 0] = _collapse(_dotf(ub, grb, "tn"))
            dbim_ref[d, 0] = _collapse(_dotf(ub, gib, "tn"))
            dcre_ref[d, 0] = _collapse(_dotf(dyb, sre[...].astype(BF16), "tn"))
            dcim_ref[d, 0] = -_collapse(_dotf(dyb, sim[...].astype(BF16), "tn"))
        _from_seg_order(dus, du_ref, T)

    ublk, lam, mat = _scan_specs(T)
    lam_s = jax.ShapeDtypeStruct(lam_re.shape, F32)
    mat_s = jax.ShapeDtypeStruct(bre.shape, F32)
    return pl.pallas_call(
        body, grid=(NJ,), in_specs=[ublk, ublk, lam, lam, mat, mat, mat, mat],
        out_specs=[ublk, lam, lam, mat, mat, mat, mat],
        out_shape=[jax.ShapeDtypeStruct((T, G * CH), F32), lam_s, lam_s, mat_s, mat_s, mat_s, mat_s],
        scratch_shapes=[pltpu.VMEM((T, UB), F32)] * 3 + [pltpu.VMEM((T, SB), F32)] * 4 + [pltpu.VMEM((SEG, SB), F32)] * 8,
        compiler_params=_cp(("arbitrary",)), name=name)(u, dy, lam_re, lam_im, bre, bim, cre, cim)


class Exchange:
    def __init__(self, xs, modes):
        self.n = len(xs)
        self.modes = [modes] * self.n if isinstance(modes, (str, int)) else list(modes)
        self.out_shape = [jax.ShapeDtypeStruct(self._shape(x, md), x.dtype) for x, md in zip(xs, self.modes)]
        self.scratch = [pltpu.SemaphoreType.DMA((NDEV - 1, self.n)), pltpu.SemaphoreType.DMA((NDEV - 1, self.n)),
                        pltpu.SemaphoreType.DMA((self.n,))]
        self.specs = [pl.BlockSpec(memory_space=pl.ANY)] * self.n

    @staticmethod
    def _shape(x, mode):
        if mode == "gather":
            return (NDEV,) + tuple(x.shape)
        return tuple(x.shape) if mode == "lead" else (NDEV, x.shape[0], mode) + tuple(x.shape[2:])

    @staticmethod
    def _piece(x_ref, mode, dev):
        if mode == "gather":
            return x_ref
        return x_ref.at[dev] if mode == "lead" else x_ref.at[:, pl.ds(dev * mode, mode)]

    def _copies(self, x_refs, out_refs, sems):
        send_sems, recv_sems, local_sems = sems
        mx, my, mc = lax.axis_index("x"), lax.axis_index("y"), lax.axis_index("c")
        me = 4 * mx + 2 * my + mc
        peer_of = lambda k: (1 - mx if k & 4 else mx, 1 - my if k & 2 else my, 1 - mc if k & 1 else mc)
        local, first, relay, arrivals = [], [], [], []
        for a, (x_ref, out_ref) in enumerate(zip(x_refs, out_refs)):
            mode = self.modes[a]
            local.append(pltpu.make_async_copy(self._piece(x_ref, mode, me), out_ref.at[me], local_sems.at[a]))

            def remote(src, dst, k, pair, a=a):
                return pltpu.make_async_remote_copy(src_ref=src, dst_ref=dst, send_sem=send_sems.at[pair, a],
                                                    recv_sem=recv_sems.at[pair, a], device_id=peer_of(k), device_id_type=MESH_T)

            for k in range(1, NDEV):
                peer = peer_of(k)
                pid = 4 * peer[0] + 2 * peer[1] + peer[2]
                if mode != "gather":
                    src = self._piece(x_ref, mode, pid)
                    first.append(remote(src, out_ref.at[me], k, k - 1))
                    arrivals.append(remote(src, out_ref.at[pid], k, k - 1))
                elif k == 1:
                    first.append(remote(x_ref, out_ref.at[me], k, k - 1))
                    arrivals.append(remote(x_ref, out_ref.at[pid], k, k - 1))
                elif k % 2 == 0:
                    first.append(remote(x_ref, out_ref.at[me], k, k - 1))
                    relay.append((remote(x_ref, out_ref.at[pid], k, k - 1), remote(out_ref.at[pid], out_ref.at[pid], 1, k)))
                else:
                    arrivals.append(remote(x_ref, out_ref.at[pid], 1, k - 1))
        return local, first, relay, arrivals

    def start(self, x_refs, out_refs, sems):
        local, first, _, _ = self._copies(x_refs, out_refs, sems)
        for cp in local + first:
            cp.start()

    def finish(self, x_refs, out_refs, sems):
        local, first, relay, arrivals = self._copies(x_refs, out_refs, sems)
        for arrival, onward in relay:
            arrival.wait_recv()
            onward.start()
        for cp in arrivals:
            cp.wait_recv()
        for cp in first + [onward for _, onward in relay]:
            cp.wait_send()
        for cp in local:
            cp.wait()


def exchange(xs, modes, name):
    ex = Exchange(xs, modes)
    n = ex.n

    def body(*refs):
        ex.start(refs[:n], refs[n:2 * n], refs[2 * n:])
        ex.finish(refs[:n], refs[n:2 * n], refs[2 * n:])

    return pl.pallas_call(body, in_specs=ex.specs, out_specs=ex.specs, out_shape=ex.out_shape, scratch_shapes=ex.scratch,
                          compiler_params=pltpu.CompilerParams(has_side_effects=True), name=name)(*xs)


def _dot_f32(a, b, dn):
    return lax.dot_general(a, b, dn, preferred_element_type=F32, precision=lax.Precision.HIGHEST)


def ada_fwd(cg, c_ctx, ada_w, ada_b_loc, name):
    W = ada_w.shape[2]

    def body(cg_ref, cc_ref, w_ref, b_ref, o_ref):
        a = jnp.concatenate([_silu(cg_ref[...]), jnp.broadcast_to(_silu(cc_ref[...]), (NDEV, D))], axis=0)
        for i in range(2):
            o_ref[i] = _dot_f32(a, w_ref[i], _DN["nn"]) + b_ref[i]

    return pl.pallas_call(body, out_shape=jax.ShapeDtypeStruct((2, 2 * NDEV, W), F32),
                          compiler_params=_cp(), name=name)(cg, c_ctx, ada_w, ada_b_loc)


def ada_bwd(cg, c_ctx, ada_w, dm_loc, dm_all, name):
    W = ada_w.shape[2]

    def body(cg_ref, cc_ref, w_ref, dl_ref, da_ref, gw_ref, dcc_ref, gb_ref):
        a = jnp.concatenate([_silu(cg_ref[...]), jnp.broadcast_to(_silu(cc_ref[...]), (NDEV, D))], axis=0)
        dcc = jnp.zeros((1, D), F32)
        for i in range(2):
            dl = dl_ref[i]
            gw_ref[i] = _dot_f32(a, dl, _DN["tn"])
            dctx = jnp.sum(dl[NDEV:], axis=0, keepdims=True)
            dcc = dcc + _dot_f32(dctx, w_ref[i], _DN["nt"])
        dcc_ref[...] = dcc
        gb_ref[...] = jnp.sum(da_ref[...], axis=0)

    return pl.pallas_call(body, out_shape=[jax.ShapeDtypeStruct((2, D, W), F32), jax.ShapeDtypeStruct((1, D), F32),
                                           jax.ShapeDtypeStruct((2, 3 * D), F32)],
                          compiler_params=_cp(), name=name)(cg, c_ctx, ada_w, dm_loc, dm_all)


def cctx_finish(parts, c_ctx, name):
    def body(p_ref, cc_ref, o_ref):
        o_ref[...] = jnp.sum(p_ref[...], axis=0, keepdims=True) * _dsilu(cc_ref[...])

    return pl.pallas_call(body, out_shape=jax.ShapeDtypeStruct((1, D), F32), name=name)(parts, c_ctx)


def _adamw_update(g_ref, w_ref, m_ref, v_ref, go_ref, d_ref, mo_ref, vo_ref):
    g = g_ref[0].astype(F32)
    for s in range(1, g_ref.shape[0]):
        g = g + g_ref[s].astype(F32)
    mn = B1 * m_ref[...] + (1.0 - B1) * g
    vn = B2 * v_ref[...] + (1.0 - B2) * g * g
    go_ref[...] = g
    mo_ref[...] = mn
    vo_ref[...] = vn
    d_ref[...] = -LR * ((mn * (1.0 / (1.0 - B1 ** STEP))) / (jnp.sqrt(vn * (1.0 / (1.0 - B2 ** STEP))) + AEPS) + WD * w_ref[...])


def adamw(gstack, w, m, v, name, tr=256):
    n, R, C = gstack.shape
    tr = max(t for t in range(8, min(tr, R) + 1, 8) if R % t == 0)
    spec = pl.BlockSpec((tr, C), lambda i: (i, 0))
    return pl.pallas_call(_adamw_body(1), grid=(R // tr,),
                          in_specs=[pl.BlockSpec((n, tr, C), lambda i: (0, i, 0)), spec, spec, spec],
                          out_specs=[spec] * 4, out_shape=[jax.ShapeDtypeStruct((R, C), F32)] * 4,
                          compiler_params=_cp(("parallel",)), name=name)(gstack, w, m, v)


def _adamw_body(k):
    def body(*refs):
        for t in range(k):
            _adamw_update(*refs[4 * t:4 * t + 4], *refs[4 * k + 4 * t:4 * k + 4 * t + 4])
    return body


def adamw_multi(items, grid, name):
    k = len(items)
    ins, in_specs, out_specs, out_shape = [], [], [], []
    for g, g_spec, w, m, v, w_spec in items:
        ins += [g, w, m, v]
        in_specs += [g_spec, w_spec, w_spec, w_spec]
    for g, g_spec, w, m, v, w_spec in items:
        out_specs += [w_spec] * 4
        out_shape += [jax.ShapeDtypeStruct(w.shape, F32)] * 4
    res = pl.pallas_call(_adamw_body(k), grid=grid, in_specs=in_specs, out_specs=out_specs, out_shape=out_shape,
                         compiler_params=_cp(("arbitrary",) * len(grid)), name=name)(*ins)
    return [res[4 * t:4 * t + 4] for t in range(k)]


def _whole(a, grid_rank):
    zeros = (0,) * a.ndim
    return pl.BlockSpec(a.shape, lambda *idx: zeros)


def sum_slots(xs, name):
    def body(*refs):
        for x_ref, o_ref in zip(refs[:len(xs)], refs[len(xs):]):
            acc = x_ref[0]
            for s in range(1, NDEV):
                acc = acc + x_ref[s]
            o_ref[...] = acc

    return pl.pallas_call(body, out_shape=[jax.ShapeDtypeStruct(x.shape[1:], F32) for x in xs],
                          compiler_params=_cp(), name=name)(*xs)


def _col_shards(g):
    R, N = g.shape
    return g.reshape(R, NDEV, N // NDEV).transpose(1, 0, 2)


def _vec2(v):
    return jnp.broadcast_to(v.reshape(1, 1, -1), (2, 1, v.size))


SHARD_ROWS = {"mla_w_in": 192, "mla_w_uq": 192, "mla_w_ukv": 256, "s5_w_in": 256}


def _t_shard(wsh, rows):
    t = wsh[0].T.astype(BF16)
    return jnp.pad(t, ((0, rows - t.shape[0]), (0, 0)))


def _win_order():
    w = IN_W // NDEV
    perm = np.zeros((IN_WP, NDEV * SHARD_ROWS["mla_w_in"]), np.float32)
    first = QL + KVL + ROPE
    for c in range(IN_W):
        n = c + HEADS * VD if c < first else c - first
        perm[n, (c // w) * SHARD_ROWS["mla_w_in"] + c % w] = 1.0
    return jnp.asarray(perm, BF16)


def local_step(ctx, x, tgt, mod, Wt, small, l1_shards):
    T = LC + x.shape[0]
    xa = ("cat", ctx, x)
    sh = [mod[i, :, None, 0:D] for i in range(2)]
    sc = [mod[i, :, None, D:2 * D] for i in range(2)]
    gt = [mod[i, :, None, 2 * D:] for i in range(2)]
    ng = [_vec2(small["norm_g"][i]) for i in range(2)]
    qg, kvg = _vec2(small["mla_q_norm"]), _vec2(small["mla_kv_norm"])
    cosf, sinf, pm, pmt = _rope_tables(T)

    (h0, p0, cqn, ckvn), _ = rowwise(st_l0_pre, [xa], [ng[0], sc[0], sh[0], qg, kvg],
                                     [(D, BF16), (IN_WP, F32), (QL, BF16), (KVL, BF16)], [], "l0_pre", mats=[Wt["mla_w_in"]])
    z0, cq, ckv = (p0, 0, HEADS * VD), (p0, HEADS * VD // QL, QL), (p0, (HEADS * VD + QL) // KVL, KVL)
    Q = project_q(cqn, Wt["mla_w_uq"], cosf, sinf, pm, "l0_uq")
    K, V = project_kv(ckvn, Wt["mla_w_ukv"], p0, (HEADS * VD + QL + KVL) // 128, "l0_ukv")
    (o, lse), got = attn_fwd(Q, K, V, "l0_attn", rode=l1_shards, modes="gather")
    Wt, small = dict(Wt), dict(small)
    for n, a in zip(L1_BIG, got):
        Wt[n] = a.reshape(-1, a.shape[-1])
    vecs = lax.bitcast_convert_type(got[-1].reshape(NDEV, 2, -1, 2), F32)
    small["s5_d"], small["s5_b_glu"] = vecs[:, 0, :].reshape(D), vecs[:, 1, :].reshape(D)
    o2 = o.transpose(1, 0, 2).reshape(T, HEADS * VD)
    (og, out0, x1), _ = rowwise(st_l0_post, [o2, z0, xa], [gt[0]], [(D, BF16), (D, BF16), (D, F32)], [], "l0_post",
                                mats=[Wt["mla_w_out"]])

    ls = small["s5_log_step"].reshape(2, G, 1)
    a_re, a_im = small["s5_a_re"].reshape(2, G, P), small["s5_a_im"].reshape(2, G, P)
    b_re, b_im = small["s5_b_re"].reshape(2, G * P, CH), small["s5_b_im"].reshape(2, G * P, CH)
    lam_re, lam_im, f_re, f_im = disc_fwd(a_re, a_im, ls, "s5_disc")
    f_re2, f_im2 = f_re.reshape(2, G * P, 1), f_im.reshape(2, G * P, 1)
    bb_re, bb_im = disc_b(f_re2, f_im2, b_re, b_im, "s5_disc_b")
    compact = lambda m: m.reshape(2, NJ, UB, P)
    bre = compact(bb_re.reshape(2, G, P, CH).transpose(0, 1, 3, 2))
    bim = compact(bb_im.reshape(2, G, P, CH).transpose(0, 1, 3, 2))
    cre, cim = compact(small["s5_c_re"]), compact(small["s5_c_im"])
    lam_re4, lam_im4 = lam_re.reshape(2, NJ, 1, SB), lam_im.reshape(2, NJ, 1, SB)

    (h1, p1), _ = rowwise(st_l1_pre, [x1], [ng[1], sc[1], sh[1]], [(D, BF16), (2 * D, F32)], [], "l1_pre", mats=[Wt["s5_w_in"]])
    u, z1 = (p1, 0, D), (p1, 1, D)
    yssm = scan_fwd(p1, lam_re4, lam_im4, bre, bim, cre, cim, "s5_scan")
    dvec, bglu = _vec2(small["s5_d"]), _vec2(small["s5_b_glu"])
    fg = _vec2(small["final_g"])
    lat_mask = jnp.stack([jnp.zeros((1, D), F32), jnp.ones((1, D), F32)])
    (y, y1b, gl, y3, out1, dx2), (dfg, lvec) = rowwise(
        st_l1_mlp, [yssm, u, z1, x1, ("lat", tgt)], [dvec, bglu, gt[1], fg, lat_mask],
        [(D, F32), (D, BF16), (D, BF16), (D, BF16), (D, BF16), (D, F32)], [D, 128], "l1_mlp",
        mats=[Wt["s5_w_glu"], Wt["s5_w_out"]])

    (dz1, dy, du_d), (dgt1, dbglu, dd), (g_w_out5, g_w_glu) = rowwise(
        st_l1_mlp_bwd, [dx2, out1, y3, y, gl, z1, u, y1b], [gt[1], bglu, dvec], [(D, BF16), (D, F32), (D, F32)], [D, D, D],
        "l1_mlp_b", mats=[Wt["s5_w_out"], Wt["s5_w_glu"]], out_accs=[(D, D), (D, D)])
    du_s, dlr, dli, dbre, dbim, dcre, dcim = scan_bwd(p1, dy, lam_re4, lam_im4, bre, bim, cre, cim, "s5_scan_b")
    dbb_re = dbre.reshape(2, G, CH, P).transpose(0, 1, 3, 2).reshape(2, G * P, CH)
    dbb_im = dbim.reshape(2, G, CH, P).transpose(0, 1, 3, 2).reshape(2, G * P, CH)
    g_c_re, g_c_im = dcre.reshape(2, G, CH, P), dcim.reshape(2, G, CH, P)
    g_b_re, g_b_im, dfr, dfi = disc_b_bwd(f_re2, f_im2, b_re, b_im, dbb_re, dbb_im, "s5_disc_b_b")
    g_a_re, g_a_im, g_ls = disc_a_bwd(a_re, a_im, ls, dlr.reshape(2, G, P), dli.reshape(2, G, P),
                                      dfr.reshape(2, G, P), dfi.reshape(2, G, P), "s5_disc_b_a")
    (dx1,), (dsh1, dsc1, dng1), (g_w_in5,) = rowwise(
        st_l1_tail_bwd, [du_d, du_s, dz1, h1, x1, dx2], [ng[1], sc[1]], [(D, F32)], [D, D, D], "l1_pre_b",
        mats=[Wt["s5_w_in"]], out_accs=[(D, 2 * D)])
    g_w_in5 = _col_shards(g_w_in5)

    (do2, dz0), (dgt0,), (g_w_out,) = rowwise(st_l0_post_bwd, [dx1, out0, og, o2, z0], [gt[0]], [(D, F32), (D, F32)], [D],
                                              "l0_post_b", mats=[Wt["mla_w_out"]], out_accs=[(D, D)])
    doh = do2.reshape(T, HEADS, VD).transpose(1, 0, 2)
    rows8 = lambda g: g.reshape(NDEV, -1, g.shape[-1])
    both = lambda s: s[0, 0] + s[1, 0]
    dense = lambda g: g.reshape(2, G * P * CH // 128, 128)
    chunks = [dense(g_b_re), dense(g_b_im), g_c_re, g_c_im]
    l1_send = [g_w_in5, rows8(g_w_glu), rows8(g_w_out5), rows8(g_w_out),
               both(dd).reshape(NDEV, 1, -1), both(dbglu).reshape(NDEV, 1, -1)]
    (dQ, dK, dV), l1_recv = attn_bwd(Q, K, V, o, lse, doh, "l0_attn_b", rode=l1_send + chunks,
                                     modes=["lead"] * len(l1_send) + [a.shape[1] // NDEV for a in chunks])
    dqh = rope(dQ, cosf, sinf, pmt, True, BF16, "l0_rope_q_b", scale=SCALE)
    dq = dqh.transpose(1, 0, 2).reshape(T, HEADS * QK)
    dkv, dkr = split_kv_grads(dK, dV, "l0_kv_b")
    (grad_x,), (dqg, dkvg, dsh0, dsc0, dng0), (g_uq, g_ukv, g_p) = rowwise(
        st_l0_tail_bwd, [dq, dkv, dkr, dz0, cq, ckv, cqn, ckvn, h0, xa, dx1], [qg, kvg, ng[0], sc[0]],
        [(D, F32, "lat")], [QL, KVL, D, D, D], "l0_pre_b", mats=[Wt["mla_w_uq"], Wt["mla_w_ukv"], Wt["mla_w_in"]],
        out_accs=[(QL, HEADS * QK), (KVL, HEADS * KVW), (D, IN_WP)])
    g_w_uq, g_w_ukv = _col_shards(g_uq).astype(BF16), _col_shards(g_ukv).astype(BF16)
    g_w_in = _col_shards(jnp.concatenate([g_p[:, HEADS * VD:IN_W], g_p[:, :HEADS * VD]], axis=1)).astype(BF16)

    dmod = jnp.stack([jnp.concatenate([dsh0, dsc0, dgt0], axis=-1)[:, 0], jnp.concatenate([dsh1, dsc1, dgt1], axis=-1)[:, 0]])
    gbig = {"mla_w_in": g_w_in, "mla_w_uq": g_w_uq, "mla_w_ukv": g_w_ukv}
    gsmall = {"norm_g": jnp.stack([both(dng0), both(dng1)]), "mla_q_norm": both(dqg), "mla_kv_norm": both(dkvg),
              "s5_a_re": g_a_re, "s5_a_im": g_a_im, "s5_log_step": g_ls, "final_g": dfg[1, 0]}
    return lvec[1], grad_x, dmod, gbig, gsmall, l1_recv


COL_SHARDED = ("mla_w_in", "mla_w_uq", "mla_w_ukv", "s5_w_in")
ROW_SHARDED = ("mla_w_out", "s5_w_glu", "s5_w_out")
VEC_SHARDED = ("s5_d", "s5_b_glu")
BIG = COL_SHARDED + ROW_SHARDED
L0_BIG = ("mla_w_in", "mla_w_uq", "mla_w_ukv")
L1_BIG = ("s5_w_in", "s5_w_glu", "s5_w_out", "mla_w_out")
BITS16 = jnp.bfloat16
SMALL_RS = ("norm_g", "mla_q_norm", "mla_kv_norm", "s5_a_re", "s5_a_im", "s5_log_step", "s5_b_re", "s5_b_im",
            "s5_c_re", "s5_c_im", "final_g")
CHUNKED = ("s5_b_re", "s5_b_im", "s5_c_re", "s5_c_im")
DENSE = ("s5_b_re", "s5_b_im")
TINY = ("norm_g", "mla_q_norm", "mla_kv_norm", "s5_a_re", "s5_a_im", "s5_log_step", "final_g")
ORDER = ("c_ctx", "ada_w", "ada_b", "norm_g", "mla_w_in", "mla_q_norm", "mla_w_uq", "mla_kv_norm", "mla_w_ukv",
         "mla_w_out", "s5_w_in", "s5_a_re", "s5_a_im", "s5_log_step", "s5_b_re", "s5_b_im", "s5_c_re", "s5_c_im",
         "s5_d", "s5_w_glu", "s5_b_glu", "s5_w_out", "final_g")


def kernel(x, c, ctx, c_ctx, ada_w, ada_b, norm_g, mla_w_in, mla_q_norm, mla_w_uq, mla_kv_norm, mla_w_ukv, mla_w_out, s5_w_in, s5_a_re, s5_a_im, s5_log_step, s5_b_re, s5_b_im, s5_c_re, s5_c_im, s5_d, s5_w_glu, s5_b_glu, s5_w_out, final_g, loss_target, m_c_ctx, m_ada_w, m_ada_b, m_norm_g, m_mla_w_in, m_mla_q_norm, m_mla_w_uq, m_mla_kv_norm, m_mla_w_ukv, m_mla_w_out, m_s5_w_in, m_s5_a_re, m_s5_a_im, m_s5_log_step, m_s5_b_re, m_s5_b_im, m_s5_c_re, m_s5_c_im, m_s5_d, m_s5_w_glu, m_s5_b_glu, m_s5_w_out, m_final_g, v_c_ctx, v_ada_w, v_ada_b, v_norm_g, v_mla_w_in, v_mla_q_norm, v_mla_w_uq, v_mla_kv_norm, v_mla_w_ukv, v_mla_w_out, v_s5_w_in, v_s5_a_re, v_s5_a_im, v_s5_log_step, v_s5_b_re, v_s5_b_im, v_s5_c_re, v_s5_c_im, v_s5_d, v_s5_w_glu, v_s5_b_glu, v_s5_w_out, v_final_g):
    w = dict(c_ctx=c_ctx, ada_w=ada_w, ada_b=ada_b, norm_g=norm_g, mla_w_in=mla_w_in, mla_q_norm=mla_q_norm,
             mla_w_uq=mla_w_uq, mla_kv_norm=mla_kv_norm, mla_w_ukv=mla_w_ukv, mla_w_out=mla_w_out, s5_w_in=s5_w_in,
             s5_a_re=s5_a_re, s5_a_im=s5_a_im, s5_log_step=s5_log_step, s5_b_re=s5_b_re, s5_b_im=s5_b_im,
             s5_c_re=s5_c_re, s5_c_im=s5_c_im, s5_d=s5_d, s5_w_glu=s5_w_glu, s5_b_glu=s5_b_glu, s5_w_out=s5_w_out,
             final_g=final_g)
    m = dict(c_ctx=m_c_ctx, ada_w=m_ada_w, ada_b=m_ada_b, norm_g=m_norm_g, mla_w_in=m_mla_w_in, mla_q_norm=m_mla_q_norm,
             mla_w_uq=m_mla_w_uq, mla_kv_norm=m_mla_kv_norm, mla_w_ukv=m_mla_w_ukv, mla_w_out=m_mla_w_out,
             s5_w_in=m_s5_w_in, s5_a_re=m_s5_a_re, s5_a_im=m_s5_a_im, s5_log_step=m_s5_log_step, s5_b_re=m_s5_b_re,
             s5_b_im=m_s5_b_im, s5_c_re=m_s5_c_re, s5_c_im=m_s5_c_im, s5_d=m_s5_d, s5_w_glu=m_s5_w_glu,
             s5_b_glu=m_s5_b_glu, s5_w_out=m_s5_w_out, final_g=m_final_g)
    v = dict(c_ctx=v_c_ctx, ada_w=v_ada_w, ada_b=v_ada_b, norm_g=v_norm_g, mla_w_in=v_mla_w_in, mla_q_norm=v_mla_q_norm,
             mla_w_uq=v_mla_w_uq, mla_kv_norm=v_mla_kv_norm, mla_w_ukv=v_mla_w_ukv, mla_w_out=v_mla_w_out,
             s5_w_in=v_s5_w_in, s5_a_re=v_s5_a_re, s5_a_im=v_s5_a_im, s5_log_step=v_s5_log_step, s5_b_re=v_s5_b_re,
             s5_b_im=v_s5_b_im, s5_c_re=v_s5_c_re, s5_c_im=v_s5_c_im, s5_d=v_s5_d, s5_w_glu=v_s5_w_glu,
             s5_b_glu=v_s5_b_glu, s5_w_out=v_s5_w_out, final_g=v_final_g)

    me = 4 * lax.axis_index("x") + 2 * lax.axis_index("y") + lax.axis_index("c")
    WA = ada_w.shape[2]

    def shard(n):
        return _t_shard(w[n], SHARD_ROWS[n]) if n in COL_SHARDED else w[n][0].astype(BF16)

    wgot = exchange([c] + [shard(n) for n in L0_BIG], "gather", "gather_w")

    cg = wgot[0].reshape(NDEV, D)
    cc2 = c_ctx.reshape(1, D)
    ada_b_loc = lax.dynamic_slice_in_dim(ada_b.reshape(2, 3 * D // WA, WA), me, 1, axis=1)
    part = ada_fwd(cg, cc2, ada_w, ada_b_loc, "ada_fwd")
    pg = exchange([part], "gather", "gather_mod")[0]
    mod_l = lax.dynamic_index_in_dim(pg, me, axis=2, keepdims=False).transpose(1, 0, 2).reshape(2, 3 * D)
    mod_c = pg[:, :, NDEV, :].transpose(1, 0, 2).reshape(2, 3 * D)
    mod = jnp.stack([mod_c, mod_l], axis=1)

    Wt = {n: a.reshape(-1, a.shape[-1]) for n, a in zip(L0_BIG, wgot[1:])}
    Wt["mla_w_in"] = mm(_win_order(), Wt["mla_w_in"], "nn", "w_in_order", out_dtype=BF16)
    vec_bits = lax.bitcast_convert_type(jnp.concatenate([s5_d, s5_b_glu], axis=0), BITS16).reshape(2, -1)
    small = {n: w[n] for n in SMALL_RS}

    lvec, grad_x, dmod, gbig, gsmall, l1_recv = local_step(ctx[0], x[0], loss_target[0], mod, Wt, small,
                                                           [shard(n) for n in L1_BIG] + [vec_bits])
    grad_x = grad_x[None]

    per_dev = G // NDEV
    recv = dict(zip(L0_BIG, exchange([gbig[n] for n in L0_BIG], "lead", "scatter_grads")))
    recv.update(dict(zip(L1_BIG + VEC_SHARDED, l1_recv)))
    out = {}

    def keep(n, res):
        for key, arr in zip("gdmv", res):
            out[key, n] = arr.reshape(w[n].shape)

    for n in BIG:
        keep(n, adamw(recv[n], w[n][0], m[n][0], v[n][0], "adamw_" + n))
    reduced = sum_slots(l1_recv[len(L1_BIG + VEC_SHARDED):], "sum_chunks")

    kshape = lambda n: w[n].shape if w[n].ndim > 1 else (1, w[n].size)
    flat = jnp.concatenate([gsmall[n].reshape(-1) for n in TINY] + [dmod.reshape(-1), lvec.reshape(-1)])[None]
    bb_all, cc_all, flat_all = exchange([jnp.stack(reduced[:2]), jnp.stack(reduced[2:]), flat], "gather", "gather_small")
    chunk_all = [bb_all[:, 0], bb_all[:, 1], cc_all[:, 0], cc_all[:, 1]]
    tiny_all, off = [], 0
    for n in TINY:
        tiny_all.append(flat_all[:, 0, off:off + w[n].size].reshape((NDEV,) + kshape(n)))
        off += w[n].size
    dm_all = flat_all[:, 0, off:off + dmod.size].reshape((NDEV,) + dmod.shape)
    loss = sum_slots([flat_all[:, :, off + dmod.size:]], "loss_sum")[0][0, 0]

    dm_cols = lax.dynamic_slice_in_dim(dm_all.reshape(NDEV, 2, 2, 3 * D // WA, WA), me, 1, axis=3)[:, :, :, 0]
    dm_loc = jnp.concatenate([dm_cols[:, :, 1].transpose(1, 0, 2), dm_cols[:, :, 0].transpose(1, 0, 2)], axis=1)
    g_ada_w, dcc_part, g_ada_b = ada_bwd(cg, cc2, ada_w, dm_loc, dm_all.transpose(0, 2, 1, 3).reshape(2 * NDEV, 2, 3 * D), "ada_bwd")
    dcc_all = exchange([dcc_part], "gather", "gather_dcc")[0].reshape(NDEV, D)
    g_c_ctx = cctx_finish(dcc_all, cc2, "cctx_finish")

    flat2 = lambda t: t.reshape(-1, t.shape[-1])
    keep("ada_w", adamw(flat2(g_ada_w)[None], flat2(ada_w), flat2(m_ada_w), flat2(v_ada_w), "adamw_ada"))
    items = []
    halves = 2
    for n, g in zip(CHUNKED, chunk_all):
        blk = (1, 1, G // halves) + w[n].shape[3:]
        g = jnp.moveaxis(g, 0, 1).reshape(w[n].shape)
        g_spec = pl.BlockSpec((1,) + blk, lambda d, s: (0, 0, d, s, 0, 0))
        items.append((g[None], g_spec, w[n], m[n], v[n], pl.BlockSpec(blk, lambda d, s: (0, d, s, 0, 0))))
    for n, res in zip(CHUNKED, adamw_multi(items, (2, halves), "adamw_bc")):
        keep(n, res)
    tiny_g = dict(zip(TINY, tiny_all))
    tiny_g.update({n: recv[n] for n in VEC_SHARDED})
    tiny_g["c_ctx"], tiny_g["ada_b"] = g_c_ctx[None], g_ada_b[None]
    names = list(tiny_g)
    items = [(tiny_g[n], _whole(tiny_g[n], 1)) + tuple(t[n].reshape(kshape(n)) for t in (w, m, v))
             + (pl.BlockSpec(kshape(n), lambda i, r=len(kshape(n)): (0,) * r),) for n in names]
    for n, res in zip(names, adamw_multi(items, (1,), "adamw_small")):
        keep(n, res)

    return (loss, grad_x, *[out["g", n] for n in ORDER], *[out["d", n] for n in ORDER],
            *[out["m", n] for n in ORDER], *[out["v", n] for n in ORDER])
```

```python
import math

import numpy as np
import jax
import jax.numpy as jnp
from jax import lax
from jax.experimental import pallas as pl
from jax.experimental.pallas import tpu as pltpu

F32 = jnp.float32
BF16 = jnp.bfloat16

D = 1024
L = 2048
LC = 256
NDEV = 8
GRID_W = 64
EPS = 1e-6
HEADS = 16
NOPE = 64
ROPE = 32
QK = NOPE + ROPE
VD = 64
IN_W = 256 + 128 + ROPE + HEADS * 64
IN_WP = 1536
QL = 256
KVL = 128
SCALE = QK ** -0.5
LOG2E = math.log2(math.e)
THETA = 10000.0
G = 64
P = 64
CH = 16
GB = 8
NJ = G // GB
UB = GB * CH
SB = GB * P
SEG = 16
TB = 256
VMEM_LIMIT = 56 * 1024 * 1024
B1, B2, LR, AEPS, WD, STEP = 0.9, 0.999, 0.001, 1e-8, 0.01, 10
MESH_T = pl.DeviceIdType.MESH


def _cp(sem=None):
    return pltpu.CompilerParams(dimension_semantics=sem, vmem_limit_bytes=VMEM_LIMIT)


def _sig(x):
    return 1.0 / (1.0 + jnp.exp(-x))


def _silu(x):
    return x * _sig(x)


def _dsilu(x):
    s = _sig(x)
    return s * (1.0 + x * (1.0 - s))


_GK = math.sqrt(2.0 / math.pi)


def _gelu(x):
    return 0.5 * x * (1.0 + jnp.tanh(_GK * (x + 0.044715 * x * x * x)))


def _dgelu(x):
    t = jnp.tanh(_GK * (x + 0.044715 * x * x * x))
    return 0.5 * (1.0 + t) + 0.5 * x * (1.0 - t * t) * _GK * (1.0 + 3 * 0.044715 * x * x)


def _rs(x):
    return lax.rsqrt(jnp.mean(x * x, axis=-1, keepdims=True) + EPS)


def _sum0(x):
    return jnp.sum(x, axis=0, keepdims=True)


def st_norm_mod(x, g, sc, sh):
    y = x * _rs(x) * g
    return (y * (1.0 + sc) + sh,), ()


def st_norm_mod_bwd(x, dh, dres, g, sc):
    r = _rs(x)
    xn = x * r
    y = xn * g
    dy = dh * (1.0 + sc)
    dxn = dy * g
    dx = r * (dxn - xn * jnp.mean(dxn * xn, axis=-1, keepdims=True))
    return (dres + dx,), (_sum0(dh), _sum0(dh * y), _sum0(dy * xn))


def st_rms(x, g):
    return (x * _rs(x) * g,), ()


def st_rms_bwd(x, dy, g):
    r = _rs(x)
    n = x * r
    dn = dy * g
    dx = r * (dn - n * jnp.mean(dn * n, axis=-1, keepdims=True))
    return (dx,), (_sum0(dy * n),)


def st_rms2(x1, x2, g1, g2):
    return st_rms(x1, g1)[0] + st_rms(x2, g2)[0], ()


def st_rms2_bwd(x1, dy1, x2, dy2, g1, g2):
    (d1,), (s1,) = st_rms_bwd(x1, dy1, g1)
    (d2,), (s2,) = st_rms_bwd(x2, dy2, g2)
    return (d1, d2), (s1, s2)


def st_gate(o, z):
    return (o * _silu(z),), ()


def st_gate_bwd(dog, o, z):
    return (dog * _silu(z), dog * o * _dsilu(z)), ()


def st_resid(x, out, gt):
    return (x + gt * out,), ()


def st_resid_bwd(dx, out, gt):
    return (dx * gt,), (_sum0(dx * out),)


def st_s5a(yssm, u, d):
    y = yssm + d * u
    return (y, _gelu(y)), ()


def st_s5b(y, gl, z, b):
    return (_gelu(y) * _sig(gl + b) * _silu(z),), ()


def st_s5b_bwd(dy3, y, gl, z, b):
    y1 = _gelu(y)
    s = _sig(gl + b)
    dy2 = dy3 * _silu(z)
    dz = dy3 * y1 * s * _dsilu(z)
    dgl = dy2 * y1 * s * (1.0 - s)
    return (dgl, dz, dy2 * s), (_sum0(dgl),)


def st_s5a_bwd(dy1a, dy1b, y, u, d):
    dy = (dy1a + dy1b) * _dgelu(y)
    return (dy, dy * d), (_sum0(dy * u),)


def st_l0_pre(x, g, sc, sh, qg, kvg, w_in):
    hb = st_norm_mod(x, g, sc, sh)[0][0].astype(BF16)
    p = lax.dot_general(hb, w_in, _DN["nt"], preferred_element_type=F32)
    cq, ckv = p[:, HEADS * VD:HEADS * VD + QL], p[:, HEADS * VD + QL:HEADS * VD + QL + KVL]
    return (hb, p) + st_rms2(cq, ckv, qg, kvg)[0], ()


def st_l0_tail_bwd(dq, dkv, dkr, dz, cq, ckv, cqn, ckvn, h, x, dres, qg, kvg, g, sc, w_uq, w_ukv, w_in):
    dcqn = jnp.dot(dq, w_uq, preferred_element_type=F32)
    dckvn = jnp.dot(dkv, w_ukv, preferred_element_type=F32)
    (dcq, dckv), (dqg, dkvg) = st_rms2_bwd(cq, dcqn, ckv, dckvn, qg, kvg)
    dp = jnp.concatenate([dz, dcq, dckv, dkr], axis=1).astype(BF16)
    dh = jnp.dot(dp, w_in, preferred_element_type=F32)
    outs, sums = st_norm_mod_bwd(x, dh, dres, g, sc)
    tn = lambda a, b: lax.dot_general(a, b, _DN["tn"], preferred_element_type=F32)
    return outs, (dqg, dkvg) + sums, (tn(cqn, dq), tn(ckvn, dkv), tn(h, dp))


def st_l1_pre(x, g, sc, sh, w_in):
    hb = st_norm_mod(x, g, sc, sh)[0][0].astype(BF16)
    return (hb, lax.dot_general(hb, w_in, _DN["nt"], preferred_element_type=F32)), ()


def st_l1_tail_bwd(du_a, du_b, dz, h, x, dres, g, sc, w_in):
    dp = jnp.concatenate([(du_a + du_b).astype(BF16), dz], axis=1)
    dh = jnp.dot(dp, w_in, preferred_element_type=F32)
    outs, sums = st_norm_mod_bwd(x, dh, dres, g, sc)
    return outs, sums, (lax.dot_general(h, dp, _DN["tn"], preferred_element_type=F32),)


def st_l0_post(o, z, x, gt, w_out):
    og = (o * _silu(z)).astype(BF16)
    out = jnp.dot(og, w_out, preferred_element_type=F32)
    return (og, out, x + gt * out), ()


def st_l0_post_bwd(dx1, out, og, o, z, gt, w_out):
    (dout,), (dgt,) = st_resid_bwd(dx1, out.astype(F32), gt)
    doutb = dout.astype(BF16)
    dog = lax.dot_general(doutb, w_out, _DN["nt"], preferred_element_type=F32)
    return st_gate_bwd(dog, o, z)[0], (dgt,), (lax.dot_general(og, doutb, _DN["tn"], preferred_element_type=F32),)


def st_l1_mlp(yssm, u, z, x1, tgt, d, bglu, gt, fg, mask, w_glu, w_out):
    (y, y1), _ = st_s5a(yssm, u, d)
    y1b = y1.astype(BF16)
    gl = jnp.dot(y1b, w_glu, preferred_element_type=F32)
    y3 = (y1 * _sig(gl + bglu) * _silu(z)).astype(BF16)
    out = jnp.dot(y3, w_out, preferred_element_type=F32)
    (dx2,), sums = st_final(x1 + gt * out, tgt, fg, mask)
    return (y, y1b, gl, y3, out, dx2), sums


def st_l1_mlp_bwd(dx2, out, y3, y, gl, z, u, y1b, gt, bglu, d, w_out, w_glu):
    out, gl = out.astype(F32), gl.astype(F32)
    (dout,), (dgt,) = st_resid_bwd(dx2, out, gt)
    doutb = dout.astype(BF16)
    dy3 = lax.dot_general(doutb, w_out, _DN["nt"], preferred_element_type=F32)
    (dgl, dz, dy1a), (dbglu,) = st_s5b_bwd(dy3, y, gl, z, bglu)
    dglb = dgl.astype(BF16)
    dy1b = lax.dot_general(dglb, w_glu, _DN["nt"], preferred_element_type=F32)
    (dy, du), (dd,) = st_s5a_bwd(dy1a, dy1b, y, u, d)
    g_w_out = lax.dot_general(y3, doutb, _DN["tn"], preferred_element_type=F32)
    g_w_glu = lax.dot_general(y1b, dglb, _DN["tn"], preferred_element_type=F32)
    return (dz, dy, du), (dgt, dbglu, dd), (g_w_out, g_w_glu)


def st_final(x2, tgt, g, mask):
    r = _rs(x2)
    n = x2 * r
    e = n * g - tgt
    dyo = e * (1.0 / D)
    dn = dyo * g
    dx = r * (dn - n * jnp.mean(dn * n, axis=-1, keepdims=True))
    lsum = jnp.sum(_sum0(e * e), axis=1, keepdims=True) * (0.5 / D)
    return (dx * mask,), (_sum0(dyo * n), jnp.broadcast_to(lsum, (1, 128)))


def rowwise(fn, rows, vecs, out_rows, out_sums, name, mats=(), out_accs=()):
    lat_blk = lambda i: jnp.maximum(i - 1, 0)
    arrays, in_specs, pick = [], [], []
    for a in rows:
        if not isinstance(a, tuple):
            a = (a, 0, a.shape[1])
        tag = a[0] if isinstance(a[0], str) else None
        if tag == "cat":
            _, ctx, x = a
            arrays += [ctx, x]
            in_specs += [pl.BlockSpec((TB, ctx.shape[1]), lambda i: (0, 0)),
                         pl.BlockSpec((TB, x.shape[1]), lambda i: (lat_blk(i), 0))]
            pick.append(2)
        elif tag == "lat":
            arrays.append(a[1])
            in_specs.append(pl.BlockSpec((TB, a[1].shape[1]), lambda i: (lat_blk(i), 0)))
            pick.append(1)
        else:
            arr, cb, width = a
            arrays.append(arr)
            in_specs.append(pl.BlockSpec((TB, width), lambda i, cb=cb: (i, cb)))
            pick.append(1)
    T = LC + L
    nin, nv, nm, no, ns = len(arrays), len(vecs), len(mats), len(out_rows), len(out_sums)

    def body(*refs):
        i = pl.program_id(0)
        vals, k = [], 0
        for p in pick:
            if p == 2:
                vals.append(jnp.where(i == 0, refs[k][...], refs[k + 1][...]))
            else:
                vals.append(refs[k][...])
            k += p
        vals += [r[0] for r in refs[nin:nin + nv]] + [r[...] for r in refs[nin + nv:nin + nv + nm]]
        res = fn(*vals)
        first_out = nin + nv + nm
        for r, o in zip(refs[first_out:first_out + no], res[0]):
            r[...] = o.astype(r.dtype)
        sum_refs = refs[first_out + no:first_out + no + ns]
        if sum_refs:
            @pl.when(i <= 1)
            def _():
                for r in sum_refs:
                    r[...] = jnp.zeros_like(r)
            for r, s in zip(sum_refs, res[1]):
                r[0] += s
        acc_refs = refs[first_out + no + ns:]
        if acc_refs:
            @pl.when(i == 0)
            def _():
                for r in acc_refs:
                    r[...] = jnp.zeros_like(r)
            for r, a in zip(acc_refs, res[2]):
                r[...] += a

    kind = lambda i: (jnp.minimum(i, 1), 0, 0)
    in_specs += [pl.BlockSpec((1, 1, v.shape[2]), kind) for v in vecs]
    in_specs += [pl.BlockSpec(m.shape, lambda i: (0, 0), pipeline_mode=pl.Buffered(1)) for m in mats]
    out_specs, out_shape = [], []
    for o in out_rows:
        lat = len(o) == 3
        out_specs.append(pl.BlockSpec((TB, o[0]), (lambda i: (lat_blk(i), 0)) if lat else (lambda i: (i, 0))))
        out_shape.append(jax.ShapeDtypeStruct((L if lat else T, o[0]), o[1]))
    out_specs += [pl.BlockSpec((1, 1, c), kind) for c in out_sums]
    out_shape += [jax.ShapeDtypeStruct((2, 1, c), F32) for c in out_sums]
    out_specs += [pl.BlockSpec(s, lambda i: (0, 0)) for s in out_accs]
    out_shape += [jax.ShapeDtypeStruct(s, F32) for s in out_accs]
    res = pl.pallas_call(body, grid=(T // TB,), in_specs=in_specs, out_specs=out_specs, out_shape=out_shape,
                         compiler_params=_cp(("arbitrary",)), name=name)(*arrays, *vecs, *mats)
    if out_accs:
        return res[:no], res[no:no + ns], res[no + ns:]
    return res[:no], res[no:]


_DN = {"nn": (((1,), (0,)), ((), ())), "nt": (((1,), (1,)), ((), ())), "tn": (((0,), (0,)), ((), ()))}


def mm(a, b, mode, name, out_dtype=F32, tm=None, tn=None, shard_out=False):
    if mode == "nn":
        (M, K), (_, N) = a.shape, b.shape
    elif mode == "nt":
        (M, K), (N, _) = a.shape, b.shape
    else:
        (K, M), (_, N) = a.shape, b.shape
    if tm is None:
        tm = next((t for t in (768, 512, 256) if M % t == 0 and M > t), M)
    tn = N if tn is None else tn
    dn = _DN[mode]

    def body(a_ref, b_ref, o_ref):
        o_ref[...] = lax.dot_general(a_ref[...].astype(BF16), b_ref[...].astype(BF16), dn,
                                     preferred_element_type=F32).astype(o_ref.dtype)

    if shard_out:
        def body(a_ref, b_ref, o_ref):
            av = a_ref[...].astype(BF16)
            for j in range(N // tn):
                bj = b_ref[pl.ds(j * tn, tn), :] if mode == "nt" else b_ref[:, pl.ds(j * tn, tn)]
                o_ref[j] = lax.dot_general(av, bj.astype(BF16), dn, preferred_element_type=F32).astype(o_ref.dtype)

        a_spec = pl.BlockSpec((K, tm), lambda i: (0, i)) if mode == "tn" else pl.BlockSpec((tm, K), lambda i: (i, 0))
        return pl.pallas_call(body, grid=(M // tm,), in_specs=[a_spec, pl.BlockSpec(b.shape, lambda i: (0, 0))],
                              out_specs=pl.BlockSpec((N // tn, tm, tn), lambda i: (0, i, 0)),
                              out_shape=jax.ShapeDtypeStruct((N // tn, M, tn), out_dtype),
                              compiler_params=_cp(("parallel",)), name=name)(a, b)
    a_spec = pl.BlockSpec((K, tm), lambda i, j: (0, i)) if mode == "tn" else pl.BlockSpec((tm, K), lambda i, j: (i, 0))
    b_spec = pl.BlockSpec((tn, K), lambda i, j: (j, 0)) if mode == "nt" else pl.BlockSpec((K, tn), lambda i, j: (0, j))
    return pl.pallas_call(body, grid=(M // tm, N // tn), in_specs=[a_spec, b_spec],
                          out_specs=pl.BlockSpec((tm, tn), lambda i, j: (i, j)), out_shape=jax.ShapeDtypeStruct((M, N), out_dtype),
                          compiler_params=_cp(("parallel", "arbitrary")), name=name)(a, b)


def _rope_tables(T, width=QK, first=NOPE):
    nlat = T - LC
    pos = np.arange(nlat)
    row, col = pos // GRID_W, pos % GRID_W
    half = ROPE // 2
    inv = 1.0 / (THETA ** (np.arange(0, half, 2, dtype=np.float64) / half))
    cosf = np.ones((T, width), np.float64)
    sinf = np.zeros((T, width), np.float64)
    perm = np.zeros((width, width), np.float32)
    for m in range(ROPE):
        j = first + m
        blk, w = m // half, m % half
        ang = (row if blk == 0 else col)[:, None] * inv[None, :]
        f = w % (half // 2)
        cosf[LC:, j] = np.cos(ang[:, f])
        if w < half // 2:
            sinf[LC:, j] = -np.sin(ang[:, f])
            perm[j + half // 2, j] = 1.0
        else:
            sinf[LC:, j] = np.sin(ang[:, f])
            perm[j - half // 2, j] = 1.0
    return jnp.asarray(cosf, F32), jnp.asarray(sinf, F32), jnp.asarray(perm, BF16), jnp.asarray(perm.T, BF16)


def _exact_perm(x, pm):
    hi = x.astype(BF16)
    r1 = x - hi.astype(F32)
    mid = r1.astype(BF16)
    lo = (r1 - mid.astype(F32)).astype(BF16)
    dot = lambda a: jnp.dot(a, pm, preferred_element_type=F32)
    return dot(hi) + dot(mid) + dot(lo)


def _rot(x, cv, sv, pv, inverse):
    if inverse:
        return x * cv + _exact_perm(x * sv, pv)
    return x * cv + _exact_perm(x, pv) * sv


def rope(x, cosf, sinf, pm, inverse, out_dtype, name, scale=1.0):
    H, T, _ = x.shape

    def body(x_ref, c_ref, s_ref, p_ref, o_ref):
        cv, sv, pv = c_ref[...], s_ref[...], p_ref[...]
        for h in range(H):
            o_ref[h] = (_rot(x_ref[h], cv, sv, pv, inverse) * scale).astype(o_ref.dtype)

    return pl.pallas_call(
        body, grid=(T // TB,),
        in_specs=[pl.BlockSpec((H, TB, QK), lambda i: (0, i, 0)), pl.BlockSpec((TB, QK), lambda i: (i, 0)),
                  pl.BlockSpec((TB, QK), lambda i: (i, 0)), pl.BlockSpec((QK, QK), lambda i: (0, 0))],
        out_specs=pl.BlockSpec((H, TB, QK), lambda i: (0, i, 0)), out_shape=jax.ShapeDtypeStruct((H, T, QK), out_dtype),
        compiler_params=_cp(("parallel",)), name=name)(x, cosf, sinf, pm)


KVW = NOPE + VD


def _kv_selectors():
    s_kn = np.zeros((KVW, QK), np.float32)
    s_kr = np.zeros((128, QK), np.float32)
    s_v = np.zeros((KVW, VD), np.float32)
    for l in range(NOPE):
        s_kn[l, l] = 1.0
    for l in range(ROPE):
        s_kr[l, NOPE + l] = 1.0
    for l in range(VD):
        s_v[NOPE + l, l] = 1.0
    return s_kn, s_kr, s_v


def project_q(cqn, w, cosf, sinf, pm, name):
    T = cqn.shape[0]

    def body(a_ref, w_ref, c_ref, s_ref, p_ref, o_ref):
        a, cv, sv, pv = a_ref[...], c_ref[...], s_ref[...], p_ref[...]
        for h in range(HEADS):
            qh = _dotf(a, w_ref[pl.ds(h * QK, QK), :], "nt")
            o_ref[h] = (_rot(qh, cv, sv, pv, False) * (SCALE * LOG2E)).astype(BF16)

    rows = lambda c: pl.BlockSpec((TB, c), lambda i: (i, 0))
    const = lambda x: pl.BlockSpec(x.shape, lambda i: (0, 0))
    return pl.pallas_call(
        body, grid=(T // TB,), in_specs=[rows(QL), const(w), rows(QK), rows(QK), const(pm)],
        out_specs=pl.BlockSpec((HEADS, TB, QK), lambda i: (0, i, 0)), out_shape=jax.ShapeDtypeStruct((HEADS, T, QK), BF16),
        compiler_params=_cp(("parallel",)), name=name)(cqn, w, cosf, sinf, pm)


def project_kv(ckvn, w, p0, kr_block, name):
    T = ckvn.shape[0]
    cosf, sinf, pm, _ = _rope_tables(T, 128, 0)
    s_kn, s_kr, s_v = (jnp.asarray(s, BF16) for s in _kv_selectors())

    def body(a_ref, w_ref, kr_ref, c_ref, s_ref, p_ref, skn_ref, skr_ref, sv_ref, k_ref, v_ref):
        a = a_ref[...]
        krr = _rot(kr_ref[...], c_ref[...], s_ref[...], p_ref[...], False).astype(BF16)
        kr_part = jnp.dot(krr, skr_ref[...], preferred_element_type=F32)
        for h in range(HEADS):
            kvb = _dotf(a, w_ref[pl.ds(h * KVW, KVW), :], "nt").astype(BF16)
            k_ref[h] = (jnp.dot(kvb, skn_ref[...], preferred_element_type=F32) + kr_part).astype(BF16)
            v_ref[h] = jnp.dot(kvb, sv_ref[...], preferred_element_type=F32).astype(BF16)

    rows = lambda c: pl.BlockSpec((TB, c), lambda i: (i, 0))
    const = lambda x: pl.BlockSpec(x.shape, lambda i: (0, 0))
    return pl.pallas_call(
        body, grid=(T // TB,),
        in_specs=[rows(KVL), const(w), pl.BlockSpec((TB, 128), lambda i: (i, kr_block)),
                  rows(128), rows(128), const(pm), const(s_kn), const(s_kr), const(s_v)],
        out_specs=[pl.BlockSpec((HEADS, TB, QK), lambda i: (0, i, 0)), pl.BlockSpec((HEADS, TB, VD), lambda i: (0, i, 0))],
        out_shape=[jax.ShapeDtypeStruct((HEADS, T, QK), BF16), jax.ShapeDtypeStruct((HEADS, T, VD), BF16)],
        compiler_params=_cp(("parallel",)), name=name)(ckvn, w, p0, cosf, sinf, pm, s_kn, s_kr, s_v)


def split_kv_grads(dk, dv, name):
    H, T, _ = dk.shape
    cosf, sinf, _, pmt = _rope_tables(T, 128, 0)
    s_kn, s_kr, s_v = _kv_selectors()
    s_knt, s_krt, s_vt = (jnp.asarray(s.T, BF16) for s in (s_kn, s_kr, s_v))

    def body(dk_ref, dv_ref, c_ref, s_ref, p_ref, skn_ref, skr_ref, sv_ref, dkv_ref, dkr_ref):
        total = None
        for h in range(H):
            dkh = dk_ref[h] * (1.0 / LOG2E)
            total = dkh if total is None else total + dkh
            dkv_ref[:, pl.ds(h * KVW, KVW)] = (
                jnp.dot(dkh.astype(BF16), skn_ref[...], preferred_element_type=F32)
                + jnp.dot(dv_ref[h].astype(BF16), sv_ref[...], preferred_element_type=F32)).astype(BF16)
        dkr_ref[...] = _rot(_exact_perm(total, skr_ref[...]), c_ref[...], s_ref[...], p_ref[...], True)

    rows = lambda c: pl.BlockSpec((TB, c), lambda i: (i, 0))
    const = lambda a: pl.BlockSpec(a.shape, lambda i: (0, 0))
    return pl.pallas_call(
        body, grid=(T // TB,),
        in_specs=[pl.BlockSpec((H, TB, QK), lambda i: (0, i, 0)), pl.BlockSpec((H, TB, VD), lambda i: (0, i, 0)),
                  rows(128), rows(128), const(pmt), const(s_knt), const(s_krt), const(s_vt)],
        out_specs=[rows(H * KVW), rows(128)],
        out_shape=[jax.ShapeDtypeStruct((T, H * KVW), BF16), jax.ShapeDtypeStruct((T, 128), F32)],
        compiler_params=_cp(("parallel",)), name=name)(dk, dv, cosf, sinf, pmt, s_knt, s_krt, s_vt)


def _by_query_block(run, T):
    @pl.when(pl.program_id(1) == 0)
    def _():
        run(LC)

    @pl.when(pl.program_id(1) > 0)
    def _():
        run(T)


def _with_rider(body, nin, nout, ride, grid):
    if ride is None:
        return body
    n = ride.n

    def wrapped(*refs):
        ins, xs = refs[:nin], refs[nin:nin + n]
        outs, got = refs[nin + n:nin + n + nout], refs[nin + n + nout:nin + 2 * n + nout]
        sems = refs[nin + 2 * n + nout:]
        step = pl.program_id(0) * grid[1] + pl.program_id(1)

        @pl.when(step == 0)
        def _():
            ride.start(xs, got, sems)

        body(*ins, *outs)

        @pl.when(step == grid[0] * grid[1] - 1)
        def _():
            ride.finish(xs, got, sems)

    return wrapped


def _ride_call(body, grid, in_specs, out_specs, out_shape, ride, rode, name, args):
    if ride is None:
        return pl.pallas_call(body, grid=grid, in_specs=in_specs, out_specs=out_specs, out_shape=out_shape,
                              compiler_params=_cp(("parallel", "arbitrary")), name=name)(*args), []
    res = pl.pallas_call(
        _with_rider(body, len(in_specs), len(out_specs), ride, grid), grid=grid,
        in_specs=in_specs + ride.specs, out_specs=out_specs + ride.specs, out_shape=out_shape + ride.out_shape,
        scratch_shapes=ride.scratch,
        compiler_params=pltpu.CompilerParams(dimension_semantics=("arbitrary", "arbitrary"), vmem_limit_bytes=VMEM_LIMIT,
                                             has_side_effects=True), name=name)(*args, *rode)
    return res[:len(out_specs)], res[len(out_specs):]


def attn_fwd(q, k, v, name, rode=None, modes=None):
    H, T, _ = q.shape

    def body(q_ref, k_ref, v_ref, o_ref, lse_ref):
        def run(nk):
            s = _dotf(q_ref[0], k_ref[0, pl.ds(0, nk), :], "nt")
            m = jnp.max(s, axis=1, keepdims=True)
            p = jnp.exp2(s - m)
            l = jnp.sum(p, axis=1, keepdims=True)
            o = jnp.dot(p.astype(BF16), v_ref[0, pl.ds(0, nk), :], preferred_element_type=F32)
            o_ref[0] = o / l
            lse_ref[0] = m + jnp.log2(l)

        _by_query_block(run, T)

    return _ride_call(
        body, (H, T // TB),
        [pl.BlockSpec((1, TB, QK), lambda h, i: (h, i, 0)), pl.BlockSpec((1, T, QK), lambda h, i: (h, 0, 0)),
         pl.BlockSpec((1, T, VD), lambda h, i: (h, 0, 0))],
        [pl.BlockSpec((1, TB, VD), lambda h, i: (h, i, 0)), pl.BlockSpec((1, TB, 1), lambda h, i: (h, i, 0))],
        [jax.ShapeDtypeStruct((H, T, VD), F32), jax.ShapeDtypeStruct((H, T, 1), F32)],
        Exchange(rode, modes) if rode else None, rode, name, (q, k, v))


def attn_bwd(q, k, v, o, lse, do, name, rode=None, modes=None):
    H, T, _ = q.shape

    def body(q_ref, k_ref, v_ref, o_ref, lse_ref, do_ref, dq_ref, dk_ref, dv_ref):
        i = pl.program_id(1)

        @pl.when(i == 0)
        def _():
            dk_ref[...] = jnp.zeros_like(dk_ref)
            dv_ref[...] = jnp.zeros_like(dv_ref)

        def run(nk):
            keys = pl.ds(0, nk)
            qv, kv, dov = q_ref[0], k_ref[0, keys, :], do_ref[0]
            p = jnp.exp2(_dotf(qv, kv, "nt") - lse_ref[0])
            delta = jnp.sum(dov * o_ref[0], axis=1, keepdims=True)
            dob = dov.astype(BF16)
            dv_ref[0, keys, :] += _dotf(p.astype(BF16), dob, "tn")
            dp = _dotf(dob, v_ref[0, keys, :], "nt")
            ds = (p * (dp - delta)).astype(BF16)
            dq_ref[0] = jnp.dot(ds, kv, preferred_element_type=F32)
            dk_ref[0, keys, :] += _dotf(ds, qv, "tn")

        _by_query_block(run, T)

    blk = lambda c: pl.BlockSpec((1, TB, c), lambda h, i: (h, i, 0))
    full = lambda c: pl.BlockSpec((1, T, c), lambda h, i: (h, 0, 0))
    return _ride_call(
        body, (H, T // TB), [blk(QK), full(QK), full(VD), blk(VD), blk(1), blk(VD)], [blk(QK), full(QK), full(VD)],
        [jax.ShapeDtypeStruct((H, T, QK), F32), jax.ShapeDtypeStruct((H, T, QK), F32), jax.ShapeDtypeStruct((H, T, VD), F32)],
        Exchange(rode, modes) if rode else None, rode, name, (q, k, v, o, lse, do))


def disc_fwd(a_re, a_im, ls, name):
    def body(ar_ref, ai_ref, ls_ref, lr_ref, li_ref, fr_ref, fi_ref):
        ar, ai = ar_ref[...], ai_ref[...]
        dt = jnp.exp(ls_ref[...])
        mag = jnp.exp(ar * dt)
        lr = mag * jnp.cos(ai * dt)
        li = mag * jnp.sin(ai * dt)
        den = ar * ar + ai * ai
        nr = lr - 1.0
        lr_ref[...] = lr
        li_ref[...] = li
        fr_ref[...] = (nr * ar + li * ai) / den
        fi_ref[...] = (li * ar - nr * ai) / den

    return pl.pallas_call(body, out_shape=[jax.ShapeDtypeStruct(a_re.shape, F32)] * 4, name=name)(a_re, a_im, ls)


def disc_b(f_re, f_im, b_re, b_im, name):
    def body(fr_ref, fi_ref, br_ref, bi_ref, or_ref, oi_ref):
        fr, fi, br, bi = fr_ref[...], fi_ref[...], br_ref[...], bi_ref[...]
        or_ref[...] = fr * br - fi * bi
        oi_ref[...] = fr * bi + fi * br

    fs, bs = _disc_b_specs()
    return pl.pallas_call(body, grid=(2, G * P // DISC_ROWS), in_specs=[fs, fs, bs, bs], out_specs=[bs, bs],
                          out_shape=[jax.ShapeDtypeStruct(b_re.shape, F32)] * 2, name=name)(f_re, f_im, b_re, b_im)


DISC_ROWS = G * P


def _disc_b_specs():
    return (pl.BlockSpec((1, DISC_ROWS, 1), lambda d, i: (d, i, 0)), pl.BlockSpec((1, DISC_ROWS, CH), lambda d, i: (d, i, 0)))


def disc_b_bwd(f_re, f_im, b_re, b_im, dbb_re, dbb_im, name):
    def body(fr_ref, fi_ref, br_ref, bi_ref, dr_ref, di_ref, dbr_ref, dbi_ref, dfr_ref, dfi_ref):
        fr, fi, br, bi, dr, di = fr_ref[...], fi_ref[...], br_ref[...], bi_ref[...], dr_ref[...], di_ref[...]
        dbr_ref[...] = fr * dr + fi * di
        dbi_ref[...] = fr * di - fi * dr
        dfr_ref[...] = jnp.sum(dr * br + di * bi, axis=-1, keepdims=True)
        dfi_ref[...] = jnp.sum(di * br - dr * bi, axis=-1, keepdims=True)

    fs, bs = _disc_b_specs()
    return pl.pallas_call(body, grid=(2, G * P // DISC_ROWS), in_specs=[fs, fs, bs, bs, bs, bs], out_specs=[bs, bs, fs, fs],
                          out_shape=[jax.ShapeDtypeStruct(b_re.shape, F32)] * 2 + [jax.ShapeDtypeStruct(f_re.shape, F32)] * 2,
                          name=name)(f_re, f_im, b_re, b_im, dbb_re, dbb_im)


def disc_a_bwd(a_re, a_im, ls, dlr, dli, dfr, dfi, name):
    def body(ar_ref, ai_ref, ls_ref, dlr_ref, dli_ref, dfr_ref, dfi_ref, dar_ref, dai_ref, dls_ref):
        ar, ai = ar_ref[...], ai_ref[...]
        dt = jnp.exp(ls_ref[...])
        mag = jnp.exp(ar * dt)
        cs, sn = jnp.cos(ai * dt), jnp.sin(ai * dt)
        lr, li = mag * cs, mag * sn
        den = ar * ar + ai * ai
        nr = lr - 1.0
        f_re = (nr * ar + li * ai) / den
        f_im = (li * ar - nr * ai) / den
        dn1 = dfr_ref[...] / den
        dn2 = dfi_ref[...] / den
        dden = -(dfr_ref[...] * f_re + dfi_ref[...] * f_im) / den
        dlr_t = dlr_ref[...] + dn1 * ar - dn2 * ai
        dli_t = dli_ref[...] + dn1 * ai + dn2 * ar
        dar = dn1 * nr + dn2 * li + dden * 2.0 * ar
        dai = dn1 * li - dn2 * nr + dden * 2.0 * ai
        dmag = dlr_t * cs + dli_t * sn
        dth = dli_t * lr - dlr_t * li
        dar_ref[...] = dar + dmag * mag * dt
        dai_ref[...] = dai + dth * dt
        dls_ref[...] = jnp.sum(dmag * mag * ar + dth * ai, axis=-1, keepdims=True) * dt

    return pl.pallas_call(body, out_shape=[jax.ShapeDtypeStruct(a_re.shape, F32)] * 2 +
                          [jax.ShapeDtypeStruct(ls.shape, F32)], name=name)(a_re, a_im, ls, dlr, dli, dfr, dfi)


def _cpow(lr, li, n):
    rr, ri = None, None
    br, bi = lr, li
    while n:
        if n & 1:
            if rr is None:
                rr, ri = br, bi
            else:
                rr, ri = rr * br - ri * bi, rr * bi + ri * br
        n >>= 1
        if n:
            br, bi = br * br - bi * bi, 2.0 * br * bi
    return rr, ri


UNROLL = 4


def _steps(trips, fn, init):
    main = trips // UNROLL

    def body(i, c):
        for j in range(UNROLL):
            c = fn(i * UNROLL + j, c)
        return c

    c = lax.fori_loop(0, main, body, init) if main else init
    for n in range(main * UNROLL, trips):
        c = fn(n, c)
    return c


def _seg_scan(xre, xim, lam8, pw, base, seglen, rev, init, fin_re, fin_im, ini_re, ini_im, prev=None):
    lr, li = lam8
    nsub = SEG // 8

    def rows(t, s):
        first = base + t * SEG + 8 * s
        return pl.ds(first if isinstance(first, int) else pl.multiple_of(first, 8), 8)

    tmap = (lambda n: seglen - 1 - n) if rev else (lambda n: n)
    zeros = tuple(jnp.zeros((8, SB), F32) for _ in range(2 * nsub))

    def advance(c, t):
        out = []
        for s in range(nsub):
            a, b = c[2 * s], c[2 * s + 1]
            out += [lr * a - li * b + xre[rows(t, s), :], lr * b + li * a + xim[rows(t, s), :]]
        return tuple(out)

    fin = _steps(seglen, lambda n, c: advance(c, tmap(n)), zeros)
    for s in range(nsub):
        fin_re[pl.ds(8 * s, 8), :] = fin[2 * s]
        fin_im[pl.ds(8 * s, 8), :] = fin[2 * s + 1]
    (cr, ci), (pr, pi) = init, pw
    for i in (range(SEG - 1, -1, -1) if rev else range(SEG)):
        ini_re[pl.ds(i, 1), :] = cr
        ini_im[pl.ds(i, 1), :] = ci
        cr, ci = pr * cr - pi * ci + fin_re[pl.ds(i, 1), :], pr * ci + pi * cr + fin_im[pl.ds(i, 1), :]
    tiles = lambda re, im: tuple(r[pl.ds(8 * s, 8), :] for s in range(nsub) for r in (re, im))
    start = tiles(ini_re, ini_im)

    def store(c, t):
        new = advance(c, t)
        for s in range(nsub):
            xre[rows(t, s), :] = new[2 * s]
            xim[rows(t, s), :] = new[2 * s + 1]
        return new

    if prev is None:
        _steps(seglen, lambda n, c: store(c, tmap(n)), start)
        return (cr, ci), None

    sre, sim, s_ini_re, s_ini_im = prev

    def acc_step(c, t, before):
        new = store(c[:2 * nsub], t)
        acc = []
        for s in range(nsub):
            (na, nb), (pre, pim) = new[2 * s:2 * s + 2], before[2 * s:2 * s + 2]
            acc += [c[2 * nsub + 2 * s] + na * pre + nb * pim, c[2 * nsub + 2 * s + 1] + nb * pre - na * pim]
        return new + tuple(acc)

    def body(n, c):
        t = tmap(n)
        tp = t - 1 if rev else t + 1
        return acc_step(c, t, tuple(r[rows(tp, s), :] for s in range(nsub) for r in (sre, sim)))

    c = _steps(seglen - 1, body, start + zeros)
    c = acc_step(c, 0 if rev else seglen - 1, tiles(s_ini_re, s_ini_im))
    acc = c[2 * nsub:]
    return (cr, ci), (sum(acc[0::2][1:], acc[0]), sum(acc[1::2][1:], acc[1]))


def _lam_tiles(lr, li, lens, conj=False):
    if conj:
        li = -li
    lam8 = (jnp.broadcast_to(lr, (8, SB)), jnp.broadcast_to(li, (8, SB)))
    return lam8, [_cpow(lr, li, n) for n in lens]


def _stretches(T):
    return ((0, LC // SEG), (LC, (T - LC) // SEG))


def _to_seg_order(src, dst, T):
    for base, seglen in _stretches(T):
        def body(t, carry, base=base, seglen=seglen):
            dst[pl.ds(pl.multiple_of(base + t * SEG, SEG), SEG), :] = src[pl.ds(base + t, SEG, stride=seglen), :]
            return carry
        lax.fori_loop(0, seglen, body, 0, unroll=8)


def _from_seg_order(src, dst, T):
    for base, seglen in _stretches(T):
        def body(t, carry, base=base, seglen=seglen):
            dst[pl.ds(base + t, SEG, stride=seglen), :] = src[pl.ds(pl.multiple_of(base + t * SEG, SEG), SEG), :]
            return carry
        lax.fori_loop(0, seglen, body, 0, unroll=8)


def _scan_specs(T):
    ublk = pl.BlockSpec((T, UB), lambda j: (0, j))
    lam = pl.BlockSpec((2, 1, 1, SB), lambda j: (0, j, 0, 0))
    mat = pl.BlockSpec((2, 1, UB, P), lambda j: (0, j, 0, 0))
    return ublk, lam, mat


def _dotf(a, b, mode="nn"):
    return lax.dot_general(a, b, _DN[mode], preferred_element_type=F32)


def _diag_mask():
    r = lax.broadcasted_iota(jnp.int32, (UB, SB), 0)
    c = lax.broadcasted_iota(jnp.int32, (UB, SB), 1)
    return lax.shift_right_logical(r, int(math.log2(CH))) == lax.shift_right_logical(c, int(math.log2(P)))


def _expand(m):
    p = lax.broadcasted_iota(jnp.int32, (P, SB), 0)
    c = lax.broadcasted_iota(jnp.int32, (P, SB), 1)
    tile = jnp.where(lax.bitwise_and(c, P - 1) == p, 1.0, 0.0).astype(BF16)
    wide = jnp.dot(m.astype(BF16), tile, preferred_element_type=F32)
    return jnp.where(_diag_mask(), wide, 0.0).astype(BF16)


def _collapse(full):
    c = lax.broadcasted_iota(jnp.int32, (SB, P), 0)
    p = lax.broadcasted_iota(jnp.int32, (SB, P), 1)
    pick = jnp.where(lax.bitwise_and(c, P - 1) == p, 1.0, 0.0).astype(BF16)
    return _exact_perm(jnp.where(_diag_mask(), full, 0.0), pick)


def _zero_state():
    return jnp.zeros((1, SB), F32), jnp.zeros((1, SB), F32)


def scan_fwd(u, lam_re, lam_im, bre, bim, cre, cim, name):
    T = u.shape[0]
    s_ctx, s_lat = LC // SEG, (T - LC) // SEG

    def body(u_ref, lr_ref, li_ref, bre_ref, bim_ref, cre_ref, cim_ref, y_ref, us, ys, sre, sim, fre, fim, ire, iim):
        _to_seg_order(u_ref, us, T)
        ub = us[...].astype(BF16)
        for d in range(2):
            lam8, (pw_c, pw_l) = _lam_tiles(lr_ref[d, 0], li_ref[d, 0], (s_ctx, s_lat))
            sre[...] = _dotf(ub, _expand(bre_ref[d, 0]))
            sim[...] = _dotf(ub, _expand(bim_ref[d, 0]))
            end_c, _ = _seg_scan(sre, sim, lam8, pw_c, 0, s_ctx, bool(d), _zero_state(), fre, fim, ire, iim)
            _seg_scan(sre, sim, lam8, pw_l, LC, s_lat, bool(d), end_c, fre, fim, ire, iim)
            y = (_dotf(sre[...].astype(BF16), _expand(cre_ref[d, 0]), "nt")
                 - _dotf(sim[...].astype(BF16), _expand(cim_ref[d, 0]), "nt"))
            if d == 0:
                ys[...] = y
            else:
                ys[...] += y
        _from_seg_order(ys, y_ref, T)

    ublk, lam, mat = _scan_specs(T)
    return pl.pallas_call(
        body, grid=(NJ,), in_specs=[ublk, lam, lam, mat, mat, mat, mat], out_specs=ublk,
        out_shape=jax.ShapeDtypeStruct((T, G * CH), F32),
        scratch_shapes=[pltpu.VMEM((T, UB), F32)] * 2 + [pltpu.VMEM((T, SB), F32)] * 2 + [pltpu.VMEM((SEG, SB), F32)] * 4,
        compiler_params=_cp(("arbitrary",)), name=name)(u, lam_re, lam_im, bre, bim, cre, cim)


def scan_bwd(u, dy, lam_re, lam_im, bre, bim, cre, cim, name):
    T = u.shape[0]
    s_ctx, s_lat = LC // SEG, (T - LC) // SEG

    def body(u_ref, dy_ref, lr_ref, li_ref, bre_ref, bim_ref, cre_ref, cim_ref,
             du_ref, dlr_ref, dli_ref, dbre_ref, dbim_ref, dcre_ref, dcim_ref,
             us, dys, dus, sre, sim, gre, gim, fre, fim, ic_re, ic_im, il_re, il_im, jre, jim):
        _to_seg_order(u_ref, us, T)
        _to_seg_order(dy_ref, dys, T)
        ub, dyb = us[...].astype(BF16), dys[...].astype(BF16)
        for d in range(2):
            rev = bool(d)
            lam8, (pw_c, pw_l) = _lam_tiles(lr_ref[d, 0], li_ref[d, 0], (s_ctx, s_lat))
            cam8, (cw_c, cw_l) = _lam_tiles(lr_ref[d, 0], li_ref[d, 0], (s_ctx, s_lat), conj=True)
            bre_v, bim_v = _expand(bre_ref[d, 0]), _expand(bim_ref[d, 0])
            sre[...] = _dotf(ub, bre_v)
            sim[...] = _dotf(ub, bim_v)
            end_c, _ = _seg_scan(sre, sim, lam8, pw_c, 0, s_ctx, rev, _zero_state(), fre, fim, ic_re, ic_im)
            _seg_scan(sre, sim, lam8, pw_l, LC, s_lat, rev, end_c, fre, fim, il_re, il_im)
            gre[...] = _dotf(dyb, _expand(cre_ref[d, 0]))
            gim[...] = -_dotf(dyb, _expand(cim_ref[d, 0]))
            end_g, acc_l = _seg_scan(gre, gim, cam8, cw_l, LC, s_lat, not rev, _zero_state(), fre, fim, jre, jim,
                                     prev=(sre, sim, il_re, il_im))
            _, acc_c = _seg_scan(gre, gim, cam8, cw_c, 0, s_ctx, not rev, end_g, fre, fim, jre, jim,
                                 prev=(sre, sim, ic_re, ic_im))
            dlr_ref[d, 0] = _sum0(acc_l[0] + acc_c[0])
            dli_ref[d, 0] = _sum0(acc_l[1] + acc_c[1])
            grb, gib = gre[...].astype(BF16), gim[...].astype(BF16)
            du = _dotf(grb, bre_v, "nt") + _dotf(gib, bim_v, "nt")
            if d == 0:
                dus[...] = du
            else:
                dus[...] += du
            dbre_ref[d, 0] = _collapse(_dotf(ub, grb, "tn"))
            dbim_ref[d, 0] = _collapse(_dotf(ub, gib, "tn"))
            dcre_ref[d, 0] = _collapse(_dotf(dyb, sre[...].astype(BF16), "tn"))
            dcim_ref[d, 0] = -_collapse(_dotf(dyb, sim[...].astype(BF16), "tn"))
        _from_seg_order(dus, du_ref, T)

    ublk, lam, mat = _scan_specs(T)
    lam_s = jax.ShapeDtypeStruct(lam_re.shape, F32)
    mat_s = jax.ShapeDtypeStruct(bre.shape, F32)
    return pl.pallas_call(
        body, grid=(NJ,), in_specs=[ublk, ublk, lam, lam, mat, mat, mat, mat],
        out_specs=[ublk, lam, lam, mat, mat, mat, mat],
        out_shape=[jax.ShapeDtypeStruct((T, G * CH), F32), lam_s, lam_s, mat_s, mat_s, mat_s, mat_s],
        scratch_shapes=[pltpu.VMEM((T, UB), F32)] * 3 + [pltpu.VMEM((T, SB), F32)] * 4 + [pltpu.VMEM((SEG, SB), F32)] * 8,
        compiler_params=_cp(("arbitrary",)), name=name)(u, dy, lam_re, lam_im, bre, bim, cre, cim)


class Exchange:
    def __init__(self, xs, modes):
        self.n = len(xs)
        self.modes = [modes] * self.n if isinstance(modes, (str, int)) else list(modes)
        self.out_shape = [jax.ShapeDtypeStruct(self._shape(x, md), x.dtype) for x, md in zip(xs, self.modes)]
        self.scratch = [pltpu.SemaphoreType.DMA((NDEV - 1, self.n)), pltpu.SemaphoreType.DMA((NDEV - 1, self.n)),
                        pltpu.SemaphoreType.DMA((self.n,))]
        self.specs = [pl.BlockSpec(memory_space=pl.ANY)] * self.n

    @staticmethod
    def _shape(x, mode):
        if mode == "gather":
            return (NDEV,) + tuple(x.shape)
        return tuple(x.shape) if mode == "lead" else (NDEV, x.shape[0], mode) + tuple(x.shape[2:])

    @staticmethod
    def _piece(x_ref, mode, dev):
        if mode == "gather":
            return x_ref
        return x_ref.at[dev] if mode == "lead" else x_ref.at[:, pl.ds(dev * mode, mode)]

    def _copies(self, x_refs, out_refs, sems):
        send_sems, recv_sems, local_sems = sems
        mx, my, mc = lax.axis_index("x"), lax.axis_index("y"), lax.axis_index("c")
        me = 4 * mx + 2 * my + mc
        peer_of = lambda k: (1 - mx if k & 4 else mx, 1 - my if k & 2 else my, 1 - mc if k & 1 else mc)
        local, first, relay, arrivals = [], [], [], []
        for a, (x_ref, out_ref) in enumerate(zip(x_refs, out_refs)):
            mode = self.modes[a]
            local.append(pltpu.make_async_copy(self._piece(x_ref, mode, me), out_ref.at[me], local_sems.at[a]))

            def remote(src, dst, k, pair, a=a):
                return pltpu.make_async_remote_copy(src_ref=src, dst_ref=dst, send_sem=send_sems.at[pair, a],
                                                    recv_sem=recv_sems.at[pair, a], device_id=peer_of(k), device_id_type=MESH_T)

            for k in range(1, NDEV):
                peer = peer_of(k)
                pid = 4 * peer[0] + 2 * peer[1] + peer[2]
                if mode != "gather":
                    src = self._piece(x_ref, mode, pid)
                    first.append(remote(src, out_ref.at[me], k, k - 1))
                    arrivals.append(remote(src, out_ref.at[pid], k, k - 1))
                elif k == 1:
                    first.append(remote(x_ref, out_ref.at[me], k, k - 1))
                    arrivals.append(remote(x_ref, out_ref.at[pid], k, k - 1))
                elif k % 2 == 0:
                    first.append(remote(x_ref, out_ref.at[me], k, k - 1))
                    relay.append((remote(x_ref, out_ref.at[pid], k, k - 1), remote(out_ref.at[pid], out_ref.at[pid], 1, k)))
                else:
                    arrivals.append(remote(x_ref, out_ref.at[pid], 1, k - 1))
        return local, first, relay, arrivals

    def start(self, x_refs, out_refs, sems):
        local, first, _, _ = self._copies(x_refs, out_refs, sems)
        for cp in local + first:
            cp.start()

    def finish(self, x_refs, out_refs, sems):
        local, first, relay, arrivals = self._copies(x_refs, out_refs, sems)
        for arrival, onward in relay:
            arrival.wait_recv()
            onward.start()
        for cp in arrivals:
            cp.wait_recv()
        for cp in first + [onward for _, onward in relay]:
            cp.wait_send()
        for cp in local:
            cp.wait()


def exchange(xs, modes, name):
    ex = Exchange(xs, modes)
    n = ex.n

    def body(*refs):
        ex.start(refs[:n], refs[n:2 * n], refs[2 * n:])
        ex.finish(refs[:n], refs[n:2 * n], refs[2 * n:])

    return pl.pallas_call(body, in_specs=ex.specs, out_specs=ex.specs, out_shape=ex.out_shape, scratch_shapes=ex.scratch,
                          compiler_params=pltpu.CompilerParams(has_side_effects=True), name=name)(*xs)


def _dot_f32(a, b, dn):
    return lax.dot_general(a, b, dn, preferred_element_type=F32, precision=lax.Precision.HIGHEST)


def ada_fwd(cg, c_ctx, ada_w, ada_b_loc, name):
    W = ada_w.shape[2]

    def body(cg_ref, cc_ref, w_ref, b_ref, o_ref):
        a = jnp.concatenate([_silu(cg_ref[...]), jnp.broadcast_to(_silu(cc_ref[...]), (NDEV, D))], axis=0)
        for i in range(2):
            o_ref[i] = _dot_f32(a, w_ref[i], _DN["nn"]) + b_ref[i]

    return pl.pallas_call(body, out_shape=jax.ShapeDtypeStruct((2, 2 * NDEV, W), F32),
                          compiler_params=_cp(), name=name)(cg, c_ctx, ada_w, ada_b_loc)


def ada_bwd(cg, c_ctx, ada_w, dm_loc, dm_all, name):
    W = ada_w.shape[2]

    def body(cg_ref, cc_ref, w_ref, dl_ref, da_ref, gw_ref, dcc_ref, gb_ref):
        a = jnp.concatenate([_silu(cg_ref[...]), jnp.broadcast_to(_silu(cc_ref[...]), (NDEV, D))], axis=0)
        dcc = jnp.zeros((1, D), F32)
        for i in range(2):
            dl = dl_ref[i]
            gw_ref[i] = _dot_f32(a, dl, _DN["tn"])
            dctx = jnp.sum(dl[NDEV:], axis=0, keepdims=True)
            dcc = dcc + _dot_f32(dctx, w_ref[i], _DN["nt"])
        dcc_ref[...] = dcc
        gb_ref[...] = jnp.sum(da_ref[...], axis=0)

    return pl.pallas_call(body, out_shape=[jax.ShapeDtypeStruct((2, D, W), F32), jax.ShapeDtypeStruct((1, D), F32),
                                           jax.ShapeDtypeStruct((2, 3 * D), F32)],
                          compiler_params=_cp(), name=name)(cg, c_ctx, ada_w, dm_loc, dm_all)


def cctx_finish(parts, c_ctx, name):
    def body(p_ref, cc_ref, o_ref):
        o_ref[...] = jnp.sum(p_ref[...], axis=0, keepdims=True) * _dsilu(cc_ref[...])

    return pl.pallas_call(body, out_shape=jax.ShapeDtypeStruct((1, D), F32), name=name)(parts, c_ctx)


def _adamw_update(g_ref, w_ref, m_ref, v_ref, go_ref, d_ref, mo_ref, vo_ref):
    g = g_ref[0].astype(F32)
    for s in range(1, g_ref.shape[0]):
        g = g + g_ref[s].astype(F32)
    mn = B1 * m_ref[...] + (1.0 - B1) * g
    vn = B2 * v_ref[...] + (1.0 - B2) * g * g
    go_ref[...] = g
    mo_ref[...] = mn
    vo_ref[...] = vn
    d_ref[...] = -LR * ((mn * (1.0 / (1.0 - B1 ** STEP))) / (jnp.sqrt(vn * (1.0 / (1.0 - B2 ** STEP))) + AEPS) + WD * w_ref[...])


def adamw(gstack, w, m, v, name, tr=256):
    n, R, C = gstack.shape
    tr = max(t for t in range(8, min(tr, R) + 1, 8) if R % t == 0)
    spec = pl.BlockSpec((tr, C), lambda i: (i, 0))
    return pl.pallas_call(_adamw_body(1), grid=(R // tr,),
                          in_specs=[pl.BlockSpec((n, tr, C), lambda i: (0, i, 0)), spec, spec, spec],
                          out_specs=[spec] * 4, out_shape=[jax.ShapeDtypeStruct((R, C), F32)] * 4,
                          compiler_params=_cp(("parallel",)), name=name)(gstack, w, m, v)


def _adamw_body(k):
    def body(*refs):
        for t in range(k):
            _adamw_update(*refs[4 * t:4 * t + 4], *refs[4 * k + 4 * t:4 * k + 4 * t + 4])
    return body


def adamw_multi(items, grid, name):
    k = len(items)
    ins, in_specs, out_specs, out_shape = [], [], [], []
    for g, g_spec, w, m, v, w_spec in items:
        ins += [g, w, m, v]
        in_specs += [g_spec, w_spec, w_spec, w_spec]
    for g, g_spec, w, m, v, w_spec in items:
        out_specs += [w_spec] * 4
        out_shape += [jax.ShapeDtypeStruct(w.shape, F32)] * 4
    res = pl.pallas_call(_adamw_body(k), grid=grid, in_specs=in_specs, out_specs=out_specs, out_shape=out_shape,
                         compiler_params=_cp(("arbitrary",) * len(grid)), name=name)(*ins)
    return [res[4 * t:4 * t + 4] for t in range(k)]


def _whole(a, grid_rank):
    zeros = (0,) * a.ndim
    return pl.BlockSpec(a.shape, lambda *idx: zeros)


def sum_slots(xs, name):
    def body(*refs):
        for x_ref, o_ref in zip(refs[:len(xs)], refs[len(xs):]):
            acc = x_ref[0]
            for s in range(1, NDEV):
                acc = acc + x_ref[s]
            o_ref[...] = acc

    return pl.pallas_call(body, out_shape=[jax.ShapeDtypeStruct(x.shape[1:], F32) for x in xs],
                          compiler_params=_cp(), name=name)(*xs)


def _col_shards(g):
    R, N = g.shape
    return g.reshape(R, NDEV, N // NDEV).transpose(1, 0, 2)


def _vec2(v):
    return jnp.broadcast_to(v.reshape(1, 1, -1), (2, 1, v.size))


SHARD_ROWS = {"mla_w_in": 192, "mla_w_uq": 192, "mla_w_ukv": 256, "s5_w_in": 256}


def _t_shard(wsh, rows):
    t = wsh[0].T.astype(BF16)
    return jnp.pad(t, ((0, rows - t.shape[0]), (0, 0)))


def _win_order():
    w = IN_W // NDEV
    perm = np.zeros((IN_WP, NDEV * SHARD_ROWS["mla_w_in"]), np.float32)
    first = QL + KVL + ROPE
    for c in range(IN_W):
        n = c + HEADS * VD if c < first else c - first
        perm[n, (c // w) * SHARD_ROWS["mla_w_in"] + c % w] = 1.0
    return jnp.asarray(perm, BF16)


def local_step(ctx, x, tgt, mod, Wt, small, l1_shards):
    T = LC + x.shape[0]
    xa = ("cat", ctx, x)
    sh = [mod[i, :, None, 0:D] for i in range(2)]
    sc = [mod[i, :, None, D:2 * D] for i in range(2)]
    gt = [mod[i, :, None, 2 * D:] for i in range(2)]
    ng = [_vec2(small["norm_g"][i]) for i in range(2)]
    qg, kvg = _vec2(small["mla_q_norm"]), _vec2(small["mla_kv_norm"])
    cosf, sinf, pm, pmt = _rope_tables(T)

    (h0, p0, cqn, ckvn), _ = rowwise(st_l0_pre, [xa], [ng[0], sc[0], sh[0], qg, kvg],
                                     [(D, BF16), (IN_WP, F32), (QL, BF16), (KVL, BF16)], [], "l0_pre", mats=[Wt["mla_w_in"]])
    z0, cq, ckv = (p0, 0, HEADS * VD), (p0, HEADS * VD // QL, QL), (p0, (HEADS * VD + QL) // KVL, KVL)
    Q = project_q(cqn, Wt["mla_w_uq"], cosf, sinf, pm, "l0_uq")
    K, V = project_kv(ckvn, Wt["mla_w_ukv"], p0, (HEADS * VD + QL + KVL) // 128, "l0_ukv")
    (o, lse), got = attn_fwd(Q, K, V, "l0_attn", rode=l1_shards, modes="gather")
    Wt, small = dict(Wt), dict(small)
    for n, a in zip(L1_BIG, got):
        Wt[n] = a.reshape(-1, a.shape[-1])
    vecs = lax.bitcast_convert_type(got[-1].reshape(NDEV, 2, -1, 2), F32)
    small["s5_d"], small["s5_b_glu"] = vecs[:, 0, :].reshape(D), vecs[:, 1, :].reshape(D)
    o2 = o.transpose(1, 0, 2).reshape(T, HEADS * VD)
    (og, out0, x1), _ = rowwise(st_l0_post, [o2, z0, xa], [gt[0]], [(D, BF16), (D, BF16), (D, F32)], [], "l0_post",
                                mats=[Wt["mla_w_out"]])

    ls = small["s5_log_step"].reshape(2, G, 1)
    a_re, a_im = small["s5_a_re"].reshape(2, G, P), small["s5_a_im"].reshape(2, G, P)
    b_re, b_im = small["s5_b_re"].reshape(2, G * P, CH), small["s5_b_im"].reshape(2, G * P, CH)
    lam_re, lam_im, f_re, f_im = disc_fwd(a_re, a_im, ls, "s5_disc")
    f_re2, f_im2 = f_re.reshape(2, G * P, 1), f_im.reshape(2, G * P, 1)
    bb_re, bb_im = disc_b(f_re2, f_im2, b_re, b_im, "s5_disc_b")
    compact = lambda m: m.reshape(2, NJ, UB, P)
    bre = compact(bb_re.reshape(2, G, P, CH).transpose(0, 1, 3, 2))
    bim = compact(bb_im.reshape(2, G, P, CH).transpose(0, 1, 3, 2))
    cre, cim = compact(small["s5_c_re"]), compact(small["s5_c_im"])
    lam_re4, lam_im4 = lam_re.reshape(2, NJ, 1, SB), lam_im.reshape(2, NJ, 1, SB)

    (h1, p1), _ = rowwise(st_l1_pre, [x1], [ng[1], sc[1], sh[1]], [(D, BF16), (2 * D, F32)], [], "l1_pre", mats=[Wt["s5_w_in"]])
    u, z1 = (p1, 0, D), (p1, 1, D)
    yssm = scan_fwd(p1, lam_re4, lam_im4, bre, bim, cre, cim, "s5_scan")
    dvec, bglu = _vec2(small["s5_d"]), _vec2(small["s5_b_glu"])
    fg = _vec2(small["final_g"])
    lat_mask = jnp.stack([jnp.zeros((1, D), F32), jnp.ones((1, D), F32)])
    (y, y1b, gl, y3, out1, dx2), (dfg, lvec) = rowwise(
        st_l1_mlp, [yssm, u, z1, x1, ("lat", tgt)], [dvec, bglu, gt[1], fg, lat_mask],
        [(D, F32), (D, BF16), (D, BF16), (D, BF16), (D, BF16), (D, F32)], [D, 128], "l1_mlp",
        mats=[Wt["s5_w_glu"], Wt["s5_w_out"]])

    (dz1, dy, du_d), (dgt1, dbglu, dd), (g_w_out5, g_w_glu) = rowwise(
        st_l1_mlp_bwd, [dx2, out1, y3, y, gl, z1, u, y1b], [gt[1], bglu, dvec], [(D, BF16), (D, F32), (D, F32)], [D, D, D],
        "l1_mlp_b", mats=[Wt["s5_w_out"], Wt["s5_w_glu"]], out_accs=[(D, D), (D, D)])
    du_s, dlr, dli, dbre, dbim, dcre, dcim = scan_bwd(p1, dy, lam_re4, lam_im4, bre, bim, cre, cim, "s5_scan_b")
    dbb_re = dbre.reshape(2, G, CH, P).transpose(0, 1, 3, 2).reshape(2, G * P, CH)
    dbb_im = dbim.reshape(2, G, CH, P).transpose(0, 1, 3, 2).reshape(2, G * P, CH)
    g_c_re, g_c_im = dcre.reshape(2, G, CH, P), dcim.reshape(2, G, CH, P)
    g_b_re, g_b_im, dfr, dfi = disc_b_bwd(f_re2, f_im2, b_re, b_im, dbb_re, dbb_im, "s5_disc_b_b")
    g_a_re, g_a_im, g_ls = disc_a_bwd(a_re, a_im, ls, dlr.reshape(2, G, P), dli.reshape(2, G, P),
                                      dfr.reshape(2, G, P), dfi.reshape(2, G, P), "s5_disc_b_a")
    (dx1,), (dsh1, dsc1, dng1), (g_w_in5,) = rowwise(
        st_l1_tail_bwd, [du_d, du_s, dz1, h1, x1, dx2], [ng[1], sc[1]], [(D, F32)], [D, D, D], "l1_pre_b",
        mats=[Wt["s5_w_in"]], out_accs=[(D, 2 * D)])
    g_w_in5 = _col_shards(g_w_in5)

    (do2, dz0), (dgt0,), (g_w_out,) = rowwise(st_l0_post_bwd, [dx1, out0, og, o2, z0], [gt[0]], [(D, F32), (D, F32)], [D],
                                              "l0_post_b", mats=[Wt["mla_w_out"]], out_accs=[(D, D)])
    doh = do2.reshape(T, HEADS, VD).transpose(1, 0, 2)
    rows8 = lambda g: g.reshape(NDEV, -1, g.shape[-1])
    both = lambda s: s[0, 0] + s[1, 0]
    dense = lambda g: g.reshape(2, G * P * CH // 128, 128)
    chunks = [dense(g_b_re), dense(g_b_im), g_c_re, g_c_im]
    l1_send = [g_w_in5, rows8(g_w_glu), rows8(g_w_out5), rows8(g_w_out),
               both(dd).reshape(NDEV, 1, -1), both(dbglu).reshape(NDEV, 1, -1)]
    (dQ, dK, dV), l1_recv = attn_bwd(Q, K, V, o, lse, doh, "l0_attn_b", rode=l1_send + chunks,
                                     modes=["lead"] * len(l1_send) + [a.shape[1] // NDEV for a in chunks])
    dqh = rope(dQ, cosf, sinf, pmt, True, BF16, "l0_rope_q_b", scale=SCALE)
    dq = dqh.transpose(1, 0, 2).reshape(T, HEADS * QK)
    dkv, dkr = split_kv_grads(dK, dV, "l0_kv_b")
    (grad_x,), (dqg, dkvg, dsh0, dsc0, dng0), (g_uq, g_ukv, g_p) = rowwise(
        st_l0_tail_bwd, [dq, dkv, dkr, dz0, cq, ckv, cqn, ckvn, h0, xa, dx1], [qg, kvg, ng[0], sc[0]],
        [(D, F32, "lat")], [QL, KVL, D, D, D], "l0_pre_b", mats=[Wt["mla_w_uq"], Wt["mla_w_ukv"], Wt["mla_w_in"]],
        out_accs=[(QL, HEADS * QK), (KVL, HEADS * KVW), (D, IN_WP)])
    g_w_uq, g_w_ukv = _col_shards(g_uq).astype(BF16), _col_shards(g_ukv).astype(BF16)
    g_w_in = _col_shards(jnp.concatenate([g_p[:, HEADS * VD:IN_W], g_p[:, :HEADS * VD]], axis=1)).astype(BF16)

    dmod = jnp.stack([jnp.concatenate([dsh0, dsc0, dgt0], axis=-1)[:, 0], jnp.concatenate([dsh1, dsc1, dgt1], axis=-1)[:, 0]])
    gbig = {"mla_w_in": g_w_in, "mla_w_uq": g_w_uq, "mla_w_ukv": g_w_ukv}
    gsmall = {"norm_g": jnp.stack([both(dng0), both(dng1)]), "mla_q_norm": both(dqg), "mla_kv_norm": both(dkvg),
              "s5_a_re": g_a_re, "s5_a_im": g_a_im, "s5_log_step": g_ls, "final_g": dfg[1, 0]}
    return lvec[1], grad_x, dmod, gbig, gsmall, l1_recv


COL_SHARDED = ("mla_w_in", "mla_w_uq", "mla_w_ukv", "s5_w_in")
ROW_SHARDED = ("mla_w_out", "s5_w_glu", "s5_w_out")
VEC_SHARDED = ("s5_d", "s5_b_glu")
BIG = COL_SHARDED + ROW_SHARDED
L0_BIG = ("mla_w_in", "mla_w_uq", "mla_w_ukv")
L1_BIG = ("s5_w_in", "s5_w_glu", "s5_w_out", "mla_w_out")
BITS16 = jnp.bfloat16
SMALL_RS = ("norm_g", "mla_q_norm", "mla_kv_norm", "s5_a_re", "s5_a_im", "s5_log_step", "s5_b_re", "s5_b_im",
            "s5_c_re", "s5_c_im", "final_g")
CHUNKED = ("s5_b_re", "s5_b_im", "s5_c_re", "s5_c_im")
DENSE = ("s5_b_re", "s5_b_im")
TINY = ("norm_g", "mla_q_norm", "mla_kv_norm", "s5_a_re", "s5_a_im", "s5_log_step", "final_g")
ORDER = ("c_ctx", "ada_w", "ada_b", "norm_g", "mla_w_in", "mla_q_norm", "mla_w_uq", "mla_kv_norm", "mla_w_ukv",
         "mla_w_out", "s5_w_in", "s5_a_re", "s5_a_im", "s5_log_step", "s5_b_re", "s5_b_im", "s5_c_re", "s5_c_im",
         "s5_d", "s5_w_glu", "s5_b_glu", "s5_w_out", "final_g")


def kernel(x, c, ctx, c_ctx, ada_w, ada_b, norm_g, mla_w_in, mla_q_norm, mla_w_uq, mla_kv_norm, mla_w_ukv, mla_w_out, s5_w_in, s5_a_re, s5_a_im, s5_log_step, s5_b_re, s5_b_im, s5_c_re, s5_c_im, s5_d, s5_w_glu, s5_b_glu, s5_w_out, final_g, loss_target, m_c_ctx, m_ada_w, m_ada_b, m_norm_g, m_mla_w_in, m_mla_q_norm, m_mla_w_uq, m_mla_kv_norm, m_mla_w_ukv, m_mla_w_out, m_s5_w_in, m_s5_a_re, m_s5_a_im, m_s5_log_step, m_s5_b_re, m_s5_b_im, m_s5_c_re, m_s5_c_im, m_s5_d, m_s5_w_glu, m_s5_b_glu, m_s5_w_out, m_final_g, v_c_ctx, v_ada_w, v_ada_b, v_norm_g, v_mla_w_in, v_mla_q_norm, v_mla_w_uq, v_mla_kv_norm, v_mla_w_ukv, v_mla_w_out, v_s5_w_in, v_s5_a_re, v_s5_a_im, v_s5_log_step, v_s5_b_re, v_s5_b_im, v_s5_c_re, v_s5_c_im, v_s5_d, v_s5_w_glu, v_s5_b_glu, v_s5_w_out, v_final_g):
    w = dict(c_ctx=c_ctx, ada_w=ada_w, ada_b=ada_b, norm_g=norm_g, mla_w_in=mla_w_in, mla_q_norm=mla_q_norm,
             mla_w_uq=mla_w_uq, mla_kv_norm=mla_kv_norm, mla_w_ukv=mla_w_ukv, mla_w_out=mla_w_out, s5_w_in=s5_w_in,
             s5_a_re=s5_a_re, s5_a_im=s5_a_im, s5_log_step=s5_log_step, s5_b_re=s5_b_re, s5_b_im=s5_b_im,
             s5_c_re=s5_c_re, s5_c_im=s5_c_im, s5_d=s5_d, s5_w_glu=s5_w_glu, s5_b_glu=s5_b_glu, s5_w_out=s5_w_out,
             final_g=final_g)
    m = dict(c_ctx=m_c_ctx, ada_w=m_ada_w, ada_b=m_ada_b, norm_g=m_norm_g, mla_w_in=m_mla_w_in, mla_q_norm=m_mla_q_norm,
             mla_w_uq=m_mla_w_uq, mla_kv_norm=m_mla_kv_norm, mla_w_ukv=m_mla_w_ukv, mla_w_out=m_mla_w_out,
             s5_w_in=m_s5_w_in, s5_a_re=m_s5_a_re, s5_a_im=m_s5_a_im, s5_log_step=m_s5_log_step, s5_b_re=m_s5_b_re,
             s5_b_im=m_s5_b_im, s5_c_re=m_s5_c_re, s5_c_im=m_s5_c_im, s5_d=m_s5_d, s5_w_glu=m_s5_w_glu,
             s5_b_glu=m_s5_b_glu, s5_w_out=m_s5_w_out, final_g=m_final_g)
    v = dict(c_ctx=v_c_ctx, ada_w=v_ada_w, ada_b=v_ada_b, norm_g=v_norm_g, mla_w_in=v_mla_w_in, mla_q_norm=v_mla_q_norm,
             mla_w_uq=v_mla_w_uq, mla_kv_norm=v_mla_kv_norm, mla_w_ukv=v_mla_w_ukv, mla_w_out=v_mla_w_out,
             s5_w_in=v_s5_w_in, s5_a_re=v_s5_a_re, s5_a_im=v_s5_a_im, s5_log_step=v_s5_log_step, s5_b_re=v_s5_b_re,
             s5_b_im=v_s5_b_im, s5_c_re=v_s5_c_re, s5_c_im=v_s5_c_im, s5_d=v_s5_d, s5_w_glu=v_s5_w_glu,
             s5_b_glu=v_s5_b_glu, s5_w_out=v_s5_w_out, final_g=v_final_g)

    me = 4 * lax.axis_index("x") + 2 * lax.axis_index("y") + lax.axis_index("c")
    WA = ada_w.shape[2]

    def shard(n):
        return _t_shard(w[n], SHARD_ROWS[n]) if n in COL_SHARDED else w[n][0].astype(BF16)

    wgot = exchange([c] + [shard(n) for n in L0_BIG], "gather", "gather_w")

    cg = wgot[0].reshape(NDEV, D)
    cc2 = c_ctx.reshape(1, D)
    ada_b_loc = lax.dynamic_slice_in_dim(ada_b.reshape(2, 3 * D // WA, WA), me, 1, axis=1)
    part = ada_fwd(cg, cc2, ada_w, ada_b_loc, "ada_fwd")
    pg = exchange([part], "gather", "gather_mod")[0]
    mod_l = lax.dynamic_index_in_dim(pg, me, axis=2, keepdims=False).transpose(1, 0, 2).reshape(2, 3 * D)
    mod_c = pg[:, :, NDEV, :].transpose(1, 0, 2).reshape(2, 3 * D)
    mod = jnp.stack([mod_c, mod_l], axis=1)

    Wt = {n: a.reshape(-1, a.shape[-1]) for n, a in zip(L0_BIG, wgot[1:])}
    Wt["mla_w_in"] = mm(_win_order(), Wt["mla_w_in"], "nn", "w_in_order", out_dtype=BF16)
    vec_bits = lax.bitcast_convert_type(jnp.concatenate([s5_d, s5_b_glu], axis=0), BITS16).reshape(2, -1)
    small = {n: w[n] for n in SMALL_RS}

    lvec, grad_x, dmod, gbig, gsmall, l1_recv = local_step(ctx[0], x[0], loss_target[0], mod, Wt, small,
                                                           [shard(n) for n in L1_BIG] + [vec_bits])
    grad_x = grad_x[None]

    per_dev = G // NDEV
    recv = dict(zip(L0_BIG, exchange([gbig[n] for n in L0_BIG], "lead", "scatter_grads")))
    recv.update(dict(zip(L1_BIG + VEC_SHARDED, l1_recv)))
    out = {}

    def keep(n, res):
        for key, arr in zip("gdmv", res):
            out[key, n] = arr.reshape(w[n].shape)

    for n in BIG:
        keep(n, adamw(recv[n], w[n][0], m[n][0], v[n][0], "adamw_" + n))
    reduced = sum_slots(l1_recv[len(L1_BIG + VEC_SHARDED):], "sum_chunks")

    kshape = lambda n: w[n].shape if w[n].ndim > 1 else (1, w[n].size)
    flat = jnp.concatenate([gsmall[n].reshape(-1) for n in TINY] + [dmod.reshape(-1), lvec.reshape(-1)])[None]
    bb_all, cc_all, flat_all = exchange([jnp.stack(reduced[:2]), jnp.stack(reduced[2:]), flat], "gather", "gather_small")
    chunk_all = [bb_all[:, 0], bb_all[:, 1], cc_all[:, 0], cc_all[:, 1]]
    tiny_all, off = [], 0
    for n in TINY:
        tiny_all.append(flat_all[:, 0, off:off + w[n].size].reshape((NDEV,) + kshape(n)))
        off += w[n].size
    dm_all = flat_all[:, 0, off:off + dmod.size].reshape((NDEV,) + dmod.shape)
    loss = sum_slots([flat_all[:, :, off + dmod.size:]], "loss_sum")[0][0, 0]

    dm_cols = lax.dynamic_slice_in_dim(dm_all.reshape(NDEV, 2, 2, 3 * D // WA, WA), me, 1, axis=3)[:, :, :, 0]
    dm_loc = jnp.concatenate([dm_cols[:, :, 1].transpose(1, 0, 2), dm_cols[:, :, 0].transpose(1, 0, 2)], axis=1)
    g_ada_w, dcc_part, g_ada_b = ada_bwd(cg, cc2, ada_w, dm_loc, dm_all.transpose(0, 2, 1, 3).reshape(2 * NDEV, 2, 3 * D), "ada_bwd")
    dcc_all = exchange([dcc_part], "gather", "gather_dcc")[0].reshape(NDEV, D)
    g_c_ctx = cctx_finish(dcc_all, cc2, "cctx_finish")

    flat2 = lambda t: t.reshape(-1, t.shape[-1])
    keep("ada_w", adamw(flat2(g_ada_w)[None], flat2(ada_w), flat2(m_ada_w), flat2(v_ada_w), "adamw_ada"))
    items = []
    halves = 2
    for n, g in zip(CHUNKED, chunk_all):
        blk = (1, 1, G // halves) + w[n].shape[3:]
        g = jnp.moveaxis(g, 0, 1).reshape(w[n].shape)
        g_spec = pl.BlockSpec((1,) + blk, lambda d, s: (0, 0, d, s, 0, 0))
        items.append((g[None], g_spec, w[n], m[n], v[n], pl.BlockSpec(blk, lambda d, s: (0, d, s, 0, 0))))
    for n, res in zip(CHUNKED, adamw_multi(items, (2, halves), "adamw_bc")):
        keep(n, res)
    tiny_g = dict(zip(TINY, tiny_all))
    tiny_g.update({n: recv[n] for n in VEC_SHARDED})
    tiny_g["c_ctx"], tiny_g["ada_b"] = g_c_ctx[None], g_ada_b[None]
    names = list(tiny_g)
    items = [(tiny_g[n], _whole(tiny_g[n], 1)) + tuple(t[n].reshape(kshape(n)) for t in (w, m, v))
             + (pl.BlockSpec(kshape(n), lambda i, r=len(kshape(n)): (0,) * r),) for n in names]
    for n, res in zip(names, adamw_multi(items, (1,), "adamw_small")):
        keep(n, res)

    return (loss, grad_x, *[out["g", n] for n in ORDER], *[out["d", n] for n in ORDER],
            *[out["m", n] for n in ORDER], *[out["v", n] for n in ORDER])
```

```python
import math

import numpy as np
import jax
import jax.numpy as jnp
from jax import lax
from jax.experimental import pallas as pl
from jax.experimental.pallas import tpu as pltpu

F32 = jnp.float32
BF16 = jnp.bfloat16

D = 1024
L = 2048
LC = 256
NDEV = 8
GRID_W = 64
EPS = 1e-6
HEADS = 16
NOPE = 64
ROPE = 32
QK = NOPE + ROPE
VD = 64
IN_W = 256 + 128 + ROPE + HEADS * 64
IN_WP = 1536
QL = 256
KVL = 128
SCALE = QK ** -0.5
LOG2E = math.log2(math.e)
THETA = 10000.0
G = 64
P = 64
CH = 16
GB = 8
NJ = G // GB
UB = GB * CH
SB = GB * P
SEG = 16
TB = 256
VMEM_LIMIT = 56 * 1024 * 1024
B1, B2, LR, AEPS, WD, STEP = 0.9, 0.999, 0.001, 1e-8, 0.01, 10
MESH_T = pl.DeviceIdType.MESH


def _cp(sem=None):
    return pltpu.CompilerParams(dimension_semantics=sem, vmem_limit_bytes=VMEM_LIMIT)


def _sig(x):
    return 1.0 / (1.0 + jnp.exp(-x))


def _silu(x):
    return x * _sig(x)


def _dsilu(x):
    s = _sig(x)
    return s * (1.0 + x * (1.0 - s))


_GK = math.sqrt(2.0 / math.pi)


def _gelu(x):
    return 0.5 * x * (1.0 + jnp.tanh(_GK * (x + 0.044715 * x * x * x)))


def _dgelu(x):
    t = jnp.tanh(_GK * (x + 0.044715 * x * x * x))
    return 0.5 * (1.0 + t) + 0.5 * x * (1.0 - t * t) * _GK * (1.0 + 3 * 0.044715 * x * x)


def _rs(x):
    return lax.rsqrt(jnp.mean(x * x, axis=-1, keepdims=True) + EPS)


def _sum0(x):
    return jnp.sum(x, axis=0, keepdims=True)


def st_norm_mod(x, g, sc, sh):
    y = x * _rs(x) * g
    return (y * (1.0 + sc) + sh,), ()


def st_norm_mod_bwd(x, dh, dres, g, sc):
    r = _rs(x)
    xn = x * r
    y = xn * g
    dy = dh * (1.0 + sc)
    dxn = dy * g
    dx = r * (dxn - xn * jnp.mean(dxn * xn, axis=-1, keepdims=True))
    return (dres + dx,), (_sum0(dh), _sum0(dh * y), _sum0(dy * xn))


def st_rms(x, g):
    return (x * _rs(x) * g,), ()


def st_rms_bwd(x, dy, g):
    r = _rs(x)
    n = x * r
    dn = dy * g
    dx = r * (dn - n * jnp.mean(dn * n, axis=-1, keepdims=True))
    return (dx,), (_sum0(dy * n),)


def st_rms2(x1, x2, g1, g2):
    return st_rms(x1, g1)[0] + st_rms(x2, g2)[0], ()


def st_rms2_bwd(x1, dy1, x2, dy2, g1, g2):
    (d1,), (s1,) = st_rms_bwd(x1, dy1, g1)
    (d2,), (s2,) = st_rms_bwd(x2, dy2, g2)
    return (d1, d2), (s1, s2)


def st_gate(o, z):
    return (o * _silu(z),), ()


def st_gate_bwd(dog, o, z):
    return (dog * _silu(z), dog * o * _dsilu(z)), ()


def st_resid(x, out, gt):
    return (x + gt * out,), ()


def st_resid_bwd(dx, out, gt):
    return (dx * gt,), (_sum0(dx * out),)


def st_s5a(yssm, u, d):
    y = yssm + d * u
    return (y, _gelu(y)), ()


def st_s5b(y, gl, z, b):
    return (_gelu(y) * _sig(gl + b) * _silu(z),), ()


def st_s5b_bwd(dy3, y, gl, z, b):
    y1 = _gelu(y)
    s = _sig(gl + b)
    dy2 = dy3 * _silu(z)
    dz = dy3 * y1 * s * _dsilu(z)
    dgl = dy2 * y1 * s * (1.0 - s)
    return (dgl, dz, dy2 * s), (_sum0(dgl),)


def st_s5a_bwd(dy1a, dy1b, y, u, d):
    dy = (dy1a + dy1b) * _dgelu(y)
    return (dy, dy * d), (_sum0(dy * u),)


def st_l0_pre(x, g, sc, sh, qg, kvg, w_in):
    hb = st_norm_mod(x, g, sc, sh)[0][0].astype(BF16)
    p = lax.dot_general(hb, w_in, _DN["nt"], preferred_element_type=F32)
    cq, ckv = p[:, HEADS * VD:HEADS * VD + QL], p[:, HEADS * VD + QL:HEADS * VD + QL + KVL]
    return (hb, p) + st_rms2(cq, ckv, qg, kvg)[0], ()


def st_l0_tail_bwd(dq, dkv, dkr, dz, cq, ckv, cqn, ckvn, h, x, dres, qg, kvg, g, sc, w_uq, w_ukv, w_in):
    dcqn = jnp.dot(dq, w_uq, preferred_element_type=F32)
    dckvn = jnp.dot(dkv, w_ukv, preferred_element_type=F32)
    (dcq, dckv), (dqg, dkvg) = st_rms2_bwd(cq, dcqn, ckv, dckvn, qg, kvg)
    dp = jnp.concatenate([dz, dcq, dckv, dkr], axis=1).astype(BF16)
    dh = jnp.dot(dp, w_in, preferred_element_type=F32)
    outs, sums = st_norm_mod_bwd(x, dh, dres, g, sc)
    tn = lambda a, b: lax.dot_general(a, b, _DN["tn"], preferred_element_type=F32)
    return outs, (dqg, dkvg) + sums, (tn(cqn, dq), tn(ckvn, dkv), tn(h, dp))


def st_l1_pre(x, g, sc, sh, w_in):
    hb = st_norm_mod(x, g, sc, sh)[0][0].astype(BF16)
    return (hb, lax.dot_general(hb, w_in, _DN["nt"], preferred_element_type=F32)), ()


def st_l1_tail_bwd(du_a, du_b, dz, h, x, dres, g, sc, w_in):
    dp = jnp.concatenate([(du_a + du_b).astype(BF16), dz], axis=1)
    dh = jnp.dot(dp, w_in, preferred_element_type=F32)
    outs, sums = st_norm_mod_bwd(x, dh, dres, g, sc)
    return outs, sums, (lax.dot_general(h, dp, _DN["tn"], preferred_element_type=F32),)


def st_l0_post(o, z, x, gt, w_out):
    og = (o * _silu(z)).astype(BF16)
    out = jnp.dot(og, w_out, preferred_element_type=F32)
    return (og, out, x + gt * out), ()


def st_l0_post_bwd(dx1, out, og, o, z, gt, w_out):
    (dout,), (dgt,) = st_resid_bwd(dx1, out.astype(F32), gt)
    doutb = dout.astype(BF16)
    dog = lax.dot_general(doutb, w_out, _DN["nt"], preferred_element_type=F32)
    return st_gate_bwd(dog, o, z)[0], (dgt,), (lax.dot_general(og, doutb, _DN["tn"], preferred_element_type=F32),)


def st_l1_mlp(yssm, u, z, x1, tgt, d, bglu, gt, fg, mask, w_glu, w_out):
    (y, y1), _ = st_s5a(yssm, u, d)
    y1b = y1.astype(BF16)
    gl = jnp.dot(y1b, w_glu, preferred_element_type=F32)
    y3 = (y1 * _sig(gl + bglu) * _silu(z)).astype(BF16)
    out = jnp.dot(y3, w_out, preferred_element_type=F32)
    (dx2,), sums = st_final(x1 + gt * out, tgt, fg, mask)
    return (y, y1b, gl, y3, out, dx2), sums


def st_l1_mlp_bwd(dx2, out, y3, y, gl, z, u, y1b, gt, bglu, d, w_out, w_glu):
    out, gl = out.astype(F32), gl.astype(F32)
    (dout,), (dgt,) = st_resid_bwd(dx2, out, gt)
    doutb = dout.astype(BF16)
    dy3 = lax.dot_general(doutb, w_out, _DN["nt"], preferred_element_type=F32)
    (dgl, dz, dy1a), (dbglu,) = st_s5b_bwd(dy3, y, gl, z, bglu)
    dglb = dgl.astype(BF16)
    dy1b = lax.dot_general(dglb, w_glu, _DN["nt"], preferred_element_type=F32)
    (dy, du), (dd,) = st_s5a_bwd(dy1a, dy1b, y, u, d)
    g_w_out = lax.dot_general(y3, doutb, _DN["tn"], preferred_element_type=F32)
    g_w_glu = lax.dot_general(y1b, dglb, _DN["tn"], preferred_element_type=F32)
    return (dz, dy, du), (dgt, dbglu, dd), (g_w_out, g_w_glu)


def st_final(x2, tgt, g, mask):
    r = _rs(x2)
    n = x2 * r
    e = n * g - tgt
    dyo = e * (1.0 / D)
    dn = dyo * g
    dx = r * (dn - n * jnp.mean(dn * n, axis=-1, keepdims=True))
    lsum = jnp.sum(_sum0(e * e), axis=1, keepdims=True) * (0.5 / D)
    return (dx * mask,), (_sum0(dyo * n), jnp.broadcast_to(lsum, (1, 128)))


def rowwise(fn, rows, vecs, out_rows, out_sums, name, mats=(), out_accs=()):
    lat_blk = lambda i: jnp.maximum(i - 1, 0)
    arrays, in_specs, pick = [], [], []
    for a in rows:
        if not isinstance(a, tuple):
            a = (a, 0, a.shape[1])
        tag = a[0] if isinstance(a[0], str) else None
        if tag == "cat":
            _, ctx, x = a
            arrays += [ctx, x]
            in_specs += [pl.BlockSpec((TB, ctx.shape[1]), lambda i: (0, 0)),
                         pl.BlockSpec((TB, x.shape[1]), lambda i: (lat_blk(i), 0))]
            pick.append(2)
        elif tag == "lat":
            arrays.append(a[1])
            in_specs.append(pl.BlockSpec((TB, a[1].shape[1]), lambda i: (lat_blk(i), 0)))
            pick.append(1)
        else:
            arr, cb, width = a
            arrays.append(arr)
            in_specs.append(pl.BlockSpec((TB, width), lambda i, cb=cb: (i, cb)))
            pick.append(1)
    T = LC + L
    nin, nv, nm, no, ns = len(arrays), len(vecs), len(mats), len(out_rows), len(out_sums)

    def body(*refs):
        i = pl.program_id(0)
        vals, k = [], 0
        for p in pick:
            if p == 2:
                vals.append(jnp.where(i == 0, refs[k][...], refs[k + 1][...]))
            else:
                vals.append(refs[k][...])
            k += p
        vals += [r[0] for r in refs[nin:nin + nv]] + [r[...] for r in refs[nin + nv:nin + nv + nm]]
        res = fn(*vals)
        first_out = nin + nv + nm
        for r, o in zip(refs[first_out:first_out + no], res[0]):
            r[...] = o.astype(r.dtype)
        sum_refs = refs[first_out + no:first_out + no + ns]
        if sum_refs:
            @pl.when(i <= 1)
            def _():
                for r in sum_refs:
                    r[...] = jnp.zeros_like(r)
            for r, s in zip(sum_refs, res[1]):
                r[0] += s
        acc_refs = refs[first_out + no + ns:]
        if acc_refs:
            @pl.when(i == 0)
            def _():
                for r in acc_refs:
                    r[...] = jnp.zeros_like(r)
            for r, a in zip(acc_refs, res[2]):
                r[...] += a

    kind = lambda i: (jnp.minimum(i, 1), 0, 0)
    in_specs += [pl.BlockSpec((1, 1, v.shape[2]), kind) for v in vecs]
    in_specs += [pl.BlockSpec(m.shape, lambda i: (0, 0), pipeline_mode=pl.Buffered(1)) for m in mats]
    out_specs, out_shape = [], []
    for o in out_rows:
        lat = len(o) == 3
        out_specs.append(pl.BlockSpec((TB, o[0]), (lambda i: (lat_blk(i), 0)) if lat else (lambda i: (i, 0))))
        out_shape.append(jax.ShapeDtypeStruct((L if lat else T, o[0]), o[1]))
    out_specs += [pl.BlockSpec((1, 1, c), kind) for c in out_sums]
    out_shape += [jax.ShapeDtypeStruct((2, 1, c), F32) for c in out_sums]
    out_specs += [pl.BlockSpec(s, lambda i: (0, 0)) for s in out_accs]
    out_shape += [jax.ShapeDtypeStruct(s, F32) for s in out_accs]
    res = pl.pallas_call(body, grid=(T // TB,), in_specs=in_specs, out_specs=out_specs, out_shape=out_shape,
                         compiler_params=_cp(("arbitrary",)), name=name)(*arrays, *vecs, *mats)
    if out_accs:
        return res[:no], res[no:no + ns], res[no + ns:]
    return res[:no], res[no:]


_DN = {"nn": (((1,), (0,)), ((), ())), "nt": (((1,), (1,)), ((), ())), "tn": (((0,), (0,)), ((), ()))}


def mm(a, b, mode, name, out_dtype=F32, tm=None, tn=None, shard_out=False):
    if mode == "nn":
        (M, K), (_, N) = a.shape, b.shape
    elif mode == "nt":
        (M, K), (N, _) = a.shape, b.shape
    else:
        (K, M), (_, N) = a.shape, b.shape
    if tm is None:
        tm = next((t for t in (768, 512, 256) if M % t == 0 and M > t), M)
    tn = N if tn is None else tn
    dn = _DN[mode]

    def body(a_ref, b_ref, o_ref):
        o_ref[...] = lax.dot_general(a_ref[...].astype(BF16), b_ref[...].astype(BF16), dn,
                                     preferred_element_type=F32).astype(o_ref.dtype)

    if shard_out:
        def body(a_ref, b_ref, o_ref):
            av = a_ref[...].astype(BF16)
            for j in range(N // tn):
                bj = b_ref[pl.ds(j * tn, tn), :] if mode == "nt" else b_ref[:, pl.ds(j * tn, tn)]
                o_ref[j] = lax.dot_general(av, bj.astype(BF16), dn, preferred_element_type=F32).astype(o_ref.dtype)

        a_spec = pl.BlockSpec((K, tm), lambda i: (0, i)) if mode == "tn" else pl.BlockSpec((tm, K), lambda i: (i, 0))
        return pl.pallas_call(body, grid=(M // tm,), in_specs=[a_spec, pl.BlockSpec(b.shape, lambda i: (0, 0))],
                              out_specs=pl.BlockSpec((N // tn, tm, tn), lambda i: (0, i, 0)),
                              out_shape=jax.ShapeDtypeStruct((N // tn, M, tn), out_dtype),
                              compiler_params=_cp(("parallel",)), name=name)(a, b)
    a_spec = pl.BlockSpec((K, tm), lambda i, j: (0, i)) if mode == "tn" else pl.BlockSpec((tm, K), lambda i, j: (i, 0))
    b_spec = pl.BlockSpec((tn, K), lambda i, j: (j, 0)) if mode == "nt" else pl.BlockSpec((K, tn), lambda i, j: (0, j))
    return pl.pallas_call(body, grid=(M // tm, N // tn), in_specs=[a_spec, b_spec],
                          out_specs=pl.BlockSpec((tm, tn), lambda i, j: (i, j)), out_shape=jax.ShapeDtypeStruct((M, N), out_dtype),
                          compiler_params=_cp(("parallel", "arbitrary")), name=name)(a, b)


def _rope_tables(T, width=QK, first=NOPE):
    nlat = T - LC
    pos = np.arange(nlat)
    row, col = pos // GRID_W, pos % GRID_W
    half = ROPE // 2
    inv = 1.0 / (THETA ** (np.arange(0, half, 2, dtype=np.float64) / half))
    cosf = np.ones((T, width), np.float64)
    sinf = np.zeros((T, width), np.float64)
    perm = np.zeros((width, width), np.float32)
    for m in range(ROPE):
        j = first + m
        blk, w = m // half, m % half
        ang = (row if blk == 0 else col)[:, None] * inv[None, :]
        f = w % (half // 2)
        cosf[LC:, j] = np.cos(ang[:, f])
        if w < half // 2:
            sinf[LC:, j] = -np.sin(ang[:, f])
            perm[j + half // 2, j] = 1.0
        else:
            sinf[LC:, j] = np.sin(ang[:, f])
            perm[j - half // 2, j] = 1.0
    return jnp.asarray(cosf, F32), jnp.asarray(sinf, F32), jnp.asarray(perm, BF16), jnp.asarray(perm.T, BF16)


def _exact_perm(x, pm):
    hi = x.astype(BF16)
    r1 = x - hi.astype(F32)
    mid = r1.astype(BF16)
    lo = (r1 - mid.astype(F32)).astype(BF16)
    dot = lambda a: jnp.dot(a, pm, preferred_element_type=F32)
    return dot(hi) + dot(mid) + dot(lo)


def _rot(x, cv, sv, pv, inverse):
    if inverse:
        return x * cv + _exact_perm(x * sv, pv)
    return x * cv + _exact_perm(x, pv) * sv


def rope(x, cosf, sinf, pm, inverse, out_dtype, name, scale=1.0):
    H, T, _ = x.shape

    def body(x_ref, c_ref, s_ref, p_ref, o_ref):
        cv, sv, pv = c_ref[...], s_ref[...], p_ref[...]
        for h in range(H):
            o_ref[h] = (_rot(x_ref[h], cv, sv, pv, inverse) * scale).astype(o_ref.dtype)

    return pl.pallas_call(
        body, grid=(T // TB,),
        in_specs=[pl.BlockSpec((H, TB, QK), lambda i: (0, i, 0)), pl.BlockSpec((TB, QK), lambda i: (i, 0)),
                  pl.BlockSpec((TB, QK), lambda i: (i, 0)), pl.BlockSpec((QK, QK), lambda i: (0, 0))],
        out_specs=pl.BlockSpec((H, TB, QK), lambda i: (0, i, 0)), out_shape=jax.ShapeDtypeStruct((H, T, QK), out_dtype),
        compiler_params=_cp(("parallel",)), name=name)(x, cosf, sinf, pm)


KVW = NOPE + VD


def _kv_selectors():
    s_kn = np.zeros((KVW, QK), np.float32)
    s_kr = np.zeros((128, QK), np.float32)
    s_v = np.zeros((KVW, VD), np.float32)
    for l in range(NOPE):
        s_kn[l, l] = 1.0
    for l in range(ROPE):
        s_kr[l, NOPE + l] = 1.0
    for l in range(VD):
        s_v[NOPE + l, l] = 1.0
    return s_kn, s_kr, s_v


def project_q(cqn, w, cosf, sinf, pm, name):
    T = cqn.shape[0]

    def body(a_ref, w_ref, c_ref, s_ref, p_ref, o_ref):
        a, cv, sv, pv = a_ref[...], c_ref[...], s_ref[...], p_ref[...]
        for h in range(HEADS):
            qh = _dotf(a, w_ref[pl.ds(h * QK, QK), :], "nt")
            o_ref[h] = (_rot(qh, cv, sv, pv, False) * (SCALE * LOG2E)).astype(BF16)

    rows = lambda c: pl.BlockSpec((TB, c), lambda i: (i, 0))
    const = lambda x: pl.BlockSpec(x.shape, lambda i: (0, 0))
    return pl.pallas_call(
        body, grid=(T // TB,), in_specs=[rows(QL), const(w), rows(QK), rows(QK), const(pm)],
        out_specs=pl.BlockSpec((HEADS, TB, QK), lambda i: (0, i, 0)), out_shape=jax.ShapeDtypeStruct((HEADS, T, QK), BF16),
        compiler_params=_cp(("parallel",)), name=name)(cqn, w, cosf, sinf, pm)


def project_kv(ckvn, w, p0, kr_block, name):
    T = ckvn.shape[0]
    cosf, sinf, pm, _ = _rope_tables(T, 128, 0)
    s_kn, s_kr, s_v = (jnp.asarray(s, BF16) for s in _kv_selectors())

    def body(a_ref, w_ref, kr_ref, c_ref, s_ref, p_ref, skn_ref, skr_ref, sv_ref, k_ref, v_ref):
        a = a_ref[...]
        krr = _rot(kr_ref[...], c_ref[...], s_ref[...], p_ref[...], False).astype(BF16)
        kr_part = jnp.dot(krr, skr_ref[...], preferred_element_type=F32)
        for h in range(HEADS):
            kvb = _dotf(a, w_ref[pl.ds(h * KVW, KVW), :], "nt").astype(BF16)
            k_ref[h] = (jnp.dot(kvb, skn_ref[...], preferred_element_type=F32) + kr_part).astype(BF16)
            v_ref[h] = jnp.dot(kvb, sv_ref[...], preferred_element_type=F32).astype(BF16)

    rows = lambda c: pl.BlockSpec((TB, c), lambda i: (i, 0))
    const = lambda x: pl.BlockSpec(x.shape, lambda i: (0, 0))
    return pl.pallas_call(
        body, grid=(T // TB,),
        in_specs=[rows(KVL), const(w), pl.BlockSpec((TB, 128), lambda i: (i, kr_block)),
                  rows(128), rows(128), const(pm), const(s_kn), const(s_kr), const(s_v)],
        out_specs=[pl.BlockSpec((HEADS, TB, QK), lambda i: (0, i, 0)), pl.BlockSpec((HEADS, TB, VD), lambda i: (0, i, 0))],
        out_shape=[jax.ShapeDtypeStruct((HEADS, T, QK), BF16), jax.ShapeDtypeStruct((HEADS, T, VD), BF16)],
        compiler_params=_cp(("parallel",)), name=name)(ckvn, w, p0, cosf, sinf, pm, s_kn, s_kr, s_v)


def split_kv_grads(dk, dv, name):
    H, T, _ = dk.shape
    cosf, sinf, _, pmt = _rope_tables(T, 128, 0)
    s_kn, s_kr, s_v = _kv_selectors()
    s_knt, s_krt, s_vt = (jnp.asarray(s.T, BF16) for s in (s_kn, s_kr, s_v))

    def body(dk_ref, dv_ref, c_ref, s_ref, p_ref, skn_ref, skr_ref, sv_ref, dkv_ref, dkr_ref):
        total = None
        for h in range(H):
            dkh = dk_ref[h] * (1.0 / LOG2E)
            total = dkh if total is None else total + dkh
            dkv_ref[:, pl.ds(h * KVW, KVW)] = (
                jnp.dot(dkh.astype(BF16), skn_ref[...], preferred_element_type=F32)
                + jnp.dot(dv_ref[h].astype(BF16), sv_ref[...], preferred_element_type=F32)).astype(BF16)
        dkr_ref[...] = _rot(_exact_perm(total, skr_ref[...]), c_ref[...], s_ref[...], p_ref[...], True)

    rows = lambda c: pl.BlockSpec((TB, c), lambda i: (i, 0))
    const = lambda a: pl.BlockSpec(a.shape, lambda i: (0, 0))
    return pl.pallas_call(
        body, grid=(T // TB,),
        in_specs=[pl.BlockSpec((H, TB, QK), lambda i: (0, i, 0)), pl.BlockSpec((H, TB, VD), lambda i: (0, i, 0)),
                  rows(128), rows(128), const(pmt), const(s_knt), const(s_krt), const(s_vt)],
        out_specs=[rows(H * KVW), rows(128)],
        out_shape=[jax.ShapeDtypeStruct((T, H * KVW), BF16), jax.ShapeDtypeStruct((T, 128), F32)],
        compiler_params=_cp(("parallel",)), name=name)(dk, dv, cosf, sinf, pmt, s_knt, s_krt, s_vt)


HB = 4


def _by_query_block(run, T):
    @pl.when(pl.program_id(1) == 0)
    def _():
        run(LC)

    @pl.when(pl.program_id(1) > 0)
    def _():
        run(T)


def _with_rider(body, nin, nout, ride, grid):
    if ride is None:
        return body
    n = ride.n

    def wrapped(*refs):
        ins, xs = refs[:nin], refs[nin:nin + n]
        outs, got = refs[nin + n:nin + n + nout], refs[nin + n + nout:nin + 2 * n + nout]
        sems = refs[nin + 2 * n + nout:]
        step = pl.program_id(0) * grid[1] + pl.program_id(1)

        @pl.when(step == 0)
        def _():
            ride.start(xs, got, sems)

        body(*ins, *outs)

        @pl.when(step == grid[0] * grid[1] - 1)
        def _():
            ride.finish(xs, got, sems)

    return wrapped


def _ride_call(body, grid, in_specs, out_specs, out_shape, ride, rode, name, args):
    if ride is None:
        return pl.pallas_call(body, grid=grid, in_specs=in_specs, out_specs=out_specs, out_shape=out_shape,
                              compiler_params=_cp(("parallel", "arbitrary")), name=name)(*args), []
    res = pl.pallas_call(
        _with_rider(body, len(in_specs), len(out_specs), ride, grid), grid=grid,
        in_specs=in_specs + ride.specs, out_specs=out_specs + ride.specs, out_shape=out_shape + ride.out_shape,
        scratch_shapes=ride.scratch,
        compiler_params=pltpu.CompilerParams(dimension_semantics=("arbitrary", "arbitrary"), vmem_limit_bytes=VMEM_LIMIT,
                                             has_side_effects=True), name=name)(*args, *rode)
    return res[:len(out_specs)], res[len(out_specs):]


def attn_fwd(q, k, v, name, rode=None, modes=None):
    H, T, _ = q.shape

    def body(q_ref, k_ref, v_ref, o_ref, lse_ref):
        def run(nk):
            for hh in range(HB):
                s = _dotf(q_ref[hh], k_ref[hh, pl.ds(0, nk), :], "nt")
                m = jnp.max(s, axis=1, keepdims=True)
                p = jnp.exp2(s - m)
                l = jnp.sum(p, axis=1, keepdims=True)
                o = jnp.dot(p.astype(BF16), v_ref[hh, pl.ds(0, nk), :], preferred_element_type=F32)
                o_ref[hh] = o / l
                lse_ref[hh] = m + jnp.log2(l)

        _by_query_block(run, T)

    return _ride_call(
        body, (H // HB, T // TB),
        [pl.BlockSpec((HB, TB, QK), lambda h, i: (h, i, 0)), pl.BlockSpec((HB, T, QK), lambda h, i: (h, 0, 0)),
         pl.BlockSpec((HB, T, VD), lambda h, i: (h, 0, 0))],
        [pl.BlockSpec((HB, TB, VD), lambda h, i: (h, i, 0)), pl.BlockSpec((HB, TB, 1), lambda h, i: (h, i, 0))],
        [jax.ShapeDtypeStruct((H, T, VD), F32), jax.ShapeDtypeStruct((H, T, 1), F32)],
        Exchange(rode, modes) if rode else None, rode, name, (q, k, v))


def attn_bwd(q, k, v, o, lse, do, name, rode=None, modes=None):
    H, T, _ = q.shape

    def body(q_ref, k_ref, v_ref, o_ref, lse_ref, do_ref, dq_ref, dk_ref, dv_ref):
        i = pl.program_id(1)

        @pl.when(i == 0)
        def _():
            dk_ref[...] = jnp.zeros_like(dk_ref)
            dv_ref[...] = jnp.zeros_like(dv_ref)

        def run(nk):
            keys = pl.ds(0, nk)
            for hh in range(HB):
                qv, kv, dov = q_ref[hh], k_ref[hh, keys, :], do_ref[hh]
                p = jnp.exp2(_dotf(qv, kv, "nt") - lse_ref[hh])
                delta = jnp.sum(dov * o_ref[hh], axis=1, keepdims=True)
                dob = dov.astype(BF16)
                dv_ref[hh, keys, :] += _dotf(p.astype(BF16), dob, "tn")
                dp = _dotf(dob, v_ref[hh, keys, :], "nt")
                ds = (p * (dp - delta)).astype(BF16)
                dq_ref[hh] = jnp.dot(ds, kv, preferred_element_type=F32)
                dk_ref[hh, keys, :] += _dotf(ds, qv, "tn")

        _by_query_block(run, T)

    blk = lambda c: pl.BlockSpec((HB, TB, c), lambda h, i: (h, i, 0))
    full = lambda c: pl.BlockSpec((HB, T, c), lambda h, i: (h, 0, 0))
    return _ride_call(
        body, (H // HB, T // TB), [blk(QK), full(QK), full(VD), blk(VD), blk(1), blk(VD)], [blk(QK), full(QK), full(VD)],
        [jax.ShapeDtypeStruct((H, T, QK), F32), jax.ShapeDtypeStruct((H, T, QK), F32), jax.ShapeDtypeStruct((H, T, VD), F32)],
        Exchange(rode, modes) if rode else None, rode, name, (q, k, v, o, lse, do))


def disc_fwd(a_re, a_im, ls, name):
    def body(ar_ref, ai_ref, ls_ref, lr_ref, li_ref, fr_ref, fi_ref):
        ar, ai = ar_ref[...], ai_ref[...]
        dt = jnp.exp(ls_ref[...])
        mag = jnp.exp(ar * dt)
        lr = mag * jnp.cos(ai * dt)
        li = mag * jnp.sin(ai * dt)
        den = ar * ar + ai * ai
        nr = lr - 1.0
        lr_ref[...] = lr
        li_ref[...] = li
        fr_ref[...] = (nr * ar + li * ai) / den
        fi_ref[...] = (li * ar - nr * ai) / den

    return pl.pallas_call(body, out_shape=[jax.ShapeDtypeStruct(a_re.shape, F32)] * 4, name=name)(a_re, a_im, ls)


def disc_b(f_re, f_im, b_re, b_im, name):
    def body(fr_ref, fi_ref, br_ref, bi_ref, or_ref, oi_ref):
        fr, fi, br, bi = fr_ref[...], fi_ref[...], br_ref[...], bi_ref[...]
        or_ref[...] = fr * br - fi * bi
        oi_ref[...] = fr * bi + fi * br

    fs, bs = _disc_b_specs()
    return pl.pallas_call(body, grid=(2, G * P // DISC_ROWS), in_specs=[fs, fs, bs, bs], out_specs=[bs, bs],
                          out_shape=[jax.ShapeDtypeStruct(b_re.shape, F32)] * 2, name=name)(f_re, f_im, b_re, b_im)


DISC_ROWS = G * P


def _disc_b_specs():
    return (pl.BlockSpec((1, DISC_ROWS, 1), lambda d, i: (d, i, 0)), pl.BlockSpec((1, DISC_ROWS, CH), lambda d, i: (d, i, 0)))


def disc_b_bwd(f_re, f_im, b_re, b_im, dbb_re, dbb_im, name):
    def body(fr_ref, fi_ref, br_ref, bi_ref, dr_ref, di_ref, dbr_ref, dbi_ref, dfr_ref, dfi_ref):
        fr, fi, br, bi, dr, di = fr_ref[...], fi_ref[...], br_ref[...], bi_ref[...], dr_ref[...], di_ref[...]
        dbr_ref[...] = fr * dr + fi * di
        dbi_ref[...] = fr * di - fi * dr
        dfr_ref[...] = jnp.sum(dr * br + di * bi, axis=-1, keepdims=True)
        dfi_ref[...] = jnp.sum(di * br - dr * bi, axis=-1, keepdims=True)

    fs, bs = _disc_b_specs()
    return pl.pallas_call(body, grid=(2, G * P // DISC_ROWS), in_specs=[fs, fs, bs, bs, bs, bs], out_specs=[bs, bs, fs, fs],
                          out_shape=[jax.ShapeDtypeStruct(b_re.shape, F32)] * 2 + [jax.ShapeDtypeStruct(f_re.shape, F32)] * 2,
                          name=name)(f_re, f_im, b_re, b_im, dbb_re, dbb_im)


def disc_a_bwd(a_re, a_im, ls, dlr, dli, dfr, dfi, name):
    def body(ar_ref, ai_ref, ls_ref, dlr_ref, dli_ref, dfr_ref, dfi_ref, dar_ref, dai_ref, dls_ref):
        ar, ai = ar_ref[...], ai_ref[...]
        dt = jnp.exp(ls_ref[...])
        mag = jnp.exp(ar * dt)
        cs, sn = jnp.cos(ai * dt), jnp.sin(ai * dt)
        lr, li = mag * cs, mag * sn
        den = ar * ar + ai * ai
        nr = lr - 1.0
        f_re = (nr * ar + li * ai) / den
        f_im = (li * ar - nr * ai) / den
        dn1 = dfr_ref[...] / den
        dn2 = dfi_ref[...] / den
        dden = -(dfr_ref[...] * f_re + dfi_ref[...] * f_im) / den
        dlr_t = dlr_ref[...] + dn1 * ar - dn2 * ai
        dli_t = dli_ref[...] + dn1 * ai + dn2 * ar
        dar = dn1 * nr + dn2 * li + dden * 2.0 * ar
        dai = dn1 * li - dn2 * nr + dden * 2.0 * ai
        dmag = dlr_t * cs + dli_t * sn
        dth = dli_t * lr - dlr_t * li
        dar_ref[...] = dar + dmag * mag * dt
        dai_ref[...] = dai + dth * dt
        dls_ref[...] = jnp.sum(dmag * mag * ar + dth * ai, axis=-1, keepdims=True) * dt

    return pl.pallas_call(body, out_shape=[jax.ShapeDtypeStruct(a_re.shape, F32)] * 2 +
                          [jax.ShapeDtypeStruct(ls.shape, F32)], name=name)(a_re, a_im, ls, dlr, dli, dfr, dfi)


def _cpow(lr, li, n):
    rr, ri = None, None
    br, bi = lr, li
    while n:
        if n & 1:
            if rr is None:
                rr, ri = br, bi
            else:
                rr, ri = rr * br - ri * bi, rr * bi + ri * br
        n >>= 1
        if n:
            br, bi = br * br - bi * bi, 2.0 * br * bi
    return rr, ri


UNROLL = 4


def _steps(trips, fn, init):
    main = trips // UNROLL

    def body(i, c):
        for j in range(UNROLL):
            c = fn(i * UNROLL + j, c)
        return c

    c = lax.fori_loop(0, main, body, init) if main else init
    for n in range(main * UNROLL, trips):
        c = fn(n, c)
    return c


def _seg_scan(xre, xim, lam8, pw, base, seglen, rev, init, fin_re, fin_im, ini_re, ini_im, prev=None):
    lr, li = lam8
    nsub = SEG // 8

    def rows(t, s):
        first = base + t * SEG + 8 * s
        return pl.ds(first if isinstance(first, int) else pl.multiple_of(first, 8), 8)

    tmap = (lambda n: seglen - 1 - n) if rev else (lambda n: n)
    zeros = tuple(jnp.zeros((8, SB), F32) for _ in range(2 * nsub))

    def advance(c, t):
        out = []
        for s in range(nsub):
            a, b = c[2 * s], c[2 * s + 1]
            out += [lr * a - li * b + xre[rows(t, s), :], lr * b + li * a + xim[rows(t, s), :]]
        return tuple(out)

    fin = _steps(seglen, lambda n, c: advance(c, tmap(n)), zeros)
    for s in range(nsub):
        fin_re[pl.ds(8 * s, 8), :] = fin[2 * s]
        fin_im[pl.ds(8 * s, 8), :] = fin[2 * s + 1]
    (cr, ci), (pr, pi) = init, pw
    for i in (range(SEG - 1, -1, -1) if rev else range(SEG)):
        ini_re[pl.ds(i, 1), :] = cr
        ini_im[pl.ds(i, 1), :] = ci
        cr, ci = pr * cr - pi * ci + fin_re[pl.ds(i, 1), :], pr * ci + pi * cr + fin_im[pl.ds(i, 1), :]
    tiles = lambda re, im: tuple(r[pl.ds(8 * s, 8), :] for s in range(nsub) for r in (re, im))
    start = tiles(ini_re, ini_im)

    def store(c, t):
        new = advance(c, t)
        for s in range(nsub):
            xre[rows(t, s), :] = new[2 * s]
            xim[rows(t, s), :] = new[2 * s + 1]
        return new

    if prev is None:
        _steps(seglen, lambda n, c: store(c, tmap(n)), start)
        return (cr, ci), None

    sre, sim, s_ini_re, s_ini_im = prev

    def acc_step(c, t, before):
        new = store(c[:2 * nsub], t)
        acc = []
        for s in range(nsub):
            (na, nb), (pre, pim) = new[2 * s:2 * s + 2], before[2 * s:2 * s + 2]
            acc += [c[2 * nsub + 2 * s] + na * pre + nb * pim, c[2 * nsub + 2 * s + 1] + nb * pre - na * pim]
        return new + tuple(acc)

    def body(n, c):
        t = tmap(n)
        tp = t - 1 if rev else t + 1
        return acc_step(c, t, tuple(r[rows(tp, s), :] for s in range(nsub) for r in (sre, sim)))

    c = _steps(seglen - 1, body, start + zeros)
    c = acc_step(c, 0 if rev else seglen - 1, tiles(s_ini_re, s_ini_im))
    acc = c[2 * nsub:]
    return (cr, ci), (sum(acc[0::2][1:], acc[0]), sum(acc[1::2][1:], acc[1]))


def _lam_tiles(lr, li, lens, conj=False):
    if conj:
        li = -li
    lam8 = (jnp.broadcast_to(lr, (8, SB)), jnp.broadcast_to(li, (8, SB)))
    return lam8, [_cpow(lr, li, n) for n in lens]


def _stretches(T):
    return ((0, LC // SEG), (LC, (T - LC) // SEG))


def _to_seg_order(src, dst, T):
    for base, seglen in _stretches(T):
        def body(t, carry, base=base, seglen=seglen):
            dst[pl.ds(pl.multiple_of(base + t * SEG, SEG), SEG), :] = src[pl.ds(base + t, SEG, stride=seglen), :]
            return carry
        lax.fori_loop(0, seglen, body, 0, unroll=8)


def _from_seg_order(src, dst, T):
    for base, seglen in _stretches(T):
        def body(t, carry, base=base, seglen=seglen):
            dst[pl.ds(base + t, SEG, stride=seglen), :] = src[pl.ds(pl.multiple_of(base + t * SEG, SEG), SEG), :]
            return carry
        lax.fori_loop(0, seglen, body, 0, unroll=8)


def _scan_specs(T):
    ublk = pl.BlockSpec((T, UB), lambda j: (0, j))
    lam = pl.BlockSpec((2, 1, 1, SB), lambda j: (0, j, 0, 0))
    mat = pl.BlockSpec((2, 1, UB, P), lambda j: (0, j, 0, 0))
    return ublk, lam, mat


def _dotf(a, b, mode="nn"):
    return lax.dot_general(a, b, _DN[mode], preferred_element_type=F32)


def _diag_mask():
    r = lax.broadcasted_iota(jnp.int32, (UB, SB), 0)
    c = lax.broadcasted_iota(jnp.int32, (UB, SB), 1)
    return lax.shift_right_logical(r, int(math.log2(CH))) == lax.shift_right_logical(c, int(math.log2(P)))


def _expand(m):
    p = lax.broadcasted_iota(jnp.int32, (P, SB), 0)
    c = lax.broadcasted_iota(jnp.int32, (P, SB), 1)
    tile = jnp.where(lax.bitwise_and(c, P - 1) == p, 1.0, 0.0).astype(BF16)
    wide = jnp.dot(m.astype(BF16), tile, preferred_element_type=F32)
    return jnp.where(_diag_mask(), wide, 0.0).astype(BF16)


def _collapse(full):
    c = lax.broadcasted_iota(jnp.int32, (SB, P), 0)
    p = lax.broadcasted_iota(jnp.int32, (SB, P), 1)
    pick = jnp.where(lax.bitwise_and(c, P - 1) == p, 1.0, 0.0).astype(BF16)
    return _exact_perm(jnp.where(_diag_mask(), full, 0.0), pick)


def _zero_state():
    return jnp.zeros((1, SB), F32), jnp.zeros((1, SB), F32)


def scan_fwd(u, lam_re, lam_im, bre, bim, cre, cim, name):
    T = u.shape[0]
    s_ctx, s_lat = LC // SEG, (T - LC) // SEG

    def body(u_ref, lr_ref, li_ref, bre_ref, bim_ref, cre_ref, cim_ref, y_ref, us, ys, sre, sim, fre, fim, ire, iim):
        _to_seg_order(u_ref, us, T)
        ub = us[...].astype(BF16)
        for d in range(2):
            lam8, (pw_c, pw_l) = _lam_tiles(lr_ref[d, 0], li_ref[d, 0], (s_ctx, s_lat))
            sre[...] = _dotf(ub, _expand(bre_ref[d, 0]))
            sim[...] = _dotf(ub, _expand(bim_ref[d, 0]))
            end_c, _ = _seg_scan(sre, sim, lam8, pw_c, 0, s_ctx, bool(d), _zero_state(), fre, fim, ire, iim)
            _seg_scan(sre, sim, lam8, pw_l, LC, s_lat, bool(d), end_c, fre, fim, ire, iim)
            y = (_dotf(sre[...].astype(BF16), _expand(cre_ref[d, 0]), "nt")
                 - _dotf(sim[...].astype(BF16), _expand(cim_ref[d, 0]), "nt"))
            if d == 0:
                ys[...] = y
            else:
                ys[...] += y
        _from_seg_order(ys, y_ref, T)

    ublk, lam, mat = _scan_specs(T)
    return pl.pallas_call(
        body, grid=(NJ,), in_specs=[ublk, lam, lam, mat, mat, mat, mat], out_specs=ublk,
        out_shape=jax.ShapeDtypeStruct((T, G * CH), F32),
        scratch_shapes=[pltpu.VMEM((T, UB), F32)] * 2 + [pltpu.VMEM((T, SB), F32)] * 2 + [pltpu.VMEM((SEG, SB), F32)] * 4,
        compiler_params=_cp(("arbitrary",)), name=name)(u, lam_re, lam_im, bre, bim, cre, cim)


def scan_bwd(u, dy, lam_re, lam_im, bre, bim, cre, cim, name):
    T = u.shape[0]
    s_ctx, s_lat = LC // SEG, (T - LC) // SEG

    def body(u_ref, dy_ref, lr_ref, li_ref, bre_ref, bim_ref, cre_ref, cim_ref,
             du_ref, dlr_ref, dli_ref, dbre_ref, dbim_ref, dcre_ref, dcim_ref,
             us, dys, dus, sre, sim, gre, gim, fre, fim, ic_re, ic_im, il_re, il_im, jre, jim):
        _to_seg_order(u_ref, us, T)
        _to_seg_order(dy_ref, dys, T)
        ub, dyb = us[...].astype(BF16), dys[...].astype(BF16)
        for d in range(2):
            rev = bool(d)
            lam8, (pw_c, pw_l) = _lam_tiles(lr_ref[d, 0], li_ref[d, 0], (s_ctx, s_lat))
            cam8, (cw_c, cw_l) = _lam_tiles(lr_ref[d, 0], li_ref[d, 0], (s_ctx, s_lat), conj=True)
            bre_v, bim_v = _expand(bre_ref[d, 0]), _expand(bim_ref[d, 0])
            sre[...] = _dotf(ub, bre_v)
            sim[...] = _dotf(ub, bim_v)
            end_c, _ = _seg_scan(sre, sim, lam8, pw_c, 0, s_ctx, rev, _zero_state(), fre, fim, ic_re, ic_im)
            _seg_scan(sre, sim, lam8, pw_l, LC, s_lat, rev, end_c, fre, fim, il_re, il_im)
            gre[...] = _dotf(dyb, _expand(cre_ref[d, 0]))
            gim[...] = -_dotf(dyb, _expand(cim_ref[d, 0]))
            end_g, acc_l = _seg_scan(gre, gim, cam8, cw_l, LC, s_lat, not rev, _zero_state(), fre, fim, jre, jim,
                                     prev=(sre, sim, il_re, il_im))
            _, acc_c = _seg_scan(gre, gim, cam8, cw_c, 0, s_ctx, not rev, end_g, fre, fim, jre, jim,
                                 prev=(sre, sim, ic_re, ic_im))
            dlr_ref[d, 0] = _sum0(acc_l[0] + acc_c[0])
            dli_ref[d, 0] = _sum0(acc_l[1] + acc_c[1])
            grb, gib = gre[...].astype(BF16), gim[...].astype(BF16)
            du = _dotf(grb, bre_v, "nt") + _dotf(gib, bim_v, "nt")
            if d == 0:
                dus[...] = du
            else:
                dus[...] += du
            dbre_ref[d, 0] = _collapse(_dotf(ub, grb, "tn"))
            dbim_ref[d, 0] = _collapse(_dotf(ub, gib, "tn"))
            dcre_ref[d, 0] = _collapse(_dotf(dyb, sre[...].astype(BF16), "tn"))
            dcim_ref[d, 0] = -_collapse(_dotf(dyb, sim[...].astype(BF16), "tn"))
        _from_seg_order(dus, du_ref, T)

    ublk, lam, mat = _scan_specs(T)
    lam_s = jax.ShapeDtypeStruct(lam_re.shape, F32)
    mat_s = jax.ShapeDtypeStruct(bre.shape, F32)
    return pl.pallas_call(
        body, grid=(NJ,), in_specs=[ublk, ublk, lam, lam, mat, mat, mat, mat],
        out_specs=[ublk, lam, lam, mat, mat, mat, mat],
        out_shape=[jax.ShapeDtypeStruct((T, G * CH), F32), lam_s, lam_s, mat_s, mat_s, mat_s, mat_s],
        scratch_shapes=[pltpu.VMEM((T, UB), F32)] * 3 + [pltpu.VMEM((T, SB), F32)] * 4 + [pltpu.VMEM((SEG, SB), F32)] * 8,
        compiler_params=_cp(("arbitrary",)), name=name)(u, dy, lam_re, lam_im, bre, bim, cre, cim)


class Exchange:
    def __init__(self, xs, modes):
        self.n = len(xs)
        self.modes = [modes] * self.n if isinstance(modes, (str, int)) else list(modes)
        self.out_shape = [jax.ShapeDtypeStruct(self._shape(x, md), x.dtype) for x, md in zip(xs, self.modes)]
        self.scratch = [pltpu.SemaphoreType.DMA((NDEV - 1, self.n)), pltpu.SemaphoreType.DMA((NDEV - 1, self.n)),
                        pltpu.SemaphoreType.DMA((self.n,))]
        self.specs = [pl.BlockSpec(memory_space=pl.ANY)] * self.n

    @staticmethod
    def _shape(x, mode):
        if mode == "gather":
            return (NDEV,) + tuple(x.shape)
        return tuple(x.shape) if mode == "lead" else (NDEV, x.shape[0], mode) + tuple(x.shape[2:])

    @staticmethod
    def _piece(x_ref, mode, dev):
        if mode == "gather":
            return x_ref
        return x_ref.at[dev] if mode == "lead" else x_ref.at[:, pl.ds(dev * mode, mode)]

    def _copies(self, x_refs, out_refs, sems):
        send_sems, recv_sems, local_sems = sems
        mx, my, mc = lax.axis_index("x"), lax.axis_index("y"), lax.axis_index("c")
        me = 4 * mx + 2 * my + mc
        peer_of = lambda k: (1 - mx if k & 4 else mx, 1 - my if k & 2 else my, 1 - mc if k & 1 else mc)
        local, first, relay, arrivals = [], [], [], []
        for a, (x_ref, out_ref) in enumerate(zip(x_refs, out_refs)):
            mode = self.modes[a]
            local.append(pltpu.make_async_copy(self._piece(x_ref, mode, me), out_ref.at[me], local_sems.at[a]))

            def remote(src, dst, k, pair, a=a):
                return pltpu.make_async_remote_copy(src_ref=src, dst_ref=dst, send_sem=send_sems.at[pair, a],
                                                    recv_sem=recv_sems.at[pair, a], device_id=peer_of(k), device_id_type=MESH_T)

            for k in range(1, NDEV):
                peer = peer_of(k)
                pid = 4 * peer[0] + 2 * peer[1] + peer[2]
                if mode != "gather":
                    src = self._piece(x_ref, mode, pid)
                    first.append(remote(src, out_ref.at[me], k, k - 1))
                    arrivals.append(remote(src, out_ref.at[pid], k, k - 1))
                elif k == 1:
                    first.append(remote(x_ref, out_ref.at[me], k, k - 1))
                    arrivals.append(remote(x_ref, out_ref.at[pid], k, k - 1))
                elif k % 2 == 0:
                    first.append(remote(x_ref, out_ref.at[me], k, k - 1))
                    relay.append((remote(x_ref, out_ref.at[pid], k, k - 1), remote(out_ref.at[pid], out_ref.at[pid], 1, k)))
                else:
                    arrivals.append(remote(x_ref, out_ref.at[pid], 1, k - 1))
        return local, first, relay, arrivals

    def start(self, x_refs, out_refs, sems):
        local, first, _, _ = self._copies(x_refs, out_refs, sems)
        for cp in local + first:
            cp.start()

    def finish(self, x_refs, out_refs, sems):
        local, first, relay, arrivals = self._copies(x_refs, out_refs, sems)
        for arrival, onward in relay:
            arrival.wait_recv()
            onward.start()
        for cp in arrivals:
            cp.wait_recv()
        for cp in first + [onward for _, onward in relay]:
            cp.wait_send()
        for cp in local:
            cp.wait()


def exchange(xs, modes, name):
    ex = Exchange(xs, modes)
    n = ex.n

    def body(*refs):
        ex.start(refs[:n], refs[n:2 * n], refs[2 * n:])
        ex.finish(refs[:n], refs[n:2 * n], refs[2 * n:])

    return pl.pallas_call(body, in_specs=ex.specs, out_specs=ex.specs, out_shape=ex.out_shape, scratch_shapes=ex.scratch,
                          compiler_params=pltpu.CompilerParams(has_side_effects=True), name=name)(*xs)


def _dot_f32(a, b, dn):
    return lax.dot_general(a, b, dn, preferred_element_type=F32, precision=lax.Precision.HIGHEST)


def ada_fwd(cg, c_ctx, ada_w, ada_b_loc, name):
    W = ada_w.shape[2]

    def body(cg_ref, cc_ref, w_ref, b_ref, o_ref):
        a = jnp.concatenate([_silu(cg_ref[...]), jnp.broadcast_to(_silu(cc_ref[...]), (NDEV, D))], axis=0)
        for i in range(2):
            o_ref[i] = _dot_f32(a, w_ref[i], _DN["nn"]) + b_ref[i]

    return pl.pallas_call(body, out_shape=jax.ShapeDtypeStruct((2, 2 * NDEV, W), F32),
                          compiler_params=_cp(), name=name)(cg, c_ctx, ada_w, ada_b_loc)


def ada_bwd(cg, c_ctx, ada_w, dm_loc, dm_all, name):
    W = ada_w.shape[2]

    def body(cg_ref, cc_ref, w_ref, dl_ref, da_ref, gw_ref, dcc_ref, gb_ref):
        a = jnp.concatenate([_silu(cg_ref[...]), jnp.broadcast_to(_silu(cc_ref[...]), (NDEV, D))], axis=0)
        dcc = jnp.zeros((1, D), F32)
        for i in range(2):
            dl = dl_ref[i]
            gw_ref[i] = _dot_f32(a, dl, _DN["tn"])
            dctx = jnp.sum(dl[NDEV:], axis=0, keepdims=True)
            dcc = dcc + _dot_f32(dctx, w_ref[i], _DN["nt"])
        dcc_ref[...] = dcc
        gb_ref[...] = jnp.sum(da_ref[...], axis=0)

    return pl.pallas_call(body, out_shape=[jax.ShapeDtypeStruct((2, D, W), F32), jax.ShapeDtypeStruct((1, D), F32),
                                           jax.ShapeDtypeStruct((2, 3 * D), F32)],
                          compiler_params=_cp(), name=name)(cg, c_ctx, ada_w, dm_loc, dm_all)


def cctx_finish(parts, c_ctx, name):
    def body(p_ref, cc_ref, o_ref):
        o_ref[...] = jnp.sum(p_ref[...], axis=0, keepdims=True) * _dsilu(cc_ref[...])

    return pl.pallas_call(body, out_shape=jax.ShapeDtypeStruct((1, D), F32), name=name)(parts, c_ctx)


def _adamw_update(g_ref, w_ref, m_ref, v_ref, go_ref, d_ref, mo_ref, vo_ref):
    g = g_ref[0].astype(F32)
    for s in range(1, g_ref.shape[0]):
        g = g + g_ref[s].astype(F32)
    mn = B1 * m_ref[...] + (1.0 - B1) * g
    vn = B2 * v_ref[...] + (1.0 - B2) * g * g
    go_ref[...] = g
    mo_ref[...] = mn
    vo_ref[...] = vn
    d_ref[...] = -LR * ((mn * (1.0 / (1.0 - B1 ** STEP))) / (jnp.sqrt(vn * (1.0 / (1.0 - B2 ** STEP))) + AEPS) + WD * w_ref[...])


def adamw(gstack, w, m, v, name, tr=256):
    n, R, C = gstack.shape
    tr = max(t for t in range(8, min(tr, R) + 1, 8) if R % t == 0)
    spec = pl.BlockSpec((tr, C), lambda i: (i, 0))
    return pl.pallas_call(_adamw_body(1), grid=(R // tr,),
                          in_specs=[pl.BlockSpec((n, tr, C), lambda i: (0, i, 0)), spec, spec, spec],
                          out_specs=[spec] * 4, out_shape=[jax.ShapeDtypeStruct((R, C), F32)] * 4,
                          compiler_params=_cp(("parallel",)), name=name)(gstack, w, m, v)


def _adamw_body(k):
    def body(*refs):
        for t in range(k):
            _adamw_update(*refs[4 * t:4 * t + 4], *refs[4 * k + 4 * t:4 * k + 4 * t + 4])
    return body


def adamw_multi(items, grid, name):
    k = len(items)
    ins, in_specs, out_specs, out_shape = [], [], [], []
    for g, g_spec, w, m, v, w_spec in items:
        ins += [g, w, m, v]
        in_specs += [g_spec, w_spec, w_spec, w_spec]
    for g, g_spec, w, m, v, w_spec in items:
        out_specs += [w_spec] * 4
        out_shape += [jax.ShapeDtypeStruct(w.shape, F32)] * 4
    res = pl.pallas_call(_adamw_body(k), grid=grid, in_specs=in_specs, out_specs=out_specs, out_shape=out_shape,
                         compiler_params=_cp(("arbitrary",) * len(grid)), name=name)(*ins)
    return [res[4 * t:4 * t + 4] for t in range(k)]


def _whole(a, grid_rank):
    zeros = (0,) * a.ndim
    return pl.BlockSpec(a.shape, lambda *idx: zeros)


def sum_slots(xs, name):
    def body(*refs):
        for x_ref, o_ref in zip(refs[:len(xs)], refs[len(xs):]):
            acc = x_ref[0]
            for s in range(1, NDEV):
                acc = acc + x_ref[s]
            o_ref[...] = acc

    return pl.pallas_call(body, out_shape=[jax.ShapeDtypeStruct(x.shape[1:], F32) for x in xs],
                          compiler_params=_cp(), name=name)(*xs)


def _col_shards(g):
    R, N = g.shape
    return g.reshape(R, NDEV, N // NDEV).transpose(1, 0, 2)


def _vec2(v):
    return jnp.broadcast_to(v.reshape(1, 1, -1), (2, 1, v.size))


SHARD_ROWS = {"mla_w_in": 192, "mla_w_uq": 192, "mla_w_ukv": 256, "s5_w_in": 256}


def _t_shard(wsh, rows):
    t = wsh[0].T.astype(BF16)
    return jnp.pad(t, ((0, rows - t.shape[0]), (0, 0)))


def _win_order():
    w = IN_W // NDEV
    perm = np.zeros((IN_WP, NDEV * SHARD_ROWS["mla_w_in"]), np.float32)
    first = QL + KVL + ROPE
    for c in range(IN_W):
        n = c + HEADS * VD if c < first else c - first
        perm[n, (c // w) * SHARD_ROWS["mla_w_in"] + c % w] = 1.0
    return jnp.asarray(perm, BF16)


def local_step(ctx, x, tgt, mod, Wt, small, l1_shards):
    T = LC + x.shape[0]
    xa = ("cat", ctx, x)
    sh = [mod[i, :, None, 0:D] for i in range(2)]
    sc = [mod[i, :, None, D:2 * D] for i in range(2)]
    gt = [mod[i, :, None, 2 * D:] for i in range(2)]
    ng = [_vec2(small["norm_g"][i]) for i in range(2)]
    qg, kvg = _vec2(small["mla_q_norm"]), _vec2(small["mla_kv_norm"])
    cosf, sinf, pm, pmt = _rope_tables(T)

    (h0, p0, cqn, ckvn), _ = rowwise(st_l0_pre, [xa], [ng[0], sc[0], sh[0], qg, kvg],
                                     [(D, BF16), (IN_WP, F32), (QL, BF16), (KVL, BF16)], [], "l0_pre", mats=[Wt["mla_w_in"]])
    z0, cq, ckv = (p0, 0, HEADS * VD), (p0, HEADS * VD // QL, QL), (p0, (HEADS * VD + QL) // KVL, KVL)
    Q = project_q(cqn, Wt["mla_w_uq"], cosf, sinf, pm, "l0_uq")
    K, V = project_kv(ckvn, Wt["mla_w_ukv"], p0, (HEADS * VD + QL + KVL) // 128, "l0_ukv")
    (o, lse), got = attn_fwd(Q, K, V, "l0_attn", rode=l1_shards, modes="gather")
    Wt, small = dict(Wt), dict(small)
    for n, a in zip(L1_BIG, got):
        Wt[n] = a.reshape(-1, a.shape[-1])
    vecs = lax.bitcast_convert_type(got[-1].reshape(NDEV, 2, -1, 2), F32)
    small["s5_d"], small["s5_b_glu"] = vecs[:, 0, :].reshape(D), vecs[:, 1, :].reshape(D)
    o2 = o.transpose(1, 0, 2).reshape(T, HEADS * VD)
    (og, out0, x1), _ = rowwise(st_l0_post, [o2, z0, xa], [gt[0]], [(D, BF16), (D, BF16), (D, F32)], [], "l0_post",
                                mats=[Wt["mla_w_out"]])

    ls = small["s5_log_step"].reshape(2, G, 1)
    a_re, a_im = small["s5_a_re"].reshape(2, G, P), small["s5_a_im"].reshape(2, G, P)
    b_re, b_im = small["s5_b_re"].reshape(2, G * P, CH), small["s5_b_im"].reshape(2, G * P, CH)
    lam_re, lam_im, f_re, f_im = disc_fwd(a_re, a_im, ls, "s5_disc")
    f_re2, f_im2 = f_re.reshape(2, G * P, 1), f_im.reshape(2, G * P, 1)
    bb_re, bb_im = disc_b(f_re2, f_im2, b_re, b_im, "s5_disc_b")
    compact = lambda m: m.reshape(2, NJ, UB, P)
    bre = compact(bb_re.reshape(2, G, P, CH).transpose(0, 1, 3, 2))
    bim = compact(bb_im.reshape(2, G, P, CH).transpose(0, 1, 3, 2))
    cre, cim = compact(small["s5_c_re"]), compact(small["s5_c_im"])
    lam_re4, lam_im4 = lam_re.reshape(2, NJ, 1, SB), lam_im.reshape(2, NJ, 1, SB)

    (h1, p1), _ = rowwise(st_l1_pre, [x1], [ng[1], sc[1], sh[1]], [(D, BF16), (2 * D, F32)], [], "l1_pre", mats=[Wt["s5_w_in"]])
    u, z1 = (p1, 0, D), (p1, 1, D)
    yssm = scan_fwd(p1, lam_re4, lam_im4, bre, bim, cre, cim, "s5_scan")
    dvec, bglu = _vec2(small["s5_d"]), _vec2(small["s5_b_glu"])
    fg = _vec2(small["final_g"])
    lat_mask = jnp.stack([jnp.zeros((1, D), F32), jnp.ones((1, D), F32)])
    (y, y1b, gl, y3, out1, dx2), (dfg, lvec) = rowwise(
        st_l1_mlp, [yssm, u, z1, x1, ("lat", tgt)], [dvec, bglu, gt[1], fg, lat_mask],
        [(D, F32), (D, BF16), (D, BF16), (D, BF16), (D, BF16), (D, F32)], [D, 128], "l1_mlp",
        mats=[Wt["s5_w_glu"], Wt["s5_w_out"]])

    (dz1, dy, du_d), (dgt1, dbglu, dd), (g_w_out5, g_w_glu) = rowwise(
        st_l1_mlp_bwd, [dx2, out1, y3, y, gl, z1, u, y1b], [gt[1], bglu, dvec], [(D, BF16), (D, F32), (D, F32)], [D, D, D],
        "l1_mlp_b", mats=[Wt["s5_w_out"], Wt["s5_w_glu"]], out_accs=[(D, D), (D, D)])
    du_s, dlr, dli, dbre, dbim, dcre, dcim = scan_bwd(p1, dy, lam_re4, lam_im4, bre, bim, cre, cim, "s5_scan_b")
    dbb_re = dbre.reshape(2, G, CH, P).transpose(0, 1, 3, 2).reshape(2, G * P, CH)
    dbb_im = dbim.reshape(2, G, CH, P).transpose(0, 1, 3, 2).reshape(2, G * P, CH)
    g_c_re, g_c_im = dcre.reshape(2, G, CH, P), dcim.reshape(2, G, CH, P)
    g_b_re, g_b_im, dfr, dfi = disc_b_bwd(f_re2, f_im2, b_re, b_im, dbb_re, dbb_im, "s5_disc_b_b")
    g_a_re, g_a_im, g_ls = disc_a_bwd(a_re, a_im, ls, dlr.reshape(2, G, P), dli.reshape(2, G, P),
                                      dfr.reshape(2, G, P), dfi.reshape(2, G, P), "s5_disc_b_a")
    (dx1,), (dsh1, dsc1, dng1), (g_w_in5,) = rowwise(
        st_l1_tail_bwd, [du_d, du_s, dz1, h1, x1, dx2], [ng[1], sc[1]], [(D, F32)], [D, D, D], "l1_pre_b",
        mats=[Wt["s5_w_in"]], out_accs=[(D, 2 * D)])
    g_w_in5 = _col_shards(g_w_in5)

    (do2, dz0), (dgt0,), (g_w_out,) = rowwise(st_l0_post_bwd, [dx1, out0, og, o2, z0], [gt[0]], [(D, F32), (D, F32)], [D],
                                              "l0_post_b", mats=[Wt["mla_w_out"]], out_accs=[(D, D)])
    doh = do2.reshape(T, HEADS, VD).transpose(1, 0, 2)
    rows8 = lambda g: g.reshape(NDEV, -1, g.shape[-1])
    both = lambda s: s[0, 0] + s[1, 0]
    dense = lambda g: g.reshape(2, G * P * CH // 128, 128)
    chunks = [dense(g_b_re), dense(g_b_im), g_c_re, g_c_im]
    l1_send = [g_w_in5, rows8(g_w_glu), rows8(g_w_out5), rows8(g_w_out),
               both(dd).reshape(NDEV, 1, -1), both(dbglu).reshape(NDEV, 1, -1)]
    (dQ, dK, dV), l1_recv = attn_bwd(Q, K, V, o, lse, doh, "l0_attn_b", rode=l1_send + chunks,
                                     modes=["lead"] * len(l1_send) + [a.shape[1] // NDEV for a in chunks])
    dqh = rope(dQ, cosf, sinf, pmt, True, BF16, "l0_rope_q_b", scale=SCALE)
    dq = dqh.transpose(1, 0, 2).reshape(T, HEADS * QK)
    dkv, dkr = split_kv_grads(dK, dV, "l0_kv_b")
    (grad_x,), (dqg, dkvg, dsh0, dsc0, dng0), (g_uq, g_ukv, g_p) = rowwise(
        st_l0_tail_bwd, [dq, dkv, dkr, dz0, cq, ckv, cqn, ckvn, h0, xa, dx1], [qg, kvg, ng[0], sc[0]],
        [(D, F32, "lat")], [QL, KVL, D, D, D], "l0_pre_b", mats=[Wt["mla_w_uq"], Wt["mla_w_ukv"], Wt["mla_w_in"]],
        out_accs=[(QL, HEADS * QK), (KVL, HEADS * KVW), (D, IN_WP)])
    g_w_uq, g_w_ukv = _col_shards(g_uq).astype(BF16), _col_shards(g_ukv).astype(BF16)
    g_w_in = _col_shards(jnp.concatenate([g_p[:, HEADS * VD:IN_W], g_p[:, :HEADS * VD]], axis=1)).astype(BF16)

    dmod = jnp.stack([jnp.concatenate([dsh0, dsc0, dgt0], axis=-1)[:, 0], jnp.concatenate([dsh1, dsc1, dgt1], axis=-1)[:, 0]])
    gbig = {"mla_w_in": g_w_in, "mla_w_uq": g_w_uq, "mla_w_ukv": g_w_ukv}
    gsmall = {"norm_g": jnp.stack([both(dng0), both(dng1)]), "mla_q_norm": both(dqg), "mla_kv_norm": both(dkvg),
              "s5_a_re": g_a_re, "s5_a_im": g_a_im, "s5_log_step": g_ls, "final_g": dfg[1, 0]}
    return lvec[1], grad_x, dmod, gbig, gsmall, l1_recv


COL_SHARDED = ("mla_w_in", "mla_w_uq", "mla_w_ukv", "s5_w_in")
ROW_SHARDED = ("mla_w_out", "s5_w_glu", "s5_w_out")
VEC_SHARDED = ("s5_d", "s5_b_glu")
BIG = COL_SHARDED + ROW_SHARDED
L0_BIG = ("mla_w_in", "mla_w_uq", "mla_w_ukv")
L1_BIG = ("s5_w_in", "s5_w_glu", "s5_w_out", "mla_w_out")
BITS16 = jnp.bfloat16
SMALL_RS = ("norm_g", "mla_q_norm", "mla_kv_norm", "s5_a_re", "s5_a_im", "s5_log_step", "s5_b_re", "s5_b_im",
            "s5_c_re", "s5_c_im", "final_g")
CHUNKED = ("s5_b_re", "s5_b_im", "s5_c_re", "s5_c_im")
DENSE = ("s5_b_re", "s5_b_im")
TINY = ("norm_g", "mla_q_norm", "mla_kv_norm", "s5_a_re", "s5_a_im", "s5_log_step", "final_g")
ORDER = ("c_ctx", "ada_w", "ada_b", "norm_g", "mla_w_in", "mla_q_norm", "mla_w_uq", "mla_kv_norm", "mla_w_ukv",
         "mla_w_out", "s5_w_in", "s5_a_re", "s5_a_im", "s5_log_step", "s5_b_re", "s5_b_im", "s5_c_re", "s5_c_im",
         "s5_d", "s5_w_glu", "s5_b_glu", "s5_w_out", "final_g")


def kernel(x, c, ctx, c_ctx, ada_w, ada_b, norm_g, mla_w_in, mla_q_norm, mla_w_uq, mla_kv_norm, mla_w_ukv, mla_w_out, s5_w_in, s5_a_re, s5_a_im, s5_log_step, s5_b_re, s5_b_im, s5_c_re, s5_c_im, s5_d, s5_w_glu, s5_b_glu, s5_w_out, final_g, loss_target, m_c_ctx, m_ada_w, m_ada_b, m_norm_g, m_mla_w_in, m_mla_q_norm, m_mla_w_uq, m_mla_kv_norm, m_mla_w_ukv, m_mla_w_out, m_s5_w_in, m_s5_a_re, m_s5_a_im, m_s5_log_step, m_s5_b_re, m_s5_b_im, m_s5_c_re, m_s5_c_im, m_s5_d, m_s5_w_glu, m_s5_b_glu, m_s5_w_out, m_final_g, v_c_ctx, v_ada_w, v_ada_b, v_norm_g, v_mla_w_in, v_mla_q_norm, v_mla_w_uq, v_mla_kv_norm, v_mla_w_ukv, v_mla_w_out, v_s5_w_in, v_s5_a_re, v_s5_a_im, v_s5_log_step, v_s5_b_re, v_s5_b_im, v_s5_c_re, v_s5_c_im, v_s5_d, v_s5_w_glu, v_s5_b_glu, v_s5_w_out, v_final_g):
    w = dict(c_ctx=c_ctx, ada_w=ada_w, ada_b=ada_b, norm_g=norm_g, mla_w_in=mla_w_in, mla_q_norm=mla_q_norm,
             mla_w_uq=mla_w_uq, mla_kv_norm=mla_kv_norm, mla_w_ukv=mla_w_ukv, mla_w_out=mla_w_out, s5_w_in=s5_w_in,
             s5_a_re=s5_a_re, s5_a_im=s5_a_im, s5_log_step=s5_log_step, s5_b_re=s5_b_re, s5_b_im=s5_b_im,
             s5_c_re=s5_c_re, s5_c_im=s5_c_im, s5_d=s5_d, s5_w_glu=s5_w_glu, s5_b_glu=s5_b_glu, s5_w_out=s5_w_out,
             final_g=final_g)
    m = dict(c_ctx=m_c_ctx, ada_w=m_ada_w, ada_b=m_ada_b, norm_g=m_norm_g, mla_w_in=m_mla_w_in, mla_q_norm=m_mla_q_norm,
             mla_w_uq=m_mla_w_uq, mla_kv_norm=m_mla_kv_norm, mla_w_ukv=m_mla_w_ukv, mla_w_out=m_mla_w_out,
             s5_w_in=m_s5_w_in, s5_a_re=m_s5_a_re, s5_a_im=m_s5_a_im, s5_log_step=m_s5_log_step, s5_b_re=m_s5_b_re,
             s5_b_im=m_s5_b_im, s5_c_re=m_s5_c_re, s5_c_im=m_s5_c_im, s5_d=m_s5_d, s5_w_glu=m_s5_w_glu,
             s5_b_glu=m_s5_b_glu, s5_w_out=m_s5_w_out, final_g=m_final_g)
    v = dict(c_ctx=v_c_ctx, ada_w=v_ada_w, ada_b=v_ada_b, norm_g=v_norm_g, mla_w_in=v_mla_w_in, mla_q_norm=v_mla_q_norm,
             mla_w_uq=v_mla_w_uq, mla_kv_norm=v_mla_kv_norm, mla_w_ukv=v_mla_w_ukv, mla_w_out=v_mla_w_out,
             s5_w_in=v_s5_w_in, s5_a_re=v_s5_a_re, s5_a_im=v_s5_a_im, s5_log_step=v_s5_log_step, s5_b_re=v_s5_b_re,
             s5_b_im=v_s5_b_im, s5_c_re=v_s5_c_re, s5_c_im=v_s5_c_im, s5_d=v_s5_d, s5_w_glu=v_s5_w_glu,
             s5_b_glu=v_s5_b_glu, s5_w_out=v_s5_w_out, final_g=v_final_g)

    me = 4 * lax.axis_index("x") + 2 * lax.axis_index("y") + lax.axis_index("c")
    WA = ada_w.shape[2]

    def shard(n):
        return _t_shard(w[n], SHARD_ROWS[n]) if n in COL_SHARDED else w[n][0].astype(BF16)

    wgot = exchange([c] + [shard(n) for n in L0_BIG], "gather", "gather_w")

    cg = wgot[0].reshape(NDEV, D)
    cc2 = c_ctx.reshape(1, D)
    ada_b_loc = lax.dynamic_slice_in_dim(ada_b.reshape(2, 3 * D // WA, WA), me, 1, axis=1)
    part = ada_fwd(cg, cc2, ada_w, ada_b_loc, "ada_fwd")
    pg = exchange([part], "gather", "gather_mod")[0]
    mod_l = lax.dynamic_index_in_dim(pg, me, axis=2, keepdims=False).transpose(1, 0, 2).reshape(2, 3 * D)
    mod_c = pg[:, :, NDEV, :].transpose(1, 0, 2).reshape(2, 3 * D)
    mod = jnp.stack([mod_c, mod_l], axis=1)

    Wt = {n: a.reshape(-1, a.shape[-1]) for n, a in zip(L0_BIG, wgot[1:])}
    Wt["mla_w_in"] = mm(_win_order(), Wt["mla_w_in"], "nn", "w_in_order", out_dtype=BF16)
    vec_bits = lax.bitcast_convert_type(jnp.concatenate([s5_d, s5_b_glu], axis=0), BITS16).reshape(2, -1)
    small = {n: w[n] for n in SMALL_RS}

    lvec, grad_x, dmod, gbig, gsmall, l1_recv = local_step(ctx[0], x[0], loss_target[0], mod, Wt, small,
                                                           [shard(n) for n in L1_BIG] + [vec_bits])
    grad_x = grad_x[None]

    per_dev = G // NDEV
    recv = dict(zip(L0_BIG, exchange([gbig[n] for n in L0_BIG], "lead", "scatter_grads")))
    recv.update(dict(zip(L1_BIG + VEC_SHARDED, l1_recv)))
    out = {}

    def keep(n, res):
        for key, arr in zip("gdmv", res):
            out[key, n] = arr.reshape(w[n].shape)

    for n in BIG:
        keep(n, adamw(recv[n], w[n][0], m[n][0], v[n][0], "adamw_" + n))
    reduced = sum_slots(l1_recv[len(L1_BIG + VEC_SHARDED):], "sum_chunks")

    kshape = lambda n: w[n].shape if w[n].ndim > 1 else (1, w[n].size)
    flat = jnp.concatenate([gsmall[n].reshape(-1) for n in TINY] + [dmod.reshape(-1), lvec.reshape(-1)])[None]
    bb_all, cc_all, flat_all = exchange([jnp.stack(reduced[:2]), jnp.stack(reduced[2:]), flat], "gather", "gather_small")
    chunk_all = [bb_all[:, 0], bb_all[:, 1], cc_all[:, 0], cc_all[:, 1]]
    tiny_all, off = [], 0
    for n in TINY:
        tiny_all.append(flat_all[:, 0, off:off + w[n].size].reshape((NDEV,) + kshape(n)))
        off += w[n].size
    dm_all = flat_all[:, 0, off:off + dmod.size].reshape((NDEV,) + dmod.shape)
    loss = sum_slots([flat_all[:, :, off + dmod.size:]], "loss_sum")[0][0, 0]

    dm_cols = lax.dynamic_slice_in_dim(dm_all.reshape(NDEV, 2, 2, 3 * D // WA, WA), me, 1, axis=3)[:, :, :, 0]
    dm_loc = jnp.concatenate([dm_cols[:, :, 1].transpose(1, 0, 2), dm_cols[:, :, 0].transpose(1, 0, 2)], axis=1)
    g_ada_w, dcc_part, g_ada_b = ada_bwd(cg, cc2, ada_w, dm_loc, dm_all.transpose(0, 2, 1, 3).reshape(2 * NDEV, 2, 3 * D), "ada_bwd")
    dcc_all = exchange([dcc_part], "gather", "gather_dcc")[0].reshape(NDEV, D)
    g_c_ctx = cctx_finish(dcc_all, cc2, "cctx_finish")

    flat2 = lambda t: t.reshape(-1, t.shape[-1])
    keep("ada_w", adamw(flat2(g_ada_w)[None], flat2(ada_w), flat2(m_ada_w), flat2(v_ada_w), "adamw_ada"))
    items = []
    halves = 2
    for n, g in zip(CHUNKED, chunk_all):
        blk = (1, 1, G // halves) + w[n].shape[3:]
        g = jnp.moveaxis(g, 0, 1).reshape(w[n].shape)
        g_spec = pl.BlockSpec((1,) + blk, lambda d, s: (0, 0, d, s, 0, 0))
        items.append((g[None], g_spec, w[n], m[n], v[n], pl.BlockSpec(blk, lambda d, s: (0, d, s, 0, 0))))
    for n, res in zip(CHUNKED, adamw_multi(items, (2, halves), "adamw_bc")):
        keep(n, res)
    tiny_g = dict(zip(TINY, tiny_all))
    tiny_g.update({n: recv[n] for n in VEC_SHARDED})
    tiny_g["c_ctx"], tiny_g["ada_b"] = g_c_ctx[None], g_ada_b[None]
    names = list(tiny_g)
    items = [(tiny_g[n], _whole(tiny_g[n], 1)) + tuple(t[n].reshape(kshape(n)) for t in (w, m, v))
             + (pl.BlockSpec(kshape(n), lambda i, r=len(kshape(n)): (0,) * r),) for n in names]
    for n, res in zip(names, adamw_multi(items, (1,), "adamw_small")):
        keep(n, res)

    return (loss, grad_x, *[out["g", n] for n in ORDER], *[out["d", n] for n in ORDER],
            *[out["m", n] for n in ORDER], *[out["v", n] for n in ORDER])
```

```python
import math

import numpy as np
import jax
import jax.numpy as jnp
from jax import lax
from jax.experimental import pallas as pl
from jax.experimental.pallas import tpu as pltpu

F32 = jnp.float32
BF16 = jnp.bfloat16

D = 1024
L = 2048
LC = 256
NDEV = 8
GRID_W = 64
EPS = 1e-6
HEADS = 16
NOPE = 64
ROPE = 32
QK = NOPE + ROPE
VD = 64
IN_W = 256 + 128 + ROPE + HEADS * 64
IN_WP = 1536
QL = 256
KVL = 128
SCALE = QK ** -0.5
LOG2E = math.log2(math.e)
THETA = 10000.0
G = 64
P = 64
CH = 16
GB = 8
NJ = G // GB
UB = GB * CH
SB = GB * P
SEG = 16
TB = 256
VMEM_LIMIT = 56 * 1024 * 1024
B1, B2, LR, AEPS, WD, STEP = 0.9, 0.999, 0.001, 1e-8, 0.01, 10
MESH_T = pl.DeviceIdType.MESH


def _cp(sem=None):
    return pltpu.CompilerParams(dimension_semantics=sem, vmem_limit_bytes=VMEM_LIMIT)


def _sig(x):
    return 1.0 / (1.0 + jnp.exp(-x))


def _silu(x):
    return x * _sig(x)


def _dsilu(x):
    s = _sig(x)
    return s * (1.0 + x * (1.0 - s))


_GK = math.sqrt(2.0 / math.pi)


def _gelu(x):
    return 0.5 * x * (1.0 + jnp.tanh(_GK * (x + 0.044715 * x * x * x)))


def _dgelu(x):
    t = jnp.tanh(_GK * (x + 0.044715 * x * x * x))
    return 0.5 * (1.0 + t) + 0.5 * x * (1.0 - t * t) * _GK * (1.0 + 3 * 0.044715 * x * x)


def _rs(x):
    return lax.rsqrt(jnp.mean(x * x, axis=-1, keepdims=True) + EPS)


def _sum0(x):
    return jnp.sum(x, axis=0, keepdims=True)


def st_norm_mod(x, g, sc, sh):
    y = x * _rs(x) * g
    return (y * (1.0 + sc) + sh,), ()


def st_norm_mod_bwd(x, dh, dres, g, sc):
    r = _rs(x)
    xn = x * r
    y = xn * g
    dy = dh * (1.0 + sc)
    dxn = dy * g
    dx = r * (dxn - xn * jnp.mean(dxn * xn, axis=-1, keepdims=True))
    return (dres + dx,), (_sum0(dh), _sum0(dh * y), _sum0(dy * xn))


def st_rms(x, g):
    return (x * _rs(x) * g,), ()


def st_rms_bwd(x, dy, g):
    r = _rs(x)
    n = x * r
    dn = dy * g
    dx = r * (dn - n * jnp.mean(dn * n, axis=-1, keepdims=True))
    return (dx,), (_sum0(dy * n),)


def st_rms2(x1, x2, g1, g2):
    return st_rms(x1, g1)[0] + st_rms(x2, g2)[0], ()


def st_rms2_bwd(x1, dy1, x2, dy2, g1, g2):
    (d1,), (s1,) = st_rms_bwd(x1, dy1, g1)
    (d2,), (s2,) = st_rms_bwd(x2, dy2, g2)
    return (d1, d2), (s1, s2)


def st_gate(o, z):
    return (o * _silu(z),), ()


def st_gate_bwd(dog, o, z):
    return (dog * _silu(z), dog * o * _dsilu(z)), ()


def st_resid(x, out, gt):
    return (x + gt * out,), ()


def st_resid_bwd(dx, out, gt):
    return (dx * gt,), (_sum0(dx * out),)


def st_s5a(yssm, u, d):
    y = yssm + d * u
    return (y, _gelu(y)), ()


def st_s5b(y, gl, z, b):
    return (_gelu(y) * _sig(gl + b) * _silu(z),), ()


def st_s5b_bwd(dy3, y, gl, z, b):
    y1 = _gelu(y)
    s = _sig(gl + b)
    dy2 = dy3 * _silu(z)
    dz = dy3 * y1 * s * _dsilu(z)
    dgl = dy2 * y1 * s * (1.0 - s)
    return (dgl, dz, dy2 * s), (_sum0(dgl),)


def st_s5a_bwd(dy1a, dy1b, y, u, d):
    dy = (dy1a + dy1b) * _dgelu(y)
    return (dy, dy * d), (_sum0(dy * u),)


def st_l0_pre(x, g, sc, sh, qg, kvg, w_in):
    hb = st_norm_mod(x, g, sc, sh)[0][0].astype(BF16)
    p = lax.dot_general(hb, w_in, _DN["nt"], preferred_element_type=F32)
    cq, ckv = p[:, HEADS * VD:HEADS * VD + QL], p[:, HEADS * VD + QL:HEADS * VD + QL + KVL]
    return (hb, p) + st_rms2(cq, ckv, qg, kvg)[0], ()


def st_l0_tail_bwd(dq, dkv, dkr, dz, cq, ckv, cqn, ckvn, h, x, dres, qg, kvg, g, sc, w_uq, w_ukv, w_in):
    dcqn = jnp.dot(dq, w_uq, preferred_element_type=F32)
    dckvn = jnp.dot(dkv, w_ukv, preferred_element_type=F32)
    (dcq, dckv), (dqg, dkvg) = st_rms2_bwd(cq, dcqn, ckv, dckvn, qg, kvg)
    dp = jnp.concatenate([dz, dcq, dckv, dkr], axis=1).astype(BF16)
    dh = jnp.dot(dp, w_in, preferred_element_type=F32)
    outs, sums = st_norm_mod_bwd(x, dh, dres, g, sc)
    tn = lambda a, b: lax.dot_general(a, b, _DN["tn"], preferred_element_type=F32)
    return outs, (dqg, dkvg) + sums, (tn(cqn, dq), tn(ckvn, dkv), tn(h, dp))


def st_l1_pre(x, g, sc, sh, w_in):
    hb = st_norm_mod(x, g, sc, sh)[0][0].astype(BF16)
    return (hb, lax.dot_general(hb, w_in, _DN["nt"], preferred_element_type=F32)), ()


def st_l1_tail_bwd(du_a, du_b, dz, h, x, dres, g, sc, w_in):
    dp = jnp.concatenate([(du_a + du_b).astype(BF16), dz], axis=1)
    dh = jnp.dot(dp, w_in, preferred_element_type=F32)
    outs, sums = st_norm_mod_bwd(x, dh, dres, g, sc)
    return outs, sums, (lax.dot_general(h, dp, _DN["tn"], preferred_element_type=F32),)


def st_l0_post(o, z, x, gt, w_out):
    og = (o * _silu(z)).astype(BF16)
    out = jnp.dot(og, w_out, preferred_element_type=F32)
    return (og, out, x + gt * out), ()


def st_l0_post_bwd(dx1, out, og, o, z, gt, w_out):
    (dout,), (dgt,) = st_resid_bwd(dx1, out.astype(F32), gt)
    doutb = dout.astype(BF16)
    dog = lax.dot_general(doutb, w_out, _DN["nt"], preferred_element_type=F32)
    return st_gate_bwd(dog, o, z)[0], (dgt,), (lax.dot_general(og, doutb, _DN["tn"], preferred_element_type=F32),)


def st_l1_mlp(yssm, u, z, x1, tgt, d, bglu, gt, fg, mask, w_glu, w_out):
    (y, y1), _ = st_s5a(yssm, u, d)
    y1b = y1.astype(BF16)
    gl = jnp.dot(y1b, w_glu, preferred_element_type=F32)
    y3 = (y1 * _sig(gl + bglu) * _silu(z)).astype(BF16)
    out = jnp.dot(y3, w_out, preferred_element_type=F32)
    (dx2,), sums = st_final(x1 + gt * out, tgt, fg, mask)
    return (y, y1b, gl, y3, out, dx2), sums


def st_l1_mlp_bwd(dx2, out, y3, y, gl, z, u, y1b, gt, bglu, d, w_out, w_glu):
    out, gl = out.astype(F32), gl.astype(F32)
    (dout,), (dgt,) = st_resid_bwd(dx2, out, gt)
    doutb = dout.astype(BF16)
    dy3 = lax.dot_general(doutb, w_out, _DN["nt"], preferred_element_type=F32)
    (dgl, dz, dy1a), (dbglu,) = st_s5b_bwd(dy3, y, gl, z, bglu)
    dglb = dgl.astype(BF16)
    dy1b = lax.dot_general(dglb, w_glu, _DN["nt"], preferred_element_type=F32)
    (dy, du), (dd,) = st_s5a_bwd(dy1a, dy1b, y, u, d)
    g_w_out = lax.dot_general(y3, doutb, _DN["tn"], preferred_element_type=F32)
    g_w_glu = lax.dot_general(y1b, dglb, _DN["tn"], preferred_element_type=F32)
    return (dz, dy, du), (dgt, dbglu, dd), (g_w_out, g_w_glu)


def st_final(x2, tgt, g, mask):
    r = _rs(x2)
    n = x2 * r
    e = n * g - tgt
    dyo = e * (1.0 / D)
    dn = dyo * g
    dx = r * (dn - n * jnp.mean(dn * n, axis=-1, keepdims=True))
    lsum = jnp.sum(_sum0(e * e), axis=1, keepdims=True) * (0.5 / D)
    return (dx * mask,), (_sum0(dyo * n), jnp.broadcast_to(lsum, (1, 128)))


def rowwise(fn, rows, vecs, out_rows, out_sums, name, mats=(), out_accs=()):
    lat_blk = lambda i: jnp.maximum(i - 1, 0)
    arrays, in_specs, pick = [], [], []
    for a in rows:
        if not isinstance(a, tuple):
            a = (a, 0, a.shape[1])
        tag = a[0] if isinstance(a[0], str) else None
        if tag == "cat":
            _, ctx, x = a
            arrays += [ctx, x]
            in_specs += [pl.BlockSpec((TB, ctx.shape[1]), lambda i: (0, 0)),
                         pl.BlockSpec((TB, x.shape[1]), lambda i: (lat_blk(i), 0))]
            pick.append(2)
        elif tag == "lat":
            arrays.append(a[1])
            in_specs.append(pl.BlockSpec((TB, a[1].shape[1]), lambda i: (lat_blk(i), 0)))
            pick.append(1)
        else:
            arr, cb, width = a
            arrays.append(arr)
            in_specs.append(pl.BlockSpec((TB, width), lambda i, cb=cb: (i, cb)))
            pick.append(1)
    T = LC + L
    nin, nv, nm, no, ns = len(arrays), len(vecs), len(mats), len(out_rows), len(out_sums)

    def body(*refs):
        i = pl.program_id(0)
        vals, k = [], 0
        for p in pick:
            if p == 2:
                vals.append(jnp.where(i == 0, refs[k][...], refs[k + 1][...]))
            else:
                vals.append(refs[k][...])
            k += p
        vals += [r[0] for r in refs[nin:nin + nv]] + [r[...] for r in refs[nin + nv:nin + nv + nm]]
        res = fn(*vals)
        first_out = nin + nv + nm
        for r, o in zip(refs[first_out:first_out + no], res[0]):
            r[...] = o.astype(r.dtype)
        sum_refs = refs[first_out + no:first_out + no + ns]
        if sum_refs:
            @pl.when(i <= 1)
            def _():
                for r in sum_refs:
                    r[...] = jnp.zeros_like(r)
            for r, s in zip(sum_refs, res[1]):
                r[0] += s
        na = len(out_accs)
        if na:
            acc_out, acc = refs[first_out + no + ns:first_out + no + ns + na], refs[first_out + no + ns + na:]

            @pl.when(i == 0)
            def _():
                for r in acc:
                    r[...] = jnp.zeros_like(r)
            for r, a in zip(acc, res[2]):
                r[...] += a

            @pl.when(i == T // TB - 1)
            def _():
                for o, r in zip(acc_out, acc):
                    o[...] = r[...].astype(o.dtype)

    kind = lambda i: (jnp.minimum(i, 1), 0, 0)
    in_specs += [pl.BlockSpec((1, 1, v.shape[2]), kind) for v in vecs]
    in_specs += [pl.BlockSpec(m.shape, lambda i: (0, 0), pipeline_mode=pl.Buffered(1)) for m in mats]
    out_specs, out_shape = [], []
    for o in out_rows:
        lat = len(o) == 3
        out_specs.append(pl.BlockSpec((TB, o[0]), (lambda i: (lat_blk(i), 0)) if lat else (lambda i: (i, 0))))
        out_shape.append(jax.ShapeDtypeStruct((L if lat else T, o[0]), o[1]))
    out_specs += [pl.BlockSpec((1, 1, c), kind) for c in out_sums]
    out_shape += [jax.ShapeDtypeStruct((2, 1, c), F32) for c in out_sums]
    out_specs += [pl.BlockSpec(s, lambda i: (0, 0)) for s in out_accs]
    out_shape += [jax.ShapeDtypeStruct(s, BF16) for s in out_accs]
    res = pl.pallas_call(body, grid=(T // TB,), in_specs=in_specs, out_specs=out_specs, out_shape=out_shape,
                         scratch_shapes=[pltpu.VMEM(s, F32) for s in out_accs],
                         compiler_params=_cp(("arbitrary",)), name=name)(*arrays, *vecs, *mats)
    if out_accs:
        return res[:no], res[no:no + ns], res[no + ns:]
    return res[:no], res[no:]


_DN = {"nn": (((1,), (0,)), ((), ())), "nt": (((1,), (1,)), ((), ())), "tn": (((0,), (0,)), ((), ()))}


def mm(a, b, mode, name, out_dtype=F32, tm=None, tn=None, shard_out=False):
    if mode == "nn":
        (M, K), (_, N) = a.shape, b.shape
    elif mode == "nt":
        (M, K), (N, _) = a.shape, b.shape
    else:
        (K, M), (_, N) = a.shape, b.shape
    if tm is None:
        tm = next((t for t in (768, 512, 256) if M % t == 0 and M > t), M)
    tn = N if tn is None else tn
    dn = _DN[mode]

    def body(a_ref, b_ref, o_ref):
        o_ref[...] = lax.dot_general(a_ref[...].astype(BF16), b_ref[...].astype(BF16), dn,
                                     preferred_element_type=F32).astype(o_ref.dtype)

    if shard_out:
        def body(a_ref, b_ref, o_ref):
            av = a_ref[...].astype(BF16)
            for j in range(N // tn):
                bj = b_ref[pl.ds(j * tn, tn), :] if mode == "nt" else b_ref[:, pl.ds(j * tn, tn)]
                o_ref[j] = lax.dot_general(av, bj.astype(BF16), dn, preferred_element_type=F32).astype(o_ref.dtype)

        a_spec = pl.BlockSpec((K, tm), lambda i: (0, i)) if mode == "tn" else pl.BlockSpec((tm, K), lambda i: (i, 0))
        return pl.pallas_call(body, grid=(M // tm,), in_specs=[a_spec, pl.BlockSpec(b.shape, lambda i: (0, 0))],
                              out_specs=pl.BlockSpec((N // tn, tm, tn), lambda i: (0, i, 0)),
                              out_shape=jax.ShapeDtypeStruct((N // tn, M, tn), out_dtype),
                              compiler_params=_cp(("parallel",)), name=name)(a, b)
    a_spec = pl.BlockSpec((K, tm), lambda i, j: (0, i)) if mode == "tn" else pl.BlockSpec((tm, K), lambda i, j: (i, 0))
    b_spec = pl.BlockSpec((tn, K), lambda i, j: (j, 0)) if mode == "nt" else pl.BlockSpec((K, tn), lambda i, j: (0, j))
    return pl.pallas_call(body, grid=(M // tm, N // tn), in_specs=[a_spec, b_spec],
                          out_specs=pl.BlockSpec((tm, tn), lambda i, j: (i, j)), out_shape=jax.ShapeDtypeStruct((M, N), out_dtype),
                          compiler_params=_cp(("parallel", "arbitrary")), name=name)(a, b)


def _rope_tables(T, width=QK, first=NOPE):
    nlat = T - LC
    pos = np.arange(nlat)
    row, col = pos // GRID_W, pos % GRID_W
    half = ROPE // 2
    inv = 1.0 / (THETA ** (np.arange(0, half, 2, dtype=np.float64) / half))
    cosf = np.ones((T, width), np.float64)
    sinf = np.zeros((T, width), np.float64)
    perm = np.zeros((width, width), np.float32)
    for m in range(ROPE):
        j = first + m
        blk, w = m // half, m % half
        ang = (row if blk == 0 else col)[:, None] * inv[None, :]
        f = w % (half // 2)
        cosf[LC:, j] = np.cos(ang[:, f])
        if w < half // 2:
            sinf[LC:, j] = -np.sin(ang[:, f])
            perm[j + half // 2, j] = 1.0
        else:
            sinf[LC:, j] = np.sin(ang[:, f])
            perm[j - half // 2, j] = 1.0
    return jnp.asarray(cosf, F32), jnp.asarray(sinf, F32), jnp.asarray(perm, BF16), jnp.asarray(perm.T, BF16)


def _exact_perm(x, pm):
    hi = x.astype(BF16)
    r1 = x - hi.astype(F32)
    mid = r1.astype(BF16)
    lo = (r1 - mid.astype(F32)).astype(BF16)
    dot = lambda a: jnp.dot(a, pm, preferred_element_type=F32)
    return dot(hi) + dot(mid) + dot(lo)


def _rot(x, cv, sv, pv, inverse):
    if inverse:
        return x * cv + _exact_perm(x * sv, pv)
    return x * cv + _exact_perm(x, pv) * sv


def rope(x, cosf, sinf, pm, inverse, out_dtype, name, scale=1.0):
    H, T, _ = x.shape

    def body(x_ref, c_ref, s_ref, p_ref, o_ref):
        cv, sv, pv = c_ref[...], s_ref[...], p_ref[...]
        for h in range(H):
            o_ref[h] = (_rot(x_ref[h], cv, sv, pv, inverse) * scale).astype(o_ref.dtype)

    return pl.pallas_call(
        body, grid=(T // TB,),
        in_specs=[pl.BlockSpec((H, TB, QK), lambda i: (0, i, 0)), pl.BlockSpec((TB, QK), lambda i: (i, 0)),
                  pl.BlockSpec((TB, QK), lambda i: (i, 0)), pl.BlockSpec((QK, QK), lambda i: (0, 0))],
        out_specs=pl.BlockSpec((H, TB, QK), lambda i: (0, i, 0)), out_shape=jax.ShapeDtypeStruct((H, T, QK), out_dtype),
        compiler_params=_cp(("parallel",)), name=name)(x, cosf, sinf, pm)


KVW = NOPE + VD


def _kv_selectors():
    s_kn = np.zeros((KVW, QK), np.float32)
    s_kr = np.zeros((128, QK), np.float32)
    s_v = np.zeros((KVW, VD), np.float32)
    for l in range(NOPE):
        s_kn[l, l] = 1.0
    for l in range(ROPE):
        s_kr[l, NOPE + l] = 1.0
    for l in range(VD):
        s_v[NOPE + l, l] = 1.0
    return s_kn, s_kr, s_v


def project_q(cqn, w, cosf, sinf, pm, name):
    T = cqn.shape[0]

    def body(a_ref, w_ref, c_ref, s_ref, p_ref, o_ref):
        a, cv, sv, pv = a_ref[...], c_ref[...], s_ref[...], p_ref[...]
        for h in range(HEADS):
            qh = _dotf(a, w_ref[pl.ds(h * QK, QK), :], "nt")
            o_ref[h] = (_rot(qh, cv, sv, pv, False) * (SCALE * LOG2E)).astype(BF16)

    rows = lambda c: pl.BlockSpec((TB, c), lambda i: (i, 0))
    const = lambda x: pl.BlockSpec(x.shape, lambda i: (0, 0))
    return pl.pallas_call(
        body, grid=(T // TB,), in_specs=[rows(QL), const(w), rows(QK), rows(QK), const(pm)],
        out_specs=pl.BlockSpec((HEADS, TB, QK), lambda i: (0, i, 0)), out_shape=jax.ShapeDtypeStruct((HEADS, T, QK), BF16),
        compiler_params=_cp(("parallel",)), name=name)(cqn, w, cosf, sinf, pm)


def project_kv(ckvn, w, p0, kr_block, name):
    T = ckvn.shape[0]
    cosf, sinf, pm, _ = _rope_tables(T, 128, 0)
    s_kn, s_kr, s_v = (jnp.asarray(s, BF16) for s in _kv_selectors())

    def body(a_ref, w_ref, kr_ref, c_ref, s_ref, p_ref, skn_ref, skr_ref, sv_ref, k_ref, v_ref):
        a = a_ref[...]
        krr = _rot(kr_ref[...], c_ref[...], s_ref[...], p_ref[...], False).astype(BF16)
        kr_part = jnp.dot(krr, skr_ref[...], preferred_element_type=F32)
        for h in range(HEADS):
            kvb = _dotf(a, w_ref[pl.ds(h * KVW, KVW), :], "nt").astype(BF16)
            k_ref[h] = (jnp.dot(kvb, skn_ref[...], preferred_element_type=F32) + kr_part).astype(BF16)
            v_ref[h] = jnp.dot(kvb, sv_ref[...], preferred_element_type=F32).astype(BF16)

    rows = lambda c: pl.BlockSpec((TB, c), lambda i: (i, 0))
    const = lambda x: pl.BlockSpec(x.shape, lambda i: (0, 0))
    return pl.pallas_call(
        body, grid=(T // TB,),
        in_specs=[rows(KVL), const(w), pl.BlockSpec((TB, 128), lambda i: (i, kr_block)),
                  rows(128), rows(128), const(pm), const(s_kn), const(s_kr), const(s_v)],
        out_specs=[pl.BlockSpec((HEADS, TB, QK), lambda i: (0, i, 0)), pl.BlockSpec((HEADS, TB, VD), lambda i: (0, i, 0))],
        out_shape=[jax.ShapeDtypeStruct((HEADS, T, QK), BF16), jax.ShapeDtypeStruct((HEADS, T, VD), BF16)],
        compiler_params=_cp(("parallel",)), name=name)(ckvn, w, p0, cosf, sinf, pm, s_kn, s_kr, s_v)


def split_kv_grads(dk, dv, name):
    H, T, _ = dk.shape
    cosf, sinf, _, pmt = _rope_tables(T, 128, 0)
    s_kn, s_kr, s_v = _kv_selectors()
    s_knt, s_krt, s_vt = (jnp.asarray(s.T, BF16) for s in (s_kn, s_kr, s_v))

    def body(dk_ref, dv_ref, c_ref, s_ref, p_ref, skn_ref, skr_ref, sv_ref, dkv_ref, dkr_ref):
        total = None
        for h in range(H):
            dkh = dk_ref[h] * (1.0 / LOG2E)
            total = dkh if total is None else total + dkh
            dkv_ref[:, pl.ds(h * KVW, KVW)] = (
                jnp.dot(dkh.astype(BF16), skn_ref[...], preferred_element_type=F32)
                + jnp.dot(dv_ref[h].astype(BF16), sv_ref[...], preferred_element_type=F32)).astype(BF16)
        dkr_ref[...] = _rot(_exact_perm(total, skr_ref[...]), c_ref[...], s_ref[...], p_ref[...], True)

    rows = lambda c: pl.BlockSpec((TB, c), lambda i: (i, 0))
    const = lambda a: pl.BlockSpec(a.shape, lambda i: (0, 0))
    return pl.pallas_call(
        body, grid=(T // TB,),
        in_specs=[pl.BlockSpec((H, TB, QK), lambda i: (0, i, 0)), pl.BlockSpec((H, TB, VD), lambda i: (0, i, 0)),
                  rows(128), rows(128), const(pmt), const(s_knt), const(s_krt), const(s_vt)],
        out_specs=[rows(H * KVW), rows(128)],
        out_shape=[jax.ShapeDtypeStruct((T, H * KVW), BF16), jax.ShapeDtypeStruct((T, 128), F32)],
        compiler_params=_cp(("parallel",)), name=name)(dk, dv, cosf, sinf, pmt, s_knt, s_krt, s_vt)


HB = 4


def _by_query_block(run, T):
    @pl.when(pl.program_id(1) == 0)
    def _():
        run(LC)

    @pl.when(pl.program_id(1) > 0)
    def _():
        run(T)


def _with_rider(body, nin, nout, ride, grid):
    if ride is None:
        return body
    n = ride.n

    def wrapped(*refs):
        ins, xs = refs[:nin], refs[nin:nin + n]
        outs, got = refs[nin + n:nin + n + nout], refs[nin + n + nout:nin + 2 * n + nout]
        sems = refs[nin + 2 * n + nout:]
        step = pl.program_id(0) * grid[1] + pl.program_id(1)

        @pl.when(step == 0)
        def _():
            ride.start(xs, got, sems)

        body(*ins, *outs)

        @pl.when(step == grid[0] * grid[1] - 1)
        def _():
            ride.finish(xs, got, sems)

    return wrapped


def _ride_call(body, grid, in_specs, out_specs, out_shape, ride, rode, name, args):
    if ride is None:
        return pl.pallas_call(body, grid=grid, in_specs=in_specs, out_specs=out_specs, out_shape=out_shape,
                              compiler_params=_cp(("parallel", "arbitrary")), name=name)(*args), []
    res = pl.pallas_call(
        _with_rider(body, len(in_specs), len(out_specs), ride, grid), grid=grid,
        in_specs=in_specs + ride.specs, out_specs=out_specs + ride.specs, out_shape=out_shape + ride.out_shape,
        scratch_shapes=ride.scratch,
        compiler_params=pltpu.CompilerParams(dimension_semantics=("arbitrary", "arbitrary"), vmem_limit_bytes=VMEM_LIMIT,
                                             has_side_effects=True), name=name)(*args, *rode)
    return res[:len(out_specs)], res[len(out_specs):]


def attn_fwd(q, k, v, name, rode=None, modes=None):
    H, T, _ = q.shape

    def body(q_ref, k_ref, v_ref, o_ref, lse_ref):
        def run(nk):
            for hh in range(HB):
                s = _dotf(q_ref[hh], k_ref[hh, pl.ds(0, nk), :], "nt")
                m = jnp.max(s, axis=1, keepdims=True)
                p = jnp.exp2(s - m)
                l = jnp.sum(p, axis=1, keepdims=True)
                o = jnp.dot(p.astype(BF16), v_ref[hh, pl.ds(0, nk), :], preferred_element_type=F32)
                o_ref[hh] = o / l
                lse_ref[hh] = m + jnp.log2(l)

        _by_query_block(run, T)

    return _ride_call(
        body, (H // HB, T // TB),
        [pl.BlockSpec((HB, TB, QK), lambda h, i: (h, i, 0)), pl.BlockSpec((HB, T, QK), lambda h, i: (h, 0, 0)),
         pl.BlockSpec((HB, T, VD), lambda h, i: (h, 0, 0))],
        [pl.BlockSpec((HB, TB, VD), lambda h, i: (h, i, 0)), pl.BlockSpec((HB, TB, 1), lambda h, i: (h, i, 0))],
        [jax.ShapeDtypeStruct((H, T, VD), F32), jax.ShapeDtypeStruct((H, T, 1), F32)],
        Exchange(rode, modes) if rode else None, rode, name, (q, k, v))


def attn_bwd(q, k, v, o, lse, do, name, rode=None, modes=None):
    H, T, _ = q.shape

    def body(q_ref, k_ref, v_ref, o_ref, lse_ref, do_ref, dq_ref, dk_ref, dv_ref):
        i = pl.program_id(1)

        @pl.when(i == 0)
        def _():
            dk_ref[...] = jnp.zeros_like(dk_ref)
            dv_ref[...] = jnp.zeros_like(dv_ref)

        def run(nk):
            keys = pl.ds(0, nk)
            for hh in range(HB):
                qv, kv, dov = q_ref[hh], k_ref[hh, keys, :], do_ref[hh]
                p = jnp.exp2(_dotf(qv, kv, "nt") - lse_ref[hh])
                delta = jnp.sum(dov * o_ref[hh], axis=1, keepdims=True)
                dob = dov.astype(BF16)
                dv_ref[hh, keys, :] += _dotf(p.astype(BF16), dob, "tn")
                dp = _dotf(dob, v_ref[hh, keys, :], "nt")
                ds = (p * (dp - delta)).astype(BF16)
                dq_ref[hh] = jnp.dot(ds, kv, preferred_element_type=F32)
                dk_ref[hh, keys, :] += _dotf(ds, qv, "tn")

        _by_query_block(run, T)

    blk = lambda c: pl.BlockSpec((HB, TB, c), lambda h, i: (h, i, 0))
    full = lambda c: pl.BlockSpec((HB, T, c), lambda h, i: (h, 0, 0))
    return _ride_call(
        body, (H // HB, T // TB), [blk(QK), full(QK), full(VD), blk(VD), blk(1), blk(VD)], [blk(QK), full(QK), full(VD)],
        [jax.ShapeDtypeStruct((H, T, QK), F32), jax.ShapeDtypeStruct((H, T, QK), F32), jax.ShapeDtypeStruct((H, T, VD), F32)],
        Exchange(rode, modes) if rode else None, rode, name, (q, k, v, o, lse, do))


def disc_fwd(a_re, a_im, ls, name):
    def body(ar_ref, ai_ref, ls_ref, lr_ref, li_ref, fr_ref, fi_ref):
        ar, ai = ar_ref[...], ai_ref[...]
        dt = jnp.exp(ls_ref[...])
        mag = jnp.exp(ar * dt)
        lr = mag * jnp.cos(ai * dt)
        li = mag * jnp.sin(ai * dt)
        den = ar * ar + ai * ai
        nr = lr - 1.0
        lr_ref[...] = lr
        li_ref[...] = li
        fr_ref[...] = (nr * ar + li * ai) / den
        fi_ref[...] = (li * ar - nr * ai) / den

    return pl.pallas_call(body, out_shape=[jax.ShapeDtypeStruct(a_re.shape, F32)] * 4, name=name)(a_re, a_im, ls)


def disc_b(f_re, f_im, b_re, b_im, name):
    def body(fr_ref, fi_ref, br_ref, bi_ref, or_ref, oi_ref):
        fr, fi, br, bi = fr_ref[...], fi_ref[...], br_ref[...], bi_ref[...]
        or_ref[...] = fr * br - fi * bi
        oi_ref[...] = fr * bi + fi * br

    fs, bs = _disc_b_specs()
    return pl.pallas_call(body, grid=(2, G * P // DISC_ROWS), in_specs=[fs, fs, bs, bs], out_specs=[bs, bs],
                          out_shape=[jax.ShapeDtypeStruct(b_re.shape, F32)] * 2, name=name)(f_re, f_im, b_re, b_im)


DISC_ROWS = G * P


def _disc_b_specs():
    return (pl.BlockSpec((1, DISC_ROWS, 1), lambda d, i: (d, i, 0)), pl.BlockSpec((1, DISC_ROWS, CH), lambda d, i: (d, i, 0)))


def disc_b_bwd(f_re, f_im, b_re, b_im, dbb_re, dbb_im, name):
    def body(fr_ref, fi_ref, br_ref, bi_ref, dr_ref, di_ref, dbr_ref, dbi_ref, dfr_ref, dfi_ref):
        fr, fi, br, bi, dr, di = fr_ref[...], fi_ref[...], br_ref[...], bi_ref[...], dr_ref[...], di_ref[...]
        dbr_ref[...] = fr * dr + fi * di
        dbi_ref[...] = fr * di - fi * dr
        dfr_ref[...] = jnp.sum(dr * br + di * bi, axis=-1, keepdims=True)
        dfi_ref[...] = jnp.sum(di * br - dr * bi, axis=-1, keepdims=True)

    fs, bs = _disc_b_specs()
    return pl.pallas_call(body, grid=(2, G * P // DISC_ROWS), in_specs=[fs, fs, bs, bs, bs, bs], out_specs=[bs, bs, fs, fs],
                          out_shape=[jax.ShapeDtypeStruct(b_re.shape, F32)] * 2 + [jax.ShapeDtypeStruct(f_re.shape, F32)] * 2,
                          name=name)(f_re, f_im, b_re, b_im, dbb_re, dbb_im)


def disc_a_bwd(a_re, a_im, ls, dlr, dli, dfr, dfi, name):
    def body(ar_ref, ai_ref, ls_ref, dlr_ref, dli_ref, dfr_ref, dfi_ref, dar_ref, dai_ref, dls_ref):
        ar, ai = ar_ref[...], ai_ref[...]
        dt = jnp.exp(ls_ref[...])
        mag = jnp.exp(ar * dt)
        cs, sn = jnp.cos(ai * dt), jnp.sin(ai * dt)
        lr, li = mag * cs, mag * sn
        den = ar * ar + ai * ai
        nr = lr - 1.0
        f_re = (nr * ar + li * ai) / den
        f_im = (li * ar - nr * ai) / den
        dn1 = dfr_ref[...] / den
        dn2 = dfi_ref[...] / den
        dden = -(dfr_ref[...] * f_re + dfi_ref[...] * f_im) / den
        dlr_t = dlr_ref[...] + dn1 * ar - dn2 * ai
        dli_t = dli_ref[...] + dn1 * ai + dn2 * ar
        dar = dn1 * nr + dn2 * li + dden * 2.0 * ar
        dai = dn1 * li - dn2 * nr + dden * 2.0 * ai
        dmag = dlr_t * cs + dli_t * sn
        dth = dli_t * lr - dlr_t * li
        dar_ref[...] = dar + dmag * mag * dt
        dai_ref[...] = dai + dth * dt
        dls_ref[...] = jnp.sum(dmag * mag * ar + dth * ai, axis=-1, keepdims=True) * dt

    return pl.pallas_call(body, out_shape=[jax.ShapeDtypeStruct(a_re.shape, F32)] * 2 +
                          [jax.ShapeDtypeStruct(ls.shape, F32)], name=name)(a_re, a_im, ls, dlr, dli, dfr, dfi)


def _cpow(lr, li, n):
    rr, ri = None, None
    br, bi = lr, li
    while n:
        if n & 1:
            if rr is None:
                rr, ri = br, bi
            else:
                rr, ri = rr * br - ri * bi, rr * bi + ri * br
        n >>= 1
        if n:
            br, bi = br * br - bi * bi, 2.0 * br * bi
    return rr, ri


UNROLL = 4


def _steps(trips, fn, init):
    main = trips // UNROLL

    def body(i, c):
        for j in range(UNROLL):
            c = fn(i * UNROLL + j, c)
        return c

    c = lax.fori_loop(0, main, body, init) if main else init
    for n in range(main * UNROLL, trips):
        c = fn(n, c)
    return c


def _seg_scan(xre, xim, lam8, pw, base, seglen, rev, init, fin_re, fin_im, ini_re, ini_im, prev=None):
    lr, li = lam8
    nsub = SEG // 8

    def rows(t, s):
        first = base + t * SEG + 8 * s
        return pl.ds(first if isinstance(first, int) else pl.multiple_of(first, 8), 8)

    tmap = (lambda n: seglen - 1 - n) if rev else (lambda n: n)
    zeros = tuple(jnp.zeros((8, SB), F32) for _ in range(2 * nsub))

    def advance(c, t):
        out = []
        for s in range(nsub):
            a, b = c[2 * s], c[2 * s + 1]
            out += [lr * a - li * b + xre[rows(t, s), :], lr * b + li * a + xim[rows(t, s), :]]
        return tuple(out)

    fin = _steps(seglen, lambda n, c: advance(c, tmap(n)), zeros)
    for s in range(nsub):
        fin_re[pl.ds(8 * s, 8), :] = fin[2 * s]
        fin_im[pl.ds(8 * s, 8), :] = fin[2 * s + 1]
    (cr, ci), (pr, pi) = init, pw
    for i in (range(SEG - 1, -1, -1) if rev else range(SEG)):
        ini_re[pl.ds(i, 1), :] = cr
        ini_im[pl.ds(i, 1), :] = ci
        cr, ci = pr * cr - pi * ci + fin_re[pl.ds(i, 1), :], pr * ci + pi * cr + fin_im[pl.ds(i, 1), :]
    tiles = lambda re, im: tuple(r[pl.ds(8 * s, 8), :] for s in range(nsub) for r in (re, im))
    start = tiles(ini_re, ini_im)

    def store(c, t):
        new = advance(c, t)
        for s in range(nsub):
            xre[rows(t, s), :] = new[2 * s]
            xim[rows(t, s), :] = new[2 * s + 1]
        return new

    if prev is None:
        _steps(seglen, lambda n, c: store(c, tmap(n)), start)
        return (cr, ci), None

    sre, sim, s_ini_re, s_ini_im = prev

    def acc_step(c, t, before):
        new = store(c[:2 * nsub], t)
        acc = []
        for s in range(nsub):
            (na, nb), (pre, pim) = new[2 * s:2 * s + 2], before[2 * s:2 * s + 2]
            acc += [c[2 * nsub + 2 * s] + na * pre + nb * pim, c[2 * nsub + 2 * s + 1] + nb * pre - na * pim]
        return new + tuple(acc)

    def body(n, c):
        t = tmap(n)
        tp = t - 1 if rev else t + 1
        return acc_step(c, t, tuple(r[rows(tp, s), :] for s in range(nsub) for r in (sre, sim)))

    c = _steps(seglen - 1, body, start + zeros)
    c = acc_step(c, 0 if rev else seglen - 1, tiles(s_ini_re, s_ini_im))
    acc = c[2 * nsub:]
    return (cr, ci), (sum(acc[0::2][1:], acc[0]), sum(acc[1::2][1:], acc[1]))


def _lam_tiles(lr, li, lens, conj=False):
    if conj:
        li = -li
    lam8 = (jnp.broadcast_to(lr, (8, SB)), jnp.broadcast_to(li, (8, SB)))
    return lam8, [_cpow(lr, li, n) for n in lens]


def _stretches(T):
    return ((0, LC // SEG), (LC, (T - LC) // SEG))


def _to_seg_order(src, dst, T):
    for base, seglen in _stretches(T):
        def body(t, carry, base=base, seglen=seglen):
            dst[pl.ds(pl.multiple_of(base + t * SEG, SEG), SEG), :] = src[pl.ds(base + t, SEG, stride=seglen), :]
            return carry
        lax.fori_loop(0, seglen, body, 0, unroll=8)


def _from_seg_order(src, dst, T):
    for base, seglen in _stretches(T):
        def body(t, carry, base=base, seglen=seglen):
            dst[pl.ds(base + t, SEG, stride=seglen), :] = src[pl.ds(pl.multiple_of(base + t * SEG, SEG), SEG), :]
            return carry
        lax.fori_loop(0, seglen, body, 0, unroll=8)


def _scan_specs(T):
    ublk = pl.BlockSpec((T, UB), lambda j: (0, j))
    lam = pl.BlockSpec((2, 1, 1, SB), lambda j: (0, j, 0, 0))
    mat = pl.BlockSpec((2, 1, UB, P), lambda j: (0, j, 0, 0))
    return ublk, lam, mat


def _dotf(a, b, mode="nn"):
    return lax.dot_general(a, b, _DN[mode], preferred_element_type=F32)


def _diag_mask():
    r = lax.broadcasted_iota(jnp.int32, (UB, SB), 0)
    c = lax.broadcasted_iota(jnp.int32, (UB, SB), 1)
    return lax.shift_right_logical(r, int(math.log2(CH))) == lax.shift_right_logical(c, int(math.log2(P)))


def _expand(m):
    p = lax.broadcasted_iota(jnp.int32, (P, SB), 0)
    c = lax.broadcasted_iota(jnp.int32, (P, SB), 1)
    tile = jnp.where(lax.bitwise_and(c, P - 1) == p, 1.0, 0.0).astype(BF16)
    wide = jnp.dot(m.astype(BF16), tile, preferred_element_type=F32)
    return jnp.where(_diag_mask(), wide, 0.0).astype(BF16)


def _collapse(full):
    c = lax.broadcasted_iota(jnp.int32, (SB, P), 0)
    p = lax.broadcasted_iota(jnp.int32, (SB, P), 1)
    pick = jnp.where(lax.bitwise_and(c, P - 1) == p, 1.0, 0.0).astype(BF16)
    return _exact_perm(jnp.where(_diag_mask(), full, 0.0), pick)


def _zero_state():
    return jnp.zeros((1, SB), F32), jnp.zeros((1, SB), F32)


def scan_fwd(u, lam_re, lam_im, bre, bim, cre, cim, name):
    T = u.shape[0]
    s_ctx, s_lat = LC // SEG, (T - LC) // SEG

    def body(u_ref, lr_ref, li_ref, bre_ref, bim_ref, cre_ref, cim_ref, y_ref, us, ys, sre, sim, fre, fim, ire, iim):
        _to_seg_order(u_ref, us, T)
        ub = us[...].astype(BF16)
        for d in range(2):
            lam8, (pw_c, pw_l) = _lam_tiles(lr_ref[d, 0], li_ref[d, 0], (s_ctx, s_lat))
            sre[...] = _dotf(ub, _expand(bre_ref[d, 0]))
            sim[...] = _dotf(ub, _expand(bim_ref[d, 0]))
            end_c, _ = _seg_scan(sre, sim, lam8, pw_c, 0, s_ctx, bool(d), _zero_state(), fre, fim, ire, iim)
            _seg_scan(sre, sim, lam8, pw_l, LC, s_lat, bool(d), end_c, fre, fim, ire, iim)
            y = (_dotf(sre[...].astype(BF16), _expand(cre_ref[d, 0]), "nt")
                 - _dotf(sim[...].astype(BF16), _expand(cim_ref[d, 0]), "nt"))
            if d == 0:
                ys[...] = y
            else:
                ys[...] += y
        _from_seg_order(ys, y_ref, T)

    ublk, lam, mat = _scan_specs(T)
    return pl.pallas_call(
        body, grid=(NJ,), in_specs=[ublk, lam, lam, mat, mat, mat, mat], out_specs=ublk,
        out_shape=jax.ShapeDtypeStruct((T, G * CH), F32),
        scratch_shapes=[pltpu.VMEM((T, UB), F32)] * 2 + [pltpu.VMEM((T, SB), F32)] * 2 + [pltpu.VMEM((SEG, SB), F32)] * 4,
        compiler_params=_cp(("arbitrary",)), name=name)(u, lam_re, lam_im, bre, bim, cre, cim)


def scan_bwd(u, dy, lam_re, lam_im, bre, bim, cre, cim, name):
    T = u.shape[0]
    s_ctx, s_lat = LC // SEG, (T - LC) // SEG

    def body(u_ref, dy_ref, lr_ref, li_ref, bre_ref, bim_ref, cre_ref, cim_ref,
             du_ref, dlr_ref, dli_ref, dbre_ref, dbim_ref, dcre_ref, dcim_ref,
             us, dys, dus, sre, sim, gre, gim, fre, fim, ic_re, ic_im, il_re, il_im, jre, jim):
        _to_seg_order(u_ref, us, T)
        _to_seg_order(dy_ref, dys, T)
        ub, dyb = us[...].astype(BF16), dys[...].astype(BF16)
        for d in range(2):
            rev = bool(d)
            lam8, (pw_c, pw_l) = _lam_tiles(lr_ref[d, 0], li_ref[d, 0], (s_ctx, s_lat))
            cam8, (cw_c, cw_l) = _lam_tiles(lr_ref[d, 0], li_ref[d, 0], (s_ctx, s_lat), conj=True)
            bre_v, bim_v = _expand(bre_ref[d, 0]), _expand(bim_ref[d, 0])
            sre[...] = _dotf(ub, bre_v)
            sim[...] = _dotf(ub, bim_v)
            end_c, _ = _seg_scan(sre, sim, lam8, pw_c, 0, s_ctx, rev, _zero_state(), fre, fim, ic_re, ic_im)
            _seg_scan(sre, sim, lam8, pw_l, LC, s_lat, rev, end_c, fre, fim, il_re, il_im)
            gre[...] = _dotf(dyb, _expand(cre_ref[d, 0]))
            gim[...] = -_dotf(dyb, _expand(cim_ref[d, 0]))
            end_g, acc_l = _seg_scan(gre, gim, cam8, cw_l, LC, s_lat, not rev, _zero_state(), fre, fim, jre, jim,
                                     prev=(sre, sim, il_re, il_im))
            _, acc_c = _seg_scan(gre, gim, cam8, cw_c, 0, s_ctx, not rev, end_g, fre, fim, jre, jim,
                                 prev=(sre, sim, ic_re, ic_im))
            dlr_ref[d, 0] = _sum0(acc_l[0] + acc_c[0])
            dli_ref[d, 0] = _sum0(acc_l[1] + acc_c[1])
            grb, gib = gre[...].astype(BF16), gim[...].astype(BF16)
            du = _dotf(grb, bre_v, "nt") + _dotf(gib, bim_v, "nt")
            if d == 0:
                dus[...] = du
            else:
                dus[...] += du
            dbre_ref[d, 0] = _collapse(_dotf(ub, grb, "tn"))
            dbim_ref[d, 0] = _collapse(_dotf(ub, gib, "tn"))
            dcre_ref[d, 0] = _collapse(_dotf(dyb, sre[...].astype(BF16), "tn"))
            dcim_ref[d, 0] = -_collapse(_dotf(dyb, sim[...].astype(BF16), "tn"))
        _from_seg_order(dus, du_ref, T)

    ublk, lam, mat = _scan_specs(T)
    lam_s = jax.ShapeDtypeStruct(lam_re.shape, F32)
    mat_s = jax.ShapeDtypeStruct(bre.shape, F32)
    return pl.pallas_call(
        body, grid=(NJ,), in_specs=[ublk, ublk, lam, lam, mat, mat, mat, mat],
        out_specs=[ublk, lam, lam, mat, mat, mat, mat],
        out_shape=[jax.ShapeDtypeStruct((T, G * CH), F32), lam_s, lam_s, mat_s, mat_s, mat_s, mat_s],
        scratch_shapes=[pltpu.VMEM((T, UB), F32)] * 3 + [pltpu.VMEM((T, SB), F32)] * 4 + [pltpu.VMEM((SEG, SB), F32)] * 8,
        compiler_params=_cp(("arbitrary",)), name=name)(u, dy, lam_re, lam_im, bre, bim, cre, cim)


class Exchange:
    def __init__(self, xs, modes):
        self.n = len(xs)
        self.modes = [modes] * self.n if isinstance(modes, (str, int)) else list(modes)
        self.out_shape = [jax.ShapeDtypeStruct(self._shape(x, md), x.dtype) for x, md in zip(xs, self.modes)]
        self.scratch = [pltpu.SemaphoreType.DMA((NDEV - 1, self.n)), pltpu.SemaphoreType.DMA((NDEV - 1, self.n)),
                        pltpu.SemaphoreType.DMA((self.n,))]
        self.specs = [pl.BlockSpec(memory_space=pl.ANY)] * self.n

    @staticmethod
    def _shape(x, mode):
        if mode == "gather":
            return (NDEV,) + tuple(x.shape)
        return tuple(x.shape) if mode == "lead" else (NDEV, x.shape[0], mode) + tuple(x.shape[2:])

    @staticmethod
    def _piece(x_ref, mode, dev):
        if mode == "gather":
            return x_ref
        return x_ref.at[dev] if mode == "lead" else x_ref.at[:, pl.ds(dev * mode, mode)]

    def _copies(self, x_refs, out_refs, sems):
        send_sems, recv_sems, local_sems = sems
        mx, my, mc = lax.axis_index("x"), lax.axis_index("y"), lax.axis_index("c")
        me = 4 * mx + 2 * my + mc
        peer_of = lambda k: (1 - mx if k & 4 else mx, 1 - my if k & 2 else my, 1 - mc if k & 1 else mc)
        local, first, relay, arrivals = [], [], [], []
        for a, (x_ref, out_ref) in enumerate(zip(x_refs, out_refs)):
            mode = self.modes[a]
            local.append(pltpu.make_async_copy(self._piece(x_ref, mode, me), out_ref.at[me], local_sems.at[a]))

            def remote(src, dst, k, pair, a=a):
                return pltpu.make_async_remote_copy(src_ref=src, dst_ref=dst, send_sem=send_sems.at[pair, a],
                                                    recv_sem=recv_sems.at[pair, a], device_id=peer_of(k), device_id_type=MESH_T)

            for k in range(1, NDEV):
                peer = peer_of(k)
                pid = 4 * peer[0] + 2 * peer[1] + peer[2]
                if mode != "gather":
                    src = self._piece(x_ref, mode, pid)
                    first.append(remote(src, out_ref.at[me], k, k - 1))
                    arrivals.append(remote(src, out_ref.at[pid], k, k - 1))
                elif k == 1:
                    first.append(remote(x_ref, out_ref.at[me], k, k - 1))
                    arrivals.append(remote(x_ref, out_ref.at[pid], k, k - 1))
                elif k % 2 == 0:
                    first.append(remote(x_ref, out_ref.at[me], k, k - 1))
                    relay.append((remote(x_ref, out_ref.at[pid], k, k - 1), remote(out_ref.at[pid], out_ref.at[pid], 1, k)))
                else:
                    arrivals.append(remote(x_ref, out_ref.at[pid], 1, k - 1))
        return local, first, relay, arrivals

    def start(self, x_refs, out_refs, sems):
        local, first, _, _ = self._copies(x_refs, out_refs, sems)
        for cp in local + first:
            cp.start()

    def finish(self, x_refs, out_refs, sems):
        local, first, relay, arrivals = self._copies(x_refs, out_refs, sems)
        for arrival, onward in relay:
            arrival.wait_recv()
            onward.start()
        for cp in arrivals:
            cp.wait_recv()
        for cp in first + [onward for _, onward in relay]:
            cp.wait_send()
        for cp in local:
            cp.wait()


def exchange(xs, modes, name):
    ex = Exchange(xs, modes)
    n = ex.n

    def body(*refs):
        ex.start(refs[:n], refs[n:2 * n], refs[2 * n:])
        ex.finish(refs[:n], refs[n:2 * n], refs[2 * n:])

    return pl.pallas_call(body, in_specs=ex.specs, out_specs=ex.specs, out_shape=ex.out_shape, scratch_shapes=ex.scratch,
                          compiler_params=pltpu.CompilerParams(has_side_effects=True), name=name)(*xs)


def _dot_f32(a, b, dn):
    return lax.dot_general(a, b, dn, preferred_element_type=F32, precision=lax.Precision.HIGHEST)


def ada_fwd(cg, c_ctx, ada_w, ada_b_loc, name):
    W = ada_w.shape[2]

    def body(cg_ref, cc_ref, w_ref, b_ref, o_ref):
        a = jnp.concatenate([_silu(cg_ref[...]), jnp.broadcast_to(_silu(cc_ref[...]), (NDEV, D))], axis=0)
        for i in range(2):
            o_ref[i] = _dot_f32(a, w_ref[i], _DN["nn"]) + b_ref[i]

    return pl.pallas_call(body, out_shape=jax.ShapeDtypeStruct((2, 2 * NDEV, W), F32),
                          compiler_params=_cp(), name=name)(cg, c_ctx, ada_w, ada_b_loc)


def ada_bwd(cg, c_ctx, ada_w, dm_loc, dm_all, name):
    W = ada_w.shape[2]

    def body(cg_ref, cc_ref, w_ref, dl_ref, da_ref, gw_ref, dcc_ref, gb_ref):
        a = jnp.concatenate([_silu(cg_ref[...]), jnp.broadcast_to(_silu(cc_ref[...]), (NDEV, D))], axis=0)
        dcc = jnp.zeros((1, D), F32)
        for i in range(2):
            dl = dl_ref[i]
            gw_ref[i] = _dot_f32(a, dl, _DN["tn"])
            dctx = jnp.sum(dl[NDEV:], axis=0, keepdims=True)
            dcc = dcc + _dot_f32(dctx, w_ref[i], _DN["nt"])
        dcc_ref[...] = dcc
        gb_ref[...] = jnp.sum(da_ref[...], axis=0)

    return pl.pallas_call(body, out_shape=[jax.ShapeDtypeStruct((2, D, W), F32), jax.ShapeDtypeStruct((1, D), F32),
                                           jax.ShapeDtypeStruct((2, 3 * D), F32)],
                          compiler_params=_cp(), name=name)(cg, c_ctx, ada_w, dm_loc, dm_all)


def cctx_finish(parts, c_ctx, name):
    def body(p_ref, cc_ref, o_ref):
        o_ref[...] = jnp.sum(p_ref[...], axis=0, keepdims=True) * _dsilu(cc_ref[...])

    return pl.pallas_call(body, out_shape=jax.ShapeDtypeStruct((1, D), F32), name=name)(parts, c_ctx)


def _adamw_update(g_ref, w_ref, m_ref, v_ref, go_ref, d_ref, mo_ref, vo_ref):
    g = g_ref[0].astype(F32)
    for s in range(1, g_ref.shape[0]):
        g = g + g_ref[s].astype(F32)
    mn = B1 * m_ref[...] + (1.0 - B1) * g
    vn = B2 * v_ref[...] + (1.0 - B2) * g * g
    go_ref[...] = g
    mo_ref[...] = mn
    vo_ref[...] = vn
    d_ref[...] = -LR * ((mn * (1.0 / (1.0 - B1 ** STEP))) / (jnp.sqrt(vn * (1.0 / (1.0 - B2 ** STEP))) + AEPS) + WD * w_ref[...])


def adamw(gstack, w, m, v, name, tr=256):
    n, R, C = gstack.shape
    tr = max(t for t in range(8, min(tr, R) + 1, 8) if R % t == 0)
    spec = pl.BlockSpec((tr, C), lambda i: (i, 0))
    return pl.pallas_call(_adamw_body(1), grid=(R // tr,),
                          in_specs=[pl.BlockSpec((n, tr, C), lambda i: (0, i, 0)), spec, spec, spec],
                          out_specs=[spec] * 4, out_shape=[jax.ShapeDtypeStruct((R, C), F32)] * 4,
                          compiler_params=_cp(("parallel",)), name=name)(gstack, w, m, v)


def _adamw_body(k):
    def body(*refs):
        for t in range(k):
            _adamw_update(*refs[4 * t:4 * t + 4], *refs[4 * k + 4 * t:4 * k + 4 * t + 4])
    return body


def adamw_multi(items, grid, name):
    k = len(items)
    ins, in_specs, out_specs, out_shape = [], [], [], []
    for g, g_spec, w, m, v, w_spec in items:
        ins += [g, w, m, v]
        in_specs += [g_spec, w_spec, w_spec, w_spec]
    for g, g_spec, w, m, v, w_spec in items:
        out_specs += [w_spec] * 4
        out_shape += [jax.ShapeDtypeStruct(w.shape, F32)] * 4
    res = pl.pallas_call(_adamw_body(k), grid=grid, in_specs=in_specs, out_specs=out_specs, out_shape=out_shape,
                         compiler_params=_cp(("arbitrary",) * len(grid)), name=name)(*ins)
    return [res[4 * t:4 * t + 4] for t in range(k)]


def _whole(a, grid_rank):
    zeros = (0,) * a.ndim
    return pl.BlockSpec(a.shape, lambda *idx: zeros)


def sum_slots(xs, name):
    def body(*refs):
        for x_ref, o_ref in zip(refs[:len(xs)], refs[len(xs):]):
            acc = x_ref[0]
            for s in range(1, NDEV):
                acc = acc + x_ref[s]
            o_ref[...] = acc

    return pl.pallas_call(body, out_shape=[jax.ShapeDtypeStruct(x.shape[1:], F32) for x in xs],
                          compiler_params=_cp(), name=name)(*xs)


def _col_shards(g):
    R, N = g.shape
    return g.reshape(R, NDEV, N // NDEV).transpose(1, 0, 2)


def _vec2(v):
    return jnp.broadcast_to(v.reshape(1, 1, -1), (2, 1, v.size))


SHARD_ROWS = {"mla_w_in": 192, "mla_w_uq": 192, "mla_w_ukv": 256, "s5_w_in": 256}


def _t_shard(wsh, rows):
    t = wsh[0].T.astype(BF16)
    return jnp.pad(t, ((0, rows - t.shape[0]), (0, 0)))


def _win_order():
    w = IN_W // NDEV
    perm = np.zeros((IN_WP, NDEV * SHARD_ROWS["mla_w_in"]), np.float32)
    first = QL + KVL + ROPE
    for c in range(IN_W):
        n = c + HEADS * VD if c < first else c - first
        perm[n, (c // w) * SHARD_ROWS["mla_w_in"] + c % w] = 1.0
    return jnp.asarray(perm, BF16)


def local_step(ctx, x, tgt, mod, Wt, small, l1_shards):
    T = LC + x.shape[0]
    xa = ("cat", ctx, x)
    sh = [mod[i, :, None, 0:D] for i in range(2)]
    sc = [mod[i, :, None, D:2 * D] for i in range(2)]
    gt = [mod[i, :, None, 2 * D:] for i in range(2)]
    ng = [_vec2(small["norm_g"][i]) for i in range(2)]
    qg, kvg = _vec2(small["mla_q_norm"]), _vec2(small["mla_kv_norm"])
    cosf, sinf, pm, pmt = _rope_tables(T)

    (h0, p0, cqn, ckvn), _ = rowwise(st_l0_pre, [xa], [ng[0], sc[0], sh[0], qg, kvg],
                                     [(D, BF16), (IN_WP, F32), (QL, BF16), (KVL, BF16)], [], "l0_pre", mats=[Wt["mla_w_in"]])
    z0, cq, ckv = (p0, 0, HEADS * VD), (p0, HEADS * VD // QL, QL), (p0, (HEADS * VD + QL) // KVL, KVL)
    Q = project_q(cqn, Wt["mla_w_uq"], cosf, sinf, pm, "l0_uq")
    K, V = project_kv(ckvn, Wt["mla_w_ukv"], p0, (HEADS * VD + QL + KVL) // 128, "l0_ukv")
    (o, lse), got = attn_fwd(Q, K, V, "l0_attn", rode=l1_shards, modes="gather")
    Wt, small = dict(Wt), dict(small)
    for n, a in zip(L1_BIG, got):
        Wt[n] = a.reshape(-1, a.shape[-1])
    vecs = lax.bitcast_convert_type(got[-1].reshape(NDEV, 2, -1, 2), F32)
    small["s5_d"], small["s5_b_glu"] = vecs[:, 0, :].reshape(D), vecs[:, 1, :].reshape(D)
    o2 = o.transpose(1, 0, 2).reshape(T, HEADS * VD)
    (og, out0, x1), _ = rowwise(st_l0_post, [o2, z0, xa], [gt[0]], [(D, BF16), (D, BF16), (D, F32)], [], "l0_post",
                                mats=[Wt["mla_w_out"]])

    ls = small["s5_log_step"].reshape(2, G, 1)
    a_re, a_im = small["s5_a_re"].reshape(2, G, P), small["s5_a_im"].reshape(2, G, P)
    b_re, b_im = small["s5_b_re"].reshape(2, G * P, CH), small["s5_b_im"].reshape(2, G * P, CH)
    lam_re, lam_im, f_re, f_im = disc_fwd(a_re, a_im, ls, "s5_disc")
    f_re2, f_im2 = f_re.reshape(2, G * P, 1), f_im.reshape(2, G * P, 1)
    bb_re, bb_im = disc_b(f_re2, f_im2, b_re, b_im, "s5_disc_b")
    compact = lambda m: m.reshape(2, NJ, UB, P)
    bre = compact(bb_re.reshape(2, G, P, CH).transpose(0, 1, 3, 2))
    bim = compact(bb_im.reshape(2, G, P, CH).transpose(0, 1, 3, 2))
    cre, cim = compact(small["s5_c_re"]), compact(small["s5_c_im"])
    lam_re4, lam_im4 = lam_re.reshape(2, NJ, 1, SB), lam_im.reshape(2, NJ, 1, SB)

    (h1, p1), _ = rowwise(st_l1_pre, [x1], [ng[1], sc[1], sh[1]], [(D, BF16), (2 * D, F32)], [], "l1_pre", mats=[Wt["s5_w_in"]])
    u, z1 = (p1, 0, D), (p1, 1, D)
    yssm = scan_fwd(p1, lam_re4, lam_im4, bre, bim, cre, cim, "s5_scan")
    dvec, bglu = _vec2(small["s5_d"]), _vec2(small["s5_b_glu"])
    fg = _vec2(small["final_g"])
    lat_mask = jnp.stack([jnp.zeros((1, D), F32), jnp.ones((1, D), F32)])
    (y, y1b, gl, y3, out1, dx2), (dfg, lvec) = rowwise(
        st_l1_mlp, [yssm, u, z1, x1, ("lat", tgt)], [dvec, bglu, gt[1], fg, lat_mask],
        [(D, F32), (D, BF16), (D, BF16), (D, BF16), (D, BF16), (D, F32)], [D, 128], "l1_mlp",
        mats=[Wt["s5_w_glu"], Wt["s5_w_out"]])

    (dz1, dy, du_d), (dgt1, dbglu, dd), (g_w_out5, g_w_glu) = rowwise(
        st_l1_mlp_bwd, [dx2, out1, y3, y, gl, z1, u, y1b], [gt[1], bglu, dvec], [(D, BF16), (D, F32), (D, F32)], [D, D, D],
        "l1_mlp_b", mats=[Wt["s5_w_out"], Wt["s5_w_glu"]], out_accs=[(D, D), (D, D)])
    du_s, dlr, dli, dbre, dbim, dcre, dcim = scan_bwd(p1, dy, lam_re4, lam_im4, bre, bim, cre, cim, "s5_scan_b")
    dbb_re = dbre.reshape(2, G, CH, P).transpose(0, 1, 3, 2).reshape(2, G * P, CH)
    dbb_im = dbim.reshape(2, G, CH, P).transpose(0, 1, 3, 2).reshape(2, G * P, CH)
    g_c_re, g_c_im = dcre.reshape(2, G, CH, P), dcim.reshape(2, G, CH, P)
    g_b_re, g_b_im, dfr, dfi = disc_b_bwd(f_re2, f_im2, b_re, b_im, dbb_re, dbb_im, "s5_disc_b_b")
    g_a_re, g_a_im, g_ls = disc_a_bwd(a_re, a_im, ls, dlr.reshape(2, G, P), dli.reshape(2, G, P),
                                      dfr.reshape(2, G, P), dfi.reshape(2, G, P), "s5_disc_b_a")
    (dx1,), (dsh1, dsc1, dng1), (g_w_in5,) = rowwise(
        st_l1_tail_bwd, [du_d, du_s, dz1, h1, x1, dx2], [ng[1], sc[1]], [(D, F32)], [D, D, D], "l1_pre_b",
        mats=[Wt["s5_w_in"]], out_accs=[(D, 2 * D)])
    g_w_in5 = _col_shards(g_w_in5)

    (do2, dz0), (dgt0,), (g_w_out,) = rowwise(st_l0_post_bwd, [dx1, out0, og, o2, z0], [gt[0]], [(D, F32), (D, F32)], [D],
                                              "l0_post_b", mats=[Wt["mla_w_out"]], out_accs=[(D, D)])
    doh = do2.reshape(T, HEADS, VD).transpose(1, 0, 2)
    rows8 = lambda g: g.reshape(NDEV, -1, g.shape[-1])
    both = lambda s: s[0, 0] + s[1, 0]
    dense = lambda g: g.reshape(2, G * P * CH // 128, 128)
    chunks = [dense(g_b_re), dense(g_b_im), g_c_re, g_c_im]
    l1_send = [g_w_in5, rows8(g_w_glu), rows8(g_w_out5), rows8(g_w_out),
               both(dd).reshape(NDEV, 1, -1), both(dbglu).reshape(NDEV, 1, -1)]
    (dQ, dK, dV), l1_recv = attn_bwd(Q, K, V, o, lse, doh, "l0_attn_b", rode=l1_send + chunks,
                                     modes=["lead"] * len(l1_send) + [a.shape[1] // NDEV for a in chunks])
    dqh = rope(dQ, cosf, sinf, pmt, True, BF16, "l0_rope_q_b", scale=SCALE)
    dq = dqh.transpose(1, 0, 2).reshape(T, HEADS * QK)
    dkv, dkr = split_kv_grads(dK, dV, "l0_kv_b")
    (grad_x,), (dqg, dkvg, dsh0, dsc0, dng0), (g_uq, g_ukv, g_p) = rowwise(
        st_l0_tail_bwd, [dq, dkv, dkr, dz0, cq, ckv, cqn, ckvn, h0, xa, dx1], [qg, kvg, ng[0], sc[0]],
        [(D, F32, "lat")], [QL, KVL, D, D, D], "l0_pre_b", mats=[Wt["mla_w_uq"], Wt["mla_w_ukv"], Wt["mla_w_in"]],
        out_accs=[(QL, HEADS * QK), (KVL, HEADS * KVW), (D, IN_WP)])
    g_w_uq, g_w_ukv = _col_shards(g_uq).astype(BF16), _col_shards(g_ukv).astype(BF16)
    g_w_in = _col_shards(jnp.concatenate([g_p[:, HEADS * VD:IN_W], g_p[:, :HEADS * VD]], axis=1)).astype(BF16)

    dmod = jnp.stack([jnp.concatenate([dsh0, dsc0, dgt0], axis=-1)[:, 0], jnp.concatenate([dsh1, dsc1, dgt1], axis=-1)[:, 0]])
    gbig = {"mla_w_in": g_w_in, "mla_w_uq": g_w_uq, "mla_w_ukv": g_w_ukv}
    gsmall = {"norm_g": jnp.stack([both(dng0), both(dng1)]), "mla_q_norm": both(dqg), "mla_kv_norm": both(dkvg),
              "s5_a_re": g_a_re, "s5_a_im": g_a_im, "s5_log_step": g_ls, "final_g": dfg[1, 0]}
    return lvec[1], grad_x, dmod, gbig, gsmall, l1_recv


COL_SHARDED = ("mla_w_in", "mla_w_uq", "mla_w_ukv", "s5_w_in")
ROW_SHARDED = ("mla_w_out", "s5_w_glu", "s5_w_out")
VEC_SHARDED = ("s5_d", "s5_b_glu")
BIG = COL_SHARDED + ROW_SHARDED
L0_BIG = ("mla_w_in", "mla_w_uq", "mla_w_ukv")
L1_BIG = ("s5_w_in", "s5_w_glu", "s5_w_out", "mla_w_out")
BITS16 = jnp.bfloat16
SMALL_RS = ("norm_g", "mla_q_norm", "mla_kv_norm", "s5_a_re", "s5_a_im", "s5_log_step", "s5_b_re", "s5_b_im",
            "s5_c_re", "s5_c_im", "final_g")
CHUNKED = ("s5_b_re", "s5_b_im", "s5_c_re", "s5_c_im")
DENSE = ("s5_b_re", "s5_b_im")
TINY = ("norm_g", "mla_q_norm", "mla_kv_norm", "s5_a_re", "s5_a_im", "s5_log_step", "final_g")
ORDER = ("c_ctx", "ada_w", "ada_b", "norm_g", "mla_w_in", "mla_q_norm", "mla_w_uq", "mla_kv_norm", "mla_w_ukv",
         "mla_w_out", "s5_w_in", "s5_a_re", "s5_a_im", "s5_log_step", "s5_b_re", "s5_b_im", "s5_c_re", "s5_c_im",
         "s5_d", "s5_w_glu", "s5_b_glu", "s5_w_out", "final_g")


def kernel(x, c, ctx, c_ctx, ada_w, ada_b, norm_g, mla_w_in, mla_q_norm, mla_w_uq, mla_kv_norm, mla_w_ukv, mla_w_out, s5_w_in, s5_a_re, s5_a_im, s5_log_step, s5_b_re, s5_b_im, s5_c_re, s5_c_im, s5_d, s5_w_glu, s5_b_glu, s5_w_out, final_g, loss_target, m_c_ctx, m_ada_w, m_ada_b, m_norm_g, m_mla_w_in, m_mla_q_norm, m_mla_w_uq, m_mla_kv_norm, m_mla_w_ukv, m_mla_w_out, m_s5_w_in, m_s5_a_re, m_s5_a_im, m_s5_log_step, m_s5_b_re, m_s5_b_im, m_s5_c_re, m_s5_c_im, m_s5_d, m_s5_w_glu, m_s5_b_glu, m_s5_w_out, m_final_g, v_c_ctx, v_ada_w, v_ada_b, v_norm_g, v_mla_w_in, v_mla_q_norm, v_mla_w_uq, v_mla_kv_norm, v_mla_w_ukv, v_mla_w_out, v_s5_w_in, v_s5_a_re, v_s5_a_im, v_s5_log_step, v_s5_b_re, v_s5_b_im, v_s5_c_re, v_s5_c_im, v_s5_d, v_s5_w_glu, v_s5_b_glu, v_s5_w_out, v_final_g):
    w = dict(c_ctx=c_ctx, ada_w=ada_w, ada_b=ada_b, norm_g=norm_g, mla_w_in=mla_w_in, mla_q_norm=mla_q_norm,
             mla_w_uq=mla_w_uq, mla_kv_norm=mla_kv_norm, mla_w_ukv=mla_w_ukv, mla_w_out=mla_w_out, s5_w_in=s5_w_in,
             s5_a_re=s5_a_re, s5_a_im=s5_a_im, s5_log_step=s5_log_step, s5_b_re=s5_b_re, s5_b_im=s5_b_im,
             s5_c_re=s5_c_re, s5_c_im=s5_c_im, s5_d=s5_d, s5_w_glu=s5_w_glu, s5_b_glu=s5_b_glu, s5_w_out=s5_w_out,
             final_g=final_g)
    m = dict(c_ctx=m_c_ctx, ada_w=m_ada_w, ada_b=m_ada_b, norm_g=m_norm_g, mla_w_in=m_mla_w_in, mla_q_norm=m_mla_q_norm,
             mla_w_uq=m_mla_w_uq, mla_kv_norm=m_mla_kv_norm, mla_w_ukv=m_mla_w_ukv, mla_w_out=m_mla_w_out,
             s5_w_in=m_s5_w_in, s5_a_re=m_s5_a_re, s5_a_im=m_s5_a_im, s5_log_step=m_s5_log_step, s5_b_re=m_s5_b_re,
             s5_b_im=m_s5_b_im, s5_c_re=m_s5_c_re, s5_c_im=m_s5_c_im, s5_d=m_s5_d, s5_w_glu=m_s5_w_glu,
             s5_b_glu=m_s5_b_glu, s5_w_out=m_s5_w_out, final_g=m_final_g)
    v = dict(c_ctx=v_c_ctx, ada_w=v_ada_w, ada_b=v_ada_b, norm_g=v_norm_g, mla_w_in=v_mla_w_in, mla_q_norm=v_mla_q_norm,
             mla_w_uq=v_mla_w_uq, mla_kv_norm=v_mla_kv_norm, mla_w_ukv=v_mla_w_ukv, mla_w_out=v_mla_w_out,
             s5_w_in=v_s5_w_in, s5_a_re=v_s5_a_re, s5_a_im=v_s5_a_im, s5_log_step=v_s5_log_step, s5_b_re=v_s5_b_re,
             s5_b_im=v_s5_b_im, s5_c_re=v_s5_c_re, s5_c_im=v_s5_c_im, s5_d=v_s5_d, s5_w_glu=v_s5_w_glu,
             s5_b_glu=v_s5_b_glu, s5_w_out=v_s5_w_out, final_g=v_final_g)

    me = 4 * lax.axis_index("x") + 2 * lax.axis_index("y") + lax.axis_index("c")
    WA = ada_w.shape[2]

    def shard(n):
        return _t_shard(w[n], SHARD_ROWS[n]) if n in COL_SHARDED else w[n][0].astype(BF16)

    wgot = exchange([c] + [shard(n) for n in L0_BIG], "gather", "gather_w")

    cg = wgot[0].reshape(NDEV, D)
    cc2 = c_ctx.reshape(1, D)
    ada_b_loc = lax.dynamic_slice_in_dim(ada_b.reshape(2, 3 * D // WA, WA), me, 1, axis=1)
    part = ada_fwd(cg, cc2, ada_w, ada_b_loc, "ada_fwd")
    pg = exchange([part], "gather", "gather_mod")[0]
    mod_l = lax.dynamic_index_in_dim(pg, me, axis=2, keepdims=False).transpose(1, 0, 2).reshape(2, 3 * D)
    mod_c = pg[:, :, NDEV, :].transpose(1, 0, 2).reshape(2, 3 * D)
    mod = jnp.stack([mod_c, mod_l], axis=1)

    Wt = {n: a.reshape(-1, a.shape[-1]) for n, a in zip(L0_BIG, wgot[1:])}
    Wt["mla_w_in"] = mm(_win_order(), Wt["mla_w_in"], "nn", "w_in_order", out_dtype=BF16)
    vec_bits = lax.bitcast_convert_type(jnp.concatenate([s5_d, s5_b_glu], axis=0), BITS16).reshape(2, -1)
    small = {n: w[n] for n in SMALL_RS}

    lvec, grad_x, dmod, gbig, gsmall, l1_recv = local_step(ctx[0], x[0], loss_target[0], mod, Wt, small,
                                                           [shard(n) for n in L1_BIG] + [vec_bits])
    grad_x = grad_x[None]

    per_dev = G // NDEV
    recv = dict(zip(L0_BIG, exchange([gbig[n] for n in L0_BIG], "lead", "scatter_grads")))
    recv.update(dict(zip(L1_BIG + VEC_SHARDED, l1_recv)))
    out = {}

    def keep(n, res):
        for key, arr in zip("gdmv", res):
            out[key, n] = arr.reshape(w[n].shape)

    for n in BIG:
        keep(n, adamw(recv[n], w[n][0], m[n][0], v[n][0], "adamw_" + n))
    reduced = sum_slots(l1_recv[len(L1_BIG + VEC_SHARDED):], "sum_chunks")

    kshape = lambda n: w[n].shape if w[n].ndim > 1 else (1, w[n].size)
    flat = jnp.concatenate([gsmall[n].reshape(-1) for n in TINY] + [dmod.reshape(-1), lvec.reshape(-1)])[None]
    bb_all, cc_all, flat_all = exchange([jnp.stack(reduced[:2]), jnp.stack(reduced[2:]), flat], "gather", "gather_small")
    chunk_all = [bb_all[:, 0], bb_all[:, 1], cc_all[:, 0], cc_all[:, 1]]
    tiny_all, off = [], 0
    for n in TINY:
        tiny_all.append(flat_all[:, 0, off:off + w[n].size].reshape((NDEV,) + kshape(n)))
        off += w[n].size
    dm_all = flat_all[:, 0, off:off + dmod.size].reshape((NDEV,) + dmod.shape)
    loss = sum_slots([flat_all[:, :, off + dmod.size:]], "loss_sum")[0][0, 0]

    dm_cols = lax.dynamic_slice_in_dim(dm_all.reshape(NDEV, 2, 2, 3 * D // WA, WA), me, 1, axis=3)[:, :, :, 0]
    dm_loc = jnp.concatenate([dm_cols[:, :, 1].transpose(1, 0, 2), dm_cols[:, :, 0].transpose(1, 0, 2)], axis=1)
    g_ada_w, dcc_part, g_ada_b = ada_bwd(cg, cc2, ada_w, dm_loc, dm_all.transpose(0, 2, 1, 3).reshape(2 * NDEV, 2, 3 * D), "ada_bwd")
    dcc_all = exchange([dcc_part], "gather", "gather_dcc")[0].reshape(NDEV, D)
    g_c_ctx = cctx_finish(dcc_all, cc2, "cctx_finish")

    flat2 = lambda t: t.reshape(-1, t.shape[-1])
    keep("ada_w", adamw(flat2(g_ada_w)[None], flat2(ada_w), flat2(m_ada_w), flat2(v_ada_w), "adamw_ada"))
    items = []
    halves = 2
    for n, g in zip(CHUNKED, chunk_all):
        blk = (1, 1, G // halves) + w[n].shape[3:]
        g = jnp.moveaxis(g, 0, 1).reshape(w[n].shape)
        g_spec = pl.BlockSpec((1,) + blk, lambda d, s: (0, 0, d, s, 0, 0))
        items.append((g[None], g_spec, w[n], m[n], v[n], pl.BlockSpec(blk, lambda d, s: (0, d, s, 0, 0))))
    for n, res in zip(CHUNKED, adamw_multi(items, (2, halves), "adamw_bc")):
        keep(n, res)
    tiny_g = dict(zip(TINY, tiny_all))
    tiny_g.update({n: recv[n] for n in VEC_SHARDED})
    tiny_g["c_ctx"], tiny_g["ada_b"] = g_c_ctx[None], g_ada_b[None]
    names = list(tiny_g)
    items = [(tiny_g[n], _whole(tiny_g[n], 1)) + tuple(t[n].reshape(kshape(n)) for t in (w, m, v))
             + (pl.BlockSpec(kshape(n), lambda i, r=len(kshape(n)): (0,) * r),) for n in names]
    for n, res in zip(names, adamw_multi(items, (1,), "adamw_small")):
        keep(n, res)

    return (loss, grad_x, *[out["g", n] for n in ORDER], *[out["d", n] for n in ORDER],
            *[out["m", n] for n in ORDER], *[out["v", n] for n in ORDER])
```

```python
import math

import numpy as np
import jax
import jax.numpy as jnp
from jax import lax
from jax.experimental import pallas as pl
from jax.experimental.pallas import tpu as pltpu

F32 = jnp.float32
BF16 = jnp.bfloat16

D = 1024
L = 2048
LC = 256
NDEV = 8
GRID_W = 64
EPS = 1e-6
HEADS = 16
NOPE = 64
ROPE = 32
QK = NOPE + ROPE
VD = 64
IN_W = 256 + 128 + ROPE + HEADS * 64
IN_WP = 1536
QL = 256
KVL = 128
SCALE = QK ** -0.5
LOG2E = math.log2(math.e)
THETA = 10000.0
G = 64
P = 64
CH = 16
GB = 8
NJ = G // GB
UB = GB * CH
SB = GB * P
SEG = 16
TB = 256
VMEM_LIMIT = 56 * 1024 * 1024
B1, B2, LR, AEPS, WD, STEP = 0.9, 0.999, 0.001, 1e-8, 0.01, 10
MESH_T = pl.DeviceIdType.MESH


def _cp(sem=None):
    return pltpu.CompilerParams(dimension_semantics=sem, vmem_limit_bytes=VMEM_LIMIT)


def _sig(x):
    return 1.0 / (1.0 + jnp.exp(-x))


def _silu(x):
    return x * _sig(x)


def _dsilu(x):
    s = _sig(x)
    return s * (1.0 + x * (1.0 - s))


_GK = math.sqrt(2.0 / math.pi)


def _gelu(x):
    return 0.5 * x * (1.0 + jnp.tanh(_GK * (x + 0.044715 * x * x * x)))


def _dgelu(x):
    t = jnp.tanh(_GK * (x + 0.044715 * x * x * x))
    return 0.5 * (1.0 + t) + 0.5 * x * (1.0 - t * t) * _GK * (1.0 + 3 * 0.044715 * x * x)


def _rs(x):
    return lax.rsqrt(jnp.mean(x * x, axis=-1, keepdims=True) + EPS)


def _sum0(x):
    return jnp.sum(x, axis=0, keepdims=True)


def st_norm_mod(x, g, sc, sh):
    y = x * _rs(x) * g
    return (y * (1.0 + sc) + sh,), ()


def st_norm_mod_bwd(x, dh, dres, g, sc):
    r = _rs(x)
    xn = x * r
    y = xn * g
    dy = dh * (1.0 + sc)
    dxn = dy * g
    dx = r * (dxn - xn * jnp.mean(dxn * xn, axis=-1, keepdims=True))
    return (dres + dx,), (_sum0(dh), _sum0(dh * y), _sum0(dy * xn))


def st_rms(x, g):
    return (x * _rs(x) * g,), ()


def st_rms_bwd(x, dy, g):
    r = _rs(x)
    n = x * r
    dn = dy * g
    dx = r * (dn - n * jnp.mean(dn * n, axis=-1, keepdims=True))
    return (dx,), (_sum0(dy * n),)


def st_rms2(x1, x2, g1, g2):
    return st_rms(x1, g1)[0] + st_rms(x2, g2)[0], ()


def st_rms2_bwd(x1, dy1, x2, dy2, g1, g2):
    (d1,), (s1,) = st_rms_bwd(x1, dy1, g1)
    (d2,), (s2,) = st_rms_bwd(x2, dy2, g2)
    return (d1, d2), (s1, s2)


def st_gate(o, z):
    return (o * _silu(z),), ()


def st_gate_bwd(dog, o, z):
    return (dog * _silu(z), dog * o * _dsilu(z)), ()


def st_resid(x, out, gt):
    return (x + gt * out,), ()


def st_resid_bwd(dx, out, gt):
    return (dx * gt,), (_sum0(dx * out),)


def st_s5a(yssm, u, d):
    y = yssm + d * u
    return (y, _gelu(y)), ()


def st_s5b(y, gl, z, b):
    return (_gelu(y) * _sig(gl + b) * _silu(z),), ()


def st_s5b_bwd(dy3, y, gl, z, b):
    y1 = _gelu(y)
    s = _sig(gl + b)
    dy2 = dy3 * _silu(z)
    dz = dy3 * y1 * s * _dsilu(z)
    dgl = dy2 * y1 * s * (1.0 - s)
    return (dgl, dz, dy2 * s), (_sum0(dgl),)


def st_s5a_bwd(dy1a, dy1b, y, u, d):
    dy = (dy1a + dy1b) * _dgelu(y)
    return (dy, dy * d), (_sum0(dy * u),)


def st_l0_pre(x, g, sc, sh, qg, kvg, w_in):
    hb = st_norm_mod(x, g, sc, sh)[0][0].astype(BF16)
    p = lax.dot_general(hb, w_in, _DN["nt"], preferred_element_type=F32)
    cq, ckv = p[:, HEADS * VD:HEADS * VD + QL], p[:, HEADS * VD + QL:HEADS * VD + QL + KVL]
    return (hb, p) + st_rms2(cq, ckv, qg, kvg)[0], ()


def st_l0_tail_bwd(dq, dkv, dkr, dz, cq, ckv, cqn, ckvn, h, x, dres, qg, kvg, g, sc, w_uq, w_ukv, w_in):
    dcqn = jnp.dot(dq, w_uq, preferred_element_type=F32)
    dckvn = jnp.dot(dkv, w_ukv, preferred_element_type=F32)
    (dcq, dckv), (dqg, dkvg) = st_rms2_bwd(cq, dcqn, ckv, dckvn, qg, kvg)
    dp = jnp.concatenate([dz, dcq, dckv, dkr], axis=1).astype(BF16)
    dh = jnp.dot(dp, w_in, preferred_element_type=F32)
    outs, sums = st_norm_mod_bwd(x, dh, dres, g, sc)
    tn = lambda a, b: lax.dot_general(a, b, _DN["tn"], preferred_element_type=F32)
    return outs, (dqg, dkvg) + sums, (tn(cqn, dq), tn(ckvn, dkv), tn(h, dp))


def st_l1_pre(x, g, sc, sh, w_in):
    hb = st_norm_mod(x, g, sc, sh)[0][0].astype(BF16)
    return (hb, lax.dot_general(hb, w_in, _DN["nt"], preferred_element_type=F32)), ()


def st_l1_tail_bwd(du_a, du_b, dz, h, x, dres, g, sc, w_in):
    dp = jnp.concatenate([(du_a + du_b).astype(BF16), dz], axis=1)
    dh = jnp.dot(dp, w_in, preferred_element_type=F32)
    outs, sums = st_norm_mod_bwd(x, dh, dres, g, sc)
    return outs, sums, (lax.dot_general(h, dp, _DN["tn"], preferred_element_type=F32),)


def st_l0_post(o, z, x, gt, w_out):
    og = (o * _silu(z)).astype(BF16)
    out = jnp.dot(og, w_out, preferred_element_type=F32)
    return (og, out, x + gt * out), ()


def st_l0_post_bwd(dx1, out, og, o, z, gt, w_out):
    (dout,), (dgt,) = st_resid_bwd(dx1, out.astype(F32), gt)
    doutb = dout.astype(BF16)
    dog = lax.dot_general(doutb, w_out, _DN["nt"], preferred_element_type=F32)
    return st_gate_bwd(dog, o, z)[0], (dgt,), (lax.dot_general(og, doutb, _DN["tn"], preferred_element_type=F32),)


def st_l1_mlp(yssm, u, z, x1, tgt, d, bglu, gt, fg, mask, w_glu, w_out):
    (y, y1), _ = st_s5a(yssm, u, d)
    y1b = y1.astype(BF16)
    gl = jnp.dot(y1b, w_glu, preferred_element_type=F32)
    y3 = (y1 * _sig(gl + bglu) * _silu(z)).astype(BF16)
    out = jnp.dot(y3, w_out, preferred_element_type=F32)
    (dx2,), sums = st_final(x1 + gt * out, tgt, fg, mask)
    return (y, y1b, gl, y3, out, dx2), sums


def st_l1_mlp_bwd(dx2, out, y3, y, gl, z, u, y1b, gt, bglu, d, w_out, w_glu):
    out, gl = out.astype(F32), gl.astype(F32)
    (dout,), (dgt,) = st_resid_bwd(dx2, out, gt)
    doutb = dout.astype(BF16)
    dy3 = lax.dot_general(doutb, w_out, _DN["nt"], preferred_element_type=F32)
    (dgl, dz, dy1a), (dbglu,) = st_s5b_bwd(dy3, y, gl, z, bglu)
    dglb = dgl.astype(BF16)
    dy1b = lax.dot_general(dglb, w_glu, _DN["nt"], preferred_element_type=F32)
    (dy, du), (dd,) = st_s5a_bwd(dy1a, dy1b, y, u, d)
    g_w_out = lax.dot_general(y3, doutb, _DN["tn"], preferred_element_type=F32)
    g_w_glu = lax.dot_general(y1b, dglb, _DN["tn"], preferred_element_type=F32)
    return (dz, dy, du), (dgt, dbglu, dd), (g_w_out, g_w_glu)


def st_final(x2, tgt, g, mask):
    r = _rs(x2)
    n = x2 * r
    e = n * g - tgt
    dyo = e * (1.0 / D)
    dn = dyo * g
    dx = r * (dn - n * jnp.mean(dn * n, axis=-1, keepdims=True))
    lsum = jnp.sum(_sum0(e * e), axis=1, keepdims=True) * (0.5 / D)
    return (dx * mask,), (_sum0(dyo * n), jnp.broadcast_to(lsum, (1, 128)))


def rowwise(fn, rows, vecs, out_rows, out_sums, name, mats=(), out_accs=()):
    lat_blk = lambda i: jnp.maximum(i - 1, 0)
    arrays, in_specs, pick = [], [], []
    for a in rows:
        if not isinstance(a, tuple):
            a = (a, 0, a.shape[1])
        tag = a[0] if isinstance(a[0], str) else None
        if tag == "cat":
            _, ctx, x = a
            arrays += [ctx, x]
            in_specs += [pl.BlockSpec((TB, ctx.shape[1]), lambda i: (0, 0)),
                         pl.BlockSpec((TB, x.shape[1]), lambda i: (lat_blk(i), 0))]
            pick.append(2)
        elif tag == "lat":
            arrays.append(a[1])
            in_specs.append(pl.BlockSpec((TB, a[1].shape[1]), lambda i: (lat_blk(i), 0)))
            pick.append(1)
        else:
            arr, cb, width = a
            arrays.append(arr)
            in_specs.append(pl.BlockSpec((TB, width), lambda i, cb=cb: (i, cb)))
            pick.append(1)
    T = LC + L
    nin, nv, nm, no, ns = len(arrays), len(vecs), len(mats), len(out_rows), len(out_sums)

    def body(*refs):
        i = pl.program_id(0)
        vals, k = [], 0
        for p in pick:
            if p == 2:
                vals.append(jnp.where(i == 0, refs[k][...], refs[k + 1][...]))
            else:
                vals.append(refs[k][...])
            k += p
        vals += [r[0] for r in refs[nin:nin + nv]] + [r[...] for r in refs[nin + nv:nin + nv + nm]]
        res = fn(*vals)
        first_out = nin + nv + nm
        for r, o in zip(refs[first_out:first_out + no], res[0]):
            r[...] = o.astype(r.dtype)
        sum_refs = refs[first_out + no:first_out + no + ns]
        if sum_refs:
            @pl.when(i <= 1)
            def _():
                for r in sum_refs:
                    r[...] = jnp.zeros_like(r)
            for r, s in zip(sum_refs, res[1]):
                r[0] += s
        na = len(out_accs)
        if na:
            acc_out, acc = refs[first_out + no + ns:first_out + no + ns + na], refs[first_out + no + ns + na:]

            @pl.when(i == 0)
            def _():
                for r in acc:
                    r[...] = jnp.zeros_like(r)
            for r, a in zip(acc, res[2]):
                r[...] += a

            @pl.when(i == T // TB - 1)
            def _():
                for o, r in zip(acc_out, acc):
                    o[...] = r[...].astype(o.dtype)

    kind = lambda i: (jnp.minimum(i, 1), 0, 0)
    in_specs += [pl.BlockSpec((1, 1, v.shape[2]), kind) for v in vecs]
    in_specs += [pl.BlockSpec(m.shape, lambda i: (0, 0), pipeline_mode=pl.Buffered(1)) for m in mats]
    out_specs, out_shape = [], []
    for o in out_rows:
        lat = len(o) == 3
        out_specs.append(pl.BlockSpec((TB, o[0]), (lambda i: (lat_blk(i), 0)) if lat else (lambda i: (i, 0))))
        out_shape.append(jax.ShapeDtypeStruct((L if lat else T, o[0]), o[1]))
    out_specs += [pl.BlockSpec((1, 1, c), kind) for c in out_sums]
    out_shape += [jax.ShapeDtypeStruct((2, 1, c), F32) for c in out_sums]
    out_specs += [pl.BlockSpec(s, lambda i: (0, 0)) for s in out_accs]
    out_shape += [jax.ShapeDtypeStruct(s, BF16) for s in out_accs]
    res = pl.pallas_call(body, grid=(T // TB,), in_specs=in_specs, out_specs=out_specs, out_shape=out_shape,
                         scratch_shapes=[pltpu.VMEM(s, F32) for s in out_accs],
                         compiler_params=_cp(("arbitrary",)), name=name)(*arrays, *vecs, *mats)
    if out_accs:
        return res[:no], res[no:no + ns], res[no + ns:]
    return res[:no], res[no:]


_DN = {"nn": (((1,), (0,)), ((), ())), "nt": (((1,), (1,)), ((), ())), "tn": (((0,), (0,)), ((), ()))}


def mm(a, b, mode, name, out_dtype=F32, tm=None, tn=None, shard_out=False):
    if mode == "nn":
        (M, K), (_, N) = a.shape, b.shape
    elif mode == "nt":
        (M, K), (N, _) = a.shape, b.shape
    else:
        (K, M), (_, N) = a.shape, b.shape
    if tm is None:
        tm = next((t for t in (768, 512, 256) if M % t == 0 and M > t), M)
    tn = N if tn is None else tn
    dn = _DN[mode]

    def body(a_ref, b_ref, o_ref):
        o_ref[...] = lax.dot_general(a_ref[...].astype(BF16), b_ref[...].astype(BF16), dn,
                                     preferred_element_type=F32).astype(o_ref.dtype)

    if shard_out:
        def body(a_ref, b_ref, o_ref):
            av = a_ref[...].astype(BF16)
            for j in range(N // tn):
                bj = b_ref[pl.ds(j * tn, tn), :] if mode == "nt" else b_ref[:, pl.ds(j * tn, tn)]
                o_ref[j] = lax.dot_general(av, bj.astype(BF16), dn, preferred_element_type=F32).astype(o_ref.dtype)

        a_spec = pl.BlockSpec((K, tm), lambda i: (0, i)) if mode == "tn" else pl.BlockSpec((tm, K), lambda i: (i, 0))
        return pl.pallas_call(body, grid=(M // tm,), in_specs=[a_spec, pl.BlockSpec(b.shape, lambda i: (0, 0))],
                              out_specs=pl.BlockSpec((N // tn, tm, tn), lambda i: (0, i, 0)),
                              out_shape=jax.ShapeDtypeStruct((N // tn, M, tn), out_dtype),
                              compiler_params=_cp(("parallel",)), name=name)(a, b)
    a_spec = pl.BlockSpec((K, tm), lambda i, j: (0, i)) if mode == "tn" else pl.BlockSpec((tm, K), lambda i, j: (i, 0))
    b_spec = pl.BlockSpec((tn, K), lambda i, j: (j, 0)) if mode == "nt" else pl.BlockSpec((K, tn), lambda i, j: (0, j))
    return pl.pallas_call(body, grid=(M // tm, N // tn), in_specs=[a_spec, b_spec],
                          out_specs=pl.BlockSpec((tm, tn), lambda i, j: (i, j)), out_shape=jax.ShapeDtypeStruct((M, N), out_dtype),
                          compiler_params=_cp(("parallel", "arbitrary")), name=name)(a, b)


def _rope_tables(T, width=QK, first=NOPE):
    nlat = T - LC
    pos = np.arange(nlat)
    row, col = pos // GRID_W, pos % GRID_W
    half = ROPE // 2
    inv = 1.0 / (THETA ** (np.arange(0, half, 2, dtype=np.float64) / half))
    cosf = np.ones((T, width), np.float64)
    sinf = np.zeros((T, width), np.float64)
    perm = np.zeros((width, width), np.float32)
    for m in range(ROPE):
        j = first + m
        blk, w = m // half, m % half
        ang = (row if blk == 0 else col)[:, None] * inv[None, :]
        f = w % (half // 2)
        cosf[LC:, j] = np.cos(ang[:, f])
        if w < half // 2:
            sinf[LC:, j] = -np.sin(ang[:, f])
            perm[j + half // 2, j] = 1.0
        else:
            sinf[LC:, j] = np.sin(ang[:, f])
            perm[j - half // 2, j] = 1.0
    return jnp.asarray(cosf, F32), jnp.asarray(sinf, F32), jnp.asarray(perm, BF16), jnp.asarray(perm.T, BF16)


def _exact_perm(x, pm):
    hi = x.astype(BF16)
    r1 = x - hi.astype(F32)
    mid = r1.astype(BF16)
    lo = (r1 - mid.astype(F32)).astype(BF16)
    dot = lambda a: jnp.dot(a, pm, preferred_element_type=F32)
    return dot(hi) + dot(mid) + dot(lo)


def _rot(x, cv, sv, pv, inverse):
    if inverse:
        return x * cv + _exact_perm(x * sv, pv)
    return x * cv + _exact_perm(x, pv) * sv


def rope(x, cosf, sinf, pm, inverse, out_dtype, name, scale=1.0):
    H, T, _ = x.shape

    def body(x_ref, c_ref, s_ref, p_ref, o_ref):
        cv, sv, pv = c_ref[...], s_ref[...], p_ref[...]
        for h in range(H):
            o_ref[h] = (_rot(x_ref[h], cv, sv, pv, inverse) * scale).astype(o_ref.dtype)

    return pl.pallas_call(
        body, grid=(T // TB,),
        in_specs=[pl.BlockSpec((H, TB, QK), lambda i: (0, i, 0)), pl.BlockSpec((TB, QK), lambda i: (i, 0)),
                  pl.BlockSpec((TB, QK), lambda i: (i, 0)), pl.BlockSpec((QK, QK), lambda i: (0, 0))],
        out_specs=pl.BlockSpec((H, TB, QK), lambda i: (0, i, 0)), out_shape=jax.ShapeDtypeStruct((H, T, QK), out_dtype),
        compiler_params=_cp(("parallel",)), name=name)(x, cosf, sinf, pm)


KVW = NOPE + VD


def _kv_selectors():
    s_kn = np.zeros((KVW, QK), np.float32)
    s_kr = np.zeros((128, QK), np.float32)
    s_v = np.zeros((KVW, VD), np.float32)
    for l in range(NOPE):
        s_kn[l, l] = 1.0
    for l in range(ROPE):
        s_kr[l, NOPE + l] = 1.0
    for l in range(VD):
        s_v[NOPE + l, l] = 1.0
    return s_kn, s_kr, s_v


def project_q(cqn, w, cosf, sinf, pm, name):
    T = cqn.shape[0]

    def body(a_ref, w_ref, c_ref, s_ref, p_ref, o_ref):
        a, cv, sv, pv = a_ref[...], c_ref[...], s_ref[...], p_ref[...]
        for h in range(HEADS):
            qh = _dotf(a, w_ref[pl.ds(h * QK, QK), :], "nt")
            o_ref[h] = (_rot(qh, cv, sv, pv, False) * (SCALE * LOG2E)).astype(BF16)

    rows = lambda c: pl.BlockSpec((TB, c), lambda i: (i, 0))
    const = lambda x: pl.BlockSpec(x.shape, lambda i: (0, 0))
    return pl.pallas_call(
        body, grid=(T // TB,), in_specs=[rows(QL), const(w), rows(QK), rows(QK), const(pm)],
        out_specs=pl.BlockSpec((HEADS, TB, QK), lambda i: (0, i, 0)), out_shape=jax.ShapeDtypeStruct((HEADS, T, QK), BF16),
        compiler_params=_cp(("parallel",)), name=name)(cqn, w, cosf, sinf, pm)


def project_kv(ckvn, w, p0, kr_block, name):
    T = ckvn.shape[0]
    cosf, sinf, pm, _ = _rope_tables(T, 128, 0)
    s_kn, s_kr, s_v = (jnp.asarray(s, BF16) for s in _kv_selectors())

    def body(a_ref, w_ref, kr_ref, c_ref, s_ref, p_ref, skn_ref, skr_ref, sv_ref, k_ref, v_ref):
        a = a_ref[...]
        krr = _rot(kr_ref[...], c_ref[...], s_ref[...], p_ref[...], False).astype(BF16)
        kr_part = jnp.dot(krr, skr_ref[...], preferred_element_type=F32)
        for h in range(HEADS):
            kvb = _dotf(a, w_ref[pl.ds(h * KVW, KVW), :], "nt").astype(BF16)
            k_ref[h] = (jnp.dot(kvb, skn_ref[...], preferred_element_type=F32) + kr_part).astype(BF16)
            v_ref[h] = jnp.dot(kvb, sv_ref[...], preferred_element_type=F32).astype(BF16)

    rows = lambda c: pl.BlockSpec((TB, c), lambda i: (i, 0))
    const = lambda x: pl.BlockSpec(x.shape, lambda i: (0, 0))
    return pl.pallas_call(
        body, grid=(T // TB,),
        in_specs=[rows(KVL), const(w), pl.BlockSpec((TB, 128), lambda i: (i, kr_block)),
                  rows(128), rows(128), const(pm), const(s_kn), const(s_kr), const(s_v)],
        out_specs=[pl.BlockSpec((HEADS, TB, QK), lambda i: (0, i, 0)), pl.BlockSpec((HEADS, TB, VD), lambda i: (0, i, 0))],
        out_shape=[jax.ShapeDtypeStruct((HEADS, T, QK), BF16), jax.ShapeDtypeStruct((HEADS, T, VD), BF16)],
        compiler_params=_cp(("parallel",)), name=name)(ckvn, w, p0, cosf, sinf, pm, s_kn, s_kr, s_v)


def split_kv_grads(dk, dv, name):
    H, T, _ = dk.shape
    cosf, sinf, _, pmt = _rope_tables(T, 128, 0)
    s_kn, s_kr, s_v = _kv_selectors()
    s_knt, s_krt, s_vt = (jnp.asarray(s.T, BF16) for s in (s_kn, s_kr, s_v))

    def body(dk_ref, dv_ref, c_ref, s_ref, p_ref, skn_ref, skr_ref, sv_ref, dkv_ref, dkr_ref):
        total = None
        for h in range(H):
            dkh = dk_ref[h] * (1.0 / LOG2E)
            total = dkh if total is None else total + dkh
            dkv_ref[:, pl.ds(h * KVW, KVW)] = (
                jnp.dot(dkh.astype(BF16), skn_ref[...], preferred_element_type=F32)
                + jnp.dot(dv_ref[h].astype(BF16), sv_ref[...], preferred_element_type=F32)).astype(BF16)
        dkr_ref[...] = _rot(_exact_perm(total, skr_ref[...]), c_ref[...], s_ref[...], p_ref[...], True)

    rows = lambda c: pl.BlockSpec((TB, c), lambda i: (i, 0))
    const = lambda a: pl.BlockSpec(a.shape, lambda i: (0, 0))
    return pl.pallas_call(
        body, grid=(T // TB,),
        in_specs=[pl.BlockSpec((H, TB, QK), lambda i: (0, i, 0)), pl.BlockSpec((H, TB, VD), lambda i: (0, i, 0)),
                  rows(128), rows(128), const(pmt), const(s_knt), const(s_krt), const(s_vt)],
        out_specs=[rows(H * KVW), rows(128)],
        out_shape=[jax.ShapeDtypeStruct((T, H * KVW), BF16), jax.ShapeDtypeStruct((T, 128), F32)],
        compiler_params=_cp(("parallel",)), name=name)(dk, dv, cosf, sinf, pmt, s_knt, s_krt, s_vt)


HB = 4


def _by_query_block(run, T):
    @pl.when(pl.program_id(1) == 0)
    def _():
        run(LC)

    @pl.when(pl.program_id(1) > 0)
    def _():
        run(T)


def _with_rider(body, nin, nout, ride, grid):
    if ride is None:
        return body
    n = ride.n

    def wrapped(*refs):
        ins, xs = refs[:nin], refs[nin:nin + n]
        outs, got = refs[nin + n:nin + n + nout], refs[nin + n + nout:nin + 2 * n + nout]
        sems = refs[nin + 2 * n + nout:]
        step = pl.program_id(0) * grid[1] + pl.program_id(1)

        @pl.when(step == 0)
        def _():
            ride.start(xs, got, sems)

        body(*ins, *outs)

        @pl.when(step == grid[0] * grid[1] - 1)
        def _():
            ride.finish(xs, got, sems)

    return wrapped


def _ride_call(body, grid, in_specs, out_specs, out_shape, ride, rode, name, args):
    if ride is None:
        return pl.pallas_call(body, grid=grid, in_specs=in_specs, out_specs=out_specs, out_shape=out_shape,
                              compiler_params=_cp(("parallel", "arbitrary")), name=name)(*args), []
    res = pl.pallas_call(
        _with_rider(body, len(in_specs), len(out_specs), ride, grid), grid=grid,
        in_specs=in_specs + ride.specs, out_specs=out_specs + ride.specs, out_shape=out_shape + ride.out_shape,
        scratch_shapes=ride.scratch,
        compiler_params=pltpu.CompilerParams(dimension_semantics=("arbitrary", "arbitrary"), vmem_limit_bytes=VMEM_LIMIT,
                                             has_side_effects=True), name=name)(*args, *rode)
    return res[:len(out_specs)], res[len(out_specs):]


def attn_fwd(q, k, v, name, rode=None, modes=None):
    H, T, _ = q.shape

    def body(q_ref, k_ref, v_ref, o_ref, lse_ref):
        def run(nk):
            for hh in range(HB):
                s = _dotf(q_ref[hh], k_ref[hh, pl.ds(0, nk), :], "nt")
                m = jnp.max(s, axis=1, keepdims=True)
                p = jnp.exp2(s - m)
                l = jnp.sum(p, axis=1, keepdims=True)
                o = jnp.dot(p.astype(BF16), v_ref[hh, pl.ds(0, nk), :], preferred_element_type=F32)
                o_ref[hh] = o / l
                lse_ref[hh] = m + jnp.log2(l)

        _by_query_block(run, T)

    return _ride_call(
        body, (H // HB, T // TB),
        [pl.BlockSpec((HB, TB, QK), lambda h, i: (h, i, 0)), pl.BlockSpec((HB, T, QK), lambda h, i: (h, 0, 0)),
         pl.BlockSpec((HB, T, VD), lambda h, i: (h, 0, 0))],
        [pl.BlockSpec((HB, TB, VD), lambda h, i: (h, i, 0)), pl.BlockSpec((HB, TB, 1), lambda h, i: (h, i, 0))],
        [jax.ShapeDtypeStruct((H, T, VD), F32), jax.ShapeDtypeStruct((H, T, 1), F32)],
        Exchange(rode, modes) if rode else None, rode, name, (q, k, v))


def attn_bwd(q, k, v, o, lse, do, name, rode=None, modes=None):
    H, T, _ = q.shape

    def body(q_ref, k_ref, v_ref, o_ref, lse_ref, do_ref, dq_ref, dk_ref, dv_ref):
        i = pl.program_id(1)

        @pl.when(i == 0)
        def _():
            dk_ref[...] = jnp.zeros_like(dk_ref)
            dv_ref[...] = jnp.zeros_like(dv_ref)

        def run(nk):
            keys = pl.ds(0, nk)
            for hh in range(HB):
                qv, kv, dov = q_ref[hh], k_ref[hh, keys, :], do_ref[hh]
                p = jnp.exp2(_dotf(qv, kv, "nt") - lse_ref[hh])
                delta = jnp.sum(dov * o_ref[hh], axis=1, keepdims=True)
                dob = dov.astype(BF16)
                dv_ref[hh, keys, :] += _dotf(p.astype(BF16), dob, "tn")
                dp = _dotf(dob, v_ref[hh, keys, :], "nt")
                ds = (p * (dp - delta)).astype(BF16)
                dq_ref[hh] = jnp.dot(ds, kv, preferred_element_type=F32)
                dk_ref[hh, keys, :] += _dotf(ds, qv, "tn")

        _by_query_block(run, T)

    blk = lambda c: pl.BlockSpec((HB, TB, c), lambda h, i: (h, i, 0))
    full = lambda c: pl.BlockSpec((HB, T, c), lambda h, i: (h, 0, 0))
    return _ride_call(
        body, (H // HB, T // TB), [blk(QK), full(QK), full(VD), blk(VD), blk(1), blk(VD)], [blk(QK), full(QK), full(VD)],
        [jax.ShapeDtypeStruct((H, T, QK), F32), jax.ShapeDtypeStruct((H, T, QK), F32), jax.ShapeDtypeStruct((H, T, VD), F32)],
        Exchange(rode, modes) if rode else None, rode, name, (q, k, v, o, lse, do))


def disc_fwd(a_re, a_im, ls, name):
    def body(ar_ref, ai_ref, ls_ref, lr_ref, li_ref, fr_ref, fi_ref):
        ar, ai = ar_ref[...], ai_ref[...]
        dt = jnp.exp(ls_ref[...])
        mag = jnp.exp(ar * dt)
        lr = mag * jnp.cos(ai * dt)
        li = mag * jnp.sin(ai * dt)
        den = ar * ar + ai * ai
        nr = lr - 1.0
        lr_ref[...] = lr
        li_ref[...] = li
        fr_ref[...] = (nr * ar + li * ai) / den
        fi_ref[...] = (li * ar - nr * ai) / den

    return pl.pallas_call(body, out_shape=[jax.ShapeDtypeStruct(a_re.shape, F32)] * 4, name=name)(a_re, a_im, ls)


def disc_b(f_re, f_im, b_re, b_im, name):
    def body(fr_ref, fi_ref, br_ref, bi_ref, or_ref, oi_ref):
        fr, fi, br, bi = fr_ref[...], fi_ref[...], br_ref[...], bi_ref[...]
        or_ref[...] = fr * br - fi * bi
        oi_ref[...] = fr * bi + fi * br

    fs, bs = _disc_b_specs()
    return pl.pallas_call(body, grid=(2, G * P // DISC_ROWS), in_specs=[fs, fs, bs, bs], out_specs=[bs, bs],
                          out_shape=[jax.ShapeDtypeStruct(b_re.shape, F32)] * 2, name=name)(f_re, f_im, b_re, b_im)


DISC_ROWS = G * P


def _disc_b_specs():
    return (pl.BlockSpec((1, DISC_ROWS, 1), lambda d, i: (d, i, 0)), pl.BlockSpec((1, DISC_ROWS, CH), lambda d, i: (d, i, 0)))


def disc_b_bwd(f_re, f_im, b_re, b_im, dbb_re, dbb_im, name):
    def body(fr_ref, fi_ref, br_ref, bi_ref, dr_ref, di_ref, dbr_ref, dbi_ref, dfr_ref, dfi_ref):
        fr, fi, br, bi, dr, di = fr_ref[...], fi_ref[...], br_ref[...], bi_ref[...], dr_ref[...], di_ref[...]
        dbr_ref[...] = fr * dr + fi * di
        dbi_ref[...] = fr * di - fi * dr
        dfr_ref[...] = jnp.sum(dr * br + di * bi, axis=-1, keepdims=True)
        dfi_ref[...] = jnp.sum(di * br - dr * bi, axis=-1, keepdims=True)

    fs, bs = _disc_b_specs()
    return pl.pallas_call(body, grid=(2, G * P // DISC_ROWS), in_specs=[fs, fs, bs, bs, bs, bs], out_specs=[bs, bs, fs, fs],
                          out_shape=[jax.ShapeDtypeStruct(b_re.shape, F32)] * 2 + [jax.ShapeDtypeStruct(f_re.shape, F32)] * 2,
                          name=name)(f_re, f_im, b_re, b_im, dbb_re, dbb_im)


def disc_a_bwd(a_re, a_im, ls, dlr, dli, dfr, dfi, name):
    def body(ar_ref, ai_ref, ls_ref, dlr_ref, dli_ref, dfr_ref, dfi_ref, dar_ref, dai_ref, dls_ref):
        ar, ai = ar_ref[...], ai_ref[...]
        dt = jnp.exp(ls_ref[...])
        mag = jnp.exp(ar * dt)
        cs, sn = jnp.cos(ai * dt), jnp.sin(ai * dt)
        lr, li = mag * cs, mag * sn
        den = ar * ar + ai * ai
        nr = lr - 1.0
        f_re = (nr * ar + li * ai) / den
        f_im = (li * ar - nr * ai) / den
        dn1 = dfr_ref[...] / den
        dn2 = dfi_ref[...] / den
        dden = -(dfr_ref[...] * f_re + dfi_ref[...] * f_im) / den
        dlr_t = dlr_ref[...] + dn1 * ar - dn2 * ai
        dli_t = dli_ref[...] + dn1 * ai + dn2 * ar
        dar = dn1 * nr + dn2 * li + dden * 2.0 * ar
        dai = dn1 * li - dn2 * nr + dden * 2.0 * ai
        dmag = dlr_t * cs + dli_t * sn
        dth = dli_t * lr - dlr_t * li
        dar_ref[...] = dar + dmag * mag * dt
        dai_ref[...] = dai + dth * dt
        dls_ref[...] = jnp.sum(dmag * mag * ar + dth * ai, axis=-1, keepdims=True) * dt

    return pl.pallas_call(body, out_shape=[jax.ShapeDtypeStruct(a_re.shape, F32)] * 2 +
                          [jax.ShapeDtypeStruct(ls.shape, F32)], name=name)(a_re, a_im, ls, dlr, dli, dfr, dfi)


def _cpow(lr, li, n):
    rr, ri = None, None
    br, bi = lr, li
    while n:
        if n & 1:
            if rr is None:
                rr, ri = br, bi
            else:
                rr, ri = rr * br - ri * bi, rr * bi + ri * br
        n >>= 1
        if n:
            br, bi = br * br - bi * bi, 2.0 * br * bi
    return rr, ri


UNROLL = 4


def _steps(trips, fn, init):
    main = trips // UNROLL

    def body(i, c):
        for j in range(UNROLL):
            c = fn(i * UNROLL + j, c)
        return c

    c = lax.fori_loop(0, main, body, init) if main else init
    for n in range(main * UNROLL, trips):
        c = fn(n, c)
    return c


def _seg_scan(xre, xim, lam8, pw, base, seglen, rev, init, fin_re, fin_im, ini_re, ini_im, prev=None):
    lr, li = lam8
    nsub = SEG // 8

    def rows(t, s):
        first = base + t * SEG + 8 * s
        return pl.ds(first if isinstance(first, int) else pl.multiple_of(first, 8), 8)

    tmap = (lambda n: seglen - 1 - n) if rev else (lambda n: n)
    zeros = tuple(jnp.zeros((8, SB), F32) for _ in range(2 * nsub))

    def advance(c, t):
        out = []
        for s in range(nsub):
            a, b = c[2 * s], c[2 * s + 1]
            out += [lr * a - li * b + xre[rows(t, s), :], lr * b + li * a + xim[rows(t, s), :]]
        return tuple(out)

    fin = _steps(seglen, lambda n, c: advance(c, tmap(n)), zeros)
    for s in range(nsub):
        fin_re[pl.ds(8 * s, 8), :] = fin[2 * s]
        fin_im[pl.ds(8 * s, 8), :] = fin[2 * s + 1]
    (cr, ci), (pr, pi) = init, pw
    for i in (range(SEG - 1, -1, -1) if rev else range(SEG)):
        ini_re[pl.ds(i, 1), :] = cr
        ini_im[pl.ds(i, 1), :] = ci
        cr, ci = pr * cr - pi * ci + fin_re[pl.ds(i, 1), :], pr * ci + pi * cr + fin_im[pl.ds(i, 1), :]
    tiles = lambda re, im: tuple(r[pl.ds(8 * s, 8), :] for s in range(nsub) for r in (re, im))
    start = tiles(ini_re, ini_im)

    def store(c, t):
        new = advance(c, t)
        for s in range(nsub):
            xre[rows(t, s), :] = new[2 * s]
            xim[rows(t, s), :] = new[2 * s + 1]
        return new

    if prev is None:
        _steps(seglen, lambda n, c: store(c, tmap(n)), start)
        return (cr, ci), None

    sre, sim, s_ini_re, s_ini_im = prev

    def acc_step(c, t, before):
        new = store(c[:2 * nsub], t)
        acc = []
        for s in range(nsub):
            (na, nb), (pre, pim) = new[2 * s:2 * s + 2], before[2 * s:2 * s + 2]
            acc += [c[2 * nsub + 2 * s] + na * pre + nb * pim, c[2 * nsub + 2 * s + 1] + nb * pre - na * pim]
        return new + tuple(acc)

    def body(n, c):
        t = tmap(n)
        tp = t - 1 if rev else t + 1
        return acc_step(c, t, tuple(r[rows(tp, s), :] for s in range(nsub) for r in (sre, sim)))

    c = _steps(seglen - 1, body, start + zeros)
    c = acc_step(c, 0 if rev else seglen - 1, tiles(s_ini_re, s_ini_im))
    acc = c[2 * nsub:]
    return (cr, ci), (sum(acc[0::2][1:], acc[0]), sum(acc[1::2][1:], acc[1]))


def _lam_tiles(lr, li, lens, conj=False):
    if conj:
        li = -li
    lam8 = (jnp.broadcast_to(lr, (8, SB)), jnp.broadcast_to(li, (8, SB)))
    return lam8, [_cpow(lr, li, n) for n in lens]


def _stretches(T):
    return ((0, LC // SEG), (LC, (T - LC) // SEG))


def _to_seg_order(src, dst, T):
    for base, seglen in _stretches(T):
        def body(t, carry, base=base, seglen=seglen):
            dst[pl.ds(pl.multiple_of(base + t * SEG, SEG), SEG), :] = src[pl.ds(base + t, SEG, stride=seglen), :]
            return carry
        lax.fori_loop(0, seglen, body, 0, unroll=8)


def _from_seg_order(src, dst, T):
    for base, seglen in _stretches(T):
        def body(t, carry, base=base, seglen=seglen):
            dst[pl.ds(base + t, SEG, stride=seglen), :] = src[pl.ds(pl.multiple_of(base + t * SEG, SEG), SEG), :]
            return carry
        lax.fori_loop(0, seglen, body, 0, unroll=8)


def _scan_specs(T):
    ublk = pl.BlockSpec((T, UB), lambda j: (0, j))
    lam = pl.BlockSpec((2, 1, 1, SB), lambda j: (0, j, 0, 0))
    mat = pl.BlockSpec((2, 1, UB, P), lambda j: (0, j, 0, 0))
    return ublk, lam, mat


def _dotf(a, b, mode="nn"):
    return lax.dot_general(a, b, _DN[mode], preferred_element_type=F32)


def _diag_mask():
    r = lax.broadcasted_iota(jnp.int32, (UB, SB), 0)
    c = lax.broadcasted_iota(jnp.int32, (UB, SB), 1)
    return lax.shift_right_logical(r, int(math.log2(CH))) == lax.shift_right_logical(c, int(math.log2(P)))


def _expand(m):
    p = lax.broadcasted_iota(jnp.int32, (P, SB), 0)
    c = lax.broadcasted_iota(jnp.int32, (P, SB), 1)
    tile = jnp.where(lax.bitwise_and(c, P - 1) == p, 1.0, 0.0).astype(BF16)
    wide = jnp.dot(m.astype(BF16), tile, preferred_element_type=F32)
    return jnp.where(_diag_mask(), wide, 0.0).astype(BF16)


def _collapse(full):
    c = lax.broadcasted_iota(jnp.int32, (SB, P), 0)
    p = lax.broadcasted_iota(jnp.int32, (SB, P), 1)
    pick = jnp.where(lax.bitwise_and(c, P - 1) == p, 1.0, 0.0).astype(BF16)
    return _exact_perm(jnp.where(_diag_mask(), full, 0.0), pick)


def _zero_state():
    return jnp.zeros((1, SB), F32), jnp.zeros((1, SB), F32)


def scan_fwd(u, lam_re, lam_im, bre, bim, cre, cim, name):
    T = u.shape[0]
    s_ctx, s_lat = LC // SEG, (T - LC) // SEG

    def body(u_ref, lr_ref, li_ref, bre_ref, bim_ref, cre_ref, cim_ref, y_ref, us, ys, sre, sim, fre, fim, ire, iim):
        _to_seg_order(u_ref, us, T)
        ub = us[...].astype(BF16)
        for d in range(2):
            lam8, (pw_c, pw_l) = _lam_tiles(lr_ref[d, 0], li_ref[d, 0], (s_ctx, s_lat))
            sre[...] = _dotf(ub, _expand(bre_ref[d, 0]))
            sim[...] = _dotf(ub, _expand(bim_ref[d, 0]))
            end_c, _ = _seg_scan(sre, sim, lam8, pw_c, 0, s_ctx, bool(d), _zero_state(), fre, fim, ire, iim)
            _seg_scan(sre, sim, lam8, pw_l, LC, s_lat, bool(d), end_c, fre, fim, ire, iim)
            y = (_dotf(sre[...].astype(BF16), _expand(cre_ref[d, 0]), "nt")
                 - _dotf(sim[...].astype(BF16), _expand(cim_ref[d, 0]), "nt"))
            if d == 0:
                ys[...] = y
            else:
                ys[...] += y
        _from_seg_order(ys, y_ref, T)

    ublk, lam, mat = _scan_specs(T)
    return pl.pallas_call(
        body, grid=(NJ,), in_specs=[ublk, lam, lam, mat, mat, mat, mat], out_specs=ublk,
        out_shape=jax.ShapeDtypeStruct((T, G * CH), F32),
        scratch_shapes=[pltpu.VMEM((T, UB), F32)] * 2 + [pltpu.VMEM((T, SB), F32)] * 2 + [pltpu.VMEM((SEG, SB), F32)] * 4,
        compiler_params=_cp(("arbitrary",)), name=name)(u, lam_re, lam_im, bre, bim, cre, cim)


def scan_bwd(u, dy, lam_re, lam_im, bre, bim, cre, cim, name):
    T = u.shape[0]
    s_ctx, s_lat = LC // SEG, (T - LC) // SEG

    def body(u_ref, dy_ref, lr_ref, li_ref, bre_ref, bim_ref, cre_ref, cim_ref,
             du_ref, dlr_ref, dli_ref, dbre_ref, dbim_ref, dcre_ref, dcim_ref,
             us, dys, dus, sre, sim, gre, gim, fre, fim, ic_re, ic_im, il_re, il_im, jre, jim):
        _to_seg_order(u_ref, us, T)
        _to_seg_order(dy_ref, dys, T)
        ub, dyb = us[...].astype(BF16), dys[...].astype(BF16)
        for d in range(2):
            rev = bool(d)
            lam8, (pw_c, pw_l) = _lam_tiles(lr_ref[d, 0], li_ref[d, 0], (s_ctx, s_lat))
            cam8, (cw_c, cw_l) = _lam_tiles(lr_ref[d, 0], li_ref[d, 0], (s_ctx, s_lat), conj=True)
            bre_v, bim_v = _expand(bre_ref[d, 0]), _expand(bim_ref[d, 0])
            sre[...] = _dotf(ub, bre_v)
            sim[...] = _dotf(ub, bim_v)
            end_c, _ = _seg_scan(sre, sim, lam8, pw_c, 0, s_ctx, rev, _zero_state(), fre, fim, ic_re, ic_im)
            _seg_scan(sre, sim, lam8, pw_l, LC, s_lat, rev, end_c, fre, fim, il_re, il_im)
            gre[...] = _dotf(dyb, _expand(cre_ref[d, 0]))
            gim[...] = -_dotf(dyb, _expand(cim_ref[d, 0]))
            end_g, acc_l = _seg_scan(gre, gim, cam8, cw_l, LC, s_lat, not rev, _zero_state(), fre, fim, jre, jim,
                                     prev=(sre, sim, il_re, il_im))
            _, acc_c = _seg_scan(gre, gim, cam8, cw_c, 0, s_ctx, not rev, end_g, fre, fim, jre, jim,
                                 prev=(sre, sim, ic_re, ic_im))
            dlr_ref[d, 0] = _sum0(acc_l[0] + acc_c[0])
            dli_ref[d, 0] = _sum0(acc_l[1] + acc_c[1])
            grb, gib = gre[...].astype(BF16), gim[...].astype(BF16)
            du = _dotf(grb, bre_v, "nt") + _dotf(gib, bim_v, "nt")
            if d == 0:
                dus[...] = du
            else:
                dus[...] += du
            dbre_ref[d, 0] = _collapse(_dotf(ub, grb, "tn"))
            dbim_ref[d, 0] = _collapse(_dotf(ub, gib, "tn"))
            dcre_ref[d, 0] = _collapse(_dotf(dyb, sre[...].astype(BF16), "tn"))
            dcim_ref[d, 0] = -_collapse(_dotf(dyb, sim[...].astype(BF16), "tn"))
        _from_seg_order(dus, du_ref, T)

    ublk, lam, mat = _scan_specs(T)
    lam_s = jax.ShapeDtypeStruct(lam_re.shape, F32)
    mat_s = jax.ShapeDtypeStruct(bre.shape, F32)
    return pl.pallas_call(
        body, grid=(NJ,), in_specs=[ublk, ublk, lam, lam, mat, mat, mat, mat],
        out_specs=[ublk, lam, lam, mat, mat, mat, mat],
        out_shape=[jax.ShapeDtypeStruct((T, G * CH), F32), lam_s, lam_s, mat_s, mat_s, mat_s, mat_s],
        scratch_shapes=[pltpu.VMEM((T, UB), F32)] * 3 + [pltpu.VMEM((T, SB), F32)] * 4 + [pltpu.VMEM((SEG, SB), F32)] * 8,
        compiler_params=_cp(("arbitrary",)), name=name)(u, dy, lam_re, lam_im, bre, bim, cre, cim)


class Exchange:
    def __init__(self, xs, modes):
        self.n = len(xs)
        self.modes = [modes] * self.n if isinstance(modes, (str, int)) else list(modes)
        self.out_shape = [jax.ShapeDtypeStruct(self._shape(x, md), x.dtype) for x, md in zip(xs, self.modes)]
        self.scratch = [pltpu.SemaphoreType.DMA((NDEV - 1, self.n)), pltpu.SemaphoreType.DMA((NDEV - 1, self.n)),
                        pltpu.SemaphoreType.DMA((self.n,))]
        self.specs = [pl.BlockSpec(memory_space=pl.ANY)] * self.n

    @staticmethod
    def _shape(x, mode):
        if mode == "gather":
            return (NDEV,) + tuple(x.shape)
        return tuple(x.shape) if mode == "lead" else (NDEV, x.shape[0], mode) + tuple(x.shape[2:])

    @staticmethod
    def _piece(x_ref, mode, dev):
        if mode == "gather":
            return x_ref
        return x_ref.at[dev] if mode == "lead" else x_ref.at[:, pl.ds(dev * mode, mode)]

    def _copies(self, x_refs, out_refs, sems):
        send_sems, recv_sems, local_sems = sems
        mx, my, mc = lax.axis_index("x"), lax.axis_index("y"), lax.axis_index("c")
        me = 4 * mx + 2 * my + mc
        peer_of = lambda k: (1 - mx if k & 4 else mx, 1 - my if k & 2 else my, 1 - mc if k & 1 else mc)
        local, first, relay, arrivals = [], [], [], []
        for a, (x_ref, out_ref) in enumerate(zip(x_refs, out_refs)):
            mode = self.modes[a]
            local.append(pltpu.make_async_copy(self._piece(x_ref, mode, me), out_ref.at[me], local_sems.at[a]))

            def remote(src, dst, k, pair, a=a):
                return pltpu.make_async_remote_copy(src_ref=src, dst_ref=dst, send_sem=send_sems.at[pair, a],
                                                    recv_sem=recv_sems.at[pair, a], device_id=peer_of(k), device_id_type=MESH_T)

            for k in range(1, NDEV):
                peer = peer_of(k)
                pid = 4 * peer[0] + 2 * peer[1] + peer[2]
                if mode != "gather":
                    src = self._piece(x_ref, mode, pid)
                    first.append(remote(src, out_ref.at[me], k, k - 1))
                    arrivals.append(remote(src, out_ref.at[pid], k, k - 1))
                elif k == 1:
                    first.append(remote(x_ref, out_ref.at[me], k, k - 1))
                    arrivals.append(remote(x_ref, out_ref.at[pid], k, k - 1))
                elif k % 2 == 0:
                    first.append(remote(x_ref, out_ref.at[me], k, k - 1))
                    relay.append((remote(x_ref, out_ref.at[pid], k, k - 1), remote(out_ref.at[pid], out_ref.at[pid], 1, k)))
                else:
                    arrivals.append(remote(x_ref, out_ref.at[pid], 1, k - 1))
        return local, first, relay, arrivals

    def start(self, x_refs, out_refs, sems):
        local, first, _, _ = self._copies(x_refs, out_refs, sems)
        for cp in local + first:
            cp.start()

    def finish(self, x_refs, out_refs, sems):
        local, first, relay, arrivals = self._copies(x_refs, out_refs, sems)
        for arrival, onward in relay:
            arrival.wait_recv()
            onward.start()
        for cp in arrivals:
            cp.wait_recv()
        for cp in first + [onward for _, onward in relay]:
            cp.wait_send()
        for cp in local:
            cp.wait()


def exchange(xs, modes, name):
    ex = Exchange(xs, modes)
    n = ex.n

    def body(*refs):
        ex.start(refs[:n], refs[n:2 * n], refs[2 * n:])
        ex.finish(refs[:n], refs[n:2 * n], refs[2 * n:])

    return pl.pallas_call(body, in_specs=ex.specs, out_specs=ex.specs, out_shape=ex.out_shape, scratch_shapes=ex.scratch,
                          compiler_params=pltpu.CompilerParams(has_side_effects=True), name=name)(*xs)


def _dot_f32(a, b, dn):
    return lax.dot_general(a, b, dn, preferred_element_type=F32, precision=lax.Precision.HIGHEST)


def ada_fwd(cg, c_ctx, ada_w, ada_b_loc, name):
    W = ada_w.shape[2]

    def body(cg_ref, cc_ref, w_ref, b_ref, o_ref):
        a = jnp.concatenate([_silu(cg_ref[...]), jnp.broadcast_to(_silu(cc_ref[...]), (NDEV, D))], axis=0)
        for i in range(2):
            o_ref[i] = _dot_f32(a, w_ref[i], _DN["nn"]) + b_ref[i]

    return pl.pallas_call(body, out_shape=jax.ShapeDtypeStruct((2, 2 * NDEV, W), F32),
                          compiler_params=_cp(), name=name)(cg, c_ctx, ada_w, ada_b_loc)


def ada_bwd(cg, c_ctx, ada_w, dm_loc, dm_all, name):
    W = ada_w.shape[2]

    def body(cg_ref, cc_ref, w_ref, dl_ref, da_ref, gw_ref, dcc_ref, gb_ref):
        a = jnp.concatenate([_silu(cg_ref[...]), jnp.broadcast_to(_silu(cc_ref[...]), (NDEV, D))], axis=0)
        dcc = jnp.zeros((1, D), F32)
        for i in range(2):
            dl = dl_ref[i]
            gw_ref[i] = _dot_f32(a, dl, _DN["tn"])
            dctx = jnp.sum(dl[NDEV:], axis=0, keepdims=True)
            dcc = dcc + _dot_f32(dctx, w_ref[i], _DN["nt"])
        dcc_ref[...] = dcc
        gb_ref[...] = jnp.sum(da_ref[...], axis=0)

    return pl.pallas_call(body, out_shape=[jax.ShapeDtypeStruct((2, D, W), F32), jax.ShapeDtypeStruct((1, D), F32),
                                           jax.ShapeDtypeStruct((2, 3 * D), F32)],
                          compiler_params=_cp(), name=name)(cg, c_ctx, ada_w, dm_loc, dm_all)


def cctx_finish(parts, c_ctx, name):
    def body(p_ref, cc_ref, o_ref):
        o_ref[...] = jnp.sum(p_ref[...], axis=0, keepdims=True) * _dsilu(cc_ref[...])

    return pl.pallas_call(body, out_shape=jax.ShapeDtypeStruct((1, D), F32), name=name)(parts, c_ctx)


def _adamw_update(g_ref, w_ref, m_ref, v_ref, go_ref, d_ref, mo_ref, vo_ref):
    g = g_ref[0].astype(F32)
    for s in range(1, g_ref.shape[0]):
        g = g + g_ref[s].astype(F32)
    mn = B1 * m_ref[...] + (1.0 - B1) * g
    vn = B2 * v_ref[...] + (1.0 - B2) * g * g
    go_ref[...] = g
    mo_ref[...] = mn
    vo_ref[...] = vn
    d_ref[...] = -LR * ((mn * (1.0 / (1.0 - B1 ** STEP))) / (jnp.sqrt(vn * (1.0 / (1.0 - B2 ** STEP))) + AEPS) + WD * w_ref[...])


ADAMW_PARTS = 4


def adamw_rows(items, name, rode=None, modes=None):
    in_specs, out_specs, out_shape, args = [], [], [], []
    for g, w, m, v in items:
        n, R, C = g.shape
        tr = R // ADAMW_PARTS
        spec = pl.BlockSpec((tr, C), lambda i, j: (i, 0))
        in_specs += [pl.BlockSpec((n, tr, C), lambda i, j: (0, i, 0)), spec, spec, spec]
        args += [g, w, m, v]
    for g, w, m, v in items:
        tr = w.shape[0] // ADAMW_PARTS
        out_specs += [pl.BlockSpec((tr, w.shape[1]), lambda i, j: (i, 0))] * 4
        out_shape += [jax.ShapeDtypeStruct(w.shape, F32)] * 4
    res, got = _ride_call(_adamw_body(len(items)), (ADAMW_PARTS, 1), in_specs, out_specs, out_shape,
                          Exchange(rode, modes) if rode else None, rode, name, args)
    return [res[4 * t:4 * t + 4] for t in range(len(items))], got


def _adamw_body(k):
    def body(*refs):
        for t in range(k):
            _adamw_update(*refs[4 * t:4 * t + 4], *refs[4 * k + 4 * t:4 * k + 4 * t + 4])
    return body


def adamw_multi(items, grid, name):
    k = len(items)
    ins, in_specs, out_specs, out_shape = [], [], [], []
    for g, g_spec, w, m, v, w_spec in items:
        ins += [g, w, m, v]
        in_specs += [g_spec, w_spec, w_spec, w_spec]
    for g, g_spec, w, m, v, w_spec in items:
        out_specs += [w_spec] * 4
        out_shape += [jax.ShapeDtypeStruct(w.shape, F32)] * 4
    res = pl.pallas_call(_adamw_body(k), grid=grid, in_specs=in_specs, out_specs=out_specs, out_shape=out_shape,
                         compiler_params=_cp(("arbitrary",) * len(grid)), name=name)(*ins)
    return [res[4 * t:4 * t + 4] for t in range(k)]


def _whole(a, grid_rank):
    zeros = (0,) * a.ndim
    return pl.BlockSpec(a.shape, lambda *idx: zeros)


def sum_slots(xs, name):
    def body(*refs):
        for x_ref, o_ref in zip(refs[:len(xs)], refs[len(xs):]):
            acc = x_ref[0]
            for s in range(1, NDEV):
                acc = acc + x_ref[s]
            o_ref[...] = acc

    return pl.pallas_call(body, out_shape=[jax.ShapeDtypeStruct(x.shape[1:], F32) for x in xs],
                          compiler_params=_cp(), name=name)(*xs)


def _col_shards(g):
    R, N = g.shape
    return g.reshape(R, NDEV, N // NDEV).transpose(1, 0, 2)


def _vec2(v):
    return jnp.broadcast_to(v.reshape(1, 1, -1), (2, 1, v.size))


SHARD_ROWS = {"mla_w_in": 192, "mla_w_uq": 192, "mla_w_ukv": 256, "s5_w_in": 256}


def _t_shard(wsh, rows):
    t = wsh[0].T.astype(BF16)
    return jnp.pad(t, ((0, rows - t.shape[0]), (0, 0)))


def _win_order():
    w = IN_W // NDEV
    perm = np.zeros((IN_WP, NDEV * SHARD_ROWS["mla_w_in"]), np.float32)
    first = QL + KVL + ROPE
    for c in range(IN_W):
        n = c + HEADS * VD if c < first else c - first
        perm[n, (c // w) * SHARD_ROWS["mla_w_in"] + c % w] = 1.0
    return jnp.asarray(perm, BF16)


def local_step(ctx, x, tgt, mod, Wt, small, l1_shards):
    T = LC + x.shape[0]
    xa = ("cat", ctx, x)
    sh = [mod[i, :, None, 0:D] for i in range(2)]
    sc = [mod[i, :, None, D:2 * D] for i in range(2)]
    gt = [mod[i, :, None, 2 * D:] for i in range(2)]
    ng = [_vec2(small["norm_g"][i]) for i in range(2)]
    qg, kvg = _vec2(small["mla_q_norm"]), _vec2(small["mla_kv_norm"])
    cosf, sinf, pm, pmt = _rope_tables(T)

    (h0, p0, cqn, ckvn), _ = rowwise(st_l0_pre, [xa], [ng[0], sc[0], sh[0], qg, kvg],
                                     [(D, BF16), (IN_WP, F32), (QL, BF16), (KVL, BF16)], [], "l0_pre", mats=[Wt["mla_w_in"]])
    z0, cq, ckv = (p0, 0, HEADS * VD), (p0, HEADS * VD // QL, QL), (p0, (HEADS * VD + QL) // KVL, KVL)
    Q = project_q(cqn, Wt["mla_w_uq"], cosf, sinf, pm, "l0_uq")
    K, V = project_kv(ckvn, Wt["mla_w_ukv"], p0, (HEADS * VD + QL + KVL) // 128, "l0_ukv")
    (o, lse), got = attn_fwd(Q, K, V, "l0_attn", rode=l1_shards, modes="gather")
    Wt, small = dict(Wt), dict(small)
    for n, a in zip(L1_BIG, got):
        Wt[n] = a.reshape(-1, a.shape[-1])
    vecs = lax.bitcast_convert_type(got[-1].reshape(NDEV, 2, -1, 2), F32)
    small["s5_d"], small["s5_b_glu"] = vecs[:, 0, :].reshape(D), vecs[:, 1, :].reshape(D)
    o2 = o.transpose(1, 0, 2).reshape(T, HEADS * VD)
    (og, out0, x1), _ = rowwise(st_l0_post, [o2, z0, xa], [gt[0]], [(D, BF16), (D, BF16), (D, F32)], [], "l0_post",
                                mats=[Wt["mla_w_out"]])

    ls = small["s5_log_step"].reshape(2, G, 1)
    a_re, a_im = small["s5_a_re"].reshape(2, G, P), small["s5_a_im"].reshape(2, G, P)
    b_re, b_im = small["s5_b_re"].reshape(2, G * P, CH), small["s5_b_im"].reshape(2, G * P, CH)
    lam_re, lam_im, f_re, f_im = disc_fwd(a_re, a_im, ls, "s5_disc")
    f_re2, f_im2 = f_re.reshape(2, G * P, 1), f_im.reshape(2, G * P, 1)
    bb_re, bb_im = disc_b(f_re2, f_im2, b_re, b_im, "s5_disc_b")
    compact = lambda m: m.reshape(2, NJ, UB, P)
    bre = compact(bb_re.reshape(2, G, P, CH).transpose(0, 1, 3, 2))
    bim = compact(bb_im.reshape(2, G, P, CH).transpose(0, 1, 3, 2))
    cre, cim = compact(small["s5_c_re"]), compact(small["s5_c_im"])
    lam_re4, lam_im4 = lam_re.reshape(2, NJ, 1, SB), lam_im.reshape(2, NJ, 1, SB)

    (h1, p1), _ = rowwise(st_l1_pre, [x1], [ng[1], sc[1], sh[1]], [(D, BF16), (2 * D, F32)], [], "l1_pre", mats=[Wt["s5_w_in"]])
    u, z1 = (p1, 0, D), (p1, 1, D)
    yssm = scan_fwd(p1, lam_re4, lam_im4, bre, bim, cre, cim, "s5_scan")
    dvec, bglu = _vec2(small["s5_d"]), _vec2(small["s5_b_glu"])
    fg = _vec2(small["final_g"])
    lat_mask = jnp.stack([jnp.zeros((1, D), F32), jnp.ones((1, D), F32)])
    (y, y1b, gl, y3, out1, dx2), (dfg, lvec) = rowwise(
        st_l1_mlp, [yssm, u, z1, x1, ("lat", tgt)], [dvec, bglu, gt[1], fg, lat_mask],
        [(D, F32), (D, BF16), (D, BF16), (D, BF16), (D, BF16), (D, F32)], [D, 128], "l1_mlp",
        mats=[Wt["s5_w_glu"], Wt["s5_w_out"]])

    (dz1, dy, du_d), (dgt1, dbglu, dd), (g_w_out5, g_w_glu) = rowwise(
        st_l1_mlp_bwd, [dx2, out1, y3, y, gl, z1, u, y1b], [gt[1], bglu, dvec], [(D, BF16), (D, F32), (D, F32)], [D, D, D],
        "l1_mlp_b", mats=[Wt["s5_w_out"], Wt["s5_w_glu"]], out_accs=[(D, D), (D, D)])
    du_s, dlr, dli, dbre, dbim, dcre, dcim = scan_bwd(p1, dy, lam_re4, lam_im4, bre, bim, cre, cim, "s5_scan_b")
    dbb_re = dbre.reshape(2, G, CH, P).transpose(0, 1, 3, 2).reshape(2, G * P, CH)
    dbb_im = dbim.reshape(2, G, CH, P).transpose(0, 1, 3, 2).reshape(2, G * P, CH)
    g_c_re, g_c_im = dcre.reshape(2, G, CH, P), dcim.reshape(2, G, CH, P)
    g_b_re, g_b_im, dfr, dfi = disc_b_bwd(f_re2, f_im2, b_re, b_im, dbb_re, dbb_im, "s5_disc_b_b")
    g_a_re, g_a_im, g_ls = disc_a_bwd(a_re, a_im, ls, dlr.reshape(2, G, P), dli.reshape(2, G, P),
                                      dfr.reshape(2, G, P), dfi.reshape(2, G, P), "s5_disc_b_a")
    (dx1,), (dsh1, dsc1, dng1), (g_w_in5,) = rowwise(
        st_l1_tail_bwd, [du_d, du_s, dz1, h1, x1, dx2], [ng[1], sc[1]], [(D, F32)], [D, D, D], "l1_pre_b",
        mats=[Wt["s5_w_in"]], out_accs=[(D, 2 * D)])
    g_w_in5 = _col_shards(g_w_in5)

    (do2, dz0), (dgt0,), (g_w_out,) = rowwise(st_l0_post_bwd, [dx1, out0, og, o2, z0], [gt[0]], [(D, F32), (D, F32)], [D],
                                              "l0_post_b", mats=[Wt["mla_w_out"]], out_accs=[(D, D)])
    doh = do2.reshape(T, HEADS, VD).transpose(1, 0, 2)
    rows8 = lambda g: g.reshape(NDEV, -1, g.shape[-1])
    both = lambda s: s[0, 0] + s[1, 0]
    dense = lambda g: g.reshape(2, G * P * CH // 128, 128)
    chunks = [dense(g_b_re), dense(g_b_im), g_c_re, g_c_im]
    l1_send = [g_w_in5, rows8(g_w_glu), rows8(g_w_out5), rows8(g_w_out),
               both(dd).reshape(NDEV, 1, -1), both(dbglu).reshape(NDEV, 1, -1)]
    (dQ, dK, dV), l1_recv = attn_bwd(Q, K, V, o, lse, doh, "l0_attn_b", rode=l1_send + chunks,
                                     modes=["lead"] * len(l1_send) + [a.shape[1] // NDEV for a in chunks])
    dqh = rope(dQ, cosf, sinf, pmt, True, BF16, "l0_rope_q_b", scale=SCALE)
    dq = dqh.transpose(1, 0, 2).reshape(T, HEADS * QK)
    dkv, dkr = split_kv_grads(dK, dV, "l0_kv_b")
    (grad_x,), (dqg, dkvg, dsh0, dsc0, dng0), (g_uq, g_ukv, g_p) = rowwise(
        st_l0_tail_bwd, [dq, dkv, dkr, dz0, cq, ckv, cqn, ckvn, h0, xa, dx1], [qg, kvg, ng[0], sc[0]],
        [(D, F32, "lat")], [QL, KVL, D, D, D], "l0_pre_b", mats=[Wt["mla_w_uq"], Wt["mla_w_ukv"], Wt["mla_w_in"]],
        out_accs=[(QL, HEADS * QK), (KVL, HEADS * KVW), (D, IN_WP)])
    g_w_uq, g_w_ukv = _col_shards(g_uq).astype(BF16), _col_shards(g_ukv).astype(BF16)
    g_w_in = _col_shards(jnp.concatenate([g_p[:, HEADS * VD:IN_W], g_p[:, :HEADS * VD]], axis=1)).astype(BF16)

    dmod = jnp.stack([jnp.concatenate([dsh0, dsc0, dgt0], axis=-1)[:, 0], jnp.concatenate([dsh1, dsc1, dgt1], axis=-1)[:, 0]])
    gbig = {"mla_w_in": g_w_in, "mla_w_uq": g_w_uq, "mla_w_ukv": g_w_ukv}
    gsmall = {"norm_g": jnp.stack([both(dng0), both(dng1)]), "mla_q_norm": both(dqg), "mla_kv_norm": both(dkvg),
              "s5_a_re": g_a_re, "s5_a_im": g_a_im, "s5_log_step": g_ls, "final_g": dfg[1, 0]}
    return lvec[1], grad_x, dmod, gbig, gsmall, l1_recv


COL_SHARDED = ("mla_w_in", "mla_w_uq", "mla_w_ukv", "s5_w_in")
ROW_SHARDED = ("mla_w_out", "s5_w_glu", "s5_w_out")
VEC_SHARDED = ("s5_d", "s5_b_glu")
BIG = COL_SHARDED + ROW_SHARDED
L0_BIG = ("mla_w_in", "mla_w_uq", "mla_w_ukv")
L1_BIG = ("s5_w_in", "s5_w_glu", "s5_w_out", "mla_w_out")
BITS16 = jnp.bfloat16
SMALL_RS = ("norm_g", "mla_q_norm", "mla_kv_norm", "s5_a_re", "s5_a_im", "s5_log_step", "s5_b_re", "s5_b_im",
            "s5_c_re", "s5_c_im", "final_g")
CHUNKED = ("s5_b_re", "s5_b_im", "s5_c_re", "s5_c_im")
DENSE = ("s5_b_re", "s5_b_im")
TINY = ("norm_g", "mla_q_norm", "mla_kv_norm", "s5_a_re", "s5_a_im", "s5_log_step", "final_g")
ORDER = ("c_ctx", "ada_w", "ada_b", "norm_g", "mla_w_in", "mla_q_norm", "mla_w_uq", "mla_kv_norm", "mla_w_ukv",
         "mla_w_out", "s5_w_in", "s5_a_re", "s5_a_im", "s5_log_step", "s5_b_re", "s5_b_im", "s5_c_re", "s5_c_im",
         "s5_d", "s5_w_glu", "s5_b_glu", "s5_w_out", "final_g")


def kernel(x, c, ctx, c_ctx, ada_w, ada_b, norm_g, mla_w_in, mla_q_norm, mla_w_uq, mla_kv_norm, mla_w_ukv, mla_w_out, s5_w_in, s5_a_re, s5_a_im, s5_log_step, s5_b_re, s5_b_im, s5_c_re, s5_c_im, s5_d, s5_w_glu, s5_b_glu, s5_w_out, final_g, loss_target, m_c_ctx, m_ada_w, m_ada_b, m_norm_g, m_mla_w_in, m_mla_q_norm, m_mla_w_uq, m_mla_kv_norm, m_mla_w_ukv, m_mla_w_out, m_s5_w_in, m_s5_a_re, m_s5_a_im, m_s5_log_step, m_s5_b_re, m_s5_b_im, m_s5_c_re, m_s5_c_im, m_s5_d, m_s5_w_glu, m_s5_b_glu, m_s5_w_out, m_final_g, v_c_ctx, v_ada_w, v_ada_b, v_norm_g, v_mla_w_in, v_mla_q_norm, v_mla_w_uq, v_mla_kv_norm, v_mla_w_ukv, v_mla_w_out, v_s5_w_in, v_s5_a_re, v_s5_a_im, v_s5_log_step, v_s5_b_re, v_s5_b_im, v_s5_c_re, v_s5_c_im, v_s5_d, v_s5_w_glu, v_s5_b_glu, v_s5_w_out, v_final_g):
    w = dict(c_ctx=c_ctx, ada_w=ada_w, ada_b=ada_b, norm_g=norm_g, mla_w_in=mla_w_in, mla_q_norm=mla_q_norm,
             mla_w_uq=mla_w_uq, mla_kv_norm=mla_kv_norm, mla_w_ukv=mla_w_ukv, mla_w_out=mla_w_out, s5_w_in=s5_w_in,
             s5_a_re=s5_a_re, s5_a_im=s5_a_im, s5_log_step=s5_log_step, s5_b_re=s5_b_re, s5_b_im=s5_b_im,
             s5_c_re=s5_c_re, s5_c_im=s5_c_im, s5_d=s5_d, s5_w_glu=s5_w_glu, s5_b_glu=s5_b_glu, s5_w_out=s5_w_out,
             final_g=final_g)
    m = dict(c_ctx=m_c_ctx, ada_w=m_ada_w, ada_b=m_ada_b, norm_g=m_norm_g, mla_w_in=m_mla_w_in, mla_q_norm=m_mla_q_norm,
             mla_w_uq=m_mla_w_uq, mla_kv_norm=m_mla_kv_norm, mla_w_ukv=m_mla_w_ukv, mla_w_out=m_mla_w_out,
             s5_w_in=m_s5_w_in, s5_a_re=m_s5_a_re, s5_a_im=m_s5_a_im, s5_log_step=m_s5_log_step, s5_b_re=m_s5_b_re,
             s5_b_im=m_s5_b_im, s5_c_re=m_s5_c_re, s5_c_im=m_s5_c_im, s5_d=m_s5_d, s5_w_glu=m_s5_w_glu,
             s5_b_glu=m_s5_b_glu, s5_w_out=m_s5_w_out, final_g=m_final_g)
    v = dict(c_ctx=v_c_ctx, ada_w=v_ada_w, ada_b=v_ada_b, norm_g=v_norm_g, mla_w_in=v_mla_w_in, mla_q_norm=v_mla_q_norm,
             mla_w_uq=v_mla_w_uq, mla_kv_norm=v_mla_kv_norm, mla_w_ukv=v_mla_w_ukv, mla_w_out=v_mla_w_out,
             s5_w_in=v_s5_w_in, s5_a_re=v_s5_a_re, s5_a_im=v_s5_a_im, s5_log_step=v_s5_log_step, s5_b_re=v_s5_b_re,
             s5_b_im=v_s5_b_im, s5_c_re=v_s5_c_re, s5_c_im=v_s5_c_im, s5_d=v_s5_d, s5_w_glu=v_s5_w_glu,
             s5_b_glu=v_s5_b_glu, s5_w_out=v_s5_w_out, final_g=v_final_g)

    me = 4 * lax.axis_index("x") + 2 * lax.axis_index("y") + lax.axis_index("c")
    WA = ada_w.shape[2]

    def shard(n):
        return _t_shard(w[n], SHARD_ROWS[n]) if n in COL_SHARDED else w[n][0].astype(BF16)

    wgot = exchange([c] + [shard(n) for n in L0_BIG], "gather", "gather_w")

    cg = wgot[0].reshape(NDEV, D)
    cc2 = c_ctx.reshape(1, D)
    ada_b_loc = lax.dynamic_slice_in_dim(ada_b.reshape(2, 3 * D // WA, WA), me, 1, axis=1)
    part = ada_fwd(cg, cc2, ada_w, ada_b_loc, "ada_fwd")
    pg = exchange([part], "gather", "gather_mod")[0]
    mod_l = lax.dynamic_index_in_dim(pg, me, axis=2, keepdims=False).transpose(1, 0, 2).reshape(2, 3 * D)
    mod_c = pg[:, :, NDEV, :].transpose(1, 0, 2).reshape(2, 3 * D)
    mod = jnp.stack([mod_c, mod_l], axis=1)

    Wt = {n: a.reshape(-1, a.shape[-1]) for n, a in zip(L0_BIG, wgot[1:])}
    Wt["mla_w_in"] = mm(_win_order(), Wt["mla_w_in"], "nn", "w_in_order", out_dtype=BF16)
    vec_bits = lax.bitcast_convert_type(jnp.concatenate([s5_d, s5_b_glu], axis=0), BITS16).reshape(2, -1)
    small = {n: w[n] for n in SMALL_RS}

    lvec, grad_x, dmod, gbig, gsmall, l1_recv = local_step(ctx[0], x[0], loss_target[0], mod, Wt, small,
                                                           [shard(n) for n in L1_BIG] + [vec_bits])
    grad_x = grad_x[None]

    recv = dict(zip(L1_BIG + VEC_SHARDED, l1_recv))
    out = {}

    def keep(n, res):
        for key, arr in zip("gdmv", res):
            out[key, n] = arr.reshape(w[n].shape)

    reduced = sum_slots(l1_recv[len(L1_BIG + VEC_SHARDED):], "sum_chunks")

    kshape = lambda n: w[n].shape if w[n].ndim > 1 else (1, w[n].size)
    flat = jnp.concatenate([gsmall[n].reshape(-1) for n in TINY] + [dmod.reshape(-1), lvec.reshape(-1)])[None]
    bb_all, cc_all, flat_all = exchange([jnp.stack(reduced[:2]), jnp.stack(reduced[2:]), flat], "gather", "gather_small")
    chunk_all = [bb_all[:, 0], bb_all[:, 1], cc_all[:, 0], cc_all[:, 1]]
    tiny_all, off = [], 0
    for n in TINY:
        tiny_all.append(flat_all[:, 0, off:off + w[n].size].reshape((NDEV,) + kshape(n)))
        off += w[n].size
    dm_all = flat_all[:, 0, off:off + dmod.size].reshape((NDEV,) + dmod.shape)
    loss = sum_slots([flat_all[:, :, off + dmod.size:]], "loss_sum")[0][0, 0]

    dm_cols = lax.dynamic_slice_in_dim(dm_all.reshape(NDEV, 2, 2, 3 * D // WA, WA), me, 1, axis=3)[:, :, :, 0]
    dm_loc = jnp.concatenate([dm_cols[:, :, 1].transpose(1, 0, 2), dm_cols[:, :, 0].transpose(1, 0, 2)], axis=1)
    g_ada_w, dcc_part, g_ada_b = ada_bwd(cg, cc2, ada_w, dm_loc, dm_all.transpose(0, 2, 1, 3).reshape(2 * NDEV, 2, 3 * D), "ada_bwd")
    dcc_all = exchange([dcc_part], "gather", "gather_dcc")[0].reshape(NDEV, D)
    g_c_ctx = cctx_finish(dcc_all, cc2, "cctx_finish")

    flat2 = lambda t: t.reshape(-1, t.shape[-1])
    first = [(recv[n], w[n][0], m[n][0], v[n][0]) for n in L1_BIG]
    first.append((flat2(g_ada_w)[None], flat2(ada_w), flat2(m_ada_w), flat2(v_ada_w)))
    res, l0_recv = adamw_rows(first, "adamw_l1", rode=[gbig[n] for n in L0_BIG], modes="lead")
    for n, r in zip(L1_BIG + ("ada_w",), res):
        keep(n, r)
    res, _ = adamw_rows([(g, w[n][0], m[n][0], v[n][0]) for n, g in zip(L0_BIG, l0_recv)], "adamw_l0")
    for n, r in zip(L0_BIG, res):
        keep(n, r)
    items = []
    halves = 2
    for n, g in zip(CHUNKED, chunk_all):
        blk = (1, 1, G // halves) + w[n].shape[3:]
        g = jnp.moveaxis(g, 0, 1).reshape(w[n].shape)
        g_spec = pl.BlockSpec((1,) + blk, lambda d, s: (0, 0, d, s, 0, 0))
        items.append((g[None], g_spec, w[n], m[n], v[n], pl.BlockSpec(blk, lambda d, s: (0, d, s, 0, 0))))
    for n, res in zip(CHUNKED, adamw_multi(items, (2, halves), "adamw_bc")):
        keep(n, res)
    tiny_g = dict(zip(TINY, tiny_all))
    tiny_g.update({n: recv[n] for n in VEC_SHARDED})
    tiny_g["c_ctx"], tiny_g["ada_b"] = g_c_ctx[None], g_ada_b[None]
    names = list(tiny_g)
    items = [(tiny_g[n], _whole(tiny_g[n], 1)) + tuple(t[n].reshape(kshape(n)) for t in (w, m, v))
             + (pl.BlockSpec(kshape(n), lambda i, r=len(kshape(n)): (0,) * r),) for n in names]
    for n, res in zip(names, adamw_multi(items, (1,), "adamw_small")):
        keep(n, res)

    return (loss, grad_x, *[out["g", n] for n in ORDER], *[out["d", n] for n in ORDER],
            *[out["m", n] for n in ORDER], *[out["v", n] for n in ORDER])
```

```python
import math

import numpy as np
import jax
import jax.numpy as jnp
from jax import lax
from jax.experimental import pallas as pl
from jax.experimental.pallas import tpu as pltpu

F32 = jnp.float32
BF16 = jnp.bfloat16

D = 1024
L = 2048
LC = 256
NDEV = 8
GRID_W = 64
EPS = 1e-6
HEADS = 16
NOPE = 64
ROPE = 32
QK = NOPE + ROPE
VD = 64
IN_W = 256 + 128 + ROPE + HEADS * 64
IN_WP = 1536
QL = 256
KVL = 128
SCALE = QK ** -0.5
LOG2E = math.log2(math.e)
THETA = 10000.0
G = 64
P = 64
CH = 16
GB = 8
NJ = G // GB
UB = GB * CH
SB = GB * P
SEG = 16
TB = 256
VMEM_LIMIT = 56 * 1024 * 1024
B1, B2, LR, AEPS, WD, STEP = 0.9, 0.999, 0.001, 1e-8, 0.01, 10
MESH_T = pl.DeviceIdType.MESH


def _cp(sem=None):
    return pltpu.CompilerParams(dimension_semantics=sem, vmem_limit_bytes=VMEM_LIMIT)


def _sig(x):
    return 1.0 / (1.0 + jnp.exp(-x))


def _silu(x):
    return x * _sig(x)


def _dsilu(x):
    s = _sig(x)
    return s * (1.0 + x * (1.0 - s))


_GK = math.sqrt(2.0 / math.pi)


def _gelu(x):
    return 0.5 * x * (1.0 + jnp.tanh(_GK * (x + 0.044715 * x * x * x)))


def _dgelu(x):
    t = jnp.tanh(_GK * (x + 0.044715 * x * x * x))
    return 0.5 * (1.0 + t) + 0.5 * x * (1.0 - t * t) * _GK * (1.0 + 3 * 0.044715 * x * x)


def _rs(x):
    return lax.rsqrt(jnp.mean(x * x, axis=-1, keepdims=True) + EPS)


def _sum0(x):
    return jnp.sum(x, axis=0, keepdims=True)


def st_norm_mod(x, g, sc, sh):
    y = x * _rs(x) * g
    return (y * (1.0 + sc) + sh,), ()


def st_norm_mod_bwd(x, dh, dres, g, sc):
    r = _rs(x)
    xn = x * r
    y = xn * g
    dy = dh * (1.0 + sc)
    dxn = dy * g
    dx = r * (dxn - xn * jnp.mean(dxn * xn, axis=-1, keepdims=True))
    return (dres + dx,), (_sum0(dh), _sum0(dh * y), _sum0(dy * xn))


def st_rms(x, g):
    return (x * _rs(x) * g,), ()


def st_rms_bwd(x, dy, g):
    r = _rs(x)
    n = x * r
    dn = dy * g
    dx = r * (dn - n * jnp.mean(dn * n, axis=-1, keepdims=True))
    return (dx,), (_sum0(dy * n),)


def st_rms2(x1, x2, g1, g2):
    return st_rms(x1, g1)[0] + st_rms(x2, g2)[0], ()


def st_rms2_bwd(x1, dy1, x2, dy2, g1, g2):
    (d1,), (s1,) = st_rms_bwd(x1, dy1, g1)
    (d2,), (s2,) = st_rms_bwd(x2, dy2, g2)
    return (d1, d2), (s1, s2)


def st_gate(o, z):
    return (o * _silu(z),), ()


def st_gate_bwd(dog, o, z):
    return (dog * _silu(z), dog * o * _dsilu(z)), ()


def st_resid(x, out, gt):
    return (x + gt * out,), ()


def st_resid_bwd(dx, out, gt):
    return (dx * gt,), (_sum0(dx * out),)


def st_s5a(yssm, u, d):
    y = yssm + d * u
    return (y, _gelu(y)), ()


def st_s5b(y, gl, z, b):
    return (_gelu(y) * _sig(gl + b) * _silu(z),), ()


def st_s5b_bwd(dy3, y, gl, z, b):
    y1 = _gelu(y)
    s = _sig(gl + b)
    dy2 = dy3 * _silu(z)
    dz = dy3 * y1 * s * _dsilu(z)
    dgl = dy2 * y1 * s * (1.0 - s)
    return (dgl, dz, dy2 * s), (_sum0(dgl),)


def st_s5a_bwd(dy1a, dy1b, y, u, d):
    dy = (dy1a + dy1b) * _dgelu(y)
    return (dy, dy * d), (_sum0(dy * u),)


def st_l0_pre(x, g, sc, sh, qg, kvg, w_in):
    hb = st_norm_mod(x, g, sc, sh)[0][0].astype(BF16)
    p = lax.dot_general(hb, w_in, _DN["nt"], preferred_element_type=F32)
    cq, ckv = p[:, HEADS * VD:HEADS * VD + QL], p[:, HEADS * VD + QL:HEADS * VD + QL + KVL]
    return (hb, p) + st_rms2(cq, ckv, qg, kvg)[0], ()


def st_l0_tail_bwd(dq, dkv, dkr, dz, cq, ckv, cqn, ckvn, h, x, dres, qg, kvg, g, sc, w_uq, w_ukv, w_in):
    dcqn = jnp.dot(dq, w_uq, preferred_element_type=F32)
    dckvn = jnp.dot(dkv, w_ukv, preferred_element_type=F32)
    (dcq, dckv), (dqg, dkvg) = st_rms2_bwd(cq, dcqn, ckv, dckvn, qg, kvg)
    dp = jnp.concatenate([dz, dcq, dckv, dkr], axis=1).astype(BF16)
    dh = jnp.dot(dp, w_in, preferred_element_type=F32)
    outs, sums = st_norm_mod_bwd(x, dh, dres, g, sc)
    tn = lambda a, b: lax.dot_general(a, b, _DN["tn"], preferred_element_type=F32)
    return outs, (dqg, dkvg) + sums, (tn(cqn, dq), tn(ckvn, dkv), tn(h, dp))


def st_l1_pre(x, g, sc, sh, w_in):
    hb = st_norm_mod(x, g, sc, sh)[0][0].astype(BF16)
    return (hb, lax.dot_general(hb, w_in, _DN["nt"], preferred_element_type=F32)), ()


def st_l1_tail_bwd(du_a, du_b, dz, h, x, dres, g, sc, w_in):
    dp = jnp.concatenate([(du_a + du_b).astype(BF16), dz], axis=1)
    dh = jnp.dot(dp, w_in, preferred_element_type=F32)
    outs, sums = st_norm_mod_bwd(x, dh, dres, g, sc)
    return outs, sums, (lax.dot_general(h, dp, _DN["tn"], preferred_element_type=F32),)


def st_l0_post(o, z, x, gt, w_out):
    og = (o * _silu(z)).astype(BF16)
    out = jnp.dot(og, w_out, preferred_element_type=F32)
    return (og, out, x + gt * out), ()


def st_l0_post_bwd(dx1, out, og, o, z, gt, w_out):
    (dout,), (dgt,) = st_resid_bwd(dx1, out.astype(F32), gt)
    doutb = dout.astype(BF16)
    dog = lax.dot_general(doutb, w_out, _DN["nt"], preferred_element_type=F32)
    return st_gate_bwd(dog, o, z)[0], (dgt,), (lax.dot_general(og, doutb, _DN["tn"], preferred_element_type=F32),)


def st_l1_mlp(yssm, u, z, x1, tgt, d, bglu, gt, fg, mask, w_glu, w_out):
    (y, y1), _ = st_s5a(yssm, u, d)
    y1b = y1.astype(BF16)
    gl = jnp.dot(y1b, w_glu, preferred_element_type=F32)
    y3 = (y1 * _sig(gl + bglu) * _silu(z)).astype(BF16)
    out = jnp.dot(y3, w_out, preferred_element_type=F32)
    (dx2,), sums = st_final(x1 + gt * out, tgt, fg, mask)
    return (y, y1b, gl, y3, out, dx2), sums


def st_l1_mlp_bwd(dx2, out, y3, y, gl, z, u, y1b, gt, bglu, d, w_out, w_glu):
    out, gl = out.astype(F32), gl.astype(F32)
    (dout,), (dgt,) = st_resid_bwd(dx2, out, gt)
    doutb = dout.astype(BF16)
    dy3 = lax.dot_general(doutb, w_out, _DN["nt"], preferred_element_type=F32)
    (dgl, dz, dy1a), (dbglu,) = st_s5b_bwd(dy3, y, gl, z, bglu)
    dglb = dgl.astype(BF16)
    dy1b = lax.dot_general(dglb, w_glu, _DN["nt"], preferred_element_type=F32)
    (dy, du), (dd,) = st_s5a_bwd(dy1a, dy1b, y, u, d)
    g_w_out = lax.dot_general(y3, doutb, _DN["tn"], preferred_element_type=F32)
    g_w_glu = lax.dot_general(y1b, dglb, _DN["tn"], preferred_element_type=F32)
    return (dz, dy, du), (dgt, dbglu, dd), (g_w_out, g_w_glu)


def st_final(x2, tgt, g, mask):
    r = _rs(x2)
    n = x2 * r
    e = n * g - tgt
    dyo = e * (1.0 / D)
    dn = dyo * g
    dx = r * (dn - n * jnp.mean(dn * n, axis=-1, keepdims=True))
    lsum = jnp.sum(_sum0(e * e), axis=1, keepdims=True) * (0.5 / D)
    return (dx * mask,), (_sum0(dyo * n), jnp.broadcast_to(lsum, (1, 128)))


def rowwise(fn, rows, vecs, out_rows, out_sums, name, mats=(), out_accs=()):
    lat_blk = lambda i: jnp.maximum(i - 1, 0)
    arrays, in_specs, pick = [], [], []
    for a in rows:
        if not isinstance(a, tuple):
            a = (a, 0, a.shape[1])
        tag = a[0] if isinstance(a[0], str) else None
        if tag == "cat":
            _, ctx, x = a
            arrays += [ctx, x]
            in_specs += [pl.BlockSpec((TB, ctx.shape[1]), lambda i: (0, 0)),
                         pl.BlockSpec((TB, x.shape[1]), lambda i: (lat_blk(i), 0))]
            pick.append(2)
        elif tag == "lat":
            arrays.append(a[1])
            in_specs.append(pl.BlockSpec((TB, a[1].shape[1]), lambda i: (lat_blk(i), 0)))
            pick.append(1)
        else:
            arr, cb, width = a
            arrays.append(arr)
            in_specs.append(pl.BlockSpec((TB, width), lambda i, cb=cb: (i, cb)))
            pick.append(1)
    T = LC + L
    nin, nv, nm, no, ns = len(arrays), len(vecs), len(mats), len(out_rows), len(out_sums)

    def body(*refs):
        i = pl.program_id(0)
        vals, k = [], 0
        for p in pick:
            if p == 2:
                vals.append(jnp.where(i == 0, refs[k][...], refs[k + 1][...]))
            else:
                vals.append(refs[k][...])
            k += p
        vals += [r[0] for r in refs[nin:nin + nv]] + [r[...] for r in refs[nin + nv:nin + nv + nm]]
        res = fn(*vals)
        first_out = nin + nv + nm
        for r, o in zip(refs[first_out:first_out + no], res[0]):
            r[...] = o.astype(r.dtype)
        sum_refs = refs[first_out + no:first_out + no + ns]
        if sum_refs:
            @pl.when(i <= 1)
            def _():
                for r in sum_refs:
                    r[...] = jnp.zeros_like(r)
            for r, s in zip(sum_refs, res[1]):
                r[0] += s
        na = len(out_accs)
        if na:
            acc_out, acc = refs[first_out + no + ns:first_out + no + ns + na], refs[first_out + no + ns + na:]

            @pl.when(i == 0)
            def _():
                for r in acc:
                    r[...] = jnp.zeros_like(r)
            for r, a in zip(acc, res[2]):
                r[...] += a

            @pl.when(i == T // TB - 1)
            def _():
                for o, r in zip(acc_out, acc):
                    o[...] = r[...].astype(o.dtype)

    kind = lambda i: (jnp.minimum(i, 1), 0, 0)
    in_specs += [pl.BlockSpec((1, 1, v.shape[2]), kind) for v in vecs]
    in_specs += [pl.BlockSpec(m.shape, lambda i: (0, 0), pipeline_mode=pl.Buffered(1)) for m in mats]
    out_specs, out_shape = [], []
    for o in out_rows:
        lat = len(o) == 3
        out_specs.append(pl.BlockSpec((TB, o[0]), (lambda i: (lat_blk(i), 0)) if lat else (lambda i: (i, 0))))
        out_shape.append(jax.ShapeDtypeStruct((L if lat else T, o[0]), o[1]))
    out_specs += [pl.BlockSpec((1, 1, c), kind) for c in out_sums]
    out_shape += [jax.ShapeDtypeStruct((2, 1, c), F32) for c in out_sums]
    out_specs += [pl.BlockSpec(s, lambda i: (0, 0)) for s in out_accs]
    out_shape += [jax.ShapeDtypeStruct(s, BF16) for s in out_accs]
    res = pl.pallas_call(body, grid=(T // TB,), in_specs=in_specs, out_specs=out_specs, out_shape=out_shape,
                         scratch_shapes=[pltpu.VMEM(s, F32) for s in out_accs],
                         compiler_params=_cp(("arbitrary",)), name=name)(*arrays, *vecs, *mats)
    if out_accs:
        return res[:no], res[no:no + ns], res[no + ns:]
    return res[:no], res[no:]


_DN = {"nn": (((1,), (0,)), ((), ())), "nt": (((1,), (1,)), ((), ())), "tn": (((0,), (0,)), ((), ()))}


def mm(a, b, mode, name, out_dtype=F32, tm=None, tn=None, shard_out=False):
    if mode == "nn":
        (M, K), (_, N) = a.shape, b.shape
    elif mode == "nt":
        (M, K), (N, _) = a.shape, b.shape
    else:
        (K, M), (_, N) = a.shape, b.shape
    if tm is None:
        tm = next((t for t in (768, 512, 256) if M % t == 0 and M > t), M)
    tn = N if tn is None else tn
    dn = _DN[mode]

    def body(a_ref, b_ref, o_ref):
        o_ref[...] = lax.dot_general(a_ref[...].astype(BF16), b_ref[...].astype(BF16), dn,
                                     preferred_element_type=F32).astype(o_ref.dtype)

    if shard_out:
        def body(a_ref, b_ref, o_ref):
            av = a_ref[...].astype(BF16)
            for j in range(N // tn):
                bj = b_ref[pl.ds(j * tn, tn), :] if mode == "nt" else b_ref[:, pl.ds(j * tn, tn)]
                o_ref[j] = lax.dot_general(av, bj.astype(BF16), dn, preferred_element_type=F32).astype(o_ref.dtype)

        a_spec = pl.BlockSpec((K, tm), lambda i: (0, i)) if mode == "tn" else pl.BlockSpec((tm, K), lambda i: (i, 0))
        return pl.pallas_call(body, grid=(M // tm,), in_specs=[a_spec, pl.BlockSpec(b.shape, lambda i: (0, 0))],
                              out_specs=pl.BlockSpec((N // tn, tm, tn), lambda i: (0, i, 0)),
                              out_shape=jax.ShapeDtypeStruct((N // tn, M, tn), out_dtype),
                              compiler_params=_cp(("parallel",)), name=name)(a, b)
    a_spec = pl.BlockSpec((K, tm), lambda i, j: (0, i)) if mode == "tn" else pl.BlockSpec((tm, K), lambda i, j: (i, 0))
    b_spec = pl.BlockSpec((tn, K), lambda i, j: (j, 0)) if mode == "nt" else pl.BlockSpec((K, tn), lambda i, j: (0, j))
    return pl.pallas_call(body, grid=(M // tm, N // tn), in_specs=[a_spec, b_spec],
                          out_specs=pl.BlockSpec((tm, tn), lambda i, j: (i, j)), out_shape=jax.ShapeDtypeStruct((M, N), out_dtype),
                          compiler_params=_cp(("parallel", "arbitrary")), name=name)(a, b)


def _rope_tables(T, width=QK, first=NOPE):
    nlat = T - LC
    pos = np.arange(nlat)
    row, col = pos // GRID_W, pos % GRID_W
    half = ROPE // 2
    inv = 1.0 / (THETA ** (np.arange(0, half, 2, dtype=np.float64) / half))
    cosf = np.ones((T, width), np.float64)
    sinf = np.zeros((T, width), np.float64)
    perm = np.zeros((width, width), np.float32)
    for m in range(ROPE):
        j = first + m
        blk, w = m // half, m % half
        ang = (row if blk == 0 else col)[:, None] * inv[None, :]
        f = w % (half // 2)
        cosf[LC:, j] = np.cos(ang[:, f])
        if w < half // 2:
            sinf[LC:, j] = -np.sin(ang[:, f])
            perm[j + half // 2, j] = 1.0
        else:
            sinf[LC:, j] = np.sin(ang[:, f])
            perm[j - half // 2, j] = 1.0
    return jnp.asarray(cosf, F32), jnp.asarray(sinf, F32), jnp.asarray(perm, BF16), jnp.asarray(perm.T, BF16)


def _exact_perm(x, pm):
    hi = x.astype(BF16)
    r1 = x - hi.astype(F32)
    mid = r1.astype(BF16)
    lo = (r1 - mid.astype(F32)).astype(BF16)
    dot = lambda a: jnp.dot(a, pm, preferred_element_type=F32)
    return dot(hi) + dot(mid) + dot(lo)


def _rot(x, cv, sv, pv, inverse):
    if inverse:
        return x * cv + _exact_perm(x * sv, pv)
    return x * cv + _exact_perm(x, pv) * sv


def rope(x, cosf, sinf, pm, inverse, out_dtype, name, scale=1.0):
    H, T, _ = x.shape

    def body(x_ref, c_ref, s_ref, p_ref, o_ref):
        cv, sv, pv = c_ref[...], s_ref[...], p_ref[...]
        for h in range(H):
            o_ref[h] = (_rot(x_ref[h], cv, sv, pv, inverse) * scale).astype(o_ref.dtype)

    return pl.pallas_call(
        body, grid=(T // TB,),
        in_specs=[pl.BlockSpec((H, TB, QK), lambda i: (0, i, 0)), pl.BlockSpec((TB, QK), lambda i: (i, 0)),
                  pl.BlockSpec((TB, QK), lambda i: (i, 0)), pl.BlockSpec((QK, QK), lambda i: (0, 0))],
        out_specs=pl.BlockSpec((H, TB, QK), lambda i: (0, i, 0)), out_shape=jax.ShapeDtypeStruct((H, T, QK), out_dtype),
        compiler_params=_cp(("parallel",)), name=name)(x, cosf, sinf, pm)


KVW = NOPE + VD


def _kv_selectors():
    s_kn = np.zeros((KVW, QK), np.float32)
    s_kr = np.zeros((128, QK), np.float32)
    s_v = np.zeros((KVW, VD), np.float32)
    for l in range(NOPE):
        s_kn[l, l] = 1.0
    for l in range(ROPE):
        s_kr[l, NOPE + l] = 1.0
    for l in range(VD):
        s_v[NOPE + l, l] = 1.0
    return s_kn, s_kr, s_v


def project_q(cqn, w, cosf, sinf, pm, name):
    T = cqn.shape[0]

    def body(a_ref, w_ref, c_ref, s_ref, p_ref, o_ref):
        a, cv, sv, pv = a_ref[...], c_ref[...], s_ref[...], p_ref[...]
        for h in range(HEADS):
            qh = _dotf(a, w_ref[pl.ds(h * QK, QK), :], "nt")
            o_ref[h] = (_rot(qh, cv, sv, pv, False) * (SCALE * LOG2E)).astype(BF16)

    rows = lambda c: pl.BlockSpec((TB, c), lambda i: (i, 0))
    const = lambda x: pl.BlockSpec(x.shape, lambda i: (0, 0))
    return pl.pallas_call(
        body, grid=(T // TB,), in_specs=[rows(QL), const(w), rows(QK), rows(QK), const(pm)],
        out_specs=pl.BlockSpec((HEADS, TB, QK), lambda i: (0, i, 0)), out_shape=jax.ShapeDtypeStruct((HEADS, T, QK), BF16),
        compiler_params=_cp(("parallel",)), name=name)(cqn, w, cosf, sinf, pm)


def project_kv(ckvn, w, p0, kr_block, name):
    T = ckvn.shape[0]
    cosf, sinf, pm, _ = _rope_tables(T, 128, 0)
    s_kn, s_kr, s_v = (jnp.asarray(s, BF16) for s in _kv_selectors())

    def body(a_ref, w_ref, kr_ref, c_ref, s_ref, p_ref, skn_ref, skr_ref, sv_ref, k_ref, v_ref):
        a = a_ref[...]
        krr = _rot(kr_ref[...], c_ref[...], s_ref[...], p_ref[...], False).astype(BF16)
        kr_part = jnp.dot(krr, skr_ref[...], preferred_element_type=F32)
        for h in range(HEADS):
            kvb = _dotf(a, w_ref[pl.ds(h * KVW, KVW), :], "nt").astype(BF16)
            k_ref[h] = (jnp.dot(kvb, skn_ref[...], preferred_element_type=F32) + kr_part).astype(BF16)
            v_ref[h] = jnp.dot(kvb, sv_ref[...], preferred_element_type=F32).astype(BF16)

    rows = lambda c: pl.BlockSpec((TB, c), lambda i: (i, 0))
    const = lambda x: pl.BlockSpec(x.shape, lambda i: (0, 0))
    return pl.pallas_call(
        body, grid=(T // TB,),
        in_specs=[rows(KVL), const(w), pl.BlockSpec((TB, 128), lambda i: (i, kr_block)),
                  rows(128), rows(128), const(pm), const(s_kn), const(s_kr), const(s_v)],
        out_specs=[pl.BlockSpec((HEADS, TB, QK), lambda i: (0, i, 0)), pl.BlockSpec((HEADS, TB, VD), lambda i: (0, i, 0))],
        out_shape=[jax.ShapeDtypeStruct((HEADS, T, QK), BF16), jax.ShapeDtypeStruct((HEADS, T, VD), BF16)],
        compiler_params=_cp(("parallel",)), name=name)(ckvn, w, p0, cosf, sinf, pm, s_kn, s_kr, s_v)


def split_kv_grads(dk, dv, name):
    H, T, _ = dk.shape
    cosf, sinf, _, pmt = _rope_tables(T, 128, 0)
    s_kn, s_kr, s_v = _kv_selectors()
    s_knt, s_krt, s_vt = (jnp.asarray(s.T, BF16) for s in (s_kn, s_kr, s_v))

    def body(dk_ref, dv_ref, c_ref, s_ref, p_ref, skn_ref, skr_ref, sv_ref, dkv_ref, dkr_ref):
        total = None
        for h in range(H):
            dkh = dk_ref[h] * (1.0 / LOG2E)
            total = dkh if total is None else total + dkh
            dkv_ref[:, pl.ds(h * KVW, KVW)] = (
                jnp.dot(dkh.astype(BF16), skn_ref[...], preferred_element_type=F32)
                + jnp.dot(dv_ref[h].astype(BF16), sv_ref[...], preferred_element_type=F32)).astype(BF16)
        dkr_ref[...] = _rot(_exact_perm(total, skr_ref[...]), c_ref[...], s_ref[...], p_ref[...], True)

    rows = lambda c: pl.BlockSpec((TB, c), lambda i: (i, 0))
    const = lambda a: pl.BlockSpec(a.shape, lambda i: (0, 0))
    return pl.pallas_call(
        body, grid=(T // TB,),
        in_specs=[pl.BlockSpec((H, TB, QK), lambda i: (0, i, 0)), pl.BlockSpec((H, TB, VD), lambda i: (0, i, 0)),
                  rows(128), rows(128), const(pmt), const(s_knt), const(s_krt), const(s_vt)],
        out_specs=[rows(H * KVW), rows(128)],
        out_shape=[jax.ShapeDtypeStruct((T, H * KVW), BF16), jax.ShapeDtypeStruct((T, 128), F32)],
        compiler_params=_cp(("parallel",)), name=name)(dk, dv, cosf, sinf, pmt, s_knt, s_krt, s_vt)


HB = 4


def _by_query_block(run, T):
    @pl.when(pl.program_id(1) == 0)
    def _():
        run(LC)

    @pl.when(pl.program_id(1) > 0)
    def _():
        run(T)


def _with_rider(body, nin, nout, ride, grid):
    if ride is None:
        return body
    n = ride.n

    def wrapped(*refs):
        ins, xs = refs[:nin], refs[nin:nin + n]
        outs, got = refs[nin + n:nin + n + nout], refs[nin + n + nout:nin + 2 * n + nout]
        sems = refs[nin + 2 * n + nout:]
        step = pl.program_id(0) * grid[1] + pl.program_id(1)

        @pl.when(step == 0)
        def _():
            ride.start(xs, got, sems)

        body(*ins, *outs)

        @pl.when(step == grid[0] * grid[1] - 1)
        def _():
            ride.finish(xs, got, sems)

    return wrapped


def _ride_call(body, grid, in_specs, out_specs, out_shape, ride, rode, name, args):
    if ride is None:
        return pl.pallas_call(body, grid=grid, in_specs=in_specs, out_specs=out_specs, out_shape=out_shape,
                              compiler_params=_cp(("parallel", "arbitrary")), name=name)(*args), []
    res = pl.pallas_call(
        _with_rider(body, len(in_specs), len(out_specs), ride, grid), grid=grid,
        in_specs=in_specs + ride.specs, out_specs=out_specs + ride.specs, out_shape=out_shape + ride.out_shape,
        scratch_shapes=ride.scratch,
        compiler_params=pltpu.CompilerParams(dimension_semantics=("arbitrary", "arbitrary"), vmem_limit_bytes=VMEM_LIMIT,
                                             has_side_effects=True), name=name)(*args, *rode)
    return res[:len(out_specs)], res[len(out_specs):]


def attn_fwd(q, k, v, name, rode=None, modes=None):
    H, T, _ = q.shape

    def body(q_ref, k_ref, v_ref, o_ref, lse_ref):
        def run(nk):
            for hh in range(HB):
                s = _dotf(q_ref[hh], k_ref[hh, pl.ds(0, nk), :], "nt")
                m = jnp.max(s, axis=1, keepdims=True)
                p = jnp.exp2(s - m)
                l = jnp.sum(p, axis=1, keepdims=True)
                o = jnp.dot(p.astype(BF16), v_ref[hh, pl.ds(0, nk), :], preferred_element_type=F32)
                o_ref[hh] = o / l
                lse_ref[hh] = m + jnp.log2(l)

        _by_query_block(run, T)

    return _ride_call(
        body, (H // HB, T // TB),
        [pl.BlockSpec((HB, TB, QK), lambda h, i: (h, i, 0)), pl.BlockSpec((HB, T, QK), lambda h, i: (h, 0, 0)),
         pl.BlockSpec((HB, T, VD), lambda h, i: (h, 0, 0))],
        [pl.BlockSpec((HB, TB, VD), lambda h, i: (h, i, 0)), pl.BlockSpec((HB, TB, 1), lambda h, i: (h, i, 0))],
        [jax.ShapeDtypeStruct((H, T, VD), F32), jax.ShapeDtypeStruct((H, T, 1), F32)],
        Exchange(rode, modes) if rode else None, rode, name, (q, k, v))


def attn_bwd(q, k, v, o, lse, do, name, rode=None, modes=None):
    H, T, _ = q.shape

    def body(q_ref, k_ref, v_ref, o_ref, lse_ref, do_ref, dq_ref, dk_ref, dv_ref):
        i = pl.program_id(1)

        @pl.when(i == 0)
        def _():
            dk_ref[...] = jnp.zeros_like(dk_ref)
            dv_ref[...] = jnp.zeros_like(dv_ref)

        def run(nk):
            keys = pl.ds(0, nk)
            for hh in range(HB):
                qv, kv, dov = q_ref[hh], k_ref[hh, keys, :], do_ref[hh]
                p = jnp.exp2(_dotf(qv, kv, "nt") - lse_ref[hh])
                delta = jnp.sum(dov * o_ref[hh], axis=1, keepdims=True)
                dob = dov.astype(BF16)
                dv_ref[hh, keys, :] += _dotf(p.astype(BF16), dob, "tn")
                dp = _dotf(dob, v_ref[hh, keys, :], "nt")
                ds = (p * (dp - delta)).astype(BF16)
                dq_ref[hh] = jnp.dot(ds, kv, preferred_element_type=F32)
                dk_ref[hh, keys, :] += _dotf(ds, qv, "tn")

        _by_query_block(run, T)

    blk = lambda c: pl.BlockSpec((HB, TB, c), lambda h, i: (h, i, 0))
    full = lambda c: pl.BlockSpec((HB, T, c), lambda h, i: (h, 0, 0))
    return _ride_call(
        body, (H // HB, T // TB), [blk(QK), full(QK), full(VD), blk(VD), blk(1), blk(VD)], [blk(QK), full(QK), full(VD)],
        [jax.ShapeDtypeStruct((H, T, QK), F32), jax.ShapeDtypeStruct((H, T, QK), F32), jax.ShapeDtypeStruct((H, T, VD), F32)],
        Exchange(rode, modes) if rode else None, rode, name, (q, k, v, o, lse, do))


def disc_fwd(a_re, a_im, ls, name):
    def body(ar_ref, ai_ref, ls_ref, lr_ref, li_ref, fr_ref, fi_ref):
        ar, ai = ar_ref[...], ai_ref[...]
        dt = jnp.exp(ls_ref[...])
        mag = jnp.exp(ar * dt)
        lr = mag * jnp.cos(ai * dt)
        li = mag * jnp.sin(ai * dt)
        den = ar * ar + ai * ai
        nr = lr - 1.0
        lr_ref[...] = lr
        li_ref[...] = li
        fr_ref[...] = (nr * ar + li * ai) / den
        fi_ref[...] = (li * ar - nr * ai) / den

    return pl.pallas_call(body, out_shape=[jax.ShapeDtypeStruct(a_re.shape, F32)] * 4, name=name)(a_re, a_im, ls)


def disc_b(f_re, f_im, b_re, b_im, name):
    def body(fr_ref, fi_ref, br_ref, bi_ref, or_ref, oi_ref):
        fr, fi, br, bi = fr_ref[...], fi_ref[...], br_ref[...], bi_ref[...]
        or_ref[...] = fr * br - fi * bi
        oi_ref[...] = fr * bi + fi * br

    fs, bs = _disc_b_specs()
    return pl.pallas_call(body, grid=(2, G * P // DISC_ROWS), in_specs=[fs, fs, bs, bs], out_specs=[bs, bs],
                          out_shape=[jax.ShapeDtypeStruct(b_re.shape, F32)] * 2, name=name)(f_re, f_im, b_re, b_im)


DISC_ROWS = G * P


def _disc_b_specs():
    return (pl.BlockSpec((1, DISC_ROWS, 1), lambda d, i: (d, i, 0)), pl.BlockSpec((1, DISC_ROWS, CH), lambda d, i: (d, i, 0)))


def disc_b_bwd(f_re, f_im, b_re, b_im, dbb_re, dbb_im, name):
    def body(fr_ref, fi_ref, br_ref, bi_ref, dr_ref, di_ref, dbr_ref, dbi_ref, dfr_ref, dfi_ref):
        fr, fi, br, bi, dr, di = fr_ref[...], fi_ref[...], br_ref[...], bi_ref[...], dr_ref[...], di_ref[...]
        dbr_ref[...] = fr * dr + fi * di
        dbi_ref[...] = fr * di - fi * dr
        dfr_ref[...] = jnp.sum(dr * br + di * bi, axis=-1, keepdims=True)
        dfi_ref[...] = jnp.sum(di * br - dr * bi, axis=-1, keepdims=True)

    fs, bs = _disc_b_specs()
    return pl.pallas_call(body, grid=(2, G * P // DISC_ROWS), in_specs=[fs, fs, bs, bs, bs, bs], out_specs=[bs, bs, fs, fs],
                          out_shape=[jax.ShapeDtypeStruct(b_re.shape, F32)] * 2 + [jax.ShapeDtypeStruct(f_re.shape, F32)] * 2,
                          name=name)(f_re, f_im, b_re, b_im, dbb_re, dbb_im)


def disc_a_bwd(a_re, a_im, ls, dlr, dli, dfr, dfi, name):
    def body(ar_ref, ai_ref, ls_ref, dlr_ref, dli_ref, dfr_ref, dfi_ref, dar_ref, dai_ref, dls_ref):
        ar, ai = ar_ref[...], ai_ref[...]
        dt = jnp.exp(ls_ref[...])
        mag = jnp.exp(ar * dt)
        cs, sn = jnp.cos(ai * dt), jnp.sin(ai * dt)
        lr, li = mag * cs, mag * sn
        den = ar * ar + ai * ai
        nr = lr - 1.0
        f_re = (nr * ar + li * ai) / den
        f_im = (li * ar - nr * ai) / den
        dn1 = dfr_ref[...] / den
        dn2 = dfi_ref[...] / den
        dden = -(dfr_ref[...] * f_re + dfi_ref[...] * f_im) / den
        dlr_t = dlr_ref[...] + dn1 * ar - dn2 * ai
        dli_t = dli_ref[...] + dn1 * ai + dn2 * ar
        dar = dn1 * nr + dn2 * li + dden * 2.0 * ar
        dai = dn1 * li - dn2 * nr + dden * 2.0 * ai
        dmag = dlr_t * cs + dli_t * sn
        dth = dli_t * lr - dlr_t * li
        dar_ref[...] = dar + dmag * mag * dt
        dai_ref[...] = dai + dth * dt
        dls_ref[...] = jnp.sum(dmag * mag * ar + dth * ai, axis=-1, keepdims=True) * dt

    return pl.pallas_call(body, out_shape=[jax.ShapeDtypeStruct(a_re.shape, F32)] * 2 +
                          [jax.ShapeDtypeStruct(ls.shape, F32)], name=name)(a_re, a_im, ls, dlr, dli, dfr, dfi)


def _cpow(lr, li, n):
    rr, ri = None, None
    br, bi = lr, li
    while n:
        if n & 1:
            if rr is None:
                rr, ri = br, bi
            else:
                rr, ri = rr * br - ri * bi, rr * bi + ri * br
        n >>= 1
        if n:
            br, bi = br * br - bi * bi, 2.0 * br * bi
    return rr, ri


UNROLL = 4


def _steps(trips, fn, init):
    main = trips // UNROLL

    def body(i, c):
        for j in range(UNROLL):
            c = fn(i * UNROLL + j, c)
        return c

    c = lax.fori_loop(0, main, body, init) if main else init
    for n in range(main * UNROLL, trips):
        c = fn(n, c)
    return c


def _seg_scan(xre, xim, lam8, pw, base, seglen, rev, init, fin_re, fin_im, ini_re, ini_im, prev=None):
    lr, li = lam8
    nsub = SEG // 8

    def rows(t, s):
        first = base + t * SEG + 8 * s
        return pl.ds(first if isinstance(first, int) else pl.multiple_of(first, 8), 8)

    tmap = (lambda n: seglen - 1 - n) if rev else (lambda n: n)
    zeros = tuple(jnp.zeros((8, SB), F32) for _ in range(2 * nsub))

    def advance(c, t):
        out = []
        for s in range(nsub):
            a, b = c[2 * s], c[2 * s + 1]
            out += [lr * a - li * b + xre[rows(t, s), :], lr * b + li * a + xim[rows(t, s), :]]
        return tuple(out)

    fin = _steps(seglen, lambda n, c: advance(c, tmap(n)), zeros)
    for s in range(nsub):
        fin_re[pl.ds(8 * s, 8), :] = fin[2 * s]
        fin_im[pl.ds(8 * s, 8), :] = fin[2 * s + 1]
    (cr, ci), (pr, pi) = init, pw
    for i in (range(SEG - 1, -1, -1) if rev else range(SEG)):
        ini_re[pl.ds(i, 1), :] = cr
        ini_im[pl.ds(i, 1), :] = ci
        cr, ci = pr * cr - pi * ci + fin_re[pl.ds(i, 1), :], pr * ci + pi * cr + fin_im[pl.ds(i, 1), :]
    tiles = lambda re, im: tuple(r[pl.ds(8 * s, 8), :] for s in range(nsub) for r in (re, im))
    start = tiles(ini_re, ini_im)

    def store(c, t):
        new = advance(c, t)
        for s in range(nsub):
            xre[rows(t, s), :] = new[2 * s]
            xim[rows(t, s), :] = new[2 * s + 1]
        return new

    if prev is None:
        _steps(seglen, lambda n, c: store(c, tmap(n)), start)
        return (cr, ci), None

    sre, sim, s_ini_re, s_ini_im = prev

    def acc_step(c, t, before):
        new = store(c[:2 * nsub], t)
        acc = []
        for s in range(nsub):
            (na, nb), (pre, pim) = new[2 * s:2 * s + 2], before[2 * s:2 * s + 2]
            acc += [c[2 * nsub + 2 * s] + na * pre + nb * pim, c[2 * nsub + 2 * s + 1] + nb * pre - na * pim]
        return new + tuple(acc)

    def body(n, c):
        t = tmap(n)
        tp = t - 1 if rev else t + 1
        return acc_step(c, t, tuple(r[rows(tp, s), :] for s in range(nsub) for r in (sre, sim)))

    c = _steps(seglen - 1, body, start + zeros)
    c = acc_step(c, 0 if rev else seglen - 1, tiles(s_ini_re, s_ini_im))
    acc = c[2 * nsub:]
    return (cr, ci), (sum(acc[0::2][1:], acc[0]), sum(acc[1::2][1:], acc[1]))


def _lam_tiles(lr, li, lens, conj=False):
    if conj:
        li = -li
    lam8 = (jnp.broadcast_to(lr, (8, SB)), jnp.broadcast_to(li, (8, SB)))
    return lam8, [_cpow(lr, li, n) for n in lens]


def _stretches(T):
    return ((0, LC // SEG), (LC, (T - LC) // SEG))


def _to_seg_order(src, dst, T):
    for base, seglen in _stretches(T):
        def body(t, carry, base=base, seglen=seglen):
            dst[pl.ds(pl.multiple_of(base + t * SEG, SEG), SEG), :] = src[pl.ds(base + t, SEG, stride=seglen), :]
            return carry
        lax.fori_loop(0, seglen, body, 0, unroll=8)


def _from_seg_order(src, dst, T):
    for base, seglen in _stretches(T):
        def body(t, carry, base=base, seglen=seglen):
            dst[pl.ds(base + t, SEG, stride=seglen), :] = src[pl.ds(pl.multiple_of(base + t * SEG, SEG), SEG), :]
            return carry
        lax.fori_loop(0, seglen, body, 0, unroll=8)


def _scan_specs(T):
    ublk = pl.BlockSpec((T, UB), lambda j: (0, j))
    lam = pl.BlockSpec((2, 1, 1, SB), lambda j: (0, j, 0, 0))
    mat = pl.BlockSpec((2, 1, UB, P), lambda j: (0, j, 0, 0))
    return ublk, lam, mat


def _dotf(a, b, mode="nn"):
    return lax.dot_general(a, b, _DN[mode], preferred_element_type=F32)


def _diag_mask():
    r = lax.broadcasted_iota(jnp.int32, (UB, SB), 0)
    c = lax.broadcasted_iota(jnp.int32, (UB, SB), 1)
    return lax.shift_right_logical(r, int(math.log2(CH))) == lax.shift_right_logical(c, int(math.log2(P)))


def _expand(m):
    p = lax.broadcasted_iota(jnp.int32, (P, SB), 0)
    c = lax.broadcasted_iota(jnp.int32, (P, SB), 1)
    tile = jnp.where(lax.bitwise_and(c, P - 1) == p, 1.0, 0.0).astype(BF16)
    wide = jnp.dot(m.astype(BF16), tile, preferred_element_type=F32)
    return jnp.where(_diag_mask(), wide, 0.0).astype(BF16)


def _collapse(full):
    c = lax.broadcasted_iota(jnp.int32, (SB, P), 0)
    p = lax.broadcasted_iota(jnp.int32, (SB, P), 1)
    pick = jnp.where(lax.bitwise_and(c, P - 1) == p, 1.0, 0.0).astype(BF16)
    return _exact_perm(jnp.where(_diag_mask(), full, 0.0), pick)


def _zero_state():
    return jnp.zeros((1, SB), F32), jnp.zeros((1, SB), F32)


def scan_fwd(u, lam_re, lam_im, bre, bim, cre, cim, name):
    T = u.shape[0]
    s_ctx, s_lat = LC // SEG, (T - LC) // SEG

    def body(u_ref, lr_ref, li_ref, bre_ref, bim_ref, cre_ref, cim_ref, y_ref, us, ys, sre, sim, fre, fim, ire, iim):
        _to_seg_order(u_ref, us, T)
        ub = us[...].astype(BF16)
        for d in range(2):
            lam8, (pw_c, pw_l) = _lam_tiles(lr_ref[d, 0], li_ref[d, 0], (s_ctx, s_lat))
            sre[...] = _dotf(ub, _expand(bre_ref[d, 0]))
            sim[...] = _dotf(ub, _expand(bim_ref[d, 0]))
            end_c, _ = _seg_scan(sre, sim, lam8, pw_c, 0, s_ctx, bool(d), _zero_state(), fre, fim, ire, iim)
            _seg_scan(sre, sim, lam8, pw_l, LC, s_lat, bool(d), end_c, fre, fim, ire, iim)
            y = (_dotf(sre[...].astype(BF16), _expand(cre_ref[d, 0]), "nt")
                 - _dotf(sim[...].astype(BF16), _expand(cim_ref[d, 0]), "nt"))
            if d == 0:
                ys[...] = y
            else:
                ys[...] += y
        _from_seg_order(ys, y_ref, T)

    ublk, lam, mat = _scan_specs(T)
    return pl.pallas_call(
        body, grid=(NJ,), in_specs=[ublk, lam, lam, mat, mat, mat, mat], out_specs=ublk,
        out_shape=jax.ShapeDtypeStruct((T, G * CH), F32),
        scratch_shapes=[pltpu.VMEM((T, UB), F32)] * 2 + [pltpu.VMEM((T, SB), F32)] * 2 + [pltpu.VMEM((SEG, SB), F32)] * 4,
        compiler_params=_cp(("arbitrary",)), name=name)(u, lam_re, lam_im, bre, bim, cre, cim)


def scan_bwd(u, dy, lam_re, lam_im, bre, bim, cre, cim, name):
    T = u.shape[0]
    s_ctx, s_lat = LC // SEG, (T - LC) // SEG

    def body(u_ref, dy_ref, lr_ref, li_ref, bre_ref, bim_ref, cre_ref, cim_ref,
             du_ref, dlr_ref, dli_ref, dbre_ref, dbim_ref, dcre_ref, dcim_ref,
             us, dys, dus, sre, sim, gre, gim, fre, fim, ic_re, ic_im, il_re, il_im, jre, jim):
        _to_seg_order(u_ref, us, T)
        _to_seg_order(dy_ref, dys, T)
        ub, dyb = us[...].astype(BF16), dys[...].astype(BF16)
        for d in range(2):
            rev = bool(d)
            lam8, (pw_c, pw_l) = _lam_tiles(lr_ref[d, 0], li_ref[d, 0], (s_ctx, s_lat))
            cam8, (cw_c, cw_l) = _lam_tiles(lr_ref[d, 0], li_ref[d, 0], (s_ctx, s_lat), conj=True)
            bre_v, bim_v = _expand(bre_ref[d, 0]), _expand(bim_ref[d, 0])
            sre[...] = _dotf(ub, bre_v)
            sim[...] = _dotf(ub, bim_v)
            end_c, _ = _seg_scan(sre, sim, lam8, pw_c, 0, s_ctx, rev, _zero_state(), fre, fim, ic_re, ic_im)
            _seg_scan(sre, sim, lam8, pw_l, LC, s_lat, rev, end_c, fre, fim, il_re, il_im)
            gre[...] = _dotf(dyb, _expand(cre_ref[d, 0]))
            gim[...] = -_dotf(dyb, _expand(cim_ref[d, 0]))
            end_g, acc_l = _seg_scan(gre, gim, cam8, cw_l, LC, s_lat, not rev, _zero_state(), fre, fim, jre, jim,
                                     prev=(sre, sim, il_re, il_im))
            _, acc_c = _seg_scan(gre, gim, cam8, cw_c, 0, s_ctx, not rev, end_g, fre, fim, jre, jim,
                                 prev=(sre, sim, ic_re, ic_im))
            dlr_ref[d, 0] = _sum0(acc_l[0] + acc_c[0])
            dli_ref[d, 0] = _sum0(acc_l[1] + acc_c[1])
            grb, gib = gre[...].astype(BF16), gim[...].astype(BF16)
            du = _dotf(grb, bre_v, "nt") + _dotf(gib, bim_v, "nt")
            if d == 0:
                dus[...] = du
            else:
                dus[...] += du
            dbre_ref[d, 0] = _collapse(_dotf(ub, grb, "tn"))
            dbim_ref[d, 0] = _collapse(_dotf(ub, gib, "tn"))
            dcre_ref[d, 0] = _collapse(_dotf(dyb, sre[...].astype(BF16), "tn"))
            dcim_ref[d, 0] = -_collapse(_dotf(dyb, sim[...].astype(BF16), "tn"))
        _from_seg_order(dus, du_ref, T)

    ublk, lam, mat = _scan_specs(T)
    lam_s = jax.ShapeDtypeStruct(lam_re.shape, F32)
    mat_s = jax.ShapeDtypeStruct(bre.shape, F32)
    return pl.pallas_call(
        body, grid=(NJ,), in_specs=[ublk, ublk, lam, lam, mat, mat, mat, mat],
        out_specs=[ublk, lam, lam, mat, mat, mat, mat],
        out_shape=[jax.ShapeDtypeStruct((T, G * CH), F32), lam_s, lam_s, mat_s, mat_s, mat_s, mat_s],
        scratch_shapes=[pltpu.VMEM((T, UB), F32)] * 3 + [pltpu.VMEM((T, SB), F32)] * 4 + [pltpu.VMEM((SEG, SB), F32)] * 8,
        compiler_params=_cp(("arbitrary",)), name=name)(u, dy, lam_re, lam_im, bre, bim, cre, cim)


class Exchange:
    def __init__(self, xs, modes):
        self.n = len(xs)
        self.modes = [modes] * self.n if isinstance(modes, (str, int)) else list(modes)
        self.out_shape = [jax.ShapeDtypeStruct(self._shape(x, md), x.dtype) for x, md in zip(xs, self.modes)]
        self.scratch = [pltpu.SemaphoreType.DMA((NDEV - 1, self.n)), pltpu.SemaphoreType.DMA((NDEV - 1, self.n)),
                        pltpu.SemaphoreType.DMA((self.n,))]
        self.specs = [pl.BlockSpec(memory_space=pl.ANY)] * self.n

    @staticmethod
    def _shape(x, mode):
        if mode == "gather":
            return (NDEV,) + tuple(x.shape)
        return tuple(x.shape) if mode == "lead" else (NDEV, x.shape[0], mode) + tuple(x.shape[2:])

    @staticmethod
    def _piece(x_ref, mode, dev):
        if mode == "gather":
            return x_ref
        return x_ref.at[dev] if mode == "lead" else x_ref.at[:, pl.ds(dev * mode, mode)]

    def _copies(self, x_refs, out_refs, sems):
        send_sems, recv_sems, local_sems = sems
        mx, my, mc = lax.axis_index("x"), lax.axis_index("y"), lax.axis_index("c")
        me = 4 * mx + 2 * my + mc
        peer_of = lambda k: (1 - mx if k & 4 else mx, 1 - my if k & 2 else my, 1 - mc if k & 1 else mc)
        local, first, relay, arrivals = [], [], [], []
        for a, (x_ref, out_ref) in enumerate(zip(x_refs, out_refs)):
            mode = self.modes[a]
            local.append(pltpu.make_async_copy(self._piece(x_ref, mode, me), out_ref.at[me], local_sems.at[a]))

            def remote(src, dst, k, pair, a=a):
                return pltpu.make_async_remote_copy(src_ref=src, dst_ref=dst, send_sem=send_sems.at[pair, a],
                                                    recv_sem=recv_sems.at[pair, a], device_id=peer_of(k), device_id_type=MESH_T)

            for k in range(1, NDEV):
                peer = peer_of(k)
                pid = 4 * peer[0] + 2 * peer[1] + peer[2]
                if mode != "gather":
                    src = self._piece(x_ref, mode, pid)
                    first.append(remote(src, out_ref.at[me], k, k - 1))
                    arrivals.append(remote(src, out_ref.at[pid], k, k - 1))
                elif k == 1:
                    first.append(remote(x_ref, out_ref.at[me], k, k - 1))
                    arrivals.append(remote(x_ref, out_ref.at[pid], k, k - 1))
                elif k % 2 == 0:
                    first.append(remote(x_ref, out_ref.at[me], k, k - 1))
                    relay.append((remote(x_ref, out_ref.at[pid], k, k - 1), remote(out_ref.at[pid], out_ref.at[pid], 1, k)))
                else:
                    arrivals.append(remote(x_ref, out_ref.at[pid], 1, k - 1))
        return local, first, relay, arrivals

    def start(self, x_refs, out_refs, sems):
        local, first, _, _ = self._copies(x_refs, out_refs, sems)
        for cp in local + first:
            cp.start()

    def finish(self, x_refs, out_refs, sems):
        local, first, relay, arrivals = self._copies(x_refs, out_refs, sems)
        for arrival, onward in relay:
            arrival.wait_recv()
            onward.start()
        for cp in arrivals:
            cp.wait_recv()
        for cp in first + [onward for _, onward in relay]:
            cp.wait_send()
        for cp in local:
            cp.wait()


def exchange(xs, modes, name):
    ex = Exchange(xs, modes)
    n = ex.n

    def body(*refs):
        ex.start(refs[:n], refs[n:2 * n], refs[2 * n:])
        ex.finish(refs[:n], refs[n:2 * n], refs[2 * n:])

    return pl.pallas_call(body, in_specs=ex.specs, out_specs=ex.specs, out_shape=ex.out_shape, scratch_shapes=ex.scratch,
                          compiler_params=pltpu.CompilerParams(has_side_effects=True), name=name)(*xs)


def _dot_f32(a, b, dn):
    return lax.dot_general(a, b, dn, preferred_element_type=F32, precision=lax.Precision.HIGHEST)


def ada_fwd(cg, c_ctx, ada_w, ada_b_loc, name):
    W = ada_w.shape[2]

    def body(cg_ref, cc_ref, w_ref, b_ref, o_ref):
        a = jnp.concatenate([_silu(cg_ref[...]), jnp.broadcast_to(_silu(cc_ref[...]), (NDEV, D))], axis=0)
        for i in range(2):
            o_ref[i] = _dot_f32(a, w_ref[i], _DN["nn"]) + b_ref[i]

    return pl.pallas_call(body, out_shape=jax.ShapeDtypeStruct((2, 2 * NDEV, W), F32),
                          compiler_params=_cp(), name=name)(cg, c_ctx, ada_w, ada_b_loc)


def ada_bwd(cg, c_ctx, ada_w, dm_loc, dm_all, name):
    W = ada_w.shape[2]

    def body(cg_ref, cc_ref, w_ref, dl_ref, da_ref, gw_ref, dcc_ref, gb_ref):
        a = jnp.concatenate([_silu(cg_ref[...]), jnp.broadcast_to(_silu(cc_ref[...]), (NDEV, D))], axis=0)
        dcc = jnp.zeros((1, D), F32)
        for i in range(2):
            dl = dl_ref[i]
            gw_ref[i] = _dot_f32(a, dl, _DN["tn"])
            dctx = jnp.sum(dl[NDEV:], axis=0, keepdims=True)
            dcc = dcc + _dot_f32(dctx, w_ref[i], _DN["nt"])
        dcc_ref[...] = dcc
        gb_ref[...] = jnp.sum(da_ref[...], axis=0)

    return pl.pallas_call(body, out_shape=[jax.ShapeDtypeStruct((2, D, W), F32), jax.ShapeDtypeStruct((1, D), F32),
                                           jax.ShapeDtypeStruct((2, 3 * D), F32)],
                          compiler_params=_cp(), name=name)(cg, c_ctx, ada_w, dm_loc, dm_all)


def cctx_finish(parts, c_ctx, name):
    def body(p_ref, cc_ref, o_ref):
        o_ref[...] = jnp.sum(p_ref[...], axis=0, keepdims=True) * _dsilu(cc_ref[...])

    return pl.pallas_call(body, out_shape=jax.ShapeDtypeStruct((1, D), F32), name=name)(parts, c_ctx)


def _adamw_update(g_ref, w_ref, m_ref, v_ref, go_ref, d_ref, mo_ref, vo_ref):
    g = g_ref[0].astype(F32)
    for s in range(1, g_ref.shape[0]):
        g = g + g_ref[s].astype(F32)
    mn = B1 * m_ref[...] + (1.0 - B1) * g
    vn = B2 * v_ref[...] + (1.0 - B2) * g * g
    go_ref[...] = g
    mo_ref[...] = mn
    vo_ref[...] = vn
    d_ref[...] = -LR * ((mn * (1.0 / (1.0 - B1 ** STEP))) / (jnp.sqrt(vn * (1.0 / (1.0 - B2 ** STEP))) + AEPS) + WD * w_ref[...])


ADAMW_PARTS = 4


def adamw_rows(items, name, rode=None, modes=None):
    in_specs, out_specs, out_shape, args = [], [], [], []
    for g, w, m, v in items:
        n, R, C = g.shape
        tr = R // ADAMW_PARTS
        spec = pl.BlockSpec((tr, C), lambda i, j: (i, 0))
        in_specs += [pl.BlockSpec((n, tr, C), lambda i, j: (0, i, 0)), spec, spec, spec]
        args += [g, w, m, v]
    for g, w, m, v in items:
        tr = w.shape[0] // ADAMW_PARTS
        out_specs += [pl.BlockSpec((tr, w.shape[1]), lambda i, j: (i, 0))] * 4
        out_shape += [jax.ShapeDtypeStruct(w.shape, F32)] * 4
    res, got = _ride_call(_adamw_body(len(items)), (ADAMW_PARTS, 1), in_specs, out_specs, out_shape,
                          Exchange(rode, modes) if rode else None, rode, name, args)
    return [res[4 * t:4 * t + 4] for t in range(len(items))], got


def _adamw_body(k):
    def body(*refs):
        for t in range(k):
            _adamw_update(*refs[4 * t:4 * t + 4], *refs[4 * k + 4 * t:4 * k + 4 * t + 4])
    return body


def adamw_multi(items, grid, name):
    k = len(items)
    ins, in_specs, out_specs, out_shape = [], [], [], []
    for g, g_spec, w, m, v, w_spec in items:
        ins += [g, w, m, v]
        in_specs += [g_spec, w_spec, w_spec, w_spec]
    for g, g_spec, w, m, v, w_spec in items:
        out_specs += [w_spec] * 4
        out_shape += [jax.ShapeDtypeStruct(w.shape, F32)] * 4
    res = pl.pallas_call(_adamw_body(k), grid=grid, in_specs=in_specs, out_specs=out_specs, out_shape=out_shape,
                         compiler_params=_cp(("arbitrary",) * len(grid)), name=name)(*ins)
    return [res[4 * t:4 * t + 4] for t in range(k)]


def _whole(a, grid_rank):
    zeros = (0,) * a.ndim
    return pl.BlockSpec(a.shape, lambda *idx: zeros)


def sum_slots(xs, name):
    def body(*refs):
        for x_ref, o_ref in zip(refs[:len(xs)], refs[len(xs):]):
            acc = x_ref[0]
            for s in range(1, NDEV):
                acc = acc + x_ref[s]
            o_ref[...] = acc

    return pl.pallas_call(body, out_shape=[jax.ShapeDtypeStruct(x.shape[1:], F32) for x in xs],
                          compiler_params=_cp(), name=name)(*xs)


def _col_shards(g):
    R, N = g.shape
    return g.reshape(R, NDEV, N // NDEV).transpose(1, 0, 2)


def _vec2(v):
    return jnp.broadcast_to(v.reshape(1, 1, -1), (2, 1, v.size))


SHARD_ROWS = {"mla_w_in": 192, "mla_w_uq": 192, "mla_w_ukv": 256, "s5_w_in": 256}


def _t_shard(wsh, rows):
    t = wsh[0].T.astype(BF16)
    return jnp.pad(t, ((0, rows - t.shape[0]), (0, 0)))


def _win_order():
    w = IN_W // NDEV
    perm = np.zeros((IN_WP, NDEV * SHARD_ROWS["mla_w_in"]), np.float32)
    first = QL + KVL + ROPE
    for c in range(IN_W):
        n = c + HEADS * VD if c < first else c - first
        perm[n, (c // w) * SHARD_ROWS["mla_w_in"] + c % w] = 1.0
    return jnp.asarray(perm, BF16)


def local_step(ctx, x, tgt, mod, Wt, small, l1_shards):
    T = LC + x.shape[0]
    xa = ("cat", ctx, x)
    sh = [mod[i, :, None, 0:D] for i in range(2)]
    sc = [mod[i, :, None, D:2 * D] for i in range(2)]
    gt = [mod[i, :, None, 2 * D:] for i in range(2)]
    ng = [_vec2(small["norm_g"][i]) for i in range(2)]
    qg, kvg = _vec2(small["mla_q_norm"]), _vec2(small["mla_kv_norm"])
    cosf, sinf, pm, pmt = _rope_tables(T)

    (h0, p0, cqn, ckvn), _ = rowwise(st_l0_pre, [xa], [ng[0], sc[0], sh[0], qg, kvg],
                                     [(D, BF16), (IN_WP, F32), (QL, BF16), (KVL, BF16)], [], "l0_pre", mats=[Wt["mla_w_in"]])
    z0, cq, ckv = (p0, 0, HEADS * VD), (p0, HEADS * VD // QL, QL), (p0, (HEADS * VD + QL) // KVL, KVL)
    Q = project_q(cqn, Wt["mla_w_uq"], cosf, sinf, pm, "l0_uq")
    K, V = project_kv(ckvn, Wt["mla_w_ukv"], p0, (HEADS * VD + QL + KVL) // 128, "l0_ukv")
    (o, lse), got = attn_fwd(Q, K, V, "l0_attn", rode=l1_shards, modes="gather")
    Wt, small = dict(Wt), dict(small)
    for n, a in zip(L1_BIG, got):
        Wt[n] = a.reshape(-1, a.shape[-1])
    vecs = lax.bitcast_convert_type(got[-1].reshape(NDEV, 2, -1, 2), F32)
    small["s5_d"], small["s5_b_glu"] = vecs[:, 0, :].reshape(D), vecs[:, 1, :].reshape(D)
    o2 = o.transpose(1, 0, 2).reshape(T, HEADS * VD)
    (og, out0, x1), _ = rowwise(st_l0_post, [o2, z0, xa], [gt[0]], [(D, BF16), (D, BF16), (D, F32)], [], "l0_post",
                                mats=[Wt["mla_w_out"]])

    ls = small["s5_log_step"].reshape(2, G, 1)
    a_re, a_im = small["s5_a_re"].reshape(2, G, P), small["s5_a_im"].reshape(2, G, P)
    b_re, b_im = small["s5_b_re"].reshape(2, G * P, CH), small["s5_b_im"].reshape(2, G * P, CH)
    lam_re, lam_im, f_re, f_im = disc_fwd(a_re, a_im, ls, "s5_disc")
    f_re2, f_im2 = f_re.reshape(2, G * P, 1), f_im.reshape(2, G * P, 1)
    bb_re, bb_im = disc_b(f_re2, f_im2, b_re, b_im, "s5_disc_b")
    compact = lambda m: m.reshape(2, NJ, UB, P)
    bre = compact(bb_re.reshape(2, G, P, CH).transpose(0, 1, 3, 2))
    bim = compact(bb_im.reshape(2, G, P, CH).transpose(0, 1, 3, 2))
    cre, cim = compact(small["s5_c_re"]), compact(small["s5_c_im"])
    lam_re4, lam_im4 = lam_re.reshape(2, NJ, 1, SB), lam_im.reshape(2, NJ, 1, SB)

    (h1, p1), _ = rowwise(st_l1_pre, [x1], [ng[1], sc[1], sh[1]], [(D, BF16), (2 * D, F32)], [], "l1_pre", mats=[Wt["s5_w_in"]])
    u, z1 = (p1, 0, D), (p1, 1, D)
    yssm = scan_fwd(p1, lam_re4, lam_im4, bre, bim, cre, cim, "s5_scan")
    dvec, bglu = _vec2(small["s5_d"]), _vec2(small["s5_b_glu"])
    fg = _vec2(small["final_g"])
    lat_mask = jnp.stack([jnp.zeros((1, D), F32), jnp.ones((1, D), F32)])
    (y, y1b, gl, y3, out1, dx2), (dfg, lvec) = rowwise(
        st_l1_mlp, [yssm, u, z1, x1, ("lat", tgt)], [dvec, bglu, gt[1], fg, lat_mask],
        [(D, F32), (D, BF16), (D, BF16), (D, BF16), (D, BF16), (D, F32)], [D, 128], "l1_mlp",
        mats=[Wt["s5_w_glu"], Wt["s5_w_out"]])

    (dz1, dy, du_d), (dgt1, dbglu, dd), (g_w_out5, g_w_glu) = rowwise(
        st_l1_mlp_bwd, [dx2, out1, y3, y, gl, z1, u, y1b], [gt[1], bglu, dvec], [(D, BF16), (D, F32), (D, F32)], [D, D, D],
        "l1_mlp_b", mats=[Wt["s5_w_out"], Wt["s5_w_glu"]], out_accs=[(D, D), (D, D)])
    du_s, dlr, dli, dbre, dbim, dcre, dcim = scan_bwd(p1, dy, lam_re4, lam_im4, bre, bim, cre, cim, "s5_scan_b")
    dbb_re = dbre.reshape(2, G, CH, P).transpose(0, 1, 3, 2).reshape(2, G * P, CH)
    dbb_im = dbim.reshape(2, G, CH, P).transpose(0, 1, 3, 2).reshape(2, G * P, CH)
    g_c_re, g_c_im = dcre.reshape(2, G, CH, P), dcim.reshape(2, G, CH, P)
    g_b_re, g_b_im, dfr, dfi = disc_b_bwd(f_re2, f_im2, b_re, b_im, dbb_re, dbb_im, "s5_disc_b_b")
    g_a_re, g_a_im, g_ls = disc_a_bwd(a_re, a_im, ls, dlr.reshape(2, G, P), dli.reshape(2, G, P),
                                      dfr.reshape(2, G, P), dfi.reshape(2, G, P), "s5_disc_b_a")
    (dx1,), (dsh1, dsc1, dng1), (g_w_in5,) = rowwise(
        st_l1_tail_bwd, [du_d, du_s, dz1, h1, x1, dx2], [ng[1], sc[1]], [(D, F32)], [D, D, D], "l1_pre_b",
        mats=[Wt["s5_w_in"]], out_accs=[(D, 2 * D)])
    g_w_in5 = _col_shards(g_w_in5)

    (do2, dz0), (dgt0,), (g_w_out,) = rowwise(st_l0_post_bwd, [dx1, out0, og, o2, z0], [gt[0]], [(D, F32), (D, F32)], [D],
                                              "l0_post_b", mats=[Wt["mla_w_out"]], out_accs=[(D, D)])
    doh = do2.reshape(T, HEADS, VD).transpose(1, 0, 2)
    rows8 = lambda g: g.reshape(NDEV, -1, g.shape[-1])
    both = lambda s: s[0, 0] + s[1, 0]
    dense = lambda g: g.reshape(2, G * P * CH // 128, 128)
    chunks = [dense(g_b_re), dense(g_b_im), g_c_re, g_c_im]
    l1_send = [g_w_in5, rows8(g_w_glu), rows8(g_w_out5), rows8(g_w_out),
               both(dd).reshape(NDEV, 1, -1), both(dbglu).reshape(NDEV, 1, -1)]
    (dQ, dK, dV), l1_recv = attn_bwd(Q, K, V, o, lse, doh, "l0_attn_b", rode=l1_send + chunks,
                                     modes=["lead"] * len(l1_send) + [a.shape[1] // NDEV for a in chunks])
    dqh = rope(dQ, cosf, sinf, pmt, True, BF16, "l0_rope_q_b", scale=SCALE)
    dq = dqh.transpose(1, 0, 2).reshape(T, HEADS * QK)
    dkv, dkr = split_kv_grads(dK, dV, "l0_kv_b")
    (grad_x,), (dqg, dkvg, dsh0, dsc0, dng0), (g_uq, g_ukv, g_p) = rowwise(
        st_l0_tail_bwd, [dq, dkv, dkr, dz0, cq, ckv, cqn, ckvn, h0, xa, dx1], [qg, kvg, ng[0], sc[0]],
        [(D, F32, "lat")], [QL, KVL, D, D, D], "l0_pre_b", mats=[Wt["mla_w_uq"], Wt["mla_w_ukv"], Wt["mla_w_in"]],
        out_accs=[(QL, HEADS * QK), (KVL, HEADS * KVW), (D, IN_WP)])
    g_w_uq, g_w_ukv = _col_shards(g_uq).astype(BF16), _col_shards(g_ukv).astype(BF16)
    g_w_in = _col_shards(jnp.concatenate([g_p[:, HEADS * VD:IN_W], g_p[:, :HEADS * VD]], axis=1)).astype(BF16)

    dmod = jnp.stack([jnp.concatenate([dsh0, dsc0, dgt0], axis=-1)[:, 0], jnp.concatenate([dsh1, dsc1, dgt1], axis=-1)[:, 0]])
    gbig = {"mla_w_in": g_w_in, "mla_w_uq": g_w_uq, "mla_w_ukv": g_w_ukv}
    gsmall = {"norm_g": jnp.stack([both(dng0), both(dng1)]), "mla_q_norm": both(dqg), "mla_kv_norm": both(dkvg),
              "s5_a_re": g_a_re, "s5_a_im": g_a_im, "s5_log_step": g_ls, "final_g": dfg[1, 0]}
    return lvec[1], grad_x, dmod, gbig, gsmall, l1_recv


COL_SHARDED = ("mla_w_in", "mla_w_uq", "mla_w_ukv", "s5_w_in")
ROW_SHARDED = ("mla_w_out", "s5_w_glu", "s5_w_out")
VEC_SHARDED = ("s5_d", "s5_b_glu")
BIG = COL_SHARDED + ROW_SHARDED
L0_BIG = ("mla_w_in", "mla_w_uq", "mla_w_ukv")
L1_BIG = ("s5_w_in", "s5_w_glu", "s5_w_out", "mla_w_out")
BITS16 = jnp.bfloat16
SMALL_RS = ("norm_g", "mla_q_norm", "mla_kv_norm", "s5_a_re", "s5_a_im", "s5_log_step", "s5_b_re", "s5_b_im",
            "s5_c_re", "s5_c_im", "final_g")
CHUNKED = ("s5_b_re", "s5_b_im", "s5_c_re", "s5_c_im")
DENSE = ("s5_b_re", "s5_b_im")
TINY = ("norm_g", "mla_q_norm", "mla_kv_norm", "s5_a_re", "s5_a_im", "s5_log_step", "final_g")
ORDER = ("c_ctx", "ada_w", "ada_b", "norm_g", "mla_w_in", "mla_q_norm", "mla_w_uq", "mla_kv_norm", "mla_w_ukv",
         "mla_w_out", "s5_w_in", "s5_a_re", "s5_a_im", "s5_log_step", "s5_b_re", "s5_b_im", "s5_c_re", "s5_c_im",
         "s5_d", "s5_w_glu", "s5_b_glu", "s5_w_out", "final_g")


def kernel(x, c, ctx, c_ctx, ada_w, ada_b, norm_g, mla_w_in, mla_q_norm, mla_w_uq, mla_kv_norm, mla_w_ukv, mla_w_out, s5_w_in, s5_a_re, s5_a_im, s5_log_step, s5_b_re, s5_b_im, s5_c_re, s5_c_im, s5_d, s5_w_glu, s5_b_glu, s5_w_out, final_g, loss_target, m_c_ctx, m_ada_w, m_ada_b, m_norm_g, m_mla_w_in, m_mla_q_norm, m_mla_w_uq, m_mla_kv_norm, m_mla_w_ukv, m_mla_w_out, m_s5_w_in, m_s5_a_re, m_s5_a_im, m_s5_log_step, m_s5_b_re, m_s5_b_im, m_s5_c_re, m_s5_c_im, m_s5_d, m_s5_w_glu, m_s5_b_glu, m_s5_w_out, m_final_g, v_c_ctx, v_ada_w, v_ada_b, v_norm_g, v_mla_w_in, v_mla_q_norm, v_mla_w_uq, v_mla_kv_norm, v_mla_w_ukv, v_mla_w_out, v_s5_w_in, v_s5_a_re, v_s5_a_im, v_s5_log_step, v_s5_b_re, v_s5_b_im, v_s5_c_re, v_s5_c_im, v_s5_d, v_s5_w_glu, v_s5_b_glu, v_s5_w_out, v_final_g):
    w = dict(c_ctx=c_ctx, ada_w=ada_w, ada_b=ada_b, norm_g=norm_g, mla_w_in=mla_w_in, mla_q_norm=mla_q_norm,
             mla_w_uq=mla_w_uq, mla_kv_norm=mla_kv_norm, mla_w_ukv=mla_w_ukv, mla_w_out=mla_w_out, s5_w_in=s5_w_in,
             s5_a_re=s5_a_re, s5_a_im=s5_a_im, s5_log_step=s5_log_step, s5_b_re=s5_b_re, s5_b_im=s5_b_im,
             s5_c_re=s5_c_re, s5_c_im=s5_c_im, s5_d=s5_d, s5_w_glu=s5_w_glu, s5_b_glu=s5_b_glu, s5_w_out=s5_w_out,
             final_g=final_g)
    m = dict(c_ctx=m_c_ctx, ada_w=m_ada_w, ada_b=m_ada_b, norm_g=m_norm_g, mla_w_in=m_mla_w_in, mla_q_norm=m_mla_q_norm,
             mla_w_uq=m_mla_w_uq, mla_kv_norm=m_mla_kv_norm, mla_w_ukv=m_mla_w_ukv, mla_w_out=m_mla_w_out,
             s5_w_in=m_s5_w_in, s5_a_re=m_s5_a_re, s5_a_im=m_s5_a_im, s5_log_step=m_s5_log_step, s5_b_re=m_s5_b_re,
             s5_b_im=m_s5_b_im, s5_c_re=m_s5_c_re, s5_c_im=m_s5_c_im, s5_d=m_s5_d, s5_w_glu=m_s5_w_glu,
             s5_b_glu=m_s5_b_glu, s5_w_out=m_s5_w_out, final_g=m_final_g)
    v = dict(c_ctx=v_c_ctx, ada_w=v_ada_w, ada_b=v_ada_b, norm_g=v_norm_g, mla_w_in=v_mla_w_in, mla_q_norm=v_mla_q_norm,
             mla_w_uq=v_mla_w_uq, mla_kv_norm=v_mla_kv_norm, mla_w_ukv=v_mla_w_ukv, mla_w_out=v_mla_w_out,
             s5_w_in=v_s5_w_in, s5_a_re=v_s5_a_re, s5_a_im=v_s5_a_im, s5_log_step=v_s5_log_step, s5_b_re=v_s5_b_re,
             s5_b_im=v_s5_b_im, s5_c_re=v_s5_c_re, s5_c_im=v_s5_c_im, s5_d=v_s5_d, s5_w_glu=v_s5_w_glu,
             s5_b_glu=v_s5_b_glu, s5_w_out=v_s5_w_out, final_g=v_final_g)

    me = 4 * lax.axis_index("x") + 2 * lax.axis_index("y") + lax.axis_index("c")
    WA = ada_w.shape[2]

    def shard(n):
        return _t_shard(w[n], SHARD_ROWS[n]) if n in COL_SHARDED else w[n][0].astype(BF16)

    wgot = exchange([c] + [shard(n) for n in L0_BIG], "gather", "gather_w")

    cg = wgot[0].reshape(NDEV, D)
    cc2 = c_ctx.reshape(1, D)
    ada_b_loc = lax.dynamic_slice_in_dim(ada_b.reshape(2, 3 * D // WA, WA), me, 1, axis=1)
    part = ada_fwd(cg, cc2, ada_w, ada_b_loc, "ada_fwd")
    pg = exchange([part], "gather", "gather_mod")[0]
    mod_l = lax.dynamic_index_in_dim(pg, me, axis=2, keepdims=False).transpose(1, 0, 2).reshape(2, 3 * D)
    mod_c = pg[:, :, NDEV, :].transpose(1, 0, 2).reshape(2, 3 * D)
    mod = jnp.stack([mod_c, mod_l], axis=1)

    Wt = {n: a.reshape(-1, a.shape[-1]) for n, a in zip(L0_BIG, wgot[1:])}
    Wt["mla_w_in"] = mm(_win_order(), Wt["mla_w_in"], "nn", "w_in_order", out_dtype=BF16)
    vec_bits = lax.bitcast_convert_type(jnp.concatenate([s5_d, s5_b_glu], axis=0), BITS16).reshape(2, -1)
    small = {n: w[n] for n in SMALL_RS}

    lvec, grad_x, dmod, gbig, gsmall, l1_recv = local_step(ctx[0], x[0], loss_target[0], mod, Wt, small,
                                                           [shard(n) for n in L1_BIG] + [vec_bits])
    grad_x = grad_x[None]

    recv = dict(zip(L1_BIG + VEC_SHARDED, l1_recv))
    out = {}

    def keep(n, res):
        for key, arr in zip("gdmv", res):
            out[key, n] = arr.reshape(w[n].shape)

    reduced = sum_slots(l1_recv[len(L1_BIG + VEC_SHARDED):], "sum_chunks")

    kshape = lambda n: w[n].shape if w[n].ndim > 1 else (1, w[n].size)
    flat = jnp.concatenate([gsmall[n].reshape(-1) for n in TINY] + [dmod.reshape(-1), lvec.reshape(-1)])[None]
    *l0_recv, bb_all, cc_all, flat_all = exchange(
        [gbig[n] for n in L0_BIG] + [jnp.stack(reduced[:2]), jnp.stack(reduced[2:]), flat],
        ["lead"] * len(L0_BIG) + ["gather"] * 3, "scatter_grads")
    chunk_all = [bb_all[:, 0], bb_all[:, 1], cc_all[:, 0], cc_all[:, 1]]
    tiny_all, off = [], 0
    for n in TINY:
        tiny_all.append(flat_all[:, 0, off:off + w[n].size].reshape((NDEV,) + kshape(n)))
        off += w[n].size
    dm_all = flat_all[:, 0, off:off + dmod.size].reshape((NDEV,) + dmod.shape)
    loss = sum_slots([flat_all[:, :, off + dmod.size:]], "loss_sum")[0][0, 0]

    dm_cols = lax.dynamic_slice_in_dim(dm_all.reshape(NDEV, 2, 2, 3 * D // WA, WA), me, 1, axis=3)[:, :, :, 0]
    dm_loc = jnp.concatenate([dm_cols[:, :, 1].transpose(1, 0, 2), dm_cols[:, :, 0].transpose(1, 0, 2)], axis=1)
    g_ada_w, dcc_part, g_ada_b = ada_bwd(cg, cc2, ada_w, dm_loc, dm_all.transpose(0, 2, 1, 3).reshape(2 * NDEV, 2, 3 * D), "ada_bwd")
    dcc_all = exchange([dcc_part], "gather", "gather_dcc")[0].reshape(NDEV, D)
    g_c_ctx = cctx_finish(dcc_all, cc2, "cctx_finish")

    flat2 = lambda t: t.reshape(-1, t.shape[-1])
    recv.update(dict(zip(L0_BIG, l0_recv)))
    big = [(recv[n], w[n][0], m[n][0], v[n][0]) for n in BIG]
    big.append((flat2(g_ada_w)[None], flat2(ada_w), flat2(m_ada_w), flat2(v_ada_w)))
    for n, r in zip(BIG + ("ada_w",), adamw_rows(big, "adamw_big")[0]):
        keep(n, r)
    items = []
    halves = 2
    for n, g in zip(CHUNKED, chunk_all):
        blk = (1, 1, G // halves) + w[n].shape[3:]
        g = jnp.moveaxis(g, 0, 1).reshape(w[n].shape)
        g_spec = pl.BlockSpec((1,) + blk, lambda d, s: (0, 0, d, s, 0, 0))
        items.append((g[None], g_spec, w[n], m[n], v[n], pl.BlockSpec(blk, lambda d, s: (0, d, s, 0, 0))))
    for n, res in zip(CHUNKED, adamw_multi(items, (2, halves), "adamw_bc")):
        keep(n, res)
    tiny_g = dict(zip(TINY, tiny_all))
    tiny_g.update({n: recv[n] for n in VEC_SHARDED})
    tiny_g["c_ctx"], tiny_g["ada_b"] = g_c_ctx[None], g_ada_b[None]
    names = list(tiny_g)
    items = [(tiny_g[n], _whole(tiny_g[n], 1)) + tuple(t[n].reshape(kshape(n)) for t in (w, m, v))
             + (pl.BlockSpec(kshape(n), lambda i, r=len(kshape(n)): (0,) * r),) for n in names]
    for n, res in zip(names, adamw_multi(items, (1,), "adamw_small")):
        keep(n, res)

    return (loss, grad_x, *[out["g", n] for n in ORDER], *[out["d", n] for n in ORDER],
            *[out["m", n] for n in ORDER], *[out["v", n] for n in ORDER])
```

```python
import math

import numpy as np
import jax
import jax.numpy as jnp
from jax import lax
from jax.experimental import pallas as pl
from jax.experimental.pallas import tpu as pltpu

F32 = jnp.float32
BF16 = jnp.bfloat16

D = 1024
L = 2048
LC = 256
NDEV = 8
GRID_W = 64
EPS = 1e-6
HEADS = 16
NOPE = 64
ROPE = 32
QK = NOPE + ROPE
VD = 64
IN_W = 256 + 128 + ROPE + HEADS * 64
IN_WP = 1536
QL = 256
KVL = 128
SCALE = QK ** -0.5
LOG2E = math.log2(math.e)
THETA = 10000.0
G = 64
P = 64
CH = 16
GB = 8
NJ = G // GB
UB = GB * CH
SB = GB * P
SEG = 16
TB = 256
VMEM_LIMIT = 56 * 1024 * 1024
B1, B2, LR, AEPS, WD, STEP = 0.9, 0.999, 0.001, 1e-8, 0.01, 10
MESH_T = pl.DeviceIdType.MESH


def _cp(sem=None):
    return pltpu.CompilerParams(dimension_semantics=sem, vmem_limit_bytes=VMEM_LIMIT)


def _sig(x):
    return 1.0 / (1.0 + jnp.exp(-x))


def _silu(x):
    return x * _sig(x)


def _dsilu(x):
    s = _sig(x)
    return s * (1.0 + x * (1.0 - s))


_GK = math.sqrt(2.0 / math.pi)


def _gelu(x):
    return 0.5 * x * (1.0 + jnp.tanh(_GK * (x + 0.044715 * x * x * x)))


def _dgelu(x):
    t = jnp.tanh(_GK * (x + 0.044715 * x * x * x))
    return 0.5 * (1.0 + t) + 0.5 * x * (1.0 - t * t) * _GK * (1.0 + 3 * 0.044715 * x * x)


def _rs(x):
    return lax.rsqrt(jnp.mean(x * x, axis=-1, keepdims=True) + EPS)


def _sum0(x):
    return jnp.sum(x, axis=0, keepdims=True)


def st_norm_mod(x, g, sc, sh):
    y = x * _rs(x) * g
    return (y * (1.0 + sc) + sh,), ()


def st_norm_mod_bwd(x, dh, dres, g, sc):
    r = _rs(x)
    xn = x * r
    y = xn * g
    dy = dh * (1.0 + sc)
    dxn = dy * g
    dx = r * (dxn - xn * jnp.mean(dxn * xn, axis=-1, keepdims=True))
    return (dres + dx,), (_sum0(dh), _sum0(dh * y), _sum0(dy * xn))


def st_rms(x, g):
    return (x * _rs(x) * g,), ()


def st_rms_bwd(x, dy, g):
    r = _rs(x)
    n = x * r
    dn = dy * g
    dx = r * (dn - n * jnp.mean(dn * n, axis=-1, keepdims=True))
    return (dx,), (_sum0(dy * n),)


def st_rms2(x1, x2, g1, g2):
    return st_rms(x1, g1)[0] + st_rms(x2, g2)[0], ()


def st_rms2_bwd(x1, dy1, x2, dy2, g1, g2):
    (d1,), (s1,) = st_rms_bwd(x1, dy1, g1)
    (d2,), (s2,) = st_rms_bwd(x2, dy2, g2)
    return (d1, d2), (s1, s2)


def st_gate(o, z):
    return (o * _silu(z),), ()


def st_gate_bwd(dog, o, z):
    return (dog * _silu(z), dog * o * _dsilu(z)), ()


def st_resid(x, out, gt):
    return (x + gt * out,), ()


def st_resid_bwd(dx, out, gt):
    return (dx * gt,), (_sum0(dx * out),)


def st_s5a(yssm, u, d):
    y = yssm + d * u
    return (y, _gelu(y)), ()


def st_s5b(y, gl, z, b):
    return (_gelu(y) * _sig(gl + b) * _silu(z),), ()


def st_s5b_bwd(dy3, y, gl, z, b):
    y1 = _gelu(y)
    s = _sig(gl + b)
    dy2 = dy3 * _silu(z)
    dz = dy3 * y1 * s * _dsilu(z)
    dgl = dy2 * y1 * s * (1.0 - s)
    return (dgl, dz, dy2 * s), (_sum0(dgl),)


def st_s5a_bwd(dy1a, dy1b, y, u, d):
    dy = (dy1a + dy1b) * _dgelu(y)
    return (dy, dy * d), (_sum0(dy * u),)


def st_l0_pre(x, g, sc, sh, qg, kvg, w_in):
    hb = st_norm_mod(x, g, sc, sh)[0][0].astype(BF16)
    p = lax.dot_general(hb, w_in, _DN["nt"], preferred_element_type=F32)
    cq, ckv = p[:, HEADS * VD:HEADS * VD + QL], p[:, HEADS * VD + QL:HEADS * VD + QL + KVL]
    return (hb, p) + st_rms2(cq, ckv, qg, kvg)[0], ()


def st_l0_tail_bwd(dq, dkv, dkr, dz, cq, ckv, cqn, ckvn, h, x, dres, qg, kvg, g, sc, w_uq, w_ukv, w_in):
    dcqn = jnp.dot(dq, w_uq, preferred_element_type=F32)
    dckvn = jnp.dot(dkv, w_ukv, preferred_element_type=F32)
    (dcq, dckv), (dqg, dkvg) = st_rms2_bwd(cq, dcqn, ckv, dckvn, qg, kvg)
    dp = jnp.concatenate([dz, dcq, dckv, dkr], axis=1).astype(BF16)
    dh = jnp.dot(dp, w_in, preferred_element_type=F32)
    outs, sums = st_norm_mod_bwd(x, dh, dres, g, sc)
    tn = lambda a, b: lax.dot_general(a, b, _DN["tn"], preferred_element_type=F32)
    return outs, (dqg, dkvg) + sums, (tn(cqn, dq), tn(ckvn, dkv), tn(h, dp))


def st_l1_pre(x, g, sc, sh, w_in):
    hb = st_norm_mod(x, g, sc, sh)[0][0].astype(BF16)
    return (hb, lax.dot_general(hb, w_in, _DN["nt"], preferred_element_type=F32)), ()


def st_l1_tail_bwd(du_a, du_b, dz, h, x, dres, g, sc, w_in):
    dp = jnp.concatenate([(du_a + du_b).astype(BF16), dz], axis=1)
    dh = jnp.dot(dp, w_in, preferred_element_type=F32)
    outs, sums = st_norm_mod_bwd(x, dh, dres, g, sc)
    return outs, sums, (lax.dot_general(h, dp, _DN["tn"], preferred_element_type=F32),)


def st_l0_post(o, z, x, gt, w_out):
    og = (o * _silu(z)).astype(BF16)
    out = jnp.dot(og, w_out, preferred_element_type=F32)
    return (og, out, x + gt * out), ()


def st_l0_post_bwd(dx1, out, og, o, z, gt, w_out):
    (dout,), (dgt,) = st_resid_bwd(dx1, out.astype(F32), gt)
    doutb = dout.astype(BF16)
    dog = lax.dot_general(doutb, w_out, _DN["nt"], preferred_element_type=F32)
    return st_gate_bwd(dog, o, z)[0], (dgt,), (lax.dot_general(og, doutb, _DN["tn"], preferred_element_type=F32),)


def st_l1_mlp(yssm, u, z, x1, tgt, d, bglu, gt, fg, mask, w_glu, w_out):
    (y, y1), _ = st_s5a(yssm, u, d)
    y1b = y1.astype(BF16)
    gl = jnp.dot(y1b, w_glu, preferred_element_type=F32)
    y3 = (y1 * _sig(gl + bglu) * _silu(z)).astype(BF16)
    out = jnp.dot(y3, w_out, preferred_element_type=F32)
    (dx2,), sums = st_final(x1 + gt * out, tgt, fg, mask)
    return (y, y1b, gl, y3, out, dx2), sums


def st_l1_mlp_bwd(dx2, out, y3, y, gl, z, u, y1b, gt, bglu, d, w_out, w_glu):
    out, gl = out.astype(F32), gl.astype(F32)
    (dout,), (dgt,) = st_resid_bwd(dx2, out, gt)
    doutb = dout.astype(BF16)
    dy3 = lax.dot_general(doutb, w_out, _DN["nt"], preferred_element_type=F32)
    (dgl, dz, dy1a), (dbglu,) = st_s5b_bwd(dy3, y, gl, z, bglu)
    dglb = dgl.astype(BF16)
    dy1b = lax.dot_general(dglb, w_glu, _DN["nt"], preferred_element_type=F32)
    (dy, du), (dd,) = st_s5a_bwd(dy1a, dy1b, y, u, d)
    g_w_out = lax.dot_general(y3, doutb, _DN["tn"], preferred_element_type=F32)
    g_w_glu = lax.dot_general(y1b, dglb, _DN["tn"], preferred_element_type=F32)
    return (dz, dy, du), (dgt, dbglu, dd), (g_w_out, g_w_glu)


def st_final(x2, tgt, g, mask):
    r = _rs(x2)
    n = x2 * r
    e = n * g - tgt
    dyo = e * (1.0 / D)
    dn = dyo * g
    dx = r * (dn - n * jnp.mean(dn * n, axis=-1, keepdims=True))
    lsum = jnp.sum(_sum0(e * e), axis=1, keepdims=True) * (0.5 / D)
    return (dx * mask,), (_sum0(dyo * n), jnp.broadcast_to(lsum, (1, 128)))


def rowwise(fn, rows, vecs, out_rows, out_sums, name, mats=(), out_accs=()):
    lat_blk = lambda i: jnp.maximum(i - 1, 0)
    arrays, in_specs, pick = [], [], []
    for a in rows:
        if not isinstance(a, tuple):
            a = (a, 0, a.shape[1])
        tag = a[0] if isinstance(a[0], str) else None
        if tag == "cat":
            _, ctx, x = a
            arrays += [ctx, x]
            in_specs += [pl.BlockSpec((TB, ctx.shape[1]), lambda i: (0, 0)),
                         pl.BlockSpec((TB, x.shape[1]), lambda i: (lat_blk(i), 0))]
            pick.append(2)
        elif tag == "lat":
            arrays.append(a[1])
            in_specs.append(pl.BlockSpec((TB, a[1].shape[1]), lambda i: (lat_blk(i), 0)))
            pick.append(1)
        else:
            arr, cb, width = a
            arrays.append(arr)
            in_specs.append(pl.BlockSpec((TB, width), lambda i, cb=cb: (i, cb)))
            pick.append(1)
    T = LC + L
    nin, nv, nm, no, ns = len(arrays), len(vecs), len(mats), len(out_rows), len(out_sums)

    def body(*refs):
        i = pl.program_id(0)
        vals, k = [], 0
        for p in pick:
            if p == 2:
                vals.append(jnp.where(i == 0, refs[k][...], refs[k + 1][...]))
            else:
                vals.append(refs[k][...])
            k += p
        vals += [r[0] for r in refs[nin:nin + nv]] + [r[...] for r in refs[nin + nv:nin + nv + nm]]
        res = fn(*vals)
        first_out = nin + nv + nm
        for r, o in zip(refs[first_out:first_out + no], res[0]):
            r[...] = o.astype(r.dtype)
        sum_refs = refs[first_out + no:first_out + no + ns]
        if sum_refs:
            @pl.when(i <= 1)
            def _():
                for r in sum_refs:
                    r[...] = jnp.zeros_like(r)
            for r, s in zip(sum_refs, res[1]):
                r[0] += s
        na = len(out_accs)
        if na:
            acc_out, acc = refs[first_out + no + ns:first_out + no + ns + na], refs[first_out + no + ns + na:]

            @pl.when(i == 0)
            def _():
                for r in acc:
                    r[...] = jnp.zeros_like(r)
            for r, a in zip(acc, res[2]):
                r[...] += a

            @pl.when(i == T // TB - 1)
            def _():
                for o, r in zip(acc_out, acc):
                    o[...] = r[...].astype(o.dtype)

    kind = lambda i: (jnp.minimum(i, 1), 0, 0)
    in_specs += [pl.BlockSpec((1, 1, v.shape[2]), kind) for v in vecs]
    in_specs += [pl.BlockSpec(m.shape, lambda i: (0, 0), pipeline_mode=pl.Buffered(1)) for m in mats]
    out_specs, out_shape = [], []
    for o in out_rows:
        lat = len(o) == 3
        out_specs.append(pl.BlockSpec((TB, o[0]), (lambda i: (lat_blk(i), 0)) if lat else (lambda i: (i, 0))))
        out_shape.append(jax.ShapeDtypeStruct((L if lat else T, o[0]), o[1]))
    out_specs += [pl.BlockSpec((1, 1, c), kind) for c in out_sums]
    out_shape += [jax.ShapeDtypeStruct((2, 1, c), F32) for c in out_sums]
    out_specs += [pl.BlockSpec(s, lambda i: (0, 0)) for s in out_accs]
    out_shape += [jax.ShapeDtypeStruct(s, BF16) for s in out_accs]
    res = pl.pallas_call(body, grid=(T // TB,), in_specs=in_specs, out_specs=out_specs, out_shape=out_shape,
                         scratch_shapes=[pltpu.VMEM(s, F32) for s in out_accs],
                         compiler_params=_cp(("arbitrary",)), name=name)(*arrays, *vecs, *mats)
    if out_accs:
        return res[:no], res[no:no + ns], res[no + ns:]
    return res[:no], res[no:]


_DN = {"nn": (((1,), (0,)), ((), ())), "nt": (((1,), (1,)), ((), ())), "tn": (((0,), (0,)), ((), ()))}


def mm(a, b, mode, name, out_dtype=F32, tm=None, tn=None, shard_out=False):
    if mode == "nn":
        (M, K), (_, N) = a.shape, b.shape
    elif mode == "nt":
        (M, K), (N, _) = a.shape, b.shape
    else:
        (K, M), (_, N) = a.shape, b.shape
    if tm is None:
        tm = next((t for t in (768, 512, 256) if M % t == 0 and M > t), M)
    tn = N if tn is None else tn
    dn = _DN[mode]

    def body(a_ref, b_ref, o_ref):
        o_ref[...] = lax.dot_general(a_ref[...].astype(BF16), b_ref[...].astype(BF16), dn,
                                     preferred_element_type=F32).astype(o_ref.dtype)

    if shard_out:
        def body(a_ref, b_ref, o_ref):
            av = a_ref[...].astype(BF16)
            for j in range(N // tn):
                bj = b_ref[pl.ds(j * tn, tn), :] if mode == "nt" else b_ref[:, pl.ds(j * tn, tn)]
                o_ref[j] = lax.dot_general(av, bj.astype(BF16), dn, preferred_element_type=F32).astype(o_ref.dtype)

        a_spec = pl.BlockSpec((K, tm), lambda i: (0, i)) if mode == "tn" else pl.BlockSpec((tm, K), lambda i: (i, 0))
        return pl.pallas_call(body, grid=(M // tm,), in_specs=[a_spec, pl.BlockSpec(b.shape, lambda i: (0, 0))],
                              out_specs=pl.BlockSpec((N // tn, tm, tn), lambda i: (0, i, 0)),
                              out_shape=jax.ShapeDtypeStruct((N // tn, M, tn), out_dtype),
                              compiler_params=_cp(("parallel",)), name=name)(a, b)
    a_spec = pl.BlockSpec((K, tm), lambda i, j: (0, i)) if mode == "tn" else pl.BlockSpec((tm, K), lambda i, j: (i, 0))
    b_spec = pl.BlockSpec((tn, K), lambda i, j: (j, 0)) if mode == "nt" else pl.BlockSpec((K, tn), lambda i, j: (0, j))
    return pl.pallas_call(body, grid=(M // tm, N // tn), in_specs=[a_spec, b_spec],
                          out_specs=pl.BlockSpec((tm, tn), lambda i, j: (i, j)), out_shape=jax.ShapeDtypeStruct((M, N), out_dtype),
                          compiler_params=_cp(("parallel", "arbitrary")), name=name)(a, b)


def _rope_tables(T, width=QK, first=NOPE):
    nlat = T - LC
    pos = np.arange(nlat)
    row, col = pos // GRID_W, pos % GRID_W
    half = ROPE // 2
    inv = 1.0 / (THETA ** (np.arange(0, half, 2, dtype=np.float64) / half))
    cosf = np.ones((T, width), np.float64)
    sinf = np.zeros((T, width), np.float64)
    perm = np.zeros((width, width), np.float32)
    for m in range(ROPE):
        j = first + m
        blk, w = m // half, m % half
        ang = (row if blk == 0 else col)[:, None] * inv[None, :]
        f = w % (half // 2)
        cosf[LC:, j] = np.cos(ang[:, f])
        if w < half // 2:
            sinf[LC:, j] = -np.sin(ang[:, f])
            perm[j + half // 2, j] = 1.0
        else:
            sinf[LC:, j] = np.sin(ang[:, f])
            perm[j - half // 2, j] = 1.0
    return jnp.asarray(cosf, F32), jnp.asarray(sinf, F32), jnp.asarray(perm, BF16), jnp.asarray(perm.T, BF16)


def _exact_perm(x, pm):
    hi = x.astype(BF16)
    r1 = x - hi.astype(F32)
    mid = r1.astype(BF16)
    lo = (r1 - mid.astype(F32)).astype(BF16)
    dot = lambda a: jnp.dot(a, pm, preferred_element_type=F32)
    return dot(hi) + dot(mid) + dot(lo)


def _rot(x, cv, sv, pv, inverse):
    if inverse:
        return x * cv + _exact_perm(x * sv, pv)
    return x * cv + _exact_perm(x, pv) * sv


def rope(x, cosf, sinf, pm, inverse, out_dtype, name, scale=1.0):
    H, T, _ = x.shape

    def body(x_ref, c_ref, s_ref, p_ref, o_ref):
        cv, sv, pv = c_ref[...], s_ref[...], p_ref[...]
        for h in range(H):
            o_ref[h] = (_rot(x_ref[h], cv, sv, pv, inverse) * scale).astype(o_ref.dtype)

    return pl.pallas_call(
        body, grid=(T // TB,),
        in_specs=[pl.BlockSpec((H, TB, QK), lambda i: (0, i, 0)), pl.BlockSpec((TB, QK), lambda i: (i, 0)),
                  pl.BlockSpec((TB, QK), lambda i: (i, 0)), pl.BlockSpec((QK, QK), lambda i: (0, 0))],
        out_specs=pl.BlockSpec((H, TB, QK), lambda i: (0, i, 0)), out_shape=jax.ShapeDtypeStruct((H, T, QK), out_dtype),
        compiler_params=_cp(("parallel",)), name=name)(x, cosf, sinf, pm)


KVW = NOPE + VD


def _kv_selectors():
    s_kn = np.zeros((KVW, QK), np.float32)
    s_kr = np.zeros((128, QK), np.float32)
    s_v = np.zeros((KVW, VD), np.float32)
    for l in range(NOPE):
        s_kn[l, l] = 1.0
    for l in range(ROPE):
        s_kr[l, NOPE + l] = 1.0
    for l in range(VD):
        s_v[NOPE + l, l] = 1.0
    return s_kn, s_kr, s_v


def project_q(cqn, w, cosf, sinf, pm, name):
    T = cqn.shape[0]

    def body(a_ref, w_ref, c_ref, s_ref, p_ref, o_ref):
        a, cv, sv, pv = a_ref[...], c_ref[...], s_ref[...], p_ref[...]
        for h in range(HEADS):
            qh = _dotf(a, w_ref[pl.ds(h * QK, QK), :], "nt")
            o_ref[h] = (_rot(qh, cv, sv, pv, False) * (SCALE * LOG2E)).astype(BF16)

    rows = lambda c: pl.BlockSpec((TB, c), lambda i: (i, 0))
    const = lambda x: pl.BlockSpec(x.shape, lambda i: (0, 0))
    return pl.pallas_call(
        body, grid=(T // TB,), in_specs=[rows(QL), const(w), rows(QK), rows(QK), const(pm)],
        out_specs=pl.BlockSpec((HEADS, TB, QK), lambda i: (0, i, 0)), out_shape=jax.ShapeDtypeStruct((HEADS, T, QK), BF16),
        compiler_params=_cp(("parallel",)), name=name)(cqn, w, cosf, sinf, pm)


def project_kv(ckvn, w, p0, kr_block, name):
    T = ckvn.shape[0]
    cosf, sinf, pm, _ = _rope_tables(T, 128, 0)
    s_kn, s_kr, s_v = (jnp.asarray(s, BF16) for s in _kv_selectors())

    def body(a_ref, w_ref, kr_ref, c_ref, s_ref, p_ref, skn_ref, skr_ref, sv_ref, k_ref, v_ref):
        a = a_ref[...]
        krr = _rot(kr_ref[...], c_ref[...], s_ref[...], p_ref[...], False).astype(BF16)
        kr_part = jnp.dot(krr, skr_ref[...], preferred_element_type=F32)
        for h in range(HEADS):
            kvb = _dotf(a, w_ref[pl.ds(h * KVW, KVW), :], "nt").astype(BF16)
            k_ref[h] = (jnp.dot(kvb, skn_ref[...], preferred_element_type=F32) + kr_part).astype(BF16)
            v_ref[h] = jnp.dot(kvb, sv_ref[...], preferred_element_type=F32).astype(BF16)

    rows = lambda c: pl.BlockSpec((TB, c), lambda i: (i, 0))
    const = lambda x: pl.BlockSpec(x.shape, lambda i: (0, 0))
    return pl.pallas_call(
        body, grid=(T // TB,),
        in_specs=[rows(KVL), const(w), pl.BlockSpec((TB, 128), lambda i: (i, kr_block)),
                  rows(128), rows(128), const(pm), const(s_kn), const(s_kr), const(s_v)],
        out_specs=[pl.BlockSpec((HEADS, TB, QK), lambda i: (0, i, 0)), pl.BlockSpec((HEADS, TB, VD), lambda i: (0, i, 0))],
        out_shape=[jax.ShapeDtypeStruct((HEADS, T, QK), BF16), jax.ShapeDtypeStruct((HEADS, T, VD), BF16)],
        compiler_params=_cp(("parallel",)), name=name)(ckvn, w, p0, cosf, sinf, pm, s_kn, s_kr, s_v)


def split_kv_grads(dk, dv, name):
    H, T, _ = dk.shape
    cosf, sinf, _, pmt = _rope_tables(T, 128, 0)
    s_kn, s_kr, s_v = _kv_selectors()
    s_knt, s_krt, s_vt = (jnp.asarray(s.T, BF16) for s in (s_kn, s_kr, s_v))

    def body(dk_ref, dv_ref, c_ref, s_ref, p_ref, skn_ref, skr_ref, sv_ref, dkv_ref, dkr_ref):
        total = None
        for h in range(H):
            dkh = dk_ref[h] * (1.0 / LOG2E)
            total = dkh if total is None else total + dkh
            dkv_ref[:, pl.ds(h * KVW, KVW)] = (
                jnp.dot(dkh.astype(BF16), skn_ref[...], preferred_element_type=F32)
                + jnp.dot(dv_ref[h].astype(BF16), sv_ref[...], preferred_element_type=F32)).astype(BF16)
        dkr_ref[...] = _rot(_exact_perm(total, skr_ref[...]), c_ref[...], s_ref[...], p_ref[...], True)

    rows = lambda c: pl.BlockSpec((TB, c), lambda i: (i, 0))
    const = lambda a: pl.BlockSpec(a.shape, lambda i: (0, 0))
    return pl.pallas_call(
        body, grid=(T // TB,),
        in_specs=[pl.BlockSpec((H, TB, QK), lambda i: (0, i, 0)), pl.BlockSpec((H, TB, VD), lambda i: (0, i, 0)),
                  rows(128), rows(128), const(pmt), const(s_knt), const(s_krt), const(s_vt)],
        out_specs=[rows(H * KVW), rows(128)],
        out_shape=[jax.ShapeDtypeStruct((T, H * KVW), BF16), jax.ShapeDtypeStruct((T, 128), F32)],
        compiler_params=_cp(("parallel",)), name=name)(dk, dv, cosf, sinf, pmt, s_knt, s_krt, s_vt)


HB = 4
HBF = 8


def _by_query_block(run, T):
    @pl.when(pl.program_id(1) == 0)
    def _():
        run(LC)

    @pl.when(pl.program_id(1) > 0)
    def _():
        run(T)


def _with_rider(body, nin, nout, ride, grid):
    if ride is None:
        return body
    n = ride.n

    def wrapped(*refs):
        ins, xs = refs[:nin], refs[nin:nin + n]
        outs, got = refs[nin + n:nin + n + nout], refs[nin + n + nout:nin + 2 * n + nout]
        sems = refs[nin + 2 * n + nout:]
        step = pl.program_id(0) * grid[1] + pl.program_id(1)

        @pl.when(step == 0)
        def _():
            ride.start(xs, got, sems)

        body(*ins, *outs)

        @pl.when(step == grid[0] * grid[1] - 1)
        def _():
            ride.finish(xs, got, sems)

    return wrapped


def _ride_call(body, grid, in_specs, out_specs, out_shape, ride, rode, name, args):
    if ride is None:
        return pl.pallas_call(body, grid=grid, in_specs=in_specs, out_specs=out_specs, out_shape=out_shape,
                              compiler_params=_cp(("parallel", "arbitrary")), name=name)(*args), []
    res = pl.pallas_call(
        _with_rider(body, len(in_specs), len(out_specs), ride, grid), grid=grid,
        in_specs=in_specs + ride.specs, out_specs=out_specs + ride.specs, out_shape=out_shape + ride.out_shape,
        scratch_shapes=ride.scratch,
        compiler_params=pltpu.CompilerParams(dimension_semantics=("arbitrary", "arbitrary"), vmem_limit_bytes=VMEM_LIMIT,
                                             has_side_effects=True), name=name)(*args, *rode)
    return res[:len(out_specs)], res[len(out_specs):]


def attn_fwd(q, k, v, name, rode=None, modes=None):
    H, T, _ = q.shape

    def body(q_ref, k_ref, v_ref, o_ref, lse_ref):
        def run(nk):
            for hh in range(HBF):
                s = _dotf(q_ref[hh], k_ref[hh, pl.ds(0, nk), :], "nt")
                m = jnp.max(s, axis=1, keepdims=True)
                p = jnp.exp2(s - m)
                l = jnp.sum(p, axis=1, keepdims=True)
                o = jnp.dot(p.astype(BF16), v_ref[hh, pl.ds(0, nk), :], preferred_element_type=F32)
                o_ref[hh] = o / l
                lse_ref[hh] = m + jnp.log2(l)

        _by_query_block(run, T)

    return _ride_call(
        body, (H // HBF, T // TB),
        [pl.BlockSpec((HBF, TB, QK), lambda h, i: (h, i, 0)), pl.BlockSpec((HBF, T, QK), lambda h, i: (h, 0, 0)),
         pl.BlockSpec((HBF, T, VD), lambda h, i: (h, 0, 0))],
        [pl.BlockSpec((HBF, TB, VD), lambda h, i: (h, i, 0)), pl.BlockSpec((HBF, TB, 1), lambda h, i: (h, i, 0))],
        [jax.ShapeDtypeStruct((H, T, VD), F32), jax.ShapeDtypeStruct((H, T, 1), F32)],
        Exchange(rode, modes) if rode else None, rode, name, (q, k, v))


def attn_bwd(q, k, v, o, lse, do, name, rode=None, modes=None):
    H, T, _ = q.shape

    def body(q_ref, k_ref, v_ref, o_ref, lse_ref, do_ref, dq_ref, dk_ref, dv_ref):
        i = pl.program_id(1)

        @pl.when(i == 0)
        def _():
            dk_ref[...] = jnp.zeros_like(dk_ref)
            dv_ref[...] = jnp.zeros_like(dv_ref)

        def run(nk):
            keys = pl.ds(0, nk)
            for hh in range(HB):
                qv, kv, dov = q_ref[hh], k_ref[hh, keys, :], do_ref[hh]
                p = jnp.exp2(_dotf(qv, kv, "nt") - lse_ref[hh])
                delta = jnp.sum(dov * o_ref[hh], axis=1, keepdims=True)
                dob = dov.astype(BF16)
                dv_ref[hh, keys, :] += _dotf(p.astype(BF16), dob, "tn")
                dp = _dotf(dob, v_ref[hh, keys, :], "nt")
                ds = (p * (dp - delta)).astype(BF16)
                dq_ref[hh] = jnp.dot(ds, kv, preferred_element_type=F32)
                dk_ref[hh, keys, :] += _dotf(ds, qv, "tn")

        _by_query_block(run, T)

    blk = lambda c: pl.BlockSpec((HB, TB, c), lambda h, i: (h, i, 0))
    full = lambda c: pl.BlockSpec((HB, T, c), lambda h, i: (h, 0, 0))
    return _ride_call(
        body, (H // HB, T // TB), [blk(QK), full(QK), full(VD), blk(VD), blk(1), blk(VD)], [blk(QK), full(QK), full(VD)],
        [jax.ShapeDtypeStruct((H, T, QK), F32), jax.ShapeDtypeStruct((H, T, QK), F32), jax.ShapeDtypeStruct((H, T, VD), F32)],
        Exchange(rode, modes) if rode else None, rode, name, (q, k, v, o, lse, do))


def disc_fwd(a_re, a_im, ls, name):
    def body(ar_ref, ai_ref, ls_ref, lr_ref, li_ref, fr_ref, fi_ref):
        ar, ai = ar_ref[...], ai_ref[...]
        dt = jnp.exp(ls_ref[...])
        mag = jnp.exp(ar * dt)
        lr = mag * jnp.cos(ai * dt)
        li = mag * jnp.sin(ai * dt)
        den = ar * ar + ai * ai
        nr = lr - 1.0
        lr_ref[...] = lr
        li_ref[...] = li
        fr_ref[...] = (nr * ar + li * ai) / den
        fi_ref[...] = (li * ar - nr * ai) / den

    return pl.pallas_call(body, out_shape=[jax.ShapeDtypeStruct(a_re.shape, F32)] * 4, name=name)(a_re, a_im, ls)


def disc_b(f_re, f_im, b_re, b_im, name):
    def body(fr_ref, fi_ref, br_ref, bi_ref, or_ref, oi_ref):
        fr, fi, br, bi = fr_ref[...], fi_ref[...], br_ref[...], bi_ref[...]
        or_ref[...] = fr * br - fi * bi
        oi_ref[...] = fr * bi + fi * br

    fs, bs = _disc_b_specs()
    return pl.pallas_call(body, grid=(2, G * P // DISC_ROWS), in_specs=[fs, fs, bs, bs], out_specs=[bs, bs],
                          out_shape=[jax.ShapeDtypeStruct(b_re.shape, F32)] * 2, name=name)(f_re, f_im, b_re, b_im)


DISC_ROWS = G * P


def _disc_b_specs():
    return (pl.BlockSpec((1, DISC_ROWS, 1), lambda d, i: (d, i, 0)), pl.BlockSpec((1, DISC_ROWS, CH), lambda d, i: (d, i, 0)))


def disc_b_bwd(f_re, f_im, b_re, b_im, dbb_re, dbb_im, name):
    def body(fr_ref, fi_ref, br_ref, bi_ref, dr_ref, di_ref, dbr_ref, dbi_ref, dfr_ref, dfi_ref):
        fr, fi, br, bi, dr, di = fr_ref[...], fi_ref[...], br_ref[...], bi_ref[...], dr_ref[...], di_ref[...]
        dbr_ref[...] = fr * dr + fi * di
        dbi_ref[...] = fr * di - fi * dr
        dfr_ref[...] = jnp.sum(dr * br + di * bi, axis=-1, keepdims=True)
        dfi_ref[...] = jnp.sum(di * br - dr * bi, axis=-1, keepdims=True)

    fs, bs = _disc_b_specs()
    return pl.pallas_call(body, grid=(2, G * P // DISC_ROWS), in_specs=[fs, fs, bs, bs, bs, bs], out_specs=[bs, bs, fs, fs],
                          out_shape=[jax.ShapeDtypeStruct(b_re.shape, F32)] * 2 + [jax.ShapeDtypeStruct(f_re.shape, F32)] * 2,
                          name=name)(f_re, f_im, b_re, b_im, dbb_re, dbb_im)


def disc_a_bwd(a_re, a_im, ls, dlr, dli, dfr, dfi, name):
    def body(ar_ref, ai_ref, ls_ref, dlr_ref, dli_ref, dfr_ref, dfi_ref, dar_ref, dai_ref, dls_ref):
        ar, ai = ar_ref[...], ai_ref[...]
        dt = jnp.exp(ls_ref[...])
        mag = jnp.exp(ar * dt)
        cs, sn = jnp.cos(ai * dt), jnp.sin(ai * dt)
        lr, li = mag * cs, mag * sn
        den = ar * ar + ai * ai
        nr = lr - 1.0
        f_re = (nr * ar + li * ai) / den
        f_im = (li * ar - nr * ai) / den
        dn1 = dfr_ref[...] / den
        dn2 = dfi_ref[...] / den
        dden = -(dfr_ref[...] * f_re + dfi_ref[...] * f_im) / den
        dlr_t = dlr_ref[...] + dn1 * ar - dn2 * ai
        dli_t = dli_ref[...] + dn1 * ai + dn2 * ar
        dar = dn1 * nr + dn2 * li + dden * 2.0 * ar
        dai = dn1 * li - dn2 * nr + dden * 2.0 * ai
        dmag = dlr_t * cs + dli_t * sn
        dth = dli_t * lr - dlr_t * li
        dar_ref[...] = dar + dmag * mag * dt
        dai_ref[...] = dai + dth * dt
        dls_ref[...] = jnp.sum(dmag * mag * ar + dth * ai, axis=-1, keepdims=True) * dt

    return pl.pallas_call(body, out_shape=[jax.ShapeDtypeStruct(a_re.shape, F32)] * 2 +
                          [jax.ShapeDtypeStruct(ls.shape, F32)], name=name)(a_re, a_im, ls, dlr, dli, dfr, dfi)


def _cpow(lr, li, n):
    rr, ri = None, None
    br, bi = lr, li
    while n:
        if n & 1:
            if rr is None:
                rr, ri = br, bi
            else:
                rr, ri = rr * br - ri * bi, rr * bi + ri * br
        n >>= 1
        if n:
            br, bi = br * br - bi * bi, 2.0 * br * bi
    return rr, ri


UNROLL = 4


def _steps(trips, fn, init):
    main = trips // UNROLL

    def body(i, c):
        for j in range(UNROLL):
            c = fn(i * UNROLL + j, c)
        return c

    c = lax.fori_loop(0, main, body, init) if main else init
    for n in range(main * UNROLL, trips):
        c = fn(n, c)
    return c


def _seg_scan(xre, xim, lam8, pw, base, seglen, rev, init, fin_re, fin_im, ini_re, ini_im, prev=None):
    lr, li = lam8
    nsub = SEG // 8

    def rows(t, s):
        first = base + t * SEG + 8 * s
        return pl.ds(first if isinstance(first, int) else pl.multiple_of(first, 8), 8)

    tmap = (lambda n: seglen - 1 - n) if rev else (lambda n: n)
    zeros = tuple(jnp.zeros((8, SB), F32) for _ in range(2 * nsub))

    def advance(c, t):
        out = []
        for s in range(nsub):
            a, b = c[2 * s], c[2 * s + 1]
            out += [lr * a - li * b + xre[rows(t, s), :], lr * b + li * a + xim[rows(t, s), :]]
        return tuple(out)

    fin = _steps(seglen, lambda n, c: advance(c, tmap(n)), zeros)
    for s in range(nsub):
        fin_re[pl.ds(8 * s, 8), :] = fin[2 * s]
        fin_im[pl.ds(8 * s, 8), :] = fin[2 * s + 1]
    (cr, ci), (pr, pi) = init, pw
    for i in (range(SEG - 1, -1, -1) if rev else range(SEG)):
        ini_re[pl.ds(i, 1), :] = cr
        ini_im[pl.ds(i, 1), :] = ci
        cr, ci = pr * cr - pi * ci + fin_re[pl.ds(i, 1), :], pr * ci + pi * cr + fin_im[pl.ds(i, 1), :]
    tiles = lambda re, im: tuple(r[pl.ds(8 * s, 8), :] for s in range(nsub) for r in (re, im))
    start = tiles(ini_re, ini_im)

    def store(c, t):
        new = advance(c, t)
        for s in range(nsub):
            xre[rows(t, s), :] = new[2 * s]
            xim[rows(t, s), :] = new[2 * s + 1]
        return new

    if prev is None:
        _steps(seglen, lambda n, c: store(c, tmap(n)), start)
        return (cr, ci), None

    sre, sim, s_ini_re, s_ini_im = prev

    def acc_step(c, t, before):
        new = store(c[:2 * nsub], t)
        acc = []
        for s in range(nsub):
            (na, nb), (pre, pim) = new[2 * s:2 * s + 2], before[2 * s:2 * s + 2]
            acc += [c[2 * nsub + 2 * s] + na * pre + nb * pim, c[2 * nsub + 2 * s + 1] + nb * pre - na * pim]
        return new + tuple(acc)

    def body(n, c):
        t = tmap(n)
        tp = t - 1 if rev else t + 1
        return acc_step(c, t, tuple(r[rows(tp, s), :] for s in range(nsub) for r in (sre, sim)))

    c = _steps(seglen - 1, body, start + zeros)
    c = acc_step(c, 0 if rev else seglen - 1, tiles(s_ini_re, s_ini_im))
    acc = c[2 * nsub:]
    return (cr, ci), (sum(acc[0::2][1:], acc[0]), sum(acc[1::2][1:], acc[1]))


def _lam_tiles(lr, li, lens, conj=False):
    if conj:
        li = -li
    lam8 = (jnp.broadcast_to(lr, (8, SB)), jnp.broadcast_to(li, (8, SB)))
    return lam8, [_cpow(lr, li, n) for n in lens]


def _stretches(T):
    return ((0, LC // SEG), (LC, (T - LC) // SEG))


def _to_seg_order(src, dst, T):
    for base, seglen in _stretches(T):
        def body(t, carry, base=base, seglen=seglen):
            dst[pl.ds(pl.multiple_of(base + t * SEG, SEG), SEG), :] = src[pl.ds(base + t, SEG, stride=seglen), :]
            return carry
        lax.fori_loop(0, seglen, body, 0, unroll=8)


def _from_seg_order(src, dst, T):
    for base, seglen in _stretches(T):
        def body(t, carry, base=base, seglen=seglen):
            dst[pl.ds(base + t, SEG, stride=seglen), :] = src[pl.ds(pl.multiple_of(base + t * SEG, SEG), SEG), :]
            return carry
        lax.fori_loop(0, seglen, body, 0, unroll=8)


def _scan_specs(T):
    ublk = pl.BlockSpec((T, UB), lambda j: (0, j))
    lam = pl.BlockSpec((2, 1, 1, SB), lambda j: (0, j, 0, 0))
    mat = pl.BlockSpec((2, 1, UB, P), lambda j: (0, j, 0, 0))
    return ublk, lam, mat


def _dotf(a, b, mode="nn"):
    return lax.dot_general(a, b, _DN[mode], preferred_element_type=F32)


def _diag_mask():
    r = lax.broadcasted_iota(jnp.int32, (UB, SB), 0)
    c = lax.broadcasted_iota(jnp.int32, (UB, SB), 1)
    return lax.shift_right_logical(r, int(math.log2(CH))) == lax.shift_right_logical(c, int(math.log2(P)))


def _expand(m):
    p = lax.broadcasted_iota(jnp.int32, (P, SB), 0)
    c = lax.broadcasted_iota(jnp.int32, (P, SB), 1)
    tile = jnp.where(lax.bitwise_and(c, P - 1) == p, 1.0, 0.0).astype(BF16)
    wide = jnp.dot(m.astype(BF16), tile, preferred_element_type=F32)
    return jnp.where(_diag_mask(), wide, 0.0).astype(BF16)


def _collapse(full):
    c = lax.broadcasted_iota(jnp.int32, (SB, P), 0)
    p = lax.broadcasted_iota(jnp.int32, (SB, P), 1)
    pick = jnp.where(lax.bitwise_and(c, P - 1) == p, 1.0, 0.0).astype(BF16)
    return _exact_perm(jnp.where(_diag_mask(), full, 0.0), pick)


def _zero_state():
    return jnp.zeros((1, SB), F32), jnp.zeros((1, SB), F32)


def scan_fwd(u, lam_re, lam_im, bre, bim, cre, cim, name):
    T = u.shape[0]
    s_ctx, s_lat = LC // SEG, (T - LC) // SEG

    def body(u_ref, lr_ref, li_ref, bre_ref, bim_ref, cre_ref, cim_ref, y_ref, us, ys, sre, sim, fre, fim, ire, iim):
        _to_seg_order(u_ref, us, T)
        ub = us[...].astype(BF16)
        for d in range(2):
            lam8, (pw_c, pw_l) = _lam_tiles(lr_ref[d, 0], li_ref[d, 0], (s_ctx, s_lat))
            sre[...] = _dotf(ub, _expand(bre_ref[d, 0]))
            sim[...] = _dotf(ub, _expand(bim_ref[d, 0]))
            end_c, _ = _seg_scan(sre, sim, lam8, pw_c, 0, s_ctx, bool(d), _zero_state(), fre, fim, ire, iim)
            _seg_scan(sre, sim, lam8, pw_l, LC, s_lat, bool(d), end_c, fre, fim, ire, iim)
            y = (_dotf(sre[...].astype(BF16), _expand(cre_ref[d, 0]), "nt")
                 - _dotf(sim[...].astype(BF16), _expand(cim_ref[d, 0]), "nt"))
            if d == 0:
                ys[...] = y
            else:
                ys[...] += y
        _from_seg_order(ys, y_ref, T)

    ublk, lam, mat = _scan_specs(T)
    return pl.pallas_call(
        body, grid=(NJ,), in_specs=[ublk, lam, lam, mat, mat, mat, mat], out_specs=ublk,
        out_shape=jax.ShapeDtypeStruct((T, G * CH), F32),
        scratch_shapes=[pltpu.VMEM((T, UB), F32)] * 2 + [pltpu.VMEM((T, SB), F32)] * 2 + [pltpu.VMEM((SEG, SB), F32)] * 4,
        compiler_params=_cp(("arbitrary",)), name=name)(u, lam_re, lam_im, bre, bim, cre, cim)


def scan_bwd(u, dy, lam_re, lam_im, bre, bim, cre, cim, name):
    T = u.shape[0]
    s_ctx, s_lat = LC // SEG, (T - LC) // SEG

    def body(u_ref, dy_ref, lr_ref, li_ref, bre_ref, bim_ref, cre_ref, cim_ref,
             du_ref, dlr_ref, dli_ref, dbre_ref, dbim_ref, dcre_ref, dcim_ref,
             us, dys, dus, sre, sim, gre, gim, fre, fim, ic_re, ic_im, il_re, il_im, jre, jim):
        _to_seg_order(u_ref, us, T)
        _to_seg_order(dy_ref, dys, T)
        ub, dyb = us[...].astype(BF16), dys[...].astype(BF16)
        for d in range(2):
            rev = bool(d)
            lam8, (pw_c, pw_l) = _lam_tiles(lr_ref[d, 0], li_ref[d, 0], (s_ctx, s_lat))
            cam8, (cw_c, cw_l) = _lam_tiles(lr_ref[d, 0], li_ref[d, 0], (s_ctx, s_lat), conj=True)
            bre_v, bim_v = _expand(bre_ref[d, 0]), _expand(bim_ref[d, 0])
            sre[...] = _dotf(ub, bre_v)
            sim[...] = _dotf(ub, bim_v)
            end_c, _ = _seg_scan(sre, sim, lam8, pw_c, 0, s_ctx, rev, _zero_state(), fre, fim, ic_re, ic_im)
            _seg_scan(sre, sim, lam8, pw_l, LC, s_lat, rev, end_c, fre, fim, il_re, il_im)
            gre[...] = _dotf(dyb, _expand(cre_ref[d, 0]))
            gim[...] = -_dotf(dyb, _expand(cim_ref[d, 0]))
            end_g, acc_l = _seg_scan(gre, gim, cam8, cw_l, LC, s_lat, not rev, _zero_state(), fre, fim, jre, jim,
                                     prev=(sre, sim, il_re, il_im))
            _, acc_c = _seg_scan(gre, gim, cam8, cw_c, 0, s_ctx, not rev, end_g, fre, fim, jre, jim,
                                 prev=(sre, sim, ic_re, ic_im))
            dlr_ref[d, 0] = _sum0(acc_l[0] + acc_c[0])
            dli_ref[d, 0] = _sum0(acc_l[1] + acc_c[1])
            grb, gib = gre[...].astype(BF16), gim[...].astype(BF16)
            du = _dotf(grb, bre_v, "nt") + _dotf(gib, bim_v, "nt")
            if d == 0:
                dus[...] = du
            else:
                dus[...] += du
            dbre_ref[d, 0] = _collapse(_dotf(ub, grb, "tn"))
            dbim_ref[d, 0] = _collapse(_dotf(ub, gib, "tn"))
            dcre_ref[d, 0] = _collapse(_dotf(dyb, sre[...].astype(BF16), "tn"))
            dcim_ref[d, 0] = -_collapse(_dotf(dyb, sim[...].astype(BF16), "tn"))
        _from_seg_order(dus, du_ref, T)

    ublk, lam, mat = _scan_specs(T)
    lam_s = jax.ShapeDtypeStruct(lam_re.shape, F32)
    mat_s = jax.ShapeDtypeStruct(bre.shape, F32)
    return pl.pallas_call(
        body, grid=(NJ,), in_specs=[ublk, ublk, lam, lam, mat, mat, mat, mat],
        out_specs=[ublk, lam, lam, mat, mat, mat, mat],
        out_shape=[jax.ShapeDtypeStruct((T, G * CH), F32), lam_s, lam_s, mat_s, mat_s, mat_s, mat_s],
        scratch_shapes=[pltpu.VMEM((T, UB), F32)] * 3 + [pltpu.VMEM((T, SB), F32)] * 4 + [pltpu.VMEM((SEG, SB), F32)] * 8,
        compiler_params=_cp(("arbitrary",)), name=name)(u, dy, lam_re, lam_im, bre, bim, cre, cim)


class Exchange:
    def __init__(self, xs, modes):
        self.n = len(xs)
        self.modes = [modes] * self.n if isinstance(modes, (str, int)) else list(modes)
        self.out_shape = [jax.ShapeDtypeStruct(self._shape(x, md), x.dtype) for x, md in zip(xs, self.modes)]
        self.scratch = [pltpu.SemaphoreType.DMA((NDEV - 1, self.n)), pltpu.SemaphoreType.DMA((NDEV - 1, self.n)),
                        pltpu.SemaphoreType.DMA((self.n,))]
        self.specs = [pl.BlockSpec(memory_space=pl.ANY)] * self.n

    @staticmethod
    def _shape(x, mode):
        if mode == "gather":
            return (NDEV,) + tuple(x.shape)
        return tuple(x.shape) if mode == "lead" else (NDEV, x.shape[0], mode) + tuple(x.shape[2:])

    @staticmethod
    def _piece(x_ref, mode, dev):
        if mode == "gather":
            return x_ref
        return x_ref.at[dev] if mode == "lead" else x_ref.at[:, pl.ds(dev * mode, mode)]

    def _copies(self, x_refs, out_refs, sems):
        send_sems, recv_sems, local_sems = sems
        mx, my, mc = lax.axis_index("x"), lax.axis_index("y"), lax.axis_index("c")
        me = 4 * mx + 2 * my + mc
        peer_of = lambda k: (1 - mx if k & 4 else mx, 1 - my if k & 2 else my, 1 - mc if k & 1 else mc)
        local, first, relay, arrivals = [], [], [], []
        for a, (x_ref, out_ref) in enumerate(zip(x_refs, out_refs)):
            mode = self.modes[a]
            local.append(pltpu.make_async_copy(self._piece(x_ref, mode, me), out_ref.at[me], local_sems.at[a]))

            def remote(src, dst, k, pair, a=a):
                return pltpu.make_async_remote_copy(src_ref=src, dst_ref=dst, send_sem=send_sems.at[pair, a],
                                                    recv_sem=recv_sems.at[pair, a], device_id=peer_of(k), device_id_type=MESH_T)

            for k in range(1, NDEV):
                peer = peer_of(k)
                pid = 4 * peer[0] + 2 * peer[1] + peer[2]
                if mode != "gather":
                    src = self._piece(x_ref, mode, pid)
                    first.append(remote(src, out_ref.at[me], k, k - 1))
                    arrivals.append(remote(src, out_ref.at[pid], k, k - 1))
                elif k == 1:
                    first.append(remote(x_ref, out_ref.at[me], k, k - 1))
                    arrivals.append(remote(x_ref, out_ref.at[pid], k, k - 1))
                elif k % 2 == 0:
                    first.append(remote(x_ref, out_ref.at[me], k, k - 1))
                    relay.append((remote(x_ref, out_ref.at[pid], k, k - 1), remote(out_ref.at[pid], out_ref.at[pid], 1, k)))
                else:
                    arrivals.append(remote(x_ref, out_ref.at[pid], 1, k - 1))
        return local, first, relay, arrivals

    def start(self, x_refs, out_refs, sems):
        local, first, _, _ = self._copies(x_refs, out_refs, sems)
        for cp in local + first:
            cp.start()

    def finish(self, x_refs, out_refs, sems):
        local, first, relay, arrivals = self._copies(x_refs, out_refs, sems)
        for arrival, onward in relay:
            arrival.wait_recv()
            onward.start()
        for cp in arrivals:
            cp.wait_recv()
        for cp in first + [onward for _, onward in relay]:
            cp.wait_send()
        for cp in local:
            cp.wait()


def exchange(xs, modes, name):
    ex = Exchange(xs, modes)
    n = ex.n

    def body(*refs):
        ex.start(refs[:n], refs[n:2 * n], refs[2 * n:])
        ex.finish(refs[:n], refs[n:2 * n], refs[2 * n:])

    return pl.pallas_call(body, in_specs=ex.specs, out_specs=ex.specs, out_shape=ex.out_shape, scratch_shapes=ex.scratch,
                          compiler_params=pltpu.CompilerParams(has_side_effects=True), name=name)(*xs)


def _dot_f32(a, b, dn):
    return lax.dot_general(a, b, dn, preferred_element_type=F32, precision=lax.Precision.HIGHEST)


def ada_fwd(cg, c_ctx, ada_w, ada_b_loc, name):
    W = ada_w.shape[2]

    def body(cg_ref, cc_ref, w_ref, b_ref, o_ref):
        a = jnp.concatenate([_silu(cg_ref[...]), jnp.broadcast_to(_silu(cc_ref[...]), (NDEV, D))], axis=0)
        for i in range(2):
            o_ref[i] = _dot_f32(a, w_ref[i], _DN["nn"]) + b_ref[i]

    return pl.pallas_call(body, out_shape=jax.ShapeDtypeStruct((2, 2 * NDEV, W), F32),
                          compiler_params=_cp(), name=name)(cg, c_ctx, ada_w, ada_b_loc)


def ada_bwd(cg, c_ctx, ada_w, dm_loc, dm_all, name):
    W = ada_w.shape[2]

    def body(cg_ref, cc_ref, w_ref, dl_ref, da_ref, gw_ref, dcc_ref, gb_ref):
        a = jnp.concatenate([_silu(cg_ref[...]), jnp.broadcast_to(_silu(cc_ref[...]), (NDEV, D))], axis=0)
        dcc = jnp.zeros((1, D), F32)
        for i in range(2):
            dl = dl_ref[i]
            gw_ref[i] = _dot_f32(a, dl, _DN["tn"])
            dctx = jnp.sum(dl[NDEV:], axis=0, keepdims=True)
            dcc = dcc + _dot_f32(dctx, w_ref[i], _DN["nt"])
        dcc_ref[...] = dcc
        gb_ref[...] = jnp.sum(da_ref[...], axis=0)

    return pl.pallas_call(body, out_shape=[jax.ShapeDtypeStruct((2, D, W), F32), jax.ShapeDtypeStruct((1, D), F32),
                                           jax.ShapeDtypeStruct((2, 3 * D), F32)],
                          compiler_params=_cp(), name=name)(cg, c_ctx, ada_w, dm_loc, dm_all)


def cctx_finish(parts, c_ctx, name):
    def body(p_ref, cc_ref, o_ref):
        o_ref[...] = jnp.sum(p_ref[...], axis=0, keepdims=True) * _dsilu(cc_ref[...])

    return pl.pallas_call(body, out_shape=jax.ShapeDtypeStruct((1, D), F32), name=name)(parts, c_ctx)


def _adamw_update(g_ref, w_ref, m_ref, v_ref, go_ref, d_ref, mo_ref, vo_ref):
    g = g_ref[0].astype(F32)
    for s in range(1, g_ref.shape[0]):
        g = g + g_ref[s].astype(F32)
    mn = B1 * m_ref[...] + (1.0 - B1) * g
    vn = B2 * v_ref[...] + (1.0 - B2) * g * g
    go_ref[...] = g
    mo_ref[...] = mn
    vo_ref[...] = vn
    d_ref[...] = -LR * ((mn * (1.0 / (1.0 - B1 ** STEP))) / (jnp.sqrt(vn * (1.0 / (1.0 - B2 ** STEP))) + AEPS) + WD * w_ref[...])


ADAMW_PARTS = 4


def adamw_rows(items, name, rode=None, modes=None):
    in_specs, out_specs, out_shape, args = [], [], [], []
    for g, w, m, v in items:
        n, R, C = g.shape
        tr = R // ADAMW_PARTS
        spec = pl.BlockSpec((tr, C), lambda i, j: (i, 0))
        in_specs += [pl.BlockSpec((n, tr, C), lambda i, j: (0, i, 0)), spec, spec, spec]
        args += [g, w, m, v]
    for g, w, m, v in items:
        tr = w.shape[0] // ADAMW_PARTS
        out_specs += [pl.BlockSpec((tr, w.shape[1]), lambda i, j: (i, 0))] * 4
        out_shape += [jax.ShapeDtypeStruct(w.shape, F32)] * 4
    res, got = _ride_call(_adamw_body(len(items)), (ADAMW_PARTS, 1), in_specs, out_specs, out_shape,
                          Exchange(rode, modes) if rode else None, rode, name, args)
    return [res[4 * t:4 * t + 4] for t in range(len(items))], got


def _adamw_body(k):
    def body(*refs):
        for t in range(k):
            _adamw_update(*refs[4 * t:4 * t + 4], *refs[4 * k + 4 * t:4 * k + 4 * t + 4])
    return body


def adamw_multi(items, grid, name):
    k = len(items)
    ins, in_specs, out_specs, out_shape = [], [], [], []
    for g, g_spec, w, m, v, w_spec in items:
        ins += [g, w, m, v]
        in_specs += [g_spec, w_spec, w_spec, w_spec]
    for g, g_spec, w, m, v, w_spec in items:
        out_specs += [w_spec] * 4
        out_shape += [jax.ShapeDtypeStruct(w.shape, F32)] * 4
    res = pl.pallas_call(_adamw_body(k), grid=grid, in_specs=in_specs, out_specs=out_specs, out_shape=out_shape,
                         compiler_params=_cp(("arbitrary",) * len(grid)), name=name)(*ins)
    return [res[4 * t:4 * t + 4] for t in range(k)]


def _whole(a, grid_rank):
    zeros = (0,) * a.ndim
    return pl.BlockSpec(a.shape, lambda *idx: zeros)


def sum_slots(xs, name):
    def body(*refs):
        for x_ref, o_ref in zip(refs[:len(xs)], refs[len(xs):]):
            acc = x_ref[0]
            for s in range(1, NDEV):
                acc = acc + x_ref[s]
            o_ref[...] = acc

    return pl.pallas_call(body, out_shape=[jax.ShapeDtypeStruct(x.shape[1:], F32) for x in xs],
                          compiler_params=_cp(), name=name)(*xs)


def _col_shards(g):
    R, N = g.shape
    return g.reshape(R, NDEV, N // NDEV).transpose(1, 0, 2)


def _vec2(v):
    return jnp.broadcast_to(v.reshape(1, 1, -1), (2, 1, v.size))


SHARD_ROWS = {"mla_w_in": 192, "mla_w_uq": 192, "mla_w_ukv": 256, "s5_w_in": 256}


def _t_shard(wsh, rows):
    t = wsh[0].T.astype(BF16)
    return jnp.pad(t, ((0, rows - t.shape[0]), (0, 0)))


def _win_order():
    w = IN_W // NDEV
    perm = np.zeros((IN_WP, NDEV * SHARD_ROWS["mla_w_in"]), np.float32)
    first = QL + KVL + ROPE
    for c in range(IN_W):
        n = c + HEADS * VD if c < first else c - first
        perm[n, (c // w) * SHARD_ROWS["mla_w_in"] + c % w] = 1.0
    return jnp.asarray(perm, BF16)


def local_step(ctx, x, tgt, mod, Wt, small, l1_shards):
    T = LC + x.shape[0]
    xa = ("cat", ctx, x)
    sh = [mod[i, :, None, 0:D] for i in range(2)]
    sc = [mod[i, :, None, D:2 * D] for i in range(2)]
    gt = [mod[i, :, None, 2 * D:] for i in range(2)]
    ng = [_vec2(small["norm_g"][i]) for i in range(2)]
    qg, kvg = _vec2(small["mla_q_norm"]), _vec2(small["mla_kv_norm"])
    cosf, sinf, pm, pmt = _rope_tables(T)

    (h0, p0, cqn, ckvn), _ = rowwise(st_l0_pre, [xa], [ng[0], sc[0], sh[0], qg, kvg],
                                     [(D, BF16), (IN_WP, F32), (QL, BF16), (KVL, BF16)], [], "l0_pre", mats=[Wt["mla_w_in"]])
    z0, cq, ckv = (p0, 0, HEADS * VD), (p0, HEADS * VD // QL, QL), (p0, (HEADS * VD + QL) // KVL, KVL)
    Q = project_q(cqn, Wt["mla_w_uq"], cosf, sinf, pm, "l0_uq")
    K, V = project_kv(ckvn, Wt["mla_w_ukv"], p0, (HEADS * VD + QL + KVL) // 128, "l0_ukv")
    (o, lse), got = attn_fwd(Q, K, V, "l0_attn", rode=l1_shards, modes="gather")
    Wt, small = dict(Wt), dict(small)
    for n, a in zip(L1_BIG, got):
        Wt[n] = a.reshape(-1, a.shape[-1])
    vecs = lax.bitcast_convert_type(got[-1].reshape(NDEV, 2, -1, 2), F32)
    small["s5_d"], small["s5_b_glu"] = vecs[:, 0, :].reshape(D), vecs[:, 1, :].reshape(D)
    o2 = o.transpose(1, 0, 2).reshape(T, HEADS * VD)
    (og, out0, x1), _ = rowwise(st_l0_post, [o2, z0, xa], [gt[0]], [(D, BF16), (D, BF16), (D, F32)], [], "l0_post",
                                mats=[Wt["mla_w_out"]])

    ls = small["s5_log_step"].reshape(2, G, 1)
    a_re, a_im = small["s5_a_re"].reshape(2, G, P), small["s5_a_im"].reshape(2, G, P)
    b_re, b_im = small["s5_b_re"].reshape(2, G * P, CH), small["s5_b_im"].reshape(2, G * P, CH)
    lam_re, lam_im, f_re, f_im = disc_fwd(a_re, a_im, ls, "s5_disc")
    f_re2, f_im2 = f_re.reshape(2, G * P, 1), f_im.reshape(2, G * P, 1)
    bb_re, bb_im = disc_b(f_re2, f_im2, b_re, b_im, "s5_disc_b")
    compact = lambda m: m.reshape(2, NJ, UB, P)
    bre = compact(bb_re.reshape(2, G, P, CH).transpose(0, 1, 3, 2))
    bim = compact(bb_im.reshape(2, G, P, CH).transpose(0, 1, 3, 2))
    cre, cim = compact(small["s5_c_re"]), compact(small["s5_c_im"])
    lam_re4, lam_im4 = lam_re.reshape(2, NJ, 1, SB), lam_im.reshape(2, NJ, 1, SB)

    (h1, p1), _ = rowwise(st_l1_pre, [x1], [ng[1], sc[1], sh[1]], [(D, BF16), (2 * D, F32)], [], "l1_pre", mats=[Wt["s5_w_in"]])
    u, z1 = (p1, 0, D), (p1, 1, D)
    yssm = scan_fwd(p1, lam_re4, lam_im4, bre, bim, cre, cim, "s5_scan")
    dvec, bglu = _vec2(small["s5_d"]), _vec2(small["s5_b_glu"])
    fg = _vec2(small["final_g"])
    lat_mask = jnp.stack([jnp.zeros((1, D), F32), jnp.ones((1, D), F32)])
    (y, y1b, gl, y3, out1, dx2), (dfg, lvec) = rowwise(
        st_l1_mlp, [yssm, u, z1, x1, ("lat", tgt)], [dvec, bglu, gt[1], fg, lat_mask],
        [(D, F32), (D, BF16), (D, BF16), (D, BF16), (D, BF16), (D, F32)], [D, 128], "l1_mlp",
        mats=[Wt["s5_w_glu"], Wt["s5_w_out"]])

    (dz1, dy, du_d), (dgt1, dbglu, dd), (g_w_out5, g_w_glu) = rowwise(
        st_l1_mlp_bwd, [dx2, out1, y3, y, gl, z1, u, y1b], [gt[1], bglu, dvec], [(D, BF16), (D, F32), (D, F32)], [D, D, D],
        "l1_mlp_b", mats=[Wt["s5_w_out"], Wt["s5_w_glu"]], out_accs=[(D, D), (D, D)])
    du_s, dlr, dli, dbre, dbim, dcre, dcim = scan_bwd(p1, dy, lam_re4, lam_im4, bre, bim, cre, cim, "s5_scan_b")
    dbb_re = dbre.reshape(2, G, CH, P).transpose(0, 1, 3, 2).reshape(2, G * P, CH)
    dbb_im = dbim.reshape(2, G, CH, P).transpose(0, 1, 3, 2).reshape(2, G * P, CH)
    g_c_re, g_c_im = dcre.reshape(2, G, CH, P), dcim.reshape(2, G, CH, P)
    g_b_re, g_b_im, dfr, dfi = disc_b_bwd(f_re2, f_im2, b_re, b_im, dbb_re, dbb_im, "s5_disc_b_b")
    g_a_re, g_a_im, g_ls = disc_a_bwd(a_re, a_im, ls, dlr.reshape(2, G, P), dli.reshape(2, G, P),
                                      dfr.reshape(2, G, P), dfi.reshape(2, G, P), "s5_disc_b_a")
    (dx1,), (dsh1, dsc1, dng1), (g_w_in5,) = rowwise(
        st_l1_tail_bwd, [du_d, du_s, dz1, h1, x1, dx2], [ng[1], sc[1]], [(D, F32)], [D, D, D], "l1_pre_b",
        mats=[Wt["s5_w_in"]], out_accs=[(D, 2 * D)])
    g_w_in5 = _col_shards(g_w_in5)

    (do2, dz0), (dgt0,), (g_w_out,) = rowwise(st_l0_post_bwd, [dx1, out0, og, o2, z0], [gt[0]], [(D, F32), (D, F32)], [D],
                                              "l0_post_b", mats=[Wt["mla_w_out"]], out_accs=[(D, D)])
    doh = do2.reshape(T, HEADS, VD).transpose(1, 0, 2)
    rows8 = lambda g: g.reshape(NDEV, -1, g.shape[-1])
    both = lambda s: s[0, 0] + s[1, 0]
    dense = lambda g: g.reshape(2, G * P * CH // 128, 128)
    chunks = [dense(g_b_re), dense(g_b_im), g_c_re, g_c_im]
    l1_send = [g_w_in5, rows8(g_w_glu), rows8(g_w_out5), rows8(g_w_out),
               both(dd).reshape(NDEV, 1, -1), both(dbglu).reshape(NDEV, 1, -1)]
    (dQ, dK, dV), l1_recv = attn_bwd(Q, K, V, o, lse, doh, "l0_attn_b", rode=l1_send + chunks,
                                     modes=["lead"] * len(l1_send) + [a.shape[1] // NDEV for a in chunks])
    dqh = rope(dQ, cosf, sinf, pmt, True, BF16, "l0_rope_q_b", scale=SCALE)
    dq = dqh.transpose(1, 0, 2).reshape(T, HEADS * QK)
    dkv, dkr = split_kv_grads(dK, dV, "l0_kv_b")
    (grad_x,), (dqg, dkvg, dsh0, dsc0, dng0), (g_uq, g_ukv, g_p) = rowwise(
        st_l0_tail_bwd, [dq, dkv, dkr, dz0, cq, ckv, cqn, ckvn, h0, xa, dx1], [qg, kvg, ng[0], sc[0]],
        [(D, F32, "lat")], [QL, KVL, D, D, D], "l0_pre_b", mats=[Wt["mla_w_uq"], Wt["mla_w_ukv"], Wt["mla_w_in"]],
        out_accs=[(QL, HEADS * QK), (KVL, HEADS * KVW), (D, IN_WP)])
    g_w_uq, g_w_ukv = _col_shards(g_uq).astype(BF16), _col_shards(g_ukv).astype(BF16)
    g_w_in = _col_shards(jnp.concatenate([g_p[:, HEADS * VD:IN_W], g_p[:, :HEADS * VD]], axis=1)).astype(BF16)

    dmod = jnp.stack([jnp.concatenate([dsh0, dsc0, dgt0], axis=-1)[:, 0], jnp.concatenate([dsh1, dsc1, dgt1], axis=-1)[:, 0]])
    gbig = {"mla_w_in": g_w_in, "mla_w_uq": g_w_uq, "mla_w_ukv": g_w_ukv}
    gsmall = {"norm_g": jnp.stack([both(dng0), both(dng1)]), "mla_q_norm": both(dqg), "mla_kv_norm": both(dkvg),
              "s5_a_re": g_a_re, "s5_a_im": g_a_im, "s5_log_step": g_ls, "final_g": dfg[1, 0]}
    return lvec[1], grad_x, dmod, gbig, gsmall, l1_recv


COL_SHARDED = ("mla_w_in", "mla_w_uq", "mla_w_ukv", "s5_w_in")
ROW_SHARDED = ("mla_w_out", "s5_w_glu", "s5_w_out")
VEC_SHARDED = ("s5_d", "s5_b_glu")
BIG = COL_SHARDED + ROW_SHARDED
L0_BIG = ("mla_w_in", "mla_w_uq", "mla_w_ukv")
L1_BIG = ("s5_w_in", "s5_w_glu", "s5_w_out", "mla_w_out")
BITS16 = jnp.bfloat16
SMALL_RS = ("norm_g", "mla_q_norm", "mla_kv_norm", "s5_a_re", "s5_a_im", "s5_log_step", "s5_b_re", "s5_b_im",
            "s5_c_re", "s5_c_im", "final_g")
CHUNKED = ("s5_b_re", "s5_b_im", "s5_c_re", "s5_c_im")
DENSE = ("s5_b_re", "s5_b_im")
TINY = ("norm_g", "mla_q_norm", "mla_kv_norm", "s5_a_re", "s5_a_im", "s5_log_step", "final_g")
ORDER = ("c_ctx", "ada_w", "ada_b", "norm_g", "mla_w_in", "mla_q_norm", "mla_w_uq", "mla_kv_norm", "mla_w_ukv",
         "mla_w_out", "s5_w_in", "s5_a_re", "s5_a_im", "s5_log_step", "s5_b_re", "s5_b_im", "s5_c_re", "s5_c_im",
         "s5_d", "s5_w_glu", "s5_b_glu", "s5_w_out", "final_g")


def kernel(x, c, ctx, c_ctx, ada_w, ada_b, norm_g, mla_w_in, mla_q_norm, mla_w_uq, mla_kv_norm, mla_w_ukv, mla_w_out, s5_w_in, s5_a_re, s5_a_im, s5_log_step, s5_b_re, s5_b_im, s5_c_re, s5_c_im, s5_d, s5_w_glu, s5_b_glu, s5_w_out, final_g, loss_target, m_c_ctx, m_ada_w, m_ada_b, m_norm_g, m_mla_w_in, m_mla_q_norm, m_mla_w_uq, m_mla_kv_norm, m_mla_w_ukv, m_mla_w_out, m_s5_w_in, m_s5_a_re, m_s5_a_im, m_s5_log_step, m_s5_b_re, m_s5_b_im, m_s5_c_re, m_s5_c_im, m_s5_d, m_s5_w_glu, m_s5_b_glu, m_s5_w_out, m_final_g, v_c_ctx, v_ada_w, v_ada_b, v_norm_g, v_mla_w_in, v_mla_q_norm, v_mla_w_uq, v_mla_kv_norm, v_mla_w_ukv, v_mla_w_out, v_s5_w_in, v_s5_a_re, v_s5_a_im, v_s5_log_step, v_s5_b_re, v_s5_b_im, v_s5_c_re, v_s5_c_im, v_s5_d, v_s5_w_glu, v_s5_b_glu, v_s5_w_out, v_final_g):
    w = dict(c_ctx=c_ctx, ada_w=ada_w, ada_b=ada_b, norm_g=norm_g, mla_w_in=mla_w_in, mla_q_norm=mla_q_norm,
             mla_w_uq=mla_w_uq, mla_kv_norm=mla_kv_norm, mla_w_ukv=mla_w_ukv, mla_w_out=mla_w_out, s5_w_in=s5_w_in,
             s5_a_re=s5_a_re, s5_a_im=s5_a_im, s5_log_step=s5_log_step, s5_b_re=s5_b_re, s5_b_im=s5_b_im,
             s5_c_re=s5_c_re, s5_c_im=s5_c_im, s5_d=s5_d, s5_w_glu=s5_w_glu, s5_b_glu=s5_b_glu, s5_w_out=s5_w_out,
             final_g=final_g)
    m = dict(c_ctx=m_c_ctx, ada_w=m_ada_w, ada_b=m_ada_b, norm_g=m_norm_g, mla_w_in=m_mla_w_in, mla_q_norm=m_mla_q_norm,
             mla_w_uq=m_mla_w_uq, mla_kv_norm=m_mla_kv_norm, mla_w_ukv=m_mla_w_ukv, mla_w_out=m_mla_w_out,
             s5_w_in=m_s5_w_in, s5_a_re=m_s5_a_re, s5_a_im=m_s5_a_im, s5_log_step=m_s5_log_step, s5_b_re=m_s5_b_re,
             s5_b_im=m_s5_b_im, s5_c_re=m_s5_c_re, s5_c_im=m_s5_c_im, s5_d=m_s5_d, s5_w_glu=m_s5_w_glu,
             s5_b_glu=m_s5_b_glu, s5_w_out=m_s5_w_out, final_g=m_final_g)
    v = dict(c_ctx=v_c_ctx, ada_w=v_ada_w, ada_b=v_ada_b, norm_g=v_norm_g, mla_w_in=v_mla_w_in, mla_q_norm=v_mla_q_norm,
             mla_w_uq=v_mla_w_uq, mla_kv_norm=v_mla_kv_norm, mla_w_ukv=v_mla_w_ukv, mla_w_out=v_mla_w_out,
             s5_w_in=v_s5_w_in, s5_a_re=v_s5_a_re, s5_a_im=v_s5_a_im, s5_log_step=v_s5_log_step, s5_b_re=v_s5_b_re,
             s5_b_im=v_s5_b_im, s5_c_re=v_s5_c_re, s5_c_im=v_s5_c_im, s5_d=v_s5_d, s5_w_glu=v_s5_w_glu,
             s5_b_glu=v_s5_b_glu, s5_w_out=v_s5_w_out, final_g=v_final_g)

    me = 4 * lax.axis_index("x") + 2 * lax.axis_index("y") + lax.axis_index("c")
    WA = ada_w.shape[2]

    def shard(n):
        return _t_shard(w[n], SHARD_ROWS[n]) if n in COL_SHARDED else w[n][0].astype(BF16)

    wgot = exchange([c] + [shard(n) for n in L0_BIG], "gather", "gather_w")

    cg = wgot[0].reshape(NDEV, D)
    cc2 = c_ctx.reshape(1, D)
    ada_b_loc = lax.dynamic_slice_in_dim(ada_b.reshape(2, 3 * D // WA, WA), me, 1, axis=1)
    part = ada_fwd(cg, cc2, ada_w, ada_b_loc, "ada_fwd")
    pg = exchange([part], "gather", "gather_mod")[0]
    mod_l = lax.dynamic_index_in_dim(pg, me, axis=2, keepdims=False).transpose(1, 0, 2).reshape(2, 3 * D)
    mod_c = pg[:, :, NDEV, :].transpose(1, 0, 2).reshape(2, 3 * D)
    mod = jnp.stack([mod_c, mod_l], axis=1)

    Wt = {n: a.reshape(-1, a.shape[-1]) for n, a in zip(L0_BIG, wgot[1:])}
    Wt["mla_w_in"] = mm(_win_order(), Wt["mla_w_in"], "nn", "w_in_order", out_dtype=BF16)
    vec_bits = lax.bitcast_convert_type(jnp.concatenate([s5_d, s5_b_glu], axis=0), BITS16).reshape(2, -1)
    small = {n: w[n] for n in SMALL_RS}

    lvec, grad_x, dmod, gbig, gsmall, l1_recv = local_step(ctx[0], x[0], loss_target[0], mod, Wt, small,
                                                           [shard(n) for n in L1_BIG] + [vec_bits])
    grad_x = grad_x[None]

    recv = dict(zip(L1_BIG + VEC_SHARDED, l1_recv))
    out = {}

    def keep(n, res):
        for key, arr in zip("gdmv", res):
            out[key, n] = arr.reshape(w[n].shape)

    reduced = sum_slots(l1_recv[len(L1_BIG + VEC_SHARDED):], "sum_chunks")

    kshape = lambda n: w[n].shape if w[n].ndim > 1 else (1, w[n].size)
    flat = jnp.concatenate([gsmall[n].reshape(-1) for n in TINY] + [dmod.reshape(-1), lvec.reshape(-1)])[None]
    *l0_recv, bb_all, cc_all, flat_all = exchange(
        [gbig[n] for n in L0_BIG] + [jnp.stack(reduced[:2]), jnp.stack(reduced[2:]), flat],
        ["lead"] * len(L0_BIG) + ["gather"] * 3, "scatter_grads")
    chunk_all = [bb_all[:, 0], bb_all[:, 1], cc_all[:, 0], cc_all[:, 1]]
    tiny_all, off = [], 0
    for n in TINY:
        tiny_all.append(flat_all[:, 0, off:off + w[n].size].reshape((NDEV,) + kshape(n)))
        off += w[n].size
    dm_all = flat_all[:, 0, off:off + dmod.size].reshape((NDEV,) + dmod.shape)
    loss = sum_slots([flat_all[:, :, off + dmod.size:]], "loss_sum")[0][0, 0]

    dm_cols = lax.dynamic_slice_in_dim(dm_all.reshape(NDEV, 2, 2, 3 * D // WA, WA), me, 1, axis=3)[:, :, :, 0]
    dm_loc = jnp.concatenate([dm_cols[:, :, 1].transpose(1, 0, 2), dm_cols[:, :, 0].transpose(1, 0, 2)], axis=1)
    g_ada_w, dcc_part, g_ada_b = ada_bwd(cg, cc2, ada_w, dm_loc, dm_all.transpose(0, 2, 1, 3).reshape(2 * NDEV, 2, 3 * D), "ada_bwd")
    dcc_all = exchange([dcc_part], "gather", "gather_dcc")[0].reshape(NDEV, D)
    g_c_ctx = cctx_finish(dcc_all, cc2, "cctx_finish")

    flat2 = lambda t: t.reshape(-1, t.shape[-1])
    recv.update(dict(zip(L0_BIG, l0_recv)))
    big = [(recv[n], w[n][0], m[n][0], v[n][0]) for n in BIG]
    big.append((flat2(g_ada_w)[None], flat2(ada_w), flat2(m_ada_w), flat2(v_ada_w)))
    for n, r in zip(BIG + ("ada_w",), adamw_rows(big, "adamw_big")[0]):
        keep(n, r)
    items = []
    halves = 2
    for n, g in zip(CHUNKED, chunk_all):
        blk = (1, 1, G // halves) + w[n].shape[3:]
        g = jnp.moveaxis(g, 0, 1).reshape(w[n].shape)
        g_spec = pl.BlockSpec((1,) + blk, lambda d, s: (0, 0, d, s, 0, 0))
        items.append((g[None], g_spec, w[n], m[n], v[n], pl.BlockSpec(blk, lambda d, s: (0, d, s, 0, 0))))
    for n, res in zip(CHUNKED, adamw_multi(items, (2, halves), "adamw_bc")):
        keep(n, res)
    tiny_g = dict(zip(TINY, tiny_all))
    tiny_g.update({n: recv[n] for n in VEC_SHARDED})
    tiny_g["c_ctx"], tiny_g["ada_b"] = g_c_ctx[None], g_ada_b[None]
    names = list(tiny_g)
    items = [(tiny_g[n], _whole(tiny_g[n], 1)) + tuple(t[n].reshape(kshape(n)) for t in (w, m, v))
             + (pl.BlockSpec(kshape(n), lambda i, r=len(kshape(n)): (0,) * r),) for n in names]
    for n, res in zip(names, adamw_multi(items, (1,), "adamw_small")):
        keep(n, res)

    return (loss, grad_x, *[out["g", n] for n in ORDER], *[out["d", n] for n in ORDER],
            *[out["m", n] for n in ORDER], *[out["v", n] for n in ORDER])
```

```python
import math

import numpy as np
import jax
import jax.numpy as jnp
from jax import lax
from jax.experimental import pallas as pl
from jax.experimental.pallas import tpu as pltpu

F32 = jnp.float32
BF16 = jnp.bfloat16

D = 1024
L = 2048
LC = 256
NDEV = 8
GRID_W = 64
EPS = 1e-6
HEADS = 16
NOPE = 64
ROPE = 32
QK = NOPE + ROPE
VD = 64
IN_W = 256 + 128 + ROPE + HEADS * 64
IN_WP = 1536
QL = 256
KVL = 128
SCALE = QK ** -0.5
LOG2E = math.log2(math.e)
THETA = 10000.0
G = 64
P = 64
CH = 16
GB = 8
NJ = G // GB
UB = GB * CH
SB = GB * P
SEG = 16
TB = 256
VMEM_LIMIT = 56 * 1024 * 1024
B1, B2, LR, AEPS, WD, STEP = 0.9, 0.999, 0.001, 1e-8, 0.01, 10
MESH_T = pl.DeviceIdType.MESH


def _cp(sem=None):
    return pltpu.CompilerParams(dimension_semantics=sem, vmem_limit_bytes=VMEM_LIMIT)


def _sig(x):
    return 1.0 / (1.0 + jnp.exp(-x))


def _silu(x):
    return x * _sig(x)


def _dsilu(x):
    s = _sig(x)
    return s * (1.0 + x * (1.0 - s))


_GK = math.sqrt(2.0 / math.pi)


def _gelu(x):
    return 0.5 * x * (1.0 + jnp.tanh(_GK * (x + 0.044715 * x * x * x)))


def _dgelu(x):
    t = jnp.tanh(_GK * (x + 0.044715 * x * x * x))
    return 0.5 * (1.0 + t) + 0.5 * x * (1.0 - t * t) * _GK * (1.0 + 3 * 0.044715 * x * x)


def _rs(x):
    return lax.rsqrt(jnp.mean(x * x, axis=-1, keepdims=True) + EPS)


def _sum0(x):
    return jnp.sum(x, axis=0, keepdims=True)


def st_norm_mod(x, g, sc, sh):
    y = x * _rs(x) * g
    return (y * (1.0 + sc) + sh,), ()


def st_norm_mod_bwd(x, dh, dres, g, sc):
    r = _rs(x)
    xn = x * r
    y = xn * g
    dy = dh * (1.0 + sc)
    dxn = dy * g
    dx = r * (dxn - xn * jnp.mean(dxn * xn, axis=-1, keepdims=True))
    return (dres + dx,), (_sum0(dh), _sum0(dh * y), _sum0(dy * xn))


def st_rms(x, g):
    return (x * _rs(x) * g,), ()


def st_rms_bwd(x, dy, g):
    r = _rs(x)
    n = x * r
    dn = dy * g
    dx = r * (dn - n * jnp.mean(dn * n, axis=-1, keepdims=True))
    return (dx,), (_sum0(dy * n),)


def st_rms2(x1, x2, g1, g2):
    return st_rms(x1, g1)[0] + st_rms(x2, g2)[0], ()


def st_rms2_bwd(x1, dy1, x2, dy2, g1, g2):
    (d1,), (s1,) = st_rms_bwd(x1, dy1, g1)
    (d2,), (s2,) = st_rms_bwd(x2, dy2, g2)
    return (d1, d2), (s1, s2)


def st_gate_bwd(dog, o, z):
    return (dog * _silu(z), dog * o * _dsilu(z)), ()


def st_resid_bwd(dx, out, gt):
    return (dx * gt,), (_sum0(dx * out),)


def st_s5a(yssm, u, d):
    y = yssm + d * u
    return (y, _gelu(y)), ()


def st_s5b_bwd(dy3, y, gl, z, b):
    y1 = _gelu(y)
    s = _sig(gl + b)
    dy2 = dy3 * _silu(z)
    dz = dy3 * y1 * s * _dsilu(z)
    dgl = dy2 * y1 * s * (1.0 - s)
    return (dgl, dz, dy2 * s), (_sum0(dgl),)


def st_s5a_bwd(dy1a, dy1b, y, u, d):
    dy = (dy1a + dy1b) * _dgelu(y)
    return (dy, dy * d), (_sum0(dy * u),)


def st_l0_pre(x, g, sc, sh, qg, kvg, w_in):
    hb = st_norm_mod(x, g, sc, sh)[0][0].astype(BF16)
    p = lax.dot_general(hb, w_in, _DN["nt"], preferred_element_type=F32)
    cq, ckv = p[:, HEADS * VD:HEADS * VD + QL], p[:, HEADS * VD + QL:HEADS * VD + QL + KVL]
    return (hb, p) + st_rms2(cq, ckv, qg, kvg)[0], ()


def st_l0_tail_bwd(dq, dkv, dkr, dz, cq, ckv, cqn, ckvn, h, x, dres, qg, kvg, g, sc, w_uq, w_ukv, w_in):
    dcqn = jnp.dot(dq, w_uq, preferred_element_type=F32)
    dckvn = jnp.dot(dkv, w_ukv, preferred_element_type=F32)
    (dcq, dckv), (dqg, dkvg) = st_rms2_bwd(cq, dcqn, ckv, dckvn, qg, kvg)
    dp = jnp.concatenate([dz, dcq, dckv, dkr], axis=1).astype(BF16)
    dh = jnp.dot(dp, w_in, preferred_element_type=F32)
    outs, sums = st_norm_mod_bwd(x, dh, dres, g, sc)
    tn = lambda a, b: lax.dot_general(a, b, _DN["tn"], preferred_element_type=F32)
    return outs, (dqg, dkvg) + sums, (tn(cqn, dq), tn(ckvn, dkv), tn(h, dp))


def st_l1_pre(x, g, sc, sh, w_in):
    hb = st_norm_mod(x, g, sc, sh)[0][0].astype(BF16)
    return (hb, lax.dot_general(hb, w_in, _DN["nt"], preferred_element_type=F32)), ()


def st_l1_tail_bwd(du_a, du_b, dz, h, x, dres, g, sc, w_in):
    dp = jnp.concatenate([(du_a + du_b).astype(BF16), dz], axis=1)
    dh = jnp.dot(dp, w_in, preferred_element_type=F32)
    outs, sums = st_norm_mod_bwd(x, dh, dres, g, sc)
    w = dp.shape[1] // NDEV
    shards = [lax.dot_general(h, dp[:, r * w:(r + 1) * w], _DN["tn"], preferred_element_type=F32) for r in range(NDEV)]
    return outs, sums, (jnp.stack(shards),)


def st_l0_post(o, z, x, gt, w_out):
    og = (o * _silu(z)).astype(BF16)
    out = jnp.dot(og, w_out, preferred_element_type=F32)
    return (og, out, x + gt * out), ()


def st_l0_post_bwd(dx1, out, og, o, z, gt, w_out):
    (dout,), (dgt,) = st_resid_bwd(dx1, out.astype(F32), gt)
    doutb = dout.astype(BF16)
    dog = lax.dot_general(doutb, w_out, _DN["nt"], preferred_element_type=F32)
    return st_gate_bwd(dog, o, z)[0], (dgt,), (lax.dot_general(og, doutb, _DN["tn"], preferred_element_type=F32),)


def st_l1_mlp(yssm, u, z, x1, tgt, d, bglu, gt, fg, mask, w_glu, w_out):
    (y, y1), _ = st_s5a(yssm, u, d)
    y1b = y1.astype(BF16)
    gl = jnp.dot(y1b, w_glu, preferred_element_type=F32)
    y3 = (y1 * _sig(gl + bglu) * _silu(z)).astype(BF16)
    out = jnp.dot(y3, w_out, preferred_element_type=F32)
    (dx2,), sums = st_final(x1 + gt * out, tgt, fg, mask)
    return (y, y1b, gl, y3, out, dx2), sums


def st_l1_mlp_bwd(dx2, out, y3, y, gl, z, u, y1b, gt, bglu, d, w_out, w_glu):
    out, gl = out.astype(F32), gl.astype(F32)
    (dout,), (dgt,) = st_resid_bwd(dx2, out, gt)
    doutb = dout.astype(BF16)
    dy3 = lax.dot_general(doutb, w_out, _DN["nt"], preferred_element_type=F32)
    (dgl, dz, dy1a), (dbglu,) = st_s5b_bwd(dy3, y, gl, z, bglu)
    dglb = dgl.astype(BF16)
    dy1b = lax.dot_general(dglb, w_glu, _DN["nt"], preferred_element_type=F32)
    (dy, du), (dd,) = st_s5a_bwd(dy1a, dy1b, y, u, d)
    g_w_out = lax.dot_general(y3, doutb, _DN["tn"], preferred_element_type=F32)
    g_w_glu = lax.dot_general(y1b, dglb, _DN["tn"], preferred_element_type=F32)
    return (dz, dy, du), (dgt, dbglu, dd), (g_w_out, g_w_glu)


def st_final(x2, tgt, g, mask):
    r = _rs(x2)
    n = x2 * r
    e = n * g - tgt
    dyo = e * (1.0 / D)
    dn = dyo * g
    dx = r * (dn - n * jnp.mean(dn * n, axis=-1, keepdims=True))
    lsum = jnp.sum(_sum0(e * e), axis=1, keepdims=True) * (0.5 / D)
    return (dx * mask,), (_sum0(dyo * n), jnp.broadcast_to(lsum, (1, 128)))


def rowwise(fn, rows, vecs, out_rows, out_sums, name, mats=(), out_accs=()):
    lat_blk = lambda i: jnp.maximum(i - 1, 0)
    arrays, in_specs, pick = [], [], []
    for a in rows:
        if not isinstance(a, tuple):
            a = (a, 0, a.shape[1])
        tag = a[0] if isinstance(a[0], str) else None
        if tag == "cat":
            _, ctx, x = a
            arrays += [ctx, x]
            in_specs += [pl.BlockSpec((TB, ctx.shape[1]), lambda i: (0, 0)),
                         pl.BlockSpec((TB, x.shape[1]), lambda i: (lat_blk(i), 0))]
            pick.append(2)
        elif tag == "lat":
            arrays.append(a[1])
            in_specs.append(pl.BlockSpec((TB, a[1].shape[1]), lambda i: (lat_blk(i), 0)))
            pick.append(1)
        else:
            arr, cb, width = a
            arrays.append(arr)
            in_specs.append(pl.BlockSpec((TB, width), lambda i, cb=cb: (i, cb)))
            pick.append(1)
    T = LC + L
    nin, nv, nm, no, ns = len(arrays), len(vecs), len(mats), len(out_rows), len(out_sums)

    def body(*refs):
        i = pl.program_id(0)
        vals, k = [], 0
        for p in pick:
            if p == 2:
                vals.append(jnp.where(i == 0, refs[k][...], refs[k + 1][...]))
            else:
                vals.append(refs[k][...])
            k += p
        vals += [r[0] for r in refs[nin:nin + nv]] + [r[...] for r in refs[nin + nv:nin + nv + nm]]
        res = fn(*vals)
        first_out = nin + nv + nm
        for r, o in zip(refs[first_out:first_out + no], res[0]):
            r[...] = o.astype(r.dtype)
        sum_refs = refs[first_out + no:first_out + no + ns]
        if sum_refs:
            @pl.when(i <= 1)
            def _():
                for r in sum_refs:
                    r[...] = jnp.zeros_like(r)
            for r, s in zip(sum_refs, res[1]):
                r[0] += s
        na = len(out_accs)
        if na:
            acc_out, acc = refs[first_out + no + ns:first_out + no + ns + na], refs[first_out + no + ns + na:]

            @pl.when(i == 0)
            def _():
                for r in acc:
                    r[...] = jnp.zeros_like(r)
            for r, a in zip(acc, res[2]):
                r[...] += a

            @pl.when(i == T // TB - 1)
            def _():
                for o, r in zip(acc_out, acc):
                    o[...] = r[...].astype(o.dtype)

    kind = lambda i: (jnp.minimum(i, 1), 0, 0)
    in_specs += [pl.BlockSpec((1, 1, v.shape[2]), kind) for v in vecs]
    in_specs += [pl.BlockSpec(m.shape, lambda i: (0, 0), pipeline_mode=pl.Buffered(1)) for m in mats]
    out_specs, out_shape = [], []
    for o in out_rows:
        lat = len(o) == 3
        out_specs.append(pl.BlockSpec((TB, o[0]), (lambda i: (lat_blk(i), 0)) if lat else (lambda i: (i, 0))))
        out_shape.append(jax.ShapeDtypeStruct((L if lat else T, o[0]), o[1]))
    out_specs += [pl.BlockSpec((1, 1, c), kind) for c in out_sums]
    out_shape += [jax.ShapeDtypeStruct((2, 1, c), F32) for c in out_sums]
    out_specs += [pl.BlockSpec(s, lambda i, r=len(s): (0,) * r) for s in out_accs]
    out_shape += [jax.ShapeDtypeStruct(s, BF16) for s in out_accs]
    res = pl.pallas_call(body, grid=(T // TB,), in_specs=in_specs, out_specs=out_specs, out_shape=out_shape,
                         scratch_shapes=[pltpu.VMEM(s, F32) for s in out_accs],
                         compiler_params=_cp(("arbitrary",)), name=name)(*arrays, *vecs, *mats)
    if out_accs:
        return res[:no], res[no:no + ns], res[no + ns:]
    return res[:no], res[no:]


_DN = {"nn": (((1,), (0,)), ((), ())), "nt": (((1,), (1,)), ((), ())), "tn": (((0,), (0,)), ((), ()))}


def mm(a, b, mode, name, out_dtype=F32, tm=None, tn=None):
    if mode == "nn":
        (M, K), (_, N) = a.shape, b.shape
    elif mode == "nt":
        (M, K), (N, _) = a.shape, b.shape
    else:
        (K, M), (_, N) = a.shape, b.shape
    if tm is None:
        tm = next((t for t in (768, 512, 256) if M % t == 0 and M > t), M)
    tn = N if tn is None else tn
    dn = _DN[mode]

    def body(a_ref, b_ref, o_ref):
        o_ref[...] = lax.dot_general(a_ref[...].astype(BF16), b_ref[...].astype(BF16), dn,
                                     preferred_element_type=F32).astype(o_ref.dtype)

    a_spec = pl.BlockSpec((K, tm), lambda i, j: (0, i)) if mode == "tn" else pl.BlockSpec((tm, K), lambda i, j: (i, 0))
    b_spec = pl.BlockSpec((tn, K), lambda i, j: (j, 0)) if mode == "nt" else pl.BlockSpec((K, tn), lambda i, j: (0, j))
    return pl.pallas_call(body, grid=(M // tm, N // tn), in_specs=[a_spec, b_spec],
                          out_specs=pl.BlockSpec((tm, tn), lambda i, j: (i, j)), out_shape=jax.ShapeDtypeStruct((M, N), out_dtype),
                          compiler_params=_cp(("parallel", "arbitrary")), name=name)(a, b)


def _rope_tables(T, width=QK, first=NOPE):
    nlat = T - LC
    pos = np.arange(nlat)
    row, col = pos // GRID_W, pos % GRID_W
    half = ROPE // 2
    inv = 1.0 / (THETA ** (np.arange(0, half, 2, dtype=np.float64) / half))
    cosf = np.ones((T, width), np.float64)
    sinf = np.zeros((T, width), np.float64)
    perm = np.zeros((width, width), np.float32)
    for m in range(ROPE):
        j = first + m
        blk, w = m // half, m % half
        ang = (row if blk == 0 else col)[:, None] * inv[None, :]
        f = w % (half // 2)
        cosf[LC:, j] = np.cos(ang[:, f])
        if w < half // 2:
            sinf[LC:, j] = -np.sin(ang[:, f])
            perm[j + half // 2, j] = 1.0
        else:
            sinf[LC:, j] = np.sin(ang[:, f])
            perm[j - half // 2, j] = 1.0
    return jnp.asarray(cosf, F32), jnp.asarray(sinf, F32), jnp.asarray(perm, BF16), jnp.asarray(perm.T, BF16)


def _exact_perm(x, pm):
    hi = x.astype(BF16)
    r1 = x - hi.astype(F32)
    mid = r1.astype(BF16)
    lo = (r1 - mid.astype(F32)).astype(BF16)
    dot = lambda a: jnp.dot(a, pm, preferred_element_type=F32)
    return dot(hi) + dot(mid) + dot(lo)


def _rot(x, cv, sv, pv, inverse):
    if inverse:
        return x * cv + _exact_perm(x * sv, pv)
    return x * cv + _exact_perm(x, pv) * sv


def rope(x, cosf, sinf, pm, inverse, out_dtype, name, scale=1.0):
    H, T, _ = x.shape

    def body(x_ref, c_ref, s_ref, p_ref, o_ref):
        cv, sv, pv = c_ref[...], s_ref[...], p_ref[...]
        for h in range(H):
            o_ref[h] = (_rot(x_ref[h], cv, sv, pv, inverse) * scale).astype(o_ref.dtype)

    return pl.pallas_call(
        body, grid=(T // TB,),
        in_specs=[pl.BlockSpec((H, TB, QK), lambda i: (0, i, 0)), pl.BlockSpec((TB, QK), lambda i: (i, 0)),
                  pl.BlockSpec((TB, QK), lambda i: (i, 0)), pl.BlockSpec((QK, QK), lambda i: (0, 0))],
        out_specs=pl.BlockSpec((H, TB, QK), lambda i: (0, i, 0)), out_shape=jax.ShapeDtypeStruct((H, T, QK), out_dtype),
        compiler_params=_cp(("parallel",)), name=name)(x, cosf, sinf, pm)


KVW = NOPE + VD


def _kv_selectors():
    s_kn = np.zeros((KVW, QK), np.float32)
    s_kr = np.zeros((128, QK), np.float32)
    s_v = np.zeros((KVW, VD), np.float32)
    for l in range(NOPE):
        s_kn[l, l] = 1.0
    for l in range(ROPE):
        s_kr[l, NOPE + l] = 1.0
    for l in range(VD):
        s_v[NOPE + l, l] = 1.0
    return s_kn, s_kr, s_v


def project_q(cqn, w, cosf, sinf, pm, name):
    T = cqn.shape[0]

    def body(a_ref, w_ref, c_ref, s_ref, p_ref, o_ref):
        a, cv, sv, pv = a_ref[...], c_ref[...], s_ref[...], p_ref[...]
        for h in range(HEADS):
            qh = _dotf(a, w_ref[pl.ds(h * QK, QK), :], "nt")
            o_ref[h] = (_rot(qh, cv, sv, pv, False) * (SCALE * LOG2E)).astype(BF16)

    rows = lambda c: pl.BlockSpec((TB, c), lambda i: (i, 0))
    const = lambda x: pl.BlockSpec(x.shape, lambda i: (0, 0))
    return pl.pallas_call(
        body, grid=(T // TB,), in_specs=[rows(QL), const(w), rows(QK), rows(QK), const(pm)],
        out_specs=pl.BlockSpec((HEADS, TB, QK), lambda i: (0, i, 0)), out_shape=jax.ShapeDtypeStruct((HEADS, T, QK), BF16),
        compiler_params=_cp(("parallel",)), name=name)(cqn, w, cosf, sinf, pm)


def project_kv(ckvn, w, p0, kr_block, name):
    T = ckvn.shape[0]
    cosf, sinf, pm, _ = _rope_tables(T, 128, 0)
    s_kn, s_kr, s_v = (jnp.asarray(s, BF16) for s in _kv_selectors())

    def body(a_ref, w_ref, kr_ref, c_ref, s_ref, p_ref, skn_ref, skr_ref, sv_ref, k_ref, v_ref):
        a = a_ref[...]
        krr = _rot(kr_ref[...], c_ref[...], s_ref[...], p_ref[...], False).astype(BF16)
        kr_part = jnp.dot(krr, skr_ref[...], preferred_element_type=F32)
        for h in range(HEADS):
            kvb = _dotf(a, w_ref[pl.ds(h * KVW, KVW), :], "nt").astype(BF16)
            k_ref[h] = (jnp.dot(kvb, skn_ref[...], preferred_element_type=F32) + kr_part).astype(BF16)
            v_ref[h] = jnp.dot(kvb, sv_ref[...], preferred_element_type=F32).astype(BF16)

    rows = lambda c: pl.BlockSpec((TB, c), lambda i: (i, 0))
    const = lambda x: pl.BlockSpec(x.shape, lambda i: (0, 0))
    return pl.pallas_call(
        body, grid=(T // TB,),
        in_specs=[rows(KVL), const(w), pl.BlockSpec((TB, 128), lambda i: (i, kr_block)),
                  rows(128), rows(128), const(pm), const(s_kn), const(s_kr), const(s_v)],
        out_specs=[pl.BlockSpec((HEADS, TB, QK), lambda i: (0, i, 0)), pl.BlockSpec((HEADS, TB, VD), lambda i: (0, i, 0))],
        out_shape=[jax.ShapeDtypeStruct((HEADS, T, QK), BF16), jax.ShapeDtypeStruct((HEADS, T, VD), BF16)],
        compiler_params=_cp(("parallel",)), name=name)(ckvn, w, p0, cosf, sinf, pm, s_kn, s_kr, s_v)


def split_kv_grads(dk, dv, name, rode=None, modes=None):
    H, T, _ = dk.shape
    cosf, sinf, _, pmt = _rope_tables(T, 128, 0)
    s_kn, s_kr, s_v = _kv_selectors()
    s_knt, s_krt, s_vt = (jnp.asarray(s.T, BF16) for s in (s_kn, s_kr, s_v))

    def body(dk_ref, dv_ref, c_ref, s_ref, p_ref, skn_ref, skr_ref, sv_ref, dkv_ref, dkr_ref):
        total = None
        for h in range(H):
            dkh = dk_ref[h] * (1.0 / LOG2E)
            total = dkh if total is None else total + dkh
            dkv_ref[:, pl.ds(h * KVW, KVW)] = (
                jnp.dot(dkh.astype(BF16), skn_ref[...], preferred_element_type=F32)
                + jnp.dot(dv_ref[h].astype(BF16), sv_ref[...], preferred_element_type=F32)).astype(BF16)
        dkr_ref[...] = _rot(_exact_perm(total, skr_ref[...]), c_ref[...], s_ref[...], p_ref[...], True)

    rows = lambda c: pl.BlockSpec((TB, c), lambda i, j: (i, 0))
    const = lambda a: pl.BlockSpec(a.shape, lambda i, j: (0, 0))
    return _ride_call(
        body, (T // TB, 1),
        [pl.BlockSpec((H, TB, QK), lambda i, j: (0, i, 0)), pl.BlockSpec((H, TB, VD), lambda i, j: (0, i, 0)),
         rows(128), rows(128), const(pmt), const(s_knt), const(s_krt), const(s_vt)],
        [rows(H * KVW), rows(128)],
        [jax.ShapeDtypeStruct((T, H * KVW), BF16), jax.ShapeDtypeStruct((T, 128), F32)],
        Exchange(rode, modes) if rode else None, rode, name, (dk, dv, cosf, sinf, pmt, s_knt, s_krt, s_vt))


HB = 4
HBF = 8


def _by_query_block(run, T):
    @pl.when(pl.program_id(1) == 0)
    def _():
        run(LC)

    @pl.when(pl.program_id(1) > 0)
    def _():
        run(T)


def _with_rider(body, nin, nout, ride, grid):
    if ride is None:
        return body
    n = ride.n

    def wrapped(*refs):
        ins, xs = refs[:nin], refs[nin:nin + n]
        outs, got = refs[nin + n:nin + n + nout], refs[nin + n + nout:nin + 2 * n + nout]
        sems = refs[nin + 2 * n + nout:]
        step = pl.program_id(0) * grid[1] + pl.program_id(1)

        @pl.when(step == 0)
        def _():
            ride.start(xs, got, sems)

        body(*ins, *outs)

        @pl.when(step == grid[0] * grid[1] - 1)
        def _():
            ride.finish(xs, got, sems)

    return wrapped


def _ride_call(body, grid, in_specs, out_specs, out_shape, ride, rode, name, args):
    if ride is None:
        return pl.pallas_call(body, grid=grid, in_specs=in_specs, out_specs=out_specs, out_shape=out_shape,
                              compiler_params=_cp(("parallel", "arbitrary")), name=name)(*args), []
    res = pl.pallas_call(
        _with_rider(body, len(in_specs), len(out_specs), ride, grid), grid=grid,
        in_specs=in_specs + ride.specs, out_specs=out_specs + ride.specs, out_shape=out_shape + ride.out_shape,
        scratch_shapes=ride.scratch,
        compiler_params=pltpu.CompilerParams(dimension_semantics=("arbitrary", "arbitrary"), vmem_limit_bytes=VMEM_LIMIT,
                                             has_side_effects=True), name=name)(*args, *rode)
    return res[:len(out_specs)], res[len(out_specs):]


def attn_fwd(q, k, v, name, rode=None, modes=None):
    H, T, _ = q.shape

    def body(q_ref, k_ref, v_ref, o_ref, lse_ref):
        def run(nk):
            for hh in range(HBF):
                s = _dotf(q_ref[hh], k_ref[hh, pl.ds(0, nk), :], "nt")
                m = jnp.max(s, axis=1, keepdims=True)
                p = jnp.exp2(s - m)
                l = jnp.sum(p, axis=1, keepdims=True)
                o = jnp.dot(p.astype(BF16), v_ref[hh, pl.ds(0, nk), :], preferred_element_type=F32)
                o_ref[hh] = o / l
                lse_ref[hh] = m + jnp.log2(l)

        _by_query_block(run, T)

    return _ride_call(
        body, (H // HBF, T // TB),
        [pl.BlockSpec((HBF, TB, QK), lambda h, i: (h, i, 0)), pl.BlockSpec((HBF, T, QK), lambda h, i: (h, 0, 0)),
         pl.BlockSpec((HBF, T, VD), lambda h, i: (h, 0, 0))],
        [pl.BlockSpec((HBF, TB, VD), lambda h, i: (h, i, 0)), pl.BlockSpec((HBF, TB, 1), lambda h, i: (h, i, 0))],
        [jax.ShapeDtypeStruct((H, T, VD), F32), jax.ShapeDtypeStruct((H, T, 1), F32)],
        Exchange(rode, modes) if rode else None, rode, name, (q, k, v))


def attn_bwd(q, k, v, o, lse, do, name, rode=None, modes=None):
    H, T, _ = q.shape

    def body(q_ref, k_ref, v_ref, o_ref, lse_ref, do_ref, dq_ref, dk_ref, dv_ref):
        i = pl.program_id(1)

        @pl.when(i == 0)
        def _():
            dk_ref[...] = jnp.zeros_like(dk_ref)
            dv_ref[...] = jnp.zeros_like(dv_ref)

        def run(nk):
            keys = pl.ds(0, nk)
            for hh in range(HB):
                qv, kv, dov = q_ref[hh], k_ref[hh, keys, :], do_ref[hh]
                p = jnp.exp2(_dotf(qv, kv, "nt") - lse_ref[hh])
                delta = jnp.sum(dov * o_ref[hh], axis=1, keepdims=True)
                dob = dov.astype(BF16)
                dv_ref[hh, keys, :] += _dotf(p.astype(BF16), dob, "tn")
                dp = _dotf(dob, v_ref[hh, keys, :], "nt")
                ds = (p * (dp - delta)).astype(BF16)
                dq_ref[hh] = jnp.dot(ds, kv, preferred_element_type=F32)
                dk_ref[hh, keys, :] += _dotf(ds, qv, "tn")

        _by_query_block(run, T)

    blk = lambda c: pl.BlockSpec((HB, TB, c), lambda h, i: (h, i, 0))
    full = lambda c: pl.BlockSpec((HB, T, c), lambda h, i: (h, 0, 0))
    return _ride_call(
        body, (H // HB, T // TB), [blk(QK), full(QK), full(VD), blk(VD), blk(1), blk(VD)], [blk(QK), full(QK), full(VD)],
        [jax.ShapeDtypeStruct((H, T, QK), F32), jax.ShapeDtypeStruct((H, T, QK), F32), jax.ShapeDtypeStruct((H, T, VD), F32)],
        Exchange(rode, modes) if rode else None, rode, name, (q, k, v, o, lse, do))


def disc_fwd(a_re, a_im, ls, name):
    def body(ar_ref, ai_ref, ls_ref, lr_ref, li_ref, fr_ref, fi_ref):
        ar, ai = ar_ref[...], ai_ref[...]
        dt = jnp.exp(ls_ref[...])
        mag = jnp.exp(ar * dt)
        lr = mag * jnp.cos(ai * dt)
        li = mag * jnp.sin(ai * dt)
        den = ar * ar + ai * ai
        nr = lr - 1.0
        lr_ref[...] = lr
        li_ref[...] = li
        fr_ref[...] = (nr * ar + li * ai) / den
        fi_ref[...] = (li * ar - nr * ai) / den

    return pl.pallas_call(body, out_shape=[jax.ShapeDtypeStruct(a_re.shape, F32)] * 4, name=name)(a_re, a_im, ls)


def disc_b(f_re, f_im, b_re, b_im, name):
    def body(fr_ref, fi_ref, br_ref, bi_ref, or_ref, oi_ref):
        fr, fi, br, bi = fr_ref[...], fi_ref[...], br_ref[...], bi_ref[...]
        or_ref[...] = fr * br - fi * bi
        oi_ref[...] = fr * bi + fi * br

    fs, bs = _disc_b_specs()
    return pl.pallas_call(body, grid=(2, G * P // DISC_ROWS), in_specs=[fs, fs, bs, bs], out_specs=[bs, bs],
                          out_shape=[jax.ShapeDtypeStruct(b_re.shape, F32)] * 2, name=name)(f_re, f_im, b_re, b_im)


DISC_ROWS = G * P


def _disc_b_specs():
    return (pl.BlockSpec((1, DISC_ROWS, 1), lambda d, i: (d, i, 0)), pl.BlockSpec((1, DISC_ROWS, CH), lambda d, i: (d, i, 0)))


def disc_b_bwd(f_re, f_im, b_re, b_im, dbb_re, dbb_im, name):
    def body(fr_ref, fi_ref, br_ref, bi_ref, dr_ref, di_ref, dbr_ref, dbi_ref, dfr_ref, dfi_ref):
        fr, fi, br, bi, dr, di = fr_ref[...], fi_ref[...], br_ref[...], bi_ref[...], dr_ref[...], di_ref[...]
        dbr_ref[...] = fr * dr + fi * di
        dbi_ref[...] = fr * di - fi * dr
        dfr_ref[...] = jnp.sum(dr * br + di * bi, axis=-1, keepdims=True)
        dfi_ref[...] = jnp.sum(di * br - dr * bi, axis=-1, keepdims=True)

    fs, bs = _disc_b_specs()
    return pl.pallas_call(body, grid=(2, G * P // DISC_ROWS), in_specs=[fs, fs, bs, bs, bs, bs], out_specs=[bs, bs, fs, fs],
                          out_shape=[jax.ShapeDtypeStruct(b_re.shape, F32)] * 2 + [jax.ShapeDtypeStruct(f_re.shape, F32)] * 2,
                          name=name)(f_re, f_im, b_re, b_im, dbb_re, dbb_im)


def disc_a_bwd(a_re, a_im, ls, dlr, dli, dfr, dfi, name):
    def body(ar_ref, ai_ref, ls_ref, dlr_ref, dli_ref, dfr_ref, dfi_ref, dar_ref, dai_ref, dls_ref):
        ar, ai = ar_ref[...], ai_ref[...]
        dt = jnp.exp(ls_ref[...])
        mag = jnp.exp(ar * dt)
        cs, sn = jnp.cos(ai * dt), jnp.sin(ai * dt)
        lr, li = mag * cs, mag * sn
        den = ar * ar + ai * ai
        nr = lr - 1.0
        f_re = (nr * ar + li * ai) / den
        f_im = (li * ar - nr * ai) / den
        dn1 = dfr_ref[...] / den
        dn2 = dfi_ref[...] / den
        dden = -(dfr_ref[...] * f_re + dfi_ref[...] * f_im) / den
        dlr_t = dlr_ref[...] + dn1 * ar - dn2 * ai
        dli_t = dli_ref[...] + dn1 * ai + dn2 * ar
        dar = dn1 * nr + dn2 * li + dden * 2.0 * ar
        dai = dn1 * li - dn2 * nr + dden * 2.0 * ai
        dmag = dlr_t * cs + dli_t * sn
        dth = dli_t * lr - dlr_t * li
        dar_ref[...] = dar + dmag * mag * dt
        dai_ref[...] = dai + dth * dt
        dls_ref[...] = jnp.sum(dmag * mag * ar + dth * ai, axis=-1, keepdims=True) * dt

    return pl.pallas_call(body, out_shape=[jax.ShapeDtypeStruct(a_re.shape, F32)] * 2 +
                          [jax.ShapeDtypeStruct(ls.shape, F32)], name=name)(a_re, a_im, ls, dlr, dli, dfr, dfi)


def _cpow(lr, li, n):
    rr, ri = None, None
    br, bi = lr, li
    while n:
        if n & 1:
            if rr is None:
                rr, ri = br, bi
            else:
                rr, ri = rr * br - ri * bi, rr * bi + ri * br
        n >>= 1
        if n:
            br, bi = br * br - bi * bi, 2.0 * br * bi
    return rr, ri


UNROLL = 4


def _steps(trips, fn, init):
    main = trips // UNROLL

    def body(i, c):
        for j in range(UNROLL):
            c = fn(i * UNROLL + j, c)
        return c

    c = lax.fori_loop(0, main, body, init) if main else init
    for n in range(main * UNROLL, trips):
        c = fn(n, c)
    return c


def _seg_scan(xre, xim, lam8, pw, base, seglen, rev, init, fin_re, fin_im, ini_re, ini_im, prev=None):
    lr, li = lam8
    nsub = SEG // 8

    def rows(t, s):
        first = base + t * SEG + 8 * s
        return pl.ds(first if isinstance(first, int) else pl.multiple_of(first, 8), 8)

    tmap = (lambda n: seglen - 1 - n) if rev else (lambda n: n)
    zeros = tuple(jnp.zeros((8, SB), F32) for _ in range(2 * nsub))

    def advance(c, t):
        out = []
        for s in range(nsub):
            a, b = c[2 * s], c[2 * s + 1]
            out += [lr * a - li * b + xre[rows(t, s), :], lr * b + li * a + xim[rows(t, s), :]]
        return tuple(out)

    fin = _steps(seglen, lambda n, c: advance(c, tmap(n)), zeros)
    for s in range(nsub):
        fin_re[pl.ds(8 * s, 8), :] = fin[2 * s]
        fin_im[pl.ds(8 * s, 8), :] = fin[2 * s + 1]
    (cr, ci), (pr, pi) = init, pw
    for i in (range(SEG - 1, -1, -1) if rev else range(SEG)):
        ini_re[pl.ds(i, 1), :] = cr
        ini_im[pl.ds(i, 1), :] = ci
        cr, ci = pr * cr - pi * ci + fin_re[pl.ds(i, 1), :], pr * ci + pi * cr + fin_im[pl.ds(i, 1), :]
    tiles = lambda re, im: tuple(r[pl.ds(8 * s, 8), :] for s in range(nsub) for r in (re, im))
    start = tiles(ini_re, ini_im)

    def store(c, t):
        new = advance(c, t)
        for s in range(nsub):
            xre[rows(t, s), :] = new[2 * s]
            xim[rows(t, s), :] = new[2 * s + 1]
        return new

    if prev is None:
        _steps(seglen, lambda n, c: store(c, tmap(n)), start)
        return (cr, ci), None

    sre, sim, s_ini_re, s_ini_im = prev

    def acc_step(c, t, before):
        new = store(c[:2 * nsub], t)
        acc = []
        for s in range(nsub):
            (na, nb), (pre, pim) = new[2 * s:2 * s + 2], before[2 * s:2 * s + 2]
            acc += [c[2 * nsub + 2 * s] + na * pre + nb * pim, c[2 * nsub + 2 * s + 1] + nb * pre - na * pim]
        return new + tuple(acc)

    def body(n, c):
        t = tmap(n)
        tp = t - 1 if rev else t + 1
        return acc_step(c, t, tuple(r[rows(tp, s), :] for s in range(nsub) for r in (sre, sim)))

    c = _steps(seglen - 1, body, start + zeros)
    c = acc_step(c, 0 if rev else seglen - 1, tiles(s_ini_re, s_ini_im))
    acc = c[2 * nsub:]
    return (cr, ci), (sum(acc[0::2][1:], acc[0]), sum(acc[1::2][1:], acc[1]))


def _lam_tiles(lr, li, lens, conj=False):
    if conj:
        li = -li
    lam8 = (jnp.broadcast_to(lr, (8, SB)), jnp.broadcast_to(li, (8, SB)))
    return lam8, [_cpow(lr, li, n) for n in lens]


def _stretches(T):
    return ((0, LC // SEG), (LC, (T - LC) // SEG))


def _to_seg_order(src, dst, T):
    for base, seglen in _stretches(T):
        def body(t, carry, base=base, seglen=seglen):
            dst[pl.ds(pl.multiple_of(base + t * SEG, SEG), SEG), :] = src[pl.ds(base + t, SEG, stride=seglen), :]
            return carry
        lax.fori_loop(0, seglen, body, 0, unroll=8)


def _from_seg_order(src, dst, T):
    for base, seglen in _stretches(T):
        def body(t, carry, base=base, seglen=seglen):
            dst[pl.ds(base + t, SEG, stride=seglen), :] = src[pl.ds(pl.multiple_of(base + t * SEG, SEG), SEG), :]
            return carry
        lax.fori_loop(0, seglen, body, 0, unroll=8)


def _scan_specs(T):
    ublk = pl.BlockSpec((T, UB), lambda j: (0, j))
    lam = pl.BlockSpec((2, 1, 1, SB), lambda j: (0, j, 0, 0))
    mat = pl.BlockSpec((2, 1, UB, P), lambda j: (0, j, 0, 0))
    return ublk, lam, mat


def _dotf(a, b, mode="nn"):
    return lax.dot_general(a, b, _DN[mode], preferred_element_type=F32)


def _diag_mask():
    r = lax.broadcasted_iota(jnp.int32, (UB, SB), 0)
    c = lax.broadcasted_iota(jnp.int32, (UB, SB), 1)
    return lax.shift_right_logical(r, int(math.log2(CH))) == lax.shift_right_logical(c, int(math.log2(P)))


def _expand(m):
    p = lax.broadcasted_iota(jnp.int32, (P, SB), 0)
    c = lax.broadcasted_iota(jnp.int32, (P, SB), 1)
    tile = jnp.where(lax.bitwise_and(c, P - 1) == p, 1.0, 0.0).astype(BF16)
    wide = jnp.dot(m.astype(BF16), tile, preferred_element_type=F32)
    return jnp.where(_diag_mask(), wide, 0.0).astype(BF16)


def _collapse(full):
    c = lax.broadcasted_iota(jnp.int32, (SB, P), 0)
    p = lax.broadcasted_iota(jnp.int32, (SB, P), 1)
    pick = jnp.where(lax.bitwise_and(c, P - 1) == p, 1.0, 0.0).astype(BF16)
    return _exact_perm(jnp.where(_diag_mask(), full, 0.0), pick)


def _zero_state():
    return jnp.zeros((1, SB), F32), jnp.zeros((1, SB), F32)


def scan_fwd(u, lam_re, lam_im, bre, bim, cre, cim, name):
    T = u.shape[0]
    s_ctx, s_lat = LC // SEG, (T - LC) // SEG

    def body(u_ref, lr_ref, li_ref, bre_ref, bim_ref, cre_ref, cim_ref, y_ref, us, ys, sre, sim, fre, fim, ire, iim):
        _to_seg_order(u_ref, us, T)
        ub = us[...].astype(BF16)
        for d in range(2):
            lam8, (pw_c, pw_l) = _lam_tiles(lr_ref[d, 0], li_ref[d, 0], (s_ctx, s_lat))
            sre[...] = _dotf(ub, _expand(bre_ref[d, 0]))
            sim[...] = _dotf(ub, _expand(bim_ref[d, 0]))
            end_c, _ = _seg_scan(sre, sim, lam8, pw_c, 0, s_ctx, bool(d), _zero_state(), fre, fim, ire, iim)
            _seg_scan(sre, sim, lam8, pw_l, LC, s_lat, bool(d), end_c, fre, fim, ire, iim)
            y = (_dotf(sre[...].astype(BF16), _expand(cre_ref[d, 0]), "nt")
                 - _dotf(sim[...].astype(BF16), _expand(cim_ref[d, 0]), "nt"))
            if d == 0:
                ys[...] = y
            else:
                ys[...] += y
        _from_seg_order(ys, y_ref, T)

    ublk, lam, mat = _scan_specs(T)
    return pl.pallas_call(
        body, grid=(NJ,), in_specs=[ublk, lam, lam, mat, mat, mat, mat], out_specs=ublk,
        out_shape=jax.ShapeDtypeStruct((T, G * CH), F32),
        scratch_shapes=[pltpu.VMEM((T, UB), F32)] * 2 + [pltpu.VMEM((T, SB), F32)] * 2 + [pltpu.VMEM((SEG, SB), F32)] * 4,
        compiler_params=_cp(("arbitrary",)), name=name)(u, lam_re, lam_im, bre, bim, cre, cim)


def scan_bwd(u, dy, lam_re, lam_im, bre, bim, cre, cim, name):
    T = u.shape[0]
    s_ctx, s_lat = LC // SEG, (T - LC) // SEG

    def body(u_ref, dy_ref, lr_ref, li_ref, bre_ref, bim_ref, cre_ref, cim_ref,
             du_ref, dlr_ref, dli_ref, dbre_ref, dbim_ref, dcre_ref, dcim_ref,
             us, dys, dus, sre, sim, gre, gim, fre, fim, ic_re, ic_im, il_re, il_im, jre, jim):
        _to_seg_order(u_ref, us, T)
        _to_seg_order(dy_ref, dys, T)
        ub, dyb = us[...].astype(BF16), dys[...].astype(BF16)
        for d in range(2):
            rev = bool(d)
            lam8, (pw_c, pw_l) = _lam_tiles(lr_ref[d, 0], li_ref[d, 0], (s_ctx, s_lat))
            cam8, (cw_c, cw_l) = _lam_tiles(lr_ref[d, 0], li_ref[d, 0], (s_ctx, s_lat), conj=True)
            bre_v, bim_v = _expand(bre_ref[d, 0]), _expand(bim_ref[d, 0])
            sre[...] = _dotf(ub, bre_v)
            sim[...] = _dotf(ub, bim_v)
            end_c, _ = _seg_scan(sre, sim, lam8, pw_c, 0, s_ctx, rev, _zero_state(), fre, fim, ic_re, ic_im)
            _seg_scan(sre, sim, lam8, pw_l, LC, s_lat, rev, end_c, fre, fim, il_re, il_im)
            gre[...] = _dotf(dyb, _expand(cre_ref[d, 0]))
            gim[...] = -_dotf(dyb, _expand(cim_ref[d, 0]))
            end_g, acc_l = _seg_scan(gre, gim, cam8, cw_l, LC, s_lat, not rev, _zero_state(), fre, fim, jre, jim,
                                     prev=(sre, sim, il_re, il_im))
            _, acc_c = _seg_scan(gre, gim, cam8, cw_c, 0, s_ctx, not rev, end_g, fre, fim, jre, jim,
                                 prev=(sre, sim, ic_re, ic_im))
            dlr_ref[d, 0] = _sum0(acc_l[0] + acc_c[0])
            dli_ref[d, 0] = _sum0(acc_l[1] + acc_c[1])
            grb, gib = gre[...].astype(BF16), gim[...].astype(BF16)
            du = _dotf(grb, bre_v, "nt") + _dotf(gib, bim_v, "nt")
            if d == 0:
                dus[...] = du
            else:
                dus[...] += du
            dbre_ref[d, 0] = _collapse(_dotf(ub, grb, "tn"))
            dbim_ref[d, 0] = _collapse(_dotf(ub, gib, "tn"))
            dcre_ref[d, 0] = _collapse(_dotf(dyb, sre[...].astype(BF16), "tn"))
            dcim_ref[d, 0] = -_collapse(_dotf(dyb, sim[...].astype(BF16), "tn"))
        _from_seg_order(dus, du_ref, T)

    ublk, lam, mat = _scan_specs(T)
    lam_s = jax.ShapeDtypeStruct(lam_re.shape, F32)
    mat_s = jax.ShapeDtypeStruct(bre.shape, F32)
    return pl.pallas_call(
        body, grid=(NJ,), in_specs=[ublk, ublk, lam, lam, mat, mat, mat, mat],
        out_specs=[ublk, lam, lam, mat, mat, mat, mat],
        out_shape=[jax.ShapeDtypeStruct((T, G * CH), F32), lam_s, lam_s, mat_s, mat_s, mat_s, mat_s],
        scratch_shapes=[pltpu.VMEM((T, UB), F32)] * 3 + [pltpu.VMEM((T, SB), F32)] * 4 + [pltpu.VMEM((SEG, SB), F32)] * 8,
        compiler_params=_cp(("arbitrary",)), name=name)(u, dy, lam_re, lam_im, bre, bim, cre, cim)


class Exchange:
    def __init__(self, xs, modes):
        self.n = len(xs)
        self.modes = [modes] * self.n if isinstance(modes, (str, int)) else list(modes)
        self.out_shape = [jax.ShapeDtypeStruct(self._shape(x, md), x.dtype) for x, md in zip(xs, self.modes)]
        self.scratch = [pltpu.SemaphoreType.DMA((NDEV - 1, self.n)), pltpu.SemaphoreType.DMA((NDEV - 1, self.n)),
                        pltpu.SemaphoreType.DMA((self.n,))]
        self.specs = [pl.BlockSpec(memory_space=pl.ANY)] * self.n

    @staticmethod
    def _shape(x, mode):
        if mode == "gather":
            return (NDEV,) + tuple(x.shape)
        return tuple(x.shape) if mode == "lead" else (NDEV, x.shape[0], mode) + tuple(x.shape[2:])

    @staticmethod
    def _piece(x_ref, mode, dev):
        if mode == "gather":
            return x_ref
        return x_ref.at[dev] if mode == "lead" else x_ref.at[:, pl.ds(dev * mode, mode)]

    def _copies(self, x_refs, out_refs, sems):
        send_sems, recv_sems, local_sems = sems
        mx, my, mc = lax.axis_index("x"), lax.axis_index("y"), lax.axis_index("c")
        me = 4 * mx + 2 * my + mc
        peer_of = lambda k: (1 - mx if k & 4 else mx, 1 - my if k & 2 else my, 1 - mc if k & 1 else mc)
        local, first, relay, arrivals = [], [], [], []
        for a, (x_ref, out_ref) in enumerate(zip(x_refs, out_refs)):
            mode = self.modes[a]
            local.append(pltpu.make_async_copy(self._piece(x_ref, mode, me), out_ref.at[me], local_sems.at[a]))

            def remote(src, dst, k, pair, a=a):
                return pltpu.make_async_remote_copy(src_ref=src, dst_ref=dst, send_sem=send_sems.at[pair, a],
                                                    recv_sem=recv_sems.at[pair, a], device_id=peer_of(k), device_id_type=MESH_T)

            for k in range(1, NDEV):
                peer = peer_of(k)
                pid = 4 * peer[0] + 2 * peer[1] + peer[2]
                if mode != "gather":
                    src = self._piece(x_ref, mode, pid)
                    first.append(remote(src, out_ref.at[me], k, k - 1))
                    arrivals.append(remote(src, out_ref.at[pid], k, k - 1))
                elif k == 1:
                    first.append(remote(x_ref, out_ref.at[me], k, k - 1))
                    arrivals.append(remote(x_ref, out_ref.at[pid], k, k - 1))
                elif k % 2 == 0:
                    first.append(remote(x_ref, out_ref.at[me], k, k - 1))
                    relay.append((remote(x_ref, out_ref.at[pid], k, k - 1), remote(out_ref.at[pid], out_ref.at[pid], 1, k)))
                else:
                    arrivals.append(remote(x_ref, out_ref.at[pid], 1, k - 1))
        return local, first, relay, arrivals

    def start(self, x_refs, out_refs, sems):
        local, first, _, _ = self._copies(x_refs, out_refs, sems)
        for cp in local + first:
            cp.start()

    def finish(self, x_refs, out_refs, sems):
        local, first, relay, arrivals = self._copies(x_refs, out_refs, sems)
        for arrival, onward in relay:
            arrival.wait_recv()
            onward.start()
        for cp in arrivals:
            cp.wait_recv()
        for cp in first + [onward for _, onward in relay]:
            cp.wait_send()
        for cp in local:
            cp.wait()


def exchange(xs, modes, name):
    ex = Exchange(xs, modes)
    n = ex.n

    def body(*refs):
        ex.start(refs[:n], refs[n:2 * n], refs[2 * n:])
        ex.finish(refs[:n], refs[n:2 * n], refs[2 * n:])

    return pl.pallas_call(body, in_specs=ex.specs, out_specs=ex.specs, out_shape=ex.out_shape, scratch_shapes=ex.scratch,
                          compiler_params=pltpu.CompilerParams(has_side_effects=True), name=name)(*xs)


def _dot_f32(a, b, dn):
    return lax.dot_general(a, b, dn, preferred_element_type=F32, precision=lax.Precision.HIGHEST)


def ada_fwd(cg, c_ctx, ada_w, ada_b_loc, name):
    W = ada_w.shape[2]

    def body(cg_ref, cc_ref, w_ref, b_ref, o_ref):
        a = jnp.concatenate([_silu(cg_ref[...]), jnp.broadcast_to(_silu(cc_ref[...]), (NDEV, D))], axis=0)
        for i in range(2):
            o_ref[i] = _dot_f32(a, w_ref[i], _DN["nn"]) + b_ref[i]

    return pl.pallas_call(body, out_shape=jax.ShapeDtypeStruct((2, 2 * NDEV, W), F32),
                          compiler_params=_cp(), name=name)(cg, c_ctx, ada_w, ada_b_loc)


def ada_bwd(cg, c_ctx, ada_w, dm_loc, dm_all, name):
    W = ada_w.shape[2]

    def body(cg_ref, cc_ref, w_ref, dl_ref, da_ref, gw_ref, dcc_ref, gb_ref):
        a = jnp.concatenate([_silu(cg_ref[...]), jnp.broadcast_to(_silu(cc_ref[...]), (NDEV, D))], axis=0)
        dcc = jnp.zeros((1, D), F32)
        for i in range(2):
            dl = dl_ref[i]
            gw_ref[i] = _dot_f32(a, dl, _DN["tn"])
            dctx = jnp.sum(dl[NDEV:], axis=0, keepdims=True)
            dcc = dcc + _dot_f32(dctx, w_ref[i], _DN["nt"])
        dcc_ref[...] = dcc
        gb_ref[...] = jnp.sum(da_ref[...], axis=0)

    return pl.pallas_call(body, out_shape=[jax.ShapeDtypeStruct((2, D, W), F32), jax.ShapeDtypeStruct((1, D), F32),
                                           jax.ShapeDtypeStruct((2, 3 * D), F32)],
                          compiler_params=_cp(), name=name)(cg, c_ctx, ada_w, dm_loc, dm_all)


def cctx_finish(parts, c_ctx, name):
    def body(p_ref, cc_ref, o_ref):
        o_ref[...] = jnp.sum(p_ref[...], axis=0, keepdims=True) * _dsilu(cc_ref[...])

    return pl.pallas_call(body, out_shape=jax.ShapeDtypeStruct((1, D), F32), name=name)(parts, c_ctx)


def _adamw_update(g_ref, w_ref, m_ref, v_ref, go_ref, d_ref, mo_ref, vo_ref):
    g = g_ref[0].astype(F32)
    for s in range(1, g_ref.shape[0]):
        g = g + g_ref[s].astype(F32)
    mn = B1 * m_ref[...] + (1.0 - B1) * g
    vn = B2 * v_ref[...] + (1.0 - B2) * g * g
    go_ref[...] = g
    mo_ref[...] = mn
    vo_ref[...] = vn
    d_ref[...] = -LR * ((mn * (1.0 / (1.0 - B1 ** STEP))) / (jnp.sqrt(vn * (1.0 / (1.0 - B2 ** STEP))) + AEPS) + WD * w_ref[...])


ADAMW_PARTS = 4


def adamw_rows(items, name, rode=None, modes=None):
    in_specs, out_specs, out_shape, args = [], [], [], []
    for g, w, m, v in items:
        n, R, C = g.shape
        tr = R // ADAMW_PARTS
        spec = pl.BlockSpec((tr, C), lambda i, j: (i, 0))
        in_specs += [pl.BlockSpec((n, tr, C), lambda i, j: (0, i, 0)), spec, spec, spec]
        args += [g, w, m, v]
    for g, w, m, v in items:
        tr = w.shape[0] // ADAMW_PARTS
        out_specs += [pl.BlockSpec((tr, w.shape[1]), lambda i, j: (i, 0))] * 4
        out_shape += [jax.ShapeDtypeStruct(w.shape, F32)] * 4
    res, got = _ride_call(_adamw_body(len(items)), (ADAMW_PARTS, 1), in_specs, out_specs, out_shape,
                          Exchange(rode, modes) if rode else None, rode, name, args)
    return [res[4 * t:4 * t + 4] for t in range(len(items))], got


def _adamw_body(k):
    def body(*refs):
        for t in range(k):
            _adamw_update(*refs[4 * t:4 * t + 4], *refs[4 * k + 4 * t:4 * k + 4 * t + 4])
    return body


def adamw_multi(items, grid, name):
    k = len(items)
    ins, in_specs, out_specs, out_shape = [], [], [], []
    for g, g_spec, w, m, v, w_spec in items:
        ins += [g, w, m, v]
        in_specs += [g_spec, w_spec, w_spec, w_spec]
    for g, g_spec, w, m, v, w_spec in items:
        out_specs += [w_spec] * 4
        out_shape += [jax.ShapeDtypeStruct(w.shape, F32)] * 4
    res = pl.pallas_call(_adamw_body(k), grid=grid, in_specs=in_specs, out_specs=out_specs, out_shape=out_shape,
                         compiler_params=_cp(("arbitrary",) * len(grid)), name=name)(*ins)
    return [res[4 * t:4 * t + 4] for t in range(k)]


def _whole(a, grid_rank):
    zeros = (0,) * a.ndim
    return pl.BlockSpec(a.shape, lambda *idx: zeros)


def sum_slots(xs, name):
    def body(*refs):
        for x_ref, o_ref in zip(refs[:len(xs)], refs[len(xs):]):
            acc = x_ref[0]
            for s in range(1, NDEV):
                acc = acc + x_ref[s]
            o_ref[...] = acc

    return pl.pallas_call(body, out_shape=[jax.ShapeDtypeStruct(x.shape[1:], F32) for x in xs],
                          compiler_params=_cp(), name=name)(*xs)


def _col_shards(g):
    R, N = g.shape
    return g.reshape(R, NDEV, N // NDEV).transpose(1, 0, 2)


def _vec2(v):
    return jnp.broadcast_to(v.reshape(1, 1, -1), (2, 1, v.size))


SHARD_ROWS = {"mla_w_in": 192, "mla_w_uq": 192, "mla_w_ukv": 256, "s5_w_in": 256}


def _t_shard(wsh, rows):
    t = wsh[0].T.astype(BF16)
    return jnp.pad(t, ((0, rows - t.shape[0]), (0, 0)))


def _win_order():
    w = IN_W // NDEV
    perm = np.zeros((IN_WP, NDEV * SHARD_ROWS["mla_w_in"]), np.float32)
    first = QL + KVL + ROPE
    for c in range(IN_W):
        n = c + HEADS * VD if c < first else c - first
        perm[n, (c // w) * SHARD_ROWS["mla_w_in"] + c % w] = 1.0
    return jnp.asarray(perm, BF16)


def local_step(ctx, x, tgt, mod, Wt, small, l1_shards):
    T = LC + x.shape[0]
    xa = ("cat", ctx, x)
    sh = [mod[i, :, None, 0:D] for i in range(2)]
    sc = [mod[i, :, None, D:2 * D] for i in range(2)]
    gt = [mod[i, :, None, 2 * D:] for i in range(2)]
    ng = [_vec2(small["norm_g"][i]) for i in range(2)]
    qg, kvg = _vec2(small["mla_q_norm"]), _vec2(small["mla_kv_norm"])
    cosf, sinf, pm, pmt = _rope_tables(T)

    (h0, p0, cqn, ckvn), _ = rowwise(st_l0_pre, [xa], [ng[0], sc[0], sh[0], qg, kvg],
                                     [(D, BF16), (IN_WP, F32), (QL, BF16), (KVL, BF16)], [], "l0_pre", mats=[Wt["mla_w_in"]])
    z0, cq, ckv = (p0, 0, HEADS * VD), (p0, HEADS * VD // QL, QL), (p0, (HEADS * VD + QL) // KVL, KVL)
    Q = project_q(cqn, Wt["mla_w_uq"], cosf, sinf, pm, "l0_uq")
    K, V = project_kv(ckvn, Wt["mla_w_ukv"], p0, (HEADS * VD + QL + KVL) // 128, "l0_ukv")
    (o, lse), got = attn_fwd(Q, K, V, "l0_attn", rode=l1_shards, modes="gather")
    Wt, small = dict(Wt), dict(small)
    for n, a in zip(L1_BIG, got):
        Wt[n] = a.reshape(-1, a.shape[-1])
    vecs = lax.bitcast_convert_type(got[-1].reshape(NDEV, 2, -1, 2), F32)
    small["s5_d"], small["s5_b_glu"] = vecs[:, 0, :].reshape(D), vecs[:, 1, :].reshape(D)
    o2 = o.transpose(1, 0, 2).reshape(T, HEADS * VD)
    (og, out0, x1), _ = rowwise(st_l0_post, [o2, z0, xa], [gt[0]], [(D, BF16), (D, BF16), (D, F32)], [], "l0_post",
                                mats=[Wt["mla_w_out"]])

    ls = small["s5_log_step"].reshape(2, G, 1)
    a_re, a_im = small["s5_a_re"].reshape(2, G, P), small["s5_a_im"].reshape(2, G, P)
    b_re, b_im = small["s5_b_re"].reshape(2, G * P, CH), small["s5_b_im"].reshape(2, G * P, CH)
    lam_re, lam_im, f_re, f_im = disc_fwd(a_re, a_im, ls, "s5_disc")
    f_re2, f_im2 = f_re.reshape(2, G * P, 1), f_im.reshape(2, G * P, 1)
    bb_re, bb_im = disc_b(f_re2, f_im2, b_re, b_im, "s5_disc_b")
    compact = lambda m: m.reshape(2, NJ, UB, P)
    bre = compact(bb_re.reshape(2, G, P, CH).transpose(0, 1, 3, 2))
    bim = compact(bb_im.reshape(2, G, P, CH).transpose(0, 1, 3, 2))
    cre, cim = compact(small["s5_c_re"]), compact(small["s5_c_im"])
    lam_re4, lam_im4 = lam_re.reshape(2, NJ, 1, SB), lam_im.reshape(2, NJ, 1, SB)

    (h1, p1), _ = rowwise(st_l1_pre, [x1], [ng[1], sc[1], sh[1]], [(D, BF16), (2 * D, F32)], [], "l1_pre", mats=[Wt["s5_w_in"]])
    u, z1 = (p1, 0, D), (p1, 1, D)
    yssm = scan_fwd(p1, lam_re4, lam_im4, bre, bim, cre, cim, "s5_scan")
    dvec, bglu = _vec2(small["s5_d"]), _vec2(small["s5_b_glu"])
    fg = _vec2(small["final_g"])
    lat_mask = jnp.stack([jnp.zeros((1, D), F32), jnp.ones((1, D), F32)])
    (y, y1b, gl, y3, out1, dx2), (dfg, lvec) = rowwise(
        st_l1_mlp, [yssm, u, z1, x1, ("lat", tgt)], [dvec, bglu, gt[1], fg, lat_mask],
        [(D, F32), (D, BF16), (D, BF16), (D, BF16), (D, BF16), (D, F32)], [D, 128], "l1_mlp",
        mats=[Wt["s5_w_glu"], Wt["s5_w_out"]])

    (dz1, dy, du_d), (dgt1, dbglu, dd), (g_w_out5, g_w_glu) = rowwise(
        st_l1_mlp_bwd, [dx2, out1, y3, y, gl, z1, u, y1b], [gt[1], bglu, dvec], [(D, BF16), (D, F32), (D, F32)], [D, D, D],
        "l1_mlp_b", mats=[Wt["s5_w_out"], Wt["s5_w_glu"]], out_accs=[(D, D), (D, D)])
    du_s, dlr, dli, dbre, dbim, dcre, dcim = scan_bwd(p1, dy, lam_re4, lam_im4, bre, bim, cre, cim, "s5_scan_b")
    dbb_re = dbre.reshape(2, G, CH, P).transpose(0, 1, 3, 2).reshape(2, G * P, CH)
    dbb_im = dbim.reshape(2, G, CH, P).transpose(0, 1, 3, 2).reshape(2, G * P, CH)
    g_c_re, g_c_im = dcre.reshape(2, G, CH, P), dcim.reshape(2, G, CH, P)
    g_b_re, g_b_im, dfr, dfi = disc_b_bwd(f_re2, f_im2, b_re, b_im, dbb_re, dbb_im, "s5_disc_b_b")
    g_a_re, g_a_im, g_ls = disc_a_bwd(a_re, a_im, ls, dlr.reshape(2, G, P), dli.reshape(2, G, P),
                                      dfr.reshape(2, G, P), dfi.reshape(2, G, P), "s5_disc_b_a")
    (dx1,), (dsh1, dsc1, dng1), (g_w_in5,) = rowwise(
        st_l1_tail_bwd, [du_d, du_s, dz1, h1, x1, dx2], [ng[1], sc[1]], [(D, F32)], [D, D, D], "l1_pre_b",
        mats=[Wt["s5_w_in"]], out_accs=[(NDEV, D, 2 * D // NDEV)])

    (do2, dz0), (dgt0,), (g_w_out,) = rowwise(st_l0_post_bwd, [dx1, out0, og, o2, z0], [gt[0]], [(D, F32), (D, F32)], [D],
                                              "l0_post_b", mats=[Wt["mla_w_out"]], out_accs=[(D, D)])
    doh = do2.reshape(T, HEADS, VD).transpose(1, 0, 2)
    rows8 = lambda g: g.reshape(NDEV, -1, g.shape[-1])
    both = lambda s: s[0, 0] + s[1, 0]
    dense = lambda g: g.reshape(2, G * P * CH // 128, 128)
    chunks = [dense(g_b_re), dense(g_b_im), g_c_re, g_c_im]
    l1_send = [g_w_in5, rows8(g_w_glu), rows8(g_w_out5), rows8(g_w_out),
               both(dd).reshape(NDEV, 1, -1), both(dbglu).reshape(NDEV, 1, -1)]
    (dQ, dK, dV), l1_recv = attn_bwd(Q, K, V, o, lse, doh, "l0_attn_b", rode=l1_send + chunks,
                                     modes=["lead"] * len(l1_send) + [a.shape[1] // NDEV for a in chunks])
    dqh = rope(dQ, cosf, sinf, pmt, True, BF16, "l0_rope_q_b", scale=SCALE)
    dq = dqh.transpose(1, 0, 2).reshape(T, HEADS * QK)
    n_owned = len(l1_send)
    reduced = sum_slots(l1_recv[n_owned:], "sum_chunks")
    (dkv, dkr), chunk_all = split_kv_grads(dK, dV, "l0_kv_b", rode=[jnp.stack(reduced[:2]), jnp.stack(reduced[2:])],
                                           modes="gather")
    (grad_x,), (dqg, dkvg, dsh0, dsc0, dng0), (g_uq, g_ukv, g_p) = rowwise(
        st_l0_tail_bwd, [dq, dkv, dkr, dz0, cq, ckv, cqn, ckvn, h0, xa, dx1], [qg, kvg, ng[0], sc[0]],
        [(D, F32, "lat")], [QL, KVL, D, D, D], "l0_pre_b", mats=[Wt["mla_w_uq"], Wt["mla_w_ukv"], Wt["mla_w_in"]],
        out_accs=[(QL, HEADS * QK), (KVL, HEADS * KVW), (D, IN_WP)])
    g_w_uq, g_w_ukv = _col_shards(g_uq).astype(BF16), _col_shards(g_ukv).astype(BF16)
    g_w_in = _col_shards(jnp.concatenate([g_p[:, HEADS * VD:IN_W], g_p[:, :HEADS * VD]], axis=1)).astype(BF16)

    dmod = jnp.stack([jnp.concatenate([dsh0, dsc0, dgt0], axis=-1)[:, 0], jnp.concatenate([dsh1, dsc1, dgt1], axis=-1)[:, 0]])
    gbig = {"mla_w_in": g_w_in, "mla_w_uq": g_w_uq, "mla_w_ukv": g_w_ukv}
    gsmall = {"norm_g": jnp.stack([both(dng0), both(dng1)]), "mla_q_norm": both(dqg), "mla_kv_norm": both(dkvg),
              "s5_a_re": g_a_re, "s5_a_im": g_a_im, "s5_log_step": g_ls, "final_g": dfg[1, 0]}
    return lvec[1], grad_x, dmod, gbig, gsmall, l1_recv[:n_owned], chunk_all


COL_SHARDED = ("mla_w_in", "mla_w_uq", "mla_w_ukv", "s5_w_in")
ROW_SHARDED = ("mla_w_out", "s5_w_glu", "s5_w_out")
VEC_SHARDED = ("s5_d", "s5_b_glu")
BIG = COL_SHARDED + ROW_SHARDED
L0_BIG = ("mla_w_in", "mla_w_uq", "mla_w_ukv")
L1_BIG = ("s5_w_in", "s5_w_glu", "s5_w_out", "mla_w_out")
BITS16 = jnp.bfloat16
SMALL_RS = ("norm_g", "mla_q_norm", "mla_kv_norm", "s5_a_re", "s5_a_im", "s5_log_step", "s5_b_re", "s5_b_im",
            "s5_c_re", "s5_c_im", "final_g")
CHUNKED = ("s5_b_re", "s5_b_im", "s5_c_re", "s5_c_im")
DENSE = ("s5_b_re", "s5_b_im")
TINY = ("norm_g", "mla_q_norm", "mla_kv_norm", "s5_a_re", "s5_a_im", "s5_log_step", "final_g")
ORDER = ("c_ctx", "ada_w", "ada_b", "norm_g", "mla_w_in", "mla_q_norm", "mla_w_uq", "mla_kv_norm", "mla_w_ukv",
         "mla_w_out", "s5_w_in", "s5_a_re", "s5_a_im", "s5_log_step", "s5_b_re", "s5_b_im", "s5_c_re", "s5_c_im",
         "s5_d", "s5_w_glu", "s5_b_glu", "s5_w_out", "final_g")


def kernel(x, c, ctx, c_ctx, ada_w, ada_b, norm_g, mla_w_in, mla_q_norm, mla_w_uq, mla_kv_norm, mla_w_ukv, mla_w_out, s5_w_in, s5_a_re, s5_a_im, s5_log_step, s5_b_re, s5_b_im, s5_c_re, s5_c_im, s5_d, s5_w_glu, s5_b_glu, s5_w_out, final_g, loss_target, m_c_ctx, m_ada_w, m_ada_b, m_norm_g, m_mla_w_in, m_mla_q_norm, m_mla_w_uq, m_mla_kv_norm, m_mla_w_ukv, m_mla_w_out, m_s5_w_in, m_s5_a_re, m_s5_a_im, m_s5_log_step, m_s5_b_re, m_s5_b_im, m_s5_c_re, m_s5_c_im, m_s5_d, m_s5_w_glu, m_s5_b_glu, m_s5_w_out, m_final_g, v_c_ctx, v_ada_w, v_ada_b, v_norm_g, v_mla_w_in, v_mla_q_norm, v_mla_w_uq, v_mla_kv_norm, v_mla_w_ukv, v_mla_w_out, v_s5_w_in, v_s5_a_re, v_s5_a_im, v_s5_log_step, v_s5_b_re, v_s5_b_im, v_s5_c_re, v_s5_c_im, v_s5_d, v_s5_w_glu, v_s5_b_glu, v_s5_w_out, v_final_g):
    w = dict(c_ctx=c_ctx, ada_w=ada_w, ada_b=ada_b, norm_g=norm_g, mla_w_in=mla_w_in, mla_q_norm=mla_q_norm,
             mla_w_uq=mla_w_uq, mla_kv_norm=mla_kv_norm, mla_w_ukv=mla_w_ukv, mla_w_out=mla_w_out, s5_w_in=s5_w_in,
             s5_a_re=s5_a_re, s5_a_im=s5_a_im, s5_log_step=s5_log_step, s5_b_re=s5_b_re, s5_b_im=s5_b_im,
             s5_c_re=s5_c_re, s5_c_im=s5_c_im, s5_d=s5_d, s5_w_glu=s5_w_glu, s5_b_glu=s5_b_glu, s5_w_out=s5_w_out,
             final_g=final_g)
    m = dict(c_ctx=m_c_ctx, ada_w=m_ada_w, ada_b=m_ada_b, norm_g=m_norm_g, mla_w_in=m_mla_w_in, mla_q_norm=m_mla_q_norm,
             mla_w_uq=m_mla_w_uq, mla_kv_norm=m_mla_kv_norm, mla_w_ukv=m_mla_w_ukv, mla_w_out=m_mla_w_out,
             s5_w_in=m_s5_w_in, s5_a_re=m_s5_a_re, s5_a_im=m_s5_a_im, s5_log_step=m_s5_log_step, s5_b_re=m_s5_b_re,
             s5_b_im=m_s5_b_im, s5_c_re=m_s5_c_re, s5_c_im=m_s5_c_im, s5_d=m_s5_d, s5_w_glu=m_s5_w_glu,
             s5_b_glu=m_s5_b_glu, s5_w_out=m_s5_w_out, final_g=m_final_g)
    v = dict(c_ctx=v_c_ctx, ada_w=v_ada_w, ada_b=v_ada_b, norm_g=v_norm_g, mla_w_in=v_mla_w_in, mla_q_norm=v_mla_q_norm,
             mla_w_uq=v_mla_w_uq, mla_kv_norm=v_mla_kv_norm, mla_w_ukv=v_mla_w_ukv, mla_w_out=v_mla_w_out,
             s5_w_in=v_s5_w_in, s5_a_re=v_s5_a_re, s5_a_im=v_s5_a_im, s5_log_step=v_s5_log_step, s5_b_re=v_s5_b_re,
             s5_b_im=v_s5_b_im, s5_c_re=v_s5_c_re, s5_c_im=v_s5_c_im, s5_d=v_s5_d, s5_w_glu=v_s5_w_glu,
             s5_b_glu=v_s5_b_glu, s5_w_out=v_s5_w_out, final_g=v_final_g)

    me = 4 * lax.axis_index("x") + 2 * lax.axis_index("y") + lax.axis_index("c")
    WA = ada_w.shape[2]

    def shard(n):
        return _t_shard(w[n], SHARD_ROWS[n]) if n in COL_SHARDED else w[n][0].astype(BF16)

    wgot = exchange([c] + [shard(n) for n in L0_BIG], "gather", "gather_w")

    cg = wgot[0].reshape(NDEV, D)
    cc2 = c_ctx.reshape(1, D)
    ada_b_loc = lax.dynamic_slice_in_dim(ada_b.reshape(2, 3 * D // WA, WA), me, 1, axis=1)
    part = ada_fwd(cg, cc2, ada_w, ada_b_loc, "ada_fwd")
    pg = exchange([part], "gather", "gather_mod")[0]
    mod_l = lax.dynamic_index_in_dim(pg, me, axis=2, keepdims=False).transpose(1, 0, 2).reshape(2, 3 * D)
    mod_c = pg[:, :, NDEV, :].transpose(1, 0, 2).reshape(2, 3 * D)
    mod = jnp.stack([mod_c, mod_l], axis=1)

    Wt = {n: a.reshape(-1, a.shape[-1]) for n, a in zip(L0_BIG, wgot[1:])}
    Wt["mla_w_in"] = mm(_win_order(), Wt["mla_w_in"], "nn", "w_in_order", out_dtype=BF16)
    vec_bits = lax.bitcast_convert_type(jnp.concatenate([s5_d, s5_b_glu], axis=0), BITS16).reshape(2, -1)
    small = {n: w[n] for n in SMALL_RS}

    lvec, grad_x, dmod, gbig, gsmall, l1_recv, (bb_all, cc_all) = local_step(
        ctx[0], x[0], loss_target[0], mod, Wt, small, [shard(n) for n in L1_BIG] + [vec_bits])
    grad_x = grad_x[None]

    recv = dict(zip(L1_BIG + VEC_SHARDED, l1_recv))
    out = {}

    def keep(n, res):
        for key, arr in zip("gdmv", res):
            out[key, n] = arr.reshape(w[n].shape)

    kshape = lambda n: w[n].shape if w[n].ndim > 1 else (1, w[n].size)
    flat = jnp.concatenate([gsmall[n].reshape(-1) for n in TINY] + [dmod.reshape(-1), lvec.reshape(-1)])[None]
    *l0_recv, flat_all = exchange([gbig[n] for n in L0_BIG] + [flat], ["lead"] * len(L0_BIG) + ["gather"], "scatter_grads")
    chunk_all = [bb_all[:, 0], bb_all[:, 1], cc_all[:, 0], cc_all[:, 1]]
    tiny_all, off = [], 0
    for n in TINY:
        tiny_all.append(flat_all[:, 0, off:off + w[n].size].reshape((NDEV,) + kshape(n)))
        off += w[n].size
    dm_all = flat_all[:, 0, off:off + dmod.size].reshape((NDEV,) + dmod.shape)
    loss = sum_slots([flat_all[:, :, off + dmod.size:]], "loss_sum")[0][0, 0]

    dm_cols = lax.dynamic_slice_in_dim(dm_all.reshape(NDEV, 2, 2, 3 * D // WA, WA), me, 1, axis=3)[:, :, :, 0]
    dm_loc = jnp.concatenate([dm_cols[:, :, 1].transpose(1, 0, 2), dm_cols[:, :, 0].transpose(1, 0, 2)], axis=1)
    g_ada_w, dcc_part, g_ada_b = ada_bwd(cg, cc2, ada_w, dm_loc, dm_all.transpose(0, 2, 1, 3).reshape(2 * NDEV, 2, 3 * D), "ada_bwd")
    dcc_all = exchange([dcc_part], "gather", "gather_dcc")[0].reshape(NDEV, D)
    g_c_ctx = cctx_finish(dcc_all, cc2, "cctx_finish")

    flat2 = lambda t: t.reshape(-1, t.shape[-1])
    recv.update(dict(zip(L0_BIG, l0_recv)))
    big = [(recv[n], w[n][0], m[n][0], v[n][0]) for n in BIG]
    big.append((flat2(g_ada_w)[None], flat2(ada_w), flat2(m_ada_w), flat2(v_ada_w)))
    for n, r in zip(BIG + ("ada_w",), adamw_rows(big, "adamw_big")[0]):
        keep(n, r)
    items = []
    halves = 2
    for n, g in zip(CHUNKED, chunk_all):
        blk = (1, 1, G // halves) + w[n].shape[3:]
        g = jnp.moveaxis(g, 0, 1).reshape(w[n].shape)
        g_spec = pl.BlockSpec((1,) + blk, lambda d, s: (0, 0, d, s, 0, 0))
        items.append((g[None], g_spec, w[n], m[n], v[n], pl.BlockSpec(blk, lambda d, s: (0, d, s, 0, 0))))
    for n, res in zip(CHUNKED, adamw_multi(items, (2, halves), "adamw_bc")):
        keep(n, res)
    tiny_g = dict(zip(TINY, tiny_all))
    tiny_g.update({n: recv[n] for n in VEC_SHARDED})
    tiny_g["c_ctx"], tiny_g["ada_b"] = g_c_ctx[None], g_ada_b[None]
    names = list(tiny_g)
    items = [(tiny_g[n], _whole(tiny_g[n], 1)) + tuple(t[n].reshape(kshape(n)) for t in (w, m, v))
             + (pl.BlockSpec(kshape(n), lambda i, r=len(kshape(n)): (0,) * r),) for n in names]
    for n, res in zip(names, adamw_multi(items, (1,), "adamw_small")):
        keep(n, res)

    return (loss, grad_x, *[out["g", n] for n in ORDER], *[out["d", n] for n in ORDER],
            *[out["m", n] for n in ORDER], *[out["v", n] for n in ORDER])
```

```python
import math

import numpy as np
import jax
import jax.numpy as jnp
from jax import lax
from jax.experimental import pallas as pl
from jax.experimental.pallas import tpu as pltpu

F32 = jnp.float32
BF16 = jnp.bfloat16

D = 1024
L = 2048
LC = 256
NDEV = 8
GRID_W = 64
EPS = 1e-6
HEADS = 16
NOPE = 64
ROPE = 32
QK = NOPE + ROPE
VD = 64
IN_W = 256 + 128 + ROPE + HEADS * 64
IN_WP = 1536
QL = 256
KVL = 128
SCALE = QK ** -0.5
LOG2E = math.log2(math.e)
THETA = 10000.0
G = 64
P = 64
CH = 16
GB = 8
NJ = G // GB
UB = GB * CH
SB = GB * P
SEG = 16
TB = 256
VMEM_LIMIT = 56 * 1024 * 1024
B1, B2, LR, AEPS, WD, STEP = 0.9, 0.999, 0.001, 1e-8, 0.01, 10
MESH_T = pl.DeviceIdType.MESH


def _cp(sem=None):
    return pltpu.CompilerParams(dimension_semantics=sem, vmem_limit_bytes=VMEM_LIMIT)


def _sig(x):
    return 1.0 / (1.0 + jnp.exp(-x))


def _silu(x):
    return x * _sig(x)


def _dsilu(x):
    s = _sig(x)
    return s * (1.0 + x * (1.0 - s))


_GK = math.sqrt(2.0 / math.pi)


def _gelu(x):
    return 0.5 * x * (1.0 + jnp.tanh(_GK * (x + 0.044715 * x * x * x)))


def _dgelu(x):
    t = jnp.tanh(_GK * (x + 0.044715 * x * x * x))
    return 0.5 * (1.0 + t) + 0.5 * x * (1.0 - t * t) * _GK * (1.0 + 3 * 0.044715 * x * x)


def _rs(x):
    return lax.rsqrt(jnp.mean(x * x, axis=-1, keepdims=True) + EPS)


def _sum0(x):
    return jnp.sum(x, axis=0, keepdims=True)


def st_norm_mod(x, g, sc, sh):
    y = x * _rs(x) * g
    return (y * (1.0 + sc) + sh,), ()


def st_norm_mod_bwd(x, dh, dres, g, sc):
    r = _rs(x)
    xn = x * r
    y = xn * g
    dy = dh * (1.0 + sc)
    dxn = dy * g
    dx = r * (dxn - xn * jnp.mean(dxn * xn, axis=-1, keepdims=True))
    return (dres + dx,), (_sum0(dh), _sum0(dh * y), _sum0(dy * xn))


def st_rms(x, g):
    return (x * _rs(x) * g,), ()


def st_rms_bwd(x, dy, g):
    r = _rs(x)
    n = x * r
    dn = dy * g
    dx = r * (dn - n * jnp.mean(dn * n, axis=-1, keepdims=True))
    return (dx,), (_sum0(dy * n),)


def st_rms2(x1, x2, g1, g2):
    return st_rms(x1, g1)[0] + st_rms(x2, g2)[0], ()


def st_rms2_bwd(x1, dy1, x2, dy2, g1, g2):
    (d1,), (s1,) = st_rms_bwd(x1, dy1, g1)
    (d2,), (s2,) = st_rms_bwd(x2, dy2, g2)
    return (d1, d2), (s1, s2)


def st_gate_bwd(dog, o, z):
    return (dog * _silu(z), dog * o * _dsilu(z)), ()


def st_resid_bwd(dx, out, gt):
    return (dx * gt,), (_sum0(dx * out),)


def st_s5a(yssm, u, d):
    y = yssm + d * u
    return (y, _gelu(y)), ()


def st_s5b_bwd(dy3, y, gl, z, b):
    y1 = _gelu(y)
    s = _sig(gl + b)
    dy2 = dy3 * _silu(z)
    dz = dy3 * y1 * s * _dsilu(z)
    dgl = dy2 * y1 * s * (1.0 - s)
    return (dgl, dz, dy2 * s), (_sum0(dgl),)


def st_s5a_bwd(dy1a, dy1b, y, u, d):
    dy = (dy1a + dy1b) * _dgelu(y)
    return (dy, dy * d), (_sum0(dy * u),)


def st_l0_pre(x, g, sc, sh, qg, kvg, w_in):
    hb = st_norm_mod(x, g, sc, sh)[0][0].astype(BF16)
    p = lax.dot_general(hb, w_in, _DN["nt"], preferred_element_type=F32)
    cq, ckv = p[:, HEADS * VD:HEADS * VD + QL], p[:, HEADS * VD + QL:HEADS * VD + QL + KVL]
    return (hb, p) + st_rms2(cq, ckv, qg, kvg)[0], ()


def st_l0_tail_bwd(dq, dkv, dkr, dz, cq, ckv, cqn, ckvn, h, x, dres, qg, kvg, g, sc, w_uq, w_ukv, w_in):
    dcqn = jnp.dot(dq, w_uq, preferred_element_type=F32)
    dckvn = jnp.dot(dkv, w_ukv, preferred_element_type=F32)
    (dcq, dckv), (dqg, dkvg) = st_rms2_bwd(cq, dcqn, ckv, dckvn, qg, kvg)
    dp = jnp.concatenate([dz, dcq, dckv, dkr], axis=1).astype(BF16)
    dh = jnp.dot(dp, w_in, preferred_element_type=F32)
    outs, sums = st_norm_mod_bwd(x, dh, dres, g, sc)
    tn = lambda a, b: lax.dot_general(a, b, _DN["tn"], preferred_element_type=F32)
    return outs, (dqg, dkvg) + sums, (tn(cqn, dq), tn(ckvn, dkv), tn(h, dp))


def st_l1_pre(x, g, sc, sh, w_in):
    hb = st_norm_mod(x, g, sc, sh)[0][0].astype(BF16)
    return (hb, lax.dot_general(hb, w_in, _DN["nt"], preferred_element_type=F32)), ()


def st_l1_tail_bwd(du_a, du_b, dz, h, x, dres, g, sc, w_in):
    dp = jnp.concatenate([(du_a + du_b).astype(BF16), dz], axis=1)
    dh = jnp.dot(dp, w_in, preferred_element_type=F32)
    outs, sums = st_norm_mod_bwd(x, dh, dres, g, sc)
    w = dp.shape[1] // NDEV
    shards = [lax.dot_general(h, dp[:, r * w:(r + 1) * w], _DN["tn"], preferred_element_type=F32) for r in range(NDEV)]
    return outs, sums, (jnp.stack(shards),)


def st_l0_post(o, z, x, gt, w_out):
    og = (o * _silu(z)).astype(BF16)
    out = jnp.dot(og, w_out, preferred_element_type=F32)
    return (og, out, x + gt * out), ()


def st_l0_post_bwd(dx1, out, og, o, z, gt, w_out):
    (dout,), (dgt,) = st_resid_bwd(dx1, out.astype(F32), gt)
    doutb = dout.astype(BF16)
    dog = lax.dot_general(doutb, w_out, _DN["nt"], preferred_element_type=F32)
    return st_gate_bwd(dog, o, z)[0], (dgt,), (lax.dot_general(og, doutb, _DN["tn"], preferred_element_type=F32),)


def st_l1_mlp(yssm, u, z, x1, tgt, d, bglu, gt, fg, mask, w_glu, w_out):
    (y, y1), _ = st_s5a(yssm, u, d)
    y1b = y1.astype(BF16)
    gl = jnp.dot(y1b, w_glu, preferred_element_type=F32)
    y3 = (y1 * _sig(gl + bglu) * _silu(z)).astype(BF16)
    out = jnp.dot(y3, w_out, preferred_element_type=F32)
    (dx2,), sums = st_final(x1 + gt * out, tgt, fg, mask)
    return (y, y1b, gl, y3, out, dx2), sums


def st_l1_mlp_bwd(dx2, out, y3, y, gl, z, u, y1b, gt, bglu, d, w_out, w_glu):
    out, gl = out.astype(F32), gl.astype(F32)
    (dout,), (dgt,) = st_resid_bwd(dx2, out, gt)
    doutb = dout.astype(BF16)
    dy3 = lax.dot_general(doutb, w_out, _DN["nt"], preferred_element_type=F32)
    (dgl, dz, dy1a), (dbglu,) = st_s5b_bwd(dy3, y, gl, z, bglu)
    dglb = dgl.astype(BF16)
    dy1b = lax.dot_general(dglb, w_glu, _DN["nt"], preferred_element_type=F32)
    (dy, du), (dd,) = st_s5a_bwd(dy1a, dy1b, y, u, d)
    g_w_out = lax.dot_general(y3, doutb, _DN["tn"], preferred_element_type=F32)
    g_w_glu = lax.dot_general(y1b, dglb, _DN["tn"], preferred_element_type=F32)
    return (dz, dy, du), (dgt, dbglu, dd), (g_w_out, g_w_glu)


def st_final(x2, tgt, g, mask):
    r = _rs(x2)
    n = x2 * r
    e = n * g - tgt
    dyo = e * (1.0 / D)
    dn = dyo * g
    dx = r * (dn - n * jnp.mean(dn * n, axis=-1, keepdims=True))
    lsum = jnp.sum(_sum0(e * e), axis=1, keepdims=True) * (0.5 / D)
    return (dx * mask,), (_sum0(dyo * n), jnp.broadcast_to(lsum, (1, 128)))


def rowwise(fn, rows, vecs, out_rows, out_sums, name, mats=(), out_accs=()):
    lat_blk = lambda i: jnp.maximum(i - 1, 0)
    arrays, in_specs, pick = [], [], []
    for a in rows:
        if not isinstance(a, tuple):
            a = (a, 0, a.shape[1])
        tag = a[0] if isinstance(a[0], str) else None
        if tag == "cat":
            _, ctx, x = a
            arrays += [ctx, x]
            in_specs += [pl.BlockSpec((TB, ctx.shape[1]), lambda i: (0, 0)),
                         pl.BlockSpec((TB, x.shape[1]), lambda i: (lat_blk(i), 0))]
            pick.append(2)
        elif tag == "lat":
            arrays.append(a[1])
            in_specs.append(pl.BlockSpec((TB, a[1].shape[1]), lambda i: (lat_blk(i), 0)))
            pick.append(1)
        else:
            arr, cb, width = a
            arrays.append(arr)
            in_specs.append(pl.BlockSpec((TB, width), lambda i, cb=cb: (i, cb)))
            pick.append(1)
    T = LC + L
    nin, nv, nm, no, ns = len(arrays), len(vecs), len(mats), len(out_rows), len(out_sums)

    def body(*refs):
        i = pl.program_id(0)
        vals, k = [], 0
        for p in pick:
            if p == 2:
                vals.append(jnp.where(i == 0, refs[k][...], refs[k + 1][...]))
            else:
                vals.append(refs[k][...])
            k += p
        vals += [r[0] for r in refs[nin:nin + nv]] + [r[...] for r in refs[nin + nv:nin + nv + nm]]
        res = fn(*vals)
        first_out = nin + nv + nm
        for r, o in zip(refs[first_out:first_out + no], res[0]):
            r[...] = o.astype(r.dtype)
        sum_refs = refs[first_out + no:first_out + no + ns]
        if sum_refs:
            @pl.when(i <= 1)
            def _():
                for r in sum_refs:
                    r[...] = jnp.zeros_like(r)
            for r, s in zip(sum_refs, res[1]):
                r[0] += s
        na = len(out_accs)
        if na:
            acc_out, acc = refs[first_out + no + ns:first_out + no + ns + na], refs[first_out + no + ns + na:]

            @pl.when(i == 0)
            def _():
                for r in acc:
                    r[...] = jnp.zeros_like(r)
            for r, a in zip(acc, res[2]):
                r[...] += a

            @pl.when(i == T // TB - 1)
            def _():
                for o, r in zip(acc_out, acc):
                    o[...] = r[...].astype(o.dtype)

    kind = lambda i: (jnp.minimum(i, 1), 0, 0)
    in_specs += [pl.BlockSpec((1, 1, v.shape[2]), kind) for v in vecs]
    in_specs += [pl.BlockSpec(m.shape, lambda i: (0, 0), pipeline_mode=pl.Buffered(1)) for m in mats]
    out_specs, out_shape = [], []
    for o in out_rows:
        lat = len(o) == 3
        out_specs.append(pl.BlockSpec((TB, o[0]), (lambda i: (lat_blk(i), 0)) if lat else (lambda i: (i, 0))))
        out_shape.append(jax.ShapeDtypeStruct((L if lat else T, o[0]), o[1]))
    out_specs += [pl.BlockSpec((1, 1, c), kind) for c in out_sums]
    out_shape += [jax.ShapeDtypeStruct((2, 1, c), F32) for c in out_sums]
    out_specs += [pl.BlockSpec(s, lambda i, r=len(s): (0,) * r) for s in out_accs]
    out_shape += [jax.ShapeDtypeStruct(s, BF16) for s in out_accs]
    res = pl.pallas_call(body, grid=(T // TB,), in_specs=in_specs, out_specs=out_specs, out_shape=out_shape,
                         scratch_shapes=[pltpu.VMEM(s, F32) for s in out_accs],
                         compiler_params=_cp(("arbitrary",)), name=name)(*arrays, *vecs, *mats)
    if out_accs:
        return res[:no], res[no:no + ns], res[no + ns:]
    return res[:no], res[no:]


_DN = {"nn": (((1,), (0,)), ((), ())), "nt": (((1,), (1,)), ((), ())), "tn": (((0,), (0,)), ((), ()))}


def mm(a, b, mode, name, out_dtype=F32, tm=None, tn=None):
    if mode == "nn":
        (M, K), (_, N) = a.shape, b.shape
    elif mode == "nt":
        (M, K), (N, _) = a.shape, b.shape
    else:
        (K, M), (_, N) = a.shape, b.shape
    if tm is None:
        tm = next((t for t in (768, 512, 256) if M % t == 0 and M > t), M)
    tn = N if tn is None else tn
    dn = _DN[mode]

    def body(a_ref, b_ref, o_ref):
        o_ref[...] = lax.dot_general(a_ref[...].astype(BF16), b_ref[...].astype(BF16), dn,
                                     preferred_element_type=F32).astype(o_ref.dtype)

    a_spec = pl.BlockSpec((K, tm), lambda i, j: (0, i)) if mode == "tn" else pl.BlockSpec((tm, K), lambda i, j: (i, 0))
    b_spec = pl.BlockSpec((tn, K), lambda i, j: (j, 0)) if mode == "nt" else pl.BlockSpec((K, tn), lambda i, j: (0, j))
    return pl.pallas_call(body, grid=(M // tm, N // tn), in_specs=[a_spec, b_spec],
                          out_specs=pl.BlockSpec((tm, tn), lambda i, j: (i, j)), out_shape=jax.ShapeDtypeStruct((M, N), out_dtype),
                          compiler_params=_cp(("parallel", "arbitrary")), name=name)(a, b)


def _rope_tables(T, width=QK, first=NOPE):
    nlat = T - LC
    pos = np.arange(nlat)
    row, col = pos // GRID_W, pos % GRID_W
    half = ROPE // 2
    inv = 1.0 / (THETA ** (np.arange(0, half, 2, dtype=np.float64) / half))
    cosf = np.ones((T, width), np.float64)
    sinf = np.zeros((T, width), np.float64)
    perm = np.zeros((width, width), np.float32)
    for m in range(ROPE):
        j = first + m
        blk, w = m // half, m % half
        ang = (row if blk == 0 else col)[:, None] * inv[None, :]
        f = w % (half // 2)
        cosf[LC:, j] = np.cos(ang[:, f])
        if w < half // 2:
            sinf[LC:, j] = -np.sin(ang[:, f])
            perm[j + half // 2, j] = 1.0
        else:
            sinf[LC:, j] = np.sin(ang[:, f])
            perm[j - half // 2, j] = 1.0
    return jnp.asarray(cosf, F32), jnp.asarray(sinf, F32), jnp.asarray(perm, BF16), jnp.asarray(perm.T, BF16)


def _exact_perm(x, pm):
    hi = x.astype(BF16)
    r1 = x - hi.astype(F32)
    mid = r1.astype(BF16)
    lo = (r1 - mid.astype(F32)).astype(BF16)
    dot = lambda a: jnp.dot(a, pm, preferred_element_type=F32)
    return dot(hi) + dot(mid) + dot(lo)


def _rot(x, cv, sv, pv, inverse):
    if inverse:
        return x * cv + _exact_perm(x * sv, pv)
    return x * cv + _exact_perm(x, pv) * sv


def rope(x, cosf, sinf, pm, inverse, out_dtype, name, scale=1.0):
    H, T, _ = x.shape

    def body(x_ref, c_ref, s_ref, p_ref, o_ref):
        cv, sv, pv = c_ref[...], s_ref[...], p_ref[...]
        for h in range(H):
            o_ref[h] = (_rot(x_ref[h], cv, sv, pv, inverse) * scale).astype(o_ref.dtype)

    return pl.pallas_call(
        body, grid=(T // TB,),
        in_specs=[pl.BlockSpec((H, TB, QK), lambda i: (0, i, 0)), pl.BlockSpec((TB, QK), lambda i: (i, 0)),
                  pl.BlockSpec((TB, QK), lambda i: (i, 0)), pl.BlockSpec((QK, QK), lambda i: (0, 0))],
        out_specs=pl.BlockSpec((H, TB, QK), lambda i: (0, i, 0)), out_shape=jax.ShapeDtypeStruct((H, T, QK), out_dtype),
        compiler_params=_cp(("parallel",)), name=name)(x, cosf, sinf, pm)


KVW = NOPE + VD


def _kv_selectors():
    s_kn = np.zeros((KVW, QK), np.float32)
    s_kr = np.zeros((128, QK), np.float32)
    s_v = np.zeros((KVW, VD), np.float32)
    for l in range(NOPE):
        s_kn[l, l] = 1.0
    for l in range(ROPE):
        s_kr[l, NOPE + l] = 1.0
    for l in range(VD):
        s_v[NOPE + l, l] = 1.0
    return s_kn, s_kr, s_v


def project_q(cqn, w, name):
    T = cqn.shape[0]
    cosf, sinf, _, _ = _rope_tables(T, 128, NOPE)
    wp = jnp.pad(w.reshape(HEADS, QK, QL), ((0, 0), (0, 128 - QK), (0, 0))).reshape(HEADS * 128, QL)

    def body(a_ref, w_ref, c_ref, s_ref, o_ref):
        a, cv, sv = a_ref[...], c_ref[...], s_ref[...]
        first_of_pair = lax.bitwise_and(lax.broadcasted_iota(jnp.int32, (TB, 128), 1), ROPE // 4) == 0
        for h in range(HEADS):
            qh = _dotf(a, w_ref[pl.ds(h * 128, 128), :], "nt")
            swap = jnp.where(first_of_pair, pltpu.roll(qh, 128 - ROPE // 4, 1), pltpu.roll(qh, ROPE // 4, 1))
            o_ref[h] = ((qh * cv + swap * sv) * (SCALE * LOG2E))[:, :QK].astype(BF16)

    rows = lambda c: pl.BlockSpec((TB, c), lambda i: (i, 0))
    return pl.pallas_call(
        body, grid=(T // TB,), in_specs=[rows(QL), pl.BlockSpec(wp.shape, lambda i: (0, 0)), rows(128), rows(128)],
        out_specs=pl.BlockSpec((HEADS, TB, QK), lambda i: (0, i, 0)), out_shape=jax.ShapeDtypeStruct((HEADS, T, QK), BF16),
        compiler_params=_cp(("parallel",)), name=name)(cqn, wp, cosf, sinf)


def project_kv(ckvn, w, p0, kr_block, name):
    T = ckvn.shape[0]
    cosf, sinf, pm, _ = _rope_tables(T, 128, 0)
    s_kn, s_kr, s_v = (jnp.asarray(s, BF16) for s in _kv_selectors())

    def body(a_ref, w_ref, kr_ref, c_ref, s_ref, p_ref, skn_ref, skr_ref, sv_ref, k_ref, v_ref):
        a = a_ref[...]
        krr = _rot(kr_ref[...], c_ref[...], s_ref[...], p_ref[...], False).astype(BF16)
        kr_part = jnp.dot(krr, skr_ref[...], preferred_element_type=F32)
        for h in range(HEADS):
            kvb = _dotf(a, w_ref[pl.ds(h * KVW, KVW), :], "nt").astype(BF16)
            k_ref[h] = (jnp.dot(kvb, skn_ref[...], preferred_element_type=F32) + kr_part).astype(BF16)
            v_ref[h] = jnp.dot(kvb, sv_ref[...], preferred_element_type=F32).astype(BF16)

    rows = lambda c: pl.BlockSpec((TB, c), lambda i: (i, 0))
    const = lambda x: pl.BlockSpec(x.shape, lambda i: (0, 0))
    return pl.pallas_call(
        body, grid=(T // TB,),
        in_specs=[rows(KVL), const(w), pl.BlockSpec((TB, 128), lambda i: (i, kr_block)),
                  rows(128), rows(128), const(pm), const(s_kn), const(s_kr), const(s_v)],
        out_specs=[pl.BlockSpec((HEADS, TB, QK), lambda i: (0, i, 0)), pl.BlockSpec((HEADS, TB, VD), lambda i: (0, i, 0))],
        out_shape=[jax.ShapeDtypeStruct((HEADS, T, QK), BF16), jax.ShapeDtypeStruct((HEADS, T, VD), BF16)],
        compiler_params=_cp(("parallel",)), name=name)(ckvn, w, p0, cosf, sinf, pm, s_kn, s_kr, s_v)


def split_kv_grads(dk, dv, name, rode=None, modes=None):
    H, T, _ = dk.shape
    cosf, sinf, _, pmt = _rope_tables(T, 128, 0)
    s_kn, s_kr, s_v = _kv_selectors()
    s_knt, s_krt, s_vt = (jnp.asarray(s.T, BF16) for s in (s_kn, s_kr, s_v))

    def body(dk_ref, dv_ref, c_ref, s_ref, p_ref, skn_ref, skr_ref, sv_ref, dkv_ref, dkr_ref):
        total = None
        for h in range(H):
            dkh = dk_ref[h] * (1.0 / LOG2E)
            total = dkh if total is None else total + dkh
            dkv_ref[:, pl.ds(h * KVW, KVW)] = (
                jnp.dot(dkh.astype(BF16), skn_ref[...], preferred_element_type=F32)
                + jnp.dot(dv_ref[h].astype(BF16), sv_ref[...], preferred_element_type=F32)).astype(BF16)
        dkr_ref[...] = _rot(_exact_perm(total, skr_ref[...]), c_ref[...], s_ref[...], p_ref[...], True)

    rows = lambda c: pl.BlockSpec((TB, c), lambda i, j: (i, 0))
    const = lambda a: pl.BlockSpec(a.shape, lambda i, j: (0, 0))
    return _ride_call(
        body, (T // TB, 1),
        [pl.BlockSpec((H, TB, QK), lambda i, j: (0, i, 0)), pl.BlockSpec((H, TB, VD), lambda i, j: (0, i, 0)),
         rows(128), rows(128), const(pmt), const(s_knt), const(s_krt), const(s_vt)],
        [rows(H * KVW), rows(128)],
        [jax.ShapeDtypeStruct((T, H * KVW), BF16), jax.ShapeDtypeStruct((T, 128), F32)],
        Exchange(rode, modes) if rode else None, rode, name, (dk, dv, cosf, sinf, pmt, s_knt, s_krt, s_vt))


HB = 4
HBF = 8


def _by_query_block(run, T):
    @pl.when(pl.program_id(1) == 0)
    def _():
        run(LC)

    @pl.when(pl.program_id(1) > 0)
    def _():
        run(T)


def _with_rider(body, nin, nout, ride, grid):
    if ride is None:
        return body
    n = ride.n

    def wrapped(*refs):
        ins, xs = refs[:nin], refs[nin:nin + n]
        outs, got = refs[nin + n:nin + n + nout], refs[nin + n + nout:nin + 2 * n + nout]
        sems = refs[nin + 2 * n + nout:]
        step = pl.program_id(0) * grid[1] + pl.program_id(1)

        @pl.when(step == 0)
        def _():
            ride.start(xs, got, sems)

        body(*ins, *outs)

        @pl.when(step == grid[0] * grid[1] - 1)
        def _():
            ride.finish(xs, got, sems)

    return wrapped


def _ride_call(body, grid, in_specs, out_specs, out_shape, ride, rode, name, args):
    if ride is None:
        return pl.pallas_call(body, grid=grid, in_specs=in_specs, out_specs=out_specs, out_shape=out_shape,
                              compiler_params=_cp(("parallel", "arbitrary")), name=name)(*args), []
    res = pl.pallas_call(
        _with_rider(body, len(in_specs), len(out_specs), ride, grid), grid=grid,
        in_specs=in_specs + ride.specs, out_specs=out_specs + ride.specs, out_shape=out_shape + ride.out_shape,
        scratch_shapes=ride.scratch,
        compiler_params=pltpu.CompilerParams(dimension_semantics=("arbitrary", "arbitrary"), vmem_limit_bytes=VMEM_LIMIT,
                                             has_side_effects=True), name=name)(*args, *rode)
    return res[:len(out_specs)], res[len(out_specs):]


def attn_fwd(q, k, v, name, rode=None, modes=None):
    H, T, _ = q.shape

    def body(q_ref, k_ref, v_ref, o_ref, lse_ref):
        def run(nk):
            for hh in range(HBF):
                s = _dotf(q_ref[hh], k_ref[hh, pl.ds(0, nk), :], "nt")
                m = jnp.max(s, axis=1, keepdims=True)
                p = jnp.exp2(s - m)
                l = jnp.sum(p, axis=1, keepdims=True)
                o = jnp.dot(p.astype(BF16), v_ref[hh, pl.ds(0, nk), :], preferred_element_type=F32)
                o_ref[hh] = o / l
                lse_ref[hh] = m + jnp.log2(l)

        _by_query_block(run, T)

    return _ride_call(
        body, (H // HBF, T // TB),
        [pl.BlockSpec((HBF, TB, QK), lambda h, i: (h, i, 0)), pl.BlockSpec((HBF, T, QK), lambda h, i: (h, 0, 0)),
         pl.BlockSpec((HBF, T, VD), lambda h, i: (h, 0, 0))],
        [pl.BlockSpec((HBF, TB, VD), lambda h, i: (h, i, 0)), pl.BlockSpec((HBF, TB, 1), lambda h, i: (h, i, 0))],
        [jax.ShapeDtypeStruct((H, T, VD), F32), jax.ShapeDtypeStruct((H, T, 1), F32)],
        Exchange(rode, modes) if rode else None, rode, name, (q, k, v))


def attn_bwd(q, k, v, o, lse, do, name, rode=None, modes=None):
    H, T, _ = q.shape

    def body(q_ref, k_ref, v_ref, o_ref, lse_ref, do_ref, dq_ref, dk_ref, dv_ref):
        i = pl.program_id(1)

        @pl.when(i == 0)
        def _():
            dk_ref[...] = jnp.zeros_like(dk_ref)
            dv_ref[...] = jnp.zeros_like(dv_ref)

        def run(nk):
            keys = pl.ds(0, nk)
            for hh in range(HB):
                qv, kv, dov = q_ref[hh], k_ref[hh, keys, :], do_ref[hh]
                p = jnp.exp2(_dotf(qv, kv, "nt") - lse_ref[hh])
                delta = jnp.sum(dov * o_ref[hh], axis=1, keepdims=True)
                dob = dov.astype(BF16)
                dv_ref[hh, keys, :] += _dotf(p.astype(BF16), dob, "tn")
                dp = _dotf(dob, v_ref[hh, keys, :], "nt")
                ds = (p * (dp - delta)).astype(BF16)
                dq_ref[hh] = jnp.dot(ds, kv, preferred_element_type=F32)
                dk_ref[hh, keys, :] += _dotf(ds, qv, "tn")

        _by_query_block(run, T)

    blk = lambda c: pl.BlockSpec((HB, TB, c), lambda h, i: (h, i, 0))
    full = lambda c: pl.BlockSpec((HB, T, c), lambda h, i: (h, 0, 0))
    return _ride_call(
        body, (H // HB, T // TB), [blk(QK), full(QK), full(VD), blk(VD), blk(1), blk(VD)], [blk(QK), full(QK), full(VD)],
        [jax.ShapeDtypeStruct((H, T, QK), F32), jax.ShapeDtypeStruct((H, T, QK), F32), jax.ShapeDtypeStruct((H, T, VD), F32)],
        Exchange(rode, modes) if rode else None, rode, name, (q, k, v, o, lse, do))


def disc_fwd(a_re, a_im, ls, name):
    def body(ar_ref, ai_ref, ls_ref, lr_ref, li_ref, fr_ref, fi_ref):
        ar, ai = ar_ref[...], ai_ref[...]
        dt = jnp.exp(ls_ref[...])
        mag = jnp.exp(ar * dt)
        lr = mag * jnp.cos(ai * dt)
        li = mag * jnp.sin(ai * dt)
        den = ar * ar + ai * ai
        nr = lr - 1.0
        lr_ref[...] = lr
        li_ref[...] = li
        fr_ref[...] = (nr * ar + li * ai) / den
        fi_ref[...] = (li * ar - nr * ai) / den

    return pl.pallas_call(body, out_shape=[jax.ShapeDtypeStruct(a_re.shape, F32)] * 4, name=name)(a_re, a_im, ls)


def disc_b(f_re, f_im, b_re, b_im, name):
    def body(fr_ref, fi_ref, br_ref, bi_ref, or_ref, oi_ref):
        fr, fi, br, bi = fr_ref[...], fi_ref[...], br_ref[...], bi_ref[...]
        or_ref[...] = fr * br - fi * bi
        oi_ref[...] = fr * bi + fi * br

    return pl.pallas_call(body, out_shape=[jax.ShapeDtypeStruct(b_re.shape, F32)] * 2, compiler_params=_cp(),
                          name=name)(f_re, f_im, b_re, b_im)


def disc_b_bwd(f_re, f_im, b_re, b_im, dbb_re, dbb_im, name):
    def body(fr_ref, fi_ref, br_ref, bi_ref, dr_ref, di_ref, dbr_ref, dbi_ref, dfr_ref, dfi_ref):
        fr, fi, br, bi, dr, di = fr_ref[...], fi_ref[...], br_ref[...], bi_ref[...], dr_ref[...], di_ref[...]
        dbr_ref[...] = fr * dr + fi * di
        dbi_ref[...] = fr * di - fi * dr
        dfr_ref[...] = jnp.sum(dr * br + di * bi, axis=2, keepdims=True)
        dfi_ref[...] = jnp.sum(di * br - dr * bi, axis=2, keepdims=True)

    return pl.pallas_call(body, out_shape=[jax.ShapeDtypeStruct(b_re.shape, F32)] * 2 + [jax.ShapeDtypeStruct(f_re.shape, F32)] * 2,
                          compiler_params=_cp(), name=name)(f_re, f_im, b_re, b_im, dbb_re, dbb_im)


def disc_a_bwd(a_re, a_im, ls, dlr, dli, dfr, dfi, name):
    def body(ar_ref, ai_ref, ls_ref, dlr_ref, dli_ref, dfr_ref, dfi_ref, dar_ref, dai_ref, dls_ref):
        ar, ai = ar_ref[...], ai_ref[...]
        dt = jnp.exp(ls_ref[...])
        mag = jnp.exp(ar * dt)
        cs, sn = jnp.cos(ai * dt), jnp.sin(ai * dt)
        lr, li = mag * cs, mag * sn
        den = ar * ar + ai * ai
        nr = lr - 1.0
        f_re = (nr * ar + li * ai) / den
        f_im = (li * ar - nr * ai) / den
        dn1 = dfr_ref[...] / den
        dn2 = dfi_ref[...] / den
        dden = -(dfr_ref[...] * f_re + dfi_ref[...] * f_im) / den
        dlr_t = dlr_ref[...] + dn1 * ar - dn2 * ai
        dli_t = dli_ref[...] + dn1 * ai + dn2 * ar
        dar = dn1 * nr + dn2 * li + dden * 2.0 * ar
        dai = dn1 * li - dn2 * nr + dden * 2.0 * ai
        dmag = dlr_t * cs + dli_t * sn
        dth = dli_t * lr - dlr_t * li
        dar_ref[...] = dar + dmag * mag * dt
        dai_ref[...] = dai + dth * dt
        dls_ref[...] = jnp.sum(dmag * mag * ar + dth * ai, axis=-1, keepdims=True) * dt

    return pl.pallas_call(body, out_shape=[jax.ShapeDtypeStruct(a_re.shape, F32)] * 2 +
                          [jax.ShapeDtypeStruct(ls.shape, F32)], name=name)(a_re, a_im, ls, dlr, dli, dfr, dfi)


def _cpow(lr, li, n):
    rr, ri = None, None
    br, bi = lr, li
    while n:
        if n & 1:
            if rr is None:
                rr, ri = br, bi
            else:
                rr, ri = rr * br - ri * bi, rr * bi + ri * br
        n >>= 1
        if n:
            br, bi = br * br - bi * bi, 2.0 * br * bi
    return rr, ri


UNROLL = 4


def _steps(trips, fn, init):
    main = trips // UNROLL

    def body(i, c):
        for j in range(UNROLL):
            c = fn(i * UNROLL + j, c)
        return c

    c = lax.fori_loop(0, main, body, init) if main else init
    for n in range(main * UNROLL, trips):
        c = fn(n, c)
    return c


def _seg_scan(xre, xim, lam8, pw, base, seglen, rev, init, fin_re, fin_im, ini_re, ini_im, prev=None):
    lr, li = lam8
    nsub = SEG // 8

    def rows(t, s):
        first = base + t * SEG + 8 * s
        return pl.ds(first if isinstance(first, int) else pl.multiple_of(first, 8), 8)

    tmap = (lambda n: seglen - 1 - n) if rev else (lambda n: n)
    zeros = tuple(jnp.zeros((8, SB), F32) for _ in range(2 * nsub))

    def advance(c, t):
        out = []
        for s in range(nsub):
            a, b = c[2 * s], c[2 * s + 1]
            out += [lr * a - li * b + xre[rows(t, s), :], lr * b + li * a + xim[rows(t, s), :]]
        return tuple(out)

    fin = _steps(seglen, lambda n, c: advance(c, tmap(n)), zeros)
    for s in range(nsub):
        fin_re[pl.ds(8 * s, 8), :] = fin[2 * s]
        fin_im[pl.ds(8 * s, 8), :] = fin[2 * s + 1]
    (cr, ci), (pr, pi) = init, pw
    for i in (range(SEG - 1, -1, -1) if rev else range(SEG)):
        ini_re[pl.ds(i, 1), :] = cr
        ini_im[pl.ds(i, 1), :] = ci
        cr, ci = pr * cr - pi * ci + fin_re[pl.ds(i, 1), :], pr * ci + pi * cr + fin_im[pl.ds(i, 1), :]
    tiles = lambda re, im: tuple(r[pl.ds(8 * s, 8), :] for s in range(nsub) for r in (re, im))
    start = tiles(ini_re, ini_im)

    def store(c, t):
        new = advance(c, t)
        for s in range(nsub):
            xre[rows(t, s), :] = new[2 * s]
            xim[rows(t, s), :] = new[2 * s + 1]
        return new

    if prev is None:
        _steps(seglen, lambda n, c: store(c, tmap(n)), start)
        return (cr, ci), None

    sre, sim, s_ini_re, s_ini_im = prev

    def acc_step(c, t, before):
        new = store(c[:2 * nsub], t)
        acc = []
        for s in range(nsub):
            (na, nb), (pre, pim) = new[2 * s:2 * s + 2], before[2 * s:2 * s + 2]
            acc += [c[2 * nsub + 2 * s] + na * pre + nb * pim, c[2 * nsub + 2 * s + 1] + nb * pre - na * pim]
        return new + tuple(acc)

    def body(n, c):
        t = tmap(n)
        tp = t - 1 if rev else t + 1
        return acc_step(c, t, tuple(r[rows(tp, s), :] for s in range(nsub) for r in (sre, sim)))

    c = _steps(seglen - 1, body, start + zeros)
    c = acc_step(c, 0 if rev else seglen - 1, tiles(s_ini_re, s_ini_im))
    acc = c[2 * nsub:]
    return (cr, ci), (sum(acc[0::2][1:], acc[0]), sum(acc[1::2][1:], acc[1]))


def _lam_tiles(lr, li, lens, conj=False):
    if conj:
        li = -li
    lam8 = (jnp.broadcast_to(lr, (8, SB)), jnp.broadcast_to(li, (8, SB)))
    return lam8, [_cpow(lr, li, n) for n in lens]


def _stretches(T):
    return ((0, LC // SEG), (LC, (T - LC) // SEG))


def _to_seg_order(src, dst, T):
    for base, seglen in _stretches(T):
        def body(t, carry, base=base, seglen=seglen):
            dst[pl.ds(pl.multiple_of(base + t * SEG, SEG), SEG), :] = src[pl.ds(base + t, SEG, stride=seglen), :]
            return carry
        lax.fori_loop(0, seglen, body, 0, unroll=8)


def _from_seg_order(src, dst, T):
    for base, seglen in _stretches(T):
        def body(t, carry, base=base, seglen=seglen):
            dst[pl.ds(base + t, SEG, stride=seglen), :] = src[pl.ds(pl.multiple_of(base + t * SEG, SEG), SEG), :]
            return carry
        lax.fori_loop(0, seglen, body, 0, unroll=8)


def _scan_specs(T):
    ublk = pl.BlockSpec((T, UB), lambda j: (0, j))
    lam = pl.BlockSpec((2, 1, 1, SB), lambda j: (0, j, 0, 0))
    mat = pl.BlockSpec((2, 1, UB, P), lambda j: (0, j, 0, 0))
    return ublk, lam, mat


def _dotf(a, b, mode="nn"):
    return lax.dot_general(a, b, _DN[mode], preferred_element_type=F32)


def _diag_mask():
    r = lax.broadcasted_iota(jnp.int32, (UB, SB), 0)
    c = lax.broadcasted_iota(jnp.int32, (UB, SB), 1)
    return lax.shift_right_logical(r, int(math.log2(CH))) == lax.shift_right_logical(c, int(math.log2(P)))


def _expand(m):
    p = lax.broadcasted_iota(jnp.int32, (P, SB), 0)
    c = lax.broadcasted_iota(jnp.int32, (P, SB), 1)
    tile = jnp.where(lax.bitwise_and(c, P - 1) == p, 1.0, 0.0).astype(BF16)
    wide = jnp.dot(m.astype(BF16), tile, preferred_element_type=F32)
    return jnp.where(_diag_mask(), wide, 0.0).astype(BF16)


def _collapse(full):
    c = lax.broadcasted_iota(jnp.int32, (SB, P), 0)
    p = lax.broadcasted_iota(jnp.int32, (SB, P), 1)
    pick = jnp.where(lax.bitwise_and(c, P - 1) == p, 1.0, 0.0).astype(BF16)
    return _exact_perm(jnp.where(_diag_mask(), full, 0.0), pick)


def _zero_state():
    return jnp.zeros((1, SB), F32), jnp.zeros((1, SB), F32)


def scan_fwd(u, lam_re, lam_im, bre, bim, cre, cim, name):
    T = u.shape[0]
    s_ctx, s_lat = LC // SEG, (T - LC) // SEG

    def body(u_ref, lr_ref, li_ref, bre_ref, bim_ref, cre_ref, cim_ref, y_ref, us, ys, sre, sim, fre, fim, ire, iim):
        _to_seg_order(u_ref, us, T)
        ub = us[...].astype(BF16)
        for d in range(2):
            lam8, (pw_c, pw_l) = _lam_tiles(lr_ref[d, 0], li_ref[d, 0], (s_ctx, s_lat))
            sre[...] = _dotf(ub, _expand(bre_ref[d, 0]))
            sim[...] = _dotf(ub, _expand(bim_ref[d, 0]))
            end_c, _ = _seg_scan(sre, sim, lam8, pw_c, 0, s_ctx, bool(d), _zero_state(), fre, fim, ire, iim)
            _seg_scan(sre, sim, lam8, pw_l, LC, s_lat, bool(d), end_c, fre, fim, ire, iim)
            y = (_dotf(sre[...].astype(BF16), _expand(cre_ref[d, 0]), "nt")
                 - _dotf(sim[...].astype(BF16), _expand(cim_ref[d, 0]), "nt"))
            if d == 0:
                ys[...] = y
            else:
                ys[...] += y
        _from_seg_order(ys, y_ref, T)

    ublk, lam, mat = _scan_specs(T)
    return pl.pallas_call(
        body, grid=(NJ,), in_specs=[ublk, lam, lam, mat, mat, mat, mat], out_specs=ublk,
        out_shape=jax.ShapeDtypeStruct((T, G * CH), F32),
        scratch_shapes=[pltpu.VMEM((T, UB), F32)] * 2 + [pltpu.VMEM((T, SB), F32)] * 2 + [pltpu.VMEM((SEG, SB), F32)] * 4,
        compiler_params=_cp(("arbitrary",)), name=name)(u, lam_re, lam_im, bre, bim, cre, cim)


def scan_bwd(u, dy, lam_re, lam_im, bre, bim, cre, cim, name):
    T = u.shape[0]
    s_ctx, s_lat = LC // SEG, (T - LC) // SEG

    def body(u_ref, dy_ref, lr_ref, li_ref, bre_ref, bim_ref, cre_ref, cim_ref,
             du_ref, dlr_ref, dli_ref, dbre_ref, dbim_ref, dcre_ref, dcim_ref,
             us, dys, dus, sre, sim, gre, gim, fre, fim, ic_re, ic_im, il_re, il_im, jre, jim):
        _to_seg_order(u_ref, us, T)
        _to_seg_order(dy_ref, dys, T)
        ub, dyb = us[...].astype(BF16), dys[...].astype(BF16)
        for d in range(2):
            rev = bool(d)
            lam8, (pw_c, pw_l) = _lam_tiles(lr_ref[d, 0], li_ref[d, 0], (s_ctx, s_lat))
            cam8, (cw_c, cw_l) = _lam_tiles(lr_ref[d, 0], li_ref[d, 0], (s_ctx, s_lat), conj=True)
            bre_v, bim_v = _expand(bre_ref[d, 0]), _expand(bim_ref[d, 0])
            sre[...] = _dotf(ub, bre_v)
            sim[...] = _dotf(ub, bim_v)
            end_c, _ = _seg_scan(sre, sim, lam8, pw_c, 0, s_ctx, rev, _zero_state(), fre, fim, ic_re, ic_im)
            _seg_scan(sre, sim, lam8, pw_l, LC, s_lat, rev, end_c, fre, fim, il_re, il_im)
            gre[...] = _dotf(dyb, _expand(cre_ref[d, 0]))
            gim[...] = -_dotf(dyb, _expand(cim_ref[d, 0]))
            end_g, acc_l = _seg_scan(gre, gim, cam8, cw_l, LC, s_lat, not rev, _zero_state(), fre, fim, jre, jim,
                                     prev=(sre, sim, il_re, il_im))
            _, acc_c = _seg_scan(gre, gim, cam8, cw_c, 0, s_ctx, not rev, end_g, fre, fim, jre, jim,
                                 prev=(sre, sim, ic_re, ic_im))
            dlr_ref[d, 0] = _sum0(acc_l[0] + acc_c[0])
            dli_ref[d, 0] = _sum0(acc_l[1] + acc_c[1])
            grb, gib = gre[...].astype(BF16), gim[...].astype(BF16)
            du = _dotf(grb, bre_v, "nt") + _dotf(gib, bim_v, "nt")
            if d == 0:
                dus[...] = du
            else:
                dus[...] += du
            dbre_ref[d, 0] = _collapse(_dotf(ub, grb, "tn"))
            dbim_ref[d, 0] = _collapse(_dotf(ub, gib, "tn"))
            dcre_ref[d, 0] = _collapse(_dotf(dyb, sre[...].astype(BF16), "tn"))
            dcim_ref[d, 0] = -_collapse(_dotf(dyb, sim[...].astype(BF16), "tn"))
        _from_seg_order(dus, du_ref, T)

    ublk, lam, mat = _scan_specs(T)
    lam_s = jax.ShapeDtypeStruct(lam_re.shape, F32)
    mat_s = jax.ShapeDtypeStruct(bre.shape, F32)
    return pl.pallas_call(
        body, grid=(NJ,), in_specs=[ublk, ublk, lam, lam, mat, mat, mat, mat],
        out_specs=[ublk, lam, lam, mat, mat, mat, mat],
        out_shape=[jax.ShapeDtypeStruct((T, G * CH), F32), lam_s, lam_s, mat_s, mat_s, mat_s, mat_s],
        scratch_shapes=[pltpu.VMEM((T, UB), F32)] * 3 + [pltpu.VMEM((T, SB), F32)] * 4 + [pltpu.VMEM((SEG, SB), F32)] * 8,
        compiler_params=_cp(("arbitrary",)), name=name)(u, dy, lam_re, lam_im, bre, bim, cre, cim)


class Exchange:
    def __init__(self, xs, modes):
        self.n = len(xs)
        self.modes = [modes] * self.n if isinstance(modes, (str, int)) else list(modes)
        self.out_shape = [jax.ShapeDtypeStruct(self._shape(x, md), x.dtype) for x, md in zip(xs, self.modes)]
        self.scratch = [pltpu.SemaphoreType.DMA((NDEV - 1, self.n)), pltpu.SemaphoreType.DMA((NDEV - 1, self.n)),
                        pltpu.SemaphoreType.DMA((self.n,))]
        self.specs = [pl.BlockSpec(memory_space=pl.ANY)] * self.n

    @staticmethod
    def _shape(x, mode):
        if mode == "gather":
            return (NDEV,) + tuple(x.shape)
        return tuple(x.shape) if mode == "lead" else (NDEV, x.shape[0], mode) + tuple(x.shape[2:])

    @staticmethod
    def _piece(x_ref, mode, dev):
        if mode == "gather":
            return x_ref
        return x_ref.at[dev] if mode == "lead" else x_ref.at[:, pl.ds(dev * mode, mode)]

    def _copies(self, x_refs, out_refs, sems):
        send_sems, recv_sems, local_sems = sems
        mx, my, mc = lax.axis_index("x"), lax.axis_index("y"), lax.axis_index("c")
        me = 4 * mx + 2 * my + mc
        peer_of = lambda k: (1 - mx if k & 4 else mx, 1 - my if k & 2 else my, 1 - mc if k & 1 else mc)
        local, first, relay, arrivals = [], [], [], []
        for a, (x_ref, out_ref) in enumerate(zip(x_refs, out_refs)):
            mode = self.modes[a]
            local.append(pltpu.make_async_copy(self._piece(x_ref, mode, me), out_ref.at[me], local_sems.at[a]))

            def remote(src, dst, k, pair, a=a):
                return pltpu.make_async_remote_copy(src_ref=src, dst_ref=dst, send_sem=send_sems.at[pair, a],
                                                    recv_sem=recv_sems.at[pair, a], device_id=peer_of(k), device_id_type=MESH_T)

            for k in range(1, NDEV):
                peer = peer_of(k)
                pid = 4 * peer[0] + 2 * peer[1] + peer[2]
                if mode != "gather":
                    src = self._piece(x_ref, mode, pid)
                    first.append(remote(src, out_ref.at[me], k, k - 1))
                    arrivals.append(remote(src, out_ref.at[pid], k, k - 1))
                elif k == 1:
                    first.append(remote(x_ref, out_ref.at[me], k, k - 1))
                    arrivals.append(remote(x_ref, out_ref.at[pid], k, k - 1))
                elif k % 2 == 0:
                    first.append(remote(x_ref, out_ref.at[me], k, k - 1))
                    relay.append((remote(x_ref, out_ref.at[pid], k, k - 1), remote(out_ref.at[pid], out_ref.at[pid], 1, k)))
                else:
                    arrivals.append(remote(x_ref, out_ref.at[pid], 1, k - 1))
        return local, first, relay, arrivals

    def start(self, x_refs, out_refs, sems):
        local, first, _, _ = self._copies(x_refs, out_refs, sems)
        for cp in local + first:
            cp.start()

    def finish(self, x_refs, out_refs, sems):
        local, first, relay, arrivals = self._copies(x_refs, out_refs, sems)
        for arrival, onward in relay:
            arrival.wait_recv()
            onward.start()
        for cp in arrivals:
            cp.wait_recv()
        for cp in first + [onward for _, onward in relay]:
            cp.wait_send()
        for cp in local:
            cp.wait()


def exchange(xs, modes, name):
    ex = Exchange(xs, modes)
    n = ex.n

    def body(*refs):
        ex.start(refs[:n], refs[n:2 * n], refs[2 * n:])
        ex.finish(refs[:n], refs[n:2 * n], refs[2 * n:])

    return pl.pallas_call(body, in_specs=ex.specs, out_specs=ex.specs, out_shape=ex.out_shape, scratch_shapes=ex.scratch,
                          compiler_params=pltpu.CompilerParams(has_side_effects=True), name=name)(*xs)


def _dot_f32(a, b, dn):
    return lax.dot_general(a, b, dn, preferred_element_type=F32, precision=lax.Precision.HIGHEST)


def ada_fwd(cg, c_ctx, ada_w, ada_b_loc, name):
    W = ada_w.shape[2]

    def body(cg_ref, cc_ref, w_ref, b_ref, o_ref):
        a = jnp.concatenate([_silu(cg_ref[...]), jnp.broadcast_to(_silu(cc_ref[...]), (NDEV, D))], axis=0)
        for i in range(2):
            o_ref[i] = _dot_f32(a, w_ref[i], _DN["nn"]) + b_ref[i]

    return pl.pallas_call(body, out_shape=jax.ShapeDtypeStruct((2, 2 * NDEV, W), F32),
                          compiler_params=_cp(), name=name)(cg, c_ctx, ada_w, ada_b_loc)


def ada_bwd(cg, c_ctx, ada_w, dm_loc, dm_all, name):
    W = ada_w.shape[2]

    def body(cg_ref, cc_ref, w_ref, dl_ref, da_ref, gw_ref, dcc_ref, gb_ref):
        a = jnp.concatenate([_silu(cg_ref[...]), jnp.broadcast_to(_silu(cc_ref[...]), (NDEV, D))], axis=0)
        dcc = jnp.zeros((1, D), F32)
        for i in range(2):
            dl = dl_ref[i]
            gw_ref[i] = _dot_f32(a, dl, _DN["tn"])
            dctx = jnp.sum(dl[NDEV:], axis=0, keepdims=True)
            dcc = dcc + _dot_f32(dctx, w_ref[i], _DN["nt"])
        dcc_ref[...] = dcc
        gb_ref[...] = jnp.sum(da_ref[...], axis=0)

    return pl.pallas_call(body, out_shape=[jax.ShapeDtypeStruct((2, D, W), F32), jax.ShapeDtypeStruct((1, D), F32),
                                           jax.ShapeDtypeStruct((2, 3 * D), F32)],
                          compiler_params=_cp(), name=name)(cg, c_ctx, ada_w, dm_loc, dm_all)


def cctx_finish(parts, c_ctx, name):
    def body(p_ref, cc_ref, o_ref):
        o_ref[...] = jnp.sum(p_ref[...], axis=0, keepdims=True) * _dsilu(cc_ref[...])

    return pl.pallas_call(body, out_shape=jax.ShapeDtypeStruct((1, D), F32), name=name)(parts, c_ctx)


def _adamw_update(g_ref, w_ref, m_ref, v_ref, go_ref, d_ref, mo_ref, vo_ref):
    g = g_ref[0].astype(F32)
    for s in range(1, g_ref.shape[0]):
        g = g + g_ref[s].astype(F32)
    mn = B1 * m_ref[...] + (1.0 - B1) * g
    vn = B2 * v_ref[...] + (1.0 - B2) * g * g
    go_ref[...] = g
    mo_ref[...] = mn
    vo_ref[...] = vn
    d_ref[...] = -LR * ((mn * (1.0 / (1.0 - B1 ** STEP))) / (jnp.sqrt(vn * (1.0 / (1.0 - B2 ** STEP))) + AEPS) + WD * w_ref[...])


ADAMW_PARTS = 4


def adamw_rows(items, name, rode=None, modes=None):
    in_specs, out_specs, out_shape, args = [], [], [], []
    for g, w, m, v in items:
        n, R, C = g.shape
        tr = R // ADAMW_PARTS
        spec = pl.BlockSpec((tr, C), lambda i, j: (i, 0))
        in_specs += [pl.BlockSpec((n, tr, C), lambda i, j: (0, i, 0)), spec, spec, spec]
        args += [g, w, m, v]
    for g, w, m, v in items:
        tr = w.shape[0] // ADAMW_PARTS
        out_specs += [pl.BlockSpec((tr, w.shape[1]), lambda i, j: (i, 0))] * 4
        out_shape += [jax.ShapeDtypeStruct(w.shape, F32)] * 4
    res, got = _ride_call(_adamw_body(len(items)), (ADAMW_PARTS, 1), in_specs, out_specs, out_shape,
                          Exchange(rode, modes) if rode else None, rode, name, args)
    return [res[4 * t:4 * t + 4] for t in range(len(items))], got


def _adamw_body(k):
    def body(*refs):
        for t in range(k):
            _adamw_update(*refs[4 * t:4 * t + 4], *refs[4 * k + 4 * t:4 * k + 4 * t + 4])
    return body


def adamw_multi(items, grid, name):
    k = len(items)
    ins, in_specs, out_specs, out_shape = [], [], [], []
    for g, g_spec, w, m, v, w_spec in items:
        ins += [g, w, m, v]
        in_specs += [g_spec, w_spec, w_spec, w_spec]
    for g, g_spec, w, m, v, w_spec in items:
        out_specs += [w_spec] * 4
        out_shape += [jax.ShapeDtypeStruct(w.shape, F32)] * 4
    res = pl.pallas_call(_adamw_body(k), grid=grid, in_specs=in_specs, out_specs=out_specs, out_shape=out_shape,
                         compiler_params=_cp(("arbitrary",) * len(grid)), name=name)(*ins)
    return [res[4 * t:4 * t + 4] for t in range(k)]


def _whole(a, grid_rank):
    zeros = (0,) * a.ndim
    return pl.BlockSpec(a.shape, lambda *idx: zeros)


def sum_slots(xs, name):
    def body(*refs):
        for x_ref, o_ref in zip(refs[:len(xs)], refs[len(xs):]):
            acc = x_ref[0]
            for s in range(1, NDEV):
                acc = acc + x_ref[s]
            o_ref[...] = acc

    return pl.pallas_call(body, out_shape=[jax.ShapeDtypeStruct(x.shape[1:], F32) for x in xs],
                          compiler_params=_cp(), name=name)(*xs)


def _col_shards(g):
    R, N = g.shape
    return g.reshape(R, NDEV, N // NDEV).transpose(1, 0, 2)


def _vec2(v):
    return jnp.broadcast_to(v.reshape(1, 1, -1), (2, 1, v.size))


SHARD_ROWS = {"mla_w_in": 192, "mla_w_uq": 192, "mla_w_ukv": 256, "s5_w_in": 256}


def _t_shard(wsh, rows):
    t = wsh[0].T.astype(BF16)
    return jnp.pad(t, ((0, rows - t.shape[0]), (0, 0)))


def _win_order():
    w = IN_W // NDEV
    perm = np.zeros((IN_WP, NDEV * SHARD_ROWS["mla_w_in"]), np.float32)
    first = QL + KVL + ROPE
    for c in range(IN_W):
        n = c + HEADS * VD if c < first else c - first
        perm[n, (c // w) * SHARD_ROWS["mla_w_in"] + c % w] = 1.0
    return jnp.asarray(perm, BF16)


def local_step(ctx, x, tgt, mod, Wt, small, l1_shards):
    T = LC + x.shape[0]
    xa = ("cat", ctx, x)
    sh = [mod[i, :, None, 0:D] for i in range(2)]
    sc = [mod[i, :, None, D:2 * D] for i in range(2)]
    gt = [mod[i, :, None, 2 * D:] for i in range(2)]
    ng = [_vec2(small["norm_g"][i]) for i in range(2)]
    qg, kvg = _vec2(small["mla_q_norm"]), _vec2(small["mla_kv_norm"])
    cosf, sinf, pm, pmt = _rope_tables(T)

    (h0, p0, cqn, ckvn), _ = rowwise(st_l0_pre, [xa], [ng[0], sc[0], sh[0], qg, kvg],
                                     [(D, BF16), (IN_WP, F32), (QL, BF16), (KVL, BF16)], [], "l0_pre", mats=[Wt["mla_w_in"]])
    z0, cq, ckv = (p0, 0, HEADS * VD), (p0, HEADS * VD // QL, QL), (p0, (HEADS * VD + QL) // KVL, KVL)
    Q = project_q(cqn, Wt["mla_w_uq"], "l0_uq")
    K, V = project_kv(ckvn, Wt["mla_w_ukv"], p0, (HEADS * VD + QL + KVL) // 128, "l0_ukv")
    (o, lse), got = attn_fwd(Q, K, V, "l0_attn", rode=l1_shards, modes="gather")
    Wt, small = dict(Wt), dict(small)
    for n, a in zip(L1_BIG, got):
        Wt[n] = a.reshape(-1, a.shape[-1])
    vecs = lax.bitcast_convert_type(got[-1].reshape(NDEV, 2, -1, 2), F32)
    small["s5_d"], small["s5_b_glu"] = vecs[:, 0, :].reshape(D), vecs[:, 1, :].reshape(D)
    o2 = o.transpose(1, 0, 2).reshape(T, HEADS * VD)
    (og, out0, x1), _ = rowwise(st_l0_post, [o2, z0, xa], [gt[0]], [(D, BF16), (D, BF16), (D, F32)], [], "l0_post",
                                mats=[Wt["mla_w_out"]])

    ls = small["s5_log_step"].reshape(2, G, 1)
    a_re, a_im = small["s5_a_re"].reshape(2, G, P), small["s5_a_im"].reshape(2, G, P)
    b_re = small["s5_b_re"].reshape(2, G, P, CH).transpose(0, 1, 3, 2)
    b_im = small["s5_b_im"].reshape(2, G, P, CH).transpose(0, 1, 3, 2)
    lam_re, lam_im, f_re, f_im = disc_fwd(a_re, a_im, ls, "s5_disc")
    f_re2, f_im2 = f_re.reshape(2, G, 1, P), f_im.reshape(2, G, 1, P)
    bb_re, bb_im = disc_b(f_re2, f_im2, b_re, b_im, "s5_disc_b")
    compact = lambda m: m.reshape(2, NJ, UB, P)
    bre, bim = compact(bb_re), compact(bb_im)
    cre, cim = compact(small["s5_c_re"]), compact(small["s5_c_im"])
    lam_re4, lam_im4 = lam_re.reshape(2, NJ, 1, SB), lam_im.reshape(2, NJ, 1, SB)

    (h1, p1), _ = rowwise(st_l1_pre, [x1], [ng[1], sc[1], sh[1]], [(D, BF16), (2 * D, F32)], [], "l1_pre", mats=[Wt["s5_w_in"]])
    u, z1 = (p1, 0, D), (p1, 1, D)
    yssm = scan_fwd(p1, lam_re4, lam_im4, bre, bim, cre, cim, "s5_scan")
    dvec, bglu = _vec2(small["s5_d"]), _vec2(small["s5_b_glu"])
    fg = _vec2(small["final_g"])
    lat_mask = jnp.stack([jnp.zeros((1, D), F32), jnp.ones((1, D), F32)])
    (y, y1b, gl, y3, out1, dx2), (dfg, lvec) = rowwise(
        st_l1_mlp, [yssm, u, z1, x1, ("lat", tgt)], [dvec, bglu, gt[1], fg, lat_mask],
        [(D, F32), (D, BF16), (D, BF16), (D, BF16), (D, BF16), (D, F32)], [D, 128], "l1_mlp",
        mats=[Wt["s5_w_glu"], Wt["s5_w_out"]])

    (dz1, dy, du_d), (dgt1, dbglu, dd), (g_w_out5, g_w_glu) = rowwise(
        st_l1_mlp_bwd, [dx2, out1, y3, y, gl, z1, u, y1b], [gt[1], bglu, dvec], [(D, BF16), (D, F32), (D, F32)], [D, D, D],
        "l1_mlp_b", mats=[Wt["s5_w_out"], Wt["s5_w_glu"]], out_accs=[(D, D), (D, D)])
    du_s, dlr, dli, dbre, dbim, dcre, dcim = scan_bwd(p1, dy, lam_re4, lam_im4, bre, bim, cre, cim, "s5_scan_b")
    dbb_re, dbb_im = dbre.reshape(2, G, CH, P), dbim.reshape(2, G, CH, P)
    g_c_re, g_c_im = dcre.reshape(2, G, CH, P), dcim.reshape(2, G, CH, P)
    gt_b_re, gt_b_im, dfr, dfi = disc_b_bwd(f_re2, f_im2, b_re, b_im, dbb_re, dbb_im, "s5_disc_b_b")
    g_b_re, g_b_im = gt_b_re.transpose(0, 1, 3, 2), gt_b_im.transpose(0, 1, 3, 2)
    g_a_re, g_a_im, g_ls = disc_a_bwd(a_re, a_im, ls, dlr.reshape(2, G, P), dli.reshape(2, G, P),
                                      dfr.reshape(2, G, P), dfi.reshape(2, G, P), "s5_disc_b_a")
    (dx1,), (dsh1, dsc1, dng1), (g_w_in5,) = rowwise(
        st_l1_tail_bwd, [du_d, du_s, dz1, h1, x1, dx2], [ng[1], sc[1]], [(D, F32)], [D, D, D], "l1_pre_b",
        mats=[Wt["s5_w_in"]], out_accs=[(NDEV, D, 2 * D // NDEV)])

    (do2, dz0), (dgt0,), (g_w_out,) = rowwise(st_l0_post_bwd, [dx1, out0, og, o2, z0], [gt[0]], [(D, F32), (D, F32)], [D],
                                              "l0_post_b", mats=[Wt["mla_w_out"]], out_accs=[(D, D)])
    doh = do2.reshape(T, HEADS, VD).transpose(1, 0, 2)
    rows8 = lambda g: g.reshape(NDEV, -1, g.shape[-1])
    both = lambda s: s[0, 0] + s[1, 0]
    dense = lambda g: g.reshape(2, G * P * CH // 128, 128)
    chunks = [dense(g_b_re), dense(g_b_im), g_c_re, g_c_im]
    l1_send = [g_w_in5, rows8(g_w_glu), rows8(g_w_out5), rows8(g_w_out),
               both(dd).reshape(NDEV, 1, -1), both(dbglu).reshape(NDEV, 1, -1)]
    (dQ, dK, dV), l1_recv = attn_bwd(Q, K, V, o, lse, doh, "l0_attn_b", rode=l1_send + chunks,
                                     modes=["lead"] * len(l1_send) + [a.shape[1] // NDEV for a in chunks])
    dqh = rope(dQ, cosf, sinf, pmt, True, BF16, "l0_rope_q_b", scale=SCALE)
    dq = dqh.transpose(1, 0, 2).reshape(T, HEADS * QK)
    n_owned = len(l1_send)
    reduced = sum_slots(l1_recv[n_owned:], "sum_chunks")
    (dkv, dkr), chunk_all = split_kv_grads(dK, dV, "l0_kv_b", rode=[jnp.stack(reduced[:2]), jnp.stack(reduced[2:])],
                                           modes="gather")
    (grad_x,), (dqg, dkvg, dsh0, dsc0, dng0), (g_uq, g_ukv, g_p) = rowwise(
        st_l0_tail_bwd, [dq, dkv, dkr, dz0, cq, ckv, cqn, ckvn, h0, xa, dx1], [qg, kvg, ng[0], sc[0]],
        [(D, F32, "lat")], [QL, KVL, D, D, D], "l0_pre_b", mats=[Wt["mla_w_uq"], Wt["mla_w_ukv"], Wt["mla_w_in"]],
        out_accs=[(QL, HEADS * QK), (KVL, HEADS * KVW), (D, IN_WP)])
    g_w_uq, g_w_ukv = _col_shards(g_uq).astype(BF16), _col_shards(g_ukv).astype(BF16)
    g_w_in = _col_shards(jnp.concatenate([g_p[:, HEADS * VD:IN_W], g_p[:, :HEADS * VD]], axis=1)).astype(BF16)

    dmod = jnp.stack([jnp.concatenate([dsh0, dsc0, dgt0], axis=-1)[:, 0], jnp.concatenate([dsh1, dsc1, dgt1], axis=-1)[:, 0]])
    gbig = {"mla_w_in": g_w_in, "mla_w_uq": g_w_uq, "mla_w_ukv": g_w_ukv}
    gsmall = {"norm_g": jnp.stack([both(dng0), both(dng1)]), "mla_q_norm": both(dqg), "mla_kv_norm": both(dkvg),
              "s5_a_re": g_a_re, "s5_a_im": g_a_im, "s5_log_step": g_ls, "final_g": dfg[1, 0]}
    return lvec[1], grad_x, dmod, gbig, gsmall, l1_recv[:n_owned], chunk_all


COL_SHARDED = ("mla_w_in", "mla_w_uq", "mla_w_ukv", "s5_w_in")
ROW_SHARDED = ("mla_w_out", "s5_w_glu", "s5_w_out")
VEC_SHARDED = ("s5_d", "s5_b_glu")
BIG = COL_SHARDED + ROW_SHARDED
L0_BIG = ("mla_w_in", "mla_w_uq", "mla_w_ukv")
L1_BIG = ("s5_w_in", "s5_w_glu", "s5_w_out", "mla_w_out")
BITS16 = jnp.bfloat16
SMALL_RS = ("norm_g", "mla_q_norm", "mla_kv_norm", "s5_a_re", "s5_a_im", "s5_log_step", "s5_b_re", "s5_b_im",
            "s5_c_re", "s5_c_im", "final_g")
CHUNKED = ("s5_b_re", "s5_b_im", "s5_c_re", "s5_c_im")
DENSE = ("s5_b_re", "s5_b_im")
TINY = ("norm_g", "mla_q_norm", "mla_kv_norm", "s5_a_re", "s5_a_im", "s5_log_step", "final_g")
ORDER = ("c_ctx", "ada_w", "ada_b", "norm_g", "mla_w_in", "mla_q_norm", "mla_w_uq", "mla_kv_norm", "mla_w_ukv",
         "mla_w_out", "s5_w_in", "s5_a_re", "s5_a_im", "s5_log_step", "s5_b_re", "s5_b_im", "s5_c_re", "s5_c_im",
         "s5_d", "s5_w_glu", "s5_b_glu", "s5_w_out", "final_g")


def kernel(x, c, ctx, c_ctx, ada_w, ada_b, norm_g, mla_w_in, mla_q_norm, mla_w_uq, mla_kv_norm, mla_w_ukv, mla_w_out, s5_w_in, s5_a_re, s5_a_im, s5_log_step, s5_b_re, s5_b_im, s5_c_re, s5_c_im, s5_d, s5_w_glu, s5_b_glu, s5_w_out, final_g, loss_target, m_c_ctx, m_ada_w, m_ada_b, m_norm_g, m_mla_w_in, m_mla_q_norm, m_mla_w_uq, m_mla_kv_norm, m_mla_w_ukv, m_mla_w_out, m_s5_w_in, m_s5_a_re, m_s5_a_im, m_s5_log_step, m_s5_b_re, m_s5_b_im, m_s5_c_re, m_s5_c_im, m_s5_d, m_s5_w_glu, m_s5_b_glu, m_s5_w_out, m_final_g, v_c_ctx, v_ada_w, v_ada_b, v_norm_g, v_mla_w_in, v_mla_q_norm, v_mla_w_uq, v_mla_kv_norm, v_mla_w_ukv, v_mla_w_out, v_s5_w_in, v_s5_a_re, v_s5_a_im, v_s5_log_step, v_s5_b_re, v_s5_b_im, v_s5_c_re, v_s5_c_im, v_s5_d, v_s5_w_glu, v_s5_b_glu, v_s5_w_out, v_final_g):
    w = dict(c_ctx=c_ctx, ada_w=ada_w, ada_b=ada_b, norm_g=norm_g, mla_w_in=mla_w_in, mla_q_norm=mla_q_norm,
             mla_w_uq=mla_w_uq, mla_kv_norm=mla_kv_norm, mla_w_ukv=mla_w_ukv, mla_w_out=mla_w_out, s5_w_in=s5_w_in,
             s5_a_re=s5_a_re, s5_a_im=s5_a_im, s5_log_step=s5_log_step, s5_b_re=s5_b_re, s5_b_im=s5_b_im,
             s5_c_re=s5_c_re, s5_c_im=s5_c_im, s5_d=s5_d, s5_w_glu=s5_w_glu, s5_b_glu=s5_b_glu, s5_w_out=s5_w_out,
             final_g=final_g)
    m = dict(c_ctx=m_c_ctx, ada_w=m_ada_w, ada_b=m_ada_b, norm_g=m_norm_g, mla_w_in=m_mla_w_in, mla_q_norm=m_mla_q_norm,
             mla_w_uq=m_mla_w_uq, mla_kv_norm=m_mla_kv_norm, mla_w_ukv=m_mla_w_ukv, mla_w_out=m_mla_w_out,
             s5_w_in=m_s5_w_in, s5_a_re=m_s5_a_re, s5_a_im=m_s5_a_im, s5_log_step=m_s5_log_step, s5_b_re=m_s5_b_re,
             s5_b_im=m_s5_b_im, s5_c_re=m_s5_c_re, s5_c_im=m_s5_c_im, s5_d=m_s5_d, s5_w_glu=m_s5_w_glu,
             s5_b_glu=m_s5_b_glu, s5_w_out=m_s5_w_out, final_g=m_final_g)
    v = dict(c_ctx=v_c_ctx, ada_w=v_ada_w, ada_b=v_ada_b, norm_g=v_norm_g, mla_w_in=v_mla_w_in, mla_q_norm=v_mla_q_norm,
             mla_w_uq=v_mla_w_uq, mla_kv_norm=v_mla_kv_norm, mla_w_ukv=v_mla_w_ukv, mla_w_out=v_mla_w_out,
             s5_w_in=v_s5_w_in, s5_a_re=v_s5_a_re, s5_a_im=v_s5_a_im, s5_log_step=v_s5_log_step, s5_b_re=v_s5_b_re,
             s5_b_im=v_s5_b_im, s5_c_re=v_s5_c_re, s5_c_im=v_s5_c_im, s5_d=v_s5_d, s5_w_glu=v_s5_w_glu,
             s5_b_glu=v_s5_b_glu, s5_w_out=v_s5_w_out, final_g=v_final_g)

    me = 4 * lax.axis_index("x") + 2 * lax.axis_index("y") + lax.axis_index("c")
    WA = ada_w.shape[2]

    def shard(n):
        return _t_shard(w[n], SHARD_ROWS[n]) if n in COL_SHARDED else w[n][0].astype(BF16)

    wgot = exchange([c] + [shard(n) for n in L0_BIG], "gather", "gather_w")

    cg = wgot[0].reshape(NDEV, D)
    cc2 = c_ctx.reshape(1, D)
    ada_b_loc = lax.dynamic_slice_in_dim(ada_b.reshape(2, 3 * D // WA, WA), me, 1, axis=1)
    part = ada_fwd(cg, cc2, ada_w, ada_b_loc, "ada_fwd")
    pg = exchange([part], "gather", "gather_mod")[0]
    mod_l = lax.dynamic_index_in_dim(pg, me, axis=2, keepdims=False).transpose(1, 0, 2).reshape(2, 3 * D)
    mod_c = pg[:, :, NDEV, :].transpose(1, 0, 2).reshape(2, 3 * D)
    mod = jnp.stack([mod_c, mod_l], axis=1)

    Wt = {n: a.reshape(-1, a.shape[-1]) for n, a in zip(L0_BIG, wgot[1:])}
    Wt["mla_w_in"] = mm(_win_order(), Wt["mla_w_in"], "nn", "w_in_order", out_dtype=BF16)
    vec_bits = lax.bitcast_convert_type(jnp.concatenate([s5_d, s5_b_glu], axis=0), BITS16).reshape(2, -1)
    small = {n: w[n] for n in SMALL_RS}

    lvec, grad_x, dmod, gbig, gsmall, l1_recv, (bb_all, cc_all) = local_step(
        ctx[0], x[0], loss_target[0], mod, Wt, small, [shard(n) for n in L1_BIG] + [vec_bits])
    grad_x = grad_x[None]

    recv = dict(zip(L1_BIG + VEC_SHARDED, l1_recv))
    out = {}

    def keep(n, res):
        for key, arr in zip("gdmv", res):
            out[key, n] = arr.reshape(w[n].shape)

    kshape = lambda n: w[n].shape if w[n].ndim > 1 else (1, w[n].size)
    flat = jnp.concatenate([gsmall[n].reshape(-1) for n in TINY] + [dmod.reshape(-1), lvec.reshape(-1)])[None]
    *l0_recv, flat_all = exchange([gbig[n] for n in L0_BIG] + [flat], ["lead"] * len(L0_BIG) + ["gather"], "scatter_grads")
    chunk_all = [bb_all[:, 0], bb_all[:, 1], cc_all[:, 0], cc_all[:, 1]]
    tiny_all, off = [], 0
    for n in TINY:
        tiny_all.append(flat_all[:, 0, off:off + w[n].size].reshape((NDEV,) + kshape(n)))
        off += w[n].size
    dm_all = flat_all[:, 0, off:off + dmod.size].reshape((NDEV,) + dmod.shape)
    loss = sum_slots([flat_all[:, :, off + dmod.size:]], "loss_sum")[0][0, 0]

    dm_cols = lax.dynamic_slice_in_dim(dm_all.reshape(NDEV, 2, 2, 3 * D // WA, WA), me, 1, axis=3)[:, :, :, 0]
    dm_loc = jnp.concatenate([dm_cols[:, :, 1].transpose(1, 0, 2), dm_cols[:, :, 0].transpose(1, 0, 2)], axis=1)
    g_ada_w, dcc_part, g_ada_b = ada_bwd(cg, cc2, ada_w, dm_loc, dm_all.transpose(0, 2, 1, 3).reshape(2 * NDEV, 2, 3 * D), "ada_bwd")
    dcc_all = exchange([dcc_part], "gather", "gather_dcc")[0].reshape(NDEV, D)
    g_c_ctx = cctx_finish(dcc_all, cc2, "cctx_finish")

    flat2 = lambda t: t.reshape(-1, t.shape[-1])
    recv.update(dict(zip(L0_BIG, l0_recv)))
    big = [(recv[n], w[n][0], m[n][0], v[n][0]) for n in BIG]
    big.append((flat2(g_ada_w)[None], flat2(ada_w), flat2(m_ada_w), flat2(v_ada_w)))
    for n, r in zip(BIG + ("ada_w",), adamw_rows(big, "adamw_big")[0]):
        keep(n, r)
    items = []
    halves = 2
    for n, g in zip(CHUNKED, chunk_all):
        blk = (1, 1, G // halves) + w[n].shape[3:]
        g = jnp.moveaxis(g, 0, 1).reshape(w[n].shape)
        g_spec = pl.BlockSpec((1,) + blk, lambda d, s: (0, 0, d, s, 0, 0))
        items.append((g[None], g_spec, w[n], m[n], v[n], pl.BlockSpec(blk, lambda d, s: (0, d, s, 0, 0))))
    for n, res in zip(CHUNKED, adamw_multi(items, (2, halves), "adamw_bc")):
        keep(n, res)
    tiny_g = dict(zip(TINY, tiny_all))
    tiny_g.update({n: recv[n] for n in VEC_SHARDED})
    tiny_g["c_ctx"], tiny_g["ada_b"] = g_c_ctx[None], g_ada_b[None]
    names = list(tiny_g)
    items = [(tiny_g[n], _whole(tiny_g[n], 1)) + tuple(t[n].reshape(kshape(n)) for t in (w, m, v))
             + (pl.BlockSpec(kshape(n), lambda i, r=len(kshape(n)): (0,) * r),) for n in names]
    for n, res in zip(names, adamw_multi(items, (1,), "adamw_small")):
        keep(n, res)

    return (loss, grad_x, *[out["g", n] for n in ORDER], *[out["d", n] for n in ORDER],
            *[out["m", n] for n in ORDER], *[out["v", n] for n in ORDER])
```

```python
import math

import numpy as np
import jax
import jax.numpy as jnp
from jax import lax
from jax.experimental import pallas as pl
from jax.experimental.pallas import tpu as pltpu

F32 = jnp.float32
BF16 = jnp.bfloat16

D = 1024
L = 2048
LC = 256
NDEV = 8
GRID_W = 64
EPS = 1e-6
HEADS = 16
NOPE = 64
ROPE = 32
QK = NOPE + ROPE
VD = 64
IN_W = 256 + 128 + ROPE + HEADS * 64
IN_WP = 1536
QL = 256
KVL = 128
SCALE = QK ** -0.5
LOG2E = math.log2(math.e)
THETA = 10000.0
G = 64
P = 64
CH = 16
GB = 8
NJ = G // GB
UB = GB * CH
SB = GB * P
SEG = 16
TB = 256
VMEM_LIMIT = 56 * 1024 * 1024
B1, B2, LR, AEPS, WD, STEP = 0.9, 0.999, 0.001, 1e-8, 0.01, 10
MESH_T = pl.DeviceIdType.MESH


def _cp(sem=None):
    return pltpu.CompilerParams(dimension_semantics=sem, vmem_limit_bytes=VMEM_LIMIT)


def _sig(x):
    return 1.0 / (1.0 + jnp.exp(-x))


def _silu(x):
    return x * _sig(x)


def _dsilu(x):
    s = _sig(x)
    return s * (1.0 + x * (1.0 - s))


_GK = math.sqrt(2.0 / math.pi)


def _gelu(x):
    return 0.5 * x * (1.0 + jnp.tanh(_GK * (x + 0.044715 * x * x * x)))


def _dgelu(x):
    t = jnp.tanh(_GK * (x + 0.044715 * x * x * x))
    return 0.5 * (1.0 + t) + 0.5 * x * (1.0 - t * t) * _GK * (1.0 + 3 * 0.044715 * x * x)


def _rs(x):
    return lax.rsqrt(jnp.mean(x * x, axis=-1, keepdims=True) + EPS)


def _sum0(x):
    return jnp.sum(x, axis=0, keepdims=True)


def st_norm_mod(x, g, sc, sh):
    y = x * _rs(x) * g
    return (y * (1.0 + sc) + sh,), ()


def st_norm_mod_bwd(x, dh, dres, g, sc):
    r = _rs(x)
    xn = x * r
    y = xn * g
    dy = dh * (1.0 + sc)
    dxn = dy * g
    dx = r * (dxn - xn * jnp.mean(dxn * xn, axis=-1, keepdims=True))
    return (dres + dx,), (_sum0(dh), _sum0(dh * y), _sum0(dy * xn))


def st_rms(x, g):
    return (x * _rs(x) * g,), ()


def st_rms_bwd(x, dy, g):
    r = _rs(x)
    n = x * r
    dn = dy * g
    dx = r * (dn - n * jnp.mean(dn * n, axis=-1, keepdims=True))
    return (dx,), (_sum0(dy * n),)


def st_rms2(x1, x2, g1, g2):
    return st_rms(x1, g1)[0] + st_rms(x2, g2)[0], ()


def st_rms2_bwd(x1, dy1, x2, dy2, g1, g2):
    (d1,), (s1,) = st_rms_bwd(x1, dy1, g1)
    (d2,), (s2,) = st_rms_bwd(x2, dy2, g2)
    return (d1, d2), (s1, s2)


def st_gate_bwd(dog, o, z):
    return (dog * _silu(z), dog * o * _dsilu(z)), ()


def st_resid_bwd(dx, out, gt):
    return (dx * gt,), (_sum0(dx * out),)


def st_s5a(yssm, u, d):
    y = yssm + d * u
    return (y, _gelu(y)), ()


def st_s5b_bwd(dy3, y, gl, z, b):
    y1 = _gelu(y)
    s = _sig(gl + b)
    dy2 = dy3 * _silu(z)
    dz = dy3 * y1 * s * _dsilu(z)
    dgl = dy2 * y1 * s * (1.0 - s)
    return (dgl, dz, dy2 * s), (_sum0(dgl),)


def st_s5a_bwd(dy1a, dy1b, y, u, d):
    dy = (dy1a + dy1b) * _dgelu(y)
    return (dy, dy * d), (_sum0(dy * u),)


def st_l0_pre(x, g, sc, sh, qg, kvg, w_in):
    hb = st_norm_mod(x, g, sc, sh)[0][0].astype(BF16)
    p = lax.dot_general(hb, w_in, _DN["nt"], preferred_element_type=F32)
    cq, ckv = p[:, HEADS * VD:HEADS * VD + QL], p[:, HEADS * VD + QL:HEADS * VD + QL + KVL]
    return (hb, p) + st_rms2(cq, ckv, qg, kvg)[0], ()


def st_l0_tail_bwd(dq, dkv, dkr, dz, cq, ckv, cqn, ckvn, h, x, dres, qg, kvg, g, sc, w_uq, w_ukv, w_in):
    dcqn = jnp.dot(dq, w_uq, preferred_element_type=F32)
    dckvn = jnp.dot(dkv, w_ukv, preferred_element_type=F32)
    (dcq, dckv), (dqg, dkvg) = st_rms2_bwd(cq, dcqn, ckv, dckvn, qg, kvg)
    dp = jnp.concatenate([dz, dcq, dckv, dkr], axis=1).astype(BF16)
    dh = jnp.dot(dp, w_in, preferred_element_type=F32)
    outs, sums = st_norm_mod_bwd(x, dh, dres, g, sc)
    tn = lambda a, b: lax.dot_general(a, b, _DN["tn"], preferred_element_type=F32)
    return outs, (dqg, dkvg) + sums, (tn(cqn, dq), tn(ckvn, dkv), tn(h, dp))


def st_l1_pre(x, g, sc, sh, w_in):
    hb = st_norm_mod(x, g, sc, sh)[0][0].astype(BF16)
    return (hb, lax.dot_general(hb, w_in, _DN["nt"], preferred_element_type=F32)), ()


def st_l1_tail_bwd(du_a, du_b, dz, h, x, dres, g, sc, w_in):
    dp = jnp.concatenate([(du_a + du_b).astype(BF16), dz], axis=1)
    dh = jnp.dot(dp, w_in, preferred_element_type=F32)
    outs, sums = st_norm_mod_bwd(x, dh, dres, g, sc)
    w = dp.shape[1] // NDEV
    shards = [lax.dot_general(h, dp[:, r * w:(r + 1) * w], _DN["tn"], preferred_element_type=F32) for r in range(NDEV)]
    return outs, sums, (jnp.stack(shards),)


def st_l0_post(o, z, x, gt, w_out):
    og = (o * _silu(z)).astype(BF16)
    out = jnp.dot(og, w_out, preferred_element_type=F32)
    return (og, out, x + gt * out), ()


def st_l0_post_bwd(dx1, out, og, o, z, gt, w_out):
    (dout,), (dgt,) = st_resid_bwd(dx1, out.astype(F32), gt)
    doutb = dout.astype(BF16)
    dog = lax.dot_general(doutb, w_out, _DN["nt"], preferred_element_type=F32)
    return st_gate_bwd(dog, o, z)[0], (dgt,), (lax.dot_general(og, doutb, _DN["tn"], preferred_element_type=F32),)


def st_l1_mlp(yssm, u, z, x1, tgt, d, bglu, gt, fg, mask, w_glu, w_out):
    (y, y1), _ = st_s5a(yssm, u, d)
    y1b = y1.astype(BF16)
    gl = jnp.dot(y1b, w_glu, preferred_element_type=F32)
    y3 = (y1 * _sig(gl + bglu) * _silu(z)).astype(BF16)
    out = jnp.dot(y3, w_out, preferred_element_type=F32)
    (dx2,), sums = st_final(x1 + gt * out, tgt, fg, mask)
    return (y, y1b, gl, y3, out, dx2), sums


def st_l1_mlp_bwd(dx2, out, y3, y, gl, z, u, y1b, gt, bglu, d, w_out, w_glu):
    out, gl = out.astype(F32), gl.astype(F32)
    (dout,), (dgt,) = st_resid_bwd(dx2, out, gt)
    doutb = dout.astype(BF16)
    dy3 = lax.dot_general(doutb, w_out, _DN["nt"], preferred_element_type=F32)
    (dgl, dz, dy1a), (dbglu,) = st_s5b_bwd(dy3, y, gl, z, bglu)
    dglb = dgl.astype(BF16)
    dy1b = lax.dot_general(dglb, w_glu, _DN["nt"], preferred_element_type=F32)
    (dy, du), (dd,) = st_s5a_bwd(dy1a, dy1b, y, u, d)
    g_w_out = lax.dot_general(y3, doutb, _DN["tn"], preferred_element_type=F32)
    g_w_glu = lax.dot_general(y1b, dglb, _DN["tn"], preferred_element_type=F32)
    return (dz, dy, du), (dgt, dbglu, dd), (g_w_out, g_w_glu)


def st_final(x2, tgt, g, mask):
    r = _rs(x2)
    n = x2 * r
    e = n * g - tgt
    dyo = e * (1.0 / D)
    dn = dyo * g
    dx = r * (dn - n * jnp.mean(dn * n, axis=-1, keepdims=True))
    lsum = jnp.sum(_sum0(e * e), axis=1, keepdims=True) * (0.5 / D)
    return (dx * mask,), (_sum0(dyo * n), jnp.broadcast_to(lsum, (1, 128)))


def rowwise(fn, rows, vecs, out_rows, out_sums, name, mats=(), out_accs=()):
    lat_blk = lambda i: jnp.maximum(i - 1, 0)
    arrays, in_specs, pick = [], [], []
    for a in rows:
        if not isinstance(a, tuple):
            a = (a, 0, a.shape[1])
        tag = a[0] if isinstance(a[0], str) else None
        if tag == "cat":
            _, ctx, x = a
            arrays += [ctx, x]
            in_specs += [pl.BlockSpec((TB, ctx.shape[1]), lambda i: (0, 0)),
                         pl.BlockSpec((TB, x.shape[1]), lambda i: (lat_blk(i), 0))]
            pick.append(2)
        elif tag == "lat":
            arrays.append(a[1])
            in_specs.append(pl.BlockSpec((TB, a[1].shape[1]), lambda i: (lat_blk(i), 0)))
            pick.append(1)
        else:
            arr, cb, width = a
            arrays.append(arr)
            in_specs.append(pl.BlockSpec((TB, width), lambda i, cb=cb: (i, cb)))
            pick.append(1)
    T = LC + L
    nin, nv, nm, no, ns = len(arrays), len(vecs), len(mats), len(out_rows), len(out_sums)

    def body(*refs):
        i = pl.program_id(0)
        vals, k = [], 0
        for p in pick:
            if p == 2:
                vals.append(jnp.where(i == 0, refs[k][...], refs[k + 1][...]))
            else:
                vals.append(refs[k][...])
            k += p
        vals += [r[0] for r in refs[nin:nin + nv]] + [r[...] for r in refs[nin + nv:nin + nv + nm]]
        res = fn(*vals)
        first_out = nin + nv + nm
        for r, o in zip(refs[first_out:first_out + no], res[0]):
            r[...] = o.astype(r.dtype)
        sum_refs = refs[first_out + no:first_out + no + ns]
        if sum_refs:
            @pl.when(i <= 1)
            def _():
                for r in sum_refs:
                    r[...] = jnp.zeros_like(r)
            for r, s in zip(sum_refs, res[1]):
                r[0] += s
        na = len(out_accs)
        if na:
            acc_out, acc = refs[first_out + no + ns:first_out + no + ns + na], refs[first_out + no + ns + na:]

            @pl.when(i == 0)
            def _():
                for r in acc:
                    r[...] = jnp.zeros_like(r)
            for r, a in zip(acc, res[2]):
                r[...] += a

            @pl.when(i == T // TB - 1)
            def _():
                for o, r in zip(acc_out, acc):
                    o[...] = r[...].astype(o.dtype)

    kind = lambda i: (jnp.minimum(i, 1), 0, 0)
    in_specs += [pl.BlockSpec((1, 1, v.shape[2]), kind) for v in vecs]
    in_specs += [pl.BlockSpec(m.shape, lambda i: (0, 0), pipeline_mode=pl.Buffered(1)) for m in mats]
    out_specs, out_shape = [], []
    for o in out_rows:
        lat = len(o) == 3
        out_specs.append(pl.BlockSpec((TB, o[0]), (lambda i: (lat_blk(i), 0)) if lat else (lambda i: (i, 0))))
        out_shape.append(jax.ShapeDtypeStruct((L if lat else T, o[0]), o[1]))
    out_specs += [pl.BlockSpec((1, 1, c), kind) for c in out_sums]
    out_shape += [jax.ShapeDtypeStruct((2, 1, c), F32) for c in out_sums]
    out_specs += [pl.BlockSpec(s, lambda i, r=len(s): (0,) * r) for s in out_accs]
    out_shape += [jax.ShapeDtypeStruct(s, BF16) for s in out_accs]
    res = pl.pallas_call(body, grid=(T // TB,), in_specs=in_specs, out_specs=out_specs, out_shape=out_shape,
                         scratch_shapes=[pltpu.VMEM(s, F32) for s in out_accs],
                         compiler_params=_cp(("arbitrary",)), name=name)(*arrays, *vecs, *mats)
    if out_accs:
        return res[:no], res[no:no + ns], res[no + ns:]
    return res[:no], res[no:]


_DN = {"nn": (((1,), (0,)), ((), ())), "nt": (((1,), (1,)), ((), ())), "tn": (((0,), (0,)), ((), ()))}


def mm(a, b, mode, name, out_dtype=F32, tm=None, tn=None):
    if mode == "nn":
        (M, K), (_, N) = a.shape, b.shape
    elif mode == "nt":
        (M, K), (N, _) = a.shape, b.shape
    else:
        (K, M), (_, N) = a.shape, b.shape
    if tm is None:
        tm = next((t for t in (768, 512, 256) if M % t == 0 and M > t), M)
    tn = N if tn is None else tn
    dn = _DN[mode]

    def body(a_ref, b_ref, o_ref):
        o_ref[...] = lax.dot_general(a_ref[...].astype(BF16), b_ref[...].astype(BF16), dn,
                                     preferred_element_type=F32).astype(o_ref.dtype)

    a_spec = pl.BlockSpec((K, tm), lambda i, j: (0, i)) if mode == "tn" else pl.BlockSpec((tm, K), lambda i, j: (i, 0))
    b_spec = pl.BlockSpec((tn, K), lambda i, j: (j, 0)) if mode == "nt" else pl.BlockSpec((K, tn), lambda i, j: (0, j))
    return pl.pallas_call(body, grid=(M // tm, N // tn), in_specs=[a_spec, b_spec],
                          out_specs=pl.BlockSpec((tm, tn), lambda i, j: (i, j)), out_shape=jax.ShapeDtypeStruct((M, N), out_dtype),
                          compiler_params=_cp(("parallel", "arbitrary")), name=name)(a, b)


def _rope_tables(T, width=QK, first=NOPE):
    nlat = T - LC
    pos = np.arange(nlat)
    row, col = pos // GRID_W, pos % GRID_W
    half = ROPE // 2
    inv = 1.0 / (THETA ** (np.arange(0, half, 2, dtype=np.float64) / half))
    cosf = np.ones((T, width), np.float64)
    sinf = np.zeros((T, width), np.float64)
    perm = np.zeros((width, width), np.float32)
    for m in range(ROPE):
        j = first + m
        blk, w = m // half, m % half
        ang = (row if blk == 0 else col)[:, None] * inv[None, :]
        f = w % (half // 2)
        cosf[LC:, j] = np.cos(ang[:, f])
        if w < half // 2:
            sinf[LC:, j] = -np.sin(ang[:, f])
            perm[j + half // 2, j] = 1.0
        else:
            sinf[LC:, j] = np.sin(ang[:, f])
            perm[j - half // 2, j] = 1.0
    return jnp.asarray(cosf, F32), jnp.asarray(sinf, F32), jnp.asarray(perm, BF16), jnp.asarray(perm.T, BF16)


def _exact_perm(x, pm):
    hi = x.astype(BF16)
    r1 = x - hi.astype(F32)
    mid = r1.astype(BF16)
    lo = (r1 - mid.astype(F32)).astype(BF16)
    dot = lambda a: jnp.dot(a, pm, preferred_element_type=F32)
    return dot(hi) + dot(mid) + dot(lo)


def _rot(x, cv, sv, pv, inverse):
    if inverse:
        return x * cv + _exact_perm(x * sv, pv)
    return x * cv + _exact_perm(x, pv) * sv


def rope(x, cosf, sinf, pm, inverse, out_dtype, name, scale=1.0):
    H, T, _ = x.shape

    def body(x_ref, c_ref, s_ref, p_ref, o_ref):
        cv, sv, pv = c_ref[...], s_ref[...], p_ref[...]
        for h in range(H):
            o_ref[h] = (_rot(x_ref[h], cv, sv, pv, inverse) * scale).astype(o_ref.dtype)

    return pl.pallas_call(
        body, grid=(T // TB,),
        in_specs=[pl.BlockSpec((H, TB, QK), lambda i: (0, i, 0)), pl.BlockSpec((TB, QK), lambda i: (i, 0)),
                  pl.BlockSpec((TB, QK), lambda i: (i, 0)), pl.BlockSpec((QK, QK), lambda i: (0, 0))],
        out_specs=pl.BlockSpec((H, TB, QK), lambda i: (0, i, 0)), out_shape=jax.ShapeDtypeStruct((H, T, QK), out_dtype),
        compiler_params=_cp(("parallel",)), name=name)(x, cosf, sinf, pm)


KVW = NOPE + VD


def _kv_selectors():
    s_kn = np.zeros((KVW, QK), np.float32)
    s_kr = np.zeros((128, QK), np.float32)
    s_v = np.zeros((KVW, VD), np.float32)
    for l in range(NOPE):
        s_kn[l, l] = 1.0
    for l in range(ROPE):
        s_kr[l, NOPE + l] = 1.0
    for l in range(VD):
        s_v[NOPE + l, l] = 1.0
    return s_kn, s_kr, s_v


def project_q(cqn, w, name):
    T = cqn.shape[0]
    cosf, sinf, _, _ = _rope_tables(T, 128, NOPE)
    wp = jnp.pad(w.reshape(HEADS, QK, QL), ((0, 0), (0, 128 - QK), (0, 0))).reshape(HEADS * 128, QL)

    def body(a_ref, w_ref, c_ref, s_ref, o_ref):
        a, cv, sv = a_ref[...], c_ref[...], s_ref[...]
        first_of_pair = lax.bitwise_and(lax.broadcasted_iota(jnp.int32, (TB, 128), 1), ROPE // 4) == 0
        for h in range(HEADS):
            qh = _dotf(a, w_ref[pl.ds(h * 128, 128), :], "nt")
            swap = jnp.where(first_of_pair, pltpu.roll(qh, 128 - ROPE // 4, 1), pltpu.roll(qh, ROPE // 4, 1))
            o_ref[h] = ((qh * cv + swap * sv) * (SCALE * LOG2E))[:, :QK].astype(BF16)

    rows = lambda c: pl.BlockSpec((TB, c), lambda i: (i, 0))
    return pl.pallas_call(
        body, grid=(T // TB,), in_specs=[rows(QL), pl.BlockSpec(wp.shape, lambda i: (0, 0)), rows(128), rows(128)],
        out_specs=pl.BlockSpec((HEADS, TB, QK), lambda i: (0, i, 0)), out_shape=jax.ShapeDtypeStruct((HEADS, T, QK), BF16),
        compiler_params=_cp(("parallel",)), name=name)(cqn, wp, cosf, sinf)


def project_kv(ckvn, w, p0, kr_block, name):
    T = ckvn.shape[0]
    assert KVW == 128 and NOPE == VD
    cosf, sinf, pm, _ = _rope_tables(T, 128, 0)

    def body(a_ref, w_ref, kr_ref, c_ref, s_ref, p_ref, k_ref, v_ref):
        a = a_ref[...]
        is_nope = lax.broadcasted_iota(jnp.int32, (TB, KVW), 1) < NOPE
        kr_at = pltpu.roll(_rot(kr_ref[...], c_ref[...], s_ref[...], p_ref[...], False), NOPE, 1)
        for h in range(HEADS):
            kv = _dotf(a, w_ref[pl.ds(h * KVW, KVW), :], "nt")
            k_ref[h] = jnp.where(is_nope, kv, kr_at)[:, :QK].astype(BF16)
            v_ref[h] = pltpu.roll(kv, VD, 1)[:, :VD].astype(BF16)

    rows = lambda c: pl.BlockSpec((TB, c), lambda i: (i, 0))
    const = lambda x: pl.BlockSpec(x.shape, lambda i: (0, 0))
    return pl.pallas_call(
        body, grid=(T // TB,),
        in_specs=[rows(KVL), const(w), pl.BlockSpec((TB, 128), lambda i: (i, kr_block)), rows(128), rows(128), const(pm)],
        out_specs=[pl.BlockSpec((HEADS, TB, QK), lambda i: (0, i, 0)), pl.BlockSpec((HEADS, TB, VD), lambda i: (0, i, 0))],
        out_shape=[jax.ShapeDtypeStruct((HEADS, T, QK), BF16), jax.ShapeDtypeStruct((HEADS, T, VD), BF16)],
        compiler_params=_cp(("parallel",)), name=name)(ckvn, w, p0, cosf, sinf, pm)


def split_kv_grads(dk, dv, name, rode=None, modes=None):
    H, T, _ = dk.shape
    cosf, sinf, _, pmt = _rope_tables(T, 128, 0)
    s_kn, s_kr, s_v = _kv_selectors()
    s_knt, s_krt, s_vt = (jnp.asarray(s.T, BF16) for s in (s_kn, s_kr, s_v))

    def body(dk_ref, dv_ref, c_ref, s_ref, p_ref, skn_ref, skr_ref, sv_ref, dkv_ref, dkr_ref):
        total = None
        for h in range(H):
            dkh = dk_ref[h] * (1.0 / LOG2E)
            total = dkh if total is None else total + dkh
            dkv_ref[:, pl.ds(h * KVW, KVW)] = (
                jnp.dot(dkh.astype(BF16), skn_ref[...], preferred_element_type=F32)
                + jnp.dot(dv_ref[h].astype(BF16), sv_ref[...], preferred_element_type=F32)).astype(BF16)
        dkr_ref[...] = _rot(_exact_perm(total, skr_ref[...]), c_ref[...], s_ref[...], p_ref[...], True)

    rows = lambda c: pl.BlockSpec((TB, c), lambda i, j: (i, 0))
    const = lambda a: pl.BlockSpec(a.shape, lambda i, j: (0, 0))
    return _ride_call(
        body, (T // TB, 1),
        [pl.BlockSpec((H, TB, QK), lambda i, j: (0, i, 0)), pl.BlockSpec((H, TB, VD), lambda i, j: (0, i, 0)),
         rows(128), rows(128), const(pmt), const(s_knt), const(s_krt), const(s_vt)],
        [rows(H * KVW), rows(128)],
        [jax.ShapeDtypeStruct((T, H * KVW), BF16), jax.ShapeDtypeStruct((T, 128), F32)],
        Exchange(rode, modes) if rode else None, rode, name, (dk, dv, cosf, sinf, pmt, s_knt, s_krt, s_vt))


HB = 4
HBF = 8


def _by_query_block(run, T):
    @pl.when(pl.program_id(1) == 0)
    def _():
        run(LC)

    @pl.when(pl.program_id(1) > 0)
    def _():
        run(T)


def _with_rider(body, nin, nout, ride, grid):
    if ride is None:
        return body
    n = ride.n

    def wrapped(*refs):
        ins, xs = refs[:nin], refs[nin:nin + n]
        outs, got = refs[nin + n:nin + n + nout], refs[nin + n + nout:nin + 2 * n + nout]
        sems = refs[nin + 2 * n + nout:]
        step = pl.program_id(0) * grid[1] + pl.program_id(1)

        @pl.when(step == 0)
        def _():
            ride.start(xs, got, sems)

        body(*ins, *outs)

        @pl.when(step == grid[0] * grid[1] - 1)
        def _():
            ride.finish(xs, got, sems)

    return wrapped


def _ride_call(body, grid, in_specs, out_specs, out_shape, ride, rode, name, args):
    if ride is None:
        return pl.pallas_call(body, grid=grid, in_specs=in_specs, out_specs=out_specs, out_shape=out_shape,
                              compiler_params=_cp(("parallel", "arbitrary")), name=name)(*args), []
    res = pl.pallas_call(
        _with_rider(body, len(in_specs), len(out_specs), ride, grid), grid=grid,
        in_specs=in_specs + ride.specs, out_specs=out_specs + ride.specs, out_shape=out_shape + ride.out_shape,
        scratch_shapes=ride.scratch,
        compiler_params=pltpu.CompilerParams(dimension_semantics=("arbitrary", "arbitrary"), vmem_limit_bytes=VMEM_LIMIT,
                                             has_side_effects=True), name=name)(*args, *rode)
    return res[:len(out_specs)], res[len(out_specs):]


def attn_fwd(q, k, v, name, rode=None, modes=None):
    H, T, _ = q.shape

    def body(q_ref, k_ref, v_ref, o_ref, lse_ref):
        def run(nk):
            for hh in range(HBF):
                s = _dotf(q_ref[hh], k_ref[hh, pl.ds(0, nk), :], "nt")
                m = jnp.max(s, axis=1, keepdims=True)
                p = jnp.exp2(s - m)
                l = jnp.sum(p, axis=1, keepdims=True)
                o = jnp.dot(p.astype(BF16), v_ref[hh, pl.ds(0, nk), :], preferred_element_type=F32)
                o_ref[hh] = o / l
                lse_ref[hh] = m + jnp.log2(l)

        _by_query_block(run, T)

    return _ride_call(
        body, (H // HBF, T // TB),
        [pl.BlockSpec((HBF, TB, QK), lambda h, i: (h, i, 0)), pl.BlockSpec((HBF, T, QK), lambda h, i: (h, 0, 0)),
         pl.BlockSpec((HBF, T, VD), lambda h, i: (h, 0, 0))],
        [pl.BlockSpec((HBF, TB, VD), lambda h, i: (h, i, 0)), pl.BlockSpec((HBF, TB, 1), lambda h, i: (h, i, 0))],
        [jax.ShapeDtypeStruct((H, T, VD), F32), jax.ShapeDtypeStruct((H, T, 1), F32)],
        Exchange(rode, modes) if rode else None, rode, name, (q, k, v))


def attn_bwd(q, k, v, o, lse, do, name, rode=None, modes=None):
    H, T, _ = q.shape

    def body(q_ref, k_ref, v_ref, o_ref, lse_ref, do_ref, dq_ref, dk_ref, dv_ref):
        i = pl.program_id(1)

        @pl.when(i == 0)
        def _():
            dk_ref[...] = jnp.zeros_like(dk_ref)
            dv_ref[...] = jnp.zeros_like(dv_ref)

        def run(nk):
            keys = pl.ds(0, nk)
            for hh in range(HB):
                qv, kv, dov = q_ref[hh], k_ref[hh, keys, :], do_ref[hh]
                p = jnp.exp2(_dotf(qv, kv, "nt") - lse_ref[hh])
                delta = jnp.sum(dov * o_ref[hh], axis=1, keepdims=True)
                dob = dov.astype(BF16)
                dv_ref[hh, keys, :] += _dotf(p.astype(BF16), dob, "tn")
                dp = _dotf(dob, v_ref[hh, keys, :], "nt")
                ds = (p * (dp - delta)).astype(BF16)
                dq_ref[hh] = jnp.dot(ds, kv, preferred_element_type=F32)
                dk_ref[hh, keys, :] += _dotf(ds, qv, "tn")

        _by_query_block(run, T)

    blk = lambda c: pl.BlockSpec((HB, TB, c), lambda h, i: (h, i, 0))
    full = lambda c: pl.BlockSpec((HB, T, c), lambda h, i: (h, 0, 0))
    return _ride_call(
        body, (H // HB, T // TB), [blk(QK), full(QK), full(VD), blk(VD), blk(1), blk(VD)], [blk(QK), full(QK), full(VD)],
        [jax.ShapeDtypeStruct((H, T, QK), F32), jax.ShapeDtypeStruct((H, T, QK), F32), jax.ShapeDtypeStruct((H, T, VD), F32)],
        Exchange(rode, modes) if rode else None, rode, name, (q, k, v, o, lse, do))


def disc_fwd(a_re, a_im, ls, name):
    def body(ar_ref, ai_ref, ls_ref, lr_ref, li_ref, fr_ref, fi_ref):
        ar, ai = ar_ref[...], ai_ref[...]
        dt = jnp.exp(ls_ref[...])
        mag = jnp.exp(ar * dt)
        lr = mag * jnp.cos(ai * dt)
        li = mag * jnp.sin(ai * dt)
        den = ar * ar + ai * ai
        nr = lr - 1.0
        lr_ref[...] = lr
        li_ref[...] = li
        fr_ref[...] = (nr * ar + li * ai) / den
        fi_ref[...] = (li * ar - nr * ai) / den

    return pl.pallas_call(body, out_shape=[jax.ShapeDtypeStruct(a_re.shape, F32)] * 4, name=name)(a_re, a_im, ls)


def disc_b(f_re, f_im, b_re, b_im, name):
    def body(fr_ref, fi_ref, br_ref, bi_ref, or_ref, oi_ref):
        fr, fi, br, bi = fr_ref[...], fi_ref[...], br_ref[...], bi_ref[...]
        or_ref[...] = fr * br - fi * bi
        oi_ref[...] = fr * bi + fi * br

    return pl.pallas_call(body, out_shape=[jax.ShapeDtypeStruct(b_re.shape, F32)] * 2, compiler_params=_cp(),
                          name=name)(f_re, f_im, b_re, b_im)


def disc_b_bwd(f_re, f_im, b_re, b_im, dbb_re, dbb_im, name):
    def body(fr_ref, fi_ref, br_ref, bi_ref, dr_ref, di_ref, dbr_ref, dbi_ref, dfr_ref, dfi_ref):
        fr, fi, br, bi, dr, di = fr_ref[...], fi_ref[...], br_ref[...], bi_ref[...], dr_ref[...], di_ref[...]
        dbr_ref[...] = fr * dr + fi * di
        dbi_ref[...] = fr * di - fi * dr
        dfr_ref[...] = jnp.sum(dr * br + di * bi, axis=2, keepdims=True)
        dfi_ref[...] = jnp.sum(di * br - dr * bi, axis=2, keepdims=True)

    return pl.pallas_call(body, out_shape=[jax.ShapeDtypeStruct(b_re.shape, F32)] * 2 + [jax.ShapeDtypeStruct(f_re.shape, F32)] * 2,
                          compiler_params=_cp(), name=name)(f_re, f_im, b_re, b_im, dbb_re, dbb_im)


def disc_a_bwd(a_re, a_im, ls, dlr, dli, dfr, dfi, name):
    def body(ar_ref, ai_ref, ls_ref, dlr_ref, dli_ref, dfr_ref, dfi_ref, dar_ref, dai_ref, dls_ref):
        ar, ai = ar_ref[...], ai_ref[...]
        dt = jnp.exp(ls_ref[...])
        mag = jnp.exp(ar * dt)
        cs, sn = jnp.cos(ai * dt), jnp.sin(ai * dt)
        lr, li = mag * cs, mag * sn
        den = ar * ar + ai * ai
        nr = lr - 1.0
        f_re = (nr * ar + li * ai) / den
        f_im = (li * ar - nr * ai) / den
        dn1 = dfr_ref[...] / den
        dn2 = dfi_ref[...] / den
        dden = -(dfr_ref[...] * f_re + dfi_ref[...] * f_im) / den
        dlr_t = dlr_ref[...] + dn1 * ar - dn2 * ai
        dli_t = dli_ref[...] + dn1 * ai + dn2 * ar
        dar = dn1 * nr + dn2 * li + dden * 2.0 * ar
        dai = dn1 * li - dn2 * nr + dden * 2.0 * ai
        dmag = dlr_t * cs + dli_t * sn
        dth = dli_t * lr - dlr_t * li
        dar_ref[...] = dar + dmag * mag * dt
        dai_ref[...] = dai + dth * dt
        dls_ref[...] = jnp.sum(dmag * mag * ar + dth * ai, axis=-1, keepdims=True) * dt

    return pl.pallas_call(body, out_shape=[jax.ShapeDtypeStruct(a_re.shape, F32)] * 2 +
                          [jax.ShapeDtypeStruct(ls.shape, F32)], name=name)(a_re, a_im, ls, dlr, dli, dfr, dfi)


def _cpow(lr, li, n):
    rr, ri = None, None
    br, bi = lr, li
    while n:
        if n & 1:
            if rr is None:
                rr, ri = br, bi
            else:
                rr, ri = rr * br - ri * bi, rr * bi + ri * br
        n >>= 1
        if n:
            br, bi = br * br - bi * bi, 2.0 * br * bi
    return rr, ri


UNROLL = 4


def _steps(trips, fn, init):
    main = trips // UNROLL

    def body(i, c):
        for j in range(UNROLL):
            c = fn(i * UNROLL + j, c)
        return c

    c = lax.fori_loop(0, main, body, init) if main else init
    for n in range(main * UNROLL, trips):
        c = fn(n, c)
    return c


def _seg_scan(xre, xim, lam8, pw, base, seglen, rev, init, fin_re, fin_im, ini_re, ini_im, prev=None):
    lr, li = lam8
    nsub = SEG // 8

    def rows(t, s):
        first = base + t * SEG + 8 * s
        return pl.ds(first if isinstance(first, int) else pl.multiple_of(first, 8), 8)

    tmap = (lambda n: seglen - 1 - n) if rev else (lambda n: n)
    zeros = tuple(jnp.zeros((8, SB), F32) for _ in range(2 * nsub))

    def advance(c, t):
        out = []
        for s in range(nsub):
            a, b = c[2 * s], c[2 * s + 1]
            out += [lr * a - li * b + xre[rows(t, s), :], lr * b + li * a + xim[rows(t, s), :]]
        return tuple(out)

    fin = _steps(seglen, lambda n, c: advance(c, tmap(n)), zeros)
    for s in range(nsub):
        fin_re[pl.ds(8 * s, 8), :] = fin[2 * s]
        fin_im[pl.ds(8 * s, 8), :] = fin[2 * s + 1]
    (cr, ci), (pr, pi) = init, pw
    for i in (range(SEG - 1, -1, -1) if rev else range(SEG)):
        ini_re[pl.ds(i, 1), :] = cr
        ini_im[pl.ds(i, 1), :] = ci
        cr, ci = pr * cr - pi * ci + fin_re[pl.ds(i, 1), :], pr * ci + pi * cr + fin_im[pl.ds(i, 1), :]
    tiles = lambda re, im: tuple(r[pl.ds(8 * s, 8), :] for s in range(nsub) for r in (re, im))
    start = tiles(ini_re, ini_im)

    def store(c, t):
        new = advance(c, t)
        for s in range(nsub):
            xre[rows(t, s), :] = new[2 * s]
            xim[rows(t, s), :] = new[2 * s + 1]
        return new

    if prev is None:
        _steps(seglen, lambda n, c: store(c, tmap(n)), start)
        return (cr, ci), None

    sre, sim, s_ini_re, s_ini_im = prev

    def acc_step(c, t, before):
        new = store(c[:2 * nsub], t)
        acc = []
        for s in range(nsub):
            (na, nb), (pre, pim) = new[2 * s:2 * s + 2], before[2 * s:2 * s + 2]
            acc += [c[2 * nsub + 2 * s] + na * pre + nb * pim, c[2 * nsub + 2 * s + 1] + nb * pre - na * pim]
        return new + tuple(acc)

    def body(n, c):
        t = tmap(n)
        tp = t - 1 if rev else t + 1
        return acc_step(c, t, tuple(r[rows(tp, s), :] for s in range(nsub) for r in (sre, sim)))

    c = _steps(seglen - 1, body, start + zeros)
    c = acc_step(c, 0 if rev else seglen - 1, tiles(s_ini_re, s_ini_im))
    acc = c[2 * nsub:]
    return (cr, ci), (sum(acc[0::2][1:], acc[0]), sum(acc[1::2][1:], acc[1]))


def _lam_tiles(lr, li, lens, conj=False):
    if conj:
        li = -li
    lam8 = (jnp.broadcast_to(lr, (8, SB)), jnp.broadcast_to(li, (8, SB)))
    return lam8, [_cpow(lr, li, n) for n in lens]


def _stretches(T):
    return ((0, LC // SEG), (LC, (T - LC) // SEG))


def _to_seg_order(src, dst, T):
    for base, seglen in _stretches(T):
        def body(t, carry, base=base, seglen=seglen):
            dst[pl.ds(pl.multiple_of(base + t * SEG, SEG), SEG), :] = src[pl.ds(base + t, SEG, stride=seglen), :]
            return carry
        lax.fori_loop(0, seglen, body, 0, unroll=8)


def _from_seg_order(src, dst, T):
    for base, seglen in _stretches(T):
        def body(t, carry, base=base, seglen=seglen):
            dst[pl.ds(base + t, SEG, stride=seglen), :] = src[pl.ds(pl.multiple_of(base + t * SEG, SEG), SEG), :]
            return carry
        lax.fori_loop(0, seglen, body, 0, unroll=8)


def _scan_specs(T):
    ublk = pl.BlockSpec((T, UB), lambda j: (0, j))
    lam = pl.BlockSpec((2, 1, 1, SB), lambda j: (0, j, 0, 0))
    mat = pl.BlockSpec((2, 1, UB, P), lambda j: (0, j, 0, 0))
    return ublk, lam, mat


def _dotf(a, b, mode="nn"):
    return lax.dot_general(a, b, _DN[mode], preferred_element_type=F32)


def _diag_mask():
    r = lax.broadcasted_iota(jnp.int32, (UB, SB), 0)
    c = lax.broadcasted_iota(jnp.int32, (UB, SB), 1)
    return lax.shift_right_logical(r, int(math.log2(CH))) == lax.shift_right_logical(c, int(math.log2(P)))


def _expand(m):
    p = lax.broadcasted_iota(jnp.int32, (P, SB), 0)
    c = lax.broadcasted_iota(jnp.int32, (P, SB), 1)
    tile = jnp.where(lax.bitwise_and(c, P - 1) == p, 1.0, 0.0).astype(BF16)
    wide = jnp.dot(m.astype(BF16), tile, preferred_element_type=F32)
    return jnp.where(_diag_mask(), wide, 0.0).astype(BF16)


def _collapse(full):
    c = lax.broadcasted_iota(jnp.int32, (SB, P), 0)
    p = lax.broadcasted_iota(jnp.int32, (SB, P), 1)
    pick = jnp.where(lax.bitwise_and(c, P - 1) == p, 1.0, 0.0).astype(BF16)
    return _exact_perm(jnp.where(_diag_mask(), full, 0.0), pick)


def _zero_state():
    return jnp.zeros((1, SB), F32), jnp.zeros((1, SB), F32)


def scan_fwd(u, lam_re, lam_im, bre, bim, cre, cim, name):
    T = u.shape[0]
    s_ctx, s_lat = LC // SEG, (T - LC) // SEG

    def body(u_ref, lr_ref, li_ref, bre_ref, bim_ref, cre_ref, cim_ref, y_ref, us, ys, sre, sim, fre, fim, ire, iim):
        _to_seg_order(u_ref, us, T)
        ub = us[...].astype(BF16)
        for d in range(2):
            lam8, (pw_c, pw_l) = _lam_tiles(lr_ref[d, 0], li_ref[d, 0], (s_ctx, s_lat))
            sre[...] = _dotf(ub, _expand(bre_ref[d, 0]))
            sim[...] = _dotf(ub, _expand(bim_ref[d, 0]))
            end_c, _ = _seg_scan(sre, sim, lam8, pw_c, 0, s_ctx, bool(d), _zero_state(), fre, fim, ire, iim)
            _seg_scan(sre, sim, lam8, pw_l, LC, s_lat, bool(d), end_c, fre, fim, ire, iim)
            y = (_dotf(sre[...].astype(BF16), _expand(cre_ref[d, 0]), "nt")
                 - _dotf(sim[...].astype(BF16), _expand(cim_ref[d, 0]), "nt"))
            if d == 0:
                ys[...] = y
            else:
                ys[...] += y
        _from_seg_order(ys, y_ref, T)

    ublk, lam, mat = _scan_specs(T)
    return pl.pallas_call(
        body, grid=(NJ,), in_specs=[ublk, lam, lam, mat, mat, mat, mat], out_specs=ublk,
        out_shape=jax.ShapeDtypeStruct((T, G * CH), F32),
        scratch_shapes=[pltpu.VMEM((T, UB), F32)] * 2 + [pltpu.VMEM((T, SB), F32)] * 2 + [pltpu.VMEM((SEG, SB), F32)] * 4,
        compiler_params=_cp(("arbitrary",)), name=name)(u, lam_re, lam_im, bre, bim, cre, cim)


def scan_bwd(u, dy, lam_re, lam_im, bre, bim, cre, cim, name):
    T = u.shape[0]
    s_ctx, s_lat = LC // SEG, (T - LC) // SEG

    def body(u_ref, dy_ref, lr_ref, li_ref, bre_ref, bim_ref, cre_ref, cim_ref,
             du_ref, dlr_ref, dli_ref, dbre_ref, dbim_ref, dcre_ref, dcim_ref,
             us, dys, dus, sre, sim, gre, gim, fre, fim, ic_re, ic_im, il_re, il_im, jre, jim):
        _to_seg_order(u_ref, us, T)
        _to_seg_order(dy_ref, dys, T)
        ub, dyb = us[...].astype(BF16), dys[...].astype(BF16)
        for d in range(2):
            rev = bool(d)
            lam8, (pw_c, pw_l) = _lam_tiles(lr_ref[d, 0], li_ref[d, 0], (s_ctx, s_lat))
            cam8, (cw_c, cw_l) = _lam_tiles(lr_ref[d, 0], li_ref[d, 0], (s_ctx, s_lat), conj=True)
            bre_v, bim_v = _expand(bre_ref[d, 0]), _expand(bim_ref[d, 0])
            sre[...] = _dotf(ub, bre_v)
            sim[...] = _dotf(ub, bim_v)
            end_c, _ = _seg_scan(sre, sim, lam8, pw_c, 0, s_ctx, rev, _zero_state(), fre, fim, ic_re, ic_im)
            _seg_scan(sre, sim, lam8, pw_l, LC, s_lat, rev, end_c, fre, fim, il_re, il_im)
            gre[...] = _dotf(dyb, _expand(cre_ref[d, 0]))
            gim[...] = -_dotf(dyb, _expand(cim_ref[d, 0]))
            end_g, acc_l = _seg_scan(gre, gim, cam8, cw_l, LC, s_lat, not rev, _zero_state(), fre, fim, jre, jim,
                                     prev=(sre, sim, il_re, il_im))
            _, acc_c = _seg_scan(gre, gim, cam8, cw_c, 0, s_ctx, not rev, end_g, fre, fim, jre, jim,
                                 prev=(sre, sim, ic_re, ic_im))
            dlr_ref[d, 0] = _sum0(acc_l[0] + acc_c[0])
            dli_ref[d, 0] = _sum0(acc_l[1] + acc_c[1])
            grb, gib = gre[...].astype(BF16), gim[...].astype(BF16)
            du = _dotf(grb, bre_v, "nt") + _dotf(gib, bim_v, "nt")
            if d == 0:
                dus[...] = du
            else:
                dus[...] += du
            dbre_ref[d, 0] = _collapse(_dotf(ub, grb, "tn"))
            dbim_ref[d, 0] = _collapse(_dotf(ub, gib, "tn"))
            dcre_ref[d, 0] = _collapse(_dotf(dyb, sre[...].astype(BF16), "tn"))
            dcim_ref[d, 0] = -_collapse(_dotf(dyb, sim[...].astype(BF16), "tn"))
        _from_seg_order(dus, du_ref, T)

    ublk, lam, mat = _scan_specs(T)
    lam_s = jax.ShapeDtypeStruct(lam_re.shape, F32)
    mat_s = jax.ShapeDtypeStruct(bre.shape, F32)
    return pl.pallas_call(
        body, grid=(NJ,), in_specs=[ublk, ublk, lam, lam, mat, mat, mat, mat],
        out_specs=[ublk, lam, lam, mat, mat, mat, mat],
        out_shape=[jax.ShapeDtypeStruct((T, G * CH), F32), lam_s, lam_s, mat_s, mat_s, mat_s, mat_s],
        scratch_shapes=[pltpu.VMEM((T, UB), F32)] * 3 + [pltpu.VMEM((T, SB), F32)] * 4 + [pltpu.VMEM((SEG, SB), F32)] * 8,
        compiler_params=_cp(("arbitrary",)), name=name)(u, dy, lam_re, lam_im, bre, bim, cre, cim)


class Exchange:
    def __init__(self, xs, modes):
        self.n = len(xs)
        self.modes = [modes] * self.n if isinstance(modes, (str, int)) else list(modes)
        self.out_shape = [jax.ShapeDtypeStruct(self._shape(x, md), x.dtype) for x, md in zip(xs, self.modes)]
        self.scratch = [pltpu.SemaphoreType.DMA((NDEV - 1, self.n)), pltpu.SemaphoreType.DMA((NDEV - 1, self.n)),
                        pltpu.SemaphoreType.DMA((self.n,))]
        self.specs = [pl.BlockSpec(memory_space=pl.ANY)] * self.n

    @staticmethod
    def _shape(x, mode):
        if mode == "gather":
            return (NDEV,) + tuple(x.shape)
        return tuple(x.shape) if mode == "lead" else (NDEV, x.shape[0], mode) + tuple(x.shape[2:])

    @staticmethod
    def _piece(x_ref, mode, dev):
        if mode == "gather":
            return x_ref
        return x_ref.at[dev] if mode == "lead" else x_ref.at[:, pl.ds(dev * mode, mode)]

    def _copies(self, x_refs, out_refs, sems):
        send_sems, recv_sems, local_sems = sems
        mx, my, mc = lax.axis_index("x"), lax.axis_index("y"), lax.axis_index("c")
        me = 4 * mx + 2 * my + mc
        peer_of = lambda k: (1 - mx if k & 4 else mx, 1 - my if k & 2 else my, 1 - mc if k & 1 else mc)
        local, first, relay, arrivals = [], [], [], []
        for a, (x_ref, out_ref) in enumerate(zip(x_refs, out_refs)):
            mode = self.modes[a]
            local.append(pltpu.make_async_copy(self._piece(x_ref, mode, me), out_ref.at[me], local_sems.at[a]))

            def remote(src, dst, k, pair, a=a):
                return pltpu.make_async_remote_copy(src_ref=src, dst_ref=dst, send_sem=send_sems.at[pair, a],
                                                    recv_sem=recv_sems.at[pair, a], device_id=peer_of(k), device_id_type=MESH_T)

            for k in range(1, NDEV):
                peer = peer_of(k)
                pid = 4 * peer[0] + 2 * peer[1] + peer[2]
                if mode != "gather":
                    src = self._piece(x_ref, mode, pid)
                    first.append(remote(src, out_ref.at[me], k, k - 1))
                    arrivals.append(remote(src, out_ref.at[pid], k, k - 1))
                elif k == 1:
                    first.append(remote(x_ref, out_ref.at[me], k, k - 1))
                    arrivals.append(remote(x_ref, out_ref.at[pid], k, k - 1))
                elif k % 2 == 0:
                    first.append(remote(x_ref, out_ref.at[me], k, k - 1))
                    relay.append((remote(x_ref, out_ref.at[pid], k, k - 1), remote(out_ref.at[pid], out_ref.at[pid], 1, k)))
                else:
                    arrivals.append(remote(x_ref, out_ref.at[pid], 1, k - 1))
        return local, first, relay, arrivals

    def start(self, x_refs, out_refs, sems):
        local, first, _, _ = self._copies(x_refs, out_refs, sems)
        for cp in local + first:
            cp.start()

    def finish(self, x_refs, out_refs, sems):
        local, first, relay, arrivals = self._copies(x_refs, out_refs, sems)
        for arrival, onward in relay:
            arrival.wait_recv()
            onward.start()
        for cp in arrivals:
            cp.wait_recv()
        for cp in first + [onward for _, onward in relay]:
            cp.wait_send()
        for cp in local:
            cp.wait()


def exchange(xs, modes, name):
    ex = Exchange(xs, modes)
    n = ex.n

    def body(*refs):
        ex.start(refs[:n], refs[n:2 * n], refs[2 * n:])
        ex.finish(refs[:n], refs[n:2 * n], refs[2 * n:])

    return pl.pallas_call(body, in_specs=ex.specs, out_specs=ex.specs, out_shape=ex.out_shape, scratch_shapes=ex.scratch,
                          compiler_params=pltpu.CompilerParams(has_side_effects=True), name=name)(*xs)


def _dot_f32(a, b, dn):
    return lax.dot_general(a, b, dn, preferred_element_type=F32, precision=lax.Precision.HIGHEST)


def ada_fwd(cg, c_ctx, ada_w, ada_b_loc, name):
    W = ada_w.shape[2]

    def body(cg_ref, cc_ref, w_ref, b_ref, o_ref):
        a = jnp.concatenate([_silu(cg_ref[...]), jnp.broadcast_to(_silu(cc_ref[...]), (NDEV, D))], axis=0)
        for i in range(2):
            o_ref[i] = _dot_f32(a, w_ref[i], _DN["nn"]) + b_ref[i]

    return pl.pallas_call(body, out_shape=jax.ShapeDtypeStruct((2, 2 * NDEV, W), F32),
                          compiler_params=_cp(), name=name)(cg, c_ctx, ada_w, ada_b_loc)


def ada_bwd(cg, c_ctx, ada_w, dm_loc, dm_all, name):
    W = ada_w.shape[2]

    def body(cg_ref, cc_ref, w_ref, dl_ref, da_ref, gw_ref, dcc_ref, gb_ref):
        a = jnp.concatenate([_silu(cg_ref[...]), jnp.broadcast_to(_silu(cc_ref[...]), (NDEV, D))], axis=0)
        dcc = jnp.zeros((1, D), F32)
        for i in range(2):
            dl = dl_ref[i]
            gw_ref[i] = _dot_f32(a, dl, _DN["tn"])
            dctx = jnp.sum(dl[NDEV:], axis=0, keepdims=True)
            dcc = dcc + _dot_f32(dctx, w_ref[i], _DN["nt"])
        dcc_ref[...] = dcc
        gb_ref[...] = jnp.sum(da_ref[...], axis=0)

    return pl.pallas_call(body, out_shape=[jax.ShapeDtypeStruct((2, D, W), F32), jax.ShapeDtypeStruct((1, D), F32),
                                           jax.ShapeDtypeStruct((2, 3 * D), F32)],
                          compiler_params=_cp(), name=name)(cg, c_ctx, ada_w, dm_loc, dm_all)


def cctx_finish(parts, c_ctx, name):
    def body(p_ref, cc_ref, o_ref):
        o_ref[...] = jnp.sum(p_ref[...], axis=0, keepdims=True) * _dsilu(cc_ref[...])

    return pl.pallas_call(body, out_shape=jax.ShapeDtypeStruct((1, D), F32), name=name)(parts, c_ctx)


def _adamw_update(g_ref, w_ref, m_ref, v_ref, go_ref, d_ref, mo_ref, vo_ref):
    g = g_ref[0].astype(F32)
    for s in range(1, g_ref.shape[0]):
        g = g + g_ref[s].astype(F32)
    mn = B1 * m_ref[...] + (1.0 - B1) * g
    vn = B2 * v_ref[...] + (1.0 - B2) * g * g
    go_ref[...] = g
    mo_ref[...] = mn
    vo_ref[...] = vn
    d_ref[...] = -LR * ((mn * (1.0 / (1.0 - B1 ** STEP))) / (jnp.sqrt(vn * (1.0 / (1.0 - B2 ** STEP))) + AEPS) + WD * w_ref[...])


ADAMW_PARTS = 4


def adamw_rows(items, name, rode=None, modes=None):
    in_specs, out_specs, out_shape, args = [], [], [], []
    for g, w, m, v in items:
        n, R, C = g.shape
        tr = R // ADAMW_PARTS
        spec = pl.BlockSpec((tr, C), lambda i, j: (i, 0))
        in_specs += [pl.BlockSpec((n, tr, C), lambda i, j: (0, i, 0)), spec, spec, spec]
        args += [g, w, m, v]
    for g, w, m, v in items:
        tr = w.shape[0] // ADAMW_PARTS
        out_specs += [pl.BlockSpec((tr, w.shape[1]), lambda i, j: (i, 0))] * 4
        out_shape += [jax.ShapeDtypeStruct(w.shape, F32)] * 4
    res, got = _ride_call(_adamw_body(len(items)), (ADAMW_PARTS, 1), in_specs, out_specs, out_shape,
                          Exchange(rode, modes) if rode else None, rode, name, args)
    return [res[4 * t:4 * t + 4] for t in range(len(items))], got


def _adamw_body(k):
    def body(*refs):
        for t in range(k):
            _adamw_update(*refs[4 * t:4 * t + 4], *refs[4 * k + 4 * t:4 * k + 4 * t + 4])
    return body


def adamw_multi(items, grid, name):
    k = len(items)
    ins, in_specs, out_specs, out_shape = [], [], [], []
    for g, g_spec, w, m, v, w_spec in items:
        ins += [g, w, m, v]
        in_specs += [g_spec, w_spec, w_spec, w_spec]
    for g, g_spec, w, m, v, w_spec in items:
        out_specs += [w_spec] * 4
        out_shape += [jax.ShapeDtypeStruct(w.shape, F32)] * 4
    res = pl.pallas_call(_adamw_body(k), grid=grid, in_specs=in_specs, out_specs=out_specs, out_shape=out_shape,
                         compiler_params=_cp(("arbitrary",) * len(grid)), name=name)(*ins)
    return [res[4 * t:4 * t + 4] for t in range(k)]


def _whole(a, grid_rank):
    zeros = (0,) * a.ndim
    return pl.BlockSpec(a.shape, lambda *idx: zeros)


def sum_slots(xs, name):
    def body(*refs):
        for x_ref, o_ref in zip(refs[:len(xs)], refs[len(xs):]):
            acc = x_ref[0]
            for s in range(1, NDEV):
                acc = acc + x_ref[s]
            o_ref[...] = acc

    return pl.pallas_call(body, out_shape=[jax.ShapeDtypeStruct(x.shape[1:], F32) for x in xs],
                          compiler_params=_cp(), name=name)(*xs)


def _col_shards(g):
    R, N = g.shape
    return g.reshape(R, NDEV, N // NDEV).transpose(1, 0, 2)


def _vec2(v):
    return jnp.broadcast_to(v.reshape(1, 1, -1), (2, 1, v.size))


SHARD_ROWS = {"mla_w_in": 192, "mla_w_uq": 192, "mla_w_ukv": 256, "s5_w_in": 256}


def _t_shard(wsh, rows):
    t = wsh[0].T.astype(BF16)
    return jnp.pad(t, ((0, rows - t.shape[0]), (0, 0)))


def _win_order():
    w = IN_W // NDEV
    perm = np.zeros((IN_WP, NDEV * SHARD_ROWS["mla_w_in"]), np.float32)
    first = QL + KVL + ROPE
    for c in range(IN_W):
        n = c + HEADS * VD if c < first else c - first
        perm[n, (c // w) * SHARD_ROWS["mla_w_in"] + c % w] = 1.0
    return jnp.asarray(perm, BF16)


def local_step(ctx, x, tgt, mod, Wt, small, l1_shards):
    T = LC + x.shape[0]
    xa = ("cat", ctx, x)
    sh = [mod[i, :, None, 0:D] for i in range(2)]
    sc = [mod[i, :, None, D:2 * D] for i in range(2)]
    gt = [mod[i, :, None, 2 * D:] for i in range(2)]
    ng = [_vec2(small["norm_g"][i]) for i in range(2)]
    qg, kvg = _vec2(small["mla_q_norm"]), _vec2(small["mla_kv_norm"])
    cosf, sinf, pm, pmt = _rope_tables(T)

    (h0, p0, cqn, ckvn), _ = rowwise(st_l0_pre, [xa], [ng[0], sc[0], sh[0], qg, kvg],
                                     [(D, BF16), (IN_WP, F32), (QL, BF16), (KVL, BF16)], [], "l0_pre", mats=[Wt["mla_w_in"]])
    z0, cq, ckv = (p0, 0, HEADS * VD), (p0, HEADS * VD // QL, QL), (p0, (HEADS * VD + QL) // KVL, KVL)
    Q = project_q(cqn, Wt["mla_w_uq"], "l0_uq")
    K, V = project_kv(ckvn, Wt["mla_w_ukv"], p0, (HEADS * VD + QL + KVL) // 128, "l0_ukv")
    (o, lse), got = attn_fwd(Q, K, V, "l0_attn", rode=l1_shards, modes="gather")
    Wt, small = dict(Wt), dict(small)
    for n, a in zip(L1_BIG, got):
        Wt[n] = a.reshape(-1, a.shape[-1])
    vecs = lax.bitcast_convert_type(got[-1].reshape(NDEV, 2, -1, 2), F32)
    small["s5_d"], small["s5_b_glu"] = vecs[:, 0, :].reshape(D), vecs[:, 1, :].reshape(D)
    o2 = o.transpose(1, 0, 2).reshape(T, HEADS * VD)
    (og, out0, x1), _ = rowwise(st_l0_post, [o2, z0, xa], [gt[0]], [(D, BF16), (D, BF16), (D, F32)], [], "l0_post",
                                mats=[Wt["mla_w_out"]])

    ls = small["s5_log_step"].reshape(2, G, 1)
    a_re, a_im = small["s5_a_re"].reshape(2, G, P), small["s5_a_im"].reshape(2, G, P)
    b_re = small["s5_b_re"].reshape(2, G, P, CH).transpose(0, 1, 3, 2)
    b_im = small["s5_b_im"].reshape(2, G, P, CH).transpose(0, 1, 3, 2)
    lam_re, lam_im, f_re, f_im = disc_fwd(a_re, a_im, ls, "s5_disc")
    f_re2, f_im2 = f_re.reshape(2, G, 1, P), f_im.reshape(2, G, 1, P)
    bb_re, bb_im = disc_b(f_re2, f_im2, b_re, b_im, "s5_disc_b")
    compact = lambda m: m.reshape(2, NJ, UB, P)
    bre, bim = compact(bb_re), compact(bb_im)
    cre, cim = compact(small["s5_c_re"]), compact(small["s5_c_im"])
    lam_re4, lam_im4 = lam_re.reshape(2, NJ, 1, SB), lam_im.reshape(2, NJ, 1, SB)

    (h1, p1), _ = rowwise(st_l1_pre, [x1], [ng[1], sc[1], sh[1]], [(D, BF16), (2 * D, F32)], [], "l1_pre", mats=[Wt["s5_w_in"]])
    u, z1 = (p1, 0, D), (p1, 1, D)
    yssm = scan_fwd(p1, lam_re4, lam_im4, bre, bim, cre, cim, "s5_scan")
    dvec, bglu = _vec2(small["s5_d"]), _vec2(small["s5_b_glu"])
    fg = _vec2(small["final_g"])
    lat_mask = jnp.stack([jnp.zeros((1, D), F32), jnp.ones((1, D), F32)])
    (y, y1b, gl, y3, out1, dx2), (dfg, lvec) = rowwise(
        st_l1_mlp, [yssm, u, z1, x1, ("lat", tgt)], [dvec, bglu, gt[1], fg, lat_mask],
        [(D, F32), (D, BF16), (D, BF16), (D, BF16), (D, BF16), (D, F32)], [D, 128], "l1_mlp",
        mats=[Wt["s5_w_glu"], Wt["s5_w_out"]])

    (dz1, dy, du_d), (dgt1, dbglu, dd), (g_w_out5, g_w_glu) = rowwise(
        st_l1_mlp_bwd, [dx2, out1, y3, y, gl, z1, u, y1b], [gt[1], bglu, dvec], [(D, BF16), (D, F32), (D, F32)], [D, D, D],
        "l1_mlp_b", mats=[Wt["s5_w_out"], Wt["s5_w_glu"]], out_accs=[(D, D), (D, D)])
    du_s, dlr, dli, dbre, dbim, dcre, dcim = scan_bwd(p1, dy, lam_re4, lam_im4, bre, bim, cre, cim, "s5_scan_b")
    dbb_re, dbb_im = dbre.reshape(2, G, CH, P), dbim.reshape(2, G, CH, P)
    g_c_re, g_c_im = dcre.reshape(2, G, CH, P), dcim.reshape(2, G, CH, P)
    gt_b_re, gt_b_im, dfr, dfi = disc_b_bwd(f_re2, f_im2, b_re, b_im, dbb_re, dbb_im, "s5_disc_b_b")
    g_b_re, g_b_im = gt_b_re.transpose(0, 1, 3, 2), gt_b_im.transpose(0, 1, 3, 2)
    g_a_re, g_a_im, g_ls = disc_a_bwd(a_re, a_im, ls, dlr.reshape(2, G, P), dli.reshape(2, G, P),
                                      dfr.reshape(2, G, P), dfi.reshape(2, G, P), "s5_disc_b_a")
    (dx1,), (dsh1, dsc1, dng1), (g_w_in5,) = rowwise(
        st_l1_tail_bwd, [du_d, du_s, dz1, h1, x1, dx2], [ng[1], sc[1]], [(D, F32)], [D, D, D], "l1_pre_b",
        mats=[Wt["s5_w_in"]], out_accs=[(NDEV, D, 2 * D // NDEV)])

    (do2, dz0), (dgt0,), (g_w_out,) = rowwise(st_l0_post_bwd, [dx1, out0, og, o2, z0], [gt[0]], [(D, F32), (D, F32)], [D],
                                              "l0_post_b", mats=[Wt["mla_w_out"]], out_accs=[(D, D)])
    doh = do2.reshape(T, HEADS, VD).transpose(1, 0, 2)
    rows8 = lambda g: g.reshape(NDEV, -1, g.shape[-1])
    both = lambda s: s[0, 0] + s[1, 0]
    dense = lambda g: g.reshape(2, G * P * CH // 128, 128)
    chunks = [dense(g_b_re), dense(g_b_im), g_c_re, g_c_im]
    l1_send = [g_w_in5, rows8(g_w_glu), rows8(g_w_out5), rows8(g_w_out),
               both(dd).reshape(NDEV, 1, -1), both(dbglu).reshape(NDEV, 1, -1)]
    (dQ, dK, dV), l1_recv = attn_bwd(Q, K, V, o, lse, doh, "l0_attn_b", rode=l1_send + chunks,
                                     modes=["lead"] * len(l1_send) + [a.shape[1] // NDEV for a in chunks])
    dqh = rope(dQ, cosf, sinf, pmt, True, BF16, "l0_rope_q_b", scale=SCALE)
    dq = dqh.transpose(1, 0, 2).reshape(T, HEADS * QK)
    n_owned = len(l1_send)
    reduced = sum_slots(l1_recv[n_owned:], "sum_chunks")
    (dkv, dkr), chunk_all = split_kv_grads(dK, dV, "l0_kv_b", rode=[jnp.stack(reduced[:2]), jnp.stack(reduced[2:])],
                                           modes="gather")
    (grad_x,), (dqg, dkvg, dsh0, dsc0, dng0), (g_uq, g_ukv, g_p) = rowwise(
        st_l0_tail_bwd, [dq, dkv, dkr, dz0, cq, ckv, cqn, ckvn, h0, xa, dx1], [qg, kvg, ng[0], sc[0]],
        [(D, F32, "lat")], [QL, KVL, D, D, D], "l0_pre_b", mats=[Wt["mla_w_uq"], Wt["mla_w_ukv"], Wt["mla_w_in"]],
        out_accs=[(QL, HEADS * QK), (KVL, HEADS * KVW), (D, IN_WP)])
    g_w_uq, g_w_ukv = _col_shards(g_uq).astype(BF16), _col_shards(g_ukv).astype(BF16)
    g_w_in = _col_shards(jnp.concatenate([g_p[:, HEADS * VD:IN_W], g_p[:, :HEADS * VD]], axis=1)).astype(BF16)

    dmod = jnp.stack([jnp.concatenate([dsh0, dsc0, dgt0], axis=-1)[:, 0], jnp.concatenate([dsh1, dsc1, dgt1], axis=-1)[:, 0]])
    gbig = {"mla_w_in": g_w_in, "mla_w_uq": g_w_uq, "mla_w_ukv": g_w_ukv}
    gsmall = {"norm_g": jnp.stack([both(dng0), both(dng1)]), "mla_q_norm": both(dqg), "mla_kv_norm": both(dkvg),
              "s5_a_re": g_a_re, "s5_a_im": g_a_im, "s5_log_step": g_ls, "final_g": dfg[1, 0]}
    return lvec[1], grad_x, dmod, gbig, gsmall, l1_recv[:n_owned], chunk_all


COL_SHARDED = ("mla_w_in", "mla_w_uq", "mla_w_ukv", "s5_w_in")
ROW_SHARDED = ("mla_w_out", "s5_w_glu", "s5_w_out")
VEC_SHARDED = ("s5_d", "s5_b_glu")
BIG = COL_SHARDED + ROW_SHARDED
L0_BIG = ("mla_w_in", "mla_w_uq", "mla_w_ukv")
L1_BIG = ("s5_w_in", "s5_w_glu", "s5_w_out", "mla_w_out")
BITS16 = jnp.bfloat16
SMALL_RS = ("norm_g", "mla_q_norm", "mla_kv_norm", "s5_a_re", "s5_a_im", "s5_log_step", "s5_b_re", "s5_b_im",
            "s5_c_re", "s5_c_im", "final_g")
CHUNKED = ("s5_b_re", "s5_b_im", "s5_c_re", "s5_c_im")
DENSE = ("s5_b_re", "s5_b_im")
TINY = ("norm_g", "mla_q_norm", "mla_kv_norm", "s5_a_re", "s5_a_im", "s5_log_step", "final_g")
ORDER = ("c_ctx", "ada_w", "ada_b", "norm_g", "mla_w_in", "mla_q_norm", "mla_w_uq", "mla_kv_norm", "mla_w_ukv",
         "mla_w_out", "s5_w_in", "s5_a_re", "s5_a_im", "s5_log_step", "s5_b_re", "s5_b_im", "s5_c_re", "s5_c_im",
         "s5_d", "s5_w_glu", "s5_b_glu", "s5_w_out", "final_g")


def kernel(x, c, ctx, c_ctx, ada_w, ada_b, norm_g, mla_w_in, mla_q_norm, mla_w_uq, mla_kv_norm, mla_w_ukv, mla_w_out, s5_w_in, s5_a_re, s5_a_im, s5_log_step, s5_b_re, s5_b_im, s5_c_re, s5_c_im, s5_d, s5_w_glu, s5_b_glu, s5_w_out, final_g, loss_target, m_c_ctx, m_ada_w, m_ada_b, m_norm_g, m_mla_w_in, m_mla_q_norm, m_mla_w_uq, m_mla_kv_norm, m_mla_w_ukv, m_mla_w_out, m_s5_w_in, m_s5_a_re, m_s5_a_im, m_s5_log_step, m_s5_b_re, m_s5_b_im, m_s5_c_re, m_s5_c_im, m_s5_d, m_s5_w_glu, m_s5_b_glu, m_s5_w_out, m_final_g, v_c_ctx, v_ada_w, v_ada_b, v_norm_g, v_mla_w_in, v_mla_q_norm, v_mla_w_uq, v_mla_kv_norm, v_mla_w_ukv, v_mla_w_out, v_s5_w_in, v_s5_a_re, v_s5_a_im, v_s5_log_step, v_s5_b_re, v_s5_b_im, v_s5_c_re, v_s5_c_im, v_s5_d, v_s5_w_glu, v_s5_b_glu, v_s5_w_out, v_final_g):
    w = dict(c_ctx=c_ctx, ada_w=ada_w, ada_b=ada_b, norm_g=norm_g, mla_w_in=mla_w_in, mla_q_norm=mla_q_norm,
             mla_w_uq=mla_w_uq, mla_kv_norm=mla_kv_norm, mla_w_ukv=mla_w_ukv, mla_w_out=mla_w_out, s5_w_in=s5_w_in,
             s5_a_re=s5_a_re, s5_a_im=s5_a_im, s5_log_step=s5_log_step, s5_b_re=s5_b_re, s5_b_im=s5_b_im,
             s5_c_re=s5_c_re, s5_c_im=s5_c_im, s5_d=s5_d, s5_w_glu=s5_w_glu, s5_b_glu=s5_b_glu, s5_w_out=s5_w_out,
             final_g=final_g)
    m = dict(c_ctx=m_c_ctx, ada_w=m_ada_w, ada_b=m_ada_b, norm_g=m_norm_g, mla_w_in=m_mla_w_in, mla_q_norm=m_mla_q_norm,
             mla_w_uq=m_mla_w_uq, mla_kv_norm=m_mla_kv_norm, mla_w_ukv=m_mla_w_ukv, mla_w_out=m_mla_w_out,
             s5_w_in=m_s5_w_in, s5_a_re=m_s5_a_re, s5_a_im=m_s5_a_im, s5_log_step=m_s5_log_step, s5_b_re=m_s5_b_re,
             s5_b_im=m_s5_b_im, s5_c_re=m_s5_c_re, s5_c_im=m_s5_c_im, s5_d=m_s5_d, s5_w_glu=m_s5_w_glu,
             s5_b_glu=m_s5_b_glu, s5_w_out=m_s5_w_out, final_g=m_final_g)
    v = dict(c_ctx=v_c_ctx, ada_w=v_ada_w, ada_b=v_ada_b, norm_g=v_norm_g, mla_w_in=v_mla_w_in, mla_q_norm=v_mla_q_norm,
             mla_w_uq=v_mla_w_uq, mla_kv_norm=v_mla_kv_norm, mla_w_ukv=v_mla_w_ukv, mla_w_out=v_mla_w_out,
             s5_w_in=v_s5_w_in, s5_a_re=v_s5_a_re, s5_a_im=v_s5_a_im, s5_log_step=v_s5_log_step, s5_b_re=v_s5_b_re,
             s5_b_im=v_s5_b_im, s5_c_re=v_s5_c_re, s5_c_im=v_s5_c_im, s5_d=v_s5_d, s5_w_glu=v_s5_w_glu,
             s5_b_glu=v_s5_b_glu, s5_w_out=v_s5_w_out, final_g=v_final_g)

    me = 4 * lax.axis_index("x") + 2 * lax.axis_index("y") + lax.axis_index("c")
    WA = ada_w.shape[2]

    def shard(n):
        return _t_shard(w[n], SHARD_ROWS[n]) if n in COL_SHARDED else w[n][0].astype(BF16)

    wgot = exchange([c] + [shard(n) for n in L0_BIG], "gather", "gather_w")

    cg = wgot[0].reshape(NDEV, D)
    cc2 = c_ctx.reshape(1, D)
    ada_b_loc = lax.dynamic_slice_in_dim(ada_b.reshape(2, 3 * D // WA, WA), me, 1, axis=1)
    part = ada_fwd(cg, cc2, ada_w, ada_b_loc, "ada_fwd")
    pg = exchange([part], "gather", "gather_mod")[0]
    mod_l = lax.dynamic_index_in_dim(pg, me, axis=2, keepdims=False).transpose(1, 0, 2).reshape(2, 3 * D)
    mod_c = pg[:, :, NDEV, :].transpose(1, 0, 2).reshape(2, 3 * D)
    mod = jnp.stack([mod_c, mod_l], axis=1)

    Wt = {n: a.reshape(-1, a.shape[-1]) for n, a in zip(L0_BIG, wgot[1:])}
    Wt["mla_w_in"] = mm(_win_order(), Wt["mla_w_in"], "nn", "w_in_order", out_dtype=BF16)
    vec_bits = lax.bitcast_convert_type(jnp.concatenate([s5_d, s5_b_glu], axis=0), BITS16).reshape(2, -1)
    small = {n: w[n] for n in SMALL_RS}

    lvec, grad_x, dmod, gbig, gsmall, l1_recv, (bb_all, cc_all) = local_step(
        ctx[0], x[0], loss_target[0], mod, Wt, small, [shard(n) for n in L1_BIG] + [vec_bits])
    grad_x = grad_x[None]

    recv = dict(zip(L1_BIG + VEC_SHARDED, l1_recv))
    out = {}

    def keep(n, res):
        for key, arr in zip("gdmv", res):
            out[key, n] = arr.reshape(w[n].shape)

    kshape = lambda n: w[n].shape if w[n].ndim > 1 else (1, w[n].size)
    flat = jnp.concatenate([gsmall[n].reshape(-1) for n in TINY] + [dmod.reshape(-1), lvec.reshape(-1)])[None]
    *l0_recv, flat_all = exchange([gbig[n] for n in L0_BIG] + [flat], ["lead"] * len(L0_BIG) + ["gather"], "scatter_grads")
    chunk_all = [bb_all[:, 0], bb_all[:, 1], cc_all[:, 0], cc_all[:, 1]]
    tiny_all, off = [], 0
    for n in TINY:
        tiny_all.append(flat_all[:, 0, off:off + w[n].size].reshape((NDEV,) + kshape(n)))
        off += w[n].size
    dm_all = flat_all[:, 0, off:off + dmod.size].reshape((NDEV,) + dmod.shape)
    loss = sum_slots([flat_all[:, :, off + dmod.size:]], "loss_sum")[0][0, 0]

    dm_cols = lax.dynamic_slice_in_dim(dm_all.reshape(NDEV, 2, 2, 3 * D // WA, WA), me, 1, axis=3)[:, :, :, 0]
    dm_loc = jnp.concatenate([dm_cols[:, :, 1].transpose(1, 0, 2), dm_cols[:, :, 0].transpose(1, 0, 2)], axis=1)
    g_ada_w, dcc_part, g_ada_b = ada_bwd(cg, cc2, ada_w, dm_loc, dm_all.transpose(0, 2, 1, 3).reshape(2 * NDEV, 2, 3 * D), "ada_bwd")
    dcc_all = exchange([dcc_part], "gather", "gather_dcc")[0].reshape(NDEV, D)
    g_c_ctx = cctx_finish(dcc_all, cc2, "cctx_finish")

    flat2 = lambda t: t.reshape(-1, t.shape[-1])
    recv.update(dict(zip(L0_BIG, l0_recv)))
    big = [(recv[n], w[n][0], m[n][0], v[n][0]) for n in BIG]
    big.append((flat2(g_ada_w)[None], flat2(ada_w), flat2(m_ada_w), flat2(v_ada_w)))
    for n, r in zip(BIG + ("ada_w",), adamw_rows(big, "adamw_big")[0]):
        keep(n, r)
    items = []
    halves = 2
    for n, g in zip(CHUNKED, chunk_all):
        blk = (1, 1, G // halves) + w[n].shape[3:]
        g = jnp.moveaxis(g, 0, 1).reshape(w[n].shape)
        g_spec = pl.BlockSpec((1,) + blk, lambda d, s: (0, 0, d, s, 0, 0))
        items.append((g[None], g_spec, w[n], m[n], v[n], pl.BlockSpec(blk, lambda d, s: (0, d, s, 0, 0))))
    for n, res in zip(CHUNKED, adamw_multi(items, (2, halves), "adamw_bc")):
        keep(n, res)
    tiny_g = dict(zip(TINY, tiny_all))
    tiny_g.update({n: recv[n] for n in VEC_SHARDED})
    tiny_g["c_ctx"], tiny_g["ada_b"] = g_c_ctx[None], g_ada_b[None]
    names = list(tiny_g)
    items = [(tiny_g[n], _whole(tiny_g[n], 1)) + tuple(t[n].reshape(kshape(n)) for t in (w, m, v))
             + (pl.BlockSpec(kshape(n), lambda i, r=len(kshape(n)): (0,) * r),) for n in names]
    for n, res in zip(names, adamw_multi(items, (1,), "adamw_small")):
        keep(n, res)

    return (loss, grad_x, *[out["g", n] for n in ORDER], *[out["d", n] for n in ORDER],
            *[out["m", n] for n in ORDER], *[out["v", n] for n in ORDER])
```

```python
import math

import numpy as np
import jax
import jax.numpy as jnp
from jax import lax
from jax.experimental import pallas as pl
from jax.experimental.pallas import tpu as pltpu

F32 = jnp.float32
BF16 = jnp.bfloat16

D = 1024
L = 2048
LC = 256
NDEV = 8
GRID_W = 64
EPS = 1e-6
HEADS = 16
NOPE = 64
ROPE = 32
QK = NOPE + ROPE
VD = 64
IN_W = 256 + 128 + ROPE + HEADS * 64
IN_WP = 1536
QL = 256
KVL = 128
SCALE = QK ** -0.5
LOG2E = math.log2(math.e)
THETA = 10000.0
G = 64
P = 64
CH = 16
GB = 8
NJ = G // GB
UB = GB * CH
SB = GB * P
SEG = 16
TB = 256
VMEM_LIMIT = 56 * 1024 * 1024
B1, B2, LR, AEPS, WD, STEP = 0.9, 0.999, 0.001, 1e-8, 0.01, 10
MESH_T = pl.DeviceIdType.MESH


def _cp(sem=None):
    return pltpu.CompilerParams(dimension_semantics=sem, vmem_limit_bytes=VMEM_LIMIT)


def _sig(x):
    return 1.0 / (1.0 + jnp.exp(-x))


def _silu(x):
    return x * _sig(x)


def _dsilu(x):
    s = _sig(x)
    return s * (1.0 + x * (1.0 - s))


_GK = math.sqrt(2.0 / math.pi)


def _gelu(x):
    return 0.5 * x * (1.0 + jnp.tanh(_GK * (x + 0.044715 * x * x * x)))


def _dgelu(x):
    t = jnp.tanh(_GK * (x + 0.044715 * x * x * x))
    return 0.5 * (1.0 + t) + 0.5 * x * (1.0 - t * t) * _GK * (1.0 + 3 * 0.044715 * x * x)


def _rs(x):
    return lax.rsqrt(jnp.mean(x * x, axis=-1, keepdims=True) + EPS)


def _sum0(x):
    return jnp.sum(x, axis=0, keepdims=True)


def st_norm_mod(x, g, sc, sh):
    y = x * _rs(x) * g
    return (y * (1.0 + sc) + sh,), ()


def st_norm_mod_bwd(x, dh, dres, g, sc):
    r = _rs(x)
    xn = x * r
    y = xn * g
    dy = dh * (1.0 + sc)
    dxn = dy * g
    dx = r * (dxn - xn * jnp.mean(dxn * xn, axis=-1, keepdims=True))
    return (dres + dx,), (_sum0(dh), _sum0(dh * y), _sum0(dy * xn))


def st_rms(x, g):
    return (x * _rs(x) * g,), ()


def st_rms_bwd(x, dy, g):
    r = _rs(x)
    n = x * r
    dn = dy * g
    dx = r * (dn - n * jnp.mean(dn * n, axis=-1, keepdims=True))
    return (dx,), (_sum0(dy * n),)


def st_rms2(x1, x2, g1, g2):
    return st_rms(x1, g1)[0] + st_rms(x2, g2)[0], ()


def st_rms2_bwd(x1, dy1, x2, dy2, g1, g2):
    (d1,), (s1,) = st_rms_bwd(x1, dy1, g1)
    (d2,), (s2,) = st_rms_bwd(x2, dy2, g2)
    return (d1, d2), (s1, s2)


def st_gate_bwd(dog, o, z):
    return (dog * _silu(z), dog * o * _dsilu(z)), ()


def st_resid_bwd(dx, out, gt):
    return (dx * gt,), (_sum0(dx * out),)


def st_s5a(yssm, u, d):
    y = yssm + d * u
    return (y, _gelu(y)), ()


def st_s5b_bwd(dy3, y, gl, z, b):
    y1 = _gelu(y)
    s = _sig(gl + b)
    dy2 = dy3 * _silu(z)
    dz = dy3 * y1 * s * _dsilu(z)
    dgl = dy2 * y1 * s * (1.0 - s)
    return (dgl, dz, dy2 * s), (_sum0(dgl),)


def st_s5a_bwd(dy1a, dy1b, y, u, d):
    dy = (dy1a + dy1b) * _dgelu(y)
    return (dy, dy * d), (_sum0(dy * u),)


def st_l0_pre(x, g, sc, sh, qg, kvg, w_in):
    hb = st_norm_mod(x, g, sc, sh)[0][0].astype(BF16)
    p = lax.dot_general(hb, w_in, _DN["nt"], preferred_element_type=F32)
    cq, ckv = p[:, HEADS * VD:HEADS * VD + QL], p[:, HEADS * VD + QL:HEADS * VD + QL + KVL]
    return (hb, p) + st_rms2(cq, ckv, qg, kvg)[0], ()


def st_l0_tail_bwd(dq, dkv, dkr, dz, cq, ckv, cqn, ckvn, h, x, dres, qg, kvg, g, sc, w_uq, w_ukv, w_in):
    dcqn = jnp.dot(dq, w_uq, preferred_element_type=F32)
    dckvn = jnp.dot(dkv, w_ukv, preferred_element_type=F32)
    (dcq, dckv), (dqg, dkvg) = st_rms2_bwd(cq, dcqn, ckv, dckvn, qg, kvg)
    dp = jnp.concatenate([dz, dcq, dckv, dkr], axis=1).astype(BF16)
    dh = jnp.dot(dp, w_in, preferred_element_type=F32)
    outs, sums = st_norm_mod_bwd(x, dh, dres, g, sc)
    tn = lambda a, b: lax.dot_general(a, b, _DN["tn"], preferred_element_type=F32)
    return outs, (dqg, dkvg) + sums, (tn(cqn, dq), tn(ckvn, dkv), tn(h, dp))


def st_l1_pre(x, g, sc, sh, w_in):
    hb = st_norm_mod(x, g, sc, sh)[0][0].astype(BF16)
    return (hb, lax.dot_general(hb, w_in, _DN["nt"], preferred_element_type=F32)), ()


def st_l1_tail_bwd(du_a, du_b, dz, h, x, dres, g, sc, w_in):
    dp = jnp.concatenate([(du_a + du_b).astype(BF16), dz], axis=1)
    dh = jnp.dot(dp, w_in, preferred_element_type=F32)
    outs, sums = st_norm_mod_bwd(x, dh, dres, g, sc)
    w = dp.shape[1] // NDEV
    shards = [lax.dot_general(h, dp[:, r * w:(r + 1) * w], _DN["tn"], preferred_element_type=F32) for r in range(NDEV)]
    return outs, sums, (jnp.stack(shards),)


def st_l0_post(o, z, x, gt, w_out):
    og = (o * _silu(z)).astype(BF16)
    out = jnp.dot(og, w_out, preferred_element_type=F32)
    return (og, out, x + gt * out), ()


def st_l0_post_bwd(dx1, out, og, o, z, gt, w_out):
    (dout,), (dgt,) = st_resid_bwd(dx1, out.astype(F32), gt)
    doutb = dout.astype(BF16)
    dog = lax.dot_general(doutb, w_out, _DN["nt"], preferred_element_type=F32)
    return st_gate_bwd(dog, o, z)[0], (dgt,), (lax.dot_general(og, doutb, _DN["tn"], preferred_element_type=F32),)


def st_l1_mlp(yssm, u, z, x1, tgt, d, bglu, gt, fg, mask, w_glu, w_out):
    (y, y1), _ = st_s5a(yssm, u, d)
    y1b = y1.astype(BF16)
    gl = jnp.dot(y1b, w_glu, preferred_element_type=F32)
    y3 = (y1 * _sig(gl + bglu) * _silu(z)).astype(BF16)
    out = jnp.dot(y3, w_out, preferred_element_type=F32)
    (dx2,), sums = st_final(x1 + gt * out, tgt, fg, mask)
    return (y, y1b, gl, y3, out, dx2), sums


def st_l1_mlp_bwd(dx2, out, y3, y, gl, z, u, y1b, gt, bglu, d, w_out, w_glu):
    out, gl = out.astype(F32), gl.astype(F32)
    (dout,), (dgt,) = st_resid_bwd(dx2, out, gt)
    doutb = dout.astype(BF16)
    dy3 = lax.dot_general(doutb, w_out, _DN["nt"], preferred_element_type=F32)
    (dgl, dz, dy1a), (dbglu,) = st_s5b_bwd(dy3, y, gl, z, bglu)
    dglb = dgl.astype(BF16)
    dy1b = lax.dot_general(dglb, w_glu, _DN["nt"], preferred_element_type=F32)
    (dy, du), (dd,) = st_s5a_bwd(dy1a, dy1b, y, u, d)
    g_w_out = lax.dot_general(y3, doutb, _DN["tn"], preferred_element_type=F32)
    g_w_glu = lax.dot_general(y1b, dglb, _DN["tn"], preferred_element_type=F32)
    return (dz, dy, du), (dgt, dbglu, dd), (g_w_out, g_w_glu)


def st_final(x2, tgt, g, mask):
    r = _rs(x2)
    n = x2 * r
    e = n * g - tgt
    dyo = e * (1.0 / D)
    dn = dyo * g
    dx = r * (dn - n * jnp.mean(dn * n, axis=-1, keepdims=True))
    lsum = jnp.sum(_sum0(e * e), axis=1, keepdims=True) * (0.5 / D)
    return (dx * mask,), (_sum0(dyo * n), jnp.broadcast_to(lsum, (1, 128)))


def rowwise(fn, rows, vecs, out_rows, out_sums, name, mats=(), out_accs=()):
    lat_blk = lambda i: jnp.maximum(i - 1, 0)
    arrays, in_specs, pick = [], [], []
    for a in rows:
        if not isinstance(a, tuple):
            a = (a, 0, a.shape[1])
        tag = a[0] if isinstance(a[0], str) else None
        if tag == "cat":
            _, ctx, x = a
            arrays += [ctx, x]
            in_specs += [pl.BlockSpec((TB, ctx.shape[1]), lambda i: (0, 0)),
                         pl.BlockSpec((TB, x.shape[1]), lambda i: (lat_blk(i), 0))]
            pick.append(2)
        elif tag == "lat":
            arrays.append(a[1])
            in_specs.append(pl.BlockSpec((TB, a[1].shape[1]), lambda i: (lat_blk(i), 0)))
            pick.append(1)
        else:
            arr, cb, width = a
            arrays.append(arr)
            in_specs.append(pl.BlockSpec((TB, width), lambda i, cb=cb: (i, cb)))
            pick.append(1)
    T = LC + L
    nin, nv, nm, no, ns = len(arrays), len(vecs), len(mats), len(out_rows), len(out_sums)

    def body(*refs):
        i = pl.program_id(0)
        vals, k = [], 0
        for p in pick:
            if p == 2:
                vals.append(jnp.where(i == 0, refs[k][...], refs[k + 1][...]))
            else:
                vals.append(refs[k][...])
            k += p
        vals += [r[0] for r in refs[nin:nin + nv]] + [r[...] for r in refs[nin + nv:nin + nv + nm]]
        res = fn(*vals)
        first_out = nin + nv + nm
        for r, o in zip(refs[first_out:first_out + no], res[0]):
            r[...] = o.astype(r.dtype)
        sum_refs = refs[first_out + no:first_out + no + ns]
        if sum_refs:
            @pl.when(i <= 1)
            def _():
                for r in sum_refs:
                    r[...] = jnp.zeros_like(r)
            for r, s in zip(sum_refs, res[1]):
                r[0] += s
        na = len(out_accs)
        if na:
            acc_out, acc = refs[first_out + no + ns:first_out + no + ns + na], refs[first_out + no + ns + na:]

            @pl.when(i == 0)
            def _():
                for r in acc:
                    r[...] = jnp.zeros_like(r)
            for r, a in zip(acc, res[2]):
                r[...] += a

            @pl.when(i == T // TB - 1)
            def _():
                for o, r in zip(acc_out, acc):
                    o[...] = r[...].astype(o.dtype)

    kind = lambda i: (jnp.minimum(i, 1), 0, 0)
    in_specs += [pl.BlockSpec((1, 1, v.shape[2]), kind) for v in vecs]
    in_specs += [pl.BlockSpec(m.shape, lambda i: (0, 0), pipeline_mode=pl.Buffered(1)) for m in mats]
    out_specs, out_shape = [], []
    for o in out_rows:
        lat = len(o) == 3
        out_specs.append(pl.BlockSpec((TB, o[0]), (lambda i: (lat_blk(i), 0)) if lat else (lambda i: (i, 0))))
        out_shape.append(jax.ShapeDtypeStruct((L if lat else T, o[0]), o[1]))
    out_specs += [pl.BlockSpec((1, 1, c), kind) for c in out_sums]
    out_shape += [jax.ShapeDtypeStruct((2, 1, c), F32) for c in out_sums]
    out_specs += [pl.BlockSpec(s, lambda i, r=len(s): (0,) * r) for s in out_accs]
    out_shape += [jax.ShapeDtypeStruct(s, BF16) for s in out_accs]
    res = pl.pallas_call(body, grid=(T // TB,), in_specs=in_specs, out_specs=out_specs, out_shape=out_shape,
                         scratch_shapes=[pltpu.VMEM(s, F32) for s in out_accs],
                         compiler_params=_cp(("arbitrary",)), name=name)(*arrays, *vecs, *mats)
    if out_accs:
        return res[:no], res[no:no + ns], res[no + ns:]
    return res[:no], res[no:]


_DN = {"nn": (((1,), (0,)), ((), ())), "nt": (((1,), (1,)), ((), ())), "tn": (((0,), (0,)), ((), ()))}


def mm(a, b, mode, name, out_dtype=F32, tm=None, tn=None):
    if mode == "nn":
        (M, K), (_, N) = a.shape, b.shape
    elif mode == "nt":
        (M, K), (N, _) = a.shape, b.shape
    else:
        (K, M), (_, N) = a.shape, b.shape
    if tm is None:
        tm = next((t for t in (768, 512, 256) if M % t == 0 and M > t), M)
    tn = N if tn is None else tn
    dn = _DN[mode]

    def body(a_ref, b_ref, o_ref):
        o_ref[...] = lax.dot_general(a_ref[...].astype(BF16), b_ref[...].astype(BF16), dn,
                                     preferred_element_type=F32).astype(o_ref.dtype)

    a_spec = pl.BlockSpec((K, tm), lambda i, j: (0, i)) if mode == "tn" else pl.BlockSpec((tm, K), lambda i, j: (i, 0))
    b_spec = pl.BlockSpec((tn, K), lambda i, j: (j, 0)) if mode == "nt" else pl.BlockSpec((K, tn), lambda i, j: (0, j))
    return pl.pallas_call(body, grid=(M // tm, N // tn), in_specs=[a_spec, b_spec],
                          out_specs=pl.BlockSpec((tm, tn), lambda i, j: (i, j)), out_shape=jax.ShapeDtypeStruct((M, N), out_dtype),
                          compiler_params=_cp(("parallel", "arbitrary")), name=name)(a, b)


def _rope_tables(T, width=QK, first=NOPE):
    nlat = T - LC
    pos = np.arange(nlat)
    row, col = pos // GRID_W, pos % GRID_W
    half = ROPE // 2
    inv = 1.0 / (THETA ** (np.arange(0, half, 2, dtype=np.float64) / half))
    cosf = np.ones((T, width), np.float64)
    sinf = np.zeros((T, width), np.float64)
    perm = np.zeros((width, width), np.float32)
    for m in range(ROPE):
        j = first + m
        blk, w = m // half, m % half
        ang = (row if blk == 0 else col)[:, None] * inv[None, :]
        f = w % (half // 2)
        cosf[LC:, j] = np.cos(ang[:, f])
        if w < half // 2:
            sinf[LC:, j] = -np.sin(ang[:, f])
            perm[j + half // 2, j] = 1.0
        else:
            sinf[LC:, j] = np.sin(ang[:, f])
            perm[j - half // 2, j] = 1.0
    return jnp.asarray(cosf, F32), jnp.asarray(sinf, F32), jnp.asarray(perm, BF16), jnp.asarray(perm.T, BF16)


def _exact_perm(x, pm):
    hi = x.astype(BF16)
    r1 = x - hi.astype(F32)
    mid = r1.astype(BF16)
    lo = (r1 - mid.astype(F32)).astype(BF16)
    dot = lambda a: jnp.dot(a, pm, preferred_element_type=F32)
    return dot(hi) + dot(mid) + dot(lo)


def _rot(x, cv, sv, pv, inverse):
    if inverse:
        return x * cv + _exact_perm(x * sv, pv)
    return x * cv + _exact_perm(x, pv) * sv


def rope(x, cosf, sinf, pm, inverse, out_dtype, name, scale=1.0):
    H, T, _ = x.shape

    def body(x_ref, c_ref, s_ref, p_ref, o_ref):
        cv, sv, pv = c_ref[...], s_ref[...], p_ref[...]
        for h in range(H):
            o_ref[h] = (_rot(x_ref[h], cv, sv, pv, inverse) * scale).astype(o_ref.dtype)

    return pl.pallas_call(
        body, grid=(T // TB,),
        in_specs=[pl.BlockSpec((H, TB, QK), lambda i: (0, i, 0)), pl.BlockSpec((TB, QK), lambda i: (i, 0)),
                  pl.BlockSpec((TB, QK), lambda i: (i, 0)), pl.BlockSpec((QK, QK), lambda i: (0, 0))],
        out_specs=pl.BlockSpec((H, TB, QK), lambda i: (0, i, 0)), out_shape=jax.ShapeDtypeStruct((H, T, QK), out_dtype),
        compiler_params=_cp(("parallel",)), name=name)(x, cosf, sinf, pm)


KVW = NOPE + VD


def project_q(cqn, w, name):
    T = cqn.shape[0]
    cosf, sinf, _, _ = _rope_tables(T, 128, NOPE)
    wp = jnp.pad(w.reshape(HEADS, QK, QL), ((0, 0), (0, 128 - QK), (0, 0))).reshape(HEADS * 128, QL)

    def body(a_ref, w_ref, c_ref, s_ref, o_ref):
        a, cv, sv = a_ref[...], c_ref[...], s_ref[...]
        first_of_pair = lax.bitwise_and(lax.broadcasted_iota(jnp.int32, (TB, 128), 1), ROPE // 4) == 0
        for h in range(HEADS):
            qh = _dotf(a, w_ref[pl.ds(h * 128, 128), :], "nt")
            swap = jnp.where(first_of_pair, pltpu.roll(qh, 128 - ROPE // 4, 1), pltpu.roll(qh, ROPE // 4, 1))
            o_ref[h] = ((qh * cv + swap * sv) * (SCALE * LOG2E))[:, :QK].astype(BF16)

    rows = lambda c: pl.BlockSpec((TB, c), lambda i: (i, 0))
    return pl.pallas_call(
        body, grid=(T // TB,), in_specs=[rows(QL), pl.BlockSpec(wp.shape, lambda i: (0, 0)), rows(128), rows(128)],
        out_specs=pl.BlockSpec((HEADS, TB, QK), lambda i: (0, i, 0)), out_shape=jax.ShapeDtypeStruct((HEADS, T, QK), BF16),
        compiler_params=_cp(("parallel",)), name=name)(cqn, wp, cosf, sinf)


def project_kv(ckvn, w, p0, kr_block, name):
    T = ckvn.shape[0]
    assert KVW == 128 and NOPE == VD
    cosf, sinf, pm, _ = _rope_tables(T, 128, 0)

    def body(a_ref, w_ref, kr_ref, c_ref, s_ref, p_ref, k_ref, v_ref):
        a = a_ref[...]
        is_nope = lax.broadcasted_iota(jnp.int32, (TB, KVW), 1) < NOPE
        kr_at = pltpu.roll(_rot(kr_ref[...], c_ref[...], s_ref[...], p_ref[...], False), NOPE, 1)
        for h in range(HEADS):
            kv = _dotf(a, w_ref[pl.ds(h * KVW, KVW), :], "nt")
            k_ref[h] = jnp.where(is_nope, kv, kr_at)[:, :QK].astype(BF16)
            v_ref[h] = pltpu.roll(kv, VD, 1)[:, :VD].astype(BF16)

    rows = lambda c: pl.BlockSpec((TB, c), lambda i: (i, 0))
    const = lambda x: pl.BlockSpec(x.shape, lambda i: (0, 0))
    return pl.pallas_call(
        body, grid=(T // TB,),
        in_specs=[rows(KVL), const(w), pl.BlockSpec((TB, 128), lambda i: (i, kr_block)), rows(128), rows(128), const(pm)],
        out_specs=[pl.BlockSpec((HEADS, TB, QK), lambda i: (0, i, 0)), pl.BlockSpec((HEADS, TB, VD), lambda i: (0, i, 0))],
        out_shape=[jax.ShapeDtypeStruct((HEADS, T, QK), BF16), jax.ShapeDtypeStruct((HEADS, T, VD), BF16)],
        compiler_params=_cp(("parallel",)), name=name)(ckvn, w, p0, cosf, sinf, pm)


def split_kv_grads(dk, dv, name, rode=None, modes=None):
    H, T, _ = dk.shape
    cosf, sinf, _, pmt = _rope_tables(T, 128, 0)
    to_rope_block = np.zeros((QK, 128), np.float32)
    to_rope_block[NOPE + np.arange(ROPE), np.arange(ROPE)] = 1.0
    to_rope_block = jnp.asarray(to_rope_block, BF16)

    def body(dk_ref, dv_ref, c_ref, s_ref, p_ref, sel_ref, dkv_ref, dkr_ref):
        total = None
        for h in range(H):
            dkh = dk_ref[h] * (1.0 / LOG2E)
            total = dkh if total is None else total + dkh
            dkv_ref[:, pl.ds(h * KVW, NOPE)] = dkh[:, :NOPE].astype(BF16)
            dkv_ref[:, pl.ds(h * KVW + NOPE, VD)] = dv_ref[h].astype(BF16)
        dkr_ref[...] = _rot(_exact_perm(total, sel_ref[...]), c_ref[...], s_ref[...], p_ref[...], True)

    rows = lambda c: pl.BlockSpec((TB, c), lambda i, j: (i, 0))
    const = lambda a: pl.BlockSpec(a.shape, lambda i, j: (0, 0))
    return _ride_call(
        body, (T // TB, 1),
        [pl.BlockSpec((H, TB, QK), lambda i, j: (0, i, 0)), pl.BlockSpec((H, TB, VD), lambda i, j: (0, i, 0)),
         rows(128), rows(128), const(pmt), const(to_rope_block)],
        [rows(H * KVW), rows(128)],
        [jax.ShapeDtypeStruct((T, H * KVW), BF16), jax.ShapeDtypeStruct((T, 128), F32)],
        Exchange(rode, modes) if rode else None, rode, name, (dk, dv, cosf, sinf, pmt, to_rope_block))


HB = 4
HBF = 8


def _by_query_block(run, T):
    @pl.when(pl.program_id(1) == 0)
    def _():
        run(LC)

    @pl.when(pl.program_id(1) > 0)
    def _():
        run(T)


def _with_rider(body, nin, nout, ride, grid):
    if ride is None:
        return body
    n = ride.n

    def wrapped(*refs):
        ins, xs = refs[:nin], refs[nin:nin + n]
        outs, got = refs[nin + n:nin + n + nout], refs[nin + n + nout:nin + 2 * n + nout]
        sems = refs[nin + 2 * n + nout:]
        step = pl.program_id(0) * grid[1] + pl.program_id(1)

        @pl.when(step == 0)
        def _():
            ride.start(xs, got, sems)

        body(*ins, *outs)

        @pl.when(step == grid[0] * grid[1] - 1)
        def _():
            ride.finish(xs, got, sems)

    return wrapped


def _ride_call(body, grid, in_specs, out_specs, out_shape, ride, rode, name, args):
    if ride is None:
        return pl.pallas_call(body, grid=grid, in_specs=in_specs, out_specs=out_specs, out_shape=out_shape,
                              compiler_params=_cp(("parallel", "arbitrary")), name=name)(*args), []
    res = pl.pallas_call(
        _with_rider(body, len(in_specs), len(out_specs), ride, grid), grid=grid,
        in_specs=in_specs + ride.specs, out_specs=out_specs + ride.specs, out_shape=out_shape + ride.out_shape,
        scratch_shapes=ride.scratch,
        compiler_params=pltpu.CompilerParams(dimension_semantics=("arbitrary", "arbitrary"), vmem_limit_bytes=VMEM_LIMIT,
                                             has_side_effects=True), name=name)(*args, *rode)
    return res[:len(out_specs)], res[len(out_specs):]


def attn_fwd(q, k, v, name, rode=None, modes=None):
    H, T, _ = q.shape

    def body(q_ref, k_ref, v_ref, o_ref, lse_ref):
        def run(nk):
            for hh in range(HBF):
                s = _dotf(q_ref[hh], k_ref[hh, pl.ds(0, nk), :], "nt")
                m = jnp.max(s, axis=1, keepdims=True)
                p = jnp.exp2(s - m)
                l = jnp.sum(p, axis=1, keepdims=True)
                o = jnp.dot(p.astype(BF16), v_ref[hh, pl.ds(0, nk), :], preferred_element_type=F32)
                o_ref[hh] = o / l
                lse_ref[hh] = m + jnp.log2(l)

        _by_query_block(run, T)

    return _ride_call(
        body, (H // HBF, T // TB),
        [pl.BlockSpec((HBF, TB, QK), lambda h, i: (h, i, 0)), pl.BlockSpec((HBF, T, QK), lambda h, i: (h, 0, 0)),
         pl.BlockSpec((HBF, T, VD), lambda h, i: (h, 0, 0))],
        [pl.BlockSpec((HBF, TB, VD), lambda h, i: (h, i, 0)), pl.BlockSpec((HBF, TB, 1), lambda h, i: (h, i, 0))],
        [jax.ShapeDtypeStruct((H, T, VD), F32), jax.ShapeDtypeStruct((H, T, 1), F32)],
        Exchange(rode, modes) if rode else None, rode, name, (q, k, v))


def attn_bwd(q, k, v, o, lse, do, name, rode=None, modes=None):
    H, T, _ = q.shape

    def body(q_ref, k_ref, v_ref, o_ref, lse_ref, do_ref, dq_ref, dk_ref, dv_ref):
        i = pl.program_id(1)

        @pl.when(i == 0)
        def _():
            dk_ref[...] = jnp.zeros_like(dk_ref)
            dv_ref[...] = jnp.zeros_like(dv_ref)

        def run(nk):
            keys = pl.ds(0, nk)
            for hh in range(HB):
                qv, kv, dov = q_ref[hh], k_ref[hh, keys, :], do_ref[hh]
                p = jnp.exp2(_dotf(qv, kv, "nt") - lse_ref[hh])
                delta = jnp.sum(dov * o_ref[hh], axis=1, keepdims=True)
                dob = dov.astype(BF16)
                dv_ref[hh, keys, :] += _dotf(p.astype(BF16), dob, "tn")
                dp = _dotf(dob, v_ref[hh, keys, :], "nt")
                ds = (p * (dp - delta)).astype(BF16)
                dq_ref[hh] = jnp.dot(ds, kv, preferred_element_type=F32)
                dk_ref[hh, keys, :] += _dotf(ds, qv, "tn")

        _by_query_block(run, T)

    blk = lambda c: pl.BlockSpec((HB, TB, c), lambda h, i: (h, i, 0))
    full = lambda c: pl.BlockSpec((HB, T, c), lambda h, i: (h, 0, 0))
    return _ride_call(
        body, (H // HB, T // TB), [blk(QK), full(QK), full(VD), blk(VD), blk(1), blk(VD)], [blk(QK), full(QK), full(VD)],
        [jax.ShapeDtypeStruct((H, T, QK), F32), jax.ShapeDtypeStruct((H, T, QK), F32), jax.ShapeDtypeStruct((H, T, VD), F32)],
        Exchange(rode, modes) if rode else None, rode, name, (q, k, v, o, lse, do))


def disc_fwd(a_re, a_im, ls, name):
    def body(ar_ref, ai_ref, ls_ref, lr_ref, li_ref, fr_ref, fi_ref):
        ar, ai = ar_ref[...], ai_ref[...]
        dt = jnp.exp(ls_ref[...])
        mag = jnp.exp(ar * dt)
        lr = mag * jnp.cos(ai * dt)
        li = mag * jnp.sin(ai * dt)
        den = ar * ar + ai * ai
        nr = lr - 1.0
        lr_ref[...] = lr
        li_ref[...] = li
        fr_ref[...] = (nr * ar + li * ai) / den
        fi_ref[...] = (li * ar - nr * ai) / den

    return pl.pallas_call(body, out_shape=[jax.ShapeDtypeStruct(a_re.shape, F32)] * 4, name=name)(a_re, a_im, ls)


def disc_b(f_re, f_im, b_re, b_im, name):
    def body(fr_ref, fi_ref, br_ref, bi_ref, or_ref, oi_ref):
        fr, fi, br, bi = fr_ref[...], fi_ref[...], br_ref[...], bi_ref[...]
        or_ref[...] = fr * br - fi * bi
        oi_ref[...] = fr * bi + fi * br

    return pl.pallas_call(body, out_shape=[jax.ShapeDtypeStruct(b_re.shape, F32)] * 2, compiler_params=_cp(),
                          name=name)(f_re, f_im, b_re, b_im)


def disc_b_bwd(f_re, f_im, b_re, b_im, dbb_re, dbb_im, name):
    def body(fr_ref, fi_ref, br_ref, bi_ref, dr_ref, di_ref, dbr_ref, dbi_ref, dfr_ref, dfi_ref):
        fr, fi, br, bi, dr, di = fr_ref[...], fi_ref[...], br_ref[...], bi_ref[...], dr_ref[...], di_ref[...]
        dbr_ref[...] = fr * dr + fi * di
        dbi_ref[...] = fr * di - fi * dr
        dfr_ref[...] = jnp.sum(dr * br + di * bi, axis=2, keepdims=True)
        dfi_ref[...] = jnp.sum(di * br - dr * bi, axis=2, keepdims=True)

    return pl.pallas_call(body, out_shape=[jax.ShapeDtypeStruct(b_re.shape, F32)] * 2 + [jax.ShapeDtypeStruct(f_re.shape, F32)] * 2,
                          compiler_params=_cp(), name=name)(f_re, f_im, b_re, b_im, dbb_re, dbb_im)


def disc_a_bwd(a_re, a_im, ls, dlr, dli, dfr, dfi, name):
    def body(ar_ref, ai_ref, ls_ref, dlr_ref, dli_ref, dfr_ref, dfi_ref, dar_ref, dai_ref, dls_ref):
        ar, ai = ar_ref[...], ai_ref[...]
        dt = jnp.exp(ls_ref[...])
        mag = jnp.exp(ar * dt)
        cs, sn = jnp.cos(ai * dt), jnp.sin(ai * dt)
        lr, li = mag * cs, mag * sn
        den = ar * ar + ai * ai
        nr = lr - 1.0
        f_re = (nr * ar + li * ai) / den
        f_im = (li * ar - nr * ai) / den
        dn1 = dfr_ref[...] / den
        dn2 = dfi_ref[...] / den
        dden = -(dfr_ref[...] * f_re + dfi_ref[...] * f_im) / den
        dlr_t = dlr_ref[...] + dn1 * ar - dn2 * ai
        dli_t = dli_ref[...] + dn1 * ai + dn2 * ar
        dar = dn1 * nr + dn2 * li + dden * 2.0 * ar
        dai = dn1 * li - dn2 * nr + dden * 2.0 * ai
        dmag = dlr_t * cs + dli_t * sn
        dth = dli_t * lr - dlr_t * li
        dar_ref[...] = dar + dmag * mag * dt
        dai_ref[...] = dai + dth * dt
        dls_ref[...] = jnp.sum(dmag * mag * ar + dth * ai, axis=-1, keepdims=True) * dt

    return pl.pallas_call(body, out_shape=[jax.ShapeDtypeStruct(a_re.shape, F32)] * 2 +
                          [jax.ShapeDtypeStruct(ls.shape, F32)], name=name)(a_re, a_im, ls, dlr, dli, dfr, dfi)


def _cpow(lr, li, n):
    rr, ri = None, None
    br, bi = lr, li
    while n:
        if n & 1:
            if rr is None:
                rr, ri = br, bi
            else:
                rr, ri = rr * br - ri * bi, rr * bi + ri * br
        n >>= 1
        if n:
            br, bi = br * br - bi * bi, 2.0 * br * bi
    return rr, ri


UNROLL = 4


def _steps(trips, fn, init):
    main = trips // UNROLL

    def body(i, c):
        for j in range(UNROLL):
            c = fn(i * UNROLL + j, c)
        return c

    c = lax.fori_loop(0, main, body, init) if main else init
    for n in range(main * UNROLL, trips):
        c = fn(n, c)
    return c


def _seg_scan(xre, xim, lam8, pw, base, seglen, rev, init, fin_re, fin_im, ini_re, ini_im, prev=None):
    lr, li = lam8
    nsub = SEG // 8

    def rows(t, s):
        first = base + t * SEG + 8 * s
        return pl.ds(first if isinstance(first, int) else pl.multiple_of(first, 8), 8)

    tmap = (lambda n: seglen - 1 - n) if rev else (lambda n: n)
    zeros = tuple(jnp.zeros((8, SB), F32) for _ in range(2 * nsub))

    def advance(c, t):
        out = []
        for s in range(nsub):
            a, b = c[2 * s], c[2 * s + 1]
            out += [lr * a - li * b + xre[rows(t, s), :], lr * b + li * a + xim[rows(t, s), :]]
        return tuple(out)

    fin = _steps(seglen, lambda n, c: advance(c, tmap(n)), zeros)
    for s in range(nsub):
        fin_re[pl.ds(8 * s, 8), :] = fin[2 * s]
        fin_im[pl.ds(8 * s, 8), :] = fin[2 * s + 1]
    (cr, ci), (pr, pi) = init, pw
    for i in (range(SEG - 1, -1, -1) if rev else range(SEG)):
        ini_re[pl.ds(i, 1), :] = cr
        ini_im[pl.ds(i, 1), :] = ci
        cr, ci = pr * cr - pi * ci + fin_re[pl.ds(i, 1), :], pr * ci + pi * cr + fin_im[pl.ds(i, 1), :]
    tiles = lambda re, im: tuple(r[pl.ds(8 * s, 8), :] for s in range(nsub) for r in (re, im))
    start = tiles(ini_re, ini_im)

    def store(c, t):
        new = advance(c, t)
        for s in range(nsub):
            xre[rows(t, s), :] = new[2 * s]
            xim[rows(t, s), :] = new[2 * s + 1]
        return new

    if prev is None:
        _steps(seglen, lambda n, c: store(c, tmap(n)), start)
        return (cr, ci), None

    sre, sim, s_ini_re, s_ini_im = prev

    def acc_step(c, t, before):
        new = store(c[:2 * nsub], t)
        acc = []
        for s in range(nsub):
            (na, nb), (pre, pim) = new[2 * s:2 * s + 2], before[2 * s:2 * s + 2]
            acc += [c[2 * nsub + 2 * s] + na * pre + nb * pim, c[2 * nsub + 2 * s + 1] + nb * pre - na * pim]
        return new + tuple(acc)

    def body(n, c):
        t = tmap(n)
        tp = t - 1 if rev else t + 1
        return acc_step(c, t, tuple(r[rows(tp, s), :] for s in range(nsub) for r in (sre, sim)))

    c = _steps(seglen - 1, body, start + zeros)
    c = acc_step(c, 0 if rev else seglen - 1, tiles(s_ini_re, s_ini_im))
    acc = c[2 * nsub:]
    return (cr, ci), (sum(acc[0::2][1:], acc[0]), sum(acc[1::2][1:], acc[1]))


def _lam_tiles(lr, li, lens, conj=False):
    if conj:
        li = -li
    lam8 = (jnp.broadcast_to(lr, (8, SB)), jnp.broadcast_to(li, (8, SB)))
    return lam8, [_cpow(lr, li, n) for n in lens]


def _stretches(T):
    return ((0, LC // SEG), (LC, (T - LC) // SEG))


def _to_seg_order(src, dst, T):
    for base, seglen in _stretches(T):
        def body(t, carry, base=base, seglen=seglen):
            dst[pl.ds(pl.multiple_of(base + t * SEG, SEG), SEG), :] = src[pl.ds(base + t, SEG, stride=seglen), :]
            return carry
        lax.fori_loop(0, seglen, body, 0, unroll=8)


def _from_seg_order(src, dst, T):
    for base, seglen in _stretches(T):
        def body(t, carry, base=base, seglen=seglen):
            dst[pl.ds(base + t, SEG, stride=seglen), :] = src[pl.ds(pl.multiple_of(base + t * SEG, SEG), SEG), :]
            return carry
        lax.fori_loop(0, seglen, body, 0, unroll=8)


def _scan_specs(T):
    ublk = pl.BlockSpec((T, UB), lambda j: (0, j))
    lam = pl.BlockSpec((2, 1, 1, SB), lambda j: (0, j, 0, 0))
    mat = pl.BlockSpec((2, 1, UB, P), lambda j: (0, j, 0, 0))
    return ublk, lam, mat


def _dotf(a, b, mode="nn"):
    return lax.dot_general(a, b, _DN[mode], preferred_element_type=F32)


def _diag_mask():
    r = lax.broadcasted_iota(jnp.int32, (UB, SB), 0)
    c = lax.broadcasted_iota(jnp.int32, (UB, SB), 1)
    return lax.shift_right_logical(r, int(math.log2(CH))) == lax.shift_right_logical(c, int(math.log2(P)))


def _expand(m):
    p = lax.broadcasted_iota(jnp.int32, (P, SB), 0)
    c = lax.broadcasted_iota(jnp.int32, (P, SB), 1)
    tile = jnp.where(lax.bitwise_and(c, P - 1) == p, 1.0, 0.0).astype(BF16)
    wide = jnp.dot(m.astype(BF16), tile, preferred_element_type=F32)
    return jnp.where(_diag_mask(), wide, 0.0).astype(BF16)


def _collapse(full):
    c = lax.broadcasted_iota(jnp.int32, (SB, P), 0)
    p = lax.broadcasted_iota(jnp.int32, (SB, P), 1)
    pick = jnp.where(lax.bitwise_and(c, P - 1) == p, 1.0, 0.0).astype(BF16)
    return _exact_perm(jnp.where(_diag_mask(), full, 0.0), pick)


def _zero_state():
    return jnp.zeros((1, SB), F32), jnp.zeros((1, SB), F32)


def scan_fwd(u, lam_re, lam_im, bre, bim, cre, cim, name):
    T = u.shape[0]
    s_ctx, s_lat = LC // SEG, (T - LC) // SEG

    def body(u_ref, lr_ref, li_ref, bre_ref, bim_ref, cre_ref, cim_ref, y_ref, us, ys, sre, sim, fre, fim, ire, iim):
        _to_seg_order(u_ref, us, T)
        ub = us[...].astype(BF16)
        for d in range(2):
            lam8, (pw_c, pw_l) = _lam_tiles(lr_ref[d, 0], li_ref[d, 0], (s_ctx, s_lat))
            sre[...] = _dotf(ub, _expand(bre_ref[d, 0]))
            sim[...] = _dotf(ub, _expand(bim_ref[d, 0]))
            end_c, _ = _seg_scan(sre, sim, lam8, pw_c, 0, s_ctx, bool(d), _zero_state(), fre, fim, ire, iim)
            _seg_scan(sre, sim, lam8, pw_l, LC, s_lat, bool(d), end_c, fre, fim, ire, iim)
            y = (_dotf(sre[...].astype(BF16), _expand(cre_ref[d, 0]), "nt")
                 - _dotf(sim[...].astype(BF16), _expand(cim_ref[d, 0]), "nt"))
            if d == 0:
                ys[...] = y
            else:
                ys[...] += y
        _from_seg_order(ys, y_ref, T)

    ublk, lam, mat = _scan_specs(T)
    return pl.pallas_call(
        body, grid=(NJ,), in_specs=[ublk, lam, lam, mat, mat, mat, mat], out_specs=ublk,
        out_shape=jax.ShapeDtypeStruct((T, G * CH), F32),
        scratch_shapes=[pltpu.VMEM((T, UB), F32)] * 2 + [pltpu.VMEM((T, SB), F32)] * 2 + [pltpu.VMEM((SEG, SB), F32)] * 4,
        compiler_params=_cp(("arbitrary",)), name=name)(u, lam_re, lam_im, bre, bim, cre, cim)


def scan_bwd(u, dy, lam_re, lam_im, bre, bim, cre, cim, name):
    T = u.shape[0]
    s_ctx, s_lat = LC // SEG, (T - LC) // SEG

    def body(u_ref, dy_ref, lr_ref, li_ref, bre_ref, bim_ref, cre_ref, cim_ref,
             du_ref, dlr_ref, dli_ref, dbre_ref, dbim_ref, dcre_ref, dcim_ref,
             us, dys, dus, sre, sim, gre, gim, fre, fim, ic_re, ic_im, il_re, il_im, jre, jim):
        _to_seg_order(u_ref, us, T)
        _to_seg_order(dy_ref, dys, T)
        ub, dyb = us[...].astype(BF16), dys[...].astype(BF16)
        for d in range(2):
            rev = bool(d)
            lam8, (pw_c, pw_l) = _lam_tiles(lr_ref[d, 0], li_ref[d, 0], (s_ctx, s_lat))
            cam8, (cw_c, cw_l) = _lam_tiles(lr_ref[d, 0], li_ref[d, 0], (s_ctx, s_lat), conj=True)
            bre_v, bim_v = _expand(bre_ref[d, 0]), _expand(bim_ref[d, 0])
            sre[...] = _dotf(ub, bre_v)
            sim[...] = _dotf(ub, bim_v)
            end_c, _ = _seg_scan(sre, sim, lam8, pw_c, 0, s_ctx, rev, _zero_state(), fre, fim, ic_re, ic_im)
            _seg_scan(sre, sim, lam8, pw_l, LC, s_lat, rev, end_c, fre, fim, il_re, il_im)
            gre[...] = _dotf(dyb, _expand(cre_ref[d, 0]))
            gim[...] = -_dotf(dyb, _expand(cim_ref[d, 0]))
            end_g, acc_l = _seg_scan(gre, gim, cam8, cw_l, LC, s_lat, not rev, _zero_state(), fre, fim, jre, jim,
                                     prev=(sre, sim, il_re, il_im))
            _, acc_c = _seg_scan(gre, gim, cam8, cw_c, 0, s_ctx, not rev, end_g, fre, fim, jre, jim,
                                 prev=(sre, sim, ic_re, ic_im))
            dlr_ref[d, 0] = _sum0(acc_l[0] + acc_c[0])
            dli_ref[d, 0] = _sum0(acc_l[1] + acc_c[1])
            grb, gib = gre[...].astype(BF16), gim[...].astype(BF16)
            du = _dotf(grb, bre_v, "nt") + _dotf(gib, bim_v, "nt")
            if d == 0:
                dus[...] = du
            else:
                dus[...] += du
            dbre_ref[d, 0] = _collapse(_dotf(ub, grb, "tn"))
            dbim_ref[d, 0] = _collapse(_dotf(ub, gib, "tn"))
            dcre_ref[d, 0] = _collapse(_dotf(dyb, sre[...].astype(BF16), "tn"))
            dcim_ref[d, 0] = -_collapse(_dotf(dyb, sim[...].astype(BF16), "tn"))
        _from_seg_order(dus, du_ref, T)

    ublk, lam, mat = _scan_specs(T)
    lam_s = jax.ShapeDtypeStruct(lam_re.shape, F32)
    mat_s = jax.ShapeDtypeStruct(bre.shape, F32)
    return pl.pallas_call(
        body, grid=(NJ,), in_specs=[ublk, ublk, lam, lam, mat, mat, mat, mat],
        out_specs=[ublk, lam, lam, mat, mat, mat, mat],
        out_shape=[jax.ShapeDtypeStruct((T, G * CH), F32), lam_s, lam_s, mat_s, mat_s, mat_s, mat_s],
        scratch_shapes=[pltpu.VMEM((T, UB), F32)] * 3 + [pltpu.VMEM((T, SB), F32)] * 4 + [pltpu.VMEM((SEG, SB), F32)] * 8,
        compiler_params=_cp(("arbitrary",)), name=name)(u, dy, lam_re, lam_im, bre, bim, cre, cim)


class Exchange:
    def __init__(self, xs, modes):
        self.n = len(xs)
        self.modes = [modes] * self.n if isinstance(modes, (str, int)) else list(modes)
        self.out_shape = [jax.ShapeDtypeStruct(self._shape(x, md), x.dtype) for x, md in zip(xs, self.modes)]
        self.scratch = [pltpu.SemaphoreType.DMA((NDEV - 1, self.n)), pltpu.SemaphoreType.DMA((NDEV - 1, self.n)),
                        pltpu.SemaphoreType.DMA((self.n,))]
        self.specs = [pl.BlockSpec(memory_space=pl.ANY)] * self.n

    @staticmethod
    def _shape(x, mode):
        if mode == "gather":
            return (NDEV,) + tuple(x.shape)
        return tuple(x.shape) if mode == "lead" else (NDEV, x.shape[0], mode) + tuple(x.shape[2:])

    @staticmethod
    def _piece(x_ref, mode, dev):
        if mode == "gather":
            return x_ref
        return x_ref.at[dev] if mode == "lead" else x_ref.at[:, pl.ds(dev * mode, mode)]

    def _copies(self, x_refs, out_refs, sems):
        send_sems, recv_sems, local_sems = sems
        mx, my, mc = lax.axis_index("x"), lax.axis_index("y"), lax.axis_index("c")
        me = 4 * mx + 2 * my + mc
        peer_of = lambda k: (1 - mx if k & 4 else mx, 1 - my if k & 2 else my, 1 - mc if k & 1 else mc)
        local, first, relay, arrivals = [], [], [], []
        for a, (x_ref, out_ref) in enumerate(zip(x_refs, out_refs)):
            mode = self.modes[a]
            local.append(pltpu.make_async_copy(self._piece(x_ref, mode, me), out_ref.at[me], local_sems.at[a]))

            def remote(src, dst, k, pair, a=a):
                return pltpu.make_async_remote_copy(src_ref=src, dst_ref=dst, send_sem=send_sems.at[pair, a],
                                                    recv_sem=recv_sems.at[pair, a], device_id=peer_of(k), device_id_type=MESH_T)

            for k in range(1, NDEV):
                peer = peer_of(k)
                pid = 4 * peer[0] + 2 * peer[1] + peer[2]
                if mode != "gather":
                    src = self._piece(x_ref, mode, pid)
                    first.append(remote(src, out_ref.at[me], k, k - 1))
                    arrivals.append(remote(src, out_ref.at[pid], k, k - 1))
                elif k == 1:
                    first.append(remote(x_ref, out_ref.at[me], k, k - 1))
                    arrivals.append(remote(x_ref, out_ref.at[pid], k, k - 1))
                elif k % 2 == 0:
                    first.append(remote(x_ref, out_ref.at[me], k, k - 1))
                    relay.append((remote(x_ref, out_ref.at[pid], k, k - 1), remote(out_ref.at[pid], out_ref.at[pid], 1, k)))
                else:
                    arrivals.append(remote(x_ref, out_ref.at[pid], 1, k - 1))
        return local, first, relay, arrivals

    def start(self, x_refs, out_refs, sems):
        local, first, _, _ = self._copies(x_refs, out_refs, sems)
        for cp in local + first:
            cp.start()

    def finish(self, x_refs, out_refs, sems):
        local, first, relay, arrivals = self._copies(x_refs, out_refs, sems)
        for arrival, onward in relay:
            arrival.wait_recv()
            onward.start()
        for cp in arrivals:
            cp.wait_recv()
        for cp in first + [onward for _, onward in relay]:
            cp.wait_send()
        for cp in local:
            cp.wait()


def exchange(xs, modes, name):
    ex = Exchange(xs, modes)
    n = ex.n

    def body(*refs):
        ex.start(refs[:n], refs[n:2 * n], refs[2 * n:])
        ex.finish(refs[:n], refs[n:2 * n], refs[2 * n:])

    return pl.pallas_call(body, in_specs=ex.specs, out_specs=ex.specs, out_shape=ex.out_shape, scratch_shapes=ex.scratch,
                          compiler_params=pltpu.CompilerParams(has_side_effects=True), name=name)(*xs)


def _dot_f32(a, b, dn):
    return lax.dot_general(a, b, dn, preferred_element_type=F32, precision=lax.Precision.HIGHEST)


def ada_fwd(cg, c_ctx, ada_w, ada_b_loc, name):
    W = ada_w.shape[2]

    def body(cg_ref, cc_ref, w_ref, b_ref, o_ref):
        a = jnp.concatenate([_silu(cg_ref[...]), jnp.broadcast_to(_silu(cc_ref[...]), (NDEV, D))], axis=0)
        for i in range(2):
            o_ref[i] = _dot_f32(a, w_ref[i], _DN["nn"]) + b_ref[i]

    return pl.pallas_call(body, out_shape=jax.ShapeDtypeStruct((2, 2 * NDEV, W), F32),
                          compiler_params=_cp(), name=name)(cg, c_ctx, ada_w, ada_b_loc)


def ada_bwd(cg, c_ctx, ada_w, dm_loc, dm_all, name):
    W = ada_w.shape[2]

    def body(cg_ref, cc_ref, w_ref, dl_ref, da_ref, gw_ref, dcc_ref, gb_ref):
        a = jnp.concatenate([_silu(cg_ref[...]), jnp.broadcast_to(_silu(cc_ref[...]), (NDEV, D))], axis=0)
        dcc = jnp.zeros((1, D), F32)
        for i in range(2):
            dl = dl_ref[i]
            gw_ref[i] = _dot_f32(a, dl, _DN["tn"])
            dctx = jnp.sum(dl[NDEV:], axis=0, keepdims=True)
            dcc = dcc + _dot_f32(dctx, w_ref[i], _DN["nt"])
        dcc_ref[...] = dcc
        gb_ref[...] = jnp.sum(da_ref[...], axis=0)

    return pl.pallas_call(body, out_shape=[jax.ShapeDtypeStruct((2, D, W), F32), jax.ShapeDtypeStruct((1, D), F32),
                                           jax.ShapeDtypeStruct((2, 3 * D), F32)],
                          compiler_params=_cp(), name=name)(cg, c_ctx, ada_w, dm_loc, dm_all)


def cctx_finish(parts, c_ctx, name):
    def body(p_ref, cc_ref, o_ref):
        o_ref[...] = jnp.sum(p_ref[...], axis=0, keepdims=True) * _dsilu(cc_ref[...])

    return pl.pallas_call(body, out_shape=jax.ShapeDtypeStruct((1, D), F32), name=name)(parts, c_ctx)


def _adamw_update(g_ref, w_ref, m_ref, v_ref, go_ref, d_ref, mo_ref, vo_ref):
    g = g_ref[0].astype(F32)
    for s in range(1, g_ref.shape[0]):
        g = g + g_ref[s].astype(F32)
    mn = B1 * m_ref[...] + (1.0 - B1) * g
    vn = B2 * v_ref[...] + (1.0 - B2) * g * g
    go_ref[...] = g
    mo_ref[...] = mn
    vo_ref[...] = vn
    d_ref[...] = -LR * ((mn * (1.0 / (1.0 - B1 ** STEP))) / (jnp.sqrt(vn * (1.0 / (1.0 - B2 ** STEP))) + AEPS) + WD * w_ref[...])


ADAMW_PARTS = 4


def adamw_rows(items, name, rode=None, modes=None):
    in_specs, out_specs, out_shape, args = [], [], [], []
    for g, w, m, v in items:
        n, R, C = g.shape
        tr = R // ADAMW_PARTS
        spec = pl.BlockSpec((tr, C), lambda i, j: (i, 0))
        in_specs += [pl.BlockSpec((n, tr, C), lambda i, j: (0, i, 0)), spec, spec, spec]
        args += [g, w, m, v]
    for g, w, m, v in items:
        tr = w.shape[0] // ADAMW_PARTS
        out_specs += [pl.BlockSpec((tr, w.shape[1]), lambda i, j: (i, 0))] * 4
        out_shape += [jax.ShapeDtypeStruct(w.shape, F32)] * 4
    res, got = _ride_call(_adamw_body(len(items)), (ADAMW_PARTS, 1), in_specs, out_specs, out_shape,
                          Exchange(rode, modes) if rode else None, rode, name, args)
    return [res[4 * t:4 * t + 4] for t in range(len(items))], got


def _adamw_body(k):
    def body(*refs):
        for t in range(k):
            _adamw_update(*refs[4 * t:4 * t + 4], *refs[4 * k + 4 * t:4 * k + 4 * t + 4])
    return body


def adamw_multi(items, grid, name):
    k = len(items)
    ins, in_specs, out_specs, out_shape = [], [], [], []
    for g, g_spec, w, m, v, w_spec in items:
        ins += [g, w, m, v]
        in_specs += [g_spec, w_spec, w_spec, w_spec]
    for g, g_spec, w, m, v, w_spec in items:
        out_specs += [w_spec] * 4
        out_shape += [jax.ShapeDtypeStruct(w.shape, F32)] * 4
    res = pl.pallas_call(_adamw_body(k), grid=grid, in_specs=in_specs, out_specs=out_specs, out_shape=out_shape,
                         compiler_params=_cp(("arbitrary",) * len(grid)), name=name)(*ins)
    return [res[4 * t:4 * t + 4] for t in range(k)]


def _whole(a, grid_rank):
    zeros = (0,) * a.ndim
    return pl.BlockSpec(a.shape, lambda *idx: zeros)


def sum_slots(xs, name):
    def body(*refs):
        for x_ref, o_ref in zip(refs[:len(xs)], refs[len(xs):]):
            acc = x_ref[0]
            for s in range(1, NDEV):
                acc = acc + x_ref[s]
            o_ref[...] = acc

    return pl.pallas_call(body, out_shape=[jax.ShapeDtypeStruct(x.shape[1:], F32) for x in xs],
                          compiler_params=_cp(), name=name)(*xs)


def _col_shards(g):
    R, N = g.shape
    return g.reshape(R, NDEV, N // NDEV).transpose(1, 0, 2)


def _vec2(v):
    return jnp.broadcast_to(v.reshape(1, 1, -1), (2, 1, v.size))


SHARD_ROWS = {"mla_w_in": 192, "mla_w_uq": 192, "mla_w_ukv": 256, "s5_w_in": 256}


def _t_shard(wsh, rows):
    t = wsh[0].T.astype(BF16)
    return jnp.pad(t, ((0, rows - t.shape[0]), (0, 0)))


def _win_order():
    w = IN_W // NDEV
    perm = np.zeros((IN_WP, NDEV * SHARD_ROWS["mla_w_in"]), np.float32)
    first = QL + KVL + ROPE
    for c in range(IN_W):
        n = c + HEADS * VD if c < first else c - first
        perm[n, (c // w) * SHARD_ROWS["mla_w_in"] + c % w] = 1.0
    return jnp.asarray(perm, BF16)


def local_step(ctx, x, tgt, mod, Wt, small, l1_shards):
    T = LC + x.shape[0]
    xa = ("cat", ctx, x)
    sh = [mod[i, :, None, 0:D] for i in range(2)]
    sc = [mod[i, :, None, D:2 * D] for i in range(2)]
    gt = [mod[i, :, None, 2 * D:] for i in range(2)]
    ng = [_vec2(small["norm_g"][i]) for i in range(2)]
    qg, kvg = _vec2(small["mla_q_norm"]), _vec2(small["mla_kv_norm"])
    cosf, sinf, _, pmt = _rope_tables(T)

    (h0, p0, cqn, ckvn), _ = rowwise(st_l0_pre, [xa], [ng[0], sc[0], sh[0], qg, kvg],
                                     [(D, BF16), (IN_WP, F32), (QL, BF16), (KVL, BF16)], [], "l0_pre", mats=[Wt["mla_w_in"]])
    z0, cq, ckv = (p0, 0, HEADS * VD), (p0, HEADS * VD // QL, QL), (p0, (HEADS * VD + QL) // KVL, KVL)
    Q = project_q(cqn, Wt["mla_w_uq"], "l0_uq")
    K, V = project_kv(ckvn, Wt["mla_w_ukv"], p0, (HEADS * VD + QL + KVL) // 128, "l0_ukv")
    (o, lse), got = attn_fwd(Q, K, V, "l0_attn", rode=l1_shards, modes="gather")
    Wt, small = dict(Wt), dict(small)
    for n, a in zip(L1_BIG, got):
        Wt[n] = a.reshape(-1, a.shape[-1])
    vecs = lax.bitcast_convert_type(got[-1].reshape(NDEV, 2, -1, 2), F32)
    small["s5_d"], small["s5_b_glu"] = vecs[:, 0, :].reshape(D), vecs[:, 1, :].reshape(D)
    o2 = o.transpose(1, 0, 2).reshape(T, HEADS * VD)
    (og, out0, x1), _ = rowwise(st_l0_post, [o2, z0, xa], [gt[0]], [(D, BF16), (D, BF16), (D, F32)], [], "l0_post",
                                mats=[Wt["mla_w_out"]])

    ls = small["s5_log_step"].reshape(2, G, 1)
    a_re, a_im = small["s5_a_re"].reshape(2, G, P), small["s5_a_im"].reshape(2, G, P)
    b_re = small["s5_b_re"].reshape(2, G, P, CH).transpose(0, 1, 3, 2)
    b_im = small["s5_b_im"].reshape(2, G, P, CH).transpose(0, 1, 3, 2)
    lam_re, lam_im, f_re, f_im = disc_fwd(a_re, a_im, ls, "s5_disc")
    f_re2, f_im2 = f_re.reshape(2, G, 1, P), f_im.reshape(2, G, 1, P)
    bb_re, bb_im = disc_b(f_re2, f_im2, b_re, b_im, "s5_disc_b")
    compact = lambda m: m.reshape(2, NJ, UB, P)
    bre, bim = compact(bb_re), compact(bb_im)
    cre, cim = compact(small["s5_c_re"]), compact(small["s5_c_im"])
    lam_re4, lam_im4 = lam_re.reshape(2, NJ, 1, SB), lam_im.reshape(2, NJ, 1, SB)

    (h1, p1), _ = rowwise(st_l1_pre, [x1], [ng[1], sc[1], sh[1]], [(D, BF16), (2 * D, F32)], [], "l1_pre", mats=[Wt["s5_w_in"]])
    u, z1 = (p1, 0, D), (p1, 1, D)
    yssm = scan_fwd(p1, lam_re4, lam_im4, bre, bim, cre, cim, "s5_scan")
    dvec, bglu = _vec2(small["s5_d"]), _vec2(small["s5_b_glu"])
    fg = _vec2(small["final_g"])
    lat_mask = jnp.stack([jnp.zeros((1, D), F32), jnp.ones((1, D), F32)])
    (y, y1b, gl, y3, out1, dx2), (dfg, lvec) = rowwise(
        st_l1_mlp, [yssm, u, z1, x1, ("lat", tgt)], [dvec, bglu, gt[1], fg, lat_mask],
        [(D, F32), (D, BF16), (D, BF16), (D, BF16), (D, BF16), (D, F32)], [D, 128], "l1_mlp",
        mats=[Wt["s5_w_glu"], Wt["s5_w_out"]])

    (dz1, dy, du_d), (dgt1, dbglu, dd), (g_w_out5, g_w_glu) = rowwise(
        st_l1_mlp_bwd, [dx2, out1, y3, y, gl, z1, u, y1b], [gt[1], bglu, dvec], [(D, BF16), (D, F32), (D, F32)], [D, D, D],
        "l1_mlp_b", mats=[Wt["s5_w_out"], Wt["s5_w_glu"]], out_accs=[(D, D), (D, D)])
    du_s, dlr, dli, dbre, dbim, dcre, dcim = scan_bwd(p1, dy, lam_re4, lam_im4, bre, bim, cre, cim, "s5_scan_b")
    dbb_re, dbb_im = dbre.reshape(2, G, CH, P), dbim.reshape(2, G, CH, P)
    g_c_re, g_c_im = dcre.reshape(2, G, CH, P), dcim.reshape(2, G, CH, P)
    gt_b_re, gt_b_im, dfr, dfi = disc_b_bwd(f_re2, f_im2, b_re, b_im, dbb_re, dbb_im, "s5_disc_b_b")
    g_b_re, g_b_im = gt_b_re.transpose(0, 1, 3, 2), gt_b_im.transpose(0, 1, 3, 2)
    g_a_re, g_a_im, g_ls = disc_a_bwd(a_re, a_im, ls, dlr.reshape(2, G, P), dli.reshape(2, G, P),
                                      dfr.reshape(2, G, P), dfi.reshape(2, G, P), "s5_disc_b_a")
    (dx1,), (dsh1, dsc1, dng1), (g_w_in5,) = rowwise(
        st_l1_tail_bwd, [du_d, du_s, dz1, h1, x1, dx2], [ng[1], sc[1]], [(D, F32)], [D, D, D], "l1_pre_b",
        mats=[Wt["s5_w_in"]], out_accs=[(NDEV, D, 2 * D // NDEV)])

    (do2, dz0), (dgt0,), (g_w_out,) = rowwise(st_l0_post_bwd, [dx1, out0, og, o2, z0], [gt[0]], [(D, F32), (D, F32)], [D],
                                              "l0_post_b", mats=[Wt["mla_w_out"]], out_accs=[(D, D)])
    doh = do2.reshape(T, HEADS, VD).transpose(1, 0, 2)
    rows8 = lambda g: g.reshape(NDEV, -1, g.shape[-1])
    both = lambda s: s[0, 0] + s[1, 0]
    dense = lambda g: g.reshape(2, G * P * CH // 128, 128)
    chunks = [dense(g_b_re), dense(g_b_im), g_c_re, g_c_im]
    l1_send = [g_w_in5, rows8(g_w_glu), rows8(g_w_out5), rows8(g_w_out),
               both(dd).reshape(NDEV, 1, -1), both(dbglu).reshape(NDEV, 1, -1)]
    (dQ, dK, dV), l1_recv = attn_bwd(Q, K, V, o, lse, doh, "l0_attn_b", rode=l1_send + chunks,
                                     modes=["lead"] * len(l1_send) + [a.shape[1] // NDEV for a in chunks])
    dqh = rope(dQ, cosf, sinf, pmt, True, BF16, "l0_rope_q_b", scale=SCALE)
    dq = dqh.transpose(1, 0, 2).reshape(T, HEADS * QK)
    n_owned = len(l1_send)
    reduced = sum_slots(l1_recv[n_owned:], "sum_chunks")
    (dkv, dkr), chunk_all = split_kv_grads(dK, dV, "l0_kv_b", rode=[jnp.stack(reduced[:2]), jnp.stack(reduced[2:])],
                                           modes="gather")
    (grad_x,), (dqg, dkvg, dsh0, dsc0, dng0), (g_uq, g_ukv, g_p) = rowwise(
        st_l0_tail_bwd, [dq, dkv, dkr, dz0, cq, ckv, cqn, ckvn, h0, xa, dx1], [qg, kvg, ng[0], sc[0]],
        [(D, F32, "lat")], [QL, KVL, D, D, D], "l0_pre_b", mats=[Wt["mla_w_uq"], Wt["mla_w_ukv"], Wt["mla_w_in"]],
        out_accs=[(QL, HEADS * QK), (KVL, HEADS * KVW), (D, IN_WP)])
    g_w_uq, g_w_ukv = _col_shards(g_uq).astype(BF16), _col_shards(g_ukv).astype(BF16)
    g_w_in = _col_shards(jnp.concatenate([g_p[:, HEADS * VD:IN_W], g_p[:, :HEADS * VD]], axis=1)).astype(BF16)

    dmod = jnp.stack([jnp.concatenate([dsh0, dsc0, dgt0], axis=-1)[:, 0], jnp.concatenate([dsh1, dsc1, dgt1], axis=-1)[:, 0]])
    gbig = {"mla_w_in": g_w_in, "mla_w_uq": g_w_uq, "mla_w_ukv": g_w_ukv}
    gsmall = {"norm_g": jnp.stack([both(dng0), both(dng1)]), "mla_q_norm": both(dqg), "mla_kv_norm": both(dkvg),
              "s5_a_re": g_a_re, "s5_a_im": g_a_im, "s5_log_step": g_ls, "final_g": dfg[1, 0]}
    return lvec[1], grad_x, dmod, gbig, gsmall, l1_recv[:n_owned], chunk_all


COL_SHARDED = ("mla_w_in", "mla_w_uq", "mla_w_ukv", "s5_w_in")
ROW_SHARDED = ("mla_w_out", "s5_w_glu", "s5_w_out")
VEC_SHARDED = ("s5_d", "s5_b_glu")
BIG = COL_SHARDED + ROW_SHARDED
L0_BIG = ("mla_w_in", "mla_w_uq", "mla_w_ukv")
L1_BIG = ("s5_w_in", "s5_w_glu", "s5_w_out", "mla_w_out")
BITS16 = jnp.bfloat16
SMALL_RS = ("norm_g", "mla_q_norm", "mla_kv_norm", "s5_a_re", "s5_a_im", "s5_log_step", "s5_b_re", "s5_b_im",
            "s5_c_re", "s5_c_im", "final_g")
CHUNKED = ("s5_b_re", "s5_b_im", "s5_c_re", "s5_c_im")
DENSE = ("s5_b_re", "s5_b_im")
TINY = ("norm_g", "mla_q_norm", "mla_kv_norm", "s5_a_re", "s5_a_im", "s5_log_step", "final_g")
ORDER = ("c_ctx", "ada_w", "ada_b", "norm_g", "mla_w_in", "mla_q_norm", "mla_w_uq", "mla_kv_norm", "mla_w_ukv",
         "mla_w_out", "s5_w_in", "s5_a_re", "s5_a_im", "s5_log_step", "s5_b_re", "s5_b_im", "s5_c_re", "s5_c_im",
         "s5_d", "s5_w_glu", "s5_b_glu", "s5_w_out", "final_g")


def kernel(x, c, ctx, c_ctx, ada_w, ada_b, norm_g, mla_w_in, mla_q_norm, mla_w_uq, mla_kv_norm, mla_w_ukv, mla_w_out, s5_w_in, s5_a_re, s5_a_im, s5_log_step, s5_b_re, s5_b_im, s5_c_re, s5_c_im, s5_d, s5_w_glu, s5_b_glu, s5_w_out, final_g, loss_target, m_c_ctx, m_ada_w, m_ada_b, m_norm_g, m_mla_w_in, m_mla_q_norm, m_mla_w_uq, m_mla_kv_norm, m_mla_w_ukv, m_mla_w_out, m_s5_w_in, m_s5_a_re, m_s5_a_im, m_s5_log_step, m_s5_b_re, m_s5_b_im, m_s5_c_re, m_s5_c_im, m_s5_d, m_s5_w_glu, m_s5_b_glu, m_s5_w_out, m_final_g, v_c_ctx, v_ada_w, v_ada_b, v_norm_g, v_mla_w_in, v_mla_q_norm, v_mla_w_uq, v_mla_kv_norm, v_mla_w_ukv, v_mla_w_out, v_s5_w_in, v_s5_a_re, v_s5_a_im, v_s5_log_step, v_s5_b_re, v_s5_b_im, v_s5_c_re, v_s5_c_im, v_s5_d, v_s5_w_glu, v_s5_b_glu, v_s5_w_out, v_final_g):
    w = dict(c_ctx=c_ctx, ada_w=ada_w, ada_b=ada_b, norm_g=norm_g, mla_w_in=mla_w_in, mla_q_norm=mla_q_norm,
             mla_w_uq=mla_w_uq, mla_kv_norm=mla_kv_norm, mla_w_ukv=mla_w_ukv, mla_w_out=mla_w_out, s5_w_in=s5_w_in,
             s5_a_re=s5_a_re, s5_a_im=s5_a_im, s5_log_step=s5_log_step, s5_b_re=s5_b_re, s5_b_im=s5_b_im,
             s5_c_re=s5_c_re, s5_c_im=s5_c_im, s5_d=s5_d, s5_w_glu=s5_w_glu, s5_b_glu=s5_b_glu, s5_w_out=s5_w_out,
             final_g=final_g)
    m = dict(c_ctx=m_c_ctx, ada_w=m_ada_w, ada_b=m_ada_b, norm_g=m_norm_g, mla_w_in=m_mla_w_in, mla_q_norm=m_mla_q_norm,
             mla_w_uq=m_mla_w_uq, mla_kv_norm=m_mla_kv_norm, mla_w_ukv=m_mla_w_ukv, mla_w_out=m_mla_w_out,
             s5_w_in=m_s5_w_in, s5_a_re=m_s5_a_re, s5_a_im=m_s5_a_im, s5_log_step=m_s5_log_step, s5_b_re=m_s5_b_re,
             s5_b_im=m_s5_b_im, s5_c_re=m_s5_c_re, s5_c_im=m_s5_c_im, s5_d=m_s5_d, s5_w_glu=m_s5_w_glu,
             s5_b_glu=m_s5_b_glu, s5_w_out=m_s5_w_out, final_g=m_final_g)
    v = dict(c_ctx=v_c_ctx, ada_w=v_ada_w, ada_b=v_ada_b, norm_g=v_norm_g, mla_w_in=v_mla_w_in, mla_q_norm=v_mla_q_norm,
             mla_w_uq=v_mla_w_uq, mla_kv_norm=v_mla_kv_norm, mla_w_ukv=v_mla_w_ukv, mla_w_out=v_mla_w_out,
             s5_w_in=v_s5_w_in, s5_a_re=v_s5_a_re, s5_a_im=v_s5_a_im, s5_log_step=v_s5_log_step, s5_b_re=v_s5_b_re,
             s5_b_im=v_s5_b_im, s5_c_re=v_s5_c_re, s5_c_im=v_s5_c_im, s5_d=v_s5_d, s5_w_glu=v_s5_w_glu,
             s5_b_glu=v_s5_b_glu, s5_w_out=v_s5_w_out, final_g=v_final_g)

    me = 4 * lax.axis_index("x") + 2 * lax.axis_index("y") + lax.axis_index("c")
    WA = ada_w.shape[2]

    def shard(n):
        return _t_shard(w[n], SHARD_ROWS[n]) if n in COL_SHARDED else w[n][0].astype(BF16)

    wgot = exchange([c] + [shard(n) for n in L0_BIG], "gather", "gather_w")

    cg = wgot[0].reshape(NDEV, D)
    cc2 = c_ctx.reshape(1, D)
    ada_b_loc = lax.dynamic_slice_in_dim(ada_b.reshape(2, 3 * D // WA, WA), me, 1, axis=1)
    part = ada_fwd(cg, cc2, ada_w, ada_b_loc, "ada_fwd")
    pg = exchange([part], "gather", "gather_mod")[0]
    mod_l = lax.dynamic_index_in_dim(pg, me, axis=2, keepdims=False).transpose(1, 0, 2).reshape(2, 3 * D)
    mod_c = pg[:, :, NDEV, :].transpose(1, 0, 2).reshape(2, 3 * D)
    mod = jnp.stack([mod_c, mod_l], axis=1)

    Wt = {n: a.reshape(-1, a.shape[-1]) for n, a in zip(L0_BIG, wgot[1:])}
    Wt["mla_w_in"] = mm(_win_order(), Wt["mla_w_in"], "nn", "w_in_order", out_dtype=BF16)
    vec_bits = lax.bitcast_convert_type(jnp.concatenate([s5_d, s5_b_glu], axis=0), BITS16).reshape(2, -1)
    small = {n: w[n] for n in SMALL_RS}

    lvec, grad_x, dmod, gbig, gsmall, l1_recv, (bb_all, cc_all) = local_step(
        ctx[0], x[0], loss_target[0], mod, Wt, small, [shard(n) for n in L1_BIG] + [vec_bits])
    grad_x = grad_x[None]

    recv = dict(zip(L1_BIG + VEC_SHARDED, l1_recv))
    out = {}

    def keep(n, res):
        for key, arr in zip("gdmv", res):
            out[key, n] = arr.reshape(w[n].shape)

    kshape = lambda n: w[n].shape if w[n].ndim > 1 else (1, w[n].size)
    flat = jnp.concatenate([gsmall[n].reshape(-1) for n in TINY] + [dmod.reshape(-1), lvec.reshape(-1)])[None]
    *l0_recv, flat_all = exchange([gbig[n] for n in L0_BIG] + [flat], ["lead"] * len(L0_BIG) + ["gather"], "scatter_grads")
    chunk_all = [bb_all[:, 0], bb_all[:, 1], cc_all[:, 0], cc_all[:, 1]]
    tiny_all, off = [], 0
    for n in TINY:
        tiny_all.append(flat_all[:, 0, off:off + w[n].size].reshape((NDEV,) + kshape(n)))
        off += w[n].size
    dm_all = flat_all[:, 0, off:off + dmod.size].reshape((NDEV,) + dmod.shape)
    loss = sum_slots([flat_all[:, :, off + dmod.size:]], "loss_sum")[0][0, 0]

    dm_cols = lax.dynamic_slice_in_dim(dm_all.reshape(NDEV, 2, 2, 3 * D // WA, WA), me, 1, axis=3)[:, :, :, 0]
    dm_loc = jnp.concatenate([dm_cols[:, :, 1].transpose(1, 0, 2), dm_cols[:, :, 0].transpose(1, 0, 2)], axis=1)
    g_ada_w, dcc_part, g_ada_b = ada_bwd(cg, cc2, ada_w, dm_loc, dm_all.transpose(0, 2, 1, 3).reshape(2 * NDEV, 2, 3 * D), "ada_bwd")
    dcc_all = exchange([dcc_part], "gather", "gather_dcc")[0].reshape(NDEV, D)
    g_c_ctx = cctx_finish(dcc_all, cc2, "cctx_finish")

    flat2 = lambda t: t.reshape(-1, t.shape[-1])
    recv.update(dict(zip(L0_BIG, l0_recv)))
    big = [(recv[n], w[n][0], m[n][0], v[n][0]) for n in BIG]
    big.append((flat2(g_ada_w)[None], flat2(ada_w), flat2(m_ada_w), flat2(v_ada_w)))
    for n, r in zip(BIG + ("ada_w",), adamw_rows(big, "adamw_big")[0]):
        keep(n, r)
    items = []
    halves = 2
    for n, g in zip(CHUNKED, chunk_all):
        blk = (1, 1, G // halves) + w[n].shape[3:]
        g = jnp.moveaxis(g, 0, 1).reshape(w[n].shape)
        g_spec = pl.BlockSpec((1,) + blk, lambda d, s: (0, 0, d, s, 0, 0))
        items.append((g[None], g_spec, w[n], m[n], v[n], pl.BlockSpec(blk, lambda d, s: (0, d, s, 0, 0))))
    for n, res in zip(CHUNKED, adamw_multi(items, (2, halves), "adamw_bc")):
        keep(n, res)
    tiny_g = dict(zip(TINY, tiny_all))
    tiny_g.update({n: recv[n] for n in VEC_SHARDED})
    tiny_g["c_ctx"], tiny_g["ada_b"] = g_c_ctx[None], g_ada_b[None]
    names = list(tiny_g)
    items = [(tiny_g[n], _whole(tiny_g[n], 1)) + tuple(t[n].reshape(kshape(n)) for t in (w, m, v))
             + (pl.BlockSpec(kshape(n), lambda i, r=len(kshape(n)): (0,) * r),) for n in names]
    for n, res in zip(names, adamw_multi(items, (1,), "adamw_small")):
        keep(n, res)

    return (loss, grad_x, *[out["g", n] for n in ORDER], *[out["d", n] for n in ORDER],
            *[out["m", n] for n in ORDER], *[out["v", n] for n in ORDER])
```

```python
import math

import numpy as np
import jax
import jax.numpy as jnp
from jax import lax
from jax.experimental import pallas as pl
from jax.experimental.pallas import tpu as pltpu

F32 = jnp.float32
BF16 = jnp.bfloat16

D = 1024
L = 2048
LC = 256
NDEV = 8
GRID_W = 64
EPS = 1e-6
HEADS = 16
NOPE = 64
ROPE = 32
QK = NOPE + ROPE
VD = 64
IN_W = 256 + 128 + ROPE + HEADS * 64
IN_WP = 1536
QL = 256
KVL = 128
SCALE = QK ** -0.5
LOG2E = math.log2(math.e)
THETA = 10000.0
G = 64
P = 64
CH = 16
GB = 8
NJ = G // GB
UB = GB * CH
SB = GB * P
SEG = 16
TB = 256
VMEM_LIMIT = 56 * 1024 * 1024
B1, B2, LR, AEPS, WD, STEP = 0.9, 0.999, 0.001, 1e-8, 0.01, 10
MESH_T = pl.DeviceIdType.MESH


def _cp(sem=None):
    return pltpu.CompilerParams(dimension_semantics=sem, vmem_limit_bytes=VMEM_LIMIT)


def _sig(x):
    return 1.0 / (1.0 + jnp.exp(-x))


def _silu(x):
    return x * _sig(x)


def _dsilu(x):
    s = _sig(x)
    return s * (1.0 + x * (1.0 - s))


_GK = math.sqrt(2.0 / math.pi)


def _gelu(x):
    return 0.5 * x * (1.0 + jnp.tanh(_GK * (x + 0.044715 * x * x * x)))


def _dgelu(x):
    t = jnp.tanh(_GK * (x + 0.044715 * x * x * x))
    return 0.5 * (1.0 + t) + 0.5 * x * (1.0 - t * t) * _GK * (1.0 + 3 * 0.044715 * x * x)


def _rs(x):
    return lax.rsqrt(jnp.mean(x * x, axis=-1, keepdims=True) + EPS)


def _sum0(x):
    return jnp.sum(x, axis=0, keepdims=True)


def st_norm_mod(x, g, sc, sh):
    y = x * _rs(x) * g
    return (y * (1.0 + sc) + sh,), ()


def st_norm_mod_bwd(x, dh, dres, g, sc):
    r = _rs(x)
    xn = x * r
    y = xn * g
    dy = dh * (1.0 + sc)
    dxn = dy * g
    dx = r * (dxn - xn * jnp.mean(dxn * xn, axis=-1, keepdims=True))
    return (dres + dx,), (_sum0(dh), _sum0(dh * y), _sum0(dy * xn))


def st_rms(x, g):
    return (x * _rs(x) * g,), ()


def st_rms_bwd(x, dy, g):
    r = _rs(x)
    n = x * r
    dn = dy * g
    dx = r * (dn - n * jnp.mean(dn * n, axis=-1, keepdims=True))
    return (dx,), (_sum0(dy * n),)


def st_rms2(x1, x2, g1, g2):
    return st_rms(x1, g1)[0] + st_rms(x2, g2)[0], ()


def st_rms2_bwd(x1, dy1, x2, dy2, g1, g2):
    (d1,), (s1,) = st_rms_bwd(x1, dy1, g1)
    (d2,), (s2,) = st_rms_bwd(x2, dy2, g2)
    return (d1, d2), (s1, s2)


def st_gate_bwd(dog, o, z):
    return (dog * _silu(z), dog * o * _dsilu(z)), ()


def st_resid_bwd(dx, out, gt):
    return (dx * gt,), (_sum0(dx * out),)


def st_s5a(yssm, u, d):
    y = yssm + d * u
    return (y, _gelu(y)), ()


def st_s5b_bwd(dy3, y, gl, z, b):
    y1 = _gelu(y)
    s = _sig(gl + b)
    dy2 = dy3 * _silu(z)
    dz = dy3 * y1 * s * _dsilu(z)
    dgl = dy2 * y1 * s * (1.0 - s)
    return (dgl, dz, dy2 * s), (_sum0(dgl),)


def st_s5a_bwd(dy1a, dy1b, y, u, d):
    dy = (dy1a + dy1b) * _dgelu(y)
    return (dy, dy * d), (_sum0(dy * u),)


def st_l0_pre(x, g, sc, sh, qg, kvg, w_in):
    hb = st_norm_mod(x, g, sc, sh)[0][0].astype(BF16)
    p = lax.dot_general(hb, w_in, _DN["nt"], preferred_element_type=F32)
    cq, ckv = p[:, HEADS * VD:HEADS * VD + QL], p[:, HEADS * VD + QL:HEADS * VD + QL + KVL]
    return (hb, p) + st_rms2(cq, ckv, qg, kvg)[0], ()


def st_l0_tail_bwd(dq, dkv, dkr, dz, cq, ckv, cqn, ckvn, h, x, dres, qg, kvg, g, sc, w_uq, w_ukv, w_in):
    dcqn = jnp.dot(dq, w_uq, preferred_element_type=F32)
    dckvn = jnp.dot(dkv, w_ukv, preferred_element_type=F32)
    (dcq, dckv), (dqg, dkvg) = st_rms2_bwd(cq, dcqn, ckv, dckvn, qg, kvg)
    dp = jnp.concatenate([dz, dcq, dckv, dkr], axis=1).astype(BF16)
    dh = jnp.dot(dp, w_in, preferred_element_type=F32)
    outs, sums = st_norm_mod_bwd(x, dh, dres, g, sc)
    tn = lambda a, b: lax.dot_general(a, b, _DN["tn"], preferred_element_type=F32)
    return outs, (dqg, dkvg) + sums, (tn(cqn, dq), tn(ckvn, dkv), tn(h, dp))


def st_l1_pre(x, g, sc, sh, w_in):
    hb = st_norm_mod(x, g, sc, sh)[0][0].astype(BF16)
    return (hb, lax.dot_general(hb, w_in, _DN["nt"], preferred_element_type=F32)), ()


def st_l1_tail_bwd(du_a, du_b, dz, h, x, dres, g, sc, w_in):
    dp = jnp.concatenate([(du_a + du_b).astype(BF16), dz], axis=1)
    dh = jnp.dot(dp, w_in, preferred_element_type=F32)
    outs, sums = st_norm_mod_bwd(x, dh, dres, g, sc)
    w = dp.shape[1] // NDEV
    shards = [lax.dot_general(h, dp[:, r * w:(r + 1) * w], _DN["tn"], preferred_element_type=F32) for r in range(NDEV)]
    return outs, sums, (jnp.stack(shards),)


def st_l0_post(o, z, x, gt, w_out):
    og = (o * _silu(z)).astype(BF16)
    out = jnp.dot(og, w_out, preferred_element_type=F32)
    return (og, out, x + gt * out), ()


def st_l0_post_bwd(dx1, out, og, o, z, gt, w_out):
    (dout,), (dgt,) = st_resid_bwd(dx1, out.astype(F32), gt)
    doutb = dout.astype(BF16)
    dog = lax.dot_general(doutb, w_out, _DN["nt"], preferred_element_type=F32)
    return st_gate_bwd(dog, o, z)[0], (dgt,), (lax.dot_general(og, doutb, _DN["tn"], preferred_element_type=F32),)


def st_l1_mlp(yssm, u, z, x1, tgt, d, bglu, gt, fg, mask, w_glu, w_out):
    (y, y1), _ = st_s5a(yssm, u, d)
    y1b = y1.astype(BF16)
    gl = jnp.dot(y1b, w_glu, preferred_element_type=F32)
    y3 = (y1 * _sig(gl + bglu) * _silu(z)).astype(BF16)
    out = jnp.dot(y3, w_out, preferred_element_type=F32)
    (dx2,), sums = st_final(x1 + gt * out, tgt, fg, mask)
    return (y, y1b, gl, y3, out, dx2), sums


def st_l1_mlp_bwd(dx2, out, y3, y, gl, z, u, y1b, gt, bglu, d, w_out, w_glu):
    out, gl = out.astype(F32), gl.astype(F32)
    (dout,), (dgt,) = st_resid_bwd(dx2, out, gt)
    doutb = dout.astype(BF16)
    dy3 = lax.dot_general(doutb, w_out, _DN["nt"], preferred_element_type=F32)
    (dgl, dz, dy1a), (dbglu,) = st_s5b_bwd(dy3, y, gl, z, bglu)
    dglb = dgl.astype(BF16)
    dy1b = lax.dot_general(dglb, w_glu, _DN["nt"], preferred_element_type=F32)
    (dy, du), (dd,) = st_s5a_bwd(dy1a, dy1b, y, u, d)
    g_w_out = lax.dot_general(y3, doutb, _DN["tn"], preferred_element_type=F32)
    g_w_glu = lax.dot_general(y1b, dglb, _DN["tn"], preferred_element_type=F32)
    return (dz, dy, du), (dgt, dbglu, dd), (g_w_out, g_w_glu)


def st_final(x2, tgt, g, mask):
    r = _rs(x2)
    n = x2 * r
    e = n * g - tgt
    dyo = e * (1.0 / D)
    dn = dyo * g
    dx = r * (dn - n * jnp.mean(dn * n, axis=-1, keepdims=True))
    lsum = jnp.sum(_sum0(e * e), axis=1, keepdims=True) * (0.5 / D)
    return (dx * mask,), (_sum0(dyo * n), jnp.broadcast_to(lsum, (1, 128)))


def rowwise(fn, rows, vecs, out_rows, out_sums, name, mats=(), out_accs=()):
    lat_blk = lambda i: jnp.maximum(i - 1, 0)
    arrays, in_specs, pick = [], [], []
    for a in rows:
        if not isinstance(a, tuple):
            a = (a, 0, a.shape[1])
        tag = a[0] if isinstance(a[0], str) else None
        if tag == "cat":
            _, ctx, x = a
            arrays += [ctx, x]
            in_specs += [pl.BlockSpec((TB, ctx.shape[1]), lambda i: (0, 0)),
                         pl.BlockSpec((TB, x.shape[1]), lambda i: (lat_blk(i), 0))]
            pick.append(2)
        elif tag == "lat":
            arrays.append(a[1])
            in_specs.append(pl.BlockSpec((TB, a[1].shape[1]), lambda i: (lat_blk(i), 0)))
            pick.append(1)
        else:
            arr, cb, width = a
            arrays.append(arr)
            in_specs.append(pl.BlockSpec((TB, width), lambda i, cb=cb: (i, cb)))
            pick.append(1)
    T = LC + L
    nin, nv, nm, no, ns = len(arrays), len(vecs), len(mats), len(out_rows), len(out_sums)

    def body(*refs):
        i = pl.program_id(0)
        vals, k = [], 0
        for p in pick:
            if p == 2:
                vals.append(jnp.where(i == 0, refs[k][...], refs[k + 1][...]))
            else:
                vals.append(refs[k][...])
            k += p
        vals += [r[0] for r in refs[nin:nin + nv]] + [r[...] for r in refs[nin + nv:nin + nv + nm]]
        res = fn(*vals)
        first_out = nin + nv + nm
        for r, o in zip(refs[first_out:first_out + no], res[0]):
            r[...] = o.astype(r.dtype)
        sum_refs = refs[first_out + no:first_out + no + ns]
        if sum_refs:
            @pl.when(i <= 1)
            def _():
                for r in sum_refs:
                    r[...] = jnp.zeros_like(r)
            for r, s in zip(sum_refs, res[1]):
                r[0] += s
        na = len(out_accs)
        if na:
            acc_out, acc = refs[first_out + no + ns:first_out + no + ns + na], refs[first_out + no + ns + na:]

            @pl.when(i == 0)
            def _():
                for r in acc:
                    r[...] = jnp.zeros_like(r)
            for r, a in zip(acc, res[2]):
                r[...] += a

            @pl.when(i == T // TB - 1)
            def _():
                for o, r in zip(acc_out, acc):
                    o[...] = r[...].astype(o.dtype)

    kind = lambda i: (jnp.minimum(i, 1), 0, 0)
    in_specs += [pl.BlockSpec((1, 1, v.shape[2]), kind) for v in vecs]
    in_specs += [pl.BlockSpec(m.shape, lambda i: (0, 0), pipeline_mode=pl.Buffered(1)) for m in mats]
    out_specs, out_shape = [], []
    for o in out_rows:
        lat = len(o) == 3
        out_specs.append(pl.BlockSpec((TB, o[0]), (lambda i: (lat_blk(i), 0)) if lat else (lambda i: (i, 0))))
        out_shape.append(jax.ShapeDtypeStruct((L if lat else T, o[0]), o[1]))
    out_specs += [pl.BlockSpec((1, 1, c), kind) for c in out_sums]
    out_shape += [jax.ShapeDtypeStruct((2, 1, c), F32) for c in out_sums]
    out_specs += [pl.BlockSpec(s, lambda i, r=len(s): (0,) * r) for s in out_accs]
    out_shape += [jax.ShapeDtypeStruct(s, BF16) for s in out_accs]
    res = pl.pallas_call(body, grid=(T // TB,), in_specs=in_specs, out_specs=out_specs, out_shape=out_shape,
                         scratch_shapes=[pltpu.VMEM(s, F32) for s in out_accs],
                         compiler_params=_cp(("arbitrary",)), name=name)(*arrays, *vecs, *mats)
    if out_accs:
        return res[:no], res[no:no + ns], res[no + ns:]
    return res[:no], res[no:]


_DN = {"nn": (((1,), (0,)), ((), ())), "nt": (((1,), (1,)), ((), ())), "tn": (((0,), (0,)), ((), ()))}


def mm(a, b, mode, name, out_dtype=F32, tm=None, tn=None):
    if mode == "nn":
        (M, K), (_, N) = a.shape, b.shape
    elif mode == "nt":
        (M, K), (N, _) = a.shape, b.shape
    else:
        (K, M), (_, N) = a.shape, b.shape
    if tm is None:
        tm = next((t for t in (768, 512, 256) if M % t == 0 and M > t), M)
    tn = N if tn is None else tn
    dn = _DN[mode]

    def body(a_ref, b_ref, o_ref):
        o_ref[...] = lax.dot_general(a_ref[...].astype(BF16), b_ref[...].astype(BF16), dn,
                                     preferred_element_type=F32).astype(o_ref.dtype)

    a_spec = pl.BlockSpec((K, tm), lambda i, j: (0, i)) if mode == "tn" else pl.BlockSpec((tm, K), lambda i, j: (i, 0))
    b_spec = pl.BlockSpec((tn, K), lambda i, j: (j, 0)) if mode == "nt" else pl.BlockSpec((K, tn), lambda i, j: (0, j))
    return pl.pallas_call(body, grid=(M // tm, N // tn), in_specs=[a_spec, b_spec],
                          out_specs=pl.BlockSpec((tm, tn), lambda i, j: (i, j)), out_shape=jax.ShapeDtypeStruct((M, N), out_dtype),
                          compiler_params=_cp(("parallel", "arbitrary")), name=name)(a, b)


def _rope_tables(T, width=QK, first=NOPE):
    nlat = T - LC
    pos = np.arange(nlat)
    row, col = pos // GRID_W, pos % GRID_W
    half = ROPE // 2
    inv = 1.0 / (THETA ** (np.arange(0, half, 2, dtype=np.float64) / half))
    cosf = np.ones((T, width), np.float64)
    sinf = np.zeros((T, width), np.float64)
    perm = np.zeros((width, width), np.float32)
    for m in range(ROPE):
        j = first + m
        blk, w = m // half, m % half
        ang = (row if blk == 0 else col)[:, None] * inv[None, :]
        f = w % (half // 2)
        cosf[LC:, j] = np.cos(ang[:, f])
        if w < half // 2:
            sinf[LC:, j] = -np.sin(ang[:, f])
            perm[j + half // 2, j] = 1.0
        else:
            sinf[LC:, j] = np.sin(ang[:, f])
            perm[j - half // 2, j] = 1.0
    return jnp.asarray(cosf, F32), jnp.asarray(sinf, F32), jnp.asarray(perm, BF16), jnp.asarray(perm.T, BF16)


def _exact_perm(x, pm):
    hi = x.astype(BF16)
    r1 = x - hi.astype(F32)
    mid = r1.astype(BF16)
    lo = (r1 - mid.astype(F32)).astype(BF16)
    dot = lambda a: jnp.dot(a, pm, preferred_element_type=F32)
    return dot(hi) + dot(mid) + dot(lo)


def _rot(x, cv, sv, pv, inverse):
    if inverse:
        return x * cv + _exact_perm(x * sv, pv)
    return x * cv + _exact_perm(x, pv) * sv


def rope_bwd(dx, cosf, sinf, pmt, scale, name):
    H, T, _ = dx.shape

    def body(x_ref, c_ref, s_ref, p_ref, o_ref):
        cv, sv, pv = c_ref[...], s_ref[...], p_ref[...]
        for h in range(H):
            o_ref[:, pl.ds(h * QK, QK)] = (_rot(x_ref[h], cv, sv, pv, True) * scale).astype(o_ref.dtype)

    return pl.pallas_call(
        body, grid=(T // TB,),
        in_specs=[pl.BlockSpec((H, TB, QK), lambda i: (0, i, 0)), pl.BlockSpec((TB, QK), lambda i: (i, 0)),
                  pl.BlockSpec((TB, QK), lambda i: (i, 0)), pl.BlockSpec((QK, QK), lambda i: (0, 0))],
        out_specs=pl.BlockSpec((TB, H * QK), lambda i: (i, 0)), out_shape=jax.ShapeDtypeStruct((T, H * QK), BF16),
        compiler_params=_cp(("parallel",)), name=name)(dx, cosf, sinf, pmt)


KVW = NOPE + VD


def project_q(cqn, w, name):
    T = cqn.shape[0]
    cosf, sinf, _, _ = _rope_tables(T, 128, NOPE)
    wp = jnp.pad(w.reshape(HEADS, QK, QL), ((0, 0), (0, 128 - QK), (0, 0))).reshape(HEADS * 128, QL)

    def body(a_ref, w_ref, c_ref, s_ref, o_ref):
        a, cv, sv = a_ref[...], c_ref[...], s_ref[...]
        first_of_pair = lax.bitwise_and(lax.broadcasted_iota(jnp.int32, (TB, 128), 1), ROPE // 4) == 0
        for h in range(HEADS):
            qh = _dotf(a, w_ref[pl.ds(h * 128, 128), :], "nt")
            swap = jnp.where(first_of_pair, pltpu.roll(qh, 128 - ROPE // 4, 1), pltpu.roll(qh, ROPE // 4, 1))
            o_ref[h] = ((qh * cv + swap * sv) * (SCALE * LOG2E))[:, :QK].astype(BF16)

    rows = lambda c: pl.BlockSpec((TB, c), lambda i: (i, 0))
    return pl.pallas_call(
        body, grid=(T // TB,), in_specs=[rows(QL), pl.BlockSpec(wp.shape, lambda i: (0, 0)), rows(128), rows(128)],
        out_specs=pl.BlockSpec((HEADS, TB, QK), lambda i: (0, i, 0)), out_shape=jax.ShapeDtypeStruct((HEADS, T, QK), BF16),
        compiler_params=_cp(("parallel",)), name=name)(cqn, wp, cosf, sinf)


def project_kv(ckvn, w, p0, kr_block, name):
    T = ckvn.shape[0]
    assert KVW == 128 and NOPE == VD
    cosf, sinf, pm, _ = _rope_tables(T, 128, 0)

    def body(a_ref, w_ref, kr_ref, c_ref, s_ref, p_ref, k_ref, v_ref):
        a = a_ref[...]
        is_nope = lax.broadcasted_iota(jnp.int32, (TB, KVW), 1) < NOPE
        kr_at = pltpu.roll(_rot(kr_ref[...], c_ref[...], s_ref[...], p_ref[...], False), NOPE, 1)
        for h in range(HEADS):
            kv = _dotf(a, w_ref[pl.ds(h * KVW, KVW), :], "nt")
            k_ref[h] = jnp.where(is_nope, kv, kr_at)[:, :QK].astype(BF16)
            v_ref[h] = pltpu.roll(kv, VD, 1)[:, :VD].astype(BF16)

    rows = lambda c: pl.BlockSpec((TB, c), lambda i: (i, 0))
    const = lambda x: pl.BlockSpec(x.shape, lambda i: (0, 0))
    return pl.pallas_call(
        body, grid=(T // TB,),
        in_specs=[rows(KVL), const(w), pl.BlockSpec((TB, 128), lambda i: (i, kr_block)), rows(128), rows(128), const(pm)],
        out_specs=[pl.BlockSpec((HEADS, TB, QK), lambda i: (0, i, 0)), pl.BlockSpec((HEADS, TB, VD), lambda i: (0, i, 0))],
        out_shape=[jax.ShapeDtypeStruct((HEADS, T, QK), BF16), jax.ShapeDtypeStruct((HEADS, T, VD), BF16)],
        compiler_params=_cp(("parallel",)), name=name)(ckvn, w, p0, cosf, sinf, pm)


def split_kv_grads(dk, dv, name, rode=None, modes=None):
    H, T, _ = dk.shape
    cosf, sinf, _, pmt = _rope_tables(T, 128, 0)
    to_rope_block = np.zeros((QK, 128), np.float32)
    to_rope_block[NOPE + np.arange(ROPE), np.arange(ROPE)] = 1.0
    to_rope_block = jnp.asarray(to_rope_block, BF16)

    def body(dk_ref, dv_ref, c_ref, s_ref, p_ref, sel_ref, dkv_ref, dkr_ref):
        total = None
        for h in range(H):
            dkh = dk_ref[h] * (1.0 / LOG2E)
            total = dkh if total is None else total + dkh
            dkv_ref[:, pl.ds(h * KVW, NOPE)] = dkh[:, :NOPE].astype(BF16)
            dkv_ref[:, pl.ds(h * KVW + NOPE, VD)] = dv_ref[h].astype(BF16)
        dkr_ref[...] = _rot(_exact_perm(total, sel_ref[...]), c_ref[...], s_ref[...], p_ref[...], True)

    rows = lambda c: pl.BlockSpec((TB, c), lambda i, j: (i, 0))
    const = lambda a: pl.BlockSpec(a.shape, lambda i, j: (0, 0))
    return _ride_call(
        body, (T // TB, 1),
        [pl.BlockSpec((H, TB, QK), lambda i, j: (0, i, 0)), pl.BlockSpec((H, TB, VD), lambda i, j: (0, i, 0)),
         rows(128), rows(128), const(pmt), const(to_rope_block)],
        [rows(H * KVW), rows(128)],
        [jax.ShapeDtypeStruct((T, H * KVW), BF16), jax.ShapeDtypeStruct((T, 128), F32)],
        Exchange(rode, modes) if rode else None, rode, name, (dk, dv, cosf, sinf, pmt, to_rope_block))


HB = 4
HBF = 8


def _by_query_block(run, T):
    @pl.when(pl.program_id(1) == 0)
    def _():
        run(LC)

    @pl.when(pl.program_id(1) > 0)
    def _():
        run(T)


def _with_rider(body, nin, nout, ride, grid):
    if ride is None:
        return body
    n = ride.n

    def wrapped(*refs):
        ins, xs = refs[:nin], refs[nin:nin + n]
        outs, got = refs[nin + n:nin + n + nout], refs[nin + n + nout:nin + 2 * n + nout]
        sems = refs[nin + 2 * n + nout:]
        step = pl.program_id(0) * grid[1] + pl.program_id(1)

        @pl.when(step == 0)
        def _():
            ride.start(xs, got, sems)

        body(*ins, *outs)

        @pl.when(step == grid[0] * grid[1] - 1)
        def _():
            ride.finish(xs, got, sems)

    return wrapped


def _ride_call(body, grid, in_specs, out_specs, out_shape, ride, rode, name, args):
    if ride is None:
        return pl.pallas_call(body, grid=grid, in_specs=in_specs, out_specs=out_specs, out_shape=out_shape,
                              compiler_params=_cp(("parallel", "arbitrary")), name=name)(*args), []
    res = pl.pallas_call(
        _with_rider(body, len(in_specs), len(out_specs), ride, grid), grid=grid,
        in_specs=in_specs + ride.specs, out_specs=out_specs + ride.specs, out_shape=out_shape + ride.out_shape,
        scratch_shapes=ride.scratch,
        compiler_params=pltpu.CompilerParams(dimension_semantics=("arbitrary", "arbitrary"), vmem_limit_bytes=VMEM_LIMIT,
                                             has_side_effects=True), name=name)(*args, *rode)
    return res[:len(out_specs)], res[len(out_specs):]


def attn_fwd(q, k, v, name, rode=None, modes=None):
    H, T, _ = q.shape

    def body(q_ref, k_ref, v_ref, o_ref, lse_ref):
        def run(nk):
            for hh in range(HBF):
                s = _dotf(q_ref[hh], k_ref[hh, pl.ds(0, nk), :], "nt")
                m = jnp.max(s, axis=1, keepdims=True)
                p = jnp.exp2(s - m)
                l = jnp.sum(p, axis=1, keepdims=True)
                o = jnp.dot(p.astype(BF16), v_ref[hh, pl.ds(0, nk), :], preferred_element_type=F32)
                o_ref[hh] = o / l
                lse_ref[hh] = m + jnp.log2(l)

        _by_query_block(run, T)

    return _ride_call(
        body, (H // HBF, T // TB),
        [pl.BlockSpec((HBF, TB, QK), lambda h, i: (h, i, 0)), pl.BlockSpec((HBF, T, QK), lambda h, i: (h, 0, 0)),
         pl.BlockSpec((HBF, T, VD), lambda h, i: (h, 0, 0))],
        [pl.BlockSpec((HBF, TB, VD), lambda h, i: (h, i, 0)), pl.BlockSpec((HBF, TB, 1), lambda h, i: (h, i, 0))],
        [jax.ShapeDtypeStruct((H, T, VD), F32), jax.ShapeDtypeStruct((H, T, 1), F32)],
        Exchange(rode, modes) if rode else None, rode, name, (q, k, v))


def attn_bwd(q, k, v, o, lse, do, name, rode=None, modes=None):
    H, T, _ = q.shape

    def body(q_ref, k_ref, v_ref, o_ref, lse_ref, do_ref, dq_ref, dk_ref, dv_ref):
        i = pl.program_id(1)

        @pl.when(i == 0)
        def _():
            dk_ref[...] = jnp.zeros_like(dk_ref)
            dv_ref[...] = jnp.zeros_like(dv_ref)

        def run(nk):
            keys = pl.ds(0, nk)
            for hh in range(HB):
                qv, kv, dov = q_ref[hh], k_ref[hh, keys, :], do_ref[hh]
                p = jnp.exp2(_dotf(qv, kv, "nt") - lse_ref[hh])
                delta = jnp.sum(dov * o_ref[hh], axis=1, keepdims=True)
                dob = dov.astype(BF16)
                dv_ref[hh, keys, :] += _dotf(p.astype(BF16), dob, "tn")
                dp = _dotf(dob, v_ref[hh, keys, :], "nt")
                ds = (p * (dp - delta)).astype(BF16)
                dq_ref[hh] = jnp.dot(ds, kv, preferred_element_type=F32)
                dk_ref[hh, keys, :] += _dotf(ds, qv, "tn")

        _by_query_block(run, T)

    blk = lambda c: pl.BlockSpec((HB, TB, c), lambda h, i: (h, i, 0))
    full = lambda c: pl.BlockSpec((HB, T, c), lambda h, i: (h, 0, 0))
    return _ride_call(
        body, (H // HB, T // TB), [blk(QK), full(QK), full(VD), blk(VD), blk(1), blk(VD)], [blk(QK), full(QK), full(VD)],
        [jax.ShapeDtypeStruct((H, T, QK), F32), jax.ShapeDtypeStruct((H, T, QK), F32), jax.ShapeDtypeStruct((H, T, VD), F32)],
        Exchange(rode, modes) if rode else None, rode, name, (q, k, v, o, lse, do))


def disc_fwd(a_re, a_im, ls, name):
    def body(ar_ref, ai_ref, ls_ref, lr_ref, li_ref, fr_ref, fi_ref):
        ar, ai = ar_ref[...], ai_ref[...]
        dt = jnp.exp(ls_ref[...])
        mag = jnp.exp(ar * dt)
        lr = mag * jnp.cos(ai * dt)
        li = mag * jnp.sin(ai * dt)
        den = ar * ar + ai * ai
        nr = lr - 1.0
        lr_ref[...] = lr
        li_ref[...] = li
        fr_ref[...] = (nr * ar + li * ai) / den
        fi_ref[...] = (li * ar - nr * ai) / den

    return pl.pallas_call(body, out_shape=[jax.ShapeDtypeStruct(a_re.shape, F32)] * 4, name=name)(a_re, a_im, ls)


def disc_b(f_re, f_im, b_re, b_im, name):
    def body(fr_ref, fi_ref, br_ref, bi_ref, or_ref, oi_ref):
        fr, fi, br, bi = fr_ref[...], fi_ref[...], br_ref[...], bi_ref[...]
        or_ref[...] = fr * br - fi * bi
        oi_ref[...] = fr * bi + fi * br

    return pl.pallas_call(body, out_shape=[jax.ShapeDtypeStruct(b_re.shape, F32)] * 2, compiler_params=_cp(),
                          name=name)(f_re, f_im, b_re, b_im)


def disc_b_bwd(f_re, f_im, b_re, b_im, dbb_re, dbb_im, name):
    def body(fr_ref, fi_ref, br_ref, bi_ref, dr_ref, di_ref, dbr_ref, dbi_ref, dfr_ref, dfi_ref):
        fr, fi, br, bi, dr, di = fr_ref[...], fi_ref[...], br_ref[...], bi_ref[...], dr_ref[...], di_ref[...]
        dbr_ref[...] = fr * dr + fi * di
        dbi_ref[...] = fr * di - fi * dr
        dfr_ref[...] = jnp.sum(dr * br + di * bi, axis=2, keepdims=True)
        dfi_ref[...] = jnp.sum(di * br - dr * bi, axis=2, keepdims=True)

    return pl.pallas_call(body, out_shape=[jax.ShapeDtypeStruct(b_re.shape, F32)] * 2 + [jax.ShapeDtypeStruct(f_re.shape, F32)] * 2,
                          compiler_params=_cp(), name=name)(f_re, f_im, b_re, b_im, dbb_re, dbb_im)


def disc_a_bwd(a_re, a_im, ls, dlr, dli, dfr, dfi, name):
    def body(ar_ref, ai_ref, ls_ref, dlr_ref, dli_ref, dfr_ref, dfi_ref, dar_ref, dai_ref, dls_ref):
        ar, ai = ar_ref[...], ai_ref[...]
        dt = jnp.exp(ls_ref[...])
        mag = jnp.exp(ar * dt)
        cs, sn = jnp.cos(ai * dt), jnp.sin(ai * dt)
        lr, li = mag * cs, mag * sn
        den = ar * ar + ai * ai
        nr = lr - 1.0
        f_re = (nr * ar + li * ai) / den
        f_im = (li * ar - nr * ai) / den
        dn1 = dfr_ref[...] / den
        dn2 = dfi_ref[...] / den
        dden = -(dfr_ref[...] * f_re + dfi_ref[...] * f_im) / den
        dlr_t = dlr_ref[...] + dn1 * ar - dn2 * ai
        dli_t = dli_ref[...] + dn1 * ai + dn2 * ar
        dar = dn1 * nr + dn2 * li + dden * 2.0 * ar
        dai = dn1 * li - dn2 * nr + dden * 2.0 * ai
        dmag = dlr_t * cs + dli_t * sn
        dth = dli_t * lr - dlr_t * li
        dar_ref[...] = dar + dmag * mag * dt
        dai_ref[...] = dai + dth * dt
        dls_ref[...] = jnp.sum(dmag * mag * ar + dth * ai, axis=-1, keepdims=True) * dt

    return pl.pallas_call(body, out_shape=[jax.ShapeDtypeStruct(a_re.shape, F32)] * 2 +
                          [jax.ShapeDtypeStruct(ls.shape, F32)], name=name)(a_re, a_im, ls, dlr, dli, dfr, dfi)


def _cpow(lr, li, n):
    rr, ri = None, None
    br, bi = lr, li
    while n:
        if n & 1:
            if rr is None:
                rr, ri = br, bi
            else:
                rr, ri = rr * br - ri * bi, rr * bi + ri * br
        n >>= 1
        if n:
            br, bi = br * br - bi * bi, 2.0 * br * bi
    return rr, ri


UNROLL = 4


def _steps(trips, fn, init):
    main = trips // UNROLL

    def body(i, c):
        for j in range(UNROLL):
            c = fn(i * UNROLL + j, c)
        return c

    c = lax.fori_loop(0, main, body, init) if main else init
    for n in range(main * UNROLL, trips):
        c = fn(n, c)
    return c


def _seg_scan(xre, xim, lam8, pw, base, seglen, rev, init, fin_re, fin_im, ini_re, ini_im, prev=None):
    lr, li = lam8
    nsub = SEG // 8

    def rows(t, s):
        first = base + t * SEG + 8 * s
        return pl.ds(first if isinstance(first, int) else pl.multiple_of(first, 8), 8)

    tmap = (lambda n: seglen - 1 - n) if rev else (lambda n: n)
    zeros = tuple(jnp.zeros((8, SB), F32) for _ in range(2 * nsub))

    def advance(c, t):
        out = []
        for s in range(nsub):
            a, b = c[2 * s], c[2 * s + 1]
            out += [lr * a - li * b + xre[rows(t, s), :], lr * b + li * a + xim[rows(t, s), :]]
        return tuple(out)

    fin = _steps(seglen, lambda n, c: advance(c, tmap(n)), zeros)
    for s in range(nsub):
        fin_re[pl.ds(8 * s, 8), :] = fin[2 * s]
        fin_im[pl.ds(8 * s, 8), :] = fin[2 * s + 1]
    (cr, ci), (pr, pi) = init, pw
    for i in (range(SEG - 1, -1, -1) if rev else range(SEG)):
        ini_re[pl.ds(i, 1), :] = cr
        ini_im[pl.ds(i, 1), :] = ci
        cr, ci = pr * cr - pi * ci + fin_re[pl.ds(i, 1), :], pr * ci + pi * cr + fin_im[pl.ds(i, 1), :]
    tiles = lambda re, im: tuple(r[pl.ds(8 * s, 8), :] for s in range(nsub) for r in (re, im))
    start = tiles(ini_re, ini_im)

    def store(c, t):
        new = advance(c, t)
        for s in range(nsub):
            xre[rows(t, s), :] = new[2 * s]
            xim[rows(t, s), :] = new[2 * s + 1]
        return new

    if prev is None:
        _steps(seglen, lambda n, c: store(c, tmap(n)), start)
        return (cr, ci), None

    sre, sim, s_ini_re, s_ini_im = prev

    def acc_step(c, t, before):
        new = store(c[:2 * nsub], t)
        acc = []
        for s in range(nsub):
            (na, nb), (pre, pim) = new[2 * s:2 * s + 2], before[2 * s:2 * s + 2]
            acc += [c[2 * nsub + 2 * s] + na * pre + nb * pim, c[2 * nsub + 2 * s + 1] + nb * pre - na * pim]
        return new + tuple(acc)

    def body(n, c):
        t = tmap(n)
        tp = t - 1 if rev else t + 1
        return acc_step(c, t, tuple(r[rows(tp, s), :] for s in range(nsub) for r in (sre, sim)))

    c = _steps(seglen - 1, body, start + zeros)
    c = acc_step(c, 0 if rev else seglen - 1, tiles(s_ini_re, s_ini_im))
    acc = c[2 * nsub:]
    return (cr, ci), (sum(acc[0::2][1:], acc[0]), sum(acc[1::2][1:], acc[1]))


def _lam_tiles(lr, li, lens, conj=False):
    if conj:
        li = -li
    lam8 = (jnp.broadcast_to(lr, (8, SB)), jnp.broadcast_to(li, (8, SB)))
    return lam8, [_cpow(lr, li, n) for n in lens]


def _stretches(T):
    return ((0, LC // SEG), (LC, (T - LC) // SEG))


def _to_seg_order(src, dst, T):
    for base, seglen in _stretches(T):
        def body(t, carry, base=base, seglen=seglen):
            dst[pl.ds(pl.multiple_of(base + t * SEG, SEG), SEG), :] = src[pl.ds(base + t, SEG, stride=seglen), :]
            return carry
        lax.fori_loop(0, seglen, body, 0, unroll=8)


def _from_seg_order(src, dst, T):
    for base, seglen in _stretches(T):
        def body(t, carry, base=base, seglen=seglen):
            dst[pl.ds(base + t, SEG, stride=seglen), :] = src[pl.ds(pl.multiple_of(base + t * SEG, SEG), SEG), :]
            return carry
        lax.fori_loop(0, seglen, body, 0, unroll=8)


def _scan_specs(T):
    ublk = pl.BlockSpec((T, UB), lambda j: (0, j))
    lam = pl.BlockSpec((2, 1, 1, SB), lambda j: (0, j, 0, 0))
    mat = pl.BlockSpec((2, 1, UB, P), lambda j: (0, j, 0, 0))
    return ublk, lam, mat


def _dotf(a, b, mode="nn"):
    return lax.dot_general(a, b, _DN[mode], preferred_element_type=F32)


def _diag_mask():
    r = lax.broadcasted_iota(jnp.int32, (UB, SB), 0)
    c = lax.broadcasted_iota(jnp.int32, (UB, SB), 1)
    return lax.shift_right_logical(r, int(math.log2(CH))) == lax.shift_right_logical(c, int(math.log2(P)))


def _expand(m):
    p = lax.broadcasted_iota(jnp.int32, (P, SB), 0)
    c = lax.broadcasted_iota(jnp.int32, (P, SB), 1)
    tile = jnp.where(lax.bitwise_and(c, P - 1) == p, 1.0, 0.0).astype(BF16)
    wide = jnp.dot(m.astype(BF16), tile, preferred_element_type=F32)
    return jnp.where(_diag_mask(), wide, 0.0).astype(BF16)


def _collapse(full):
    c = lax.broadcasted_iota(jnp.int32, (SB, P), 0)
    p = lax.broadcasted_iota(jnp.int32, (SB, P), 1)
    pick = jnp.where(lax.bitwise_and(c, P - 1) == p, 1.0, 0.0).astype(BF16)
    return _exact_perm(jnp.where(_diag_mask(), full, 0.0), pick)


def _zero_state():
    return jnp.zeros((1, SB), F32), jnp.zeros((1, SB), F32)


def scan_fwd(u, lam_re, lam_im, bre, bim, cre, cim, name):
    T = u.shape[0]
    s_ctx, s_lat = LC // SEG, (T - LC) // SEG

    def body(u_ref, lr_ref, li_ref, bre_ref, bim_ref, cre_ref, cim_ref, y_ref, us, ys, sre, sim, fre, fim, ire, iim):
        _to_seg_order(u_ref, us, T)
        ub = us[...].astype(BF16)
        for d in range(2):
            lam8, (pw_c, pw_l) = _lam_tiles(lr_ref[d, 0], li_ref[d, 0], (s_ctx, s_lat))
            sre[...] = _dotf(ub, _expand(bre_ref[d, 0]))
            sim[...] = _dotf(ub, _expand(bim_ref[d, 0]))
            end_c, _ = _seg_scan(sre, sim, lam8, pw_c, 0, s_ctx, bool(d), _zero_state(), fre, fim, ire, iim)
            _seg_scan(sre, sim, lam8, pw_l, LC, s_lat, bool(d), end_c, fre, fim, ire, iim)
            y = (_dotf(sre[...].astype(BF16), _expand(cre_ref[d, 0]), "nt")
                 - _dotf(sim[...].astype(BF16), _expand(cim_ref[d, 0]), "nt"))
            if d == 0:
                ys[...] = y
            else:
                ys[...] += y
        _from_seg_order(ys, y_ref, T)

    ublk, lam, mat = _scan_specs(T)
    return pl.pallas_call(
        body, grid=(NJ,), in_specs=[ublk, lam, lam, mat, mat, mat, mat], out_specs=ublk,
        out_shape=jax.ShapeDtypeStruct((T, G * CH), F32),
        scratch_shapes=[pltpu.VMEM((T, UB), F32)] * 2 + [pltpu.VMEM((T, SB), F32)] * 2 + [pltpu.VMEM((SEG, SB), F32)] * 4,
        compiler_params=_cp(("arbitrary",)), name=name)(u, lam_re, lam_im, bre, bim, cre, cim)


def scan_bwd(u, dy, lam_re, lam_im, bre, bim, cre, cim, name):
    T = u.shape[0]
    s_ctx, s_lat = LC // SEG, (T - LC) // SEG

    def body(u_ref, dy_ref, lr_ref, li_ref, bre_ref, bim_ref, cre_ref, cim_ref,
             du_ref, dlr_ref, dli_ref, dbre_ref, dbim_ref, dcre_ref, dcim_ref,
             us, dys, dus, sre, sim, gre, gim, fre, fim, ic_re, ic_im, il_re, il_im, jre, jim):
        _to_seg_order(u_ref, us, T)
        _to_seg_order(dy_ref, dys, T)
        ub, dyb = us[...].astype(BF16), dys[...].astype(BF16)
        for d in range(2):
            rev = bool(d)
            lam8, (pw_c, pw_l) = _lam_tiles(lr_ref[d, 0], li_ref[d, 0], (s_ctx, s_lat))
            cam8, (cw_c, cw_l) = _lam_tiles(lr_ref[d, 0], li_ref[d, 0], (s_ctx, s_lat), conj=True)
            bre_v, bim_v = _expand(bre_ref[d, 0]), _expand(bim_ref[d, 0])
            sre[...] = _dotf(ub, bre_v)
            sim[...] = _dotf(ub, bim_v)
            end_c, _ = _seg_scan(sre, sim, lam8, pw_c, 0, s_ctx, rev, _zero_state(), fre, fim, ic_re, ic_im)
            _seg_scan(sre, sim, lam8, pw_l, LC, s_lat, rev, end_c, fre, fim, il_re, il_im)
            gre[...] = _dotf(dyb, _expand(cre_ref[d, 0]))
            gim[...] = -_dotf(dyb, _expand(cim_ref[d, 0]))
            end_g, acc_l = _seg_scan(gre, gim, cam8, cw_l, LC, s_lat, not rev, _zero_state(), fre, fim, jre, jim,
                                     prev=(sre, sim, il_re, il_im))
            _, acc_c = _seg_scan(gre, gim, cam8, cw_c, 0, s_ctx, not rev, end_g, fre, fim, jre, jim,
                                 prev=(sre, sim, ic_re, ic_im))
            dlr_ref[d, 0] = _sum0(acc_l[0] + acc_c[0])
            dli_ref[d, 0] = _sum0(acc_l[1] + acc_c[1])
            grb, gib = gre[...].astype(BF16), gim[...].astype(BF16)
            du = _dotf(grb, bre_v, "nt") + _dotf(gib, bim_v, "nt")
            if d == 0:
                dus[...] = du
            else:
                dus[...] += du
            dbre_ref[d, 0] = _collapse(_dotf(ub, grb, "tn"))
            dbim_ref[d, 0] = _collapse(_dotf(ub, gib, "tn"))
            dcre_ref[d, 0] = _collapse(_dotf(dyb, sre[...].astype(BF16), "tn"))
            dcim_ref[d, 0] = -_collapse(_dotf(dyb, sim[...].astype(BF16), "tn"))
        _from_seg_order(dus, du_ref, T)

    ublk, lam, mat = _scan_specs(T)
    lam_s = jax.ShapeDtypeStruct(lam_re.shape, F32)
    mat_s = jax.ShapeDtypeStruct(bre.shape, F32)
    return pl.pallas_call(
        body, grid=(NJ,), in_specs=[ublk, ublk, lam, lam, mat, mat, mat, mat],
        out_specs=[ublk, lam, lam, mat, mat, mat, mat],
        out_shape=[jax.ShapeDtypeStruct((T, G * CH), F32), lam_s, lam_s, mat_s, mat_s, mat_s, mat_s],
        scratch_shapes=[pltpu.VMEM((T, UB), F32)] * 3 + [pltpu.VMEM((T, SB), F32)] * 4 + [pltpu.VMEM((SEG, SB), F32)] * 8,
        compiler_params=_cp(("arbitrary",)), name=name)(u, dy, lam_re, lam_im, bre, bim, cre, cim)


class Exchange:
    def __init__(self, xs, modes):
        self.n = len(xs)
        self.modes = [modes] * self.n if isinstance(modes, (str, int)) else list(modes)
        self.out_shape = [jax.ShapeDtypeStruct(self._shape(x, md), x.dtype) for x, md in zip(xs, self.modes)]
        self.scratch = [pltpu.SemaphoreType.DMA((NDEV - 1, self.n)), pltpu.SemaphoreType.DMA((NDEV - 1, self.n)),
                        pltpu.SemaphoreType.DMA((self.n,))]
        self.specs = [pl.BlockSpec(memory_space=pl.ANY)] * self.n

    @staticmethod
    def _shape(x, mode):
        if mode == "gather":
            return (NDEV,) + tuple(x.shape)
        return tuple(x.shape) if mode == "lead" else (NDEV, x.shape[0], mode) + tuple(x.shape[2:])

    @staticmethod
    def _piece(x_ref, mode, dev):
        if mode == "gather":
            return x_ref
        return x_ref.at[dev] if mode == "lead" else x_ref.at[:, pl.ds(dev * mode, mode)]

    def _copies(self, x_refs, out_refs, sems):
        send_sems, recv_sems, local_sems = sems
        mx, my, mc = lax.axis_index("x"), lax.axis_index("y"), lax.axis_index("c")
        me = 4 * mx + 2 * my + mc
        peer_of = lambda k: (1 - mx if k & 4 else mx, 1 - my if k & 2 else my, 1 - mc if k & 1 else mc)
        local, first, relay, arrivals = [], [], [], []
        for a, (x_ref, out_ref) in enumerate(zip(x_refs, out_refs)):
            mode = self.modes[a]
            local.append(pltpu.make_async_copy(self._piece(x_ref, mode, me), out_ref.at[me], local_sems.at[a]))

            def remote(src, dst, k, pair, a=a):
                return pltpu.make_async_remote_copy(src_ref=src, dst_ref=dst, send_sem=send_sems.at[pair, a],
                                                    recv_sem=recv_sems.at[pair, a], device_id=peer_of(k), device_id_type=MESH_T)

            for k in range(1, NDEV):
                peer = peer_of(k)
                pid = 4 * peer[0] + 2 * peer[1] + peer[2]
                if mode != "gather":
                    src = self._piece(x_ref, mode, pid)
                    first.append(remote(src, out_ref.at[me], k, k - 1))
                    arrivals.append(remote(src, out_ref.at[pid], k, k - 1))
                elif k == 1:
                    first.append(remote(x_ref, out_ref.at[me], k, k - 1))
                    arrivals.append(remote(x_ref, out_ref.at[pid], k, k - 1))
                elif k % 2 == 0:
                    first.append(remote(x_ref, out_ref.at[me], k, k - 1))
                    relay.append((remote(x_ref, out_ref.at[pid], k, k - 1), remote(out_ref.at[pid], out_ref.at[pid], 1, k)))
                else:
                    arrivals.append(remote(x_ref, out_ref.at[pid], 1, k - 1))
        return local, first, relay, arrivals

    def start(self, x_refs, out_refs, sems):
        local, first, _, _ = self._copies(x_refs, out_refs, sems)
        for cp in local + first:
            cp.start()

    def finish(self, x_refs, out_refs, sems):
        local, first, relay, arrivals = self._copies(x_refs, out_refs, sems)
        for arrival, onward in relay:
            arrival.wait_recv()
            onward.start()
        for cp in arrivals:
            cp.wait_recv()
        for cp in first + [onward for _, onward in relay]:
            cp.wait_send()
        for cp in local:
            cp.wait()


def exchange(xs, modes, name):
    ex = Exchange(xs, modes)
    n = ex.n

    def body(*refs):
        ex.start(refs[:n], refs[n:2 * n], refs[2 * n:])
        ex.finish(refs[:n], refs[n:2 * n], refs[2 * n:])

    return pl.pallas_call(body, in_specs=ex.specs, out_specs=ex.specs, out_shape=ex.out_shape, scratch_shapes=ex.scratch,
                          compiler_params=pltpu.CompilerParams(has_side_effects=True), name=name)(*xs)


def _dot_f32(a, b, dn):
    return lax.dot_general(a, b, dn, preferred_element_type=F32, precision=lax.Precision.HIGHEST)


def ada_fwd(cg, c_ctx, ada_w, ada_b_loc, name):
    W = ada_w.shape[2]

    def body(cg_ref, cc_ref, w_ref, b_ref, o_ref):
        a = jnp.concatenate([_silu(cg_ref[...]), jnp.broadcast_to(_silu(cc_ref[...]), (NDEV, D))], axis=0)
        for i in range(2):
            o_ref[i] = _dot_f32(a, w_ref[i], _DN["nn"]) + b_ref[i]

    return pl.pallas_call(body, out_shape=jax.ShapeDtypeStruct((2, 2 * NDEV, W), F32),
                          compiler_params=_cp(), name=name)(cg, c_ctx, ada_w, ada_b_loc)


def ada_bwd(cg, c_ctx, ada_w, dm_loc, dm_all, name):
    W = ada_w.shape[2]

    def body(cg_ref, cc_ref, w_ref, dl_ref, da_ref, gw_ref, dcc_ref, gb_ref):
        a = jnp.concatenate([_silu(cg_ref[...]), jnp.broadcast_to(_silu(cc_ref[...]), (NDEV, D))], axis=0)
        dcc = jnp.zeros((1, D), F32)
        for i in range(2):
            dl = dl_ref[i]
            gw_ref[i] = _dot_f32(a, dl, _DN["tn"])
            dctx = jnp.sum(dl[NDEV:], axis=0, keepdims=True)
            dcc = dcc + _dot_f32(dctx, w_ref[i], _DN["nt"])
        dcc_ref[...] = dcc
        gb_ref[...] = jnp.sum(da_ref[...], axis=0)

    return pl.pallas_call(body, out_shape=[jax.ShapeDtypeStruct((2, D, W), F32), jax.ShapeDtypeStruct((1, D), F32),
                                           jax.ShapeDtypeStruct((2, 3 * D), F32)],
                          compiler_params=_cp(), name=name)(cg, c_ctx, ada_w, dm_loc, dm_all)


def cctx_finish(parts, c_ctx, name):
    def body(p_ref, cc_ref, o_ref):
        o_ref[...] = jnp.sum(p_ref[...], axis=0, keepdims=True) * _dsilu(cc_ref[...])

    return pl.pallas_call(body, out_shape=jax.ShapeDtypeStruct((1, D), F32), name=name)(parts, c_ctx)


def _adamw_update(g_ref, w_ref, m_ref, v_ref, go_ref, d_ref, mo_ref, vo_ref):
    g = g_ref[0].astype(F32)
    for s in range(1, g_ref.shape[0]):
        g = g + g_ref[s].astype(F32)
    mn = B1 * m_ref[...] + (1.0 - B1) * g
    vn = B2 * v_ref[...] + (1.0 - B2) * g * g
    go_ref[...] = g
    mo_ref[...] = mn
    vo_ref[...] = vn
    d_ref[...] = -LR * ((mn * (1.0 / (1.0 - B1 ** STEP))) / (jnp.sqrt(vn * (1.0 / (1.0 - B2 ** STEP))) + AEPS) + WD * w_ref[...])


ADAMW_PARTS = 4


def adamw_rows(items, name, rode=None, modes=None):
    in_specs, out_specs, out_shape, args = [], [], [], []
    for g, w, m, v in items:
        n, R, C = g.shape
        tr = R // ADAMW_PARTS
        spec = pl.BlockSpec((tr, C), lambda i, j: (i, 0))
        in_specs += [pl.BlockSpec((n, tr, C), lambda i, j: (0, i, 0)), spec, spec, spec]
        args += [g, w, m, v]
    for g, w, m, v in items:
        tr = w.shape[0] // ADAMW_PARTS
        out_specs += [pl.BlockSpec((tr, w.shape[1]), lambda i, j: (i, 0))] * 4
        out_shape += [jax.ShapeDtypeStruct(w.shape, F32)] * 4
    res, got = _ride_call(_adamw_body(len(items)), (ADAMW_PARTS, 1), in_specs, out_specs, out_shape,
                          Exchange(rode, modes) if rode else None, rode, name, args)
    return [res[4 * t:4 * t + 4] for t in range(len(items))], got


def _adamw_body(k):
    def body(*refs):
        for t in range(k):
            _adamw_update(*refs[4 * t:4 * t + 4], *refs[4 * k + 4 * t:4 * k + 4 * t + 4])
    return body


def adamw_multi(items, grid, name):
    k = len(items)
    ins, in_specs, out_specs, out_shape = [], [], [], []
    for g, g_spec, w, m, v, w_spec in items:
        ins += [g, w, m, v]
        in_specs += [g_spec, w_spec, w_spec, w_spec]
    for g, g_spec, w, m, v, w_spec in items:
        out_specs += [w_spec] * 4
        out_shape += [jax.ShapeDtypeStruct(w.shape, F32)] * 4
    res = pl.pallas_call(_adamw_body(k), grid=grid, in_specs=in_specs, out_specs=out_specs, out_shape=out_shape,
                         compiler_params=_cp(("arbitrary",) * len(grid)), name=name)(*ins)
    return [res[4 * t:4 * t + 4] for t in range(k)]


def _whole(a, grid_rank):
    zeros = (0,) * a.ndim
    return pl.BlockSpec(a.shape, lambda *idx: zeros)


def sum_slots(xs, name):
    def body(*refs):
        for x_ref, o_ref in zip(refs[:len(xs)], refs[len(xs):]):
            acc = x_ref[0]
            for s in range(1, NDEV):
                acc = acc + x_ref[s]
            o_ref[...] = acc

    return pl.pallas_call(body, out_shape=[jax.ShapeDtypeStruct(x.shape[1:], F32) for x in xs],
                          compiler_params=_cp(), name=name)(*xs)


def _col_shards(g):
    R, N = g.shape
    return g.reshape(R, NDEV, N // NDEV).transpose(1, 0, 2)


def _vec2(v):
    return jnp.broadcast_to(v.reshape(1, 1, -1), (2, 1, v.size))


SHARD_ROWS = {"mla_w_in": 192, "mla_w_uq": 192, "mla_w_ukv": 256, "s5_w_in": 256}


def _t_shard(wsh, rows):
    t = wsh[0].T.astype(BF16)
    return jnp.pad(t, ((0, rows - t.shape[0]), (0, 0)))


def _win_order():
    w = IN_W // NDEV
    perm = np.zeros((IN_WP, NDEV * SHARD_ROWS["mla_w_in"]), np.float32)
    first = QL + KVL + ROPE
    for c in range(IN_W):
        n = c + HEADS * VD if c < first else c - first
        perm[n, (c // w) * SHARD_ROWS["mla_w_in"] + c % w] = 1.0
    return jnp.asarray(perm, BF16)


def local_step(ctx, x, tgt, mod, Wt, small, l1_shards):
    T = LC + x.shape[0]
    xa = ("cat", ctx, x)
    sh = [mod[i, :, None, 0:D] for i in range(2)]
    sc = [mod[i, :, None, D:2 * D] for i in range(2)]
    gt = [mod[i, :, None, 2 * D:] for i in range(2)]
    ng = [_vec2(small["norm_g"][i]) for i in range(2)]
    qg, kvg = _vec2(small["mla_q_norm"]), _vec2(small["mla_kv_norm"])
    cosf, sinf, _, pmt = _rope_tables(T)

    (h0, p0, cqn, ckvn), _ = rowwise(st_l0_pre, [xa], [ng[0], sc[0], sh[0], qg, kvg],
                                     [(D, BF16), (IN_WP, F32), (QL, BF16), (KVL, BF16)], [], "l0_pre", mats=[Wt["mla_w_in"]])
    z0, cq, ckv = (p0, 0, HEADS * VD), (p0, HEADS * VD // QL, QL), (p0, (HEADS * VD + QL) // KVL, KVL)
    Q = project_q(cqn, Wt["mla_w_uq"], "l0_uq")
    K, V = project_kv(ckvn, Wt["mla_w_ukv"], p0, (HEADS * VD + QL + KVL) // 128, "l0_ukv")
    (o, lse), got = attn_fwd(Q, K, V, "l0_attn", rode=l1_shards, modes="gather")
    Wt, small = dict(Wt), dict(small)
    for n, a in zip(L1_BIG, got):
        Wt[n] = a.reshape(-1, a.shape[-1])
    vecs = lax.bitcast_convert_type(got[-1].reshape(NDEV, 2, -1, 2), F32)
    small["s5_d"], small["s5_b_glu"] = vecs[:, 0, :].reshape(D), vecs[:, 1, :].reshape(D)
    o2 = o.transpose(1, 0, 2).reshape(T, HEADS * VD)
    (og, out0, x1), _ = rowwise(st_l0_post, [o2, z0, xa], [gt[0]], [(D, BF16), (D, BF16), (D, F32)], [], "l0_post",
                                mats=[Wt["mla_w_out"]])

    ls = small["s5_log_step"].reshape(2, G, 1)
    a_re, a_im = small["s5_a_re"].reshape(2, G, P), small["s5_a_im"].reshape(2, G, P)
    b_re = small["s5_b_re"].reshape(2, G, P, CH).transpose(0, 1, 3, 2)
    b_im = small["s5_b_im"].reshape(2, G, P, CH).transpose(0, 1, 3, 2)
    lam_re, lam_im, f_re, f_im = disc_fwd(a_re, a_im, ls, "s5_disc")
    f_re2, f_im2 = f_re.reshape(2, G, 1, P), f_im.reshape(2, G, 1, P)
    bb_re, bb_im = disc_b(f_re2, f_im2, b_re, b_im, "s5_disc_b")
    compact = lambda m: m.reshape(2, NJ, UB, P)
    bre, bim = compact(bb_re), compact(bb_im)
    cre, cim = compact(small["s5_c_re"]), compact(small["s5_c_im"])
    lam_re4, lam_im4 = lam_re.reshape(2, NJ, 1, SB), lam_im.reshape(2, NJ, 1, SB)

    (h1, p1), _ = rowwise(st_l1_pre, [x1], [ng[1], sc[1], sh[1]], [(D, BF16), (2 * D, F32)], [], "l1_pre", mats=[Wt["s5_w_in"]])
    u, z1 = (p1, 0, D), (p1, 1, D)
    yssm = scan_fwd(p1, lam_re4, lam_im4, bre, bim, cre, cim, "s5_scan")
    dvec, bglu = _vec2(small["s5_d"]), _vec2(small["s5_b_glu"])
    fg = _vec2(small["final_g"])
    lat_mask = jnp.stack([jnp.zeros((1, D), F32), jnp.ones((1, D), F32)])
    (y, y1b, gl, y3, out1, dx2), (dfg, lvec) = rowwise(
        st_l1_mlp, [yssm, u, z1, x1, ("lat", tgt)], [dvec, bglu, gt[1], fg, lat_mask],
        [(D, F32), (D, BF16), (D, BF16), (D, BF16), (D, BF16), (D, F32)], [D, 128], "l1_mlp",
        mats=[Wt["s5_w_glu"], Wt["s5_w_out"]])

    (dz1, dy, du_d), (dgt1, dbglu, dd), (g_w_out5, g_w_glu) = rowwise(
        st_l1_mlp_bwd, [dx2, out1, y3, y, gl, z1, u, y1b], [gt[1], bglu, dvec], [(D, BF16), (D, F32), (D, F32)], [D, D, D],
        "l1_mlp_b", mats=[Wt["s5_w_out"], Wt["s5_w_glu"]], out_accs=[(D, D), (D, D)])
    du_s, dlr, dli, dbre, dbim, dcre, dcim = scan_bwd(p1, dy, lam_re4, lam_im4, bre, bim, cre, cim, "s5_scan_b")
    dbb_re, dbb_im = dbre.reshape(2, G, CH, P), dbim.reshape(2, G, CH, P)
    g_c_re, g_c_im = dcre.reshape(2, G, CH, P), dcim.reshape(2, G, CH, P)
    gt_b_re, gt_b_im, dfr, dfi = disc_b_bwd(f_re2, f_im2, b_re, b_im, dbb_re, dbb_im, "s5_disc_b_b")
    g_b_re, g_b_im = gt_b_re.transpose(0, 1, 3, 2), gt_b_im.transpose(0, 1, 3, 2)
    g_a_re, g_a_im, g_ls = disc_a_bwd(a_re, a_im, ls, dlr.reshape(2, G, P), dli.reshape(2, G, P),
                                      dfr.reshape(2, G, P), dfi.reshape(2, G, P), "s5_disc_b_a")
    (dx1,), (dsh1, dsc1, dng1), (g_w_in5,) = rowwise(
        st_l1_tail_bwd, [du_d, du_s, dz1, h1, x1, dx2], [ng[1], sc[1]], [(D, F32)], [D, D, D], "l1_pre_b",
        mats=[Wt["s5_w_in"]], out_accs=[(NDEV, D, 2 * D // NDEV)])

    (do2, dz0), (dgt0,), (g_w_out,) = rowwise(st_l0_post_bwd, [dx1, out0, og, o2, z0], [gt[0]], [(D, F32), (D, F32)], [D],
                                              "l0_post_b", mats=[Wt["mla_w_out"]], out_accs=[(D, D)])
    doh = do2.reshape(T, HEADS, VD).transpose(1, 0, 2)
    rows8 = lambda g: g.reshape(NDEV, -1, g.shape[-1])
    both = lambda s: s[0, 0] + s[1, 0]
    dense = lambda g: g.reshape(2, G * P * CH // 128, 128)
    chunks = [dense(g_b_re), dense(g_b_im), g_c_re, g_c_im]
    l1_send = [g_w_in5, rows8(g_w_glu), rows8(g_w_out5), rows8(g_w_out),
               both(dd).reshape(NDEV, 1, -1), both(dbglu).reshape(NDEV, 1, -1)]
    (dQ, dK, dV), l1_recv = attn_bwd(Q, K, V, o, lse, doh, "l0_attn_b", rode=l1_send + chunks,
                                     modes=["lead"] * len(l1_send) + [a.shape[1] // NDEV for a in chunks])
    dq = rope_bwd(dQ, cosf, sinf, pmt, SCALE, "l0_rope_q_b")
    n_owned = len(l1_send)
    reduced = sum_slots(l1_recv[n_owned:], "sum_chunks")
    (dkv, dkr), chunk_all = split_kv_grads(dK, dV, "l0_kv_b", rode=[jnp.stack(reduced[:2]), jnp.stack(reduced[2:])],
                                           modes="gather")
    (grad_x,), (dqg, dkvg, dsh0, dsc0, dng0), (g_uq, g_ukv, g_p) = rowwise(
        st_l0_tail_bwd, [dq, dkv, dkr, dz0, cq, ckv, cqn, ckvn, h0, xa, dx1], [qg, kvg, ng[0], sc[0]],
        [(D, F32, "lat")], [QL, KVL, D, D, D], "l0_pre_b", mats=[Wt["mla_w_uq"], Wt["mla_w_ukv"], Wt["mla_w_in"]],
        out_accs=[(QL, HEADS * QK), (KVL, HEADS * KVW), (D, IN_WP)])
    g_w_uq, g_w_ukv = _col_shards(g_uq).astype(BF16), _col_shards(g_ukv).astype(BF16)
    g_w_in = _col_shards(jnp.concatenate([g_p[:, HEADS * VD:IN_W], g_p[:, :HEADS * VD]], axis=1)).astype(BF16)

    dmod = jnp.stack([jnp.concatenate([dsh0, dsc0, dgt0], axis=-1)[:, 0], jnp.concatenate([dsh1, dsc1, dgt1], axis=-1)[:, 0]])
    gbig = {"mla_w_in": g_w_in, "mla_w_uq": g_w_uq, "mla_w_ukv": g_w_ukv}
    gsmall = {"norm_g": jnp.stack([both(dng0), both(dng1)]), "mla_q_norm": both(dqg), "mla_kv_norm": both(dkvg),
              "s5_a_re": g_a_re, "s5_a_im": g_a_im, "s5_log_step": g_ls, "final_g": dfg[1, 0]}
    return lvec[1], grad_x, dmod, gbig, gsmall, l1_recv[:n_owned], chunk_all


COL_SHARDED = ("mla_w_in", "mla_w_uq", "mla_w_ukv", "s5_w_in")
ROW_SHARDED = ("mla_w_out", "s5_w_glu", "s5_w_out")
VEC_SHARDED = ("s5_d", "s5_b_glu")
BIG = COL_SHARDED + ROW_SHARDED
L0_BIG = ("mla_w_in", "mla_w_uq", "mla_w_ukv")
L1_BIG = ("s5_w_in", "s5_w_glu", "s5_w_out", "mla_w_out")
BITS16 = jnp.bfloat16
SMALL_RS = ("norm_g", "mla_q_norm", "mla_kv_norm", "s5_a_re", "s5_a_im", "s5_log_step", "s5_b_re", "s5_b_im",
            "s5_c_re", "s5_c_im", "final_g")
CHUNKED = ("s5_b_re", "s5_b_im", "s5_c_re", "s5_c_im")
DENSE = ("s5_b_re", "s5_b_im")
TINY = ("norm_g", "mla_q_norm", "mla_kv_norm", "s5_a_re", "s5_a_im", "s5_log_step", "final_g")
ORDER = ("c_ctx", "ada_w", "ada_b", "norm_g", "mla_w_in", "mla_q_norm", "mla_w_uq", "mla_kv_norm", "mla_w_ukv",
         "mla_w_out", "s5_w_in", "s5_a_re", "s5_a_im", "s5_log_step", "s5_b_re", "s5_b_im", "s5_c_re", "s5_c_im",
         "s5_d", "s5_w_glu", "s5_b_glu", "s5_w_out", "final_g")


def kernel(x, c, ctx, c_ctx, ada_w, ada_b, norm_g, mla_w_in, mla_q_norm, mla_w_uq, mla_kv_norm, mla_w_ukv, mla_w_out, s5_w_in, s5_a_re, s5_a_im, s5_log_step, s5_b_re, s5_b_im, s5_c_re, s5_c_im, s5_d, s5_w_glu, s5_b_glu, s5_w_out, final_g, loss_target, m_c_ctx, m_ada_w, m_ada_b, m_norm_g, m_mla_w_in, m_mla_q_norm, m_mla_w_uq, m_mla_kv_norm, m_mla_w_ukv, m_mla_w_out, m_s5_w_in, m_s5_a_re, m_s5_a_im, m_s5_log_step, m_s5_b_re, m_s5_b_im, m_s5_c_re, m_s5_c_im, m_s5_d, m_s5_w_glu, m_s5_b_glu, m_s5_w_out, m_final_g, v_c_ctx, v_ada_w, v_ada_b, v_norm_g, v_mla_w_in, v_mla_q_norm, v_mla_w_uq, v_mla_kv_norm, v_mla_w_ukv, v_mla_w_out, v_s5_w_in, v_s5_a_re, v_s5_a_im, v_s5_log_step, v_s5_b_re, v_s5_b_im, v_s5_c_re, v_s5_c_im, v_s5_d, v_s5_w_glu, v_s5_b_glu, v_s5_w_out, v_final_g):
    w = dict(c_ctx=c_ctx, ada_w=ada_w, ada_b=ada_b, norm_g=norm_g, mla_w_in=mla_w_in, mla_q_norm=mla_q_norm,
             mla_w_uq=mla_w_uq, mla_kv_norm=mla_kv_norm, mla_w_ukv=mla_w_ukv, mla_w_out=mla_w_out, s5_w_in=s5_w_in,
             s5_a_re=s5_a_re, s5_a_im=s5_a_im, s5_log_step=s5_log_step, s5_b_re=s5_b_re, s5_b_im=s5_b_im,
             s5_c_re=s5_c_re, s5_c_im=s5_c_im, s5_d=s5_d, s5_w_glu=s5_w_glu, s5_b_glu=s5_b_glu, s5_w_out=s5_w_out,
             final_g=final_g)
    m = dict(c_ctx=m_c_ctx, ada_w=m_ada_w, ada_b=m_ada_b, norm_g=m_norm_g, mla_w_in=m_mla_w_in, mla_q_norm=m_mla_q_norm,
             mla_w_uq=m_mla_w_uq, mla_kv_norm=m_mla_kv_norm, mla_w_ukv=m_mla_w_ukv, mla_w_out=m_mla_w_out,
             s5_w_in=m_s5_w_in, s5_a_re=m_s5_a_re, s5_a_im=m_s5_a_im, s5_log_step=m_s5_log_step, s5_b_re=m_s5_b_re,
             s5_b_im=m_s5_b_im, s5_c_re=m_s5_c_re, s5_c_im=m_s5_c_im, s5_d=m_s5_d, s5_w_glu=m_s5_w_glu,
             s5_b_glu=m_s5_b_glu, s5_w_out=m_s5_w_out, final_g=m_final_g)
    v = dict(c_ctx=v_c_ctx, ada_w=v_ada_w, ada_b=v_ada_b, norm_g=v_norm_g, mla_w_in=v_mla_w_in, mla_q_norm=v_mla_q_norm,
             mla_w_uq=v_mla_w_uq, mla_kv_norm=v_mla_kv_norm, mla_w_ukv=v_mla_w_ukv, mla_w_out=v_mla_w_out,
             s5_w_in=v_s5_w_in, s5_a_re=v_s5_a_re, s5_a_im=v_s5_a_im, s5_log_step=v_s5_log_step, s5_b_re=v_s5_b_re,
             s5_b_im=v_s5_b_im, s5_c_re=v_s5_c_re, s5_c_im=v_s5_c_im, s5_d=v_s5_d, s5_w_glu=v_s5_w_glu,
             s5_b_glu=v_s5_b_glu, s5_w_out=v_s5_w_out, final_g=v_final_g)

    me = 4 * lax.axis_index("x") + 2 * lax.axis_index("y") + lax.axis_index("c")
    WA = ada_w.shape[2]

    def shard(n):
        return _t_shard(w[n], SHARD_ROWS[n]) if n in COL_SHARDED else w[n][0].astype(BF16)

    wgot = exchange([c] + [shard(n) for n in L0_BIG], "gather", "gather_w")

    cg = wgot[0].reshape(NDEV, D)
    cc2 = c_ctx.reshape(1, D)
    ada_b_loc = lax.dynamic_slice_in_dim(ada_b.reshape(2, 3 * D // WA, WA), me, 1, axis=1)
    part = ada_fwd(cg, cc2, ada_w, ada_b_loc, "ada_fwd")
    pg = exchange([part], "gather", "gather_mod")[0]
    mod_l = lax.dynamic_index_in_dim(pg, me, axis=2, keepdims=False).transpose(1, 0, 2).reshape(2, 3 * D)
    mod_c = pg[:, :, NDEV, :].transpose(1, 0, 2).reshape(2, 3 * D)
    mod = jnp.stack([mod_c, mod_l], axis=1)

    Wt = {n: a.reshape(-1, a.shape[-1]) for n, a in zip(L0_BIG, wgot[1:])}
    Wt["mla_w_in"] = mm(_win_order(), Wt["mla_w_in"], "nn", "w_in_order", out_dtype=BF16)
    vec_bits = lax.bitcast_convert_type(jnp.concatenate([s5_d, s5_b_glu], axis=0), BITS16).reshape(2, -1)
    small = {n: w[n] for n in SMALL_RS}

    lvec, grad_x, dmod, gbig, gsmall, l1_recv, (bb_all, cc_all) = local_step(
        ctx[0], x[0], loss_target[0], mod, Wt, small, [shard(n) for n in L1_BIG] + [vec_bits])
    grad_x = grad_x[None]

    recv = dict(zip(L1_BIG + VEC_SHARDED, l1_recv))
    out = {}

    def keep(n, res):
        for key, arr in zip("gdmv", res):
            out[key, n] = arr.reshape(w[n].shape)

    kshape = lambda n: w[n].shape if w[n].ndim > 1 else (1, w[n].size)
    flat = jnp.concatenate([gsmall[n].reshape(-1) for n in TINY] + [dmod.reshape(-1), lvec.reshape(-1)])[None]
    *l0_recv, flat_all = exchange([gbig[n] for n in L0_BIG] + [flat], ["lead"] * len(L0_BIG) + ["gather"], "scatter_grads")
    chunk_all = [bb_all[:, 0], bb_all[:, 1], cc_all[:, 0], cc_all[:, 1]]
    tiny_all, off = [], 0
    for n in TINY:
        tiny_all.append(flat_all[:, 0, off:off + w[n].size].reshape((NDEV,) + kshape(n)))
        off += w[n].size
    dm_all = flat_all[:, 0, off:off + dmod.size].reshape((NDEV,) + dmod.shape)
    loss = sum_slots([flat_all[:, :, off + dmod.size:]], "loss_sum")[0][0, 0]

    dm_cols = lax.dynamic_slice_in_dim(dm_all.reshape(NDEV, 2, 2, 3 * D // WA, WA), me, 1, axis=3)[:, :, :, 0]
    dm_loc = jnp.concatenate([dm_cols[:, :, 1].transpose(1, 0, 2), dm_cols[:, :, 0].transpose(1, 0, 2)], axis=1)
    g_ada_w, dcc_part, g_ada_b = ada_bwd(cg, cc2, ada_w, dm_loc, dm_all.transpose(0, 2, 1, 3).reshape(2 * NDEV, 2, 3 * D), "ada_bwd")
    dcc_all = exchange([dcc_part], "gather", "gather_dcc")[0].reshape(NDEV, D)
    g_c_ctx = cctx_finish(dcc_all, cc2, "cctx_finish")

    flat2 = lambda t: t.reshape(-1, t.shape[-1])
    recv.update(dict(zip(L0_BIG, l0_recv)))
    big = [(recv[n], w[n][0], m[n][0], v[n][0]) for n in BIG]
    big.append((flat2(g_ada_w)[None], flat2(ada_w), flat2(m_ada_w), flat2(v_ada_w)))
    for n, r in zip(BIG + ("ada_w",), adamw_rows(big, "adamw_big")[0]):
        keep(n, r)
    items = []
    halves = 2
    for n, g in zip(CHUNKED, chunk_all):
        blk = (1, 1, G // halves) + w[n].shape[3:]
        g = jnp.moveaxis(g, 0, 1).reshape(w[n].shape)
        g_spec = pl.BlockSpec((1,) + blk, lambda d, s: (0, 0, d, s, 0, 0))
        items.append((g[None], g_spec, w[n], m[n], v[n], pl.BlockSpec(blk, lambda d, s: (0, d, s, 0, 0))))
    for n, res in zip(CHUNKED, adamw_multi(items, (2, halves), "adamw_bc")):
        keep(n, res)
    tiny_g = dict(zip(TINY, tiny_all))
    tiny_g.update({n: recv[n] for n in VEC_SHARDED})
    tiny_g["c_ctx"], tiny_g["ada_b"] = g_c_ctx[None], g_ada_b[None]
    names = list(tiny_g)
    items = [(tiny_g[n], _whole(tiny_g[n], 1)) + tuple(t[n].reshape(kshape(n)) for t in (w, m, v))
             + (pl.BlockSpec(kshape(n), lambda i, r=len(kshape(n)): (0,) * r),) for n in names]
    for n, res in zip(names, adamw_multi(items, (1,), "adamw_small")):
        keep(n, res)

    return (loss, grad_x, *[out["g", n] for n in ORDER], *[out["d", n] for n in ORDER],
            *[out["m", n] for n in ORDER], *[out["v", n] for n in ORDER])
```

```python
import math

import numpy as np
import jax
import jax.numpy as jnp
from jax import lax
from jax.experimental import pallas as pl
from jax.experimental.pallas import tpu as pltpu

F32 = jnp.float32
BF16 = jnp.bfloat16

D = 1024
L = 2048
LC = 256
NDEV = 8
GRID_W = 64
EPS = 1e-6
HEADS = 16
NOPE = 64
ROPE = 32
QK = NOPE + ROPE
VD = 64
IN_W = 256 + 128 + ROPE + HEADS * 64
IN_WP = 1536
QL = 256
KVL = 128
SCALE = QK ** -0.5
LOG2E = math.log2(math.e)
THETA = 10000.0
G = 64
P = 64
CH = 16
GB = 8
NJ = G // GB
UB = GB * CH
SB = GB * P
SEG = 16
TB = 256
VMEM_LIMIT = 56 * 1024 * 1024
B1, B2, LR, AEPS, WD, STEP = 0.9, 0.999, 0.001, 1e-8, 0.01, 10
MESH_T = pl.DeviceIdType.MESH


def _cp(sem=None):
    return pltpu.CompilerParams(dimension_semantics=sem, vmem_limit_bytes=VMEM_LIMIT)


def _sig(x):
    return 1.0 / (1.0 + jnp.exp(-x))


def _silu(x):
    return x * _sig(x)


def _dsilu(x):
    s = _sig(x)
    return s * (1.0 + x * (1.0 - s))


_GK = math.sqrt(2.0 / math.pi)


def _gelu(x):
    return 0.5 * x * (1.0 + jnp.tanh(_GK * (x + 0.044715 * x * x * x)))


def _dgelu(x):
    t = jnp.tanh(_GK * (x + 0.044715 * x * x * x))
    return 0.5 * (1.0 + t) + 0.5 * x * (1.0 - t * t) * _GK * (1.0 + 3 * 0.044715 * x * x)


def _rs(x):
    return lax.rsqrt(jnp.mean(x * x, axis=-1, keepdims=True) + EPS)


def _sum0(x):
    return jnp.sum(x, axis=0, keepdims=True)


def st_norm_mod(x, g, sc, sh):
    y = x * _rs(x) * g
    return (y * (1.0 + sc) + sh,), ()


def st_norm_mod_bwd(x, dh, dres, g, sc):
    r = _rs(x)
    xn = x * r
    y = xn * g
    dy = dh * (1.0 + sc)
    dxn = dy * g
    dx = r * (dxn - xn * jnp.mean(dxn * xn, axis=-1, keepdims=True))
    return (dres + dx,), (_sum0(dh), _sum0(dh * y), _sum0(dy * xn))


def st_rms(x, g):
    return (x * _rs(x) * g,), ()


def st_rms_bwd(x, dy, g):
    r = _rs(x)
    n = x * r
    dn = dy * g
    dx = r * (dn - n * jnp.mean(dn * n, axis=-1, keepdims=True))
    return (dx,), (_sum0(dy * n),)


def st_rms2(x1, x2, g1, g2):
    return st_rms(x1, g1)[0] + st_rms(x2, g2)[0], ()


def st_rms2_bwd(x1, dy1, x2, dy2, g1, g2):
    (d1,), (s1,) = st_rms_bwd(x1, dy1, g1)
    (d2,), (s2,) = st_rms_bwd(x2, dy2, g2)
    return (d1, d2), (s1, s2)


def st_gate_bwd(dog, o, z):
    return (dog * _silu(z), dog * o * _dsilu(z)), ()


def st_resid_bwd(dx, out, gt):
    return (dx * gt,), (_sum0(dx * out),)


def st_s5a(yssm, u, d):
    y = yssm + d * u
    return (y, _gelu(y)), ()


def st_s5b_bwd(dy3, y, gl, z, b):
    y1 = _gelu(y)
    s = _sig(gl + b)
    dy2 = dy3 * _silu(z)
    dz = dy3 * y1 * s * _dsilu(z)
    dgl = dy2 * y1 * s * (1.0 - s)
    return (dgl, dz, dy2 * s), (_sum0(dgl),)


def st_s5a_bwd(dy1a, dy1b, y, u, d):
    dy = (dy1a + dy1b) * _dgelu(y)
    return (dy, dy * d), (_sum0(dy * u),)


def st_l0_pre(x, g, sc, sh, qg, kvg, w_in):
    hb = st_norm_mod(x, g, sc, sh)[0][0].astype(BF16)
    p = lax.dot_general(hb, w_in, _DN["nt"], preferred_element_type=F32)
    cq, ckv = p[:, HEADS * VD:HEADS * VD + QL], p[:, HEADS * VD + QL:HEADS * VD + QL + KVL]
    return (hb, p) + st_rms2(cq, ckv, qg, kvg)[0], ()


def st_l0_tail_bwd(dq, dkv, dkr, dz, cq, ckv, cqn, ckvn, h, x, dres, qg, kvg, g, sc, w_uq, w_ukv, w_in):
    dcqn = jnp.dot(dq, w_uq, preferred_element_type=F32)
    dckvn = jnp.dot(dkv, w_ukv, preferred_element_type=F32)
    (dcq, dckv), (dqg, dkvg) = st_rms2_bwd(cq, dcqn, ckv, dckvn, qg, kvg)
    dp = jnp.concatenate([dz, dcq, dckv, dkr], axis=1).astype(BF16)
    dh = jnp.dot(dp, w_in, preferred_element_type=F32)
    outs, sums = st_norm_mod_bwd(x, dh, dres, g, sc)
    tn = lambda a, b: lax.dot_general(a, b, _DN["tn"], preferred_element_type=F32)
    return outs, (dqg, dkvg) + sums, (tn(cqn, dq), tn(ckvn, dkv), tn(h, dp))


def st_l1_pre(x, g, sc, sh, w_in):
    hb = st_norm_mod(x, g, sc, sh)[0][0].astype(BF16)
    return (hb, lax.dot_general(hb, w_in, _DN["nt"], preferred_element_type=F32)), ()


def st_l1_tail_bwd(du_a, du_b, dz, h, x, dres, g, sc, w_in):
    dp = jnp.concatenate([(du_a + du_b).astype(BF16), dz], axis=1)
    dh = jnp.dot(dp, w_in, preferred_element_type=F32)
    outs, sums = st_norm_mod_bwd(x, dh, dres, g, sc)
    w = dp.shape[1] // NDEV
    shards = [lax.dot_general(h, dp[:, r * w:(r + 1) * w], _DN["tn"], preferred_element_type=F32) for r in range(NDEV)]
    return outs, sums, (jnp.stack(shards),)


def st_l0_post(o, z, x, gt, w_out):
    og = (o * _silu(z)).astype(BF16)
    out = jnp.dot(og, w_out, preferred_element_type=F32)
    return (og, out, x + gt * out), ()


def st_l0_post_bwd(dx1, out, og, o, z, gt, w_out):
    (dout,), (dgt,) = st_resid_bwd(dx1, out.astype(F32), gt)
    doutb = dout.astype(BF16)
    dog = lax.dot_general(doutb, w_out, _DN["nt"], preferred_element_type=F32)
    return st_gate_bwd(dog, o, z)[0], (dgt,), (lax.dot_general(og, doutb, _DN["tn"], preferred_element_type=F32),)


def st_l1_mlp(yssm, u, z, x1, tgt, d, bglu, gt, fg, mask, w_glu, w_out):
    (y, y1), _ = st_s5a(yssm, u, d)
    y1b = y1.astype(BF16)
    gl = jnp.dot(y1b, w_glu, preferred_element_type=F32)
    y3 = (y1 * _sig(gl + bglu) * _silu(z)).astype(BF16)
    out = jnp.dot(y3, w_out, preferred_element_type=F32)
    (dx2,), sums = st_final(x1 + gt * out, tgt, fg, mask)
    return (y, y1b, gl, y3, out, dx2), sums


def st_l1_mlp_bwd(dx2, out, y3, y, gl, z, u, y1b, gt, bglu, d, w_out, w_glu):
    out, gl = out.astype(F32), gl.astype(F32)
    (dout,), (dgt,) = st_resid_bwd(dx2, out, gt)
    doutb = dout.astype(BF16)
    dy3 = lax.dot_general(doutb, w_out, _DN["nt"], preferred_element_type=F32)
    (dgl, dz, dy1a), (dbglu,) = st_s5b_bwd(dy3, y, gl, z, bglu)
    dglb = dgl.astype(BF16)
    dy1b = lax.dot_general(dglb, w_glu, _DN["nt"], preferred_element_type=F32)
    (dy, du), (dd,) = st_s5a_bwd(dy1a, dy1b, y, u, d)
    g_w_out = lax.dot_general(y3, doutb, _DN["tn"], preferred_element_type=F32)
    g_w_glu = lax.dot_general(y1b, dglb, _DN["tn"], preferred_element_type=F32)
    return (dz, dy, du), (dgt, dbglu, dd), (g_w_out, g_w_glu)


def st_final(x2, tgt, g, mask):
    r = _rs(x2)
    n = x2 * r
    e = n * g - tgt
    dyo = e * (1.0 / D)
    dn = dyo * g
    dx = r * (dn - n * jnp.mean(dn * n, axis=-1, keepdims=True))
    lsum = jnp.sum(_sum0(e * e), axis=1, keepdims=True) * (0.5 / D)
    return (dx * mask,), (_sum0(dyo * n), jnp.broadcast_to(lsum, (1, 128)))


def rowwise(fn, rows, vecs, out_rows, out_sums, name, mats=(), out_accs=()):
    lat_blk = lambda i: jnp.maximum(i - 1, 0)
    arrays, in_specs, pick = [], [], []
    for a in rows:
        if not isinstance(a, tuple):
            a = (a, 0, a.shape[1])
        tag = a[0] if isinstance(a[0], str) else None
        if tag == "cat":
            _, ctx, x = a
            arrays += [ctx, x]
            in_specs += [pl.BlockSpec((TB, ctx.shape[1]), lambda i: (0, 0)),
                         pl.BlockSpec((TB, x.shape[1]), lambda i: (lat_blk(i), 0))]
            pick.append(2)
        elif tag == "lat":
            arrays.append(a[1])
            in_specs.append(pl.BlockSpec((TB, a[1].shape[1]), lambda i: (lat_blk(i), 0)))
            pick.append(1)
        else:
            arr, cb, width = a
            arrays.append(arr)
            in_specs.append(pl.BlockSpec((TB, width), lambda i, cb=cb: (i, cb)))
            pick.append(1)
    T = LC + L
    nin, nv, nm, no, ns = len(arrays), len(vecs), len(mats), len(out_rows), len(out_sums)

    def body(*refs):
        i = pl.program_id(0)
        vals, k = [], 0
        for p in pick:
            if p == 2:
                vals.append(jnp.where(i == 0, refs[k][...], refs[k + 1][...]))
            else:
                vals.append(refs[k][...])
            k += p
        vals += [r[0] for r in refs[nin:nin + nv]] + [r[...] for r in refs[nin + nv:nin + nv + nm]]
        res = fn(*vals)
        first_out = nin + nv + nm
        for r, o in zip(refs[first_out:first_out + no], res[0]):
            r[...] = o.astype(r.dtype)
        sum_refs = refs[first_out + no:first_out + no + ns]
        if sum_refs:
            @pl.when(i <= 1)
            def _():
                for r in sum_refs:
                    r[...] = jnp.zeros_like(r)
            for r, s in zip(sum_refs, res[1]):
                r[0] += s
        na = len(out_accs)
        if na:
            acc_out, acc = refs[first_out + no + ns:first_out + no + ns + na], refs[first_out + no + ns + na:]

            @pl.when(i == 0)
            def _():
                for r in acc:
                    r[...] = jnp.zeros_like(r)
            for r, a in zip(acc, res[2]):
                r[...] += a

            @pl.when(i == T // TB - 1)
            def _():
                for o, r in zip(acc_out, acc):
                    o[...] = r[...].astype(o.dtype)

    kind = lambda i: (jnp.minimum(i, 1), 0, 0)
    in_specs += [pl.BlockSpec((1, 1, v.shape[2]), kind) for v in vecs]
    in_specs += [pl.BlockSpec(m.shape, lambda i: (0, 0), pipeline_mode=pl.Buffered(1)) for m in mats]
    out_specs, out_shape = [], []
    for o in out_rows:
        lat = len(o) == 3
        out_specs.append(pl.BlockSpec((TB, o[0]), (lambda i: (lat_blk(i), 0)) if lat else (lambda i: (i, 0))))
        out_shape.append(jax.ShapeDtypeStruct((L if lat else T, o[0]), o[1]))
    out_specs += [pl.BlockSpec((1, 1, c), kind) for c in out_sums]
    out_shape += [jax.ShapeDtypeStruct((2, 1, c), F32) for c in out_sums]
    out_specs += [pl.BlockSpec(s, lambda i, r=len(s): (0,) * r) for s in out_accs]
    out_shape += [jax.ShapeDtypeStruct(s, BF16) for s in out_accs]
    res = pl.pallas_call(body, grid=(T // TB,), in_specs=in_specs, out_specs=out_specs, out_shape=out_shape,
                         scratch_shapes=[pltpu.VMEM(s, F32) for s in out_accs],
                         compiler_params=_cp(("arbitrary",)), name=name)(*arrays, *vecs, *mats)
    if out_accs:
        return res[:no], res[no:no + ns], res[no + ns:]
    return res[:no], res[no:]


_DN = {"nn": (((1,), (0,)), ((), ())), "nt": (((1,), (1,)), ((), ())), "tn": (((0,), (0,)), ((), ()))}


def mm(a, b, mode, name, out_dtype=F32, tm=None, tn=None):
    if mode == "nn":
        (M, K), (_, N) = a.shape, b.shape
    elif mode == "nt":
        (M, K), (N, _) = a.shape, b.shape
    else:
        (K, M), (_, N) = a.shape, b.shape
    if tm is None:
        tm = next((t for t in (768, 512, 256) if M % t == 0 and M > t), M)
    tn = N if tn is None else tn
    dn = _DN[mode]

    def body(a_ref, b_ref, o_ref):
        o_ref[...] = lax.dot_general(a_ref[...].astype(BF16), b_ref[...].astype(BF16), dn,
                                     preferred_element_type=F32).astype(o_ref.dtype)

    a_spec = pl.BlockSpec((K, tm), lambda i, j: (0, i)) if mode == "tn" else pl.BlockSpec((tm, K), lambda i, j: (i, 0))
    b_spec = pl.BlockSpec((tn, K), lambda i, j: (j, 0)) if mode == "nt" else pl.BlockSpec((K, tn), lambda i, j: (0, j))
    return pl.pallas_call(body, grid=(M // tm, N // tn), in_specs=[a_spec, b_spec],
                          out_specs=pl.BlockSpec((tm, tn), lambda i, j: (i, j)), out_shape=jax.ShapeDtypeStruct((M, N), out_dtype),
                          compiler_params=_cp(("parallel", "arbitrary")), name=name)(a, b)


def _rope_tables(T, width=QK, first=NOPE):
    nlat = T - LC
    pos = np.arange(nlat)
    row, col = pos // GRID_W, pos % GRID_W
    half = ROPE // 2
    inv = 1.0 / (THETA ** (np.arange(0, half, 2, dtype=np.float64) / half))
    cosf = np.ones((T, width), np.float64)
    sinf = np.zeros((T, width), np.float64)
    perm = np.zeros((width, width), np.float32)
    for m in range(ROPE):
        j = first + m
        blk, w = m // half, m % half
        ang = (row if blk == 0 else col)[:, None] * inv[None, :]
        f = w % (half // 2)
        cosf[LC:, j] = np.cos(ang[:, f])
        if w < half // 2:
            sinf[LC:, j] = -np.sin(ang[:, f])
            perm[j + half // 2, j] = 1.0
        else:
            sinf[LC:, j] = np.sin(ang[:, f])
            perm[j - half // 2, j] = 1.0
    return jnp.asarray(cosf, F32), jnp.asarray(sinf, F32), jnp.asarray(perm, BF16), jnp.asarray(perm.T, BF16)


def _exact_perm(x, pm):
    hi = x.astype(BF16)
    r1 = x - hi.astype(F32)
    mid = r1.astype(BF16)
    lo = (r1 - mid.astype(F32)).astype(BF16)
    dot = lambda a: jnp.dot(a, pm, preferred_element_type=F32)
    return dot(hi) + dot(mid) + dot(lo)


def _rot(x, cv, sv, pv, inverse):
    if inverse:
        return x * cv + _exact_perm(x * sv, pv)
    return x * cv + _exact_perm(x, pv) * sv


def rope_bwd(dx, cosf, sinf, pmt, scale, name):
    H, T, _ = dx.shape

    def body(x_ref, c_ref, s_ref, p_ref, o_ref):
        cv, sv, pv = c_ref[...], s_ref[...], p_ref[...]
        for h in range(H):
            o_ref[:, pl.ds(h * QK, QK)] = (_rot(x_ref[h], cv, sv, pv, True) * scale).astype(o_ref.dtype)

    return pl.pallas_call(
        body, grid=(T // TB,),
        in_specs=[pl.BlockSpec((H, TB, QK), lambda i: (0, i, 0)), pl.BlockSpec((TB, QK), lambda i: (i, 0)),
                  pl.BlockSpec((TB, QK), lambda i: (i, 0)), pl.BlockSpec((QK, QK), lambda i: (0, 0))],
        out_specs=pl.BlockSpec((TB, H * QK), lambda i: (i, 0)), out_shape=jax.ShapeDtypeStruct((T, H * QK), BF16),
        compiler_params=_cp(("parallel",)), name=name)(dx, cosf, sinf, pmt)


KVW = NOPE + VD


def project_q(cqn, w, name):
    T = cqn.shape[0]
    cosf, sinf, _, _ = _rope_tables(T, 128, NOPE)
    wp = jnp.pad(w.reshape(HEADS, QK, QL), ((0, 0), (0, 128 - QK), (0, 0))).reshape(HEADS * 128, QL)

    def body(a_ref, w_ref, c_ref, s_ref, o_ref):
        a, cv, sv = a_ref[...], c_ref[...], s_ref[...]
        first_of_pair = lax.bitwise_and(lax.broadcasted_iota(jnp.int32, (TB, 128), 1), ROPE // 4) == 0
        for h in range(HEADS):
            qh = _dotf(a, w_ref[pl.ds(h * 128, 128), :], "nt")
            swap = jnp.where(first_of_pair, pltpu.roll(qh, 128 - ROPE // 4, 1), pltpu.roll(qh, ROPE // 4, 1))
            o_ref[h] = ((qh * cv + swap * sv) * (SCALE * LOG2E))[:, :QK].astype(BF16)

    rows = lambda c: pl.BlockSpec((TB, c), lambda i: (i, 0))
    return pl.pallas_call(
        body, grid=(T // TB,), in_specs=[rows(QL), pl.BlockSpec(wp.shape, lambda i: (0, 0)), rows(128), rows(128)],
        out_specs=pl.BlockSpec((HEADS, TB, QK), lambda i: (0, i, 0)), out_shape=jax.ShapeDtypeStruct((HEADS, T, QK), BF16),
        compiler_params=_cp(("parallel",)), name=name)(cqn, wp, cosf, sinf)


def project_kv(ckvn, w, p0, kr_block, name):
    T = ckvn.shape[0]
    assert KVW == 128 and NOPE == VD
    cosf, sinf, pm, _ = _rope_tables(T, 128, 0)

    def body(a_ref, w_ref, kr_ref, c_ref, s_ref, p_ref, k_ref, v_ref):
        a = a_ref[...]
        is_nope = lax.broadcasted_iota(jnp.int32, (TB, KVW), 1) < NOPE
        kr_at = pltpu.roll(_rot(kr_ref[...], c_ref[...], s_ref[...], p_ref[...], False), NOPE, 1)
        for h in range(HEADS):
            kv = _dotf(a, w_ref[pl.ds(h * KVW, KVW), :], "nt")
            k_ref[h] = jnp.where(is_nope, kv, kr_at)[:, :QK].astype(BF16)
            v_ref[h] = pltpu.roll(kv, VD, 1)[:, :VD].astype(BF16)

    rows = lambda c: pl.BlockSpec((TB, c), lambda i: (i, 0))
    const = lambda x: pl.BlockSpec(x.shape, lambda i: (0, 0))
    return pl.pallas_call(
        body, grid=(T // TB,),
        in_specs=[rows(KVL), const(w), pl.BlockSpec((TB, 128), lambda i: (i, kr_block)), rows(128), rows(128), const(pm)],
        out_specs=[pl.BlockSpec((HEADS, TB, QK), lambda i: (0, i, 0)), pl.BlockSpec((HEADS, TB, VD), lambda i: (0, i, 0))],
        out_shape=[jax.ShapeDtypeStruct((HEADS, T, QK), BF16), jax.ShapeDtypeStruct((HEADS, T, VD), BF16)],
        compiler_params=_cp(("parallel",)), name=name)(ckvn, w, p0, cosf, sinf, pm)


def split_kv_grads(dk, dv, name, rode=None, modes=None):
    H, T, _ = dk.shape
    cosf, sinf, _, pmt = _rope_tables(T, 128, 0)
    to_rope_block = np.zeros((QK, 128), np.float32)
    to_rope_block[NOPE + np.arange(ROPE), np.arange(ROPE)] = 1.0
    to_rope_block = jnp.asarray(to_rope_block, BF16)

    def body(dk_ref, dv_ref, c_ref, s_ref, p_ref, sel_ref, dkv_ref, dkr_ref):
        total = None
        for h in range(H):
            dkh = dk_ref[h] * (1.0 / LOG2E)
            total = dkh if total is None else total + dkh
            dkv_ref[:, pl.ds(h * KVW, NOPE)] = dkh[:, :NOPE].astype(BF16)
            dkv_ref[:, pl.ds(h * KVW + NOPE, VD)] = dv_ref[h].astype(BF16)
        dkr_ref[...] = _rot(_exact_perm(total, sel_ref[...]), c_ref[...], s_ref[...], p_ref[...], True)

    rows = lambda c: pl.BlockSpec((TB, c), lambda i, j: (i, 0))
    const = lambda a: pl.BlockSpec(a.shape, lambda i, j: (0, 0))
    return _ride_call(
        body, (T // TB, 1),
        [pl.BlockSpec((H, TB, QK), lambda i, j: (0, i, 0)), pl.BlockSpec((H, TB, VD), lambda i, j: (0, i, 0)),
         rows(128), rows(128), const(pmt), const(to_rope_block)],
        [rows(H * KVW), rows(128)],
        [jax.ShapeDtypeStruct((T, H * KVW), BF16), jax.ShapeDtypeStruct((T, 128), F32)],
        Exchange(rode, modes) if rode else None, rode, name, (dk, dv, cosf, sinf, pmt, to_rope_block))


HB = 4
HBF = 8


def _by_query_block(run, T):
    @pl.when(pl.program_id(1) == 0)
    def _():
        run(LC)

    @pl.when(pl.program_id(1) > 0)
    def _():
        run(T)


def _with_rider(body, nin, nout, ride, grid):
    if ride is None:
        return body
    n = ride.n

    def wrapped(*refs):
        ins, xs = refs[:nin], refs[nin:nin + n]
        outs, got = refs[nin + n:nin + n + nout], refs[nin + n + nout:nin + 2 * n + nout]
        sems = refs[nin + 2 * n + nout:]
        step = pl.program_id(0) * grid[1] + pl.program_id(1)

        @pl.when(step == 0)
        def _():
            ride.start(xs, got, sems)

        body(*ins, *outs)

        @pl.when(step == grid[0] * grid[1] - 1)
        def _():
            ride.finish(xs, got, sems)

    return wrapped


def _ride_call(body, grid, in_specs, out_specs, out_shape, ride, rode, name, args):
    if ride is None:
        return pl.pallas_call(body, grid=grid, in_specs=in_specs, out_specs=out_specs, out_shape=out_shape,
                              compiler_params=_cp(("parallel", "arbitrary")), name=name)(*args), []
    res = pl.pallas_call(
        _with_rider(body, len(in_specs), len(out_specs), ride, grid), grid=grid,
        in_specs=in_specs + ride.specs, out_specs=out_specs + ride.specs, out_shape=out_shape + ride.out_shape,
        scratch_shapes=ride.scratch,
        compiler_params=pltpu.CompilerParams(dimension_semantics=("arbitrary", "arbitrary"), vmem_limit_bytes=VMEM_LIMIT,
                                             has_side_effects=True), name=name)(*args, *rode)
    return res[:len(out_specs)], res[len(out_specs):]


def attn_fwd(q, k, v, name, rode=None, modes=None):
    H, T, _ = q.shape

    def body(q_ref, k_ref, v_ref, o_ref, lse_ref):
        def run(nk):
            for hh in range(HBF):
                s = _dotf(q_ref[hh], k_ref[hh, pl.ds(0, nk), :], "nt")
                m = jnp.max(s, axis=1, keepdims=True)
                p = jnp.exp2(s - m)
                l = jnp.sum(p, axis=1, keepdims=True)
                o = jnp.dot(p.astype(BF16), v_ref[hh, pl.ds(0, nk), :], preferred_element_type=F32)
                o_ref[:, pl.ds(hh * VD, VD)] = o / l
                lse_ref[hh] = m + jnp.log2(l)

        _by_query_block(run, T)

    return _ride_call(
        body, (H // HBF, T // TB),
        [pl.BlockSpec((HBF, TB, QK), lambda h, i: (h, i, 0)), pl.BlockSpec((HBF, T, QK), lambda h, i: (h, 0, 0)),
         pl.BlockSpec((HBF, T, VD), lambda h, i: (h, 0, 0))],
        [pl.BlockSpec((TB, HBF * VD), lambda h, i: (i, h)), pl.BlockSpec((HBF, TB, 1), lambda h, i: (h, i, 0))],
        [jax.ShapeDtypeStruct((T, H * VD), F32), jax.ShapeDtypeStruct((H, T, 1), F32)],
        Exchange(rode, modes) if rode else None, rode, name, (q, k, v))


def attn_bwd(q, k, v, o, lse, do, name, rode=None, modes=None):
    H, T, _ = q.shape

    def body(q_ref, k_ref, v_ref, o_ref, lse_ref, do_ref, dq_ref, dk_ref, dv_ref):
        i = pl.program_id(1)

        @pl.when(i == 0)
        def _():
            dk_ref[...] = jnp.zeros_like(dk_ref)
            dv_ref[...] = jnp.zeros_like(dv_ref)

        def run(nk):
            keys = pl.ds(0, nk)
            for hh in range(HB):
                qv, kv, dov = q_ref[hh], k_ref[hh, keys, :], do_ref[:, pl.ds(hh * VD, VD)]
                p = jnp.exp2(_dotf(qv, kv, "nt") - lse_ref[hh])
                delta = jnp.sum(dov * o_ref[:, pl.ds(hh * VD, VD)], axis=1, keepdims=True)
                dob = dov.astype(BF16)
                dv_ref[hh, keys, :] += _dotf(p.astype(BF16), dob, "tn")
                dp = _dotf(dob, v_ref[hh, keys, :], "nt")
                ds = (p * (dp - delta)).astype(BF16)
                dq_ref[hh] = jnp.dot(ds, kv, preferred_element_type=F32)
                dk_ref[hh, keys, :] += _dotf(ds, qv, "tn")

        _by_query_block(run, T)

    blk = lambda c: pl.BlockSpec((HB, TB, c), lambda h, i: (h, i, 0))
    full = lambda c: pl.BlockSpec((HB, T, c), lambda h, i: (h, 0, 0))
    tok = pl.BlockSpec((TB, HB * VD), lambda h, i: (i, h))
    return _ride_call(
        body, (H // HB, T // TB), [blk(QK), full(QK), full(VD), tok, blk(1), tok], [blk(QK), full(QK), full(VD)],
        [jax.ShapeDtypeStruct((H, T, QK), F32), jax.ShapeDtypeStruct((H, T, QK), F32), jax.ShapeDtypeStruct((H, T, VD), F32)],
        Exchange(rode, modes) if rode else None, rode, name, (q, k, v, o, lse, do))


def disc_fwd(a_re, a_im, ls, name):
    def body(ar_ref, ai_ref, ls_ref, lr_ref, li_ref, fr_ref, fi_ref):
        ar, ai = ar_ref[...], ai_ref[...]
        dt = jnp.exp(ls_ref[...])
        mag = jnp.exp(ar * dt)
        lr = mag * jnp.cos(ai * dt)
        li = mag * jnp.sin(ai * dt)
        den = ar * ar + ai * ai
        nr = lr - 1.0
        lr_ref[...] = lr
        li_ref[...] = li
        fr_ref[...] = (nr * ar + li * ai) / den
        fi_ref[...] = (li * ar - nr * ai) / den

    return pl.pallas_call(body, out_shape=[jax.ShapeDtypeStruct(a_re.shape, F32)] * 4, name=name)(a_re, a_im, ls)


def disc_b(f_re, f_im, b_re, b_im, name):
    def body(fr_ref, fi_ref, br_ref, bi_ref, or_ref, oi_ref):
        fr, fi, br, bi = fr_ref[...], fi_ref[...], br_ref[...], bi_ref[...]
        or_ref[...] = fr * br - fi * bi
        oi_ref[...] = fr * bi + fi * br

    return pl.pallas_call(body, out_shape=[jax.ShapeDtypeStruct(b_re.shape, F32)] * 2, compiler_params=_cp(),
                          name=name)(f_re, f_im, b_re, b_im)


def disc_b_bwd(f_re, f_im, b_re, b_im, dbb_re, dbb_im, name):
    def body(fr_ref, fi_ref, br_ref, bi_ref, dr_ref, di_ref, dbr_ref, dbi_ref, dfr_ref, dfi_ref):
        fr, fi, br, bi, dr, di = fr_ref[...], fi_ref[...], br_ref[...], bi_ref[...], dr_ref[...], di_ref[...]
        dbr_ref[...] = fr * dr + fi * di
        dbi_ref[...] = fr * di - fi * dr
        dfr_ref[...] = jnp.sum(dr * br + di * bi, axis=2, keepdims=True)
        dfi_ref[...] = jnp.sum(di * br - dr * bi, axis=2, keepdims=True)

    return pl.pallas_call(body, out_shape=[jax.ShapeDtypeStruct(b_re.shape, F32)] * 2 + [jax.ShapeDtypeStruct(f_re.shape, F32)] * 2,
                          compiler_params=_cp(), name=name)(f_re, f_im, b_re, b_im, dbb_re, dbb_im)


def disc_a_bwd(a_re, a_im, ls, dlr, dli, dfr, dfi, name):
    def body(ar_ref, ai_ref, ls_ref, dlr_ref, dli_ref, dfr_ref, dfi_ref, dar_ref, dai_ref, dls_ref):
        ar, ai = ar_ref[...], ai_ref[...]
        dt = jnp.exp(ls_ref[...])
        mag = jnp.exp(ar * dt)
        cs, sn = jnp.cos(ai * dt), jnp.sin(ai * dt)
        lr, li = mag * cs, mag * sn
        den = ar * ar + ai * ai
        nr = lr - 1.0
        f_re = (nr * ar + li * ai) / den
        f_im = (li * ar - nr * ai) / den
        dn1 = dfr_ref[...] / den
        dn2 = dfi_ref[...] / den
        dden = -(dfr_ref[...] * f_re + dfi_ref[...] * f_im) / den
        dlr_t = dlr_ref[...] + dn1 * ar - dn2 * ai
        dli_t = dli_ref[...] + dn1 * ai + dn2 * ar
        dar = dn1 * nr + dn2 * li + dden * 2.0 * ar
        dai = dn1 * li - dn2 * nr + dden * 2.0 * ai
        dmag = dlr_t * cs + dli_t * sn
        dth = dli_t * lr - dlr_t * li
        dar_ref[...] = dar + dmag * mag * dt
        dai_ref[...] = dai + dth * dt
        dls_ref[...] = jnp.sum(dmag * mag * ar + dth * ai, axis=-1, keepdims=True) * dt

    return pl.pallas_call(body, out_shape=[jax.ShapeDtypeStruct(a_re.shape, F32)] * 2 +
                          [jax.ShapeDtypeStruct(ls.shape, F32)], name=name)(a_re, a_im, ls, dlr, dli, dfr, dfi)


def _cpow(lr, li, n):
    rr, ri = None, None
    br, bi = lr, li
    while n:
        if n & 1:
            if rr is None:
                rr, ri = br, bi
            else:
                rr, ri = rr * br - ri * bi, rr * bi + ri * br
        n >>= 1
        if n:
            br, bi = br * br - bi * bi, 2.0 * br * bi
    return rr, ri


UNROLL = 4


def _steps(trips, fn, init):
    main = trips // UNROLL

    def body(i, c):
        for j in range(UNROLL):
            c = fn(i * UNROLL + j, c)
        return c

    c = lax.fori_loop(0, main, body, init) if main else init
    for n in range(main * UNROLL, trips):
        c = fn(n, c)
    return c


def _seg_scan(xre, xim, lam8, pw, base, seglen, rev, init, fin_re, fin_im, ini_re, ini_im, prev=None):
    lr, li = lam8
    nsub = SEG // 8

    def rows(t, s):
        first = base + t * SEG + 8 * s
        return pl.ds(first if isinstance(first, int) else pl.multiple_of(first, 8), 8)

    tmap = (lambda n: seglen - 1 - n) if rev else (lambda n: n)
    zeros = tuple(jnp.zeros((8, SB), F32) for _ in range(2 * nsub))

    def advance(c, t):
        out = []
        for s in range(nsub):
            a, b = c[2 * s], c[2 * s + 1]
            out += [lr * a - li * b + xre[rows(t, s), :], lr * b + li * a + xim[rows(t, s), :]]
        return tuple(out)

    fin = _steps(seglen, lambda n, c: advance(c, tmap(n)), zeros)
    for s in range(nsub):
        fin_re[pl.ds(8 * s, 8), :] = fin[2 * s]
        fin_im[pl.ds(8 * s, 8), :] = fin[2 * s + 1]
    (cr, ci), (pr, pi) = init, pw
    for i in (range(SEG - 1, -1, -1) if rev else range(SEG)):
        ini_re[pl.ds(i, 1), :] = cr
        ini_im[pl.ds(i, 1), :] = ci
        cr, ci = pr * cr - pi * ci + fin_re[pl.ds(i, 1), :], pr * ci + pi * cr + fin_im[pl.ds(i, 1), :]
    tiles = lambda re, im: tuple(r[pl.ds(8 * s, 8), :] for s in range(nsub) for r in (re, im))
    start = tiles(ini_re, ini_im)

    def store(c, t):
        new = advance(c, t)
        for s in range(nsub):
            xre[rows(t, s), :] = new[2 * s]
            xim[rows(t, s), :] = new[2 * s + 1]
        return new

    if prev is None:
        _steps(seglen, lambda n, c: store(c, tmap(n)), start)
        return (cr, ci), None

    sre, sim, s_ini_re, s_ini_im = prev

    def acc_step(c, t, before):
        new = store(c[:2 * nsub], t)
        acc = []
        for s in range(nsub):
            (na, nb), (pre, pim) = new[2 * s:2 * s + 2], before[2 * s:2 * s + 2]
            acc += [c[2 * nsub + 2 * s] + na * pre + nb * pim, c[2 * nsub + 2 * s + 1] + nb * pre - na * pim]
        return new + tuple(acc)

    def body(n, c):
        t = tmap(n)
        tp = t - 1 if rev else t + 1
        return acc_step(c, t, tuple(r[rows(tp, s), :] for s in range(nsub) for r in (sre, sim)))

    c = _steps(seglen - 1, body, start + zeros)
    c = acc_step(c, 0 if rev else seglen - 1, tiles(s_ini_re, s_ini_im))
    acc = c[2 * nsub:]
    return (cr, ci), (sum(acc[0::2][1:], acc[0]), sum(acc[1::2][1:], acc[1]))


def _lam_tiles(lr, li, lens, conj=False):
    if conj:
        li = -li
    lam8 = (jnp.broadcast_to(lr, (8, SB)), jnp.broadcast_to(li, (8, SB)))
    return lam8, [_cpow(lr, li, n) for n in lens]


def _stretches(T):
    return ((0, LC // SEG), (LC, (T - LC) // SEG))


def _to_seg_order(src, dst, T):
    for base, seglen in _stretches(T):
        def body(t, carry, base=base, seglen=seglen):
            dst[pl.ds(pl.multiple_of(base + t * SEG, SEG), SEG), :] = src[pl.ds(base + t, SEG, stride=seglen), :]
            return carry
        lax.fori_loop(0, seglen, body, 0, unroll=8)


def _from_seg_order(src, dst, T):
    for base, seglen in _stretches(T):
        def body(t, carry, base=base, seglen=seglen):
            dst[pl.ds(base + t, SEG, stride=seglen), :] = src[pl.ds(pl.multiple_of(base + t * SEG, SEG), SEG), :]
            return carry
        lax.fori_loop(0, seglen, body, 0, unroll=8)


def _scan_specs(T):
    ublk = pl.BlockSpec((T, UB), lambda j: (0, j))
    lam = pl.BlockSpec((2, 1, 1, SB), lambda j: (0, j, 0, 0))
    mat = pl.BlockSpec((2, 1, UB, P), lambda j: (0, j, 0, 0))
    return ublk, lam, mat


def _dotf(a, b, mode="nn"):
    return lax.dot_general(a, b, _DN[mode], preferred_element_type=F32)


def _diag_mask():
    r = lax.broadcasted_iota(jnp.int32, (UB, SB), 0)
    c = lax.broadcasted_iota(jnp.int32, (UB, SB), 1)
    return lax.shift_right_logical(r, int(math.log2(CH))) == lax.shift_right_logical(c, int(math.log2(P)))


def _expand(m):
    p = lax.broadcasted_iota(jnp.int32, (P, SB), 0)
    c = lax.broadcasted_iota(jnp.int32, (P, SB), 1)
    tile = jnp.where(lax.bitwise_and(c, P - 1) == p, 1.0, 0.0).astype(BF16)
    wide = jnp.dot(m.astype(BF16), tile, preferred_element_type=F32)
    return jnp.where(_diag_mask(), wide, 0.0).astype(BF16)


def _collapse(full):
    c = lax.broadcasted_iota(jnp.int32, (SB, P), 0)
    p = lax.broadcasted_iota(jnp.int32, (SB, P), 1)
    pick = jnp.where(lax.bitwise_and(c, P - 1) == p, 1.0, 0.0).astype(BF16)
    return _exact_perm(jnp.where(_diag_mask(), full, 0.0), pick)


def _zero_state():
    return jnp.zeros((1, SB), F32), jnp.zeros((1, SB), F32)


def scan_fwd(u, lam_re, lam_im, bre, bim, cre, cim, name):
    T = u.shape[0]
    s_ctx, s_lat = LC // SEG, (T - LC) // SEG

    def body(u_ref, lr_ref, li_ref, bre_ref, bim_ref, cre_ref, cim_ref, y_ref, us, ys, sre, sim, fre, fim, ire, iim):
        _to_seg_order(u_ref, us, T)
        ub = us[...].astype(BF16)
        for d in range(2):
            lam8, (pw_c, pw_l) = _lam_tiles(lr_ref[d, 0], li_ref[d, 0], (s_ctx, s_lat))
            sre[...] = _dotf(ub, _expand(bre_ref[d, 0]))
            sim[...] = _dotf(ub, _expand(bim_ref[d, 0]))
            end_c, _ = _seg_scan(sre, sim, lam8, pw_c, 0, s_ctx, bool(d), _zero_state(), fre, fim, ire, iim)
            _seg_scan(sre, sim, lam8, pw_l, LC, s_lat, bool(d), end_c, fre, fim, ire, iim)
            y = (_dotf(sre[...].astype(BF16), _expand(cre_ref[d, 0]), "nt")
                 - _dotf(sim[...].astype(BF16), _expand(cim_ref[d, 0]), "nt"))
            if d == 0:
                ys[...] = y
            else:
                ys[...] += y
        _from_seg_order(ys, y_ref, T)

    ublk, lam, mat = _scan_specs(T)
    return pl.pallas_call(
        body, grid=(NJ,), in_specs=[ublk, lam, lam, mat, mat, mat, mat], out_specs=ublk,
        out_shape=jax.ShapeDtypeStruct((T, G * CH), F32),
        scratch_shapes=[pltpu.VMEM((T, UB), F32)] * 2 + [pltpu.VMEM((T, SB), F32)] * 2 + [pltpu.VMEM((SEG, SB), F32)] * 4,
        compiler_params=_cp(("arbitrary",)), name=name)(u, lam_re, lam_im, bre, bim, cre, cim)


def scan_bwd(u, dy, lam_re, lam_im, bre, bim, cre, cim, name):
    T = u.shape[0]
    s_ctx, s_lat = LC // SEG, (T - LC) // SEG

    def body(u_ref, dy_ref, lr_ref, li_ref, bre_ref, bim_ref, cre_ref, cim_ref,
             du_ref, dlr_ref, dli_ref, dbre_ref, dbim_ref, dcre_ref, dcim_ref,
             us, dys, dus, sre, sim, gre, gim, fre, fim, ic_re, ic_im, il_re, il_im, jre, jim):
        _to_seg_order(u_ref, us, T)
        _to_seg_order(dy_ref, dys, T)
        ub, dyb = us[...].astype(BF16), dys[...].astype(BF16)
        for d in range(2):
            rev = bool(d)
            lam8, (pw_c, pw_l) = _lam_tiles(lr_ref[d, 0], li_ref[d, 0], (s_ctx, s_lat))
            cam8, (cw_c, cw_l) = _lam_tiles(lr_ref[d, 0], li_ref[d, 0], (s_ctx, s_lat), conj=True)
            bre_v, bim_v = _expand(bre_ref[d, 0]), _expand(bim_ref[d, 0])
            sre[...] = _dotf(ub, bre_v)
            sim[...] = _dotf(ub, bim_v)
            end_c, _ = _seg_scan(sre, sim, lam8, pw_c, 0, s_ctx, rev, _zero_state(), fre, fim, ic_re, ic_im)
            _seg_scan(sre, sim, lam8, pw_l, LC, s_lat, rev, end_c, fre, fim, il_re, il_im)
            gre[...] = _dotf(dyb, _expand(cre_ref[d, 0]))
            gim[...] = -_dotf(dyb, _expand(cim_ref[d, 0]))
            end_g, acc_l = _seg_scan(gre, gim, cam8, cw_l, LC, s_lat, not rev, _zero_state(), fre, fim, jre, jim,
                                     prev=(sre, sim, il_re, il_im))
            _, acc_c = _seg_scan(gre, gim, cam8, cw_c, 0, s_ctx, not rev, end_g, fre, fim, jre, jim,
                                 prev=(sre, sim, ic_re, ic_im))
            dlr_ref[d, 0] = _sum0(acc_l[0] + acc_c[0])
            dli_ref[d, 0] = _sum0(acc_l[1] + acc_c[1])
            grb, gib = gre[...].astype(BF16), gim[...].astype(BF16)
            du = _dotf(grb, bre_v, "nt") + _dotf(gib, bim_v, "nt")
            if d == 0:
                dus[...] = du
            else:
                dus[...] += du
            dbre_ref[d, 0] = _collapse(_dotf(ub, grb, "tn"))
            dbim_ref[d, 0] = _collapse(_dotf(ub, gib, "tn"))
            dcre_ref[d, 0] = _collapse(_dotf(dyb, sre[...].astype(BF16), "tn"))
            dcim_ref[d, 0] = -_collapse(_dotf(dyb, sim[...].astype(BF16), "tn"))
        _from_seg_order(dus, du_ref, T)

    ublk, lam, mat = _scan_specs(T)
    lam_s = jax.ShapeDtypeStruct(lam_re.shape, F32)
    mat_s = jax.ShapeDtypeStruct(bre.shape, F32)
    return pl.pallas_call(
        body, grid=(NJ,), in_specs=[ublk, ublk, lam, lam, mat, mat, mat, mat],
        out_specs=[ublk, lam, lam, mat, mat, mat, mat],
        out_shape=[jax.ShapeDtypeStruct((T, G * CH), F32), lam_s, lam_s, mat_s, mat_s, mat_s, mat_s],
        scratch_shapes=[pltpu.VMEM((T, UB), F32)] * 3 + [pltpu.VMEM((T, SB), F32)] * 4 + [pltpu.VMEM((SEG, SB), F32)] * 8,
        compiler_params=_cp(("arbitrary",)), name=name)(u, dy, lam_re, lam_im, bre, bim, cre, cim)


class Exchange:
    def __init__(self, xs, modes):
        self.n = len(xs)
        self.modes = [modes] * self.n if isinstance(modes, (str, int)) else list(modes)
        self.out_shape = [jax.ShapeDtypeStruct(self._shape(x, md), x.dtype) for x, md in zip(xs, self.modes)]
        self.scratch = [pltpu.SemaphoreType.DMA((NDEV - 1, self.n)), pltpu.SemaphoreType.DMA((NDEV - 1, self.n)),
                        pltpu.SemaphoreType.DMA((self.n,))]
        self.specs = [pl.BlockSpec(memory_space=pl.ANY)] * self.n

    @staticmethod
    def _shape(x, mode):
        if mode == "gather":
            return (NDEV,) + tuple(x.shape)
        return tuple(x.shape) if mode == "lead" else (NDEV, x.shape[0], mode) + tuple(x.shape[2:])

    @staticmethod
    def _piece(x_ref, mode, dev):
        if mode == "gather":
            return x_ref
        return x_ref.at[dev] if mode == "lead" else x_ref.at[:, pl.ds(dev * mode, mode)]

    def _copies(self, x_refs, out_refs, sems):
        send_sems, recv_sems, local_sems = sems
        mx, my, mc = lax.axis_index("x"), lax.axis_index("y"), lax.axis_index("c")
        me = 4 * mx + 2 * my + mc
        peer_of = lambda k: (1 - mx if k & 4 else mx, 1 - my if k & 2 else my, 1 - mc if k & 1 else mc)
        local, first, relay, arrivals = [], [], [], []
        for a, (x_ref, out_ref) in enumerate(zip(x_refs, out_refs)):
            mode = self.modes[a]
            local.append(pltpu.make_async_copy(self._piece(x_ref, mode, me), out_ref.at[me], local_sems.at[a]))

            def remote(src, dst, k, pair, a=a):
                return pltpu.make_async_remote_copy(src_ref=src, dst_ref=dst, send_sem=send_sems.at[pair, a],
                                                    recv_sem=recv_sems.at[pair, a], device_id=peer_of(k), device_id_type=MESH_T)

            for k in range(1, NDEV):
                peer = peer_of(k)
                pid = 4 * peer[0] + 2 * peer[1] + peer[2]
                if mode != "gather":
                    src = self._piece(x_ref, mode, pid)
                    first.append(remote(src, out_ref.at[me], k, k - 1))
                    arrivals.append(remote(src, out_ref.at[pid], k, k - 1))
                elif k == 1:
                    first.append(remote(x_ref, out_ref.at[me], k, k - 1))
                    arrivals.append(remote(x_ref, out_ref.at[pid], k, k - 1))
                elif k % 2 == 0:
                    first.append(remote(x_ref, out_ref.at[me], k, k - 1))
                    relay.append((remote(x_ref, out_ref.at[pid], k, k - 1), remote(out_ref.at[pid], out_ref.at[pid], 1, k)))
                else:
                    arrivals.append(remote(x_ref, out_ref.at[pid], 1, k - 1))
        return local, first, relay, arrivals

    def start(self, x_refs, out_refs, sems):
        local, first, _, _ = self._copies(x_refs, out_refs, sems)
        for cp in local + first:
            cp.start()

    def finish(self, x_refs, out_refs, sems):
        local, first, relay, arrivals = self._copies(x_refs, out_refs, sems)
        for arrival, onward in relay:
            arrival.wait_recv()
            onward.start()
        for cp in arrivals:
            cp.wait_recv()
        for cp in first + [onward for _, onward in relay]:
            cp.wait_send()
        for cp in local:
            cp.wait()


def exchange(xs, modes, name):
    ex = Exchange(xs, modes)
    n = ex.n

    def body(*refs):
        ex.start(refs[:n], refs[n:2 * n], refs[2 * n:])
        ex.finish(refs[:n], refs[n:2 * n], refs[2 * n:])

    return pl.pallas_call(body, in_specs=ex.specs, out_specs=ex.specs, out_shape=ex.out_shape, scratch_shapes=ex.scratch,
                          compiler_params=pltpu.CompilerParams(has_side_effects=True), name=name)(*xs)


def _dot_f32(a, b, dn):
    return lax.dot_general(a, b, dn, preferred_element_type=F32, precision=lax.Precision.HIGHEST)


def ada_fwd(cg, c_ctx, ada_w, ada_b_loc, name):
    W = ada_w.shape[2]

    def body(cg_ref, cc_ref, w_ref, b_ref, o_ref):
        a = jnp.concatenate([_silu(cg_ref[...]), jnp.broadcast_to(_silu(cc_ref[...]), (NDEV, D))], axis=0)
        for i in range(2):
            o_ref[i] = _dot_f32(a, w_ref[i], _DN["nn"]) + b_ref[i]

    return pl.pallas_call(body, out_shape=jax.ShapeDtypeStruct((2, 2 * NDEV, W), F32),
                          compiler_params=_cp(), name=name)(cg, c_ctx, ada_w, ada_b_loc)


def ada_bwd(cg, c_ctx, ada_w, dm_loc, dm_all, name):
    W = ada_w.shape[2]

    def body(cg_ref, cc_ref, w_ref, dl_ref, da_ref, gw_ref, dcc_ref, gb_ref):
        a = jnp.concatenate([_silu(cg_ref[...]), jnp.broadcast_to(_silu(cc_ref[...]), (NDEV, D))], axis=0)
        dcc = jnp.zeros((1, D), F32)
        for i in range(2):
            dl = dl_ref[i]
            gw_ref[i] = _dot_f32(a, dl, _DN["tn"])
            dctx = jnp.sum(dl[NDEV:], axis=0, keepdims=True)
            dcc = dcc + _dot_f32(dctx, w_ref[i], _DN["nt"])
        dcc_ref[...] = dcc
        gb_ref[...] = jnp.sum(da_ref[...], axis=0)

    return pl.pallas_call(body, out_shape=[jax.ShapeDtypeStruct((2, D, W), F32), jax.ShapeDtypeStruct((1, D), F32),
                                           jax.ShapeDtypeStruct((2, 3 * D), F32)],
                          compiler_params=_cp(), name=name)(cg, c_ctx, ada_w, dm_loc, dm_all)


def cctx_finish(parts, c_ctx, name):
    def body(p_ref, cc_ref, o_ref):
        o_ref[...] = jnp.sum(p_ref[...], axis=0, keepdims=True) * _dsilu(cc_ref[...])

    return pl.pallas_call(body, out_shape=jax.ShapeDtypeStruct((1, D), F32), name=name)(parts, c_ctx)


def _adamw_update(g_ref, w_ref, m_ref, v_ref, go_ref, d_ref, mo_ref, vo_ref):
    g = g_ref[0].astype(F32)
    for s in range(1, g_ref.shape[0]):
        g = g + g_ref[s].astype(F32)
    mn = B1 * m_ref[...] + (1.0 - B1) * g
    vn = B2 * v_ref[...] + (1.0 - B2) * g * g
    go_ref[...] = g
    mo_ref[...] = mn
    vo_ref[...] = vn
    d_ref[...] = -LR * ((mn * (1.0 / (1.0 - B1 ** STEP))) / (jnp.sqrt(vn * (1.0 / (1.0 - B2 ** STEP))) + AEPS) + WD * w_ref[...])


ADAMW_PARTS = 4


def adamw_rows(items, name, rode=None, modes=None):
    in_specs, out_specs, out_shape, args = [], [], [], []
    for g, w, m, v in items:
        n, R, C = g.shape
        tr = R // ADAMW_PARTS
        spec = pl.BlockSpec((tr, C), lambda i, j: (i, 0))
        in_specs += [pl.BlockSpec((n, tr, C), lambda i, j: (0, i, 0)), spec, spec, spec]
        args += [g, w, m, v]
    for g, w, m, v in items:
        tr = w.shape[0] // ADAMW_PARTS
        out_specs += [pl.BlockSpec((tr, w.shape[1]), lambda i, j: (i, 0))] * 4
        out_shape += [jax.ShapeDtypeStruct(w.shape, F32)] * 4
    res, got = _ride_call(_adamw_body(len(items)), (ADAMW_PARTS, 1), in_specs, out_specs, out_shape,
                          Exchange(rode, modes) if rode else None, rode, name, args)
    return [res[4 * t:4 * t + 4] for t in range(len(items))], got


def _adamw_body(k):
    def body(*refs):
        for t in range(k):
            _adamw_update(*refs[4 * t:4 * t + 4], *refs[4 * k + 4 * t:4 * k + 4 * t + 4])
    return body


def adamw_multi(items, grid, name):
    k = len(items)
    ins, in_specs, out_specs, out_shape = [], [], [], []
    for g, g_spec, w, m, v, w_spec in items:
        ins += [g, w, m, v]
        in_specs += [g_spec, w_spec, w_spec, w_spec]
    for g, g_spec, w, m, v, w_spec in items:
        out_specs += [w_spec] * 4
        out_shape += [jax.ShapeDtypeStruct(w.shape, F32)] * 4
    res = pl.pallas_call(_adamw_body(k), grid=grid, in_specs=in_specs, out_specs=out_specs, out_shape=out_shape,
                         compiler_params=_cp(("arbitrary",) * len(grid)), name=name)(*ins)
    return [res[4 * t:4 * t + 4] for t in range(k)]


def _whole(a, grid_rank):
    zeros = (0,) * a.ndim
    return pl.BlockSpec(a.shape, lambda *idx: zeros)


def sum_slots(xs, name):
    def body(*refs):
        for x_ref, o_ref in zip(refs[:len(xs)], refs[len(xs):]):
            acc = x_ref[0]
            for s in range(1, NDEV):
                acc = acc + x_ref[s]
            o_ref[...] = acc

    return pl.pallas_call(body, out_shape=[jax.ShapeDtypeStruct(x.shape[1:], F32) for x in xs],
                          compiler_params=_cp(), name=name)(*xs)


def _col_shards(g):
    R, N = g.shape
    return g.reshape(R, NDEV, N // NDEV).transpose(1, 0, 2)


def _vec2(v):
    return jnp.broadcast_to(v.reshape(1, 1, -1), (2, 1, v.size))


SHARD_ROWS = {"mla_w_in": 192, "mla_w_uq": 192, "mla_w_ukv": 256, "s5_w_in": 256}


def _t_shard(wsh, rows):
    t = wsh[0].T.astype(BF16)
    return jnp.pad(t, ((0, rows - t.shape[0]), (0, 0)))


def _win_order():
    w = IN_W // NDEV
    perm = np.zeros((IN_WP, NDEV * SHARD_ROWS["mla_w_in"]), np.float32)
    first = QL + KVL + ROPE
    for c in range(IN_W):
        n = c + HEADS * VD if c < first else c - first
        perm[n, (c // w) * SHARD_ROWS["mla_w_in"] + c % w] = 1.0
    return jnp.asarray(perm, BF16)


def local_step(ctx, x, tgt, mod, Wt, small, l1_shards):
    T = LC + x.shape[0]
    xa = ("cat", ctx, x)
    sh = [mod[i, :, None, 0:D] for i in range(2)]
    sc = [mod[i, :, None, D:2 * D] for i in range(2)]
    gt = [mod[i, :, None, 2 * D:] for i in range(2)]
    ng = [_vec2(small["norm_g"][i]) for i in range(2)]
    qg, kvg = _vec2(small["mla_q_norm"]), _vec2(small["mla_kv_norm"])
    cosf, sinf, _, pmt = _rope_tables(T)

    (h0, p0, cqn, ckvn), _ = rowwise(st_l0_pre, [xa], [ng[0], sc[0], sh[0], qg, kvg],
                                     [(D, BF16), (IN_WP, F32), (QL, BF16), (KVL, BF16)], [], "l0_pre", mats=[Wt["mla_w_in"]])
    z0, cq, ckv = (p0, 0, HEADS * VD), (p0, HEADS * VD // QL, QL), (p0, (HEADS * VD + QL) // KVL, KVL)
    Q = project_q(cqn, Wt["mla_w_uq"], "l0_uq")
    K, V = project_kv(ckvn, Wt["mla_w_ukv"], p0, (HEADS * VD + QL + KVL) // 128, "l0_ukv")
    (o2, lse), got = attn_fwd(Q, K, V, "l0_attn", rode=l1_shards, modes="gather")
    Wt, small = dict(Wt), dict(small)
    for n, a in zip(L1_BIG, got):
        Wt[n] = a.reshape(-1, a.shape[-1])
    vecs = lax.bitcast_convert_type(got[-1].reshape(NDEV, 2, -1, 2), F32)
    small["s5_d"], small["s5_b_glu"] = vecs[:, 0, :].reshape(D), vecs[:, 1, :].reshape(D)
    (og, out0, x1), _ = rowwise(st_l0_post, [o2, z0, xa], [gt[0]], [(D, BF16), (D, BF16), (D, F32)], [], "l0_post",
                                mats=[Wt["mla_w_out"]])

    ls = small["s5_log_step"].reshape(2, G, 1)
    a_re, a_im = small["s5_a_re"].reshape(2, G, P), small["s5_a_im"].reshape(2, G, P)
    b_re = small["s5_b_re"].reshape(2, G, P, CH).transpose(0, 1, 3, 2)
    b_im = small["s5_b_im"].reshape(2, G, P, CH).transpose(0, 1, 3, 2)
    lam_re, lam_im, f_re, f_im = disc_fwd(a_re, a_im, ls, "s5_disc")
    f_re2, f_im2 = f_re.reshape(2, G, 1, P), f_im.reshape(2, G, 1, P)
    bb_re, bb_im = disc_b(f_re2, f_im2, b_re, b_im, "s5_disc_b")
    compact = lambda m: m.reshape(2, NJ, UB, P)
    bre, bim = compact(bb_re), compact(bb_im)
    cre, cim = compact(small["s5_c_re"]), compact(small["s5_c_im"])
    lam_re4, lam_im4 = lam_re.reshape(2, NJ, 1, SB), lam_im.reshape(2, NJ, 1, SB)

    (h1, p1), _ = rowwise(st_l1_pre, [x1], [ng[1], sc[1], sh[1]], [(D, BF16), (2 * D, F32)], [], "l1_pre", mats=[Wt["s5_w_in"]])
    u, z1 = (p1, 0, D), (p1, 1, D)
    yssm = scan_fwd(p1, lam_re4, lam_im4, bre, bim, cre, cim, "s5_scan")
    dvec, bglu = _vec2(small["s5_d"]), _vec2(small["s5_b_glu"])
    fg = _vec2(small["final_g"])
    lat_mask = jnp.stack([jnp.zeros((1, D), F32), jnp.ones((1, D), F32)])
    (y, y1b, gl, y3, out1, dx2), (dfg, lvec) = rowwise(
        st_l1_mlp, [yssm, u, z1, x1, ("lat", tgt)], [dvec, bglu, gt[1], fg, lat_mask],
        [(D, F32), (D, BF16), (D, BF16), (D, BF16), (D, BF16), (D, F32)], [D, 128], "l1_mlp",
        mats=[Wt["s5_w_glu"], Wt["s5_w_out"]])

    (dz1, dy, du_d), (dgt1, dbglu, dd), (g_w_out5, g_w_glu) = rowwise(
        st_l1_mlp_bwd, [dx2, out1, y3, y, gl, z1, u, y1b], [gt[1], bglu, dvec], [(D, BF16), (D, F32), (D, F32)], [D, D, D],
        "l1_mlp_b", mats=[Wt["s5_w_out"], Wt["s5_w_glu"]], out_accs=[(D, D), (D, D)])
    du_s, dlr, dli, dbre, dbim, dcre, dcim = scan_bwd(p1, dy, lam_re4, lam_im4, bre, bim, cre, cim, "s5_scan_b")
    dbb_re, dbb_im = dbre.reshape(2, G, CH, P), dbim.reshape(2, G, CH, P)
    g_c_re, g_c_im = dcre.reshape(2, G, CH, P), dcim.reshape(2, G, CH, P)
    gt_b_re, gt_b_im, dfr, dfi = disc_b_bwd(f_re2, f_im2, b_re, b_im, dbb_re, dbb_im, "s5_disc_b_b")
    g_b_re, g_b_im = gt_b_re.transpose(0, 1, 3, 2), gt_b_im.transpose(0, 1, 3, 2)
    g_a_re, g_a_im, g_ls = disc_a_bwd(a_re, a_im, ls, dlr.reshape(2, G, P), dli.reshape(2, G, P),
                                      dfr.reshape(2, G, P), dfi.reshape(2, G, P), "s5_disc_b_a")
    (dx1,), (dsh1, dsc1, dng1), (g_w_in5,) = rowwise(
        st_l1_tail_bwd, [du_d, du_s, dz1, h1, x1, dx2], [ng[1], sc[1]], [(D, F32)], [D, D, D], "l1_pre_b",
        mats=[Wt["s5_w_in"]], out_accs=[(NDEV, D, 2 * D // NDEV)])

    (do2, dz0), (dgt0,), (g_w_out,) = rowwise(st_l0_post_bwd, [dx1, out0, og, o2, z0], [gt[0]], [(D, F32), (D, F32)], [D],
                                              "l0_post_b", mats=[Wt["mla_w_out"]], out_accs=[(D, D)])
    rows8 = lambda g: g.reshape(NDEV, -1, g.shape[-1])
    both = lambda s: s[0, 0] + s[1, 0]
    dense = lambda g: g.reshape(2, G * P * CH // 128, 128)
    chunks = [dense(g_b_re), dense(g_b_im), g_c_re, g_c_im]
    l1_send = [g_w_in5, rows8(g_w_glu), rows8(g_w_out5), rows8(g_w_out),
               both(dd).reshape(NDEV, 1, -1), both(dbglu).reshape(NDEV, 1, -1)]
    (dQ, dK, dV), l1_recv = attn_bwd(Q, K, V, o2, lse, do2, "l0_attn_b", rode=l1_send + chunks,
                                     modes=["lead"] * len(l1_send) + [a.shape[1] // NDEV for a in chunks])
    dq = rope_bwd(dQ, cosf, sinf, pmt, SCALE, "l0_rope_q_b")
    n_owned = len(l1_send)
    reduced = sum_slots(l1_recv[n_owned:], "sum_chunks")
    (dkv, dkr), chunk_all = split_kv_grads(dK, dV, "l0_kv_b", rode=[jnp.stack(reduced[:2]), jnp.stack(reduced[2:])],
                                           modes="gather")
    (grad_x,), (dqg, dkvg, dsh0, dsc0, dng0), (g_uq, g_ukv, g_p) = rowwise(
        st_l0_tail_bwd, [dq, dkv, dkr, dz0, cq, ckv, cqn, ckvn, h0, xa, dx1], [qg, kvg, ng[0], sc[0]],
        [(D, F32, "lat")], [QL, KVL, D, D, D], "l0_pre_b", mats=[Wt["mla_w_uq"], Wt["mla_w_ukv"], Wt["mla_w_in"]],
        out_accs=[(QL, HEADS * QK), (KVL, HEADS * KVW), (D, IN_WP)])
    g_w_uq, g_w_ukv = _col_shards(g_uq).astype(BF16), _col_shards(g_ukv).astype(BF16)
    g_w_in = _col_shards(jnp.concatenate([g_p[:, HEADS * VD:IN_W], g_p[:, :HEADS * VD]], axis=1)).astype(BF16)

    dmod = jnp.stack([jnp.concatenate([dsh0, dsc0, dgt0], axis=-1)[:, 0], jnp.concatenate([dsh1, dsc1, dgt1], axis=-1)[:, 0]])
    gbig = {"mla_w_in": g_w_in, "mla_w_uq": g_w_uq, "mla_w_ukv": g_w_ukv}
    gsmall = {"norm_g": jnp.stack([both(dng0), both(dng1)]), "mla_q_norm": both(dqg), "mla_kv_norm": both(dkvg),
              "s5_a_re": g_a_re, "s5_a_im": g_a_im, "s5_log_step": g_ls, "final_g": dfg[1, 0]}
    return lvec[1], grad_x, dmod, gbig, gsmall, l1_recv[:n_owned], chunk_all


COL_SHARDED = ("mla_w_in", "mla_w_uq", "mla_w_ukv", "s5_w_in")
ROW_SHARDED = ("mla_w_out", "s5_w_glu", "s5_w_out")
VEC_SHARDED = ("s5_d", "s5_b_glu")
BIG = COL_SHARDED + ROW_SHARDED
L0_BIG = ("mla_w_in", "mla_w_uq", "mla_w_ukv")
L1_BIG = ("s5_w_in", "s5_w_glu", "s5_w_out", "mla_w_out")
BITS16 = jnp.bfloat16
SMALL_RS = ("norm_g", "mla_q_norm", "mla_kv_norm", "s5_a_re", "s5_a_im", "s5_log_step", "s5_b_re", "s5_b_im",
            "s5_c_re", "s5_c_im", "final_g")
CHUNKED = ("s5_b_re", "s5_b_im", "s5_c_re", "s5_c_im")
DENSE = ("s5_b_re", "s5_b_im")
TINY = ("norm_g", "mla_q_norm", "mla_kv_norm", "s5_a_re", "s5_a_im", "s5_log_step", "final_g")
ORDER = ("c_ctx", "ada_w", "ada_b", "norm_g", "mla_w_in", "mla_q_norm", "mla_w_uq", "mla_kv_norm", "mla_w_ukv",
         "mla_w_out", "s5_w_in", "s5_a_re", "s5_a_im", "s5_log_step", "s5_b_re", "s5_b_im", "s5_c_re", "s5_c_im",
         "s5_d", "s5_w_glu", "s5_b_glu", "s5_w_out", "final_g")


def kernel(x, c, ctx, c_ctx, ada_w, ada_b, norm_g, mla_w_in, mla_q_norm, mla_w_uq, mla_kv_norm, mla_w_ukv, mla_w_out, s5_w_in, s5_a_re, s5_a_im, s5_log_step, s5_b_re, s5_b_im, s5_c_re, s5_c_im, s5_d, s5_w_glu, s5_b_glu, s5_w_out, final_g, loss_target, m_c_ctx, m_ada_w, m_ada_b, m_norm_g, m_mla_w_in, m_mla_q_norm, m_mla_w_uq, m_mla_kv_norm, m_mla_w_ukv, m_mla_w_out, m_s5_w_in, m_s5_a_re, m_s5_a_im, m_s5_log_step, m_s5_b_re, m_s5_b_im, m_s5_c_re, m_s5_c_im, m_s5_d, m_s5_w_glu, m_s5_b_glu, m_s5_w_out, m_final_g, v_c_ctx, v_ada_w, v_ada_b, v_norm_g, v_mla_w_in, v_mla_q_norm, v_mla_w_uq, v_mla_kv_norm, v_mla_w_ukv, v_mla_w_out, v_s5_w_in, v_s5_a_re, v_s5_a_im, v_s5_log_step, v_s5_b_re, v_s5_b_im, v_s5_c_re, v_s5_c_im, v_s5_d, v_s5_w_glu, v_s5_b_glu, v_s5_w_out, v_final_g):
    w = dict(c_ctx=c_ctx, ada_w=ada_w, ada_b=ada_b, norm_g=norm_g, mla_w_in=mla_w_in, mla_q_norm=mla_q_norm,
             mla_w_uq=mla_w_uq, mla_kv_norm=mla_kv_norm, mla_w_ukv=mla_w_ukv, mla_w_out=mla_w_out, s5_w_in=s5_w_in,
             s5_a_re=s5_a_re, s5_a_im=s5_a_im, s5_log_step=s5_log_step, s5_b_re=s5_b_re, s5_b_im=s5_b_im,
             s5_c_re=s5_c_re, s5_c_im=s5_c_im, s5_d=s5_d, s5_w_glu=s5_w_glu, s5_b_glu=s5_b_glu, s5_w_out=s5_w_out,
             final_g=final_g)
    m = dict(c_ctx=m_c_ctx, ada_w=m_ada_w, ada_b=m_ada_b, norm_g=m_norm_g, mla_w_in=m_mla_w_in, mla_q_norm=m_mla_q_norm,
             mla_w_uq=m_mla_w_uq, mla_kv_norm=m_mla_kv_norm, mla_w_ukv=m_mla_w_ukv, mla_w_out=m_mla_w_out,
             s5_w_in=m_s5_w_in, s5_a_re=m_s5_a_re, s5_a_im=m_s5_a_im, s5_log_step=m_s5_log_step, s5_b_re=m_s5_b_re,
             s5_b_im=m_s5_b_im, s5_c_re=m_s5_c_re, s5_c_im=m_s5_c_im, s5_d=m_s5_d, s5_w_glu=m_s5_w_glu,
             s5_b_glu=m_s5_b_glu, s5_w_out=m_s5_w_out, final_g=m_final_g)
    v = dict(c_ctx=v_c_ctx, ada_w=v_ada_w, ada_b=v_ada_b, norm_g=v_norm_g, mla_w_in=v_mla_w_in, mla_q_norm=v_mla_q_norm,
             mla_w_uq=v_mla_w_uq, mla_kv_norm=v_mla_kv_norm, mla_w_ukv=v_mla_w_ukv, mla_w_out=v_mla_w_out,
             s5_w_in=v_s5_w_in, s5_a_re=v_s5_a_re, s5_a_im=v_s5_a_im, s5_log_step=v_s5_log_step, s5_b_re=v_s5_b_re,
             s5_b_im=v_s5_b_im, s5_c_re=v_s5_c_re, s5_c_im=v_s5_c_im, s5_d=v_s5_d, s5_w_glu=v_s5_w_glu,
             s5_b_glu=v_s5_b_glu, s5_w_out=v_s5_w_out, final_g=v_final_g)

    me = 4 * lax.axis_index("x") + 2 * lax.axis_index("y") + lax.axis_index("c")
    WA = ada_w.shape[2]

    def shard(n):
        return _t_shard(w[n], SHARD_ROWS[n]) if n in COL_SHARDED else w[n][0].astype(BF16)

    wgot = exchange([c] + [shard(n) for n in L0_BIG], "gather", "gather_w")

    cg = wgot[0].reshape(NDEV, D)
    cc2 = c_ctx.reshape(1, D)
    ada_b_loc = lax.dynamic_slice_in_dim(ada_b.reshape(2, 3 * D // WA, WA), me, 1, axis=1)
    part = ada_fwd(cg, cc2, ada_w, ada_b_loc, "ada_fwd")
    pg = exchange([part], "gather", "gather_mod")[0]
    mod_l = lax.dynamic_index_in_dim(pg, me, axis=2, keepdims=False).transpose(1, 0, 2).reshape(2, 3 * D)
    mod_c = pg[:, :, NDEV, :].transpose(1, 0, 2).reshape(2, 3 * D)
    mod = jnp.stack([mod_c, mod_l], axis=1)

    Wt = {n: a.reshape(-1, a.shape[-1]) for n, a in zip(L0_BIG, wgot[1:])}
    Wt["mla_w_in"] = mm(_win_order(), Wt["mla_w_in"], "nn", "w_in_order", out_dtype=BF16)
    vec_bits = lax.bitcast_convert_type(jnp.concatenate([s5_d, s5_b_glu], axis=0), BITS16).reshape(2, -1)
    small = {n: w[n] for n in SMALL_RS}

    lvec, grad_x, dmod, gbig, gsmall, l1_recv, (bb_all, cc_all) = local_step(
        ctx[0], x[0], loss_target[0], mod, Wt, small, [shard(n) for n in L1_BIG] + [vec_bits])
    grad_x = grad_x[None]

    recv = dict(zip(L1_BIG + VEC_SHARDED, l1_recv))
    out = {}

    def keep(n, res):
        for key, arr in zip("gdmv", res):
            out[key, n] = arr.reshape(w[n].shape)

    kshape = lambda n: w[n].shape if w[n].ndim > 1 else (1, w[n].size)
    flat = jnp.concatenate([gsmall[n].reshape(-1) for n in TINY] + [dmod.reshape(-1), lvec.reshape(-1)])[None]
    *l0_recv, flat_all = exchange([gbig[n] for n in L0_BIG] + [flat], ["lead"] * len(L0_BIG) + ["gather"], "scatter_grads")
    chunk_all = [bb_all[:, 0], bb_all[:, 1], cc_all[:, 0], cc_all[:, 1]]
    tiny_all, off = [], 0
    for n in TINY:
        tiny_all.append(flat_all[:, 0, off:off + w[n].size].reshape((NDEV,) + kshape(n)))
        off += w[n].size
    dm_all = flat_all[:, 0, off:off + dmod.size].reshape((NDEV,) + dmod.shape)
    loss = sum_slots([flat_all[:, :, off + dmod.size:]], "loss_sum")[0][0, 0]

    dm_cols = lax.dynamic_slice_in_dim(dm_all.reshape(NDEV, 2, 2, 3 * D // WA, WA), me, 1, axis=3)[:, :, :, 0]
    dm_loc = jnp.concatenate([dm_cols[:, :, 1].transpose(1, 0, 2), dm_cols[:, :, 0].transpose(1, 0, 2)], axis=1)
    g_ada_w, dcc_part, g_ada_b = ada_bwd(cg, cc2, ada_w, dm_loc, dm_all.transpose(0, 2, 1, 3).reshape(2 * NDEV, 2, 3 * D), "ada_bwd")
    dcc_all = exchange([dcc_part], "gather", "gather_dcc")[0].reshape(NDEV, D)
    g_c_ctx = cctx_finish(dcc_all, cc2, "cctx_finish")

    flat2 = lambda t: t.reshape(-1, t.shape[-1])
    recv.update(dict(zip(L0_BIG, l0_recv)))
    big = [(recv[n], w[n][0], m[n][0], v[n][0]) for n in BIG]
    big.append((flat2(g_ada_w)[None], flat2(ada_w), flat2(m_ada_w), flat2(v_ada_w)))
    for n, r in zip(BIG + ("ada_w",), adamw_rows(big, "adamw_big")[0]):
        keep(n, r)
    items = []
    halves = 2
    for n, g in zip(CHUNKED, chunk_all):
        blk = (1, 1, G // halves) + w[n].shape[3:]
        g = jnp.moveaxis(g, 0, 1).reshape(w[n].shape)
        g_spec = pl.BlockSpec((1,) + blk, lambda d, s: (0, 0, d, s, 0, 0))
        items.append((g[None], g_spec, w[n], m[n], v[n], pl.BlockSpec(blk, lambda d, s: (0, d, s, 0, 0))))
    for n, res in zip(CHUNKED, adamw_multi(items, (2, halves), "adamw_bc")):
        keep(n, res)
    tiny_g = dict(zip(TINY, tiny_all))
    tiny_g.update({n: recv[n] for n in VEC_SHARDED})
    tiny_g["c_ctx"], tiny_g["ada_b"] = g_c_ctx[None], g_ada_b[None]
    names = list(tiny_g)
    items = [(tiny_g[n], _whole(tiny_g[n], 1)) + tuple(t[n].reshape(kshape(n)) for t in (w, m, v))
             + (pl.BlockSpec(kshape(n), lambda i, r=len(kshape(n)): (0,) * r),) for n in names]
    for n, res in zip(names, adamw_multi(items, (1,), "adamw_small")):
        keep(n, res)

    return (loss, grad_x, *[out["g", n] for n in ORDER], *[out["d", n] for n in ORDER],
            *[out["m", n] for n in ORDER], *[out["v", n] for n in ORDER])
```

```python
import math

import numpy as np
import jax
import jax.numpy as jnp
from jax import lax
from jax.experimental import pallas as pl
from jax.experimental.pallas import tpu as pltpu

F32 = jnp.float32
BF16 = jnp.bfloat16

D = 1024
L = 2048
LC = 256
NDEV = 8
GRID_W = 64
EPS = 1e-6
HEADS = 16
NOPE = 64
ROPE = 32
QK = NOPE + ROPE
VD = 64
IN_W = 256 + 128 + ROPE + HEADS * 64
IN_WP = 1536
QL = 256
KVL = 128
SCALE = QK ** -0.5
LOG2E = math.log2(math.e)
THETA = 10000.0
G = 64
P = 64
CH = 16
GB = 8
NJ = G // GB
UB = GB * CH
SB = GB * P
SEG = 16
TB = 256
VMEM_LIMIT = 56 * 1024 * 1024
B1, B2, LR, AEPS, WD, STEP = 0.9, 0.999, 0.001, 1e-8, 0.01, 10
MESH_T = pl.DeviceIdType.MESH


def _cp(sem=None):
    return pltpu.CompilerParams(dimension_semantics=sem, vmem_limit_bytes=VMEM_LIMIT)


def _sig(x):
    return 1.0 / (1.0 + jnp.exp(-x))


def _silu(x):
    return x * _sig(x)


def _dsilu(x):
    s = _sig(x)
    return s * (1.0 + x * (1.0 - s))


_GK = math.sqrt(2.0 / math.pi)


def _gelu(x):
    return 0.5 * x * (1.0 + jnp.tanh(_GK * (x + 0.044715 * x * x * x)))


def _dgelu(x):
    t = jnp.tanh(_GK * (x + 0.044715 * x * x * x))
    return 0.5 * (1.0 + t) + 0.5 * x * (1.0 - t * t) * _GK * (1.0 + 3 * 0.044715 * x * x)


def _rs(x):
    return lax.rsqrt(jnp.mean(x * x, axis=-1, keepdims=True) + EPS)


def _sum0(x):
    return jnp.sum(x, axis=0, keepdims=True)


def st_norm_mod(x, g, sc, sh):
    y = x * _rs(x) * g
    return (y * (1.0 + sc) + sh,), ()


def st_norm_mod_bwd(x, dh, dres, g, sc):
    r = _rs(x)
    xn = x * r
    y = xn * g
    dy = dh * (1.0 + sc)
    dxn = dy * g
    dx = r * (dxn - xn * jnp.mean(dxn * xn, axis=-1, keepdims=True))
    return (dres + dx,), (_sum0(dh), _sum0(dh * y), _sum0(dy * xn))


def st_rms(x, g):
    return (x * _rs(x) * g,), ()


def st_rms_bwd(x, dy, g):
    r = _rs(x)
    n = x * r
    dn = dy * g
    dx = r * (dn - n * jnp.mean(dn * n, axis=-1, keepdims=True))
    return (dx,), (_sum0(dy * n),)


def st_rms2(x1, x2, g1, g2):
    return st_rms(x1, g1)[0] + st_rms(x2, g2)[0], ()


def st_rms2_bwd(x1, dy1, x2, dy2, g1, g2):
    (d1,), (s1,) = st_rms_bwd(x1, dy1, g1)
    (d2,), (s2,) = st_rms_bwd(x2, dy2, g2)
    return (d1, d2), (s1, s2)


def st_gate_bwd(dog, o, z):
    return (dog * _silu(z), dog * o * _dsilu(z)), ()


def st_resid_bwd(dx, out, gt):
    return (dx * gt,), (_sum0(dx * out),)


def st_s5a(yssm, u, d):
    y = yssm + d * u
    return (y, _gelu(y)), ()


def st_s5b_bwd(dy3, y, gl, z, b):
    y1 = _gelu(y)
    s = _sig(gl + b)
    dy2 = dy3 * _silu(z)
    dz = dy3 * y1 * s * _dsilu(z)
    dgl = dy2 * y1 * s * (1.0 - s)
    return (dgl, dz, dy2 * s), (_sum0(dgl),)


def st_s5a_bwd(dy1a, dy1b, y, u, d):
    dy = (dy1a + dy1b) * _dgelu(y)
    return (dy, dy * d), (_sum0(dy * u),)


def st_l0_pre(x, g, sc, sh, qg, kvg, w_in):
    hb = st_norm_mod(x, g, sc, sh)[0][0].astype(BF16)
    p = lax.dot_general(hb, w_in, _DN["nt"], preferred_element_type=F32)
    cq, ckv = p[:, HEADS * VD:HEADS * VD + QL], p[:, HEADS * VD + QL:HEADS * VD + QL + KVL]
    return (hb, p) + st_rms2(cq, ckv, qg, kvg)[0], ()


def st_l0_tail_bwd(dq, dkv, dkr, dz, cq, ckv, cqn, ckvn, h, x, dres, qg, kvg, g, sc, w_uq, w_ukv, w_in):
    dcqn = jnp.dot(dq, w_uq, preferred_element_type=F32)
    dckvn = jnp.dot(dkv, w_ukv, preferred_element_type=F32)
    (dcq, dckv), (dqg, dkvg) = st_rms2_bwd(cq, dcqn, ckv, dckvn, qg, kvg)
    dp = jnp.concatenate([dz, dcq, dckv, dkr], axis=1).astype(BF16)
    dh = jnp.dot(dp, w_in, preferred_element_type=F32)
    outs, sums = st_norm_mod_bwd(x, dh, dres, g, sc)
    tn = lambda a, b: lax.dot_general(a, b, _DN["tn"], preferred_element_type=F32)
    return outs, (dqg, dkvg) + sums, (tn(cqn, dq), tn(ckvn, dkv), tn(h, dp))


def st_l1_pre(x, g, sc, sh, w_in):
    hb = st_norm_mod(x, g, sc, sh)[0][0].astype(BF16)
    return (hb, lax.dot_general(hb, w_in, _DN["nt"], preferred_element_type=F32)), ()


def st_l1_tail_bwd(du_a, du_b, dz, h, x, dres, g, sc, w_in):
    dp = jnp.concatenate([(du_a + du_b).astype(BF16), dz], axis=1)
    dh = jnp.dot(dp, w_in, preferred_element_type=F32)
    outs, sums = st_norm_mod_bwd(x, dh, dres, g, sc)
    w = dp.shape[1] // NDEV
    shards = [lax.dot_general(h, dp[:, r * w:(r + 1) * w], _DN["tn"], preferred_element_type=F32) for r in range(NDEV)]
    return outs, sums, (jnp.stack(shards),)


def st_l0_post(o, z, x, gt, w_out):
    og = (o * _silu(z)).astype(BF16)
    out = jnp.dot(og, w_out, preferred_element_type=F32)
    return (og, out, x + gt * out), ()


def st_l0_post_bwd(dx1, out, og, o, z, gt, w_out):
    (dout,), (dgt,) = st_resid_bwd(dx1, out.astype(F32), gt)
    doutb = dout.astype(BF16)
    dog = lax.dot_general(doutb, w_out, _DN["nt"], preferred_element_type=F32)
    return st_gate_bwd(dog, o, z)[0], (dgt,), (lax.dot_general(og, doutb, _DN["tn"], preferred_element_type=F32),)


def st_l1_mlp(yssm, u, z, x1, tgt, d, bglu, gt, fg, mask, w_glu, w_out):
    (y, y1), _ = st_s5a(yssm, u, d)
    y1b = y1.astype(BF16)
    gl = jnp.dot(y1b, w_glu, preferred_element_type=F32)
    y3 = (y1 * _sig(gl + bglu) * _silu(z)).astype(BF16)
    out = jnp.dot(y3, w_out, preferred_element_type=F32)
    (dx2,), sums = st_final(x1 + gt * out, tgt, fg, mask)
    return (y, y1b, gl, y3, out, dx2), sums


def st_l1_mlp_bwd(dx2, out, y3, y, gl, z, u, y1b, gt, bglu, d, w_out, w_glu):
    out, gl = out.astype(F32), gl.astype(F32)
    (dout,), (dgt,) = st_resid_bwd(dx2, out, gt)
    doutb = dout.astype(BF16)
    dy3 = lax.dot_general(doutb, w_out, _DN["nt"], preferred_element_type=F32)
    (dgl, dz, dy1a), (dbglu,) = st_s5b_bwd(dy3, y, gl, z, bglu)
    dglb = dgl.astype(BF16)
    dy1b = lax.dot_general(dglb, w_glu, _DN["nt"], preferred_element_type=F32)
    (dy, du), (dd,) = st_s5a_bwd(dy1a, dy1b, y, u, d)
    g_w_out = lax.dot_general(y3, doutb, _DN["tn"], preferred_element_type=F32)
    g_w_glu = lax.dot_general(y1b, dglb, _DN["tn"], preferred_element_type=F32)
    return (dz, dy, du), (dgt, dbglu, dd), (g_w_out, g_w_glu)


def st_final(x2, tgt, g, mask):
    r = _rs(x2)
    n = x2 * r
    e = n * g - tgt
    dyo = e * (1.0 / D)
    dn = dyo * g
    dx = r * (dn - n * jnp.mean(dn * n, axis=-1, keepdims=True))
    lsum = jnp.sum(_sum0(e * e), axis=1, keepdims=True) * (0.5 / D)
    return (dx * mask,), (_sum0(dyo * n), jnp.broadcast_to(lsum, (1, 128)))


def rowwise(fn, rows, vecs, out_rows, out_sums, name, mats=(), out_accs=()):
    lat_blk = lambda i: jnp.maximum(i - 1, 0)
    arrays, in_specs, pick = [], [], []
    for a in rows:
        if not isinstance(a, tuple):
            a = (a, 0, a.shape[1])
        tag = a[0] if isinstance(a[0], str) else None
        if tag == "cat":
            _, ctx, x = a
            arrays += [ctx, x]
            in_specs += [pl.BlockSpec((TB, ctx.shape[1]), lambda i: (0, 0)),
                         pl.BlockSpec((TB, x.shape[1]), lambda i: (lat_blk(i), 0))]
            pick.append(2)
        elif tag == "lat":
            arrays.append(a[1])
            in_specs.append(pl.BlockSpec((TB, a[1].shape[1]), lambda i: (lat_blk(i), 0)))
            pick.append(1)
        else:
            arr, cb, width = a
            arrays.append(arr)
            in_specs.append(pl.BlockSpec((TB, width), lambda i, cb=cb: (i, cb)))
            pick.append(1)
    T = LC + L
    nin, nv, nm, no, ns = len(arrays), len(vecs), len(mats), len(out_rows), len(out_sums)

    def body(*refs):
        i = pl.program_id(0)
        vals, k = [], 0
        for p in pick:
            if p == 2:
                vals.append(jnp.where(i == 0, refs[k][...], refs[k + 1][...]))
            else:
                vals.append(refs[k][...])
            k += p
        vals += [r[0] for r in refs[nin:nin + nv]] + [r[...] for r in refs[nin + nv:nin + nv + nm]]
        res = fn(*vals)
        first_out = nin + nv + nm
        for r, o in zip(refs[first_out:first_out + no], res[0]):
            r[...] = o.astype(r.dtype)
        sum_refs = refs[first_out + no:first_out + no + ns]
        if sum_refs:
            @pl.when(i <= 1)
            def _():
                for r in sum_refs:
                    r[...] = jnp.zeros_like(r)
            for r, s in zip(sum_refs, res[1]):
                r[0] += s
        na = len(out_accs)
        if na:
            acc_out, acc = refs[first_out + no + ns:first_out + no + ns + na], refs[first_out + no + ns + na:]

            @pl.when(i == 0)
            def _():
                for r in acc:
                    r[...] = jnp.zeros_like(r)
            for r, a in zip(acc, res[2]):
                r[...] += a

            @pl.when(i == T // TB - 1)
            def _():
                for o, r in zip(acc_out, acc):
                    o[...] = r[...].astype(o.dtype)

    kind = lambda i: (jnp.minimum(i, 1), 0, 0)
    in_specs += [pl.BlockSpec((1, 1, v.shape[2]), kind) for v in vecs]
    in_specs += [pl.BlockSpec(m.shape, lambda i: (0, 0), pipeline_mode=pl.Buffered(1)) for m in mats]
    out_specs, out_shape = [], []
    for o in out_rows:
        lat = len(o) == 3
        out_specs.append(pl.BlockSpec((TB, o[0]), (lambda i: (lat_blk(i), 0)) if lat else (lambda i: (i, 0))))
        out_shape.append(jax.ShapeDtypeStruct((L if lat else T, o[0]), o[1]))
    out_specs += [pl.BlockSpec((1, 1, c), kind) for c in out_sums]
    out_shape += [jax.ShapeDtypeStruct((2, 1, c), F32) for c in out_sums]
    out_specs += [pl.BlockSpec(s, lambda i, r=len(s): (0,) * r) for s in out_accs]
    out_shape += [jax.ShapeDtypeStruct(s, BF16) for s in out_accs]
    res = pl.pallas_call(body, grid=(T // TB,), in_specs=in_specs, out_specs=out_specs, out_shape=out_shape,
                         scratch_shapes=[pltpu.VMEM(s, F32) for s in out_accs],
                         compiler_params=_cp(("arbitrary",)), name=name)(*arrays, *vecs, *mats)
    if out_accs:
        return res[:no], res[no:no + ns], res[no + ns:]
    return res[:no], res[no:]


_DN = {"nn": (((1,), (0,)), ((), ())), "nt": (((1,), (1,)), ((), ())), "tn": (((0,), (0,)), ((), ()))}


def mm(a, b, mode, name, out_dtype=F32, tm=None, tn=None, rode=None, modes=None):
    if mode == "nn":
        (M, K), (_, N) = a.shape, b.shape
    elif mode == "nt":
        (M, K), (N, _) = a.shape, b.shape
    else:
        (K, M), (_, N) = a.shape, b.shape
    if tm is None:
        tm = next((t for t in (768, 512, 256) if M % t == 0 and M > t), M)
    tn = N if tn is None else tn
    dn = _DN[mode]

    def body(a_ref, b_ref, o_ref):
        o_ref[...] = lax.dot_general(a_ref[...].astype(BF16), b_ref[...].astype(BF16), dn,
                                     preferred_element_type=F32).astype(o_ref.dtype)

    a_spec = pl.BlockSpec((K, tm), lambda i, j: (0, i)) if mode == "tn" else pl.BlockSpec((tm, K), lambda i, j: (i, 0))
    b_spec = pl.BlockSpec((tn, K), lambda i, j: (j, 0)) if mode == "nt" else pl.BlockSpec((K, tn), lambda i, j: (0, j))
    (prod,), got = _ride_call(body, (M // tm, N // tn), [a_spec, b_spec], [pl.BlockSpec((tm, tn), lambda i, j: (i, j))],
                              [jax.ShapeDtypeStruct((M, N), out_dtype)], Exchange(rode, modes) if rode else None, rode,
                              name, (a, b))
    return prod, got


def _rope_tables(T, width=QK, first=NOPE):
    nlat = T - LC
    pos = np.arange(nlat)
    row, col = pos // GRID_W, pos % GRID_W
    half = ROPE // 2
    inv = 1.0 / (THETA ** (np.arange(0, half, 2, dtype=np.float64) / half))
    cosf = np.ones((T, width), np.float64)
    sinf = np.zeros((T, width), np.float64)
    perm = np.zeros((width, width), np.float32)
    for m in range(ROPE):
        j = first + m
        blk, w = m // half, m % half
        ang = (row if blk == 0 else col)[:, None] * inv[None, :]
        f = w % (half // 2)
        cosf[LC:, j] = np.cos(ang[:, f])
        if w < half // 2:
            sinf[LC:, j] = -np.sin(ang[:, f])
            perm[j + half // 2, j] = 1.0
        else:
            sinf[LC:, j] = np.sin(ang[:, f])
            perm[j - half // 2, j] = 1.0
    return jnp.asarray(cosf, F32), jnp.asarray(sinf, F32), jnp.asarray(perm, BF16), jnp.asarray(perm.T, BF16)


def _exact_perm(x, pm):
    hi = x.astype(BF16)
    r1 = x - hi.astype(F32)
    mid = r1.astype(BF16)
    lo = (r1 - mid.astype(F32)).astype(BF16)
    dot = lambda a: jnp.dot(a, pm, preferred_element_type=F32)
    return dot(hi) + dot(mid) + dot(lo)


def _rot(x, cv, sv, pv, inverse):
    if inverse:
        return x * cv + _exact_perm(x * sv, pv)
    return x * cv + _exact_perm(x, pv) * sv


def rope_bwd(dx, cosf, sinf, pmt, scale, name):
    H, T, _ = dx.shape

    def body(x_ref, c_ref, s_ref, p_ref, o_ref):
        cv, sv, pv = c_ref[...], s_ref[...], p_ref[...]
        for h in range(H):
            o_ref[:, pl.ds(h * QK, QK)] = (_rot(x_ref[h], cv, sv, pv, True) * scale).astype(o_ref.dtype)

    return pl.pallas_call(
        body, grid=(T // TB,),
        in_specs=[pl.BlockSpec((H, TB, QK), lambda i: (0, i, 0)), pl.BlockSpec((TB, QK), lambda i: (i, 0)),
                  pl.BlockSpec((TB, QK), lambda i: (i, 0)), pl.BlockSpec((QK, QK), lambda i: (0, 0))],
        out_specs=pl.BlockSpec((TB, H * QK), lambda i: (i, 0)), out_shape=jax.ShapeDtypeStruct((T, H * QK), BF16),
        compiler_params=_cp(("parallel",)), name=name)(dx, cosf, sinf, pmt)


KVW = NOPE + VD


def project_q(cqn, w, name):
    T = cqn.shape[0]
    cosf, sinf, _, _ = _rope_tables(T, 128, NOPE)
    wp = jnp.pad(w.reshape(HEADS, QK, QL), ((0, 0), (0, 128 - QK), (0, 0))).reshape(HEADS * 128, QL)

    def body(a_ref, w_ref, c_ref, s_ref, o_ref):
        a, cv, sv = a_ref[...], c_ref[...], s_ref[...]
        first_of_pair = lax.bitwise_and(lax.broadcasted_iota(jnp.int32, (TB, 128), 1), ROPE // 4) == 0
        for h in range(HEADS):
            qh = _dotf(a, w_ref[pl.ds(h * 128, 128), :], "nt")
            swap = jnp.where(first_of_pair, pltpu.roll(qh, 128 - ROPE // 4, 1), pltpu.roll(qh, ROPE // 4, 1))
            o_ref[h] = ((qh * cv + swap * sv) * (SCALE * LOG2E))[:, :QK].astype(BF16)

    rows = lambda c: pl.BlockSpec((TB, c), lambda i: (i, 0))
    return pl.pallas_call(
        body, grid=(T // TB,), in_specs=[rows(QL), pl.BlockSpec(wp.shape, lambda i: (0, 0)), rows(128), rows(128)],
        out_specs=pl.BlockSpec((HEADS, TB, QK), lambda i: (0, i, 0)), out_shape=jax.ShapeDtypeStruct((HEADS, T, QK), BF16),
        compiler_params=_cp(("parallel",)), name=name)(cqn, wp, cosf, sinf)


def project_kv(ckvn, w, p0, kr_block, name):
    T = ckvn.shape[0]
    assert KVW == 128 and NOPE == VD
    cosf, sinf, pm, _ = _rope_tables(T, 128, 0)

    def body(a_ref, w_ref, kr_ref, c_ref, s_ref, p_ref, k_ref, v_ref):
        a = a_ref[...]
        is_nope = lax.broadcasted_iota(jnp.int32, (TB, KVW), 1) < NOPE
        kr_at = pltpu.roll(_rot(kr_ref[...], c_ref[...], s_ref[...], p_ref[...], False), NOPE, 1)
        for h in range(HEADS):
            kv = _dotf(a, w_ref[pl.ds(h * KVW, KVW), :], "nt")
            k_ref[h] = jnp.where(is_nope, kv, kr_at)[:, :QK].astype(BF16)
            v_ref[h] = pltpu.roll(kv, VD, 1)[:, :VD].astype(BF16)

    rows = lambda c: pl.BlockSpec((TB, c), lambda i: (i, 0))
    const = lambda x: pl.BlockSpec(x.shape, lambda i: (0, 0))
    return pl.pallas_call(
        body, grid=(T // TB,),
        in_specs=[rows(KVL), const(w), pl.BlockSpec((TB, 128), lambda i: (i, kr_block)), rows(128), rows(128), const(pm)],
        out_specs=[pl.BlockSpec((HEADS, TB, QK), lambda i: (0, i, 0)), pl.BlockSpec((HEADS, TB, VD), lambda i: (0, i, 0))],
        out_shape=[jax.ShapeDtypeStruct((HEADS, T, QK), BF16), jax.ShapeDtypeStruct((HEADS, T, VD), BF16)],
        compiler_params=_cp(("parallel",)), name=name)(ckvn, w, p0, cosf, sinf, pm)


def split_kv_grads(dk, dv, name, rode=None, modes=None):
    H, T, _ = dk.shape
    cosf, sinf, _, pmt = _rope_tables(T, 128, 0)
    to_rope_block = np.zeros((QK, 128), np.float32)
    to_rope_block[NOPE + np.arange(ROPE), np.arange(ROPE)] = 1.0
    to_rope_block = jnp.asarray(to_rope_block, BF16)

    def body(dk_ref, dv_ref, c_ref, s_ref, p_ref, sel_ref, dkv_ref, dkr_ref):
        total = None
        for h in range(H):
            dkh = dk_ref[h] * (1.0 / LOG2E)
            total = dkh if total is None else total + dkh
            dkv_ref[:, pl.ds(h * KVW, NOPE)] = dkh[:, :NOPE].astype(BF16)
            dkv_ref[:, pl.ds(h * KVW + NOPE, VD)] = dv_ref[h].astype(BF16)
        dkr_ref[...] = _rot(_exact_perm(total, sel_ref[...]), c_ref[...], s_ref[...], p_ref[...], True)

    rows = lambda c: pl.BlockSpec((TB, c), lambda i, j: (i, 0))
    const = lambda a: pl.BlockSpec(a.shape, lambda i, j: (0, 0))
    return _ride_call(
        body, (T // TB, 1),
        [pl.BlockSpec((H, TB, QK), lambda i, j: (0, i, 0)), pl.BlockSpec((H, TB, VD), lambda i, j: (0, i, 0)),
         rows(128), rows(128), const(pmt), const(to_rope_block)],
        [rows(H * KVW), rows(128)],
        [jax.ShapeDtypeStruct((T, H * KVW), BF16), jax.ShapeDtypeStruct((T, 128), F32)],
        Exchange(rode, modes) if rode else None, rode, name, (dk, dv, cosf, sinf, pmt, to_rope_block))


HB = 4
HBF = 8


def _by_query_block(run, T):
    @pl.when(pl.program_id(1) == 0)
    def _():
        run(LC)

    @pl.when(pl.program_id(1) > 0)
    def _():
        run(T)


def _with_rider(body, nin, nout, ride, grid):
    if ride is None:
        return body
    n = ride.n

    def wrapped(*refs):
        ins, xs = refs[:nin], refs[nin:nin + n]
        outs, got = refs[nin + n:nin + n + nout], refs[nin + n + nout:nin + 2 * n + nout]
        sems = refs[nin + 2 * n + nout:]
        step = pl.program_id(0) * grid[1] + pl.program_id(1)

        @pl.when(step == 0)
        def _():
            ride.start(xs, got, sems)

        body(*ins, *outs)

        @pl.when(step == grid[0] * grid[1] - 1)
        def _():
            ride.finish(xs, got, sems)

    return wrapped


def _ride_call(body, grid, in_specs, out_specs, out_shape, ride, rode, name, args):
    if ride is None:
        return pl.pallas_call(body, grid=grid, in_specs=in_specs, out_specs=out_specs, out_shape=out_shape,
                              compiler_params=_cp(("parallel", "arbitrary")), name=name)(*args), []
    res = pl.pallas_call(
        _with_rider(body, len(in_specs), len(out_specs), ride, grid), grid=grid,
        in_specs=in_specs + ride.specs, out_specs=out_specs + ride.specs, out_shape=out_shape + ride.out_shape,
        scratch_shapes=ride.scratch,
        compiler_params=pltpu.CompilerParams(dimension_semantics=("arbitrary", "arbitrary"), vmem_limit_bytes=VMEM_LIMIT,
                                             has_side_effects=True), name=name)(*args, *rode)
    return res[:len(out_specs)], res[len(out_specs):]


def attn_fwd(q, k, v, name, rode=None, modes=None):
    H, T, _ = q.shape

    def body(q_ref, k_ref, v_ref, o_ref, lse_ref):
        def run(nk):
            for hh in range(HBF):
                s = _dotf(q_ref[hh], k_ref[hh, pl.ds(0, nk), :], "nt")
                m = jnp.max(s, axis=1, keepdims=True)
                p = jnp.exp2(s - m)
                l = jnp.sum(p, axis=1, keepdims=True)
                o = jnp.dot(p.astype(BF16), v_ref[hh, pl.ds(0, nk), :], preferred_element_type=F32)
                o_ref[:, pl.ds(hh * VD, VD)] = o / l
                lse_ref[hh] = m + jnp.log2(l)

        _by_query_block(run, T)

    return _ride_call(
        body, (H // HBF, T // TB),
        [pl.BlockSpec((HBF, TB, QK), lambda h, i: (h, i, 0)), pl.BlockSpec((HBF, T, QK), lambda h, i: (h, 0, 0)),
         pl.BlockSpec((HBF, T, VD), lambda h, i: (h, 0, 0))],
        [pl.BlockSpec((TB, HBF * VD), lambda h, i: (i, h)), pl.BlockSpec((HBF, TB, 1), lambda h, i: (h, i, 0))],
        [jax.ShapeDtypeStruct((T, H * VD), F32), jax.ShapeDtypeStruct((H, T, 1), F32)],
        Exchange(rode, modes) if rode else None, rode, name, (q, k, v))


def attn_bwd(q, k, v, o, lse, do, name, rode=None, modes=None):
    H, T, _ = q.shape

    def body(q_ref, k_ref, v_ref, o_ref, lse_ref, do_ref, dq_ref, dk_ref, dv_ref):
        i = pl.program_id(1)

        @pl.when(i == 0)
        def _():
            dk_ref[...] = jnp.zeros_like(dk_ref)
            dv_ref[...] = jnp.zeros_like(dv_ref)

        def run(nk):
            keys = pl.ds(0, nk)
            for hh in range(HB):
                qv, kv, dov = q_ref[hh], k_ref[hh, keys, :], do_ref[:, pl.ds(hh * VD, VD)]
                p = jnp.exp2(_dotf(qv, kv, "nt") - lse_ref[hh])
                delta = jnp.sum(dov * o_ref[:, pl.ds(hh * VD, VD)], axis=1, keepdims=True)
                dob = dov.astype(BF16)
                dv_ref[hh, keys, :] += _dotf(p.astype(BF16), dob, "tn")
                dp = _dotf(dob, v_ref[hh, keys, :], "nt")
                ds = (p * (dp - delta)).astype(BF16)
                dq_ref[hh] = jnp.dot(ds, kv, preferred_element_type=F32)
                dk_ref[hh, keys, :] += _dotf(ds, qv, "tn")

        _by_query_block(run, T)

    blk = lambda c: pl.BlockSpec((HB, TB, c), lambda h, i: (h, i, 0))
    full = lambda c: pl.BlockSpec((HB, T, c), lambda h, i: (h, 0, 0))
    tok = pl.BlockSpec((TB, HB * VD), lambda h, i: (i, h))
    return _ride_call(
        body, (H // HB, T // TB), [blk(QK), full(QK), full(VD), tok, blk(1), tok], [blk(QK), full(QK), full(VD)],
        [jax.ShapeDtypeStruct((H, T, QK), F32), jax.ShapeDtypeStruct((H, T, QK), F32), jax.ShapeDtypeStruct((H, T, VD), F32)],
        Exchange(rode, modes) if rode else None, rode, name, (q, k, v, o, lse, do))


def disc_fwd(a_re, a_im, ls, name):
    def body(ar_ref, ai_ref, ls_ref, lr_ref, li_ref, fr_ref, fi_ref):
        ar, ai = ar_ref[...], ai_ref[...]
        dt = jnp.exp(ls_ref[...])
        mag = jnp.exp(ar * dt)
        lr = mag * jnp.cos(ai * dt)
        li = mag * jnp.sin(ai * dt)
        den = ar * ar + ai * ai
        nr = lr - 1.0
        lr_ref[...] = lr
        li_ref[...] = li
        fr_ref[...] = (nr * ar + li * ai) / den
        fi_ref[...] = (li * ar - nr * ai) / den

    return pl.pallas_call(body, out_shape=[jax.ShapeDtypeStruct(a_re.shape, F32)] * 4, name=name)(a_re, a_im, ls)


def disc_b(f_re, f_im, b_re, b_im, name):
    def body(fr_ref, fi_ref, br_ref, bi_ref, or_ref, oi_ref):
        fr, fi, br, bi = fr_ref[...], fi_ref[...], br_ref[...], bi_ref[...]
        or_ref[...] = fr * br - fi * bi
        oi_ref[...] = fr * bi + fi * br

    return pl.pallas_call(body, out_shape=[jax.ShapeDtypeStruct(b_re.shape, F32)] * 2, compiler_params=_cp(),
                          name=name)(f_re, f_im, b_re, b_im)


def disc_b_bwd(f_re, f_im, b_re, b_im, dbb_re, dbb_im, name):
    def body(fr_ref, fi_ref, br_ref, bi_ref, dr_ref, di_ref, dbr_ref, dbi_ref, dfr_ref, dfi_ref):
        fr, fi, br, bi, dr, di = fr_ref[...], fi_ref[...], br_ref[...], bi_ref[...], dr_ref[...], di_ref[...]
        dbr_ref[...] = fr * dr + fi * di
        dbi_ref[...] = fr * di - fi * dr
        dfr_ref[...] = jnp.sum(dr * br + di * bi, axis=2, keepdims=True)
        dfi_ref[...] = jnp.sum(di * br - dr * bi, axis=2, keepdims=True)

    return pl.pallas_call(body, out_shape=[jax.ShapeDtypeStruct(b_re.shape, F32)] * 2 + [jax.ShapeDtypeStruct(f_re.shape, F32)] * 2,
                          compiler_params=_cp(), name=name)(f_re, f_im, b_re, b_im, dbb_re, dbb_im)


def disc_a_bwd(a_re, a_im, ls, dlr, dli, dfr, dfi, name):
    def body(ar_ref, ai_ref, ls_ref, dlr_ref, dli_ref, dfr_ref, dfi_ref, dar_ref, dai_ref, dls_ref):
        ar, ai = ar_ref[...], ai_ref[...]
        dt = jnp.exp(ls_ref[...])
        mag = jnp.exp(ar * dt)
        cs, sn = jnp.cos(ai * dt), jnp.sin(ai * dt)
        lr, li = mag * cs, mag * sn
        den = ar * ar + ai * ai
        nr = lr - 1.0
        f_re = (nr * ar + li * ai) / den
        f_im = (li * ar - nr * ai) / den
        dn1 = dfr_ref[...] / den
        dn2 = dfi_ref[...] / den
        dden = -(dfr_ref[...] * f_re + dfi_ref[...] * f_im) / den
        dlr_t = dlr_ref[...] + dn1 * ar - dn2 * ai
        dli_t = dli_ref[...] + dn1 * ai + dn2 * ar
        dar = dn1 * nr + dn2 * li + dden * 2.0 * ar
        dai = dn1 * li - dn2 * nr + dden * 2.0 * ai
        dmag = dlr_t * cs + dli_t * sn
        dth = dli_t * lr - dlr_t * li
        dar_ref[...] = dar + dmag * mag * dt
        dai_ref[...] = dai + dth * dt
        dls_ref[...] = jnp.sum(dmag * mag * ar + dth * ai, axis=-1, keepdims=True) * dt

    return pl.pallas_call(body, out_shape=[jax.ShapeDtypeStruct(a_re.shape, F32)] * 2 +
                          [jax.ShapeDtypeStruct(ls.shape, F32)], name=name)(a_re, a_im, ls, dlr, dli, dfr, dfi)


def _cpow(lr, li, n):
    rr, ri = None, None
    br, bi = lr, li
    while n:
        if n & 1:
            if rr is None:
                rr, ri = br, bi
            else:
                rr, ri = rr * br - ri * bi, rr * bi + ri * br
        n >>= 1
        if n:
            br, bi = br * br - bi * bi, 2.0 * br * bi
    return rr, ri


UNROLL = 4


def _steps(trips, fn, init):
    main = trips // UNROLL

    def body(i, c):
        for j in range(UNROLL):
            c = fn(i * UNROLL + j, c)
        return c

    c = lax.fori_loop(0, main, body, init) if main else init
    for n in range(main * UNROLL, trips):
        c = fn(n, c)
    return c


def _seg_scan(xre, xim, lam8, pw, base, seglen, rev, init, fin_re, fin_im, ini_re, ini_im, prev=None):
    lr, li = lam8
    nsub = SEG // 8

    def rows(t, s):
        first = base + t * SEG + 8 * s
        return pl.ds(first if isinstance(first, int) else pl.multiple_of(first, 8), 8)

    tmap = (lambda n: seglen - 1 - n) if rev else (lambda n: n)
    zeros = tuple(jnp.zeros((8, SB), F32) for _ in range(2 * nsub))

    def advance(c, t):
        out = []
        for s in range(nsub):
            a, b = c[2 * s], c[2 * s + 1]
            out += [lr * a - li * b + xre[rows(t, s), :], lr * b + li * a + xim[rows(t, s), :]]
        return tuple(out)

    fin = _steps(seglen, lambda n, c: advance(c, tmap(n)), zeros)
    for s in range(nsub):
        fin_re[pl.ds(8 * s, 8), :] = fin[2 * s]
        fin_im[pl.ds(8 * s, 8), :] = fin[2 * s + 1]
    (cr, ci), (pr, pi) = init, pw
    for i in (range(SEG - 1, -1, -1) if rev else range(SEG)):
        ini_re[pl.ds(i, 1), :] = cr
        ini_im[pl.ds(i, 1), :] = ci
        cr, ci = pr * cr - pi * ci + fin_re[pl.ds(i, 1), :], pr * ci + pi * cr + fin_im[pl.ds(i, 1), :]
    tiles = lambda re, im: tuple(r[pl.ds(8 * s, 8), :] for s in range(nsub) for r in (re, im))
    start = tiles(ini_re, ini_im)

    def store(c, t):
        new = advance(c, t)
        for s in range(nsub):
            xre[rows(t, s), :] = new[2 * s]
            xim[rows(t, s), :] = new[2 * s + 1]
        return new

    if prev is None:
        _steps(seglen, lambda n, c: store(c, tmap(n)), start)
        return (cr, ci), None

    sre, sim, s_ini_re, s_ini_im = prev

    def acc_step(c, t, before):
        new = store(c[:2 * nsub], t)
        acc = []
        for s in range(nsub):
            (na, nb), (pre, pim) = new[2 * s:2 * s + 2], before[2 * s:2 * s + 2]
            acc += [c[2 * nsub + 2 * s] + na * pre + nb * pim, c[2 * nsub + 2 * s + 1] + nb * pre - na * pim]
        return new + tuple(acc)

    def body(n, c):
        t = tmap(n)
        tp = t - 1 if rev else t + 1
        return acc_step(c, t, tuple(r[rows(tp, s), :] for s in range(nsub) for r in (sre, sim)))

    c = _steps(seglen - 1, body, start + zeros)
    c = acc_step(c, 0 if rev else seglen - 1, tiles(s_ini_re, s_ini_im))
    acc = c[2 * nsub:]
    return (cr, ci), (sum(acc[0::2][1:], acc[0]), sum(acc[1::2][1:], acc[1]))


def _lam_tiles(lr, li, lens, conj=False):
    if conj:
        li = -li
    lam8 = (jnp.broadcast_to(lr, (8, SB)), jnp.broadcast_to(li, (8, SB)))
    return lam8, [_cpow(lr, li, n) for n in lens]


def _stretches(T):
    return ((0, LC // SEG), (LC, (T - LC) // SEG))


def _to_seg_order(src, dst, T):
    for base, seglen in _stretches(T):
        def body(t, carry, base=base, seglen=seglen):
            dst[pl.ds(pl.multiple_of(base + t * SEG, SEG), SEG), :] = src[pl.ds(base + t, SEG, stride=seglen), :]
            return carry
        lax.fori_loop(0, seglen, body, 0, unroll=8)


def _from_seg_order(src, dst, T):
    for base, seglen in _stretches(T):
        def body(t, carry, base=base, seglen=seglen):
            dst[pl.ds(base + t, SEG, stride=seglen), :] = src[pl.ds(pl.multiple_of(base + t * SEG, SEG), SEG), :]
            return carry
        lax.fori_loop(0, seglen, body, 0, unroll=8)


def _scan_specs(T):
    ublk = pl.BlockSpec((T, UB), lambda j: (0, j))
    lam = pl.BlockSpec((2, 1, 1, SB), lambda j: (0, j, 0, 0))
    mat = pl.BlockSpec((2, 1, UB, P), lambda j: (0, j, 0, 0))
    return ublk, lam, mat


def _dotf(a, b, mode="nn"):
    return lax.dot_general(a, b, _DN[mode], preferred_element_type=F32)


def _diag_mask():
    r = lax.broadcasted_iota(jnp.int32, (UB, SB), 0)
    c = lax.broadcasted_iota(jnp.int32, (UB, SB), 1)
    return lax.shift_right_logical(r, int(math.log2(CH))) == lax.shift_right_logical(c, int(math.log2(P)))


def _expand(m):
    p = lax.broadcasted_iota(jnp.int32, (P, SB), 0)
    c = lax.broadcasted_iota(jnp.int32, (P, SB), 1)
    tile = jnp.where(lax.bitwise_and(c, P - 1) == p, 1.0, 0.0).astype(BF16)
    wide = jnp.dot(m.astype(BF16), tile, preferred_element_type=F32)
    return jnp.where(_diag_mask(), wide, 0.0).astype(BF16)


def _collapse(full):
    c = lax.broadcasted_iota(jnp.int32, (SB, P), 0)
    p = lax.broadcasted_iota(jnp.int32, (SB, P), 1)
    pick = jnp.where(lax.bitwise_and(c, P - 1) == p, 1.0, 0.0).astype(BF16)
    return _exact_perm(jnp.where(_diag_mask(), full, 0.0), pick)


def _zero_state():
    return jnp.zeros((1, SB), F32), jnp.zeros((1, SB), F32)


def scan_fwd(u, lam_re, lam_im, bre, bim, cre, cim, name):
    T = u.shape[0]
    s_ctx, s_lat = LC // SEG, (T - LC) // SEG

    def body(u_ref, lr_ref, li_ref, bre_ref, bim_ref, cre_ref, cim_ref, y_ref, us, ys, sre, sim, fre, fim, ire, iim):
        _to_seg_order(u_ref, us, T)
        ub = us[...].astype(BF16)
        for d in range(2):
            lam8, (pw_c, pw_l) = _lam_tiles(lr_ref[d, 0], li_ref[d, 0], (s_ctx, s_lat))
            sre[...] = _dotf(ub, _expand(bre_ref[d, 0]))
            sim[...] = _dotf(ub, _expand(bim_ref[d, 0]))
            end_c, _ = _seg_scan(sre, sim, lam8, pw_c, 0, s_ctx, bool(d), _zero_state(), fre, fim, ire, iim)
            _seg_scan(sre, sim, lam8, pw_l, LC, s_lat, bool(d), end_c, fre, fim, ire, iim)
            y = (_dotf(sre[...].astype(BF16), _expand(cre_ref[d, 0]), "nt")
                 - _dotf(sim[...].astype(BF16), _expand(cim_ref[d, 0]), "nt"))
            if d == 0:
                ys[...] = y
            else:
                ys[...] += y
        _from_seg_order(ys, y_ref, T)

    ublk, lam, mat = _scan_specs(T)
    return pl.pallas_call(
        body, grid=(NJ,), in_specs=[ublk, lam, lam, mat, mat, mat, mat], out_specs=ublk,
        out_shape=jax.ShapeDtypeStruct((T, G * CH), F32),
        scratch_shapes=[pltpu.VMEM((T, UB), F32)] * 2 + [pltpu.VMEM((T, SB), F32)] * 2 + [pltpu.VMEM((SEG, SB), F32)] * 4,
        compiler_params=_cp(("arbitrary",)), name=name)(u, lam_re, lam_im, bre, bim, cre, cim)


def scan_bwd(u, dy, lam_re, lam_im, bre, bim, cre, cim, name):
    T = u.shape[0]
    s_ctx, s_lat = LC // SEG, (T - LC) // SEG

    def body(u_ref, dy_ref, lr_ref, li_ref, bre_ref, bim_ref, cre_ref, cim_ref,
             du_ref, dlr_ref, dli_ref, dbre_ref, dbim_ref, dcre_ref, dcim_ref,
             us, dys, dus, sre, sim, gre, gim, fre, fim, ic_re, ic_im, il_re, il_im, jre, jim):
        _to_seg_order(u_ref, us, T)
        _to_seg_order(dy_ref, dys, T)
        ub, dyb = us[...].astype(BF16), dys[...].astype(BF16)
        for d in range(2):
            rev = bool(d)
            lam8, (pw_c, pw_l) = _lam_tiles(lr_ref[d, 0], li_ref[d, 0], (s_ctx, s_lat))
            cam8, (cw_c, cw_l) = _lam_tiles(lr_ref[d, 0], li_ref[d, 0], (s_ctx, s_lat), conj=True)
            bre_v, bim_v = _expand(bre_ref[d, 0]), _expand(bim_ref[d, 0])
            sre[...] = _dotf(ub, bre_v)
            sim[...] = _dotf(ub, bim_v)
            end_c, _ = _seg_scan(sre, sim, lam8, pw_c, 0, s_ctx, rev, _zero_state(), fre, fim, ic_re, ic_im)
            _seg_scan(sre, sim, lam8, pw_l, LC, s_lat, rev, end_c, fre, fim, il_re, il_im)
            gre[...] = _dotf(dyb, _expand(cre_ref[d, 0]))
            gim[...] = -_dotf(dyb, _expand(cim_ref[d, 0]))
            end_g, acc_l = _seg_scan(gre, gim, cam8, cw_l, LC, s_lat, not rev, _zero_state(), fre, fim, jre, jim,
                                     prev=(sre, sim, il_re, il_im))
            _, acc_c = _seg_scan(gre, gim, cam8, cw_c, 0, s_ctx, not rev, end_g, fre, fim, jre, jim,
                                 prev=(sre, sim, ic_re, ic_im))
            dlr_ref[d, 0] = _sum0(acc_l[0] + acc_c[0])
            dli_ref[d, 0] = _sum0(acc_l[1] + acc_c[1])
            grb, gib = gre[...].astype(BF16), gim[...].astype(BF16)
            du = _dotf(grb, bre_v, "nt") + _dotf(gib, bim_v, "nt")
            if d == 0:
                dus[...] = du
            else:
                dus[...] += du
            dbre_ref[d, 0] = _collapse(_dotf(ub, grb, "tn"))
            dbim_ref[d, 0] = _collapse(_dotf(ub, gib, "tn"))
            dcre_ref[d, 0] = _collapse(_dotf(dyb, sre[...].astype(BF16), "tn"))
            dcim_ref[d, 0] = -_collapse(_dotf(dyb, sim[...].astype(BF16), "tn"))
        _from_seg_order(dus, du_ref, T)

    ublk, lam, mat = _scan_specs(T)
    lam_s = jax.ShapeDtypeStruct(lam_re.shape, F32)
    mat_s = jax.ShapeDtypeStruct(bre.shape, F32)
    return pl.pallas_call(
        body, grid=(NJ,), in_specs=[ublk, ublk, lam, lam, mat, mat, mat, mat],
        out_specs=[ublk, lam, lam, mat, mat, mat, mat],
        out_shape=[jax.ShapeDtypeStruct((T, G * CH), F32), lam_s, lam_s, mat_s, mat_s, mat_s, mat_s],
        scratch_shapes=[pltpu.VMEM((T, UB), F32)] * 3 + [pltpu.VMEM((T, SB), F32)] * 4 + [pltpu.VMEM((SEG, SB), F32)] * 8,
        compiler_params=_cp(("arbitrary",)), name=name)(u, dy, lam_re, lam_im, bre, bim, cre, cim)


class Exchange:
    def __init__(self, xs, modes):
        self.n = len(xs)
        self.modes = [modes] * self.n if isinstance(modes, (str, int)) else list(modes)
        self.out_shape = [jax.ShapeDtypeStruct(self._shape(x, md), x.dtype) for x, md in zip(xs, self.modes)]
        self.scratch = [pltpu.SemaphoreType.DMA((NDEV - 1, self.n)), pltpu.SemaphoreType.DMA((NDEV - 1, self.n)),
                        pltpu.SemaphoreType.DMA((self.n,))]
        self.specs = [pl.BlockSpec(memory_space=pl.ANY)] * self.n

    @staticmethod
    def _shape(x, mode):
        if mode == "gather":
            return (NDEV,) + tuple(x.shape)
        return tuple(x.shape) if mode == "lead" else (NDEV, x.shape[0], mode) + tuple(x.shape[2:])

    @staticmethod
    def _piece(x_ref, mode, dev):
        if mode == "gather":
            return x_ref
        return x_ref.at[dev] if mode == "lead" else x_ref.at[:, pl.ds(dev * mode, mode)]

    def _copies(self, x_refs, out_refs, sems):
        send_sems, recv_sems, local_sems = sems
        mx, my, mc = lax.axis_index("x"), lax.axis_index("y"), lax.axis_index("c")
        me = 4 * mx + 2 * my + mc
        peer_of = lambda k: (1 - mx if k & 4 else mx, 1 - my if k & 2 else my, 1 - mc if k & 1 else mc)
        local, first, relay, arrivals = [], [], [], []
        for a, (x_ref, out_ref) in enumerate(zip(x_refs, out_refs)):
            mode = self.modes[a]
            local.append(pltpu.make_async_copy(self._piece(x_ref, mode, me), out_ref.at[me], local_sems.at[a]))

            def remote(src, dst, k, pair, a=a):
                return pltpu.make_async_remote_copy(src_ref=src, dst_ref=dst, send_sem=send_sems.at[pair, a],
                                                    recv_sem=recv_sems.at[pair, a], device_id=peer_of(k), device_id_type=MESH_T)

            for k in range(1, NDEV):
                peer = peer_of(k)
                pid = 4 * peer[0] + 2 * peer[1] + peer[2]
                if mode != "gather":
                    src = self._piece(x_ref, mode, pid)
                    first.append(remote(src, out_ref.at[me], k, k - 1))
                    arrivals.append(remote(src, out_ref.at[pid], k, k - 1))
                elif k == 1:
                    first.append(remote(x_ref, out_ref.at[me], k, k - 1))
                    arrivals.append(remote(x_ref, out_ref.at[pid], k, k - 1))
                elif k % 2 == 0:
                    first.append(remote(x_ref, out_ref.at[me], k, k - 1))
                    relay.append((remote(x_ref, out_ref.at[pid], k, k - 1), remote(out_ref.at[pid], out_ref.at[pid], 1, k)))
                else:
                    arrivals.append(remote(x_ref, out_ref.at[pid], 1, k - 1))
        return local, first, relay, arrivals

    def start(self, x_refs, out_refs, sems):
        local, first, _, _ = self._copies(x_refs, out_refs, sems)
        for cp in local + first:
            cp.start()

    def finish(self, x_refs, out_refs, sems):
        local, first, relay, arrivals = self._copies(x_refs, out_refs, sems)
        for arrival, onward in relay:
            arrival.wait_recv()
            onward.start()
        for cp in arrivals:
            cp.wait_recv()
        for cp in first + [onward for _, onward in relay]:
            cp.wait_send()
        for cp in local:
            cp.wait()


def exchange(xs, modes, name):
    ex = Exchange(xs, modes)
    n = ex.n

    def body(*refs):
        ex.start(refs[:n], refs[n:2 * n], refs[2 * n:])
        ex.finish(refs[:n], refs[n:2 * n], refs[2 * n:])

    return pl.pallas_call(body, in_specs=ex.specs, out_specs=ex.specs, out_shape=ex.out_shape, scratch_shapes=ex.scratch,
                          compiler_params=pltpu.CompilerParams(has_side_effects=True), name=name)(*xs)


def _dot_f32(a, b, dn):
    return lax.dot_general(a, b, dn, preferred_element_type=F32, precision=lax.Precision.HIGHEST)


def ada_fwd(cg, c_ctx, ada_w, ada_b_loc, name):
    W = ada_w.shape[2]

    def body(cg_ref, cc_ref, w_ref, b_ref, o_ref):
        a = jnp.concatenate([_silu(cg_ref[...]), jnp.broadcast_to(_silu(cc_ref[...]), (NDEV, D))], axis=0)
        for i in range(2):
            o_ref[i] = _dot_f32(a, w_ref[i], _DN["nn"]) + b_ref[i]

    return pl.pallas_call(body, out_shape=jax.ShapeDtypeStruct((2, 2 * NDEV, W), F32),
                          compiler_params=_cp(), name=name)(cg, c_ctx, ada_w, ada_b_loc)


def ada_bwd(cg, c_ctx, ada_w, dm_loc, dm_all, name):
    W = ada_w.shape[2]

    def body(cg_ref, cc_ref, w_ref, dl_ref, da_ref, gw_ref, dcc_ref, gb_ref):
        a = jnp.concatenate([_silu(cg_ref[...]), jnp.broadcast_to(_silu(cc_ref[...]), (NDEV, D))], axis=0)
        dcc = jnp.zeros((1, D), F32)
        for i in range(2):
            dl = dl_ref[i]
            gw_ref[i] = _dot_f32(a, dl, _DN["tn"])
            dctx = jnp.sum(dl[NDEV:], axis=0, keepdims=True)
            dcc = dcc + _dot_f32(dctx, w_ref[i], _DN["nt"])
        dcc_ref[...] = dcc
        gb_ref[...] = jnp.sum(da_ref[...], axis=0)

    return pl.pallas_call(body, out_shape=[jax.ShapeDtypeStruct((2, D, W), F32), jax.ShapeDtypeStruct((1, D), F32),
                                           jax.ShapeDtypeStruct((2, 3 * D), F32)],
                          compiler_params=_cp(), name=name)(cg, c_ctx, ada_w, dm_loc, dm_all)


def cctx_finish(parts, c_ctx, name):
    def body(p_ref, cc_ref, o_ref):
        o_ref[...] = jnp.sum(p_ref[...], axis=0, keepdims=True) * _dsilu(cc_ref[...])

    return pl.pallas_call(body, out_shape=jax.ShapeDtypeStruct((1, D), F32), name=name)(parts, c_ctx)


def _adamw_update(g_ref, w_ref, m_ref, v_ref, go_ref, d_ref, mo_ref, vo_ref):
    g = g_ref[0].astype(F32)
    for s in range(1, g_ref.shape[0]):
        g = g + g_ref[s].astype(F32)
    mn = B1 * m_ref[...] + (1.0 - B1) * g
    vn = B2 * v_ref[...] + (1.0 - B2) * g * g
    go_ref[...] = g
    mo_ref[...] = mn
    vo_ref[...] = vn
    d_ref[...] = -LR * ((mn * (1.0 / (1.0 - B1 ** STEP))) / (jnp.sqrt(vn * (1.0 / (1.0 - B2 ** STEP))) + AEPS) + WD * w_ref[...])


ADAMW_PARTS = 4


def adamw_rows(items, name, rode=None, modes=None):
    in_specs, out_specs, out_shape, args = [], [], [], []
    for g, w, m, v in items:
        n, R, C = g.shape
        tr = R // ADAMW_PARTS
        spec = pl.BlockSpec((tr, C), lambda i, j: (i, 0))
        in_specs += [pl.BlockSpec((n, tr, C), lambda i, j: (0, i, 0)), spec, spec, spec]
        args += [g, w, m, v]
    for g, w, m, v in items:
        tr = w.shape[0] // ADAMW_PARTS
        out_specs += [pl.BlockSpec((tr, w.shape[1]), lambda i, j: (i, 0))] * 4
        out_shape += [jax.ShapeDtypeStruct(w.shape, F32)] * 4
    res, got = _ride_call(_adamw_body(len(items)), (ADAMW_PARTS, 1), in_specs, out_specs, out_shape,
                          Exchange(rode, modes) if rode else None, rode, name, args)
    return [res[4 * t:4 * t + 4] for t in range(len(items))], got


def _adamw_body(k):
    def body(*refs):
        for t in range(k):
            _adamw_update(*refs[4 * t:4 * t + 4], *refs[4 * k + 4 * t:4 * k + 4 * t + 4])
    return body


def adamw_multi(items, grid, name):
    k = len(items)
    ins, in_specs, out_specs, out_shape = [], [], [], []
    for g, g_spec, w, m, v, w_spec in items:
        ins += [g, w, m, v]
        in_specs += [g_spec, w_spec, w_spec, w_spec]
    for g, g_spec, w, m, v, w_spec in items:
        out_specs += [w_spec] * 4
        out_shape += [jax.ShapeDtypeStruct(w.shape, F32)] * 4
    res = pl.pallas_call(_adamw_body(k), grid=grid, in_specs=in_specs, out_specs=out_specs, out_shape=out_shape,
                         compiler_params=_cp(("arbitrary",) * len(grid)), name=name)(*ins)
    return [res[4 * t:4 * t + 4] for t in range(k)]


def _whole(a, grid_rank):
    zeros = (0,) * a.ndim
    return pl.BlockSpec(a.shape, lambda *idx: zeros)


def sum_slots(xs, name):
    def body(*refs):
        for x_ref, o_ref in zip(refs[:len(xs)], refs[len(xs):]):
            acc = x_ref[0]
            for s in range(1, NDEV):
                acc = acc + x_ref[s]
            o_ref[...] = acc

    return pl.pallas_call(body, out_shape=[jax.ShapeDtypeStruct(x.shape[1:], F32) for x in xs],
                          compiler_params=_cp(), name=name)(*xs)


def _col_shards(g):
    R, N = g.shape
    return g.reshape(R, NDEV, N // NDEV).transpose(1, 0, 2)


def _vec2(v):
    return jnp.broadcast_to(v.reshape(1, 1, -1), (2, 1, v.size))


SHARD_ROWS = {"mla_w_in": 192, "mla_w_uq": 192, "mla_w_ukv": 256, "s5_w_in": 256}


def _t_shard(wsh, rows):
    t = wsh[0].T.astype(BF16)
    return jnp.pad(t, ((0, rows - t.shape[0]), (0, 0)))


def _win_order():
    w = IN_W // NDEV
    perm = np.zeros((IN_WP, NDEV * SHARD_ROWS["mla_w_in"]), np.float32)
    first = QL + KVL + ROPE
    for c in range(IN_W):
        n = c + HEADS * VD if c < first else c - first
        perm[n, (c // w) * SHARD_ROWS["mla_w_in"] + c % w] = 1.0
    return jnp.asarray(perm, BF16)


def local_step(ctx, x, tgt, mod, Wt, small, l1_shards):
    T = LC + x.shape[0]
    xa = ("cat", ctx, x)
    sh = [mod[i, :, None, 0:D] for i in range(2)]
    sc = [mod[i, :, None, D:2 * D] for i in range(2)]
    gt = [mod[i, :, None, 2 * D:] for i in range(2)]
    ng = [_vec2(small["norm_g"][i]) for i in range(2)]
    qg, kvg = _vec2(small["mla_q_norm"]), _vec2(small["mla_kv_norm"])
    cosf, sinf, _, pmt = _rope_tables(T)

    (h0, p0, cqn, ckvn), _ = rowwise(st_l0_pre, [xa], [ng[0], sc[0], sh[0], qg, kvg],
                                     [(D, BF16), (IN_WP, F32), (QL, BF16), (KVL, BF16)], [], "l0_pre", mats=[Wt["mla_w_in"]])
    z0, cq, ckv = (p0, 0, HEADS * VD), (p0, HEADS * VD // QL, QL), (p0, (HEADS * VD + QL) // KVL, KVL)
    Q = project_q(cqn, Wt["mla_w_uq"], "l0_uq")
    K, V = project_kv(ckvn, Wt["mla_w_ukv"], p0, (HEADS * VD + QL + KVL) // 128, "l0_ukv")
    (o2, lse), got = attn_fwd(Q, K, V, "l0_attn", rode=l1_shards, modes="gather")
    Wt, small = dict(Wt), dict(small)
    for n, a in zip(L1_BIG, got):
        Wt[n] = a.reshape(-1, a.shape[-1])
    vecs = lax.bitcast_convert_type(got[-1].reshape(NDEV, 2, -1, 2), F32)
    small["s5_d"], small["s5_b_glu"] = vecs[:, 0, :].reshape(D), vecs[:, 1, :].reshape(D)
    (og, out0, x1), _ = rowwise(st_l0_post, [o2, z0, xa], [gt[0]], [(D, BF16), (D, BF16), (D, F32)], [], "l0_post",
                                mats=[Wt["mla_w_out"]])

    ls = small["s5_log_step"].reshape(2, G, 1)
    a_re, a_im = small["s5_a_re"].reshape(2, G, P), small["s5_a_im"].reshape(2, G, P)
    b_re = small["s5_b_re"].reshape(2, G, P, CH).transpose(0, 1, 3, 2)
    b_im = small["s5_b_im"].reshape(2, G, P, CH).transpose(0, 1, 3, 2)
    lam_re, lam_im, f_re, f_im = disc_fwd(a_re, a_im, ls, "s5_disc")
    f_re2, f_im2 = f_re.reshape(2, G, 1, P), f_im.reshape(2, G, 1, P)
    bb_re, bb_im = disc_b(f_re2, f_im2, b_re, b_im, "s5_disc_b")
    compact = lambda m: m.reshape(2, NJ, UB, P)
    bre, bim = compact(bb_re), compact(bb_im)
    cre, cim = compact(small["s5_c_re"]), compact(small["s5_c_im"])
    lam_re4, lam_im4 = lam_re.reshape(2, NJ, 1, SB), lam_im.reshape(2, NJ, 1, SB)

    (h1, p1), _ = rowwise(st_l1_pre, [x1], [ng[1], sc[1], sh[1]], [(D, BF16), (2 * D, F32)], [], "l1_pre", mats=[Wt["s5_w_in"]])
    u, z1 = (p1, 0, D), (p1, 1, D)
    yssm = scan_fwd(p1, lam_re4, lam_im4, bre, bim, cre, cim, "s5_scan")
    dvec, bglu = _vec2(small["s5_d"]), _vec2(small["s5_b_glu"])
    fg = _vec2(small["final_g"])
    lat_mask = jnp.stack([jnp.zeros((1, D), F32), jnp.ones((1, D), F32)])
    (y, y1b, gl, y3, out1, dx2), (dfg, lvec) = rowwise(
        st_l1_mlp, [yssm, u, z1, x1, ("lat", tgt)], [dvec, bglu, gt[1], fg, lat_mask],
        [(D, F32), (D, BF16), (D, BF16), (D, BF16), (D, BF16), (D, F32)], [D, 128], "l1_mlp",
        mats=[Wt["s5_w_glu"], Wt["s5_w_out"]])

    (dz1, dy, du_d), (dgt1, dbglu, dd), (g_w_out5, g_w_glu) = rowwise(
        st_l1_mlp_bwd, [dx2, out1, y3, y, gl, z1, u, y1b], [gt[1], bglu, dvec], [(D, BF16), (D, F32), (D, F32)], [D, D, D],
        "l1_mlp_b", mats=[Wt["s5_w_out"], Wt["s5_w_glu"]], out_accs=[(D, D), (D, D)])
    du_s, dlr, dli, dbre, dbim, dcre, dcim = scan_bwd(p1, dy, lam_re4, lam_im4, bre, bim, cre, cim, "s5_scan_b")
    dbb_re, dbb_im = dbre.reshape(2, G, CH, P), dbim.reshape(2, G, CH, P)
    g_c_re, g_c_im = dcre.reshape(2, G, CH, P), dcim.reshape(2, G, CH, P)
    gt_b_re, gt_b_im, dfr, dfi = disc_b_bwd(f_re2, f_im2, b_re, b_im, dbb_re, dbb_im, "s5_disc_b_b")
    g_b_re, g_b_im = gt_b_re.transpose(0, 1, 3, 2), gt_b_im.transpose(0, 1, 3, 2)
    g_a_re, g_a_im, g_ls = disc_a_bwd(a_re, a_im, ls, dlr.reshape(2, G, P), dli.reshape(2, G, P),
                                      dfr.reshape(2, G, P), dfi.reshape(2, G, P), "s5_disc_b_a")
    (dx1,), (dsh1, dsc1, dng1), (g_w_in5,) = rowwise(
        st_l1_tail_bwd, [du_d, du_s, dz1, h1, x1, dx2], [ng[1], sc[1]], [(D, F32)], [D, D, D], "l1_pre_b",
        mats=[Wt["s5_w_in"]], out_accs=[(NDEV, D, 2 * D // NDEV)])

    (do2, dz0), (dgt0,), (g_w_out,) = rowwise(st_l0_post_bwd, [dx1, out0, og, o2, z0], [gt[0]], [(D, F32), (D, F32)], [D],
                                              "l0_post_b", mats=[Wt["mla_w_out"]], out_accs=[(D, D)])
    rows8 = lambda g: g.reshape(NDEV, -1, g.shape[-1])
    both = lambda s: s[0, 0] + s[1, 0]
    dense = lambda g: g.reshape(2, G * P * CH // 128, 128)
    chunks = [dense(g_b_re), dense(g_b_im), g_c_re, g_c_im]
    l1_send = [g_w_in5, rows8(g_w_glu), rows8(g_w_out5), rows8(g_w_out),
               both(dd).reshape(NDEV, 1, -1), both(dbglu).reshape(NDEV, 1, -1)]
    (dQ, dK, dV), l1_recv = attn_bwd(Q, K, V, o2, lse, do2, "l0_attn_b", rode=l1_send + chunks,
                                     modes=["lead"] * len(l1_send) + [a.shape[1] // NDEV for a in chunks])
    dq = rope_bwd(dQ, cosf, sinf, pmt, SCALE, "l0_rope_q_b")
    n_owned = len(l1_send)
    reduced = sum_slots(l1_recv[n_owned:], "sum_chunks")
    (dkv, dkr), chunk_all = split_kv_grads(dK, dV, "l0_kv_b", rode=[jnp.stack(reduced[:2]), jnp.stack(reduced[2:])],
                                           modes="gather")
    (grad_x,), (dqg, dkvg, dsh0, dsc0, dng0), (g_uq, g_ukv, g_p) = rowwise(
        st_l0_tail_bwd, [dq, dkv, dkr, dz0, cq, ckv, cqn, ckvn, h0, xa, dx1], [qg, kvg, ng[0], sc[0]],
        [(D, F32, "lat")], [QL, KVL, D, D, D], "l0_pre_b", mats=[Wt["mla_w_uq"], Wt["mla_w_ukv"], Wt["mla_w_in"]],
        out_accs=[(QL, HEADS * QK), (KVL, HEADS * KVW), (D, IN_WP)])
    g_w_uq, g_w_ukv = _col_shards(g_uq).astype(BF16), _col_shards(g_ukv).astype(BF16)
    g_w_in = _col_shards(jnp.concatenate([g_p[:, HEADS * VD:IN_W], g_p[:, :HEADS * VD]], axis=1)).astype(BF16)

    dmod = jnp.stack([jnp.concatenate([dsh0, dsc0, dgt0], axis=-1)[:, 0], jnp.concatenate([dsh1, dsc1, dgt1], axis=-1)[:, 0]])
    gbig = {"mla_w_in": g_w_in, "mla_w_uq": g_w_uq, "mla_w_ukv": g_w_ukv}
    gsmall = {"norm_g": jnp.stack([both(dng0), both(dng1)]), "mla_q_norm": both(dqg), "mla_kv_norm": both(dkvg),
              "s5_a_re": g_a_re, "s5_a_im": g_a_im, "s5_log_step": g_ls, "final_g": dfg[1, 0]}
    return lvec[1], grad_x, dmod, gbig, gsmall, l1_recv[:n_owned], chunk_all


COL_SHARDED = ("mla_w_in", "mla_w_uq", "mla_w_ukv", "s5_w_in")
ROW_SHARDED = ("mla_w_out", "s5_w_glu", "s5_w_out")
VEC_SHARDED = ("s5_d", "s5_b_glu")
BIG = COL_SHARDED + ROW_SHARDED
L0_BIG = ("mla_w_in", "mla_w_uq", "mla_w_ukv")
L1_BIG = ("s5_w_in", "s5_w_glu", "s5_w_out", "mla_w_out")
BITS16 = jnp.bfloat16
SMALL_RS = ("norm_g", "mla_q_norm", "mla_kv_norm", "s5_a_re", "s5_a_im", "s5_log_step", "s5_b_re", "s5_b_im",
            "s5_c_re", "s5_c_im", "final_g")
CHUNKED = ("s5_b_re", "s5_b_im", "s5_c_re", "s5_c_im")
DENSE = ("s5_b_re", "s5_b_im")
TINY = ("norm_g", "mla_q_norm", "mla_kv_norm", "s5_a_re", "s5_a_im", "s5_log_step", "final_g")
ORDER = ("c_ctx", "ada_w", "ada_b", "norm_g", "mla_w_in", "mla_q_norm", "mla_w_uq", "mla_kv_norm", "mla_w_ukv",
         "mla_w_out", "s5_w_in", "s5_a_re", "s5_a_im", "s5_log_step", "s5_b_re", "s5_b_im", "s5_c_re", "s5_c_im",
         "s5_d", "s5_w_glu", "s5_b_glu", "s5_w_out", "final_g")


def kernel(x, c, ctx, c_ctx, ada_w, ada_b, norm_g, mla_w_in, mla_q_norm, mla_w_uq, mla_kv_norm, mla_w_ukv, mla_w_out, s5_w_in, s5_a_re, s5_a_im, s5_log_step, s5_b_re, s5_b_im, s5_c_re, s5_c_im, s5_d, s5_w_glu, s5_b_glu, s5_w_out, final_g, loss_target, m_c_ctx, m_ada_w, m_ada_b, m_norm_g, m_mla_w_in, m_mla_q_norm, m_mla_w_uq, m_mla_kv_norm, m_mla_w_ukv, m_mla_w_out, m_s5_w_in, m_s5_a_re, m_s5_a_im, m_s5_log_step, m_s5_b_re, m_s5_b_im, m_s5_c_re, m_s5_c_im, m_s5_d, m_s5_w_glu, m_s5_b_glu, m_s5_w_out, m_final_g, v_c_ctx, v_ada_w, v_ada_b, v_norm_g, v_mla_w_in, v_mla_q_norm, v_mla_w_uq, v_mla_kv_norm, v_mla_w_ukv, v_mla_w_out, v_s5_w_in, v_s5_a_re, v_s5_a_im, v_s5_log_step, v_s5_b_re, v_s5_b_im, v_s5_c_re, v_s5_c_im, v_s5_d, v_s5_w_glu, v_s5_b_glu, v_s5_w_out, v_final_g):
    w = dict(c_ctx=c_ctx, ada_w=ada_w, ada_b=ada_b, norm_g=norm_g, mla_w_in=mla_w_in, mla_q_norm=mla_q_norm,
             mla_w_uq=mla_w_uq, mla_kv_norm=mla_kv_norm, mla_w_ukv=mla_w_ukv, mla_w_out=mla_w_out, s5_w_in=s5_w_in,
             s5_a_re=s5_a_re, s5_a_im=s5_a_im, s5_log_step=s5_log_step, s5_b_re=s5_b_re, s5_b_im=s5_b_im,
             s5_c_re=s5_c_re, s5_c_im=s5_c_im, s5_d=s5_d, s5_w_glu=s5_w_glu, s5_b_glu=s5_b_glu, s5_w_out=s5_w_out,
             final_g=final_g)
    m = dict(c_ctx=m_c_ctx, ada_w=m_ada_w, ada_b=m_ada_b, norm_g=m_norm_g, mla_w_in=m_mla_w_in, mla_q_norm=m_mla_q_norm,
             mla_w_uq=m_mla_w_uq, mla_kv_norm=m_mla_kv_norm, mla_w_ukv=m_mla_w_ukv, mla_w_out=m_mla_w_out,
             s5_w_in=m_s5_w_in, s5_a_re=m_s5_a_re, s5_a_im=m_s5_a_im, s5_log_step=m_s5_log_step, s5_b_re=m_s5_b_re,
             s5_b_im=m_s5_b_im, s5_c_re=m_s5_c_re, s5_c_im=m_s5_c_im, s5_d=m_s5_d, s5_w_glu=m_s5_w_glu,
             s5_b_glu=m_s5_b_glu, s5_w_out=m_s5_w_out, final_g=m_final_g)
    v = dict(c_ctx=v_c_ctx, ada_w=v_ada_w, ada_b=v_ada_b, norm_g=v_norm_g, mla_w_in=v_mla_w_in, mla_q_norm=v_mla_q_norm,
             mla_w_uq=v_mla_w_uq, mla_kv_norm=v_mla_kv_norm, mla_w_ukv=v_mla_w_ukv, mla_w_out=v_mla_w_out,
             s5_w_in=v_s5_w_in, s5_a_re=v_s5_a_re, s5_a_im=v_s5_a_im, s5_log_step=v_s5_log_step, s5_b_re=v_s5_b_re,
             s5_b_im=v_s5_b_im, s5_c_re=v_s5_c_re, s5_c_im=v_s5_c_im, s5_d=v_s5_d, s5_w_glu=v_s5_w_glu,
             s5_b_glu=v_s5_b_glu, s5_w_out=v_s5_w_out, final_g=v_final_g)

    me = 4 * lax.axis_index("x") + 2 * lax.axis_index("y") + lax.axis_index("c")
    WA = ada_w.shape[2]

    def shard(n):
        return _t_shard(w[n], SHARD_ROWS[n]) if n in COL_SHARDED else w[n][0].astype(BF16)

    wgot = exchange([c] + [shard(n) for n in L0_BIG], "gather", "gather_w")

    cg = wgot[0].reshape(NDEV, D)
    cc2 = c_ctx.reshape(1, D)
    ada_b_loc = lax.dynamic_slice_in_dim(ada_b.reshape(2, 3 * D // WA, WA), me, 1, axis=1)
    part = ada_fwd(cg, cc2, ada_w, ada_b_loc, "ada_fwd")
    Wt = {n: a.reshape(-1, a.shape[-1]) for n, a in zip(L0_BIG, wgot[1:])}
    Wt["mla_w_in"], (pg,) = mm(_win_order(), Wt["mla_w_in"], "nn", "w_in_order", out_dtype=BF16, rode=[part], modes="gather")
    mod_l = lax.dynamic_index_in_dim(pg, me, axis=2, keepdims=False).transpose(1, 0, 2).reshape(2, 3 * D)
    mod_c = pg[:, :, NDEV, :].transpose(1, 0, 2).reshape(2, 3 * D)
    mod = jnp.stack([mod_c, mod_l], axis=1)
    vec_bits = lax.bitcast_convert_type(jnp.concatenate([s5_d, s5_b_glu], axis=0), BITS16).reshape(2, -1)
    small = {n: w[n] for n in SMALL_RS}

    lvec, grad_x, dmod, gbig, gsmall, l1_recv, (bb_all, cc_all) = local_step(
        ctx[0], x[0], loss_target[0], mod, Wt, small, [shard(n) for n in L1_BIG] + [vec_bits])
    grad_x = grad_x[None]

    recv = dict(zip(L1_BIG + VEC_SHARDED, l1_recv))
    out = {}

    def keep(n, res):
        for key, arr in zip("gdmv", res):
            out[key, n] = arr.reshape(w[n].shape)

    kshape = lambda n: w[n].shape if w[n].ndim > 1 else (1, w[n].size)
    flat = jnp.concatenate([gsmall[n].reshape(-1) for n in TINY] + [dmod.reshape(-1), lvec.reshape(-1)])[None]
    *l0_recv, flat_all = exchange([gbig[n] for n in L0_BIG] + [flat], ["lead"] * len(L0_BIG) + ["gather"], "scatter_grads")
    chunk_all = [bb_all[:, 0], bb_all[:, 1], cc_all[:, 0], cc_all[:, 1]]
    tiny_all, off = [], 0
    for n in TINY:
        tiny_all.append(flat_all[:, 0, off:off + w[n].size].reshape((NDEV,) + kshape(n)))
        off += w[n].size
    dm_all = flat_all[:, 0, off:off + dmod.size].reshape((NDEV,) + dmod.shape)
    loss = sum_slots([flat_all[:, :, off + dmod.size:]], "loss_sum")[0][0, 0]

    dm_cols = lax.dynamic_slice_in_dim(dm_all.reshape(NDEV, 2, 2, 3 * D // WA, WA), me, 1, axis=3)[:, :, :, 0]
    dm_loc = jnp.concatenate([dm_cols[:, :, 1].transpose(1, 0, 2), dm_cols[:, :, 0].transpose(1, 0, 2)], axis=1)
    g_ada_w, dcc_part, g_ada_b = ada_bwd(cg, cc2, ada_w, dm_loc, dm_all.transpose(0, 2, 1, 3).reshape(2 * NDEV, 2, 3 * D), "ada_bwd")
    dcc_all = exchange([dcc_part], "gather", "gather_dcc")[0].reshape(NDEV, D)
    g_c_ctx = cctx_finish(dcc_all, cc2, "cctx_finish")

    flat2 = lambda t: t.reshape(-1, t.shape[-1])
    recv.update(dict(zip(L0_BIG, l0_recv)))
    big = [(recv[n], w[n][0], m[n][0], v[n][0]) for n in BIG]
    big.append((flat2(g_ada_w)[None], flat2(ada_w), flat2(m_ada_w), flat2(v_ada_w)))
    for n, r in zip(BIG + ("ada_w",), adamw_rows(big, "adamw_big")[0]):
        keep(n, r)
    items = []
    halves = 2
    for n, g in zip(CHUNKED, chunk_all):
        blk = (1, 1, G // halves) + w[n].shape[3:]
        g = jnp.moveaxis(g, 0, 1).reshape(w[n].shape)
        g_spec = pl.BlockSpec((1,) + blk, lambda d, s: (0, 0, d, s, 0, 0))
        items.append((g[None], g_spec, w[n], m[n], v[n], pl.BlockSpec(blk, lambda d, s: (0, d, s, 0, 0))))
    for n, res in zip(CHUNKED, adamw_multi(items, (2, halves), "adamw_bc")):
        keep(n, res)
    tiny_g = dict(zip(TINY, tiny_all))
    tiny_g.update({n: recv[n] for n in VEC_SHARDED})
    tiny_g["c_ctx"], tiny_g["ada_b"] = g_c_ctx[None], g_ada_b[None]
    names = list(tiny_g)
    items = [(tiny_g[n], _whole(tiny_g[n], 1)) + tuple(t[n].reshape(kshape(n)) for t in (w, m, v))
             + (pl.BlockSpec(kshape(n), lambda i, r=len(kshape(n)): (0,) * r),) for n in names]
    for n, res in zip(names, adamw_multi(items, (1,), "adamw_small")):
        keep(n, res)

    return (loss, grad_x, *[out["g", n] for n in ORDER], *[out["d", n] for n in ORDER],
            *[out["m", n] for n in ORDER], *[out["v", n] for n in ORDER])
```

```python
import math

import numpy as np
import jax
import jax.numpy as jnp
from jax import lax
from jax.experimental import pallas as pl
from jax.experimental.pallas import tpu as pltpu

F32 = jnp.float32
BF16 = jnp.bfloat16

D = 1024
L = 2048
LC = 256
NDEV = 8
GRID_W = 64
EPS = 1e-6
HEADS = 16
NOPE = 64
ROPE = 32
QK = NOPE + ROPE
VD = 64
IN_W = 256 + 128 + ROPE + HEADS * 64
IN_WP = 1536
QL = 256
KVL = 128
SCALE = QK ** -0.5
LOG2E = math.log2(math.e)
THETA = 10000.0
G = 64
P = 64
CH = 16
GB = 8
NJ = G // GB
UB = GB * CH
SB = GB * P
SEG = 16
TB = 256
VMEM_LIMIT = 56 * 1024 * 1024
B1, B2, LR, AEPS, WD, STEP = 0.9, 0.999, 0.001, 1e-8, 0.01, 10
MESH_T = pl.DeviceIdType.MESH


def _cp(sem=None):
    return pltpu.CompilerParams(dimension_semantics=sem, vmem_limit_bytes=VMEM_LIMIT)


def _sig(x):
    return 1.0 / (1.0 + jnp.exp(-x))


def _silu(x):
    return x * _sig(x)


def _dsilu(x):
    s = _sig(x)
    return s * (1.0 + x * (1.0 - s))


_GK = math.sqrt(2.0 / math.pi)


def _gelu(x):
    return 0.5 * x * (1.0 + jnp.tanh(_GK * (x + 0.044715 * x * x * x)))


def _dgelu(x):
    t = jnp.tanh(_GK * (x + 0.044715 * x * x * x))
    return 0.5 * (1.0 + t) + 0.5 * x * (1.0 - t * t) * _GK * (1.0 + 3 * 0.044715 * x * x)


def _rs(x):
    return lax.rsqrt(jnp.mean(x * x, axis=-1, keepdims=True) + EPS)


def _sum0(x):
    return jnp.sum(x, axis=0, keepdims=True)


def st_norm_mod(x, g, sc, sh):
    y = x * _rs(x) * g
    return (y * (1.0 + sc) + sh,), ()


def st_norm_mod_bwd(x, dh, dres, g, sc):
    r = _rs(x)
    xn = x * r
    y = xn * g
    dy = dh * (1.0 + sc)
    dxn = dy * g
    dx = r * (dxn - xn * jnp.mean(dxn * xn, axis=-1, keepdims=True))
    return (dres + dx,), (_sum0(dh), _sum0(dh * y), _sum0(dy * xn))


def st_rms(x, g):
    return (x * _rs(x) * g,), ()


def st_rms_bwd(x, dy, g):
    r = _rs(x)
    n = x * r
    dn = dy * g
    dx = r * (dn - n * jnp.mean(dn * n, axis=-1, keepdims=True))
    return (dx,), (_sum0(dy * n),)


def st_rms2(x1, x2, g1, g2):
    return st_rms(x1, g1)[0] + st_rms(x2, g2)[0], ()


def st_rms2_bwd(x1, dy1, x2, dy2, g1, g2):
    (d1,), (s1,) = st_rms_bwd(x1, dy1, g1)
    (d2,), (s2,) = st_rms_bwd(x2, dy2, g2)
    return (d1, d2), (s1, s2)


def st_gate_bwd(dog, o, z):
    return (dog * _silu(z), dog * o * _dsilu(z)), ()


def st_resid_bwd(dx, out, gt):
    return (dx * gt,), (_sum0(dx * out),)


def st_s5a(yssm, u, d):
    y = yssm + d * u
    return (y, _gelu(y)), ()


def st_s5b_bwd(dy3, y, gl, z, b):
    y1 = _gelu(y)
    s = _sig(gl + b)
    dy2 = dy3 * _silu(z)
    dz = dy3 * y1 * s * _dsilu(z)
    dgl = dy2 * y1 * s * (1.0 - s)
    return (dgl, dz, dy2 * s), (_sum0(dgl),)


def st_s5a_bwd(dy1a, dy1b, y, u, d):
    dy = (dy1a + dy1b) * _dgelu(y)
    return (dy, dy * d), (_sum0(dy * u),)


def st_l0_pre(x, g, sc, sh, qg, kvg, w_in):
    hb = st_norm_mod(x, g, sc, sh)[0][0].astype(BF16)
    p = lax.dot_general(hb, w_in, _DN["nt"], preferred_element_type=F32)
    cq, ckv = p[:, HEADS * VD:HEADS * VD + QL], p[:, HEADS * VD + QL:HEADS * VD + QL + KVL]
    return (hb, p) + st_rms2(cq, ckv, qg, kvg)[0], ()


def st_l0_tail_bwd(dq, dkv, dkr, dz, cq, ckv, cqn, ckvn, h, x, dres, qg, kvg, g, sc, w_uq, w_ukv, w_in):
    dcqn = jnp.dot(dq, w_uq, preferred_element_type=F32)
    dckvn = jnp.dot(dkv, w_ukv, preferred_element_type=F32)
    (dcq, dckv), (dqg, dkvg) = st_rms2_bwd(cq, dcqn, ckv, dckvn, qg, kvg)
    dp = jnp.concatenate([dz, dcq, dckv, dkr], axis=1).astype(BF16)
    dh = jnp.dot(dp, w_in, preferred_element_type=F32)
    outs, sums = st_norm_mod_bwd(x, dh, dres, g, sc)
    tn = lambda a, b: lax.dot_general(a, b, _DN["tn"], preferred_element_type=F32)
    return outs, (dqg, dkvg) + sums, (tn(cqn, dq), tn(ckvn, dkv), tn(h, dp))


def st_l1_pre(x, g, sc, sh, w_in):
    hb = st_norm_mod(x, g, sc, sh)[0][0].astype(BF16)
    return (hb, lax.dot_general(hb, w_in, _DN["nt"], preferred_element_type=F32)), ()


def st_l1_tail_bwd(du_a, du_b, dz, h, x, dres, g, sc, w_in):
    dp = jnp.concatenate([(du_a + du_b).astype(BF16), dz], axis=1)
    dh = jnp.dot(dp, w_in, preferred_element_type=F32)
    outs, sums = st_norm_mod_bwd(x, dh, dres, g, sc)
    w = dp.shape[1] // NDEV
    shards = [lax.dot_general(h, dp[:, r * w:(r + 1) * w], _DN["tn"], preferred_element_type=F32) for r in range(NDEV)]
    return outs, sums, (jnp.stack(shards),)


def st_l0_post(o, z, x, gt, w_out):
    og = (o * _silu(z)).astype(BF16)
    out = jnp.dot(og, w_out, preferred_element_type=F32)
    return (og, out, x + gt * out), ()


def st_l0_post_bwd(dx1, out, og, o, z, gt, w_out):
    (dout,), (dgt,) = st_resid_bwd(dx1, out.astype(F32), gt)
    doutb = dout.astype(BF16)
    dog = lax.dot_general(doutb, w_out, _DN["nt"], preferred_element_type=F32)
    return st_gate_bwd(dog, o, z)[0], (dgt,), (lax.dot_general(og, doutb, _DN["tn"], preferred_element_type=F32),)


def st_l1_mlp(yssm, u, z, x1, tgt, d, bglu, gt, fg, mask, w_glu, w_out):
    (y, y1), _ = st_s5a(yssm, u, d)
    y1b = y1.astype(BF16)
    gl = jnp.dot(y1b, w_glu, preferred_element_type=F32)
    y3 = (y1 * _sig(gl + bglu) * _silu(z)).astype(BF16)
    out = jnp.dot(y3, w_out, preferred_element_type=F32)
    (dx2,), sums = st_final(x1 + gt * out, tgt, fg, mask)
    return (y, y1b, gl, y3, out, dx2), sums


def st_l1_mlp_bwd(dx2, out, y3, y, gl, z, u, y1b, gt, bglu, d, w_out, w_glu):
    out, gl = out.astype(F32), gl.astype(F32)
    (dout,), (dgt,) = st_resid_bwd(dx2, out, gt)
    doutb = dout.astype(BF16)
    dy3 = lax.dot_general(doutb, w_out, _DN["nt"], preferred_element_type=F32)
    (dgl, dz, dy1a), (dbglu,) = st_s5b_bwd(dy3, y, gl, z, bglu)
    dglb = dgl.astype(BF16)
    dy1b = lax.dot_general(dglb, w_glu, _DN["nt"], preferred_element_type=F32)
    (dy, du), (dd,) = st_s5a_bwd(dy1a, dy1b, y, u, d)
    g_w_out = lax.dot_general(y3, doutb, _DN["tn"], preferred_element_type=F32)
    g_w_glu = lax.dot_general(y1b, dglb, _DN["tn"], preferred_element_type=F32)
    return (dz, dy, du), (dgt, dbglu, dd), (g_w_out, g_w_glu)


def st_final(x2, tgt, g, mask):
    r = _rs(x2)
    n = x2 * r
    e = n * g - tgt
    dyo = e * (1.0 / D)
    dn = dyo * g
    dx = r * (dn - n * jnp.mean(dn * n, axis=-1, keepdims=True))
    lsum = jnp.sum(_sum0(e * e), axis=1, keepdims=True) * (0.5 / D)
    return (dx * mask,), (_sum0(dyo * n), jnp.broadcast_to(lsum, (1, 128)))


def rowwise(fn, rows, vecs, out_rows, out_sums, name, mats=(), out_accs=()):
    lat_blk = lambda i: jnp.maximum(i - 1, 0)
    arrays, in_specs, pick = [], [], []
    for a in rows:
        if not isinstance(a, tuple):
            a = (a, 0, a.shape[1])
        tag = a[0] if isinstance(a[0], str) else None
        if tag == "cat":
            _, ctx, x = a
            arrays += [ctx, x]
            in_specs += [pl.BlockSpec((TB, ctx.shape[1]), lambda i: (0, 0)),
                         pl.BlockSpec((TB, x.shape[1]), lambda i: (lat_blk(i), 0))]
            pick.append(2)
        elif tag == "lat":
            arrays.append(a[1])
            in_specs.append(pl.BlockSpec((TB, a[1].shape[1]), lambda i: (lat_blk(i), 0)))
            pick.append(1)
        else:
            arr, cb, width = a
            arrays.append(arr)
            in_specs.append(pl.BlockSpec((TB, width), lambda i, cb=cb: (i, cb)))
            pick.append(1)
    T = LC + L
    nin, nv, nm, no, ns = len(arrays), len(vecs), len(mats), len(out_rows), len(out_sums)

    def body(*refs):
        i = pl.program_id(0)
        vals, k = [], 0
        for p in pick:
            if p == 2:
                vals.append(jnp.where(i == 0, refs[k][...], refs[k + 1][...]))
            else:
                vals.append(refs[k][...])
            k += p
        vals += [r[0] for r in refs[nin:nin + nv]] + [r[...] for r in refs[nin + nv:nin + nv + nm]]
        res = fn(*vals)
        first_out = nin + nv + nm
        for r, o in zip(refs[first_out:first_out + no], res[0]):
            r[...] = o.astype(r.dtype)
        sum_refs = refs[first_out + no:first_out + no + ns]
        if sum_refs:
            @pl.when(i <= 1)
            def _():
                for r in sum_refs:
                    r[...] = jnp.zeros_like(r)
            for r, s in zip(sum_refs, res[1]):
                r[0] += s
        na = len(out_accs)
        if na:
            acc_out, acc = refs[first_out + no + ns:first_out + no + ns + na], refs[first_out + no + ns + na:]

            @pl.when(i == 0)
            def _():
                for r in acc:
                    r[...] = jnp.zeros_like(r)
            for r, a in zip(acc, res[2]):
                r[...] += a

            @pl.when(i == T // TB - 1)
            def _():
                for o, r in zip(acc_out, acc):
                    o[...] = r[...].astype(o.dtype)

    kind = lambda i: (jnp.minimum(i, 1), 0, 0)
    in_specs += [pl.BlockSpec((1, 1, v.shape[2]), kind) for v in vecs]
    in_specs += [pl.BlockSpec(m.shape, lambda i: (0, 0), pipeline_mode=pl.Buffered(1)) for m in mats]
    out_specs, out_shape = [], []
    for o in out_rows:
        lat = len(o) == 3
        out_specs.append(pl.BlockSpec((TB, o[0]), (lambda i: (lat_blk(i), 0)) if lat else (lambda i: (i, 0))))
        out_shape.append(jax.ShapeDtypeStruct((L if lat else T, o[0]), o[1]))
    out_specs += [pl.BlockSpec((1, 1, c), kind) for c in out_sums]
    out_shape += [jax.ShapeDtypeStruct((2, 1, c), F32) for c in out_sums]
    out_specs += [pl.BlockSpec(s, lambda i, r=len(s): (0,) * r) for s in out_accs]
    out_shape += [jax.ShapeDtypeStruct(s, BF16) for s in out_accs]
    res = pl.pallas_call(body, grid=(T // TB,), in_specs=in_specs, out_specs=out_specs, out_shape=out_shape,
                         scratch_shapes=[pltpu.VMEM(s, F32) for s in out_accs],
                         compiler_params=_cp(("arbitrary",)), name=name)(*arrays, *vecs, *mats)
    if out_accs:
        return res[:no], res[no:no + ns], res[no + ns:]
    return res[:no], res[no:]


_DN = {"nn": (((1,), (0,)), ((), ())), "nt": (((1,), (1,)), ((), ())), "tn": (((0,), (0,)), ((), ()))}


def mm(a, b, mode, name, out_dtype=F32, tm=None, tn=None, rode=None, modes=None):
    if mode == "nn":
        (M, K), (_, N) = a.shape, b.shape
    elif mode == "nt":
        (M, K), (N, _) = a.shape, b.shape
    else:
        (K, M), (_, N) = a.shape, b.shape
    if tm is None:
        tm = next((t for t in (768, 512, 256) if M % t == 0 and M > t), M)
    tn = N if tn is None else tn
    dn = _DN[mode]

    def body(a_ref, b_ref, o_ref):
        o_ref[...] = lax.dot_general(a_ref[...].astype(BF16), b_ref[...].astype(BF16), dn,
                                     preferred_element_type=F32).astype(o_ref.dtype)

    a_spec = pl.BlockSpec((K, tm), lambda i, j: (0, i)) if mode == "tn" else pl.BlockSpec((tm, K), lambda i, j: (i, 0))
    b_spec = pl.BlockSpec((tn, K), lambda i, j: (j, 0)) if mode == "nt" else pl.BlockSpec((K, tn), lambda i, j: (0, j))
    (prod,), got = _ride_call(body, (M // tm, N // tn), [a_spec, b_spec], [pl.BlockSpec((tm, tn), lambda i, j: (i, j))],
                              [jax.ShapeDtypeStruct((M, N), out_dtype)], Exchange(rode, modes) if rode else None, rode,
                              name, (a, b))
    return prod, got


def _rope_tables(T, width=QK, first=NOPE):
    nlat = T - LC
    pos = np.arange(nlat)
    row, col = pos // GRID_W, pos % GRID_W
    half = ROPE // 2
    inv = 1.0 / (THETA ** (np.arange(0, half, 2, dtype=np.float64) / half))
    cosf = np.ones((T, width), np.float64)
    sinf = np.zeros((T, width), np.float64)
    perm = np.zeros((width, width), np.float32)
    for m in range(ROPE):
        j = first + m
        blk, w = m // half, m % half
        ang = (row if blk == 0 else col)[:, None] * inv[None, :]
        f = w % (half // 2)
        cosf[LC:, j] = np.cos(ang[:, f])
        if w < half // 2:
            sinf[LC:, j] = -np.sin(ang[:, f])
            perm[j + half // 2, j] = 1.0
        else:
            sinf[LC:, j] = np.sin(ang[:, f])
            perm[j - half // 2, j] = 1.0
    return jnp.asarray(cosf, F32), jnp.asarray(sinf, F32), jnp.asarray(perm, BF16), jnp.asarray(perm.T, BF16)


def _exact_perm(x, pm):
    hi = x.astype(BF16)
    r1 = x - hi.astype(F32)
    mid = r1.astype(BF16)
    lo = (r1 - mid.astype(F32)).astype(BF16)
    dot = lambda a: jnp.dot(a, pm, preferred_element_type=F32)
    return dot(hi) + dot(mid) + dot(lo)


def _rot(x, cv, sv, pv, inverse):
    if inverse:
        return x * cv + _exact_perm(x * sv, pv)
    return x * cv + _exact_perm(x, pv) * sv


def rope_bwd(dx, cosf, sinf, pmt, scale, name):
    H, T, _ = dx.shape

    def body(x_ref, c_ref, s_ref, p_ref, o_ref):
        cv, sv, pv = c_ref[...], s_ref[...], p_ref[...]
        for h in range(H):
            o_ref[:, pl.ds(h * QK, QK)] = (_rot(x_ref[h], cv, sv, pv, True) * scale).astype(o_ref.dtype)

    return pl.pallas_call(
        body, grid=(T // TB,),
        in_specs=[pl.BlockSpec((H, TB, QK), lambda i: (0, i, 0)), pl.BlockSpec((TB, QK), lambda i: (i, 0)),
                  pl.BlockSpec((TB, QK), lambda i: (i, 0)), pl.BlockSpec((QK, QK), lambda i: (0, 0))],
        out_specs=pl.BlockSpec((TB, H * QK), lambda i: (i, 0)), out_shape=jax.ShapeDtypeStruct((T, H * QK), BF16),
        compiler_params=_cp(("parallel",)), name=name)(dx, cosf, sinf, pmt)


KVW = NOPE + VD


def project_q(cqn, w, name):
    T = cqn.shape[0]
    cosf, sinf, _, _ = _rope_tables(T, 128, NOPE)
    wp = jnp.pad(w.reshape(HEADS, QK, QL), ((0, 0), (0, 128 - QK), (0, 0))).reshape(HEADS * 128, QL)

    def body(a_ref, w_ref, c_ref, s_ref, o_ref):
        a, cv, sv = a_ref[...], c_ref[...], s_ref[...]
        first_of_pair = lax.bitwise_and(lax.broadcasted_iota(jnp.int32, (TB, 128), 1), ROPE // 4) == 0
        for h in range(HEADS):
            qh = _dotf(a, w_ref[pl.ds(h * 128, 128), :], "nt")
            swap = jnp.where(first_of_pair, pltpu.roll(qh, 128 - ROPE // 4, 1), pltpu.roll(qh, ROPE // 4, 1))
            o_ref[h] = ((qh * cv + swap * sv) * (SCALE * LOG2E))[:, :QK].astype(BF16)

    rows = lambda c: pl.BlockSpec((TB, c), lambda i: (i, 0))
    return pl.pallas_call(
        body, grid=(T // TB,), in_specs=[rows(QL), pl.BlockSpec(wp.shape, lambda i: (0, 0)), rows(128), rows(128)],
        out_specs=pl.BlockSpec((HEADS, TB, QK), lambda i: (0, i, 0)), out_shape=jax.ShapeDtypeStruct((HEADS, T, QK), BF16),
        compiler_params=_cp(("parallel",)), name=name)(cqn, wp, cosf, sinf)


def project_kv(ckvn, w, p0, kr_block, name):
    T = ckvn.shape[0]
    assert KVW == 128 and NOPE == VD
    cosf, sinf, pm, _ = _rope_tables(T, 128, 0)

    def body(a_ref, w_ref, kr_ref, c_ref, s_ref, p_ref, k_ref, v_ref):
        a = a_ref[...]
        is_nope = lax.broadcasted_iota(jnp.int32, (TB, KVW), 1) < NOPE
        kr_at = pltpu.roll(_rot(kr_ref[...], c_ref[...], s_ref[...], p_ref[...], False), NOPE, 1)
        for h in range(HEADS):
            kv = _dotf(a, w_ref[pl.ds(h * KVW, KVW), :], "nt")
            k_ref[h] = jnp.where(is_nope, kv, kr_at)[:, :QK].astype(BF16)
            v_ref[h] = pltpu.roll(kv, VD, 1)[:, :VD].astype(BF16)

    rows = lambda c: pl.BlockSpec((TB, c), lambda i: (i, 0))
    const = lambda x: pl.BlockSpec(x.shape, lambda i: (0, 0))
    return pl.pallas_call(
        body, grid=(T // TB,),
        in_specs=[rows(KVL), const(w), pl.BlockSpec((TB, 128), lambda i: (i, kr_block)), rows(128), rows(128), const(pm)],
        out_specs=[pl.BlockSpec((HEADS, TB, QK), lambda i: (0, i, 0)), pl.BlockSpec((HEADS, TB, VD), lambda i: (0, i, 0))],
        out_shape=[jax.ShapeDtypeStruct((HEADS, T, QK), BF16), jax.ShapeDtypeStruct((HEADS, T, VD), BF16)],
        compiler_params=_cp(("parallel",)), name=name)(ckvn, w, p0, cosf, sinf, pm)


def split_kv_grads(dk, dv, name, rode=None, modes=None):
    H, T, _ = dk.shape
    cosf, sinf, _, pmt = _rope_tables(T, 128, 0)
    to_rope_block = np.zeros((QK, 128), np.float32)
    to_rope_block[NOPE + np.arange(ROPE), np.arange(ROPE)] = 1.0
    to_rope_block = jnp.asarray(to_rope_block, BF16)

    def body(dk_ref, dv_ref, c_ref, s_ref, p_ref, sel_ref, dkv_ref, dkr_ref):
        total = None
        for h in range(H):
            dkh = dk_ref[h] * (1.0 / LOG2E)
            total = dkh if total is None else total + dkh
            dkv_ref[:, pl.ds(h * KVW, NOPE)] = dkh[:, :NOPE].astype(BF16)
            dkv_ref[:, pl.ds(h * KVW + NOPE, VD)] = dv_ref[h].astype(BF16)
        dkr_ref[...] = _rot(_exact_perm(total, sel_ref[...]), c_ref[...], s_ref[...], p_ref[...], True)

    rows = lambda c: pl.BlockSpec((TB, c), lambda i, j: (i, 0))
    const = lambda a: pl.BlockSpec(a.shape, lambda i, j: (0, 0))
    return _ride_call(
        body, (T // TB, 1),
        [pl.BlockSpec((H, TB, QK), lambda i, j: (0, i, 0)), pl.BlockSpec((H, TB, VD), lambda i, j: (0, i, 0)),
         rows(128), rows(128), const(pmt), const(to_rope_block)],
        [rows(H * KVW), rows(128)],
        [jax.ShapeDtypeStruct((T, H * KVW), BF16), jax.ShapeDtypeStruct((T, 128), F32)],
        Exchange(rode, modes) if rode else None, rode, name, (dk, dv, cosf, sinf, pmt, to_rope_block))


HB = 4
HBF = 8


def _by_query_block(run, T):
    @pl.when(pl.program_id(1) == 0)
    def _():
        run(LC)

    @pl.when(pl.program_id(1) > 0)
    def _():
        run(T)


def _with_rider(body, nin, nout, ride, grid):
    if ride is None:
        return body
    n = ride.n

    def wrapped(*refs):
        ins, xs = refs[:nin], refs[nin:nin + n]
        outs, got = refs[nin + n:nin + n + nout], refs[nin + n + nout:nin + 2 * n + nout]
        sems = refs[nin + 2 * n + nout:]
        step = pl.program_id(0) * grid[1] + pl.program_id(1)

        @pl.when(step == 0)
        def _():
            ride.start(xs, got, sems)

        body(*ins, *outs)

        @pl.when(step == grid[0] * grid[1] - 1)
        def _():
            ride.finish(xs, got, sems)

    return wrapped


def _ride_call(body, grid, in_specs, out_specs, out_shape, ride, rode, name, args):
    if ride is None:
        return pl.pallas_call(body, grid=grid, in_specs=in_specs, out_specs=out_specs, out_shape=out_shape,
                              compiler_params=_cp(("parallel", "arbitrary")), name=name)(*args), []
    res = pl.pallas_call(
        _with_rider(body, len(in_specs), len(out_specs), ride, grid), grid=grid,
        in_specs=in_specs + ride.specs, out_specs=out_specs + ride.specs, out_shape=out_shape + ride.out_shape,
        scratch_shapes=ride.scratch,
        compiler_params=pltpu.CompilerParams(dimension_semantics=("arbitrary", "arbitrary"), vmem_limit_bytes=VMEM_LIMIT,
                                             has_side_effects=True), name=name)(*args, *rode)
    return res[:len(out_specs)], res[len(out_specs):]


def attn_fwd(q, k, v, name, rode=None, modes=None):
    H, T, _ = q.shape

    def body(q_ref, k_ref, v_ref, o_ref, lse_ref):
        def run(nk):
            for hh in range(HBF):
                s = _dotf(q_ref[hh], k_ref[hh, pl.ds(0, nk), :], "nt")
                m = jnp.max(s, axis=1, keepdims=True)
                p = jnp.exp2(s - m)
                l = jnp.sum(p, axis=1, keepdims=True)
                o = jnp.dot(p.astype(BF16), v_ref[hh, pl.ds(0, nk), :], preferred_element_type=F32)
                o_ref[:, pl.ds(hh * VD, VD)] = o / l
                lse_ref[hh] = m + jnp.log2(l)

        _by_query_block(run, T)

    return _ride_call(
        body, (H // HBF, T // TB),
        [pl.BlockSpec((HBF, TB, QK), lambda h, i: (h, i, 0)), pl.BlockSpec((HBF, T, QK), lambda h, i: (h, 0, 0)),
         pl.BlockSpec((HBF, T, VD), lambda h, i: (h, 0, 0))],
        [pl.BlockSpec((TB, HBF * VD), lambda h, i: (i, h)), pl.BlockSpec((HBF, TB, 1), lambda h, i: (h, i, 0))],
        [jax.ShapeDtypeStruct((T, H * VD), F32), jax.ShapeDtypeStruct((H, T, 1), F32)],
        Exchange(rode, modes) if rode else None, rode, name, (q, k, v))


def attn_bwd(q, k, v, do, name, rode=None, modes=None):
    H, T, _ = q.shape

    def body(q_ref, k_ref, v_ref, do_ref, dq_ref, dk_ref, dv_ref):
        i = pl.program_id(1)

        @pl.when(i == 0)
        def _():
            dk_ref[...] = jnp.zeros_like(dk_ref)
            dv_ref[...] = jnp.zeros_like(dv_ref)

        def run(nk):
            keys = pl.ds(0, nk)
            for hh in range(HB):
                qv, kv = q_ref[hh], k_ref[hh, keys, :]
                dob = do_ref[:, pl.ds(hh * VD, VD)].astype(BF16)
                st = _dotf(kv, qv, "nt")
                e = jnp.exp2(st - jnp.max(st, axis=0, keepdims=True))
                p = e * (1.0 / jnp.sum(e, axis=0, keepdims=True))
                dv_ref[hh, keys, :] += _dotf(p.astype(BF16), dob)
                dp = _dotf(v_ref[hh, keys, :], dob, "nt")
                delta = jnp.sum(p * dp, axis=0, keepdims=True)
                ds = (p * (dp - delta)).astype(BF16)
                dq_ref[hh] = _dotf(ds, kv, "tn")
                dk_ref[hh, keys, :] += _dotf(ds, qv)

        _by_query_block(run, T)

    blk = lambda c: pl.BlockSpec((HB, TB, c), lambda h, i: (h, i, 0))
    full = lambda c: pl.BlockSpec((HB, T, c), lambda h, i: (h, 0, 0))
    tok = pl.BlockSpec((TB, HB * VD), lambda h, i: (i, h))
    return _ride_call(
        body, (H // HB, T // TB), [blk(QK), full(QK), full(VD), tok], [blk(QK), full(QK), full(VD)],
        [jax.ShapeDtypeStruct((H, T, QK), F32), jax.ShapeDtypeStruct((H, T, QK), F32), jax.ShapeDtypeStruct((H, T, VD), F32)],
        Exchange(rode, modes) if rode else None, rode, name, (q, k, v, do))


def disc_fwd(a_re, a_im, ls, name):
    def body(ar_ref, ai_ref, ls_ref, lr_ref, li_ref, fr_ref, fi_ref):
        ar, ai = ar_ref[...], ai_ref[...]
        dt = jnp.exp(ls_ref[...])
        mag = jnp.exp(ar * dt)
        lr = mag * jnp.cos(ai * dt)
        li = mag * jnp.sin(ai * dt)
        den = ar * ar + ai * ai
        nr = lr - 1.0
        lr_ref[...] = lr
        li_ref[...] = li
        fr_ref[...] = (nr * ar + li * ai) / den
        fi_ref[...] = (li * ar - nr * ai) / den

    return pl.pallas_call(body, out_shape=[jax.ShapeDtypeStruct(a_re.shape, F32)] * 4, name=name)(a_re, a_im, ls)


def disc_b(f_re, f_im, b_re, b_im, name):
    def body(fr_ref, fi_ref, br_ref, bi_ref, or_ref, oi_ref):
        fr, fi, br, bi = fr_ref[...], fi_ref[...], br_ref[...], bi_ref[...]
        or_ref[...] = fr * br - fi * bi
        oi_ref[...] = fr * bi + fi * br

    return pl.pallas_call(body, out_shape=[jax.ShapeDtypeStruct(b_re.shape, F32)] * 2, compiler_params=_cp(),
                          name=name)(f_re, f_im, b_re, b_im)


def disc_b_bwd(f_re, f_im, b_re, b_im, dbb_re, dbb_im, name):
    def body(fr_ref, fi_ref, br_ref, bi_ref, dr_ref, di_ref, dbr_ref, dbi_ref, dfr_ref, dfi_ref):
        fr, fi, br, bi, dr, di = fr_ref[...], fi_ref[...], br_ref[...], bi_ref[...], dr_ref[...], di_ref[...]
        dbr_ref[...] = fr * dr + fi * di
        dbi_ref[...] = fr * di - fi * dr
        dfr_ref[...] = jnp.sum(dr * br + di * bi, axis=2, keepdims=True)
        dfi_ref[...] = jnp.sum(di * br - dr * bi, axis=2, keepdims=True)

    return pl.pallas_call(body, out_shape=[jax.ShapeDtypeStruct(b_re.shape, F32)] * 2 + [jax.ShapeDtypeStruct(f_re.shape, F32)] * 2,
                          compiler_params=_cp(), name=name)(f_re, f_im, b_re, b_im, dbb_re, dbb_im)


def disc_a_bwd(a_re, a_im, ls, dlr, dli, dfr, dfi, name):
    def body(ar_ref, ai_ref, ls_ref, dlr_ref, dli_ref, dfr_ref, dfi_ref, dar_ref, dai_ref, dls_ref):
        ar, ai = ar_ref[...], ai_ref[...]
        dt = jnp.exp(ls_ref[...])
        mag = jnp.exp(ar * dt)
        cs, sn = jnp.cos(ai * dt), jnp.sin(ai * dt)
        lr, li = mag * cs, mag * sn
        den = ar * ar + ai * ai
        nr = lr - 1.0
        f_re = (nr * ar + li * ai) / den
        f_im = (li * ar - nr * ai) / den
        dn1 = dfr_ref[...] / den
        dn2 = dfi_ref[...] / den
        dden = -(dfr_ref[...] * f_re + dfi_ref[...] * f_im) / den
        dlr_t = dlr_ref[...] + dn1 * ar - dn2 * ai
        dli_t = dli_ref[...] + dn1 * ai + dn2 * ar
        dar = dn1 * nr + dn2 * li + dden * 2.0 * ar
        dai = dn1 * li - dn2 * nr + dden * 2.0 * ai
        dmag = dlr_t * cs + dli_t * sn
        dth = dli_t * lr - dlr_t * li
        dar_ref[...] = dar + dmag * mag * dt
        dai_ref[...] = dai + dth * dt
        dls_ref[...] = jnp.sum(dmag * mag * ar + dth * ai, axis=-1, keepdims=True) * dt

    return pl.pallas_call(body, out_shape=[jax.ShapeDtypeStruct(a_re.shape, F32)] * 2 +
                          [jax.ShapeDtypeStruct(ls.shape, F32)], name=name)(a_re, a_im, ls, dlr, dli, dfr, dfi)


def _cpow(lr, li, n):
    rr, ri = None, None
    br, bi = lr, li
    while n:
        if n & 1:
            if rr is None:
                rr, ri = br, bi
            else:
                rr, ri = rr * br - ri * bi, rr * bi + ri * br
        n >>= 1
        if n:
            br, bi = br * br - bi * bi, 2.0 * br * bi
    return rr, ri


UNROLL = 4


def _steps(trips, fn, init):
    main = trips // UNROLL

    def body(i, c):
        for j in range(UNROLL):
            c = fn(i * UNROLL + j, c)
        return c

    c = lax.fori_loop(0, main, body, init) if main else init
    for n in range(main * UNROLL, trips):
        c = fn(n, c)
    return c


def _seg_scan(xre, xim, lam8, pw, base, seglen, rev, init, fin_re, fin_im, ini_re, ini_im, prev=None):
    lr, li = lam8
    nsub = SEG // 8

    def rows(t, s):
        first = base + t * SEG + 8 * s
        return pl.ds(first if isinstance(first, int) else pl.multiple_of(first, 8), 8)

    tmap = (lambda n: seglen - 1 - n) if rev else (lambda n: n)
    zeros = tuple(jnp.zeros((8, SB), F32) for _ in range(2 * nsub))

    def advance(c, t):
        out = []
        for s in range(nsub):
            a, b = c[2 * s], c[2 * s + 1]
            out += [lr * a - li * b + xre[rows(t, s), :], lr * b + li * a + xim[rows(t, s), :]]
        return tuple(out)

    fin = _steps(seglen, lambda n, c: advance(c, tmap(n)), zeros)
    for s in range(nsub):
        fin_re[pl.ds(8 * s, 8), :] = fin[2 * s]
        fin_im[pl.ds(8 * s, 8), :] = fin[2 * s + 1]
    (cr, ci), (pr, pi) = init, pw
    for i in (range(SEG - 1, -1, -1) if rev else range(SEG)):
        ini_re[pl.ds(i, 1), :] = cr
        ini_im[pl.ds(i, 1), :] = ci
        cr, ci = pr * cr - pi * ci + fin_re[pl.ds(i, 1), :], pr * ci + pi * cr + fin_im[pl.ds(i, 1), :]
    tiles = lambda re, im: tuple(r[pl.ds(8 * s, 8), :] for s in range(nsub) for r in (re, im))
    start = tiles(ini_re, ini_im)

    def store(c, t):
        new = advance(c, t)
        for s in range(nsub):
            xre[rows(t, s), :] = new[2 * s]
            xim[rows(t, s), :] = new[2 * s + 1]
        return new

    if prev is None:
        _steps(seglen, lambda n, c: store(c, tmap(n)), start)
        return (cr, ci), None

    sre, sim, s_ini_re, s_ini_im = prev

    def acc_step(c, t, before):
        new = store(c[:2 * nsub], t)
        acc = []
        for s in range(nsub):
            (na, nb), (pre, pim) = new[2 * s:2 * s + 2], before[2 * s:2 * s + 2]
            acc += [c[2 * nsub + 2 * s] + na * pre + nb * pim, c[2 * nsub + 2 * s + 1] + nb * pre - na * pim]
        return new + tuple(acc)

    def body(n, c):
        t = tmap(n)
        tp = t - 1 if rev else t + 1
        return acc_step(c, t, tuple(r[rows(tp, s), :] for s in range(nsub) for r in (sre, sim)))

    c = _steps(seglen - 1, body, start + zeros)
    c = acc_step(c, 0 if rev else seglen - 1, tiles(s_ini_re, s_ini_im))
    acc = c[2 * nsub:]
    return (cr, ci), (sum(acc[0::2][1:], acc[0]), sum(acc[1::2][1:], acc[1]))


def _lam_tiles(lr, li, lens, conj=False):
    if conj:
        li = -li
    lam8 = (jnp.broadcast_to(lr, (8, SB)), jnp.broadcast_to(li, (8, SB)))
    return lam8, [_cpow(lr, li, n) for n in lens]


def _stretches(T):
    return ((0, LC // SEG), (LC, (T - LC) // SEG))


def _to_seg_order(src, dst, T):
    for base, seglen in _stretches(T):
        def body(t, carry, base=base, seglen=seglen):
            dst[pl.ds(pl.multiple_of(base + t * SEG, SEG), SEG), :] = src[pl.ds(base + t, SEG, stride=seglen), :]
            return carry
        lax.fori_loop(0, seglen, body, 0, unroll=8)


def _from_seg_order(src, dst, T):
    for base, seglen in _stretches(T):
        def body(t, carry, base=base, seglen=seglen):
            dst[pl.ds(base + t, SEG, stride=seglen), :] = src[pl.ds(pl.multiple_of(base + t * SEG, SEG), SEG), :]
            return carry
        lax.fori_loop(0, seglen, body, 0, unroll=8)


def _scan_specs(T):
    ublk = pl.BlockSpec((T, UB), lambda j: (0, j))
    lam = pl.BlockSpec((2, 1, 1, SB), lambda j: (0, j, 0, 0))
    mat = pl.BlockSpec((2, 1, UB, P), lambda j: (0, j, 0, 0))
    return ublk, lam, mat


def _dotf(a, b, mode="nn"):
    return lax.dot_general(a, b, _DN[mode], preferred_element_type=F32)


def _diag_mask():
    r = lax.broadcasted_iota(jnp.int32, (UB, SB), 0)
    c = lax.broadcasted_iota(jnp.int32, (UB, SB), 1)
    return lax.shift_right_logical(r, int(math.log2(CH))) == lax.shift_right_logical(c, int(math.log2(P)))


def _expand(m):
    p = lax.broadcasted_iota(jnp.int32, (P, SB), 0)
    c = lax.broadcasted_iota(jnp.int32, (P, SB), 1)
    tile = jnp.where(lax.bitwise_and(c, P - 1) == p, 1.0, 0.0).astype(BF16)
    wide = jnp.dot(m.astype(BF16), tile, preferred_element_type=F32)
    return jnp.where(_diag_mask(), wide, 0.0).astype(BF16)


def _collapse(full):
    c = lax.broadcasted_iota(jnp.int32, (SB, P), 0)
    p = lax.broadcasted_iota(jnp.int32, (SB, P), 1)
    pick = jnp.where(lax.bitwise_and(c, P - 1) == p, 1.0, 0.0).astype(BF16)
    return _exact_perm(jnp.where(_diag_mask(), full, 0.0), pick)


def _zero_state():
    return jnp.zeros((1, SB), F32), jnp.zeros((1, SB), F32)


def scan_fwd(u, lam_re, lam_im, bre, bim, cre, cim, name):
    T = u.shape[0]
    s_ctx, s_lat = LC // SEG, (T - LC) // SEG

    def body(u_ref, lr_ref, li_ref, bre_ref, bim_ref, cre_ref, cim_ref, y_ref, us, ys, sre, sim, fre, fim, ire, iim):
        _to_seg_order(u_ref, us, T)
        ub = us[...].astype(BF16)
        for d in range(2):
            lam8, (pw_c, pw_l) = _lam_tiles(lr_ref[d, 0], li_ref[d, 0], (s_ctx, s_lat))
            sre[...] = _dotf(ub, _expand(bre_ref[d, 0]))
            sim[...] = _dotf(ub, _expand(bim_ref[d, 0]))
            end_c, _ = _seg_scan(sre, sim, lam8, pw_c, 0, s_ctx, bool(d), _zero_state(), fre, fim, ire, iim)
            _seg_scan(sre, sim, lam8, pw_l, LC, s_lat, bool(d), end_c, fre, fim, ire, iim)
            y = (_dotf(sre[...].astype(BF16), _expand(cre_ref[d, 0]), "nt")
                 - _dotf(sim[...].astype(BF16), _expand(cim_ref[d, 0]), "nt"))
            if d == 0:
                ys[...] = y
            else:
                ys[...] += y
        _from_seg_order(ys, y_ref, T)

    ublk, lam, mat = _scan_specs(T)
    return pl.pallas_call(
        body, grid=(NJ,), in_specs=[ublk, lam, lam, mat, mat, mat, mat], out_specs=ublk,
        out_shape=jax.ShapeDtypeStruct((T, G * CH), F32),
        scratch_shapes=[pltpu.VMEM((T, UB), F32)] * 2 + [pltpu.VMEM((T, SB), F32)] * 2 + [pltpu.VMEM((SEG, SB), F32)] * 4,
        compiler_params=_cp(("arbitrary",)), name=name)(u, lam_re, lam_im, bre, bim, cre, cim)


def scan_bwd(u, dy, lam_re, lam_im, bre, bim, cre, cim, name):
    T = u.shape[0]
    s_ctx, s_lat = LC // SEG, (T - LC) // SEG

    def body(u_ref, dy_ref, lr_ref, li_ref, bre_ref, bim_ref, cre_ref, cim_ref,
             du_ref, dlr_ref, dli_ref, dbre_ref, dbim_ref, dcre_ref, dcim_ref,
             us, dys, dus, sre, sim, gre, gim, fre, fim, ic_re, ic_im, il_re, il_im, jre, jim):
        _to_seg_order(u_ref, us, T)
        _to_seg_order(dy_ref, dys, T)
        ub, dyb = us[...].astype(BF16), dys[...].astype(BF16)
        for d in range(2):
            rev = bool(d)
            lam8, (pw_c, pw_l) = _lam_tiles(lr_ref[d, 0], li_ref[d, 0], (s_ctx, s_lat))
            cam8, (cw_c, cw_l) = _lam_tiles(lr_ref[d, 0], li_ref[d, 0], (s_ctx, s_lat), conj=True)
            bre_v, bim_v = _expand(bre_ref[d, 0]), _expand(bim_ref[d, 0])
            sre[...] = _dotf(ub, bre_v)
            sim[...] = _dotf(ub, bim_v)
            end_c, _ = _seg_scan(sre, sim, lam8, pw_c, 0, s_ctx, rev, _zero_state(), fre, fim, ic_re, ic_im)
            _seg_scan(sre, sim, lam8, pw_l, LC, s_lat, rev, end_c, fre, fim, il_re, il_im)
            gre[...] = _dotf(dyb, _expand(cre_ref[d, 0]))
            gim[...] = -_dotf(dyb, _expand(cim_ref[d, 0]))
            end_g, acc_l = _seg_scan(gre, gim, cam8, cw_l, LC, s_lat, not rev, _zero_state(), fre, fim, jre, jim,
                                     prev=(sre, sim, il_re, il_im))
            _, acc_c = _seg_scan(gre, gim, cam8, cw_c, 0, s_ctx, not rev, end_g, fre, fim, jre, jim,
                                 prev=(sre, sim, ic_re, ic_im))
            dlr_ref[d, 0] = _sum0(acc_l[0] + acc_c[0])
            dli_ref[d, 0] = _sum0(acc_l[1] + acc_c[1])
            grb, gib = gre[...].astype(BF16), gim[...].astype(BF16)
            du = _dotf(grb, bre_v, "nt") + _dotf(gib, bim_v, "nt")
            if d == 0:
                dus[...] = du
            else:
                dus[...] += du
            dbre_ref[d, 0] = _collapse(_dotf(ub, grb, "tn"))
            dbim_ref[d, 0] = _collapse(_dotf(ub, gib, "tn"))
            dcre_ref[d, 0] = _collapse(_dotf(dyb, sre[...].astype(BF16), "tn"))
            dcim_ref[d, 0] = -_collapse(_dotf(dyb, sim[...].astype(BF16), "tn"))
        _from_seg_order(dus, du_ref, T)

    ublk, lam, mat = _scan_specs(T)
    lam_s = jax.ShapeDtypeStruct(lam_re.shape, F32)
    mat_s = jax.ShapeDtypeStruct(bre.shape, F32)
    return pl.pallas_call(
        body, grid=(NJ,), in_specs=[ublk, ublk, lam, lam, mat, mat, mat, mat],
        out_specs=[ublk, lam, lam, mat, mat, mat, mat],
        out_shape=[jax.ShapeDtypeStruct((T, G * CH), F32), lam_s, lam_s, mat_s, mat_s, mat_s, mat_s],
        scratch_shapes=[pltpu.VMEM((T, UB), F32)] * 3 + [pltpu.VMEM((T, SB), F32)] * 4 + [pltpu.VMEM((SEG, SB), F32)] * 8,
        compiler_params=_cp(("arbitrary",)), name=name)(u, dy, lam_re, lam_im, bre, bim, cre, cim)


class Exchange:
    def __init__(self, xs, modes):
        self.n = len(xs)
        self.modes = [modes] * self.n if isinstance(modes, (str, int)) else list(modes)
        self.out_shape = [jax.ShapeDtypeStruct(self._shape(x, md), x.dtype) for x, md in zip(xs, self.modes)]
        self.scratch = [pltpu.SemaphoreType.DMA((NDEV - 1, self.n)), pltpu.SemaphoreType.DMA((NDEV - 1, self.n)),
                        pltpu.SemaphoreType.DMA((self.n,))]
        self.specs = [pl.BlockSpec(memory_space=pl.ANY)] * self.n

    @staticmethod
    def _shape(x, mode):
        if mode == "gather":
            return (NDEV,) + tuple(x.shape)
        return tuple(x.shape) if mode == "lead" else (NDEV, x.shape[0], mode) + tuple(x.shape[2:])

    @staticmethod
    def _piece(x_ref, mode, dev):
        if mode == "gather":
            return x_ref
        return x_ref.at[dev] if mode == "lead" else x_ref.at[:, pl.ds(dev * mode, mode)]

    def _copies(self, x_refs, out_refs, sems):
        send_sems, recv_sems, local_sems = sems
        mx, my, mc = lax.axis_index("x"), lax.axis_index("y"), lax.axis_index("c")
        me = 4 * mx + 2 * my + mc
        peer_of = lambda k: (1 - mx if k & 4 else mx, 1 - my if k & 2 else my, 1 - mc if k & 1 else mc)
        local, first, relay, arrivals = [], [], [], []
        for a, (x_ref, out_ref) in enumerate(zip(x_refs, out_refs)):
            mode = self.modes[a]
            local.append(pltpu.make_async_copy(self._piece(x_ref, mode, me), out_ref.at[me], local_sems.at[a]))

            def remote(src, dst, k, pair, a=a):
                return pltpu.make_async_remote_copy(src_ref=src, dst_ref=dst, send_sem=send_sems.at[pair, a],
                                                    recv_sem=recv_sems.at[pair, a], device_id=peer_of(k), device_id_type=MESH_T)

            for k in range(1, NDEV):
                peer = peer_of(k)
                pid = 4 * peer[0] + 2 * peer[1] + peer[2]
                if mode != "gather":
                    src = self._piece(x_ref, mode, pid)
                    first.append(remote(src, out_ref.at[me], k, k - 1))
                    arrivals.append(remote(src, out_ref.at[pid], k, k - 1))
                elif k == 1:
                    first.append(remote(x_ref, out_ref.at[me], k, k - 1))
                    arrivals.append(remote(x_ref, out_ref.at[pid], k, k - 1))
                elif k % 2 == 0:
                    first.append(remote(x_ref, out_ref.at[me], k, k - 1))
                    relay.append((remote(x_ref, out_ref.at[pid], k, k - 1), remote(out_ref.at[pid], out_ref.at[pid], 1, k)))
                else:
                    arrivals.append(remote(x_ref, out_ref.at[pid], 1, k - 1))
        return local, first, relay, arrivals

    def start(self, x_refs, out_refs, sems):
        local, first, _, _ = self._copies(x_refs, out_refs, sems)
        for cp in local + first:
            cp.start()

    def finish(self, x_refs, out_refs, sems):
        local, first, relay, arrivals = self._copies(x_refs, out_refs, sems)
        for arrival, onward in relay:
            arrival.wait_recv()
            onward.start()
        for cp in arrivals:
            cp.wait_recv()
        for cp in first + [onward for _, onward in relay]:
            cp.wait_send()
        for cp in local:
            cp.wait()


def exchange(xs, modes, name):
    ex = Exchange(xs, modes)
    n = ex.n

    def body(*refs):
        ex.start(refs[:n], refs[n:2 * n], refs[2 * n:])
        ex.finish(refs[:n], refs[n:2 * n], refs[2 * n:])

    return pl.pallas_call(body, in_specs=ex.specs, out_specs=ex.specs, out_shape=ex.out_shape, scratch_shapes=ex.scratch,
                          compiler_params=pltpu.CompilerParams(has_side_effects=True), name=name)(*xs)


def _dot_f32(a, b, dn):
    return lax.dot_general(a, b, dn, preferred_element_type=F32, precision=lax.Precision.HIGHEST)


def ada_fwd(cg, c_ctx, ada_w, ada_b_loc, name):
    W = ada_w.shape[2]

    def body(cg_ref, cc_ref, w_ref, b_ref, o_ref):
        a = jnp.concatenate([_silu(cg_ref[...]), jnp.broadcast_to(_silu(cc_ref[...]), (NDEV, D))], axis=0)
        for i in range(2):
            o_ref[i] = _dot_f32(a, w_ref[i], _DN["nn"]) + b_ref[i]

    return pl.pallas_call(body, out_shape=jax.ShapeDtypeStruct((2, 2 * NDEV, W), F32),
                          compiler_params=_cp(), name=name)(cg, c_ctx, ada_w, ada_b_loc)


def ada_bwd(cg, c_ctx, ada_w, dm_loc, dm_all, name):
    W = ada_w.shape[2]

    def body(cg_ref, cc_ref, w_ref, dl_ref, da_ref, gw_ref, dcc_ref, gb_ref):
        a = jnp.concatenate([_silu(cg_ref[...]), jnp.broadcast_to(_silu(cc_ref[...]), (NDEV, D))], axis=0)
        dcc = jnp.zeros((1, D), F32)
        for i in range(2):
            dl = dl_ref[i]
            gw_ref[i] = _dot_f32(a, dl, _DN["tn"])
            dctx = jnp.sum(dl[NDEV:], axis=0, keepdims=True)
            dcc = dcc + _dot_f32(dctx, w_ref[i], _DN["nt"])
        dcc_ref[...] = dcc
        gb_ref[...] = jnp.sum(da_ref[...], axis=0)

    return pl.pallas_call(body, out_shape=[jax.ShapeDtypeStruct((2, D, W), F32), jax.ShapeDtypeStruct((1, D), F32),
                                           jax.ShapeDtypeStruct((2, 3 * D), F32)],
                          compiler_params=_cp(), name=name)(cg, c_ctx, ada_w, dm_loc, dm_all)


def cctx_finish(parts, c_ctx, name):
    def body(p_ref, cc_ref, o_ref):
        o_ref[...] = jnp.sum(p_ref[...], axis=0, keepdims=True) * _dsilu(cc_ref[...])

    return pl.pallas_call(body, out_shape=jax.ShapeDtypeStruct((1, D), F32), name=name)(parts, c_ctx)


def _adamw_update(g_ref, w_ref, m_ref, v_ref, go_ref, d_ref, mo_ref, vo_ref):
    g = g_ref[0].astype(F32)
    for s in range(1, g_ref.shape[0]):
        g = g + g_ref[s].astype(F32)
    mn = B1 * m_ref[...] + (1.0 - B1) * g
    vn = B2 * v_ref[...] + (1.0 - B2) * g * g
    go_ref[...] = g
    mo_ref[...] = mn
    vo_ref[...] = vn
    d_ref[...] = -LR * ((mn * (1.0 / (1.0 - B1 ** STEP))) / (jnp.sqrt(vn * (1.0 / (1.0 - B2 ** STEP))) + AEPS) + WD * w_ref[...])


ADAMW_PARTS = 4


def adamw_rows(items, name, rode=None, modes=None):
    in_specs, out_specs, out_shape, args = [], [], [], []
    for g, w, m, v in items:
        n, R, C = g.shape
        tr = R // ADAMW_PARTS
        spec = pl.BlockSpec((tr, C), lambda i, j: (i, 0))
        in_specs += [pl.BlockSpec((n, tr, C), lambda i, j: (0, i, 0)), spec, spec, spec]
        args += [g, w, m, v]
    for g, w, m, v in items:
        tr = w.shape[0] // ADAMW_PARTS
        out_specs += [pl.BlockSpec((tr, w.shape[1]), lambda i, j: (i, 0))] * 4
        out_shape += [jax.ShapeDtypeStruct(w.shape, F32)] * 4
    res, got = _ride_call(_adamw_body(len(items)), (ADAMW_PARTS, 1), in_specs, out_specs, out_shape,
                          Exchange(rode, modes) if rode else None, rode, name, args)
    return [res[4 * t:4 * t + 4] for t in range(len(items))], got


def _adamw_body(k):
    def body(*refs):
        for t in range(k):
            _adamw_update(*refs[4 * t:4 * t + 4], *refs[4 * k + 4 * t:4 * k + 4 * t + 4])
    return body


def adamw_multi(items, grid, name):
    k = len(items)
    ins, in_specs, out_specs, out_shape = [], [], [], []
    for g, g_spec, w, m, v, w_spec in items:
        ins += [g, w, m, v]
        in_specs += [g_spec, w_spec, w_spec, w_spec]
    for g, g_spec, w, m, v, w_spec in items:
        out_specs += [w_spec] * 4
        out_shape += [jax.ShapeDtypeStruct(w.shape, F32)] * 4
    res = pl.pallas_call(_adamw_body(k), grid=grid, in_specs=in_specs, out_specs=out_specs, out_shape=out_shape,
                         compiler_params=_cp(("arbitrary",) * len(grid)), name=name)(*ins)
    return [res[4 * t:4 * t + 4] for t in range(k)]


def _whole(a, grid_rank):
    zeros = (0,) * a.ndim
    return pl.BlockSpec(a.shape, lambda *idx: zeros)


def sum_slots(xs, name):
    def body(*refs):
        for x_ref, o_ref in zip(refs[:len(xs)], refs[len(xs):]):
            acc = x_ref[0]
            for s in range(1, NDEV):
                acc = acc + x_ref[s]
            o_ref[...] = acc

    return pl.pallas_call(body, out_shape=[jax.ShapeDtypeStruct(x.shape[1:], F32) for x in xs],
                          compiler_params=_cp(), name=name)(*xs)


def _col_shards(g):
    R, N = g.shape
    return g.reshape(R, NDEV, N // NDEV).transpose(1, 0, 2)


def _vec2(v):
    return jnp.broadcast_to(v.reshape(1, 1, -1), (2, 1, v.size))


SHARD_ROWS = {"mla_w_in": 192, "mla_w_uq": 192, "mla_w_ukv": 256, "s5_w_in": 256}


def _t_shard(wsh, rows):
    t = wsh[0].T.astype(BF16)
    return jnp.pad(t, ((0, rows - t.shape[0]), (0, 0)))


def _win_order():
    w = IN_W // NDEV
    perm = np.zeros((IN_WP, NDEV * SHARD_ROWS["mla_w_in"]), np.float32)
    first = QL + KVL + ROPE
    for c in range(IN_W):
        n = c + HEADS * VD if c < first else c - first
        perm[n, (c // w) * SHARD_ROWS["mla_w_in"] + c % w] = 1.0
    return jnp.asarray(perm, BF16)


def local_step(ctx, x, tgt, mod, Wt, small, l1_shards):
    T = LC + x.shape[0]
    xa = ("cat", ctx, x)
    sh = [mod[i, :, None, 0:D] for i in range(2)]
    sc = [mod[i, :, None, D:2 * D] for i in range(2)]
    gt = [mod[i, :, None, 2 * D:] for i in range(2)]
    ng = [_vec2(small["norm_g"][i]) for i in range(2)]
    qg, kvg = _vec2(small["mla_q_norm"]), _vec2(small["mla_kv_norm"])
    cosf, sinf, _, pmt = _rope_tables(T)

    (h0, p0, cqn, ckvn), _ = rowwise(st_l0_pre, [xa], [ng[0], sc[0], sh[0], qg, kvg],
                                     [(D, BF16), (IN_WP, F32), (QL, BF16), (KVL, BF16)], [], "l0_pre", mats=[Wt["mla_w_in"]])
    z0, cq, ckv = (p0, 0, HEADS * VD), (p0, HEADS * VD // QL, QL), (p0, (HEADS * VD + QL) // KVL, KVL)
    Q = project_q(cqn, Wt["mla_w_uq"], "l0_uq")
    K, V = project_kv(ckvn, Wt["mla_w_ukv"], p0, (HEADS * VD + QL + KVL) // 128, "l0_ukv")
    (o2, lse), got = attn_fwd(Q, K, V, "l0_attn", rode=l1_shards, modes="gather")
    Wt, small = dict(Wt), dict(small)
    for n, a in zip(L1_BIG, got):
        Wt[n] = a.reshape(-1, a.shape[-1])
    vecs = lax.bitcast_convert_type(got[-1].reshape(NDEV, 2, -1, 2), F32)
    small["s5_d"], small["s5_b_glu"] = vecs[:, 0, :].reshape(D), vecs[:, 1, :].reshape(D)
    (og, out0, x1), _ = rowwise(st_l0_post, [o2, z0, xa], [gt[0]], [(D, BF16), (D, BF16), (D, F32)], [], "l0_post",
                                mats=[Wt["mla_w_out"]])

    ls = small["s5_log_step"].reshape(2, G, 1)
    a_re, a_im = small["s5_a_re"].reshape(2, G, P), small["s5_a_im"].reshape(2, G, P)
    b_re = small["s5_b_re"].reshape(2, G, P, CH).transpose(0, 1, 3, 2)
    b_im = small["s5_b_im"].reshape(2, G, P, CH).transpose(0, 1, 3, 2)
    lam_re, lam_im, f_re, f_im = disc_fwd(a_re, a_im, ls, "s5_disc")
    f_re2, f_im2 = f_re.reshape(2, G, 1, P), f_im.reshape(2, G, 1, P)
    bb_re, bb_im = disc_b(f_re2, f_im2, b_re, b_im, "s5_disc_b")
    compact = lambda m: m.reshape(2, NJ, UB, P)
    bre, bim = compact(bb_re), compact(bb_im)
    cre, cim = compact(small["s5_c_re"]), compact(small["s5_c_im"])
    lam_re4, lam_im4 = lam_re.reshape(2, NJ, 1, SB), lam_im.reshape(2, NJ, 1, SB)

    (h1, p1), _ = rowwise(st_l1_pre, [x1], [ng[1], sc[1], sh[1]], [(D, BF16), (2 * D, F32)], [], "l1_pre", mats=[Wt["s5_w_in"]])
    u, z1 = (p1, 0, D), (p1, 1, D)
    yssm = scan_fwd(p1, lam_re4, lam_im4, bre, bim, cre, cim, "s5_scan")
    dvec, bglu = _vec2(small["s5_d"]), _vec2(small["s5_b_glu"])
    fg = _vec2(small["final_g"])
    lat_mask = jnp.stack([jnp.zeros((1, D), F32), jnp.ones((1, D), F32)])
    (y, y1b, gl, y3, out1, dx2), (dfg, lvec) = rowwise(
        st_l1_mlp, [yssm, u, z1, x1, ("lat", tgt)], [dvec, bglu, gt[1], fg, lat_mask],
        [(D, F32), (D, BF16), (D, BF16), (D, BF16), (D, BF16), (D, F32)], [D, 128], "l1_mlp",
        mats=[Wt["s5_w_glu"], Wt["s5_w_out"]])

    (dz1, dy, du_d), (dgt1, dbglu, dd), (g_w_out5, g_w_glu) = rowwise(
        st_l1_mlp_bwd, [dx2, out1, y3, y, gl, z1, u, y1b], [gt[1], bglu, dvec], [(D, BF16), (D, F32), (D, F32)], [D, D, D],
        "l1_mlp_b", mats=[Wt["s5_w_out"], Wt["s5_w_glu"]], out_accs=[(D, D), (D, D)])
    du_s, dlr, dli, dbre, dbim, dcre, dcim = scan_bwd(p1, dy, lam_re4, lam_im4, bre, bim, cre, cim, "s5_scan_b")
    dbb_re, dbb_im = dbre.reshape(2, G, CH, P), dbim.reshape(2, G, CH, P)
    g_c_re, g_c_im = dcre.reshape(2, G, CH, P), dcim.reshape(2, G, CH, P)
    gt_b_re, gt_b_im, dfr, dfi = disc_b_bwd(f_re2, f_im2, b_re, b_im, dbb_re, dbb_im, "s5_disc_b_b")
    g_b_re, g_b_im = gt_b_re.transpose(0, 1, 3, 2), gt_b_im.transpose(0, 1, 3, 2)
    g_a_re, g_a_im, g_ls = disc_a_bwd(a_re, a_im, ls, dlr.reshape(2, G, P), dli.reshape(2, G, P),
                                      dfr.reshape(2, G, P), dfi.reshape(2, G, P), "s5_disc_b_a")
    (dx1,), (dsh1, dsc1, dng1), (g_w_in5,) = rowwise(
        st_l1_tail_bwd, [du_d, du_s, dz1, h1, x1, dx2], [ng[1], sc[1]], [(D, F32)], [D, D, D], "l1_pre_b",
        mats=[Wt["s5_w_in"]], out_accs=[(NDEV, D, 2 * D // NDEV)])

    (do2, dz0), (dgt0,), (g_w_out,) = rowwise(st_l0_post_bwd, [dx1, out0, og, o2, z0], [gt[0]], [(D, F32), (D, F32)], [D],
                                              "l0_post_b", mats=[Wt["mla_w_out"]], out_accs=[(D, D)])
    rows8 = lambda g: g.reshape(NDEV, -1, g.shape[-1])
    both = lambda s: s[0, 0] + s[1, 0]
    dense = lambda g: g.reshape(2, G * P * CH // 128, 128)
    chunks = [dense(g_b_re), dense(g_b_im), g_c_re, g_c_im]
    l1_send = [g_w_in5, rows8(g_w_glu), rows8(g_w_out5), rows8(g_w_out),
               both(dd).reshape(NDEV, 1, -1), both(dbglu).reshape(NDEV, 1, -1)]
    (dQ, dK, dV), l1_recv = attn_bwd(Q, K, V, do2, "l0_attn_b", rode=l1_send + chunks,
                               modes=["lead"] * len(l1_send) + [a.shape[1] // NDEV for a in chunks])
    dq = rope_bwd(dQ, cosf, sinf, pmt, SCALE, "l0_rope_q_b")
    n_owned = len(l1_send)
    reduced = sum_slots(l1_recv[n_owned:], "sum_chunks")
    (dkv, dkr), chunk_all = split_kv_grads(dK, dV, "l0_kv_b", rode=[jnp.stack(reduced[:2]), jnp.stack(reduced[2:])],
                                           modes="gather")
    (grad_x,), (dqg, dkvg, dsh0, dsc0, dng0), (g_uq, g_ukv, g_p) = rowwise(
        st_l0_tail_bwd, [dq, dkv, dkr, dz0, cq, ckv, cqn, ckvn, h0, xa, dx1], [qg, kvg, ng[0], sc[0]],
        [(D, F32, "lat")], [QL, KVL, D, D, D], "l0_pre_b", mats=[Wt["mla_w_uq"], Wt["mla_w_ukv"], Wt["mla_w_in"]],
        out_accs=[(QL, HEADS * QK), (KVL, HEADS * KVW), (D, IN_WP)])
    g_w_uq, g_w_ukv = _col_shards(g_uq).astype(BF16), _col_shards(g_ukv).astype(BF16)
    g_w_in = _col_shards(jnp.concatenate([g_p[:, HEADS * VD:IN_W], g_p[:, :HEADS * VD]], axis=1)).astype(BF16)

    dmod = jnp.stack([jnp.concatenate([dsh0, dsc0, dgt0], axis=-1)[:, 0], jnp.concatenate([dsh1, dsc1, dgt1], axis=-1)[:, 0]])
    gbig = {"mla_w_in": g_w_in, "mla_w_uq": g_w_uq, "mla_w_ukv": g_w_ukv}
    gsmall = {"norm_g": jnp.stack([both(dng0), both(dng1)]), "mla_q_norm": both(dqg), "mla_kv_norm": both(dkvg),
              "s5_a_re": g_a_re, "s5_a_im": g_a_im, "s5_log_step": g_ls, "final_g": dfg[1, 0]}
    return lvec[1], grad_x, dmod, gbig, gsmall, l1_recv[:n_owned], chunk_all


COL_SHARDED = ("mla_w_in", "mla_w_uq", "mla_w_ukv", "s5_w_in")
ROW_SHARDED = ("mla_w_out", "s5_w_glu", "s5_w_out")
VEC_SHARDED = ("s5_d", "s5_b_glu")
BIG = COL_SHARDED + ROW_SHARDED
L0_BIG = ("mla_w_in", "mla_w_uq", "mla_w_ukv")
L1_BIG = ("s5_w_in", "s5_w_glu", "s5_w_out", "mla_w_out")
BITS16 = jnp.bfloat16
SMALL_RS = ("norm_g", "mla_q_norm", "mla_kv_norm", "s5_a_re", "s5_a_im", "s5_log_step", "s5_b_re", "s5_b_im",
            "s5_c_re", "s5_c_im", "final_g")
CHUNKED = ("s5_b_re", "s5_b_im", "s5_c_re", "s5_c_im")
DENSE = ("s5_b_re", "s5_b_im")
TINY = ("norm_g", "mla_q_norm", "mla_kv_norm", "s5_a_re", "s5_a_im", "s5_log_step", "final_g")
ORDER = ("c_ctx", "ada_w", "ada_b", "norm_g", "mla_w_in", "mla_q_norm", "mla_w_uq", "mla_kv_norm", "mla_w_ukv",
         "mla_w_out", "s5_w_in", "s5_a_re", "s5_a_im", "s5_log_step", "s5_b_re", "s5_b_im", "s5_c_re", "s5_c_im",
         "s5_d", "s5_w_glu", "s5_b_glu", "s5_w_out", "final_g")


def kernel(x, c, ctx, c_ctx, ada_w, ada_b, norm_g, mla_w_in, mla_q_norm, mla_w_uq, mla_kv_norm, mla_w_ukv, mla_w_out, s5_w_in, s5_a_re, s5_a_im, s5_log_step, s5_b_re, s5_b_im, s5_c_re, s5_c_im, s5_d, s5_w_glu, s5_b_glu, s5_w_out, final_g, loss_target, m_c_ctx, m_ada_w, m_ada_b, m_norm_g, m_mla_w_in, m_mla_q_norm, m_mla_w_uq, m_mla_kv_norm, m_mla_w_ukv, m_mla_w_out, m_s5_w_in, m_s5_a_re, m_s5_a_im, m_s5_log_step, m_s5_b_re, m_s5_b_im, m_s5_c_re, m_s5_c_im, m_s5_d, m_s5_w_glu, m_s5_b_glu, m_s5_w_out, m_final_g, v_c_ctx, v_ada_w, v_ada_b, v_norm_g, v_mla_w_in, v_mla_q_norm, v_mla_w_uq, v_mla_kv_norm, v_mla_w_ukv, v_mla_w_out, v_s5_w_in, v_s5_a_re, v_s5_a_im, v_s5_log_step, v_s5_b_re, v_s5_b_im, v_s5_c_re, v_s5_c_im, v_s5_d, v_s5_w_glu, v_s5_b_glu, v_s5_w_out, v_final_g):
    w = dict(c_ctx=c_ctx, ada_w=ada_w, ada_b=ada_b, norm_g=norm_g, mla_w_in=mla_w_in, mla_q_norm=mla_q_norm,
             mla_w_uq=mla_w_uq, mla_kv_norm=mla_kv_norm, mla_w_ukv=mla_w_ukv, mla_w_out=mla_w_out, s5_w_in=s5_w_in,
             s5_a_re=s5_a_re, s5_a_im=s5_a_im, s5_log_step=s5_log_step, s5_b_re=s5_b_re, s5_b_im=s5_b_im,
             s5_c_re=s5_c_re, s5_c_im=s5_c_im, s5_d=s5_d, s5_w_glu=s5_w_glu, s5_b_glu=s5_b_glu, s5_w_out=s5_w_out,
             final_g=final_g)
    m = dict(c_ctx=m_c_ctx, ada_w=m_ada_w, ada_b=m_ada_b, norm_g=m_norm_g, mla_w_in=m_mla_w_in, mla_q_norm=m_mla_q_norm,
             mla_w_uq=m_mla_w_uq, mla_kv_norm=m_mla_kv_norm, mla_w_ukv=m_mla_w_ukv, mla_w_out=m_mla_w_out,
             s5_w_in=m_s5_w_in, s5_a_re=m_s5_a_re, s5_a_im=m_s5_a_im, s5_log_step=m_s5_log_step, s5_b_re=m_s5_b_re,
             s5_b_im=m_s5_b_im, s5_c_re=m_s5_c_re, s5_c_im=m_s5_c_im, s5_d=m_s5_d, s5_w_glu=m_s5_w_glu,
             s5_b_glu=m_s5_b_glu, s5_w_out=m_s5_w_out, final_g=m_final_g)
    v = dict(c_ctx=v_c_ctx, ada_w=v_ada_w, ada_b=v_ada_b, norm_g=v_norm_g, mla_w_in=v_mla_w_in, mla_q_norm=v_mla_q_norm,
             mla_w_uq=v_mla_w_uq, mla_kv_norm=v_mla_kv_norm, mla_w_ukv=v_mla_w_ukv, mla_w_out=v_mla_w_out,
             s5_w_in=v_s5_w_in, s5_a_re=v_s5_a_re, s5_a_im=v_s5_a_im, s5_log_step=v_s5_log_step, s5_b_re=v_s5_b_re,
             s5_b_im=v_s5_b_im, s5_c_re=v_s5_c_re, s5_c_im=v_s5_c_im, s5_d=v_s5_d, s5_w_glu=v_s5_w_glu,
             s5_b_glu=v_s5_b_glu, s5_w_out=v_s5_w_out, final_g=v_final_g)

    me = 4 * lax.axis_index("x") + 2 * lax.axis_index("y") + lax.axis_index("c")
    WA = ada_w.shape[2]

    def shard(n):
        return _t_shard(w[n], SHARD_ROWS[n]) if n in COL_SHARDED else w[n][0].astype(BF16)

    wgot = exchange([c] + [shard(n) for n in L0_BIG], "gather", "gather_w")

    cg = wgot[0].reshape(NDEV, D)
    cc2 = c_ctx.reshape(1, D)
    ada_b_loc = lax.dynamic_slice_in_dim(ada_b.reshape(2, 3 * D // WA, WA), me, 1, axis=1)
    part = ada_fwd(cg, cc2, ada_w, ada_b_loc, "ada_fwd")
    Wt = {n: a.reshape(-1, a.shape[-1]) for n, a in zip(L0_BIG, wgot[1:])}
    Wt["mla_w_in"], (pg,) = mm(_win_order(), Wt["mla_w_in"], "nn", "w_in_order", out_dtype=BF16, rode=[part], modes="gather")
    mod_l = lax.dynamic_index_in_dim(pg, me, axis=2, keepdims=False).transpose(1, 0, 2).reshape(2, 3 * D)
    mod_c = pg[:, :, NDEV, :].transpose(1, 0, 2).reshape(2, 3 * D)
    mod = jnp.stack([mod_c, mod_l], axis=1)
    vec_bits = lax.bitcast_convert_type(jnp.concatenate([s5_d, s5_b_glu], axis=0), BITS16).reshape(2, -1)
    small = {n: w[n] for n in SMALL_RS}

    lvec, grad_x, dmod, gbig, gsmall, l1_recv, (bb_all, cc_all) = local_step(
        ctx[0], x[0], loss_target[0], mod, Wt, small, [shard(n) for n in L1_BIG] + [vec_bits])
    grad_x = grad_x[None]

    recv = dict(zip(L1_BIG + VEC_SHARDED, l1_recv))
    out = {}

    def keep(n, res):
        for key, arr in zip("gdmv", res):
            out[key, n] = arr.reshape(w[n].shape)

    kshape = lambda n: w[n].shape if w[n].ndim > 1 else (1, w[n].size)
    flat = jnp.concatenate([gsmall[n].reshape(-1) for n in TINY] + [dmod.reshape(-1), lvec.reshape(-1)])[None]
    *l0_recv, flat_all = exchange([gbig[n] for n in L0_BIG] + [flat], ["lead"] * len(L0_BIG) + ["gather"], "scatter_grads")
    chunk_all = [bb_all[:, 0], bb_all[:, 1], cc_all[:, 0], cc_all[:, 1]]
    tiny_all, off = [], 0
    for n in TINY:
        tiny_all.append(flat_all[:, 0, off:off + w[n].size].reshape((NDEV,) + kshape(n)))
        off += w[n].size
    dm_all = flat_all[:, 0, off:off + dmod.size].reshape((NDEV,) + dmod.shape)
    loss = sum_slots([flat_all[:, :, off + dmod.size:]], "loss_sum")[0][0, 0]

    dm_cols = lax.dynamic_slice_in_dim(dm_all.reshape(NDEV, 2, 2, 3 * D // WA, WA), me, 1, axis=3)[:, :, :, 0]
    dm_loc = jnp.concatenate([dm_cols[:, :, 1].transpose(1, 0, 2), dm_cols[:, :, 0].transpose(1, 0, 2)], axis=1)
    g_ada_w, dcc_part, g_ada_b = ada_bwd(cg, cc2, ada_w, dm_loc, dm_all.transpose(0, 2, 1, 3).reshape(2 * NDEV, 2, 3 * D), "ada_bwd")
    dcc_all = exchange([dcc_part], "gather", "gather_dcc")[0].reshape(NDEV, D)
    g_c_ctx = cctx_finish(dcc_all, cc2, "cctx_finish")

    flat2 = lambda t: t.reshape(-1, t.shape[-1])
    recv.update(dict(zip(L0_BIG, l0_recv)))
    big = [(recv[n], w[n][0], m[n][0], v[n][0]) for n in BIG]
    big.append((flat2(g_ada_w)[None], flat2(ada_w), flat2(m_ada_w), flat2(v_ada_w)))
    for n, r in zip(BIG + ("ada_w",), adamw_rows(big, "adamw_big")[0]):
        keep(n, r)
    items = []
    halves = 2
    for n, g in zip(CHUNKED, chunk_all):
        blk = (1, 1, G // halves) + w[n].shape[3:]
        g = jnp.moveaxis(g, 0, 1).reshape(w[n].shape)
        g_spec = pl.BlockSpec((1,) + blk, lambda d, s: (0, 0, d, s, 0, 0))
        items.append((g[None], g_spec, w[n], m[n], v[n], pl.BlockSpec(blk, lambda d, s: (0, d, s, 0, 0))))
    for n, res in zip(CHUNKED, adamw_multi(items, (2, halves), "adamw_bc")):
        keep(n, res)
    tiny_g = dict(zip(TINY, tiny_all))
    tiny_g.update({n: recv[n] for n in VEC_SHARDED})
    tiny_g["c_ctx"], tiny_g["ada_b"] = g_c_ctx[None], g_ada_b[None]
    names = list(tiny_g)
    items = [(tiny_g[n], _whole(tiny_g[n], 1)) + tuple(t[n].reshape(kshape(n)) for t in (w, m, v))
             + (pl.BlockSpec(kshape(n), lambda i, r=len(kshape(n)): (0,) * r),) for n in names]
    for n, res in zip(names, adamw_multi(items, (1,), "adamw_small")):
        keep(n, res)

    return (loss, grad_x, *[out["g", n] for n in ORDER], *[out["d", n] for n in ORDER],
            *[out["m", n] for n in ORDER], *[out["v", n] for n in ORDER])
```

```python
import math

import numpy as np
import jax
import jax.numpy as jnp
from jax import lax
from jax.experimental import pallas as pl
from jax.experimental.pallas import tpu as pltpu

F32 = jnp.float32
BF16 = jnp.bfloat16

D = 1024
L = 2048
LC = 256
NDEV = 8
GRID_W = 64
EPS = 1e-6
HEADS = 16
NOPE = 64
ROPE = 32
QK = NOPE + ROPE
VD = 64
IN_W = 256 + 128 + ROPE + HEADS * 64
IN_WP = 1536
QL = 256
KVL = 128
SCALE = QK ** -0.5
LOG2E = math.log2(math.e)
THETA = 10000.0
G = 64
P = 64
CH = 16
GB = 8
NJ = G // GB
UB = GB * CH
SB = GB * P
SEG = 16
TB = 256
VMEM_LIMIT = 56 * 1024 * 1024
B1, B2, LR, AEPS, WD, STEP = 0.9, 0.999, 0.001, 1e-8, 0.01, 10
MESH_T = pl.DeviceIdType.MESH


def _cp(sem=None):
    return pltpu.CompilerParams(dimension_semantics=sem, vmem_limit_bytes=VMEM_LIMIT)


def _sig(x):
    return 1.0 / (1.0 + jnp.exp(-x))


def _silu(x):
    return x * _sig(x)


def _dsilu(x):
    s = _sig(x)
    return s * (1.0 + x * (1.0 - s))


_GK = math.sqrt(2.0 / math.pi)


def _gelu(x):
    return 0.5 * x * (1.0 + jnp.tanh(_GK * (x + 0.044715 * x * x * x)))


def _dgelu(x):
    t = jnp.tanh(_GK * (x + 0.044715 * x * x * x))
    return 0.5 * (1.0 + t) + 0.5 * x * (1.0 - t * t) * _GK * (1.0 + 3 * 0.044715 * x * x)


def _rs(x):
    return lax.rsqrt(jnp.mean(x * x, axis=-1, keepdims=True) + EPS)


def _sum0(x):
    return jnp.sum(x, axis=0, keepdims=True)


def st_norm_mod(x, g, sc, sh):
    y = x * _rs(x) * g
    return (y * (1.0 + sc) + sh,), ()


def st_norm_mod_bwd(x, dh, dres, g, sc):
    r = _rs(x)
    xn = x * r
    y = xn * g
    dy = dh * (1.0 + sc)
    dxn = dy * g
    dx = r * (dxn - xn * jnp.mean(dxn * xn, axis=-1, keepdims=True))
    return (dres + dx,), (_sum0(dh), _sum0(dh * y), _sum0(dy * xn))


def st_rms(x, g):
    return (x * _rs(x) * g,), ()


def st_rms_bwd(x, dy, g):
    r = _rs(x)
    n = x * r
    dn = dy * g
    dx = r * (dn - n * jnp.mean(dn * n, axis=-1, keepdims=True))
    return (dx,), (_sum0(dy * n),)


def st_rms2(x1, x2, g1, g2):
    return st_rms(x1, g1)[0] + st_rms(x2, g2)[0], ()


def st_rms2_bwd(x1, dy1, x2, dy2, g1, g2):
    (d1,), (s1,) = st_rms_bwd(x1, dy1, g1)
    (d2,), (s2,) = st_rms_bwd(x2, dy2, g2)
    return (d1, d2), (s1, s2)


def st_gate_bwd(dog, o, z):
    return (dog * _silu(z), dog * o * _dsilu(z)), ()


def st_resid_bwd(dx, out, gt):
    return (dx * gt,), (_sum0(dx * out),)


def st_s5a(yssm, u, d):
    y = yssm + d * u
    return (y, _gelu(y)), ()


def st_s5b_bwd(dy3, y, gl, z, b):
    y1 = _gelu(y)
    s = _sig(gl + b)
    dy2 = dy3 * _silu(z)
    dz = dy3 * y1 * s * _dsilu(z)
    dgl = dy2 * y1 * s * (1.0 - s)
    return (dgl, dz, dy2 * s), (_sum0(dgl),)


def st_s5a_bwd(dy1a, dy1b, y, u, d):
    dy = (dy1a + dy1b) * _dgelu(y)
    return (dy, dy * d), (_sum0(dy * u),)


def st_l0_pre(x, g, sc, sh, qg, kvg, w_in):
    hb = st_norm_mod(x, g, sc, sh)[0][0].astype(BF16)
    p = lax.dot_general(hb, w_in, _DN["nt"], preferred_element_type=F32)
    cq, ckv = p[:, HEADS * VD:HEADS * VD + QL], p[:, HEADS * VD + QL:HEADS * VD + QL + KVL]
    return (hb, p) + st_rms2(cq, ckv, qg, kvg)[0], ()


def st_l0_tail_bwd(dq, dkv, dkr, dz, cq, ckv, cqn, ckvn, h, x, dres, qg, kvg, g, sc, w_uq, w_ukv, w_in):
    dcqn = jnp.dot(dq, w_uq, preferred_element_type=F32)
    dckvn = jnp.dot(dkv, w_ukv, preferred_element_type=F32)
    (dcq, dckv), (dqg, dkvg) = st_rms2_bwd(cq, dcqn, ckv, dckvn, qg, kvg)
    dp = jnp.concatenate([dz, dcq, dckv, dkr], axis=1).astype(BF16)
    dh = jnp.dot(dp, w_in, preferred_element_type=F32)
    outs, sums = st_norm_mod_bwd(x, dh, dres, g, sc)
    tn = lambda a, b: lax.dot_general(a, b, _DN["tn"], preferred_element_type=F32)
    return outs, (dqg, dkvg) + sums, (tn(cqn, dq), tn(ckvn, dkv), tn(h, dp))


def st_l1_pre(x, g, sc, sh, w_in):
    hb = st_norm_mod(x, g, sc, sh)[0][0].astype(BF16)
    return (hb, lax.dot_general(hb, w_in, _DN["nt"], preferred_element_type=F32)), ()


def st_l1_tail_bwd(du_a, du_b, dz, h, x, dres, g, sc, w_in):
    dp = jnp.concatenate([(du_a + du_b).astype(BF16), dz], axis=1)
    dh = jnp.dot(dp, w_in, preferred_element_type=F32)
    outs, sums = st_norm_mod_bwd(x, dh, dres, g, sc)
    w = dp.shape[1] // NDEV
    shards = [lax.dot_general(h, dp[:, r * w:(r + 1) * w], _DN["tn"], preferred_element_type=F32) for r in range(NDEV)]
    return outs, sums, (jnp.stack(shards),)


def st_l0_post(o, z, x, gt, w_out):
    og = (o * _silu(z)).astype(BF16)
    out = jnp.dot(og, w_out, preferred_element_type=F32)
    return (og, out, x + gt * out), ()


def st_l0_post_bwd(dx1, out, og, o, z, gt, w_out):
    (dout,), (dgt,) = st_resid_bwd(dx1, out.astype(F32), gt)
    doutb = dout.astype(BF16)
    dog = lax.dot_general(doutb, w_out, _DN["nt"], preferred_element_type=F32)
    return st_gate_bwd(dog, o, z)[0], (dgt,), (lax.dot_general(og, doutb, _DN["tn"], preferred_element_type=F32),)


def st_l1_mlp(yssm, u, z, x1, tgt, d, bglu, gt, fg, mask, w_glu, w_out):
    (y, y1), _ = st_s5a(yssm, u, d)
    y1b = y1.astype(BF16)
    gl = jnp.dot(y1b, w_glu, preferred_element_type=F32)
    y3 = (y1 * _sig(gl + bglu) * _silu(z)).astype(BF16)
    out = jnp.dot(y3, w_out, preferred_element_type=F32)
    (dx2,), sums = st_final(x1 + gt * out, tgt, fg, mask)
    return (y, y1b, gl, y3, out, dx2), sums


def st_l1_mlp_bwd(dx2, out, y3, y, gl, z, u, y1b, gt, bglu, d, w_out, w_glu):
    out, gl = out.astype(F32), gl.astype(F32)
    (dout,), (dgt,) = st_resid_bwd(dx2, out, gt)
    doutb = dout.astype(BF16)
    dy3 = lax.dot_general(doutb, w_out, _DN["nt"], preferred_element_type=F32)
    (dgl, dz, dy1a), (dbglu,) = st_s5b_bwd(dy3, y, gl, z, bglu)
    dglb = dgl.astype(BF16)
    dy1b = lax.dot_general(dglb, w_glu, _DN["nt"], preferred_element_type=F32)
    (dy, du), (dd,) = st_s5a_bwd(dy1a, dy1b, y, u, d)
    g_w_out = lax.dot_general(y3, doutb, _DN["tn"], preferred_element_type=F32)
    g_w_glu = lax.dot_general(y1b, dglb, _DN["tn"], preferred_element_type=F32)
    return (dz, dy, du), (dgt, dbglu, dd), (g_w_out, g_w_glu)


def st_final(x2, tgt, g, mask):
    r = _rs(x2)
    n = x2 * r
    e = n * g - tgt
    dyo = e * (1.0 / D)
    dn = dyo * g
    dx = r * (dn - n * jnp.mean(dn * n, axis=-1, keepdims=True))
    lsum = jnp.sum(_sum0(e * e), axis=1, keepdims=True) * (0.5 / D)
    return (dx * mask,), (_sum0(dyo * n), jnp.broadcast_to(lsum, (1, 128)))


def rowwise(fn, rows, vecs, out_rows, out_sums, name, mats=(), out_accs=()):
    lat_blk = lambda i: jnp.maximum(i - 1, 0)
    arrays, in_specs, pick = [], [], []
    for a in rows:
        if not isinstance(a, tuple):
            a = (a, 0, a.shape[1])
        tag = a[0] if isinstance(a[0], str) else None
        if tag == "cat":
            _, ctx, x = a
            arrays += [ctx, x]
            in_specs += [pl.BlockSpec((TB, ctx.shape[1]), lambda i: (0, 0)),
                         pl.BlockSpec((TB, x.shape[1]), lambda i: (lat_blk(i), 0))]
            pick.append(2)
        elif tag == "lat":
            arrays.append(a[1])
            in_specs.append(pl.BlockSpec((TB, a[1].shape[1]), lambda i: (lat_blk(i), 0)))
            pick.append(1)
        else:
            arr, cb, width = a
            arrays.append(arr)
            in_specs.append(pl.BlockSpec((TB, width), lambda i, cb=cb: (i, cb)))
            pick.append(1)
    T = LC + L
    nin, nv, nm, no, ns = len(arrays), len(vecs), len(mats), len(out_rows), len(out_sums)

    def body(*refs):
        i = pl.program_id(0)
        vals, k = [], 0
        for p in pick:
            if p == 2:
                vals.append(jnp.where(i == 0, refs[k][...], refs[k + 1][...]))
            else:
                vals.append(refs[k][...])
            k += p
        vals += [r[0] for r in refs[nin:nin + nv]] + [r[...] for r in refs[nin + nv:nin + nv + nm]]
        res = fn(*vals)
        first_out = nin + nv + nm
        for r, o in zip(refs[first_out:first_out + no], res[0]):
            r[...] = o.astype(r.dtype)
        sum_refs = refs[first_out + no:first_out + no + ns]
        if sum_refs:
            @pl.when(i <= 1)
            def _():
                for r in sum_refs:
                    r[...] = jnp.zeros_like(r)
            for r, s in zip(sum_refs, res[1]):
                r[0] += s
        na = len(out_accs)
        if na:
            acc_out, acc = refs[first_out + no + ns:first_out + no + ns + na], refs[first_out + no + ns + na:]

            @pl.when(i == 0)
            def _():
                for r in acc:
                    r[...] = jnp.zeros_like(r)
            for r, a in zip(acc, res[2]):
                r[...] += a

            @pl.when(i == T // TB - 1)
            def _():
                for o, r in zip(acc_out, acc):
                    o[...] = r[...].astype(o.dtype)

    kind = lambda i: (jnp.minimum(i, 1), 0, 0)
    in_specs += [pl.BlockSpec((1, 1, v.shape[2]), kind) for v in vecs]
    in_specs += [pl.BlockSpec(m.shape, lambda i: (0, 0), pipeline_mode=pl.Buffered(1)) for m in mats]
    out_specs, out_shape = [], []
    for o in out_rows:
        lat = len(o) == 3
        out_specs.append(pl.BlockSpec((TB, o[0]), (lambda i: (lat_blk(i), 0)) if lat else (lambda i: (i, 0))))
        out_shape.append(jax.ShapeDtypeStruct((L if lat else T, o[0]), o[1]))
    out_specs += [pl.BlockSpec((1, 1, c), kind) for c in out_sums]
    out_shape += [jax.ShapeDtypeStruct((2, 1, c), F32) for c in out_sums]
    out_specs += [pl.BlockSpec(s, lambda i, r=len(s): (0,) * r) for s in out_accs]
    out_shape += [jax.ShapeDtypeStruct(s, BF16) for s in out_accs]
    res = pl.pallas_call(body, grid=(T // TB,), in_specs=in_specs, out_specs=out_specs, out_shape=out_shape,
                         scratch_shapes=[pltpu.VMEM(s, F32) for s in out_accs],
                         compiler_params=_cp(("arbitrary",)), name=name)(*arrays, *vecs, *mats)
    if out_accs:
        return res[:no], res[no:no + ns], res[no + ns:]
    return res[:no], res[no:]


_DN = {"nn": (((1,), (0,)), ((), ())), "nt": (((1,), (1,)), ((), ())), "tn": (((0,), (0,)), ((), ()))}


def mm(a, b, mode, name, out_dtype=F32, tm=None, tn=None, rode=None, modes=None):
    if mode == "nn":
        (M, K), (_, N) = a.shape, b.shape
    elif mode == "nt":
        (M, K), (N, _) = a.shape, b.shape
    else:
        (K, M), (_, N) = a.shape, b.shape
    if tm is None:
        tm = next((t for t in (768, 512, 256) if M % t == 0 and M > t), M)
    tn = N if tn is None else tn
    dn = _DN[mode]

    def body(a_ref, b_ref, o_ref):
        o_ref[...] = lax.dot_general(a_ref[...].astype(BF16), b_ref[...].astype(BF16), dn,
                                     preferred_element_type=F32).astype(o_ref.dtype)

    a_spec = pl.BlockSpec((K, tm), lambda i, j: (0, i)) if mode == "tn" else pl.BlockSpec((tm, K), lambda i, j: (i, 0))
    b_spec = pl.BlockSpec((tn, K), lambda i, j: (j, 0)) if mode == "nt" else pl.BlockSpec((K, tn), lambda i, j: (0, j))
    (prod,), got = _ride_call(body, (M // tm, N // tn), [a_spec, b_spec], [pl.BlockSpec((tm, tn), lambda i, j: (i, j))],
                              [jax.ShapeDtypeStruct((M, N), out_dtype)], Exchange(rode, modes) if rode else None, rode,
                              name, (a, b))
    return prod, got


def _rope_tables(T, width=QK, first=NOPE):
    nlat = T - LC
    pos = np.arange(nlat)
    row, col = pos // GRID_W, pos % GRID_W
    half = ROPE // 2
    inv = 1.0 / (THETA ** (np.arange(0, half, 2, dtype=np.float64) / half))
    cosf = np.ones((T, width), np.float64)
    sinf = np.zeros((T, width), np.float64)
    perm = np.zeros((width, width), np.float32)
    for m in range(ROPE):
        j = first + m
        blk, w = m // half, m % half
        ang = (row if blk == 0 else col)[:, None] * inv[None, :]
        f = w % (half // 2)
        cosf[LC:, j] = np.cos(ang[:, f])
        if w < half // 2:
            sinf[LC:, j] = -np.sin(ang[:, f])
            perm[j + half // 2, j] = 1.0
        else:
            sinf[LC:, j] = np.sin(ang[:, f])
            perm[j - half // 2, j] = 1.0
    return jnp.asarray(cosf, F32), jnp.asarray(sinf, F32), jnp.asarray(perm, BF16), jnp.asarray(perm.T, BF16)


def _exact_perm(x, pm):
    hi = x.astype(BF16)
    r1 = x - hi.astype(F32)
    mid = r1.astype(BF16)
    lo = (r1 - mid.astype(F32)).astype(BF16)
    dot = lambda a: jnp.dot(a, pm, preferred_element_type=F32)
    return dot(hi) + dot(mid) + dot(lo)


def _rot(x, cv, sv, pv, inverse):
    if inverse:
        return x * cv + _exact_perm(x * sv, pv)
    return x * cv + _exact_perm(x, pv) * sv


def rope_bwd(dx, cosf, sinf, pmt, scale, name):
    H, T, _ = dx.shape

    def body(x_ref, c_ref, s_ref, p_ref, o_ref):
        cv, sv, pv = c_ref[...], s_ref[...], p_ref[...]
        for h in range(H):
            o_ref[:, pl.ds(h * QK, QK)] = (_rot(x_ref[h], cv, sv, pv, True) * scale).astype(o_ref.dtype)

    return pl.pallas_call(
        body, grid=(T // TB,),
        in_specs=[pl.BlockSpec((H, TB, QK), lambda i: (0, i, 0)), pl.BlockSpec((TB, QK), lambda i: (i, 0)),
                  pl.BlockSpec((TB, QK), lambda i: (i, 0)), pl.BlockSpec((QK, QK), lambda i: (0, 0))],
        out_specs=pl.BlockSpec((TB, H * QK), lambda i: (i, 0)), out_shape=jax.ShapeDtypeStruct((T, H * QK), BF16),
        compiler_params=_cp(("parallel",)), name=name)(dx, cosf, sinf, pmt)


KVW = NOPE + VD


def project_q(cqn, w, name):
    T = cqn.shape[0]
    cosf, sinf, _, _ = _rope_tables(T, 128, NOPE)
    wp = jnp.pad(w.reshape(HEADS, QK, QL), ((0, 0), (0, 128 - QK), (0, 0))).reshape(HEADS * 128, QL)

    def body(a_ref, w_ref, c_ref, s_ref, o_ref):
        a, cv, sv = a_ref[...], c_ref[...], s_ref[...]
        first_of_pair = lax.bitwise_and(lax.broadcasted_iota(jnp.int32, (TB, 128), 1), ROPE // 4) == 0
        for h in range(HEADS):
            qh = _dotf(a, w_ref[pl.ds(h * 128, 128), :], "nt")
            swap = jnp.where(first_of_pair, pltpu.roll(qh, 128 - ROPE // 4, 1), pltpu.roll(qh, ROPE // 4, 1))
            o_ref[h] = ((qh * cv + swap * sv) * (SCALE * LOG2E))[:, :QK].astype(BF16)

    rows = lambda c: pl.BlockSpec((TB, c), lambda i: (i, 0))
    return pl.pallas_call(
        body, grid=(T // TB,), in_specs=[rows(QL), pl.BlockSpec(wp.shape, lambda i: (0, 0)), rows(128), rows(128)],
        out_specs=pl.BlockSpec((HEADS, TB, QK), lambda i: (0, i, 0)), out_shape=jax.ShapeDtypeStruct((HEADS, T, QK), BF16),
        compiler_params=_cp(("parallel",)), name=name)(cqn, wp, cosf, sinf)


def project_kv(ckvn, w, p0, kr_block, name):
    T = ckvn.shape[0]
    assert KVW == 128 and NOPE == VD
    cosf, sinf, pm, _ = _rope_tables(T, 128, 0)

    def body(a_ref, w_ref, kr_ref, c_ref, s_ref, p_ref, k_ref, v_ref):
        a = a_ref[...]
        is_nope = lax.broadcasted_iota(jnp.int32, (TB, KVW), 1) < NOPE
        kr_at = pltpu.roll(_rot(kr_ref[...], c_ref[...], s_ref[...], p_ref[...], False), NOPE, 1)
        for h in range(HEADS):
            kv = _dotf(a, w_ref[pl.ds(h * KVW, KVW), :], "nt")
            k_ref[h] = jnp.where(is_nope, kv, kr_at)[:, :QK].astype(BF16)
            v_ref[h] = pltpu.roll(kv, VD, 1)[:, :VD].astype(BF16)

    rows = lambda c: pl.BlockSpec((TB, c), lambda i: (i, 0))
    const = lambda x: pl.BlockSpec(x.shape, lambda i: (0, 0))
    return pl.pallas_call(
        body, grid=(T // TB,),
        in_specs=[rows(KVL), const(w), pl.BlockSpec((TB, 128), lambda i: (i, kr_block)), rows(128), rows(128), const(pm)],
        out_specs=[pl.BlockSpec((HEADS, TB, QK), lambda i: (0, i, 0)), pl.BlockSpec((HEADS, TB, VD), lambda i: (0, i, 0))],
        out_shape=[jax.ShapeDtypeStruct((HEADS, T, QK), BF16), jax.ShapeDtypeStruct((HEADS, T, VD), BF16)],
        compiler_params=_cp(("parallel",)), name=name)(ckvn, w, p0, cosf, sinf, pm)


def split_kv_grads(dk, dv, name, rode=None, modes=None):
    H, T, _ = dk.shape
    cosf, sinf, _, pmt = _rope_tables(T, 128, 0)
    to_rope_block = np.zeros((QK, 128), np.float32)
    to_rope_block[NOPE + np.arange(ROPE), np.arange(ROPE)] = 1.0
    to_rope_block = jnp.asarray(to_rope_block, BF16)

    def body(dk_ref, dv_ref, c_ref, s_ref, p_ref, sel_ref, dkv_ref, dkr_ref):
        total = None
        for h in range(H):
            dkh = dk_ref[h] * (1.0 / LOG2E)
            total = dkh if total is None else total + dkh
            dkv_ref[:, pl.ds(h * KVW, NOPE)] = dkh[:, :NOPE].astype(BF16)
            dkv_ref[:, pl.ds(h * KVW + NOPE, VD)] = dv_ref[h].astype(BF16)
        dkr_ref[...] = _rot(_exact_perm(total, sel_ref[...]), c_ref[...], s_ref[...], p_ref[...], True)

    rows = lambda c: pl.BlockSpec((TB, c), lambda i, j: (i, 0))
    const = lambda a: pl.BlockSpec(a.shape, lambda i, j: (0, 0))
    return _ride_call(
        body, (T // TB, 1),
        [pl.BlockSpec((H, TB, QK), lambda i, j: (0, i, 0)), pl.BlockSpec((H, TB, VD), lambda i, j: (0, i, 0)),
         rows(128), rows(128), const(pmt), const(to_rope_block)],
        [rows(H * KVW), rows(128)],
        [jax.ShapeDtypeStruct((T, H * KVW), BF16), jax.ShapeDtypeStruct((T, 128), F32)],
        Exchange(rode, modes) if rode else None, rode, name, (dk, dv, cosf, sinf, pmt, to_rope_block))


HB = 4
HBF = 8


def _by_query_block(run, T):
    @pl.when(pl.program_id(1) == 0)
    def _():
        run(LC)

    @pl.when(pl.program_id(1) > 0)
    def _():
        run(T)


def _with_rider(body, nin, nout, ride, grid):
    if ride is None:
        return body
    n = ride.n

    def wrapped(*refs):
        ins, xs = refs[:nin], refs[nin:nin + n]
        outs, got = refs[nin + n:nin + n + nout], refs[nin + n + nout:nin + 2 * n + nout]
        sems = refs[nin + 2 * n + nout:]
        step = pl.program_id(0) * grid[1] + pl.program_id(1)

        @pl.when(step == 0)
        def _():
            ride.start(xs, got, sems)

        body(*ins, *outs)

        @pl.when(step == grid[0] * grid[1] - 1)
        def _():
            ride.finish(xs, got, sems)

    return wrapped


def _ride_call(body, grid, in_specs, out_specs, out_shape, ride, rode, name, args):
    if ride is None:
        return pl.pallas_call(body, grid=grid, in_specs=in_specs, out_specs=out_specs, out_shape=out_shape,
                              compiler_params=_cp(("parallel", "arbitrary")), name=name)(*args), []
    res = pl.pallas_call(
        _with_rider(body, len(in_specs), len(out_specs), ride, grid), grid=grid,
        in_specs=in_specs + ride.specs, out_specs=out_specs + ride.specs, out_shape=out_shape + ride.out_shape,
        scratch_shapes=ride.scratch,
        compiler_params=pltpu.CompilerParams(dimension_semantics=("arbitrary", "arbitrary"), vmem_limit_bytes=VMEM_LIMIT,
                                             has_side_effects=True), name=name)(*args, *rode)
    return res[:len(out_specs)], res[len(out_specs):]


def attn_fwd(q, k, v, name, rode=None, modes=None):
    H, T, _ = q.shape

    def body(q_ref, k_ref, v_ref, o_ref, lse_ref):
        def run(nk):
            for hh in range(HBF):
                s = _dotf(q_ref[hh], k_ref[hh, pl.ds(0, nk), :], "nt")
                m = jnp.max(s, axis=1, keepdims=True)
                p = jnp.exp2(s - m)
                l = jnp.sum(p, axis=1, keepdims=True)
                o = jnp.dot(p.astype(BF16), v_ref[hh, pl.ds(0, nk), :], preferred_element_type=F32)
                o_ref[:, pl.ds(hh * VD, VD)] = o / l
                lse_ref[hh] = m + jnp.log2(l)

        _by_query_block(run, T)

    return _ride_call(
        body, (H // HBF, T // TB),
        [pl.BlockSpec((HBF, TB, QK), lambda h, i: (h, i, 0)), pl.BlockSpec((HBF, T, QK), lambda h, i: (h, 0, 0)),
         pl.BlockSpec((HBF, T, VD), lambda h, i: (h, 0, 0))],
        [pl.BlockSpec((TB, HBF * VD), lambda h, i: (i, h)), pl.BlockSpec((HBF, TB, 1), lambda h, i: (h, i, 0))],
        [jax.ShapeDtypeStruct((T, H * VD), F32), jax.ShapeDtypeStruct((H, T, 1), F32)],
        Exchange(rode, modes) if rode else None, rode, name, (q, k, v))


def _split3(x):
    a = x.astype(BF16)
    r = x - a.astype(F32)
    b = r.astype(BF16)
    return [a, b, (r - b.astype(F32)).astype(BF16)]


def _aug(m, cols, width=128):
    used = m.shape[1] + sum(c.shape[1] for c in cols)
    return jnp.concatenate([m] + list(cols) + [jnp.zeros((m.shape[0], width - used), m.dtype)], axis=1)


def attn_bwd(q, k, v, o, lse, do, name, rode=None, modes=None):
    H, T, _ = q.shape

    def body(q_ref, k_ref, v_ref, o_ref, lse_ref, do_ref, dq_ref, dk_ref, dv_ref):
        i = pl.program_id(1)

        @pl.when(i == 0)
        def _():
            dk_ref[...] = jnp.zeros_like(dk_ref)
            dv_ref[...] = jnp.zeros_like(dv_ref)

        def run(nk):
            keys = pl.ds(0, nk)
            for hh in range(HB):
                qv, kv, dov = q_ref[hh], k_ref[hh, keys, :], do_ref[:, pl.ds(hh * VD, VD)]
                ones = jnp.ones((nk, 3), BF16)
                p = jnp.exp2(_dotf(_aug(qv, _split3(-lse_ref[hh])), _aug(kv, [ones]), "nt"))
                delta = jnp.sum(dov * o_ref[:, pl.ds(hh * VD, VD)], axis=1, keepdims=True)
                dob = dov.astype(BF16)
                dv_ref[hh, keys, :] += _dotf(p.astype(BF16), dob, "tn")
                dpd = _dotf(_aug(dob, _split3(-delta)), _aug(v_ref[hh, keys, :], [ones]), "nt")
                ds = (p * dpd).astype(BF16)
                dq_ref[hh] = jnp.dot(ds, kv, preferred_element_type=F32)
                dk_ref[hh, keys, :] += _dotf(ds, qv, "tn")

        _by_query_block(run, T)

    blk = lambda c: pl.BlockSpec((HB, TB, c), lambda h, i: (h, i, 0))
    full = lambda c: pl.BlockSpec((HB, T, c), lambda h, i: (h, 0, 0))
    tok = pl.BlockSpec((TB, HB * VD), lambda h, i: (i, h))
    return _ride_call(
        body, (H // HB, T // TB), [blk(QK), full(QK), full(VD), tok, blk(1), tok], [blk(QK), full(QK), full(VD)],
        [jax.ShapeDtypeStruct((H, T, QK), F32), jax.ShapeDtypeStruct((H, T, QK), F32), jax.ShapeDtypeStruct((H, T, VD), F32)],
        Exchange(rode, modes) if rode else None, rode, name, (q, k, v, o, lse, do))


def disc_fwd(a_re, a_im, ls, name):
    def body(ar_ref, ai_ref, ls_ref, lr_ref, li_ref, fr_ref, fi_ref):
        ar, ai = ar_ref[...], ai_ref[...]
        dt = jnp.exp(ls_ref[...])
        mag = jnp.exp(ar * dt)
        lr = mag * jnp.cos(ai * dt)
        li = mag * jnp.sin(ai * dt)
        den = ar * ar + ai * ai
        nr = lr - 1.0
        lr_ref[...] = lr
        li_ref[...] = li
        fr_ref[...] = (nr * ar + li * ai) / den
        fi_ref[...] = (li * ar - nr * ai) / den

    return pl.pallas_call(body, out_shape=[jax.ShapeDtypeStruct(a_re.shape, F32)] * 4, name=name)(a_re, a_im, ls)


def disc_b(f_re, f_im, b_re, b_im, name):
    def body(fr_ref, fi_ref, br_ref, bi_ref, or_ref, oi_ref):
        fr, fi, br, bi = fr_ref[...], fi_ref[...], br_ref[...], bi_ref[...]
        or_ref[...] = fr * br - fi * bi
        oi_ref[...] = fr * bi + fi * br

    return pl.pallas_call(body, out_shape=[jax.ShapeDtypeStruct(b_re.shape, F32)] * 2, compiler_params=_cp(),
                          name=name)(f_re, f_im, b_re, b_im)


def disc_b_bwd(f_re, f_im, b_re, b_im, dbb_re, dbb_im, name):
    def body(fr_ref, fi_ref, br_ref, bi_ref, dr_ref, di_ref, dbr_ref, dbi_ref, dfr_ref, dfi_ref):
        fr, fi, br, bi, dr, di = fr_ref[...], fi_ref[...], br_ref[...], bi_ref[...], dr_ref[...], di_ref[...]
        dbr_ref[...] = fr * dr + fi * di
        dbi_ref[...] = fr * di - fi * dr
        dfr_ref[...] = jnp.sum(dr * br + di * bi, axis=2, keepdims=True)
        dfi_ref[...] = jnp.sum(di * br - dr * bi, axis=2, keepdims=True)

    return pl.pallas_call(body, out_shape=[jax.ShapeDtypeStruct(b_re.shape, F32)] * 2 + [jax.ShapeDtypeStruct(f_re.shape, F32)] * 2,
                          compiler_params=_cp(), name=name)(f_re, f_im, b_re, b_im, dbb_re, dbb_im)


def disc_a_bwd(a_re, a_im, ls, dlr, dli, dfr, dfi, name):
    def body(ar_ref, ai_ref, ls_ref, dlr_ref, dli_ref, dfr_ref, dfi_ref, dar_ref, dai_ref, dls_ref):
        ar, ai = ar_ref[...], ai_ref[...]
        dt = jnp.exp(ls_ref[...])
        mag = jnp.exp(ar * dt)
        cs, sn = jnp.cos(ai * dt), jnp.sin(ai * dt)
        lr, li = mag * cs, mag * sn
        den = ar * ar + ai * ai
        nr = lr - 1.0
        f_re = (nr * ar + li * ai) / den
        f_im = (li * ar - nr * ai) / den
        dn1 = dfr_ref[...] / den
        dn2 = dfi_ref[...] / den
        dden = -(dfr_ref[...] * f_re + dfi_ref[...] * f_im) / den
        dlr_t = dlr_ref[...] + dn1 * ar - dn2 * ai
        dli_t = dli_ref[...] + dn1 * ai + dn2 * ar
        dar = dn1 * nr + dn2 * li + dden * 2.0 * ar
        dai = dn1 * li - dn2 * nr + dden * 2.0 * ai
        dmag = dlr_t * cs + dli_t * sn
        dth = dli_t * lr - dlr_t * li
        dar_ref[...] = dar + dmag * mag * dt
        dai_ref[...] = dai + dth * dt
        dls_ref[...] = jnp.sum(dmag * mag * ar + dth * ai, axis=-1, keepdims=True) * dt

    return pl.pallas_call(body, out_shape=[jax.ShapeDtypeStruct(a_re.shape, F32)] * 2 +
                          [jax.ShapeDtypeStruct(ls.shape, F32)], name=name)(a_re, a_im, ls, dlr, dli, dfr, dfi)


def _cpow(lr, li, n):
    rr, ri = None, None
    br, bi = lr, li
    while n:
        if n & 1:
            if rr is None:
                rr, ri = br, bi
            else:
                rr, ri = rr * br - ri * bi, rr * bi + ri * br
        n >>= 1
        if n:
            br, bi = br * br - bi * bi, 2.0 * br * bi
    return rr, ri


UNROLL = 4


def _steps(trips, fn, init):
    main = trips // UNROLL

    def body(i, c):
        for j in range(UNROLL):
            c = fn(i * UNROLL + j, c)
        return c

    c = lax.fori_loop(0, main, body, init) if main else init
    for n in range(main * UNROLL, trips):
        c = fn(n, c)
    return c


def _seg_scan(xre, xim, lam8, pw, base, seglen, rev, init, fin_re, fin_im, ini_re, ini_im, prev=None):
    lr, li = lam8
    nsub = SEG // 8

    def rows(t, s):
        first = base + t * SEG + 8 * s
        return pl.ds(first if isinstance(first, int) else pl.multiple_of(first, 8), 8)

    tmap = (lambda n: seglen - 1 - n) if rev else (lambda n: n)
    zeros = tuple(jnp.zeros((8, SB), F32) for _ in range(2 * nsub))

    def advance(c, t):
        out = []
        for s in range(nsub):
            a, b = c[2 * s], c[2 * s + 1]
            out += [lr * a - li * b + xre[rows(t, s), :], lr * b + li * a + xim[rows(t, s), :]]
        return tuple(out)

    fin = _steps(seglen, lambda n, c: advance(c, tmap(n)), zeros)
    for s in range(nsub):
        fin_re[pl.ds(8 * s, 8), :] = fin[2 * s]
        fin_im[pl.ds(8 * s, 8), :] = fin[2 * s + 1]
    (cr, ci), (pr, pi) = init, pw
    for i in (range(SEG - 1, -1, -1) if rev else range(SEG)):
        ini_re[pl.ds(i, 1), :] = cr
        ini_im[pl.ds(i, 1), :] = ci
        cr, ci = pr * cr - pi * ci + fin_re[pl.ds(i, 1), :], pr * ci + pi * cr + fin_im[pl.ds(i, 1), :]
    tiles = lambda re, im: tuple(r[pl.ds(8 * s, 8), :] for s in range(nsub) for r in (re, im))
    start = tiles(ini_re, ini_im)

    def store(c, t):
        new = advance(c, t)
        for s in range(nsub):
            xre[rows(t, s), :] = new[2 * s]
            xim[rows(t, s), :] = new[2 * s + 1]
        return new

    if prev is None:
        _steps(seglen, lambda n, c: store(c, tmap(n)), start)
        return (cr, ci), None

    sre, sim, s_ini_re, s_ini_im = prev

    def acc_step(c, t, before):
        new = store(c[:2 * nsub], t)
        acc = []
        for s in range(nsub):
            (na, nb), (pre, pim) = new[2 * s:2 * s + 2], before[2 * s:2 * s + 2]
            acc += [c[2 * nsub + 2 * s] + na * pre + nb * pim, c[2 * nsub + 2 * s + 1] + nb * pre - na * pim]
        return new + tuple(acc)

    def body(n, c):
        t = tmap(n)
        tp = t - 1 if rev else t + 1
        return acc_step(c, t, tuple(r[rows(tp, s), :] for s in range(nsub) for r in (sre, sim)))

    c = _steps(seglen - 1, body, start + zeros)
    c = acc_step(c, 0 if rev else seglen - 1, tiles(s_ini_re, s_ini_im))
    acc = c[2 * nsub:]
    return (cr, ci), (sum(acc[0::2][1:], acc[0]), sum(acc[1::2][1:], acc[1]))


def _lam_tiles(lr, li, lens, conj=False):
    if conj:
        li = -li
    lam8 = (jnp.broadcast_to(lr, (8, SB)), jnp.broadcast_to(li, (8, SB)))
    return lam8, [_cpow(lr, li, n) for n in lens]


def _stretches(T):
    return ((0, LC // SEG), (LC, (T - LC) // SEG))


def _to_seg_order(src, dst, T):
    for base, seglen in _stretches(T):
        def body(t, carry, base=base, seglen=seglen):
            dst[pl.ds(pl.multiple_of(base + t * SEG, SEG), SEG), :] = src[pl.ds(base + t, SEG, stride=seglen), :]
            return carry
        lax.fori_loop(0, seglen, body, 0, unroll=8)


def _from_seg_order(src, dst, T):
    for base, seglen in _stretches(T):
        def body(t, carry, base=base, seglen=seglen):
            dst[pl.ds(base + t, SEG, stride=seglen), :] = src[pl.ds(pl.multiple_of(base + t * SEG, SEG), SEG), :]
            return carry
        lax.fori_loop(0, seglen, body, 0, unroll=8)


def _scan_specs(T):
    ublk = pl.BlockSpec((T, UB), lambda j: (0, j))
    lam = pl.BlockSpec((2, 1, 1, SB), lambda j: (0, j, 0, 0))
    mat = pl.BlockSpec((2, 1, UB, P), lambda j: (0, j, 0, 0))
    return ublk, lam, mat


def _dotf(a, b, mode="nn"):
    return lax.dot_general(a, b, _DN[mode], preferred_element_type=F32)


def _diag_mask():
    r = lax.broadcasted_iota(jnp.int32, (UB, SB), 0)
    c = lax.broadcasted_iota(jnp.int32, (UB, SB), 1)
    return lax.shift_right_logical(r, int(math.log2(CH))) == lax.shift_right_logical(c, int(math.log2(P)))


def _expand(m):
    p = lax.broadcasted_iota(jnp.int32, (P, SB), 0)
    c = lax.broadcasted_iota(jnp.int32, (P, SB), 1)
    tile = jnp.where(lax.bitwise_and(c, P - 1) == p, 1.0, 0.0).astype(BF16)
    wide = jnp.dot(m.astype(BF16), tile, preferred_element_type=F32)
    return jnp.where(_diag_mask(), wide, 0.0).astype(BF16)


def _collapse(full):
    c = lax.broadcasted_iota(jnp.int32, (SB, P), 0)
    p = lax.broadcasted_iota(jnp.int32, (SB, P), 1)
    pick = jnp.where(lax.bitwise_and(c, P - 1) == p, 1.0, 0.0).astype(BF16)
    return _exact_perm(jnp.where(_diag_mask(), full, 0.0), pick)


def _zero_state():
    return jnp.zeros((1, SB), F32), jnp.zeros((1, SB), F32)


def scan_fwd(u, lam_re, lam_im, bre, bim, cre, cim, name):
    T = u.shape[0]
    s_ctx, s_lat = LC // SEG, (T - LC) // SEG

    def body(u_ref, lr_ref, li_ref, bre_ref, bim_ref, cre_ref, cim_ref, y_ref, us, ys, sre, sim, fre, fim, ire, iim):
        _to_seg_order(u_ref, us, T)
        ub = us[...].astype(BF16)
        for d in range(2):
            lam8, (pw_c, pw_l) = _lam_tiles(lr_ref[d, 0], li_ref[d, 0], (s_ctx, s_lat))
            sre[...] = _dotf(ub, _expand(bre_ref[d, 0]))
            sim[...] = _dotf(ub, _expand(bim_ref[d, 0]))
            end_c, _ = _seg_scan(sre, sim, lam8, pw_c, 0, s_ctx, bool(d), _zero_state(), fre, fim, ire, iim)
            _seg_scan(sre, sim, lam8, pw_l, LC, s_lat, bool(d), end_c, fre, fim, ire, iim)
            y = (_dotf(sre[...].astype(BF16), _expand(cre_ref[d, 0]), "nt")
                 - _dotf(sim[...].astype(BF16), _expand(cim_ref[d, 0]), "nt"))
            if d == 0:
                ys[...] = y
            else:
                ys[...] += y
        _from_seg_order(ys, y_ref, T)

    ublk, lam, mat = _scan_specs(T)
    return pl.pallas_call(
        body, grid=(NJ,), in_specs=[ublk, lam, lam, mat, mat, mat, mat], out_specs=ublk,
        out_shape=jax.ShapeDtypeStruct((T, G * CH), F32),
        scratch_shapes=[pltpu.VMEM((T, UB), F32)] * 2 + [pltpu.VMEM((T, SB), F32)] * 2 + [pltpu.VMEM((SEG, SB), F32)] * 4,
        compiler_params=_cp(("arbitrary",)), name=name)(u, lam_re, lam_im, bre, bim, cre, cim)


def scan_bwd(u, dy, lam_re, lam_im, bre, bim, cre, cim, name):
    T = u.shape[0]
    s_ctx, s_lat = LC // SEG, (T - LC) // SEG

    def body(u_ref, dy_ref, lr_ref, li_ref, bre_ref, bim_ref, cre_ref, cim_ref,
             du_ref, dlr_ref, dli_ref, dbre_ref, dbim_ref, dcre_ref, dcim_ref,
             us, dys, dus, sre, sim, gre, gim, fre, fim, ic_re, ic_im, il_re, il_im, jre, jim):
        _to_seg_order(u_ref, us, T)
        _to_seg_order(dy_ref, dys, T)
        ub, dyb = us[...].astype(BF16), dys[...].astype(BF16)
        for d in range(2):
            rev = bool(d)
            lam8, (pw_c, pw_l) = _lam_tiles(lr_ref[d, 0], li_ref[d, 0], (s_ctx, s_lat))
            cam8, (cw_c, cw_l) = _lam_tiles(lr_ref[d, 0], li_ref[d, 0], (s_ctx, s_lat), conj=True)
            bre_v, bim_v = _expand(bre_ref[d, 0]), _expand(bim_ref[d, 0])
            sre[...] = _dotf(ub, bre_v)
            sim[...] = _dotf(ub, bim_v)
            end_c, _ = _seg_scan(sre, sim, lam8, pw_c, 0, s_ctx, rev, _zero_state(), fre, fim, ic_re, ic_im)
            _seg_scan(sre, sim, lam8, pw_l, LC, s_lat, rev, end_c, fre, fim, il_re, il_im)
            gre[...] = _dotf(dyb, _expand(cre_ref[d, 0]))
            gim[...] = -_dotf(dyb, _expand(cim_ref[d, 0]))
            end_g, acc_l = _seg_scan(gre, gim, cam8, cw_l, LC, s_lat, not rev, _zero_state(), fre, fim, jre, jim,
                                     prev=(sre, sim, il_re, il_im))
            _, acc_c = _seg_scan(gre, gim, cam8, cw_c, 0, s_ctx, not rev, end_g, fre, fim, jre, jim,
                                 prev=(sre, sim, ic_re, ic_im))
            dlr_ref[d, 0] = _sum0(acc_l[0] + acc_c[0])
            dli_ref[d, 0] = _sum0(acc_l[1] + acc_c[1])
            grb, gib = gre[...].astype(BF16), gim[...].astype(BF16)
            du = _dotf(grb, bre_v, "nt") + _dotf(gib, bim_v, "nt")
            if d == 0:
                dus[...] = du
            else:
                dus[...] += du
            dbre_ref[d, 0] = _collapse(_dotf(ub, grb, "tn"))
            dbim_ref[d, 0] = _collapse(_dotf(ub, gib, "tn"))
            dcre_ref[d, 0] = _collapse(_dotf(dyb, sre[...].astype(BF16), "tn"))
            dcim_ref[d, 0] = -_collapse(_dotf(dyb, sim[...].astype(BF16), "tn"))
        _from_seg_order(dus, du_ref, T)

    ublk, lam, mat = _scan_specs(T)
    lam_s = jax.ShapeDtypeStruct(lam_re.shape, F32)
    mat_s = jax.ShapeDtypeStruct(bre.shape, F32)
    return pl.pallas_call(
        body, grid=(NJ,), in_specs=[ublk, ublk, lam, lam, mat, mat, mat, mat],
        out_specs=[ublk, lam, lam, mat, mat, mat, mat],
        out_shape=[jax.ShapeDtypeStruct((T, G * CH), F32), lam_s, lam_s, mat_s, mat_s, mat_s, mat_s],
        scratch_shapes=[pltpu.VMEM((T, UB), F32)] * 3 + [pltpu.VMEM((T, SB), F32)] * 4 + [pltpu.VMEM((SEG, SB), F32)] * 8,
        compiler_params=_cp(("arbitrary",)), name=name)(u, dy, lam_re, lam_im, bre, bim, cre, cim)


class Exchange:
    def __init__(self, xs, modes):
        self.n = len(xs)
        self.modes = [modes] * self.n if isinstance(modes, (str, int)) else list(modes)
        self.out_shape = [jax.ShapeDtypeStruct(self._shape(x, md), x.dtype) for x, md in zip(xs, self.modes)]
        self.scratch = [pltpu.SemaphoreType.DMA((NDEV - 1, self.n)), pltpu.SemaphoreType.DMA((NDEV - 1, self.n)),
                        pltpu.SemaphoreType.DMA((self.n,))]
        self.specs = [pl.BlockSpec(memory_space=pl.ANY)] * self.n

    @staticmethod
    def _shape(x, mode):
        if mode == "gather":
            return (NDEV,) + tuple(x.shape)
        return tuple(x.shape) if mode == "lead" else (NDEV, x.shape[0], mode) + tuple(x.shape[2:])

    @staticmethod
    def _piece(x_ref, mode, dev):
        if mode == "gather":
            return x_ref
        return x_ref.at[dev] if mode == "lead" else x_ref.at[:, pl.ds(dev * mode, mode)]

    def _copies(self, x_refs, out_refs, sems):
        send_sems, recv_sems, local_sems = sems
        mx, my, mc = lax.axis_index("x"), lax.axis_index("y"), lax.axis_index("c")
        me = 4 * mx + 2 * my + mc
        peer_of = lambda k: (1 - mx if k & 4 else mx, 1 - my if k & 2 else my, 1 - mc if k & 1 else mc)
        local, first, relay, arrivals = [], [], [], []
        for a, (x_ref, out_ref) in enumerate(zip(x_refs, out_refs)):
            mode = self.modes[a]
            local.append(pltpu.make_async_copy(self._piece(x_ref, mode, me), out_ref.at[me], local_sems.at[a]))

            def remote(src, dst, k, pair, a=a):
                return pltpu.make_async_remote_copy(src_ref=src, dst_ref=dst, send_sem=send_sems.at[pair, a],
                                                    recv_sem=recv_sems.at[pair, a], device_id=peer_of(k), device_id_type=MESH_T)

            for k in range(1, NDEV):
                peer = peer_of(k)
                pid = 4 * peer[0] + 2 * peer[1] + peer[2]
                if mode != "gather":
                    src = self._piece(x_ref, mode, pid)
                    first.append(remote(src, out_ref.at[me], k, k - 1))
                    arrivals.append(remote(src, out_ref.at[pid], k, k - 1))
                elif k == 1:
                    first.append(remote(x_ref, out_ref.at[me], k, k - 1))
                    arrivals.append(remote(x_ref, out_ref.at[pid], k, k - 1))
                elif k % 2 == 0:
                    first.append(remote(x_ref, out_ref.at[me], k, k - 1))
                    relay.append((remote(x_ref, out_ref.at[pid], k, k - 1), remote(out_ref.at[pid], out_ref.at[pid], 1, k)))
                else:
                    arrivals.append(remote(x_ref, out_ref.at[pid], 1, k - 1))
        return local, first, relay, arrivals

    def start(self, x_refs, out_refs, sems):
        local, first, _, _ = self._copies(x_refs, out_refs, sems)
        for cp in local + first:
            cp.start()

    def finish(self, x_refs, out_refs, sems):
        local, first, relay, arrivals = self._copies(x_refs, out_refs, sems)
        for arrival, onward in relay:
            arrival.wait_recv()
            onward.start()
        for cp in arrivals:
            cp.wait_recv()
        for cp in first + [onward for _, onward in relay]:
            cp.wait_send()
        for cp in local:
            cp.wait()


def exchange(xs, modes, name):
    ex = Exchange(xs, modes)
    n = ex.n

    def body(*refs):
        ex.start(refs[:n], refs[n:2 * n], refs[2 * n:])
        ex.finish(refs[:n], refs[n:2 * n], refs[2 * n:])

    return pl.pallas_call(body, in_specs=ex.specs, out_specs=ex.specs, out_shape=ex.out_shape, scratch_shapes=ex.scratch,
                          compiler_params=pltpu.CompilerParams(has_side_effects=True), name=name)(*xs)


def _dot_f32(a, b, dn):
    return lax.dot_general(a, b, dn, preferred_element_type=F32, precision=lax.Precision.HIGHEST)


def ada_fwd(cg, c_ctx, ada_w, ada_b_loc, name):
    W = ada_w.shape[2]

    def body(cg_ref, cc_ref, w_ref, b_ref, o_ref):
        a = jnp.concatenate([_silu(cg_ref[...]), jnp.broadcast_to(_silu(cc_ref[...]), (NDEV, D))], axis=0)
        for i in range(2):
            o_ref[i] = _dot_f32(a, w_ref[i], _DN["nn"]) + b_ref[i]

    return pl.pallas_call(body, out_shape=jax.ShapeDtypeStruct((2, 2 * NDEV, W), F32),
                          compiler_params=_cp(), name=name)(cg, c_ctx, ada_w, ada_b_loc)


def ada_bwd(cg, c_ctx, ada_w, dm_loc, dm_all, name):
    W = ada_w.shape[2]

    def body(cg_ref, cc_ref, w_ref, dl_ref, da_ref, gw_ref, dcc_ref, gb_ref):
        a = jnp.concatenate([_silu(cg_ref[...]), jnp.broadcast_to(_silu(cc_ref[...]), (NDEV, D))], axis=0)
        dcc = jnp.zeros((1, D), F32)
        for i in range(2):
            dl = dl_ref[i]
            gw_ref[i] = _dot_f32(a, dl, _DN["tn"])
            dctx = jnp.sum(dl[NDEV:], axis=0, keepdims=True)
            dcc = dcc + _dot_f32(dctx, w_ref[i], _DN["nt"])
        dcc_ref[...] = dcc
        gb_ref[...] = jnp.sum(da_ref[...], axis=0)

    return pl.pallas_call(body, out_shape=[jax.ShapeDtypeStruct((2, D, W), F32), jax.ShapeDtypeStruct((1, D), F32),
                                           jax.ShapeDtypeStruct((2, 3 * D), F32)],
                          compiler_params=_cp(), name=name)(cg, c_ctx, ada_w, dm_loc, dm_all)


def cctx_finish(parts, c_ctx, name):
    def body(p_ref, cc_ref, o_ref):
        o_ref[...] = jnp.sum(p_ref[...], axis=0, keepdims=True) * _dsilu(cc_ref[...])

    return pl.pallas_call(body, out_shape=jax.ShapeDtypeStruct((1, D), F32), name=name)(parts, c_ctx)


def _adamw_update(g_ref, w_ref, m_ref, v_ref, go_ref, d_ref, mo_ref, vo_ref):
    g = g_ref[0].astype(F32)
    for s in range(1, g_ref.shape[0]):
        g = g + g_ref[s].astype(F32)
    mn = B1 * m_ref[...] + (1.0 - B1) * g
    vn = B2 * v_ref[...] + (1.0 - B2) * g * g
    go_ref[...] = g
    mo_ref[...] = mn
    vo_ref[...] = vn
    d_ref[...] = -LR * ((mn * (1.0 / (1.0 - B1 ** STEP))) / (jnp.sqrt(vn * (1.0 / (1.0 - B2 ** STEP))) + AEPS) + WD * w_ref[...])


ADAMW_PARTS = 4


def adamw_rows(items, name, rode=None, modes=None):
    in_specs, out_specs, out_shape, args = [], [], [], []
    for g, w, m, v in items:
        n, R, C = g.shape
        tr = R // ADAMW_PARTS
        spec = pl.BlockSpec((tr, C), lambda i, j: (i, 0))
        in_specs += [pl.BlockSpec((n, tr, C), lambda i, j: (0, i, 0)), spec, spec, spec]
        args += [g, w, m, v]
    for g, w, m, v in items:
        tr = w.shape[0] // ADAMW_PARTS
        out_specs += [pl.BlockSpec((tr, w.shape[1]), lambda i, j: (i, 0))] * 4
        out_shape += [jax.ShapeDtypeStruct(w.shape, F32)] * 4
    res, got = _ride_call(_adamw_body(len(items)), (ADAMW_PARTS, 1), in_specs, out_specs, out_shape,
                          Exchange(rode, modes) if rode else None, rode, name, args)
    return [res[4 * t:4 * t + 4] for t in range(len(items))], got


def _adamw_body(k):
    def body(*refs):
        for t in range(k):
            _adamw_update(*refs[4 * t:4 * t + 4], *refs[4 * k + 4 * t:4 * k + 4 * t + 4])
    return body


def adamw_multi(items, grid, name):
    k = len(items)
    ins, in_specs, out_specs, out_shape = [], [], [], []
    for g, g_spec, w, m, v, w_spec in items:
        ins += [g, w, m, v]
        in_specs += [g_spec, w_spec, w_spec, w_spec]
    for g, g_spec, w, m, v, w_spec in items:
        out_specs += [w_spec] * 4
        out_shape += [jax.ShapeDtypeStruct(w.shape, F32)] * 4
    res = pl.pallas_call(_adamw_body(k), grid=grid, in_specs=in_specs, out_specs=out_specs, out_shape=out_shape,
                         compiler_params=_cp(("arbitrary",) * len(grid)), name=name)(*ins)
    return [res[4 * t:4 * t + 4] for t in range(k)]


def _whole(a, grid_rank):
    zeros = (0,) * a.ndim
    return pl.BlockSpec(a.shape, lambda *idx: zeros)


def sum_slots(xs, name):
    def body(*refs):
        for x_ref, o_ref in zip(refs[:len(xs)], refs[len(xs):]):
            acc = x_ref[0]
            for s in range(1, NDEV):
                acc = acc + x_ref[s]
            o_ref[...] = acc

    return pl.pallas_call(body, out_shape=[jax.ShapeDtypeStruct(x.shape[1:], F32) for x in xs],
                          compiler_params=_cp(), name=name)(*xs)


def _col_shards(g):
    R, N = g.shape
    return g.reshape(R, NDEV, N // NDEV).transpose(1, 0, 2)


def _vec2(v):
    return jnp.broadcast_to(v.reshape(1, 1, -1), (2, 1, v.size))


SHARD_ROWS = {"mla_w_in": 192, "mla_w_uq": 192, "mla_w_ukv": 256, "s5_w_in": 256}


def _t_shard(wsh, rows):
    t = wsh[0].T.astype(BF16)
    return jnp.pad(t, ((0, rows - t.shape[0]), (0, 0)))


def _win_order():
    w = IN_W // NDEV
    perm = np.zeros((IN_WP, NDEV * SHARD_ROWS["mla_w_in"]), np.float32)
    first = QL + KVL + ROPE
    for c in range(IN_W):
        n = c + HEADS * VD if c < first else c - first
        perm[n, (c // w) * SHARD_ROWS["mla_w_in"] + c % w] = 1.0
    return jnp.asarray(perm, BF16)


def local_step(ctx, x, tgt, mod, Wt, small, l1_shards):
    T = LC + x.shape[0]
    xa = ("cat", ctx, x)
    sh = [mod[i, :, None, 0:D] for i in range(2)]
    sc = [mod[i, :, None, D:2 * D] for i in range(2)]
    gt = [mod[i, :, None, 2 * D:] for i in range(2)]
    ng = [_vec2(small["norm_g"][i]) for i in range(2)]
    qg, kvg = _vec2(small["mla_q_norm"]), _vec2(small["mla_kv_norm"])
    cosf, sinf, _, pmt = _rope_tables(T)

    (h0, p0, cqn, ckvn), _ = rowwise(st_l0_pre, [xa], [ng[0], sc[0], sh[0], qg, kvg],
                                     [(D, BF16), (IN_WP, F32), (QL, BF16), (KVL, BF16)], [], "l0_pre", mats=[Wt["mla_w_in"]])
    z0, cq, ckv = (p0, 0, HEADS * VD), (p0, HEADS * VD // QL, QL), (p0, (HEADS * VD + QL) // KVL, KVL)
    Q = project_q(cqn, Wt["mla_w_uq"], "l0_uq")
    K, V = project_kv(ckvn, Wt["mla_w_ukv"], p0, (HEADS * VD + QL + KVL) // 128, "l0_ukv")
    (o2, lse), got = attn_fwd(Q, K, V, "l0_attn", rode=l1_shards, modes="gather")
    Wt, small = dict(Wt), dict(small)
    for n, a in zip(L1_BIG, got):
        Wt[n] = a.reshape(-1, a.shape[-1])
    vecs = lax.bitcast_convert_type(got[-1].reshape(NDEV, 2, -1, 2), F32)
    small["s5_d"], small["s5_b_glu"] = vecs[:, 0, :].reshape(D), vecs[:, 1, :].reshape(D)
    (og, out0, x1), _ = rowwise(st_l0_post, [o2, z0, xa], [gt[0]], [(D, BF16), (D, BF16), (D, F32)], [], "l0_post",
                                mats=[Wt["mla_w_out"]])

    ls = small["s5_log_step"].reshape(2, G, 1)
    a_re, a_im = small["s5_a_re"].reshape(2, G, P), small["s5_a_im"].reshape(2, G, P)
    b_re = small["s5_b_re"].reshape(2, G, P, CH).transpose(0, 1, 3, 2)
    b_im = small["s5_b_im"].reshape(2, G, P, CH).transpose(0, 1, 3, 2)
    lam_re, lam_im, f_re, f_im = disc_fwd(a_re, a_im, ls, "s5_disc")
    f_re2, f_im2 = f_re.reshape(2, G, 1, P), f_im.reshape(2, G, 1, P)
    bb_re, bb_im = disc_b(f_re2, f_im2, b_re, b_im, "s5_disc_b")
    compact = lambda m: m.reshape(2, NJ, UB, P)
    bre, bim = compact(bb_re), compact(bb_im)
    cre, cim = compact(small["s5_c_re"]), compact(small["s5_c_im"])
    lam_re4, lam_im4 = lam_re.reshape(2, NJ, 1, SB), lam_im.reshape(2, NJ, 1, SB)

    (h1, p1), _ = rowwise(st_l1_pre, [x1], [ng[1], sc[1], sh[1]], [(D, BF16), (2 * D, F32)], [], "l1_pre", mats=[Wt["s5_w_in"]])
    u, z1 = (p1, 0, D), (p1, 1, D)
    yssm = scan_fwd(p1, lam_re4, lam_im4, bre, bim, cre, cim, "s5_scan")
    dvec, bglu = _vec2(small["s5_d"]), _vec2(small["s5_b_glu"])
    fg = _vec2(small["final_g"])
    lat_mask = jnp.stack([jnp.zeros((1, D), F32), jnp.ones((1, D), F32)])
    (y, y1b, gl, y3, out1, dx2), (dfg, lvec) = rowwise(
        st_l1_mlp, [yssm, u, z1, x1, ("lat", tgt)], [dvec, bglu, gt[1], fg, lat_mask],
        [(D, F32), (D, BF16), (D, BF16), (D, BF16), (D, BF16), (D, F32)], [D, 128], "l1_mlp",
        mats=[Wt["s5_w_glu"], Wt["s5_w_out"]])

    (dz1, dy, du_d), (dgt1, dbglu, dd), (g_w_out5, g_w_glu) = rowwise(
        st_l1_mlp_bwd, [dx2, out1, y3, y, gl, z1, u, y1b], [gt[1], bglu, dvec], [(D, BF16), (D, F32), (D, F32)], [D, D, D],
        "l1_mlp_b", mats=[Wt["s5_w_out"], Wt["s5_w_glu"]], out_accs=[(D, D), (D, D)])
    du_s, dlr, dli, dbre, dbim, dcre, dcim = scan_bwd(p1, dy, lam_re4, lam_im4, bre, bim, cre, cim, "s5_scan_b")
    dbb_re, dbb_im = dbre.reshape(2, G, CH, P), dbim.reshape(2, G, CH, P)
    g_c_re, g_c_im = dcre.reshape(2, G, CH, P), dcim.reshape(2, G, CH, P)
    gt_b_re, gt_b_im, dfr, dfi = disc_b_bwd(f_re2, f_im2, b_re, b_im, dbb_re, dbb_im, "s5_disc_b_b")
    g_b_re, g_b_im = gt_b_re.transpose(0, 1, 3, 2), gt_b_im.transpose(0, 1, 3, 2)
    g_a_re, g_a_im, g_ls = disc_a_bwd(a_re, a_im, ls, dlr.reshape(2, G, P), dli.reshape(2, G, P),
                                      dfr.reshape(2, G, P), dfi.reshape(2, G, P), "s5_disc_b_a")
    (dx1,), (dsh1, dsc1, dng1), (g_w_in5,) = rowwise(
        st_l1_tail_bwd, [du_d, du_s, dz1, h1, x1, dx2], [ng[1], sc[1]], [(D, F32)], [D, D, D], "l1_pre_b",
        mats=[Wt["s5_w_in"]], out_accs=[(NDEV, D, 2 * D // NDEV)])

    (do2, dz0), (dgt0,), (g_w_out,) = rowwise(st_l0_post_bwd, [dx1, out0, og, o2, z0], [gt[0]], [(D, F32), (D, F32)], [D],
                                              "l0_post_b", mats=[Wt["mla_w_out"]], out_accs=[(D, D)])
    rows8 = lambda g: g.reshape(NDEV, -1, g.shape[-1])
    both = lambda s: s[0, 0] + s[1, 0]
    dense = lambda g: g.reshape(2, G * P * CH // 128, 128)
    chunks = [dense(g_b_re), dense(g_b_im), g_c_re, g_c_im]
    l1_send = [g_w_in5, rows8(g_w_glu), rows8(g_w_out5), rows8(g_w_out),
               both(dd).reshape(NDEV, 1, -1), both(dbglu).reshape(NDEV, 1, -1)]
    (dQ, dK, dV), l1_recv = attn_bwd(Q, K, V, o2, lse, do2, "l0_attn_b", rode=l1_send + chunks,
                                     modes=["lead"] * len(l1_send) + [a.shape[1] // NDEV for a in chunks])
    dq = rope_bwd(dQ, cosf, sinf, pmt, SCALE, "l0_rope_q_b")
    n_owned = len(l1_send)
    reduced = sum_slots(l1_recv[n_owned:], "sum_chunks")
    (dkv, dkr), chunk_all = split_kv_grads(dK, dV, "l0_kv_b", rode=[jnp.stack(reduced[:2]), jnp.stack(reduced[2:])],
                                           modes="gather")
    (grad_x,), (dqg, dkvg, dsh0, dsc0, dng0), (g_uq, g_ukv, g_p) = rowwise(
        st_l0_tail_bwd, [dq, dkv, dkr, dz0, cq, ckv, cqn, ckvn, h0, xa, dx1], [qg, kvg, ng[0], sc[0]],
        [(D, F32, "lat")], [QL, KVL, D, D, D], "l0_pre_b", mats=[Wt["mla_w_uq"], Wt["mla_w_ukv"], Wt["mla_w_in"]],
        out_accs=[(QL, HEADS * QK), (KVL, HEADS * KVW), (D, IN_WP)])
    g_w_uq, g_w_ukv = _col_shards(g_uq).astype(BF16), _col_shards(g_ukv).astype(BF16)
    g_w_in = _col_shards(jnp.concatenate([g_p[:, HEADS * VD:IN_W], g_p[:, :HEADS * VD]], axis=1)).astype(BF16)

    dmod = jnp.stack([jnp.concatenate([dsh0, dsc0, dgt0], axis=-1)[:, 0], jnp.concatenate([dsh1, dsc1, dgt1], axis=-1)[:, 0]])
    gbig = {"mla_w_in": g_w_in, "mla_w_uq": g_w_uq, "mla_w_ukv": g_w_ukv}
    gsmall = {"norm_g": jnp.stack([both(dng0), both(dng1)]), "mla_q_norm": both(dqg), "mla_kv_norm": both(dkvg),
              "s5_a_re": g_a_re, "s5_a_im": g_a_im, "s5_log_step": g_ls, "final_g": dfg[1, 0]}
    return lvec[1], grad_x, dmod, gbig, gsmall, l1_recv[:n_owned], chunk_all


COL_SHARDED = ("mla_w_in", "mla_w_uq", "mla_w_ukv", "s5_w_in")
ROW_SHARDED = ("mla_w_out", "s5_w_glu", "s5_w_out")
VEC_SHARDED = ("s5_d", "s5_b_glu")
BIG = COL_SHARDED + ROW_SHARDED
L0_BIG = ("mla_w_in", "mla_w_uq", "mla_w_ukv")
L1_BIG = ("s5_w_in", "s5_w_glu", "s5_w_out", "mla_w_out")
BITS16 = jnp.bfloat16
SMALL_RS = ("norm_g", "mla_q_norm", "mla_kv_norm", "s5_a_re", "s5_a_im", "s5_log_step", "s5_b_re", "s5_b_im",
            "s5_c_re", "s5_c_im", "final_g")
CHUNKED = ("s5_b_re", "s5_b_im", "s5_c_re", "s5_c_im")
DENSE = ("s5_b_re", "s5_b_im")
TINY = ("norm_g", "mla_q_norm", "mla_kv_norm", "s5_a_re", "s5_a_im", "s5_log_step", "final_g")
ORDER = ("c_ctx", "ada_w", "ada_b", "norm_g", "mla_w_in", "mla_q_norm", "mla_w_uq", "mla_kv_norm", "mla_w_ukv",
         "mla_w_out", "s5_w_in", "s5_a_re", "s5_a_im", "s5_log_step", "s5_b_re", "s5_b_im", "s5_c_re", "s5_c_im",
         "s5_d", "s5_w_glu", "s5_b_glu", "s5_w_out", "final_g")


def kernel(x, c, ctx, c_ctx, ada_w, ada_b, norm_g, mla_w_in, mla_q_norm, mla_w_uq, mla_kv_norm, mla_w_ukv, mla_w_out, s5_w_in, s5_a_re, s5_a_im, s5_log_step, s5_b_re, s5_b_im, s5_c_re, s5_c_im, s5_d, s5_w_glu, s5_b_glu, s5_w_out, final_g, loss_target, m_c_ctx, m_ada_w, m_ada_b, m_norm_g, m_mla_w_in, m_mla_q_norm, m_mla_w_uq, m_mla_kv_norm, m_mla_w_ukv, m_mla_w_out, m_s5_w_in, m_s5_a_re, m_s5_a_im, m_s5_log_step, m_s5_b_re, m_s5_b_im, m_s5_c_re, m_s5_c_im, m_s5_d, m_s5_w_glu, m_s5_b_glu, m_s5_w_out, m_final_g, v_c_ctx, v_ada_w, v_ada_b, v_norm_g, v_mla_w_in, v_mla_q_norm, v_mla_w_uq, v_mla_kv_norm, v_mla_w_ukv, v_mla_w_out, v_s5_w_in, v_s5_a_re, v_s5_a_im, v_s5_log_step, v_s5_b_re, v_s5_b_im, v_s5_c_re, v_s5_c_im, v_s5_d, v_s5_w_glu, v_s5_b_glu, v_s5_w_out, v_final_g):
    w = dict(c_ctx=c_ctx, ada_w=ada_w, ada_b=ada_b, norm_g=norm_g, mla_w_in=mla_w_in, mla_q_norm=mla_q_norm,
             mla_w_uq=mla_w_uq, mla_kv_norm=mla_kv_norm, mla_w_ukv=mla_w_ukv, mla_w_out=mla_w_out, s5_w_in=s5_w_in,
             s5_a_re=s5_a_re, s5_a_im=s5_a_im, s5_log_step=s5_log_step, s5_b_re=s5_b_re, s5_b_im=s5_b_im,
             s5_c_re=s5_c_re, s5_c_im=s5_c_im, s5_d=s5_d, s5_w_glu=s5_w_glu, s5_b_glu=s5_b_glu, s5_w_out=s5_w_out,
             final_g=final_g)
    m = dict(c_ctx=m_c_ctx, ada_w=m_ada_w, ada_b=m_ada_b, norm_g=m_norm_g, mla_w_in=m_mla_w_in, mla_q_norm=m_mla_q_norm,
             mla_w_uq=m_mla_w_uq, mla_kv_norm=m_mla_kv_norm, mla_w_ukv=m_mla_w_ukv, mla_w_out=m_mla_w_out,
             s5_w_in=m_s5_w_in, s5_a_re=m_s5_a_re, s5_a_im=m_s5_a_im, s5_log_step=m_s5_log_step, s5_b_re=m_s5_b_re,
             s5_b_im=m_s5_b_im, s5_c_re=m_s5_c_re, s5_c_im=m_s5_c_im, s5_d=m_s5_d, s5_w_glu=m_s5_w_glu,
             s5_b_glu=m_s5_b_glu, s5_w_out=m_s5_w_out, final_g=m_final_g)
    v = dict(c_ctx=v_c_ctx, ada_w=v_ada_w, ada_b=v_ada_b, norm_g=v_norm_g, mla_w_in=v_mla_w_in, mla_q_norm=v_mla_q_norm,
             mla_w_uq=v_mla_w_uq, mla_kv_norm=v_mla_kv_norm, mla_w_ukv=v_mla_w_ukv, mla_w_out=v_mla_w_out,
             s5_w_in=v_s5_w_in, s5_a_re=v_s5_a_re, s5_a_im=v_s5_a_im, s5_log_step=v_s5_log_step, s5_b_re=v_s5_b_re,
             s5_b_im=v_s5_b_im, s5_c_re=v_s5_c_re, s5_c_im=v_s5_c_im, s5_d=v_s5_d, s5_w_glu=v_s5_w_glu,
             s5_b_glu=v_s5_b_glu, s5_w_out=v_s5_w_out, final_g=v_final_g)

    me = 4 * lax.axis_index("x") + 2 * lax.axis_index("y") + lax.axis_index("c")
    WA = ada_w.shape[2]

    def shard(n):
        return _t_shard(w[n], SHARD_ROWS[n]) if n in COL_SHARDED else w[n][0].astype(BF16)

    wgot = exchange([c] + [shard(n) for n in L0_BIG], "gather", "gather_w")

    cg = wgot[0].reshape(NDEV, D)
    cc2 = c_ctx.reshape(1, D)
    ada_b_loc = lax.dynamic_slice_in_dim(ada_b.reshape(2, 3 * D // WA, WA), me, 1, axis=1)
    part = ada_fwd(cg, cc2, ada_w, ada_b_loc, "ada_fwd")
    Wt = {n: a.reshape(-1, a.shape[-1]) for n, a in zip(L0_BIG, wgot[1:])}
    Wt["mla_w_in"], (pg,) = mm(_win_order(), Wt["mla_w_in"], "nn", "w_in_order", out_dtype=BF16, rode=[part], modes="gather")
    mod_l = lax.dynamic_index_in_dim(pg, me, axis=2, keepdims=False).transpose(1, 0, 2).reshape(2, 3 * D)
    mod_c = pg[:, :, NDEV, :].transpose(1, 0, 2).reshape(2, 3 * D)
    mod = jnp.stack([mod_c, mod_l], axis=1)
    vec_bits = lax.bitcast_convert_type(jnp.concatenate([s5_d, s5_b_glu], axis=0), BITS16).reshape(2, -1)
    small = {n: w[n] for n in SMALL_RS}

    lvec, grad_x, dmod, gbig, gsmall, l1_recv, (bb_all, cc_all) = local_step(
        ctx[0], x[0], loss_target[0], mod, Wt, small, [shard(n) for n in L1_BIG] + [vec_bits])
    grad_x = grad_x[None]

    recv = dict(zip(L1_BIG + VEC_SHARDED, l1_recv))
    out = {}

    def keep(n, res):
        for key, arr in zip("gdmv", res):
            out[key, n] = arr.reshape(w[n].shape)

    kshape = lambda n: w[n].shape if w[n].ndim > 1 else (1, w[n].size)
    flat = jnp.concatenate([gsmall[n].reshape(-1) for n in TINY] + [dmod.reshape(-1), lvec.reshape(-1)])[None]
    *l0_recv, flat_all = exchange([gbig[n] for n in L0_BIG] + [flat], ["lead"] * len(L0_BIG) + ["gather"], "scatter_grads")
    chunk_all = [bb_all[:, 0], bb_all[:, 1], cc_all[:, 0], cc_all[:, 1]]
    tiny_all, off = [], 0
    for n in TINY:
        tiny_all.append(flat_all[:, 0, off:off + w[n].size].reshape((NDEV,) + kshape(n)))
        off += w[n].size
    dm_all = flat_all[:, 0, off:off + dmod.size].reshape((NDEV,) + dmod.shape)
    loss = sum_slots([flat_all[:, :, off + dmod.size:]], "loss_sum")[0][0, 0]

    dm_cols = lax.dynamic_slice_in_dim(dm_all.reshape(NDEV, 2, 2, 3 * D // WA, WA), me, 1, axis=3)[:, :, :, 0]
    dm_loc = jnp.concatenate([dm_cols[:, :, 1].transpose(1, 0, 2), dm_cols[:, :, 0].transpose(1, 0, 2)], axis=1)
    g_ada_w, dcc_part, g_ada_b = ada_bwd(cg, cc2, ada_w, dm_loc, dm_all.transpose(0, 2, 1, 3).reshape(2 * NDEV, 2, 3 * D), "ada_bwd")
    dcc_all = exchange([dcc_part], "gather", "gather_dcc")[0].reshape(NDEV, D)
    g_c_ctx = cctx_finish(dcc_all, cc2, "cctx_finish")

    flat2 = lambda t: t.reshape(-1, t.shape[-1])
    recv.update(dict(zip(L0_BIG, l0_recv)))
    big = [(recv[n], w[n][0], m[n][0], v[n][0]) for n in BIG]
    big.append((flat2(g_ada_w)[None], flat2(ada_w), flat2(m_ada_w), flat2(v_ada_w)))
    for n, r in zip(BIG + ("ada_w",), adamw_rows(big, "adamw_big")[0]):
        keep(n, r)
    items = []
    halves = 2
    for n, g in zip(CHUNKED, chunk_all):
        blk = (1, 1, G // halves) + w[n].shape[3:]
        g = jnp.moveaxis(g, 0, 1).reshape(w[n].shape)
        g_spec = pl.BlockSpec((1,) + blk, lambda d, s: (0, 0, d, s, 0, 0))
        items.append((g[None], g_spec, w[n], m[n], v[n], pl.BlockSpec(blk, lambda d, s: (0, d, s, 0, 0))))
    for n, res in zip(CHUNKED, adamw_multi(items, (2, halves), "adamw_bc")):
        keep(n, res)
    tiny_g = dict(zip(TINY, tiny_all))
    tiny_g.update({n: recv[n] for n in VEC_SHARDED})
    tiny_g["c_ctx"], tiny_g["ada_b"] = g_c_ctx[None], g_ada_b[None]
    names = list(tiny_g)
    items = [(tiny_g[n], _whole(tiny_g[n], 1)) + tuple(t[n].reshape(kshape(n)) for t in (w, m, v))
             + (pl.BlockSpec(kshape(n), lambda i, r=len(kshape(n)): (0,) * r),) for n in names]
    for n, res in zip(names, adamw_multi(items, (1,), "adamw_small")):
        keep(n, res)

    return (loss, grad_x, *[out["g", n] for n in ORDER], *[out["d", n] for n in ORDER],
            *[out["m", n] for n in ORDER], *[out["v", n] for n in ORDER])
```
